```python
import math
import jax, jax.numpy as jnp
from jax import lax
import numpy as np

D_MODEL = 1024
BATCH = 8
SEQ = 4096
DEPTH = 1

MIX_WIDTH = D_MODEL
GLA_HEADS = 4
GLA_DV = MIX_WIDTH // 2 // GLA_HEADS
GLA_DK = GLA_DV // 2
GLA_GATE_RANK = 16
GLA_GATE_NORM = 16.0
GLA_CHUNK = 64
SWA_HEAD_DIM = 64
SWA_Q_HEADS = MIX_WIDTH // 2 // SWA_HEAD_DIM
SWA_KV_HEADS = 2
SWA_BLOCK = 128
SWA_WINDOW = 128
REL_BUCKETS = 32
REL_MAX_DIST = 128
D_FF = 4 * D_MODEL
NORM_EPS = 1e-6

COL_SIZES = (
    GLA_HEADS * GLA_DK,
    GLA_HEADS * GLA_DK,
    GLA_HEADS * GLA_DV,
    GLA_HEADS * GLA_DV,
    2 * GLA_GATE_RANK,
    SWA_Q_HEADS * SWA_HEAD_DIM,
    SWA_KV_HEADS * SWA_HEAD_DIM,
    SWA_KV_HEADS * SWA_HEAD_DIM,
)
IN_COLS = int(sum(COL_SIZES))
SPLITS = [int(s) for s in np.cumsum(COL_SIZES)[:-1]]

kernel_name = "hybrid_gla_swa_bidir_encoder_layer"


def rmsnorm(x, g):
    xf = x.astype(jnp.float32)
    r = lax.rsqrt(jnp.mean(xf * xf, axis=-1, keepdims=True) + NORM_EPS)
    return (xf * r).astype(x.dtype) * g


def t5_buckets(rel):
    nb = REL_BUCKETS // 2
    ret = (rel > 0).astype(np.int32) * nb
    n = np.abs(rel)
    max_exact = nb // 2
    large = max_exact + (np.log(np.maximum(n, 1).astype(np.float32) / max_exact)
                         / math.log(REL_MAX_DIST / max_exact) * (nb - max_exact)).astype(np.int32)
    large = np.minimum(large, nb - 1)
    return ret + np.where(n < max_exact, n, large)


def gla_chunked(q, k, v, log_a):
    B, H, L, dk = q.shape
    dv = v.shape[-1]
    C = GLA_CHUNK
    N = L // C
    q = q.reshape(B, H, N, C, dk)
    k = k.reshape(B, H, N, C, dk)
    v = v.reshape(B, H, N, C, dv)
    b = jnp.cumsum(log_a.reshape(B, H, N, C, dk), axis=3)
    b_last = b[:, :, :, -1:, :]
    q_dec = q * jnp.exp(b)
    k_intra = k * jnp.exp(-b)
    k_state = k * jnp.exp(b_last - b)
    causal = jnp.asarray(np.tril(np.ones((C, C), dtype=bool)))
    A = jnp.where(causal, jnp.einsum('bhncd,bhnsd->bhncs', q_dec, k_intra), 0.0)
    o_intra = jnp.einsum('bhncs,bhnse->bhnce', A, v)
    dS = jnp.einsum('bhncd,bhnce->bhnde', k_state, v)
    decay = jnp.exp(b_last[:, :, :, 0, :])

    def step(S, inp):
        d, ds = inp
        return d[..., None] * S + ds, S

    S0 = jnp.zeros((B, H, dk, dv), jnp.float32)
    _, S_enter = lax.scan(step, S0, (jnp.moveaxis(decay, 2, 0), jnp.moveaxis(dS, 2, 0)))
    S_enter = jnp.moveaxis(S_enter, 0, 2)
    o_inter = jnp.einsum('bhncd,bhnde->bhnce', q_dec, S_enter)
    return (o_intra + o_inter).reshape(B, H, L, dv)


def banded_window_gqa(q, k, v, sink, rel_table):
    B, Hq, L, dh = q.shape
    Hkv = k.shape[1]
    G = Hq // Hkv
    W = SWA_BLOCK
    N = L // W
    qb = q.reshape(B, Hkv, G, N, W, dh)

    def band(t):
        tp = jnp.pad(t, ((0, 0), (0, 0), (W, W), (0, 0))).reshape(B, Hkv, N + 2, W, dh)
        return jnp.concatenate([tp[:, :, :N], tp[:, :, 1:N + 1], tp[:, :, 2:N + 2]], axis=3)

    kb, vb = band(k), band(v)
    c = np.arange(W)[:, None]
    s = np.arange(3 * W)[None, :]
    rel = s - W - c
    key_pos = (np.arange(N)[:, None, None] - 1) * W + s[None]
    mask = jnp.asarray((np.abs(rel)[None] <= SWA_WINDOW) & (key_pos >= 0) & (key_pos < L))
    bias = rel_table.astype(jnp.float32)[jnp.asarray(t5_buckets(rel))]
    bias = jnp.transpose(bias, (2, 0, 1)).reshape(Hkv, G, 1, W, 3 * W)

    logits = jnp.einsum('bkgncd,bknsd->bkgncs', qb, kb).astype(jnp.float32) * (dh ** -0.5) + bias
    logits = jnp.where(mask, logits, -1e30)
    sink_l = sink.astype(jnp.float32).reshape(Hkv, G, 1, 1, 1)
    m = jnp.maximum(jnp.max(logits, axis=-1, keepdims=True), sink_l)
    p = jnp.exp(logits - m)
    denom = jnp.sum(p, axis=-1, keepdims=True) + jnp.exp(sink_l - m)
    o = jnp.einsum('bkgncs,bknsd->bkgncd', (p / denom).astype(v.dtype), vb)
    return o.reshape(B, Hq, L, dh)


def hybrid_mixer(u, w_in, w_gu_f, b_g_f, w_gu_b, b_g_b, gla_norm, sink, rel_table, w_out):
    B, L, _ = u.shape
    proj = u @ w_in
    qa, ka, va, ga, za, qs, ks, vs = jnp.split(proj, SPLITS, axis=-1)

    def heads(t, h):
        return t.reshape(B, L, h, -1).transpose(0, 2, 1, 3)

    f32 = jnp.float32
    qh = heads(qa, GLA_HEADS).astype(f32) * (GLA_DK ** -0.5)
    kh = heads(ka, GLA_HEADS).astype(f32)
    vh = heads(va, GLA_HEADS).astype(f32)
    zf, zb = za[..., :GLA_GATE_RANK], za[..., GLA_GATE_RANK:]
    la_f = heads(jax.nn.log_sigmoid((zf @ w_gu_f + b_g_f).astype(f32)) / GLA_GATE_NORM, GLA_HEADS)
    la_b = heads(jax.nn.log_sigmoid((zb @ w_gu_b + b_g_b).astype(f32)) / GLA_GATE_NORM, GLA_HEADS)
    o_f = gla_chunked(qh, kh, vh, la_f)
    flip = lambda t: jnp.flip(t, axis=2)
    o_b = flip(gla_chunked(flip(qh), flip(kh), flip(vh), flip(la_b)))
    o_a = o_f + o_b
    o_a = o_a * lax.rsqrt(jnp.mean(o_a * o_a, axis=-1, keepdims=True) + NORM_EPS)
    o_a = o_a.transpose(0, 2, 1, 3) * gla_norm.astype(f32)
    o_a = (o_a.reshape(B, L, GLA_HEADS * GLA_DV) * jax.nn.silu(ga.astype(f32))).astype(u.dtype)

    o_s = banded_window_gqa(heads(qs, SWA_Q_HEADS), heads(ks, SWA_KV_HEADS), heads(vs, SWA_KV_HEADS),
                            sink, rel_table)
    o_s = o_s.transpose(0, 2, 1, 3).reshape(B, L, SWA_Q_HEADS * SWA_HEAD_DIM)

    return jnp.concatenate([o_a, o_s], axis=-1) @ w_out


def _fwd_setup_inputs(seed: int = 0) -> dict:
    key = jax.random.key(seed)
    ks = jax.random.split(key, 20)
    nrm = lambda k, shape, scale: jax.random.normal(k, shape, jnp.float32) * scale
    gain = lambda k, shape: 1.0 + nrm(k, shape, 0.02)
    gla_w = GLA_HEADS * GLA_DK
    return {
        "x": nrm(ks[0], (BATCH, SEQ, D_MODEL), 1.0),
        "norm_mix_pre": gain(ks[1], (DEPTH, D_MODEL)),
        "w_in": nrm(ks[2], (DEPTH, D_MODEL, IN_COLS), D_MODEL ** -0.5),
        "w_gate_up_fwd": nrm(ks[3], (DEPTH, GLA_GATE_RANK, gla_w), GLA_GATE_RANK ** -0.5),
        "b_gate_fwd": nrm(ks[4], (DEPTH, gla_w), 0.1),
        "w_gate_up_bwd": nrm(ks[5], (DEPTH, GLA_GATE_RANK, gla_w), GLA_GATE_RANK ** -0.5),
        "b_gate_bwd": nrm(ks[6], (DEPTH, gla_w), 0.1),
        "gla_norm": gain(ks[7], (DEPTH, GLA_DV)),
        "swa_sink": nrm(ks[8], (DEPTH, SWA_Q_HEADS), 0.5),
        "rel_bias": nrm(ks[9], (REL_BUCKETS, SWA_Q_HEADS), 0.5),
        "w_out": nrm(ks[10], (DEPTH, MIX_WIDTH, D_MODEL), MIX_WIDTH ** -0.5),
        "norm_mix_post": gain(ks[11], (DEPTH, D_MODEL)),
        "norm_mlp_pre": gain(ks[12], (DEPTH, D_MODEL)),
        "w_up": nrm(ks[13], (DEPTH, D_MODEL, D_FF), D_MODEL ** -0.5),
        "w_down": nrm(ks[14], (DEPTH, D_FF, D_MODEL), D_FF ** -0.5),
        "norm_mlp_post": gain(ks[15], (DEPTH, D_MODEL)),
    }


def _fwd_reference(x, norm_mix_pre, w_in, w_gate_up_fwd, b_gate_fwd, w_gate_up_bwd, b_gate_bwd,
              gla_norm, swa_sink, rel_bias, w_out, norm_mix_post, norm_mlp_pre, w_up, w_down,
              norm_mlp_post):
    h = x
    for l in range(DEPTH):
        u = rmsnorm(h, norm_mix_pre[l])
        mix = hybrid_mixer(u, w_in[l], w_gate_up_fwd[l], b_gate_fwd[l], w_gate_up_bwd[l], b_gate_bwd[l],
                           gla_norm[l], swa_sink[l], rel_bias, w_out[l])
        h = h + rmsnorm(mix, norm_mix_post[l])
        z = rmsnorm(h, norm_mlp_pre[l]) @ w_up[l]
        ff = jnp.square(jax.nn.relu(z)) @ w_down[l]
        h = h + rmsnorm(ff, norm_mlp_post[l])
    return h


import jax as _jax
import jax.numpy as _jnp

TWIN_FORMAT = 'train_step'
FWD_PARAMS = ['x', 'norm_mix_pre', 'w_in', 'w_gate_up_fwd', 'b_gate_fwd', 'w_gate_up_bwd', 'b_gate_bwd', 'gla_norm', 'swa_sink', 'rel_bias', 'w_out', 'norm_mix_post', 'norm_mlp_pre', 'w_up', 'w_down', 'norm_mlp_post']
TWIN_WEIGHTS = ['norm_mix_pre', 'w_in', 'w_gate_up_fwd', 'b_gate_fwd', 'w_gate_up_bwd', 'b_gate_bwd', 'gla_norm', 'swa_sink', 'rel_bias', 'w_out', 'norm_mix_post', 'norm_mlp_pre', 'w_up', 'w_down', 'norm_mlp_post']
TWIN_DIFF_INPUT = 'x'
TWIN_INPUTS = ['x', 'norm_mix_pre', 'w_in', 'w_gate_up_fwd', 'b_gate_fwd', 'w_gate_up_bwd', 'b_gate_bwd', 'gla_norm', 'swa_sink', 'rel_bias', 'w_out', 'norm_mix_post', 'norm_mlp_pre', 'w_up', 'w_down', 'norm_mlp_post', 'loss_target', 'm_norm_mix_pre', 'm_w_in', 'm_w_gate_up_fwd', 'm_b_gate_fwd', 'm_w_gate_up_bwd', 'm_b_gate_bwd', 'm_gla_norm', 'm_swa_sink', 'm_rel_bias', 'm_w_out', 'm_norm_mix_post', 'm_norm_mlp_pre', 'm_w_up', 'm_w_down', 'm_norm_mlp_post', 'v_norm_mix_pre', 'v_w_in', 'v_w_gate_up_fwd', 'v_b_gate_fwd', 'v_w_gate_up_bwd', 'v_b_gate_bwd', 'v_gla_norm', 'v_swa_sink', 'v_rel_bias', 'v_w_out', 'v_norm_mix_post', 'v_norm_mlp_pre', 'v_w_up', 'v_w_down', 'v_norm_mlp_post']
TWIN_OUTPUTS = ['loss', 'grad_x', 'grad_norm_mix_pre', 'grad_w_in', 'grad_w_gate_up_fwd', 'grad_b_gate_fwd', 'grad_w_gate_up_bwd', 'grad_b_gate_bwd', 'grad_gla_norm', 'grad_swa_sink', 'grad_rel_bias', 'grad_w_out', 'grad_norm_mix_post', 'grad_norm_mlp_pre', 'grad_w_up', 'grad_w_down', 'grad_norm_mlp_post', 'delta_norm_mix_pre', 'delta_w_in', 'delta_w_gate_up_fwd', 'delta_b_gate_fwd', 'delta_w_gate_up_bwd', 'delta_b_gate_bwd', 'delta_gla_norm', 'delta_swa_sink', 'delta_rel_bias', 'delta_w_out', 'delta_norm_mix_post', 'delta_norm_mlp_pre', 'delta_w_up', 'delta_w_down', 'delta_norm_mlp_post', 'new_m_norm_mix_pre', 'new_m_w_in', 'new_m_w_gate_up_fwd', 'new_m_b_gate_fwd', 'new_m_w_gate_up_bwd', 'new_m_b_gate_bwd', 'new_m_gla_norm', 'new_m_swa_sink', 'new_m_rel_bias', 'new_m_w_out', 'new_m_norm_mix_post', 'new_m_norm_mlp_pre', 'new_m_w_up', 'new_m_w_down', 'new_m_norm_mlp_post', 'new_v_norm_mix_pre', 'new_v_w_in', 'new_v_w_gate_up_fwd', 'new_v_b_gate_fwd', 'new_v_w_gate_up_bwd', 'new_v_b_gate_bwd', 'new_v_gla_norm', 'new_v_swa_sink', 'new_v_rel_bias', 'new_v_w_out', 'new_v_norm_mix_post', 'new_v_norm_mlp_pre', 'new_v_w_up', 'new_v_w_down', 'new_v_norm_mlp_post']
TWIN_LEAF_KINDS = {'loss': 'loss', 'grad_x': 'grad_x', 'grad_norm_mix_pre': 'grad_w', 'grad_w_in': 'grad_w', 'grad_w_gate_up_fwd': 'grad_w', 'grad_b_gate_fwd': 'grad_w', 'grad_w_gate_up_bwd': 'grad_w', 'grad_b_gate_bwd': 'grad_w', 'grad_gla_norm': 'grad_w', 'grad_swa_sink': 'grad_w', 'grad_rel_bias': 'grad_w', 'grad_w_out': 'grad_w', 'grad_norm_mix_post': 'grad_w', 'grad_norm_mlp_pre': 'grad_w', 'grad_w_up': 'grad_w', 'grad_w_down': 'grad_w', 'grad_norm_mlp_post': 'grad_w', 'delta_norm_mix_pre': 'delta_w', 'delta_w_in': 'delta_w', 'delta_w_gate_up_fwd': 'delta_w', 'delta_b_gate_fwd': 'delta_w', 'delta_w_gate_up_bwd': 'delta_w', 'delta_b_gate_bwd': 'delta_w', 'delta_gla_norm': 'delta_w', 'delta_swa_sink': 'delta_w', 'delta_rel_bias': 'delta_w', 'delta_w_out': 'delta_w', 'delta_norm_mix_post': 'delta_w', 'delta_norm_mlp_pre': 'delta_w', 'delta_w_up': 'delta_w', 'delta_w_down': 'delta_w', 'delta_norm_mlp_post': 'delta_w', 'new_m_norm_mix_pre': 'new_m', 'new_m_w_in': 'new_m', 'new_m_w_gate_up_fwd': 'new_m', 'new_m_b_gate_fwd': 'new_m', 'new_m_w_gate_up_bwd': 'new_m', 'new_m_b_gate_bwd': 'new_m', 'new_m_gla_norm': 'new_m', 'new_m_swa_sink': 'new_m', 'new_m_rel_bias': 'new_m', 'new_m_w_out': 'new_m', 'new_m_norm_mix_post': 'new_m', 'new_m_norm_mlp_pre': 'new_m', 'new_m_w_up': 'new_m', 'new_m_w_down': 'new_m', 'new_m_norm_mlp_post': 'new_m', 'new_v_norm_mix_pre': 'new_v', 'new_v_w_in': 'new_v', 'new_v_w_gate_up_fwd': 'new_v', 'new_v_b_gate_fwd': 'new_v', 'new_v_w_gate_up_bwd': 'new_v', 'new_v_b_gate_bwd': 'new_v', 'new_v_gla_norm': 'new_v', 'new_v_swa_sink': 'new_v', 'new_v_rel_bias': 'new_v', 'new_v_w_out': 'new_v', 'new_v_norm_mix_post': 'new_v', 'new_v_norm_mlp_pre': 'new_v', 'new_v_w_up': 'new_v', 'new_v_w_down': 'new_v', 'new_v_norm_mlp_post': 'new_v'}


def _forward(args):
    return _fwd_reference(*[args[k] for k in FWD_PARAMS])


def _output_shape():
    out = _jax.eval_shape(lambda: _forward(_fwd_setup_inputs(0)))
    return out.shape, out.dtype

N_MICROBATCH = 1
ADAM_LR = 0.001
ADAM_B1 = 0.9
ADAM_B2 = 0.999
ADAM_EPS = 1e-08
ADAM_WD = 0.01
ADAM_STEP = 10
PER_EXAMPLE_BATCH_AXIS = {'x': 0, 'loss_target': 0}
SHARED_INPUTS = []
_WEIGHT_DTYPES = {'norm_mix_pre': _jnp.float32, 'w_in': _jnp.float32, 'w_gate_up_fwd': _jnp.float32, 'b_gate_fwd': _jnp.float32, 'w_gate_up_bwd': _jnp.float32, 'b_gate_bwd': _jnp.float32, 'gla_norm': _jnp.float32, 'swa_sink': _jnp.float32, 'rel_bias': _jnp.float32, 'w_out': _jnp.float32, 'norm_mix_post': _jnp.float32, 'norm_mlp_pre': _jnp.float32, 'w_up': _jnp.float32, 'w_down': _jnp.float32, 'norm_mlp_post': _jnp.float32}
MOMENT_SCALE = {'norm_mix_pre': 9.044506e-01, 'w_in': 5.521921e-01, 'w_gate_up_fwd': 6.058496e-02, 'b_gate_fwd': 2.491437e-01, 'w_gate_up_bwd': 5.938219e-02, 'b_gate_bwd': 2.408959e-01, 'gla_norm': 2.809582e+00, 'swa_sink': 3.540626e-03, 'rel_bias': 1.581326e-01, 'w_out': 6.713311e-01, 'norm_mix_post': 3.201287e+01, 'norm_mlp_pre': 7.886286e-01, 'w_up': 3.993311e-01, 'w_down': 8.122115e-01, 'norm_mlp_post': 3.285559e+01}


def _to_microbatches(a, axis):
    t = _jnp.moveaxis(a, axis, 0)
    t = t.reshape((N_MICROBATCH, t.shape[0] // N_MICROBATCH) + t.shape[1:])
    return _jnp.moveaxis(t, 1, axis + 1)


def setup_inputs(seed: int = 0) -> dict:
    inp = _fwd_setup_inputs(seed)
    key = _jax.random.fold_in(_jax.random.key(seed), 7919)
    shape, _ = _output_shape()
    out = dict(inp)
    out["loss_target"] = _jax.random.normal(_jax.random.fold_in(key, 0), shape, _jnp.float32)
    for i, name in enumerate(TWIN_WEIGHTS):
        w = inp[name].astype(_jnp.float32)
        if MOMENT_SCALE is None:
            s = _jnp.sqrt(_jnp.mean(_jnp.square(w)) + 1e-30)
        else:
            s = MOMENT_SCALE[name]
        km, kv = _jax.random.split(_jax.random.fold_in(key, i + 1))
        out[name] = w
        out["m_" + name] = s * _jax.random.normal(km, w.shape, _jnp.float32)
        out["v_" + name] = (s * s) * _jax.random.uniform(kv, w.shape, _jnp.float32, 0.5, 1.5)
    if N_MICROBATCH > 1:
        for name, axis in PER_EXAMPLE_BATCH_AXIS.items():
            out[name] = _to_microbatches(out[name], axis)
    return {'x': out['x'], 'norm_mix_pre': out['norm_mix_pre'], 'w_in': out['w_in'], 'w_gate_up_fwd': out['w_gate_up_fwd'], 'b_gate_fwd': out['b_gate_fwd'], 'w_gate_up_bwd': out['w_gate_up_bwd'], 'b_gate_bwd': out['b_gate_bwd'], 'gla_norm': out['gla_norm'], 'swa_sink': out['swa_sink'], 'rel_bias': out['rel_bias'], 'w_out': out['w_out'], 'norm_mix_post': out['norm_mix_post'], 'norm_mlp_pre': out['norm_mlp_pre'], 'w_up': out['w_up'], 'w_down': out['w_down'], 'norm_mlp_post': out['norm_mlp_post'], 'loss_target': out['loss_target'], 'm_norm_mix_pre': out['m_norm_mix_pre'], 'm_w_in': out['m_w_in'], 'm_w_gate_up_fwd': out['m_w_gate_up_fwd'], 'm_b_gate_fwd': out['m_b_gate_fwd'], 'm_w_gate_up_bwd': out['m_w_gate_up_bwd'], 'm_b_gate_bwd': out['m_b_gate_bwd'], 'm_gla_norm': out['m_gla_norm'], 'm_swa_sink': out['m_swa_sink'], 'm_rel_bias': out['m_rel_bias'], 'm_w_out': out['m_w_out'], 'm_norm_mix_post': out['m_norm_mix_post'], 'm_norm_mlp_pre': out['m_norm_mlp_pre'], 'm_w_up': out['m_w_up'], 'm_w_down': out['m_w_down'], 'm_norm_mlp_post': out['m_norm_mlp_post'], 'v_norm_mix_pre': out['v_norm_mix_pre'], 'v_w_in': out['v_w_in'], 'v_w_gate_up_fwd': out['v_w_gate_up_fwd'], 'v_b_gate_fwd': out['v_b_gate_fwd'], 'v_w_gate_up_bwd': out['v_w_gate_up_bwd'], 'v_b_gate_bwd': out['v_b_gate_bwd'], 'v_gla_norm': out['v_gla_norm'], 'v_swa_sink': out['v_swa_sink'], 'v_rel_bias': out['v_rel_bias'], 'v_w_out': out['v_w_out'], 'v_norm_mix_post': out['v_norm_mix_post'], 'v_norm_mlp_pre': out['v_norm_mlp_pre'], 'v_w_up': out['v_w_up'], 'v_w_down': out['v_w_down'], 'v_norm_mlp_post': out['v_norm_mlp_post']}


def _loss(weights, diff, rest, loss_target):
    with _jax.named_scope("forward"):
        args = {**rest, TWIN_DIFF_INPUT: diff, **{k: w.astype(_WEIGHT_DTYPES[k]) for k, w in weights.items()}}
        y = _forward(args)
    with _jax.named_scope("loss_head"):
        err = _jnp.square(y.astype(_jnp.float32) - loss_target)
        return 0.5 * _jnp.sum(_jnp.mean(err, axis=-1)) if err.ndim else 0.5 * err


def _adamw(w, g, m, v):
    m = ADAM_B1 * m + (1.0 - ADAM_B1) * g
    v = ADAM_B2 * v + (1.0 - ADAM_B2) * _jnp.square(g)
    m_hat = m / (1.0 - ADAM_B1 ** ADAM_STEP)
    v_hat = v / (1.0 - ADAM_B2 ** ADAM_STEP)
    delta = -ADAM_LR * (m_hat / (_jnp.sqrt(v_hat) + ADAM_EPS) + ADAM_WD * w)
    return delta, m, v


def reference(x, norm_mix_pre, w_in, w_gate_up_fwd, b_gate_fwd, w_gate_up_bwd, b_gate_bwd, gla_norm, swa_sink, rel_bias, w_out, norm_mix_post, norm_mlp_pre, w_up, w_down, norm_mlp_post, loss_target, m_norm_mix_pre, m_w_in, m_w_gate_up_fwd, m_b_gate_fwd, m_w_gate_up_bwd, m_b_gate_bwd, m_gla_norm, m_swa_sink, m_rel_bias, m_w_out, m_norm_mix_post, m_norm_mlp_pre, m_w_up, m_w_down, m_norm_mlp_post, v_norm_mix_pre, v_w_in, v_w_gate_up_fwd, v_b_gate_fwd, v_w_gate_up_bwd, v_b_gate_bwd, v_gla_norm, v_swa_sink, v_rel_bias, v_w_out, v_norm_mix_post, v_norm_mlp_pre, v_w_up, v_w_down, v_norm_mlp_post):
    given = dict(x=x, norm_mix_pre=norm_mix_pre, w_in=w_in, w_gate_up_fwd=w_gate_up_fwd, b_gate_fwd=b_gate_fwd, w_gate_up_bwd=w_gate_up_bwd, b_gate_bwd=b_gate_bwd, gla_norm=gla_norm, swa_sink=swa_sink, rel_bias=rel_bias, w_out=w_out, norm_mix_post=norm_mix_post, norm_mlp_pre=norm_mlp_pre, w_up=w_up, w_down=w_down, norm_mlp_post=norm_mlp_post, loss_target=loss_target, m_norm_mix_pre=m_norm_mix_pre, m_w_in=m_w_in, m_w_gate_up_fwd=m_w_gate_up_fwd, m_b_gate_fwd=m_b_gate_fwd, m_w_gate_up_bwd=m_w_gate_up_bwd, m_b_gate_bwd=m_b_gate_bwd, m_gla_norm=m_gla_norm, m_swa_sink=m_swa_sink, m_rel_bias=m_rel_bias, m_w_out=m_w_out, m_norm_mix_post=m_norm_mix_post, m_norm_mlp_pre=m_norm_mlp_pre, m_w_up=m_w_up, m_w_down=m_w_down, m_norm_mlp_post=m_norm_mlp_post, v_norm_mix_pre=v_norm_mix_pre, v_w_in=v_w_in, v_w_gate_up_fwd=v_w_gate_up_fwd, v_b_gate_fwd=v_b_gate_fwd, v_w_gate_up_bwd=v_w_gate_up_bwd, v_b_gate_bwd=v_b_gate_bwd, v_gla_norm=v_gla_norm, v_swa_sink=v_swa_sink, v_rel_bias=v_rel_bias, v_w_out=v_w_out, v_norm_mix_post=v_norm_mix_post, v_norm_mlp_pre=v_norm_mlp_pre, v_w_up=v_w_up, v_w_down=v_w_down, v_norm_mlp_post=v_norm_mlp_post)
    weights = {n: given[n] for n in TWIN_WEIGHTS}
    shared = {n: given[n] for n in SHARED_INPUTS}
    per_example = {n: given[n] for n in ['x']}
    grad_fn = _jax.value_and_grad(_loss, argnums=(0, 1))

    def one_microbatch(ex, loss_target):
        ex = dict(ex)
        diff = ex.pop(TWIN_DIFF_INPUT)
        return grad_fn(weights, diff, {**shared, **ex}, loss_target)

    if N_MICROBATCH == 1:
        loss, (grad_w, grad_x) = one_microbatch(per_example, given["loss_target"])
    else:
        def body(carry, xs):
            loss_sum, grad_sum = carry
            l_k, (gw_k, gx_k) = one_microbatch(xs[0], xs[1])
            with _jax.named_scope("update"):
                return (loss_sum + l_k, _jax.tree.map(_jnp.add, grad_sum, gw_k)), gx_k

        init = (_jnp.zeros((), _jnp.float32), _jax.tree.map(_jnp.zeros_like, weights))
        (loss, grad_w), grad_x = _jax.lax.scan(body, init, (per_example, given["loss_target"]))
    with _jax.named_scope("update"):
        delta_w, new_m, new_v = {}, {}, {}
        for n in TWIN_WEIGHTS:
            delta_w[n], new_m[n], new_v[n] = _adamw(weights[n], grad_w[n], given["m_" + n], given["v_" + n])
    return (loss, grad_x, *[grad_w[n] for n in TWIN_WEIGHTS], *[delta_w[n] for n in TWIN_WEIGHTS],
            *[new_m[n] for n in TWIN_WEIGHTS], *[new_v[n] for n in TWIN_WEIGHTS])
```

```python
import collections
import math

import numpy as np
import jax
import jax.numpy as jnp
from jax import lax
from jax.experimental import pallas as pl
from jax.experimental.pallas import tpu as pltpu

F32 = jnp.float32
MXU_DTYPE = jnp.bfloat16
COMM_DTYPE = jnp.bfloat16

D_MODEL = 1024
D_FF = 4096
N_CHIPS = 4
GLA_HEADS = 4
GLA_CHUNK = 64
GLA_GATE_RANK = 16
GLA_GATE_NORM = 16.0
SWA_Q_HEADS = 8
SWA_KV_HEADS = 2
SWA_BLOCK = 128
REL_BUCKETS = 32
REL_MAX_DIST = 128
NORM_EPS = 1e-6
HEAD_PAD = 128

ADAM_LR = 0.001
ADAM_B1 = 0.9
ADAM_B2 = 0.999
ADAM_EPS = 1e-08
ADAM_WD = 0.01
ADAM_STEP = 10

C_QA, C_KA, C_VA, C_GA = (0, 512), (512, 512), (1024, 512), (1536, 512)
C_QS, C_KS, C_VS, C_ZA = (2048, 1024), (3072, 256), (3328, 256), (3584, 128)
IN_PAD = 3712
OUT_PAD = 1536

R_IN, R_OUT, R_UP, R_DOWN = 584, 256, 1024, 1024
PACK_USED = R_IN + R_OUT + R_UP + R_DOWN + 2
PACK_ROWS = 2944
PACK_HALF = PACK_ROWS // 2
ADD_ROWS = 368

VMEM_BIG = 56 * 1024 * 1024
MESH_AXES = ("x", "y", "c")
MESH_ID = pl.DeviceIdType.MESH


def _mx(a):
    return a.astype(MXU_DTYPE)


def _dot(a, b):
    return jnp.dot(a, b, preferred_element_type=F32)


def _dot_nt(a, b):
    return lax.dot_general(a, b, (((1,), (1,)), ((), ())), preferred_element_type=F32)


def _dot_tn(a, b):
    return lax.dot_general(a, b, (((0,), (0,)), ((), ())), preferred_element_type=F32)


def _dot_exact(a, b):
    return jnp.dot(a, b, precision=lax.Precision.HIGHEST, preferred_element_type=F32)


def _dot_tn_exact(a, b):
    return lax.dot_general(a, b, (((0,), (0,)), ((), ())), precision=lax.Precision.HIGHEST,
                           preferred_element_type=F32)


def _rms_r(x):
    return lax.rsqrt(jnp.mean(x * x, axis=-1, keepdims=True) + NORM_EPS)


def _rms_bwd(x, r, g, dy):
    xh = x * r
    gdy = dy * g
    dx = r * (gdy - xh * jnp.mean(gdy * xh, axis=-1, keepdims=True))
    return dx, jnp.sum(dy * xh, axis=0, keepdims=True)


def _params(sem=None, vmem=None):
    kw = {}
    if sem is not None:
        kw["dimension_semantics"] = sem
    if vmem is not None:
        kw["vmem_limit_bytes"] = vmem
    return pltpu.CompilerParams(**kw)


def _vmem_spec():
    return pl.BlockSpec(memory_space=pltpu.VMEM)


def _row_spec(tm, width):
    return pl.BlockSpec((tm, width), lambda i: (i, 0))


def _full_spec(shape):
    return pl.BlockSpec(shape, lambda i: (0,) * len(shape))


def _proj_call(x, g_pre, w_in_p):
    L = x.shape[0]
    tm = min(256, L)
    groups = [(C_QA, F32), (C_KA, F32), (C_VA, MXU_DTYPE), (C_GA, F32),
              (C_QS, MXU_DTYPE), (C_KS, MXU_DTYPE), (C_VS, MXU_DTYPE), (C_ZA, F32)]

    def body(x_ref, g_ref, w_ref, *outs):
        xv = x_ref[...]
        u = _mx(xv * _rms_r(xv) * g_ref[...])
        for ref, ((off, width), _) in zip(outs, groups):
            ref[...] = _dot(u, w_ref[:, off:off + width]).astype(ref.dtype)

    return pl.pallas_call(
        body, name="proj_fwd", grid=(L // tm,),
        in_specs=[_row_spec(tm, D_MODEL), _full_spec((1, D_MODEL)), _vmem_spec()],
        out_specs=[_row_spec(tm, w) for (_, w), _ in groups],
        out_shape=[jax.ShapeDtypeStruct((L, w), dt) for (_, w), dt in groups],
        compiler_params=_params(("arbitrary",), VMEM_BIG),
    )(x, g_pre, w_in_p)


_GlaPre = collections.namedtuple("_GlaPre", "z g eb enb elb dec qd ki ks")


def _gla_chunk_pre(q, k, z, w, bias, tri, rev):
    g = _dot(_mx(z), w) + bias
    la = (jnp.minimum(g, 0.0) - jnp.log(1.0 + jnp.exp(-jnp.abs(g)))) / GLA_GATE_NORM
    b = _dot_exact(tri.astype(F32), la)
    blast = b[0:1] if rev else b[GLA_CHUNK - 1:GLA_CHUNK]
    eb = jnp.exp(b)
    enb = jnp.exp(-b)
    elb = jnp.exp(blast - b)
    dec = jnp.exp(blast)
    qd = q * 0.125 * eb
    return _GlaPre(z, g, eb, enb, elb, dec, qd, k * enb, k * elb)


def _tri_masks():
    row = lax.broadcasted_iota(jnp.int32, (GLA_CHUNK, GLA_CHUNK), 0)
    col = lax.broadcasted_iota(jnp.int32, (GLA_CHUNK, GLA_CHUNK), 1)
    return row >= col, row <= col, row


def _gla_fwd_call(qa, ka, va, za, wgf, bgf, wgb, bgb):
    L = qa.shape[0]
    br = min(512, L)
    nb, nc, n_chunks = L // br, br // GLA_CHUNK, L // GLA_CHUNK
    hw = GLA_HEADS * HEAD_PAD

    def body(qaf, kaf, vaf, zaf, qab, kab, vab, zab, wgf_r, bgf_r, wgb_r, bgb_r,
             of_r, ob_r, sf_r, sb_r, st_f, st_b):
        @pl.when(pl.program_id(0) == 0)
        def _():
            st_f[...] = jnp.zeros_like(st_f)
            st_b[...] = jnp.zeros_like(st_b)

        tri_f, tri_b, _ = _tri_masks()

        def one(rev, q_r, k_r, v_r, z_r, w_r, b_r, o_r, s_r, st, ci):
            tri = tri_b if rev else tri_f
            rows = pl.ds(pl.multiple_of(ci * GLA_CHUNK, GLA_CHUNK), GLA_CHUNK)
            pre = _gla_chunk_pre(q_r[rows, :], k_r[rows, :], z_r[rows, :], w_r[...], b_r[...], tri, rev)
            for h in range(GLA_HEADS):
                sl = slice(HEAD_PAD * h, HEAD_PAD * (h + 1))
                qd, ki, ks = _mx(pre.qd[:, sl]), _mx(pre.ki[:, sl]), _mx(pre.ks[:, sl])
                a = jnp.where(tri, _dot_nt(qd, ki), 0.0)
                v = v_r[rows, sl]
                s_t = st[h]
                s_r[ci, h] = s_t
                o_r[rows, sl] = _dot(_mx(a), v) + _dot_nt(qd, _mx(s_t))
                st[h] = s_t * pre.dec[:, sl] + _dot_tn(v, ks)

        def loop(t, carry):
            one(False, qaf, kaf, vaf, zaf, wgf_r, bgf_r, of_r, sf_r, st_f, t)
            one(True, qab, kab, vab, zab, wgb_r, bgb_r, ob_r, sb_r, st_b, nc - 1 - t)
            return carry

        lax.fori_loop(0, nc, loop, 0)

    fwd = lambda i: (i, 0)
    bwd = lambda i: (nb - 1 - i, 0)
    ins = lambda m: [pl.BlockSpec((br, hw), m), pl.BlockSpec((br, hw), m),
                     pl.BlockSpec((br, hw), m), pl.BlockSpec((br, 128), m)]
    wspecs = [_full_spec((128, hw)), _full_spec((1, hw))] * 2
    s_shape = (nc, GLA_HEADS, HEAD_PAD, HEAD_PAD)
    return pl.pallas_call(
        body, name="gla_fwd", grid=(nb,),
        in_specs=ins(fwd) + ins(bwd) + wspecs,
        out_specs=[pl.BlockSpec((br, hw), fwd), pl.BlockSpec((br, hw), bwd),
                   pl.BlockSpec(s_shape, lambda i: (i, 0, 0, 0)),
                   pl.BlockSpec(s_shape, lambda i: (nb - 1 - i, 0, 0, 0))],
        out_shape=[jax.ShapeDtypeStruct((L, hw), F32), jax.ShapeDtypeStruct((L, hw), F32),
                   jax.ShapeDtypeStruct((n_chunks,) + s_shape[1:], F32),
                   jax.ShapeDtypeStruct((n_chunks,) + s_shape[1:], F32)],
        scratch_shapes=[pltpu.VMEM(s_shape[1:], F32), pltpu.VMEM(s_shape[1:], F32)],
        compiler_params=_params(("arbitrary",), VMEM_BIG),
    )(qa, ka, va, za, qa, ka, va, za, wgf, bgf, wgb, bgb)


def _gla_bwd_call(qa, ka, va, za, do, sf, sb, wgf, bgf, wgb, bgb):
    L = qa.shape[0]
    br = min(256, L)
    nb, nc = L // br, br // GLA_CHUNK
    hw = GLA_HEADS * HEAD_PAD

    def body(qaf, kaf, vaf, zaf, dof, sf_r, qab, kab, vab, zab, dob, sb_r, wgf_r, bgf_r, wgb_r, bgb_r,
             dqf, dkf, dvf, dzf, dwf, dbf, dqb, dkb, dvb, dzb, dwb, dbb, gt_f, gt_b):
        @pl.when(pl.program_id(0) == 0)
        def _():
            for ref in (gt_f, gt_b, dwf, dbf, dwb, dbb):
                ref[...] = jnp.zeros_like(ref)

        tri_f, tri_b, row = _tri_masks()
        row_w = lax.broadcasted_iota(jnp.int32, (GLA_CHUNK, HEAD_PAD), 0)

        def one(rev, q_r, k_r, v_r, z_r, do_r, s_r, w_r, b_r, dq_r, dk_r, dv_r, dz_r, dw_r, dbias_r, gt, ci):
            tri = tri_b if rev else tri_f
            last_row = 0 if rev else GLA_CHUNK - 1
            rows = pl.ds(pl.multiple_of(ci * GLA_CHUNK, GLA_CHUNK), GLA_CHUNK)
            w = w_r[...]
            pre = _gla_chunk_pre(q_r[rows, :], k_r[rows, :], z_r[rows, :], w, b_r[...], tri, rev)
            db_parts = []
            for h in range(GLA_HEADS):
                sl = slice(HEAD_PAD * h, HEAD_PAD * (h + 1))
                qd_f, ki_f, ks_f = pre.qd[:, sl], pre.ki[:, sl], pre.ks[:, sl]
                qd, ki, ks = _mx(qd_f), _mx(ki_f), _mx(ks_f)
                a = _mx(jnp.where(tri, _dot_nt(qd, ki), 0.0))
                v = v_r[rows, sl]
                do_h = _mx(do_r[rows, sl])
                s_t = s_r[ci, h]
                g_t = gt[h]
                g_m = _mx(g_t)
                da = _mx(jnp.where(tri, _dot_nt(do_h, v), 0.0))
                dv_r[rows, sl] = _dot_tn(a, do_h) + _dot_nt(ks, g_m)
                dqd = _dot(da, ki) + _dot(do_h, _mx(s_t))
                dki = _dot_tn(da, qd)
                dks = _dot(v, g_m)
                ddec = jnp.sum(g_t * s_t, axis=0, keepdims=True)
                gt[h] = g_t * pre.dec[:, sl] + _dot_tn(do_h, qd)
                dq_r[rows, sl] = dqd * pre.eb[:, sl] * 0.125
                dk_r[rows, sl] = dki * pre.enb[:, sl] + dks * pre.elb[:, sl]
                dblast = jnp.sum(dks * ks_f, axis=0, keepdims=True) + pre.dec[:, sl] * ddec
                db_h = dqd * qd_f - dki * ki_f - dks * ks_f
                db_parts.append(db_h + jnp.where(row_w == last_row, dblast, 0.0))
            db = jnp.concatenate(db_parts, axis=1)
            dla = _dot_tn_exact(tri.astype(F32), db)
            dg = dla * (1.0 / GLA_GATE_NORM) * (1.0 / (1.0 + jnp.exp(pre.g)))
            dg_m = _mx(dg)
            dz_r[rows, :] = _dot_nt(dg_m, w)
            dw_r[...] += _dot_tn(_mx(pre.z), dg_m)
            dbias_r[...] += jnp.sum(dg, axis=0, keepdims=True)

        def loop(t, carry):
            one(False, qaf, kaf, vaf, zaf, dof, sf_r, wgf_r, bgf_r, dqf, dkf, dvf, dzf, dwf, dbf, gt_f, nc - 1 - t)
            one(True, qab, kab, vab, zab, dob, sb_r, wgb_r, bgb_r, dqb, dkb, dvb, dzb, dwb, dbb, gt_b, t)
            return carry

        lax.fori_loop(0, nc, loop, 0)

    last_first = lambda i: (nb - 1 - i, 0)
    first_last = lambda i: (i, 0)
    s_shape = (nc, GLA_HEADS, HEAD_PAD, HEAD_PAD)

    def ins(m):
        return [pl.BlockSpec((br, hw), m), pl.BlockSpec((br, hw), m), pl.BlockSpec((br, hw), m),
                pl.BlockSpec((br, 128), m), pl.BlockSpec((br, hw), m),
                pl.BlockSpec(s_shape, lambda i: m(i) + (0, 0))]

    def outs(m):
        return [pl.BlockSpec((br, hw), m), pl.BlockSpec((br, hw), m), pl.BlockSpec((br, hw), m),
                pl.BlockSpec((br, 128), m), _full_spec((128, hw)), _full_spec((1, hw))]

    out_shape = [jax.ShapeDtypeStruct((L, hw), F32)] * 3 + [
        jax.ShapeDtypeStruct((L, 128), F32), jax.ShapeDtypeStruct((128, hw), F32),
        jax.ShapeDtypeStruct((1, hw), F32)]
    wspecs = [_full_spec((128, hw)), _full_spec((1, hw))] * 2
    return pl.pallas_call(
        body, name="gla_bwd", grid=(nb,),
        in_specs=ins(last_first) + ins(first_last) + wspecs,
        out_specs=outs(last_first) + outs(first_last),
        out_shape=out_shape + out_shape,
        scratch_shapes=[pltpu.VMEM(s_shape[1:], F32), pltpu.VMEM(s_shape[1:], F32)],
        compiler_params=_params(("arbitrary",), VMEM_BIG),
    )(qa, ka, va, za, do, sf, qa, ka, va, za, do, sb, wgf, bgf, wgb, bgb)


def _t5_buckets(rel):
    nb = REL_BUCKETS // 2
    ret = (rel > 0).astype(np.int32) * nb
    n = np.abs(rel)
    max_exact = nb // 2
    large = max_exact + (np.log(np.maximum(n, 1).astype(np.float32) / max_exact)
                         / math.log(REL_MAX_DIST / max_exact) * (nb - max_exact)).astype(np.int32)
    large = np.minimum(large, nb - 1)
    return ret + np.where(n < max_exact, n, large)


def _band_buckets():
    c = np.arange(SWA_BLOCK)[:, None]
    s = np.arange(3 * SWA_BLOCK)[None, :]
    return _t5_buckets(s - SWA_BLOCK - c).astype(np.int32)


def _swa_valid(n, seq_len):
    c = lax.broadcasted_iota(jnp.int32, (SWA_BLOCK, 3 * SWA_BLOCK), 0)
    s = lax.broadcasted_iota(jnp.int32, (SWA_BLOCK, 3 * SWA_BLOCK), 1)
    rel = s - SWA_BLOCK - c
    key_pos = (n - 1) * SWA_BLOCK + s
    return (jnp.abs(rel) <= SWA_BLOCK) & (key_pos >= 0) & (key_pos < seq_len)


def _swa_probs(q, kk, bias_h, sink_h, valid):
    lg = _dot_nt(q, kk) * 0.125 + bias_h
    lg = jnp.where(valid, lg, -1e30)
    m = jnp.maximum(jnp.max(lg, axis=-1, keepdims=True), sink_h)
    p = jnp.exp(lg - m)
    e_sink = jnp.exp(sink_h - m)
    inv = 1.0 / (jnp.sum(p, axis=-1, keepdims=True) + e_sink)
    return p * inv, e_sink * inv


def _swa_fwd_call(qs, ks, vs, bias, sink):
    L = qs.shape[0]
    group = SWA_Q_HEADS // SWA_KV_HEADS

    def body(q_r, k_r, v_r, bias_r, sink_r, o_r):
        n = pl.program_id(0)
        base = pl.multiple_of(n * SWA_BLOCK, SWA_BLOCK)
        valid = _swa_valid(n, L)
        for kv in range(SWA_KV_HEADS):
            ksl = slice(HEAD_PAD * kv, HEAD_PAD * (kv + 1))
            kk = k_r[pl.ds(base, 3 * SWA_BLOCK), ksl]
            vv = v_r[pl.ds(base, 3 * SWA_BLOCK), ksl]
            for g in range(group):
                h = kv * group + g
                sl = slice(HEAD_PAD * h, HEAD_PAD * (h + 1))
                pn, _ = _swa_probs(q_r[:, sl], kk, bias_r[h], sink_r[h], valid)
                o_r[:, sl] = _dot(_mx(pn), vv).astype(o_r.dtype)

    qw = SWA_Q_HEADS * HEAD_PAD
    return pl.pallas_call(
        body, name="swa_fwd", grid=(L // SWA_BLOCK,),
        in_specs=[_row_spec(SWA_BLOCK, qw), _vmem_spec(), _vmem_spec(), _vmem_spec(),
                  pl.BlockSpec(memory_space=pltpu.SMEM)],
        out_specs=_row_spec(SWA_BLOCK, qw),
        out_shape=jax.ShapeDtypeStruct((L, qw), MXU_DTYPE),
        compiler_params=_params(("arbitrary",), VMEM_BIG),
    )(qs, ks, vs, bias, sink)


def _swa_bwd_call(qs, ks, vs, bias, sink, do):
    L = qs.shape[0]
    group = SWA_Q_HEADS // SWA_KV_HEADS
    qw = SWA_Q_HEADS * HEAD_PAD
    kw = SWA_KV_HEADS * HEAD_PAD

    def body(q_r, k_r, v_r, bias_r, sink_r, do_r, dq_r, dk_r, dv_r, dbias_r, dsink_r):
        n = pl.program_id(0)

        @pl.when(n == 0)
        def _():
            for ref in (dk_r, dv_r, dbias_r, dsink_r):
                ref[...] = jnp.zeros_like(ref)

        base = pl.multiple_of(n * SWA_BLOCK, SWA_BLOCK)
        span = pl.ds(base, 3 * SWA_BLOCK)
        valid = _swa_valid(n, L)
        lane = lax.broadcasted_iota(jnp.int32, (1, 128), 1)
        dsink = jnp.zeros((1, 128), F32)
        for kv in range(SWA_KV_HEADS):
            ksl = slice(HEAD_PAD * kv, HEAD_PAD * (kv + 1))
            kk = k_r[span, ksl]
            vv = v_r[span, ksl]
            dk_acc = jnp.zeros((3 * SWA_BLOCK, HEAD_PAD), F32)
            dv_acc = jnp.zeros((3 * SWA_BLOCK, HEAD_PAD), F32)
            for g in range(group):
                h = kv * group + g
                sl = slice(HEAD_PAD * h, HEAD_PAD * (h + 1))
                q = q_r[:, sl]
                pn, p_sink = _swa_probs(q, kk, bias_r[h], sink_r[h], valid)
                do_h = do_r[:, sl]
                dp = _dot_nt(do_h, vv)
                delta = jnp.sum(pn * dp, axis=-1, keepdims=True)
                ds = pn * (dp - delta)
                dsink = dsink + jnp.where(lane == h, -jnp.sum(p_sink * delta), 0.0)
                dbias_r[h] += ds
                ds_m = _mx(ds)
                dq_r[:, sl] = _dot(ds_m, kk) * 0.125
                dk_acc = dk_acc + _dot_tn(ds_m, q) * 0.125
                dv_acc = dv_acc + _dot_tn(_mx(pn), do_h)
            dk_r[span, ksl] += dk_acc
            dv_r[span, ksl] += dv_acc
        dsink_r[...] += dsink

    return pl.pallas_call(
        body, name="swa_bwd", grid=(L // SWA_BLOCK,),
        in_specs=[_row_spec(SWA_BLOCK, qw), _vmem_spec(), _vmem_spec(), _vmem_spec(),
                  pl.BlockSpec(memory_space=pltpu.SMEM), _row_spec(SWA_BLOCK, qw)],
        out_specs=[_row_spec(SWA_BLOCK, qw), _vmem_spec(), _vmem_spec(), _vmem_spec(), _full_spec((1, 128))],
        out_shape=[jax.ShapeDtypeStruct((L, qw), F32),
                   jax.ShapeDtypeStruct((L + 2 * SWA_BLOCK, kw), F32),
                   jax.ShapeDtypeStruct((L + 2 * SWA_BLOCK, kw), F32),
                   jax.ShapeDtypeStruct((SWA_Q_HEADS, SWA_BLOCK, 3 * SWA_BLOCK), F32),
                   jax.ShapeDtypeStruct((1, 128), F32)],
        compiler_params=_params(("arbitrary",), VMEM_BIG),
    )(qs, ks, vs, bias, sink, do)


def _bias_call(rel_bias, buckets):
    def body(t_r, bk_r, o_r):
        bk = bk_r[...]
        for h in range(SWA_Q_HEADS):
            acc = jnp.zeros(bk.shape, F32)
            for b in range(REL_BUCKETS):
                acc = jnp.where(bk == b, t_r[b, h], acc)
            o_r[h] = acc

    return pl.pallas_call(
        body, name="band_bias",
        in_specs=[pl.BlockSpec(memory_space=pltpu.SMEM), _vmem_spec()], out_specs=_vmem_spec(),
        out_shape=jax.ShapeDtypeStruct((SWA_Q_HEADS,) + buckets.shape, F32),
    )(rel_bias, buckets)


def _relbias_call(dbias, buckets):
    def body(db_r, bk_r, o_r):
        bk = bk_r[...]
        rowi = lax.broadcasted_iota(jnp.int32, (REL_BUCKETS, 128), 0)
        lanei = lax.broadcasted_iota(jnp.int32, (REL_BUCKETS, 128), 1)
        acc = jnp.zeros((REL_BUCKETS, 128), F32)
        for b in range(REL_BUCKETS):
            m = bk == b
            for h in range(SWA_Q_HEADS):
                s = jnp.sum(jnp.where(m, db_r[h], 0.0))
                acc = acc + jnp.where((rowi == b) & (lanei == h), s, 0.0)
        o_r[...] = acc

    return pl.pallas_call(
        body, name="relbias_grad",
        in_specs=[_vmem_spec(), _vmem_spec()], out_specs=_vmem_spec(),
        out_shape=jax.ShapeDtypeStruct((REL_BUCKETS, 128), F32),
    )(dbias, buckets)


def _mix_call(o_f, o_b, ga, o_s, x, gn, w_out_p, g_post, g_pre2):
    L = x.shape[0]
    tm = min(256, L)
    hw = GLA_HEADS * HEAD_PAD

    def body(of_r, ob_r, ga_r, os_r, x_r, gn_r, w_r, gp_r, g2_r, cat_r, mix_r, h1_r, n2_r):
        gn_v = gn_r[...]
        for h in range(GLA_HEADS):
            sl = slice(HEAD_PAD * h, HEAD_PAD * (h + 1))
            oh = of_r[:, sl] + ob_r[:, sl]
            on = oh * _rms_r(oh) * gn_v
            gate = ga_r[:, sl]
            cat_r[:, sl] = (on * (gate * jax.nn.sigmoid(gate))).astype(cat_r.dtype)
        os_v = os_r[...]
        cat_r[:, hw:] = os_v
        mix = _dot(cat_r[:, :hw], w_r[:hw, :]) + _dot(os_v, w_r[hw:, :])
        mix_r[...] = mix
        h1 = x_r[...] + mix * _rms_r(mix) * gp_r[...]
        h1_r[...] = h1
        n2_r[...] = (h1 * _rms_r(h1) * g2_r[...]).astype(n2_r.dtype)

    return pl.pallas_call(
        body, name="mix_fwd", grid=(L // tm,),
        in_specs=[_row_spec(tm, hw), _row_spec(tm, hw), _row_spec(tm, hw), _row_spec(tm, OUT_PAD - hw),
                  _row_spec(tm, D_MODEL), _full_spec((1, HEAD_PAD)), _vmem_spec(),
                  _full_spec((1, D_MODEL)), _full_spec((1, D_MODEL))],
        out_specs=[_row_spec(tm, OUT_PAD), _row_spec(tm, D_MODEL), _row_spec(tm, D_MODEL), _row_spec(tm, D_MODEL)],
        out_shape=[jax.ShapeDtypeStruct((L, OUT_PAD), MXU_DTYPE), jax.ShapeDtypeStruct((L, D_MODEL), F32),
                   jax.ShapeDtypeStruct((L, D_MODEL), F32), jax.ShapeDtypeStruct((L, D_MODEL), MXU_DTYPE)],
        compiler_params=_params(("arbitrary",), VMEM_BIG),
    )(o_f, o_b, ga, o_s, x, gn, w_out_p, g_post, g_pre2)


def _mlp_fwd_call(n2, h1, tgt, w_up4, w_down4, g_post):
    L = n2.shape[0]
    tm = min(256, L)
    blk = D_FF // N_CHIPS

    def body(n2_r, h1_r, t_r, wu_r, wd_r, g_r, a_r, rz_r, dh2_r, dff_r, loss_r, dg_r):
        @pl.when(pl.program_id(0) == 0)
        def _():
            loss_r[...] = jnp.zeros_like(loss_r)
            dg_r[...] = jnp.zeros_like(dg_r)

        n2v = n2_r[...]
        ff = jnp.zeros((tm, D_MODEL), F32)
        for j in range(N_CHIPS):
            sl = slice(blk * j, blk * (j + 1))
            rz = jnp.maximum(_dot(n2v, wu_r[j]), 0.0)
            a = _mx(rz * rz)
            rz_r[:, sl] = rz.astype(rz_r.dtype)
            a_r[:, sl] = a
            ff = ff + _dot(a, wd_r[j])
        g = g_r[...]
        r = _rms_r(ff)
        err = h1_r[...] + ff * r * g - t_r[...]
        loss_r[...] += 0.5 * jnp.sum(err * err) / D_MODEL
        dh2 = err * (1.0 / D_MODEL)
        dh2_r[...] = dh2
        dff, dg = _rms_bwd(ff, r, g, dh2)
        dff_r[...] = dff.astype(dff_r.dtype)
        dg_r[...] += dg

    return pl.pallas_call(
        body, name="mlp_fwd", grid=(L // tm,),
        in_specs=[_row_spec(tm, D_MODEL), _row_spec(tm, D_MODEL), _row_spec(tm, D_MODEL),
                  _vmem_spec(), _vmem_spec(), _full_spec((1, D_MODEL))],
        out_specs=[_row_spec(tm, D_FF), _row_spec(tm, D_FF), _row_spec(tm, D_MODEL), _row_spec(tm, D_MODEL),
                   _full_spec((1, 128)), _full_spec((1, D_MODEL))],
        out_shape=[jax.ShapeDtypeStruct((L, D_FF), MXU_DTYPE), jax.ShapeDtypeStruct((L, D_FF), MXU_DTYPE),
                   jax.ShapeDtypeStruct((L, D_MODEL), F32), jax.ShapeDtypeStruct((L, D_MODEL), MXU_DTYPE),
                   jax.ShapeDtypeStruct((1, 128), F32), jax.ShapeDtypeStruct((1, D_MODEL), F32)],
        compiler_params=_params(("arbitrary",), VMEM_BIG),
    )(n2, h1, tgt, w_up4, w_down4, g_post)


def _mlp_bwd_call(dff, rz, w_up4, w_down4):
    L = dff.shape[0]
    tm = min(256, L)
    blk = D_FF // N_CHIPS

    def body(dff_r, rz_r, wu_r, wd_r, dz_r, dn2_r):
        dffv = dff_r[...]
        dn2 = jnp.zeros((tm, D_MODEL), F32)
        for j in range(N_CHIPS):
            sl = slice(blk * j, blk * (j + 1))
            dz = _mx(_dot_nt(dffv, wd_r[j]) * 2.0 * rz_r[:, sl].astype(F32))
            dz_r[:, sl] = dz
            dn2 = dn2 + _dot_nt(dz, wu_r[j])
        dn2_r[...] = dn2

    return pl.pallas_call(
        body, name="mlp_bwd", grid=(L // tm,),
        in_specs=[_row_spec(tm, D_MODEL), _row_spec(tm, D_FF), _vmem_spec(), _vmem_spec()],
        out_specs=[_row_spec(tm, D_FF), _row_spec(tm, D_MODEL)],
        out_shape=[jax.ShapeDtypeStruct((L, D_FF), MXU_DTYPE), jax.ShapeDtypeStruct((L, D_MODEL), F32)],
        compiler_params=_params(("arbitrary",), VMEM_BIG),
    )(dff, rz, w_up4, w_down4)


def _mlp_wgrad_call(a, dff, n2, dz):
    L = a.shape[0]
    tf = 512
    per = (D_FF // N_CHIPS) // tf

    def body(a_r, dff_r, n2_r, dz_r, dwd_r, dwu_r):
        dwd_r[...] = _dot_tn(a_r[...], dff_r[...])
        dwu_r[...] = _dot_tn(n2_r[...], dz_r[...])

    return pl.pallas_call(
        body, name="mlp_wgrad", grid=(D_FF // tf,),
        in_specs=[pl.BlockSpec((L, tf), lambda j: (0, j)), _vmem_spec(), _vmem_spec(),
                  pl.BlockSpec((L, tf), lambda j: (0, j))],
        out_specs=[pl.BlockSpec((tf, D_MODEL), lambda j: (j, 0)),
                   pl.BlockSpec((None, D_MODEL, tf), lambda j: (j // per, 0, j % per))],
        out_shape=[jax.ShapeDtypeStruct((D_FF, D_MODEL), F32),
                   jax.ShapeDtypeStruct((N_CHIPS, D_MODEL, D_FF // N_CHIPS), F32)],
        compiler_params=_params(("arbitrary",), VMEM_BIG),
    )(a, dff, n2, dz)


def _mix_bwd_call(dn2, dh2, h1, mix, cat, o_f, o_b, ga, gn, g_post, g_pre2, w_out_p):
    L = dn2.shape[0]
    tm = min(256, L)
    hw = GLA_HEADS * HEAD_PAD

    def body(dn2_r, dh2_r, h1_r, mix_r, cat_r, of_r, ob_r, ga_r, gn_r, gp_r, g2_r, w_r,
             dh1_r, do_r, dga_r, dos_r, dw_r, dg2_r, dgp_r, dgn_r):
        @pl.when(pl.program_id(0) == 0)
        def _():
            for ref in (dw_r, dg2_r, dgp_r, dgn_r):
                ref[...] = jnp.zeros_like(ref)

        h1 = h1_r[...]
        dx2, dg2 = _rms_bwd(h1, _rms_r(h1), g2_r[...], dn2_r[...])
        dh1 = dh2_r[...] + dx2
        dh1_r[...] = dh1
        dg2_r[...] += dg2
        mix = mix_r[...]
        dmix, dgp = _rms_bwd(mix, _rms_r(mix), gp_r[...], dh1)
        dgp_r[...] += dgp
        dmix_m = _mx(dmix)
        dw_r[...] += _dot_tn(cat_r[...], dmix_m)
        dcat = _dot_nt(dmix_m, w_r[...])
        dos_r[...] = dcat[:, hw:].astype(dos_r.dtype)
        gn_v = gn_r[...]
        dgn = jnp.zeros((1, HEAD_PAD), F32)
        for h in range(GLA_HEADS):
            sl = slice(HEAD_PAD * h, HEAD_PAD * (h + 1))
            oh = of_r[:, sl] + ob_r[:, sl]
            rr = _rms_r(oh)
            gate = ga_r[:, sl]
            sg = jax.nn.sigmoid(gate)
            doa = dcat[:, sl]
            dga_r[:, sl] = doa * (oh * rr * gn_v) * (sg * (1.0 + gate * (1.0 - sg)))
            do_h, dgn_h = _rms_bwd(oh, rr, gn_v, doa * (gate * sg))
            do_r[:, sl] = do_h
            dgn = dgn + dgn_h
        dgn_r[...] += dgn

    return pl.pallas_call(
        body, name="mix_bwd", grid=(L // tm,),
        in_specs=[_row_spec(tm, D_MODEL)] * 4 + [_row_spec(tm, OUT_PAD)] + [_row_spec(tm, hw)] * 3
        + [_full_spec((1, HEAD_PAD)), _full_spec((1, D_MODEL)), _full_spec((1, D_MODEL)), _vmem_spec()],
        out_specs=[_row_spec(tm, D_MODEL), _row_spec(tm, hw), _row_spec(tm, hw), _row_spec(tm, OUT_PAD - hw),
                   _full_spec((OUT_PAD, D_MODEL)), _full_spec((1, D_MODEL)), _full_spec((1, D_MODEL)),
                   _full_spec((1, HEAD_PAD))],
        out_shape=[jax.ShapeDtypeStruct((L, D_MODEL), F32), jax.ShapeDtypeStruct((L, hw), F32),
                   jax.ShapeDtypeStruct((L, hw), F32), jax.ShapeDtypeStruct((L, OUT_PAD - hw), MXU_DTYPE),
                   jax.ShapeDtypeStruct((OUT_PAD, D_MODEL), F32), jax.ShapeDtypeStruct((1, D_MODEL), F32),
                   jax.ShapeDtypeStruct((1, D_MODEL), F32), jax.ShapeDtypeStruct((1, HEAD_PAD), F32)],
        compiler_params=_params(("arbitrary",), VMEM_BIG),
    )(dn2, dh2, h1, mix, cat, o_f, o_b, ga, gn, g_post, g_pre2, w_out_p)


def _in_bwd_call(x, dh1, g_pre, w_in_p, pairs, singles):
    L = x.shape[0]
    tm = min(256, L)
    n_pair, n_single = len(pairs), len(singles)
    groups = [c for c, _ in pairs] + [c for c, _ in singles]

    def body(*refs):
        x_r, dh1_r, g_r, w_r = refs[:4]
        pair_refs = refs[4:4 + 2 * n_pair]
        single_refs = refs[4 + 2 * n_pair:4 + 2 * n_pair + n_single]
        dx_r, dw_r, dg_r = refs[4 + 2 * n_pair + n_single:]

        @pl.when(pl.program_id(0) == 0)
        def _():
            dw_r[...] = jnp.zeros_like(dw_r)
            dg_r[...] = jnp.zeros_like(dg_r)

        xv = x_r[...]
        r = _rms_r(xv)
        g = g_r[...]
        u = _mx(xv * r * g)
        vals = [pair_refs[2 * i][...] + pair_refs[2 * i + 1][...] for i in range(n_pair)]
        vals += [ref[...].astype(F32) for ref in single_refs]
        du = jnp.zeros((tm, D_MODEL), F32)
        for (off, width), val in zip(groups, vals):
            d = _mx(val)
            du = du + _dot_nt(d, w_r[:, off:off + width])
            dw_r[:, off:off + width] += _dot_tn(u, d)
        dx, dg = _rms_bwd(xv, r, g, du)
        dx_r[...] = dh1_r[...] + dx
        dg_r[...] += dg

    arrays = [a for _, pr in pairs for a in pr] + [a for _, a in singles]
    specs = [_row_spec(tm, a.shape[1]) for a in arrays]
    return pl.pallas_call(
        body, name="in_bwd", grid=(L // tm,),
        in_specs=[_row_spec(tm, D_MODEL), _row_spec(tm, D_MODEL), _full_spec((1, D_MODEL)), _vmem_spec()] + specs,
        out_specs=[_row_spec(tm, D_MODEL), _full_spec((D_MODEL, IN_PAD)), _full_spec((1, D_MODEL))],
        out_shape=[jax.ShapeDtypeStruct((L, D_MODEL), F32), jax.ShapeDtypeStruct((D_MODEL, IN_PAD), F32),
                   jax.ShapeDtypeStruct((1, D_MODEL), F32)],
        compiler_params=_params(("arbitrary",), VMEM_BIG),
    )(x, dh1, g_pre, w_in_p, *arrays)


def _adamw_math(w, g, m, v):
    m = ADAM_B1 * m + (1.0 - ADAM_B1) * g
    v = ADAM_B2 * v + (1.0 - ADAM_B2) * (g * g)
    m_hat = m / (1.0 - ADAM_B1 ** ADAM_STEP)
    v_hat = v / (1.0 - ADAM_B2 ** ADAM_STEP)
    delta = -ADAM_LR * (m_hat / (jnp.sqrt(v_hat) + ADAM_EPS) + ADAM_WD * w)
    return delta, m, v


def _adamw_call(w, g, m, v, name):
    rows, cols = w.shape
    tr = min(256, rows)

    def body(w_r, g_r, m_r, v_r, d_r, nm_r, nv_r):
        d_r[...], nm_r[...], nv_r[...] = _adamw_math(w_r[...], g_r[...], m_r[...], v_r[...])

    spec = _row_spec(tr, cols)
    return pl.pallas_call(
        body, name=name, grid=(rows // tr,),
        in_specs=[spec] * 4, out_specs=[spec] * 3,
        out_shape=[jax.ShapeDtypeStruct(w.shape, F32)] * 3,
        compiler_params=_params(("arbitrary",)),
    )(w, g, m, v)


def _position():
    return lax.axis_index("x"), lax.axis_index("y"), lax.axis_index("c")


def _other_chips(x, y):
    return [(1 - x, y), (x, 1 - y), (1 - x, 1 - y)]


def _any_spec():
    return pl.BlockSpec(memory_space=pl.ANY)


def _weight_gather_call(packed):
    rows = packed.shape[0]
    half = rows // 2

    def body(p_ref, out_ref, send_sems, recv_sems, local_sem):
        x, y, c = _position()
        sibling = (x, y, 1 - c)
        chips = _other_chips(x, y)

        def piece(px, py, pc):
            return out_ref.at[2 * px + py, pl.ds(pl.multiple_of(pc * half, 16), half), :]

        def copy(k, block, to, src=None):
            return pltpu.make_async_remote_copy(
                src_ref=piece(*block) if src is None else src, dst_ref=piece(*block),
                send_sem=send_sems.at[k], recv_sem=recv_sems.at[k], device_id=to, device_id_type=MESH_ID)

        mine = pltpu.make_async_copy(p_ref, out_ref.at[2 * x + y], local_sem)
        mine.start()
        my_half = p_ref.at[pl.ds(pl.multiple_of(c * half, 16), half), :]
        first = [copy(j, (x, y, c), (*chip, c), src=my_half) for j, chip in enumerate(chips)]
        for cp in first:
            cp.start()
        passed = [copy(3 + j, (*chip, c), sibling) for j, chip in enumerate(chips)]
        for j, chip in enumerate(chips):
            copy(j, (*chip, c), (x, y, c)).wait_recv()
            passed[j].start()
        for j, chip in enumerate(chips):
            copy(3 + j, (*chip, 1 - c), (x, y, c)).wait_recv()
        for cp in first + passed:
            cp.wait_send()
        mine.wait()

    return pl.pallas_call(
        body, name="weight_gather",
        in_specs=[_any_spec()], out_specs=_any_spec(),
        out_shape=jax.ShapeDtypeStruct((N_CHIPS, rows, packed.shape[1]), packed.dtype),
        scratch_shapes=[pltpu.SemaphoreType.DMA((6,)), pltpu.SemaphoreType.DMA((6,)), pltpu.SemaphoreType.DMA],
    )(packed)


def _grad_pair_swap_call(g):
    half = g.shape[1] // 2

    def body(g_ref, out_ref, send_sem, recv_sem):
        x, y, c = _position()
        theirs = g_ref.at[:, pl.ds(pl.multiple_of((1 - c) * half, 16), half), :]
        cp = pltpu.make_async_remote_copy(src_ref=theirs, dst_ref=out_ref, send_sem=send_sem, recv_sem=recv_sem,
                                          device_id=(x, y, 1 - c), device_id_type=MESH_ID)
        cp.start()
        cp.wait()

    return pl.pallas_call(
        body, name="grad_pair_swap",
        in_specs=[_any_spec()], out_specs=_any_spec(),
        out_shape=jax.ShapeDtypeStruct((N_CHIPS, half, g.shape[2]), g.dtype),
        scratch_shapes=[pltpu.SemaphoreType.DMA, pltpu.SemaphoreType.DMA],
    )(g)


def _grad_pair_add_call(g, got, c_idx):
    half = got.shape[1]
    nblk = half // ADD_ROWS

    def body(c_ref, g_r, got_r, o_r):
        o_r[...] = (g_r[...] + got_r[...]).astype(o_r.dtype)

    return pl.pallas_call(
        body, name="grad_pair_add",
        grid_spec=pltpu.PrefetchScalarGridSpec(
            num_scalar_prefetch=1, grid=(N_CHIPS, nblk),
            in_specs=[pl.BlockSpec((None, ADD_ROWS, D_MODEL), lambda j, i, c_ref: (j, c_ref[0] * nblk + i, 0)),
                      pl.BlockSpec((None, ADD_ROWS, D_MODEL), lambda j, i, c_ref: (j, i, 0))],
            out_specs=pl.BlockSpec((None, ADD_ROWS, D_MODEL), lambda j, i, c_ref: (j, i, 0))),
        out_shape=jax.ShapeDtypeStruct((N_CHIPS, half, D_MODEL), COMM_DTYPE),
        compiler_params=_params(("arbitrary", "arbitrary")),
    )(c_idx, g, got)


def _grad_chip_swap_call(hsum):
    def body(h_ref, out_ref, send_sems, recv_sems):
        x, y, c = _position()
        copies = [pltpu.make_async_remote_copy(
            src_ref=h_ref.at[2 * px + py], dst_ref=out_ref.at[j], send_sem=send_sems.at[j],
            recv_sem=recv_sems.at[j], device_id=(px, py, c), device_id_type=MESH_ID)
            for j, (px, py) in enumerate(_other_chips(x, y))]
        for cp in copies:
            cp.start()
        for cp in copies:
            cp.wait()

    return pl.pallas_call(
        body, name="grad_chip_swap",
        in_specs=[_any_spec()], out_specs=_any_spec(),
        out_shape=jax.ShapeDtypeStruct((3,) + hsum.shape[1:], hsum.dtype),
        scratch_shapes=[pltpu.SemaphoreType.DMA((3,)), pltpu.SemaphoreType.DMA((3,))],
    )(hsum)


def _grad_chip_add_call(hsum, got, chip_idx):
    half = hsum.shape[1]

    def body(s_ref, own_r, got_r, o_r):
        acc = own_r[...].astype(F32)
        for j in range(3):
            acc = acc + got_r[j].astype(F32)
        o_r[...] = acc

    return pl.pallas_call(
        body, name="grad_chip_add",
        grid_spec=pltpu.PrefetchScalarGridSpec(
            num_scalar_prefetch=1, grid=(half // ADD_ROWS,),
            in_specs=[pl.BlockSpec((None, ADD_ROWS, D_MODEL), lambda i, s_ref: (s_ref[0], i, 0)),
                      pl.BlockSpec((3, ADD_ROWS, D_MODEL), lambda i, s_ref: (0, i, 0))],
            out_specs=pl.BlockSpec((ADD_ROWS, D_MODEL), lambda i, s_ref: (i, 0))),
        out_shape=jax.ShapeDtypeStruct((half, D_MODEL), F32),
        compiler_params=_params(("arbitrary",)),
    )(chip_idx, hsum, got)


def _grad_pair_join_call(red):
    half = red.shape[0]

    def body(r_ref, out_ref, send_sem, recv_sem, local_sem):
        x, y, c = _position()
        mine = out_ref.at[pl.ds(pl.multiple_of(c * half, 8), half), :]
        theirs = out_ref.at[pl.ds(pl.multiple_of((1 - c) * half, 8), half), :]
        local = pltpu.make_async_copy(r_ref, mine, local_sem)
        local.start()
        send = pltpu.make_async_remote_copy(src_ref=r_ref, dst_ref=mine, send_sem=send_sem, recv_sem=recv_sem,
                                            device_id=(x, y, 1 - c), device_id_type=MESH_ID)
        send.start()
        send.wait_send()
        pltpu.make_async_remote_copy(src_ref=r_ref, dst_ref=theirs, send_sem=send_sem, recv_sem=recv_sem,
                                     device_id=(x, y, 1 - c), device_id_type=MESH_ID).wait_recv()
        local.wait()

    return pl.pallas_call(
        body, name="grad_pair_join",
        in_specs=[_any_spec()], out_specs=_any_spec(),
        out_shape=jax.ShapeDtypeStruct((2 * half, red.shape[1]), red.dtype),
        scratch_shapes=[pltpu.SemaphoreType.DMA, pltpu.SemaphoreType.DMA, pltpu.SemaphoreType.DMA],
    )(red)


def _small_allreduce_call(small):
    n_dev = 8

    def body(s_ref, sum_ref, all_ref, send_sems, recv_sems):
        x, y, c = _position()
        me = 4 * x + 2 * y + c
        all_ref[me] = s_ref[...]
        copies = []
        for k in range(1, n_dev):
            fx, fy, fc = (k >> 2) & 1, (k >> 1) & 1, k & 1
            to = (1 - x if fx else x, 1 - y if fy else y, 1 - c if fc else c)
            copies.append(pltpu.make_async_remote_copy(
                src_ref=s_ref, dst_ref=all_ref.at[me], send_sem=send_sems.at[k - 1],
                recv_sem=recv_sems.at[k - 1], device_id=to, device_id_type=MESH_ID))
        for cp in copies:
            cp.start()
        for cp in copies:
            cp.wait()
        acc = all_ref[0]
        for d in range(1, n_dev):
            acc = acc + all_ref[d]
        sum_ref[...] = acc

    return pl.pallas_call(
        body, name="small_allreduce",
        in_specs=[_vmem_spec()], out_specs=_vmem_spec(),
        out_shape=jax.ShapeDtypeStruct(small.shape, F32),
        scratch_shapes=[pltpu.VMEM((n_dev,) + small.shape, F32),
                        pltpu.SemaphoreType.DMA((n_dev - 1,)), pltpu.SemaphoreType.DMA((n_dev - 1,))],
    )(small)


def _pad_heads(t, n_heads, axis=-1):
    axis = axis % t.ndim
    shape = t.shape
    t = t.reshape(shape[:axis] + (n_heads, 64) + shape[axis + 1:])
    pad = [(0, 0)] * t.ndim
    pad[axis + 1] = (0, HEAD_PAD - 64)
    return jnp.pad(t, pad).reshape(shape[:axis] + (n_heads * HEAD_PAD,) + shape[axis + 1:])


def _unpad_heads(t, n_heads, axis=-1):
    axis = axis % t.ndim
    shape = t.shape
    t = t.reshape(shape[:axis] + (n_heads, HEAD_PAD) + shape[axis + 1:])
    t = lax.slice_in_dim(t, 0, 64, axis=axis + 1)
    return t.reshape(shape[:axis] + (n_heads * 64,) + shape[axis + 1:])


def _pad_w_in(w):
    return jnp.concatenate([
        _pad_heads(w[:, 0:256], 4), _pad_heads(w[:, 256:512], 4), w[:, 512:1024], w[:, 1024:1536],
        _pad_heads(w[:, 1568:2080], 8), _pad_heads(w[:, 2080:2208], 2), _pad_heads(w[:, 2208:2336], 2),
        jnp.pad(w[:, 1536:1568], ((0, 0), (0, 96)))], axis=1)


def _unpad_w_in(g):
    return jnp.concatenate([
        _unpad_heads(g[:, 0:512], 4), _unpad_heads(g[:, 512:1024], 4), g[:, 1024:1536], g[:, 1536:2048],
        g[:, 3584:3616], _unpad_heads(g[:, 2048:3072], 8), _unpad_heads(g[:, 3072:3328], 2),
        _unpad_heads(g[:, 3328:3584], 2)], axis=1)


def _pad_w_out(w):
    return jnp.concatenate([w[:512], _pad_heads(w[512:], 8, axis=0)], axis=0)


def _unpad_w_out(g):
    return jnp.concatenate([g[:512], _unpad_heads(g[512:], 8, axis=0)], axis=0)


def _pad_gate(w, first_row):
    return jnp.pad(_pad_heads(w, 4), ((first_row, 128 - GLA_GATE_RANK - first_row), (0, 0)))


def _pack_rows(parts, dtype):
    rows = [p.reshape(-1, D_MODEL).astype(dtype) for p in parts]
    used = sum(r.shape[0] for r in rows)
    rows.append(jnp.zeros((PACK_ROWS - used, D_MODEL), dtype))
    return jnp.concatenate(rows, axis=0)


def _unpack_rows(packed):
    o0, o1, o2, o3 = R_IN, R_IN + R_OUT, R_IN + R_OUT + R_UP, R_IN + R_OUT + R_UP + R_DOWN
    return (packed[..., 0:o0, :], packed[..., o0:o1, :], packed[..., o1:o2, :], packed[..., o2:o3, :],
            packed[..., o3, :], packed[..., o3 + 1, :])


def _local_step(x, tgt, w_in, w_out, w_up4, w_down4, wg_f, wg_b, b_f, b_b, gn, sink, rel_bias,
                g_pre, g_post, g_pre2, g_post2):
    L = x.shape[0]
    w_in_p = _pad_w_in(w_in)
    w_out_p = _pad_w_out(w_out)
    wgf_p, wgb_p = _pad_gate(wg_f, 0), _pad_gate(wg_b, GLA_GATE_RANK)
    bf_p, bb_p = _pad_heads(b_f, 4), _pad_heads(b_b, 4)
    buckets = jnp.asarray(_band_buckets())
    bias = _bias_call(rel_bias, buckets)
    sink1 = sink.reshape(SWA_Q_HEADS)

    qa, ka, va, ga, qs, ks, vs, za = _proj_call(x, g_pre, w_in_p)
    halo = ((SWA_BLOCK, SWA_BLOCK), (0, 0))
    ks_p, vs_p = jnp.pad(ks, halo), jnp.pad(vs, halo)
    o_f, o_b, s_f, s_b = _gla_fwd_call(qa, ka, va, za, wgf_p, bf_p, wgb_p, bb_p)
    o_s = _swa_fwd_call(qs, ks_p, vs_p, bias, sink1)
    cat, mix, h1, n2 = _mix_call(o_f, o_b, ga, o_s, x, gn, w_out_p, g_post, g_pre2)
    a, rz, dh2, dff, loss, d_post2 = _mlp_fwd_call(n2, h1, tgt, w_up4, w_down4, g_post2)

    dz, dn2 = _mlp_bwd_call(dff, rz, w_up4, w_down4)
    dw_down, dw_up4 = _mlp_wgrad_call(a, dff, n2, dz)
    dh1, do, dga, dos, dw_out_p, d_pre2, d_post, d_gn = _mix_bwd_call(
        dn2, dh2, h1, mix, cat, o_f, o_b, ga, gn, g_post, g_pre2, w_out_p)
    (dqf, dkf, dvf, dzf, dwf, dbf, dqb, dkb, dvb, dzb, dwb, dbb) = _gla_bwd_call(
        qa, ka, va, za, do, s_f, s_b, wgf_p, bf_p, wgb_p, bb_p)
    dqs, dks_p, dvs_p, dbias, dsink = _swa_bwd_call(qs, ks_p, vs_p, bias, sink1, dos)
    drel = _relbias_call(dbias, buckets)
    dks = dks_p[SWA_BLOCK:SWA_BLOCK + L]
    dvs = dvs_p[SWA_BLOCK:SWA_BLOCK + L]
    dx, dw_in_p, d_pre = _in_bwd_call(
        x, dh1, g_pre, w_in_p,
        pairs=[(C_QA, (dqf, dqb)), (C_KA, (dkf, dkb)), (C_VA, (dvf, dvb)), (C_ZA, (dzf, dzb))],
        singles=[(C_GA, dga), (C_QS, dqs), (C_KS, dks), (C_VS, dvs)])

    dwg_f = _unpad_heads(dwf[0:GLA_GATE_RANK], 4)
    dwg_b = _unpad_heads(dwb[GLA_GATE_RANK:2 * GLA_GATE_RANK], 4)
    small = dict(pre=d_pre, post=d_post, pre2=d_pre2, post2=d_post2, b_f=_unpad_heads(dbf, 4),
                 b_b=_unpad_heads(dbb, 4), gn=d_gn, sink=dsink[:, :SWA_Q_HEADS], rel=drel[:, :SWA_Q_HEADS])
    return loss, dx, _unpad_w_in(dw_in_p), _unpad_w_out(dw_out_p), dw_up4, dw_down, dwg_f, dwg_b, small


SMALL_ROWS = 8


def _pack_small(pre, post, pre2, post2, b_f, b_b, gn, sink, rel, wg_f, wg_b):
    row4 = jnp.concatenate([b_f.reshape(1, 256), b_b.reshape(1, 256), gn.reshape(1, 128),
                            jnp.pad(sink.reshape(1, 8), ((0, 0), (0, 120))), jnp.zeros((1, 256), F32)], axis=1)
    row5 = jnp.pad(rel.reshape(1, 256), ((0, 0), (0, 768)))
    return jnp.concatenate([pre.reshape(1, -1), post.reshape(1, -1), pre2.reshape(1, -1), post2.reshape(1, -1),
                            row4, row5, wg_f.reshape(1, -1), wg_b.reshape(1, -1)], axis=0)


def _unpack_small(p):
    return dict(pre=p[0:1], post=p[1:2], pre2=p[2:3], post2=p[3:4], b_f=p[4:5, 0:256], b_b=p[4:5, 256:512],
                gn=p[4:5, 512:640], sink=p[4:5, 640:648], rel=p[5, 0:256].reshape(REL_BUCKETS, SWA_Q_HEADS),
                wg_f=p[6].reshape(GLA_GATE_RANK, 64), wg_b=p[7].reshape(GLA_GATE_RANK, 64))


def kernel(x, norm_mix_pre, w_in, w_gate_up_fwd, b_gate_fwd, w_gate_up_bwd, b_gate_bwd, gla_norm, swa_sink, rel_bias, w_out, norm_mix_post, norm_mlp_pre, w_up, w_down, norm_mlp_post, loss_target, m_norm_mix_pre, m_w_in, m_w_gate_up_fwd, m_b_gate_fwd, m_w_gate_up_bwd, m_b_gate_bwd, m_gla_norm, m_swa_sink, m_rel_bias, m_w_out, m_norm_mix_post, m_norm_mlp_pre, m_w_up, m_w_down, m_norm_mlp_post, v_norm_mix_pre, v_w_in, v_w_gate_up_fwd, v_b_gate_fwd, v_w_gate_up_bwd, v_b_gate_bwd, v_gla_norm, v_swa_sink, v_rel_bias, v_w_out, v_norm_mix_post, v_norm_mlp_pre, v_w_up, v_w_down, v_norm_mlp_post):
    cx, cy, cc = _position()
    c_idx = cc.astype(jnp.int32).reshape(1)
    chip_idx = (2 * cx + cy).astype(jnp.int32).reshape(1)

    packed = _pack_rows([w_in[0], w_out[0], w_up[0], w_down[0], w_gate_up_fwd[0], w_gate_up_bwd[0]], COMM_DTYPE)
    gathered = _weight_gather_call(packed)
    p_in, p_out, p_up, p_down, p_gf, p_gb = _unpack_rows(gathered)
    w_in_full = jnp.concatenate([p_in[j].reshape(D_MODEL, R_IN) for j in range(N_CHIPS)], axis=1)
    w_out_full = p_out.reshape(N_CHIPS * R_OUT, D_MODEL)
    wg_f_full = jnp.concatenate([p_gf[j].reshape(GLA_GATE_RANK, 64) for j in range(N_CHIPS)], axis=1)
    wg_b_full = jnp.concatenate([p_gb[j].reshape(GLA_GATE_RANK, 64) for j in range(N_CHIPS)], axis=1)

    loss, dx, dw_in, dw_out, dw_up4, dw_down, dwg_f, dwg_b, small = _local_step(
        x[0], loss_target[0], _mx(w_in_full), _mx(w_out_full), _mx(p_up), _mx(p_down), _mx(wg_f_full),
        _mx(wg_b_full), b_gate_fwd, b_gate_bwd, gla_norm, swa_sink, rel_bias,
        norm_mix_pre, norm_mix_post, norm_mlp_pre, norm_mlp_post)

    g_pack = jnp.stack([_pack_rows([dw_in[:, R_IN * j:R_IN * (j + 1)], dw_out[R_OUT * j:R_OUT * (j + 1)],
                                    dw_up4[j], dw_down[R_DOWN * j:R_DOWN * (j + 1)],
                                    dwg_f[:, 64 * j:64 * (j + 1)], dwg_b[:, 64 * j:64 * (j + 1)]], F32)
                        for j in range(N_CHIPS)], axis=0)
    from_sibling = _grad_pair_swap_call(g_pack)
    pair_sum = _grad_pair_add_call(g_pack, from_sibling, c_idx)
    from_chips = _grad_chip_swap_call(pair_sum)
    reduced_half = _grad_chip_add_call(pair_sum, from_chips, chip_idx)
    reduced = _grad_pair_join_call(reduced_half)
    g_in, g_out, g_up, g_down, g_gf, g_gb = _unpack_rows(reduced)
    g_in = g_in.reshape(D_MODEL, R_IN)

    zeros_row = jnp.zeros((GLA_GATE_RANK, 64), F32)
    small_sum = _small_allreduce_call(_pack_small(
        small["pre"], small["post"], small["pre2"], small["post2"], small["b_f"], small["b_b"], small["gn"],
        small["sink"], small["rel"], zeros_row, zeros_row))
    g_small = jnp.concatenate([small_sum[0:6], g_gf.reshape(1, -1), g_gb.reshape(1, -1)], axis=0)

    big = [("w_in", w_in[0], g_in, m_w_in[0], v_w_in[0]), ("w_out", w_out[0], g_out, m_w_out[0], v_w_out[0]),
           ("w_up", w_up[0], g_up, m_w_up[0], v_w_up[0]), ("w_down", w_down[0], g_down, m_w_down[0], v_w_down[0])]
    upd = {name: (g,) + tuple(_adamw_call(w, g, m, v, "adamw_" + name)) for name, w, g, m, v in big}
    small_w = _pack_small(norm_mix_pre, norm_mix_post, norm_mlp_pre, norm_mlp_post, b_gate_fwd, b_gate_bwd,
                          gla_norm, swa_sink, rel_bias, w_gate_up_fwd, w_gate_up_bwd)
    small_m = _pack_small(m_norm_mix_pre, m_norm_mix_post, m_norm_mlp_pre, m_norm_mlp_post, m_b_gate_fwd,
                          m_b_gate_bwd, m_gla_norm, m_swa_sink, m_rel_bias, m_w_gate_up_fwd, m_w_gate_up_bwd)
    small_v = _pack_small(v_norm_mix_pre, v_norm_mix_post, v_norm_mlp_pre, v_norm_mlp_post, v_b_gate_fwd,
                          v_b_gate_bwd, v_gla_norm, v_swa_sink, v_rel_bias, v_w_gate_up_fwd, v_w_gate_up_bwd)
    s_delta, s_m, s_v = _adamw_call(small_w, g_small, small_m, small_v, "adamw_small")
    kinds = [_unpack_small(t) for t in (g_small, s_delta, s_m, s_v)]

    def leaf(kind, name):
        k = kinds[kind]
        table = {
            "norm_mix_pre": lambda: k["pre"], "w_in": lambda: upd["w_in"][kind][None],
            "w_gate_up_fwd": lambda: k["wg_f"][None], "b_gate_fwd": lambda: k["b_f"],
            "w_gate_up_bwd": lambda: k["wg_b"][None], "b_gate_bwd": lambda: k["b_b"],
            "gla_norm": lambda: k["gn"], "swa_sink": lambda: k["sink"], "rel_bias": lambda: k["rel"],
            "w_out": lambda: upd["w_out"][kind][None], "norm_mix_post": lambda: k["post"],
            "norm_mlp_pre": lambda: k["pre2"], "w_up": lambda: upd["w_up"][kind][None],
            "w_down": lambda: upd["w_down"][kind][None], "norm_mlp_post": lambda: k["post2"]}
        return table[name]()

    names = ["norm_mix_pre", "w_in", "w_gate_up_fwd", "b_gate_fwd", "w_gate_up_bwd", "b_gate_bwd", "gla_norm",
             "swa_sink", "rel_bias", "w_out", "norm_mix_post", "norm_mlp_pre", "w_up", "w_down", "norm_mlp_post"]
    total_loss = lax.psum(loss[0, 0], MESH_AXES)
    outs = [total_loss, dx[None]]
    for kind in range(4):
        outs += [leaf(kind, n) for n in names]
    return tuple(outs)
```

```python
import collections
import math

import numpy as np
import jax
import jax.numpy as jnp
from jax import lax
from jax.experimental import pallas as pl
from jax.experimental.pallas import tpu as pltpu

F32 = jnp.float32
MXU_DTYPE = jnp.bfloat16
COMM_DTYPE = jnp.bfloat16

D_MODEL = 1024
D_FF = 4096
N_CHIPS = 4
GLA_HEADS = 4
GLA_CHUNK = 64
GLA_GATE_RANK = 16
GLA_GATE_NORM = 16.0
SWA_Q_HEADS = 8
SWA_KV_HEADS = 2
SWA_BLOCK = 128
REL_BUCKETS = 32
REL_MAX_DIST = 128
NORM_EPS = 1e-6
HEAD_PAD = 128

ADAM_LR = 0.001
ADAM_B1 = 0.9
ADAM_B2 = 0.999
ADAM_EPS = 1e-08
ADAM_WD = 0.01
ADAM_STEP = 10

C_QA, C_KA, C_VA, C_GA = (0, 512), (512, 512), (1024, 512), (1536, 512)
C_QS, C_KS, C_VS, C_ZA = (2048, 1024), (3072, 256), (3328, 256), (3584, 128)
IN_PAD = 3712
OUT_PAD = 1536

R_IN, R_OUT, R_UP, R_DOWN = 584, 256, 1024, 1024
PACK_USED = R_IN + R_OUT + R_UP + R_DOWN + 2
PACK_ROWS = 2944
PACK_HALF = PACK_ROWS // 2
ADD_ROWS = 368

VMEM_BIG = 56 * 1024 * 1024
MESH_AXES = ("x", "y", "c")
MESH_ID = pl.DeviceIdType.MESH


def _mx(a):
    return a.astype(MXU_DTYPE)


def _dot(a, b):
    return jnp.dot(a, b, preferred_element_type=F32)


def _dot_nt(a, b):
    return lax.dot_general(a, b, (((1,), (1,)), ((), ())), preferred_element_type=F32)


def _dot_tn(a, b):
    return lax.dot_general(a, b, (((0,), (0,)), ((), ())), preferred_element_type=F32)


def _dot_exact(a, b):
    return jnp.dot(a, b, precision=lax.Precision.HIGHEST, preferred_element_type=F32)


def _dot_tn_exact(a, b):
    return lax.dot_general(a, b, (((0,), (0,)), ((), ())), precision=lax.Precision.HIGHEST,
                           preferred_element_type=F32)


def _rms_r(x):
    return lax.rsqrt(jnp.mean(x * x, axis=-1, keepdims=True) + NORM_EPS)


def _rms_bwd(x, r, g, dy):
    xh = x * r
    gdy = dy * g
    dx = r * (gdy - xh * jnp.mean(gdy * xh, axis=-1, keepdims=True))
    return dx, jnp.sum(dy * xh, axis=0, keepdims=True)


def _params(sem=None, vmem=None):
    kw = {}
    if sem is not None:
        kw["dimension_semantics"] = sem
    if vmem is not None:
        kw["vmem_limit_bytes"] = vmem
    return pltpu.CompilerParams(**kw)


def _vmem_spec():
    return pl.BlockSpec(memory_space=pltpu.VMEM)


def _row_spec(tm, width):
    return pl.BlockSpec((tm, width), lambda i: (i, 0))


def _full_spec(shape):
    return pl.BlockSpec(shape, lambda i: (0,) * len(shape))


def _any_spec():
    return pl.BlockSpec(memory_space=pl.ANY)


def _after(body, n_in, dep):
    if dep is None:
        return body, [], []
    return (lambda *refs: body(*refs[:n_in], *refs[n_in + 1:])), [dep], [_any_spec()]


def _proj_call(x, g_pre, w_in_p, dep=None):
    L = x.shape[0]
    tm = min(256, L)
    groups = [(C_QA, F32), (C_KA, F32), (C_VA, MXU_DTYPE), (C_GA, F32),
              (C_QS, MXU_DTYPE), (C_KS, MXU_DTYPE), (C_VS, MXU_DTYPE), (C_ZA, F32)]

    def body(x_ref, g_ref, w_ref, *outs):
        xv = x_ref[...]
        u = _mx(xv * _rms_r(xv) * g_ref[...])
        for ref, ((off, width), _) in zip(outs, groups):
            ref[...] = _dot(u, w_ref[:, off:off + width]).astype(ref.dtype)

    body, extra, extra_specs = _after(body, 3, dep)
    return pl.pallas_call(
        body, name="proj_fwd", grid=(L // tm,),
        in_specs=[_row_spec(tm, D_MODEL), _full_spec((1, D_MODEL)), _vmem_spec()] + extra_specs,
        out_specs=[_row_spec(tm, w) for (_, w), _ in groups],
        out_shape=[jax.ShapeDtypeStruct((L, w), dt) for (_, w), dt in groups],
        compiler_params=_params(("arbitrary",), VMEM_BIG),
    )(x, g_pre, w_in_p, *extra)


_GlaPre = collections.namedtuple("_GlaPre", "z g eb enb elb dec qd ki ks")


def _gla_chunk_pre(q, k, z, w, bias, tri, rev):
    g = _dot(_mx(z), w) + bias
    la = (jnp.minimum(g, 0.0) - jnp.log(1.0 + jnp.exp(-jnp.abs(g)))) / GLA_GATE_NORM
    b = _dot_exact(tri.astype(F32), la)
    blast = b[0:1] if rev else b[GLA_CHUNK - 1:GLA_CHUNK]
    eb = jnp.exp(b)
    enb = jnp.exp(-b)
    elb = jnp.exp(blast - b)
    dec = jnp.exp(blast)
    qd = q * 0.125 * eb
    return _GlaPre(z, g, eb, enb, elb, dec, qd, k * enb, k * elb)


def _tri_masks():
    row = lax.broadcasted_iota(jnp.int32, (GLA_CHUNK, GLA_CHUNK), 0)
    col = lax.broadcasted_iota(jnp.int32, (GLA_CHUNK, GLA_CHUNK), 1)
    return row >= col, row <= col, row


def _gla_fwd_call(qa, ka, va, za, wgf, bgf, wgb, bgb):
    L = qa.shape[0]
    br = min(512, L)
    nb, nc, n_chunks = L // br, br // GLA_CHUNK, L // GLA_CHUNK
    hw = GLA_HEADS * HEAD_PAD

    def body(qaf, kaf, vaf, zaf, qab, kab, vab, zab, wgf_r, bgf_r, wgb_r, bgb_r,
             of_r, ob_r, sf_r, sb_r, st_f, st_b):
        @pl.when(pl.program_id(0) == 0)
        def _():
            st_f[...] = jnp.zeros_like(st_f)
            st_b[...] = jnp.zeros_like(st_b)

        tri_f, tri_b, _ = _tri_masks()

        def one(rev, q_r, k_r, v_r, z_r, w_r, b_r, o_r, s_r, st, ci):
            tri = tri_b if rev else tri_f
            rows = pl.ds(pl.multiple_of(ci * GLA_CHUNK, GLA_CHUNK), GLA_CHUNK)
            pre = _gla_chunk_pre(q_r[rows, :], k_r[rows, :], z_r[rows, :], w_r[...], b_r[...], tri, rev)
            for h in range(GLA_HEADS):
                sl = slice(HEAD_PAD * h, HEAD_PAD * (h + 1))
                qd, ki, ks = _mx(pre.qd[:, sl]), _mx(pre.ki[:, sl]), _mx(pre.ks[:, sl])
                a = jnp.where(tri, _dot_nt(qd, ki), 0.0)
                v = v_r[rows, sl]
                s_t = st[h]
                s_r[ci, h] = s_t
                o_r[rows, sl] = _dot(_mx(a), v) + _dot_nt(qd, _mx(s_t))
                st[h] = s_t * pre.dec[:, sl] + _dot_tn(v, ks)

        def loop(t, carry):
            one(False, qaf, kaf, vaf, zaf, wgf_r, bgf_r, of_r, sf_r, st_f, t)
            one(True, qab, kab, vab, zab, wgb_r, bgb_r, ob_r, sb_r, st_b, nc - 1 - t)
            return carry

        lax.fori_loop(0, nc, loop, 0)

    fwd = lambda i: (i, 0)
    bwd = lambda i: (nb - 1 - i, 0)
    ins = lambda m: [pl.BlockSpec((br, hw), m), pl.BlockSpec((br, hw), m),
                     pl.BlockSpec((br, hw), m), pl.BlockSpec((br, 128), m)]
    wspecs = [_full_spec((128, hw)), _full_spec((1, hw))] * 2
    s_shape = (nc, GLA_HEADS, HEAD_PAD, HEAD_PAD)
    return pl.pallas_call(
        body, name="gla_fwd", grid=(nb,),
        in_specs=ins(fwd) + ins(bwd) + wspecs,
        out_specs=[pl.BlockSpec((br, hw), fwd), pl.BlockSpec((br, hw), bwd),
                   pl.BlockSpec(s_shape, lambda i: (i, 0, 0, 0)),
                   pl.BlockSpec(s_shape, lambda i: (nb - 1 - i, 0, 0, 0))],
        out_shape=[jax.ShapeDtypeStruct((L, hw), F32), jax.ShapeDtypeStruct((L, hw), F32),
                   jax.ShapeDtypeStruct((n_chunks,) + s_shape[1:], F32),
                   jax.ShapeDtypeStruct((n_chunks,) + s_shape[1:], F32)],
        scratch_shapes=[pltpu.VMEM(s_shape[1:], F32), pltpu.VMEM(s_shape[1:], F32)],
        compiler_params=_params(("arbitrary",), VMEM_BIG),
    )(qa, ka, va, za, qa, ka, va, za, wgf, bgf, wgb, bgb)


def _gla_bwd_call(qa, ka, va, za, do, sf, sb, wgf, bgf, wgb, bgb, dep=None):
    L = qa.shape[0]
    br = min(256, L)
    nb, nc = L // br, br // GLA_CHUNK
    hw = GLA_HEADS * HEAD_PAD

    def body(qaf, kaf, vaf, zaf, dof, sf_r, qab, kab, vab, zab, dob, sb_r, wgf_r, bgf_r, wgb_r, bgb_r,
             dqf, dkf, dvf, dzf, dwf, dbf, dqb, dkb, dvb, dzb, dwb, dbb, gt_f, gt_b):
        @pl.when(pl.program_id(0) == 0)
        def _():
            for ref in (gt_f, gt_b, dwf, dbf, dwb, dbb):
                ref[...] = jnp.zeros_like(ref)

        tri_f, tri_b, row = _tri_masks()
        row_w = lax.broadcasted_iota(jnp.int32, (GLA_CHUNK, HEAD_PAD), 0)

        def one(rev, q_r, k_r, v_r, z_r, do_r, s_r, w_r, b_r, dq_r, dk_r, dv_r, dz_r, dw_r, dbias_r, gt, ci):
            tri = tri_b if rev else tri_f
            last_row = 0 if rev else GLA_CHUNK - 1
            rows = pl.ds(pl.multiple_of(ci * GLA_CHUNK, GLA_CHUNK), GLA_CHUNK)
            w = w_r[...]
            pre = _gla_chunk_pre(q_r[rows, :], k_r[rows, :], z_r[rows, :], w, b_r[...], tri, rev)
            db_parts = []
            for h in range(GLA_HEADS):
                sl = slice(HEAD_PAD * h, HEAD_PAD * (h + 1))
                qd_f, ki_f, ks_f = pre.qd[:, sl], pre.ki[:, sl], pre.ks[:, sl]
                qd, ki, ks = _mx(qd_f), _mx(ki_f), _mx(ks_f)
                a = _mx(jnp.where(tri, _dot_nt(qd, ki), 0.0))
                v = v_r[rows, sl]
                do_h = _mx(do_r[rows, sl])
                s_t = s_r[ci, h]
                g_t = gt[h]
                g_m = _mx(g_t)
                da = _mx(jnp.where(tri, _dot_nt(do_h, v), 0.0))
                dv_r[rows, sl] = _dot_tn(a, do_h) + _dot_nt(ks, g_m)
                dqd = _dot(da, ki) + _dot(do_h, _mx(s_t))
                dki = _dot_tn(da, qd)
                dks = _dot(v, g_m)
                ddec = jnp.sum(g_t * s_t, axis=0, keepdims=True)
                gt[h] = g_t * pre.dec[:, sl] + _dot_tn(do_h, qd)
                dq_r[rows, sl] = dqd * pre.eb[:, sl] * 0.125
                dk_r[rows, sl] = dki * pre.enb[:, sl] + dks * pre.elb[:, sl]
                dblast = jnp.sum(dks * ks_f, axis=0, keepdims=True) + pre.dec[:, sl] * ddec
                db_h = dqd * qd_f - dki * ki_f - dks * ks_f
                db_parts.append(db_h + jnp.where(row_w == last_row, dblast, 0.0))
            db = jnp.concatenate(db_parts, axis=1)
            dla = _dot_tn_exact(tri.astype(F32), db)
            dg = dla * (1.0 / GLA_GATE_NORM) * (1.0 / (1.0 + jnp.exp(pre.g)))
            dg_m = _mx(dg)
            dz_r[rows, :] = _dot_nt(dg_m, w)
            dw_r[...] += _dot_tn(_mx(pre.z), dg_m)
            dbias_r[...] += jnp.sum(dg, axis=0, keepdims=True)

        def loop(t, carry):
            one(False, qaf, kaf, vaf, zaf, dof, sf_r, wgf_r, bgf_r, dqf, dkf, dvf, dzf, dwf, dbf, gt_f, nc - 1 - t)
            one(True, qab, kab, vab, zab, dob, sb_r, wgb_r, bgb_r, dqb, dkb, dvb, dzb, dwb, dbb, gt_b, t)
            return carry

        lax.fori_loop(0, nc, loop, 0)

    last_first = lambda i: (nb - 1 - i, 0)
    first_last = lambda i: (i, 0)
    s_shape = (nc, GLA_HEADS, HEAD_PAD, HEAD_PAD)

    def ins(m):
        return [pl.BlockSpec((br, hw), m), pl.BlockSpec((br, hw), m), pl.BlockSpec((br, hw), m),
                pl.BlockSpec((br, 128), m), pl.BlockSpec((br, hw), m),
                pl.BlockSpec(s_shape, lambda i: m(i) + (0, 0))]

    def outs(m):
        return [pl.BlockSpec((br, hw), m), pl.BlockSpec((br, hw), m), pl.BlockSpec((br, hw), m),
                pl.BlockSpec((br, 128), m), _full_spec((128, hw)), _full_spec((1, hw))]

    out_shape = [jax.ShapeDtypeStruct((L, hw), F32)] * 3 + [
        jax.ShapeDtypeStruct((L, 128), F32), jax.ShapeDtypeStruct((128, hw), F32),
        jax.ShapeDtypeStruct((1, hw), F32)]
    wspecs = [_full_spec((128, hw)), _full_spec((1, hw))] * 2
    body, extra, extra_specs = _after(body, 16, dep)
    return pl.pallas_call(
        body, name="gla_bwd", grid=(nb,),
        in_specs=ins(last_first) + ins(first_last) + wspecs + extra_specs,
        out_specs=outs(last_first) + outs(first_last),
        out_shape=out_shape + out_shape,
        scratch_shapes=[pltpu.VMEM(s_shape[1:], F32), pltpu.VMEM(s_shape[1:], F32)],
        compiler_params=_params(("arbitrary",), VMEM_BIG),
    )(qa, ka, va, za, do, sf, qa, ka, va, za, do, sb, wgf, bgf, wgb, bgb, *extra)


def _t5_buckets(rel):
    nb = REL_BUCKETS // 2
    ret = (rel > 0).astype(np.int32) * nb
    n = np.abs(rel)
    max_exact = nb // 2
    large = max_exact + (np.log(np.maximum(n, 1).astype(np.float32) / max_exact)
                         / math.log(REL_MAX_DIST / max_exact) * (nb - max_exact)).astype(np.int32)
    large = np.minimum(large, nb - 1)
    return ret + np.where(n < max_exact, n, large)


def _band_buckets():
    c = np.arange(SWA_BLOCK)[:, None]
    s = np.arange(3 * SWA_BLOCK)[None, :]
    return _t5_buckets(s - SWA_BLOCK - c).astype(np.int32)


def _swa_valid(n, seq_len):
    c = lax.broadcasted_iota(jnp.int32, (SWA_BLOCK, 3 * SWA_BLOCK), 0)
    s = lax.broadcasted_iota(jnp.int32, (SWA_BLOCK, 3 * SWA_BLOCK), 1)
    rel = s - SWA_BLOCK - c
    key_pos = (n - 1) * SWA_BLOCK + s
    return (jnp.abs(rel) <= SWA_BLOCK) & (key_pos >= 0) & (key_pos < seq_len)


def _swa_probs(q, kk, bias_h, sink_h, valid):
    lg = _dot_nt(q, kk) * 0.125 + bias_h
    lg = jnp.where(valid, lg, -1e30)
    m = jnp.maximum(jnp.max(lg, axis=-1, keepdims=True), sink_h)
    p = jnp.exp(lg - m)
    e_sink = jnp.exp(sink_h - m)
    inv = 1.0 / (jnp.sum(p, axis=-1, keepdims=True) + e_sink)
    return p * inv, e_sink * inv


def _swa_fwd_call(qs, ks, vs, bias, sink, dep=None):
    L = qs.shape[0]
    group = SWA_Q_HEADS // SWA_KV_HEADS

    def body(q_r, k_r, v_r, bias_r, sink_r, o_r):
        n = pl.program_id(0)
        base = pl.multiple_of(n * SWA_BLOCK, SWA_BLOCK)
        valid = _swa_valid(n, L)
        for kv in range(SWA_KV_HEADS):
            ksl = slice(HEAD_PAD * kv, HEAD_PAD * (kv + 1))
            kk = k_r[pl.ds(base, 3 * SWA_BLOCK), ksl]
            vv = v_r[pl.ds(base, 3 * SWA_BLOCK), ksl]
            for g in range(group):
                h = kv * group + g
                sl = slice(HEAD_PAD * h, HEAD_PAD * (h + 1))
                pn, _ = _swa_probs(q_r[:, sl], kk, bias_r[h], sink_r[h], valid)
                o_r[:, sl] = _dot(_mx(pn), vv).astype(o_r.dtype)

    qw = SWA_Q_HEADS * HEAD_PAD
    body, extra, extra_specs = _after(body, 5, dep)
    return pl.pallas_call(
        body, name="swa_fwd", grid=(L // SWA_BLOCK,),
        in_specs=[_row_spec(SWA_BLOCK, qw), _vmem_spec(), _vmem_spec(), _vmem_spec(),
                  pl.BlockSpec(memory_space=pltpu.SMEM)] + extra_specs,
        out_specs=_row_spec(SWA_BLOCK, qw),
        out_shape=jax.ShapeDtypeStruct((L, qw), MXU_DTYPE),
        compiler_params=_params(("arbitrary",), VMEM_BIG),
    )(qs, ks, vs, bias, sink, *extra)


def _swa_bwd_call(qs, ks, vs, bias, sink, do, dep=None):
    L = qs.shape[0]
    group = SWA_Q_HEADS // SWA_KV_HEADS
    qw = SWA_Q_HEADS * HEAD_PAD
    kw = SWA_KV_HEADS * HEAD_PAD

    def body(q_r, k_r, v_r, bias_r, sink_r, do_r, dq_r, dk_r, dv_r, dbias_r, dsink_r):
        n = pl.program_id(0)

        @pl.when(n == 0)
        def _():
            for ref in (dk_r, dv_r, dbias_r, dsink_r):
                ref[...] = jnp.zeros_like(ref)

        base = pl.multiple_of(n * SWA_BLOCK, SWA_BLOCK)
        span = pl.ds(base, 3 * SWA_BLOCK)
        valid = _swa_valid(n, L)
        lane = lax.broadcasted_iota(jnp.int32, (1, 128), 1)
        dsink = jnp.zeros((1, 128), F32)
        for kv in range(SWA_KV_HEADS):
            ksl = slice(HEAD_PAD * kv, HEAD_PAD * (kv + 1))
            kk = k_r[span, ksl]
            vv = v_r[span, ksl]
            dk_acc = jnp.zeros((3 * SWA_BLOCK, HEAD_PAD), F32)
            dv_acc = jnp.zeros((3 * SWA_BLOCK, HEAD_PAD), F32)
            for g in range(group):
                h = kv * group + g
                sl = slice(HEAD_PAD * h, HEAD_PAD * (h + 1))
                q = q_r[:, sl]
                pn, p_sink = _swa_probs(q, kk, bias_r[h], sink_r[h], valid)
                do_h = do_r[:, sl]
                dp = _dot_nt(do_h, vv)
                delta = jnp.sum(pn * dp, axis=-1, keepdims=True)
                ds = pn * (dp - delta)
                dsink = dsink + jnp.where(lane == h, -jnp.sum(p_sink * delta), 0.0)
                dbias_r[h] += ds
                ds_m = _mx(ds)
                dq_r[:, sl] = _dot(ds_m, kk) * 0.125
                dk_acc = dk_acc + _dot_tn(ds_m, q) * 0.125
                dv_acc = dv_acc + _dot_tn(_mx(pn), do_h)
            dk_r[span, ksl] += dk_acc
            dv_r[span, ksl] += dv_acc
        dsink_r[...] += dsink

    body, extra, extra_specs = _after(body, 6, dep)
    return pl.pallas_call(
        body, name="swa_bwd", grid=(L // SWA_BLOCK,),
        in_specs=[_row_spec(SWA_BLOCK, qw), _vmem_spec(), _vmem_spec(), _vmem_spec(),
                  pl.BlockSpec(memory_space=pltpu.SMEM), _row_spec(SWA_BLOCK, qw)] + extra_specs,
        out_specs=[_row_spec(SWA_BLOCK, qw), _vmem_spec(), _vmem_spec(), _vmem_spec(), _full_spec((1, 128))],
        out_shape=[jax.ShapeDtypeStruct((L, qw), F32),
                   jax.ShapeDtypeStruct((L + 2 * SWA_BLOCK, kw), F32),
                   jax.ShapeDtypeStruct((L + 2 * SWA_BLOCK, kw), F32),
                   jax.ShapeDtypeStruct((SWA_Q_HEADS, SWA_BLOCK, 3 * SWA_BLOCK), F32),
                   jax.ShapeDtypeStruct((1, 128), F32)],
        compiler_params=_params(("arbitrary",), VMEM_BIG),
    )(qs, ks, vs, bias, sink, do, *extra)


def _bias_call(rel_bias, buckets):
    def body(t_r, bk_r, o_r):
        bk = bk_r[...]
        for h in range(SWA_Q_HEADS):
            acc = jnp.zeros(bk.shape, F32)
            for b in range(REL_BUCKETS):
                acc = jnp.where(bk == b, t_r[b, h], acc)
            o_r[h] = acc

    return pl.pallas_call(
        body, name="band_bias",
        in_specs=[pl.BlockSpec(memory_space=pltpu.SMEM), _vmem_spec()], out_specs=_vmem_spec(),
        out_shape=jax.ShapeDtypeStruct((SWA_Q_HEADS,) + buckets.shape, F32),
    )(rel_bias, buckets)


def _relbias_call(dbias, buckets):
    def body(db_r, bk_r, o_r):
        bk = bk_r[...]
        rowi = lax.broadcasted_iota(jnp.int32, (REL_BUCKETS, 128), 0)
        lanei = lax.broadcasted_iota(jnp.int32, (REL_BUCKETS, 128), 1)
        acc = jnp.zeros((REL_BUCKETS, 128), F32)
        for b in range(REL_BUCKETS):
            m = bk == b
            for h in range(SWA_Q_HEADS):
                s = jnp.sum(jnp.where(m, db_r[h], 0.0))
                acc = acc + jnp.where((rowi == b) & (lanei == h), s, 0.0)
        o_r[...] = acc

    return pl.pallas_call(
        body, name="relbias_grad",
        in_specs=[_vmem_spec(), _vmem_spec()], out_specs=_vmem_spec(),
        out_shape=jax.ShapeDtypeStruct((REL_BUCKETS, 128), F32),
    )(dbias, buckets)


def _mix_call(o_f, o_b, ga, o_s, x, gn, w_out_p, g_post, g_pre2):
    L = x.shape[0]
    tm = min(256, L)
    hw = GLA_HEADS * HEAD_PAD

    def body(of_r, ob_r, ga_r, os_r, x_r, gn_r, w_r, gp_r, g2_r, cat_r, mix_r, h1_r, n2_r):
        gn_v = gn_r[...]
        for h in range(GLA_HEADS):
            sl = slice(HEAD_PAD * h, HEAD_PAD * (h + 1))
            oh = of_r[:, sl] + ob_r[:, sl]
            on = oh * _rms_r(oh) * gn_v
            gate = ga_r[:, sl]
            cat_r[:, sl] = (on * (gate * jax.nn.sigmoid(gate))).astype(cat_r.dtype)
        os_v = os_r[...]
        cat_r[:, hw:] = os_v
        mix = _dot(cat_r[:, :hw], w_r[:hw, :]) + _dot(os_v, w_r[hw:, :])
        mix_r[...] = mix
        h1 = x_r[...] + mix * _rms_r(mix) * gp_r[...]
        h1_r[...] = h1
        n2_r[...] = (h1 * _rms_r(h1) * g2_r[...]).astype(n2_r.dtype)

    return pl.pallas_call(
        body, name="mix_fwd", grid=(L // tm,),
        in_specs=[_row_spec(tm, hw), _row_spec(tm, hw), _row_spec(tm, hw), _row_spec(tm, OUT_PAD - hw),
                  _row_spec(tm, D_MODEL), _full_spec((1, HEAD_PAD)), _vmem_spec(),
                  _full_spec((1, D_MODEL)), _full_spec((1, D_MODEL))],
        out_specs=[_row_spec(tm, OUT_PAD), _row_spec(tm, D_MODEL), _row_spec(tm, D_MODEL), _row_spec(tm, D_MODEL)],
        out_shape=[jax.ShapeDtypeStruct((L, OUT_PAD), MXU_DTYPE), jax.ShapeDtypeStruct((L, D_MODEL), F32),
                   jax.ShapeDtypeStruct((L, D_MODEL), F32), jax.ShapeDtypeStruct((L, D_MODEL), MXU_DTYPE)],
        compiler_params=_params(("arbitrary",), VMEM_BIG),
    )(o_f, o_b, ga, o_s, x, gn, w_out_p, g_post, g_pre2)


def _mlp_fwd_call(n2, h1, tgt, w_ud, g_post):
    L = n2.shape[0]
    tm = min(256, L)
    blk = D_FF // N_CHIPS

    def body(n2_r, h1_r, t_r, w_r, g_r, a_r, rz_r, dh2_r, dff_r, loss_r, dg_r):
        @pl.when(pl.program_id(0) == 0)
        def _():
            loss_r[...] = jnp.zeros_like(loss_r)
            dg_r[...] = jnp.zeros_like(dg_r)

        n2v = n2_r[...]
        ff = jnp.zeros((tm, D_MODEL), F32)
        for j in range(N_CHIPS):
            sl = slice(blk * j, blk * (j + 1))
            rz = jnp.maximum(_dot(n2v, w_r[j, 0]), 0.0)
            a = _mx(rz * rz)
            rz_r[:, sl] = rz.astype(rz_r.dtype)
            a_r[:, sl] = a
            ff = ff + _dot(a, w_r[j, 1])
        g = g_r[...]
        r = _rms_r(ff)
        err = h1_r[...] + ff * r * g - t_r[...]
        loss_r[...] += 0.5 * jnp.sum(err * err) / D_MODEL
        dh2 = err * (1.0 / D_MODEL)
        dh2_r[...] = dh2
        dff, dg = _rms_bwd(ff, r, g, dh2)
        dff_r[...] = dff.astype(dff_r.dtype)
        dg_r[...] += dg

    return pl.pallas_call(
        body, name="mlp_fwd", grid=(L // tm,),
        in_specs=[_row_spec(tm, D_MODEL), _row_spec(tm, D_MODEL), _row_spec(tm, D_MODEL),
                  _vmem_spec(), _full_spec((1, D_MODEL))],
        out_specs=[_row_spec(tm, D_FF), _row_spec(tm, D_FF), _row_spec(tm, D_MODEL), _row_spec(tm, D_MODEL),
                   _full_spec((1, 128)), _full_spec((1, D_MODEL))],
        out_shape=[jax.ShapeDtypeStruct((L, D_FF), MXU_DTYPE), jax.ShapeDtypeStruct((L, D_FF), MXU_DTYPE),
                   jax.ShapeDtypeStruct((L, D_MODEL), F32), jax.ShapeDtypeStruct((L, D_MODEL), MXU_DTYPE),
                   jax.ShapeDtypeStruct((1, 128), F32), jax.ShapeDtypeStruct((1, D_MODEL), F32)],
        compiler_params=_params(("arbitrary",), VMEM_BIG),
    )(n2, h1, tgt, w_ud, g_post)


def _mlp_bwd_call(dff, rz, w_ud):
    L = dff.shape[0]
    tm = min(256, L)
    blk = D_FF // N_CHIPS

    def body(dff_r, rz_r, w_r, dz_r, dn2_r):
        dffv = dff_r[...]
        dn2 = jnp.zeros((tm, D_MODEL), F32)
        for j in range(N_CHIPS):
            sl = slice(blk * j, blk * (j + 1))
            dz = _mx(_dot_nt(dffv, w_r[j, 1]) * 2.0 * rz_r[:, sl].astype(F32))
            dz_r[:, sl] = dz
            dn2 = dn2 + _dot_nt(dz, w_r[j, 0])
        dn2_r[...] = dn2

    return pl.pallas_call(
        body, name="mlp_bwd", grid=(L // tm,),
        in_specs=[_row_spec(tm, D_MODEL), _row_spec(tm, D_FF), _vmem_spec()],
        out_specs=[_row_spec(tm, D_FF), _row_spec(tm, D_MODEL)],
        out_shape=[jax.ShapeDtypeStruct((L, D_FF), MXU_DTYPE), jax.ShapeDtypeStruct((L, D_MODEL), F32)],
        compiler_params=_params(("arbitrary",), VMEM_BIG),
    )(dff, rz, w_ud)


def _mlp_wgrad_call(a, dff, n2, dz):
    L = a.shape[0]
    tf = 512
    per = (D_FF // N_CHIPS) // tf

    def body(a_r, dff_r, n2_r, dz_r, dwd_r, dwu_r):
        dwd_r[...] = _dot_tn(a_r[...], dff_r[...])
        dwu_r[...] = _dot_tn(n2_r[...], dz_r[...])

    return pl.pallas_call(
        body, name="mlp_wgrad", grid=(D_FF // tf,),
        in_specs=[pl.BlockSpec((L, tf), lambda j: (0, j)), _vmem_spec(), _vmem_spec(),
                  pl.BlockSpec((L, tf), lambda j: (0, j))],
        out_specs=[pl.BlockSpec((tf, D_MODEL), lambda j: (j, 0)),
                   pl.BlockSpec((None, D_MODEL, tf), lambda j: (j // per, 0, j % per))],
        out_shape=[jax.ShapeDtypeStruct((D_FF, D_MODEL), F32),
                   jax.ShapeDtypeStruct((N_CHIPS, D_MODEL, D_FF // N_CHIPS), F32)],
        compiler_params=_params(("arbitrary",), VMEM_BIG),
    )(a, dff, n2, dz)


def _mix_bwd_call(dn2, dh2, h1, mix, cat, o_f, o_b, ga, gn, g_post, g_pre2, w_out_p):
    L = dn2.shape[0]
    tm = min(256, L)
    hw = GLA_HEADS * HEAD_PAD

    def body(dn2_r, dh2_r, h1_r, mix_r, cat_r, of_r, ob_r, ga_r, gn_r, gp_r, g2_r, w_r,
             dh1_r, do_r, dga_r, dos_r, dw_r, dg2_r, dgp_r, dgn_r):
        @pl.when(pl.program_id(0) == 0)
        def _():
            for ref in (dw_r, dg2_r, dgp_r, dgn_r):
                ref[...] = jnp.zeros_like(ref)

        h1 = h1_r[...]
        dx2, dg2 = _rms_bwd(h1, _rms_r(h1), g2_r[...], dn2_r[...])
        dh1 = dh2_r[...] + dx2
        dh1_r[...] = dh1
        dg2_r[...] += dg2
        mix = mix_r[...]
        dmix, dgp = _rms_bwd(mix, _rms_r(mix), gp_r[...], dh1)
        dgp_r[...] += dgp
        dmix_m = _mx(dmix)
        dw_r[...] += _dot_tn(cat_r[...], dmix_m)
        dcat = _dot_nt(dmix_m, w_r[...])
        dos_r[...] = dcat[:, hw:].astype(dos_r.dtype)
        gn_v = gn_r[...]
        dgn = jnp.zeros((1, HEAD_PAD), F32)
        for h in range(GLA_HEADS):
            sl = slice(HEAD_PAD * h, HEAD_PAD * (h + 1))
            oh = of_r[:, sl] + ob_r[:, sl]
            rr = _rms_r(oh)
            gate = ga_r[:, sl]
            sg = jax.nn.sigmoid(gate)
            doa = dcat[:, sl]
            dga_r[:, sl] = doa * (oh * rr * gn_v) * (sg * (1.0 + gate * (1.0 - sg)))
            do_h, dgn_h = _rms_bwd(oh, rr, gn_v, doa * (gate * sg))
            do_r[:, sl] = do_h
            dgn = dgn + dgn_h
        dgn_r[...] += dgn

    return pl.pallas_call(
        body, name="mix_bwd", grid=(L // tm,),
        in_specs=[_row_spec(tm, D_MODEL)] * 4 + [_row_spec(tm, OUT_PAD)] + [_row_spec(tm, hw)] * 3
        + [_full_spec((1, HEAD_PAD)), _full_spec((1, D_MODEL)), _full_spec((1, D_MODEL)), _vmem_spec()],
        out_specs=[_row_spec(tm, D_MODEL), _row_spec(tm, hw), _row_spec(tm, hw), _row_spec(tm, OUT_PAD - hw),
                   _full_spec((OUT_PAD, D_MODEL)), _full_spec((1, D_MODEL)), _full_spec((1, D_MODEL)),
                   _full_spec((1, HEAD_PAD))],
        out_shape=[jax.ShapeDtypeStruct((L, D_MODEL), F32), jax.ShapeDtypeStruct((L, hw), F32),
                   jax.ShapeDtypeStruct((L, hw), F32), jax.ShapeDtypeStruct((L, OUT_PAD - hw), MXU_DTYPE),
                   jax.ShapeDtypeStruct((OUT_PAD, D_MODEL), F32), jax.ShapeDtypeStruct((1, D_MODEL), F32),
                   jax.ShapeDtypeStruct((1, D_MODEL), F32), jax.ShapeDtypeStruct((1, HEAD_PAD), F32)],
        compiler_params=_params(("arbitrary",), VMEM_BIG),
    )(dn2, dh2, h1, mix, cat, o_f, o_b, ga, gn, g_post, g_pre2, w_out_p)


def _in_bwd_call(x, dh1, g_pre, w_in_p, pairs, singles, dep=None):
    L = x.shape[0]
    tm = min(256, L)
    n_pair, n_single = len(pairs), len(singles)
    groups = [c for c, _ in pairs] + [c for c, _ in singles]

    def body(*refs):
        x_r, dh1_r, g_r, w_r = refs[:4]
        pair_refs = refs[4:4 + 2 * n_pair]
        single_refs = refs[4 + 2 * n_pair:4 + 2 * n_pair + n_single]
        dx_r, dw_r, dg_r = refs[4 + 2 * n_pair + n_single:]

        @pl.when(pl.program_id(0) == 0)
        def _():
            dw_r[...] = jnp.zeros_like(dw_r)
            dg_r[...] = jnp.zeros_like(dg_r)

        xv = x_r[...]
        r = _rms_r(xv)
        g = g_r[...]
        u = _mx(xv * r * g)
        vals = [pair_refs[2 * i][...] + pair_refs[2 * i + 1][...] for i in range(n_pair)]
        vals += [ref[...].astype(F32) for ref in single_refs]
        du = jnp.zeros((tm, D_MODEL), F32)
        for (off, width), val in zip(groups, vals):
            d = _mx(val)
            du = du + _dot_nt(d, w_r[:, off:off + width])
            dw_r[:, off:off + width] += _dot_tn(u, d)
        dx, dg = _rms_bwd(xv, r, g, du)
        dx_r[...] = dh1_r[...] + dx
        dg_r[...] += dg

    arrays = [a for _, pr in pairs for a in pr] + [a for _, a in singles]
    specs = [_row_spec(tm, a.shape[1]) for a in arrays]
    body, extra, extra_specs = _after(body, 4 + len(arrays), dep)
    return pl.pallas_call(
        body, name="in_bwd", grid=(L // tm,),
        in_specs=[_row_spec(tm, D_MODEL), _row_spec(tm, D_MODEL), _full_spec((1, D_MODEL)), _vmem_spec()] + specs
        + extra_specs,
        out_specs=[_row_spec(tm, D_MODEL), _full_spec((D_MODEL, IN_PAD)), _full_spec((1, D_MODEL))],
        out_shape=[jax.ShapeDtypeStruct((L, D_MODEL), F32), jax.ShapeDtypeStruct((D_MODEL, IN_PAD), F32),
                   jax.ShapeDtypeStruct((1, D_MODEL), F32)],
        compiler_params=_params(("arbitrary",), VMEM_BIG),
    )(x, dh1, g_pre, w_in_p, *arrays, *extra)


def _adamw_math(w, g, m, v):
    m = ADAM_B1 * m + (1.0 - ADAM_B1) * g
    v = ADAM_B2 * v + (1.0 - ADAM_B2) * (g * g)
    m_hat = m / (1.0 - ADAM_B1 ** ADAM_STEP)
    v_hat = v / (1.0 - ADAM_B2 ** ADAM_STEP)
    delta = -ADAM_LR * (m_hat / (jnp.sqrt(v_hat) + ADAM_EPS) + ADAM_WD * w)
    return delta, m, v


def _adamw_call(w, g, m, v, name, dep=None):
    rows, cols = w.shape
    tr = min(256, rows)

    def body(w_r, g_r, m_r, v_r, d_r, nm_r, nv_r):
        d_r[...], nm_r[...], nv_r[...] = _adamw_math(w_r[...], g_r[...], m_r[...], v_r[...])

    spec = _row_spec(tr, cols)
    body, extra, extra_specs = _after(body, 4, dep)
    return pl.pallas_call(
        body, name=name, grid=(rows // tr,),
        in_specs=[spec] * 4 + extra_specs, out_specs=[spec] * 3,
        out_shape=[jax.ShapeDtypeStruct(w.shape, F32)] * 3,
        compiler_params=_params(("arbitrary",)),
    )(w, g, m, v, *extra)


def _position():
    return lax.axis_index("x"), lax.axis_index("y"), lax.axis_index("c")


def _other_chips(x, y):
    return [(1 - x, y), (x, 1 - y), (1 - x, 1 - y)]


def _rows(ref, start, size):
    span = pl.ds(pl.multiple_of(start, 16), size)
    return ref.at[span, :] if len(ref.shape) == 2 else ref.at[:, span, :]


def _first_gather_call(shards):
    n = len(shards)

    def body(*refs):
        srcs, outs = refs[:n], refs[n:2 * n]
        send_sems, recv_sems, local_sems = refs[2 * n:]
        x, y, c = _position()
        sibling = (x, y, 1 - c)
        chips = _other_chips(x, y)
        local = [pltpu.make_async_copy(srcs[a], outs[a].at[2 * x + y], local_sems.at[a]) for a in range(n)]
        for cp in local:
            cp.start()

        def copy(a, k, block, to, src=None):
            px, py, pc = block
            half = shards[a].shape[0] // 2
            dst = _rows(outs[a].at[2 * px + py], pc * half, half)
            return pltpu.make_async_remote_copy(
                src_ref=dst if src is None else src, dst_ref=dst, send_sem=send_sems.at[6 * a + k],
                recv_sem=recv_sems.at[6 * a + k], device_id=to, device_id_type=MESH_ID)

        first, passed = [], []
        for a in range(n):
            half = shards[a].shape[0] // 2
            my_half = _rows(srcs[a], c * half, half)
            first += [copy(a, j, (x, y, c), (*chip, c), src=my_half) for j, chip in enumerate(chips)]
        for cp in first:
            cp.start()
        for a in range(n):
            for j, chip in enumerate(chips):
                copy(a, j, (*chip, c), (x, y, c)).wait_recv()
                passed.append(copy(a, 3 + j, (*chip, c), sibling))
                passed[-1].start()
        for a in range(n):
            for j, chip in enumerate(chips):
                copy(a, 3 + j, (*chip, 1 - c), (x, y, c)).wait_recv()
        for cp in first + passed:
            cp.wait_send()
        for cp in local:
            cp.wait()

    return pl.pallas_call(
        body, name="first_gather",
        in_specs=[_any_spec()] * n, out_specs=[_any_spec()] * n,
        out_shape=[jax.ShapeDtypeStruct((N_CHIPS,) + s.shape, s.dtype) for s in shards],
        scratch_shapes=[pltpu.SemaphoreType.DMA((6 * n,)), pltpu.SemaphoreType.DMA((6 * n,)),
                        pltpu.SemaphoreType.DMA((n,))],
    )(*shards)


def _split_start(name, arrays, n_copies, plan):
    n = len(arrays)

    def body(*refs):
        ins, send_sems, recv_sems, token = refs[:n], refs[n], refs[n + 1], refs[-1]
        for k, (src, dst, to, _) in enumerate(plan(ins)):
            pltpu.make_async_remote_copy(src_ref=src, dst_ref=dst, send_sem=send_sems.at[k],
                                         recv_sem=recv_sems.at[k], device_id=to, device_id_type=MESH_ID).start()
        token[...] = jnp.zeros_like(token)

    hbm = pl.BlockSpec(memory_space=pltpu.HBM)
    sem = pl.BlockSpec(memory_space=pltpu.SEMAPHORE)
    out = pl.pallas_call(
        body, name=name,
        out_shape=(pltpu.SemaphoreType.DMA((n_copies,)), pltpu.SemaphoreType.DMA((n_copies,)))
        + tuple(pltpu.HBM(a.shape, a.dtype) for a in arrays) + (jax.ShapeDtypeStruct((8, 128), F32),),
        in_specs=[hbm] * n, out_specs=(sem, sem) + (hbm,) * n + (_vmem_spec(),),
        input_output_aliases={i: 2 + i for i in range(n)},
        compiler_params=pltpu.CompilerParams(has_side_effects=pltpu.SideEffectType.DATAFLOW_SIDE_EFFECTING),
    )(*[pltpu.with_memory_space_constraint(a, pltpu.HBM) for a in arrays])
    return (out[0], out[1], tuple(out[2:2 + n])), out[-1]


def _split_wait(name, handle, n_copies, plan, after):
    send_sems, recv_sems, arrays = handle
    n = len(arrays)

    def body(*refs):
        ins, s_sems, r_sems = refs[:n], refs[n], refs[n + 1]
        for k, (src, dst, to, landed) in enumerate(plan(ins)):
            cp = pltpu.make_async_remote_copy(src_ref=src, dst_ref=landed, send_sem=s_sems.at[k],
                                              recv_sem=r_sems.at[k], device_id=to, device_id_type=MESH_ID)
            cp.wait_send()
            cp.wait_recv()

    hbm = pl.BlockSpec(memory_space=pltpu.HBM)
    sem = pl.BlockSpec(memory_space=pltpu.SEMAPHORE)
    out = pl.pallas_call(
        body, name=name,
        out_shape=tuple(pltpu.HBM(a.shape, a.dtype) for a in arrays),
        in_specs=[hbm] * n + [sem, sem, _any_spec()], out_specs=(hbm,) * n,
        input_output_aliases={i: i for i in range(n)},
        compiler_params=pltpu.CompilerParams(has_side_effects=pltpu.SideEffectType.DATAFLOW_SIDE_EFFECTING),
    )(*arrays, send_sems, recv_sems, after)
    return tuple(out)


def _gather_plans(shard_rows):
    n = len(shard_rows)

    def stage_one(refs):
        x, y, c = _position()
        copies = []
        for a, rows in enumerate(shard_rows):
            half = rows // 2
            for px, py in _other_chips(x, y):
                copies.append((_rows(refs[a], c * half, half), _rows(refs[n + a].at[2 * x + y], c * half, half),
                               (px, py, c), _rows(refs[n + a].at[2 * px + py], c * half, half)))
        return copies

    def stage_two(refs):
        x, y, c = _position()
        copies = []
        for a, rows in enumerate(shard_rows):
            half = rows // 2
            for px, py in _other_chips(x, y):
                piece = _rows(refs[n + a].at[2 * px + py], c * half, half)
                copies.append((piece, piece, (x, y, 1 - c), _rows(refs[n + a].at[2 * px + py], (1 - c) * half, half)))
        return copies

    return stage_one, stage_two


def _pair_swap_plan(n):
    def plan(refs):
        x, y, c = _position()
        copies = []
        for a in range(n):
            half = refs[a].shape[1] // 2
            copies.append((_rows(refs[a], (1 - c) * half, half), refs[n + a], (x, y, 1 - c), refs[n + a]))
        return copies

    return plan


def _chip_swap_plan(n):
    def plan(refs):
        x, y, c = _position()
        copies = []
        for a in range(n):
            for j, (px, py) in enumerate(_other_chips(x, y)):
                copies.append((refs[a].at[2 * px + py], refs[n + a].at[j], (px, py, c), refs[n + a].at[j]))
        return copies

    return plan


def _pair_join_plan(n):
    def plan(refs):
        x, y, c = _position()
        copies = []
        for a in range(n):
            half = refs[a].shape[0] // 2
            mine = _rows(refs[a], c * half, half)
            copies.append((mine, mine, (x, y, 1 - c), _rows(refs[a], (1 - c) * half, half)))
        return copies

    return plan


def _pair_add_call(g, got, pos, name):
    half, cols = got.shape[1], got.shape[2]
    tr = min(256, half)
    nblk = half // tr

    def body(pos_r, g_r, got_r, o_r):
        o_r[...] = (g_r[...] + got_r[...]).astype(o_r.dtype)

    return pl.pallas_call(
        body, name=name,
        grid_spec=pltpu.PrefetchScalarGridSpec(
            num_scalar_prefetch=1, grid=(N_CHIPS, nblk),
            in_specs=[pl.BlockSpec((None, tr, cols), lambda j, i, p: (j, p[1] * nblk + i, 0)),
                      pl.BlockSpec((None, tr, cols), lambda j, i, p: (j, i, 0))],
            out_specs=pl.BlockSpec((None, tr, cols), lambda j, i, p: (j, i, 0))),
        out_shape=jax.ShapeDtypeStruct((N_CHIPS, half, cols), COMM_DTYPE),
        compiler_params=_params(("arbitrary", "arbitrary")),
    )(pos, g, got)


def _chip_add_call(hsum, got, pos, name):
    half, cols = hsum.shape[1], hsum.shape[2]
    tr = min(256, half)
    nblk = half // tr

    def body(pos_r, own_r, got_r, o_r):
        acc = own_r[...].astype(F32)
        for j in range(3):
            acc = acc + got_r[j].astype(F32)
        o_r[...] = acc

    return pl.pallas_call(
        body, name=name,
        grid_spec=pltpu.PrefetchScalarGridSpec(
            num_scalar_prefetch=1, grid=(nblk,),
            in_specs=[pl.BlockSpec((None, tr, cols), lambda i, p: (p[0], i, 0)),
                      pl.BlockSpec((3, tr, cols), lambda i, p: (0, i, 0))],
            out_specs=pl.BlockSpec((tr, cols), lambda i, p: (p[1] * nblk + i, 0))),
        out_shape=jax.ShapeDtypeStruct((2 * half, cols), F32),
        compiler_params=_params(("arbitrary",)),
    )(pos, hsum, got)


SMALL_NAMES = ("norm_mix_pre", "norm_mix_post", "norm_mlp_pre", "norm_mlp_post", "b_gate_fwd", "b_gate_bwd",
               "gla_norm", "swa_sink", "rel_bias")


def _small_update_call(grads, gate_grads, params, dep=None):
    n_dev = 8
    n_small = len(SMALL_NAMES)
    wmv = [t for p in params for t in p]
    shapes = [p[0].shape for p in params]

    def body(*refs):
        g_refs = refs[:n_small + 2]
        wmv_refs = refs[n_small + 2:n_small + 2 + 3 * n_small]
        n_in = n_small + 2 + 3 * n_small
        out_refs = refs[n_in:n_in + 4 * n_small + 2]
        pack_a, pack_b, all_a, all_b, send_sems, recv_sems = refs[n_in + 4 * n_small + 2:]
        x, y, c = _position()
        me = 4 * x + 2 * y + c
        pack_a[...] = jnp.zeros_like(pack_a)
        pack_b[...] = jnp.zeros_like(pack_b)
        for i in range(4):
            pack_a[i:i + 1, :] = g_refs[i][...]
        pack_a[4:5, 0:256] = g_refs[4][...]
        pack_a[5:6, 0:256] = g_refs[5][...]
        pack_a[6:7, 0:128] = g_refs[6][...]
        pack_a[7:8, 0:128] = g_refs[7][...]
        pack_b[0:32, 0:128] = g_refs[8][...]
        pack_b[32:48, :] = g_refs[9][...]
        pack_b[48:64, :] = g_refs[10][...]
        all_a[me] = pack_a[...]
        all_b[me] = pack_b[...]
        copies = []
        for k in range(1, n_dev):
            fx, fy, fc = (k >> 2) & 1, (k >> 1) & 1, k & 1
            to = (1 - x if fx else x, 1 - y if fy else y, 1 - c if fc else c)
            for t, (pack, dst) in enumerate(((pack_a, all_a), (pack_b, all_b))):
                copies.append(pltpu.make_async_remote_copy(
                    src_ref=pack, dst_ref=dst.at[me], send_sem=send_sems.at[2 * (k - 1) + t],
                    recv_sem=recv_sems.at[2 * (k - 1) + t], device_id=to, device_id_type=MESH_ID))
        for cp in copies:
            cp.start()
        for cp in copies:
            cp.wait()
        sum_a, sum_b = all_a[0], all_b[0]
        for d in range(1, n_dev):
            sum_a = sum_a + all_a[d]
            sum_b = sum_b + all_b[d]
        gsum = [sum_a[0:1], sum_a[1:2], sum_a[2:3], sum_a[3:4], sum_a[4:5, 0:256], sum_a[5:6, 0:256],
                sum_a[6:7, 0:128], sum_a[7:8, 0:SWA_Q_HEADS], sum_b[0:32, 0:SWA_Q_HEADS]]
        for i in range(n_small):
            w_r, m_r, v_r = wmv_refs[3 * i:3 * i + 3]
            delta, new_m, new_v = _adamw_math(w_r[...], gsum[i], m_r[...], v_r[...])
            out_refs[4 * i][...] = gsum[i]
            out_refs[4 * i + 1][...] = delta
            out_refs[4 * i + 2][...] = new_m
            out_refs[4 * i + 3][...] = new_v
        out_refs[4 * n_small][...] = sum_b[32:48]
        out_refs[4 * n_small + 1][...] = sum_b[48:64]

    n_in = n_small + 2 + 3 * n_small
    body, extra, extra_specs = _after(body, n_in, dep)
    out_shape = [jax.ShapeDtypeStruct(s, F32) for s in shapes for _ in range(4)]
    out_shape += [jax.ShapeDtypeStruct((GLA_GATE_RANK, 256), F32)] * 2
    out = pl.pallas_call(
        body, name="small_update",
        in_specs=[_vmem_spec()] * n_in + extra_specs, out_specs=[_vmem_spec()] * len(out_shape),
        out_shape=out_shape,
        scratch_shapes=[pltpu.VMEM((8, D_MODEL), F32), pltpu.VMEM((64, 256), F32),
                        pltpu.VMEM((n_dev, 8, D_MODEL), F32), pltpu.VMEM((n_dev, 64, 256), F32),
                        pltpu.SemaphoreType.DMA((2 * (n_dev - 1),)), pltpu.SemaphoreType.DMA((2 * (n_dev - 1),))],
    )(*grads, *gate_grads, *wmv, *extra)
    per_name = [tuple(out[4 * i:4 * i + 4]) for i in range(n_small)]
    return per_name, out[4 * n_small], out[4 * n_small + 1]


def _pad_heads(t, n_heads, axis=-1):
    axis = axis % t.ndim
    shape = t.shape
    t = t.reshape(shape[:axis] + (n_heads, 64) + shape[axis + 1:])
    pad = [(0, 0)] * t.ndim
    pad[axis + 1] = (0, HEAD_PAD - 64)
    return jnp.pad(t, pad).reshape(shape[:axis] + (n_heads * HEAD_PAD,) + shape[axis + 1:])


def _unpad_heads(t, n_heads, axis=-1):
    axis = axis % t.ndim
    shape = t.shape
    t = t.reshape(shape[:axis] + (n_heads, HEAD_PAD) + shape[axis + 1:])
    t = lax.slice_in_dim(t, 0, 64, axis=axis + 1)
    return t.reshape(shape[:axis] + (n_heads * 64,) + shape[axis + 1:])


def _pad_w_in(w):
    return jnp.concatenate([
        _pad_heads(w[:, 0:256], 4), _pad_heads(w[:, 256:512], 4), w[:, 512:1024], w[:, 1024:1536],
        _pad_heads(w[:, 1568:2080], 8), _pad_heads(w[:, 2080:2208], 2), _pad_heads(w[:, 2208:2336], 2),
        jnp.pad(w[:, 1536:1568], ((0, 0), (0, 96)))], axis=1)


def _unpad_w_in(g):
    return jnp.concatenate([
        _unpad_heads(g[:, 0:512], 4), _unpad_heads(g[:, 512:1024], 4), g[:, 1024:1536], g[:, 1536:2048],
        g[:, 3584:3616], _unpad_heads(g[:, 2048:3072], 8), _unpad_heads(g[:, 3072:3328], 2),
        _unpad_heads(g[:, 3328:3584], 2)], axis=1)


def _pad_w_out(w):
    return jnp.concatenate([w[:512], _pad_heads(w[512:], 8, axis=0)], axis=0)


def _unpad_w_out(g):
    return jnp.concatenate([g[:512], _unpad_heads(g[512:], 8, axis=0)], axis=0)


def _pad_gate(w, first_row):
    return jnp.pad(_pad_heads(w, 4), ((first_row, 128 - GLA_GATE_RANK - first_row), (0, 0)))


def _own_slot(shard, chip):
    zone = lax.empty((N_CHIPS,) + shard.shape, shard.dtype)
    return lax.dynamic_update_slice(zone, shard[None], (chip,) + (0,) * shard.ndim)


def _reduce_to_owners(grads, pos, tag, overlap):
    n = len(grads)
    lands = [lax.empty((N_CHIPS, g.shape[1] // 2, g.shape[2]), F32) for g in grads]
    handle, token = _split_start(tag + "_pair_start", list(grads) + lands, n, _pair_swap_plan(n))
    got = _split_wait(tag + "_pair_wait", handle, n, _pair_swap_plan(n), overlap[0](token))
    sums = [_pair_add_call(got[a], got[n + a], pos, f"{tag}_pair_add{a}") for a in range(n)]
    lands = [lax.empty((3,) + s.shape[1:], s.dtype) for s in sums]
    handle, token = _split_start(tag + "_chip_start", sums + lands, 3 * n, _chip_swap_plan(n))
    got = _split_wait(tag + "_chip_wait", handle, 3 * n, _chip_swap_plan(n), overlap[1](token))
    halves = [_chip_add_call(got[a], got[n + a], pos, f"{tag}_chip_add{a}") for a in range(n)]
    handle, token = _split_start(tag + "_join_start", halves, n, _pair_join_plan(n))
    return _split_wait(tag + "_join_wait", handle, n, _pair_join_plan(n), overlap[2](token))


def kernel(x, norm_mix_pre, w_in, w_gate_up_fwd, b_gate_fwd, w_gate_up_bwd, b_gate_bwd, gla_norm, swa_sink, rel_bias, w_out, norm_mix_post, norm_mlp_pre, w_up, w_down, norm_mlp_post, loss_target, m_norm_mix_pre, m_w_in, m_w_gate_up_fwd, m_b_gate_fwd, m_w_gate_up_bwd, m_b_gate_bwd, m_gla_norm, m_swa_sink, m_rel_bias, m_w_out, m_norm_mix_post, m_norm_mlp_pre, m_w_up, m_w_down, m_norm_mlp_post, v_norm_mix_pre, v_w_in, v_w_gate_up_fwd, v_b_gate_fwd, v_w_gate_up_bwd, v_b_gate_bwd, v_gla_norm, v_swa_sink, v_rel_bias, v_w_out, v_norm_mix_post, v_norm_mlp_pre, v_w_up, v_w_down, v_norm_mlp_post):
    given = dict(locals())
    cx, cy, cc = _position()
    chip = (2 * cx + cy).astype(jnp.int32)
    pos = jnp.stack([chip, cc.astype(jnp.int32)])
    seq, tgt = x[0], loss_target[0]
    L = seq.shape[0]

    gates = jnp.concatenate([w_gate_up_fwd[0], w_gate_up_bwd[0]], axis=0).astype(COMM_DTYPE)
    all_in, all_gates = _first_gather_call([w_in[0].astype(COMM_DTYPE), gates])
    rest = [w_out[0].astype(COMM_DTYPE), jnp.stack([w_up[0], w_down[0]]).astype(COMM_DTYPE)]
    stage_one, stage_two = _gather_plans([R_OUT, R_UP])
    handle, token = _split_start("gather_chip_start", rest + [_own_slot(s, chip) for s in rest], 6, stage_one)

    w_in_p = _mx(_pad_w_in(jnp.concatenate([all_in[j] for j in range(N_CHIPS)], axis=1)))
    gates_full = jnp.concatenate([all_gates[j] for j in range(N_CHIPS)], axis=1)
    wgf_p = _mx(_pad_gate(gates_full[:GLA_GATE_RANK], 0))
    wgb_p = _mx(_pad_gate(gates_full[GLA_GATE_RANK:], GLA_GATE_RANK))
    bf_p, bb_p = _pad_heads(b_gate_fwd, 4), _pad_heads(b_gate_bwd, 4)
    buckets = jnp.asarray(_band_buckets())
    bias = _bias_call(rel_bias, buckets)
    sink1 = swa_sink.reshape(SWA_Q_HEADS)

    qa, ka, va, ga, qs, ks, vs, za = _proj_call(seq, norm_mix_pre, w_in_p, dep=token)
    halo = ((SWA_BLOCK, SWA_BLOCK), (0, 0))
    ks_p, vs_p = jnp.pad(ks, halo), jnp.pad(vs, halo)
    o_f, o_b, s_f, s_b = _gla_fwd_call(qa, ka, va, za, wgf_p, bf_p, wgb_p, bb_p)
    arrays = _split_wait("gather_chip_wait", handle, 6, stage_one, o_f)
    handle, token = _split_start("gather_pair_start", list(arrays), 6, stage_two)
    o_s = _swa_fwd_call(qs, ks_p, vs_p, bias, sink1, dep=token)
    arrays = _split_wait("gather_pair_wait", handle, 6, stage_two, o_s)
    w_out_p = _mx(_pad_w_out(arrays[2].reshape(N_CHIPS * R_OUT, D_MODEL)))
    w_ud = _mx(arrays[3])
    cat, mix, h1, n2 = _mix_call(o_f, o_b, ga, o_s, seq, gla_norm, w_out_p, norm_mix_post, norm_mlp_pre)
    a, rz, dh2, dff, loss, d_post2 = _mlp_fwd_call(n2, h1, tgt, w_ud, norm_mlp_post)

    dz, dn2 = _mlp_bwd_call(dff, rz, w_ud)
    dw_down, dw_up4 = _mlp_wgrad_call(a, dff, n2, dz)
    dh1, do, dga, dos, dw_out_p, d_pre2, d_post, d_gn = _mix_bwd_call(
        dn2, dh2, h1, mix, cat, o_f, o_b, ga, gla_norm, norm_mix_post, norm_mlp_pre, w_out_p)
    done = {}

    def gla_backward(tok):
        done["gla"] = _gla_bwd_call(qa, ka, va, za, do, s_f, s_b, wgf_p, bf_p, wgb_p, bb_p, dep=tok)
        return done["gla"][0]

    def swa_backward(tok):
        done["swa"] = _swa_bwd_call(qs, ks_p, vs_p, bias, sink1, dos, dep=tok)
        return done["swa"][0]

    def in_backward(tok):
        dqf, dkf, dvf, dzf, _, _, dqb, dkb, dvb, dzb, _, _ = done["gla"]
        dqs, dks_p, dvs_p, _, _ = done["swa"]
        dks = dks_p[SWA_BLOCK:SWA_BLOCK + L]
        dvs = dvs_p[SWA_BLOCK:SWA_BLOCK + L]
        done["in"] = _in_bwd_call(
            seq, dh1, norm_mix_pre, w_in_p,
            pairs=[(C_QA, (dqf, dqb)), (C_KA, (dkf, dkb)), (C_VA, (dvf, dvb)), (C_ZA, (dzf, dzb))],
            singles=[(C_GA, dga), (C_QS, dqs), (C_KS, dks), (C_VS, dvs)], dep=tok)
        return done["in"][0]

    g_up, g_down, g_out = _reduce_to_owners(
        [dw_up4, dw_down.reshape(N_CHIPS, R_DOWN, D_MODEL), _unpad_w_out(dw_out_p).reshape(N_CHIPS, R_OUT, D_MODEL)],
        pos, "mlp", [gla_backward, swa_backward, in_backward])
    dx, dw_in_p, d_pre = done["in"]
    dwf, dbf, dwb, dbb = done["gla"][4], done["gla"][5], done["gla"][10], done["gla"][11]
    drel = _relbias_call(done["swa"][3], buckets)
    dsink = done["swa"][4]

    small_grads = [d_pre, d_post, d_pre2, d_post2, _unpad_heads(dbf, 4), _unpad_heads(dbb, 4), d_gn, dsink, drel]
    gate_grads = [_unpad_heads(dwf[:GLA_GATE_RANK], 4), _unpad_heads(dwb[GLA_GATE_RANK:2 * GLA_GATE_RANK], 4)]
    small_params = [(given[n], given["m_" + n], given["v_" + n]) for n in SMALL_NAMES]
    upd = {}

    def update_up(tok):
        upd["w_up"] = (g_up,) + tuple(_adamw_call(w_up[0], g_up, m_w_up[0], v_w_up[0], "adamw_w_up", dep=tok))
        return upd["w_up"][1]

    def update_small(tok):
        per_name, gf_sum, gb_sum = _small_update_call(small_grads, gate_grads, small_params, dep=tok)
        upd.update(dict(zip(SMALL_NAMES, per_name)))
        for name, total in (("w_gate_up_fwd", gf_sum), ("w_gate_up_bwd", gb_sum)):
            g = lax.dynamic_slice(total, (0, chip * 64), (GLA_GATE_RANK, 64))
            upd[name] = (g,) + tuple(_adamw_call(given[name][0], g, given["m_" + name][0], given["v_" + name][0],
                                                 "adamw_" + name))
        upd["w_down"] = (g_down,) + tuple(
            _adamw_call(w_down[0], g_down, m_w_down[0], v_w_down[0], "adamw_w_down", dep=gf_sum))
        return upd["w_down"][1]

    def update_out(tok):
        upd["w_out"] = (g_out,) + tuple(_adamw_call(w_out[0], g_out, m_w_out[0], v_w_out[0], "adamw_w_out", dep=tok))
        return upd["w_out"][1]

    dw_in = _unpad_w_in(dw_in_p)
    dw_in4 = jnp.stack([dw_in[:, R_IN * j:R_IN * (j + 1)] for j in range(N_CHIPS)], axis=0)
    (g_in,) = _reduce_to_owners([dw_in4], pos, "in", [update_up, update_small, update_out])
    upd["w_in"] = (g_in,) + tuple(_adamw_call(w_in[0], g_in, m_w_in[0], v_w_in[0], "adamw_w_in"))

    big = ("w_in", "w_gate_up_fwd", "w_gate_up_bwd", "w_out", "w_up", "w_down")
    names = ["norm_mix_pre", "w_in", "w_gate_up_fwd", "b_gate_fwd", "w_gate_up_bwd", "b_gate_bwd", "gla_norm",
             "swa_sink", "rel_bias", "w_out", "norm_mix_post", "norm_mlp_pre", "w_up", "w_down", "norm_mlp_post"]
    outs = [lax.psum(loss[0, 0], MESH_AXES), dx[None]]
    for kind in range(4):
        outs += [upd[n][kind][None] if n in big else upd[n][kind] for n in names]
    return tuple(outs)
```

```python
import collections
import math

import numpy as np
import jax
import jax.numpy as jnp
from jax import lax
from jax.experimental import pallas as pl
from jax.experimental.pallas import tpu as pltpu

F32 = jnp.float32
MXU_DTYPE = jnp.bfloat16
COMM_DTYPE = jnp.bfloat16

D_MODEL = 1024
D_FF = 4096
N_CHIPS = 4
GLA_HEADS = 4
GLA_CHUNK = 64
GLA_GATE_RANK = 16
GLA_GATE_NORM = 16.0
SWA_Q_HEADS = 8
SWA_KV_HEADS = 2
SWA_BLOCK = 128
REL_BUCKETS = 32
REL_MAX_DIST = 128
NORM_EPS = 1e-6
HEAD_PAD = 128

ADAM_LR = 0.001
ADAM_B1 = 0.9
ADAM_B2 = 0.999
ADAM_EPS = 1e-08
ADAM_WD = 0.01
ADAM_STEP = 10

C_QA, C_KA, C_VA, C_GA = (0, 512), (512, 512), (1024, 512), (1536, 512)
C_QS, C_KS, C_VS, C_ZA = (2048, 1024), (3072, 256), (3328, 256), (3584, 128)
IN_PAD = 3712
OUT_PAD = 1536

R_IN, R_OUT, R_UP, R_DOWN = 584, 256, 1024, 1024
PACK_USED = R_IN + R_OUT + R_UP + R_DOWN + 2
PACK_ROWS = 2944
PACK_HALF = PACK_ROWS // 2
ADD_ROWS = 368

VMEM_BIG = 56 * 1024 * 1024
MESH_AXES = ("x", "y", "c")
MESH_ID = pl.DeviceIdType.MESH


def _mx(a):
    return a.astype(MXU_DTYPE)


def _dot(a, b):
    return jnp.dot(a, b, preferred_element_type=F32)


def _dot_nt(a, b):
    return lax.dot_general(a, b, (((1,), (1,)), ((), ())), preferred_element_type=F32)


def _dot_tn(a, b):
    return lax.dot_general(a, b, (((0,), (0,)), ((), ())), preferred_element_type=F32)


def _dot_exact(a, b):
    return jnp.dot(a, b, precision=lax.Precision.HIGHEST, preferred_element_type=F32)


def _dot_tn_exact(a, b):
    return lax.dot_general(a, b, (((0,), (0,)), ((), ())), precision=lax.Precision.HIGHEST,
                           preferred_element_type=F32)


def _rms_r(x):
    return lax.rsqrt(jnp.mean(x * x, axis=-1, keepdims=True) + NORM_EPS)


def _rms_bwd(x, r, g, dy):
    xh = x * r
    gdy = dy * g
    dx = r * (gdy - xh * jnp.mean(gdy * xh, axis=-1, keepdims=True))
    return dx, jnp.sum(dy * xh, axis=0, keepdims=True)


def _params(sem=None, vmem=None):
    kw = {}
    if sem is not None:
        kw["dimension_semantics"] = sem
    if vmem is not None:
        kw["vmem_limit_bytes"] = vmem
    return pltpu.CompilerParams(**kw)


def _vmem_spec():
    return pl.BlockSpec(memory_space=pltpu.VMEM)


def _row_spec(tm, width):
    return pl.BlockSpec((tm, width), lambda i: (i, 0))


def _full_spec(shape):
    return pl.BlockSpec(shape, lambda i: (0,) * len(shape))


def _any_spec():
    return pl.BlockSpec(memory_space=pl.ANY)


def _after(body, n_in, dep):
    if dep is None:
        return body, [], []
    return (lambda *refs: body(*refs[:n_in], *refs[n_in + 1:])), [dep], [_any_spec()]


def _proj_call(x, g_pre, w_in_p, dep=None):
    L = x.shape[0]
    tm = min(256, L)
    groups = [(C_QA, F32), (C_KA, F32), (C_VA, MXU_DTYPE), (C_GA, F32),
              (C_QS, MXU_DTYPE), (C_KS, MXU_DTYPE), (C_VS, MXU_DTYPE), (C_ZA, F32)]

    def body(x_ref, g_ref, w_ref, *outs):
        xv = x_ref[...]
        u = _mx(xv * _rms_r(xv) * g_ref[...])
        for ref, ((off, width), _) in zip(outs, groups):
            ref[...] = _dot(u, w_ref[:, off:off + width]).astype(ref.dtype)

    body, extra, extra_specs = _after(body, 3, dep)
    return pl.pallas_call(
        body, name="proj_fwd", grid=(L // tm,),
        in_specs=[_row_spec(tm, D_MODEL), _full_spec((1, D_MODEL)), _vmem_spec()] + extra_specs,
        out_specs=[_row_spec(tm, w) for (_, w), _ in groups],
        out_shape=[jax.ShapeDtypeStruct((L, w), dt) for (_, w), dt in groups],
        compiler_params=_params(("arbitrary",), VMEM_BIG),
    )(x, g_pre, w_in_p, *extra)


_GlaPre = collections.namedtuple("_GlaPre", "z g eb enb elb dec qd ki ks")


def _gla_chunk_pre(q, k, z, w, bias, tri, rev):
    g = _dot(_mx(z), w) + bias
    la = (jnp.minimum(g, 0.0) - jnp.log(1.0 + jnp.exp(-jnp.abs(g)))) / GLA_GATE_NORM
    b = _dot_exact(tri.astype(F32), la)
    blast = b[0:1] if rev else b[GLA_CHUNK - 1:GLA_CHUNK]
    eb = jnp.exp(b)
    enb = jnp.exp(-b)
    elb = jnp.exp(blast - b)
    dec = jnp.exp(blast)
    qd = q * 0.125 * eb
    return _GlaPre(z, g, eb, enb, elb, dec, qd, k * enb, k * elb)


def _tri_masks():
    row = lax.broadcasted_iota(jnp.int32, (GLA_CHUNK, GLA_CHUNK), 0)
    col = lax.broadcasted_iota(jnp.int32, (GLA_CHUNK, GLA_CHUNK), 1)
    return row >= col, row <= col, row


def _gla_fwd_call(qa, ka, va, za, wgf, bgf, wgb, bgb):
    L = qa.shape[0]
    br = min(512, L)
    nb, nc, n_chunks = L // br, br // GLA_CHUNK, L // GLA_CHUNK
    hw = GLA_HEADS * HEAD_PAD

    def body(qaf, kaf, vaf, zaf, qab, kab, vab, zab, wgf_r, bgf_r, wgb_r, bgb_r,
             of_r, ob_r, sf_r, sb_r, st_f, st_b):
        @pl.when(pl.program_id(0) == 0)
        def _():
            st_f[...] = jnp.zeros_like(st_f)
            st_b[...] = jnp.zeros_like(st_b)

        tri_f, tri_b, _ = _tri_masks()

        def one(rev, q_r, k_r, v_r, z_r, w_r, b_r, o_r, s_r, st, ci):
            tri = tri_b if rev else tri_f
            rows = pl.ds(pl.multiple_of(ci * GLA_CHUNK, GLA_CHUNK), GLA_CHUNK)
            pre = _gla_chunk_pre(q_r[rows, :], k_r[rows, :], z_r[rows, :], w_r[...], b_r[...], tri, rev)
            for h in range(GLA_HEADS):
                sl = slice(HEAD_PAD * h, HEAD_PAD * (h + 1))
                qd, ki, ks = _mx(pre.qd[:, sl]), _mx(pre.ki[:, sl]), _mx(pre.ks[:, sl])
                a = jnp.where(tri, _dot_nt(qd, ki), 0.0)
                v = v_r[rows, sl]
                s_t = st[h]
                s_r[ci, h] = s_t
                o_r[rows, sl] = _dot(_mx(a), v) + _dot_nt(qd, _mx(s_t))
                st[h] = s_t * pre.dec[:, sl] + _dot_tn(v, ks)

        def loop(t, carry):
            one(False, qaf, kaf, vaf, zaf, wgf_r, bgf_r, of_r, sf_r, st_f, t)
            one(True, qab, kab, vab, zab, wgb_r, bgb_r, ob_r, sb_r, st_b, nc - 1 - t)
            return carry

        lax.fori_loop(0, nc, loop, 0)

    fwd = lambda i: (i, 0)
    bwd = lambda i: (nb - 1 - i, 0)
    ins = lambda m: [pl.BlockSpec((br, hw), m), pl.BlockSpec((br, hw), m),
                     pl.BlockSpec((br, hw), m), pl.BlockSpec((br, 128), m)]
    wspecs = [_full_spec((128, hw)), _full_spec((1, hw))] * 2
    s_shape = (nc, GLA_HEADS, HEAD_PAD, HEAD_PAD)
    return pl.pallas_call(
        body, name="gla_fwd", grid=(nb,),
        in_specs=ins(fwd) + ins(bwd) + wspecs,
        out_specs=[pl.BlockSpec((br, hw), fwd), pl.BlockSpec((br, hw), bwd),
                   pl.BlockSpec(s_shape, lambda i: (i, 0, 0, 0)),
                   pl.BlockSpec(s_shape, lambda i: (nb - 1 - i, 0, 0, 0))],
        out_shape=[jax.ShapeDtypeStruct((L, hw), F32), jax.ShapeDtypeStruct((L, hw), F32),
                   jax.ShapeDtypeStruct((n_chunks,) + s_shape[1:], F32),
                   jax.ShapeDtypeStruct((n_chunks,) + s_shape[1:], F32)],
        scratch_shapes=[pltpu.VMEM(s_shape[1:], F32), pltpu.VMEM(s_shape[1:], F32)],
        compiler_params=_params(("arbitrary",), VMEM_BIG),
    )(qa, ka, va, za, qa, ka, va, za, wgf, bgf, wgb, bgb)


def _gla_bwd_call(qa, ka, va, za, do, sf, sb, wgf, bgf, wgb, bgb, dep=None):
    L = qa.shape[0]
    br = min(256, L)
    nb, nc = L // br, br // GLA_CHUNK
    hw = GLA_HEADS * HEAD_PAD

    def body(qaf, kaf, vaf, zaf, dof, sf_r, qab, kab, vab, zab, dob, sb_r, wgf_r, bgf_r, wgb_r, bgb_r,
             dqf, dkf, dvf, dzf, dwf, dbf, dqb, dkb, dvb, dzb, dwb, dbb, gt_f, gt_b):
        @pl.when(pl.program_id(0) == 0)
        def _():
            for ref in (gt_f, gt_b, dwf, dbf, dwb, dbb):
                ref[...] = jnp.zeros_like(ref)

        tri_f, tri_b, row = _tri_masks()
        row_w = lax.broadcasted_iota(jnp.int32, (GLA_CHUNK, HEAD_PAD), 0)

        def one(rev, q_r, k_r, v_r, z_r, do_r, s_r, w_r, b_r, dq_r, dk_r, dv_r, dz_r, dw_r, dbias_r, gt, ci):
            tri = tri_b if rev else tri_f
            last_row = 0 if rev else GLA_CHUNK - 1
            rows = pl.ds(pl.multiple_of(ci * GLA_CHUNK, GLA_CHUNK), GLA_CHUNK)
            w = w_r[...]
            pre = _gla_chunk_pre(q_r[rows, :], k_r[rows, :], z_r[rows, :], w, b_r[...], tri, rev)
            db_parts = []
            for h in range(GLA_HEADS):
                sl = slice(HEAD_PAD * h, HEAD_PAD * (h + 1))
                qd_f, ki_f, ks_f = pre.qd[:, sl], pre.ki[:, sl], pre.ks[:, sl]
                qd, ki, ks = _mx(qd_f), _mx(ki_f), _mx(ks_f)
                a = _mx(jnp.where(tri, _dot_nt(qd, ki), 0.0))
                v = v_r[rows, sl]
                do_h = _mx(do_r[rows, sl])
                s_t = s_r[ci, h]
                g_t = gt[h]
                g_m = _mx(g_t)
                da = _mx(jnp.where(tri, _dot_nt(do_h, v), 0.0))
                dv_r[rows, sl] = _dot_tn(a, do_h) + _dot_nt(ks, g_m)
                dqd = _dot(da, ki) + _dot(do_h, _mx(s_t))
                dki = _dot_tn(da, qd)
                dks = _dot(v, g_m)
                ddec = jnp.sum(g_t * s_t, axis=0, keepdims=True)
                gt[h] = g_t * pre.dec[:, sl] + _dot_tn(do_h, qd)
                dq_r[rows, sl] = dqd * pre.eb[:, sl] * 0.125
                dk_r[rows, sl] = dki * pre.enb[:, sl] + dks * pre.elb[:, sl]
                dblast = jnp.sum(dks * ks_f, axis=0, keepdims=True) + pre.dec[:, sl] * ddec
                db_h = dqd * qd_f - dki * ki_f - dks * ks_f
                db_parts.append(db_h + jnp.where(row_w == last_row, dblast, 0.0))
            db = jnp.concatenate(db_parts, axis=1)
            dla = _dot_tn_exact(tri.astype(F32), db)
            dg = dla * (1.0 / GLA_GATE_NORM) * (1.0 / (1.0 + jnp.exp(pre.g)))
            dg_m = _mx(dg)
            dz_r[rows, :] = _dot_nt(dg_m, w)
            dw_r[...] += _dot_tn(_mx(pre.z), dg_m)
            dbias_r[...] += jnp.sum(dg, axis=0, keepdims=True)

        def loop(t, carry):
            one(False, qaf, kaf, vaf, zaf, dof, sf_r, wgf_r, bgf_r, dqf, dkf, dvf, dzf, dwf, dbf, gt_f, nc - 1 - t)
            one(True, qab, kab, vab, zab, dob, sb_r, wgb_r, bgb_r, dqb, dkb, dvb, dzb, dwb, dbb, gt_b, t)
            return carry

        lax.fori_loop(0, nc, loop, 0)

    last_first = lambda i: (nb - 1 - i, 0)
    first_last = lambda i: (i, 0)
    s_shape = (nc, GLA_HEADS, HEAD_PAD, HEAD_PAD)

    def ins(m):
        return [pl.BlockSpec((br, hw), m), pl.BlockSpec((br, hw), m), pl.BlockSpec((br, hw), m),
                pl.BlockSpec((br, 128), m), pl.BlockSpec((br, hw), m),
                pl.BlockSpec(s_shape, lambda i: m(i) + (0, 0))]

    def outs(m):
        return [pl.BlockSpec((br, hw), m), pl.BlockSpec((br, hw), m), pl.BlockSpec((br, hw), m),
                pl.BlockSpec((br, 128), m), _full_spec((128, hw)), _full_spec((1, hw))]

    out_shape = [jax.ShapeDtypeStruct((L, hw), F32)] * 3 + [
        jax.ShapeDtypeStruct((L, 128), F32), jax.ShapeDtypeStruct((128, hw), F32),
        jax.ShapeDtypeStruct((1, hw), F32)]
    wspecs = [_full_spec((128, hw)), _full_spec((1, hw))] * 2
    body, extra, extra_specs = _after(body, 16, dep)
    return pl.pallas_call(
        body, name="gla_bwd", grid=(nb,),
        in_specs=ins(last_first) + ins(first_last) + wspecs + extra_specs,
        out_specs=outs(last_first) + outs(first_last),
        out_shape=out_shape + out_shape,
        scratch_shapes=[pltpu.VMEM(s_shape[1:], F32), pltpu.VMEM(s_shape[1:], F32)],
        compiler_params=_params(("arbitrary",), VMEM_BIG),
    )(qa, ka, va, za, do, sf, qa, ka, va, za, do, sb, wgf, bgf, wgb, bgb, *extra)


def _t5_buckets(rel):
    nb = REL_BUCKETS // 2
    ret = (rel > 0).astype(np.int32) * nb
    n = np.abs(rel)
    max_exact = nb // 2
    large = max_exact + (np.log(np.maximum(n, 1).astype(np.float32) / max_exact)
                         / math.log(REL_MAX_DIST / max_exact) * (nb - max_exact)).astype(np.int32)
    large = np.minimum(large, nb - 1)
    return ret + np.where(n < max_exact, n, large)


SWA_GROUP = SWA_Q_HEADS // SWA_KV_HEADS
SWA_SPAN = 3 * SWA_BLOCK
SWA_GROUP_LANES = SWA_GROUP * SWA_BLOCK


def _band_buckets():
    s = np.arange(SWA_SPAN)[:, None]
    c = np.arange(SWA_BLOCK)[None, :]
    return _t5_buckets(s - SWA_BLOCK - c).astype(np.int32)


def _swa_valid(n, seq_len):
    s = lax.broadcasted_iota(jnp.int32, (SWA_SPAN, SWA_GROUP_LANES), 0)
    c = lax.broadcasted_iota(jnp.int32, (SWA_SPAN, SWA_GROUP_LANES), 1) & (SWA_BLOCK - 1)
    rel = s - SWA_BLOCK - c
    key_pos = (n - 1) * SWA_BLOCK + s
    return (jnp.abs(rel) <= SWA_BLOCK) & (key_pos >= 0) & (key_pos < seq_len)


def _swa_sink_row(sink_r, kv):
    lane = lax.broadcasted_iota(jnp.int32, (1, SWA_GROUP_LANES), 1)
    row = jnp.full((1, SWA_GROUP_LANES), sink_r[kv * SWA_GROUP], F32)
    for g in range(1, SWA_GROUP):
        row = jnp.where(lane >= g * SWA_BLOCK, sink_r[kv * SWA_GROUP + g], row)
    return row


def _swa_group(ref, kv):
    first = kv * SWA_GROUP
    return jnp.concatenate([ref[:, HEAD_PAD * h:HEAD_PAD * (h + 1)] for h in range(first, first + SWA_GROUP)],
                           axis=0)


def _swa_probs(kk, qg, bias_t, sink_row, valid):
    st = _dot_nt(kk, qg) * 0.125 + bias_t
    st = jnp.where(valid, st, -1e30)
    m = jnp.maximum(jnp.max(st, axis=0, keepdims=True), sink_row)
    p = jnp.exp(st - m)
    e_sink = jnp.exp(sink_row - m)
    inv = 1.0 / (jnp.sum(p, axis=0, keepdims=True) + e_sink)
    return p * inv, e_sink * inv


def _swa_fwd_call(qs, ks, vs, bias, sink, dep=None):
    L = qs.shape[0]

    def body(q_r, k_r, v_r, bias_r, sink_r, o_r):
        n = pl.program_id(0)
        span = pl.ds(pl.multiple_of(n * SWA_BLOCK, SWA_BLOCK), SWA_SPAN)
        valid = _swa_valid(n, L)
        for kv in range(SWA_KV_HEADS):
            ksl = slice(HEAD_PAD * kv, HEAD_PAD * (kv + 1))
            pn, _ = _swa_probs(k_r[span, ksl], _swa_group(q_r, kv), bias_r[kv], _swa_sink_row(sink_r, kv), valid)
            og = _dot_tn(_mx(pn), v_r[span, ksl])
            for g in range(SWA_GROUP):
                h = kv * SWA_GROUP + g
                o_r[:, HEAD_PAD * h:HEAD_PAD * (h + 1)] = og[SWA_BLOCK * g:SWA_BLOCK * (g + 1)].astype(o_r.dtype)

    qw = SWA_Q_HEADS * HEAD_PAD
    body, extra, extra_specs = _after(body, 5, dep)
    return pl.pallas_call(
        body, name="swa_fwd", grid=(L // SWA_BLOCK,),
        in_specs=[_row_spec(SWA_BLOCK, qw), _vmem_spec(), _vmem_spec(), _vmem_spec(),
                  pl.BlockSpec(memory_space=pltpu.SMEM)] + extra_specs,
        out_specs=_row_spec(SWA_BLOCK, qw),
        out_shape=jax.ShapeDtypeStruct((L, qw), MXU_DTYPE),
        compiler_params=_params(("arbitrary",), VMEM_BIG),
    )(qs, ks, vs, bias, sink, *extra)


def _swa_bwd_call(qs, ks, vs, bias, sink, do, dep=None):
    L = qs.shape[0]
    qw = SWA_Q_HEADS * HEAD_PAD
    kw = SWA_KV_HEADS * HEAD_PAD

    def body(q_r, k_r, v_r, bias_r, sink_r, do_r, dq_r, dk_r, dv_r, dbias_r, dsink_r):
        n = pl.program_id(0)

        @pl.when(n == 0)
        def _():
            for ref in (dk_r, dv_r, dbias_r, dsink_r):
                ref[...] = jnp.zeros_like(ref)

        span = pl.ds(pl.multiple_of(n * SWA_BLOCK, SWA_BLOCK), SWA_SPAN)
        valid = _swa_valid(n, L)
        for kv in range(SWA_KV_HEADS):
            ksl = slice(HEAD_PAD * kv, HEAD_PAD * (kv + 1))
            kk = k_r[span, ksl]
            vv = v_r[span, ksl]
            qg = _swa_group(q_r, kv)
            dog = _swa_group(do_r, kv)
            pn, p_sink = _swa_probs(kk, qg, bias_r[kv], _swa_sink_row(sink_r, kv), valid)
            dp = _dot_nt(vv, dog)
            delta = jnp.sum(pn * dp, axis=0, keepdims=True)
            ds = pn * (dp - delta)
            dsink_r[kv] -= p_sink * delta
            dbias_r[kv] += ds
            ds_m = _mx(ds)
            dqg = _dot_tn(ds_m, kk) * 0.125
            for g in range(SWA_GROUP):
                h = kv * SWA_GROUP + g
                dq_r[:, HEAD_PAD * h:HEAD_PAD * (h + 1)] = dqg[SWA_BLOCK * g:SWA_BLOCK * (g + 1)]
            dk_r[span, ksl] += _dot(ds_m, qg) * 0.125
            dv_r[span, ksl] += _dot(_mx(pn), dog)

    body, extra, extra_specs = _after(body, 6, dep)
    return pl.pallas_call(
        body, name="swa_bwd", grid=(L // SWA_BLOCK,),
        in_specs=[_row_spec(SWA_BLOCK, qw), _vmem_spec(), _vmem_spec(), _vmem_spec(),
                  pl.BlockSpec(memory_space=pltpu.SMEM), _row_spec(SWA_BLOCK, qw)] + extra_specs,
        out_specs=[_row_spec(SWA_BLOCK, qw), _vmem_spec(), _vmem_spec(), _vmem_spec(), _vmem_spec()],
        out_shape=[jax.ShapeDtypeStruct((L, qw), F32),
                   jax.ShapeDtypeStruct((L + 2 * SWA_BLOCK, kw), F32),
                   jax.ShapeDtypeStruct((L + 2 * SWA_BLOCK, kw), F32),
                   jax.ShapeDtypeStruct((SWA_KV_HEADS, SWA_SPAN, SWA_GROUP_LANES), F32),
                   jax.ShapeDtypeStruct((SWA_KV_HEADS, 1, SWA_GROUP_LANES), F32)],
        compiler_params=_params(("arbitrary",), VMEM_BIG),
    )(qs, ks, vs, bias, sink, do, *extra)


def _bias_call(rel_bias, buckets):
    def body(t_r, bk_r, o_r):
        bk = bk_r[...]
        for h in range(SWA_Q_HEADS):
            acc = jnp.zeros(bk.shape, F32)
            for b in range(REL_BUCKETS):
                acc = jnp.where(bk == b, t_r[b, h], acc)
            g = h % SWA_GROUP
            o_r[h // SWA_GROUP, :, SWA_BLOCK * g:SWA_BLOCK * (g + 1)] = acc

    return pl.pallas_call(
        body, name="band_bias",
        in_specs=[pl.BlockSpec(memory_space=pltpu.SMEM), _vmem_spec()], out_specs=_vmem_spec(),
        out_shape=jax.ShapeDtypeStruct((SWA_KV_HEADS, SWA_SPAN, SWA_GROUP_LANES), F32),
    )(rel_bias, buckets)


def _relbias_call(dbias, dsink, buckets):
    def body(db_r, ds_r, bk_r, o_r, os_r):
        bk = bk_r[...]
        rowi = lax.broadcasted_iota(jnp.int32, (REL_BUCKETS, 128), 0)
        lanei = lax.broadcasted_iota(jnp.int32, (REL_BUCKETS, 128), 1)
        lane1 = lax.broadcasted_iota(jnp.int32, (1, 128), 1)
        acc = jnp.zeros((REL_BUCKETS, 128), F32)
        acc_sink = jnp.zeros((1, 128), F32)
        for h in range(SWA_Q_HEADS):
            kv, g = h // SWA_GROUP, h % SWA_GROUP
            lanes = slice(SWA_BLOCK * g, SWA_BLOCK * (g + 1))
            part = db_r[kv, :, lanes]
            for b in range(REL_BUCKETS):
                s = jnp.sum(jnp.where(bk == b, part, 0.0))
                acc = acc + jnp.where((rowi == b) & (lanei == h), s, 0.0)
            acc_sink = acc_sink + jnp.where(lane1 == h, jnp.sum(ds_r[kv, :, lanes]), 0.0)
        o_r[...] = acc
        os_r[...] = acc_sink

    return pl.pallas_call(
        body, name="relbias_grad",
        in_specs=[_vmem_spec()] * 3, out_specs=[_vmem_spec()] * 2,
        out_shape=[jax.ShapeDtypeStruct((REL_BUCKETS, 128), F32), jax.ShapeDtypeStruct((1, 128), F32)],
    )(dbias, dsink, buckets)


def _mix_call(o_f, o_b, ga, o_s, x, gn, w_out_p, g_post, g_pre2):
    L = x.shape[0]
    tm = min(256, L)
    hw = GLA_HEADS * HEAD_PAD

    def body(of_r, ob_r, ga_r, os_r, x_r, gn_r, w_r, gp_r, g2_r, cat_r, mix_r, h1_r, n2_r):
        gn_v = gn_r[...]
        for h in range(GLA_HEADS):
            sl = slice(HEAD_PAD * h, HEAD_PAD * (h + 1))
            oh = of_r[:, sl] + ob_r[:, sl]
            on = oh * _rms_r(oh) * gn_v
            gate = ga_r[:, sl]
            cat_r[:, sl] = (on * (gate * jax.nn.sigmoid(gate))).astype(cat_r.dtype)
        os_v = os_r[...]
        cat_r[:, hw:] = os_v
        mix = _dot(cat_r[:, :hw], w_r[:hw, :]) + _dot(os_v, w_r[hw:, :])
        mix_r[...] = mix
        h1 = x_r[...] + mix * _rms_r(mix) * gp_r[...]
        h1_r[...] = h1
        n2_r[...] = (h1 * _rms_r(h1) * g2_r[...]).astype(n2_r.dtype)

    return pl.pallas_call(
        body, name="mix_fwd", grid=(L // tm,),
        in_specs=[_row_spec(tm, hw), _row_spec(tm, hw), _row_spec(tm, hw), _row_spec(tm, OUT_PAD - hw),
                  _row_spec(tm, D_MODEL), _full_spec((1, HEAD_PAD)), _vmem_spec(),
                  _full_spec((1, D_MODEL)), _full_spec((1, D_MODEL))],
        out_specs=[_row_spec(tm, OUT_PAD), _row_spec(tm, D_MODEL), _row_spec(tm, D_MODEL), _row_spec(tm, D_MODEL)],
        out_shape=[jax.ShapeDtypeStruct((L, OUT_PAD), MXU_DTYPE), jax.ShapeDtypeStruct((L, D_MODEL), F32),
                   jax.ShapeDtypeStruct((L, D_MODEL), F32), jax.ShapeDtypeStruct((L, D_MODEL), MXU_DTYPE)],
        compiler_params=_params(("arbitrary",), VMEM_BIG),
    )(o_f, o_b, ga, o_s, x, gn, w_out_p, g_post, g_pre2)


def _mlp_fwd_call(n2, h1, tgt, w_ud, g_post):
    L = n2.shape[0]
    tm = min(256, L)
    blk = D_FF // N_CHIPS

    def body(n2_r, h1_r, t_r, w_r, g_r, a_r, rz_r, dh2_r, dff_r, loss_r, dg_r):
        @pl.when(pl.program_id(0) == 0)
        def _():
            loss_r[...] = jnp.zeros_like(loss_r)
            dg_r[...] = jnp.zeros_like(dg_r)

        n2v = n2_r[...]
        ff = jnp.zeros((tm, D_MODEL), F32)
        for j in range(N_CHIPS):
            sl = slice(blk * j, blk * (j + 1))
            rz = jnp.maximum(_dot(n2v, w_r[j, 0]), 0.0)
            a = _mx(rz * rz)
            rz_r[:, sl] = rz.astype(rz_r.dtype)
            a_r[:, sl] = a
            ff = ff + _dot(a, w_r[j, 1])
        g = g_r[...]
        r = _rms_r(ff)
        err = h1_r[...] + ff * r * g - t_r[...]
        loss_r[...] += 0.5 * jnp.sum(err * err) / D_MODEL
        dh2 = err * (1.0 / D_MODEL)
        dh2_r[...] = dh2
        dff, dg = _rms_bwd(ff, r, g, dh2)
        dff_r[...] = dff.astype(dff_r.dtype)
        dg_r[...] += dg

    return pl.pallas_call(
        body, name="mlp_fwd", grid=(L // tm,),
        in_specs=[_row_spec(tm, D_MODEL), _row_spec(tm, D_MODEL), _row_spec(tm, D_MODEL),
                  _vmem_spec(), _full_spec((1, D_MODEL))],
        out_specs=[_row_spec(tm, D_FF), _row_spec(tm, D_FF), _row_spec(tm, D_MODEL), _row_spec(tm, D_MODEL),
                   _full_spec((1, 128)), _full_spec((1, D_MODEL))],
        out_shape=[jax.ShapeDtypeStruct((L, D_FF), MXU_DTYPE), jax.ShapeDtypeStruct((L, D_FF), MXU_DTYPE),
                   jax.ShapeDtypeStruct((L, D_MODEL), F32), jax.ShapeDtypeStruct((L, D_MODEL), MXU_DTYPE),
                   jax.ShapeDtypeStruct((1, 128), F32), jax.ShapeDtypeStruct((1, D_MODEL), F32)],
        compiler_params=_params(("arbitrary",), VMEM_BIG),
    )(n2, h1, tgt, w_ud, g_post)


def _mlp_bwd_call(dff, rz, w_ud):
    L = dff.shape[0]
    tm = min(256, L)
    blk = D_FF // N_CHIPS

    def body(dff_r, rz_r, w_r, dz_r, dn2_r):
        dffv = dff_r[...]
        dn2 = jnp.zeros((tm, D_MODEL), F32)
        for j in range(N_CHIPS):
            sl = slice(blk * j, blk * (j + 1))
            dz = _mx(_dot_nt(dffv, w_r[j, 1]) * 2.0 * rz_r[:, sl].astype(F32))
            dz_r[:, sl] = dz
            dn2 = dn2 + _dot_nt(dz, w_r[j, 0])
        dn2_r[...] = dn2

    return pl.pallas_call(
        body, name="mlp_bwd", grid=(L // tm,),
        in_specs=[_row_spec(tm, D_MODEL), _row_spec(tm, D_FF), _vmem_spec()],
        out_specs=[_row_spec(tm, D_FF), _row_spec(tm, D_MODEL)],
        out_shape=[jax.ShapeDtypeStruct((L, D_FF), MXU_DTYPE), jax.ShapeDtypeStruct((L, D_MODEL), F32)],
        compiler_params=_params(("arbitrary",), VMEM_BIG),
    )(dff, rz, w_ud)


def _mlp_wgrad_call(a, dff, n2, dz):
    L = a.shape[0]
    tf = 512
    per = (D_FF // N_CHIPS) // tf

    def body(a_r, dff_r, n2_r, dz_r, dwd_r, dwu_r):
        dwd_r[...] = _dot_tn(a_r[...], dff_r[...])
        dwu_r[...] = _dot_tn(n2_r[...], dz_r[...])

    return pl.pallas_call(
        body, name="mlp_wgrad", grid=(D_FF // tf,),
        in_specs=[pl.BlockSpec((L, tf), lambda j: (0, j)), _vmem_spec(), _vmem_spec(),
                  pl.BlockSpec((L, tf), lambda j: (0, j))],
        out_specs=[pl.BlockSpec((tf, D_MODEL), lambda j: (j, 0)),
                   pl.BlockSpec((None, D_MODEL, tf), lambda j: (j // per, 0, j % per))],
        out_shape=[jax.ShapeDtypeStruct((D_FF, D_MODEL), F32),
                   jax.ShapeDtypeStruct((N_CHIPS, D_MODEL, D_FF // N_CHIPS), F32)],
        compiler_params=_params(("arbitrary",), VMEM_BIG),
    )(a, dff, n2, dz)


def _mix_bwd_call(dn2, dh2, h1, mix, cat, o_f, o_b, ga, gn, g_post, g_pre2, w_out_p):
    L = dn2.shape[0]
    tm = min(256, L)
    hw = GLA_HEADS * HEAD_PAD

    def body(dn2_r, dh2_r, h1_r, mix_r, cat_r, of_r, ob_r, ga_r, gn_r, gp_r, g2_r, w_r,
             dh1_r, do_r, dga_r, dos_r, dw_r, dg2_r, dgp_r, dgn_r):
        @pl.when(pl.program_id(0) == 0)
        def _():
            for ref in (dw_r, dg2_r, dgp_r, dgn_r):
                ref[...] = jnp.zeros_like(ref)

        h1 = h1_r[...]
        dx2, dg2 = _rms_bwd(h1, _rms_r(h1), g2_r[...], dn2_r[...])
        dh1 = dh2_r[...] + dx2
        dh1_r[...] = dh1
        dg2_r[...] += dg2
        mix = mix_r[...]
        dmix, dgp = _rms_bwd(mix, _rms_r(mix), gp_r[...], dh1)
        dgp_r[...] += dgp
        dmix_m = _mx(dmix)
        dw_r[...] += _dot_tn(cat_r[...], dmix_m)
        dcat = _dot_nt(dmix_m, w_r[...])
        dos_r[...] = dcat[:, hw:].astype(dos_r.dtype)
        gn_v = gn_r[...]
        dgn = jnp.zeros((1, HEAD_PAD), F32)
        for h in range(GLA_HEADS):
            sl = slice(HEAD_PAD * h, HEAD_PAD * (h + 1))
            oh = of_r[:, sl] + ob_r[:, sl]
            rr = _rms_r(oh)
            gate = ga_r[:, sl]
            sg = jax.nn.sigmoid(gate)
            doa = dcat[:, sl]
            dga_r[:, sl] = doa * (oh * rr * gn_v) * (sg * (1.0 + gate * (1.0 - sg)))
            do_h, dgn_h = _rms_bwd(oh, rr, gn_v, doa * (gate * sg))
            do_r[:, sl] = do_h
            dgn = dgn + dgn_h
        dgn_r[...] += dgn

    return pl.pallas_call(
        body, name="mix_bwd", grid=(L // tm,),
        in_specs=[_row_spec(tm, D_MODEL)] * 4 + [_row_spec(tm, OUT_PAD)] + [_row_spec(tm, hw)] * 3
        + [_full_spec((1, HEAD_PAD)), _full_spec((1, D_MODEL)), _full_spec((1, D_MODEL)), _vmem_spec()],
        out_specs=[_row_spec(tm, D_MODEL), _row_spec(tm, hw), _row_spec(tm, hw), _row_spec(tm, OUT_PAD - hw),
                   _full_spec((OUT_PAD, D_MODEL)), _full_spec((1, D_MODEL)), _full_spec((1, D_MODEL)),
                   _full_spec((1, HEAD_PAD))],
        out_shape=[jax.ShapeDtypeStruct((L, D_MODEL), F32), jax.ShapeDtypeStruct((L, hw), F32),
                   jax.ShapeDtypeStruct((L, hw), F32), jax.ShapeDtypeStruct((L, OUT_PAD - hw), MXU_DTYPE),
                   jax.ShapeDtypeStruct((OUT_PAD, D_MODEL), F32), jax.ShapeDtypeStruct((1, D_MODEL), F32),
                   jax.ShapeDtypeStruct((1, D_MODEL), F32), jax.ShapeDtypeStruct((1, HEAD_PAD), F32)],
        compiler_params=_params(("arbitrary",), VMEM_BIG),
    )(dn2, dh2, h1, mix, cat, o_f, o_b, ga, gn, g_post, g_pre2, w_out_p)


def _in_bwd_call(x, dh1, g_pre, w_in_p, pairs, singles, dep=None):
    L = x.shape[0]
    tm = min(256, L)
    n_pair, n_single = len(pairs), len(singles)
    groups = [c for c, _ in pairs] + [c for c, _ in singles]

    def body(*refs):
        x_r, dh1_r, g_r, w_r = refs[:4]
        pair_refs = refs[4:4 + 2 * n_pair]
        single_refs = refs[4 + 2 * n_pair:4 + 2 * n_pair + n_single]
        dx_r, dw_r, dg_r = refs[4 + 2 * n_pair + n_single:]

        @pl.when(pl.program_id(0) == 0)
        def _():
            dw_r[...] = jnp.zeros_like(dw_r)
            dg_r[...] = jnp.zeros_like(dg_r)

        xv = x_r[...]
        r = _rms_r(xv)
        g = g_r[...]
        u = _mx(xv * r * g)
        vals = [pair_refs[2 * i][...] + pair_refs[2 * i + 1][...] for i in range(n_pair)]
        vals += [ref[...].astype(F32) for ref in single_refs]
        du = jnp.zeros((tm, D_MODEL), F32)
        for (off, width), val in zip(groups, vals):
            d = _mx(val)
            du = du + _dot_nt(d, w_r[:, off:off + width])
            dw_r[:, off:off + width] += _dot_tn(u, d)
        dx, dg = _rms_bwd(xv, r, g, du)
        dx_r[...] = dh1_r[...] + dx
        dg_r[...] += dg

    arrays = [a for _, pr in pairs for a in pr] + [a for _, a in singles]
    specs = [_row_spec(tm, a.shape[1]) for a in arrays]
    body, extra, extra_specs = _after(body, 4 + len(arrays), dep)
    return pl.pallas_call(
        body, name="in_bwd", grid=(L // tm,),
        in_specs=[_row_spec(tm, D_MODEL), _row_spec(tm, D_MODEL), _full_spec((1, D_MODEL)), _vmem_spec()] + specs
        + extra_specs,
        out_specs=[_row_spec(tm, D_MODEL), _full_spec((D_MODEL, IN_PAD)), _full_spec((1, D_MODEL))],
        out_shape=[jax.ShapeDtypeStruct((L, D_MODEL), F32), jax.ShapeDtypeStruct((D_MODEL, IN_PAD), F32),
                   jax.ShapeDtypeStruct((1, D_MODEL), F32)],
        compiler_params=_params(("arbitrary",), VMEM_BIG),
    )(x, dh1, g_pre, w_in_p, *arrays, *extra)


def _adamw_math(w, g, m, v):
    m = ADAM_B1 * m + (1.0 - ADAM_B1) * g
    v = ADAM_B2 * v + (1.0 - ADAM_B2) * (g * g)
    m_hat = m / (1.0 - ADAM_B1 ** ADAM_STEP)
    v_hat = v / (1.0 - ADAM_B2 ** ADAM_STEP)
    delta = -ADAM_LR * (m_hat / (jnp.sqrt(v_hat) + ADAM_EPS) + ADAM_WD * w)
    return delta, m, v


def _adamw_call(w, g, m, v, name, dep=None):
    rows, cols = w.shape
    tr = min(256, rows)

    def body(w_r, g_r, m_r, v_r, d_r, nm_r, nv_r):
        d_r[...], nm_r[...], nv_r[...] = _adamw_math(w_r[...], g_r[...], m_r[...], v_r[...])

    spec = _row_spec(tr, cols)
    body, extra, extra_specs = _after(body, 4, dep)
    return pl.pallas_call(
        body, name=name, grid=(rows // tr,),
        in_specs=[spec] * 4 + extra_specs, out_specs=[spec] * 3,
        out_shape=[jax.ShapeDtypeStruct(w.shape, F32)] * 3,
        compiler_params=_params(("arbitrary",)),
    )(w, g, m, v, *extra)


def _position():
    return lax.axis_index("x"), lax.axis_index("y"), lax.axis_index("c")


def _other_chips(x, y):
    return [(1 - x, y), (x, 1 - y), (1 - x, 1 - y)]


def _rows(ref, start, size):
    span = pl.ds(pl.multiple_of(start, 16), size)
    return ref.at[span, :] if len(ref.shape) == 2 else ref.at[:, span, :]


def _first_gather_call(shards):
    n = len(shards)

    def body(*refs):
        srcs, outs = refs[:n], refs[n:2 * n]
        send_sems, recv_sems, local_sems = refs[2 * n:]
        x, y, c = _position()
        sibling = (x, y, 1 - c)
        chips = _other_chips(x, y)
        local = [pltpu.make_async_copy(srcs[a], outs[a].at[2 * x + y], local_sems.at[a]) for a in range(n)]
        for cp in local:
            cp.start()

        def copy(a, k, block, to, src=None):
            px, py, pc = block
            half = shards[a].shape[0] // 2
            dst = _rows(outs[a].at[2 * px + py], pc * half, half)
            return pltpu.make_async_remote_copy(
                src_ref=dst if src is None else src, dst_ref=dst, send_sem=send_sems.at[6 * a + k],
                recv_sem=recv_sems.at[6 * a + k], device_id=to, device_id_type=MESH_ID)

        first, passed = [], []
        for a in range(n):
            half = shards[a].shape[0] // 2
            my_half = _rows(srcs[a], c * half, half)
            first += [copy(a, j, (x, y, c), (*chip, c), src=my_half) for j, chip in enumerate(chips)]
        for cp in first:
            cp.start()
        for a in range(n):
            for j, chip in enumerate(chips):
                copy(a, j, (*chip, c), (x, y, c)).wait_recv()
                passed.append(copy(a, 3 + j, (*chip, c), sibling))
                passed[-1].start()
        for a in range(n):
            for j, chip in enumerate(chips):
                copy(a, 3 + j, (*chip, 1 - c), (x, y, c)).wait_recv()
        for cp in first + passed:
            cp.wait_send()
        for cp in local:
            cp.wait()

    return pl.pallas_call(
        body, name="first_gather",
        in_specs=[_any_spec()] * n, out_specs=[_any_spec()] * n,
        out_shape=[jax.ShapeDtypeStruct((N_CHIPS,) + s.shape, s.dtype) for s in shards],
        scratch_shapes=[pltpu.SemaphoreType.DMA((6 * n,)), pltpu.SemaphoreType.DMA((6 * n,)),
                        pltpu.SemaphoreType.DMA((n,))],
    )(*shards)


def _split_start(name, arrays, n_copies, plan):
    n = len(arrays)

    def body(*refs):
        ins, send_sems, recv_sems, token = refs[:n], refs[n], refs[n + 1], refs[-1]
        for k, (src, dst, to, _) in enumerate(plan(ins)):
            pltpu.make_async_remote_copy(src_ref=src, dst_ref=dst, send_sem=send_sems.at[k],
                                         recv_sem=recv_sems.at[k], device_id=to, device_id_type=MESH_ID).start()
        token[...] = jnp.zeros_like(token)

    hbm = pl.BlockSpec(memory_space=pltpu.HBM)
    sem = pl.BlockSpec(memory_space=pltpu.SEMAPHORE)
    out = pl.pallas_call(
        body, name=name,
        out_shape=(pltpu.SemaphoreType.DMA((n_copies,)), pltpu.SemaphoreType.DMA((n_copies,)))
        + tuple(pltpu.HBM(a.shape, a.dtype) for a in arrays) + (jax.ShapeDtypeStruct((8, 128), F32),),
        in_specs=[hbm] * n, out_specs=(sem, sem) + (hbm,) * n + (_vmem_spec(),),
        input_output_aliases={i: 2 + i for i in range(n)},
        compiler_params=pltpu.CompilerParams(has_side_effects=pltpu.SideEffectType.DATAFLOW_SIDE_EFFECTING),
    )(*[pltpu.with_memory_space_constraint(a, pltpu.HBM) for a in arrays])
    return (out[0], out[1], tuple(out[2:2 + n])), out[-1]


def _split_wait(name, handle, n_copies, plan, after):
    send_sems, recv_sems, arrays = handle
    n = len(arrays)

    def body(*refs):
        ins, s_sems, r_sems = refs[:n], refs[n], refs[n + 1]
        for k, (src, dst, to, landed) in enumerate(plan(ins)):
            cp = pltpu.make_async_remote_copy(src_ref=src, dst_ref=landed, send_sem=s_sems.at[k],
                                              recv_sem=r_sems.at[k], device_id=to, device_id_type=MESH_ID)
            cp.wait_send()
            cp.wait_recv()

    hbm = pl.BlockSpec(memory_space=pltpu.HBM)
    sem = pl.BlockSpec(memory_space=pltpu.SEMAPHORE)
    out = pl.pallas_call(
        body, name=name,
        out_shape=tuple(pltpu.HBM(a.shape, a.dtype) for a in arrays),
        in_specs=[hbm] * n + [sem, sem, _any_spec()], out_specs=(hbm,) * n,
        input_output_aliases={i: i for i in range(n)},
        compiler_params=pltpu.CompilerParams(has_side_effects=pltpu.SideEffectType.DATAFLOW_SIDE_EFFECTING),
    )(*arrays, send_sems, recv_sems, after)
    return tuple(out)


def _gather_plans(shard_rows):
    n = len(shard_rows)

    def stage_one(refs):
        x, y, c = _position()
        copies = []
        for a, rows in enumerate(shard_rows):
            half = rows // 2
            for px, py in _other_chips(x, y):
                copies.append((_rows(refs[a], c * half, half), _rows(refs[n + a].at[2 * x + y], c * half, half),
                               (px, py, c), _rows(refs[n + a].at[2 * px + py], c * half, half)))
        return copies

    def stage_two(refs):
        x, y, c = _position()
        copies = []
        for a, rows in enumerate(shard_rows):
            half = rows // 2
            for px, py in _other_chips(x, y):
                piece = _rows(refs[n + a].at[2 * px + py], c * half, half)
                copies.append((piece, piece, (x, y, 1 - c), _rows(refs[n + a].at[2 * px + py], (1 - c) * half, half)))
        return copies

    return stage_one, stage_two


def _pair_swap_plan(n):
    def plan(refs):
        x, y, c = _position()
        copies = []
        for a in range(n):
            half = refs[a].shape[1] // 2
            copies.append((_rows(refs[a], (1 - c) * half, half), refs[n + a], (x, y, 1 - c), refs[n + a]))
        return copies

    return plan


def _chip_swap_plan(n):
    def plan(refs):
        x, y, c = _position()
        copies = []
        for a in range(n):
            for j, (px, py) in enumerate(_other_chips(x, y)):
                copies.append((refs[a].at[2 * px + py], refs[n + a].at[j], (px, py, c), refs[n + a].at[j]))
        return copies

    return plan


def _pair_join_plan(n):
    def plan(refs):
        x, y, c = _position()
        copies = []
        for a in range(n):
            half = refs[a].shape[0] // 2
            mine = _rows(refs[a], c * half, half)
            copies.append((mine, mine, (x, y, 1 - c), _rows(refs[a], (1 - c) * half, half)))
        return copies

    return plan


def _pair_add_call(g, got, pos, name):
    half, cols = got.shape[1], got.shape[2]
    tr = min(256, half)
    nblk = half // tr

    def body(pos_r, g_r, got_r, o_r):
        o_r[...] = (g_r[...] + got_r[...]).astype(o_r.dtype)

    return pl.pallas_call(
        body, name=name,
        grid_spec=pltpu.PrefetchScalarGridSpec(
            num_scalar_prefetch=1, grid=(N_CHIPS, nblk),
            in_specs=[pl.BlockSpec((None, tr, cols), lambda j, i, p: (j, p[1] * nblk + i, 0)),
                      pl.BlockSpec((None, tr, cols), lambda j, i, p: (j, i, 0))],
            out_specs=pl.BlockSpec((None, tr, cols), lambda j, i, p: (j, i, 0))),
        out_shape=jax.ShapeDtypeStruct((N_CHIPS, half, cols), COMM_DTYPE),
        compiler_params=_params(("arbitrary", "arbitrary")),
    )(pos, g, got)


def _chip_add_call(hsum, got, pos, name):
    half, cols = hsum.shape[1], hsum.shape[2]
    tr = min(256, half)
    nblk = half // tr

    def body(pos_r, own_r, got_r, o_r):
        acc = own_r[...].astype(F32)
        for j in range(3):
            acc = acc + got_r[j].astype(F32)
        o_r[...] = acc

    return pl.pallas_call(
        body, name=name,
        grid_spec=pltpu.PrefetchScalarGridSpec(
            num_scalar_prefetch=1, grid=(nblk,),
            in_specs=[pl.BlockSpec((None, tr, cols), lambda i, p: (p[0], i, 0)),
                      pl.BlockSpec((3, tr, cols), lambda i, p: (0, i, 0))],
            out_specs=pl.BlockSpec((tr, cols), lambda i, p: (p[1] * nblk + i, 0))),
        out_shape=jax.ShapeDtypeStruct((2 * half, cols), F32),
        compiler_params=_params(("arbitrary",)),
    )(pos, hsum, got)


SMALL_NAMES = ("norm_mix_pre", "norm_mix_post", "norm_mlp_pre", "norm_mlp_post", "b_gate_fwd", "b_gate_bwd",
               "gla_norm", "swa_sink", "rel_bias")


def _small_update_call(grads, gate_grads, params, dep=None):
    n_dev = 8
    n_small = len(SMALL_NAMES)
    wmv = [t for p in params for t in p]
    shapes = [p[0].shape for p in params]

    def body(*refs):
        g_refs = refs[:n_small + 2]
        wmv_refs = refs[n_small + 2:n_small + 2 + 3 * n_small]
        n_in = n_small + 2 + 3 * n_small
        out_refs = refs[n_in:n_in + 4 * n_small + 2]
        pack_a, pack_b, all_a, all_b, send_sems, recv_sems = refs[n_in + 4 * n_small + 2:]
        x, y, c = _position()
        me = 4 * x + 2 * y + c
        pack_a[...] = jnp.zeros_like(pack_a)
        pack_b[...] = jnp.zeros_like(pack_b)
        for i in range(4):
            pack_a[i:i + 1, :] = g_refs[i][...]
        pack_a[4:5, 0:256] = g_refs[4][...]
        pack_a[5:6, 0:256] = g_refs[5][...]
        pack_a[6:7, 0:128] = g_refs[6][...]
        pack_a[7:8, 0:128] = g_refs[7][...]
        pack_b[0:32, 0:128] = g_refs[8][...]
        pack_b[32:48, :] = g_refs[9][...]
        pack_b[48:64, :] = g_refs[10][...]
        all_a[me] = pack_a[...]
        all_b[me] = pack_b[...]
        copies = []
        for k in range(1, n_dev):
            fx, fy, fc = (k >> 2) & 1, (k >> 1) & 1, k & 1
            to = (1 - x if fx else x, 1 - y if fy else y, 1 - c if fc else c)
            for t, (pack, dst) in enumerate(((pack_a, all_a), (pack_b, all_b))):
                copies.append(pltpu.make_async_remote_copy(
                    src_ref=pack, dst_ref=dst.at[me], send_sem=send_sems.at[2 * (k - 1) + t],
                    recv_sem=recv_sems.at[2 * (k - 1) + t], device_id=to, device_id_type=MESH_ID))
        for cp in copies:
            cp.start()
        for cp in copies:
            cp.wait()
        sum_a, sum_b = all_a[0], all_b[0]
        for d in range(1, n_dev):
            sum_a = sum_a + all_a[d]
            sum_b = sum_b + all_b[d]
        gsum = [sum_a[0:1], sum_a[1:2], sum_a[2:3], sum_a[3:4], sum_a[4:5, 0:256], sum_a[5:6, 0:256],
                sum_a[6:7, 0:128], sum_a[7:8, 0:SWA_Q_HEADS], sum_b[0:32, 0:SWA_Q_HEADS]]
        for i in range(n_small):
            w_r, m_r, v_r = wmv_refs[3 * i:3 * i + 3]
            delta, new_m, new_v = _adamw_math(w_r[...], gsum[i], m_r[...], v_r[...])
            out_refs[4 * i][...] = gsum[i]
            out_refs[4 * i + 1][...] = delta
            out_refs[4 * i + 2][...] = new_m
            out_refs[4 * i + 3][...] = new_v
        out_refs[4 * n_small][...] = sum_b[32:48]
        out_refs[4 * n_small + 1][...] = sum_b[48:64]

    n_in = n_small + 2 + 3 * n_small
    body, extra, extra_specs = _after(body, n_in, dep)
    out_shape = [jax.ShapeDtypeStruct(s, F32) for s in shapes for _ in range(4)]
    out_shape += [jax.ShapeDtypeStruct((GLA_GATE_RANK, 256), F32)] * 2
    out = pl.pallas_call(
        body, name="small_update",
        in_specs=[_vmem_spec()] * n_in + extra_specs, out_specs=[_vmem_spec()] * len(out_shape),
        out_shape=out_shape,
        scratch_shapes=[pltpu.VMEM((8, D_MODEL), F32), pltpu.VMEM((64, 256), F32),
                        pltpu.VMEM((n_dev, 8, D_MODEL), F32), pltpu.VMEM((n_dev, 64, 256), F32),
                        pltpu.SemaphoreType.DMA((2 * (n_dev - 1),)), pltpu.SemaphoreType.DMA((2 * (n_dev - 1),))],
    )(*grads, *gate_grads, *wmv, *extra)
    per_name = [tuple(out[4 * i:4 * i + 4]) for i in range(n_small)]
    return per_name, out[4 * n_small], out[4 * n_small + 1]


def _pad_heads(t, n_heads, axis=-1):
    axis = axis % t.ndim
    shape = t.shape
    t = t.reshape(shape[:axis] + (n_heads, 64) + shape[axis + 1:])
    pad = [(0, 0)] * t.ndim
    pad[axis + 1] = (0, HEAD_PAD - 64)
    return jnp.pad(t, pad).reshape(shape[:axis] + (n_heads * HEAD_PAD,) + shape[axis + 1:])


def _unpad_heads(t, n_heads, axis=-1):
    axis = axis % t.ndim
    shape = t.shape
    t = t.reshape(shape[:axis] + (n_heads, HEAD_PAD) + shape[axis + 1:])
    t = lax.slice_in_dim(t, 0, 64, axis=axis + 1)
    return t.reshape(shape[:axis] + (n_heads * 64,) + shape[axis + 1:])


def _pad_w_in(w):
    return jnp.concatenate([
        _pad_heads(w[:, 0:256], 4), _pad_heads(w[:, 256:512], 4), w[:, 512:1024], w[:, 1024:1536],
        _pad_heads(w[:, 1568:2080], 8), _pad_heads(w[:, 2080:2208], 2), _pad_heads(w[:, 2208:2336], 2),
        jnp.pad(w[:, 1536:1568], ((0, 0), (0, 96)))], axis=1)


def _unpad_w_in(g):
    return jnp.concatenate([
        _unpad_heads(g[:, 0:512], 4), _unpad_heads(g[:, 512:1024], 4), g[:, 1024:1536], g[:, 1536:2048],
        g[:, 3584:3616], _unpad_heads(g[:, 2048:3072], 8), _unpad_heads(g[:, 3072:3328], 2),
        _unpad_heads(g[:, 3328:3584], 2)], axis=1)


def _pad_w_out(w):
    return jnp.concatenate([w[:512], _pad_heads(w[512:], 8, axis=0)], axis=0)


def _unpad_w_out(g):
    return jnp.concatenate([g[:512], _unpad_heads(g[512:], 8, axis=0)], axis=0)


def _pad_gate(w, first_row):
    return jnp.pad(_pad_heads(w, 4), ((first_row, 128 - GLA_GATE_RANK - first_row), (0, 0)))


def _own_slot(shard, chip):
    zone = lax.empty((N_CHIPS,) + shard.shape, shard.dtype)
    return lax.dynamic_update_slice(zone, shard[None], (chip,) + (0,) * shard.ndim)


def _reduce_to_owners(grads, pos, tag, overlap):
    n = len(grads)
    lands = [lax.empty((N_CHIPS, g.shape[1] // 2, g.shape[2]), F32) for g in grads]
    handle, token = _split_start(tag + "_pair_start", list(grads) + lands, n, _pair_swap_plan(n))
    got = _split_wait(tag + "_pair_wait", handle, n, _pair_swap_plan(n), overlap[0](token))
    sums = [_pair_add_call(got[a], got[n + a], pos, f"{tag}_pair_add{a}") for a in range(n)]
    lands = [lax.empty((3,) + s.shape[1:], s.dtype) for s in sums]
    handle, token = _split_start(tag + "_chip_start", sums + lands, 3 * n, _chip_swap_plan(n))
    got = _split_wait(tag + "_chip_wait", handle, 3 * n, _chip_swap_plan(n), overlap[1](token))
    halves = [_chip_add_call(got[a], got[n + a], pos, f"{tag}_chip_add{a}") for a in range(n)]
    handle, token = _split_start(tag + "_join_start", halves, n, _pair_join_plan(n))
    return _split_wait(tag + "_join_wait", handle, n, _pair_join_plan(n), overlap[2](token))


def kernel(x, norm_mix_pre, w_in, w_gate_up_fwd, b_gate_fwd, w_gate_up_bwd, b_gate_bwd, gla_norm, swa_sink, rel_bias, w_out, norm_mix_post, norm_mlp_pre, w_up, w_down, norm_mlp_post, loss_target, m_norm_mix_pre, m_w_in, m_w_gate_up_fwd, m_b_gate_fwd, m_w_gate_up_bwd, m_b_gate_bwd, m_gla_norm, m_swa_sink, m_rel_bias, m_w_out, m_norm_mix_post, m_norm_mlp_pre, m_w_up, m_w_down, m_norm_mlp_post, v_norm_mix_pre, v_w_in, v_w_gate_up_fwd, v_b_gate_fwd, v_w_gate_up_bwd, v_b_gate_bwd, v_gla_norm, v_swa_sink, v_rel_bias, v_w_out, v_norm_mix_post, v_norm_mlp_pre, v_w_up, v_w_down, v_norm_mlp_post):
    given = dict(locals())
    cx, cy, cc = _position()
    chip = (2 * cx + cy).astype(jnp.int32)
    pos = jnp.stack([chip, cc.astype(jnp.int32)])
    seq, tgt = x[0], loss_target[0]
    L = seq.shape[0]

    gates = jnp.concatenate([w_gate_up_fwd[0], w_gate_up_bwd[0]], axis=0).astype(COMM_DTYPE)
    all_in, all_gates = _first_gather_call([w_in[0].astype(COMM_DTYPE), gates])
    rest = [w_out[0].astype(COMM_DTYPE), jnp.stack([w_up[0], w_down[0]]).astype(COMM_DTYPE)]
    stage_one, stage_two = _gather_plans([R_OUT, R_UP])
    handle, token = _split_start("gather_chip_start", rest + [_own_slot(s, chip) for s in rest], 6, stage_one)

    w_in_p = _mx(_pad_w_in(jnp.concatenate([all_in[j] for j in range(N_CHIPS)], axis=1)))
    gates_full = jnp.concatenate([all_gates[j] for j in range(N_CHIPS)], axis=1)
    wgf_p = _mx(_pad_gate(gates_full[:GLA_GATE_RANK], 0))
    wgb_p = _mx(_pad_gate(gates_full[GLA_GATE_RANK:], GLA_GATE_RANK))
    bf_p, bb_p = _pad_heads(b_gate_fwd, 4), _pad_heads(b_gate_bwd, 4)
    buckets = jnp.asarray(_band_buckets())
    bias = _bias_call(rel_bias, buckets)
    sink1 = swa_sink.reshape(SWA_Q_HEADS)

    qa, ka, va, ga, qs, ks, vs, za = _proj_call(seq, norm_mix_pre, w_in_p, dep=token)
    halo = ((SWA_BLOCK, SWA_BLOCK), (0, 0))
    ks_p, vs_p = jnp.pad(ks, halo), jnp.pad(vs, halo)
    o_f, o_b, s_f, s_b = _gla_fwd_call(qa, ka, va, za, wgf_p, bf_p, wgb_p, bb_p)
    arrays = _split_wait("gather_chip_wait", handle, 6, stage_one, o_f)
    handle, token = _split_start("gather_pair_start", list(arrays), 6, stage_two)
    o_s = _swa_fwd_call(qs, ks_p, vs_p, bias, sink1, dep=token)
    arrays = _split_wait("gather_pair_wait", handle, 6, stage_two, o_s)
    w_out_p = _mx(_pad_w_out(arrays[2].reshape(N_CHIPS * R_OUT, D_MODEL)))
    w_ud = _mx(arrays[3])
    cat, mix, h1, n2 = _mix_call(o_f, o_b, ga, o_s, seq, gla_norm, w_out_p, norm_mix_post, norm_mlp_pre)
    a, rz, dh2, dff, loss, d_post2 = _mlp_fwd_call(n2, h1, tgt, w_ud, norm_mlp_post)

    dz, dn2 = _mlp_bwd_call(dff, rz, w_ud)
    dw_down, dw_up4 = _mlp_wgrad_call(a, dff, n2, dz)
    dh1, do, dga, dos, dw_out_p, d_pre2, d_post, d_gn = _mix_bwd_call(
        dn2, dh2, h1, mix, cat, o_f, o_b, ga, gla_norm, norm_mix_post, norm_mlp_pre, w_out_p)
    done = {}

    def gla_backward(tok):
        done["gla"] = _gla_bwd_call(qa, ka, va, za, do, s_f, s_b, wgf_p, bf_p, wgb_p, bb_p, dep=tok)
        return done["gla"][0]

    def swa_backward(tok):
        done["swa"] = _swa_bwd_call(qs, ks_p, vs_p, bias, sink1, dos, dep=tok)
        return done["swa"][0]

    def in_backward(tok):
        dqf, dkf, dvf, dzf, _, _, dqb, dkb, dvb, dzb, _, _ = done["gla"]
        dqs, dks_p, dvs_p, _, _ = done["swa"]
        dks = dks_p[SWA_BLOCK:SWA_BLOCK + L]
        dvs = dvs_p[SWA_BLOCK:SWA_BLOCK + L]
        done["in"] = _in_bwd_call(
            seq, dh1, norm_mix_pre, w_in_p,
            pairs=[(C_QA, (dqf, dqb)), (C_KA, (dkf, dkb)), (C_VA, (dvf, dvb)), (C_ZA, (dzf, dzb))],
            singles=[(C_GA, dga), (C_QS, dqs), (C_KS, dks), (C_VS, dvs)], dep=tok)
        return done["in"][0]

    g_up, g_down, g_out = _reduce_to_owners(
        [dw_up4, dw_down.reshape(N_CHIPS, R_DOWN, D_MODEL), _unpad_w_out(dw_out_p).reshape(N_CHIPS, R_OUT, D_MODEL)],
        pos, "mlp", [gla_backward, swa_backward, in_backward])
    dx, dw_in_p, d_pre = done["in"]
    dwf, dbf, dwb, dbb = done["gla"][4], done["gla"][5], done["gla"][10], done["gla"][11]
    drel, dsink = _relbias_call(done["swa"][3], done["swa"][4], buckets)

    small_grads = [d_pre, d_post, d_pre2, d_post2, _unpad_heads(dbf, 4), _unpad_heads(dbb, 4), d_gn, dsink, drel]
    gate_grads = [_unpad_heads(dwf[:GLA_GATE_RANK], 4), _unpad_heads(dwb[GLA_GATE_RANK:2 * GLA_GATE_RANK], 4)]
    small_params = [(given[n], given["m_" + n], given["v_" + n]) for n in SMALL_NAMES]
    upd = {}

    def update_up(tok):
        upd["w_up"] = (g_up,) + tuple(_adamw_call(w_up[0], g_up, m_w_up[0], v_w_up[0], "adamw_w_up", dep=tok))
        return upd["w_up"][1]

    def update_small(tok):
        per_name, gf_sum, gb_sum = _small_update_call(small_grads, gate_grads, small_params, dep=tok)
        upd.update(dict(zip(SMALL_NAMES, per_name)))
        for name, total in (("w_gate_up_fwd", gf_sum), ("w_gate_up_bwd", gb_sum)):
            g = lax.dynamic_slice(total, (0, chip * 64), (GLA_GATE_RANK, 64))
            upd[name] = (g,) + tuple(_adamw_call(given[name][0], g, given["m_" + name][0], given["v_" + name][0],
                                                 "adamw_" + name))
        upd["w_down"] = (g_down,) + tuple(
            _adamw_call(w_down[0], g_down, m_w_down[0], v_w_down[0], "adamw_w_down", dep=gf_sum))
        return upd["w_down"][1]

    def update_out(tok):
        upd["w_out"] = (g_out,) + tuple(_adamw_call(w_out[0], g_out, m_w_out[0], v_w_out[0], "adamw_w_out", dep=tok))
        return upd["w_out"][1]

    dw_in = _unpad_w_in(dw_in_p)
    dw_in4 = jnp.stack([dw_in[:, R_IN * j:R_IN * (j + 1)] for j in range(N_CHIPS)], axis=0)
    (g_in,) = _reduce_to_owners([dw_in4], pos, "in", [update_up, update_small, update_out])
    upd["w_in"] = (g_in,) + tuple(_adamw_call(w_in[0], g_in, m_w_in[0], v_w_in[0], "adamw_w_in"))

    big = ("w_in", "w_gate_up_fwd", "w_gate_up_bwd", "w_out", "w_up", "w_down")
    names = ["norm_mix_pre", "w_in", "w_gate_up_fwd", "b_gate_fwd", "w_gate_up_bwd", "b_gate_bwd", "gla_norm",
             "swa_sink", "rel_bias", "w_out", "norm_mix_post", "norm_mlp_pre", "w_up", "w_down", "norm_mlp_post"]
    outs = [lax.psum(loss[0, 0], MESH_AXES), dx[None]]
    for kind in range(4):
        outs += [upd[n][kind][None] if n in big else upd[n][kind] for n in names]
    return tuple(outs)
```

```python
import collections
import math

import numpy as np
import jax
import jax.numpy as jnp
from jax import lax
from jax.experimental import pallas as pl
from jax.experimental.pallas import tpu as pltpu

F32 = jnp.float32
MXU_DTYPE = jnp.bfloat16
COMM_DTYPE = jnp.bfloat16

D_MODEL = 1024
D_FF = 4096
N_CHIPS = 4
GLA_HEADS = 4
GLA_CHUNK = 64
GLA_GATE_RANK = 16
GLA_GATE_NORM = 16.0
SWA_Q_HEADS = 8
SWA_KV_HEADS = 2
SWA_BLOCK = 128
REL_BUCKETS = 32
REL_MAX_DIST = 128
NORM_EPS = 1e-6
HEAD_PAD = 128

ADAM_LR = 0.001
ADAM_B1 = 0.9
ADAM_B2 = 0.999
ADAM_EPS = 1e-08
ADAM_WD = 0.01
ADAM_STEP = 10

C_QA, C_KA, C_VA, C_GA = (0, 512), (512, 512), (1024, 512), (1536, 512)
C_QS, C_KS, C_VS, C_ZA = (2048, 1024), (3072, 256), (3328, 256), (3584, 128)
IN_PAD = 3712
OUT_PAD = 1536

R_IN, R_OUT, R_UP, R_DOWN = 584, 256, 1024, 1024
PACK_USED = R_IN + R_OUT + R_UP + R_DOWN + 2
PACK_ROWS = 2944
PACK_HALF = PACK_ROWS // 2
ADD_ROWS = 368

VMEM_BIG = 56 * 1024 * 1024
MESH_AXES = ("x", "y", "c")
MESH_ID = pl.DeviceIdType.MESH


def _mx(a):
    return a.astype(MXU_DTYPE)


def _dot(a, b):
    return jnp.dot(a, b, preferred_element_type=F32)


def _dot_nt(a, b):
    return lax.dot_general(a, b, (((1,), (1,)), ((), ())), preferred_element_type=F32)


def _dot_tn(a, b):
    return lax.dot_general(a, b, (((0,), (0,)), ((), ())), preferred_element_type=F32)


def _dot_exact(a, b):
    return jnp.dot(a, b, precision=lax.Precision.HIGHEST, preferred_element_type=F32)


def _dot_tn_exact(a, b):
    return lax.dot_general(a, b, (((0,), (0,)), ((), ())), precision=lax.Precision.HIGHEST,
                           preferred_element_type=F32)


def _rms_r(x):
    return lax.rsqrt(jnp.mean(x * x, axis=-1, keepdims=True) + NORM_EPS)


def _rms_bwd(x, r, g, dy):
    xh = x * r
    gdy = dy * g
    dx = r * (gdy - xh * jnp.mean(gdy * xh, axis=-1, keepdims=True))
    return dx, jnp.sum(dy * xh, axis=0, keepdims=True)


def _params(sem=None, vmem=None):
    kw = {}
    if sem is not None:
        kw["dimension_semantics"] = sem
    if vmem is not None:
        kw["vmem_limit_bytes"] = vmem
    return pltpu.CompilerParams(**kw)


def _vmem_spec():
    return pl.BlockSpec(memory_space=pltpu.VMEM)


def _row_spec(tm, width):
    return pl.BlockSpec((tm, width), lambda i: (i, 0))


def _full_spec(shape):
    return pl.BlockSpec(shape, lambda i: (0,) * len(shape))


def _any_spec():
    return pl.BlockSpec(memory_space=pl.ANY)


def _after(body, n_in, dep):
    if dep is None:
        return body, [], []
    return (lambda *refs: body(*refs[:n_in], *refs[n_in + 1:])), [dep], [_any_spec()]


def _proj_call(x, g_pre, w_in_p, dep=None):
    L = x.shape[0]
    tm = min(256, L)
    groups = [(C_QA, F32), (C_KA, F32), (C_VA, MXU_DTYPE), (C_GA, F32),
              (C_QS, MXU_DTYPE), (C_KS, MXU_DTYPE), (C_VS, MXU_DTYPE), (C_ZA, F32)]

    def body(x_ref, g_ref, w_ref, *outs):
        xv = x_ref[...]
        u = _mx(xv * _rms_r(xv) * g_ref[...])
        for ref, ((off, width), _) in zip(outs, groups):
            ref[...] = _dot(u, w_ref[:, off:off + width]).astype(ref.dtype)

    body, extra, extra_specs = _after(body, 3, dep)
    return pl.pallas_call(
        body, name="proj_fwd", grid=(L // tm,),
        in_specs=[_row_spec(tm, D_MODEL), _full_spec((1, D_MODEL)), _vmem_spec()] + extra_specs,
        out_specs=[_row_spec(tm, w) for (_, w), _ in groups],
        out_shape=[jax.ShapeDtypeStruct((L, w), dt) for (_, w), dt in groups],
        compiler_params=_params(("arbitrary",), VMEM_BIG),
    )(x, g_pre, w_in_p, *extra)


def _tri_masks():
    row = lax.broadcasted_iota(jnp.int32, (GLA_CHUNK, GLA_CHUNK), 0)
    col = lax.broadcasted_iota(jnp.int32, (GLA_CHUNK, GLA_CHUNK), 1)
    return row >= col, row <= col


def _chunk_sums(tri_m, x):
    hi = _mx(x)
    rest = x - hi.astype(F32)
    mid = _mx(rest)
    lo = _mx(rest - mid.astype(F32))
    return _dot(tri_m, hi) + _dot(tri_m, mid) + _dot(tri_m, lo)


def _gla_block_pre(q_r, k_r, z_r, w_r, b_r, rev, nc, qd_s, ki_s, ks_s, dec_s, keep=None):
    tri_f, tri_b = _tri_masks()
    tri_m = _mx((tri_b if rev else tri_f).astype(F32))
    g = _dot(_mx(z_r[...]), w_r[...]) + b_r[...]
    la = (jnp.minimum(g, 0.0) - jnp.log(1.0 + jnp.exp(-jnp.abs(g)))) / GLA_GATE_NORM
    sums, lasts = [], []
    for c in range(nc):
        b_c = _chunk_sums(tri_m, la[GLA_CHUNK * c:GLA_CHUNK * (c + 1)])
        blast = b_c[0:1] if rev else b_c[GLA_CHUNK - 1:GLA_CHUNK]
        dec_s[c] = jnp.exp(blast)
        sums.append(b_c)
        lasts.append(jnp.broadcast_to(blast, b_c.shape))
    b = jnp.concatenate(sums, axis=0)
    eb = jnp.exp(b)
    enb = jnp.exp(-b)
    elb = jnp.exp(jnp.concatenate(lasts, axis=0) - b)
    k = k_r[...]
    qd_s[...] = (q_r[...] * 0.125 * eb).astype(qd_s.dtype)
    ki_s[...] = (k * enb).astype(ki_s.dtype)
    ks_s[...] = (k * elb).astype(ks_s.dtype)
    if keep is not None:
        for ref, val in zip(keep, (g, eb, enb, elb)):
            ref[...] = val


def _gla_fwd_call(qa, ka, va, za, wgf, bgf, wgb, bgb):
    L = qa.shape[0]
    br = min(512, L)
    nb, nc, n_chunks = L // br, br // GLA_CHUNK, L // GLA_CHUNK
    hw = GLA_HEADS * HEAD_PAD

    def body(qaf, kaf, vaf, zaf, qab, kab, vab, zab, wgf_r, bgf_r, wgb_r, bgb_r,
             of_r, ob_r, sf_r, sb_r, st_f, st_b, pre_f, pre_b):
        @pl.when(pl.program_id(0) == 0)
        def _():
            st_f[...] = jnp.zeros_like(st_f)
            st_b[...] = jnp.zeros_like(st_b)

        _gla_block_pre(qaf, kaf, zaf, wgf_r, bgf_r, False, nc, *pre_f)
        _gla_block_pre(qab, kab, zab, wgb_r, bgb_r, True, nc, *pre_b)
        tri_f, tri_b = _tri_masks()

        def one(tri, pre, v_r, o_r, s_r, st, ci):
            qd_s, ki_s, ks_s, dec_s = pre
            rows = pl.ds(pl.multiple_of(ci * GLA_CHUNK, GLA_CHUNK), GLA_CHUNK)
            dec = dec_s[ci]
            for h in range(GLA_HEADS):
                sl = slice(HEAD_PAD * h, HEAD_PAD * (h + 1))
                qd = qd_s[rows, sl]
                a = jnp.where(tri, _dot_nt(qd, ki_s[rows, sl]), 0.0)
                v = v_r[rows, sl]
                s_t = st[h]
                s_r[ci, h] = s_t
                o_r[rows, sl] = _dot(_mx(a), v) + _dot_nt(qd, _mx(s_t))
                st[h] = s_t * dec[:, sl] + _dot_tn(v, ks_s[rows, sl])

        def loop(t, carry):
            one(tri_f, pre_f, vaf, of_r, sf_r, st_f, t)
            one(tri_b, pre_b, vab, ob_r, sb_r, st_b, nc - 1 - t)
            return carry

        lax.fori_loop(0, nc, loop, 0)

    fwd = lambda i: (i, 0)
    bwd = lambda i: (nb - 1 - i, 0)
    ins = lambda m: [pl.BlockSpec((br, hw), m), pl.BlockSpec((br, hw), m),
                     pl.BlockSpec((br, hw), m), pl.BlockSpec((br, 128), m)]
    wspecs = [_full_spec((128, hw)), _full_spec((1, hw))] * 2
    s_shape = (nc, GLA_HEADS, HEAD_PAD, HEAD_PAD)
    pre_scratch = [pltpu.VMEM((br, hw), MXU_DTYPE)] * 3 + [pltpu.VMEM((nc, 1, hw), F32)]
    return pl.pallas_call(
        body, name="gla_fwd", grid=(nb,),
        in_specs=ins(fwd) + ins(bwd) + wspecs,
        out_specs=[pl.BlockSpec((br, hw), fwd), pl.BlockSpec((br, hw), bwd),
                   pl.BlockSpec(s_shape, lambda i: (i, 0, 0, 0)),
                   pl.BlockSpec(s_shape, lambda i: (nb - 1 - i, 0, 0, 0))],
        out_shape=[jax.ShapeDtypeStruct((L, hw), F32), jax.ShapeDtypeStruct((L, hw), F32),
                   jax.ShapeDtypeStruct((n_chunks,) + s_shape[1:], F32),
                   jax.ShapeDtypeStruct((n_chunks,) + s_shape[1:], F32)],
        scratch_shapes=[pltpu.VMEM(s_shape[1:], F32), pltpu.VMEM(s_shape[1:], F32), pre_scratch, pre_scratch],
        compiler_params=_params(("arbitrary",), VMEM_BIG),
    )(qa, ka, va, za, qa, ka, va, za, wgf, bgf, wgb, bgb)


def _gla_bwd_call(qa, ka, va, za, do, sf, sb, wgf, bgf, wgb, bgb, dep=None):
    L = qa.shape[0]
    br = min(256, L)
    nb, nc = L // br, br // GLA_CHUNK
    hw = GLA_HEADS * HEAD_PAD

    def body(qaf, kaf, vaf, zaf, dof, sf_r, qab, kab, vab, zab, dob, sb_r, wgf_r, bgf_r, wgb_r, bgb_r,
             dqf, dkf, dvf, dzf, dwf, dbf, dqb, dkb, dvb, dzb, dwb, dbb, gt_f, gt_b, pre_f, pre_b):
        @pl.when(pl.program_id(0) == 0)
        def _():
            for ref in (gt_f, gt_b, dwf, dbf, dwb, dbb):
                ref[...] = jnp.zeros_like(ref)

        _gla_block_pre(qaf, kaf, zaf, wgf_r, bgf_r, False, nc, *pre_f[:4], keep=pre_f[4:8])
        _gla_block_pre(qab, kab, zab, wgb_r, bgb_r, True, nc, *pre_b[:4], keep=pre_b[4:8])
        tri_f, tri_b = _tri_masks()
        row_w = lax.broadcasted_iota(jnp.int32, (GLA_CHUNK, HEAD_PAD), 0)

        def one(rev, pre, q_r, k_r, v_r, do_r, s_r, dq_r, dk_r, dv_r, gt, ci):
            qd_s, ki_s, ks_s, dec_s, _, eb_s, enb_s, elb_s, db_s = pre
            tri = tri_b if rev else tri_f
            last_row = 0 if rev else GLA_CHUNK - 1
            rows = pl.ds(pl.multiple_of(ci * GLA_CHUNK, GLA_CHUNK), GLA_CHUNK)
            dec = dec_s[ci]
            for h in range(GLA_HEADS):
                sl = slice(HEAD_PAD * h, HEAD_PAD * (h + 1))
                qd, ki, ks = qd_s[rows, sl], ki_s[rows, sl], ks_s[rows, sl]
                a = _mx(jnp.where(tri, _dot_nt(qd, ki), 0.0))
                v = v_r[rows, sl]
                do_h = _mx(do_r[rows, sl])
                s_t = s_r[ci, h]
                g_t = gt[h]
                g_m = _mx(g_t)
                da = _mx(jnp.where(tri, _dot_nt(do_h, v), 0.0))
                dv_r[rows, sl] = _dot_tn(a, do_h) + _dot_nt(ks, g_m)
                dqd = _dot(da, ki) + _dot(do_h, _mx(s_t))
                dki = _dot_tn(da, qd)
                dks = _dot(v, g_m)
                ddec = jnp.sum(g_t * s_t, axis=0, keepdims=True)
                gt[h] = g_t * dec[:, sl] + _dot_tn(do_h, qd)
                dq = dqd * eb_s[rows, sl] * 0.125
                dk_state = dks * elb_s[rows, sl]
                dk = dki * enb_s[rows, sl] + dk_state
                dq_r[rows, sl] = dq
                dk_r[rows, sl] = dk
                k = k_r[rows, sl]
                dblast = jnp.sum(dk_state * k, axis=0, keepdims=True) + dec[:, sl] * ddec
                db_s[rows, sl] = q_r[rows, sl] * dq - k * dk + jnp.where(row_w == last_row, dblast, 0.0)

        def loop(t, carry):
            one(False, pre_f, qaf, kaf, vaf, dof, sf_r, dqf, dkf, dvf, gt_f, nc - 1 - t)
            one(True, pre_b, qab, kab, vab, dob, sb_r, dqb, dkb, dvb, gt_b, t)
            return carry

        lax.fori_loop(0, nc, loop, 0)

        def gate_grads(rev, pre, z_r, w_r, dz_r, dw_r, dbias_r):
            g_s, db_s = pre[4], pre[8]
            back_m = _mx((tri_f if rev else tri_b).astype(F32))
            db = db_s[...]
            dla = jnp.concatenate([_chunk_sums(back_m, db[GLA_CHUNK * c:GLA_CHUNK * (c + 1)]) for c in range(nc)],
                                  axis=0)
            dg = dla * (1.0 / GLA_GATE_NORM) * (1.0 / (1.0 + jnp.exp(g_s[...])))
            dg_m = _mx(dg)
            dz_r[...] = _dot_nt(dg_m, w_r[...])
            dw_r[...] += _dot_tn(_mx(z_r[...]), dg_m)
            dbias_r[...] += jnp.sum(dg, axis=0, keepdims=True)

        gate_grads(False, pre_f, zaf, wgf_r, dzf, dwf, dbf)
        gate_grads(True, pre_b, zab, wgb_r, dzb, dwb, dbb)

    last_first = lambda i: (nb - 1 - i, 0)
    first_last = lambda i: (i, 0)
    s_shape = (nc, GLA_HEADS, HEAD_PAD, HEAD_PAD)

    def ins(m):
        return [pl.BlockSpec((br, hw), m), pl.BlockSpec((br, hw), m), pl.BlockSpec((br, hw), m),
                pl.BlockSpec((br, 128), m), pl.BlockSpec((br, hw), m),
                pl.BlockSpec(s_shape, lambda i: m(i) + (0, 0))]

    def outs(m):
        return [pl.BlockSpec((br, hw), m), pl.BlockSpec((br, hw), m), pl.BlockSpec((br, hw), m),
                pl.BlockSpec((br, 128), m), _full_spec((128, hw)), _full_spec((1, hw))]

    out_shape = [jax.ShapeDtypeStruct((L, hw), F32)] * 3 + [
        jax.ShapeDtypeStruct((L, 128), F32), jax.ShapeDtypeStruct((128, hw), F32),
        jax.ShapeDtypeStruct((1, hw), F32)]
    wspecs = [_full_spec((128, hw)), _full_spec((1, hw))] * 2
    body, extra, extra_specs = _after(body, 16, dep)
    pre_scratch = ([pltpu.VMEM((br, hw), MXU_DTYPE)] * 3 + [pltpu.VMEM((nc, 1, hw), F32)]
                   + [pltpu.VMEM((br, hw), F32)] * 5)
    return pl.pallas_call(
        body, name="gla_bwd", grid=(nb,),
        in_specs=ins(last_first) + ins(first_last) + wspecs + extra_specs,
        out_specs=outs(last_first) + outs(first_last),
        out_shape=out_shape + out_shape,
        scratch_shapes=[pltpu.VMEM(s_shape[1:], F32), pltpu.VMEM(s_shape[1:], F32), pre_scratch, pre_scratch],
        compiler_params=_params(("arbitrary",), VMEM_BIG),
    )(qa, ka, va, za, do, sf, qa, ka, va, za, do, sb, wgf, bgf, wgb, bgb, *extra)


def _t5_buckets(rel):
    nb = REL_BUCKETS // 2
    ret = (rel > 0).astype(np.int32) * nb
    n = np.abs(rel)
    max_exact = nb // 2
    large = max_exact + (np.log(np.maximum(n, 1).astype(np.float32) / max_exact)
                         / math.log(REL_MAX_DIST / max_exact) * (nb - max_exact)).astype(np.int32)
    large = np.minimum(large, nb - 1)
    return ret + np.where(n < max_exact, n, large)


SWA_GROUP = SWA_Q_HEADS // SWA_KV_HEADS
SWA_SPAN = 3 * SWA_BLOCK
SWA_GROUP_LANES = SWA_GROUP * SWA_BLOCK


def _band_buckets():
    s = np.arange(SWA_SPAN)[:, None]
    c = np.arange(SWA_BLOCK)[None, :]
    return _t5_buckets(s - SWA_BLOCK - c).astype(np.int32)


def _swa_valid(n, seq_len):
    s = lax.broadcasted_iota(jnp.int32, (SWA_SPAN, SWA_GROUP_LANES), 0)
    c = lax.broadcasted_iota(jnp.int32, (SWA_SPAN, SWA_GROUP_LANES), 1) & (SWA_BLOCK - 1)
    rel = s - SWA_BLOCK - c
    key_pos = (n - 1) * SWA_BLOCK + s
    return (jnp.abs(rel) <= SWA_BLOCK) & (key_pos >= 0) & (key_pos < seq_len)


def _swa_sink_row(sink_r, kv):
    lane = lax.broadcasted_iota(jnp.int32, (1, SWA_GROUP_LANES), 1)
    row = jnp.full((1, SWA_GROUP_LANES), sink_r[kv * SWA_GROUP], F32)
    for g in range(1, SWA_GROUP):
        row = jnp.where(lane >= g * SWA_BLOCK, sink_r[kv * SWA_GROUP + g], row)
    return row


def _swa_group(ref, kv):
    first = kv * SWA_GROUP
    return jnp.concatenate([ref[:, HEAD_PAD * h:HEAD_PAD * (h + 1)] for h in range(first, first + SWA_GROUP)],
                           axis=0)


def _swa_probs(kk, qg, bias_t, sink_row, valid):
    st = _dot_nt(kk, qg) * 0.125 + bias_t
    st = jnp.where(valid, st, -1e30)
    m = jnp.maximum(jnp.max(st, axis=0, keepdims=True), sink_row)
    p = jnp.exp(st - m)
    e_sink = jnp.exp(sink_row - m)
    inv = 1.0 / (jnp.sum(p, axis=0, keepdims=True) + e_sink)
    return p * inv, e_sink * inv


def _swa_fwd_call(qs, ks, vs, bias, sink, dep=None):
    L = qs.shape[0]

    def body(q_r, k_r, v_r, bias_r, sink_r, o_r):
        n = pl.program_id(0)
        span = pl.ds(pl.multiple_of(n * SWA_BLOCK, SWA_BLOCK), SWA_SPAN)
        valid = _swa_valid(n, L)
        for kv in range(SWA_KV_HEADS):
            ksl = slice(HEAD_PAD * kv, HEAD_PAD * (kv + 1))
            pn, _ = _swa_probs(k_r[span, ksl], _swa_group(q_r, kv), bias_r[kv], _swa_sink_row(sink_r, kv), valid)
            og = _dot_tn(_mx(pn), v_r[span, ksl])
            for g in range(SWA_GROUP):
                h = kv * SWA_GROUP + g
                o_r[:, HEAD_PAD * h:HEAD_PAD * (h + 1)] = og[SWA_BLOCK * g:SWA_BLOCK * (g + 1)].astype(o_r.dtype)

    qw = SWA_Q_HEADS * HEAD_PAD
    body, extra, extra_specs = _after(body, 5, dep)
    return pl.pallas_call(
        body, name="swa_fwd", grid=(L // SWA_BLOCK,),
        in_specs=[_row_spec(SWA_BLOCK, qw), _vmem_spec(), _vmem_spec(), _vmem_spec(),
                  pl.BlockSpec(memory_space=pltpu.SMEM)] + extra_specs,
        out_specs=_row_spec(SWA_BLOCK, qw),
        out_shape=jax.ShapeDtypeStruct((L, qw), MXU_DTYPE),
        compiler_params=_params(("arbitrary",), VMEM_BIG),
    )(qs, ks, vs, bias, sink, *extra)


def _swa_bwd_call(qs, ks, vs, bias, sink, do, dep=None):
    L = qs.shape[0]
    qw = SWA_Q_HEADS * HEAD_PAD
    kw = SWA_KV_HEADS * HEAD_PAD

    def body(q_r, k_r, v_r, bias_r, sink_r, do_r, dq_r, dk_r, dv_r, dbias_r, dsink_r):
        n = pl.program_id(0)

        @pl.when(n == 0)
        def _():
            for ref in (dk_r, dv_r, dbias_r, dsink_r):
                ref[...] = jnp.zeros_like(ref)

        span = pl.ds(pl.multiple_of(n * SWA_BLOCK, SWA_BLOCK), SWA_SPAN)
        valid = _swa_valid(n, L)
        for kv in range(SWA_KV_HEADS):
            ksl = slice(HEAD_PAD * kv, HEAD_PAD * (kv + 1))
            kk = k_r[span, ksl]
            vv = v_r[span, ksl]
            qg = _swa_group(q_r, kv)
            dog = _swa_group(do_r, kv)
            pn, p_sink = _swa_probs(kk, qg, bias_r[kv], _swa_sink_row(sink_r, kv), valid)
            dp = _dot_nt(vv, dog)
            delta = jnp.sum(pn * dp, axis=0, keepdims=True)
            ds = pn * (dp - delta)
            dsink_r[kv] -= p_sink * delta
            dbias_r[kv] += ds
            ds_m = _mx(ds)
            dqg = _dot_tn(ds_m, kk) * 0.125
            for g in range(SWA_GROUP):
                h = kv * SWA_GROUP + g
                dq_r[:, HEAD_PAD * h:HEAD_PAD * (h + 1)] = dqg[SWA_BLOCK * g:SWA_BLOCK * (g + 1)]
            dk_r[span, ksl] += _dot(ds_m, qg) * 0.125
            dv_r[span, ksl] += _dot(_mx(pn), dog)

    body, extra, extra_specs = _after(body, 6, dep)
    return pl.pallas_call(
        body, name="swa_bwd", grid=(L // SWA_BLOCK,),
        in_specs=[_row_spec(SWA_BLOCK, qw), _vmem_spec(), _vmem_spec(), _vmem_spec(),
                  pl.BlockSpec(memory_space=pltpu.SMEM), _row_spec(SWA_BLOCK, qw)] + extra_specs,
        out_specs=[_row_spec(SWA_BLOCK, qw), _vmem_spec(), _vmem_spec(), _vmem_spec(), _vmem_spec()],
        out_shape=[jax.ShapeDtypeStruct((L, qw), F32),
                   jax.ShapeDtypeStruct((L + 2 * SWA_BLOCK, kw), F32),
                   jax.ShapeDtypeStruct((L + 2 * SWA_BLOCK, kw), F32),
                   jax.ShapeDtypeStruct((SWA_KV_HEADS, SWA_SPAN, SWA_GROUP_LANES), F32),
                   jax.ShapeDtypeStruct((SWA_KV_HEADS, 1, SWA_GROUP_LANES), F32)],
        compiler_params=_params(("arbitrary",), VMEM_BIG),
    )(qs, ks, vs, bias, sink, do, *extra)


def _bias_call(rel_bias, buckets):
    def body(t_r, bk_r, o_r):
        bk = bk_r[...]
        for h in range(SWA_Q_HEADS):
            acc = jnp.zeros(bk.shape, F32)
            for b in range(REL_BUCKETS):
                acc = jnp.where(bk == b, t_r[b, h], acc)
            g = h % SWA_GROUP
            o_r[h // SWA_GROUP, :, SWA_BLOCK * g:SWA_BLOCK * (g + 1)] = acc

    return pl.pallas_call(
        body, name="band_bias",
        in_specs=[pl.BlockSpec(memory_space=pltpu.SMEM), _vmem_spec()], out_specs=_vmem_spec(),
        out_shape=jax.ShapeDtypeStruct((SWA_KV_HEADS, SWA_SPAN, SWA_GROUP_LANES), F32),
    )(rel_bias, buckets)


def _relbias_call(dbias, dsink, buckets):
    def body(db_r, ds_r, bk_r, o_r, os_r):
        bk = bk_r[...]
        rowi = lax.broadcasted_iota(jnp.int32, (REL_BUCKETS, 128), 0)
        lanei = lax.broadcasted_iota(jnp.int32, (REL_BUCKETS, 128), 1)
        lane1 = lax.broadcasted_iota(jnp.int32, (1, 128), 1)
        acc = jnp.zeros((REL_BUCKETS, 128), F32)
        acc_sink = jnp.zeros((1, 128), F32)
        for h in range(SWA_Q_HEADS):
            kv, g = h // SWA_GROUP, h % SWA_GROUP
            lanes = slice(SWA_BLOCK * g, SWA_BLOCK * (g + 1))
            part = db_r[kv, :, lanes]
            for b in range(REL_BUCKETS):
                s = jnp.sum(jnp.where(bk == b, part, 0.0))
                acc = acc + jnp.where((rowi == b) & (lanei == h), s, 0.0)
            acc_sink = acc_sink + jnp.where(lane1 == h, jnp.sum(ds_r[kv, :, lanes]), 0.0)
        o_r[...] = acc
        os_r[...] = acc_sink

    return pl.pallas_call(
        body, name="relbias_grad",
        in_specs=[_vmem_spec()] * 3, out_specs=[_vmem_spec()] * 2,
        out_shape=[jax.ShapeDtypeStruct((REL_BUCKETS, 128), F32), jax.ShapeDtypeStruct((1, 128), F32)],
    )(dbias, dsink, buckets)


def _mix_call(o_f, o_b, ga, o_s, x, gn, w_out_p, g_post, g_pre2):
    L = x.shape[0]
    tm = min(256, L)
    hw = GLA_HEADS * HEAD_PAD

    def body(of_r, ob_r, ga_r, os_r, x_r, gn_r, w_r, gp_r, g2_r, cat_r, mix_r, h1_r, n2_r):
        gn_v = gn_r[...]
        for h in range(GLA_HEADS):
            sl = slice(HEAD_PAD * h, HEAD_PAD * (h + 1))
            oh = of_r[:, sl] + ob_r[:, sl]
            on = oh * _rms_r(oh) * gn_v
            gate = ga_r[:, sl]
            cat_r[:, sl] = (on * (gate * jax.nn.sigmoid(gate))).astype(cat_r.dtype)
        os_v = os_r[...]
        cat_r[:, hw:] = os_v
        mix = _dot(cat_r[:, :hw], w_r[:hw, :]) + _dot(os_v, w_r[hw:, :])
        mix_r[...] = mix
        h1 = x_r[...] + mix * _rms_r(mix) * gp_r[...]
        h1_r[...] = h1
        n2_r[...] = (h1 * _rms_r(h1) * g2_r[...]).astype(n2_r.dtype)

    return pl.pallas_call(
        body, name="mix_fwd", grid=(L // tm,),
        in_specs=[_row_spec(tm, hw), _row_spec(tm, hw), _row_spec(tm, hw), _row_spec(tm, OUT_PAD - hw),
                  _row_spec(tm, D_MODEL), _full_spec((1, HEAD_PAD)), _vmem_spec(),
                  _full_spec((1, D_MODEL)), _full_spec((1, D_MODEL))],
        out_specs=[_row_spec(tm, OUT_PAD), _row_spec(tm, D_MODEL), _row_spec(tm, D_MODEL), _row_spec(tm, D_MODEL)],
        out_shape=[jax.ShapeDtypeStruct((L, OUT_PAD), MXU_DTYPE), jax.ShapeDtypeStruct((L, D_MODEL), F32),
                   jax.ShapeDtypeStruct((L, D_MODEL), F32), jax.ShapeDtypeStruct((L, D_MODEL), MXU_DTYPE)],
        compiler_params=_params(("arbitrary",), VMEM_BIG),
    )(o_f, o_b, ga, o_s, x, gn, w_out_p, g_post, g_pre2)


def _mlp_fwd_call(n2, h1, tgt, w_ud, g_post):
    L = n2.shape[0]
    tm = min(256, L)
    blk = D_FF // N_CHIPS

    def body(n2_r, h1_r, t_r, w_r, g_r, a_r, rz_r, dh2_r, dff_r, loss_r, dg_r):
        @pl.when(pl.program_id(0) == 0)
        def _():
            loss_r[...] = jnp.zeros_like(loss_r)
            dg_r[...] = jnp.zeros_like(dg_r)

        n2v = n2_r[...]
        ff = jnp.zeros((tm, D_MODEL), F32)
        for j in range(N_CHIPS):
            sl = slice(blk * j, blk * (j + 1))
            rz = jnp.maximum(_dot(n2v, w_r[j, 0]), 0.0)
            a = _mx(rz * rz)
            rz_r[:, sl] = rz.astype(rz_r.dtype)
            a_r[:, sl] = a
            ff = ff + _dot(a, w_r[j, 1])
        g = g_r[...]
        r = _rms_r(ff)
        err = h1_r[...] + ff * r * g - t_r[...]
        loss_r[...] += 0.5 * jnp.sum(err * err) / D_MODEL
        dh2 = err * (1.0 / D_MODEL)
        dh2_r[...] = dh2
        dff, dg = _rms_bwd(ff, r, g, dh2)
        dff_r[...] = dff.astype(dff_r.dtype)
        dg_r[...] += dg

    return pl.pallas_call(
        body, name="mlp_fwd", grid=(L // tm,),
        in_specs=[_row_spec(tm, D_MODEL), _row_spec(tm, D_MODEL), _row_spec(tm, D_MODEL),
                  _vmem_spec(), _full_spec((1, D_MODEL))],
        out_specs=[_row_spec(tm, D_FF), _row_spec(tm, D_FF), _row_spec(tm, D_MODEL), _row_spec(tm, D_MODEL),
                   _full_spec((1, 128)), _full_spec((1, D_MODEL))],
        out_shape=[jax.ShapeDtypeStruct((L, D_FF), MXU_DTYPE), jax.ShapeDtypeStruct((L, D_FF), MXU_DTYPE),
                   jax.ShapeDtypeStruct((L, D_MODEL), F32), jax.ShapeDtypeStruct((L, D_MODEL), MXU_DTYPE),
                   jax.ShapeDtypeStruct((1, 128), F32), jax.ShapeDtypeStruct((1, D_MODEL), F32)],
        compiler_params=_params(("arbitrary",), VMEM_BIG),
    )(n2, h1, tgt, w_ud, g_post)


def _mlp_bwd_call(dff, rz, w_ud):
    L = dff.shape[0]
    tm = min(256, L)
    blk = D_FF // N_CHIPS

    def body(dff_r, rz_r, w_r, dz_r, dn2_r):
        dffv = dff_r[...]
        dn2 = jnp.zeros((tm, D_MODEL), F32)
        for j in range(N_CHIPS):
            sl = slice(blk * j, blk * (j + 1))
            dz = _mx(_dot_nt(dffv, w_r[j, 1]) * 2.0 * rz_r[:, sl].astype(F32))
            dz_r[:, sl] = dz
            dn2 = dn2 + _dot_nt(dz, w_r[j, 0])
        dn2_r[...] = dn2

    return pl.pallas_call(
        body, name="mlp_bwd", grid=(L // tm,),
        in_specs=[_row_spec(tm, D_MODEL), _row_spec(tm, D_FF), _vmem_spec()],
        out_specs=[_row_spec(tm, D_FF), _row_spec(tm, D_MODEL)],
        out_shape=[jax.ShapeDtypeStruct((L, D_FF), MXU_DTYPE), jax.ShapeDtypeStruct((L, D_MODEL), F32)],
        compiler_params=_params(("arbitrary",), VMEM_BIG),
    )(dff, rz, w_ud)


def _mlp_wgrad_call(a, dff, n2, dz):
    L = a.shape[0]
    tf = 512
    per = (D_FF // N_CHIPS) // tf

    def body(a_r, dff_r, n2_r, dz_r, dwd_r, dwu_r):
        dwd_r[...] = _dot_tn(a_r[...], dff_r[...])
        dwu_r[...] = _dot_tn(n2_r[...], dz_r[...])

    return pl.pallas_call(
        body, name="mlp_wgrad", grid=(D_FF // tf,),
        in_specs=[pl.BlockSpec((L, tf), lambda j: (0, j)), _vmem_spec(), _vmem_spec(),
                  pl.BlockSpec((L, tf), lambda j: (0, j))],
        out_specs=[pl.BlockSpec((tf, D_MODEL), lambda j: (j, 0)),
                   pl.BlockSpec((None, D_MODEL, tf), lambda j: (j // per, 0, j % per))],
        out_shape=[jax.ShapeDtypeStruct((D_FF, D_MODEL), F32),
                   jax.ShapeDtypeStruct((N_CHIPS, D_MODEL, D_FF // N_CHIPS), F32)],
        compiler_params=_params(("arbitrary",), VMEM_BIG),
    )(a, dff, n2, dz)


def _mix_bwd_call(dn2, dh2, h1, mix, cat, o_f, o_b, ga, gn, g_post, g_pre2, w_out_p):
    L = dn2.shape[0]
    tm = min(256, L)
    hw = GLA_HEADS * HEAD_PAD

    def body(dn2_r, dh2_r, h1_r, mix_r, cat_r, of_r, ob_r, ga_r, gn_r, gp_r, g2_r, w_r,
             dh1_r, do_r, dga_r, dos_r, dw_r, dg2_r, dgp_r, dgn_r):
        @pl.when(pl.program_id(0) == 0)
        def _():
            for ref in (dw_r, dg2_r, dgp_r, dgn_r):
                ref[...] = jnp.zeros_like(ref)

        h1 = h1_r[...]
        dx2, dg2 = _rms_bwd(h1, _rms_r(h1), g2_r[...], dn2_r[...])
        dh1 = dh2_r[...] + dx2
        dh1_r[...] = dh1
        dg2_r[...] += dg2
        mix = mix_r[...]
        dmix, dgp = _rms_bwd(mix, _rms_r(mix), gp_r[...], dh1)
        dgp_r[...] += dgp
        dmix_m = _mx(dmix)
        dw_r[...] += _dot_tn(cat_r[...], dmix_m)
        dcat = _dot_nt(dmix_m, w_r[...])
        dos_r[...] = dcat[:, hw:].astype(dos_r.dtype)
        gn_v = gn_r[...]
        dgn = jnp.zeros((1, HEAD_PAD), F32)
        for h in range(GLA_HEADS):
            sl = slice(HEAD_PAD * h, HEAD_PAD * (h + 1))
            oh = of_r[:, sl] + ob_r[:, sl]
            rr = _rms_r(oh)
            gate = ga_r[:, sl]
            sg = jax.nn.sigmoid(gate)
            doa = dcat[:, sl]
            dga_r[:, sl] = doa * (oh * rr * gn_v) * (sg * (1.0 + gate * (1.0 - sg)))
            do_h, dgn_h = _rms_bwd(oh, rr, gn_v, doa * (gate * sg))
            do_r[:, sl] = do_h
            dgn = dgn + dgn_h
        dgn_r[...] += dgn

    return pl.pallas_call(
        body, name="mix_bwd", grid=(L // tm,),
        in_specs=[_row_spec(tm, D_MODEL)] * 4 + [_row_spec(tm, OUT_PAD)] + [_row_spec(tm, hw)] * 3
        + [_full_spec((1, HEAD_PAD)), _full_spec((1, D_MODEL)), _full_spec((1, D_MODEL)), _vmem_spec()],
        out_specs=[_row_spec(tm, D_MODEL), _row_spec(tm, hw), _row_spec(tm, hw), _row_spec(tm, OUT_PAD - hw),
                   _full_spec((OUT_PAD, D_MODEL)), _full_spec((1, D_MODEL)), _full_spec((1, D_MODEL)),
                   _full_spec((1, HEAD_PAD))],
        out_shape=[jax.ShapeDtypeStruct((L, D_MODEL), F32), jax.ShapeDtypeStruct((L, hw), F32),
                   jax.ShapeDtypeStruct((L, hw), F32), jax.ShapeDtypeStruct((L, OUT_PAD - hw), MXU_DTYPE),
                   jax.ShapeDtypeStruct((OUT_PAD, D_MODEL), F32), jax.ShapeDtypeStruct((1, D_MODEL), F32),
                   jax.ShapeDtypeStruct((1, D_MODEL), F32), jax.ShapeDtypeStruct((1, HEAD_PAD), F32)],
        compiler_params=_params(("arbitrary",), VMEM_BIG),
    )(dn2, dh2, h1, mix, cat, o_f, o_b, ga, gn, g_post, g_pre2, w_out_p)


def _in_bwd_call(x, dh1, g_pre, w_in_p, pairs, singles, dep=None):
    L = x.shape[0]
    tm = min(256, L)
    n_pair, n_single = len(pairs), len(singles)
    groups = [c for c, _ in pairs] + [c for c, _ in singles]

    def body(*refs):
        x_r, dh1_r, g_r, w_r = refs[:4]
        pair_refs = refs[4:4 + 2 * n_pair]
        single_refs = refs[4 + 2 * n_pair:4 + 2 * n_pair + n_single]
        dx_r, dw_r, dg_r = refs[4 + 2 * n_pair + n_single:]

        @pl.when(pl.program_id(0) == 0)
        def _():
            dw_r[...] = jnp.zeros_like(dw_r)
            dg_r[...] = jnp.zeros_like(dg_r)

        xv = x_r[...]
        r = _rms_r(xv)
        g = g_r[...]
        u = _mx(xv * r * g)
        vals = [pair_refs[2 * i][...] + pair_refs[2 * i + 1][...] for i in range(n_pair)]
        vals += [ref[...].astype(F32) for ref in single_refs]
        du = jnp.zeros((tm, D_MODEL), F32)
        for (off, width), val in zip(groups, vals):
            d = _mx(val)
            du = du + _dot_nt(d, w_r[:, off:off + width])
            dw_r[:, off:off + width] += _dot_tn(u, d)
        dx, dg = _rms_bwd(xv, r, g, du)
        dx_r[...] = dh1_r[...] + dx
        dg_r[...] += dg

    arrays = [a for _, pr in pairs for a in pr] + [a for _, a in singles]
    specs = [_row_spec(tm, a.shape[1]) for a in arrays]
    body, extra, extra_specs = _after(body, 4 + len(arrays), dep)
    return pl.pallas_call(
        body, name="in_bwd", grid=(L // tm,),
        in_specs=[_row_spec(tm, D_MODEL), _row_spec(tm, D_MODEL), _full_spec((1, D_MODEL)), _vmem_spec()] + specs
        + extra_specs,
        out_specs=[_row_spec(tm, D_MODEL), _full_spec((D_MODEL, IN_PAD)), _full_spec((1, D_MODEL))],
        out_shape=[jax.ShapeDtypeStruct((L, D_MODEL), F32), jax.ShapeDtypeStruct((D_MODEL, IN_PAD), F32),
                   jax.ShapeDtypeStruct((1, D_MODEL), F32)],
        compiler_params=_params(("arbitrary",), VMEM_BIG),
    )(x, dh1, g_pre, w_in_p, *arrays, *extra)


def _adamw_math(w, g, m, v):
    m = ADAM_B1 * m + (1.0 - ADAM_B1) * g
    v = ADAM_B2 * v + (1.0 - ADAM_B2) * (g * g)
    m_hat = m / (1.0 - ADAM_B1 ** ADAM_STEP)
    v_hat = v / (1.0 - ADAM_B2 ** ADAM_STEP)
    delta = -ADAM_LR * (m_hat / (jnp.sqrt(v_hat) + ADAM_EPS) + ADAM_WD * w)
    return delta, m, v


def _adamw_call(w, g, m, v, name, dep=None):
    rows, cols = w.shape
    tr = min(256, rows)

    def body(w_r, g_r, m_r, v_r, d_r, nm_r, nv_r):
        d_r[...], nm_r[...], nv_r[...] = _adamw_math(w_r[...], g_r[...], m_r[...], v_r[...])

    spec = _row_spec(tr, cols)
    body, extra, extra_specs = _after(body, 4, dep)
    return pl.pallas_call(
        body, name=name, grid=(rows // tr,),
        in_specs=[spec] * 4 + extra_specs, out_specs=[spec] * 3,
        out_shape=[jax.ShapeDtypeStruct(w.shape, F32)] * 3,
        compiler_params=_params(("arbitrary",)),
    )(w, g, m, v, *extra)


def _position():
    return lax.axis_index("x"), lax.axis_index("y"), lax.axis_index("c")


def _other_chips(x, y):
    return [(1 - x, y), (x, 1 - y), (1 - x, 1 - y)]


def _rows(ref, start, size):
    span = pl.ds(pl.multiple_of(start, 16), size)
    return ref.at[span, :] if len(ref.shape) == 2 else ref.at[:, span, :]


def _first_gather_call(shards):
    n = len(shards)

    def body(*refs):
        srcs, outs = refs[:n], refs[n:2 * n]
        send_sems, recv_sems, local_sems = refs[2 * n:]
        x, y, c = _position()
        sibling = (x, y, 1 - c)
        chips = _other_chips(x, y)
        local = [pltpu.make_async_copy(srcs[a], outs[a].at[2 * x + y], local_sems.at[a]) for a in range(n)]
        for cp in local:
            cp.start()

        def copy(a, k, block, to, src=None):
            px, py, pc = block
            half = shards[a].shape[0] // 2
            dst = _rows(outs[a].at[2 * px + py], pc * half, half)
            return pltpu.make_async_remote_copy(
                src_ref=dst if src is None else src, dst_ref=dst, send_sem=send_sems.at[6 * a + k],
                recv_sem=recv_sems.at[6 * a + k], device_id=to, device_id_type=MESH_ID)

        first, passed = [], []
        for a in range(n):
            half = shards[a].shape[0] // 2
            my_half = _rows(srcs[a], c * half, half)
            first += [copy(a, j, (x, y, c), (*chip, c), src=my_half) for j, chip in enumerate(chips)]
        for cp in first:
            cp.start()
        for a in range(n):
            for j, chip in enumerate(chips):
                copy(a, j, (*chip, c), (x, y, c)).wait_recv()
                passed.append(copy(a, 3 + j, (*chip, c), sibling))
                passed[-1].start()
        for a in range(n):
            for j, chip in enumerate(chips):
                copy(a, 3 + j, (*chip, 1 - c), (x, y, c)).wait_recv()
        for cp in first + passed:
            cp.wait_send()
        for cp in local:
            cp.wait()

    return pl.pallas_call(
        body, name="first_gather",
        in_specs=[_any_spec()] * n, out_specs=[_any_spec()] * n,
        out_shape=[jax.ShapeDtypeStruct((N_CHIPS,) + s.shape, s.dtype) for s in shards],
        scratch_shapes=[pltpu.SemaphoreType.DMA((6 * n,)), pltpu.SemaphoreType.DMA((6 * n,)),
                        pltpu.SemaphoreType.DMA((n,))],
    )(*shards)


def _split_start(name, arrays, n_copies, plan):
    n = len(arrays)

    def body(*refs):
        ins, send_sems, recv_sems, token = refs[:n], refs[n], refs[n + 1], refs[-1]
        for k, (src, dst, to, _) in enumerate(plan(ins)):
            pltpu.make_async_remote_copy(src_ref=src, dst_ref=dst, send_sem=send_sems.at[k],
                                         recv_sem=recv_sems.at[k], device_id=to, device_id_type=MESH_ID).start()
        token[...] = jnp.zeros_like(token)

    hbm = pl.BlockSpec(memory_space=pltpu.HBM)
    sem = pl.BlockSpec(memory_space=pltpu.SEMAPHORE)
    out = pl.pallas_call(
        body, name=name,
        out_shape=(pltpu.SemaphoreType.DMA((n_copies,)), pltpu.SemaphoreType.DMA((n_copies,)))
        + tuple(pltpu.HBM(a.shape, a.dtype) for a in arrays) + (jax.ShapeDtypeStruct((8, 128), F32),),
        in_specs=[hbm] * n, out_specs=(sem, sem) + (hbm,) * n + (_vmem_spec(),),
        input_output_aliases={i: 2 + i for i in range(n)},
        compiler_params=pltpu.CompilerParams(has_side_effects=pltpu.SideEffectType.DATAFLOW_SIDE_EFFECTING),
    )(*[pltpu.with_memory_space_constraint(a, pltpu.HBM) for a in arrays])
    return (out[0], out[1], tuple(out[2:2 + n])), out[-1]


def _split_wait(name, handle, n_copies, plan, after):
    send_sems, recv_sems, arrays = handle
    n = len(arrays)

    def body(*refs):
        ins, s_sems, r_sems = refs[:n], refs[n], refs[n + 1]
        for k, (src, dst, to, landed) in enumerate(plan(ins)):
            cp = pltpu.make_async_remote_copy(src_ref=src, dst_ref=landed, send_sem=s_sems.at[k],
                                              recv_sem=r_sems.at[k], device_id=to, device_id_type=MESH_ID)
            cp.wait_send()
            cp.wait_recv()

    hbm = pl.BlockSpec(memory_space=pltpu.HBM)
    sem = pl.BlockSpec(memory_space=pltpu.SEMAPHORE)
    out = pl.pallas_call(
        body, name=name,
        out_shape=tuple(pltpu.HBM(a.shape, a.dtype) for a in arrays),
        in_specs=[hbm] * n + [sem, sem, _any_spec()], out_specs=(hbm,) * n,
        input_output_aliases={i: i for i in range(n)},
        compiler_params=pltpu.CompilerParams(has_side_effects=pltpu.SideEffectType.DATAFLOW_SIDE_EFFECTING),
    )(*arrays, send_sems, recv_sems, after)
    return tuple(out)


def _gather_plans(shard_rows):
    n = len(shard_rows)

    def stage_one(refs):
        x, y, c = _position()
        copies = []
        for a, rows in enumerate(shard_rows):
            half = rows // 2
            for px, py in _other_chips(x, y):
                copies.append((_rows(refs[a], c * half, half), _rows(refs[n + a].at[2 * x + y], c * half, half),
                               (px, py, c), _rows(refs[n + a].at[2 * px + py], c * half, half)))
        return copies

    def stage_two(refs):
        x, y, c = _position()
        copies = []
        for a, rows in enumerate(shard_rows):
            half = rows // 2
            for px, py in _other_chips(x, y):
                piece = _rows(refs[n + a].at[2 * px + py], c * half, half)
                copies.append((piece, piece, (x, y, 1 - c), _rows(refs[n + a].at[2 * px + py], (1 - c) * half, half)))
        return copies

    return stage_one, stage_two


def _pair_swap_plan(n):
    def plan(refs):
        x, y, c = _position()
        copies = []
        for a in range(n):
            half = refs[a].shape[1] // 2
            copies.append((_rows(refs[a], (1 - c) * half, half), refs[n + a], (x, y, 1 - c), refs[n + a]))
        return copies

    return plan


def _chip_swap_plan(n):
    def plan(refs):
        x, y, c = _position()
        copies = []
        for a in range(n):
            for j, (px, py) in enumerate(_other_chips(x, y)):
                copies.append((refs[a].at[2 * px + py], refs[n + a].at[j], (px, py, c), refs[n + a].at[j]))
        return copies

    return plan


def _pair_join_plan(n):
    def plan(refs):
        x, y, c = _position()
        copies = []
        for a in range(n):
            half = refs[a].shape[0] // 2
            mine = _rows(refs[a], c * half, half)
            copies.append((mine, mine, (x, y, 1 - c), _rows(refs[a], (1 - c) * half, half)))
        return copies

    return plan


def _pair_add_call(g, got, pos, name):
    half, cols = got.shape[1], got.shape[2]
    tr = min(256, half)
    nblk = half // tr

    def body(pos_r, g_r, got_r, o_r):
        o_r[...] = (g_r[...] + got_r[...]).astype(o_r.dtype)

    return pl.pallas_call(
        body, name=name,
        grid_spec=pltpu.PrefetchScalarGridSpec(
            num_scalar_prefetch=1, grid=(N_CHIPS, nblk),
            in_specs=[pl.BlockSpec((None, tr, cols), lambda j, i, p: (j, p[1] * nblk + i, 0)),
                      pl.BlockSpec((None, tr, cols), lambda j, i, p: (j, i, 0))],
            out_specs=pl.BlockSpec((None, tr, cols), lambda j, i, p: (j, i, 0))),
        out_shape=jax.ShapeDtypeStruct((N_CHIPS, half, cols), COMM_DTYPE),
        compiler_params=_params(("arbitrary", "arbitrary")),
    )(pos, g, got)


def _chip_add_call(hsum, got, pos, name):
    half, cols = hsum.shape[1], hsum.shape[2]
    tr = min(256, half)
    nblk = half // tr

    def body(pos_r, own_r, got_r, o_r):
        acc = own_r[...].astype(F32)
        for j in range(3):
            acc = acc + got_r[j].astype(F32)
        o_r[...] = acc

    return pl.pallas_call(
        body, name=name,
        grid_spec=pltpu.PrefetchScalarGridSpec(
            num_scalar_prefetch=1, grid=(nblk,),
            in_specs=[pl.BlockSpec((None, tr, cols), lambda i, p: (p[0], i, 0)),
                      pl.BlockSpec((3, tr, cols), lambda i, p: (0, i, 0))],
            out_specs=pl.BlockSpec((tr, cols), lambda i, p: (p[1] * nblk + i, 0))),
        out_shape=jax.ShapeDtypeStruct((2 * half, cols), F32),
        compiler_params=_params(("arbitrary",)),
    )(pos, hsum, got)


SMALL_NAMES = ("norm_mix_pre", "norm_mix_post", "norm_mlp_pre", "norm_mlp_post", "b_gate_fwd", "b_gate_bwd",
               "gla_norm", "swa_sink", "rel_bias")


def _small_update_call(grads, gate_grads, params, dep=None):
    n_dev = 8
    n_small = len(SMALL_NAMES)
    wmv = [t for p in params for t in p]
    shapes = [p[0].shape for p in params]

    def body(*refs):
        g_refs = refs[:n_small + 2]
        wmv_refs = refs[n_small + 2:n_small + 2 + 3 * n_small]
        n_in = n_small + 2 + 3 * n_small
        out_refs = refs[n_in:n_in + 4 * n_small + 2]
        pack_a, pack_b, all_a, all_b, send_sems, recv_sems = refs[n_in + 4 * n_small + 2:]
        x, y, c = _position()
        me = 4 * x + 2 * y + c
        pack_a[...] = jnp.zeros_like(pack_a)
        pack_b[...] = jnp.zeros_like(pack_b)
        for i in range(4):
            pack_a[i:i + 1, :] = g_refs[i][...]
        pack_a[4:5, 0:256] = g_refs[4][...]
        pack_a[5:6, 0:256] = g_refs[5][...]
        pack_a[6:7, 0:128] = g_refs[6][...]
        pack_a[7:8, 0:128] = g_refs[7][...]
        pack_b[0:32, 0:128] = g_refs[8][...]
        pack_b[32:48, :] = g_refs[9][...]
        pack_b[48:64, :] = g_refs[10][...]
        all_a[me] = pack_a[...]
        all_b[me] = pack_b[...]
        copies = []
        for k in range(1, n_dev):
            fx, fy, fc = (k >> 2) & 1, (k >> 1) & 1, k & 1
            to = (1 - x if fx else x, 1 - y if fy else y, 1 - c if fc else c)
            for t, (pack, dst) in enumerate(((pack_a, all_a), (pack_b, all_b))):
                copies.append(pltpu.make_async_remote_copy(
                    src_ref=pack, dst_ref=dst.at[me], send_sem=send_sems.at[2 * (k - 1) + t],
                    recv_sem=recv_sems.at[2 * (k - 1) + t], device_id=to, device_id_type=MESH_ID))
        for cp in copies:
            cp.start()
        for cp in copies:
            cp.wait()
        sum_a, sum_b = all_a[0], all_b[0]
        for d in range(1, n_dev):
            sum_a = sum_a + all_a[d]
            sum_b = sum_b + all_b[d]
        gsum = [sum_a[0:1], sum_a[1:2], sum_a[2:3], sum_a[3:4], sum_a[4:5, 0:256], sum_a[5:6, 0:256],
                sum_a[6:7, 0:128], sum_a[7:8, 0:SWA_Q_HEADS], sum_b[0:32, 0:SWA_Q_HEADS]]
        for i in range(n_small):
            w_r, m_r, v_r = wmv_refs[3 * i:3 * i + 3]
            delta, new_m, new_v = _adamw_math(w_r[...], gsum[i], m_r[...], v_r[...])
            out_refs[4 * i][...] = gsum[i]
            out_refs[4 * i + 1][...] = delta
            out_refs[4 * i + 2][...] = new_m
            out_refs[4 * i + 3][...] = new_v
        out_refs[4 * n_small][...] = sum_b[32:48]
        out_refs[4 * n_small + 1][...] = sum_b[48:64]

    n_in = n_small + 2 + 3 * n_small
    body, extra, extra_specs = _after(body, n_in, dep)
    out_shape = [jax.ShapeDtypeStruct(s, F32) for s in shapes for _ in range(4)]
    out_shape += [jax.ShapeDtypeStruct((GLA_GATE_RANK, 256), F32)] * 2
    out = pl.pallas_call(
        body, name="small_update",
        in_specs=[_vmem_spec()] * n_in + extra_specs, out_specs=[_vmem_spec()] * len(out_shape),
        out_shape=out_shape,
        scratch_shapes=[pltpu.VMEM((8, D_MODEL), F32), pltpu.VMEM((64, 256), F32),
                        pltpu.VMEM((n_dev, 8, D_MODEL), F32), pltpu.VMEM((n_dev, 64, 256), F32),
                        pltpu.SemaphoreType.DMA((2 * (n_dev - 1),)), pltpu.SemaphoreType.DMA((2 * (n_dev - 1),))],
    )(*grads, *gate_grads, *wmv, *extra)
    per_name = [tuple(out[4 * i:4 * i + 4]) for i in range(n_small)]
    return per_name, out[4 * n_small], out[4 * n_small + 1]


def _pad_heads(t, n_heads, axis=-1):
    axis = axis % t.ndim
    shape = t.shape
    t = t.reshape(shape[:axis] + (n_heads, 64) + shape[axis + 1:])
    pad = [(0, 0)] * t.ndim
    pad[axis + 1] = (0, HEAD_PAD - 64)
    return jnp.pad(t, pad).reshape(shape[:axis] + (n_heads * HEAD_PAD,) + shape[axis + 1:])


def _unpad_heads(t, n_heads, axis=-1):
    axis = axis % t.ndim
    shape = t.shape
    t = t.reshape(shape[:axis] + (n_heads, HEAD_PAD) + shape[axis + 1:])
    t = lax.slice_in_dim(t, 0, 64, axis=axis + 1)
    return t.reshape(shape[:axis] + (n_heads * 64,) + shape[axis + 1:])


def _pad_w_in(w):
    return jnp.concatenate([
        _pad_heads(w[:, 0:256], 4), _pad_heads(w[:, 256:512], 4), w[:, 512:1024], w[:, 1024:1536],
        _pad_heads(w[:, 1568:2080], 8), _pad_heads(w[:, 2080:2208], 2), _pad_heads(w[:, 2208:2336], 2),
        jnp.pad(w[:, 1536:1568], ((0, 0), (0, 96)))], axis=1)


def _unpad_w_in(g):
    return jnp.concatenate([
        _unpad_heads(g[:, 0:512], 4), _unpad_heads(g[:, 512:1024], 4), g[:, 1024:1536], g[:, 1536:2048],
        g[:, 3584:3616], _unpad_heads(g[:, 2048:3072], 8), _unpad_heads(g[:, 3072:3328], 2),
        _unpad_heads(g[:, 3328:3584], 2)], axis=1)


def _pad_w_out(w):
    return jnp.concatenate([w[:512], _pad_heads(w[512:], 8, axis=0)], axis=0)


def _unpad_w_out(g):
    return jnp.concatenate([g[:512], _unpad_heads(g[512:], 8, axis=0)], axis=0)


def _pad_gate(w, first_row):
    return jnp.pad(_pad_heads(w, 4), ((first_row, 128 - GLA_GATE_RANK - first_row), (0, 0)))


def _own_slot(shard, chip):
    zone = lax.empty((N_CHIPS,) + shard.shape, shard.dtype)
    return lax.dynamic_update_slice(zone, shard[None], (chip,) + (0,) * shard.ndim)


def _reduce_to_owners(grads, pos, tag, overlap):
    n = len(grads)
    lands = [lax.empty((N_CHIPS, g.shape[1] // 2, g.shape[2]), F32) for g in grads]
    handle, token = _split_start(tag + "_pair_start", list(grads) + lands, n, _pair_swap_plan(n))
    got = _split_wait(tag + "_pair_wait", handle, n, _pair_swap_plan(n), overlap[0](token))
    sums = [_pair_add_call(got[a], got[n + a], pos, f"{tag}_pair_add{a}") for a in range(n)]
    lands = [lax.empty((3,) + s.shape[1:], s.dtype) for s in sums]
    handle, token = _split_start(tag + "_chip_start", sums + lands, 3 * n, _chip_swap_plan(n))
    got = _split_wait(tag + "_chip_wait", handle, 3 * n, _chip_swap_plan(n), overlap[1](token))
    halves = [_chip_add_call(got[a], got[n + a], pos, f"{tag}_chip_add{a}") for a in range(n)]
    handle, token = _split_start(tag + "_join_start", halves, n, _pair_join_plan(n))
    return _split_wait(tag + "_join_wait", handle, n, _pair_join_plan(n), overlap[2](token))


def kernel(x, norm_mix_pre, w_in, w_gate_up_fwd, b_gate_fwd, w_gate_up_bwd, b_gate_bwd, gla_norm, swa_sink, rel_bias, w_out, norm_mix_post, norm_mlp_pre, w_up, w_down, norm_mlp_post, loss_target, m_norm_mix_pre, m_w_in, m_w_gate_up_fwd, m_b_gate_fwd, m_w_gate_up_bwd, m_b_gate_bwd, m_gla_norm, m_swa_sink, m_rel_bias, m_w_out, m_norm_mix_post, m_norm_mlp_pre, m_w_up, m_w_down, m_norm_mlp_post, v_norm_mix_pre, v_w_in, v_w_gate_up_fwd, v_b_gate_fwd, v_w_gate_up_bwd, v_b_gate_bwd, v_gla_norm, v_swa_sink, v_rel_bias, v_w_out, v_norm_mix_post, v_norm_mlp_pre, v_w_up, v_w_down, v_norm_mlp_post):
    given = dict(locals())
    cx, cy, cc = _position()
    chip = (2 * cx + cy).astype(jnp.int32)
    pos = jnp.stack([chip, cc.astype(jnp.int32)])
    seq, tgt = x[0], loss_target[0]
    L = seq.shape[0]

    gates = jnp.concatenate([w_gate_up_fwd[0], w_gate_up_bwd[0]], axis=0).astype(COMM_DTYPE)
    all_in, all_gates = _first_gather_call([w_in[0].astype(COMM_DTYPE), gates])
    rest = [w_out[0].astype(COMM_DTYPE), jnp.stack([w_up[0], w_down[0]]).astype(COMM_DTYPE)]
    stage_one, stage_two = _gather_plans([R_OUT, R_UP])
    handle, token = _split_start("gather_chip_start", rest + [_own_slot(s, chip) for s in rest], 6, stage_one)

    w_in_p = _mx(_pad_w_in(jnp.concatenate([all_in[j] for j in range(N_CHIPS)], axis=1)))
    gates_full = jnp.concatenate([all_gates[j] for j in range(N_CHIPS)], axis=1)
    wgf_p = _mx(_pad_gate(gates_full[:GLA_GATE_RANK], 0))
    wgb_p = _mx(_pad_gate(gates_full[GLA_GATE_RANK:], GLA_GATE_RANK))
    bf_p, bb_p = _pad_heads(b_gate_fwd, 4), _pad_heads(b_gate_bwd, 4)
    buckets = jnp.asarray(_band_buckets())
    bias = _bias_call(rel_bias, buckets)
    sink1 = swa_sink.reshape(SWA_Q_HEADS)

    qa, ka, va, ga, qs, ks, vs, za = _proj_call(seq, norm_mix_pre, w_in_p, dep=token)
    halo = ((SWA_BLOCK, SWA_BLOCK), (0, 0))
    ks_p, vs_p = jnp.pad(ks, halo), jnp.pad(vs, halo)
    o_f, o_b, s_f, s_b = _gla_fwd_call(qa, ka, va, za, wgf_p, bf_p, wgb_p, bb_p)
    arrays = _split_wait("gather_chip_wait", handle, 6, stage_one, o_f)
    handle, token = _split_start("gather_pair_start", list(arrays), 6, stage_two)
    o_s = _swa_fwd_call(qs, ks_p, vs_p, bias, sink1, dep=token)
    arrays = _split_wait("gather_pair_wait", handle, 6, stage_two, o_s)
    w_out_p = _mx(_pad_w_out(arrays[2].reshape(N_CHIPS * R_OUT, D_MODEL)))
    w_ud = _mx(arrays[3])
    cat, mix, h1, n2 = _mix_call(o_f, o_b, ga, o_s, seq, gla_norm, w_out_p, norm_mix_post, norm_mlp_pre)
    a, rz, dh2, dff, loss, d_post2 = _mlp_fwd_call(n2, h1, tgt, w_ud, norm_mlp_post)

    dz, dn2 = _mlp_bwd_call(dff, rz, w_ud)
    dw_down, dw_up4 = _mlp_wgrad_call(a, dff, n2, dz)
    dh1, do, dga, dos, dw_out_p, d_pre2, d_post, d_gn = _mix_bwd_call(
        dn2, dh2, h1, mix, cat, o_f, o_b, ga, gla_norm, norm_mix_post, norm_mlp_pre, w_out_p)
    done = {}

    def gla_backward(tok):
        done["gla"] = _gla_bwd_call(qa, ka, va, za, do, s_f, s_b, wgf_p, bf_p, wgb_p, bb_p, dep=tok)
        return done["gla"][0]

    def swa_backward(tok):
        done["swa"] = _swa_bwd_call(qs, ks_p, vs_p, bias, sink1, dos, dep=tok)
        return done["swa"][0]

    def in_backward(tok):
        dqf, dkf, dvf, dzf, _, _, dqb, dkb, dvb, dzb, _, _ = done["gla"]
        dqs, dks_p, dvs_p, _, _ = done["swa"]
        dks = dks_p[SWA_BLOCK:SWA_BLOCK + L]
        dvs = dvs_p[SWA_BLOCK:SWA_BLOCK + L]
        done["in"] = _in_bwd_call(
            seq, dh1, norm_mix_pre, w_in_p,
            pairs=[(C_QA, (dqf, dqb)), (C_KA, (dkf, dkb)), (C_VA, (dvf, dvb)), (C_ZA, (dzf, dzb))],
            singles=[(C_GA, dga), (C_QS, dqs), (C_KS, dks), (C_VS, dvs)], dep=tok)
        return done["in"][0]

    g_up, g_down, g_out = _reduce_to_owners(
        [dw_up4, dw_down.reshape(N_CHIPS, R_DOWN, D_MODEL), _unpad_w_out(dw_out_p).reshape(N_CHIPS, R_OUT, D_MODEL)],
        pos, "mlp", [swa_backward, gla_backward, in_backward])
    dx, dw_in_p, d_pre = done["in"]
    dwf, dbf, dwb, dbb = done["gla"][4], done["gla"][5], done["gla"][10], done["gla"][11]
    drel, dsink = _relbias_call(done["swa"][3], done["swa"][4], buckets)

    small_grads = [d_pre, d_post, d_pre2, d_post2, _unpad_heads(dbf, 4), _unpad_heads(dbb, 4), d_gn, dsink, drel]
    gate_grads = [_unpad_heads(dwf[:GLA_GATE_RANK], 4), _unpad_heads(dwb[GLA_GATE_RANK:2 * GLA_GATE_RANK], 4)]
    small_params = [(given[n], given["m_" + n], given["v_" + n]) for n in SMALL_NAMES]
    upd = {}

    def update_up(tok):
        upd["w_up"] = (g_up,) + tuple(_adamw_call(w_up[0], g_up, m_w_up[0], v_w_up[0], "adamw_w_up", dep=tok))
        return upd["w_up"][1]

    def update_small(tok):
        per_name, gf_sum, gb_sum = _small_update_call(small_grads, gate_grads, small_params, dep=tok)
        upd.update(dict(zip(SMALL_NAMES, per_name)))
        for name, total in (("w_gate_up_fwd", gf_sum), ("w_gate_up_bwd", gb_sum)):
            g = lax.dynamic_slice(total, (0, chip * 64), (GLA_GATE_RANK, 64))
            upd[name] = (g,) + tuple(_adamw_call(given[name][0], g, given["m_" + name][0], given["v_" + name][0],
                                                 "adamw_" + name))
        upd["w_down"] = (g_down,) + tuple(
            _adamw_call(w_down[0], g_down, m_w_down[0], v_w_down[0], "adamw_w_down", dep=gf_sum))
        return upd["w_down"][1]

    def update_out(tok):
        upd["w_out"] = (g_out,) + tuple(_adamw_call(w_out[0], g_out, m_w_out[0], v_w_out[0], "adamw_w_out", dep=tok))
        return upd["w_out"][1]

    dw_in = _unpad_w_in(dw_in_p)
    dw_in4 = jnp.stack([dw_in[:, R_IN * j:R_IN * (j + 1)] for j in range(N_CHIPS)], axis=0)
    (g_in,) = _reduce_to_owners([dw_in4], pos, "in", [update_up, update_small, update_out])
    upd["w_in"] = (g_in,) + tuple(_adamw_call(w_in[0], g_in, m_w_in[0], v_w_in[0], "adamw_w_in"))

    big = ("w_in", "w_gate_up_fwd", "w_gate_up_bwd", "w_out", "w_up", "w_down")
    names = ["norm_mix_pre", "w_in", "w_gate_up_fwd", "b_gate_fwd", "w_gate_up_bwd", "b_gate_bwd", "gla_norm",
             "swa_sink", "rel_bias", "w_out", "norm_mix_post", "norm_mlp_pre", "w_up", "w_down", "norm_mlp_post"]
    outs = [lax.psum(loss[0, 0], MESH_AXES), dx[None]]
    for kind in range(4):
        outs += [upd[n][kind][None] if n in big else upd[n][kind] for n in names]
    return tuple(outs)
```

```python
import math

import numpy as np
import jax
import jax.numpy as jnp
from jax import lax
from jax.experimental import pallas as pl
from jax.experimental.pallas import tpu as pltpu

F32 = jnp.float32
MXU_DTYPE = jnp.bfloat16
COMM_DTYPE = jnp.bfloat16

D_MODEL = 1024
D_FF = 4096
N_CHIPS = 4
GLA_HEADS = 4
GLA_CHUNK = 64
GLA_GATE_RANK = 16
GLA_GATE_NORM = 16.0
SWA_Q_HEADS = 8
SWA_KV_HEADS = 2
SWA_BLOCK = 128
REL_BUCKETS = 32
REL_MAX_DIST = 128
NORM_EPS = 1e-6
HEAD_PAD = 128

ADAM_LR = 0.001
ADAM_B1 = 0.9
ADAM_B2 = 0.999
ADAM_EPS = 1e-08
ADAM_WD = 0.01
ADAM_STEP = 10

OUT_PAD = 1024

R_IN, R_OUT, R_UP, R_DOWN = 584, 256, 1024, 1024

VMEM_BIG = 56 * 1024 * 1024
MESH_AXES = ("x", "y", "c")
MESH_ID = pl.DeviceIdType.MESH


def _mx(a):
    return a.astype(MXU_DTYPE)


def _dot(a, b):
    return jnp.dot(a, b, preferred_element_type=F32)


def _dot_nt(a, b):
    return lax.dot_general(a, b, (((1,), (1,)), ((), ())), preferred_element_type=F32)


def _dot_tn(a, b):
    return lax.dot_general(a, b, (((0,), (0,)), ((), ())), preferred_element_type=F32)


def _rms_r(x):
    return lax.rsqrt(jnp.mean(x * x, axis=-1, keepdims=True) + NORM_EPS)


def _rms_bwd(x, r, g, dy):
    xh = x * r
    gdy = dy * g
    dx = r * (gdy - xh * jnp.mean(gdy * xh, axis=-1, keepdims=True))
    return dx, jnp.sum(dy * xh, axis=0, keepdims=True)


def _low_half(rows):
    return lax.broadcasted_iota(jnp.int32, (rows, HEAD_PAD), 1) < 64


def _spread_heads(x):
    low = _low_half(x.shape[0])
    parts = []
    for p in range(x.shape[1] // HEAD_PAD):
        pair = x[:, HEAD_PAD * p:HEAD_PAD * (p + 1)]
        parts += [jnp.where(low, pair, 0.0), jnp.where(low, pltpu.roll(pair, 64, 1), 0.0)]
    return jnp.concatenate(parts, axis=1)


def _squeeze_heads(x):
    low = _low_half(x.shape[0])
    parts = []
    for p in range(x.shape[1] // (2 * HEAD_PAD)):
        even = x[:, 2 * HEAD_PAD * p:2 * HEAD_PAD * p + HEAD_PAD]
        odd = x[:, 2 * HEAD_PAD * p + HEAD_PAD:2 * HEAD_PAD * (p + 1)]
        parts.append(jnp.where(low, even, pltpu.roll(odd, 64, 1)))
    return parts[0] if len(parts) == 1 else jnp.concatenate(parts, axis=1)


def _params(sem=None, vmem=None):
    kw = {}
    if sem is not None:
        kw["dimension_semantics"] = sem
    if vmem is not None:
        kw["vmem_limit_bytes"] = vmem
    return pltpu.CompilerParams(**kw)


def _vmem_spec():
    return pl.BlockSpec(memory_space=pltpu.VMEM)


def _row_spec(tm, width):
    return pl.BlockSpec((tm, width), lambda i: (i, 0))


def _full_spec(shape):
    return pl.BlockSpec(shape, lambda i: (0,) * len(shape))


def _any_spec():
    return pl.BlockSpec(memory_space=pl.ANY)


def _after(body, n_in, dep):
    if dep is None:
        return body, [], []
    return (lambda *refs: body(*refs[:n_in], *refs[n_in + 1:])), [dep], [_any_spec()]


T_QA, T_KA, T_VA, T_GA = (0, 256, 4), (256, 256, 4), (512, 512, 0), (1024, 512, 0)
T_QS, T_KS, T_VS = (1568, 512, 8), (2080, 128, 2), (2208, 128, 2)
T_ZA = (1536, 128, 0)
ZA_COLS = 2 * GLA_GATE_RANK
IN_COLS = 2336


def _proj_call(x, g_pre, w_in_t, dep=None):
    L = x.shape[0]
    tm = min(256, L)
    groups = [(T_QA, F32), (T_KA, F32), (T_VA, MXU_DTYPE), (T_GA, F32),
              (T_QS, MXU_DTYPE), (T_KS, MXU_DTYPE), (T_VS, MXU_DTYPE), (T_ZA, F32)]
    widths = [rows * (2 if heads else 1) for (_, rows, heads), _ in groups]

    def body(x_ref, g_ref, w_ref, *outs):
        xv = x_ref[...]
        u = _mx(xv * _rms_r(xv) * g_ref[...])
        for ref, (grp, _) in zip(outs, groups):
            first, rows, heads = grp
            val = _dot_nt(u, w_ref[first:first + rows, :])
            if heads:
                val = _spread_heads(val)
            if grp is T_ZA:
                val = jnp.where(lax.broadcasted_iota(jnp.int32, val.shape, 1) < ZA_COLS, val, 0.0)
            ref[...] = val.astype(ref.dtype)

    body, extra, extra_specs = _after(body, 3, dep)
    return pl.pallas_call(
        body, name="proj_fwd", grid=(L // tm,),
        in_specs=[_row_spec(tm, D_MODEL), _full_spec((1, D_MODEL)), _vmem_spec()] + extra_specs,
        out_specs=[_row_spec(tm, w) for w in widths],
        out_shape=[jax.ShapeDtypeStruct((L, w), dt) for w, (_, dt) in zip(widths, groups)],
        compiler_params=_params(("arbitrary",), VMEM_BIG),
    )(x, g_pre, w_in_t, *extra)


def _tri_masks():
    row = lax.broadcasted_iota(jnp.int32, (GLA_CHUNK, GLA_CHUNK), 0)
    col = lax.broadcasted_iota(jnp.int32, (GLA_CHUNK, GLA_CHUNK), 1)
    return row >= col, row <= col


def _chunk_sums(tri_m, x):
    hi = _mx(x)
    rest = x - hi.astype(F32)
    mid = _mx(rest)
    lo = _mx(rest - mid.astype(F32))
    return _dot(tri_m, hi) + _dot(tri_m, mid) + _dot(tri_m, lo)


def _gla_block_pre(q_r, k_r, z_r, w_r, b_r, rev, nc, qd_s, ki_s, ks_s, dec_s, keep=None):
    tri_f, tri_b = _tri_masks()
    tri_m = _mx((tri_b if rev else tri_f).astype(F32))
    g = _dot(_mx(z_r[...]), w_r[...]) + b_r[...]
    la = (jnp.minimum(g, 0.0) - jnp.log(1.0 + jnp.exp(-jnp.abs(g)))) / GLA_GATE_NORM
    sums, lasts = [], []
    for c in range(nc):
        b_c = _chunk_sums(tri_m, la[GLA_CHUNK * c:GLA_CHUNK * (c + 1)])
        blast = b_c[0:1] if rev else b_c[GLA_CHUNK - 1:GLA_CHUNK]
        dec_s[c] = jnp.exp(blast)
        sums.append(b_c)
        lasts.append(jnp.broadcast_to(blast, b_c.shape))
    b = jnp.concatenate(sums, axis=0)
    eb = jnp.exp(b)
    enb = jnp.exp(-b)
    elb = jnp.exp(jnp.concatenate(lasts, axis=0) - b)
    k = k_r[...]
    qd_s[...] = (q_r[...] * 0.125 * eb).astype(qd_s.dtype)
    ki_s[...] = (k * enb).astype(ki_s.dtype)
    ks_s[...] = (k * elb).astype(ks_s.dtype)
    if keep is not None:
        for ref, val in zip(keep, (g, eb, enb, elb)):
            ref[...] = val


def _gla_fwd_call(qa, ka, va, za, wgf, bgf, wgb, bgb):
    L = qa.shape[0]
    br = min(512, L)
    nb, nc, n_chunks = L // br, br // GLA_CHUNK, L // GLA_CHUNK
    hw = GLA_HEADS * HEAD_PAD

    def body(qaf, kaf, vaf, zaf, qab, kab, vab, zab, wgf_r, bgf_r, wgb_r, bgb_r,
             of_r, ob_r, sf_r, sb_r, st_f, st_b, pre_f, pre_b):
        @pl.when(pl.program_id(0) == 0)
        def _():
            st_f[...] = jnp.zeros_like(st_f)
            st_b[...] = jnp.zeros_like(st_b)

        _gla_block_pre(qaf, kaf, zaf, wgf_r, bgf_r, False, nc, *pre_f)
        _gla_block_pre(qab, kab, zab, wgb_r, bgb_r, True, nc, *pre_b)
        tri_f, tri_b = _tri_masks()

        def one(tri, pre, v_r, o_r, s_r, st, ci):
            qd_s, ki_s, ks_s, dec_s = pre
            rows = pl.ds(pl.multiple_of(ci * GLA_CHUNK, GLA_CHUNK), GLA_CHUNK)
            dec = dec_s[ci]
            for h in range(GLA_HEADS):
                sl = slice(HEAD_PAD * h, HEAD_PAD * (h + 1))
                qd = qd_s[rows, sl]
                a = jnp.where(tri, _dot_nt(qd, ki_s[rows, sl]), 0.0)
                v = v_r[rows, sl]
                s_t = st[h]
                s_r[ci, h] = s_t
                o_r[rows, sl] = _dot(_mx(a), v) + _dot_nt(qd, _mx(s_t))
                st[h] = s_t * dec[:, sl] + _dot_tn(v, ks_s[rows, sl])

        def loop(t, carry):
            one(tri_f, pre_f, vaf, of_r, sf_r, st_f, t)
            one(tri_b, pre_b, vab, ob_r, sb_r, st_b, nc - 1 - t)
            return carry

        lax.fori_loop(0, nc, loop, 0)

    fwd = lambda i: (i, 0)
    bwd = lambda i: (nb - 1 - i, 0)
    ins = lambda m: [pl.BlockSpec((br, hw), m), pl.BlockSpec((br, hw), m),
                     pl.BlockSpec((br, hw), m), pl.BlockSpec((br, 128), m)]
    wspecs = [_full_spec((128, hw)), _full_spec((1, hw))] * 2
    s_shape = (nc, GLA_HEADS, HEAD_PAD, HEAD_PAD)
    pre_scratch = [pltpu.VMEM((br, hw), MXU_DTYPE)] * 3 + [pltpu.VMEM((nc, 1, hw), F32)]
    return pl.pallas_call(
        body, name="gla_fwd", grid=(nb,),
        in_specs=ins(fwd) + ins(bwd) + wspecs,
        out_specs=[pl.BlockSpec((br, hw), fwd), pl.BlockSpec((br, hw), bwd),
                   pl.BlockSpec(s_shape, lambda i: (i, 0, 0, 0)),
                   pl.BlockSpec(s_shape, lambda i: (nb - 1 - i, 0, 0, 0))],
        out_shape=[jax.ShapeDtypeStruct((L, hw), F32), jax.ShapeDtypeStruct((L, hw), F32),
                   jax.ShapeDtypeStruct((n_chunks,) + s_shape[1:], F32),
                   jax.ShapeDtypeStruct((n_chunks,) + s_shape[1:], F32)],
        scratch_shapes=[pltpu.VMEM(s_shape[1:], F32), pltpu.VMEM(s_shape[1:], F32), pre_scratch, pre_scratch],
        compiler_params=_params(("arbitrary",), VMEM_BIG),
    )(qa, ka, va, za, qa, ka, va, za, wgf, bgf, wgb, bgb)


def _gla_bwd_call(qa, ka, va, za, do, sf, sb, wgf, bgf, wgb, bgb, dep=None):
    L = qa.shape[0]
    br = min(256, L)
    nb, nc = L // br, br // GLA_CHUNK
    hw = GLA_HEADS * HEAD_PAD

    def body(qaf, kaf, vaf, zaf, dof, sf_r, qab, kab, vab, zab, dob, sb_r, wgf_r, bgf_r, wgb_r, bgb_r,
             dqf, dkf, dvf, dzf, dwf, dbf, dqb, dkb, dvb, dzb, dwb, dbb, gt_f, gt_b, pre_f, pre_b):
        @pl.when(pl.program_id(0) == 0)
        def _():
            for ref in (gt_f, gt_b, dwf, dbf, dwb, dbb):
                ref[...] = jnp.zeros_like(ref)

        _gla_block_pre(qaf, kaf, zaf, wgf_r, bgf_r, False, nc, *pre_f[:4], keep=pre_f[4:8])
        _gla_block_pre(qab, kab, zab, wgb_r, bgb_r, True, nc, *pre_b[:4], keep=pre_b[4:8])
        tri_f, tri_b = _tri_masks()
        row_w = lax.broadcasted_iota(jnp.int32, (GLA_CHUNK, HEAD_PAD), 0)

        def one(rev, pre, q_r, k_r, v_r, do_r, s_r, dq_r, dk_r, dv_r, gt, ci):
            qd_s, ki_s, ks_s, dec_s, _, eb_s, enb_s, elb_s, db_s = pre
            tri = tri_b if rev else tri_f
            last_row = 0 if rev else GLA_CHUNK - 1
            rows = pl.ds(pl.multiple_of(ci * GLA_CHUNK, GLA_CHUNK), GLA_CHUNK)
            dec = dec_s[ci]
            for h in range(GLA_HEADS):
                sl = slice(HEAD_PAD * h, HEAD_PAD * (h + 1))
                qd, ki, ks = qd_s[rows, sl], ki_s[rows, sl], ks_s[rows, sl]
                a = _mx(jnp.where(tri, _dot_nt(qd, ki), 0.0))
                v = v_r[rows, sl]
                do_h = _mx(do_r[rows, sl])
                s_t = s_r[ci, h]
                g_t = gt[h]
                g_m = _mx(g_t)
                da = _mx(jnp.where(tri, _dot_nt(do_h, v), 0.0))
                dv_r[rows, sl] = _dot_tn(a, do_h) + _dot_nt(ks, g_m)
                dqd = _dot(da, ki) + _dot(do_h, _mx(s_t))
                dki = _dot_tn(da, qd)
                dks = _dot(v, g_m)
                ddec = jnp.sum(g_t * s_t, axis=0, keepdims=True)
                gt[h] = g_t * dec[:, sl] + _dot_tn(do_h, qd)
                dq = dqd * eb_s[rows, sl] * 0.125
                dk_state = dks * elb_s[rows, sl]
                dk = dki * enb_s[rows, sl] + dk_state
                dq_r[rows, sl] = dq
                dk_r[rows, sl] = dk
                k = k_r[rows, sl]
                dblast = jnp.sum(dk_state * k, axis=0, keepdims=True) + dec[:, sl] * ddec
                db_s[rows, sl] = q_r[rows, sl] * dq - k * dk + jnp.where(row_w == last_row, dblast, 0.0)

        def loop(t, carry):
            one(False, pre_f, qaf, kaf, vaf, dof, sf_r, dqf, dkf, dvf, gt_f, nc - 1 - t)
            one(True, pre_b, qab, kab, vab, dob, sb_r, dqb, dkb, dvb, gt_b, t)
            return carry

        lax.fori_loop(0, nc, loop, 0)

        def gate_grads(rev, pre, z_r, w_r, dz_r, dw_r, dbias_r):
            g_s, db_s = pre[4], pre[8]
            back_m = _mx((tri_f if rev else tri_b).astype(F32))
            db = db_s[...]
            dla = jnp.concatenate([_chunk_sums(back_m, db[GLA_CHUNK * c:GLA_CHUNK * (c + 1)]) for c in range(nc)],
                                  axis=0)
            dg = dla * (1.0 / GLA_GATE_NORM) * (1.0 / (1.0 + jnp.exp(g_s[...])))
            dg_m = _mx(dg)
            dz_r[...] = _dot_nt(dg_m, w_r[...])
            dw_r[...] += _dot_tn(_mx(z_r[...]), dg_m)
            dbias_r[...] += jnp.sum(dg, axis=0, keepdims=True)

        gate_grads(False, pre_f, zaf, wgf_r, dzf, dwf, dbf)
        gate_grads(True, pre_b, zab, wgb_r, dzb, dwb, dbb)

    last_first = lambda i: (nb - 1 - i, 0)
    first_last = lambda i: (i, 0)
    s_shape = (nc, GLA_HEADS, HEAD_PAD, HEAD_PAD)

    def ins(m):
        return [pl.BlockSpec((br, hw), m), pl.BlockSpec((br, hw), m), pl.BlockSpec((br, hw), m),
                pl.BlockSpec((br, 128), m), pl.BlockSpec((br, hw), m),
                pl.BlockSpec(s_shape, lambda i: m(i) + (0, 0))]

    def outs(m):
        return [pl.BlockSpec((br, hw), m), pl.BlockSpec((br, hw), m), pl.BlockSpec((br, hw), m),
                pl.BlockSpec((br, 128), m), _full_spec((128, hw)), _full_spec((1, hw))]

    out_shape = [jax.ShapeDtypeStruct((L, hw), F32)] * 3 + [
        jax.ShapeDtypeStruct((L, 128), F32), jax.ShapeDtypeStruct((128, hw), F32),
        jax.ShapeDtypeStruct((1, hw), F32)]
    wspecs = [_full_spec((128, hw)), _full_spec((1, hw))] * 2
    body, extra, extra_specs = _after(body, 16, dep)
    pre_scratch = ([pltpu.VMEM((br, hw), MXU_DTYPE)] * 3 + [pltpu.VMEM((nc, 1, hw), F32)]
                   + [pltpu.VMEM((br, hw), F32)] * 5)
    return pl.pallas_call(
        body, name="gla_bwd", grid=(nb,),
        in_specs=ins(last_first) + ins(first_last) + wspecs + extra_specs,
        out_specs=outs(last_first) + outs(first_last),
        out_shape=out_shape + out_shape,
        scratch_shapes=[pltpu.VMEM(s_shape[1:], F32), pltpu.VMEM(s_shape[1:], F32), pre_scratch, pre_scratch],
        compiler_params=_params(("arbitrary",), VMEM_BIG),
    )(qa, ka, va, za, do, sf, qa, ka, va, za, do, sb, wgf, bgf, wgb, bgb, *extra)


def _t5_buckets(rel):
    nb = REL_BUCKETS // 2
    ret = (rel > 0).astype(np.int32) * nb
    n = np.abs(rel)
    max_exact = nb // 2
    large = max_exact + (np.log(np.maximum(n, 1).astype(np.float32) / max_exact)
                         / math.log(REL_MAX_DIST / max_exact) * (nb - max_exact)).astype(np.int32)
    large = np.minimum(large, nb - 1)
    return ret + np.where(n < max_exact, n, large)


SWA_GROUP = SWA_Q_HEADS // SWA_KV_HEADS
SWA_SPAN = 3 * SWA_BLOCK
SWA_GROUP_LANES = SWA_GROUP * SWA_BLOCK


def _band_buckets():
    s = np.arange(SWA_SPAN)[:, None]
    c = np.arange(SWA_BLOCK)[None, :]
    return _t5_buckets(s - SWA_BLOCK - c).astype(np.int32)


def _swa_valid(n, seq_len):
    s = lax.broadcasted_iota(jnp.int32, (SWA_SPAN, SWA_GROUP_LANES), 0)
    c = lax.broadcasted_iota(jnp.int32, (SWA_SPAN, SWA_GROUP_LANES), 1) & (SWA_BLOCK - 1)
    rel = s - SWA_BLOCK - c
    key_pos = (n - 1) * SWA_BLOCK + s
    return (jnp.abs(rel) <= SWA_BLOCK) & (key_pos >= 0) & (key_pos < seq_len)


def _swa_sink_row(sink_r, kv):
    lane = lax.broadcasted_iota(jnp.int32, (1, SWA_GROUP_LANES), 1)
    row = jnp.full((1, SWA_GROUP_LANES), sink_r[kv * SWA_GROUP], F32)
    for g in range(1, SWA_GROUP):
        row = jnp.where(lane >= g * SWA_BLOCK, sink_r[kv * SWA_GROUP + g], row)
    return row


def _swa_group(ref, kv):
    first = kv * SWA_GROUP
    return jnp.concatenate([ref[:, HEAD_PAD * h:HEAD_PAD * (h + 1)] for h in range(first, first + SWA_GROUP)],
                           axis=0)


def _swa_probs(kk, qg, bias_t, sink_row, valid):
    st = _dot_nt(kk, qg) * 0.125 + bias_t
    st = jnp.where(valid, st, -1e30)
    m = jnp.maximum(jnp.max(st, axis=0, keepdims=True), sink_row)
    p = jnp.exp(st - m)
    e_sink = jnp.exp(sink_row - m)
    inv = 1.0 / (jnp.sum(p, axis=0, keepdims=True) + e_sink)
    return p * inv, e_sink * inv


def _swa_fwd_call(qs, ks, vs, bias, sink, dep=None):
    L = qs.shape[0]

    def body(q_r, k_r, v_r, bias_r, sink_r, o_r):
        n = pl.program_id(0)
        span = pl.ds(pl.multiple_of(n * SWA_BLOCK, SWA_BLOCK), SWA_SPAN)
        valid = _swa_valid(n, L)
        for kv in range(SWA_KV_HEADS):
            ksl = slice(HEAD_PAD * kv, HEAD_PAD * (kv + 1))
            pn, _ = _swa_probs(k_r[span, ksl], _swa_group(q_r, kv), bias_r[kv], _swa_sink_row(sink_r, kv), valid)
            og = _dot_tn(_mx(pn), v_r[span, ksl])
            low = _low_half(SWA_BLOCK)
            for pair in range(SWA_GROUP // 2):
                even = og[2 * SWA_BLOCK * pair:2 * SWA_BLOCK * pair + SWA_BLOCK]
                odd = og[2 * SWA_BLOCK * pair + SWA_BLOCK:2 * SWA_BLOCK * (pair + 1)]
                first = HEAD_PAD * (kv * SWA_GROUP // 2 + pair)
                o_r[:, first:first + HEAD_PAD] = jnp.where(low, even, pltpu.roll(odd, 64, 1)).astype(o_r.dtype)

    qw = SWA_Q_HEADS * HEAD_PAD
    body, extra, extra_specs = _after(body, 5, dep)
    return pl.pallas_call(
        body, name="swa_fwd", grid=(L // SWA_BLOCK,),
        in_specs=[_row_spec(SWA_BLOCK, qw), _vmem_spec(), _vmem_spec(), _vmem_spec(),
                  pl.BlockSpec(memory_space=pltpu.SMEM)] + extra_specs,
        out_specs=_row_spec(SWA_BLOCK, qw // 2),
        out_shape=jax.ShapeDtypeStruct((L, qw // 2), MXU_DTYPE),
        compiler_params=_params(("arbitrary",), VMEM_BIG),
    )(qs, ks, vs, bias, sink, *extra)


def _swa_bwd_call(qs, ks, vs, bias, sink, do, dep=None):
    L = qs.shape[0]
    qw = SWA_Q_HEADS * HEAD_PAD
    kw = SWA_KV_HEADS * HEAD_PAD

    def body(q_r, k_r, v_r, bias_r, sink_r, do_r, dq_r, dk_r, dv_r, dbias_r, dsink_r):
        n = pl.program_id(0)

        @pl.when(n == 0)
        def _():
            for ref in (dk_r, dv_r, dbias_r, dsink_r):
                ref[...] = jnp.zeros_like(ref)

        span = pl.ds(pl.multiple_of(n * SWA_BLOCK, SWA_BLOCK), SWA_SPAN)
        valid = _swa_valid(n, L)
        for kv in range(SWA_KV_HEADS):
            ksl = slice(HEAD_PAD * kv, HEAD_PAD * (kv + 1))
            kk = k_r[span, ksl]
            vv = v_r[span, ksl]
            qg = _swa_group(q_r, kv)
            dog = _swa_group(do_r, kv)
            pn, p_sink = _swa_probs(kk, qg, bias_r[kv], _swa_sink_row(sink_r, kv), valid)
            dp = _dot_nt(vv, dog)
            delta = jnp.sum(pn * dp, axis=0, keepdims=True)
            ds = pn * (dp - delta)
            dsink_r[kv] -= p_sink * delta
            dbias_r[kv] += ds
            ds_m = _mx(ds)
            dqg = _dot_tn(ds_m, kk) * 0.125
            for g in range(SWA_GROUP):
                h = kv * SWA_GROUP + g
                dq_r[:, HEAD_PAD * h:HEAD_PAD * (h + 1)] = dqg[SWA_BLOCK * g:SWA_BLOCK * (g + 1)]
            dk_r[span, ksl] += _dot(ds_m, qg) * 0.125
            dv_r[span, ksl] += _dot(_mx(pn), dog)

    body, extra, extra_specs = _after(body, 6, dep)
    return pl.pallas_call(
        body, name="swa_bwd", grid=(L // SWA_BLOCK,),
        in_specs=[_row_spec(SWA_BLOCK, qw), _vmem_spec(), _vmem_spec(), _vmem_spec(),
                  pl.BlockSpec(memory_space=pltpu.SMEM), _row_spec(SWA_BLOCK, qw)] + extra_specs,
        out_specs=[_row_spec(SWA_BLOCK, qw), _vmem_spec(), _vmem_spec(), _vmem_spec(), _vmem_spec()],
        out_shape=[jax.ShapeDtypeStruct((L, qw), F32),
                   jax.ShapeDtypeStruct((L + 2 * SWA_BLOCK, kw), F32),
                   jax.ShapeDtypeStruct((L + 2 * SWA_BLOCK, kw), F32),
                   jax.ShapeDtypeStruct((SWA_KV_HEADS, SWA_SPAN, SWA_GROUP_LANES), F32),
                   jax.ShapeDtypeStruct((SWA_KV_HEADS, 1, SWA_GROUP_LANES), F32)],
        compiler_params=_params(("arbitrary",), VMEM_BIG),
    )(qs, ks, vs, bias, sink, do, *extra)


def _bias_call(rel_bias, buckets):
    def body(t_r, bk_r, o_r):
        bk = bk_r[...]
        for h in range(SWA_Q_HEADS):
            acc = jnp.zeros(bk.shape, F32)
            for b in range(REL_BUCKETS):
                acc = jnp.where(bk == b, t_r[b, h], acc)
            g = h % SWA_GROUP
            o_r[h // SWA_GROUP, :, SWA_BLOCK * g:SWA_BLOCK * (g + 1)] = acc

    return pl.pallas_call(
        body, name="band_bias",
        in_specs=[pl.BlockSpec(memory_space=pltpu.SMEM), _vmem_spec()], out_specs=_vmem_spec(),
        out_shape=jax.ShapeDtypeStruct((SWA_KV_HEADS, SWA_SPAN, SWA_GROUP_LANES), F32),
    )(rel_bias, buckets)


def _relbias_call(dbias, dsink, buckets):
    def body(db_r, ds_r, bk_r, o_r, os_r):
        bk = bk_r[...]
        rowi = lax.broadcasted_iota(jnp.int32, (REL_BUCKETS, 128), 0)
        lanei = lax.broadcasted_iota(jnp.int32, (REL_BUCKETS, 128), 1)
        lane1 = lax.broadcasted_iota(jnp.int32, (1, 128), 1)
        acc = jnp.zeros((REL_BUCKETS, 128), F32)
        acc_sink = jnp.zeros((1, 128), F32)
        for h in range(SWA_Q_HEADS):
            kv, g = h // SWA_GROUP, h % SWA_GROUP
            lanes = slice(SWA_BLOCK * g, SWA_BLOCK * (g + 1))
            part = db_r[kv, :, lanes]
            for b in range(REL_BUCKETS):
                s = jnp.sum(jnp.where(bk == b, part, 0.0))
                acc = acc + jnp.where((rowi == b) & (lanei == h), s, 0.0)
            acc_sink = acc_sink + jnp.where(lane1 == h, jnp.sum(ds_r[kv, :, lanes]), 0.0)
        o_r[...] = acc
        os_r[...] = acc_sink

    return pl.pallas_call(
        body, name="relbias_grad",
        in_specs=[_vmem_spec()] * 3, out_specs=[_vmem_spec()] * 2,
        out_shape=[jax.ShapeDtypeStruct((REL_BUCKETS, 128), F32), jax.ShapeDtypeStruct((1, 128), F32)],
    )(dbias, dsink, buckets)


def _mix_call(o_f, o_b, ga, o_s, x, gn, w_out_p, g_post, g_pre2):
    L = x.shape[0]
    tm = min(256, L)
    hw = GLA_HEADS * HEAD_PAD

    def body(of_r, ob_r, ga_r, os_r, x_r, gn_r, w_r, gp_r, g2_r, cat_r, mix_r, h1_r, n2_r):
        gn_v = gn_r[...]
        for h in range(GLA_HEADS):
            sl = slice(HEAD_PAD * h, HEAD_PAD * (h + 1))
            oh = of_r[:, sl] + ob_r[:, sl]
            on = oh * _rms_r(oh) * gn_v
            gate = ga_r[:, sl]
            cat_r[:, sl] = (on * (gate * jax.nn.sigmoid(gate))).astype(cat_r.dtype)
        os_v = os_r[...]
        cat_r[:, hw:] = os_v
        mix = _dot(cat_r[:, :hw], w_r[:hw, :]) + _dot(os_v, w_r[hw:, :])
        mix_r[...] = mix
        h1 = x_r[...] + mix * _rms_r(mix) * gp_r[...]
        h1_r[...] = h1
        n2_r[...] = (h1 * _rms_r(h1) * g2_r[...]).astype(n2_r.dtype)

    return pl.pallas_call(
        body, name="mix_fwd", grid=(L // tm,),
        in_specs=[_row_spec(tm, hw), _row_spec(tm, hw), _row_spec(tm, hw), _row_spec(tm, OUT_PAD - hw),
                  _row_spec(tm, D_MODEL), _full_spec((1, HEAD_PAD)), _vmem_spec(),
                  _full_spec((1, D_MODEL)), _full_spec((1, D_MODEL))],
        out_specs=[_row_spec(tm, OUT_PAD), _row_spec(tm, D_MODEL), _row_spec(tm, D_MODEL), _row_spec(tm, D_MODEL)],
        out_shape=[jax.ShapeDtypeStruct((L, OUT_PAD), MXU_DTYPE), jax.ShapeDtypeStruct((L, D_MODEL), F32),
                   jax.ShapeDtypeStruct((L, D_MODEL), F32), jax.ShapeDtypeStruct((L, D_MODEL), MXU_DTYPE)],
        compiler_params=_params(("arbitrary",), VMEM_BIG),
    )(o_f, o_b, ga, o_s, x, gn, w_out_p, g_post, g_pre2)


def _mlp_fwd_call(n2, h1, tgt, w_ud, g_post):
    L = n2.shape[0]
    tm = min(256, L)
    blk = D_FF // N_CHIPS

    def body(n2_r, h1_r, t_r, w_r, g_r, a_r, rz_r, dh2_r, dff_r, loss_r, dg_r):
        @pl.when(pl.program_id(0) == 0)
        def _():
            loss_r[...] = jnp.zeros_like(loss_r)
            dg_r[...] = jnp.zeros_like(dg_r)

        n2v = n2_r[...]
        ff = jnp.zeros((tm, D_MODEL), F32)
        for j in range(N_CHIPS):
            sl = slice(blk * j, blk * (j + 1))
            rz = jnp.maximum(_dot(n2v, w_r[j, 0]), 0.0)
            a = _mx(rz * rz)
            rz_r[:, sl] = rz.astype(rz_r.dtype)
            a_r[:, sl] = a
            ff = ff + _dot(a, w_r[j, 1])
        g = g_r[...]
        r = _rms_r(ff)
        err = h1_r[...] + ff * r * g - t_r[...]
        loss_r[...] += 0.5 * jnp.sum(err * err) / D_MODEL
        dh2 = err * (1.0 / D_MODEL)
        dh2_r[...] = dh2
        dff, dg = _rms_bwd(ff, r, g, dh2)
        dff_r[...] = dff.astype(dff_r.dtype)
        dg_r[...] += dg

    return pl.pallas_call(
        body, name="mlp_fwd", grid=(L // tm,),
        in_specs=[_row_spec(tm, D_MODEL), _row_spec(tm, D_MODEL), _row_spec(tm, D_MODEL),
                  _vmem_spec(), _full_spec((1, D_MODEL))],
        out_specs=[_row_spec(tm, D_FF), _row_spec(tm, D_FF), _row_spec(tm, D_MODEL), _row_spec(tm, D_MODEL),
                   _full_spec((1, 128)), _full_spec((1, D_MODEL))],
        out_shape=[jax.ShapeDtypeStruct((L, D_FF), MXU_DTYPE), jax.ShapeDtypeStruct((L, D_FF), MXU_DTYPE),
                   jax.ShapeDtypeStruct((L, D_MODEL), F32), jax.ShapeDtypeStruct((L, D_MODEL), MXU_DTYPE),
                   jax.ShapeDtypeStruct((1, 128), F32), jax.ShapeDtypeStruct((1, D_MODEL), F32)],
        compiler_params=_params(("arbitrary",), VMEM_BIG),
    )(n2, h1, tgt, w_ud, g_post)


def _mlp_bwd_call(dff, rz, w_ud):
    L = dff.shape[0]
    tm = min(256, L)
    blk = D_FF // N_CHIPS

    def body(dff_r, rz_r, w_r, dz_r, dn2_r):
        dffv = dff_r[...]
        dn2 = jnp.zeros((tm, D_MODEL), F32)
        for j in range(N_CHIPS):
            sl = slice(blk * j, blk * (j + 1))
            dz = _mx(_dot_nt(dffv, w_r[j, 1]) * 2.0 * rz_r[:, sl].astype(F32))
            dz_r[:, sl] = dz
            dn2 = dn2 + _dot_nt(dz, w_r[j, 0])
        dn2_r[...] = dn2

    return pl.pallas_call(
        body, name="mlp_bwd", grid=(L // tm,),
        in_specs=[_row_spec(tm, D_MODEL), _row_spec(tm, D_FF), _vmem_spec()],
        out_specs=[_row_spec(tm, D_FF), _row_spec(tm, D_MODEL)],
        out_shape=[jax.ShapeDtypeStruct((L, D_FF), MXU_DTYPE), jax.ShapeDtypeStruct((L, D_MODEL), F32)],
        compiler_params=_params(("arbitrary",), VMEM_BIG),
    )(dff, rz, w_ud)


def _mlp_wgrad_call(a, dff, n2, dz):
    L = a.shape[0]
    tf = 512
    per = (D_FF // N_CHIPS) // tf

    def body(a_r, dff_r, n2_r, dz_r, dwd_r, dwu_r):
        dwd_r[...] = _dot_tn(a_r[...], dff_r[...])
        dwu_r[...] = _dot_tn(n2_r[...], dz_r[...])

    return pl.pallas_call(
        body, name="mlp_wgrad", grid=(D_FF // tf,),
        in_specs=[pl.BlockSpec((L, tf), lambda j: (0, j)), _vmem_spec(), _vmem_spec(),
                  pl.BlockSpec((L, tf), lambda j: (0, j))],
        out_specs=[pl.BlockSpec((tf, D_MODEL), lambda j: (j, 0)),
                   pl.BlockSpec((None, D_MODEL, tf), lambda j: (j // per, 0, j % per))],
        out_shape=[jax.ShapeDtypeStruct((D_FF, D_MODEL), F32),
                   jax.ShapeDtypeStruct((N_CHIPS, D_MODEL, D_FF // N_CHIPS), F32)],
        compiler_params=_params(("arbitrary",), VMEM_BIG),
    )(a, dff, n2, dz)


def _mix_bwd_call(dn2, dh2, h1, mix, cat, o_f, o_b, ga, gn, g_post, g_pre2, w_out_p):
    L = dn2.shape[0]
    tm = min(256, L)
    hw = GLA_HEADS * HEAD_PAD

    def body(dn2_r, dh2_r, h1_r, mix_r, cat_r, of_r, ob_r, ga_r, gn_r, gp_r, g2_r, w_r,
             dh1_r, do_r, dga_r, dos_r, dw_r, dg2_r, dgp_r, dgn_r):
        @pl.when(pl.program_id(0) == 0)
        def _():
            for ref in (dw_r, dg2_r, dgp_r, dgn_r):
                ref[...] = jnp.zeros_like(ref)

        h1 = h1_r[...]
        dx2, dg2 = _rms_bwd(h1, _rms_r(h1), g2_r[...], dn2_r[...])
        dh1 = dh2_r[...] + dx2
        dh1_r[...] = dh1
        dg2_r[...] += dg2
        mix = mix_r[...]
        dmix, dgp = _rms_bwd(mix, _rms_r(mix), gp_r[...], dh1)
        dgp_r[...] += dgp
        dmix_m = _mx(dmix)
        dw_r[...] += _dot_tn(cat_r[...], dmix_m)
        dcat = _dot_nt(dmix_m, w_r[...])
        dos_r[...] = _spread_heads(dcat[:, hw:]).astype(dos_r.dtype)
        gn_v = gn_r[...]
        dgn = jnp.zeros((1, HEAD_PAD), F32)
        for h in range(GLA_HEADS):
            sl = slice(HEAD_PAD * h, HEAD_PAD * (h + 1))
            oh = of_r[:, sl] + ob_r[:, sl]
            rr = _rms_r(oh)
            gate = ga_r[:, sl]
            sg = jax.nn.sigmoid(gate)
            doa = dcat[:, sl]
            dga_r[:, sl] = doa * (oh * rr * gn_v) * (sg * (1.0 + gate * (1.0 - sg)))
            do_h, dgn_h = _rms_bwd(oh, rr, gn_v, doa * (gate * sg))
            do_r[:, sl] = do_h
            dgn = dgn + dgn_h
        dgn_r[...] += dgn

    return pl.pallas_call(
        body, name="mix_bwd", grid=(L // tm,),
        in_specs=[_row_spec(tm, D_MODEL)] * 4 + [_row_spec(tm, OUT_PAD)] + [_row_spec(tm, hw)] * 3
        + [_full_spec((1, HEAD_PAD)), _full_spec((1, D_MODEL)), _full_spec((1, D_MODEL)), _vmem_spec()],
        out_specs=[_row_spec(tm, D_MODEL), _row_spec(tm, hw), _row_spec(tm, hw),
                   _row_spec(tm, SWA_Q_HEADS * HEAD_PAD),
                   _full_spec((OUT_PAD, D_MODEL)), _full_spec((1, D_MODEL)), _full_spec((1, D_MODEL)),
                   _full_spec((1, HEAD_PAD))],
        out_shape=[jax.ShapeDtypeStruct((L, D_MODEL), F32), jax.ShapeDtypeStruct((L, hw), F32),
                   jax.ShapeDtypeStruct((L, hw), F32), jax.ShapeDtypeStruct((L, SWA_Q_HEADS * HEAD_PAD), MXU_DTYPE),
                   jax.ShapeDtypeStruct((OUT_PAD, D_MODEL), F32), jax.ShapeDtypeStruct((1, D_MODEL), F32),
                   jax.ShapeDtypeStruct((1, D_MODEL), F32), jax.ShapeDtypeStruct((1, HEAD_PAD), F32)],
        compiler_params=_params(("arbitrary",), VMEM_BIG),
    )(dn2, dh2, h1, mix, cat, o_f, o_b, ga, gn, g_post, g_pre2, w_out_p)


def _in_bwd_call(x, dh1, g_pre, w_in_t, pairs, singles, dep=None):
    L = x.shape[0]
    tm = min(256, L)
    n_pair, n_single = len(pairs), len(singles)
    groups = [c for c, _ in pairs] + [c for c, _ in singles]

    def body(*refs):
        x_r, dh1_r, g_r, w_r = refs[:4]
        pair_refs = refs[4:4 + 2 * n_pair]
        single_refs = refs[4 + 2 * n_pair:4 + 2 * n_pair + n_single]
        dx_r, dw_r, dg_r = refs[4 + 2 * n_pair + n_single:]

        @pl.when(pl.program_id(0) == 0)
        def _():
            dw_r[...] = jnp.zeros_like(dw_r)
            dg_r[...] = jnp.zeros_like(dg_r)

        xv = x_r[...]
        r = _rms_r(xv)
        g = g_r[...]
        u = _mx(xv * r * g)
        vals = [pair_refs[2 * i][...] + pair_refs[2 * i + 1][...] for i in range(n_pair)]
        vals += [ref[...].astype(F32) for ref in single_refs]
        du = jnp.zeros((tm, D_MODEL), F32)
        for (first, rows, heads), val in zip(groups, vals):
            d = _mx(_squeeze_heads(val) if heads else val)
            du = du + _dot(d, w_r[first:first + rows, :])
            dw_r[first:first + rows, :] += _dot_tn(d, u)
        dx, dg = _rms_bwd(xv, r, g, du)
        dx_r[...] = dh1_r[...] + dx
        dg_r[...] += dg

    arrays = [a for _, pr in pairs for a in pr] + [a for _, a in singles]
    specs = [_row_spec(tm, a.shape[1]) for a in arrays]
    body, extra, extra_specs = _after(body, 4 + len(arrays), dep)
    return pl.pallas_call(
        body, name="in_bwd", grid=(L // tm,),
        in_specs=[_row_spec(tm, D_MODEL), _row_spec(tm, D_MODEL), _full_spec((1, D_MODEL)), _vmem_spec()] + specs
        + extra_specs,
        out_specs=[_row_spec(tm, D_MODEL), _full_spec((IN_COLS, D_MODEL)), _full_spec((1, D_MODEL))],
        out_shape=[jax.ShapeDtypeStruct((L, D_MODEL), F32), jax.ShapeDtypeStruct((IN_COLS, D_MODEL), F32),
                   jax.ShapeDtypeStruct((1, D_MODEL), F32)],
        compiler_params=_params(("arbitrary",), VMEM_BIG),
    )(x, dh1, g_pre, w_in_t, *arrays, *extra)


def _adamw_math(w, g, m, v):
    m = ADAM_B1 * m + (1.0 - ADAM_B1) * g
    v = ADAM_B2 * v + (1.0 - ADAM_B2) * (g * g)
    m_hat = m / (1.0 - ADAM_B1 ** ADAM_STEP)
    v_hat = v / (1.0 - ADAM_B2 ** ADAM_STEP)
    delta = -ADAM_LR * (m_hat / (jnp.sqrt(v_hat) + ADAM_EPS) + ADAM_WD * w)
    return delta, m, v


def _adamw_call(w, g, m, v, name, dep=None):
    rows, cols = w.shape
    tr = min(256, rows)

    def body(w_r, g_r, m_r, v_r, d_r, nm_r, nv_r):
        d_r[...], nm_r[...], nv_r[...] = _adamw_math(w_r[...], g_r[...], m_r[...], v_r[...])

    if rows % tr == 0:
        spec, steps = _row_spec(tr, cols), rows // tr
    else:
        spec, steps = pl.BlockSpec((rows, 256), lambda i: (0, i)), cols // 256
    body, extra, extra_specs = _after(body, 4, dep)
    return pl.pallas_call(
        body, name=name, grid=(steps,),
        in_specs=[spec] * 4 + extra_specs, out_specs=[spec] * 3,
        out_shape=[jax.ShapeDtypeStruct(w.shape, F32)] * 3,
        compiler_params=_params(("arbitrary",)),
    )(w, g, m, v, *extra)


def _position():
    return lax.axis_index("x"), lax.axis_index("y"), lax.axis_index("c")


def _other_chips(x, y):
    return [(1 - x, y), (x, 1 - y), (1 - x, 1 - y)]


ROWS, COLS = -2, -1


def _half(ref, which, axis):
    size = ref.shape[axis] // 2
    span = pl.ds(pl.multiple_of(which * size, 16 if axis == ROWS else 128), size)
    index = [slice(None)] * len(ref.shape)
    index[axis] = span
    return ref.at[tuple(index)]


def _first_gather_call(shards, axes):
    n = len(shards)

    def body(*refs):
        srcs, outs = refs[:n], refs[n:2 * n]
        send_sems, recv_sems, local_sems = refs[2 * n:]
        x, y, c = _position()
        sibling = (x, y, 1 - c)
        chips = _other_chips(x, y)
        local = [pltpu.make_async_copy(srcs[a], outs[a].at[2 * x + y], local_sems.at[a]) for a in range(n)]
        for cp in local:
            cp.start()

        def copy(a, k, block, to, src=None):
            px, py, pc = block
            dst = _half(outs[a].at[2 * px + py], pc, axes[a])
            return pltpu.make_async_remote_copy(
                src_ref=dst if src is None else src, dst_ref=dst, send_sem=send_sems.at[6 * a + k],
                recv_sem=recv_sems.at[6 * a + k], device_id=to, device_id_type=MESH_ID)

        first, passed = [], []
        for a in range(n):
            my_half = _half(srcs[a], c, axes[a])
            first += [copy(a, j, (x, y, c), (*chip, c), src=my_half) for j, chip in enumerate(chips)]
        for cp in first:
            cp.start()
        for a in range(n):
            for j, chip in enumerate(chips):
                copy(a, j, (*chip, c), (x, y, c)).wait_recv()
                passed.append(copy(a, 3 + j, (*chip, c), sibling))
                passed[-1].start()
        for a in range(n):
            for j, chip in enumerate(chips):
                copy(a, 3 + j, (*chip, 1 - c), (x, y, c)).wait_recv()
        for cp in first + passed:
            cp.wait_send()
        for cp in local:
            cp.wait()

    return pl.pallas_call(
        body, name="first_gather",
        in_specs=[_any_spec()] * n, out_specs=[_any_spec()] * n,
        out_shape=[jax.ShapeDtypeStruct((N_CHIPS,) + s.shape, s.dtype) for s in shards],
        scratch_shapes=[pltpu.SemaphoreType.DMA((6 * n,)), pltpu.SemaphoreType.DMA((6 * n,)),
                        pltpu.SemaphoreType.DMA((n,))],
    )(*shards)


def _split_start(name, arrays, n_copies, plan):
    n = len(arrays)

    def body(*refs):
        ins, send_sems, recv_sems, token = refs[:n], refs[n], refs[n + 1], refs[-1]
        for k, (src, dst, to, _) in enumerate(plan(ins)):
            pltpu.make_async_remote_copy(src_ref=src, dst_ref=dst, send_sem=send_sems.at[k],
                                         recv_sem=recv_sems.at[k], device_id=to, device_id_type=MESH_ID).start()
        token[...] = jnp.zeros_like(token)

    hbm = pl.BlockSpec(memory_space=pltpu.HBM)
    sem = pl.BlockSpec(memory_space=pltpu.SEMAPHORE)
    out = pl.pallas_call(
        body, name=name,
        out_shape=(pltpu.SemaphoreType.DMA((n_copies,)), pltpu.SemaphoreType.DMA((n_copies,)))
        + tuple(pltpu.HBM(a.shape, a.dtype) for a in arrays) + (jax.ShapeDtypeStruct((8, 128), F32),),
        in_specs=[hbm] * n, out_specs=(sem, sem) + (hbm,) * n + (_vmem_spec(),),
        input_output_aliases={i: 2 + i for i in range(n)},
        compiler_params=pltpu.CompilerParams(has_side_effects=pltpu.SideEffectType.DATAFLOW_SIDE_EFFECTING),
    )(*[pltpu.with_memory_space_constraint(a, pltpu.HBM) for a in arrays])
    return (out[0], out[1], tuple(out[2:2 + n])), out[-1]


def _split_wait(name, handle, n_copies, plan, after):
    send_sems, recv_sems, arrays = handle
    n = len(arrays)

    def body(*refs):
        ins, s_sems, r_sems = refs[:n], refs[n], refs[n + 1]
        for k, (src, dst, to, landed) in enumerate(plan(ins)):
            cp = pltpu.make_async_remote_copy(src_ref=src, dst_ref=landed, send_sem=s_sems.at[k],
                                              recv_sem=r_sems.at[k], device_id=to, device_id_type=MESH_ID)
            cp.wait_send()
            cp.wait_recv()

    hbm = pl.BlockSpec(memory_space=pltpu.HBM)
    sem = pl.BlockSpec(memory_space=pltpu.SEMAPHORE)
    out = pl.pallas_call(
        body, name=name,
        out_shape=tuple(pltpu.HBM(a.shape, a.dtype) for a in arrays),
        in_specs=[hbm] * n + [sem, sem, _any_spec()], out_specs=(hbm,) * n,
        input_output_aliases={i: i for i in range(n)},
        compiler_params=pltpu.CompilerParams(has_side_effects=pltpu.SideEffectType.DATAFLOW_SIDE_EFFECTING),
    )(*arrays, send_sems, recv_sems, after)
    return tuple(out)


def _gather_plans(axes):
    n = len(axes)

    def stage_one(refs):
        x, y, c = _position()
        copies = []
        for a, axis in enumerate(axes):
            for px, py in _other_chips(x, y):
                copies.append((_half(refs[a], c, axis), _half(refs[n + a].at[2 * x + y], c, axis),
                               (px, py, c), _half(refs[n + a].at[2 * px + py], c, axis)))
        return copies

    def stage_two(refs):
        x, y, c = _position()
        copies = []
        for a, axis in enumerate(axes):
            for px, py in _other_chips(x, y):
                piece = _half(refs[n + a].at[2 * px + py], c, axis)
                copies.append((piece, piece, (x, y, 1 - c), _half(refs[n + a].at[2 * px + py], 1 - c, axis)))
        return copies

    return stage_one, stage_two


def _pair_swap_plan(axes):
    n = len(axes)

    def plan(refs):
        x, y, c = _position()
        return [(_half(refs[a], 1 - c, axes[a]), refs[n + a], (x, y, 1 - c), refs[n + a]) for a in range(n)]

    return plan


def _chip_swap_plan(n):
    def plan(refs):
        x, y, c = _position()
        copies = []
        for a in range(n):
            for j, (px, py) in enumerate(_other_chips(x, y)):
                copies.append((refs[a].at[2 * px + py], refs[n + a].at[j], (px, py, c), refs[n + a].at[j]))
        return copies

    return plan


def _pair_join_plan(axes):
    def plan(refs):
        x, y, c = _position()
        copies = []
        for a, axis in enumerate(axes):
            mine = _half(refs[a], c, axis)
            copies.append((mine, mine, (x, y, 1 - c), _half(refs[a], 1 - c, axis)))
        return copies

    return plan


def _pair_add_call(g, got, pos, name, axis):
    rows, cols = got.shape[1], got.shape[2]
    tr = min(256, rows) if axis == ROWS else rows
    nblk = rows // tr
    if axis == ROWS:
        mine = lambda j, i, p: (j, p[1] * nblk + i, 0)
    else:
        mine = lambda j, i, p: (j, 0, p[1])

    def body(pos_r, g_r, got_r, o_r):
        o_r[...] = (g_r[...] + got_r[...]).astype(o_r.dtype)

    return pl.pallas_call(
        body, name=name,
        grid_spec=pltpu.PrefetchScalarGridSpec(
            num_scalar_prefetch=1, grid=(N_CHIPS, nblk),
            in_specs=[pl.BlockSpec((None, tr, cols), mine),
                      pl.BlockSpec((None, tr, cols), lambda j, i, p: (j, i, 0))],
            out_specs=pl.BlockSpec((None, tr, cols), lambda j, i, p: (j, i, 0))),
        out_shape=jax.ShapeDtypeStruct(got.shape, COMM_DTYPE),
        compiler_params=_params(("arbitrary", "arbitrary")),
    )(pos, g, got)


def _chip_add_call(hsum, got, pos, name, axis):
    rows, cols = hsum.shape[1], hsum.shape[2]
    tr = min(256, rows) if axis == ROWS else rows
    nblk = rows // tr
    if axis == ROWS:
        out_shape, mine = (2 * rows, cols), (lambda i, p: (p[1] * nblk + i, 0))
    else:
        out_shape, mine = (rows, 2 * cols), (lambda i, p: (0, p[1]))

    def body(pos_r, own_r, got_r, o_r):
        acc = own_r[...].astype(F32)
        for j in range(3):
            acc = acc + got_r[j].astype(F32)
        o_r[...] = acc

    return pl.pallas_call(
        body, name=name,
        grid_spec=pltpu.PrefetchScalarGridSpec(
            num_scalar_prefetch=1, grid=(nblk,),
            in_specs=[pl.BlockSpec((None, tr, cols), lambda i, p: (p[0], i, 0)),
                      pl.BlockSpec((3, tr, cols), lambda i, p: (0, i, 0))],
            out_specs=pl.BlockSpec((tr, cols), mine)),
        out_shape=jax.ShapeDtypeStruct(out_shape, F32),
        compiler_params=_params(("arbitrary",)),
    )(pos, hsum, got)


SMALL_NAMES = ("norm_mix_pre", "norm_mix_post", "norm_mlp_pre", "norm_mlp_post", "b_gate_fwd", "b_gate_bwd",
               "gla_norm", "swa_sink", "rel_bias")


def _small_update_call(grads, gate_grads, params, dep=None):
    n_dev = 8
    n_small = len(SMALL_NAMES)
    wmv = [t for p in params for t in p]
    shapes = [p[0].shape for p in params]

    def body(*refs):
        g_refs = refs[:n_small + 2]
        wmv_refs = refs[n_small + 2:n_small + 2 + 3 * n_small]
        n_in = n_small + 2 + 3 * n_small
        out_refs = refs[n_in:n_in + 4 * n_small + 2]
        pack_a, pack_b, all_a, all_b, send_sems, recv_sems = refs[n_in + 4 * n_small + 2:]
        x, y, c = _position()
        me = 4 * x + 2 * y + c
        pack_a[...] = jnp.zeros_like(pack_a)
        pack_b[...] = jnp.zeros_like(pack_b)
        for i in range(4):
            pack_a[i:i + 1, :] = g_refs[i][...]
        pack_a[4:5, 0:256] = g_refs[4][...]
        pack_a[5:6, 0:256] = g_refs[5][...]
        pack_a[6:7, 0:128] = g_refs[6][...]
        pack_a[7:8, 0:128] = g_refs[7][...]
        pack_b[0:32, 0:128] = g_refs[8][...]
        pack_b[32:48, :] = g_refs[9][...]
        pack_b[48:64, :] = g_refs[10][...]
        all_a[me] = pack_a[...]
        all_b[me] = pack_b[...]
        copies = []
        for k in range(1, n_dev):
            fx, fy, fc = (k >> 2) & 1, (k >> 1) & 1, k & 1
            to = (1 - x if fx else x, 1 - y if fy else y, 1 - c if fc else c)
            for t, (pack, dst) in enumerate(((pack_a, all_a), (pack_b, all_b))):
                copies.append(pltpu.make_async_remote_copy(
                    src_ref=pack, dst_ref=dst.at[me], send_sem=send_sems.at[2 * (k - 1) + t],
                    recv_sem=recv_sems.at[2 * (k - 1) + t], device_id=to, device_id_type=MESH_ID))
        for cp in copies:
            cp.start()
        for cp in copies:
            cp.wait()
        sum_a, sum_b = all_a[0], all_b[0]
        for d in range(1, n_dev):
            sum_a = sum_a + all_a[d]
            sum_b = sum_b + all_b[d]
        gsum = [sum_a[0:1], sum_a[1:2], sum_a[2:3], sum_a[3:4], sum_a[4:5, 0:256], sum_a[5:6, 0:256],
                sum_a[6:7, 0:128], sum_a[7:8, 0:SWA_Q_HEADS], sum_b[0:32, 0:SWA_Q_HEADS]]
        for i in range(n_small):
            w_r, m_r, v_r = wmv_refs[3 * i:3 * i + 3]
            delta, new_m, new_v = _adamw_math(w_r[...], gsum[i], m_r[...], v_r[...])
            out_refs[4 * i][...] = gsum[i]
            out_refs[4 * i + 1][...] = delta
            out_refs[4 * i + 2][...] = new_m
            out_refs[4 * i + 3][...] = new_v
        out_refs[4 * n_small][...] = sum_b[32:48]
        out_refs[4 * n_small + 1][...] = sum_b[48:64]

    n_in = n_small + 2 + 3 * n_small
    body, extra, extra_specs = _after(body, n_in, dep)
    out_shape = [jax.ShapeDtypeStruct(s, F32) for s in shapes for _ in range(4)]
    out_shape += [jax.ShapeDtypeStruct((GLA_GATE_RANK, 256), F32)] * 2
    out = pl.pallas_call(
        body, name="small_update",
        in_specs=[_vmem_spec()] * n_in + extra_specs, out_specs=[_vmem_spec()] * len(out_shape),
        out_shape=out_shape,
        scratch_shapes=[pltpu.VMEM((8, D_MODEL), F32), pltpu.VMEM((64, 256), F32),
                        pltpu.VMEM((n_dev, 8, D_MODEL), F32), pltpu.VMEM((n_dev, 64, 256), F32),
                        pltpu.SemaphoreType.DMA((2 * (n_dev - 1),)), pltpu.SemaphoreType.DMA((2 * (n_dev - 1),))],
    )(*grads, *gate_grads, *wmv, *extra)
    per_name = [tuple(out[4 * i:4 * i + 4]) for i in range(n_small)]
    return per_name, out[4 * n_small], out[4 * n_small + 1]


def _pad_heads(t, n_heads, axis=-1):
    axis = axis % t.ndim
    shape = t.shape
    t = t.reshape(shape[:axis] + (n_heads, 64) + shape[axis + 1:])
    pad = [(0, 0)] * t.ndim
    pad[axis + 1] = (0, HEAD_PAD - 64)
    return jnp.pad(t, pad).reshape(shape[:axis] + (n_heads * HEAD_PAD,) + shape[axis + 1:])


def _unpad_heads(t, n_heads, axis=-1):
    axis = axis % t.ndim
    shape = t.shape
    t = t.reshape(shape[:axis] + (n_heads, HEAD_PAD) + shape[axis + 1:])
    t = lax.slice_in_dim(t, 0, 64, axis=axis + 1)
    return t.reshape(shape[:axis] + (n_heads * 64,) + shape[axis + 1:])


def _pad_gate(w, first_row):
    return jnp.pad(_pad_heads(w, 4), ((first_row, 128 - GLA_GATE_RANK - first_row), (0, 0)))


def _own_slot(shard, chip):
    zone = lax.empty((N_CHIPS,) + shard.shape, shard.dtype)
    return lax.dynamic_update_slice(zone, shard[None], (chip,) + (0,) * shard.ndim)


def _reduce_to_owners(grads, axes, pos, tag, overlap):
    n = len(grads)

    def half_shape(g, axis):
        return (N_CHIPS, g.shape[1] // 2, g.shape[2]) if axis == ROWS else (N_CHIPS, g.shape[1], g.shape[2] // 2)

    lands = [lax.empty(half_shape(g, axis), F32) for g, axis in zip(grads, axes)]
    handle, token = _split_start(tag + "_pair_start", list(grads) + lands, n, _pair_swap_plan(axes))
    got = _split_wait(tag + "_pair_wait", handle, n, _pair_swap_plan(axes), overlap[0](token))
    sums = [_pair_add_call(got[a], got[n + a], pos, f"{tag}_pair_add{a}", axes[a]) for a in range(n)]
    lands = [lax.empty((3,) + s.shape[1:], s.dtype) for s in sums]
    handle, token = _split_start(tag + "_chip_start", sums + lands, 3 * n, _chip_swap_plan(n))
    got = _split_wait(tag + "_chip_wait", handle, 3 * n, _chip_swap_plan(n), overlap[1](token))
    halves = [_chip_add_call(got[a], got[n + a], pos, f"{tag}_chip_add{a}", axes[a]) for a in range(n)]
    handle, token = _split_start(tag + "_join_start", halves, n, _pair_join_plan(axes))
    return _split_wait(tag + "_join_wait", handle, n, _pair_join_plan(axes), overlap[2](token))


def kernel(x, norm_mix_pre, w_in, w_gate_up_fwd, b_gate_fwd, w_gate_up_bwd, b_gate_bwd, gla_norm, swa_sink, rel_bias, w_out, norm_mix_post, norm_mlp_pre, w_up, w_down, norm_mlp_post, loss_target, m_norm_mix_pre, m_w_in, m_w_gate_up_fwd, m_b_gate_fwd, m_w_gate_up_bwd, m_b_gate_bwd, m_gla_norm, m_swa_sink, m_rel_bias, m_w_out, m_norm_mix_post, m_norm_mlp_pre, m_w_up, m_w_down, m_norm_mlp_post, v_norm_mix_pre, v_w_in, v_w_gate_up_fwd, v_b_gate_fwd, v_w_gate_up_bwd, v_b_gate_bwd, v_gla_norm, v_swa_sink, v_rel_bias, v_w_out, v_norm_mix_post, v_norm_mlp_pre, v_w_up, v_w_down, v_norm_mlp_post):
    given = dict(locals())
    cx, cy, cc = _position()
    chip = (2 * cx + cy).astype(jnp.int32)
    pos = jnp.stack([chip, cc.astype(jnp.int32)])
    seq, tgt = x[0], loss_target[0]
    L = seq.shape[0]

    gates = jnp.concatenate([w_gate_up_fwd[0], w_gate_up_bwd[0]], axis=0).astype(COMM_DTYPE)
    all_in, all_gates = _first_gather_call([w_in[0].T.astype(COMM_DTYPE), gates], [COLS, ROWS])
    rest = [w_out[0].astype(COMM_DTYPE), jnp.stack([w_up[0], w_down[0]]).astype(COMM_DTYPE)]
    stage_one, stage_two = _gather_plans([ROWS, ROWS])
    handle, token = _split_start("gather_chip_start", rest + [_own_slot(s, chip) for s in rest], 6, stage_one)

    w_in_t = _mx(all_in.reshape(IN_COLS, D_MODEL))
    gates_full = jnp.concatenate([all_gates[j] for j in range(N_CHIPS)], axis=1)
    wgf_p = _mx(_pad_gate(gates_full[:GLA_GATE_RANK], 0))
    wgb_p = _mx(_pad_gate(gates_full[GLA_GATE_RANK:], GLA_GATE_RANK))
    bf_p, bb_p = _pad_heads(b_gate_fwd, 4), _pad_heads(b_gate_bwd, 4)
    buckets = jnp.asarray(_band_buckets())
    bias = _bias_call(rel_bias, buckets)
    sink1 = swa_sink.reshape(SWA_Q_HEADS)

    qa, ka, va, ga, qs, ks, vs, za = _proj_call(seq, norm_mix_pre, w_in_t, dep=token)
    halo = ((SWA_BLOCK, SWA_BLOCK), (0, 0))
    ks_p, vs_p = jnp.pad(ks, halo), jnp.pad(vs, halo)
    o_f, o_b, s_f, s_b = _gla_fwd_call(qa, ka, va, za, wgf_p, bf_p, wgb_p, bb_p)
    arrays = _split_wait("gather_chip_wait", handle, 6, stage_one, o_f)
    handle, token = _split_start("gather_pair_start", list(arrays), 6, stage_two)
    o_s = _swa_fwd_call(qs, ks_p, vs_p, bias, sink1, dep=token)
    arrays = _split_wait("gather_pair_wait", handle, 6, stage_two, o_s)
    w_out_full = _mx(arrays[2].reshape(N_CHIPS * R_OUT, D_MODEL))
    w_ud = _mx(arrays[3])
    cat, mix, h1, n2 = _mix_call(o_f, o_b, ga, o_s, seq, gla_norm, w_out_full, norm_mix_post, norm_mlp_pre)
    a, rz, dh2, dff, loss, d_post2 = _mlp_fwd_call(n2, h1, tgt, w_ud, norm_mlp_post)

    dz, dn2 = _mlp_bwd_call(dff, rz, w_ud)
    dw_down, dw_up4 = _mlp_wgrad_call(a, dff, n2, dz)
    dh1, do, dga, dos, dw_out, d_pre2, d_post, d_gn = _mix_bwd_call(
        dn2, dh2, h1, mix, cat, o_f, o_b, ga, gla_norm, norm_mix_post, norm_mlp_pre, w_out_full)
    done = {}

    def gla_backward(tok):
        done["gla"] = _gla_bwd_call(qa, ka, va, za, do, s_f, s_b, wgf_p, bf_p, wgb_p, bb_p, dep=tok)
        return done["gla"][0]

    def swa_backward(tok):
        done["swa"] = _swa_bwd_call(qs, ks_p, vs_p, bias, sink1, dos, dep=tok)
        return done["swa"][0]

    def in_backward(tok):
        dqf, dkf, dvf, dzf, _, _, dqb, dkb, dvb, dzb, _, _ = done["gla"]
        dqs, dks_p, dvs_p, _, _ = done["swa"]
        dks = dks_p[SWA_BLOCK:SWA_BLOCK + L]
        dvs = dvs_p[SWA_BLOCK:SWA_BLOCK + L]
        done["in"] = _in_bwd_call(
            seq, dh1, norm_mix_pre, w_in_t,
            pairs=[(T_QA, (dqf, dqb)), (T_KA, (dkf, dkb)), (T_VA, (dvf, dvb)), (T_ZA, (dzf, dzb))],
            singles=[(T_GA, dga), (T_QS, dqs), (T_KS, dks), (T_VS, dvs)], dep=tok)
        return done["in"][0]

    g_up, g_down, g_out = _reduce_to_owners(
        [dw_up4, dw_down.reshape(N_CHIPS, R_DOWN, D_MODEL), dw_out.reshape(N_CHIPS, R_OUT, D_MODEL)],
        [ROWS, ROWS, ROWS], pos, "mlp", [swa_backward, gla_backward, in_backward])
    dx, dw_in_t, d_pre = done["in"]
    dwf, dbf, dwb, dbb = done["gla"][4], done["gla"][5], done["gla"][10], done["gla"][11]
    drel, dsink = _relbias_call(done["swa"][3], done["swa"][4], buckets)

    small_grads = [d_pre, d_post, d_pre2, d_post2, _unpad_heads(dbf, 4), _unpad_heads(dbb, 4), d_gn, dsink, drel]
    gate_grads = [_unpad_heads(dwf[:GLA_GATE_RANK], 4), _unpad_heads(dwb[GLA_GATE_RANK:2 * GLA_GATE_RANK], 4)]
    small_params = [(given[n], given["m_" + n], given["v_" + n]) for n in SMALL_NAMES]
    upd = {}

    def update_up(tok):
        upd["w_up"] = (g_up,) + tuple(_adamw_call(w_up[0], g_up, m_w_up[0], v_w_up[0], "adamw_w_up", dep=tok))
        return upd["w_up"][1]

    def update_small(tok):
        per_name, gf_sum, gb_sum = _small_update_call(small_grads, gate_grads, small_params, dep=tok)
        upd.update(dict(zip(SMALL_NAMES, per_name)))
        for name, total in (("w_gate_up_fwd", gf_sum), ("w_gate_up_bwd", gb_sum)):
            g = lax.dynamic_slice(total, (0, chip * 64), (GLA_GATE_RANK, 64))
            upd[name] = (g,) + tuple(_adamw_call(given[name][0], g, given["m_" + name][0], given["v_" + name][0],
                                                 "adamw_" + name))
        upd["w_down"] = (g_down,) + tuple(
            _adamw_call(w_down[0], g_down, m_w_down[0], v_w_down[0], "adamw_w_down", dep=gf_sum))
        return upd["w_down"][1]

    def update_out(tok):
        upd["w_out"] = (g_out,) + tuple(_adamw_call(w_out[0], g_out, m_w_out[0], v_w_out[0], "adamw_w_out", dep=tok))
        return upd["w_out"][1]

    (g_in_t,) = _reduce_to_owners([dw_in_t.reshape(N_CHIPS, R_IN, D_MODEL)], [COLS], pos, "in",
                                  [update_up, update_small, update_out])
    in_t = (g_in_t,) + tuple(_adamw_call(w_in[0].T, g_in_t, m_w_in[0].T, v_w_in[0].T, "adamw_w_in"))
    upd["w_in"] = tuple(t.T for t in in_t)

    big = ("w_in", "w_gate_up_fwd", "w_gate_up_bwd", "w_out", "w_up", "w_down")
    names = ["norm_mix_pre", "w_in", "w_gate_up_fwd", "b_gate_fwd", "w_gate_up_bwd", "b_gate_bwd", "gla_norm",
             "swa_sink", "rel_bias", "w_out", "norm_mix_post", "norm_mlp_pre", "w_up", "w_down", "norm_mlp_post"]
    outs = [lax.psum(loss[0, 0], MESH_AXES), dx[None]]
    for kind in range(4):
        outs += [upd[n][kind][None] if n in big else upd[n][kind] for n in names]
    return tuple(outs)
```

```python
import math

import numpy as np
import jax
import jax.numpy as jnp
from jax import lax
from jax.experimental import pallas as pl
from jax.experimental.pallas import tpu as pltpu

F32 = jnp.float32
MXU_DTYPE = jnp.bfloat16
COMM_DTYPE = jnp.bfloat16

D_MODEL = 1024
D_FF = 4096
N_CHIPS = 4
GLA_HEADS = 4
GLA_CHUNK = 64
GLA_GATE_RANK = 16
GLA_GATE_NORM = 16.0
SWA_Q_HEADS = 8
SWA_KV_HEADS = 2
SWA_BLOCK = 128
REL_BUCKETS = 32
REL_MAX_DIST = 128
NORM_EPS = 1e-6
HEAD_PAD = 128

ADAM_LR = 0.001
ADAM_B1 = 0.9
ADAM_B2 = 0.999
ADAM_EPS = 1e-08
ADAM_WD = 0.01
ADAM_STEP = 10

OUT_PAD = 1024

R_IN, R_OUT, R_UP, R_DOWN = 584, 256, 1024, 1024

VMEM_BIG = 56 * 1024 * 1024
MESH_AXES = ("x", "y", "c")
MESH_ID = pl.DeviceIdType.MESH


def _mx(a):
    return a.astype(MXU_DTYPE)


def _dot(a, b):
    return jnp.dot(a, b, preferred_element_type=F32)


def _dot_nt(a, b):
    return lax.dot_general(a, b, (((1,), (1,)), ((), ())), preferred_element_type=F32)


def _dot_tn(a, b):
    return lax.dot_general(a, b, (((0,), (0,)), ((), ())), preferred_element_type=F32)


def _rms_r(x):
    return lax.rsqrt(jnp.mean(x * x, axis=-1, keepdims=True) + NORM_EPS)


def _rms_bwd(x, r, g, dy):
    xh = x * r
    gdy = dy * g
    dx = r * (gdy - xh * jnp.mean(gdy * xh, axis=-1, keepdims=True))
    return dx, jnp.sum(dy * xh, axis=0, keepdims=True)


def _low_half(rows):
    return lax.broadcasted_iota(jnp.int32, (rows, HEAD_PAD), 1) < 64


def _spread_heads(x):
    low = _low_half(x.shape[0])
    parts = []
    for p in range(x.shape[1] // HEAD_PAD):
        pair = x[:, HEAD_PAD * p:HEAD_PAD * (p + 1)]
        parts += [jnp.where(low, pair, 0.0), jnp.where(low, pltpu.roll(pair, 64, 1), 0.0)]
    return jnp.concatenate(parts, axis=1)


def _squeeze_heads(x):
    low = _low_half(x.shape[0])
    parts = []
    for p in range(x.shape[1] // (2 * HEAD_PAD)):
        even = x[:, 2 * HEAD_PAD * p:2 * HEAD_PAD * p + HEAD_PAD]
        odd = x[:, 2 * HEAD_PAD * p + HEAD_PAD:2 * HEAD_PAD * (p + 1)]
        parts.append(jnp.where(low, even, pltpu.roll(odd, 64, 1)))
    return parts[0] if len(parts) == 1 else jnp.concatenate(parts, axis=1)


def _params(sem=None, vmem=None):
    kw = {}
    if sem is not None:
        kw["dimension_semantics"] = sem
    if vmem is not None:
        kw["vmem_limit_bytes"] = vmem
    return pltpu.CompilerParams(**kw)


def _vmem_spec():
    return pl.BlockSpec(memory_space=pltpu.VMEM)


def _row_spec(tm, width):
    return pl.BlockSpec((tm, width), lambda i: (i, 0))


def _full_spec(shape):
    return pl.BlockSpec(shape, lambda i: (0,) * len(shape))


def _any_spec():
    return pl.BlockSpec(memory_space=pl.ANY)


def _after(body, n_in, dep):
    if dep is None:
        return body, [], []
    return (lambda *refs: body(*refs[:n_in], *refs[n_in + 1:])), [dep], [_any_spec()]


T_QA, T_KA, T_VA, T_GA = (0, 256, 4), (256, 256, 4), (512, 512, 0), (1024, 512, 0)
T_QS, T_KS, T_VS = (1568, 512, 8), (2080, 128, 2), (2208, 128, 2)
T_ZA = (1536, 128, 0)
ZA_COLS = 2 * GLA_GATE_RANK
IN_COLS = 2336


def _proj_call(x, g_pre, w_in_t, dep=None):
    L = x.shape[0]
    tm = min(256, L)
    groups = [(T_QA, F32), (T_KA, F32), (T_VA, MXU_DTYPE), (T_GA, F32),
              (T_QS, MXU_DTYPE), (T_KS, MXU_DTYPE), (T_VS, MXU_DTYPE), (T_ZA, F32)]
    widths = [rows * (2 if heads else 1) for (_, rows, heads), _ in groups]

    def body(x_ref, g_ref, w_ref, *outs):
        xv = x_ref[...]
        u = _mx(xv * _rms_r(xv) * g_ref[...])
        for ref, (grp, _) in zip(outs, groups):
            first, rows, heads = grp
            val = _dot_nt(u, w_ref[first:first + rows, :])
            if heads:
                val = _spread_heads(val)
            if grp is T_ZA:
                val = jnp.where(lax.broadcasted_iota(jnp.int32, val.shape, 1) < ZA_COLS, val, 0.0)
            ref[...] = val.astype(ref.dtype)

    body, extra, extra_specs = _after(body, 3, dep)
    return pl.pallas_call(
        body, name="proj_fwd", grid=(L // tm,),
        in_specs=[_row_spec(tm, D_MODEL), _full_spec((1, D_MODEL)), _vmem_spec()] + extra_specs,
        out_specs=[_row_spec(tm, w) for w in widths],
        out_shape=[jax.ShapeDtypeStruct((L, w), dt) for w, (_, dt) in zip(widths, groups)],
        compiler_params=_params(("arbitrary",), VMEM_BIG),
    )(x, g_pre, w_in_t, *extra)


def _tri_masks():
    row = lax.broadcasted_iota(jnp.int32, (GLA_CHUNK, GLA_CHUNK), 0)
    col = lax.broadcasted_iota(jnp.int32, (GLA_CHUNK, GLA_CHUNK), 1)
    return row >= col, row <= col


def _chunk_sums(tri_m, x):
    hi = _mx(x)
    rest = x - hi.astype(F32)
    mid = _mx(rest)
    lo = _mx(rest - mid.astype(F32))
    return _dot(tri_m, hi) + _dot(tri_m, mid) + _dot(tri_m, lo)


def _gla_block_pre(q_r, k_r, z_r, w_r, b_r, rev, nc, qd_s, ki_s, ks_s, dec_s, keep=None):
    tri_f, tri_b = _tri_masks()
    tri_m = _mx((tri_b if rev else tri_f).astype(F32))
    g = _dot(_mx(z_r[...]), w_r[...]) + b_r[...]
    la = (jnp.minimum(g, 0.0) - jnp.log(1.0 + jnp.exp(-jnp.abs(g)))) / GLA_GATE_NORM
    sums, lasts = [], []
    for c in range(nc):
        b_c = _chunk_sums(tri_m, la[GLA_CHUNK * c:GLA_CHUNK * (c + 1)])
        blast = b_c[0:1] if rev else b_c[GLA_CHUNK - 1:GLA_CHUNK]
        dec_s[c] = jnp.exp(blast)
        sums.append(b_c)
        lasts.append(jnp.broadcast_to(blast, b_c.shape))
    b = jnp.concatenate(sums, axis=0)
    eb = jnp.exp(b)
    enb = jnp.exp(-b)
    elb = jnp.exp(jnp.concatenate(lasts, axis=0) - b)
    k = k_r[...]
    qd_s[...] = (q_r[...] * 0.125 * eb).astype(qd_s.dtype)
    ki_s[...] = (k * enb).astype(ki_s.dtype)
    ks_s[...] = (k * elb).astype(ks_s.dtype)
    if keep is not None:
        for ref, val in zip(keep, (g, eb, enb, elb)):
            ref[...] = val


def _gla_fwd_call(qa, ka, va, za, wgf, bgf, wgb, bgb):
    L = qa.shape[0]
    br = min(512, L)
    nb, nc, n_chunks = L // br, br // GLA_CHUNK, L // GLA_CHUNK
    hw = GLA_HEADS * HEAD_PAD

    def body(qaf, kaf, vaf, zaf, qab, kab, vab, zab, wgf_r, bgf_r, wgb_r, bgb_r,
             of_r, ob_r, sf_r, sb_r, st_f, st_b, pre_f, pre_b):
        @pl.when(pl.program_id(0) == 0)
        def _():
            st_f[...] = jnp.zeros_like(st_f)
            st_b[...] = jnp.zeros_like(st_b)

        _gla_block_pre(qaf, kaf, zaf, wgf_r, bgf_r, False, nc, *pre_f)
        _gla_block_pre(qab, kab, zab, wgb_r, bgb_r, True, nc, *pre_b)
        tri_f, tri_b = _tri_masks()

        def one(tri, pre, v_r, o_r, s_r, st, ci):
            qd_s, ki_s, ks_s, dec_s = pre
            rows = pl.ds(pl.multiple_of(ci * GLA_CHUNK, GLA_CHUNK), GLA_CHUNK)
            dec = dec_s[ci]
            for h in range(GLA_HEADS):
                sl = slice(HEAD_PAD * h, HEAD_PAD * (h + 1))
                qd = qd_s[rows, sl]
                a = jnp.where(tri, _dot_nt(qd, ki_s[rows, sl]), 0.0)
                v = v_r[rows, sl]
                s_t = st[h]
                s_r[ci, h] = s_t
                o_r[rows, sl] = _dot(_mx(a), v) + _dot_nt(qd, _mx(s_t))
                st[h] = s_t * dec[:, sl] + _dot_tn(v, ks_s[rows, sl])

        def loop(t, carry):
            one(tri_f, pre_f, vaf, of_r, sf_r, st_f, t)
            one(tri_b, pre_b, vab, ob_r, sb_r, st_b, nc - 1 - t)
            return carry

        lax.fori_loop(0, nc, loop, 0)

    fwd = lambda i: (i, 0)
    bwd = lambda i: (nb - 1 - i, 0)
    ins = lambda m: [pl.BlockSpec((br, hw), m), pl.BlockSpec((br, hw), m),
                     pl.BlockSpec((br, hw), m), pl.BlockSpec((br, 128), m)]
    wspecs = [_full_spec((128, hw)), _full_spec((1, hw))] * 2
    s_shape = (nc, GLA_HEADS, HEAD_PAD, HEAD_PAD)
    pre_scratch = [pltpu.VMEM((br, hw), MXU_DTYPE)] * 3 + [pltpu.VMEM((nc, 1, hw), F32)]
    return pl.pallas_call(
        body, name="gla_fwd", grid=(nb,),
        in_specs=ins(fwd) + ins(bwd) + wspecs,
        out_specs=[pl.BlockSpec((br, hw), fwd), pl.BlockSpec((br, hw), bwd),
                   pl.BlockSpec(s_shape, lambda i: (i, 0, 0, 0)),
                   pl.BlockSpec(s_shape, lambda i: (nb - 1 - i, 0, 0, 0))],
        out_shape=[jax.ShapeDtypeStruct((L, hw), F32), jax.ShapeDtypeStruct((L, hw), F32),
                   jax.ShapeDtypeStruct((n_chunks,) + s_shape[1:], F32),
                   jax.ShapeDtypeStruct((n_chunks,) + s_shape[1:], F32)],
        scratch_shapes=[pltpu.VMEM(s_shape[1:], F32), pltpu.VMEM(s_shape[1:], F32), pre_scratch, pre_scratch],
        compiler_params=_params(("arbitrary",), VMEM_BIG),
    )(qa, ka, va, za, qa, ka, va, za, wgf, bgf, wgb, bgb)


def _gla_bwd_call(qa, ka, va, za, do, sf, sb, wgf, bgf, wgb, bgb, dep=None):
    L = qa.shape[0]
    br = min(256, L)
    nb, nc = L // br, br // GLA_CHUNK
    hw = GLA_HEADS * HEAD_PAD

    def body(qaf, kaf, vaf, zaf, dof, sf_r, qab, kab, vab, zab, dob, sb_r, wgf_r, bgf_r, wgb_r, bgb_r,
             dqf, dkf, dvf, dzf, dwf, dbf, dqb, dkb, dvb, dzb, dwb, dbb, gt_f, gt_b, pre_f, pre_b):
        @pl.when(pl.program_id(0) == 0)
        def _():
            for ref in (gt_f, gt_b, dwf, dbf, dwb, dbb):
                ref[...] = jnp.zeros_like(ref)

        _gla_block_pre(qaf, kaf, zaf, wgf_r, bgf_r, False, nc, *pre_f[:4], keep=pre_f[4:8])
        _gla_block_pre(qab, kab, zab, wgb_r, bgb_r, True, nc, *pre_b[:4], keep=pre_b[4:8])
        tri_f, tri_b = _tri_masks()
        row_w = lax.broadcasted_iota(jnp.int32, (GLA_CHUNK, HEAD_PAD), 0)

        def one(rev, pre, q_r, k_r, v_r, do_r, s_r, dq_r, dk_r, dv_r, gt, ci):
            qd_s, ki_s, ks_s, dec_s, _, eb_s, enb_s, elb_s, db_s = pre
            tri = tri_b if rev else tri_f
            last_row = 0 if rev else GLA_CHUNK - 1
            rows = pl.ds(pl.multiple_of(ci * GLA_CHUNK, GLA_CHUNK), GLA_CHUNK)
            dec = dec_s[ci]
            for h in range(GLA_HEADS):
                sl = slice(HEAD_PAD * h, HEAD_PAD * (h + 1))
                qd, ki, ks = qd_s[rows, sl], ki_s[rows, sl], ks_s[rows, sl]
                a = _mx(jnp.where(tri, _dot_nt(qd, ki), 0.0))
                v = v_r[rows, sl]
                do_h = _mx(do_r[rows, sl])
                s_t = s_r[ci, h]
                g_t = gt[h]
                g_m = _mx(g_t)
                da = _mx(jnp.where(tri, _dot_nt(do_h, v), 0.0))
                dv_r[rows, sl] = _dot_tn(a, do_h) + _dot_nt(ks, g_m)
                dqd = _dot(da, ki) + _dot(do_h, _mx(s_t))
                dki = _dot_tn(da, qd)
                dks = _dot(v, g_m)
                ddec = jnp.sum(g_t * s_t, axis=0, keepdims=True)
                gt[h] = g_t * dec[:, sl] + _dot_tn(do_h, qd)
                dq = dqd * eb_s[rows, sl] * 0.125
                dk_state = dks * elb_s[rows, sl]
                dk = dki * enb_s[rows, sl] + dk_state
                dq_r[rows, sl] = dq
                dk_r[rows, sl] = dk
                k = k_r[rows, sl]
                dblast = jnp.sum(dk_state * k, axis=0, keepdims=True) + dec[:, sl] * ddec
                db_s[rows, sl] = q_r[rows, sl] * dq - k * dk + jnp.where(row_w == last_row, dblast, 0.0)

        def loop(t, carry):
            one(False, pre_f, qaf, kaf, vaf, dof, sf_r, dqf, dkf, dvf, gt_f, nc - 1 - t)
            one(True, pre_b, qab, kab, vab, dob, sb_r, dqb, dkb, dvb, gt_b, t)
            return carry

        lax.fori_loop(0, nc, loop, 0)

        def gate_grads(rev, pre, z_r, w_r, dz_r, dw_r, dbias_r):
            g_s, db_s = pre[4], pre[8]
            back_m = _mx((tri_f if rev else tri_b).astype(F32))
            db = db_s[...]
            dla = jnp.concatenate([_chunk_sums(back_m, db[GLA_CHUNK * c:GLA_CHUNK * (c + 1)]) for c in range(nc)],
                                  axis=0)
            dg = dla * (1.0 / GLA_GATE_NORM) * (1.0 / (1.0 + jnp.exp(g_s[...])))
            dg_m = _mx(dg)
            dz_r[...] = _dot_nt(dg_m, w_r[...])
            dw_r[...] += _dot_tn(_mx(z_r[...]), dg_m)
            dbias_r[...] += jnp.sum(dg, axis=0, keepdims=True)

        gate_grads(False, pre_f, zaf, wgf_r, dzf, dwf, dbf)
        gate_grads(True, pre_b, zab, wgb_r, dzb, dwb, dbb)

    last_first = lambda i: (nb - 1 - i, 0)
    first_last = lambda i: (i, 0)
    s_shape = (nc, GLA_HEADS, HEAD_PAD, HEAD_PAD)

    def ins(m):
        return [pl.BlockSpec((br, hw), m), pl.BlockSpec((br, hw), m), pl.BlockSpec((br, hw), m),
                pl.BlockSpec((br, 128), m), pl.BlockSpec((br, hw), m),
                pl.BlockSpec(s_shape, lambda i: m(i) + (0, 0))]

    def outs(m):
        return [pl.BlockSpec((br, hw), m), pl.BlockSpec((br, hw), m), pl.BlockSpec((br, hw), m),
                pl.BlockSpec((br, 128), m), _full_spec((128, hw)), _full_spec((1, hw))]

    out_shape = [jax.ShapeDtypeStruct((L, hw), F32)] * 3 + [
        jax.ShapeDtypeStruct((L, 128), F32), jax.ShapeDtypeStruct((128, hw), F32),
        jax.ShapeDtypeStruct((1, hw), F32)]
    wspecs = [_full_spec((128, hw)), _full_spec((1, hw))] * 2
    body, extra, extra_specs = _after(body, 16, dep)
    pre_scratch = ([pltpu.VMEM((br, hw), MXU_DTYPE)] * 3 + [pltpu.VMEM((nc, 1, hw), F32)]
                   + [pltpu.VMEM((br, hw), F32)] * 5)
    return pl.pallas_call(
        body, name="gla_bwd", grid=(nb,),
        in_specs=ins(last_first) + ins(first_last) + wspecs + extra_specs,
        out_specs=outs(last_first) + outs(first_last),
        out_shape=out_shape + out_shape,
        scratch_shapes=[pltpu.VMEM(s_shape[1:], F32), pltpu.VMEM(s_shape[1:], F32), pre_scratch, pre_scratch],
        compiler_params=_params(("arbitrary",), VMEM_BIG),
    )(qa, ka, va, za, do, sf, qa, ka, va, za, do, sb, wgf, bgf, wgb, bgb, *extra)


def _t5_buckets(rel):
    nb = REL_BUCKETS // 2
    ret = (rel > 0).astype(np.int32) * nb
    n = np.abs(rel)
    max_exact = nb // 2
    large = max_exact + (np.log(np.maximum(n, 1).astype(np.float32) / max_exact)
                         / math.log(REL_MAX_DIST / max_exact) * (nb - max_exact)).astype(np.int32)
    large = np.minimum(large, nb - 1)
    return ret + np.where(n < max_exact, n, large)


SWA_GROUP = SWA_Q_HEADS // SWA_KV_HEADS
SWA_SPAN = 3 * SWA_BLOCK
SWA_GROUP_LANES = SWA_GROUP * SWA_BLOCK


def _band_buckets():
    s = np.arange(SWA_SPAN)[:, None]
    c = np.arange(SWA_BLOCK)[None, :]
    return _t5_buckets(s - SWA_BLOCK - c).astype(np.int32)


def _swa_valid(n, seq_len):
    s = lax.broadcasted_iota(jnp.int32, (SWA_SPAN, SWA_GROUP_LANES), 0)
    c = lax.broadcasted_iota(jnp.int32, (SWA_SPAN, SWA_GROUP_LANES), 1) & (SWA_BLOCK - 1)
    rel = s - SWA_BLOCK - c
    key_pos = (n - 1) * SWA_BLOCK + s
    return (jnp.abs(rel) <= SWA_BLOCK) & (key_pos >= 0) & (key_pos < seq_len)


def _swa_sink_row(sink_r, kv):
    lane = lax.broadcasted_iota(jnp.int32, (1, SWA_GROUP_LANES), 1)
    row = jnp.full((1, SWA_GROUP_LANES), sink_r[kv * SWA_GROUP], F32)
    for g in range(1, SWA_GROUP):
        row = jnp.where(lane >= g * SWA_BLOCK, sink_r[kv * SWA_GROUP + g], row)
    return row


def _swa_group(ref, kv):
    first = kv * SWA_GROUP
    return jnp.concatenate([ref[:, HEAD_PAD * h:HEAD_PAD * (h + 1)] for h in range(first, first + SWA_GROUP)],
                           axis=0)


def _swa_probs(kk, qg, bias_t, sink_row, valid):
    st = _dot_nt(kk, qg) * 0.125 + bias_t
    st = jnp.where(valid, st, -1e30)
    m = jnp.maximum(jnp.max(st, axis=0, keepdims=True), sink_row)
    p = jnp.exp(st - m)
    e_sink = jnp.exp(sink_row - m)
    inv = 1.0 / (jnp.sum(p, axis=0, keepdims=True) + e_sink)
    return p * inv, e_sink * inv


def _swa_fwd_call(qs, ks, vs, bias, sink, dep=None):
    L = qs.shape[0]

    def body(q_r, k_r, v_r, bias_r, sink_r, o_r):
        n = pl.program_id(0)
        span = pl.ds(pl.multiple_of(n * SWA_BLOCK, SWA_BLOCK), SWA_SPAN)
        valid = _swa_valid(n, L)
        for kv in range(SWA_KV_HEADS):
            ksl = slice(HEAD_PAD * kv, HEAD_PAD * (kv + 1))
            pn, _ = _swa_probs(k_r[span, ksl], _swa_group(q_r, kv), bias_r[kv], _swa_sink_row(sink_r, kv), valid)
            og = _dot_tn(_mx(pn), v_r[span, ksl])
            low = _low_half(SWA_BLOCK)
            for pair in range(SWA_GROUP // 2):
                even = og[2 * SWA_BLOCK * pair:2 * SWA_BLOCK * pair + SWA_BLOCK]
                odd = og[2 * SWA_BLOCK * pair + SWA_BLOCK:2 * SWA_BLOCK * (pair + 1)]
                first = HEAD_PAD * (kv * SWA_GROUP // 2 + pair)
                o_r[:, first:first + HEAD_PAD] = jnp.where(low, even, pltpu.roll(odd, 64, 1)).astype(o_r.dtype)

    qw = SWA_Q_HEADS * HEAD_PAD
    body, extra, extra_specs = _after(body, 5, dep)
    return pl.pallas_call(
        body, name="swa_fwd", grid=(L // SWA_BLOCK,),
        in_specs=[_row_spec(SWA_BLOCK, qw), _vmem_spec(), _vmem_spec(), _vmem_spec(),
                  pl.BlockSpec(memory_space=pltpu.SMEM)] + extra_specs,
        out_specs=_row_spec(SWA_BLOCK, qw // 2),
        out_shape=jax.ShapeDtypeStruct((L, qw // 2), MXU_DTYPE),
        compiler_params=_params(("arbitrary",), VMEM_BIG),
    )(qs, ks, vs, bias, sink, *extra)


def _swa_bwd_call(qs, ks, vs, bias, sink, do, dep=None):
    L = qs.shape[0]
    qw = SWA_Q_HEADS * HEAD_PAD
    kw = SWA_KV_HEADS * HEAD_PAD

    def body(q_r, k_r, v_r, bias_r, sink_r, do_r, dq_r, dk_r, dv_r, dbias_r, dsink_r):
        n = pl.program_id(0)

        @pl.when(n == 0)
        def _():
            for ref in (dk_r, dv_r, dbias_r, dsink_r):
                ref[...] = jnp.zeros_like(ref)

        span = pl.ds(pl.multiple_of(n * SWA_BLOCK, SWA_BLOCK), SWA_SPAN)
        valid = _swa_valid(n, L)
        for kv in range(SWA_KV_HEADS):
            ksl = slice(HEAD_PAD * kv, HEAD_PAD * (kv + 1))
            kk = k_r[span, ksl]
            vv = v_r[span, ksl]
            qg = _swa_group(q_r, kv)
            dog = _swa_group(do_r, kv)
            pn, p_sink = _swa_probs(kk, qg, bias_r[kv], _swa_sink_row(sink_r, kv), valid)
            dp = _dot_nt(vv, dog)
            delta = jnp.sum(pn * dp, axis=0, keepdims=True)
            ds = pn * (dp - delta)
            dsink_r[kv] -= p_sink * delta
            dbias_r[kv] += ds
            ds_m = _mx(ds)
            dqg = _dot_tn(ds_m, kk) * 0.125
            for g in range(SWA_GROUP):
                h = kv * SWA_GROUP + g
                dq_r[:, HEAD_PAD * h:HEAD_PAD * (h + 1)] = dqg[SWA_BLOCK * g:SWA_BLOCK * (g + 1)]
            dk_r[span, ksl] += _dot(ds_m, qg) * 0.125
            dv_r[span, ksl] += _dot(_mx(pn), dog)

    body, extra, extra_specs = _after(body, 6, dep)
    return pl.pallas_call(
        body, name="swa_bwd", grid=(L // SWA_BLOCK,),
        in_specs=[_row_spec(SWA_BLOCK, qw), _vmem_spec(), _vmem_spec(), _vmem_spec(),
                  pl.BlockSpec(memory_space=pltpu.SMEM), _row_spec(SWA_BLOCK, qw)] + extra_specs,
        out_specs=[_row_spec(SWA_BLOCK, qw), _vmem_spec(), _vmem_spec(), _vmem_spec(), _vmem_spec()],
        out_shape=[jax.ShapeDtypeStruct((L, qw), F32),
                   jax.ShapeDtypeStruct((L + 2 * SWA_BLOCK, kw), F32),
                   jax.ShapeDtypeStruct((L + 2 * SWA_BLOCK, kw), F32),
                   jax.ShapeDtypeStruct((SWA_KV_HEADS, SWA_SPAN, SWA_GROUP_LANES), F32),
                   jax.ShapeDtypeStruct((SWA_KV_HEADS, 1, SWA_GROUP_LANES), F32)],
        compiler_params=_params(("arbitrary",), VMEM_BIG),
    )(qs, ks, vs, bias, sink, do, *extra)


def _bias_call(rel_bias, buckets):
    def body(t_r, bk_r, o_r):
        bk = bk_r[...]
        for h in range(SWA_Q_HEADS):
            acc = jnp.zeros(bk.shape, F32)
            for b in range(REL_BUCKETS):
                acc = jnp.where(bk == b, t_r[b, h], acc)
            g = h % SWA_GROUP
            o_r[h // SWA_GROUP, :, SWA_BLOCK * g:SWA_BLOCK * (g + 1)] = acc

    return pl.pallas_call(
        body, name="band_bias",
        in_specs=[pl.BlockSpec(memory_space=pltpu.SMEM), _vmem_spec()], out_specs=_vmem_spec(),
        out_shape=jax.ShapeDtypeStruct((SWA_KV_HEADS, SWA_SPAN, SWA_GROUP_LANES), F32),
    )(rel_bias, buckets)


def _relbias_call(dbias, dsink, buckets):
    def body(db_r, ds_r, bk_r, o_r, os_r):
        bk = bk_r[...]
        rowi = lax.broadcasted_iota(jnp.int32, (REL_BUCKETS, 128), 0)
        lanei = lax.broadcasted_iota(jnp.int32, (REL_BUCKETS, 128), 1)
        lane1 = lax.broadcasted_iota(jnp.int32, (1, 128), 1)
        acc = jnp.zeros((REL_BUCKETS, 128), F32)
        acc_sink = jnp.zeros((1, 128), F32)
        for h in range(SWA_Q_HEADS):
            kv, g = h // SWA_GROUP, h % SWA_GROUP
            lanes = slice(SWA_BLOCK * g, SWA_BLOCK * (g + 1))
            part = db_r[kv, :, lanes]
            for b in range(REL_BUCKETS):
                s = jnp.sum(jnp.where(bk == b, part, 0.0))
                acc = acc + jnp.where((rowi == b) & (lanei == h), s, 0.0)
            acc_sink = acc_sink + jnp.where(lane1 == h, jnp.sum(ds_r[kv, :, lanes]), 0.0)
        o_r[...] = acc
        os_r[...] = acc_sink

    return pl.pallas_call(
        body, name="relbias_grad",
        in_specs=[_vmem_spec()] * 3, out_specs=[_vmem_spec()] * 2,
        out_shape=[jax.ShapeDtypeStruct((REL_BUCKETS, 128), F32), jax.ShapeDtypeStruct((1, 128), F32)],
    )(dbias, dsink, buckets)


def _mix_call(o_f, o_b, ga, o_s, x, gn, w_out_p, g_post, g_pre2):
    L = x.shape[0]
    tm = min(256, L)
    hw = GLA_HEADS * HEAD_PAD

    def body(of_r, ob_r, ga_r, os_r, x_r, gn_r, w_r, gp_r, g2_r, cat_r, mix_r, h1_r, n2_r):
        gn_v = gn_r[...]
        for h in range(GLA_HEADS):
            sl = slice(HEAD_PAD * h, HEAD_PAD * (h + 1))
            oh = of_r[:, sl] + ob_r[:, sl]
            on = oh * _rms_r(oh) * gn_v
            gate = ga_r[:, sl]
            cat_r[:, sl] = (on * (gate * jax.nn.sigmoid(gate))).astype(cat_r.dtype)
        os_v = os_r[...]
        cat_r[:, hw:] = os_v
        mix = _dot(cat_r[:, :hw], w_r[:hw, :]) + _dot(os_v, w_r[hw:, :])
        mix_r[...] = mix
        h1 = x_r[...] + mix * _rms_r(mix) * gp_r[...]
        h1_r[...] = h1
        n2_r[...] = (h1 * _rms_r(h1) * g2_r[...]).astype(n2_r.dtype)

    return pl.pallas_call(
        body, name="mix_fwd", grid=(L // tm,),
        in_specs=[_row_spec(tm, hw), _row_spec(tm, hw), _row_spec(tm, hw), _row_spec(tm, OUT_PAD - hw),
                  _row_spec(tm, D_MODEL), _full_spec((1, HEAD_PAD)), _vmem_spec(),
                  _full_spec((1, D_MODEL)), _full_spec((1, D_MODEL))],
        out_specs=[_row_spec(tm, OUT_PAD), _row_spec(tm, D_MODEL), _row_spec(tm, D_MODEL), _row_spec(tm, D_MODEL)],
        out_shape=[jax.ShapeDtypeStruct((L, OUT_PAD), MXU_DTYPE), jax.ShapeDtypeStruct((L, D_MODEL), F32),
                   jax.ShapeDtypeStruct((L, D_MODEL), F32), jax.ShapeDtypeStruct((L, D_MODEL), MXU_DTYPE)],
        compiler_params=_params(("arbitrary",), VMEM_BIG),
    )(o_f, o_b, ga, o_s, x, gn, w_out_p, g_post, g_pre2)


def _mlp_fwd_call(n2, h1, tgt, w_ud, g_post):
    L = n2.shape[0]
    tm = min(256, L)
    blk = D_FF // N_CHIPS

    def body(n2_r, h1_r, t_r, w_r, g_r, a_r, rz_r, dh2_r, dff_r, loss_r, dg_r):
        @pl.when(pl.program_id(0) == 0)
        def _():
            loss_r[...] = jnp.zeros_like(loss_r)
            dg_r[...] = jnp.zeros_like(dg_r)

        n2v = n2_r[...]
        ff = jnp.zeros((tm, D_MODEL), F32)
        for j in range(N_CHIPS):
            sl = slice(blk * j, blk * (j + 1))
            rz = jnp.maximum(_dot(n2v, w_r[j, 0]), 0.0)
            a = _mx(rz * rz)
            rz_r[:, sl] = rz.astype(rz_r.dtype)
            a_r[:, sl] = a
            ff = ff + _dot(a, w_r[j, 1])
        g = g_r[...]
        r = _rms_r(ff)
        err = h1_r[...] + ff * r * g - t_r[...]
        loss_r[...] += 0.5 * jnp.sum(err * err) / D_MODEL
        dh2 = err * (1.0 / D_MODEL)
        dh2_r[...] = dh2
        dff, dg = _rms_bwd(ff, r, g, dh2)
        dff_r[...] = dff.astype(dff_r.dtype)
        dg_r[...] += dg

    return pl.pallas_call(
        body, name="mlp_fwd", grid=(L // tm,),
        in_specs=[_row_spec(tm, D_MODEL), _row_spec(tm, D_MODEL), _row_spec(tm, D_MODEL),
                  _vmem_spec(), _full_spec((1, D_MODEL))],
        out_specs=[_row_spec(tm, D_FF), _row_spec(tm, D_FF), _row_spec(tm, D_MODEL), _row_spec(tm, D_MODEL),
                   _full_spec((1, 128)), _full_spec((1, D_MODEL))],
        out_shape=[jax.ShapeDtypeStruct((L, D_FF), MXU_DTYPE), jax.ShapeDtypeStruct((L, D_FF), MXU_DTYPE),
                   jax.ShapeDtypeStruct((L, D_MODEL), F32), jax.ShapeDtypeStruct((L, D_MODEL), MXU_DTYPE),
                   jax.ShapeDtypeStruct((1, 128), F32), jax.ShapeDtypeStruct((1, D_MODEL), F32)],
        compiler_params=_params(("arbitrary",), VMEM_BIG),
    )(n2, h1, tgt, w_ud, g_post)


def _mlp_bwd_call(dff, rz, w_ud):
    L = dff.shape[0]
    tm = min(256, L)
    blk = D_FF // N_CHIPS

    def body(dff_r, rz_r, w_r, dz_r, dn2_r):
        dffv = dff_r[...]
        dn2 = jnp.zeros((tm, D_MODEL), F32)
        for j in range(N_CHIPS):
            sl = slice(blk * j, blk * (j + 1))
            dz = _mx(_dot_nt(dffv, w_r[j, 1]) * 2.0 * rz_r[:, sl].astype(F32))
            dz_r[:, sl] = dz
            dn2 = dn2 + _dot_nt(dz, w_r[j, 0])
        dn2_r[...] = dn2

    return pl.pallas_call(
        body, name="mlp_bwd", grid=(L // tm,),
        in_specs=[_row_spec(tm, D_MODEL), _row_spec(tm, D_FF), _vmem_spec()],
        out_specs=[_row_spec(tm, D_FF), _row_spec(tm, D_MODEL)],
        out_shape=[jax.ShapeDtypeStruct((L, D_FF), MXU_DTYPE), jax.ShapeDtypeStruct((L, D_MODEL), F32)],
        compiler_params=_params(("arbitrary",), VMEM_BIG),
    )(dff, rz, w_ud)


def _mlp_wgrad_call(a, dff, n2, dz):
    L = a.shape[0]
    tf = 512
    per = (D_FF // N_CHIPS) // tf

    def body(a_r, dff_r, n2_r, dz_r, dwd_r, dwu_r):
        dwd_r[...] = _dot_tn(a_r[...], dff_r[...])
        dwu_r[...] = _dot_tn(n2_r[...], dz_r[...])

    return pl.pallas_call(
        body, name="mlp_wgrad", grid=(D_FF // tf,),
        in_specs=[pl.BlockSpec((L, tf), lambda j: (0, j)), _vmem_spec(), _vmem_spec(),
                  pl.BlockSpec((L, tf), lambda j: (0, j))],
        out_specs=[pl.BlockSpec((tf, D_MODEL), lambda j: (j, 0)),
                   pl.BlockSpec((None, D_MODEL, tf), lambda j: (j // per, 0, j % per))],
        out_shape=[jax.ShapeDtypeStruct((D_FF, D_MODEL), F32),
                   jax.ShapeDtypeStruct((N_CHIPS, D_MODEL, D_FF // N_CHIPS), F32)],
        compiler_params=_params(("arbitrary",), VMEM_BIG),
    )(a, dff, n2, dz)


def _mix_bwd_call(dn2, dh2, h1, mix, cat, o_f, o_b, ga, gn, g_post, g_pre2, w_out_p):
    L = dn2.shape[0]
    tm = min(256, L)
    hw = GLA_HEADS * HEAD_PAD

    def body(dn2_r, dh2_r, h1_r, mix_r, cat_r, of_r, ob_r, ga_r, gn_r, gp_r, g2_r, w_r,
             dh1_r, do_r, dga_r, dos_r, dw_r, dg2_r, dgp_r, dgn_r):
        @pl.when(pl.program_id(0) == 0)
        def _():
            for ref in (dw_r, dg2_r, dgp_r, dgn_r):
                ref[...] = jnp.zeros_like(ref)

        h1 = h1_r[...]
        dx2, dg2 = _rms_bwd(h1, _rms_r(h1), g2_r[...], dn2_r[...])
        dh1 = dh2_r[...] + dx2
        dh1_r[...] = dh1
        dg2_r[...] += dg2
        mix = mix_r[...]
        dmix, dgp = _rms_bwd(mix, _rms_r(mix), gp_r[...], dh1)
        dgp_r[...] += dgp
        dmix_m = _mx(dmix)
        dw_r[...] += _dot_tn(cat_r[...], dmix_m)
        dcat = _dot_nt(dmix_m, w_r[...])
        dos_r[...] = _spread_heads(dcat[:, hw:]).astype(dos_r.dtype)
        gn_v = gn_r[...]
        dgn = jnp.zeros((1, HEAD_PAD), F32)
        for h in range(GLA_HEADS):
            sl = slice(HEAD_PAD * h, HEAD_PAD * (h + 1))
            oh = of_r[:, sl] + ob_r[:, sl]
            rr = _rms_r(oh)
            gate = ga_r[:, sl]
            sg = jax.nn.sigmoid(gate)
            doa = dcat[:, sl]
            dga_r[:, sl] = doa * (oh * rr * gn_v) * (sg * (1.0 + gate * (1.0 - sg)))
            do_h, dgn_h = _rms_bwd(oh, rr, gn_v, doa * (gate * sg))
            do_r[:, sl] = do_h
            dgn = dgn + dgn_h
        dgn_r[...] += dgn

    return pl.pallas_call(
        body, name="mix_bwd", grid=(L // tm,),
        in_specs=[_row_spec(tm, D_MODEL)] * 4 + [_row_spec(tm, OUT_PAD)] + [_row_spec(tm, hw)] * 3
        + [_full_spec((1, HEAD_PAD)), _full_spec((1, D_MODEL)), _full_spec((1, D_MODEL)), _vmem_spec()],
        out_specs=[_row_spec(tm, D_MODEL), _row_spec(tm, hw), _row_spec(tm, hw),
                   _row_spec(tm, SWA_Q_HEADS * HEAD_PAD),
                   _full_spec((OUT_PAD, D_MODEL)), _full_spec((1, D_MODEL)), _full_spec((1, D_MODEL)),
                   _full_spec((1, HEAD_PAD))],
        out_shape=[jax.ShapeDtypeStruct((L, D_MODEL), F32), jax.ShapeDtypeStruct((L, hw), F32),
                   jax.ShapeDtypeStruct((L, hw), F32), jax.ShapeDtypeStruct((L, SWA_Q_HEADS * HEAD_PAD), MXU_DTYPE),
                   jax.ShapeDtypeStruct((OUT_PAD, D_MODEL), F32), jax.ShapeDtypeStruct((1, D_MODEL), F32),
                   jax.ShapeDtypeStruct((1, D_MODEL), F32), jax.ShapeDtypeStruct((1, HEAD_PAD), F32)],
        compiler_params=_params(("arbitrary",), VMEM_BIG),
    )(dn2, dh2, h1, mix, cat, o_f, o_b, ga, gn, g_post, g_pre2, w_out_p)


def _in_bwd_call(x, dh1, g_pre, w_in_t, pairs, singles, dep=None):
    L = x.shape[0]
    tm = min(256, L)
    n_pair, n_single = len(pairs), len(singles)
    groups = [c for c, _ in pairs] + [c for c, _ in singles]

    def body(*refs):
        x_r, dh1_r, g_r, w_r = refs[:4]
        pair_refs = refs[4:4 + 2 * n_pair]
        single_refs = refs[4 + 2 * n_pair:4 + 2 * n_pair + n_single]
        dx_r, dw_r, dg_r = refs[4 + 2 * n_pair + n_single:]

        @pl.when(pl.program_id(0) == 0)
        def _():
            dw_r[...] = jnp.zeros_like(dw_r)
            dg_r[...] = jnp.zeros_like(dg_r)

        xv = x_r[...]
        r = _rms_r(xv)
        g = g_r[...]
        u = _mx(xv * r * g)
        vals = [pair_refs[2 * i][...] + pair_refs[2 * i + 1][...] for i in range(n_pair)]
        vals += [ref[...].astype(F32) for ref in single_refs]
        du = jnp.zeros((tm, D_MODEL), F32)
        for (first, rows, heads), val in zip(groups, vals):
            d = _mx(_squeeze_heads(val) if heads else val)
            du = du + _dot(d, w_r[first:first + rows, :])
            dw_r[first:first + rows, :] += _dot_tn(d, u)
        dx, dg = _rms_bwd(xv, r, g, du)
        dx_r[...] = dh1_r[...] + dx
        dg_r[...] += dg

    arrays = [a for _, pr in pairs for a in pr] + [a for _, a in singles]
    specs = [_row_spec(tm, a.shape[1]) for a in arrays]
    body, extra, extra_specs = _after(body, 4 + len(arrays), dep)
    return pl.pallas_call(
        body, name="in_bwd", grid=(L // tm,),
        in_specs=[_row_spec(tm, D_MODEL), _row_spec(tm, D_MODEL), _full_spec((1, D_MODEL)), _vmem_spec()] + specs
        + extra_specs,
        out_specs=[_row_spec(tm, D_MODEL), _full_spec((IN_COLS, D_MODEL)), _full_spec((1, D_MODEL))],
        out_shape=[jax.ShapeDtypeStruct((L, D_MODEL), F32), jax.ShapeDtypeStruct((IN_COLS, D_MODEL), F32),
                   jax.ShapeDtypeStruct((1, D_MODEL), F32)],
        compiler_params=_params(("arbitrary",), VMEM_BIG),
    )(x, dh1, g_pre, w_in_t, *arrays, *extra)


def _adamw_math(w, g, m, v):
    m = ADAM_B1 * m + (1.0 - ADAM_B1) * g
    v = ADAM_B2 * v + (1.0 - ADAM_B2) * (g * g)
    m_hat = m / (1.0 - ADAM_B1 ** ADAM_STEP)
    v_hat = v / (1.0 - ADAM_B2 ** ADAM_STEP)
    delta = -ADAM_LR * (m_hat / (jnp.sqrt(v_hat) + ADAM_EPS) + ADAM_WD * w)
    return delta, m, v


def _adamw_call(w, g, m, v, name, dep=None):
    rows, cols = w.shape
    tr = min(256, rows)

    def body(w_r, g_r, m_r, v_r, d_r, nm_r, nv_r):
        d_r[...], nm_r[...], nv_r[...] = _adamw_math(w_r[...], g_r[...], m_r[...], v_r[...])

    if rows % tr == 0:
        spec, steps = _row_spec(tr, cols), rows // tr
    else:
        spec, steps = pl.BlockSpec((rows, 256), lambda i: (0, i)), cols // 256
    body, extra, extra_specs = _after(body, 4, dep)
    return pl.pallas_call(
        body, name=name, grid=(steps,),
        in_specs=[spec] * 4 + extra_specs, out_specs=[spec] * 3,
        out_shape=[jax.ShapeDtypeStruct(w.shape, F32)] * 3,
        compiler_params=_params(("arbitrary",)),
    )(w, g, m, v, *extra)


def _position():
    return lax.axis_index("x"), lax.axis_index("y"), lax.axis_index("c")


def _other_chips(x, y):
    return [(1 - x, y), (x, 1 - y), (1 - x, 1 - y)]


ROWS, COLS = -2, -1


def _half(ref, which, axis):
    size = ref.shape[axis] // 2
    span = pl.ds(pl.multiple_of(which * size, 16 if axis == ROWS else 128), size)
    index = [slice(None)] * len(ref.shape)
    index[axis] = span
    return ref.at[tuple(index)]


def _first_gather_call(shards, axes):
    n = len(shards)

    def body(*refs):
        srcs, outs = refs[:n], refs[n:2 * n]
        send_sems, recv_sems, local_sems = refs[2 * n:]
        x, y, c = _position()
        sibling = (x, y, 1 - c)
        chips = _other_chips(x, y)
        local = [pltpu.make_async_copy(srcs[a], outs[a].at[2 * x + y], local_sems.at[a]) for a in range(n)]
        for cp in local:
            cp.start()

        def copy(a, k, block, to, src=None):
            px, py, pc = block
            dst = _half(outs[a].at[2 * px + py], pc, axes[a])
            return pltpu.make_async_remote_copy(
                src_ref=dst if src is None else src, dst_ref=dst, send_sem=send_sems.at[6 * a + k],
                recv_sem=recv_sems.at[6 * a + k], device_id=to, device_id_type=MESH_ID)

        first, passed = [], []
        for a in range(n):
            my_half = _half(srcs[a], c, axes[a])
            first += [copy(a, j, (x, y, c), (*chip, c), src=my_half) for j, chip in enumerate(chips)]
        for cp in first:
            cp.start()
        for a in range(n):
            for j, chip in enumerate(chips):
                copy(a, j, (*chip, c), (x, y, c)).wait_recv()
                passed.append(copy(a, 3 + j, (*chip, c), sibling))
                passed[-1].start()
        for a in range(n):
            for j, chip in enumerate(chips):
                copy(a, 3 + j, (*chip, 1 - c), (x, y, c)).wait_recv()
        for cp in first + passed:
            cp.wait_send()
        for cp in local:
            cp.wait()

    return pl.pallas_call(
        body, name="first_gather",
        in_specs=[_any_spec()] * n, out_specs=[_any_spec()] * n,
        out_shape=[jax.ShapeDtypeStruct((N_CHIPS,) + s.shape, s.dtype) for s in shards],
        scratch_shapes=[pltpu.SemaphoreType.DMA((6 * n,)), pltpu.SemaphoreType.DMA((6 * n,)),
                        pltpu.SemaphoreType.DMA((n,))],
    )(*shards)


def _split_start(name, arrays, n_copies, plan):
    n = len(arrays)

    def body(*refs):
        ins, send_sems, recv_sems, token = refs[:n], refs[n], refs[n + 1], refs[-1]
        for k, (src, dst, to, _) in enumerate(plan(ins)):
            pltpu.make_async_remote_copy(src_ref=src, dst_ref=dst, send_sem=send_sems.at[k],
                                         recv_sem=recv_sems.at[k], device_id=to, device_id_type=MESH_ID).start()
        token[...] = jnp.zeros_like(token)

    hbm = pl.BlockSpec(memory_space=pltpu.HBM)
    sem = pl.BlockSpec(memory_space=pltpu.SEMAPHORE)
    out = pl.pallas_call(
        body, name=name,
        out_shape=(pltpu.SemaphoreType.DMA((n_copies,)), pltpu.SemaphoreType.DMA((n_copies,)))
        + tuple(pltpu.HBM(a.shape, a.dtype) for a in arrays) + (jax.ShapeDtypeStruct((8, 128), F32),),
        in_specs=[hbm] * n, out_specs=(sem, sem) + (hbm,) * n + (_vmem_spec(),),
        input_output_aliases={i: 2 + i for i in range(n)},
        compiler_params=pltpu.CompilerParams(has_side_effects=pltpu.SideEffectType.DATAFLOW_SIDE_EFFECTING),
    )(*[pltpu.with_memory_space_constraint(a, pltpu.HBM) for a in arrays])
    return (out[0], out[1], tuple(out[2:2 + n])), out[-1]


def _split_wait(name, handle, n_copies, plan, after):
    send_sems, recv_sems, arrays = handle
    n = len(arrays)

    def body(*refs):
        ins, s_sems, r_sems = refs[:n], refs[n], refs[n + 1]
        for k, (src, dst, to, landed) in enumerate(plan(ins)):
            cp = pltpu.make_async_remote_copy(src_ref=src, dst_ref=landed, send_sem=s_sems.at[k],
                                              recv_sem=r_sems.at[k], device_id=to, device_id_type=MESH_ID)
            cp.wait_send()
            cp.wait_recv()

    hbm = pl.BlockSpec(memory_space=pltpu.HBM)
    sem = pl.BlockSpec(memory_space=pltpu.SEMAPHORE)
    out = pl.pallas_call(
        body, name=name,
        out_shape=tuple(pltpu.HBM(a.shape, a.dtype) for a in arrays),
        in_specs=[hbm] * n + [sem, sem, _any_spec()], out_specs=(hbm,) * n,
        input_output_aliases={i: i for i in range(n)},
        compiler_params=pltpu.CompilerParams(has_side_effects=pltpu.SideEffectType.DATAFLOW_SIDE_EFFECTING),
    )(*arrays, send_sems, recv_sems, after)
    return tuple(out)


def _gather_plans(axes):
    n = len(axes)

    def stage_one(refs):
        x, y, c = _position()
        copies = []
        for a, axis in enumerate(axes):
            for px, py in _other_chips(x, y):
                copies.append((_half(refs[a], c, axis), _half(refs[n + a].at[2 * x + y], c, axis),
                               (px, py, c), _half(refs[n + a].at[2 * px + py], c, axis)))
        return copies

    def stage_two(refs):
        x, y, c = _position()
        copies = []
        for a, axis in enumerate(axes):
            for px, py in _other_chips(x, y):
                piece = _half(refs[n + a].at[2 * px + py], c, axis)
                copies.append((piece, piece, (x, y, 1 - c), _half(refs[n + a].at[2 * px + py], 1 - c, axis)))
        return copies

    return stage_one, stage_two


def _pair_swap_plan(axes):
    n = len(axes)

    def plan(refs):
        x, y, c = _position()
        return [(_half(refs[a], 1 - c, axes[a]), refs[n + a], (x, y, 1 - c), refs[n + a]) for a in range(n)]

    return plan


def _chip_swap_plan(n):
    def plan(refs):
        x, y, c = _position()
        copies = []
        for a in range(n):
            for j, (px, py) in enumerate(_other_chips(x, y)):
                copies.append((refs[a].at[2 * px + py], refs[n + a].at[j], (px, py, c), refs[n + a].at[j]))
        return copies

    return plan


def _pair_join_plan(axes):
    def plan(refs):
        x, y, c = _position()
        copies = []
        for a, axis in enumerate(axes):
            mine = _half(refs[a], c, axis)
            copies.append((mine, mine, (x, y, 1 - c), _half(refs[a], 1 - c, axis)))
        return copies

    return plan


def _pair_add_call(g, got, pos, name, axis):
    rows, cols = got.shape[1], got.shape[2]
    tr = min(256, rows) if axis == ROWS else rows
    nblk = rows // tr
    if axis == ROWS:
        mine = lambda j, i, p: (j, p[1] * nblk + i, 0)
    else:
        mine = lambda j, i, p: (j, 0, p[1])

    def body(pos_r, g_r, got_r, o_r):
        o_r[...] = (g_r[...] + got_r[...]).astype(o_r.dtype)

    return pl.pallas_call(
        body, name=name,
        grid_spec=pltpu.PrefetchScalarGridSpec(
            num_scalar_prefetch=1, grid=(N_CHIPS, nblk),
            in_specs=[pl.BlockSpec((None, tr, cols), mine),
                      pl.BlockSpec((None, tr, cols), lambda j, i, p: (j, i, 0))],
            out_specs=pl.BlockSpec((None, tr, cols), lambda j, i, p: (j, i, 0))),
        out_shape=jax.ShapeDtypeStruct(got.shape, COMM_DTYPE),
        compiler_params=_params(("arbitrary", "arbitrary")),
    )(pos, g, got)


def _chip_add_call(hsum, got, pos, name, axis):
    rows, cols = hsum.shape[1], hsum.shape[2]
    tr = min(256, rows) if axis == ROWS else rows
    nblk = rows // tr
    if axis == ROWS:
        out_shape, mine = (2 * rows, cols), (lambda i, p: (p[1] * nblk + i, 0))
    else:
        out_shape, mine = (rows, 2 * cols), (lambda i, p: (0, p[1]))

    def body(pos_r, own_r, got_r, o_r):
        acc = own_r[...].astype(F32)
        for j in range(3):
            acc = acc + got_r[j].astype(F32)
        o_r[...] = acc

    return pl.pallas_call(
        body, name=name,
        grid_spec=pltpu.PrefetchScalarGridSpec(
            num_scalar_prefetch=1, grid=(nblk,),
            in_specs=[pl.BlockSpec((None, tr, cols), lambda i, p: (p[0], i, 0)),
                      pl.BlockSpec((3, tr, cols), lambda i, p: (0, i, 0))],
            out_specs=pl.BlockSpec((tr, cols), mine)),
        out_shape=jax.ShapeDtypeStruct(out_shape, F32),
        compiler_params=_params(("arbitrary",)),
    )(pos, hsum, got)


SMALL_NAMES = ("norm_mix_pre", "norm_mix_post", "norm_mlp_pre", "norm_mlp_post", "b_gate_fwd", "b_gate_bwd",
               "gla_norm", "swa_sink", "rel_bias")


def _small_update_call(grads, gate_grads, params, dep=None):
    n_dev = 8
    n_small = len(SMALL_NAMES)
    wmv = [t for p in params for t in p]
    shapes = [p[0].shape for p in params]

    def body(*refs):
        g_refs = refs[:n_small + 2]
        wmv_refs = refs[n_small + 2:n_small + 2 + 3 * n_small]
        n_in = n_small + 2 + 3 * n_small
        out_refs = refs[n_in:n_in + 4 * n_small + 2]
        pack_a, pack_b, all_a, all_b, send_sems, recv_sems = refs[n_in + 4 * n_small + 2:]
        x, y, c = _position()
        me = 4 * x + 2 * y + c
        pack_a[...] = jnp.zeros_like(pack_a)
        pack_b[...] = jnp.zeros_like(pack_b)
        for i in range(4):
            pack_a[i:i + 1, :] = g_refs[i][...]
        pack_a[4:5, 0:256] = g_refs[4][...]
        pack_a[5:6, 0:256] = g_refs[5][...]
        pack_a[6:7, 0:128] = g_refs[6][...]
        pack_a[7:8, 0:128] = g_refs[7][...]
        pack_b[0:32, 0:128] = g_refs[8][...]
        pack_b[32:48, :] = g_refs[9][...]
        pack_b[48:64, :] = g_refs[10][...]
        all_a[me] = pack_a[...]
        all_b[me] = pack_b[...]
        copies = []
        for k in range(1, n_dev):
            fx, fy, fc = (k >> 2) & 1, (k >> 1) & 1, k & 1
            to = (1 - x if fx else x, 1 - y if fy else y, 1 - c if fc else c)
            for t, (pack, dst) in enumerate(((pack_a, all_a), (pack_b, all_b))):
                copies.append(pltpu.make_async_remote_copy(
                    src_ref=pack, dst_ref=dst.at[me], send_sem=send_sems.at[2 * (k - 1) + t],
                    recv_sem=recv_sems.at[2 * (k - 1) + t], device_id=to, device_id_type=MESH_ID))
        for cp in copies:
            cp.start()
        for cp in copies:
            cp.wait()
        sum_a, sum_b = all_a[0], all_b[0]
        for d in range(1, n_dev):
            sum_a = sum_a + all_a[d]
            sum_b = sum_b + all_b[d]
        gsum = [sum_a[0:1], sum_a[1:2], sum_a[2:3], sum_a[3:4], sum_a[4:5, 0:256], sum_a[5:6, 0:256],
                sum_a[6:7, 0:128], sum_a[7:8, 0:SWA_Q_HEADS], sum_b[0:32, 0:SWA_Q_HEADS]]
        for i in range(n_small):
            w_r, m_r, v_r = wmv_refs[3 * i:3 * i + 3]
            delta, new_m, new_v = _adamw_math(w_r[...], gsum[i], m_r[...], v_r[...])
            out_refs[4 * i][...] = gsum[i]
            out_refs[4 * i + 1][...] = delta
            out_refs[4 * i + 2][...] = new_m
            out_refs[4 * i + 3][...] = new_v
        out_refs[4 * n_small][...] = sum_b[32:48]
        out_refs[4 * n_small + 1][...] = sum_b[48:64]

    n_in = n_small + 2 + 3 * n_small
    body, extra, extra_specs = _after(body, n_in, dep)
    out_shape = [jax.ShapeDtypeStruct(s, F32) for s in shapes for _ in range(4)]
    out_shape += [jax.ShapeDtypeStruct((GLA_GATE_RANK, 256), F32)] * 2
    out = pl.pallas_call(
        body, name="small_update",
        in_specs=[_vmem_spec()] * n_in + extra_specs, out_specs=[_vmem_spec()] * len(out_shape),
        out_shape=out_shape,
        scratch_shapes=[pltpu.VMEM((8, D_MODEL), F32), pltpu.VMEM((64, 256), F32),
                        pltpu.VMEM((n_dev, 8, D_MODEL), F32), pltpu.VMEM((n_dev, 64, 256), F32),
                        pltpu.SemaphoreType.DMA((2 * (n_dev - 1),)), pltpu.SemaphoreType.DMA((2 * (n_dev - 1),))],
    )(*grads, *gate_grads, *wmv, *extra)
    per_name = [tuple(out[4 * i:4 * i + 4]) for i in range(n_small)]
    return per_name, out[4 * n_small], out[4 * n_small + 1]


def _pad_heads(t, n_heads, axis=-1):
    axis = axis % t.ndim
    shape = t.shape
    t = t.reshape(shape[:axis] + (n_heads, 64) + shape[axis + 1:])
    pad = [(0, 0)] * t.ndim
    pad[axis + 1] = (0, HEAD_PAD - 64)
    return jnp.pad(t, pad).reshape(shape[:axis] + (n_heads * HEAD_PAD,) + shape[axis + 1:])


def _unpad_heads(t, n_heads, axis=-1):
    axis = axis % t.ndim
    shape = t.shape
    t = t.reshape(shape[:axis] + (n_heads, HEAD_PAD) + shape[axis + 1:])
    t = lax.slice_in_dim(t, 0, 64, axis=axis + 1)
    return t.reshape(shape[:axis] + (n_heads * 64,) + shape[axis + 1:])


def _pad_gate(w, first_row):
    return jnp.pad(_pad_heads(w, 4), ((first_row, 128 - GLA_GATE_RANK - first_row), (0, 0)))


def _own_slot(shard, chip):
    zone = lax.empty((N_CHIPS,) + shard.shape, shard.dtype)
    return lax.dynamic_update_slice(zone, shard[None], (chip,) + (0,) * shard.ndim)


def _reduce_to_owners(grads, axes, pos, tag, overlap):
    n = len(grads)

    def half_shape(g, axis):
        return (N_CHIPS, g.shape[1] // 2, g.shape[2]) if axis == ROWS else (N_CHIPS, g.shape[1], g.shape[2] // 2)

    lands = [lax.empty(half_shape(g, axis), F32) for g, axis in zip(grads, axes)]
    handle, token = _split_start(tag + "_pair_start", list(grads) + lands, n, _pair_swap_plan(axes))
    got = _split_wait(tag + "_pair_wait", handle, n, _pair_swap_plan(axes), overlap[0](token))
    sums = [_pair_add_call(got[a], got[n + a], pos, f"{tag}_pair_add{a}", axes[a]) for a in range(n)]
    lands = [lax.empty((3,) + s.shape[1:], s.dtype) for s in sums]
    handle, token = _split_start(tag + "_chip_start", sums + lands, 3 * n, _chip_swap_plan(n))
    got = _split_wait(tag + "_chip_wait", handle, 3 * n, _chip_swap_plan(n), overlap[1](token))
    halves = [_chip_add_call(got[a], got[n + a], pos, f"{tag}_chip_add{a}", axes[a]) for a in range(n)]
    handle, token = _split_start(tag + "_join_start", halves, n, _pair_join_plan(axes))
    return _split_wait(tag + "_join_wait", handle, n, _pair_join_plan(axes), overlap[2](token))


def kernel(x, norm_mix_pre, w_in, w_gate_up_fwd, b_gate_fwd, w_gate_up_bwd, b_gate_bwd, gla_norm, swa_sink, rel_bias, w_out, norm_mix_post, norm_mlp_pre, w_up, w_down, norm_mlp_post, loss_target, m_norm_mix_pre, m_w_in, m_w_gate_up_fwd, m_b_gate_fwd, m_w_gate_up_bwd, m_b_gate_bwd, m_gla_norm, m_swa_sink, m_rel_bias, m_w_out, m_norm_mix_post, m_norm_mlp_pre, m_w_up, m_w_down, m_norm_mlp_post, v_norm_mix_pre, v_w_in, v_w_gate_up_fwd, v_b_gate_fwd, v_w_gate_up_bwd, v_b_gate_bwd, v_gla_norm, v_swa_sink, v_rel_bias, v_w_out, v_norm_mix_post, v_norm_mlp_pre, v_w_up, v_w_down, v_norm_mlp_post):
    given = dict(locals())
    cx, cy, cc = _position()
    chip = (2 * cx + cy).astype(jnp.int32)
    pos = jnp.stack([chip, cc.astype(jnp.int32)])
    seq, tgt = x[0], loss_target[0]
    L = seq.shape[0]

    gates = jnp.concatenate([w_gate_up_fwd[0], w_gate_up_bwd[0]], axis=0).astype(COMM_DTYPE)
    all_in, all_gates = _first_gather_call([w_in[0].T.astype(COMM_DTYPE), gates], [COLS, ROWS])
    rest = [w_out[0].astype(COMM_DTYPE), jnp.stack([w_up[0], w_down[0]]).astype(COMM_DTYPE)]
    stage_one, stage_two = _gather_plans([ROWS, ROWS])
    handle, token = _split_start("gather_chip_start", rest + [_own_slot(s, chip) for s in rest] + [all_gates], 6,
                                 stage_one)

    w_in_t = _mx(all_in.reshape(IN_COLS, D_MODEL))
    gates_full = jnp.concatenate([all_gates[j] for j in range(N_CHIPS)], axis=1)
    wgf_p = _mx(_pad_gate(gates_full[:GLA_GATE_RANK], 0))
    wgb_p = _mx(_pad_gate(gates_full[GLA_GATE_RANK:], GLA_GATE_RANK))
    bf_p, bb_p = _pad_heads(b_gate_fwd, 4), _pad_heads(b_gate_bwd, 4)
    buckets = jnp.asarray(_band_buckets())
    bias = _bias_call(rel_bias, buckets)
    sink1 = swa_sink.reshape(SWA_Q_HEADS)

    qa, ka, va, ga, qs, ks, vs, za = _proj_call(seq, norm_mix_pre, w_in_t, dep=token)
    halo = ((SWA_BLOCK, SWA_BLOCK), (0, 0))
    ks_p, vs_p = jnp.pad(ks, halo), jnp.pad(vs, halo)
    o_f, o_b, s_f, s_b = _gla_fwd_call(qa, ka, va, za, wgf_p, bf_p, wgb_p, bb_p)
    arrays = _split_wait("gather_chip_wait", handle, 6, stage_one, o_f)
    handle, token = _split_start("gather_pair_start", list(arrays), 6, stage_two)
    o_s = _swa_fwd_call(qs, ks_p, vs_p, bias, sink1, dep=token)
    arrays = _split_wait("gather_pair_wait", handle, 6, stage_two, o_s)
    w_out_full = _mx(arrays[2].reshape(N_CHIPS * R_OUT, D_MODEL))
    w_ud = _mx(arrays[3])
    cat, mix, h1, n2 = _mix_call(o_f, o_b, ga, o_s, seq, gla_norm, w_out_full, norm_mix_post, norm_mlp_pre)
    a, rz, dh2, dff, loss, d_post2 = _mlp_fwd_call(n2, h1, tgt, w_ud, norm_mlp_post)

    dz, dn2 = _mlp_bwd_call(dff, rz, w_ud)
    dw_down, dw_up4 = _mlp_wgrad_call(a, dff, n2, dz)
    dh1, do, dga, dos, dw_out, d_pre2, d_post, d_gn = _mix_bwd_call(
        dn2, dh2, h1, mix, cat, o_f, o_b, ga, gla_norm, norm_mix_post, norm_mlp_pre, w_out_full)
    done = {}

    def gla_backward(tok):
        done["gla"] = _gla_bwd_call(qa, ka, va, za, do, s_f, s_b, wgf_p, bf_p, wgb_p, bb_p, dep=tok)
        return done["gla"][0]

    def swa_backward(tok):
        done["swa"] = _swa_bwd_call(qs, ks_p, vs_p, bias, sink1, dos, dep=tok)
        return done["swa"][0]

    def in_backward(tok):
        dqf, dkf, dvf, dzf, _, _, dqb, dkb, dvb, dzb, _, _ = done["gla"]
        dqs, dks_p, dvs_p, _, _ = done["swa"]
        dks = dks_p[SWA_BLOCK:SWA_BLOCK + L]
        dvs = dvs_p[SWA_BLOCK:SWA_BLOCK + L]
        done["in"] = _in_bwd_call(
            seq, dh1, norm_mix_pre, w_in_t,
            pairs=[(T_QA, (dqf, dqb)), (T_KA, (dkf, dkb)), (T_VA, (dvf, dvb)), (T_ZA, (dzf, dzb))],
            singles=[(T_GA, dga), (T_QS, dqs), (T_KS, dks), (T_VS, dvs)], dep=tok)
        return done["in"][0]

    g_up, g_down, g_out = _reduce_to_owners(
        [dw_up4, dw_down.reshape(N_CHIPS, R_DOWN, D_MODEL), dw_out.reshape(N_CHIPS, R_OUT, D_MODEL)],
        [ROWS, ROWS, ROWS], pos, "mlp", [swa_backward, gla_backward, in_backward])
    dx, dw_in_t, d_pre = done["in"]
    dwf, dbf, dwb, dbb = done["gla"][4], done["gla"][5], done["gla"][10], done["gla"][11]
    drel, dsink = _relbias_call(done["swa"][3], done["swa"][4], buckets)

    small_grads = [d_pre, d_post, d_pre2, d_post2, _unpad_heads(dbf, 4), _unpad_heads(dbb, 4), d_gn, dsink, drel]
    gate_grads = [_unpad_heads(dwf[:GLA_GATE_RANK], 4), _unpad_heads(dwb[GLA_GATE_RANK:2 * GLA_GATE_RANK], 4)]
    small_params = [(given[n], given["m_" + n], given["v_" + n]) for n in SMALL_NAMES]
    upd = {}

    def update_up(tok):
        upd["w_up"] = (g_up,) + tuple(_adamw_call(w_up[0], g_up, m_w_up[0], v_w_up[0], "adamw_w_up", dep=tok))
        return upd["w_up"][1]

    def update_small(tok):
        per_name, gf_sum, gb_sum = _small_update_call(small_grads, gate_grads, small_params, dep=tok)
        upd.update(dict(zip(SMALL_NAMES, per_name)))
        for name, total in (("w_gate_up_fwd", gf_sum), ("w_gate_up_bwd", gb_sum)):
            g = lax.dynamic_slice(total, (0, chip * 64), (GLA_GATE_RANK, 64))
            upd[name] = (g,) + tuple(_adamw_call(given[name][0], g, given["m_" + name][0], given["v_" + name][0],
                                                 "adamw_" + name))
        upd["w_down"] = (g_down,) + tuple(
            _adamw_call(w_down[0], g_down, m_w_down[0], v_w_down[0], "adamw_w_down", dep=gf_sum))
        return upd["w_down"][1]

    def update_out(tok):
        upd["w_out"] = (g_out,) + tuple(_adamw_call(w_out[0], g_out, m_w_out[0], v_w_out[0], "adamw_w_out", dep=tok))
        return upd["w_out"][1]

    (g_in_t,) = _reduce_to_owners([dw_in_t.reshape(N_CHIPS, R_IN, D_MODEL)], [COLS], pos, "in",
                                  [update_up, update_small, update_out])
    in_t = (g_in_t,) + tuple(_adamw_call(w_in[0].T, g_in_t, m_w_in[0].T, v_w_in[0].T, "adamw_w_in"))
    upd["w_in"] = tuple(t.T for t in in_t)

    big = ("w_in", "w_gate_up_fwd", "w_gate_up_bwd", "w_out", "w_up", "w_down")
    names = ["norm_mix_pre", "w_in", "w_gate_up_fwd", "b_gate_fwd", "w_gate_up_bwd", "b_gate_bwd", "gla_norm",
             "swa_sink", "rel_bias", "w_out", "norm_mix_post", "norm_mlp_pre", "w_up", "w_down", "norm_mlp_post"]
    outs = [lax.psum(loss[0, 0], MESH_AXES), dx[None]]
    for kind in range(4):
        outs += [upd[n][kind][None] if n in big else upd[n][kind] for n in names]
    return tuple(outs)
```

```python
import math

import numpy as np
import jax
import jax.numpy as jnp
from jax import lax
from jax.experimental import pallas as pl
from jax.experimental.pallas import tpu as pltpu

F32 = jnp.float32
MXU_DTYPE = jnp.bfloat16
COMM_DTYPE = jnp.bfloat16

D_MODEL = 1024
D_FF = 4096
N_CHIPS = 4
GLA_HEADS = 4
GLA_CHUNK = 64
GLA_GATE_RANK = 16
GLA_GATE_NORM = 16.0
SWA_Q_HEADS = 8
SWA_KV_HEADS = 2
SWA_BLOCK = 128
REL_BUCKETS = 32
REL_MAX_DIST = 128
NORM_EPS = 1e-6
HEAD_PAD = 128

ADAM_LR = 0.001
ADAM_B1 = 0.9
ADAM_B2 = 0.999
ADAM_EPS = 1e-08
ADAM_WD = 0.01
ADAM_STEP = 10

OUT_PAD = 1024

R_IN, R_OUT, R_UP, R_DOWN = 584, 256, 1024, 1024

VMEM_BIG = 56 * 1024 * 1024
MESH_AXES = ("x", "y", "c")
MESH_ID = pl.DeviceIdType.MESH


def _mx(a):
    return a.astype(MXU_DTYPE)


def _dot(a, b):
    return jnp.dot(a, b, preferred_element_type=F32)


def _dot_nt(a, b):
    return lax.dot_general(a, b, (((1,), (1,)), ((), ())), preferred_element_type=F32)


def _dot_tn(a, b):
    return lax.dot_general(a, b, (((0,), (0,)), ((), ())), preferred_element_type=F32)


def _rms_r(x):
    return lax.rsqrt(jnp.mean(x * x, axis=-1, keepdims=True) + NORM_EPS)


def _rms_bwd(x, r, g, dy):
    xh = x * r
    gdy = dy * g
    dx = r * (gdy - xh * jnp.mean(gdy * xh, axis=-1, keepdims=True))
    return dx, jnp.sum(dy * xh, axis=0, keepdims=True)


def _low_half(rows):
    return lax.broadcasted_iota(jnp.int32, (rows, HEAD_PAD), 1) < 64


def _spread_heads(x):
    low = _low_half(x.shape[0])
    parts = []
    for p in range(x.shape[1] // HEAD_PAD):
        pair = x[:, HEAD_PAD * p:HEAD_PAD * (p + 1)]
        parts += [jnp.where(low, pair, 0.0), jnp.where(low, pltpu.roll(pair, 64, 1), 0.0)]
    return jnp.concatenate(parts, axis=1)


def _squeeze_heads(x):
    low = _low_half(x.shape[0])
    parts = []
    for p in range(x.shape[1] // (2 * HEAD_PAD)):
        even = x[:, 2 * HEAD_PAD * p:2 * HEAD_PAD * p + HEAD_PAD]
        odd = x[:, 2 * HEAD_PAD * p + HEAD_PAD:2 * HEAD_PAD * (p + 1)]
        parts.append(jnp.where(low, even, pltpu.roll(odd, 64, 1)))
    return parts[0] if len(parts) == 1 else jnp.concatenate(parts, axis=1)


def _params(sem=None, vmem=None):
    kw = {}
    if sem is not None:
        kw["dimension_semantics"] = sem
    if vmem is not None:
        kw["vmem_limit_bytes"] = vmem
    return pltpu.CompilerParams(**kw)


def _vmem_spec():
    return pl.BlockSpec(memory_space=pltpu.VMEM)


def _row_spec(tm, width):
    return pl.BlockSpec((tm, width), lambda i: (i, 0))


def _full_spec(shape):
    return pl.BlockSpec(shape, lambda i: (0,) * len(shape))


def _any_spec():
    return pl.BlockSpec(memory_space=pl.ANY)


def _after(body, n_in, dep):
    if dep is None:
        return body, [], []
    return (lambda *refs: body(*refs[:n_in], *refs[n_in + 1:])), [dep], [_any_spec()]


T_QA, T_KA, T_VA, T_GA = (0, 256, 4), (256, 256, 4), (512, 512, 0), (1024, 512, 0)
T_QS, T_KS, T_VS = (1568, 512, 8), (2080, 128, 2), (2208, 128, 2)
T_ZA = (1536, 128, 0)
ZA_COLS = 2 * GLA_GATE_RANK
IN_COLS = 2336


def _proj_call(x, g_pre, w_in_t, dep=None):
    L = x.shape[0]
    tm = min(256, L)
    groups = [(T_QA, F32), (T_KA, F32), (T_VA, MXU_DTYPE), (T_GA, F32),
              (T_QS, MXU_DTYPE), (T_KS, MXU_DTYPE), (T_VS, MXU_DTYPE), (T_ZA, F32)]
    widths = [rows * (2 if heads else 1) for (_, rows, heads), _ in groups]

    def body(x_ref, g_ref, w_ref, *outs):
        xv = x_ref[...]
        u = _mx(xv * _rms_r(xv) * g_ref[...])
        for ref, (grp, _) in zip(outs, groups):
            first, rows, heads = grp
            val = _dot_nt(u, w_ref[first:first + rows, :])
            if heads:
                val = _spread_heads(val)
            if grp is T_ZA:
                val = jnp.where(lax.broadcasted_iota(jnp.int32, val.shape, 1) < ZA_COLS, val, 0.0)
            ref[...] = val.astype(ref.dtype)

    body, extra, extra_specs = _after(body, 3, dep)
    return pl.pallas_call(
        body, name="proj_fwd", grid=(L // tm,),
        in_specs=[_row_spec(tm, D_MODEL), _full_spec((1, D_MODEL)), _vmem_spec()] + extra_specs,
        out_specs=[_row_spec(tm, w) for w in widths],
        out_shape=[jax.ShapeDtypeStruct((L, w), dt) for w, (_, dt) in zip(widths, groups)],
        compiler_params=_params(("arbitrary",), VMEM_BIG),
    )(x, g_pre, w_in_t, *extra)


def _tri_masks():
    row = lax.broadcasted_iota(jnp.int32, (GLA_CHUNK, GLA_CHUNK), 0)
    col = lax.broadcasted_iota(jnp.int32, (GLA_CHUNK, GLA_CHUNK), 1)
    return row >= col, row <= col


def _chunk_sums(tri_m, x):
    hi = _mx(x)
    rest = x - hi.astype(F32)
    mid = _mx(rest)
    lo = _mx(rest - mid.astype(F32))
    return _dot(tri_m, hi) + _dot(tri_m, mid) + _dot(tri_m, lo)


def _gla_block_pre(q_r, k_r, z_r, w_r, b_r, rev, nc, qd_s, ki_s, ks_s, dec_s, keep=None):
    tri_f, tri_b = _tri_masks()
    tri_m = _mx((tri_b if rev else tri_f).astype(F32))
    g = _dot(_mx(z_r[...]), w_r[...]) + b_r[...]
    la = (jnp.minimum(g, 0.0) - jnp.log(1.0 + jnp.exp(-jnp.abs(g)))) / GLA_GATE_NORM
    sums, lasts = [], []
    for c in range(nc):
        b_c = _chunk_sums(tri_m, la[GLA_CHUNK * c:GLA_CHUNK * (c + 1)])
        blast = b_c[0:1] if rev else b_c[GLA_CHUNK - 1:GLA_CHUNK]
        dec_s[c] = jnp.exp(blast)
        sums.append(b_c)
        lasts.append(jnp.broadcast_to(blast, b_c.shape))
    b = jnp.concatenate(sums, axis=0)
    eb = jnp.exp(b)
    enb = jnp.exp(-b)
    elb = jnp.exp(jnp.concatenate(lasts, axis=0) - b)
    k = k_r[...]
    qd_s[...] = (q_r[...] * 0.125 * eb).astype(qd_s.dtype)
    ki_s[...] = (k * enb).astype(ki_s.dtype)
    ks_s[...] = (k * elb).astype(ks_s.dtype)
    if keep is not None:
        for ref, val in zip(keep, (g, eb, enb, elb)):
            ref[...] = val


def _gla_fwd_call(qa, ka, va, za, wgf, bgf, wgb, bgb):
    L = qa.shape[0]
    br = min(512, L)
    nb, nc, n_chunks = L // br, br // GLA_CHUNK, L // GLA_CHUNK
    hw = GLA_HEADS * HEAD_PAD

    def body(qaf, kaf, vaf, zaf, qab, kab, vab, zab, wgf_r, bgf_r, wgb_r, bgb_r,
             of_r, ob_r, sf_r, sb_r, st_f, st_b, pre_f, pre_b):
        @pl.when(pl.program_id(0) == 0)
        def _():
            st_f[...] = jnp.zeros_like(st_f)
            st_b[...] = jnp.zeros_like(st_b)

        _gla_block_pre(qaf, kaf, zaf, wgf_r, bgf_r, False, nc, *pre_f)
        _gla_block_pre(qab, kab, zab, wgb_r, bgb_r, True, nc, *pre_b)
        tri_f, tri_b = _tri_masks()

        def one(tri, pre, v_r, o_r, s_r, st, ci):
            qd_s, ki_s, ks_s, dec_s = pre
            rows = pl.ds(pl.multiple_of(ci * GLA_CHUNK, GLA_CHUNK), GLA_CHUNK)
            dec = dec_s[ci]
            for h in range(GLA_HEADS):
                sl = slice(HEAD_PAD * h, HEAD_PAD * (h + 1))
                qd = qd_s[rows, sl]
                a = jnp.where(tri, _dot_nt(qd, ki_s[rows, sl]), 0.0)
                v = v_r[rows, sl]
                s_t = st[h]
                s_r[ci, h] = s_t
                o_r[rows, sl] = _dot(_mx(a), v) + _dot_nt(qd, _mx(s_t))
                st[h] = s_t * dec[:, sl] + _dot_tn(v, ks_s[rows, sl])

        def loop(t, carry):
            one(tri_f, pre_f, vaf, of_r, sf_r, st_f, t)
            one(tri_b, pre_b, vab, ob_r, sb_r, st_b, nc - 1 - t)
            return carry

        lax.fori_loop(0, nc, loop, 0, unroll=True)

    fwd = lambda i: (i, 0)
    bwd = lambda i: (nb - 1 - i, 0)
    ins = lambda m: [pl.BlockSpec((br, hw), m), pl.BlockSpec((br, hw), m),
                     pl.BlockSpec((br, hw), m), pl.BlockSpec((br, 128), m)]
    wspecs = [_full_spec((128, hw)), _full_spec((1, hw))] * 2
    s_shape = (nc, GLA_HEADS, HEAD_PAD, HEAD_PAD)
    pre_scratch = [pltpu.VMEM((br, hw), MXU_DTYPE)] * 3 + [pltpu.VMEM((nc, 1, hw), F32)]
    return pl.pallas_call(
        body, name="gla_fwd", grid=(nb,),
        in_specs=ins(fwd) + ins(bwd) + wspecs,
        out_specs=[pl.BlockSpec((br, hw), fwd), pl.BlockSpec((br, hw), bwd),
                   pl.BlockSpec(s_shape, lambda i: (i, 0, 0, 0)),
                   pl.BlockSpec(s_shape, lambda i: (nb - 1 - i, 0, 0, 0))],
        out_shape=[jax.ShapeDtypeStruct((L, hw), F32), jax.ShapeDtypeStruct((L, hw), F32),
                   jax.ShapeDtypeStruct((n_chunks,) + s_shape[1:], F32),
                   jax.ShapeDtypeStruct((n_chunks,) + s_shape[1:], F32)],
        scratch_shapes=[pltpu.VMEM(s_shape[1:], F32), pltpu.VMEM(s_shape[1:], F32), pre_scratch, pre_scratch],
        compiler_params=_params(("arbitrary",), VMEM_BIG),
    )(qa, ka, va, za, qa, ka, va, za, wgf, bgf, wgb, bgb)


def _gla_bwd_call(qa, ka, va, za, do, sf, sb, wgf, bgf, wgb, bgb, dep=None):
    L = qa.shape[0]
    br = min(256, L)
    nb, nc = L // br, br // GLA_CHUNK
    hw = GLA_HEADS * HEAD_PAD

    def body(qaf, kaf, vaf, zaf, dof, sf_r, qab, kab, vab, zab, dob, sb_r, wgf_r, bgf_r, wgb_r, bgb_r,
             dqf, dkf, dvf, dzf, dwf, dbf, dqb, dkb, dvb, dzb, dwb, dbb, gt_f, gt_b, pre_f, pre_b):
        @pl.when(pl.program_id(0) == 0)
        def _():
            for ref in (gt_f, gt_b, dwf, dbf, dwb, dbb):
                ref[...] = jnp.zeros_like(ref)

        _gla_block_pre(qaf, kaf, zaf, wgf_r, bgf_r, False, nc, *pre_f[:4], keep=pre_f[4:8])
        _gla_block_pre(qab, kab, zab, wgb_r, bgb_r, True, nc, *pre_b[:4], keep=pre_b[4:8])
        tri_f, tri_b = _tri_masks()
        row_w = lax.broadcasted_iota(jnp.int32, (GLA_CHUNK, HEAD_PAD), 0)

        def one(rev, pre, q_r, k_r, v_r, do_r, s_r, dq_r, dk_r, dv_r, gt, ci):
            qd_s, ki_s, ks_s, dec_s, _, eb_s, enb_s, elb_s, db_s = pre
            tri = tri_b if rev else tri_f
            last_row = 0 if rev else GLA_CHUNK - 1
            rows = pl.ds(pl.multiple_of(ci * GLA_CHUNK, GLA_CHUNK), GLA_CHUNK)
            dec = dec_s[ci]
            for h in range(GLA_HEADS):
                sl = slice(HEAD_PAD * h, HEAD_PAD * (h + 1))
                qd, ki, ks = qd_s[rows, sl], ki_s[rows, sl], ks_s[rows, sl]
                a = _mx(jnp.where(tri, _dot_nt(qd, ki), 0.0))
                v = v_r[rows, sl]
                do_h = _mx(do_r[rows, sl])
                s_t = s_r[ci, h]
                g_t = gt[h]
                g_m = _mx(g_t)
                da = _mx(jnp.where(tri, _dot_nt(do_h, v), 0.0))
                dv_r[rows, sl] = _dot_tn(a, do_h) + _dot_nt(ks, g_m)
                dqd = _dot(da, ki) + _dot(do_h, _mx(s_t))
                dki = _dot_tn(da, qd)
                dks = _dot(v, g_m)
                ddec = jnp.sum(g_t * s_t, axis=0, keepdims=True)
                gt[h] = g_t * dec[:, sl] + _dot_tn(do_h, qd)
                dq = dqd * eb_s[rows, sl] * 0.125
                dk_state = dks * elb_s[rows, sl]
                dk = dki * enb_s[rows, sl] + dk_state
                dq_r[rows, sl] = dq
                dk_r[rows, sl] = dk
                k = k_r[rows, sl]
                dblast = jnp.sum(dk_state * k, axis=0, keepdims=True) + dec[:, sl] * ddec
                db_s[rows, sl] = q_r[rows, sl] * dq - k * dk + jnp.where(row_w == last_row, dblast, 0.0)

        def loop(t, carry):
            one(False, pre_f, qaf, kaf, vaf, dof, sf_r, dqf, dkf, dvf, gt_f, nc - 1 - t)
            one(True, pre_b, qab, kab, vab, dob, sb_r, dqb, dkb, dvb, gt_b, t)
            return carry

        lax.fori_loop(0, nc, loop, 0, unroll=True)

        def gate_grads(rev, pre, z_r, w_r, dz_r, dw_r, dbias_r):
            g_s, db_s = pre[4], pre[8]
            back_m = _mx((tri_f if rev else tri_b).astype(F32))
            db = db_s[...]
            dla = jnp.concatenate([_chunk_sums(back_m, db[GLA_CHUNK * c:GLA_CHUNK * (c + 1)]) for c in range(nc)],
                                  axis=0)
            dg = dla * (1.0 / GLA_GATE_NORM) * (1.0 / (1.0 + jnp.exp(g_s[...])))
            dg_m = _mx(dg)
            dz_r[...] = _dot_nt(dg_m, w_r[...])
            dw_r[...] += _dot_tn(_mx(z_r[...]), dg_m)
            dbias_r[...] += jnp.sum(dg, axis=0, keepdims=True)

        gate_grads(False, pre_f, zaf, wgf_r, dzf, dwf, dbf)
        gate_grads(True, pre_b, zab, wgb_r, dzb, dwb, dbb)

    last_first = lambda i: (nb - 1 - i, 0)
    first_last = lambda i: (i, 0)
    s_shape = (nc, GLA_HEADS, HEAD_PAD, HEAD_PAD)

    def ins(m):
        return [pl.BlockSpec((br, hw), m), pl.BlockSpec((br, hw), m), pl.BlockSpec((br, hw), m),
                pl.BlockSpec((br, 128), m), pl.BlockSpec((br, hw), m),
                pl.BlockSpec(s_shape, lambda i: m(i) + (0, 0))]

    def outs(m):
        return [pl.BlockSpec((br, hw), m), pl.BlockSpec((br, hw), m), pl.BlockSpec((br, hw), m),
                pl.BlockSpec((br, 128), m), _full_spec((128, hw)), _full_spec((1, hw))]

    out_shape = [jax.ShapeDtypeStruct((L, hw), F32)] * 3 + [
        jax.ShapeDtypeStruct((L, 128), F32), jax.ShapeDtypeStruct((128, hw), F32),
        jax.ShapeDtypeStruct((1, hw), F32)]
    wspecs = [_full_spec((128, hw)), _full_spec((1, hw))] * 2
    body, extra, extra_specs = _after(body, 16, dep)
    pre_scratch = ([pltpu.VMEM((br, hw), MXU_DTYPE)] * 3 + [pltpu.VMEM((nc, 1, hw), F32)]
                   + [pltpu.VMEM((br, hw), F32)] * 5)
    return pl.pallas_call(
        body, name="gla_bwd", grid=(nb,),
        in_specs=ins(last_first) + ins(first_last) + wspecs + extra_specs,
        out_specs=outs(last_first) + outs(first_last),
        out_shape=out_shape + out_shape,
        scratch_shapes=[pltpu.VMEM(s_shape[1:], F32), pltpu.VMEM(s_shape[1:], F32), pre_scratch, pre_scratch],
        compiler_params=_params(("arbitrary",), VMEM_BIG),
    )(qa, ka, va, za, do, sf, qa, ka, va, za, do, sb, wgf, bgf, wgb, bgb, *extra)


def _t5_buckets(rel):
    nb = REL_BUCKETS // 2
    ret = (rel > 0).astype(np.int32) * nb
    n = np.abs(rel)
    max_exact = nb // 2
    large = max_exact + (np.log(np.maximum(n, 1).astype(np.float32) / max_exact)
                         / math.log(REL_MAX_DIST / max_exact) * (nb - max_exact)).astype(np.int32)
    large = np.minimum(large, nb - 1)
    return ret + np.where(n < max_exact, n, large)


SWA_GROUP = SWA_Q_HEADS // SWA_KV_HEADS
SWA_SPAN = 3 * SWA_BLOCK
SWA_GROUP_LANES = SWA_GROUP * SWA_BLOCK


def _band_buckets():
    s = np.arange(SWA_SPAN)[:, None]
    c = np.arange(SWA_BLOCK)[None, :]
    return _t5_buckets(s - SWA_BLOCK - c).astype(np.int32)


def _swa_valid(n, seq_len):
    s = lax.broadcasted_iota(jnp.int32, (SWA_SPAN, SWA_GROUP_LANES), 0)
    c = lax.broadcasted_iota(jnp.int32, (SWA_SPAN, SWA_GROUP_LANES), 1) & (SWA_BLOCK - 1)
    rel = s - SWA_BLOCK - c
    key_pos = (n - 1) * SWA_BLOCK + s
    return (jnp.abs(rel) <= SWA_BLOCK) & (key_pos >= 0) & (key_pos < seq_len)


def _swa_sink_row(sink_r, kv):
    lane = lax.broadcasted_iota(jnp.int32, (1, SWA_GROUP_LANES), 1)
    row = jnp.full((1, SWA_GROUP_LANES), sink_r[kv * SWA_GROUP], F32)
    for g in range(1, SWA_GROUP):
        row = jnp.where(lane >= g * SWA_BLOCK, sink_r[kv * SWA_GROUP + g], row)
    return row


def _swa_group(ref, kv):
    first = kv * SWA_GROUP
    return jnp.concatenate([ref[:, HEAD_PAD * h:HEAD_PAD * (h + 1)] for h in range(first, first + SWA_GROUP)],
                           axis=0)


def _swa_probs(kk, qg, bias_t, sink_row, valid):
    st = _dot_nt(kk, qg) * 0.125 + bias_t
    st = jnp.where(valid, st, -1e30)
    m = jnp.maximum(jnp.max(st, axis=0, keepdims=True), sink_row)
    p = jnp.exp(st - m)
    e_sink = jnp.exp(sink_row - m)
    inv = 1.0 / (jnp.sum(p, axis=0, keepdims=True) + e_sink)
    return p * inv, e_sink * inv


def _swa_fwd_call(qs, ks, vs, bias, sink, dep=None):
    L = qs.shape[0]

    def body(q_r, k_r, v_r, bias_r, sink_r, o_r):
        n = pl.program_id(0)
        span = pl.ds(pl.multiple_of(n * SWA_BLOCK, SWA_BLOCK), SWA_SPAN)
        valid = _swa_valid(n, L)
        for kv in range(SWA_KV_HEADS):
            ksl = slice(HEAD_PAD * kv, HEAD_PAD * (kv + 1))
            pn, _ = _swa_probs(k_r[span, ksl], _swa_group(q_r, kv), bias_r[kv], _swa_sink_row(sink_r, kv), valid)
            og = _dot_tn(_mx(pn), v_r[span, ksl])
            low = _low_half(SWA_BLOCK)
            for pair in range(SWA_GROUP // 2):
                even = og[2 * SWA_BLOCK * pair:2 * SWA_BLOCK * pair + SWA_BLOCK]
                odd = og[2 * SWA_BLOCK * pair + SWA_BLOCK:2 * SWA_BLOCK * (pair + 1)]
                first = HEAD_PAD * (kv * SWA_GROUP // 2 + pair)
                o_r[:, first:first + HEAD_PAD] = jnp.where(low, even, pltpu.roll(odd, 64, 1)).astype(o_r.dtype)

    qw = SWA_Q_HEADS * HEAD_PAD
    body, extra, extra_specs = _after(body, 5, dep)
    return pl.pallas_call(
        body, name="swa_fwd", grid=(L // SWA_BLOCK,),
        in_specs=[_row_spec(SWA_BLOCK, qw), _vmem_spec(), _vmem_spec(), _vmem_spec(),
                  pl.BlockSpec(memory_space=pltpu.SMEM)] + extra_specs,
        out_specs=_row_spec(SWA_BLOCK, qw // 2),
        out_shape=jax.ShapeDtypeStruct((L, qw // 2), MXU_DTYPE),
        compiler_params=_params(("arbitrary",), VMEM_BIG),
    )(qs, ks, vs, bias, sink, *extra)


def _swa_bwd_call(qs, ks, vs, bias, sink, do, dep=None):
    L = qs.shape[0]
    qw = SWA_Q_HEADS * HEAD_PAD
    kw = SWA_KV_HEADS * HEAD_PAD

    def body(q_r, k_r, v_r, bias_r, sink_r, do_r, dq_r, dk_r, dv_r, dbias_r, dsink_r):
        n = pl.program_id(0)

        @pl.when(n == 0)
        def _():
            for ref in (dk_r, dv_r, dbias_r, dsink_r):
                ref[...] = jnp.zeros_like(ref)

        span = pl.ds(pl.multiple_of(n * SWA_BLOCK, SWA_BLOCK), SWA_SPAN)
        valid = _swa_valid(n, L)
        for kv in range(SWA_KV_HEADS):
            ksl = slice(HEAD_PAD * kv, HEAD_PAD * (kv + 1))
            kk = k_r[span, ksl]
            vv = v_r[span, ksl]
            qg = _swa_group(q_r, kv)
            dog = _swa_group(do_r, kv)
            pn, p_sink = _swa_probs(kk, qg, bias_r[kv], _swa_sink_row(sink_r, kv), valid)
            dp = _dot_nt(vv, dog)
            delta = jnp.sum(pn * dp, axis=0, keepdims=True)
            ds = pn * (dp - delta)
            dsink_r[kv] -= p_sink * delta
            dbias_r[kv] += ds
            ds_m = _mx(ds)
            dqg = _dot_tn(ds_m, kk) * 0.125
            for g in range(SWA_GROUP):
                h = kv * SWA_GROUP + g
                dq_r[:, HEAD_PAD * h:HEAD_PAD * (h + 1)] = dqg[SWA_BLOCK * g:SWA_BLOCK * (g + 1)]
            dk_r[span, ksl] += _dot(ds_m, qg) * 0.125
            dv_r[span, ksl] += _dot(_mx(pn), dog)

    body, extra, extra_specs = _after(body, 6, dep)
    return pl.pallas_call(
        body, name="swa_bwd", grid=(L // SWA_BLOCK,),
        in_specs=[_row_spec(SWA_BLOCK, qw), _vmem_spec(), _vmem_spec(), _vmem_spec(),
                  pl.BlockSpec(memory_space=pltpu.SMEM), _row_spec(SWA_BLOCK, qw)] + extra_specs,
        out_specs=[_row_spec(SWA_BLOCK, qw), _vmem_spec(), _vmem_spec(), _vmem_spec(), _vmem_spec()],
        out_shape=[jax.ShapeDtypeStruct((L, qw), F32),
                   jax.ShapeDtypeStruct((L + 2 * SWA_BLOCK, kw), F32),
                   jax.ShapeDtypeStruct((L + 2 * SWA_BLOCK, kw), F32),
                   jax.ShapeDtypeStruct((SWA_KV_HEADS, SWA_SPAN, SWA_GROUP_LANES), F32),
                   jax.ShapeDtypeStruct((SWA_KV_HEADS, 1, SWA_GROUP_LANES), F32)],
        compiler_params=_params(("arbitrary",), VMEM_BIG),
    )(qs, ks, vs, bias, sink, do, *extra)


def _bias_call(rel_bias, buckets):
    def body(t_r, bk_r, o_r):
        bk = bk_r[...]
        for h in range(SWA_Q_HEADS):
            acc = jnp.zeros(bk.shape, F32)
            for b in range(REL_BUCKETS):
                acc = jnp.where(bk == b, t_r[b, h], acc)
            g = h % SWA_GROUP
            o_r[h // SWA_GROUP, :, SWA_BLOCK * g:SWA_BLOCK * (g + 1)] = acc

    return pl.pallas_call(
        body, name="band_bias",
        in_specs=[pl.BlockSpec(memory_space=pltpu.SMEM), _vmem_spec()], out_specs=_vmem_spec(),
        out_shape=jax.ShapeDtypeStruct((SWA_KV_HEADS, SWA_SPAN, SWA_GROUP_LANES), F32),
    )(rel_bias, buckets)


def _relbias_call(dbias, dsink, buckets):
    def body(db_r, ds_r, bk_r, o_r, os_r):
        bk = bk_r[...]
        rowi = lax.broadcasted_iota(jnp.int32, (REL_BUCKETS, 128), 0)
        lanei = lax.broadcasted_iota(jnp.int32, (REL_BUCKETS, 128), 1)
        lane1 = lax.broadcasted_iota(jnp.int32, (1, 128), 1)
        acc = jnp.zeros((REL_BUCKETS, 128), F32)
        acc_sink = jnp.zeros((1, 128), F32)
        for h in range(SWA_Q_HEADS):
            kv, g = h // SWA_GROUP, h % SWA_GROUP
            lanes = slice(SWA_BLOCK * g, SWA_BLOCK * (g + 1))
            part = db_r[kv, :, lanes]
            for b in range(REL_BUCKETS):
                s = jnp.sum(jnp.where(bk == b, part, 0.0))
                acc = acc + jnp.where((rowi == b) & (lanei == h), s, 0.0)
            acc_sink = acc_sink + jnp.where(lane1 == h, jnp.sum(ds_r[kv, :, lanes]), 0.0)
        o_r[...] = acc
        os_r[...] = acc_sink

    return pl.pallas_call(
        body, name="relbias_grad",
        in_specs=[_vmem_spec()] * 3, out_specs=[_vmem_spec()] * 2,
        out_shape=[jax.ShapeDtypeStruct((REL_BUCKETS, 128), F32), jax.ShapeDtypeStruct((1, 128), F32)],
    )(dbias, dsink, buckets)


def _mix_call(o_f, o_b, ga, o_s, x, gn, w_out_p, g_post, g_pre2):
    L = x.shape[0]
    tm = min(256, L)
    hw = GLA_HEADS * HEAD_PAD

    def body(of_r, ob_r, ga_r, os_r, x_r, gn_r, w_r, gp_r, g2_r, cat_r, mix_r, h1_r, n2_r):
        gn_v = gn_r[...]
        for h in range(GLA_HEADS):
            sl = slice(HEAD_PAD * h, HEAD_PAD * (h + 1))
            oh = of_r[:, sl] + ob_r[:, sl]
            on = oh * _rms_r(oh) * gn_v
            gate = ga_r[:, sl]
            cat_r[:, sl] = (on * (gate * jax.nn.sigmoid(gate))).astype(cat_r.dtype)
        os_v = os_r[...]
        cat_r[:, hw:] = os_v
        mix = _dot(cat_r[:, :hw], w_r[:hw, :]) + _dot(os_v, w_r[hw:, :])
        mix_r[...] = mix
        h1 = x_r[...] + mix * _rms_r(mix) * gp_r[...]
        h1_r[...] = h1
        n2_r[...] = (h1 * _rms_r(h1) * g2_r[...]).astype(n2_r.dtype)

    return pl.pallas_call(
        body, name="mix_fwd", grid=(L // tm,),
        in_specs=[_row_spec(tm, hw), _row_spec(tm, hw), _row_spec(tm, hw), _row_spec(tm, OUT_PAD - hw),
                  _row_spec(tm, D_MODEL), _full_spec((1, HEAD_PAD)), _vmem_spec(),
                  _full_spec((1, D_MODEL)), _full_spec((1, D_MODEL))],
        out_specs=[_row_spec(tm, OUT_PAD), _row_spec(tm, D_MODEL), _row_spec(tm, D_MODEL), _row_spec(tm, D_MODEL)],
        out_shape=[jax.ShapeDtypeStruct((L, OUT_PAD), MXU_DTYPE), jax.ShapeDtypeStruct((L, D_MODEL), F32),
                   jax.ShapeDtypeStruct((L, D_MODEL), F32), jax.ShapeDtypeStruct((L, D_MODEL), MXU_DTYPE)],
        compiler_params=_params(("arbitrary",), VMEM_BIG),
    )(o_f, o_b, ga, o_s, x, gn, w_out_p, g_post, g_pre2)


def _mlp_fwd_call(n2, h1, tgt, w_ud, g_post):
    L = n2.shape[0]
    tm = min(256, L)
    blk = D_FF // N_CHIPS

    def body(n2_r, h1_r, t_r, w_r, g_r, a_r, rz_r, dh2_r, dff_r, loss_r, dg_r):
        @pl.when(pl.program_id(0) == 0)
        def _():
            loss_r[...] = jnp.zeros_like(loss_r)
            dg_r[...] = jnp.zeros_like(dg_r)

        n2v = n2_r[...]
        ff = jnp.zeros((tm, D_MODEL), F32)
        for j in range(N_CHIPS):
            sl = slice(blk * j, blk * (j + 1))
            rz = jnp.maximum(_dot(n2v, w_r[j, 0]), 0.0)
            a = _mx(rz * rz)
            rz_r[:, sl] = rz.astype(rz_r.dtype)
            a_r[:, sl] = a
            ff = ff + _dot(a, w_r[j, 1])
        g = g_r[...]
        r = _rms_r(ff)
        err = h1_r[...] + ff * r * g - t_r[...]
        loss_r[...] += 0.5 * jnp.sum(err * err) / D_MODEL
        dh2 = err * (1.0 / D_MODEL)
        dh2_r[...] = dh2
        dff, dg = _rms_bwd(ff, r, g, dh2)
        dff_r[...] = dff.astype(dff_r.dtype)
        dg_r[...] += dg

    return pl.pallas_call(
        body, name="mlp_fwd", grid=(L // tm,),
        in_specs=[_row_spec(tm, D_MODEL), _row_spec(tm, D_MODEL), _row_spec(tm, D_MODEL),
                  _vmem_spec(), _full_spec((1, D_MODEL))],
        out_specs=[_row_spec(tm, D_FF), _row_spec(tm, D_FF), _row_spec(tm, D_MODEL), _row_spec(tm, D_MODEL),
                   _full_spec((1, 128)), _full_spec((1, D_MODEL))],
        out_shape=[jax.ShapeDtypeStruct((L, D_FF), MXU_DTYPE), jax.ShapeDtypeStruct((L, D_FF), MXU_DTYPE),
                   jax.ShapeDtypeStruct((L, D_MODEL), F32), jax.ShapeDtypeStruct((L, D_MODEL), MXU_DTYPE),
                   jax.ShapeDtypeStruct((1, 128), F32), jax.ShapeDtypeStruct((1, D_MODEL), F32)],
        compiler_params=_params(("arbitrary",), VMEM_BIG),
    )(n2, h1, tgt, w_ud, g_post)


def _mlp_bwd_call(dff, rz, w_ud):
    L = dff.shape[0]
    tm = min(256, L)
    blk = D_FF // N_CHIPS

    def body(dff_r, rz_r, w_r, dz_r, dn2_r):
        dffv = dff_r[...]
        dn2 = jnp.zeros((tm, D_MODEL), F32)
        for j in range(N_CHIPS):
            sl = slice(blk * j, blk * (j + 1))
            dz = _mx(_dot_nt(dffv, w_r[j, 1]) * 2.0 * rz_r[:, sl].astype(F32))
            dz_r[:, sl] = dz
            dn2 = dn2 + _dot_nt(dz, w_r[j, 0])
        dn2_r[...] = dn2

    return pl.pallas_call(
        body, name="mlp_bwd", grid=(L // tm,),
        in_specs=[_row_spec(tm, D_MODEL), _row_spec(tm, D_FF), _vmem_spec()],
        out_specs=[_row_spec(tm, D_FF), _row_spec(tm, D_MODEL)],
        out_shape=[jax.ShapeDtypeStruct((L, D_FF), MXU_DTYPE), jax.ShapeDtypeStruct((L, D_MODEL), F32)],
        compiler_params=_params(("arbitrary",), VMEM_BIG),
    )(dff, rz, w_ud)


def _mlp_wgrad_call(a, dff, n2, dz):
    L = a.shape[0]
    tf = 512
    per = (D_FF // N_CHIPS) // tf

    def body(a_r, dff_r, n2_r, dz_r, dwd_r, dwu_r):
        dwd_r[...] = _dot_tn(a_r[...], dff_r[...])
        dwu_r[...] = _dot_tn(n2_r[...], dz_r[...])

    return pl.pallas_call(
        body, name="mlp_wgrad", grid=(D_FF // tf,),
        in_specs=[pl.BlockSpec((L, tf), lambda j: (0, j)), _vmem_spec(), _vmem_spec(),
                  pl.BlockSpec((L, tf), lambda j: (0, j))],
        out_specs=[pl.BlockSpec((tf, D_MODEL), lambda j: (j, 0)),
                   pl.BlockSpec((None, D_MODEL, tf), lambda j: (j // per, 0, j % per))],
        out_shape=[jax.ShapeDtypeStruct((D_FF, D_MODEL), F32),
                   jax.ShapeDtypeStruct((N_CHIPS, D_MODEL, D_FF // N_CHIPS), F32)],
        compiler_params=_params(("arbitrary",), VMEM_BIG),
    )(a, dff, n2, dz)


def _mix_bwd_call(dn2, dh2, h1, mix, cat, o_f, o_b, ga, gn, g_post, g_pre2, w_out_p):
    L = dn2.shape[0]
    tm = min(256, L)
    hw = GLA_HEADS * HEAD_PAD

    def body(dn2_r, dh2_r, h1_r, mix_r, cat_r, of_r, ob_r, ga_r, gn_r, gp_r, g2_r, w_r,
             dh1_r, do_r, dga_r, dos_r, dw_r, dg2_r, dgp_r, dgn_r):
        @pl.when(pl.program_id(0) == 0)
        def _():
            for ref in (dw_r, dg2_r, dgp_r, dgn_r):
                ref[...] = jnp.zeros_like(ref)

        h1 = h1_r[...]
        dx2, dg2 = _rms_bwd(h1, _rms_r(h1), g2_r[...], dn2_r[...])
        dh1 = dh2_r[...] + dx2
        dh1_r[...] = dh1
        dg2_r[...] += dg2
        mix = mix_r[...]
        dmix, dgp = _rms_bwd(mix, _rms_r(mix), gp_r[...], dh1)
        dgp_r[...] += dgp
        dmix_m = _mx(dmix)
        dw_r[...] += _dot_tn(cat_r[...], dmix_m)
        dcat = _dot_nt(dmix_m, w_r[...])
        dos_r[...] = _spread_heads(dcat[:, hw:]).astype(dos_r.dtype)
        gn_v = gn_r[...]
        dgn = jnp.zeros((1, HEAD_PAD), F32)
        for h in range(GLA_HEADS):
            sl = slice(HEAD_PAD * h, HEAD_PAD * (h + 1))
            oh = of_r[:, sl] + ob_r[:, sl]
            rr = _rms_r(oh)
            gate = ga_r[:, sl]
            sg = jax.nn.sigmoid(gate)
            doa = dcat[:, sl]
            dga_r[:, sl] = doa * (oh * rr * gn_v) * (sg * (1.0 + gate * (1.0 - sg)))
            do_h, dgn_h = _rms_bwd(oh, rr, gn_v, doa * (gate * sg))
            do_r[:, sl] = do_h
            dgn = dgn + dgn_h
        dgn_r[...] += dgn

    return pl.pallas_call(
        body, name="mix_bwd", grid=(L // tm,),
        in_specs=[_row_spec(tm, D_MODEL)] * 4 + [_row_spec(tm, OUT_PAD)] + [_row_spec(tm, hw)] * 3
        + [_full_spec((1, HEAD_PAD)), _full_spec((1, D_MODEL)), _full_spec((1, D_MODEL)), _vmem_spec()],
        out_specs=[_row_spec(tm, D_MODEL), _row_spec(tm, hw), _row_spec(tm, hw),
                   _row_spec(tm, SWA_Q_HEADS * HEAD_PAD),
                   _full_spec((OUT_PAD, D_MODEL)), _full_spec((1, D_MODEL)), _full_spec((1, D_MODEL)),
                   _full_spec((1, HEAD_PAD))],
        out_shape=[jax.ShapeDtypeStruct((L, D_MODEL), F32), jax.ShapeDtypeStruct((L, hw), F32),
                   jax.ShapeDtypeStruct((L, hw), F32), jax.ShapeDtypeStruct((L, SWA_Q_HEADS * HEAD_PAD), MXU_DTYPE),
                   jax.ShapeDtypeStruct((OUT_PAD, D_MODEL), F32), jax.ShapeDtypeStruct((1, D_MODEL), F32),
                   jax.ShapeDtypeStruct((1, D_MODEL), F32), jax.ShapeDtypeStruct((1, HEAD_PAD), F32)],
        compiler_params=_params(("arbitrary",), VMEM_BIG),
    )(dn2, dh2, h1, mix, cat, o_f, o_b, ga, gn, g_post, g_pre2, w_out_p)


def _in_bwd_call(x, dh1, g_pre, w_in_t, pairs, singles, dep=None):
    L = x.shape[0]
    tm = min(256, L)
    n_pair, n_single = len(pairs), len(singles)
    groups = [c for c, _ in pairs] + [c for c, _ in singles]

    def body(*refs):
        x_r, dh1_r, g_r, w_r = refs[:4]
        pair_refs = refs[4:4 + 2 * n_pair]
        single_refs = refs[4 + 2 * n_pair:4 + 2 * n_pair + n_single]
        dx_r, dw_r, dg_r = refs[4 + 2 * n_pair + n_single:]

        @pl.when(pl.program_id(0) == 0)
        def _():
            dw_r[...] = jnp.zeros_like(dw_r)
            dg_r[...] = jnp.zeros_like(dg_r)

        xv = x_r[...]
        r = _rms_r(xv)
        g = g_r[...]
        u = _mx(xv * r * g)
        vals = [pair_refs[2 * i][...] + pair_refs[2 * i + 1][...] for i in range(n_pair)]
        vals += [ref[...].astype(F32) for ref in single_refs]
        du = jnp.zeros((tm, D_MODEL), F32)
        for (first, rows, heads), val in zip(groups, vals):
            d = _mx(_squeeze_heads(val) if heads else val)
            du = du + _dot(d, w_r[first:first + rows, :])
            dw_r[first:first + rows, :] += _dot_tn(d, u)
        dx, dg = _rms_bwd(xv, r, g, du)
        dx_r[...] = dh1_r[...] + dx
        dg_r[...] += dg

    arrays = [a for _, pr in pairs for a in pr] + [a for _, a in singles]
    specs = [_row_spec(tm, a.shape[1]) for a in arrays]
    body, extra, extra_specs = _after(body, 4 + len(arrays), dep)
    return pl.pallas_call(
        body, name="in_bwd", grid=(L // tm,),
        in_specs=[_row_spec(tm, D_MODEL), _row_spec(tm, D_MODEL), _full_spec((1, D_MODEL)), _vmem_spec()] + specs
        + extra_specs,
        out_specs=[_row_spec(tm, D_MODEL), _full_spec((IN_COLS, D_MODEL)), _full_spec((1, D_MODEL))],
        out_shape=[jax.ShapeDtypeStruct((L, D_MODEL), F32), jax.ShapeDtypeStruct((IN_COLS, D_MODEL), F32),
                   jax.ShapeDtypeStruct((1, D_MODEL), F32)],
        compiler_params=_params(("arbitrary",), VMEM_BIG),
    )(x, dh1, g_pre, w_in_t, *arrays, *extra)


def _adamw_math(w, g, m, v):
    m = ADAM_B1 * m + (1.0 - ADAM_B1) * g
    v = ADAM_B2 * v + (1.0 - ADAM_B2) * (g * g)
    m_hat = m / (1.0 - ADAM_B1 ** ADAM_STEP)
    v_hat = v / (1.0 - ADAM_B2 ** ADAM_STEP)
    delta = -ADAM_LR * (m_hat / (jnp.sqrt(v_hat) + ADAM_EPS) + ADAM_WD * w)
    return delta, m, v


def _adamw_call(w, g, m, v, name, dep=None):
    rows, cols = w.shape
    tr = min(256, rows)

    def body(w_r, g_r, m_r, v_r, d_r, nm_r, nv_r):
        d_r[...], nm_r[...], nv_r[...] = _adamw_math(w_r[...], g_r[...], m_r[...], v_r[...])

    if rows % tr == 0:
        spec, steps = _row_spec(tr, cols), rows // tr
    else:
        spec, steps = pl.BlockSpec((rows, 256), lambda i: (0, i)), cols // 256
    body, extra, extra_specs = _after(body, 4, dep)
    return pl.pallas_call(
        body, name=name, grid=(steps,),
        in_specs=[spec] * 4 + extra_specs, out_specs=[spec] * 3,
        out_shape=[jax.ShapeDtypeStruct(w.shape, F32)] * 3,
        compiler_params=_params(("arbitrary",)),
    )(w, g, m, v, *extra)


def _position():
    return lax.axis_index("x"), lax.axis_index("y"), lax.axis_index("c")


def _other_chips(x, y):
    return [(1 - x, y), (x, 1 - y), (1 - x, 1 - y)]


ROWS, COLS = -2, -1


def _half(ref, which, axis):
    size = ref.shape[axis] // 2
    span = pl.ds(pl.multiple_of(which * size, 16 if axis == ROWS else 128), size)
    index = [slice(None)] * len(ref.shape)
    index[axis] = span
    return ref.at[tuple(index)]


def _first_gather_call(shards, axes):
    n = len(shards)

    def body(*refs):
        srcs, outs = refs[:n], refs[n:2 * n]
        send_sems, recv_sems, local_sems = refs[2 * n:]
        x, y, c = _position()
        sibling = (x, y, 1 - c)
        chips = _other_chips(x, y)
        local = [pltpu.make_async_copy(srcs[a], outs[a].at[2 * x + y], local_sems.at[a]) for a in range(n)]
        for cp in local:
            cp.start()

        def copy(a, k, block, to, src=None):
            px, py, pc = block
            dst = _half(outs[a].at[2 * px + py], pc, axes[a])
            return pltpu.make_async_remote_copy(
                src_ref=dst if src is None else src, dst_ref=dst, send_sem=send_sems.at[6 * a + k],
                recv_sem=recv_sems.at[6 * a + k], device_id=to, device_id_type=MESH_ID)

        first, passed = [], []
        for a in range(n):
            my_half = _half(srcs[a], c, axes[a])
            first += [copy(a, j, (x, y, c), (*chip, c), src=my_half) for j, chip in enumerate(chips)]
        for cp in first:
            cp.start()
        for a in range(n):
            for j, chip in enumerate(chips):
                copy(a, j, (*chip, c), (x, y, c)).wait_recv()
                passed.append(copy(a, 3 + j, (*chip, c), sibling))
                passed[-1].start()
        for a in range(n):
            for j, chip in enumerate(chips):
                copy(a, 3 + j, (*chip, 1 - c), (x, y, c)).wait_recv()
        for cp in first + passed:
            cp.wait_send()
        for cp in local:
            cp.wait()

    return pl.pallas_call(
        body, name="first_gather",
        in_specs=[_any_spec()] * n, out_specs=[_any_spec()] * n,
        out_shape=[jax.ShapeDtypeStruct((N_CHIPS,) + s.shape, s.dtype) for s in shards],
        scratch_shapes=[pltpu.SemaphoreType.DMA((6 * n,)), pltpu.SemaphoreType.DMA((6 * n,)),
                        pltpu.SemaphoreType.DMA((n,))],
    )(*shards)


def _split_start(name, arrays, n_copies, plan):
    n = len(arrays)

    def body(*refs):
        ins, send_sems, recv_sems, token = refs[:n], refs[n], refs[n + 1], refs[-1]
        for k, (src, dst, to, _) in enumerate(plan(ins)):
            pltpu.make_async_remote_copy(src_ref=src, dst_ref=dst, send_sem=send_sems.at[k],
                                         recv_sem=recv_sems.at[k], device_id=to, device_id_type=MESH_ID).start()
        token[...] = jnp.zeros_like(token)

    hbm = pl.BlockSpec(memory_space=pltpu.HBM)
    sem = pl.BlockSpec(memory_space=pltpu.SEMAPHORE)
    out = pl.pallas_call(
        body, name=name,
        out_shape=(pltpu.SemaphoreType.DMA((n_copies,)), pltpu.SemaphoreType.DMA((n_copies,)))
        + tuple(pltpu.HBM(a.shape, a.dtype) for a in arrays) + (jax.ShapeDtypeStruct((8, 128), F32),),
        in_specs=[hbm] * n, out_specs=(sem, sem) + (hbm,) * n + (_vmem_spec(),),
        input_output_aliases={i: 2 + i for i in range(n)},
        compiler_params=pltpu.CompilerParams(has_side_effects=pltpu.SideEffectType.DATAFLOW_SIDE_EFFECTING),
    )(*[pltpu.with_memory_space_constraint(a, pltpu.HBM) for a in arrays])
    return (out[0], out[1], tuple(out[2:2 + n])), out[-1]


def _split_wait(name, handle, n_copies, plan, after):
    send_sems, recv_sems, arrays = handle
    n = len(arrays)

    def body(*refs):
        ins, s_sems, r_sems = refs[:n], refs[n], refs[n + 1]
        for k, (src, dst, to, landed) in enumerate(plan(ins)):
            cp = pltpu.make_async_remote_copy(src_ref=src, dst_ref=landed, send_sem=s_sems.at[k],
                                              recv_sem=r_sems.at[k], device_id=to, device_id_type=MESH_ID)
            cp.wait_send()
            cp.wait_recv()

    hbm = pl.BlockSpec(memory_space=pltpu.HBM)
    sem = pl.BlockSpec(memory_space=pltpu.SEMAPHORE)
    out = pl.pallas_call(
        body, name=name,
        out_shape=tuple(pltpu.HBM(a.shape, a.dtype) for a in arrays),
        in_specs=[hbm] * n + [sem, sem, _any_spec()], out_specs=(hbm,) * n,
        input_output_aliases={i: i for i in range(n)},
        compiler_params=pltpu.CompilerParams(has_side_effects=pltpu.SideEffectType.DATAFLOW_SIDE_EFFECTING),
    )(*arrays, send_sems, recv_sems, after)
    return tuple(out)


def _gather_plans(axes):
    n = len(axes)

    def stage_one(refs):
        x, y, c = _position()
        copies = []
        for a, axis in enumerate(axes):
            for px, py in _other_chips(x, y):
                copies.append((_half(refs[a], c, axis), _half(refs[n + a].at[2 * x + y], c, axis),
                               (px, py, c), _half(refs[n + a].at[2 * px + py], c, axis)))
        return copies

    def stage_two(refs):
        x, y, c = _position()
        copies = []
        for a, axis in enumerate(axes):
            for px, py in _other_chips(x, y):
                piece = _half(refs[n + a].at[2 * px + py], c, axis)
                copies.append((piece, piece, (x, y, 1 - c), _half(refs[n + a].at[2 * px + py], 1 - c, axis)))
        return copies

    return stage_one, stage_two


def _pair_swap_plan(axes):
    n = len(axes)

    def plan(refs):
        x, y, c = _position()
        return [(_half(refs[a], 1 - c, axes[a]), refs[n + a], (x, y, 1 - c), refs[n + a]) for a in range(n)]

    return plan


def _chip_swap_plan(n):
    def plan(refs):
        x, y, c = _position()
        copies = []
        for a in range(n):
            for j, (px, py) in enumerate(_other_chips(x, y)):
                copies.append((refs[a].at[2 * px + py], refs[n + a].at[j], (px, py, c), refs[n + a].at[j]))
        return copies

    return plan


def _pair_join_plan(axes):
    def plan(refs):
        x, y, c = _position()
        copies = []
        for a, axis in enumerate(axes):
            mine = _half(refs[a], c, axis)
            copies.append((mine, mine, (x, y, 1 - c), _half(refs[a], 1 - c, axis)))
        return copies

    return plan


def _pair_add_call(g, got, pos, name, axis):
    rows, cols = got.shape[1], got.shape[2]
    tr = min(256, rows) if axis == ROWS else rows
    nblk = rows // tr
    if axis == ROWS:
        mine = lambda j, i, p: (j, p[1] * nblk + i, 0)
    else:
        mine = lambda j, i, p: (j, 0, p[1])

    def body(pos_r, g_r, got_r, o_r):
        o_r[...] = (g_r[...] + got_r[...]).astype(o_r.dtype)

    return pl.pallas_call(
        body, name=name,
        grid_spec=pltpu.PrefetchScalarGridSpec(
            num_scalar_prefetch=1, grid=(N_CHIPS, nblk),
            in_specs=[pl.BlockSpec((None, tr, cols), mine),
                      pl.BlockSpec((None, tr, cols), lambda j, i, p: (j, i, 0))],
            out_specs=pl.BlockSpec((None, tr, cols), lambda j, i, p: (j, i, 0))),
        out_shape=jax.ShapeDtypeStruct(got.shape, COMM_DTYPE),
        compiler_params=_params(("arbitrary", "arbitrary")),
    )(pos, g, got)


def _chip_add_call(hsum, got, pos, name, axis):
    rows, cols = hsum.shape[1], hsum.shape[2]
    tr = min(256, rows) if axis == ROWS else rows
    nblk = rows // tr
    if axis == ROWS:
        out_shape, mine = (2 * rows, cols), (lambda i, p: (p[1] * nblk + i, 0))
    else:
        out_shape, mine = (rows, 2 * cols), (lambda i, p: (0, p[1]))

    def body(pos_r, own_r, got_r, o_r):
        acc = own_r[...].astype(F32)
        for j in range(3):
            acc = acc + got_r[j].astype(F32)
        o_r[...] = acc

    return pl.pallas_call(
        body, name=name,
        grid_spec=pltpu.PrefetchScalarGridSpec(
            num_scalar_prefetch=1, grid=(nblk,),
            in_specs=[pl.BlockSpec((None, tr, cols), lambda i, p: (p[0], i, 0)),
                      pl.BlockSpec((3, tr, cols), lambda i, p: (0, i, 0))],
            out_specs=pl.BlockSpec((tr, cols), mine)),
        out_shape=jax.ShapeDtypeStruct(out_shape, F32),
        compiler_params=_params(("arbitrary",)),
    )(pos, hsum, got)


SMALL_NAMES = ("norm_mix_pre", "norm_mix_post", "norm_mlp_pre", "norm_mlp_post", "b_gate_fwd", "b_gate_bwd",
               "gla_norm", "swa_sink", "rel_bias")


def _small_update_call(grads, gate_grads, params, dep=None):
    n_dev = 8
    n_small = len(SMALL_NAMES)
    wmv = [t for p in params for t in p]
    shapes = [p[0].shape for p in params]

    def body(*refs):
        g_refs = refs[:n_small + 2]
        wmv_refs = refs[n_small + 2:n_small + 2 + 3 * n_small]
        n_in = n_small + 2 + 3 * n_small
        out_refs = refs[n_in:n_in + 4 * n_small + 2]
        pack_a, pack_b, all_a, all_b, send_sems, recv_sems = refs[n_in + 4 * n_small + 2:]
        x, y, c = _position()
        me = 4 * x + 2 * y + c
        pack_a[...] = jnp.zeros_like(pack_a)
        pack_b[...] = jnp.zeros_like(pack_b)
        for i in range(4):
            pack_a[i:i + 1, :] = g_refs[i][...]
        pack_a[4:5, 0:256] = g_refs[4][...]
        pack_a[5:6, 0:256] = g_refs[5][...]
        pack_a[6:7, 0:128] = g_refs[6][...]
        pack_a[7:8, 0:128] = g_refs[7][...]
        pack_b[0:32, 0:128] = g_refs[8][...]
        pack_b[32:48, :] = g_refs[9][...]
        pack_b[48:64, :] = g_refs[10][...]
        all_a[me] = pack_a[...]
        all_b[me] = pack_b[...]
        copies = []
        for k in range(1, n_dev):
            fx, fy, fc = (k >> 2) & 1, (k >> 1) & 1, k & 1
            to = (1 - x if fx else x, 1 - y if fy else y, 1 - c if fc else c)
            for t, (pack, dst) in enumerate(((pack_a, all_a), (pack_b, all_b))):
                copies.append(pltpu.make_async_remote_copy(
                    src_ref=pack, dst_ref=dst.at[me], send_sem=send_sems.at[2 * (k - 1) + t],
                    recv_sem=recv_sems.at[2 * (k - 1) + t], device_id=to, device_id_type=MESH_ID))
        for cp in copies:
            cp.start()
        for cp in copies:
            cp.wait()
        sum_a, sum_b = all_a[0], all_b[0]
        for d in range(1, n_dev):
            sum_a = sum_a + all_a[d]
            sum_b = sum_b + all_b[d]
        gsum = [sum_a[0:1], sum_a[1:2], sum_a[2:3], sum_a[3:4], sum_a[4:5, 0:256], sum_a[5:6, 0:256],
                sum_a[6:7, 0:128], sum_a[7:8, 0:SWA_Q_HEADS], sum_b[0:32, 0:SWA_Q_HEADS]]
        for i in range(n_small):
            w_r, m_r, v_r = wmv_refs[3 * i:3 * i + 3]
            delta, new_m, new_v = _adamw_math(w_r[...], gsum[i], m_r[...], v_r[...])
            out_refs[4 * i][...] = gsum[i]
            out_refs[4 * i + 1][...] = delta
            out_refs[4 * i + 2][...] = new_m
            out_refs[4 * i + 3][...] = new_v
        out_refs[4 * n_small][...] = sum_b[32:48]
        out_refs[4 * n_small + 1][...] = sum_b[48:64]

    n_in = n_small + 2 + 3 * n_small
    body, extra, extra_specs = _after(body, n_in, dep)
    out_shape = [jax.ShapeDtypeStruct(s, F32) for s in shapes for _ in range(4)]
    out_shape += [jax.ShapeDtypeStruct((GLA_GATE_RANK, 256), F32)] * 2
    out = pl.pallas_call(
        body, name="small_update",
        in_specs=[_vmem_spec()] * n_in + extra_specs, out_specs=[_vmem_spec()] * len(out_shape),
        out_shape=out_shape,
        scratch_shapes=[pltpu.VMEM((8, D_MODEL), F32), pltpu.VMEM((64, 256), F32),
                        pltpu.VMEM((n_dev, 8, D_MODEL), F32), pltpu.VMEM((n_dev, 64, 256), F32),
                        pltpu.SemaphoreType.DMA((2 * (n_dev - 1),)), pltpu.SemaphoreType.DMA((2 * (n_dev - 1),))],
    )(*grads, *gate_grads, *wmv, *extra)
    per_name = [tuple(out[4 * i:4 * i + 4]) for i in range(n_small)]
    return per_name, out[4 * n_small], out[4 * n_small + 1]


def _pad_heads(t, n_heads, axis=-1):
    axis = axis % t.ndim
    shape = t.shape
    t = t.reshape(shape[:axis] + (n_heads, 64) + shape[axis + 1:])
    pad = [(0, 0)] * t.ndim
    pad[axis + 1] = (0, HEAD_PAD - 64)
    return jnp.pad(t, pad).reshape(shape[:axis] + (n_heads * HEAD_PAD,) + shape[axis + 1:])


def _unpad_heads(t, n_heads, axis=-1):
    axis = axis % t.ndim
    shape = t.shape
    t = t.reshape(shape[:axis] + (n_heads, HEAD_PAD) + shape[axis + 1:])
    t = lax.slice_in_dim(t, 0, 64, axis=axis + 1)
    return t.reshape(shape[:axis] + (n_heads * 64,) + shape[axis + 1:])


def _pad_gate(w, first_row):
    return jnp.pad(_pad_heads(w, 4), ((first_row, 128 - GLA_GATE_RANK - first_row), (0, 0)))


def _own_slot(shard, chip):
    zone = lax.empty((N_CHIPS,) + shard.shape, shard.dtype)
    return lax.dynamic_update_slice(zone, shard[None], (chip,) + (0,) * shard.ndim)


def _reduce_to_owners(grads, axes, pos, tag, overlap):
    n = len(grads)

    def half_shape(g, axis):
        return (N_CHIPS, g.shape[1] // 2, g.shape[2]) if axis == ROWS else (N_CHIPS, g.shape[1], g.shape[2] // 2)

    lands = [lax.empty(half_shape(g, axis), F32) for g, axis in zip(grads, axes)]
    handle, token = _split_start(tag + "_pair_start", list(grads) + lands, n, _pair_swap_plan(axes))
    got = _split_wait(tag + "_pair_wait", handle, n, _pair_swap_plan(axes), overlap[0](token))
    sums = [_pair_add_call(got[a], got[n + a], pos, f"{tag}_pair_add{a}", axes[a]) for a in range(n)]
    lands = [lax.empty((3,) + s.shape[1:], s.dtype) for s in sums]
    handle, token = _split_start(tag + "_chip_start", sums + lands, 3 * n, _chip_swap_plan(n))
    got = _split_wait(tag + "_chip_wait", handle, 3 * n, _chip_swap_plan(n), overlap[1](token))
    halves = [_chip_add_call(got[a], got[n + a], pos, f"{tag}_chip_add{a}", axes[a]) for a in range(n)]
    handle, token = _split_start(tag + "_join_start", halves, n, _pair_join_plan(axes))
    return _split_wait(tag + "_join_wait", handle, n, _pair_join_plan(axes), overlap[2](token))


def kernel(x, norm_mix_pre, w_in, w_gate_up_fwd, b_gate_fwd, w_gate_up_bwd, b_gate_bwd, gla_norm, swa_sink, rel_bias, w_out, norm_mix_post, norm_mlp_pre, w_up, w_down, norm_mlp_post, loss_target, m_norm_mix_pre, m_w_in, m_w_gate_up_fwd, m_b_gate_fwd, m_w_gate_up_bwd, m_b_gate_bwd, m_gla_norm, m_swa_sink, m_rel_bias, m_w_out, m_norm_mix_post, m_norm_mlp_pre, m_w_up, m_w_down, m_norm_mlp_post, v_norm_mix_pre, v_w_in, v_w_gate_up_fwd, v_b_gate_fwd, v_w_gate_up_bwd, v_b_gate_bwd, v_gla_norm, v_swa_sink, v_rel_bias, v_w_out, v_norm_mix_post, v_norm_mlp_pre, v_w_up, v_w_down, v_norm_mlp_post):
    given = dict(locals())
    cx, cy, cc = _position()
    chip = (2 * cx + cy).astype(jnp.int32)
    pos = jnp.stack([chip, cc.astype(jnp.int32)])
    seq, tgt = x[0], loss_target[0]
    L = seq.shape[0]

    gates = jnp.concatenate([w_gate_up_fwd[0], w_gate_up_bwd[0]], axis=0).astype(COMM_DTYPE)
    all_in, all_gates = _first_gather_call([w_in[0].T.astype(COMM_DTYPE), gates], [COLS, ROWS])
    rest = [w_out[0].astype(COMM_DTYPE), jnp.stack([w_up[0], w_down[0]]).astype(COMM_DTYPE)]
    stage_one, stage_two = _gather_plans([ROWS, ROWS])
    handle, token = _split_start("gather_chip_start", rest + [_own_slot(s, chip) for s in rest] + [all_gates], 6,
                                 stage_one)

    w_in_t = _mx(all_in.reshape(IN_COLS, D_MODEL))
    gates_full = jnp.concatenate([all_gates[j] for j in range(N_CHIPS)], axis=1)
    wgf_p = _mx(_pad_gate(gates_full[:GLA_GATE_RANK], 0))
    wgb_p = _mx(_pad_gate(gates_full[GLA_GATE_RANK:], GLA_GATE_RANK))
    bf_p, bb_p = _pad_heads(b_gate_fwd, 4), _pad_heads(b_gate_bwd, 4)
    buckets = jnp.asarray(_band_buckets())
    bias = _bias_call(rel_bias, buckets)
    sink1 = swa_sink.reshape(SWA_Q_HEADS)

    qa, ka, va, ga, qs, ks, vs, za = _proj_call(seq, norm_mix_pre, w_in_t, dep=token)
    halo = ((SWA_BLOCK, SWA_BLOCK), (0, 0))
    ks_p, vs_p = jnp.pad(ks, halo), jnp.pad(vs, halo)
    o_f, o_b, s_f, s_b = _gla_fwd_call(qa, ka, va, za, wgf_p, bf_p, wgb_p, bb_p)
    arrays = _split_wait("gather_chip_wait", handle, 6, stage_one, o_f)
    handle, token = _split_start("gather_pair_start", list(arrays), 6, stage_two)
    o_s = _swa_fwd_call(qs, ks_p, vs_p, bias, sink1, dep=token)
    arrays = _split_wait("gather_pair_wait", handle, 6, stage_two, o_s)
    w_out_full = _mx(arrays[2].reshape(N_CHIPS * R_OUT, D_MODEL))
    w_ud = _mx(arrays[3])
    cat, mix, h1, n2 = _mix_call(o_f, o_b, ga, o_s, seq, gla_norm, w_out_full, norm_mix_post, norm_mlp_pre)
    a, rz, dh2, dff, loss, d_post2 = _mlp_fwd_call(n2, h1, tgt, w_ud, norm_mlp_post)

    dz, dn2 = _mlp_bwd_call(dff, rz, w_ud)
    dw_down, dw_up4 = _mlp_wgrad_call(a, dff, n2, dz)
    dh1, do, dga, dos, dw_out, d_pre2, d_post, d_gn = _mix_bwd_call(
        dn2, dh2, h1, mix, cat, o_f, o_b, ga, gla_norm, norm_mix_post, norm_mlp_pre, w_out_full)
    done = {}

    def gla_backward(tok):
        done["gla"] = _gla_bwd_call(qa, ka, va, za, do, s_f, s_b, wgf_p, bf_p, wgb_p, bb_p, dep=tok)
        return done["gla"][0]

    def swa_backward(tok):
        done["swa"] = _swa_bwd_call(qs, ks_p, vs_p, bias, sink1, dos, dep=tok)
        return done["swa"][0]

    def in_backward(tok):
        dqf, dkf, dvf, dzf, _, _, dqb, dkb, dvb, dzb, _, _ = done["gla"]
        dqs, dks_p, dvs_p, _, _ = done["swa"]
        dks = dks_p[SWA_BLOCK:SWA_BLOCK + L]
        dvs = dvs_p[SWA_BLOCK:SWA_BLOCK + L]
        done["in"] = _in_bwd_call(
            seq, dh1, norm_mix_pre, w_in_t,
            pairs=[(T_QA, (dqf, dqb)), (T_KA, (dkf, dkb)), (T_VA, (dvf, dvb)), (T_ZA, (dzf, dzb))],
            singles=[(T_GA, dga), (T_QS, dqs), (T_KS, dks), (T_VS, dvs)], dep=tok)
        return done["in"][0]

    g_up, g_down, g_out = _reduce_to_owners(
        [dw_up4, dw_down.reshape(N_CHIPS, R_DOWN, D_MODEL), dw_out.reshape(N_CHIPS, R_OUT, D_MODEL)],
        [ROWS, ROWS, ROWS], pos, "mlp", [swa_backward, gla_backward, in_backward])
    dx, dw_in_t, d_pre = done["in"]
    dwf, dbf, dwb, dbb = done["gla"][4], done["gla"][5], done["gla"][10], done["gla"][11]
    drel, dsink = _relbias_call(done["swa"][3], done["swa"][4], buckets)

    small_grads = [d_pre, d_post, d_pre2, d_post2, _unpad_heads(dbf, 4), _unpad_heads(dbb, 4), d_gn, dsink, drel]
    gate_grads = [_unpad_heads(dwf[:GLA_GATE_RANK], 4), _unpad_heads(dwb[GLA_GATE_RANK:2 * GLA_GATE_RANK], 4)]
    small_params = [(given[n], given["m_" + n], given["v_" + n]) for n in SMALL_NAMES]
    upd = {}

    def update_up(tok):
        upd["w_up"] = (g_up,) + tuple(_adamw_call(w_up[0], g_up, m_w_up[0], v_w_up[0], "adamw_w_up", dep=tok))
        return upd["w_up"][1]

    def update_small(tok):
        per_name, gf_sum, gb_sum = _small_update_call(small_grads, gate_grads, small_params, dep=tok)
        upd.update(dict(zip(SMALL_NAMES, per_name)))
        for name, total in (("w_gate_up_fwd", gf_sum), ("w_gate_up_bwd", gb_sum)):
            g = lax.dynamic_slice(total, (0, chip * 64), (GLA_GATE_RANK, 64))
            upd[name] = (g,) + tuple(_adamw_call(given[name][0], g, given["m_" + name][0], given["v_" + name][0],
                                                 "adamw_" + name))
        upd["w_down"] = (g_down,) + tuple(
            _adamw_call(w_down[0], g_down, m_w_down[0], v_w_down[0], "adamw_w_down", dep=gf_sum))
        return upd["w_down"][1]

    def update_out(tok):
        upd["w_out"] = (g_out,) + tuple(_adamw_call(w_out[0], g_out, m_w_out[0], v_w_out[0], "adamw_w_out", dep=tok))
        return upd["w_out"][1]

    (g_in_t,) = _reduce_to_owners([dw_in_t.reshape(N_CHIPS, R_IN, D_MODEL)], [COLS], pos, "in",
                                  [update_up, update_small, update_out])
    in_t = (g_in_t,) + tuple(_adamw_call(w_in[0].T, g_in_t, m_w_in[0].T, v_w_in[0].T, "adamw_w_in"))
    upd["w_in"] = tuple(t.T for t in in_t)

    big = ("w_in", "w_gate_up_fwd", "w_gate_up_bwd", "w_out", "w_up", "w_down")
    names = ["norm_mix_pre", "w_in", "w_gate_up_fwd", "b_gate_fwd", "w_gate_up_bwd", "b_gate_bwd", "gla_norm",
             "swa_sink", "rel_bias", "w_out", "norm_mix_post", "norm_mlp_pre", "w_up", "w_down", "norm_mlp_post"]
    outs = [lax.psum(loss[0, 0], MESH_AXES), dx[None]]
    for kind in range(4):
        outs += [upd[n][kind][None] if n in big else upd[n][kind] for n in names]
    return tuple(outs)
```

```python
import math

import numpy as np
import jax
import jax.numpy as jnp
from jax import lax
from jax.experimental import pallas as pl
from jax.experimental.pallas import tpu as pltpu

F32 = jnp.float32
MXU_DTYPE = jnp.bfloat16
COMM_DTYPE = jnp.bfloat16

D_MODEL = 1024
D_FF = 4096
N_CHIPS = 4
GLA_HEADS = 4
GLA_CHUNK = 64
GLA_GATE_RANK = 16
GLA_GATE_NORM = 16.0
SWA_Q_HEADS = 8
SWA_KV_HEADS = 2
SWA_BLOCK = 128
REL_BUCKETS = 32
REL_MAX_DIST = 128
NORM_EPS = 1e-6
HEAD_PAD = 128

ADAM_LR = 0.001
ADAM_B1 = 0.9
ADAM_B2 = 0.999
ADAM_EPS = 1e-08
ADAM_WD = 0.01
ADAM_STEP = 10

OUT_PAD = 1024

R_IN, R_OUT, R_UP, R_DOWN = 584, 256, 1024, 1024

VMEM_BIG = 56 * 1024 * 1024
MESH_AXES = ("x", "y", "c")
MESH_ID = pl.DeviceIdType.MESH


def _mx(a):
    return a.astype(MXU_DTYPE)


def _dot(a, b):
    return jnp.dot(a, b, preferred_element_type=F32)


def _dot_nt(a, b):
    return lax.dot_general(a, b, (((1,), (1,)), ((), ())), preferred_element_type=F32)


def _dot_tn(a, b):
    return lax.dot_general(a, b, (((0,), (0,)), ((), ())), preferred_element_type=F32)


def _rms_r(x):
    return lax.rsqrt(jnp.mean(x * x, axis=-1, keepdims=True) + NORM_EPS)


def _rms_bwd(x, r, g, dy):
    xh = x * r
    gdy = dy * g
    dx = r * (gdy - xh * jnp.mean(gdy * xh, axis=-1, keepdims=True))
    return dx, jnp.sum(dy * xh, axis=0, keepdims=True)


def _low_half(rows):
    return lax.broadcasted_iota(jnp.int32, (rows, HEAD_PAD), 1) < 64


def _spread_heads(x):
    low = _low_half(x.shape[0])
    parts = []
    for p in range(x.shape[1] // HEAD_PAD):
        pair = x[:, HEAD_PAD * p:HEAD_PAD * (p + 1)]
        parts += [jnp.where(low, pair, 0.0), jnp.where(low, pltpu.roll(pair, 64, 1), 0.0)]
    return jnp.concatenate(parts, axis=1)


def _squeeze_heads(x):
    low = _low_half(x.shape[0])
    parts = []
    for p in range(x.shape[1] // (2 * HEAD_PAD)):
        even = x[:, 2 * HEAD_PAD * p:2 * HEAD_PAD * p + HEAD_PAD]
        odd = x[:, 2 * HEAD_PAD * p + HEAD_PAD:2 * HEAD_PAD * (p + 1)]
        parts.append(jnp.where(low, even, pltpu.roll(odd, 64, 1)))
    return parts[0] if len(parts) == 1 else jnp.concatenate(parts, axis=1)


def _params(sem=None, vmem=None):
    kw = {}
    if sem is not None:
        kw["dimension_semantics"] = sem
    if vmem is not None:
        kw["vmem_limit_bytes"] = vmem
    return pltpu.CompilerParams(**kw)


def _vmem_spec():
    return pl.BlockSpec(memory_space=pltpu.VMEM)


def _row_spec(tm, width):
    return pl.BlockSpec((tm, width), lambda i: (i, 0))


def _full_spec(shape):
    return pl.BlockSpec(shape, lambda i: (0,) * len(shape))


def _any_spec():
    return pl.BlockSpec(memory_space=pl.ANY)


def _after(body, n_in, dep):
    if dep is None:
        return body, [], []
    return (lambda *refs: body(*refs[:n_in], *refs[n_in + 1:])), [dep], [_any_spec()]


T_QA, T_KA, T_VA, T_GA = (0, 256, 4), (256, 256, 4), (512, 512, 0), (1024, 512, 0)
T_QS, T_KS, T_VS = (1568, 512, 8), (2080, 128, 2), (2208, 128, 2)
T_ZA = (1536, 128, 0)
ZA_COLS = 2 * GLA_GATE_RANK
IN_COLS = 2336


def _proj_call(x, g_pre, w_in_t, dep=None):
    L = x.shape[0]
    tm = min(256, L)
    groups = [(T_QA, F32), (T_KA, F32), (T_VA, MXU_DTYPE), (T_GA, F32),
              (T_QS, MXU_DTYPE), (T_KS, MXU_DTYPE), (T_VS, MXU_DTYPE), (T_ZA, F32)]
    widths = [rows * (2 if heads else 1) for (_, rows, heads), _ in groups]

    def body(x_ref, g_ref, w_ref, *outs):
        xv = x_ref[...]
        u = _mx(xv * _rms_r(xv) * g_ref[...])
        for ref, (grp, _) in zip(outs, groups):
            first, rows, heads = grp
            val = _dot_nt(u, w_ref[first:first + rows, :])
            if heads:
                val = _spread_heads(val)
            if grp is T_ZA:
                val = jnp.where(lax.broadcasted_iota(jnp.int32, val.shape, 1) < ZA_COLS, val, 0.0)
            ref[...] = val.astype(ref.dtype)

    body, extra, extra_specs = _after(body, 3, dep)
    return pl.pallas_call(
        body, name="proj_fwd", grid=(L // tm,),
        in_specs=[_row_spec(tm, D_MODEL), _full_spec((1, D_MODEL)), _vmem_spec()] + extra_specs,
        out_specs=[_row_spec(tm, w) for w in widths],
        out_shape=[jax.ShapeDtypeStruct((L, w), dt) for w, (_, dt) in zip(widths, groups)],
        compiler_params=_params(("arbitrary",), VMEM_BIG),
    )(x, g_pre, w_in_t, *extra)


def _tri_masks():
    row = lax.broadcasted_iota(jnp.int32, (GLA_CHUNK, GLA_CHUNK), 0)
    col = lax.broadcasted_iota(jnp.int32, (GLA_CHUNK, GLA_CHUNK), 1)
    return row >= col, row <= col


def _chunk_sums(tri_m, x):
    hi = _mx(x)
    rest = x - hi.astype(F32)
    mid = _mx(rest)
    lo = _mx(rest - mid.astype(F32))
    return _dot(tri_m, hi) + _dot(tri_m, mid) + _dot(tri_m, lo)


def _gla_block_pre(q_r, k_r, z_r, w_r, b_r, rev, nc, qd_s, ki_s, ks_s, dec_s, keep=None):
    tri_f, tri_b = _tri_masks()
    tri_m = _mx((tri_b if rev else tri_f).astype(F32))
    g = _dot(_mx(z_r[...]), w_r[...]) + b_r[...]
    la = (jnp.minimum(g, 0.0) - jnp.log(1.0 + jnp.exp(-jnp.abs(g)))) / GLA_GATE_NORM
    sums, lasts = [], []
    for c in range(nc):
        b_c = _chunk_sums(tri_m, la[GLA_CHUNK * c:GLA_CHUNK * (c + 1)])
        blast = b_c[0:1] if rev else b_c[GLA_CHUNK - 1:GLA_CHUNK]
        dec_s[c] = jnp.exp(blast)
        sums.append(b_c)
        lasts.append(jnp.broadcast_to(blast, b_c.shape))
    b = jnp.concatenate(sums, axis=0)
    eb = jnp.exp(b)
    enb = jnp.exp(-b)
    elb = jnp.exp(jnp.concatenate(lasts, axis=0) - b)
    k = k_r[...]
    qd_s[...] = (q_r[...] * 0.125 * eb).astype(qd_s.dtype)
    ki_s[...] = (k * enb).astype(ki_s.dtype)
    ks_s[...] = (k * elb).astype(ks_s.dtype)
    if keep is not None:
        for ref, val in zip(keep, (g, eb, enb, elb)):
            ref[...] = val


def _gla_fwd_call(qa, ka, va, za, wgf, bgf, wgb, bgb):
    L = qa.shape[0]
    br = min(512, L)
    nb, nc, n_chunks = L // br, br // GLA_CHUNK, L // GLA_CHUNK
    hw = GLA_HEADS * HEAD_PAD

    def body(qaf, kaf, vaf, zaf, qab, kab, vab, zab, wgf_r, bgf_r, wgb_r, bgb_r,
             of_r, ob_r, sf_r, sb_r, st_f, st_b, pre_f, pre_b):
        @pl.when(pl.program_id(0) == 0)
        def _():
            st_f[...] = jnp.zeros_like(st_f)
            st_b[...] = jnp.zeros_like(st_b)

        _gla_block_pre(qaf, kaf, zaf, wgf_r, bgf_r, False, nc, *pre_f)
        _gla_block_pre(qab, kab, zab, wgb_r, bgb_r, True, nc, *pre_b)
        tri_f, tri_b = _tri_masks()

        def one(tri, pre, v_r, o_r, s_r, st, ci):
            qd_s, ki_s, ks_s, dec_s = pre
            rows = pl.ds(pl.multiple_of(ci * GLA_CHUNK, GLA_CHUNK), GLA_CHUNK)
            dec = dec_s[ci]
            for h in range(GLA_HEADS):
                sl = slice(HEAD_PAD * h, HEAD_PAD * (h + 1))
                qd = qd_s[rows, sl]
                a = jnp.where(tri, _dot_nt(qd, ki_s[rows, sl]), 0.0)
                v = v_r[rows, sl]
                s_t = st[h]
                s_r[ci, h] = s_t
                o_r[rows, sl] = _dot(_mx(a), v) + _dot_nt(qd, _mx(s_t))
                st[h] = s_t * dec[:, sl] + _dot_tn(v, ks_s[rows, sl])

        def loop(t, carry):
            one(tri_f, pre_f, vaf, of_r, sf_r, st_f, t)
            one(tri_b, pre_b, vab, ob_r, sb_r, st_b, nc - 1 - t)
            return carry

        lax.fori_loop(0, nc, loop, 0, unroll=True)

    fwd = lambda i: (i, 0)
    bwd = lambda i: (nb - 1 - i, 0)
    ins = lambda m: [pl.BlockSpec((br, hw), m), pl.BlockSpec((br, hw), m),
                     pl.BlockSpec((br, hw), m), pl.BlockSpec((br, 128), m)]
    wspecs = [_full_spec((128, hw)), _full_spec((1, hw))] * 2
    s_shape = (nc, GLA_HEADS, HEAD_PAD, HEAD_PAD)
    pre_scratch = [pltpu.VMEM((br, hw), MXU_DTYPE)] * 3 + [pltpu.VMEM((nc, 1, hw), F32)]
    return pl.pallas_call(
        body, name="gla_fwd", grid=(nb,),
        in_specs=ins(fwd) + ins(bwd) + wspecs,
        out_specs=[pl.BlockSpec((br, hw), fwd), pl.BlockSpec((br, hw), bwd),
                   pl.BlockSpec(s_shape, lambda i: (i, 0, 0, 0)),
                   pl.BlockSpec(s_shape, lambda i: (nb - 1 - i, 0, 0, 0))],
        out_shape=[jax.ShapeDtypeStruct((L, hw), F32), jax.ShapeDtypeStruct((L, hw), F32),
                   jax.ShapeDtypeStruct((n_chunks,) + s_shape[1:], F32),
                   jax.ShapeDtypeStruct((n_chunks,) + s_shape[1:], F32)],
        scratch_shapes=[pltpu.VMEM(s_shape[1:], F32), pltpu.VMEM(s_shape[1:], F32), pre_scratch, pre_scratch],
        compiler_params=_params(("arbitrary",), VMEM_BIG),
    )(qa, ka, va, za, qa, ka, va, za, wgf, bgf, wgb, bgb)


def _gla_bwd_call(qa, ka, va, za, do, sf, sb, wgf, bgf, wgb, bgb, dep=None):
    L = qa.shape[0]
    br = min(256, L)
    nb, nc = L // br, br // GLA_CHUNK
    hw = GLA_HEADS * HEAD_PAD

    def body(qaf, kaf, vaf, zaf, dof, sf_r, qab, kab, vab, zab, dob, sb_r, wgf_r, bgf_r, wgb_r, bgb_r,
             dqf, dkf, dvf, dzf, dwf, dbf, dqb, dkb, dvb, dzb, dwb, dbb, gt_f, gt_b, pre_f, pre_b):
        @pl.when(pl.program_id(0) == 0)
        def _():
            for ref in (gt_f, gt_b, dwf, dbf, dwb, dbb):
                ref[...] = jnp.zeros_like(ref)

        _gla_block_pre(qaf, kaf, zaf, wgf_r, bgf_r, False, nc, *pre_f[:4], keep=pre_f[4:8])
        _gla_block_pre(qab, kab, zab, wgb_r, bgb_r, True, nc, *pre_b[:4], keep=pre_b[4:8])
        tri_f, tri_b = _tri_masks()
        row_w = lax.broadcasted_iota(jnp.int32, (GLA_CHUNK, HEAD_PAD), 0)

        def one(rev, pre, q_r, k_r, v_r, do_r, s_r, dq_r, dk_r, dv_r, gt, ci):
            qd_s, ki_s, ks_s, dec_s, _, eb_s, enb_s, elb_s, db_s = pre
            tri = tri_b if rev else tri_f
            last_row = 0 if rev else GLA_CHUNK - 1
            rows = pl.ds(pl.multiple_of(ci * GLA_CHUNK, GLA_CHUNK), GLA_CHUNK)
            dec = dec_s[ci]
            for h in range(GLA_HEADS):
                sl = slice(HEAD_PAD * h, HEAD_PAD * (h + 1))
                qd, ki, ks = qd_s[rows, sl], ki_s[rows, sl], ks_s[rows, sl]
                a = _mx(jnp.where(tri, _dot_nt(qd, ki), 0.0))
                v = v_r[rows, sl]
                do_h = _mx(do_r[rows, sl])
                s_t = s_r[ci, h]
                g_t = gt[h]
                g_m = _mx(g_t)
                da = _mx(jnp.where(tri, _dot_nt(do_h, v), 0.0))
                dv_r[rows, sl] = _dot_tn(a, do_h) + _dot_nt(ks, g_m)
                dqd = _dot(da, ki) + _dot(do_h, _mx(s_t))
                dki = _dot_tn(da, qd)
                dks = _dot(v, g_m)
                ddec = jnp.sum(g_t * s_t, axis=0, keepdims=True)
                gt[h] = g_t * dec[:, sl] + _dot_tn(do_h, qd)
                dq = dqd * eb_s[rows, sl] * 0.125
                dk_state = dks * elb_s[rows, sl]
                dk = dki * enb_s[rows, sl] + dk_state
                dq_r[rows, sl] = dq
                dk_r[rows, sl] = dk
                k = k_r[rows, sl]
                dblast = jnp.sum(dk_state * k, axis=0, keepdims=True) + dec[:, sl] * ddec
                db_s[rows, sl] = q_r[rows, sl] * dq - k * dk + jnp.where(row_w == last_row, dblast, 0.0)

        def loop(t, carry):
            one(False, pre_f, qaf, kaf, vaf, dof, sf_r, dqf, dkf, dvf, gt_f, nc - 1 - t)
            one(True, pre_b, qab, kab, vab, dob, sb_r, dqb, dkb, dvb, gt_b, t)
            return carry

        lax.fori_loop(0, nc, loop, 0, unroll=True)

        def gate_grads(rev, pre, z_r, w_r, dz_r, dw_r, dbias_r):
            g_s, db_s = pre[4], pre[8]
            back_m = _mx((tri_f if rev else tri_b).astype(F32))
            db = db_s[...]
            dla = jnp.concatenate([_chunk_sums(back_m, db[GLA_CHUNK * c:GLA_CHUNK * (c + 1)]) for c in range(nc)],
                                  axis=0)
            dg = dla * (1.0 / GLA_GATE_NORM) * (1.0 / (1.0 + jnp.exp(g_s[...])))
            dg_m = _mx(dg)
            dz_r[...] = _dot_nt(dg_m, w_r[...])
            dw_r[...] += _dot_tn(_mx(z_r[...]), dg_m)
            dbias_r[...] += jnp.sum(dg, axis=0, keepdims=True)

        gate_grads(False, pre_f, zaf, wgf_r, dzf, dwf, dbf)
        gate_grads(True, pre_b, zab, wgb_r, dzb, dwb, dbb)

    last_first = lambda i: (nb - 1 - i, 0)
    first_last = lambda i: (i, 0)
    s_shape = (nc, GLA_HEADS, HEAD_PAD, HEAD_PAD)

    def ins(m):
        return [pl.BlockSpec((br, hw), m), pl.BlockSpec((br, hw), m), pl.BlockSpec((br, hw), m),
                pl.BlockSpec((br, 128), m), pl.BlockSpec((br, hw), m),
                pl.BlockSpec(s_shape, lambda i: m(i) + (0, 0))]

    def outs(m):
        return [pl.BlockSpec((br, hw), m), pl.BlockSpec((br, hw), m), pl.BlockSpec((br, hw), m),
                pl.BlockSpec((br, 128), m), _full_spec((128, hw)), _full_spec((1, hw))]

    out_shape = [jax.ShapeDtypeStruct((L, hw), F32)] * 3 + [
        jax.ShapeDtypeStruct((L, 128), F32), jax.ShapeDtypeStruct((128, hw), F32),
        jax.ShapeDtypeStruct((1, hw), F32)]
    wspecs = [_full_spec((128, hw)), _full_spec((1, hw))] * 2
    body, extra, extra_specs = _after(body, 16, dep)
    pre_scratch = ([pltpu.VMEM((br, hw), MXU_DTYPE)] * 3 + [pltpu.VMEM((nc, 1, hw), F32)]
                   + [pltpu.VMEM((br, hw), F32)] * 5)
    return pl.pallas_call(
        body, name="gla_bwd", grid=(nb,),
        in_specs=ins(last_first) + ins(first_last) + wspecs + extra_specs,
        out_specs=outs(last_first) + outs(first_last),
        out_shape=out_shape + out_shape,
        scratch_shapes=[pltpu.VMEM(s_shape[1:], F32), pltpu.VMEM(s_shape[1:], F32), pre_scratch, pre_scratch],
        compiler_params=_params(("arbitrary",), VMEM_BIG),
    )(qa, ka, va, za, do, sf, qa, ka, va, za, do, sb, wgf, bgf, wgb, bgb, *extra)


def _t5_buckets(rel):
    nb = REL_BUCKETS // 2
    ret = (rel > 0).astype(np.int32) * nb
    n = np.abs(rel)
    max_exact = nb // 2
    large = max_exact + (np.log(np.maximum(n, 1).astype(np.float32) / max_exact)
                         / math.log(REL_MAX_DIST / max_exact) * (nb - max_exact)).astype(np.int32)
    large = np.minimum(large, nb - 1)
    return ret + np.where(n < max_exact, n, large)


SWA_GROUP = SWA_Q_HEADS // SWA_KV_HEADS
SWA_SPAN = 3 * SWA_BLOCK
SWA_GROUP_LANES = SWA_GROUP * SWA_BLOCK


def _band_buckets():
    s = np.arange(SWA_SPAN)[:, None]
    c = np.arange(SWA_BLOCK)[None, :]
    return _t5_buckets(s - SWA_BLOCK - c).astype(np.int32)


def _swa_valid(n, seq_len):
    s = lax.broadcasted_iota(jnp.int32, (SWA_SPAN, SWA_GROUP_LANES), 0)
    c = lax.broadcasted_iota(jnp.int32, (SWA_SPAN, SWA_GROUP_LANES), 1) & (SWA_BLOCK - 1)
    rel = s - SWA_BLOCK - c
    key_pos = (n - 1) * SWA_BLOCK + s
    return (jnp.abs(rel) <= SWA_BLOCK) & (key_pos >= 0) & (key_pos < seq_len)


def _swa_sink_row(sink_r, kv):
    lane = lax.broadcasted_iota(jnp.int32, (1, SWA_GROUP_LANES), 1)
    row = jnp.full((1, SWA_GROUP_LANES), sink_r[kv * SWA_GROUP], F32)
    for g in range(1, SWA_GROUP):
        row = jnp.where(lane >= g * SWA_BLOCK, sink_r[kv * SWA_GROUP + g], row)
    return row


def _swa_group(ref, kv):
    first = kv * SWA_GROUP
    return jnp.concatenate([ref[:, HEAD_PAD * h:HEAD_PAD * (h + 1)] for h in range(first, first + SWA_GROUP)],
                           axis=0)


def _swa_probs(kk, qg, bias_t, sink_row, valid):
    st = _dot_nt(kk, qg) * 0.125 + bias_t
    st = jnp.where(valid, st, -1e30)
    m = jnp.maximum(jnp.max(st, axis=0, keepdims=True), sink_row)
    p = jnp.exp(st - m)
    e_sink = jnp.exp(sink_row - m)
    inv = 1.0 / (jnp.sum(p, axis=0, keepdims=True) + e_sink)
    return p * inv, e_sink * inv


def _swa_fwd_call(qs, ks, vs, bias, sink, dep=None):
    L = qs.shape[0]

    def body(q_r, k_r, v_r, bias_r, sink_r, o_r):
        n = pl.program_id(0)
        span = pl.ds(pl.multiple_of(n * SWA_BLOCK, SWA_BLOCK), SWA_SPAN)
        valid = _swa_valid(n, L)
        for kv in range(SWA_KV_HEADS):
            ksl = slice(HEAD_PAD * kv, HEAD_PAD * (kv + 1))
            pn, _ = _swa_probs(k_r[span, ksl], _swa_group(q_r, kv), bias_r[kv], _swa_sink_row(sink_r, kv), valid)
            og = _dot_tn(_mx(pn), v_r[span, ksl])
            low = _low_half(SWA_BLOCK)
            for pair in range(SWA_GROUP // 2):
                even = og[2 * SWA_BLOCK * pair:2 * SWA_BLOCK * pair + SWA_BLOCK]
                odd = og[2 * SWA_BLOCK * pair + SWA_BLOCK:2 * SWA_BLOCK * (pair + 1)]
                first = HEAD_PAD * (kv * SWA_GROUP // 2 + pair)
                o_r[:, first:first + HEAD_PAD] = jnp.where(low, even, pltpu.roll(odd, 64, 1)).astype(o_r.dtype)

    qw = SWA_Q_HEADS * HEAD_PAD
    body, extra, extra_specs = _after(body, 5, dep)
    return pl.pallas_call(
        body, name="swa_fwd", grid=(L // SWA_BLOCK,),
        in_specs=[_row_spec(SWA_BLOCK, qw), _vmem_spec(), _vmem_spec(), _vmem_spec(),
                  pl.BlockSpec(memory_space=pltpu.SMEM)] + extra_specs,
        out_specs=_row_spec(SWA_BLOCK, qw // 2),
        out_shape=jax.ShapeDtypeStruct((L, qw // 2), MXU_DTYPE),
        compiler_params=_params(("arbitrary",), VMEM_BIG),
    )(qs, ks, vs, bias, sink, *extra)


def _swa_bwd_call(qs, ks, vs, bias, sink, do, dep=None):
    L = qs.shape[0]
    qw = SWA_Q_HEADS * HEAD_PAD
    kw = SWA_KV_HEADS * HEAD_PAD

    def body(q_r, k_r, v_r, bias_r, sink_r, do_r, dq_r, dk_r, dv_r, dbias_r, dsink_r):
        n = pl.program_id(0)

        @pl.when(n == 0)
        def _():
            for ref in (dk_r, dv_r, dbias_r, dsink_r):
                ref[...] = jnp.zeros_like(ref)

        span = pl.ds(pl.multiple_of(n * SWA_BLOCK, SWA_BLOCK), SWA_SPAN)
        valid = _swa_valid(n, L)
        for kv in range(SWA_KV_HEADS):
            ksl = slice(HEAD_PAD * kv, HEAD_PAD * (kv + 1))
            kk = k_r[span, ksl]
            vv = v_r[span, ksl]
            qg = _swa_group(q_r, kv)
            dog = _swa_group(do_r, kv)
            pn, p_sink = _swa_probs(kk, qg, bias_r[kv], _swa_sink_row(sink_r, kv), valid)
            dp = _dot_nt(vv, dog)
            delta = jnp.sum(pn * dp, axis=0, keepdims=True)
            ds = pn * (dp - delta)
            dsink_r[kv] -= p_sink * delta
            dbias_r[kv] += ds
            ds_m = _mx(ds)
            dqg = _dot_tn(ds_m, kk) * 0.125
            for g in range(SWA_GROUP):
                h = kv * SWA_GROUP + g
                dq_r[:, HEAD_PAD * h:HEAD_PAD * (h + 1)] = dqg[SWA_BLOCK * g:SWA_BLOCK * (g + 1)]
            dk_r[span, ksl] += _dot(ds_m, qg) * 0.125
            dv_r[span, ksl] += _dot(_mx(pn), dog)

    body, extra, extra_specs = _after(body, 6, dep)
    return pl.pallas_call(
        body, name="swa_bwd", grid=(L // SWA_BLOCK,),
        in_specs=[_row_spec(SWA_BLOCK, qw), _vmem_spec(), _vmem_spec(), _vmem_spec(),
                  pl.BlockSpec(memory_space=pltpu.SMEM), _row_spec(SWA_BLOCK, qw)] + extra_specs,
        out_specs=[_row_spec(SWA_BLOCK, qw), _vmem_spec(), _vmem_spec(), _vmem_spec(), _vmem_spec()],
        out_shape=[jax.ShapeDtypeStruct((L, qw), F32),
                   jax.ShapeDtypeStruct((L + 2 * SWA_BLOCK, kw), F32),
                   jax.ShapeDtypeStruct((L + 2 * SWA_BLOCK, kw), F32),
                   jax.ShapeDtypeStruct((SWA_KV_HEADS, SWA_SPAN, SWA_GROUP_LANES), F32),
                   jax.ShapeDtypeStruct((SWA_KV_HEADS, 1, SWA_GROUP_LANES), F32)],
        compiler_params=_params(("arbitrary",), VMEM_BIG),
    )(qs, ks, vs, bias, sink, do, *extra)


def _bias_call(rel_bias, buckets):
    def body(t_r, bk_r, o_r):
        bk = bk_r[...]
        for h in range(SWA_Q_HEADS):
            acc = jnp.zeros(bk.shape, F32)
            for b in range(REL_BUCKETS):
                acc = jnp.where(bk == b, t_r[b, h], acc)
            g = h % SWA_GROUP
            o_r[h // SWA_GROUP, :, SWA_BLOCK * g:SWA_BLOCK * (g + 1)] = acc

    return pl.pallas_call(
        body, name="band_bias",
        in_specs=[pl.BlockSpec(memory_space=pltpu.SMEM), _vmem_spec()], out_specs=_vmem_spec(),
        out_shape=jax.ShapeDtypeStruct((SWA_KV_HEADS, SWA_SPAN, SWA_GROUP_LANES), F32),
    )(rel_bias, buckets)


def _relbias_call(dbias, dsink, buckets):
    def body(db_r, ds_r, bk_r, o_r, os_r):
        bk = bk_r[...]
        rowi = lax.broadcasted_iota(jnp.int32, (REL_BUCKETS, 128), 0)
        lanei = lax.broadcasted_iota(jnp.int32, (REL_BUCKETS, 128), 1)
        lane1 = lax.broadcasted_iota(jnp.int32, (1, 128), 1)
        acc = jnp.zeros((REL_BUCKETS, 128), F32)
        acc_sink = jnp.zeros((1, 128), F32)
        for h in range(SWA_Q_HEADS):
            kv, g = h // SWA_GROUP, h % SWA_GROUP
            lanes = slice(SWA_BLOCK * g, SWA_BLOCK * (g + 1))
            part = db_r[kv, :, lanes]
            for b in range(REL_BUCKETS):
                s = jnp.sum(jnp.where(bk == b, part, 0.0))
                acc = acc + jnp.where((rowi == b) & (lanei == h), s, 0.0)
            acc_sink = acc_sink + jnp.where(lane1 == h, jnp.sum(ds_r[kv, :, lanes]), 0.0)
        o_r[...] = acc
        os_r[...] = acc_sink

    return pl.pallas_call(
        body, name="relbias_grad",
        in_specs=[_vmem_spec()] * 3, out_specs=[_vmem_spec()] * 2,
        out_shape=[jax.ShapeDtypeStruct((REL_BUCKETS, 128), F32), jax.ShapeDtypeStruct((1, 128), F32)],
    )(dbias, dsink, buckets)


def _mix_call(o_f, o_b, ga, o_s, x, gn, w_out_p, g_post, g_pre2):
    L = x.shape[0]
    tm = min(256, L)
    hw = GLA_HEADS * HEAD_PAD

    def body(of_r, ob_r, ga_r, os_r, x_r, gn_r, w_r, gp_r, g2_r, cat_r, mix_r, h1_r, n2_r):
        gn_v = gn_r[...]
        for h in range(GLA_HEADS):
            sl = slice(HEAD_PAD * h, HEAD_PAD * (h + 1))
            oh = of_r[:, sl] + ob_r[:, sl]
            on = oh * _rms_r(oh) * gn_v
            gate = ga_r[:, sl]
            cat_r[:, sl] = (on * (gate * jax.nn.sigmoid(gate))).astype(cat_r.dtype)
        os_v = os_r[...]
        cat_r[:, hw:] = os_v
        mix = _dot(cat_r[:, :hw], w_r[:hw, :]) + _dot(os_v, w_r[hw:, :])
        mix_r[...] = mix
        h1 = x_r[...] + mix * _rms_r(mix) * gp_r[...]
        h1_r[...] = h1
        n2_r[...] = (h1 * _rms_r(h1) * g2_r[...]).astype(n2_r.dtype)

    return pl.pallas_call(
        body, name="mix_fwd", grid=(L // tm,),
        in_specs=[_row_spec(tm, hw), _row_spec(tm, hw), _row_spec(tm, hw), _row_spec(tm, OUT_PAD - hw),
                  _row_spec(tm, D_MODEL), _full_spec((1, HEAD_PAD)), _vmem_spec(),
                  _full_spec((1, D_MODEL)), _full_spec((1, D_MODEL))],
        out_specs=[_row_spec(tm, OUT_PAD), _row_spec(tm, D_MODEL), _row_spec(tm, D_MODEL), _row_spec(tm, D_MODEL)],
        out_shape=[jax.ShapeDtypeStruct((L, OUT_PAD), MXU_DTYPE), jax.ShapeDtypeStruct((L, D_MODEL), F32),
                   jax.ShapeDtypeStruct((L, D_MODEL), F32), jax.ShapeDtypeStruct((L, D_MODEL), MXU_DTYPE)],
        compiler_params=_params(("arbitrary",), VMEM_BIG),
    )(o_f, o_b, ga, o_s, x, gn, w_out_p, g_post, g_pre2)


def _mlp_fwd_call(n2, h1, tgt, w_ud, g_post):
    L = n2.shape[0]
    tm = min(512, L)
    blk = D_FF // N_CHIPS

    def body(n2_r, h1_r, t_r, w_r, g_r, a_r, rz_r, dh2_r, dff_r, loss_r, dg_r):
        @pl.when(pl.program_id(0) == 0)
        def _():
            loss_r[...] = jnp.zeros_like(loss_r)
            dg_r[...] = jnp.zeros_like(dg_r)

        n2v = n2_r[...]
        ff = jnp.zeros((tm, D_MODEL), F32)
        for j in range(N_CHIPS):
            sl = slice(blk * j, blk * (j + 1))
            rz = jnp.maximum(_dot(n2v, w_r[j, 0]), 0.0)
            a = _mx(rz * rz)
            rz_r[:, sl] = rz.astype(rz_r.dtype)
            a_r[:, sl] = a
            ff = ff + _dot(a, w_r[j, 1])
        g = g_r[...]
        r = _rms_r(ff)
        err = h1_r[...] + ff * r * g - t_r[...]
        loss_r[...] += 0.5 * jnp.sum(err * err) / D_MODEL
        dh2 = err * (1.0 / D_MODEL)
        dh2_r[...] = dh2
        dff, dg = _rms_bwd(ff, r, g, dh2)
        dff_r[...] = dff.astype(dff_r.dtype)
        dg_r[...] += dg

    return pl.pallas_call(
        body, name="mlp_fwd", grid=(L // tm,),
        in_specs=[_row_spec(tm, D_MODEL), _row_spec(tm, D_MODEL), _row_spec(tm, D_MODEL),
                  _vmem_spec(), _full_spec((1, D_MODEL))],
        out_specs=[_row_spec(tm, D_FF), _row_spec(tm, D_FF), _row_spec(tm, D_MODEL), _row_spec(tm, D_MODEL),
                   _full_spec((1, 128)), _full_spec((1, D_MODEL))],
        out_shape=[jax.ShapeDtypeStruct((L, D_FF), MXU_DTYPE), jax.ShapeDtypeStruct((L, D_FF), MXU_DTYPE),
                   jax.ShapeDtypeStruct((L, D_MODEL), F32), jax.ShapeDtypeStruct((L, D_MODEL), MXU_DTYPE),
                   jax.ShapeDtypeStruct((1, 128), F32), jax.ShapeDtypeStruct((1, D_MODEL), F32)],
        compiler_params=_params(("arbitrary",), VMEM_BIG),
    )(n2, h1, tgt, w_ud, g_post)


def _mlp_bwd_call(dff, rz, w_ud):
    L = dff.shape[0]
    tm = min(512, L)
    blk = D_FF // N_CHIPS

    def body(dff_r, rz_r, w_r, dz_r, dn2_r):
        dffv = dff_r[...]
        dn2 = jnp.zeros((tm, D_MODEL), F32)
        for j in range(N_CHIPS):
            sl = slice(blk * j, blk * (j + 1))
            dz = _mx(_dot_nt(dffv, w_r[j, 1]) * 2.0 * rz_r[:, sl].astype(F32))
            dz_r[:, sl] = dz
            dn2 = dn2 + _dot_nt(dz, w_r[j, 0])
        dn2_r[...] = dn2

    return pl.pallas_call(
        body, name="mlp_bwd", grid=(L // tm,),
        in_specs=[_row_spec(tm, D_MODEL), _row_spec(tm, D_FF), _vmem_spec()],
        out_specs=[_row_spec(tm, D_FF), _row_spec(tm, D_MODEL)],
        out_shape=[jax.ShapeDtypeStruct((L, D_FF), MXU_DTYPE), jax.ShapeDtypeStruct((L, D_MODEL), F32)],
        compiler_params=_params(("arbitrary",), VMEM_BIG),
    )(dff, rz, w_ud)


def _mlp_wgrad_call(a, dff, n2, dz):
    L = a.shape[0]
    tf = 512
    per = (D_FF // N_CHIPS) // tf

    def body(a_r, dff_r, n2_r, dz_r, dwd_r, dwu_r):
        dwd_r[...] = _dot_tn(a_r[...], dff_r[...])
        dwu_r[...] = _dot_tn(n2_r[...], dz_r[...])

    return pl.pallas_call(
        body, name="mlp_wgrad", grid=(D_FF // tf,),
        in_specs=[pl.BlockSpec((L, tf), lambda j: (0, j)), _vmem_spec(), _vmem_spec(),
                  pl.BlockSpec((L, tf), lambda j: (0, j))],
        out_specs=[pl.BlockSpec((tf, D_MODEL), lambda j: (j, 0)),
                   pl.BlockSpec((None, D_MODEL, tf), lambda j: (j // per, 0, j % per))],
        out_shape=[jax.ShapeDtypeStruct((D_FF, D_MODEL), F32),
                   jax.ShapeDtypeStruct((N_CHIPS, D_MODEL, D_FF // N_CHIPS), F32)],
        compiler_params=_params(("arbitrary",), VMEM_BIG),
    )(a, dff, n2, dz)


def _mix_bwd_call(dn2, dh2, h1, mix, cat, o_f, o_b, ga, gn, g_post, g_pre2, w_out_p):
    L = dn2.shape[0]
    tm = min(256, L)
    hw = GLA_HEADS * HEAD_PAD

    def body(dn2_r, dh2_r, h1_r, mix_r, cat_r, of_r, ob_r, ga_r, gn_r, gp_r, g2_r, w_r,
             dh1_r, do_r, dga_r, dos_r, dw_r, dg2_r, dgp_r, dgn_r):
        @pl.when(pl.program_id(0) == 0)
        def _():
            for ref in (dw_r, dg2_r, dgp_r, dgn_r):
                ref[...] = jnp.zeros_like(ref)

        h1 = h1_r[...]
        dx2, dg2 = _rms_bwd(h1, _rms_r(h1), g2_r[...], dn2_r[...])
        dh1 = dh2_r[...] + dx2
        dh1_r[...] = dh1
        dg2_r[...] += dg2
        mix = mix_r[...]
        dmix, dgp = _rms_bwd(mix, _rms_r(mix), gp_r[...], dh1)
        dgp_r[...] += dgp
        dmix_m = _mx(dmix)
        dw_r[...] += _dot_tn(cat_r[...], dmix_m)
        dcat = _dot_nt(dmix_m, w_r[...])
        dos_r[...] = _spread_heads(dcat[:, hw:]).astype(dos_r.dtype)
        gn_v = gn_r[...]
        dgn = jnp.zeros((1, HEAD_PAD), F32)
        for h in range(GLA_HEADS):
            sl = slice(HEAD_PAD * h, HEAD_PAD * (h + 1))
            oh = of_r[:, sl] + ob_r[:, sl]
            rr = _rms_r(oh)
            gate = ga_r[:, sl]
            sg = jax.nn.sigmoid(gate)
            doa = dcat[:, sl]
            dga_r[:, sl] = doa * (oh * rr * gn_v) * (sg * (1.0 + gate * (1.0 - sg)))
            do_h, dgn_h = _rms_bwd(oh, rr, gn_v, doa * (gate * sg))
            do_r[:, sl] = do_h
            dgn = dgn + dgn_h
        dgn_r[...] += dgn

    return pl.pallas_call(
        body, name="mix_bwd", grid=(L // tm,),
        in_specs=[_row_spec(tm, D_MODEL)] * 4 + [_row_spec(tm, OUT_PAD)] + [_row_spec(tm, hw)] * 3
        + [_full_spec((1, HEAD_PAD)), _full_spec((1, D_MODEL)), _full_spec((1, D_MODEL)), _vmem_spec()],
        out_specs=[_row_spec(tm, D_MODEL), _row_spec(tm, hw), _row_spec(tm, hw),
                   _row_spec(tm, SWA_Q_HEADS * HEAD_PAD),
                   _full_spec((OUT_PAD, D_MODEL)), _full_spec((1, D_MODEL)), _full_spec((1, D_MODEL)),
                   _full_spec((1, HEAD_PAD))],
        out_shape=[jax.ShapeDtypeStruct((L, D_MODEL), F32), jax.ShapeDtypeStruct((L, hw), F32),
                   jax.ShapeDtypeStruct((L, hw), F32), jax.ShapeDtypeStruct((L, SWA_Q_HEADS * HEAD_PAD), MXU_DTYPE),
                   jax.ShapeDtypeStruct((OUT_PAD, D_MODEL), F32), jax.ShapeDtypeStruct((1, D_MODEL), F32),
                   jax.ShapeDtypeStruct((1, D_MODEL), F32), jax.ShapeDtypeStruct((1, HEAD_PAD), F32)],
        compiler_params=_params(("arbitrary",), VMEM_BIG),
    )(dn2, dh2, h1, mix, cat, o_f, o_b, ga, gn, g_post, g_pre2, w_out_p)


def _in_bwd_call(x, dh1, g_pre, w_in_t, pairs, singles, halos, dep=None):
    L = x.shape[0]
    tm = min(256, L)
    n_pair, n_single, n_halo = len(pairs), len(singles), len(halos)
    groups = [c for c, _ in pairs] + [c for c, _ in singles] + [c for c, _ in halos]

    def body(*refs):
        x_r, dh1_r, g_r, w_r = refs[:4]
        pair_refs = refs[4:4 + 2 * n_pair]
        single_refs = refs[4 + 2 * n_pair:4 + 2 * n_pair + n_single]
        halo_refs = refs[4 + 2 * n_pair + n_single:4 + 2 * n_pair + n_single + n_halo]
        dx_r, dw_r, dg_r = refs[4 + 2 * n_pair + n_single + n_halo:]

        @pl.when(pl.program_id(0) == 0)
        def _():
            dw_r[...] = jnp.zeros_like(dw_r)
            dg_r[...] = jnp.zeros_like(dg_r)

        xv = x_r[...]
        r = _rms_r(xv)
        g = g_r[...]
        u = _mx(xv * r * g)
        vals = [pair_refs[2 * i][...] + pair_refs[2 * i + 1][...] for i in range(n_pair)]
        vals += [ref[...].astype(F32) for ref in single_refs]
        inner = pl.ds(pl.multiple_of(pl.program_id(0) * tm + SWA_BLOCK, SWA_BLOCK), tm)
        vals += [ref[inner, :] for ref in halo_refs]
        du = jnp.zeros((tm, D_MODEL), F32)
        for (first, rows, heads), val in zip(groups, vals):
            d = _mx(_squeeze_heads(val) if heads else val)
            du = du + _dot(d, w_r[first:first + rows, :])
            dw_r[first:first + rows, :] += _dot_tn(d, u)
        dx, dg = _rms_bwd(xv, r, g, du)
        dx_r[...] = dh1_r[...] + dx
        dg_r[...] += dg

    arrays = [a for _, pr in pairs for a in pr] + [a for _, a in singles]
    specs = [_row_spec(tm, a.shape[1]) for a in arrays] + [_vmem_spec()] * n_halo
    arrays += [a for _, a in halos]
    body, extra, extra_specs = _after(body, 4 + len(arrays), dep)
    return pl.pallas_call(
        body, name="in_bwd", grid=(L // tm,),
        in_specs=[_row_spec(tm, D_MODEL), _row_spec(tm, D_MODEL), _full_spec((1, D_MODEL)), _vmem_spec()] + specs
        + extra_specs,
        out_specs=[_row_spec(tm, D_MODEL), _full_spec((IN_COLS, D_MODEL)), _full_spec((1, D_MODEL))],
        out_shape=[jax.ShapeDtypeStruct((L, D_MODEL), F32), jax.ShapeDtypeStruct((IN_COLS, D_MODEL), F32),
                   jax.ShapeDtypeStruct((1, D_MODEL), F32)],
        compiler_params=_params(("arbitrary",), VMEM_BIG),
    )(x, dh1, g_pre, w_in_t, *arrays, *extra)


def _adamw_math(w, g, m, v):
    m = ADAM_B1 * m + (1.0 - ADAM_B1) * g
    v = ADAM_B2 * v + (1.0 - ADAM_B2) * (g * g)
    m_hat = m / (1.0 - ADAM_B1 ** ADAM_STEP)
    v_hat = v / (1.0 - ADAM_B2 ** ADAM_STEP)
    delta = -ADAM_LR * (m_hat / (jnp.sqrt(v_hat) + ADAM_EPS) + ADAM_WD * w)
    return delta, m, v


def _adamw_call(w, g, m, v, name, dep=None):
    rows, cols = w.shape
    tr = min(256, rows)

    def body(w_r, g_r, m_r, v_r, d_r, nm_r, nv_r):
        d_r[...], nm_r[...], nv_r[...] = _adamw_math(w_r[...], g_r[...], m_r[...], v_r[...])

    if rows % tr == 0:
        spec, steps = _row_spec(tr, cols), rows // tr
    else:
        spec, steps = pl.BlockSpec((rows, 256), lambda i: (0, i)), cols // 256
    body, extra, extra_specs = _after(body, 4, dep)
    return pl.pallas_call(
        body, name=name, grid=(steps,),
        in_specs=[spec] * 4 + extra_specs, out_specs=[spec] * 3,
        out_shape=[jax.ShapeDtypeStruct(w.shape, F32)] * 3,
        compiler_params=_params(("arbitrary",)),
    )(w, g, m, v, *extra)


def _position():
    return lax.axis_index("x"), lax.axis_index("y"), lax.axis_index("c")


def _other_chips(x, y):
    return [(1 - x, y), (x, 1 - y), (1 - x, 1 - y)]


ROWS, COLS = -2, -1


def _half(ref, which, axis):
    size = ref.shape[axis] // 2
    span = pl.ds(pl.multiple_of(which * size, 16 if axis == ROWS else 128), size)
    index = [slice(None)] * len(ref.shape)
    index[axis] = span
    return ref.at[tuple(index)]


def _first_gather_call(shards, axes):
    n = len(shards)

    def body(*refs):
        srcs, outs = refs[:n], refs[n:2 * n]
        send_sems, recv_sems, local_sems = refs[2 * n:]
        x, y, c = _position()
        sibling = (x, y, 1 - c)
        chips = _other_chips(x, y)
        local = [pltpu.make_async_copy(srcs[a], outs[a].at[2 * x + y], local_sems.at[a]) for a in range(n)]
        for cp in local:
            cp.start()

        def copy(a, k, block, to, src=None):
            px, py, pc = block
            dst = _half(outs[a].at[2 * px + py], pc, axes[a])
            return pltpu.make_async_remote_copy(
                src_ref=dst if src is None else src, dst_ref=dst, send_sem=send_sems.at[6 * a + k],
                recv_sem=recv_sems.at[6 * a + k], device_id=to, device_id_type=MESH_ID)

        first, passed = [], []
        for a in range(n):
            my_half = _half(srcs[a], c, axes[a])
            first += [copy(a, j, (x, y, c), (*chip, c), src=my_half) for j, chip in enumerate(chips)]
        for cp in first:
            cp.start()
        for a in range(n):
            for j, chip in enumerate(chips):
                copy(a, j, (*chip, c), (x, y, c)).wait_recv()
                passed.append(copy(a, 3 + j, (*chip, c), sibling))
                passed[-1].start()
        for a in range(n):
            for j, chip in enumerate(chips):
                copy(a, 3 + j, (*chip, 1 - c), (x, y, c)).wait_recv()
        for cp in first + passed:
            cp.wait_send()
        for cp in local:
            cp.wait()

    return pl.pallas_call(
        body, name="first_gather",
        in_specs=[_any_spec()] * n, out_specs=[_any_spec()] * n,
        out_shape=[jax.ShapeDtypeStruct((N_CHIPS,) + s.shape, s.dtype) for s in shards],
        scratch_shapes=[pltpu.SemaphoreType.DMA((6 * n,)), pltpu.SemaphoreType.DMA((6 * n,)),
                        pltpu.SemaphoreType.DMA((n,))],
    )(*shards)


def _split_start(name, arrays, n_copies, plan):
    n = len(arrays)

    def body(*refs):
        ins, send_sems, recv_sems, token = refs[:n], refs[n], refs[n + 1], refs[-1]
        for k, (src, dst, to, _) in enumerate(plan(ins)):
            pltpu.make_async_remote_copy(src_ref=src, dst_ref=dst, send_sem=send_sems.at[k],
                                         recv_sem=recv_sems.at[k], device_id=to, device_id_type=MESH_ID).start()
        token[...] = jnp.zeros_like(token)

    hbm = pl.BlockSpec(memory_space=pltpu.HBM)
    sem = pl.BlockSpec(memory_space=pltpu.SEMAPHORE)
    out = pl.pallas_call(
        body, name=name,
        out_shape=(pltpu.SemaphoreType.DMA((n_copies,)), pltpu.SemaphoreType.DMA((n_copies,)))
        + tuple(pltpu.HBM(a.shape, a.dtype) for a in arrays) + (jax.ShapeDtypeStruct((8, 128), F32),),
        in_specs=[hbm] * n, out_specs=(sem, sem) + (hbm,) * n + (_vmem_spec(),),
        input_output_aliases={i: 2 + i for i in range(n)},
        compiler_params=pltpu.CompilerParams(has_side_effects=pltpu.SideEffectType.DATAFLOW_SIDE_EFFECTING),
    )(*[pltpu.with_memory_space_constraint(a, pltpu.HBM) for a in arrays])
    return (out[0], out[1], tuple(out[2:2 + n])), out[-1]


def _split_wait(name, handle, n_copies, plan, after):
    send_sems, recv_sems, arrays = handle
    n = len(arrays)

    def body(*refs):
        ins, s_sems, r_sems = refs[:n], refs[n], refs[n + 1]
        for k, (src, dst, to, landed) in enumerate(plan(ins)):
            cp = pltpu.make_async_remote_copy(src_ref=src, dst_ref=landed, send_sem=s_sems.at[k],
                                              recv_sem=r_sems.at[k], device_id=to, device_id_type=MESH_ID)
            cp.wait_send()
            cp.wait_recv()

    hbm = pl.BlockSpec(memory_space=pltpu.HBM)
    sem = pl.BlockSpec(memory_space=pltpu.SEMAPHORE)
    out = pl.pallas_call(
        body, name=name,
        out_shape=tuple(pltpu.HBM(a.shape, a.dtype) for a in arrays),
        in_specs=[hbm] * n + [sem, sem, _any_spec()], out_specs=(hbm,) * n,
        input_output_aliases={i: i for i in range(n)},
        compiler_params=pltpu.CompilerParams(has_side_effects=pltpu.SideEffectType.DATAFLOW_SIDE_EFFECTING),
    )(*arrays, send_sems, recv_sems, after)
    return tuple(out)


def _gather_plans(axes):
    n = len(axes)

    def stage_one(refs):
        x, y, c = _position()
        copies = []
        for a, axis in enumerate(axes):
            for px, py in _other_chips(x, y):
                copies.append((_half(refs[a], c, axis), _half(refs[n + a].at[2 * x + y], c, axis),
                               (px, py, c), _half(refs[n + a].at[2 * px + py], c, axis)))
        return copies

    def stage_two(refs):
        x, y, c = _position()
        copies = []
        for a, axis in enumerate(axes):
            for px, py in _other_chips(x, y):
                piece = _half(refs[n + a].at[2 * px + py], c, axis)
                copies.append((piece, piece, (x, y, 1 - c), _half(refs[n + a].at[2 * px + py], 1 - c, axis)))
        return copies

    return stage_one, stage_two


def _pair_swap_plan(axes):
    n = len(axes)

    def plan(refs):
        x, y, c = _position()
        return [(_half(refs[a], 1 - c, axes[a]), refs[n + a], (x, y, 1 - c), refs[n + a]) for a in range(n)]

    return plan


def _chip_swap_plan(n):
    def plan(refs):
        x, y, c = _position()
        copies = []
        for a in range(n):
            for j, (px, py) in enumerate(_other_chips(x, y)):
                copies.append((refs[a].at[2 * px + py], refs[n + a].at[j], (px, py, c), refs[n + a].at[j]))
        return copies

    return plan


def _pair_join_plan(axes):
    def plan(refs):
        x, y, c = _position()
        copies = []
        for a, axis in enumerate(axes):
            mine = _half(refs[a], c, axis)
            copies.append((mine, mine, (x, y, 1 - c), _half(refs[a], 1 - c, axis)))
        return copies

    return plan


def _pair_add_call(g, got, pos, name, axis):
    rows, cols = got.shape[1], got.shape[2]
    tr = min(256, rows) if axis == ROWS else rows
    nblk = rows // tr
    if axis == ROWS:
        mine = lambda j, i, p: (j, p[1] * nblk + i, 0)
    else:
        mine = lambda j, i, p: (j, 0, p[1])

    def body(pos_r, g_r, got_r, o_r):
        o_r[...] = (g_r[...] + got_r[...]).astype(o_r.dtype)

    return pl.pallas_call(
        body, name=name,
        grid_spec=pltpu.PrefetchScalarGridSpec(
            num_scalar_prefetch=1, grid=(N_CHIPS, nblk),
            in_specs=[pl.BlockSpec((None, tr, cols), mine),
                      pl.BlockSpec((None, tr, cols), lambda j, i, p: (j, i, 0))],
            out_specs=pl.BlockSpec((None, tr, cols), lambda j, i, p: (j, i, 0))),
        out_shape=jax.ShapeDtypeStruct(got.shape, COMM_DTYPE),
        compiler_params=_params(("arbitrary", "arbitrary")),
    )(pos, g, got)


def _chip_add_call(hsum, got, pos, name, axis):
    rows, cols = hsum.shape[1], hsum.shape[2]
    tr = min(256, rows) if axis == ROWS else rows
    nblk = rows // tr
    if axis == ROWS:
        out_shape, mine = (2 * rows, cols), (lambda i, p: (p[1] * nblk + i, 0))
    else:
        out_shape, mine = (rows, 2 * cols), (lambda i, p: (0, p[1]))

    def body(pos_r, own_r, got_r, o_r):
        acc = own_r[...].astype(F32)
        for j in range(3):
            acc = acc + got_r[j].astype(F32)
        o_r[...] = acc

    return pl.pallas_call(
        body, name=name,
        grid_spec=pltpu.PrefetchScalarGridSpec(
            num_scalar_prefetch=1, grid=(nblk,),
            in_specs=[pl.BlockSpec((None, tr, cols), lambda i, p: (p[0], i, 0)),
                      pl.BlockSpec((3, tr, cols), lambda i, p: (0, i, 0))],
            out_specs=pl.BlockSpec((tr, cols), mine)),
        out_shape=jax.ShapeDtypeStruct(out_shape, F32),
        compiler_params=_params(("arbitrary",)),
    )(pos, hsum, got)


SMALL_NAMES = ("norm_mix_pre", "norm_mix_post", "norm_mlp_pre", "norm_mlp_post", "b_gate_fwd", "b_gate_bwd",
               "gla_norm", "swa_sink", "rel_bias")


def _small_update_call(grads, gate_grads, params, dep=None):
    n_dev = 8
    n_small = len(SMALL_NAMES)
    wmv = [t for p in params for t in p]
    shapes = [p[0].shape for p in params]

    def body(*refs):
        g_refs = refs[:n_small + 3]
        wmv_refs = refs[n_small + 3:n_small + 3 + 3 * n_small]
        n_in = n_small + 3 + 3 * n_small
        out_refs = refs[n_in:n_in + 4 * n_small + 3]
        pack_a, pack_b, all_a, all_b, send_sems, recv_sems = refs[n_in + 4 * n_small + 3:]
        x, y, c = _position()
        me = 4 * x + 2 * y + c
        pack_a[...] = jnp.zeros_like(pack_a)
        pack_b[...] = jnp.zeros_like(pack_b)
        for i in range(4):
            pack_a[i:i + 1, :] = g_refs[i][...]
        pack_a[4:5, 0:256] = g_refs[4][...]
        pack_a[5:6, 0:256] = g_refs[5][...]
        pack_a[6:7, 0:128] = g_refs[6][...]
        pack_a[7:8, 0:128] = g_refs[7][...]
        pack_a[7:8, 128:256] = g_refs[11][...]
        pack_b[0:32, 0:128] = g_refs[8][...]
        pack_b[32:48, :] = g_refs[9][...]
        pack_b[48:64, :] = g_refs[10][...]
        all_a[me] = pack_a[...]
        all_b[me] = pack_b[...]
        copies = []
        for k in range(1, n_dev):
            fx, fy, fc = (k >> 2) & 1, (k >> 1) & 1, k & 1
            to = (1 - x if fx else x, 1 - y if fy else y, 1 - c if fc else c)
            for t, (pack, dst) in enumerate(((pack_a, all_a), (pack_b, all_b))):
                copies.append(pltpu.make_async_remote_copy(
                    src_ref=pack, dst_ref=dst.at[me], send_sem=send_sems.at[2 * (k - 1) + t],
                    recv_sem=recv_sems.at[2 * (k - 1) + t], device_id=to, device_id_type=MESH_ID))
        for cp in copies:
            cp.start()
        for cp in copies:
            cp.wait()
        sum_a, sum_b = all_a[0], all_b[0]
        for d in range(1, n_dev):
            sum_a = sum_a + all_a[d]
            sum_b = sum_b + all_b[d]
        gsum = [sum_a[0:1], sum_a[1:2], sum_a[2:3], sum_a[3:4], sum_a[4:5, 0:256], sum_a[5:6, 0:256],
                sum_a[6:7, 0:128], sum_a[7:8, 0:SWA_Q_HEADS], sum_b[0:32, 0:SWA_Q_HEADS]]
        for i in range(n_small):
            w_r, m_r, v_r = wmv_refs[3 * i:3 * i + 3]
            delta, new_m, new_v = _adamw_math(w_r[...], gsum[i], m_r[...], v_r[...])
            out_refs[4 * i][...] = gsum[i]
            out_refs[4 * i + 1][...] = delta
            out_refs[4 * i + 2][...] = new_m
            out_refs[4 * i + 3][...] = new_v
        out_refs[4 * n_small][...] = sum_b[32:48]
        out_refs[4 * n_small + 1][...] = sum_b[48:64]
        out_refs[4 * n_small + 2][...] = sum_a[7:8, 128:256]

    n_in = n_small + 3 + 3 * n_small
    body, extra, extra_specs = _after(body, n_in, dep)
    out_shape = [jax.ShapeDtypeStruct(s, F32) for s in shapes for _ in range(4)]
    out_shape += [jax.ShapeDtypeStruct((GLA_GATE_RANK, 256), F32)] * 2 + [jax.ShapeDtypeStruct((1, 128), F32)]
    out = pl.pallas_call(
        body, name="small_update",
        in_specs=[_vmem_spec()] * n_in + extra_specs, out_specs=[_vmem_spec()] * len(out_shape),
        out_shape=out_shape,
        scratch_shapes=[pltpu.VMEM((8, D_MODEL), F32), pltpu.VMEM((64, 256), F32),
                        pltpu.VMEM((n_dev, 8, D_MODEL), F32), pltpu.VMEM((n_dev, 64, 256), F32),
                        pltpu.SemaphoreType.DMA((2 * (n_dev - 1),)), pltpu.SemaphoreType.DMA((2 * (n_dev - 1),))],
    )(*grads, *gate_grads, *wmv, *extra)
    per_name = [tuple(out[4 * i:4 * i + 4]) for i in range(n_small)]
    return per_name, out[4 * n_small], out[4 * n_small + 1], out[4 * n_small + 2]


def _pad_heads(t, n_heads, axis=-1):
    axis = axis % t.ndim
    shape = t.shape
    t = t.reshape(shape[:axis] + (n_heads, 64) + shape[axis + 1:])
    pad = [(0, 0)] * t.ndim
    pad[axis + 1] = (0, HEAD_PAD - 64)
    return jnp.pad(t, pad).reshape(shape[:axis] + (n_heads * HEAD_PAD,) + shape[axis + 1:])


def _unpad_heads(t, n_heads, axis=-1):
    axis = axis % t.ndim
    shape = t.shape
    t = t.reshape(shape[:axis] + (n_heads, HEAD_PAD) + shape[axis + 1:])
    t = lax.slice_in_dim(t, 0, 64, axis=axis + 1)
    return t.reshape(shape[:axis] + (n_heads * 64,) + shape[axis + 1:])


def _pad_gate(w, first_row):
    return jnp.pad(_pad_heads(w, 4), ((first_row, 128 - GLA_GATE_RANK - first_row), (0, 0)))


def _own_slot(shard, chip):
    zone = lax.empty((N_CHIPS,) + shard.shape, shard.dtype)
    return lax.dynamic_update_slice(zone, shard[None], (chip,) + (0,) * shard.ndim)


def _reduce_to_owners(grads, axes, pos, tag, overlap):
    n = len(grads)

    def half_shape(g, axis):
        return (N_CHIPS, g.shape[1] // 2, g.shape[2]) if axis == ROWS else (N_CHIPS, g.shape[1], g.shape[2] // 2)

    lands = [lax.empty(half_shape(g, axis), F32) for g, axis in zip(grads, axes)]
    handle, token = _split_start(tag + "_pair_start", list(grads) + lands, n, _pair_swap_plan(axes))
    got = _split_wait(tag + "_pair_wait", handle, n, _pair_swap_plan(axes), overlap[0](token))
    sums = [_pair_add_call(got[a], got[n + a], pos, f"{tag}_pair_add{a}", axes[a]) for a in range(n)]
    lands = [lax.empty((3,) + s.shape[1:], s.dtype) for s in sums]
    handle, token = _split_start(tag + "_chip_start", sums + lands, 3 * n, _chip_swap_plan(n))
    got = _split_wait(tag + "_chip_wait", handle, 3 * n, _chip_swap_plan(n), overlap[1](token))
    halves = [_chip_add_call(got[a], got[n + a], pos, f"{tag}_chip_add{a}", axes[a]) for a in range(n)]
    handle, token = _split_start(tag + "_join_start", halves, n, _pair_join_plan(axes))
    return _split_wait(tag + "_join_wait", handle, n, _pair_join_plan(axes), overlap[2](token))


def kernel(x, norm_mix_pre, w_in, w_gate_up_fwd, b_gate_fwd, w_gate_up_bwd, b_gate_bwd, gla_norm, swa_sink, rel_bias, w_out, norm_mix_post, norm_mlp_pre, w_up, w_down, norm_mlp_post, loss_target, m_norm_mix_pre, m_w_in, m_w_gate_up_fwd, m_b_gate_fwd, m_w_gate_up_bwd, m_b_gate_bwd, m_gla_norm, m_swa_sink, m_rel_bias, m_w_out, m_norm_mix_post, m_norm_mlp_pre, m_w_up, m_w_down, m_norm_mlp_post, v_norm_mix_pre, v_w_in, v_w_gate_up_fwd, v_b_gate_fwd, v_w_gate_up_bwd, v_b_gate_bwd, v_gla_norm, v_swa_sink, v_rel_bias, v_w_out, v_norm_mix_post, v_norm_mlp_pre, v_w_up, v_w_down, v_norm_mlp_post):
    given = dict(locals())
    cx, cy, cc = _position()
    chip = (2 * cx + cy).astype(jnp.int32)
    pos = jnp.stack([chip, cc.astype(jnp.int32)])
    seq, tgt = x[0], loss_target[0]
    L = seq.shape[0]

    gates = jnp.concatenate([w_gate_up_fwd[0], w_gate_up_bwd[0]], axis=0).astype(COMM_DTYPE)
    all_in, all_gates = _first_gather_call([w_in[0].T.astype(COMM_DTYPE), gates], [COLS, ROWS])
    rest = [w_out[0].astype(COMM_DTYPE), jnp.stack([w_up[0], w_down[0]]).astype(COMM_DTYPE)]
    stage_one, stage_two = _gather_plans([ROWS, ROWS])
    handle, token = _split_start("gather_chip_start", rest + [_own_slot(s, chip) for s in rest] + [all_gates], 6,
                                 stage_one)

    w_in_t = _mx(all_in.reshape(IN_COLS, D_MODEL))
    gates_full = jnp.concatenate([all_gates[j] for j in range(N_CHIPS)], axis=1)
    wgf_p = _mx(_pad_gate(gates_full[:GLA_GATE_RANK], 0))
    wgb_p = _mx(_pad_gate(gates_full[GLA_GATE_RANK:], GLA_GATE_RANK))
    bf_p, bb_p = _pad_heads(b_gate_fwd, 4), _pad_heads(b_gate_bwd, 4)
    buckets = jnp.asarray(_band_buckets())
    bias = _bias_call(rel_bias, buckets)
    sink1 = swa_sink.reshape(SWA_Q_HEADS)

    qa, ka, va, ga, qs, ks, vs, za = _proj_call(seq, norm_mix_pre, w_in_t, dep=token)
    halo = ((SWA_BLOCK, SWA_BLOCK), (0, 0))
    ks_p, vs_p = jnp.pad(ks, halo), jnp.pad(vs, halo)
    o_f, o_b, s_f, s_b = _gla_fwd_call(qa, ka, va, za, wgf_p, bf_p, wgb_p, bb_p)
    arrays = _split_wait("gather_chip_wait", handle, 6, stage_one, o_f)
    handle, token = _split_start("gather_pair_start", list(arrays), 6, stage_two)
    o_s = _swa_fwd_call(qs, ks_p, vs_p, bias, sink1, dep=token)
    arrays = _split_wait("gather_pair_wait", handle, 6, stage_two, o_s)
    w_out_full = _mx(arrays[2].reshape(N_CHIPS * R_OUT, D_MODEL))
    w_ud = _mx(arrays[3])
    cat, mix, h1, n2 = _mix_call(o_f, o_b, ga, o_s, seq, gla_norm, w_out_full, norm_mix_post, norm_mlp_pre)
    a, rz, dh2, dff, loss, d_post2 = _mlp_fwd_call(n2, h1, tgt, w_ud, norm_mlp_post)

    dz, dn2 = _mlp_bwd_call(dff, rz, w_ud)
    dw_down, dw_up4 = _mlp_wgrad_call(a, dff, n2, dz)
    dh1, do, dga, dos, dw_out, d_pre2, d_post, d_gn = _mix_bwd_call(
        dn2, dh2, h1, mix, cat, o_f, o_b, ga, gla_norm, norm_mix_post, norm_mlp_pre, w_out_full)
    done = {}

    def gla_backward(tok):
        done["gla"] = _gla_bwd_call(qa, ka, va, za, do, s_f, s_b, wgf_p, bf_p, wgb_p, bb_p, dep=tok)
        return done["gla"][0]

    def swa_backward(tok):
        done["swa"] = _swa_bwd_call(qs, ks_p, vs_p, bias, sink1, dos, dep=tok)
        return done["swa"][0]

    def in_backward(tok):
        dqf, dkf, dvf, dzf, _, _, dqb, dkb, dvb, dzb, _, _ = done["gla"]
        dqs, dks_p, dvs_p, _, _ = done["swa"]
        done["in"] = _in_bwd_call(
            seq, dh1, norm_mix_pre, w_in_t,
            pairs=[(T_QA, (dqf, dqb)), (T_KA, (dkf, dkb)), (T_VA, (dvf, dvb)), (T_ZA, (dzf, dzb))],
            singles=[(T_GA, dga), (T_QS, dqs)], halos=[(T_KS, dks_p), (T_VS, dvs_p)], dep=tok)
        return done["in"][0]

    g_up, g_down, g_out = _reduce_to_owners(
        [dw_up4, dw_down.reshape(N_CHIPS, R_DOWN, D_MODEL), dw_out.reshape(N_CHIPS, R_OUT, D_MODEL)],
        [ROWS, ROWS, ROWS], pos, "mlp", [swa_backward, gla_backward, in_backward])
    dx, dw_in_t, d_pre = done["in"]
    dwf, dbf, dwb, dbb = done["gla"][4], done["gla"][5], done["gla"][10], done["gla"][11]
    drel, dsink = _relbias_call(done["swa"][3], done["swa"][4], buckets)

    small_grads = [d_pre, d_post, d_pre2, d_post2, _unpad_heads(dbf, 4), _unpad_heads(dbb, 4), d_gn, dsink, drel]
    gate_grads = [_unpad_heads(dwf[:GLA_GATE_RANK], 4), _unpad_heads(dwb[GLA_GATE_RANK:2 * GLA_GATE_RANK], 4)]
    small_params = [(given[n], given["m_" + n], given["v_" + n]) for n in SMALL_NAMES]
    upd = {}

    def update_up(tok):
        upd["w_up"] = (g_up,) + tuple(_adamw_call(w_up[0], g_up, m_w_up[0], v_w_up[0], "adamw_w_up", dep=tok))
        return upd["w_up"][1]

    def update_small(tok):
        per_name, gf_sum, gb_sum, upd["loss"] = _small_update_call(small_grads, gate_grads + [loss], small_params,
                                                                   dep=tok)
        upd.update(dict(zip(SMALL_NAMES, per_name)))
        for name, total in (("w_gate_up_fwd", gf_sum), ("w_gate_up_bwd", gb_sum)):
            g = lax.dynamic_slice(total, (0, chip * 64), (GLA_GATE_RANK, 64))
            upd[name] = (g,) + tuple(_adamw_call(given[name][0], g, given["m_" + name][0], given["v_" + name][0],
                                                 "adamw_" + name))
        upd["w_down"] = (g_down,) + tuple(
            _adamw_call(w_down[0], g_down, m_w_down[0], v_w_down[0], "adamw_w_down", dep=gf_sum))
        return upd["w_down"][1]

    def update_out(tok):
        upd["w_out"] = (g_out,) + tuple(_adamw_call(w_out[0], g_out, m_w_out[0], v_w_out[0], "adamw_w_out", dep=tok))
        return upd["w_out"][1]

    (g_in_t,) = _reduce_to_owners([dw_in_t.reshape(N_CHIPS, R_IN, D_MODEL)], [COLS], pos, "in",
                                  [update_up, update_small, update_out])
    in_t = (g_in_t,) + tuple(_adamw_call(w_in[0].T, g_in_t, m_w_in[0].T, v_w_in[0].T, "adamw_w_in"))
    upd["w_in"] = tuple(t.T for t in in_t)

    big = ("w_in", "w_gate_up_fwd", "w_gate_up_bwd", "w_out", "w_up", "w_down")
    names = ["norm_mix_pre", "w_in", "w_gate_up_fwd", "b_gate_fwd", "w_gate_up_bwd", "b_gate_bwd", "gla_norm",
             "swa_sink", "rel_bias", "w_out", "norm_mix_post", "norm_mlp_pre", "w_up", "w_down", "norm_mlp_post"]
    outs = [upd["loss"][0, 0], dx[None]]
    for kind in range(4):
        outs += [upd[n][kind][None] if n in big else upd[n][kind] for n in names]
    return tuple(outs)
```

```python
import math

import numpy as np
import jax
import jax.numpy as jnp
from jax import lax
from jax.experimental import pallas as pl
from jax.experimental.pallas import tpu as pltpu

F32 = jnp.float32
MXU_DTYPE = jnp.bfloat16
COMM_DTYPE = jnp.bfloat16

D_MODEL = 1024
D_FF = 4096
N_CHIPS = 4
GLA_HEADS = 4
GLA_CHUNK = 64
GLA_GATE_RANK = 16
GLA_GATE_NORM = 16.0
SWA_Q_HEADS = 8
SWA_KV_HEADS = 2
SWA_BLOCK = 128
REL_BUCKETS = 32
REL_MAX_DIST = 128
NORM_EPS = 1e-6
HEAD_PAD = 128

ADAM_LR = 0.001
ADAM_B1 = 0.9
ADAM_B2 = 0.999
ADAM_EPS = 1e-08
ADAM_WD = 0.01
ADAM_STEP = 10

OUT_PAD = 1024

R_IN, R_OUT, R_UP, R_DOWN = 584, 256, 1024, 1024

VMEM_BIG = 56 * 1024 * 1024
MESH_AXES = ("x", "y", "c")
MESH_ID = pl.DeviceIdType.MESH


def _mx(a):
    return a.astype(MXU_DTYPE)


def _dot(a, b):
    return jnp.dot(a, b, preferred_element_type=F32)


def _dot_nt(a, b):
    return lax.dot_general(a, b, (((1,), (1,)), ((), ())), preferred_element_type=F32)


def _dot_tn(a, b):
    return lax.dot_general(a, b, (((0,), (0,)), ((), ())), preferred_element_type=F32)


def _rms_r(x):
    return lax.rsqrt(jnp.mean(x * x, axis=-1, keepdims=True) + NORM_EPS)


def _rms_bwd(x, r, g, dy):
    xh = x * r
    gdy = dy * g
    dx = r * (gdy - xh * jnp.mean(gdy * xh, axis=-1, keepdims=True))
    return dx, jnp.sum(dy * xh, axis=0, keepdims=True)


def _low_half(rows):
    return lax.broadcasted_iota(jnp.int32, (rows, HEAD_PAD), 1) < 64


def _spread_heads(x):
    low = _low_half(x.shape[0])
    parts = []
    for p in range(x.shape[1] // HEAD_PAD):
        pair = x[:, HEAD_PAD * p:HEAD_PAD * (p + 1)]
        parts += [jnp.where(low, pair, 0.0), jnp.where(low, pltpu.roll(pair, 64, 1), 0.0)]
    return jnp.concatenate(parts, axis=1)


def _squeeze_heads(x):
    low = _low_half(x.shape[0])
    parts = []
    for p in range(x.shape[1] // (2 * HEAD_PAD)):
        even = x[:, 2 * HEAD_PAD * p:2 * HEAD_PAD * p + HEAD_PAD]
        odd = x[:, 2 * HEAD_PAD * p + HEAD_PAD:2 * HEAD_PAD * (p + 1)]
        parts.append(jnp.where(low, even, pltpu.roll(odd, 64, 1)))
    return parts[0] if len(parts) == 1 else jnp.concatenate(parts, axis=1)


def _params(sem=None, vmem=None):
    kw = {}
    if sem is not None:
        kw["dimension_semantics"] = sem
    if vmem is not None:
        kw["vmem_limit_bytes"] = vmem
    return pltpu.CompilerParams(**kw)


def _vmem_spec():
    return pl.BlockSpec(memory_space=pltpu.VMEM)


def _row_spec(tm, width):
    return pl.BlockSpec((tm, width), lambda i: (i, 0))


def _full_spec(shape):
    return pl.BlockSpec(shape, lambda i: (0,) * len(shape))


def _any_spec():
    return pl.BlockSpec(memory_space=pl.ANY)


def _after(body, n_in, dep):
    if dep is None:
        return body, [], []
    return (lambda *refs: body(*refs[:n_in], *refs[n_in + 1:])), [dep], [_any_spec()]


T_QA, T_KA, T_VA, T_GA = (0, 256, 4), (256, 256, 4), (512, 512, 0), (1024, 512, 0)
T_QS, T_KS, T_VS = (1568, 512, 8), (2080, 128, 2), (2208, 128, 2)
T_ZA = (1536, 128, 0)
ZA_COLS = 2 * GLA_GATE_RANK
IN_COLS = 2336


def _proj_call(x, g_pre, w_in_t, dep=None):
    L = x.shape[0]
    tm = min(256, L)
    groups = [(T_QA, F32), (T_KA, F32), (T_VA, MXU_DTYPE), (T_GA, F32),
              (T_QS, MXU_DTYPE), (T_KS, MXU_DTYPE), (T_VS, MXU_DTYPE), (T_ZA, F32)]
    widths = [rows * (2 if heads else 1) for (_, rows, heads), _ in groups]

    def body(x_ref, g_ref, w_ref, *outs):
        xv = x_ref[...]
        u = _mx(xv * _rms_r(xv) * g_ref[...])
        for ref, (grp, _) in zip(outs, groups):
            first, rows, heads = grp
            val = _dot_nt(u, w_ref[first:first + rows, :])
            if heads:
                val = _spread_heads(val)
            if grp is T_ZA:
                val = jnp.where(lax.broadcasted_iota(jnp.int32, val.shape, 1) < ZA_COLS, val, 0.0)
            ref[...] = val.astype(ref.dtype)

    body, extra, extra_specs = _after(body, 3, dep)
    return pl.pallas_call(
        body, name="proj_fwd", grid=(L // tm,),
        in_specs=[_row_spec(tm, D_MODEL), _full_spec((1, D_MODEL)), _vmem_spec()] + extra_specs,
        out_specs=[_row_spec(tm, w) for w in widths],
        out_shape=[jax.ShapeDtypeStruct((L, w), dt) for w, (_, dt) in zip(widths, groups)],
        compiler_params=_params(("arbitrary",), VMEM_BIG),
    )(x, g_pre, w_in_t, *extra)


def _tri_masks():
    row = lax.broadcasted_iota(jnp.int32, (GLA_CHUNK, GLA_CHUNK), 0)
    col = lax.broadcasted_iota(jnp.int32, (GLA_CHUNK, GLA_CHUNK), 1)
    return row >= col, row <= col


def _chunk_sums(tri_m, x):
    hi = _mx(x)
    rest = x - hi.astype(F32)
    mid = _mx(rest)
    lo = _mx(rest - mid.astype(F32))
    return _dot(tri_m, hi) + _dot(tri_m, mid) + _dot(tri_m, lo)


def _gla_block_pre(q_r, k_r, z_r, w_r, b_r, rev, nc, qd_s, ki_s, ks_s, dec_s, keep=None):
    tri_f, tri_b = _tri_masks()
    tri_m = _mx((tri_b if rev else tri_f).astype(F32))
    g = _dot(_mx(z_r[...]), w_r[...]) + b_r[...]
    la = (jnp.minimum(g, 0.0) - jnp.log(1.0 + jnp.exp(-jnp.abs(g)))) / GLA_GATE_NORM
    sums, lasts = [], []
    for c in range(nc):
        b_c = _chunk_sums(tri_m, la[GLA_CHUNK * c:GLA_CHUNK * (c + 1)])
        blast = b_c[0:1] if rev else b_c[GLA_CHUNK - 1:GLA_CHUNK]
        dec_s[c] = jnp.exp(blast)
        sums.append(b_c)
        lasts.append(jnp.broadcast_to(blast, b_c.shape))
    b = jnp.concatenate(sums, axis=0)
    eb = jnp.exp(b)
    enb = jnp.exp(-b)
    elb = jnp.exp(jnp.concatenate(lasts, axis=0) - b)
    k = k_r[...]
    qd_s[...] = (q_r[...] * 0.125 * eb).astype(qd_s.dtype)
    ki_s[...] = (k * enb).astype(ki_s.dtype)
    ks_s[...] = (k * elb).astype(ks_s.dtype)
    if keep is not None:
        for ref, val in zip(keep, (g, eb, enb, elb)):
            ref[...] = val


def _gla_fwd_call(qa, ka, va, za, wgf, bgf, wgb, bgb):
    L = qa.shape[0]
    br = min(512, L)
    nb, nc, n_chunks = L // br, br // GLA_CHUNK, L // GLA_CHUNK
    hw = GLA_HEADS * HEAD_PAD

    def body(qaf, kaf, vaf, zaf, qab, kab, vab, zab, wgf_r, bgf_r, wgb_r, bgb_r,
             of_r, ob_r, sf_r, sb_r, st_f, st_b, pre_f, pre_b):
        @pl.when(pl.program_id(0) == 0)
        def _():
            st_f[...] = jnp.zeros_like(st_f)
            st_b[...] = jnp.zeros_like(st_b)

        _gla_block_pre(qaf, kaf, zaf, wgf_r, bgf_r, False, nc, *pre_f)
        _gla_block_pre(qab, kab, zab, wgb_r, bgb_r, True, nc, *pre_b)
        tri_f, tri_b = _tri_masks()

        def one(tri, pre, v_r, o_r, s_r, st, ci):
            qd_s, ki_s, ks_s, dec_s = pre
            rows = pl.ds(pl.multiple_of(ci * GLA_CHUNK, GLA_CHUNK), GLA_CHUNK)
            dec = dec_s[ci]
            heads = range(GLA_HEADS)
            lanes = [slice(HEAD_PAD * h, HEAD_PAD * (h + 1)) for h in heads]
            qd = [qd_s[rows, sl] for sl in lanes]
            v = [v_r[rows, sl] for sl in lanes]
            s_t = [st[h] for h in heads]
            a = [_dot_nt(qd[h], ki_s[rows, lanes[h]]) for h in heads]
            carried = [_dot_nt(qd[h], _mx(s_t[h])) for h in heads]
            grown = [_dot_tn(v[h], ks_s[rows, lanes[h]]) for h in heads]
            a = [_mx(jnp.where(tri, a[h], 0.0)) for h in heads]
            inner = [_dot(a[h], v[h]) for h in heads]
            for h in heads:
                s_r[ci, h] = s_t[h]
                o_r[rows, lanes[h]] = inner[h] + carried[h]
                st[h] = s_t[h] * dec[:, lanes[h]] + grown[h]

        def loop(t, carry):
            one(tri_f, pre_f, vaf, of_r, sf_r, st_f, t)
            one(tri_b, pre_b, vab, ob_r, sb_r, st_b, nc - 1 - t)
            return carry

        lax.fori_loop(0, nc, loop, 0, unroll=True)

    fwd = lambda i: (i, 0)
    bwd = lambda i: (nb - 1 - i, 0)
    ins = lambda m: [pl.BlockSpec((br, hw), m), pl.BlockSpec((br, hw), m),
                     pl.BlockSpec((br, hw), m), pl.BlockSpec((br, 128), m)]
    wspecs = [_full_spec((128, hw)), _full_spec((1, hw))] * 2
    s_shape = (nc, GLA_HEADS, HEAD_PAD, HEAD_PAD)
    pre_scratch = [pltpu.VMEM((br, hw), MXU_DTYPE)] * 3 + [pltpu.VMEM((nc, 1, hw), F32)]
    return pl.pallas_call(
        body, name="gla_fwd", grid=(nb,),
        in_specs=ins(fwd) + ins(bwd) + wspecs,
        out_specs=[pl.BlockSpec((br, hw), fwd), pl.BlockSpec((br, hw), bwd),
                   pl.BlockSpec(s_shape, lambda i: (i, 0, 0, 0)),
                   pl.BlockSpec(s_shape, lambda i: (nb - 1 - i, 0, 0, 0))],
        out_shape=[jax.ShapeDtypeStruct((L, hw), F32), jax.ShapeDtypeStruct((L, hw), F32),
                   jax.ShapeDtypeStruct((n_chunks,) + s_shape[1:], F32),
                   jax.ShapeDtypeStruct((n_chunks,) + s_shape[1:], F32)],
        scratch_shapes=[pltpu.VMEM(s_shape[1:], F32), pltpu.VMEM(s_shape[1:], F32), pre_scratch, pre_scratch],
        compiler_params=_params(("arbitrary",), VMEM_BIG),
    )(qa, ka, va, za, qa, ka, va, za, wgf, bgf, wgb, bgb)


def _gla_bwd_call(qa, ka, va, za, do, sf, sb, wgf, bgf, wgb, bgb, dep=None):
    L = qa.shape[0]
    br = min(256, L)
    nb, nc = L // br, br // GLA_CHUNK
    hw = GLA_HEADS * HEAD_PAD

    def body(qaf, kaf, vaf, zaf, dof, sf_r, qab, kab, vab, zab, dob, sb_r, wgf_r, bgf_r, wgb_r, bgb_r,
             dqf, dkf, dvf, dzf, dwf, dbf, dqb, dkb, dvb, dzb, dwb, dbb, gt_f, gt_b, pre_f, pre_b):
        @pl.when(pl.program_id(0) == 0)
        def _():
            for ref in (gt_f, gt_b, dwf, dbf, dwb, dbb):
                ref[...] = jnp.zeros_like(ref)

        _gla_block_pre(qaf, kaf, zaf, wgf_r, bgf_r, False, nc, *pre_f[:4], keep=pre_f[4:8])
        _gla_block_pre(qab, kab, zab, wgb_r, bgb_r, True, nc, *pre_b[:4], keep=pre_b[4:8])
        tri_f, tri_b = _tri_masks()
        row_w = lax.broadcasted_iota(jnp.int32, (GLA_CHUNK, HEAD_PAD), 0)

        def one(rev, pre, q_r, k_r, v_r, do_r, s_r, dq_r, dk_r, dv_r, gt, ci):
            qd_s, ki_s, ks_s, dec_s, _, eb_s, enb_s, elb_s, db_s = pre
            tri = tri_b if rev else tri_f
            last_row = 0 if rev else GLA_CHUNK - 1
            rows = pl.ds(pl.multiple_of(ci * GLA_CHUNK, GLA_CHUNK), GLA_CHUNK)
            dec = dec_s[ci]
            heads = range(GLA_HEADS)
            lanes = [slice(HEAD_PAD * h, HEAD_PAD * (h + 1)) for h in heads]
            qd = [qd_s[rows, sl] for sl in lanes]
            ki = [ki_s[rows, sl] for sl in lanes]
            ks = [ks_s[rows, sl] for sl in lanes]
            v = [v_r[rows, sl] for sl in lanes]
            do_h = [_mx(do_r[rows, sl]) for sl in lanes]
            s_t = [s_r[ci, h] for h in heads]
            g_t = [gt[h] for h in heads]
            g_m = [_mx(g_t[h]) for h in heads]
            a = [_dot_nt(qd[h], ki[h]) for h in heads]
            da = [_dot_nt(do_h[h], v[h]) for h in heads]
            dv_carried = [_dot_nt(ks[h], g_m[h]) for h in heads]
            dqd_carried = [_dot(do_h[h], _mx(s_t[h])) for h in heads]
            dks = [_dot(v[h], g_m[h]) for h in heads]
            g_grown = [_dot_tn(do_h[h], qd[h]) for h in heads]
            a = [_mx(jnp.where(tri, a[h], 0.0)) for h in heads]
            da = [_mx(jnp.where(tri, da[h], 0.0)) for h in heads]
            dv_inner = [_dot_tn(a[h], do_h[h]) for h in heads]
            dqd_inner = [_dot(da[h], ki[h]) for h in heads]
            dki = [_dot_tn(da[h], qd[h]) for h in heads]
            for h in heads:
                sl = lanes[h]
                dv_r[rows, sl] = dv_inner[h] + dv_carried[h]
                ddec = jnp.sum(g_t[h] * s_t[h], axis=0, keepdims=True)
                gt[h] = g_t[h] * dec[:, sl] + g_grown[h]
                dq = (dqd_inner[h] + dqd_carried[h]) * eb_s[rows, sl] * 0.125
                dk_state = dks[h] * elb_s[rows, sl]
                dk = dki[h] * enb_s[rows, sl] + dk_state
                dq_r[rows, sl] = dq
                dk_r[rows, sl] = dk
                k = k_r[rows, sl]
                dblast = jnp.sum(dk_state * k, axis=0, keepdims=True) + dec[:, sl] * ddec
                db_s[rows, sl] = q_r[rows, sl] * dq - k * dk + jnp.where(row_w == last_row, dblast, 0.0)

        def loop(t, carry):
            one(False, pre_f, qaf, kaf, vaf, dof, sf_r, dqf, dkf, dvf, gt_f, nc - 1 - t)
            one(True, pre_b, qab, kab, vab, dob, sb_r, dqb, dkb, dvb, gt_b, t)
            return carry

        lax.fori_loop(0, nc, loop, 0, unroll=True)

        def gate_grads(rev, pre, z_r, w_r, dz_r, dw_r, dbias_r):
            g_s, db_s = pre[4], pre[8]
            back_m = _mx((tri_f if rev else tri_b).astype(F32))
            db = db_s[...]
            dla = jnp.concatenate([_chunk_sums(back_m, db[GLA_CHUNK * c:GLA_CHUNK * (c + 1)]) for c in range(nc)],
                                  axis=0)
            dg = dla * (1.0 / GLA_GATE_NORM) * (1.0 / (1.0 + jnp.exp(g_s[...])))
            dg_m = _mx(dg)
            dz_r[...] = _dot_nt(dg_m, w_r[...])
            dw_r[...] += _dot_tn(_mx(z_r[...]), dg_m)
            dbias_r[...] += jnp.sum(dg, axis=0, keepdims=True)

        gate_grads(False, pre_f, zaf, wgf_r, dzf, dwf, dbf)
        gate_grads(True, pre_b, zab, wgb_r, dzb, dwb, dbb)

    last_first = lambda i: (nb - 1 - i, 0)
    first_last = lambda i: (i, 0)
    s_shape = (nc, GLA_HEADS, HEAD_PAD, HEAD_PAD)

    def ins(m):
        return [pl.BlockSpec((br, hw), m), pl.BlockSpec((br, hw), m), pl.BlockSpec((br, hw), m),
                pl.BlockSpec((br, 128), m), pl.BlockSpec((br, hw), m),
                pl.BlockSpec(s_shape, lambda i: m(i) + (0, 0))]

    def outs(m):
        return [pl.BlockSpec((br, hw), m), pl.BlockSpec((br, hw), m), pl.BlockSpec((br, hw), m),
                pl.BlockSpec((br, 128), m), _full_spec((128, hw)), _full_spec((1, hw))]

    out_shape = [jax.ShapeDtypeStruct((L, hw), F32)] * 3 + [
        jax.ShapeDtypeStruct((L, 128), F32), jax.ShapeDtypeStruct((128, hw), F32),
        jax.ShapeDtypeStruct((1, hw), F32)]
    wspecs = [_full_spec((128, hw)), _full_spec((1, hw))] * 2
    body, extra, extra_specs = _after(body, 16, dep)
    pre_scratch = ([pltpu.VMEM((br, hw), MXU_DTYPE)] * 3 + [pltpu.VMEM((nc, 1, hw), F32)]
                   + [pltpu.VMEM((br, hw), F32)] * 5)
    return pl.pallas_call(
        body, name="gla_bwd", grid=(nb,),
        in_specs=ins(last_first) + ins(first_last) + wspecs + extra_specs,
        out_specs=outs(last_first) + outs(first_last),
        out_shape=out_shape + out_shape,
        scratch_shapes=[pltpu.VMEM(s_shape[1:], F32), pltpu.VMEM(s_shape[1:], F32), pre_scratch, pre_scratch],
        compiler_params=_params(("arbitrary",), VMEM_BIG),
    )(qa, ka, va, za, do, sf, qa, ka, va, za, do, sb, wgf, bgf, wgb, bgb, *extra)


def _t5_buckets(rel):
    nb = REL_BUCKETS // 2
    ret = (rel > 0).astype(np.int32) * nb
    n = np.abs(rel)
    max_exact = nb // 2
    large = max_exact + (np.log(np.maximum(n, 1).astype(np.float32) / max_exact)
                         / math.log(REL_MAX_DIST / max_exact) * (nb - max_exact)).astype(np.int32)
    large = np.minimum(large, nb - 1)
    return ret + np.where(n < max_exact, n, large)


SWA_GROUP = SWA_Q_HEADS // SWA_KV_HEADS
SWA_SPAN = 3 * SWA_BLOCK
SWA_GROUP_LANES = SWA_GROUP * SWA_BLOCK


def _band_buckets():
    s = np.arange(SWA_SPAN)[:, None]
    c = np.arange(SWA_BLOCK)[None, :]
    return _t5_buckets(s - SWA_BLOCK - c).astype(np.int32)


def _swa_valid(n, seq_len):
    s = lax.broadcasted_iota(jnp.int32, (SWA_SPAN, SWA_GROUP_LANES), 0)
    c = lax.broadcasted_iota(jnp.int32, (SWA_SPAN, SWA_GROUP_LANES), 1) & (SWA_BLOCK - 1)
    rel = s - SWA_BLOCK - c
    key_pos = (n - 1) * SWA_BLOCK + s
    return (jnp.abs(rel) <= SWA_BLOCK) & (key_pos >= 0) & (key_pos < seq_len)


def _swa_sink_row(sink_r, kv):
    lane = lax.broadcasted_iota(jnp.int32, (1, SWA_GROUP_LANES), 1)
    row = jnp.full((1, SWA_GROUP_LANES), sink_r[kv * SWA_GROUP], F32)
    for g in range(1, SWA_GROUP):
        row = jnp.where(lane >= g * SWA_BLOCK, sink_r[kv * SWA_GROUP + g], row)
    return row


def _swa_group(ref, kv):
    first = kv * SWA_GROUP
    return jnp.concatenate([ref[:, HEAD_PAD * h:HEAD_PAD * (h + 1)] for h in range(first, first + SWA_GROUP)],
                           axis=0)


def _swa_softmax(scores, bias_t, sink_row, valid):
    st = scores * 0.125 + bias_t
    st = jnp.where(valid, st, -1e30)
    m = jnp.maximum(jnp.max(st, axis=0, keepdims=True), sink_row)
    p = jnp.exp(st - m)
    e_sink = jnp.exp(sink_row - m)
    inv = 1.0 / (jnp.sum(p, axis=0, keepdims=True) + e_sink)
    return p * inv, e_sink * inv


def _swa_fwd_call(qs, ks, vs, bias, sink, dep=None):
    L = qs.shape[0]

    def body(q_r, k_r, v_r, bias_r, sink_r, o_r):
        n = pl.program_id(0)
        span = pl.ds(pl.multiple_of(n * SWA_BLOCK, SWA_BLOCK), SWA_SPAN)
        valid = _swa_valid(n, L)
        groups = range(SWA_KV_HEADS)
        lanes = [slice(HEAD_PAD * kv, HEAD_PAD * (kv + 1)) for kv in groups]
        scores = [_dot_nt(k_r[span, lanes[kv]], _swa_group(q_r, kv)) for kv in groups]
        probs = [_swa_softmax(scores[kv], bias_r[kv], _swa_sink_row(sink_r, kv), valid)[0] for kv in groups]
        low = _low_half(SWA_BLOCK)
        for kv in groups:
            og = _dot_tn(_mx(probs[kv]), v_r[span, lanes[kv]])
            for pair in range(SWA_GROUP // 2):
                even = og[2 * SWA_BLOCK * pair:2 * SWA_BLOCK * pair + SWA_BLOCK]
                odd = og[2 * SWA_BLOCK * pair + SWA_BLOCK:2 * SWA_BLOCK * (pair + 1)]
                first = HEAD_PAD * (kv * SWA_GROUP // 2 + pair)
                o_r[:, first:first + HEAD_PAD] = jnp.where(low, even, pltpu.roll(odd, 64, 1)).astype(o_r.dtype)

    qw = SWA_Q_HEADS * HEAD_PAD
    body, extra, extra_specs = _after(body, 5, dep)
    return pl.pallas_call(
        body, name="swa_fwd", grid=(L // SWA_BLOCK,),
        in_specs=[_row_spec(SWA_BLOCK, qw), _vmem_spec(), _vmem_spec(), _vmem_spec(),
                  pl.BlockSpec(memory_space=pltpu.SMEM)] + extra_specs,
        out_specs=_row_spec(SWA_BLOCK, qw // 2),
        out_shape=jax.ShapeDtypeStruct((L, qw // 2), MXU_DTYPE),
        compiler_params=_params(("arbitrary",), VMEM_BIG),
    )(qs, ks, vs, bias, sink, *extra)


def _swa_bwd_call(qs, ks, vs, bias, sink, do, dep=None):
    L = qs.shape[0]
    qw = SWA_Q_HEADS * HEAD_PAD
    kw = SWA_KV_HEADS * HEAD_PAD

    def body(q_r, k_r, v_r, bias_r, sink_r, do_r, dq_r, dk_r, dv_r, dbias_r, dsink_r):
        n = pl.program_id(0)

        @pl.when(n == 0)
        def _():
            for ref in (dk_r, dv_r, dbias_r, dsink_r):
                ref[...] = jnp.zeros_like(ref)

        span = pl.ds(pl.multiple_of(n * SWA_BLOCK, SWA_BLOCK), SWA_SPAN)
        valid = _swa_valid(n, L)
        groups = range(SWA_KV_HEADS)
        lanes = [slice(HEAD_PAD * kv, HEAD_PAD * (kv + 1)) for kv in groups]
        kk = [k_r[span, sl] for sl in lanes]
        vv = [v_r[span, sl] for sl in lanes]
        qg = [_swa_group(q_r, kv) for kv in groups]
        dog = [_swa_group(do_r, kv) for kv in groups]
        scores = [_dot_nt(kk[kv], qg[kv]) for kv in groups]
        dp = [_dot_nt(vv[kv], dog[kv]) for kv in groups]
        probs = [_swa_softmax(scores[kv], bias_r[kv], _swa_sink_row(sink_r, kv), valid) for kv in groups]
        ds_m, pn_m = [], []
        for kv in groups:
            pn, p_sink = probs[kv]
            delta = jnp.sum(pn * dp[kv], axis=0, keepdims=True)
            ds = pn * (dp[kv] - delta)
            dsink_r[kv] -= p_sink * delta
            dbias_r[kv] += ds
            ds_m.append(_mx(ds))
            pn_m.append(_mx(pn))
        dqg = [_dot_tn(ds_m[kv], kk[kv]) * 0.125 for kv in groups]
        dkk = [_dot(ds_m[kv], qg[kv]) * 0.125 for kv in groups]
        dvv = [_dot(pn_m[kv], dog[kv]) for kv in groups]
        for kv in groups:
            for g in range(SWA_GROUP):
                h = kv * SWA_GROUP + g
                dq_r[:, HEAD_PAD * h:HEAD_PAD * (h + 1)] = dqg[kv][SWA_BLOCK * g:SWA_BLOCK * (g + 1)]
            dk_r[span, lanes[kv]] += dkk[kv]
            dv_r[span, lanes[kv]] += dvv[kv]

    body, extra, extra_specs = _after(body, 6, dep)
    return pl.pallas_call(
        body, name="swa_bwd", grid=(L // SWA_BLOCK,),
        in_specs=[_row_spec(SWA_BLOCK, qw), _vmem_spec(), _vmem_spec(), _vmem_spec(),
                  pl.BlockSpec(memory_space=pltpu.SMEM), _row_spec(SWA_BLOCK, qw)] + extra_specs,
        out_specs=[_row_spec(SWA_BLOCK, qw), _vmem_spec(), _vmem_spec(), _vmem_spec(), _vmem_spec()],
        out_shape=[jax.ShapeDtypeStruct((L, qw), F32),
                   jax.ShapeDtypeStruct((L + 2 * SWA_BLOCK, kw), F32),
                   jax.ShapeDtypeStruct((L + 2 * SWA_BLOCK, kw), F32),
                   jax.ShapeDtypeStruct((SWA_KV_HEADS, SWA_SPAN, SWA_GROUP_LANES), F32),
                   jax.ShapeDtypeStruct((SWA_KV_HEADS, 1, SWA_GROUP_LANES), F32)],
        compiler_params=_params(("arbitrary",), VMEM_BIG),
    )(qs, ks, vs, bias, sink, do, *extra)


def _bias_call(rel_bias, buckets):
    def body(t_r, bk_r, o_r):
        bk = bk_r[...]
        for h in range(SWA_Q_HEADS):
            acc = jnp.zeros(bk.shape, F32)
            for b in range(REL_BUCKETS):
                acc = jnp.where(bk == b, t_r[b, h], acc)
            g = h % SWA_GROUP
            o_r[h // SWA_GROUP, :, SWA_BLOCK * g:SWA_BLOCK * (g + 1)] = acc

    return pl.pallas_call(
        body, name="band_bias",
        in_specs=[pl.BlockSpec(memory_space=pltpu.SMEM), _vmem_spec()], out_specs=_vmem_spec(),
        out_shape=jax.ShapeDtypeStruct((SWA_KV_HEADS, SWA_SPAN, SWA_GROUP_LANES), F32),
    )(rel_bias, buckets)


def _relbias_call(dbias, dsink, buckets):
    def body(db_r, ds_r, bk_r, o_r, os_r):
        bk = bk_r[...]
        rowi = lax.broadcasted_iota(jnp.int32, (REL_BUCKETS, 128), 0)
        lanei = lax.broadcasted_iota(jnp.int32, (REL_BUCKETS, 128), 1)
        lane1 = lax.broadcasted_iota(jnp.int32, (1, 128), 1)
        acc = jnp.zeros((REL_BUCKETS, 128), F32)
        acc_sink = jnp.zeros((1, 128), F32)
        for h in range(SWA_Q_HEADS):
            kv, g = h // SWA_GROUP, h % SWA_GROUP
            lanes = slice(SWA_BLOCK * g, SWA_BLOCK * (g + 1))
            part = db_r[kv, :, lanes]
            for b in range(REL_BUCKETS):
                s = jnp.sum(jnp.where(bk == b, part, 0.0))
                acc = acc + jnp.where((rowi == b) & (lanei == h), s, 0.0)
            acc_sink = acc_sink + jnp.where(lane1 == h, jnp.sum(ds_r[kv, :, lanes]), 0.0)
        o_r[...] = acc
        os_r[...] = acc_sink

    return pl.pallas_call(
        body, name="relbias_grad",
        in_specs=[_vmem_spec()] * 3, out_specs=[_vmem_spec()] * 2,
        out_shape=[jax.ShapeDtypeStruct((REL_BUCKETS, 128), F32), jax.ShapeDtypeStruct((1, 128), F32)],
    )(dbias, dsink, buckets)


def _mix_call(o_f, o_b, ga, o_s, x, gn, w_out_p, g_post, g_pre2):
    L = x.shape[0]
    tm = min(256, L)
    hw = GLA_HEADS * HEAD_PAD

    def body(of_r, ob_r, ga_r, os_r, x_r, gn_r, w_r, gp_r, g2_r, cat_r, mix_r, h1_r, n2_r):
        gn_v = gn_r[...]
        for h in range(GLA_HEADS):
            sl = slice(HEAD_PAD * h, HEAD_PAD * (h + 1))
            oh = of_r[:, sl] + ob_r[:, sl]
            on = oh * _rms_r(oh) * gn_v
            gate = ga_r[:, sl]
            cat_r[:, sl] = (on * (gate * jax.nn.sigmoid(gate))).astype(cat_r.dtype)
        os_v = os_r[...]
        cat_r[:, hw:] = os_v
        mix = _dot(cat_r[:, :hw], w_r[:hw, :]) + _dot(os_v, w_r[hw:, :])
        mix_r[...] = mix
        h1 = x_r[...] + mix * _rms_r(mix) * gp_r[...]
        h1_r[...] = h1
        n2_r[...] = (h1 * _rms_r(h1) * g2_r[...]).astype(n2_r.dtype)

    return pl.pallas_call(
        body, name="mix_fwd", grid=(L // tm,),
        in_specs=[_row_spec(tm, hw), _row_spec(tm, hw), _row_spec(tm, hw), _row_spec(tm, OUT_PAD - hw),
                  _row_spec(tm, D_MODEL), _full_spec((1, HEAD_PAD)), _vmem_spec(),
                  _full_spec((1, D_MODEL)), _full_spec((1, D_MODEL))],
        out_specs=[_row_spec(tm, OUT_PAD), _row_spec(tm, D_MODEL), _row_spec(tm, D_MODEL), _row_spec(tm, D_MODEL)],
        out_shape=[jax.ShapeDtypeStruct((L, OUT_PAD), MXU_DTYPE), jax.ShapeDtypeStruct((L, D_MODEL), F32),
                   jax.ShapeDtypeStruct((L, D_MODEL), F32), jax.ShapeDtypeStruct((L, D_MODEL), MXU_DTYPE)],
        compiler_params=_params(("arbitrary",), VMEM_BIG),
    )(o_f, o_b, ga, o_s, x, gn, w_out_p, g_post, g_pre2)


def _mlp_fwd_call(n2, h1, tgt, w_ud, g_post):
    L = n2.shape[0]
    tm = min(512, L)
    blk = D_FF // N_CHIPS

    def body(n2_r, h1_r, t_r, w_r, g_r, a_r, rz_r, dh2_r, dff_r, loss_r, dg_r):
        @pl.when(pl.program_id(0) == 0)
        def _():
            loss_r[...] = jnp.zeros_like(loss_r)
            dg_r[...] = jnp.zeros_like(dg_r)

        n2v = n2_r[...]
        ff = jnp.zeros((tm, D_MODEL), F32)
        for j in range(N_CHIPS):
            sl = slice(blk * j, blk * (j + 1))
            rz = jnp.maximum(_dot(n2v, w_r[j, 0]), 0.0)
            a = _mx(rz * rz)
            rz_r[:, sl] = rz.astype(rz_r.dtype)
            a_r[:, sl] = a
            ff = ff + _dot(a, w_r[j, 1])
        g = g_r[...]
        r = _rms_r(ff)
        err = h1_r[...] + ff * r * g - t_r[...]
        loss_r[...] += 0.5 * jnp.sum(err * err) / D_MODEL
        dh2 = err * (1.0 / D_MODEL)
        dh2_r[...] = dh2
        dff, dg = _rms_bwd(ff, r, g, dh2)
        dff_r[...] = dff.astype(dff_r.dtype)
        dg_r[...] += dg

    return pl.pallas_call(
        body, name="mlp_fwd", grid=(L // tm,),
        in_specs=[_row_spec(tm, D_MODEL), _row_spec(tm, D_MODEL), _row_spec(tm, D_MODEL),
                  _vmem_spec(), _full_spec((1, D_MODEL))],
        out_specs=[_row_spec(tm, D_FF), _row_spec(tm, D_FF), _row_spec(tm, D_MODEL), _row_spec(tm, D_MODEL),
                   _full_spec((1, 128)), _full_spec((1, D_MODEL))],
        out_shape=[jax.ShapeDtypeStruct((L, D_FF), MXU_DTYPE), jax.ShapeDtypeStruct((L, D_FF), MXU_DTYPE),
                   jax.ShapeDtypeStruct((L, D_MODEL), F32), jax.ShapeDtypeStruct((L, D_MODEL), MXU_DTYPE),
                   jax.ShapeDtypeStruct((1, 128), F32), jax.ShapeDtypeStruct((1, D_MODEL), F32)],
        compiler_params=_params(("arbitrary",), VMEM_BIG),
    )(n2, h1, tgt, w_ud, g_post)


def _mlp_bwd_call(dff, rz, w_ud):
    L = dff.shape[0]
    tm = min(512, L)
    blk = D_FF // N_CHIPS

    def body(dff_r, rz_r, w_r, dz_r, dn2_r):
        dffv = dff_r[...]
        dn2 = jnp.zeros((tm, D_MODEL), F32)
        for j in range(N_CHIPS):
            sl = slice(blk * j, blk * (j + 1))
            dz = _mx(_dot_nt(dffv, w_r[j, 1]) * 2.0 * rz_r[:, sl].astype(F32))
            dz_r[:, sl] = dz
            dn2 = dn2 + _dot_nt(dz, w_r[j, 0])
        dn2_r[...] = dn2

    return pl.pallas_call(
        body, name="mlp_bwd", grid=(L // tm,),
        in_specs=[_row_spec(tm, D_MODEL), _row_spec(tm, D_FF), _vmem_spec()],
        out_specs=[_row_spec(tm, D_FF), _row_spec(tm, D_MODEL)],
        out_shape=[jax.ShapeDtypeStruct((L, D_FF), MXU_DTYPE), jax.ShapeDtypeStruct((L, D_MODEL), F32)],
        compiler_params=_params(("arbitrary",), VMEM_BIG),
    )(dff, rz, w_ud)


def _mlp_wgrad_call(a, dff, n2, dz):
    L = a.shape[0]
    tf = 512
    per = (D_FF // N_CHIPS) // tf

    def body(a_r, dff_r, n2_r, dz_r, dwd_r, dwu_r):
        dwd_r[...] = _dot_tn(a_r[...], dff_r[...])
        dwu_r[...] = _dot_tn(n2_r[...], dz_r[...])

    return pl.pallas_call(
        body, name="mlp_wgrad", grid=(D_FF // tf,),
        in_specs=[pl.BlockSpec((L, tf), lambda j: (0, j)), _vmem_spec(), _vmem_spec(),
                  pl.BlockSpec((L, tf), lambda j: (0, j))],
        out_specs=[pl.BlockSpec((tf, D_MODEL), lambda j: (j, 0)),
                   pl.BlockSpec((None, D_MODEL, tf), lambda j: (j // per, 0, j % per))],
        out_shape=[jax.ShapeDtypeStruct((D_FF, D_MODEL), F32),
                   jax.ShapeDtypeStruct((N_CHIPS, D_MODEL, D_FF // N_CHIPS), F32)],
        compiler_params=_params(("arbitrary",), VMEM_BIG),
    )(a, dff, n2, dz)


def _mix_bwd_call(dn2, dh2, h1, mix, cat, o_f, o_b, ga, gn, g_post, g_pre2, w_out_p):
    L = dn2.shape[0]
    tm = min(256, L)
    hw = GLA_HEADS * HEAD_PAD

    def body(dn2_r, dh2_r, h1_r, mix_r, cat_r, of_r, ob_r, ga_r, gn_r, gp_r, g2_r, w_r,
             dh1_r, do_r, dga_r, dos_r, dw_r, dg2_r, dgp_r, dgn_r):
        @pl.when(pl.program_id(0) == 0)
        def _():
            for ref in (dw_r, dg2_r, dgp_r, dgn_r):
                ref[...] = jnp.zeros_like(ref)

        h1 = h1_r[...]
        dx2, dg2 = _rms_bwd(h1, _rms_r(h1), g2_r[...], dn2_r[...])
        dh1 = dh2_r[...] + dx2
        dh1_r[...] = dh1
        dg2_r[...] += dg2
        mix = mix_r[...]
        dmix, dgp = _rms_bwd(mix, _rms_r(mix), gp_r[...], dh1)
        dgp_r[...] += dgp
        dmix_m = _mx(dmix)
        dw_r[...] += _dot_tn(cat_r[...], dmix_m)
        dcat = _dot_nt(dmix_m, w_r[...])
        dos_r[...] = _spread_heads(dcat[:, hw:]).astype(dos_r.dtype)
        gn_v = gn_r[...]
        dgn = jnp.zeros((1, HEAD_PAD), F32)
        for h in range(GLA_HEADS):
            sl = slice(HEAD_PAD * h, HEAD_PAD * (h + 1))
            oh = of_r[:, sl] + ob_r[:, sl]
            rr = _rms_r(oh)
            gate = ga_r[:, sl]
            sg = jax.nn.sigmoid(gate)
            doa = dcat[:, sl]
            dga_r[:, sl] = doa * (oh * rr * gn_v) * (sg * (1.0 + gate * (1.0 - sg)))
            do_h, dgn_h = _rms_bwd(oh, rr, gn_v, doa * (gate * sg))
            do_r[:, sl] = do_h
            dgn = dgn + dgn_h
        dgn_r[...] += dgn

    return pl.pallas_call(
        body, name="mix_bwd", grid=(L // tm,),
        in_specs=[_row_spec(tm, D_MODEL)] * 4 + [_row_spec(tm, OUT_PAD)] + [_row_spec(tm, hw)] * 3
        + [_full_spec((1, HEAD_PAD)), _full_spec((1, D_MODEL)), _full_spec((1, D_MODEL)), _vmem_spec()],
        out_specs=[_row_spec(tm, D_MODEL), _row_spec(tm, hw), _row_spec(tm, hw),
                   _row_spec(tm, SWA_Q_HEADS * HEAD_PAD),
                   _full_spec((OUT_PAD, D_MODEL)), _full_spec((1, D_MODEL)), _full_spec((1, D_MODEL)),
                   _full_spec((1, HEAD_PAD))],
        out_shape=[jax.ShapeDtypeStruct((L, D_MODEL), F32), jax.ShapeDtypeStruct((L, hw), F32),
                   jax.ShapeDtypeStruct((L, hw), F32), jax.ShapeDtypeStruct((L, SWA_Q_HEADS * HEAD_PAD), MXU_DTYPE),
                   jax.ShapeDtypeStruct((OUT_PAD, D_MODEL), F32), jax.ShapeDtypeStruct((1, D_MODEL), F32),
                   jax.ShapeDtypeStruct((1, D_MODEL), F32), jax.ShapeDtypeStruct((1, HEAD_PAD), F32)],
        compiler_params=_params(("arbitrary",), VMEM_BIG),
    )(dn2, dh2, h1, mix, cat, o_f, o_b, ga, gn, g_post, g_pre2, w_out_p)


def _in_bwd_call(x, dh1, g_pre, w_in_t, pairs, singles, halos, dep=None):
    L = x.shape[0]
    tm = min(256, L)
    n_pair, n_single, n_halo = len(pairs), len(singles), len(halos)
    groups = [c for c, _ in pairs] + [c for c, _ in singles] + [c for c, _ in halos]

    def body(*refs):
        x_r, dh1_r, g_r, w_r = refs[:4]
        pair_refs = refs[4:4 + 2 * n_pair]
        single_refs = refs[4 + 2 * n_pair:4 + 2 * n_pair + n_single]
        halo_refs = refs[4 + 2 * n_pair + n_single:4 + 2 * n_pair + n_single + n_halo]
        dx_r, dw_r, dg_r = refs[4 + 2 * n_pair + n_single + n_halo:]

        @pl.when(pl.program_id(0) == 0)
        def _():
            dw_r[...] = jnp.zeros_like(dw_r)
            dg_r[...] = jnp.zeros_like(dg_r)

        xv = x_r[...]
        r = _rms_r(xv)
        g = g_r[...]
        u = _mx(xv * r * g)
        vals = [pair_refs[2 * i][...] + pair_refs[2 * i + 1][...] for i in range(n_pair)]
        vals += [ref[...].astype(F32) for ref in single_refs]
        inner = pl.ds(pl.multiple_of(pl.program_id(0) * tm + SWA_BLOCK, SWA_BLOCK), tm)
        vals += [ref[inner, :] for ref in halo_refs]
        du = jnp.zeros((tm, D_MODEL), F32)
        for (first, rows, heads), val in zip(groups, vals):
            d = _mx(_squeeze_heads(val) if heads else val)
            du = du + _dot(d, w_r[first:first + rows, :])
            dw_r[first:first + rows, :] += _dot_tn(d, u)
        dx, dg = _rms_bwd(xv, r, g, du)
        dx_r[...] = dh1_r[...] + dx
        dg_r[...] += dg

    arrays = [a for _, pr in pairs for a in pr] + [a for _, a in singles]
    specs = [_row_spec(tm, a.shape[1]) for a in arrays] + [_vmem_spec()] * n_halo
    arrays += [a for _, a in halos]
    body, extra, extra_specs = _after(body, 4 + len(arrays), dep)
    return pl.pallas_call(
        body, name="in_bwd", grid=(L // tm,),
        in_specs=[_row_spec(tm, D_MODEL), _row_spec(tm, D_MODEL), _full_spec((1, D_MODEL)), _vmem_spec()] + specs
        + extra_specs,
        out_specs=[_row_spec(tm, D_MODEL), _full_spec((IN_COLS, D_MODEL)), _full_spec((1, D_MODEL))],
        out_shape=[jax.ShapeDtypeStruct((L, D_MODEL), F32), jax.ShapeDtypeStruct((IN_COLS, D_MODEL), F32),
                   jax.ShapeDtypeStruct((1, D_MODEL), F32)],
        compiler_params=_params(("arbitrary",), VMEM_BIG),
    )(x, dh1, g_pre, w_in_t, *arrays, *extra)


def _adamw_math(w, g, m, v):
    m = ADAM_B1 * m + (1.0 - ADAM_B1) * g
    v = ADAM_B2 * v + (1.0 - ADAM_B2) * (g * g)
    m_hat = m / (1.0 - ADAM_B1 ** ADAM_STEP)
    v_hat = v / (1.0 - ADAM_B2 ** ADAM_STEP)
    delta = -ADAM_LR * (m_hat / (jnp.sqrt(v_hat) + ADAM_EPS) + ADAM_WD * w)
    return delta, m, v


def _adamw_call(w, g, m, v, name, dep=None):
    rows, cols = w.shape
    tr = min(256, rows)

    def body(w_r, g_r, m_r, v_r, d_r, nm_r, nv_r):
        d_r[...], nm_r[...], nv_r[...] = _adamw_math(w_r[...], g_r[...], m_r[...], v_r[...])

    if rows % tr == 0:
        spec, steps = _row_spec(tr, cols), rows // tr
    else:
        spec, steps = pl.BlockSpec((rows, 256), lambda i: (0, i)), cols // 256
    body, extra, extra_specs = _after(body, 4, dep)
    return pl.pallas_call(
        body, name=name, grid=(steps,),
        in_specs=[spec] * 4 + extra_specs, out_specs=[spec] * 3,
        out_shape=[jax.ShapeDtypeStruct(w.shape, F32)] * 3,
        compiler_params=_params(("arbitrary",)),
    )(w, g, m, v, *extra)


def _position():
    return lax.axis_index("x"), lax.axis_index("y"), lax.axis_index("c")


def _other_chips(x, y):
    return [(1 - x, y), (x, 1 - y), (1 - x, 1 - y)]


ROWS, COLS = -2, -1


def _half(ref, which, axis):
    size = ref.shape[axis] // 2
    span = pl.ds(pl.multiple_of(which * size, 16 if axis == ROWS else 128), size)
    index = [slice(None)] * len(ref.shape)
    index[axis] = span
    return ref.at[tuple(index)]


def _first_gather_call(shards, axes):
    n = len(shards)

    def body(*refs):
        srcs, outs = refs[:n], refs[n:2 * n]
        send_sems, recv_sems, local_sems = refs[2 * n:]
        x, y, c = _position()
        sibling = (x, y, 1 - c)
        chips = _other_chips(x, y)
        local = [pltpu.make_async_copy(srcs[a], outs[a].at[2 * x + y], local_sems.at[a]) for a in range(n)]
        for cp in local:
            cp.start()

        def copy(a, k, block, to, src=None):
            px, py, pc = block
            dst = _half(outs[a].at[2 * px + py], pc, axes[a])
            return pltpu.make_async_remote_copy(
                src_ref=dst if src is None else src, dst_ref=dst, send_sem=send_sems.at[6 * a + k],
                recv_sem=recv_sems.at[6 * a + k], device_id=to, device_id_type=MESH_ID)

        first, passed = [], []
        for a in range(n):
            my_half = _half(srcs[a], c, axes[a])
            first += [copy(a, j, (x, y, c), (*chip, c), src=my_half) for j, chip in enumerate(chips)]
        for cp in first:
            cp.start()
        for a in range(n):
            for j, chip in enumerate(chips):
                copy(a, j, (*chip, c), (x, y, c)).wait_recv()
                passed.append(copy(a, 3 + j, (*chip, c), sibling))
                passed[-1].start()
        for a in range(n):
            for j, chip in enumerate(chips):
                copy(a, 3 + j, (*chip, 1 - c), (x, y, c)).wait_recv()
        for cp in first + passed:
            cp.wait_send()
        for cp in local:
            cp.wait()

    return pl.pallas_call(
        body, name="first_gather",
        in_specs=[_any_spec()] * n, out_specs=[_any_spec()] * n,
        out_shape=[jax.ShapeDtypeStruct((N_CHIPS,) + s.shape, s.dtype) for s in shards],
        scratch_shapes=[pltpu.SemaphoreType.DMA((6 * n,)), pltpu.SemaphoreType.DMA((6 * n,)),
                        pltpu.SemaphoreType.DMA((n,))],
    )(*shards)


def _split_start(name, arrays, n_copies, plan):
    n = len(arrays)

    def body(*refs):
        ins, send_sems, recv_sems, token = refs[:n], refs[n], refs[n + 1], refs[-1]
        for k, (src, dst, to, _) in enumerate(plan(ins)):
            pltpu.make_async_remote_copy(src_ref=src, dst_ref=dst, send_sem=send_sems.at[k],
                                         recv_sem=recv_sems.at[k], device_id=to, device_id_type=MESH_ID).start()
        token[...] = jnp.zeros_like(token)

    hbm = pl.BlockSpec(memory_space=pltpu.HBM)
    sem = pl.BlockSpec(memory_space=pltpu.SEMAPHORE)
    out = pl.pallas_call(
        body, name=name,
        out_shape=(pltpu.SemaphoreType.DMA((n_copies,)), pltpu.SemaphoreType.DMA((n_copies,)))
        + tuple(pltpu.HBM(a.shape, a.dtype) for a in arrays) + (jax.ShapeDtypeStruct((8, 128), F32),),
        in_specs=[hbm] * n, out_specs=(sem, sem) + (hbm,) * n + (_vmem_spec(),),
        input_output_aliases={i: 2 + i for i in range(n)},
        compiler_params=pltpu.CompilerParams(has_side_effects=pltpu.SideEffectType.DATAFLOW_SIDE_EFFECTING),
    )(*[pltpu.with_memory_space_constraint(a, pltpu.HBM) for a in arrays])
    return (out[0], out[1], tuple(out[2:2 + n])), out[-1]


def _split_wait(name, handle, n_copies, plan, after):
    send_sems, recv_sems, arrays = handle
    n = len(arrays)

    def body(*refs):
        ins, s_sems, r_sems = refs[:n], refs[n], refs[n + 1]
        for k, (src, dst, to, landed) in enumerate(plan(ins)):
            cp = pltpu.make_async_remote_copy(src_ref=src, dst_ref=landed, send_sem=s_sems.at[k],
                                              recv_sem=r_sems.at[k], device_id=to, device_id_type=MESH_ID)
            cp.wait_send()
            cp.wait_recv()

    hbm = pl.BlockSpec(memory_space=pltpu.HBM)
    sem = pl.BlockSpec(memory_space=pltpu.SEMAPHORE)
    out = pl.pallas_call(
        body, name=name,
        out_shape=tuple(pltpu.HBM(a.shape, a.dtype) for a in arrays),
        in_specs=[hbm] * n + [sem, sem, _any_spec()], out_specs=(hbm,) * n,
        input_output_aliases={i: i for i in range(n)},
        compiler_params=pltpu.CompilerParams(has_side_effects=pltpu.SideEffectType.DATAFLOW_SIDE_EFFECTING),
    )(*arrays, send_sems, recv_sems, after)
    return tuple(out)


def _gather_plans(axes):
    n = len(axes)

    def stage_one(refs):
        x, y, c = _position()
        copies = []
        for a, axis in enumerate(axes):
            for px, py in _other_chips(x, y):
                copies.append((_half(refs[a], c, axis), _half(refs[n + a].at[2 * x + y], c, axis),
                               (px, py, c), _half(refs[n + a].at[2 * px + py], c, axis)))
        return copies

    def stage_two(refs):
        x, y, c = _position()
        copies = []
        for a, axis in enumerate(axes):
            for px, py in _other_chips(x, y):
                piece = _half(refs[n + a].at[2 * px + py], c, axis)
                copies.append((piece, piece, (x, y, 1 - c), _half(refs[n + a].at[2 * px + py], 1 - c, axis)))
        return copies

    return stage_one, stage_two


def _pair_swap_plan(axes):
    n = len(axes)

    def plan(refs):
        x, y, c = _position()
        return [(_half(refs[a], 1 - c, axes[a]), refs[n + a], (x, y, 1 - c), refs[n + a]) for a in range(n)]

    return plan


def _chip_swap_plan(n):
    def plan(refs):
        x, y, c = _position()
        copies = []
        for a in range(n):
            for j, (px, py) in enumerate(_other_chips(x, y)):
                copies.append((refs[a].at[2 * px + py], refs[n + a].at[j], (px, py, c), refs[n + a].at[j]))
        return copies

    return plan


def _pair_join_plan(axes):
    def plan(refs):
        x, y, c = _position()
        copies = []
        for a, axis in enumerate(axes):
            mine = _half(refs[a], c, axis)
            copies.append((mine, mine, (x, y, 1 - c), _half(refs[a], 1 - c, axis)))
        return copies

    return plan


def _pair_add_call(g, got, pos, name, axis):
    rows, cols = got.shape[1], got.shape[2]
    tr = min(256, rows) if axis == ROWS else rows
    nblk = rows // tr
    if axis == ROWS:
        mine = lambda j, i, p: (j, p[1] * nblk + i, 0)
    else:
        mine = lambda j, i, p: (j, 0, p[1])

    def body(pos_r, g_r, got_r, o_r):
        o_r[...] = (g_r[...] + got_r[...]).astype(o_r.dtype)

    return pl.pallas_call(
        body, name=name,
        grid_spec=pltpu.PrefetchScalarGridSpec(
            num_scalar_prefetch=1, grid=(N_CHIPS, nblk),
            in_specs=[pl.BlockSpec((None, tr, cols), mine),
                      pl.BlockSpec((None, tr, cols), lambda j, i, p: (j, i, 0))],
            out_specs=pl.BlockSpec((None, tr, cols), lambda j, i, p: (j, i, 0))),
        out_shape=jax.ShapeDtypeStruct(got.shape, COMM_DTYPE),
        compiler_params=_params(("arbitrary", "arbitrary")),
    )(pos, g, got)


def _chip_add_call(hsum, got, pos, name, axis):
    rows, cols = hsum.shape[1], hsum.shape[2]
    tr = min(256, rows) if axis == ROWS else rows
    nblk = rows // tr
    if axis == ROWS:
        out_shape, mine = (2 * rows, cols), (lambda i, p: (p[1] * nblk + i, 0))
    else:
        out_shape, mine = (rows, 2 * cols), (lambda i, p: (0, p[1]))

    def body(pos_r, own_r, got_r, o_r):
        acc = own_r[...].astype(F32)
        for j in range(3):
            acc = acc + got_r[j].astype(F32)
        o_r[...] = acc

    return pl.pallas_call(
        body, name=name,
        grid_spec=pltpu.PrefetchScalarGridSpec(
            num_scalar_prefetch=1, grid=(nblk,),
            in_specs=[pl.BlockSpec((None, tr, cols), lambda i, p: (p[0], i, 0)),
                      pl.BlockSpec((3, tr, cols), lambda i, p: (0, i, 0))],
            out_specs=pl.BlockSpec((tr, cols), mine)),
        out_shape=jax.ShapeDtypeStruct(out_shape, F32),
        compiler_params=_params(("arbitrary",)),
    )(pos, hsum, got)


SMALL_NAMES = ("norm_mix_pre", "norm_mix_post", "norm_mlp_pre", "norm_mlp_post", "b_gate_fwd", "b_gate_bwd",
               "gla_norm", "swa_sink", "rel_bias")


def _small_update_call(grads, gate_grads, params, dep=None):
    n_dev = 8
    n_small = len(SMALL_NAMES)
    wmv = [t for p in params for t in p]
    shapes = [p[0].shape for p in params]

    def body(*refs):
        g_refs = refs[:n_small + 3]
        wmv_refs = refs[n_small + 3:n_small + 3 + 3 * n_small]
        n_in = n_small + 3 + 3 * n_small
        out_refs = refs[n_in:n_in + 4 * n_small + 3]
        pack_a, pack_b, all_a, all_b, send_sems, recv_sems = refs[n_in + 4 * n_small + 3:]
        x, y, c = _position()
        me = 4 * x + 2 * y + c
        pack_a[...] = jnp.zeros_like(pack_a)
        pack_b[...] = jnp.zeros_like(pack_b)
        for i in range(4):
            pack_a[i:i + 1, :] = g_refs[i][...]
        pack_a[4:5, 0:256] = g_refs[4][...]
        pack_a[5:6, 0:256] = g_refs[5][...]
        pack_a[6:7, 0:128] = g_refs[6][...]
        pack_a[7:8, 0:128] = g_refs[7][...]
        pack_a[7:8, 128:256] = g_refs[11][...]
        pack_b[0:32, 0:128] = g_refs[8][...]
        pack_b[32:48, :] = g_refs[9][...]
        pack_b[48:64, :] = g_refs[10][...]
        all_a[me] = pack_a[...]
        all_b[me] = pack_b[...]
        copies = []
        for k in range(1, n_dev):
            fx, fy, fc = (k >> 2) & 1, (k >> 1) & 1, k & 1
            to = (1 - x if fx else x, 1 - y if fy else y, 1 - c if fc else c)
            for t, (pack, dst) in enumerate(((pack_a, all_a), (pack_b, all_b))):
                copies.append(pltpu.make_async_remote_copy(
                    src_ref=pack, dst_ref=dst.at[me], send_sem=send_sems.at[2 * (k - 1) + t],
                    recv_sem=recv_sems.at[2 * (k - 1) + t], device_id=to, device_id_type=MESH_ID))
        for cp in copies:
            cp.start()
        for cp in copies:
            cp.wait()
        sum_a, sum_b = all_a[0], all_b[0]
        for d in range(1, n_dev):
            sum_a = sum_a + all_a[d]
            sum_b = sum_b + all_b[d]
        gsum = [sum_a[0:1], sum_a[1:2], sum_a[2:3], sum_a[3:4], sum_a[4:5, 0:256], sum_a[5:6, 0:256],
                sum_a[6:7, 0:128], sum_a[7:8, 0:SWA_Q_HEADS], sum_b[0:32, 0:SWA_Q_HEADS]]
        for i in range(n_small):
            w_r, m_r, v_r = wmv_refs[3 * i:3 * i + 3]
            delta, new_m, new_v = _adamw_math(w_r[...], gsum[i], m_r[...], v_r[...])
            out_refs[4 * i][...] = gsum[i]
            out_refs[4 * i + 1][...] = delta
            out_refs[4 * i + 2][...] = new_m
            out_refs[4 * i + 3][...] = new_v
        out_refs[4 * n_small][...] = sum_b[32:48]
        out_refs[4 * n_small + 1][...] = sum_b[48:64]
        out_refs[4 * n_small + 2][...] = sum_a[7:8, 128:256]

    n_in = n_small + 3 + 3 * n_small
    body, extra, extra_specs = _after(body, n_in, dep)
    out_shape = [jax.ShapeDtypeStruct(s, F32) for s in shapes for _ in range(4)]
    out_shape += [jax.ShapeDtypeStruct((GLA_GATE_RANK, 256), F32)] * 2 + [jax.ShapeDtypeStruct((1, 128), F32)]
    out = pl.pallas_call(
        body, name="small_update",
        in_specs=[_vmem_spec()] * n_in + extra_specs, out_specs=[_vmem_spec()] * len(out_shape),
        out_shape=out_shape,
        scratch_shapes=[pltpu.VMEM((8, D_MODEL), F32), pltpu.VMEM((64, 256), F32),
                        pltpu.VMEM((n_dev, 8, D_MODEL), F32), pltpu.VMEM((n_dev, 64, 256), F32),
                        pltpu.SemaphoreType.DMA((2 * (n_dev - 1),)), pltpu.SemaphoreType.DMA((2 * (n_dev - 1),))],
    )(*grads, *gate_grads, *wmv, *extra)
    per_name = [tuple(out[4 * i:4 * i + 4]) for i in range(n_small)]
    return per_name, out[4 * n_small], out[4 * n_small + 1], out[4 * n_small + 2]


def _pad_heads(t, n_heads, axis=-1):
    axis = axis % t.ndim
    shape = t.shape
    t = t.reshape(shape[:axis] + (n_heads, 64) + shape[axis + 1:])
    pad = [(0, 0)] * t.ndim
    pad[axis + 1] = (0, HEAD_PAD - 64)
    return jnp.pad(t, pad).reshape(shape[:axis] + (n_heads * HEAD_PAD,) + shape[axis + 1:])


def _unpad_heads(t, n_heads, axis=-1):
    axis = axis % t.ndim
    shape = t.shape
    t = t.reshape(shape[:axis] + (n_heads, HEAD_PAD) + shape[axis + 1:])
    t = lax.slice_in_dim(t, 0, 64, axis=axis + 1)
    return t.reshape(shape[:axis] + (n_heads * 64,) + shape[axis + 1:])


def _pad_gate(w, first_row):
    return jnp.pad(_pad_heads(w, 4), ((first_row, 128 - GLA_GATE_RANK - first_row), (0, 0)))


def _own_slot(shard, chip):
    zone = lax.empty((N_CHIPS,) + shard.shape, shard.dtype)
    return lax.dynamic_update_slice(zone, shard[None], (chip,) + (0,) * shard.ndim)


def _reduce_to_owners(grads, axes, pos, tag, overlap):
    n = len(grads)

    def half_shape(g, axis):
        return (N_CHIPS, g.shape[1] // 2, g.shape[2]) if axis == ROWS else (N_CHIPS, g.shape[1], g.shape[2] // 2)

    lands = [lax.empty(half_shape(g, axis), F32) for g, axis in zip(grads, axes)]
    handle, token = _split_start(tag + "_pair_start", list(grads) + lands, n, _pair_swap_plan(axes))
    got = _split_wait(tag + "_pair_wait", handle, n, _pair_swap_plan(axes), overlap[0](token))
    sums = [_pair_add_call(got[a], got[n + a], pos, f"{tag}_pair_add{a}", axes[a]) for a in range(n)]
    lands = [lax.empty((3,) + s.shape[1:], s.dtype) for s in sums]
    handle, token = _split_start(tag + "_chip_start", sums + lands, 3 * n, _chip_swap_plan(n))
    got = _split_wait(tag + "_chip_wait", handle, 3 * n, _chip_swap_plan(n), overlap[1](token))
    halves = [_chip_add_call(got[a], got[n + a], pos, f"{tag}_chip_add{a}", axes[a]) for a in range(n)]
    handle, token = _split_start(tag + "_join_start", halves, n, _pair_join_plan(axes))
    return _split_wait(tag + "_join_wait", handle, n, _pair_join_plan(axes), overlap[2](token))


def kernel(x, norm_mix_pre, w_in, w_gate_up_fwd, b_gate_fwd, w_gate_up_bwd, b_gate_bwd, gla_norm, swa_sink, rel_bias, w_out, norm_mix_post, norm_mlp_pre, w_up, w_down, norm_mlp_post, loss_target, m_norm_mix_pre, m_w_in, m_w_gate_up_fwd, m_b_gate_fwd, m_w_gate_up_bwd, m_b_gate_bwd, m_gla_norm, m_swa_sink, m_rel_bias, m_w_out, m_norm_mix_post, m_norm_mlp_pre, m_w_up, m_w_down, m_norm_mlp_post, v_norm_mix_pre, v_w_in, v_w_gate_up_fwd, v_b_gate_fwd, v_w_gate_up_bwd, v_b_gate_bwd, v_gla_norm, v_swa_sink, v_rel_bias, v_w_out, v_norm_mix_post, v_norm_mlp_pre, v_w_up, v_w_down, v_norm_mlp_post):
    given = dict(locals())
    cx, cy, cc = _position()
    chip = (2 * cx + cy).astype(jnp.int32)
    pos = jnp.stack([chip, cc.astype(jnp.int32)])
    seq, tgt = x[0], loss_target[0]
    L = seq.shape[0]

    gates = jnp.concatenate([w_gate_up_fwd[0], w_gate_up_bwd[0]], axis=0).astype(COMM_DTYPE)
    all_in, all_gates = _first_gather_call([w_in[0].T.astype(COMM_DTYPE), gates], [COLS, ROWS])
    rest = [w_out[0].astype(COMM_DTYPE), jnp.stack([w_up[0], w_down[0]]).astype(COMM_DTYPE)]
    stage_one, stage_two = _gather_plans([ROWS, ROWS])
    handle, token = _split_start("gather_chip_start", rest + [_own_slot(s, chip) for s in rest] + [all_gates], 6,
                                 stage_one)

    w_in_t = _mx(all_in.reshape(IN_COLS, D_MODEL))
    gates_full = jnp.concatenate([all_gates[j] for j in range(N_CHIPS)], axis=1)
    wgf_p = _mx(_pad_gate(gates_full[:GLA_GATE_RANK], 0))
    wgb_p = _mx(_pad_gate(gates_full[GLA_GATE_RANK:], GLA_GATE_RANK))
    bf_p, bb_p = _pad_heads(b_gate_fwd, 4), _pad_heads(b_gate_bwd, 4)
    buckets = jnp.asarray(_band_buckets())
    bias = _bias_call(rel_bias, buckets)
    sink1 = swa_sink.reshape(SWA_Q_HEADS)

    qa, ka, va, ga, qs, ks, vs, za = _proj_call(seq, norm_mix_pre, w_in_t, dep=token)
    halo = ((SWA_BLOCK, SWA_BLOCK), (0, 0))
    ks_p, vs_p = jnp.pad(ks, halo), jnp.pad(vs, halo)
    o_f, o_b, s_f, s_b = _gla_fwd_call(qa, ka, va, za, wgf_p, bf_p, wgb_p, bb_p)
    arrays = _split_wait("gather_chip_wait", handle, 6, stage_one, o_f)
    handle, token = _split_start("gather_pair_start", list(arrays), 6, stage_two)
    o_s = _swa_fwd_call(qs, ks_p, vs_p, bias, sink1, dep=token)
    arrays = _split_wait("gather_pair_wait", handle, 6, stage_two, o_s)
    w_out_full = _mx(arrays[2].reshape(N_CHIPS * R_OUT, D_MODEL))
    w_ud = _mx(arrays[3])
    cat, mix, h1, n2 = _mix_call(o_f, o_b, ga, o_s, seq, gla_norm, w_out_full, norm_mix_post, norm_mlp_pre)
    a, rz, dh2, dff, loss, d_post2 = _mlp_fwd_call(n2, h1, tgt, w_ud, norm_mlp_post)

    dz, dn2 = _mlp_bwd_call(dff, rz, w_ud)
    dw_down, dw_up4 = _mlp_wgrad_call(a, dff, n2, dz)
    dh1, do, dga, dos, dw_out, d_pre2, d_post, d_gn = _mix_bwd_call(
        dn2, dh2, h1, mix, cat, o_f, o_b, ga, gla_norm, norm_mix_post, norm_mlp_pre, w_out_full)
    done = {}

    def gla_backward(tok):
        done["gla"] = _gla_bwd_call(qa, ka, va, za, do, s_f, s_b, wgf_p, bf_p, wgb_p, bb_p, dep=tok)
        return done["gla"][0]

    def swa_backward(tok):
        done["swa"] = _swa_bwd_call(qs, ks_p, vs_p, bias, sink1, dos, dep=tok)
        return done["swa"][0]

    def in_backward(tok):
        dqf, dkf, dvf, dzf, _, _, dqb, dkb, dvb, dzb, _, _ = done["gla"]
        dqs, dks_p, dvs_p, _, _ = done["swa"]
        done["in"] = _in_bwd_call(
            seq, dh1, norm_mix_pre, w_in_t,
            pairs=[(T_QA, (dqf, dqb)), (T_KA, (dkf, dkb)), (T_VA, (dvf, dvb)), (T_ZA, (dzf, dzb))],
            singles=[(T_GA, dga), (T_QS, dqs)], halos=[(T_KS, dks_p), (T_VS, dvs_p)], dep=tok)
        return done["in"][0]

    g_up, g_down, g_out = _reduce_to_owners(
        [dw_up4, dw_down.reshape(N_CHIPS, R_DOWN, D_MODEL), dw_out.reshape(N_CHIPS, R_OUT, D_MODEL)],
        [ROWS, ROWS, ROWS], pos, "mlp", [swa_backward, gla_backward, in_backward])
    dx, dw_in_t, d_pre = done["in"]
    dwf, dbf, dwb, dbb = done["gla"][4], done["gla"][5], done["gla"][10], done["gla"][11]
    drel, dsink = _relbias_call(done["swa"][3], done["swa"][4], buckets)

    small_grads = [d_pre, d_post, d_pre2, d_post2, _unpad_heads(dbf, 4), _unpad_heads(dbb, 4), d_gn, dsink, drel]
    gate_grads = [_unpad_heads(dwf[:GLA_GATE_RANK], 4), _unpad_heads(dwb[GLA_GATE_RANK:2 * GLA_GATE_RANK], 4)]
    small_params = [(given[n], given["m_" + n], given["v_" + n]) for n in SMALL_NAMES]
    upd = {}

    def update_up(tok):
        upd["w_up"] = (g_up,) + tuple(_adamw_call(w_up[0], g_up, m_w_up[0], v_w_up[0], "adamw_w_up", dep=tok))
        return upd["w_up"][1]

    def update_small(tok):
        per_name, gf_sum, gb_sum, upd["loss"] = _small_update_call(small_grads, gate_grads + [loss], small_params,
                                                                   dep=tok)
        upd.update(dict(zip(SMALL_NAMES, per_name)))
        for name, total in (("w_gate_up_fwd", gf_sum), ("w_gate_up_bwd", gb_sum)):
            g = lax.dynamic_slice(total, (0, chip * 64), (GLA_GATE_RANK, 64))
            upd[name] = (g,) + tuple(_adamw_call(given[name][0], g, given["m_" + name][0], given["v_" + name][0],
                                                 "adamw_" + name))
        upd["w_down"] = (g_down,) + tuple(
            _adamw_call(w_down[0], g_down, m_w_down[0], v_w_down[0], "adamw_w_down", dep=gf_sum))
        return upd["w_down"][1]

    def update_out(tok):
        upd["w_out"] = (g_out,) + tuple(_adamw_call(w_out[0], g_out, m_w_out[0], v_w_out[0], "adamw_w_out", dep=tok))
        return upd["w_out"][1]

    (g_in_t,) = _reduce_to_owners([dw_in_t.reshape(N_CHIPS, R_IN, D_MODEL)], [COLS], pos, "in",
                                  [update_up, update_small, update_out])
    in_t = (g_in_t,) + tuple(_adamw_call(w_in[0].T, g_in_t, m_w_in[0].T, v_w_in[0].T, "adamw_w_in"))
    upd["w_in"] = tuple(t.T for t in in_t)

    big = ("w_in", "w_gate_up_fwd", "w_gate_up_bwd", "w_out", "w_up", "w_down")
    names = ["norm_mix_pre", "w_in", "w_gate_up_fwd", "b_gate_fwd", "w_gate_up_bwd", "b_gate_bwd", "gla_norm",
             "swa_sink", "rel_bias", "w_out", "norm_mix_post", "norm_mlp_pre", "w_up", "w_down", "norm_mlp_post"]
    outs = [upd["loss"][0, 0], dx[None]]
    for kind in range(4):
        outs += [upd[n][kind][None] if n in big else upd[n][kind] for n in names]
    return tuple(outs)
```

```python
import math

import numpy as np
import jax
import jax.numpy as jnp
from jax import lax
from jax.experimental import pallas as pl
from jax.experimental.pallas import tpu as pltpu

F32 = jnp.float32
MXU_DTYPE = jnp.bfloat16
COMM_DTYPE = jnp.bfloat16

D_MODEL = 1024
D_FF = 4096
N_CHIPS = 4
GLA_HEADS = 4
GLA_CHUNK = 64
GLA_GATE_RANK = 16
GLA_GATE_NORM = 16.0
SWA_Q_HEADS = 8
SWA_KV_HEADS = 2
SWA_BLOCK = 128
REL_BUCKETS = 32
REL_MAX_DIST = 128
NORM_EPS = 1e-6
HEAD_PAD = 128

ADAM_LR = 0.001
ADAM_B1 = 0.9
ADAM_B2 = 0.999
ADAM_EPS = 1e-08
ADAM_WD = 0.01
ADAM_STEP = 10

OUT_PAD = 1024

R_IN, R_OUT, R_UP, R_DOWN = 584, 256, 1024, 1024

VMEM_BIG = 56 * 1024 * 1024
MESH_AXES = ("x", "y", "c")
MESH_ID = pl.DeviceIdType.MESH


def _mx(a):
    return a.astype(MXU_DTYPE)


def _dot(a, b):
    return jnp.dot(a, b, preferred_element_type=F32)


def _dot_nt(a, b):
    return lax.dot_general(a, b, (((1,), (1,)), ((), ())), preferred_element_type=F32)


def _dot_tn(a, b):
    return lax.dot_general(a, b, (((0,), (0,)), ((), ())), preferred_element_type=F32)


def _rms_r(x):
    return lax.rsqrt(jnp.mean(x * x, axis=-1, keepdims=True) + NORM_EPS)


def _rms_bwd(x, r, g, dy):
    xh = x * r
    gdy = dy * g
    dx = r * (gdy - xh * jnp.mean(gdy * xh, axis=-1, keepdims=True))
    return dx, jnp.sum(dy * xh, axis=0, keepdims=True)


def _low_half(rows):
    return lax.broadcasted_iota(jnp.int32, (rows, HEAD_PAD), 1) < 64


def _spread_heads(x):
    low = _low_half(x.shape[0])
    parts = []
    for p in range(x.shape[1] // HEAD_PAD):
        pair = x[:, HEAD_PAD * p:HEAD_PAD * (p + 1)]
        parts += [jnp.where(low, pair, 0.0), jnp.where(low, pltpu.roll(pair, 64, 1), 0.0)]
    return jnp.concatenate(parts, axis=1)


def _squeeze_heads(x):
    low = _low_half(x.shape[0])
    parts = []
    for p in range(x.shape[1] // (2 * HEAD_PAD)):
        even = x[:, 2 * HEAD_PAD * p:2 * HEAD_PAD * p + HEAD_PAD]
        odd = x[:, 2 * HEAD_PAD * p + HEAD_PAD:2 * HEAD_PAD * (p + 1)]
        parts.append(jnp.where(low, even, pltpu.roll(odd, 64, 1)))
    return parts[0] if len(parts) == 1 else jnp.concatenate(parts, axis=1)


def _params(sem=None, vmem=None):
    kw = {}
    if sem is not None:
        kw["dimension_semantics"] = sem
    if vmem is not None:
        kw["vmem_limit_bytes"] = vmem
    return pltpu.CompilerParams(**kw)


def _vmem_spec():
    return pl.BlockSpec(memory_space=pltpu.VMEM)


def _row_spec(tm, width):
    return pl.BlockSpec((tm, width), lambda i: (i, 0))


def _full_spec(shape):
    return pl.BlockSpec(shape, lambda i: (0,) * len(shape))


def _any_spec():
    return pl.BlockSpec(memory_space=pl.ANY)


def _after(body, n_in, dep):
    if dep is None:
        return body, [], []
    return (lambda *refs: body(*refs[:n_in], *refs[n_in + 1:])), [dep], [_any_spec()]


T_QA, T_KA, T_VA, T_GA = (0, 256, 4), (256, 256, 4), (512, 512, 0), (1024, 512, 0)
T_QS, T_KS, T_VS = (1568, 512, 8), (2080, 128, 2), (2208, 128, 2)
T_ZA = (1536, 128, 0)
ZA_COLS = 2 * GLA_GATE_RANK
IN_COLS = 2336


def _proj_call(x, g_pre, w_in_t, dep=None):
    L = x.shape[0]
    tm = min(256, L)
    groups = [(T_QA, F32), (T_KA, F32), (T_VA, MXU_DTYPE), (T_GA, F32),
              (T_QS, MXU_DTYPE), (T_KS, MXU_DTYPE), (T_VS, MXU_DTYPE), (T_ZA, F32)]
    widths = [rows * (2 if heads else 1) for (_, rows, heads), _ in groups]

    def body(x_ref, g_ref, w_ref, *outs):
        xv = x_ref[...]
        u = _mx(xv * _rms_r(xv) * g_ref[...])
        for ref, (grp, _) in zip(outs, groups):
            first, rows, heads = grp
            val = _dot_nt(u, w_ref[first:first + rows, :])
            if heads:
                val = _spread_heads(val)
            if grp is T_ZA:
                val = jnp.where(lax.broadcasted_iota(jnp.int32, val.shape, 1) < ZA_COLS, val, 0.0)
            ref[...] = val.astype(ref.dtype)

    body, extra, extra_specs = _after(body, 3, dep)
    return pl.pallas_call(
        body, name="proj_fwd", grid=(L // tm,),
        in_specs=[_row_spec(tm, D_MODEL), _full_spec((1, D_MODEL)), _vmem_spec()] + extra_specs,
        out_specs=[_row_spec(tm, w) for w in widths],
        out_shape=[jax.ShapeDtypeStruct((L, w), dt) for w, (_, dt) in zip(widths, groups)],
        compiler_params=_params(("arbitrary",), VMEM_BIG),
    )(x, g_pre, w_in_t, *extra)


def _tri_masks():
    row = lax.broadcasted_iota(jnp.int32, (GLA_CHUNK, GLA_CHUNK), 0)
    col = lax.broadcasted_iota(jnp.int32, (GLA_CHUNK, GLA_CHUNK), 1)
    return row >= col, row <= col


def _chunk_sums(tri_m, x):
    hi = _mx(x)
    rest = x - hi.astype(F32)
    mid = _mx(rest)
    lo = _mx(rest - mid.astype(F32))
    return _dot(tri_m, hi) + _dot(tri_m, mid) + _dot(tri_m, lo)


def _gla_block_pre(q_r, k_r, z_r, w_r, b_r, rev, nc, qd_s, ki_s, ks_s, dec_s, keep=None):
    tri_f, tri_b = _tri_masks()
    tri_m = _mx((tri_b if rev else tri_f).astype(F32))
    g = _dot(_mx(z_r[...]), w_r[...]) + b_r[...]
    la = (jnp.minimum(g, 0.0) - jnp.log(1.0 + jnp.exp(-jnp.abs(g)))) / GLA_GATE_NORM
    sums, lasts = [], []
    for c in range(nc):
        b_c = _chunk_sums(tri_m, la[GLA_CHUNK * c:GLA_CHUNK * (c + 1)])
        blast = b_c[0:1] if rev else b_c[GLA_CHUNK - 1:GLA_CHUNK]
        dec_s[c] = jnp.exp(blast)
        sums.append(b_c)
        lasts.append(jnp.broadcast_to(blast, b_c.shape))
    b = jnp.concatenate(sums, axis=0)
    eb = jnp.exp(b)
    enb = jnp.exp(-b)
    elb = jnp.exp(jnp.concatenate(lasts, axis=0) - b)
    k = k_r[...]
    qd_s[...] = (q_r[...] * 0.125 * eb).astype(qd_s.dtype)
    ki_s[...] = (k * enb).astype(ki_s.dtype)
    ks_s[...] = (k * elb).astype(ks_s.dtype)
    if keep is not None:
        for ref, val in zip(keep, (g, eb, enb, elb)):
            ref[...] = val


def _gla_fwd_call(qa, ka, va, za, wgf, bgf, wgb, bgb):
    L = qa.shape[0]
    br = min(512, L)
    nb, nc, n_chunks = L // br, br // GLA_CHUNK, L // GLA_CHUNK
    hw = GLA_HEADS * HEAD_PAD

    def body(qaf, kaf, vaf, zaf, qab, kab, vab, zab, wgf_r, bgf_r, wgb_r, bgb_r,
             of_r, ob_r, sf_r, sb_r, st_f, st_b, pre_f, pre_b):
        @pl.when(pl.program_id(0) == 0)
        def _():
            st_f[...] = jnp.zeros_like(st_f)
            st_b[...] = jnp.zeros_like(st_b)

        _gla_block_pre(qaf, kaf, zaf, wgf_r, bgf_r, False, nc, *pre_f)
        _gla_block_pre(qab, kab, zab, wgb_r, bgb_r, True, nc, *pre_b)
        tri_f, tri_b = _tri_masks()

        def one(tri, pre, v_r, o_r, s_r, st, ci):
            qd_s, ki_s, ks_s, dec_s = pre
            rows = pl.ds(pl.multiple_of(ci * GLA_CHUNK, GLA_CHUNK), GLA_CHUNK)
            dec = dec_s[ci]
            heads = range(GLA_HEADS)
            lanes = [slice(HEAD_PAD * h, HEAD_PAD * (h + 1)) for h in heads]
            qd = [qd_s[rows, sl] for sl in lanes]
            v = [v_r[rows, sl] for sl in lanes]
            s_t = [st[h] for h in heads]
            a = [_dot_nt(qd[h], ki_s[rows, lanes[h]]) for h in heads]
            carried = [_dot_nt(qd[h], _mx(s_t[h])) for h in heads]
            grown = [_dot_tn(v[h], ks_s[rows, lanes[h]]) for h in heads]
            a = [_mx(jnp.where(tri, a[h], 0.0)) for h in heads]
            inner = [_dot(a[h], v[h]) for h in heads]
            for h in heads:
                s_r[ci, h] = s_t[h]
                o_r[rows, lanes[h]] = inner[h] + carried[h]
                st[h] = s_t[h] * dec[:, lanes[h]] + grown[h]

        def loop(t, carry):
            one(tri_f, pre_f, vaf, of_r, sf_r, st_f, t)
            one(tri_b, pre_b, vab, ob_r, sb_r, st_b, nc - 1 - t)
            return carry

        lax.fori_loop(0, nc, loop, 0, unroll=True)

    fwd = lambda i: (i, 0)
    bwd = lambda i: (nb - 1 - i, 0)
    ins = lambda m: [pl.BlockSpec((br, hw), m), pl.BlockSpec((br, hw), m),
                     pl.BlockSpec((br, hw), m), pl.BlockSpec((br, 128), m)]
    wspecs = [_full_spec((128, hw)), _full_spec((1, hw))] * 2
    s_shape = (nc, GLA_HEADS, HEAD_PAD, HEAD_PAD)
    pre_scratch = [pltpu.VMEM((br, hw), MXU_DTYPE)] * 3 + [pltpu.VMEM((nc, 1, hw), F32)]
    return pl.pallas_call(
        body, name="gla_fwd", grid=(nb,),
        in_specs=ins(fwd) + ins(bwd) + wspecs,
        out_specs=[pl.BlockSpec((br, hw), fwd), pl.BlockSpec((br, hw), bwd),
                   pl.BlockSpec(s_shape, lambda i: (i, 0, 0, 0)),
                   pl.BlockSpec(s_shape, lambda i: (nb - 1 - i, 0, 0, 0))],
        out_shape=[jax.ShapeDtypeStruct((L, hw), F32), jax.ShapeDtypeStruct((L, hw), F32),
                   jax.ShapeDtypeStruct((n_chunks,) + s_shape[1:], F32),
                   jax.ShapeDtypeStruct((n_chunks,) + s_shape[1:], F32)],
        scratch_shapes=[pltpu.VMEM(s_shape[1:], F32), pltpu.VMEM(s_shape[1:], F32), pre_scratch, pre_scratch],
        compiler_params=_params(("arbitrary",), VMEM_BIG),
    )(qa, ka, va, za, qa, ka, va, za, wgf, bgf, wgb, bgb)


def _gla_bwd_call(qa, ka, va, za, do, sf, sb, wgf, bgf, wgb, bgb, dep=None):
    L = qa.shape[0]
    br = min(256, L)
    nb, nc = L // br, br // GLA_CHUNK
    hw = GLA_HEADS * HEAD_PAD

    def body(qaf, kaf, vaf, zaf, dof, sf_r, qab, kab, vab, zab, dob, sb_r, wgf_r, bgf_r, wgb_r, bgb_r,
             dqf, dkf, dvf, dzf, dwf, dbf, dqb, dkb, dvb, dzb, dwb, dbb, gt_f, gt_b, pre_f, pre_b):
        @pl.when(pl.program_id(0) == 0)
        def _():
            for ref in (gt_f, gt_b, dwf, dbf, dwb, dbb):
                ref[...] = jnp.zeros_like(ref)

        _gla_block_pre(qaf, kaf, zaf, wgf_r, bgf_r, False, nc, *pre_f[:4], keep=pre_f[4:8])
        _gla_block_pre(qab, kab, zab, wgb_r, bgb_r, True, nc, *pre_b[:4], keep=pre_b[4:8])
        tri_f, tri_b = _tri_masks()
        row_w = lax.broadcasted_iota(jnp.int32, (GLA_CHUNK, HEAD_PAD), 0)

        def one(rev, pre, q_r, k_r, v_r, do_r, s_r, dq_r, dk_r, dv_r, gt, ci):
            qd_s, ki_s, ks_s, dec_s, _, eb_s, enb_s, elb_s, db_s = pre
            tri = tri_b if rev else tri_f
            last_row = 0 if rev else GLA_CHUNK - 1
            rows = pl.ds(pl.multiple_of(ci * GLA_CHUNK, GLA_CHUNK), GLA_CHUNK)
            dec = dec_s[ci]
            heads = range(GLA_HEADS)
            lanes = [slice(HEAD_PAD * h, HEAD_PAD * (h + 1)) for h in heads]
            qd = [qd_s[rows, sl] for sl in lanes]
            ki = [ki_s[rows, sl] for sl in lanes]
            ks = [ks_s[rows, sl] for sl in lanes]
            v = [v_r[rows, sl] for sl in lanes]
            do_h = [_mx(do_r[rows, sl]) for sl in lanes]
            s_t = [s_r[ci, h] for h in heads]
            g_t = [gt[h] for h in heads]
            g_m = [_mx(g_t[h]) for h in heads]
            a = [_dot_nt(qd[h], ki[h]) for h in heads]
            da = [_dot_nt(do_h[h], v[h]) for h in heads]
            dv_carried = [_dot_nt(ks[h], g_m[h]) for h in heads]
            dqd_carried = [_dot(do_h[h], _mx(s_t[h])) for h in heads]
            dks = [_dot(v[h], g_m[h]) for h in heads]
            g_grown = [_dot_tn(do_h[h], qd[h]) for h in heads]
            a = [_mx(jnp.where(tri, a[h], 0.0)) for h in heads]
            da = [_mx(jnp.where(tri, da[h], 0.0)) for h in heads]
            dv_inner = [_dot_tn(a[h], do_h[h]) for h in heads]
            dqd_inner = [_dot(da[h], ki[h]) for h in heads]
            dki = [_dot_tn(da[h], qd[h]) for h in heads]
            for h in heads:
                sl = lanes[h]
                dv_r[rows, sl] = dv_inner[h] + dv_carried[h]
                ddec = jnp.sum(g_t[h] * s_t[h], axis=0, keepdims=True)
                gt[h] = g_t[h] * dec[:, sl] + g_grown[h]
                dq = (dqd_inner[h] + dqd_carried[h]) * eb_s[rows, sl] * 0.125
                dk_state = dks[h] * elb_s[rows, sl]
                dk = dki[h] * enb_s[rows, sl] + dk_state
                dq_r[rows, sl] = dq
                dk_r[rows, sl] = dk
                k = k_r[rows, sl]
                dblast = jnp.sum(dk_state * k, axis=0, keepdims=True) + dec[:, sl] * ddec
                db_s[rows, sl] = q_r[rows, sl] * dq - k * dk + jnp.where(row_w == last_row, dblast, 0.0)

        def loop(t, carry):
            one(False, pre_f, qaf, kaf, vaf, dof, sf_r, dqf, dkf, dvf, gt_f, nc - 1 - t)
            one(True, pre_b, qab, kab, vab, dob, sb_r, dqb, dkb, dvb, gt_b, t)
            return carry

        lax.fori_loop(0, nc, loop, 0, unroll=True)

        def gate_grads(rev, pre, z_r, w_r, dz_r, dw_r, dbias_r):
            g_s, db_s = pre[4], pre[8]
            back_m = _mx((tri_f if rev else tri_b).astype(F32))
            db = db_s[...]
            dla = jnp.concatenate([_chunk_sums(back_m, db[GLA_CHUNK * c:GLA_CHUNK * (c + 1)]) for c in range(nc)],
                                  axis=0)
            dg = dla * (1.0 / GLA_GATE_NORM) * (1.0 / (1.0 + jnp.exp(g_s[...])))
            dg_m = _mx(dg)
            dz_r[...] = _dot_nt(dg_m, w_r[...])
            dw_r[...] += _dot_tn(_mx(z_r[...]), dg_m)
            dbias_r[...] += jnp.sum(dg, axis=0, keepdims=True)

        gate_grads(False, pre_f, zaf, wgf_r, dzf, dwf, dbf)
        gate_grads(True, pre_b, zab, wgb_r, dzb, dwb, dbb)

    last_first = lambda i: (nb - 1 - i, 0)
    first_last = lambda i: (i, 0)
    s_shape = (nc, GLA_HEADS, HEAD_PAD, HEAD_PAD)

    def ins(m):
        return [pl.BlockSpec((br, hw), m), pl.BlockSpec((br, hw), m), pl.BlockSpec((br, hw), m),
                pl.BlockSpec((br, 128), m), pl.BlockSpec((br, hw), m),
                pl.BlockSpec(s_shape, lambda i: m(i) + (0, 0))]

    def outs(m):
        return [pl.BlockSpec((br, hw), m), pl.BlockSpec((br, hw), m), pl.BlockSpec((br, hw), m),
                pl.BlockSpec((br, 128), m), _full_spec((128, hw)), _full_spec((1, hw))]

    out_shape = [jax.ShapeDtypeStruct((L, hw), F32)] * 3 + [
        jax.ShapeDtypeStruct((L, 128), F32), jax.ShapeDtypeStruct((128, hw), F32),
        jax.ShapeDtypeStruct((1, hw), F32)]
    wspecs = [_full_spec((128, hw)), _full_spec((1, hw))] * 2
    body, extra, extra_specs = _after(body, 16, dep)
    pre_scratch = ([pltpu.VMEM((br, hw), MXU_DTYPE)] * 3 + [pltpu.VMEM((nc, 1, hw), F32)]
                   + [pltpu.VMEM((br, hw), F32)] * 5)
    return pl.pallas_call(
        body, name="gla_bwd", grid=(nb,),
        in_specs=ins(last_first) + ins(first_last) + wspecs + extra_specs,
        out_specs=outs(last_first) + outs(first_last),
        out_shape=out_shape + out_shape,
        scratch_shapes=[pltpu.VMEM(s_shape[1:], F32), pltpu.VMEM(s_shape[1:], F32), pre_scratch, pre_scratch],
        compiler_params=_params(("arbitrary",), VMEM_BIG),
    )(qa, ka, va, za, do, sf, qa, ka, va, za, do, sb, wgf, bgf, wgb, bgb, *extra)


def _t5_buckets(rel):
    nb = REL_BUCKETS // 2
    ret = (rel > 0).astype(np.int32) * nb
    n = np.abs(rel)
    max_exact = nb // 2
    large = max_exact + (np.log(np.maximum(n, 1).astype(np.float32) / max_exact)
                         / math.log(REL_MAX_DIST / max_exact) * (nb - max_exact)).astype(np.int32)
    large = np.minimum(large, nb - 1)
    return ret + np.where(n < max_exact, n, large)


SWA_GROUP = SWA_Q_HEADS // SWA_KV_HEADS
SWA_SPAN = 3 * SWA_BLOCK
SWA_GROUP_LANES = SWA_GROUP * SWA_BLOCK


def _band_buckets():
    s = np.arange(SWA_SPAN)[:, None]
    c = np.arange(SWA_BLOCK)[None, :]
    return _t5_buckets(s - SWA_BLOCK - c).astype(np.int32)


def _swa_valid(n, seq_len):
    s = lax.broadcasted_iota(jnp.int32, (SWA_SPAN, SWA_GROUP_LANES), 0)
    c = lax.broadcasted_iota(jnp.int32, (SWA_SPAN, SWA_GROUP_LANES), 1) & (SWA_BLOCK - 1)
    rel = s - SWA_BLOCK - c
    key_pos = (n - 1) * SWA_BLOCK + s
    return (jnp.abs(rel) <= SWA_BLOCK) & (key_pos >= 0) & (key_pos < seq_len)


def _swa_sink_row(sink_r, kv):
    lane = lax.broadcasted_iota(jnp.int32, (1, SWA_GROUP_LANES), 1)
    row = jnp.full((1, SWA_GROUP_LANES), sink_r[kv * SWA_GROUP], F32)
    for g in range(1, SWA_GROUP):
        row = jnp.where(lane >= g * SWA_BLOCK, sink_r[kv * SWA_GROUP + g], row)
    return row


def _swa_group(ref, kv):
    first = kv * SWA_GROUP
    return jnp.concatenate([ref[:, HEAD_PAD * h:HEAD_PAD * (h + 1)] for h in range(first, first + SWA_GROUP)],
                           axis=0)


def _swa_softmax(scores, bias_t, sink_row, valid):
    st = scores * 0.125 + bias_t
    st = jnp.where(valid, st, -1e30)
    m = jnp.maximum(jnp.max(st, axis=0, keepdims=True), sink_row)
    p = jnp.exp(st - m)
    e_sink = jnp.exp(sink_row - m)
    inv = 1.0 / (jnp.sum(p, axis=0, keepdims=True) + e_sink)
    return p * inv, e_sink * inv


def _swa_fwd_call(qs, ks, vs, bias, sink, dep=None):
    L = qs.shape[0]

    def body(q_r, k_r, v_r, bias_r, sink_r, o_r):
        n = pl.program_id(0)
        span = pl.ds(pl.multiple_of(n * SWA_BLOCK, SWA_BLOCK), SWA_SPAN)
        valid = _swa_valid(n, L)
        groups = range(SWA_KV_HEADS)
        lanes = [slice(HEAD_PAD * kv, HEAD_PAD * (kv + 1)) for kv in groups]
        scores = [_dot_nt(k_r[span, lanes[kv]], _swa_group(q_r, kv)) for kv in groups]
        probs = [_swa_softmax(scores[kv], bias_r[kv], _swa_sink_row(sink_r, kv), valid)[0] for kv in groups]
        low = _low_half(SWA_BLOCK)
        for kv in groups:
            og = _dot_tn(_mx(probs[kv]), v_r[span, lanes[kv]])
            for pair in range(SWA_GROUP // 2):
                even = og[2 * SWA_BLOCK * pair:2 * SWA_BLOCK * pair + SWA_BLOCK]
                odd = og[2 * SWA_BLOCK * pair + SWA_BLOCK:2 * SWA_BLOCK * (pair + 1)]
                first = HEAD_PAD * (kv * SWA_GROUP // 2 + pair)
                o_r[:, first:first + HEAD_PAD] = jnp.where(low, even, pltpu.roll(odd, 64, 1)).astype(o_r.dtype)

    qw = SWA_Q_HEADS * HEAD_PAD
    body, extra, extra_specs = _after(body, 5, dep)
    return pl.pallas_call(
        body, name="swa_fwd", grid=(L // SWA_BLOCK,),
        in_specs=[_row_spec(SWA_BLOCK, qw), _vmem_spec(), _vmem_spec(), _vmem_spec(),
                  pl.BlockSpec(memory_space=pltpu.SMEM)] + extra_specs,
        out_specs=_row_spec(SWA_BLOCK, qw // 2),
        out_shape=jax.ShapeDtypeStruct((L, qw // 2), MXU_DTYPE),
        compiler_params=_params(("arbitrary",), VMEM_BIG),
    )(qs, ks, vs, bias, sink, *extra)


def _swa_bwd_call(qs, ks, vs, bias, sink, do, dep=None):
    L = qs.shape[0]
    qw = SWA_Q_HEADS * HEAD_PAD
    kw = SWA_KV_HEADS * HEAD_PAD

    def body(q_r, k_r, v_r, bias_r, sink_r, do_r, dq_r, dk_r, dv_r, dbias_r, dsink_r):
        n = pl.program_id(0)

        @pl.when(n == 0)
        def _():
            for ref in (dk_r, dv_r, dbias_r, dsink_r):
                ref[...] = jnp.zeros_like(ref)

        span = pl.ds(pl.multiple_of(n * SWA_BLOCK, SWA_BLOCK), SWA_SPAN)
        valid = _swa_valid(n, L)
        groups = range(SWA_KV_HEADS)
        lanes = [slice(HEAD_PAD * kv, HEAD_PAD * (kv + 1)) for kv in groups]
        kk = [k_r[span, sl] for sl in lanes]
        vv = [v_r[span, sl] for sl in lanes]
        qg = [_swa_group(q_r, kv) for kv in groups]
        dog = [_swa_group(do_r, kv) for kv in groups]
        scores = [_dot_nt(kk[kv], qg[kv]) for kv in groups]
        dp = [_dot_nt(vv[kv], dog[kv]) for kv in groups]
        probs = [_swa_softmax(scores[kv], bias_r[kv], _swa_sink_row(sink_r, kv), valid) for kv in groups]
        ds_m, pn_m = [], []
        for kv in groups:
            pn, p_sink = probs[kv]
            delta = jnp.sum(pn * dp[kv], axis=0, keepdims=True)
            ds = pn * (dp[kv] - delta)
            dsink_r[kv] -= p_sink * delta
            dbias_r[kv] += ds
            ds_m.append(_mx(ds))
            pn_m.append(_mx(pn))
        dqg = [_dot_tn(ds_m[kv], kk[kv]) * 0.125 for kv in groups]
        dkk = [_dot(ds_m[kv], qg[kv]) * 0.125 for kv in groups]
        dvv = [_dot(pn_m[kv], dog[kv]) for kv in groups]
        for kv in groups:
            for g in range(SWA_GROUP):
                h = kv * SWA_GROUP + g
                dq_r[:, HEAD_PAD * h:HEAD_PAD * (h + 1)] = dqg[kv][SWA_BLOCK * g:SWA_BLOCK * (g + 1)]
            dk_r[span, lanes[kv]] += dkk[kv]
            dv_r[span, lanes[kv]] += dvv[kv]

    body, extra, extra_specs = _after(body, 6, dep)
    return pl.pallas_call(
        body, name="swa_bwd", grid=(L // SWA_BLOCK,),
        in_specs=[_row_spec(SWA_BLOCK, qw), _vmem_spec(), _vmem_spec(), _vmem_spec(),
                  pl.BlockSpec(memory_space=pltpu.SMEM), _row_spec(SWA_BLOCK, qw)] + extra_specs,
        out_specs=[_row_spec(SWA_BLOCK, qw), _vmem_spec(), _vmem_spec(), _vmem_spec(), _vmem_spec()],
        out_shape=[jax.ShapeDtypeStruct((L, qw), F32),
                   jax.ShapeDtypeStruct((L + 2 * SWA_BLOCK, kw), F32),
                   jax.ShapeDtypeStruct((L + 2 * SWA_BLOCK, kw), F32),
                   jax.ShapeDtypeStruct((SWA_KV_HEADS, SWA_SPAN, SWA_GROUP_LANES), F32),
                   jax.ShapeDtypeStruct((SWA_KV_HEADS, 1, SWA_GROUP_LANES), F32)],
        compiler_params=_params(("arbitrary",), VMEM_BIG),
    )(qs, ks, vs, bias, sink, do, *extra)


def _bias_call(rel_bias, buckets):
    def body(t_r, bk_r, o_r):
        bk = bk_r[...]
        for h in range(SWA_Q_HEADS):
            acc = jnp.zeros(bk.shape, F32)
            for b in range(REL_BUCKETS):
                acc = jnp.where(bk == b, t_r[b, h], acc)
            g = h % SWA_GROUP
            o_r[h // SWA_GROUP, :, SWA_BLOCK * g:SWA_BLOCK * (g + 1)] = acc

    return pl.pallas_call(
        body, name="band_bias",
        in_specs=[pl.BlockSpec(memory_space=pltpu.SMEM), _vmem_spec()], out_specs=_vmem_spec(),
        out_shape=jax.ShapeDtypeStruct((SWA_KV_HEADS, SWA_SPAN, SWA_GROUP_LANES), F32),
    )(rel_bias, buckets)


def _relbias_call(dbias, dsink, buckets, dep=None):
    def body(db_r, ds_r, bk_r, o_r, os_r):
        bk = bk_r[...]
        rowi = lax.broadcasted_iota(jnp.int32, (REL_BUCKETS, 128), 0)
        lanei = lax.broadcasted_iota(jnp.int32, (REL_BUCKETS, 128), 1)
        lane1 = lax.broadcasted_iota(jnp.int32, (1, 128), 1)
        acc = jnp.zeros((REL_BUCKETS, 128), F32)
        acc_sink = jnp.zeros((1, 128), F32)
        for h in range(SWA_Q_HEADS):
            kv, g = h // SWA_GROUP, h % SWA_GROUP
            lanes = slice(SWA_BLOCK * g, SWA_BLOCK * (g + 1))
            part = db_r[kv, :, lanes]
            for b in range(REL_BUCKETS):
                s = jnp.sum(jnp.where(bk == b, part, 0.0))
                acc = acc + jnp.where((rowi == b) & (lanei == h), s, 0.0)
            acc_sink = acc_sink + jnp.where(lane1 == h, jnp.sum(ds_r[kv, :, lanes]), 0.0)
        o_r[...] = acc
        os_r[...] = acc_sink

    body, extra, extra_specs = _after(body, 3, dep)
    return pl.pallas_call(
        body, name="relbias_grad",
        in_specs=[_vmem_spec()] * 3 + extra_specs, out_specs=[_vmem_spec()] * 2,
        out_shape=[jax.ShapeDtypeStruct((REL_BUCKETS, 128), F32), jax.ShapeDtypeStruct((1, 128), F32)],
    )(dbias, dsink, buckets, *extra)


def _mix_call(o_f, o_b, ga, o_s, x, gn, w_out_p, g_post, g_pre2):
    L = x.shape[0]
    tm = min(256, L)
    hw = GLA_HEADS * HEAD_PAD

    def body(of_r, ob_r, ga_r, os_r, x_r, gn_r, w_r, gp_r, g2_r, cat_r, mix_r, h1_r, n2_r):
        gn_v = gn_r[...]
        for h in range(GLA_HEADS):
            sl = slice(HEAD_PAD * h, HEAD_PAD * (h + 1))
            oh = of_r[:, sl] + ob_r[:, sl]
            on = oh * _rms_r(oh) * gn_v
            gate = ga_r[:, sl]
            cat_r[:, sl] = (on * (gate * jax.nn.sigmoid(gate))).astype(cat_r.dtype)
        os_v = os_r[...]
        cat_r[:, hw:] = os_v
        mix = _dot(cat_r[:, :hw], w_r[:hw, :]) + _dot(os_v, w_r[hw:, :])
        mix_r[...] = mix
        h1 = x_r[...] + mix * _rms_r(mix) * gp_r[...]
        h1_r[...] = h1
        n2_r[...] = (h1 * _rms_r(h1) * g2_r[...]).astype(n2_r.dtype)

    return pl.pallas_call(
        body, name="mix_fwd", grid=(L // tm,),
        in_specs=[_row_spec(tm, hw), _row_spec(tm, hw), _row_spec(tm, hw), _row_spec(tm, OUT_PAD - hw),
                  _row_spec(tm, D_MODEL), _full_spec((1, HEAD_PAD)), _vmem_spec(),
                  _full_spec((1, D_MODEL)), _full_spec((1, D_MODEL))],
        out_specs=[_row_spec(tm, OUT_PAD), _row_spec(tm, D_MODEL), _row_spec(tm, D_MODEL), _row_spec(tm, D_MODEL)],
        out_shape=[jax.ShapeDtypeStruct((L, OUT_PAD), MXU_DTYPE), jax.ShapeDtypeStruct((L, D_MODEL), F32),
                   jax.ShapeDtypeStruct((L, D_MODEL), F32), jax.ShapeDtypeStruct((L, D_MODEL), MXU_DTYPE)],
        compiler_params=_params(("arbitrary",), VMEM_BIG),
    )(o_f, o_b, ga, o_s, x, gn, w_out_p, g_post, g_pre2)


def _mlp_fwd_call(n2, h1, tgt, w_ud, g_post):
    L = n2.shape[0]
    tm = min(512, L)
    blk = D_FF // N_CHIPS

    def body(n2_r, h1_r, t_r, w_r, g_r, a_r, rz_r, dh2_r, dff_r, loss_r, dg_r):
        @pl.when(pl.program_id(0) == 0)
        def _():
            loss_r[...] = jnp.zeros_like(loss_r)
            dg_r[...] = jnp.zeros_like(dg_r)

        n2v = n2_r[...]
        ff = jnp.zeros((tm, D_MODEL), F32)
        for j in range(N_CHIPS):
            sl = slice(blk * j, blk * (j + 1))
            rz = jnp.maximum(_dot(n2v, w_r[j, 0]), 0.0)
            a = _mx(rz * rz)
            rz_r[:, sl] = rz.astype(rz_r.dtype)
            a_r[:, sl] = a
            ff = ff + _dot(a, w_r[j, 1])
        g = g_r[...]
        r = _rms_r(ff)
        err = h1_r[...] + ff * r * g - t_r[...]
        loss_r[...] += 0.5 * jnp.sum(err * err) / D_MODEL
        dh2 = err * (1.0 / D_MODEL)
        dh2_r[...] = dh2
        dff, dg = _rms_bwd(ff, r, g, dh2)
        dff_r[...] = dff.astype(dff_r.dtype)
        dg_r[...] += dg

    return pl.pallas_call(
        body, name="mlp_fwd", grid=(L // tm,),
        in_specs=[_row_spec(tm, D_MODEL), _row_spec(tm, D_MODEL), _row_spec(tm, D_MODEL),
                  _vmem_spec(), _full_spec((1, D_MODEL))],
        out_specs=[_row_spec(tm, D_FF), _row_spec(tm, D_FF), _row_spec(tm, D_MODEL), _row_spec(tm, D_MODEL),
                   _full_spec((1, 128)), _full_spec((1, D_MODEL))],
        out_shape=[jax.ShapeDtypeStruct((L, D_FF), MXU_DTYPE), jax.ShapeDtypeStruct((L, D_FF), MXU_DTYPE),
                   jax.ShapeDtypeStruct((L, D_MODEL), F32), jax.ShapeDtypeStruct((L, D_MODEL), MXU_DTYPE),
                   jax.ShapeDtypeStruct((1, 128), F32), jax.ShapeDtypeStruct((1, D_MODEL), F32)],
        compiler_params=_params(("arbitrary",), VMEM_BIG),
    )(n2, h1, tgt, w_ud, g_post)


def _mlp_bwd_call(dff, rz, w_ud):
    L = dff.shape[0]
    tm = min(512, L)
    blk = D_FF // N_CHIPS

    def body(dff_r, rz_r, w_r, dz_r, dn2_r):
        dffv = dff_r[...]
        dn2 = jnp.zeros((tm, D_MODEL), F32)
        for j in range(N_CHIPS):
            sl = slice(blk * j, blk * (j + 1))
            dz = _mx(_dot_nt(dffv, w_r[j, 1]) * 2.0 * rz_r[:, sl].astype(F32))
            dz_r[:, sl] = dz
            dn2 = dn2 + _dot_nt(dz, w_r[j, 0])
        dn2_r[...] = dn2

    return pl.pallas_call(
        body, name="mlp_bwd", grid=(L // tm,),
        in_specs=[_row_spec(tm, D_MODEL), _row_spec(tm, D_FF), _vmem_spec()],
        out_specs=[_row_spec(tm, D_FF), _row_spec(tm, D_MODEL)],
        out_shape=[jax.ShapeDtypeStruct((L, D_FF), MXU_DTYPE), jax.ShapeDtypeStruct((L, D_MODEL), F32)],
        compiler_params=_params(("arbitrary",), VMEM_BIG),
    )(dff, rz, w_ud)


def _mlp_wgrad_call(a, dff, n2, dz):
    L = a.shape[0]
    tf = 512
    per = (D_FF // N_CHIPS) // tf

    def body(a_r, dff_r, n2_r, dz_r, dwd_r, dwu_r):
        dwd_r[...] = _dot_tn(a_r[...], dff_r[...])
        dwu_r[...] = _dot_tn(n2_r[...], dz_r[...])

    return pl.pallas_call(
        body, name="mlp_wgrad", grid=(D_FF // tf,),
        in_specs=[pl.BlockSpec((L, tf), lambda j: (0, j)), _vmem_spec(), _vmem_spec(),
                  pl.BlockSpec((L, tf), lambda j: (0, j))],
        out_specs=[pl.BlockSpec((tf, D_MODEL), lambda j: (j, 0)),
                   pl.BlockSpec((None, D_MODEL, tf), lambda j: (j // per, 0, j % per))],
        out_shape=[jax.ShapeDtypeStruct((D_FF, D_MODEL), F32),
                   jax.ShapeDtypeStruct((N_CHIPS, D_MODEL, D_FF // N_CHIPS), F32)],
        compiler_params=_params(("arbitrary",), VMEM_BIG),
    )(a, dff, n2, dz)


def _mix_bwd_call(dn2, dh2, h1, mix, cat, o_f, o_b, ga, gn, g_post, g_pre2, w_out_p):
    L = dn2.shape[0]
    tm = min(256, L)
    hw = GLA_HEADS * HEAD_PAD

    def body(dn2_r, dh2_r, h1_r, mix_r, cat_r, of_r, ob_r, ga_r, gn_r, gp_r, g2_r, w_r,
             dh1_r, do_r, dga_r, dos_r, dw_r, dg2_r, dgp_r, dgn_r):
        @pl.when(pl.program_id(0) == 0)
        def _():
            for ref in (dw_r, dg2_r, dgp_r, dgn_r):
                ref[...] = jnp.zeros_like(ref)

        h1 = h1_r[...]
        dx2, dg2 = _rms_bwd(h1, _rms_r(h1), g2_r[...], dn2_r[...])
        dh1 = dh2_r[...] + dx2
        dh1_r[...] = dh1
        dg2_r[...] += dg2
        mix = mix_r[...]
        dmix, dgp = _rms_bwd(mix, _rms_r(mix), gp_r[...], dh1)
        dgp_r[...] += dgp
        dmix_m = _mx(dmix)
        dw_r[...] += _dot_tn(cat_r[...], dmix_m)
        dcat = _dot_nt(dmix_m, w_r[...])
        dos_r[...] = _spread_heads(dcat[:, hw:]).astype(dos_r.dtype)
        gn_v = gn_r[...]
        dgn = jnp.zeros((1, HEAD_PAD), F32)
        for h in range(GLA_HEADS):
            sl = slice(HEAD_PAD * h, HEAD_PAD * (h + 1))
            oh = of_r[:, sl] + ob_r[:, sl]
            rr = _rms_r(oh)
            gate = ga_r[:, sl]
            sg = jax.nn.sigmoid(gate)
            doa = dcat[:, sl]
            dga_r[:, sl] = doa * (oh * rr * gn_v) * (sg * (1.0 + gate * (1.0 - sg)))
            do_h, dgn_h = _rms_bwd(oh, rr, gn_v, doa * (gate * sg))
            do_r[:, sl] = do_h
            dgn = dgn + dgn_h
        dgn_r[...] += dgn

    return pl.pallas_call(
        body, name="mix_bwd", grid=(L // tm,),
        in_specs=[_row_spec(tm, D_MODEL)] * 4 + [_row_spec(tm, OUT_PAD)] + [_row_spec(tm, hw)] * 3
        + [_full_spec((1, HEAD_PAD)), _full_spec((1, D_MODEL)), _full_spec((1, D_MODEL)), _vmem_spec()],
        out_specs=[_row_spec(tm, D_MODEL), _row_spec(tm, hw), _row_spec(tm, hw),
                   _row_spec(tm, SWA_Q_HEADS * HEAD_PAD),
                   _full_spec((OUT_PAD, D_MODEL)), _full_spec((1, D_MODEL)), _full_spec((1, D_MODEL)),
                   _full_spec((1, HEAD_PAD))],
        out_shape=[jax.ShapeDtypeStruct((L, D_MODEL), F32), jax.ShapeDtypeStruct((L, hw), F32),
                   jax.ShapeDtypeStruct((L, hw), F32), jax.ShapeDtypeStruct((L, SWA_Q_HEADS * HEAD_PAD), MXU_DTYPE),
                   jax.ShapeDtypeStruct((OUT_PAD, D_MODEL), F32), jax.ShapeDtypeStruct((1, D_MODEL), F32),
                   jax.ShapeDtypeStruct((1, D_MODEL), F32), jax.ShapeDtypeStruct((1, HEAD_PAD), F32)],
        compiler_params=_params(("arbitrary",), VMEM_BIG),
    )(dn2, dh2, h1, mix, cat, o_f, o_b, ga, gn, g_post, g_pre2, w_out_p)


def _in_bwd_call(x, dh1, g_pre, w_in_t, pairs, singles, halos, dep=None):
    L = x.shape[0]
    tm = min(256, L)
    n_pair, n_single, n_halo = len(pairs), len(singles), len(halos)
    groups = [c for c, _ in pairs] + [c for c, _ in singles] + [c for c, _ in halos]

    def body(*refs):
        x_r, dh1_r, g_r, w_r = refs[:4]
        pair_refs = refs[4:4 + 2 * n_pair]
        single_refs = refs[4 + 2 * n_pair:4 + 2 * n_pair + n_single]
        halo_refs = refs[4 + 2 * n_pair + n_single:4 + 2 * n_pair + n_single + n_halo]
        dx_r, dw_r, dg_r = refs[4 + 2 * n_pair + n_single + n_halo:]

        @pl.when(pl.program_id(0) == 0)
        def _():
            dw_r[...] = jnp.zeros_like(dw_r)
            dg_r[...] = jnp.zeros_like(dg_r)

        xv = x_r[...]
        r = _rms_r(xv)
        g = g_r[...]
        u = _mx(xv * r * g)
        vals = [pair_refs[2 * i][...] + pair_refs[2 * i + 1][...] for i in range(n_pair)]
        vals += [ref[...].astype(F32) for ref in single_refs]
        inner = pl.ds(pl.multiple_of(pl.program_id(0) * tm + SWA_BLOCK, SWA_BLOCK), tm)
        vals += [ref[inner, :] for ref in halo_refs]
        ds = [_mx(_squeeze_heads(val) if heads else val) for (_, _, heads), val in zip(groups, vals)]
        du = jnp.zeros((tm, D_MODEL), F32)
        for (first, rows, _), d in zip(groups, ds):
            du = du + _dot(d, w_r[first:first + rows, :])
        for (first, rows, _), d in zip(groups, ds):
            dw_r[first:first + rows, :] += _dot_tn(d, u)
        dx, dg = _rms_bwd(xv, r, g, du)
        dx_r[...] = dh1_r[...] + dx
        dg_r[...] += dg

    arrays = [a for _, pr in pairs for a in pr] + [a for _, a in singles]
    specs = [_row_spec(tm, a.shape[1]) for a in arrays] + [_vmem_spec()] * n_halo
    arrays += [a for _, a in halos]
    body, extra, extra_specs = _after(body, 4 + len(arrays), dep)
    return pl.pallas_call(
        body, name="in_bwd", grid=(L // tm,),
        in_specs=[_row_spec(tm, D_MODEL), _row_spec(tm, D_MODEL), _full_spec((1, D_MODEL)), _vmem_spec()] + specs
        + extra_specs,
        out_specs=[_row_spec(tm, D_MODEL), _full_spec((IN_COLS, D_MODEL)), _full_spec((1, D_MODEL))],
        out_shape=[jax.ShapeDtypeStruct((L, D_MODEL), F32), jax.ShapeDtypeStruct((IN_COLS, D_MODEL), F32),
                   jax.ShapeDtypeStruct((1, D_MODEL), F32)],
        compiler_params=_params(("arbitrary",), VMEM_BIG),
    )(x, dh1, g_pre, w_in_t, *arrays, *extra)


def _adamw_math(w, g, m, v):
    m = ADAM_B1 * m + (1.0 - ADAM_B1) * g
    v = ADAM_B2 * v + (1.0 - ADAM_B2) * (g * g)
    m_hat = m / (1.0 - ADAM_B1 ** ADAM_STEP)
    v_hat = v / (1.0 - ADAM_B2 ** ADAM_STEP)
    delta = -ADAM_LR * (m_hat / (jnp.sqrt(v_hat) + ADAM_EPS) + ADAM_WD * w)
    return delta, m, v


def _adamw_call(w, g, m, v, name, dep=None):
    rows, cols = w.shape
    tr = min(256, rows)

    def body(w_r, g_r, m_r, v_r, d_r, nm_r, nv_r):
        d_r[...], nm_r[...], nv_r[...] = _adamw_math(w_r[...], g_r[...], m_r[...], v_r[...])

    if rows % tr == 0:
        spec, steps = _row_spec(tr, cols), rows // tr
    else:
        spec, steps = pl.BlockSpec((rows, 256), lambda i: (0, i)), cols // 256
    body, extra, extra_specs = _after(body, 4, dep)
    return pl.pallas_call(
        body, name=name, grid=(steps,),
        in_specs=[spec] * 4 + extra_specs, out_specs=[spec] * 3,
        out_shape=[jax.ShapeDtypeStruct(w.shape, F32)] * 3,
        compiler_params=_params(("arbitrary",)),
    )(w, g, m, v, *extra)


def _position():
    return lax.axis_index("x"), lax.axis_index("y"), lax.axis_index("c")


def _other_chips(x, y):
    return [(1 - x, y), (x, 1 - y), (1 - x, 1 - y)]


ROWS, COLS = -2, -1


def _half(ref, which, axis):
    size = ref.shape[axis] // 2
    span = pl.ds(pl.multiple_of(which * size, 16 if axis == ROWS else 128), size)
    index = [slice(None)] * len(ref.shape)
    index[axis] = span
    return ref.at[tuple(index)]


def _first_gather_call(shards, axes):
    n = len(shards)

    def body(*refs):
        srcs, outs = refs[:n], refs[n:2 * n]
        send_sems, recv_sems, local_sems = refs[2 * n:]
        x, y, c = _position()
        sibling = (x, y, 1 - c)
        chips = _other_chips(x, y)
        local = [pltpu.make_async_copy(srcs[a], outs[a].at[2 * x + y], local_sems.at[a]) for a in range(n)]
        for cp in local:
            cp.start()

        def copy(a, k, block, to, src=None):
            px, py, pc = block
            dst = _half(outs[a].at[2 * px + py], pc, axes[a])
            return pltpu.make_async_remote_copy(
                src_ref=dst if src is None else src, dst_ref=dst, send_sem=send_sems.at[6 * a + k],
                recv_sem=recv_sems.at[6 * a + k], device_id=to, device_id_type=MESH_ID)

        first, passed = [], []
        for a in range(n):
            my_half = _half(srcs[a], c, axes[a])
            first += [copy(a, j, (x, y, c), (*chip, c), src=my_half) for j, chip in enumerate(chips)]
        for cp in first:
            cp.start()
        for a in range(n):
            for j, chip in enumerate(chips):
                copy(a, j, (*chip, c), (x, y, c)).wait_recv()
                passed.append(copy(a, 3 + j, (*chip, c), sibling))
                passed[-1].start()
        for a in range(n):
            for j, chip in enumerate(chips):
                copy(a, 3 + j, (*chip, 1 - c), (x, y, c)).wait_recv()
        for cp in first + passed:
            cp.wait_send()
        for cp in local:
            cp.wait()

    return pl.pallas_call(
        body, name="first_gather",
        in_specs=[_any_spec()] * n, out_specs=[_any_spec()] * n,
        out_shape=[jax.ShapeDtypeStruct((N_CHIPS,) + s.shape, s.dtype) for s in shards],
        scratch_shapes=[pltpu.SemaphoreType.DMA((6 * n,)), pltpu.SemaphoreType.DMA((6 * n,)),
                        pltpu.SemaphoreType.DMA((n,))],
    )(*shards)


def _split_start(name, arrays, n_copies, plan):
    n = len(arrays)

    def body(*refs):
        ins, send_sems, recv_sems, token = refs[:n], refs[n], refs[n + 1], refs[-1]
        for k, (src, dst, to, _) in enumerate(plan(ins)):
            pltpu.make_async_remote_copy(src_ref=src, dst_ref=dst, send_sem=send_sems.at[k],
                                         recv_sem=recv_sems.at[k], device_id=to, device_id_type=MESH_ID).start()
        token[...] = jnp.zeros_like(token)

    hbm = pl.BlockSpec(memory_space=pltpu.HBM)
    sem = pl.BlockSpec(memory_space=pltpu.SEMAPHORE)
    out = pl.pallas_call(
        body, name=name,
        out_shape=(pltpu.SemaphoreType.DMA((n_copies,)), pltpu.SemaphoreType.DMA((n_copies,)))
        + tuple(pltpu.HBM(a.shape, a.dtype) for a in arrays) + (jax.ShapeDtypeStruct((8, 128), F32),),
        in_specs=[hbm] * n, out_specs=(sem, sem) + (hbm,) * n + (_vmem_spec(),),
        input_output_aliases={i: 2 + i for i in range(n)},
        compiler_params=pltpu.CompilerParams(has_side_effects=pltpu.SideEffectType.DATAFLOW_SIDE_EFFECTING),
    )(*[pltpu.with_memory_space_constraint(a, pltpu.HBM) for a in arrays])
    return (out[0], out[1], tuple(out[2:2 + n])), out[-1]


def _split_wait(name, handle, n_copies, plan, after):
    send_sems, recv_sems, arrays = handle
    n = len(arrays)

    def body(*refs):
        ins, s_sems, r_sems = refs[:n], refs[n], refs[n + 1]
        for k, (src, dst, to, landed) in enumerate(plan(ins)):
            cp = pltpu.make_async_remote_copy(src_ref=src, dst_ref=landed, send_sem=s_sems.at[k],
                                              recv_sem=r_sems.at[k], device_id=to, device_id_type=MESH_ID)
            cp.wait_send()
            cp.wait_recv()

    hbm = pl.BlockSpec(memory_space=pltpu.HBM)
    sem = pl.BlockSpec(memory_space=pltpu.SEMAPHORE)
    out = pl.pallas_call(
        body, name=name,
        out_shape=tuple(pltpu.HBM(a.shape, a.dtype) for a in arrays),
        in_specs=[hbm] * n + [sem, sem, _any_spec()], out_specs=(hbm,) * n,
        input_output_aliases={i: i for i in range(n)},
        compiler_params=pltpu.CompilerParams(has_side_effects=pltpu.SideEffectType.DATAFLOW_SIDE_EFFECTING),
    )(*arrays, send_sems, recv_sems, after)
    return tuple(out)


def _gather_plans(axes):
    n = len(axes)

    def stage_one(refs):
        x, y, c = _position()
        copies = []
        for a, axis in enumerate(axes):
            for px, py in _other_chips(x, y):
                copies.append((_half(refs[a], c, axis), _half(refs[n + a].at[2 * x + y], c, axis),
                               (px, py, c), _half(refs[n + a].at[2 * px + py], c, axis)))
        return copies

    def stage_two(refs):
        x, y, c = _position()
        copies = []
        for a, axis in enumerate(axes):
            for px, py in _other_chips(x, y):
                piece = _half(refs[n + a].at[2 * px + py], c, axis)
                copies.append((piece, piece, (x, y, 1 - c), _half(refs[n + a].at[2 * px + py], 1 - c, axis)))
        return copies

    return stage_one, stage_two


def _pair_swap_plan(axes):
    n = len(axes)

    def plan(refs):
        x, y, c = _position()
        return [(_half(refs[a], 1 - c, axes[a]), refs[n + a], (x, y, 1 - c), refs[n + a]) for a in range(n)]

    return plan


def _chip_swap_plan(n):
    def plan(refs):
        x, y, c = _position()
        copies = []
        for a in range(n):
            for j, (px, py) in enumerate(_other_chips(x, y)):
                copies.append((refs[a].at[2 * px + py], refs[n + a].at[j], (px, py, c), refs[n + a].at[j]))
        return copies

    return plan


def _pair_join_plan(axes):
    def plan(refs):
        x, y, c = _position()
        copies = []
        for a, axis in enumerate(axes):
            mine = _half(refs[a], c, axis)
            copies.append((mine, mine, (x, y, 1 - c), _half(refs[a], 1 - c, axis)))
        return copies

    return plan


def _pair_add_call(g, got, pos, name, axis):
    rows, cols = got.shape[1], got.shape[2]
    tr = min(512, rows) if axis == ROWS else rows
    nblk = rows // tr
    if axis == ROWS:
        mine = lambda j, i, p: (j, p[1] * nblk + i, 0)
    else:
        mine = lambda j, i, p: (j, 0, p[1])

    def body(pos_r, g_r, got_r, o_r):
        o_r[...] = (g_r[...] + got_r[...]).astype(o_r.dtype)

    return pl.pallas_call(
        body, name=name,
        grid_spec=pltpu.PrefetchScalarGridSpec(
            num_scalar_prefetch=1, grid=(N_CHIPS, nblk),
            in_specs=[pl.BlockSpec((None, tr, cols), mine),
                      pl.BlockSpec((None, tr, cols), lambda j, i, p: (j, i, 0))],
            out_specs=pl.BlockSpec((None, tr, cols), lambda j, i, p: (j, i, 0))),
        out_shape=jax.ShapeDtypeStruct(got.shape, COMM_DTYPE),
        compiler_params=_params(("arbitrary", "arbitrary"), VMEM_BIG),
    )(pos, g, got)


def _chip_add_call(hsum, got, pos, name, axis):
    rows, cols = hsum.shape[1], hsum.shape[2]
    tr = min(512, rows) if axis == ROWS else rows
    nblk = rows // tr
    if axis == ROWS:
        out_shape, mine = (2 * rows, cols), (lambda i, p: (p[1] * nblk + i, 0))
    else:
        out_shape, mine = (rows, 2 * cols), (lambda i, p: (0, p[1]))

    def body(pos_r, own_r, got_r, o_r):
        acc = own_r[...].astype(F32)
        for j in range(3):
            acc = acc + got_r[j].astype(F32)
        o_r[...] = acc

    return pl.pallas_call(
        body, name=name,
        grid_spec=pltpu.PrefetchScalarGridSpec(
            num_scalar_prefetch=1, grid=(nblk,),
            in_specs=[pl.BlockSpec((None, tr, cols), lambda i, p: (p[0], i, 0)),
                      pl.BlockSpec((3, tr, cols), lambda i, p: (0, i, 0))],
            out_specs=pl.BlockSpec((tr, cols), mine)),
        out_shape=jax.ShapeDtypeStruct(out_shape, F32),
        compiler_params=_params(("arbitrary",), VMEM_BIG),
    )(pos, hsum, got)


SMALL_NAMES = ("norm_mix_pre", "norm_mix_post", "norm_mlp_pre", "norm_mlp_post", "b_gate_fwd", "b_gate_bwd",
               "gla_norm", "swa_sink", "rel_bias")


def _small_update_call(grads, gate_grads, params, dep=None):
    n_dev = 8
    n_small = len(SMALL_NAMES)
    wmv = [t for p in params for t in p]
    shapes = [p[0].shape for p in params]

    def body(*refs):
        g_refs = refs[:n_small + 3]
        wmv_refs = refs[n_small + 3:n_small + 3 + 3 * n_small]
        n_in = n_small + 3 + 3 * n_small
        out_refs = refs[n_in:n_in + 4 * n_small + 3]
        pack_a, pack_b, all_a, all_b, send_sems, recv_sems = refs[n_in + 4 * n_small + 3:]
        x, y, c = _position()
        me = 4 * x + 2 * y + c
        pack_a[...] = jnp.zeros_like(pack_a)
        pack_b[...] = jnp.zeros_like(pack_b)
        for i in range(4):
            pack_a[i:i + 1, :] = g_refs[i][...]
        pack_a[4:5, 0:256] = g_refs[4][...]
        pack_a[5:6, 0:256] = g_refs[5][...]
        pack_a[6:7, 0:128] = g_refs[6][...]
        pack_a[7:8, 0:128] = g_refs[7][...]
        pack_a[7:8, 128:256] = g_refs[11][...]
        pack_b[0:32, 0:128] = g_refs[8][...]
        pack_b[32:48, :] = g_refs[9][...]
        pack_b[48:64, :] = g_refs[10][...]
        all_a[me] = pack_a[...]
        all_b[me] = pack_b[...]
        copies = []
        for k in range(1, n_dev):
            fx, fy, fc = (k >> 2) & 1, (k >> 1) & 1, k & 1
            to = (1 - x if fx else x, 1 - y if fy else y, 1 - c if fc else c)
            for t, (pack, dst) in enumerate(((pack_a, all_a), (pack_b, all_b))):
                copies.append(pltpu.make_async_remote_copy(
                    src_ref=pack, dst_ref=dst.at[me], send_sem=send_sems.at[2 * (k - 1) + t],
                    recv_sem=recv_sems.at[2 * (k - 1) + t], device_id=to, device_id_type=MESH_ID))
        for cp in copies:
            cp.start()
        for cp in copies:
            cp.wait()
        sum_a, sum_b = all_a[0], all_b[0]
        for d in range(1, n_dev):
            sum_a = sum_a + all_a[d]
            sum_b = sum_b + all_b[d]
        gsum = [sum_a[0:1], sum_a[1:2], sum_a[2:3], sum_a[3:4], sum_a[4:5, 0:256], sum_a[5:6, 0:256],
                sum_a[6:7, 0:128], sum_a[7:8, 0:SWA_Q_HEADS], sum_b[0:32, 0:SWA_Q_HEADS]]
        for i in range(n_small):
            w_r, m_r, v_r = wmv_refs[3 * i:3 * i + 3]
            delta, new_m, new_v = _adamw_math(w_r[...], gsum[i], m_r[...], v_r[...])
            out_refs[4 * i][...] = gsum[i]
            out_refs[4 * i + 1][...] = delta
            out_refs[4 * i + 2][...] = new_m
            out_refs[4 * i + 3][...] = new_v
        out_refs[4 * n_small][...] = sum_b[32:48]
        out_refs[4 * n_small + 1][...] = sum_b[48:64]
        out_refs[4 * n_small + 2][...] = sum_a[7:8, 128:256]

    n_in = n_small + 3 + 3 * n_small
    body, extra, extra_specs = _after(body, n_in, dep)
    out_shape = [jax.ShapeDtypeStruct(s, F32) for s in shapes for _ in range(4)]
    out_shape += [jax.ShapeDtypeStruct((GLA_GATE_RANK, 256), F32)] * 2 + [jax.ShapeDtypeStruct((1, 128), F32)]
    out = pl.pallas_call(
        body, name="small_update",
        in_specs=[_vmem_spec()] * n_in + extra_specs, out_specs=[_vmem_spec()] * len(out_shape),
        out_shape=out_shape,
        scratch_shapes=[pltpu.VMEM((8, D_MODEL), F32), pltpu.VMEM((64, 256), F32),
                        pltpu.VMEM((n_dev, 8, D_MODEL), F32), pltpu.VMEM((n_dev, 64, 256), F32),
                        pltpu.SemaphoreType.DMA((2 * (n_dev - 1),)), pltpu.SemaphoreType.DMA((2 * (n_dev - 1),))],
    )(*grads, *gate_grads, *wmv, *extra)
    per_name = [tuple(out[4 * i:4 * i + 4]) for i in range(n_small)]
    return per_name, out[4 * n_small], out[4 * n_small + 1], out[4 * n_small + 2]


def _pad_heads(t, n_heads, axis=-1):
    axis = axis % t.ndim
    shape = t.shape
    t = t.reshape(shape[:axis] + (n_heads, 64) + shape[axis + 1:])
    pad = [(0, 0)] * t.ndim
    pad[axis + 1] = (0, HEAD_PAD - 64)
    return jnp.pad(t, pad).reshape(shape[:axis] + (n_heads * HEAD_PAD,) + shape[axis + 1:])


def _unpad_heads(t, n_heads, axis=-1):
    axis = axis % t.ndim
    shape = t.shape
    t = t.reshape(shape[:axis] + (n_heads, HEAD_PAD) + shape[axis + 1:])
    t = lax.slice_in_dim(t, 0, 64, axis=axis + 1)
    return t.reshape(shape[:axis] + (n_heads * 64,) + shape[axis + 1:])


def _pad_gate(w, first_row):
    return jnp.pad(_pad_heads(w, 4), ((first_row, 128 - GLA_GATE_RANK - first_row), (0, 0)))


def _own_slot(shard, chip):
    zone = lax.empty((N_CHIPS,) + shard.shape, shard.dtype)
    return lax.dynamic_update_slice(zone, shard[None], (chip,) + (0,) * shard.ndim)


def _reduce_to_owners(grads, axes, pos, tag, overlap):
    n = len(grads)

    def half_shape(g, axis):
        return (N_CHIPS, g.shape[1] // 2, g.shape[2]) if axis == ROWS else (N_CHIPS, g.shape[1], g.shape[2] // 2)

    lands = [lax.empty(half_shape(g, axis), F32) for g, axis in zip(grads, axes)]
    handle, token = _split_start(tag + "_pair_start", list(grads) + lands, n, _pair_swap_plan(axes))
    got = _split_wait(tag + "_pair_wait", handle, n, _pair_swap_plan(axes), overlap[0](token))
    sums = [_pair_add_call(got[a], got[n + a], pos, f"{tag}_pair_add{a}", axes[a]) for a in range(n)]
    lands = [lax.empty((3,) + s.shape[1:], s.dtype) for s in sums]
    handle, token = _split_start(tag + "_chip_start", sums + lands, 3 * n, _chip_swap_plan(n))
    got = _split_wait(tag + "_chip_wait", handle, 3 * n, _chip_swap_plan(n), overlap[1](token))
    halves = [_chip_add_call(got[a], got[n + a], pos, f"{tag}_chip_add{a}", axes[a]) for a in range(n)]
    handle, token = _split_start(tag + "_join_start", halves, n, _pair_join_plan(axes))
    return _split_wait(tag + "_join_wait", handle, n, _pair_join_plan(axes), overlap[2](token))


def kernel(x, norm_mix_pre, w_in, w_gate_up_fwd, b_gate_fwd, w_gate_up_bwd, b_gate_bwd, gla_norm, swa_sink, rel_bias, w_out, norm_mix_post, norm_mlp_pre, w_up, w_down, norm_mlp_post, loss_target, m_norm_mix_pre, m_w_in, m_w_gate_up_fwd, m_b_gate_fwd, m_w_gate_up_bwd, m_b_gate_bwd, m_gla_norm, m_swa_sink, m_rel_bias, m_w_out, m_norm_mix_post, m_norm_mlp_pre, m_w_up, m_w_down, m_norm_mlp_post, v_norm_mix_pre, v_w_in, v_w_gate_up_fwd, v_b_gate_fwd, v_w_gate_up_bwd, v_b_gate_bwd, v_gla_norm, v_swa_sink, v_rel_bias, v_w_out, v_norm_mix_post, v_norm_mlp_pre, v_w_up, v_w_down, v_norm_mlp_post):
    given = dict(locals())
    cx, cy, cc = _position()
    chip = (2 * cx + cy).astype(jnp.int32)
    pos = jnp.stack([chip, cc.astype(jnp.int32)])
    seq, tgt = x[0], loss_target[0]
    L = seq.shape[0]

    gates = jnp.concatenate([w_gate_up_fwd[0], w_gate_up_bwd[0]], axis=0).astype(COMM_DTYPE)
    all_in, all_gates = _first_gather_call([w_in[0].T.astype(COMM_DTYPE), gates], [COLS, ROWS])
    rest = [w_out[0].astype(COMM_DTYPE), jnp.stack([w_up[0], w_down[0]]).astype(COMM_DTYPE)]
    stage_one, stage_two = _gather_plans([ROWS, ROWS])
    handle, token = _split_start("gather_chip_start", rest + [_own_slot(s, chip) for s in rest] + [all_gates], 6,
                                 stage_one)

    w_in_t = _mx(all_in.reshape(IN_COLS, D_MODEL))
    gates_full = jnp.concatenate([all_gates[j] for j in range(N_CHIPS)], axis=1)
    wgf_p = _mx(_pad_gate(gates_full[:GLA_GATE_RANK], 0))
    wgb_p = _mx(_pad_gate(gates_full[GLA_GATE_RANK:], GLA_GATE_RANK))
    bf_p, bb_p = _pad_heads(b_gate_fwd, 4), _pad_heads(b_gate_bwd, 4)
    buckets = jnp.asarray(_band_buckets())
    bias = _bias_call(rel_bias, buckets)
    sink1 = swa_sink.reshape(SWA_Q_HEADS)

    qa, ka, va, ga, qs, ks, vs, za = _proj_call(seq, norm_mix_pre, w_in_t, dep=token)
    halo = ((SWA_BLOCK, SWA_BLOCK), (0, 0))
    ks_p, vs_p = jnp.pad(ks, halo), jnp.pad(vs, halo)
    o_f, o_b, s_f, s_b = _gla_fwd_call(qa, ka, va, za, wgf_p, bf_p, wgb_p, bb_p)
    arrays = _split_wait("gather_chip_wait", handle, 6, stage_one, o_f)
    handle, token = _split_start("gather_pair_start", list(arrays), 6, stage_two)
    o_s = _swa_fwd_call(qs, ks_p, vs_p, bias, sink1, dep=token)
    arrays = _split_wait("gather_pair_wait", handle, 6, stage_two, o_s)
    w_out_full = _mx(arrays[2].reshape(N_CHIPS * R_OUT, D_MODEL))
    w_ud = _mx(arrays[3])
    cat, mix, h1, n2 = _mix_call(o_f, o_b, ga, o_s, seq, gla_norm, w_out_full, norm_mix_post, norm_mlp_pre)
    a, rz, dh2, dff, loss, d_post2 = _mlp_fwd_call(n2, h1, tgt, w_ud, norm_mlp_post)

    dz, dn2 = _mlp_bwd_call(dff, rz, w_ud)
    dw_down, dw_up4 = _mlp_wgrad_call(a, dff, n2, dz)
    dh1, do, dga, dos, dw_out, d_pre2, d_post, d_gn = _mix_bwd_call(
        dn2, dh2, h1, mix, cat, o_f, o_b, ga, gla_norm, norm_mix_post, norm_mlp_pre, w_out_full)
    done = {}

    def swa_backward(tok):
        done["swa"] = _swa_bwd_call(qs, ks_p, vs_p, bias, sink1, dos, dep=tok)
        return done["swa"][0]

    def gla_in_backward(tok):
        done["gla"] = _gla_bwd_call(qa, ka, va, za, do, s_f, s_b, wgf_p, bf_p, wgb_p, bb_p, dep=tok)
        dqf, dkf, dvf, dzf, _, _, dqb, dkb, dvb, dzb, _, _ = done["gla"]
        dqs, dks_p, dvs_p, _, _ = done["swa"]
        done["in"] = _in_bwd_call(
            seq, dh1, norm_mix_pre, w_in_t,
            pairs=[(T_QA, (dqf, dqb)), (T_KA, (dkf, dkb)), (T_VA, (dvf, dvb)), (T_ZA, (dzf, dzb))],
            singles=[(T_GA, dga), (T_QS, dqs)], halos=[(T_KS, dks_p), (T_VS, dvs_p)])
        return done["in"][0]

    def bias_backward(tok):
        done["rel"] = _relbias_call(done["swa"][3], done["swa"][4], buckets, dep=tok)
        return done["rel"][0]

    g_up, g_down, g_out = _reduce_to_owners(
        [dw_up4, dw_down.reshape(N_CHIPS, R_DOWN, D_MODEL), dw_out.reshape(N_CHIPS, R_OUT, D_MODEL)],
        [ROWS, ROWS, ROWS], pos, "mlp", [swa_backward, gla_in_backward, bias_backward])
    dx, dw_in_t, d_pre = done["in"]
    dwf, dbf, dwb, dbb = done["gla"][4], done["gla"][5], done["gla"][10], done["gla"][11]
    drel, dsink = done["rel"]

    small_grads = [d_pre, d_post, d_pre2, d_post2, _unpad_heads(dbf, 4), _unpad_heads(dbb, 4), d_gn, dsink, drel]
    gate_grads = [_unpad_heads(dwf[:GLA_GATE_RANK], 4), _unpad_heads(dwb[GLA_GATE_RANK:2 * GLA_GATE_RANK], 4)]
    small_params = [(given[n], given["m_" + n], given["v_" + n]) for n in SMALL_NAMES]
    upd = {}

    def update_up(tok):
        upd["w_up"] = (g_up,) + tuple(_adamw_call(w_up[0], g_up, m_w_up[0], v_w_up[0], "adamw_w_up", dep=tok))
        return upd["w_up"][1]

    def update_small(tok):
        per_name, gf_sum, gb_sum, upd["loss"] = _small_update_call(small_grads, gate_grads + [loss], small_params,
                                                                   dep=tok)
        upd.update(dict(zip(SMALL_NAMES, per_name)))
        for name, total in (("w_gate_up_fwd", gf_sum), ("w_gate_up_bwd", gb_sum)):
            g = lax.dynamic_slice(total, (0, chip * 64), (GLA_GATE_RANK, 64))
            upd[name] = (g,) + tuple(_adamw_call(given[name][0], g, given["m_" + name][0], given["v_" + name][0],
                                                 "adamw_" + name))
        upd["w_down"] = (g_down,) + tuple(
            _adamw_call(w_down[0], g_down, m_w_down[0], v_w_down[0], "adamw_w_down", dep=gf_sum))
        return upd["w_down"][1]

    def update_out(tok):
        upd["w_out"] = (g_out,) + tuple(_adamw_call(w_out[0], g_out, m_w_out[0], v_w_out[0], "adamw_w_out", dep=tok))
        return upd["w_out"][1]

    (g_in_t,) = _reduce_to_owners([dw_in_t.reshape(N_CHIPS, R_IN, D_MODEL)], [COLS], pos, "in",
                                  [update_up, update_small, update_out])
    in_t = (g_in_t,) + tuple(_adamw_call(w_in[0].T, g_in_t, m_w_in[0].T, v_w_in[0].T, "adamw_w_in"))
    upd["w_in"] = tuple(t.T for t in in_t)

    big = ("w_in", "w_gate_up_fwd", "w_gate_up_bwd", "w_out", "w_up", "w_down")
    names = ["norm_mix_pre", "w_in", "w_gate_up_fwd", "b_gate_fwd", "w_gate_up_bwd", "b_gate_bwd", "gla_norm",
             "swa_sink", "rel_bias", "w_out", "norm_mix_post", "norm_mlp_pre", "w_up", "w_down", "norm_mlp_post"]
    outs = [upd["loss"][0, 0], dx[None]]
    for kind in range(4):
        outs += [upd[n][kind][None] if n in big else upd[n][kind] for n in names]
    return tuple(outs)
```

```python
import math

import numpy as np
import jax
import jax.numpy as jnp
from jax import lax
from jax.experimental import pallas as pl
from jax.experimental.pallas import tpu as pltpu

F32 = jnp.float32
MXU_DTYPE = jnp.bfloat16
COMM_DTYPE = jnp.bfloat16

D_MODEL = 1024
D_FF = 4096
N_CHIPS = 4
GLA_HEADS = 4
GLA_CHUNK = 64
GLA_GATE_RANK = 16
GLA_GATE_NORM = 16.0
SWA_Q_HEADS = 8
SWA_KV_HEADS = 2
SWA_BLOCK = 128
REL_BUCKETS = 32
REL_MAX_DIST = 128
NORM_EPS = 1e-6
HEAD_PAD = 128

ADAM_LR = 0.001
ADAM_B1 = 0.9
ADAM_B2 = 0.999
ADAM_EPS = 1e-08
ADAM_WD = 0.01
ADAM_STEP = 10

OUT_PAD = 1024

R_IN, R_OUT, R_UP, R_DOWN = 584, 256, 1024, 1024

VMEM_BIG = 56 * 1024 * 1024
MESH_AXES = ("x", "y", "c")
MESH_ID = pl.DeviceIdType.MESH


def _mx(a):
    return a.astype(MXU_DTYPE)


def _dot(a, b):
    return jnp.dot(a, b, preferred_element_type=F32)


def _dot_nt(a, b):
    return lax.dot_general(a, b, (((1,), (1,)), ((), ())), preferred_element_type=F32)


def _dot_tn(a, b):
    return lax.dot_general(a, b, (((0,), (0,)), ((), ())), preferred_element_type=F32)


def _rms_r(x):
    return lax.rsqrt(jnp.mean(x * x, axis=-1, keepdims=True) + NORM_EPS)


def _rms_bwd(x, r, g, dy):
    xh = x * r
    gdy = dy * g
    dx = r * (gdy - xh * jnp.mean(gdy * xh, axis=-1, keepdims=True))
    return dx, jnp.sum(dy * xh, axis=0, keepdims=True)


def _low_half(rows):
    return lax.broadcasted_iota(jnp.int32, (rows, HEAD_PAD), 1) < 64


def _spread_heads(x):
    low = _low_half(x.shape[0])
    parts = []
    for p in range(x.shape[1] // HEAD_PAD):
        pair = x[:, HEAD_PAD * p:HEAD_PAD * (p + 1)]
        parts += [jnp.where(low, pair, 0.0), jnp.where(low, pltpu.roll(pair, 64, 1), 0.0)]
    return jnp.concatenate(parts, axis=1)


def _squeeze_heads(x):
    low = _low_half(x.shape[0])
    parts = []
    for p in range(x.shape[1] // (2 * HEAD_PAD)):
        even = x[:, 2 * HEAD_PAD * p:2 * HEAD_PAD * p + HEAD_PAD]
        odd = x[:, 2 * HEAD_PAD * p + HEAD_PAD:2 * HEAD_PAD * (p + 1)]
        parts.append(jnp.where(low, even, pltpu.roll(odd, 64, 1)))
    return parts[0] if len(parts) == 1 else jnp.concatenate(parts, axis=1)


def _params(sem=None, vmem=None):
    kw = {}
    if sem is not None:
        kw["dimension_semantics"] = sem
    if vmem is not None:
        kw["vmem_limit_bytes"] = vmem
    return pltpu.CompilerParams(**kw)


def _vmem_spec():
    return pl.BlockSpec(memory_space=pltpu.VMEM)


def _row_spec(tm, width):
    return pl.BlockSpec((tm, width), lambda i: (i, 0))


def _full_spec(shape):
    return pl.BlockSpec(shape, lambda i: (0,) * len(shape))


def _any_spec():
    return pl.BlockSpec(memory_space=pl.ANY)


def _after(body, n_in, dep):
    if dep is None:
        return body, [], []
    return (lambda *refs: body(*refs[:n_in], *refs[n_in + 1:])), [dep], [_any_spec()]


T_QA, T_KA, T_VA, T_GA = (0, 256, 4), (256, 256, 4), (512, 512, 0), (1024, 512, 0)
T_QS, T_KS, T_VS = (1568, 512, 8), (2080, 128, 2), (2208, 128, 2)
T_ZA = (1536, 128, 0)
ZA_COLS = 2 * GLA_GATE_RANK
IN_COLS = 2336


def _proj_call(x, g_pre, w_in_t, dep=None):
    L = x.shape[0]
    tm = min(256, L)
    groups = [(T_QA, F32), (T_KA, F32), (T_VA, MXU_DTYPE), (T_GA, F32),
              (T_QS, MXU_DTYPE), (T_KS, MXU_DTYPE), (T_VS, MXU_DTYPE), (T_ZA, F32)]
    widths = [rows * (2 if heads else 1) for (_, rows, heads), _ in groups]

    def body(x_ref, g_ref, w_ref, *outs):
        xv = x_ref[...]
        u = _mx(xv * _rms_r(xv) * g_ref[...])
        for ref, (grp, _) in zip(outs, groups):
            first, rows, heads = grp
            val = _dot_nt(u, w_ref[first:first + rows, :])
            if heads:
                val = _spread_heads(val)
            if grp is T_ZA:
                val = jnp.where(lax.broadcasted_iota(jnp.int32, val.shape, 1) < ZA_COLS, val, 0.0)
            if grp is T_QS:
                val = val * 0.125
            ref[...] = val.astype(ref.dtype)

    body, extra, extra_specs = _after(body, 3, dep)
    return pl.pallas_call(
        body, name="proj_fwd", grid=(L // tm,),
        in_specs=[_row_spec(tm, D_MODEL), _full_spec((1, D_MODEL)), _vmem_spec()] + extra_specs,
        out_specs=[_row_spec(tm, w) for w in widths],
        out_shape=[jax.ShapeDtypeStruct((L, w), dt) for w, (_, dt) in zip(widths, groups)],
        compiler_params=_params(("arbitrary",), VMEM_BIG),
    )(x, g_pre, w_in_t, *extra)


def _tri_masks():
    row = lax.broadcasted_iota(jnp.int32, (GLA_CHUNK, GLA_CHUNK), 0)
    col = lax.broadcasted_iota(jnp.int32, (GLA_CHUNK, GLA_CHUNK), 1)
    return row >= col, row <= col


def _chunk_sums(tri_m, x):
    hi = _mx(x)
    rest = x - hi.astype(F32)
    mid = _mx(rest)
    lo = _mx(rest - mid.astype(F32))
    return _dot(tri_m, hi) + _dot(tri_m, mid) + _dot(tri_m, lo)


def _gla_block_pre(q_r, k_r, z_r, w_r, b_r, rev, nc, qd_s, ki_s, ks_s, dec_s, keep=None):
    tri_f, tri_b = _tri_masks()
    tri_m = _mx((tri_b if rev else tri_f).astype(F32))
    g = _dot(_mx(z_r[...]), w_r[...]) + b_r[...]
    la = (jnp.minimum(g, 0.0) - jnp.log(1.0 + jnp.exp(-jnp.abs(g)))) / GLA_GATE_NORM
    sums, lasts = [], []
    for c in range(nc):
        b_c = _chunk_sums(tri_m, la[GLA_CHUNK * c:GLA_CHUNK * (c + 1)])
        blast = b_c[0:1] if rev else b_c[GLA_CHUNK - 1:GLA_CHUNK]
        dec_s[c] = jnp.exp(blast)
        sums.append(b_c)
        lasts.append(jnp.broadcast_to(blast, b_c.shape))
    b = jnp.concatenate(sums, axis=0)
    eb = jnp.exp(b)
    enb = jnp.exp(-b)
    elb = jnp.exp(jnp.concatenate(lasts, axis=0) - b)
    k = k_r[...]
    qd_s[...] = (q_r[...] * 0.125 * eb).astype(qd_s.dtype)
    ki_s[...] = (k * enb).astype(ki_s.dtype)
    ks_s[...] = (k * elb).astype(ks_s.dtype)
    if keep is not None:
        for ref, val in zip(keep, (g, eb, enb, elb)):
            ref[...] = val


def _gla_fwd_call(qa, ka, va, za, wgf, bgf, wgb, bgb):
    L = qa.shape[0]
    br = min(512, L)
    nb, nc, n_chunks = L // br, br // GLA_CHUNK, L // GLA_CHUNK
    hw = GLA_HEADS * HEAD_PAD

    def body(qaf, kaf, vaf, zaf, qab, kab, vab, zab, wgf_r, bgf_r, wgb_r, bgb_r,
             of_r, ob_r, sf_r, sb_r, st_f, st_b, pre_f, pre_b):
        @pl.when(pl.program_id(0) == 0)
        def _():
            st_f[...] = jnp.zeros_like(st_f)
            st_b[...] = jnp.zeros_like(st_b)

        _gla_block_pre(qaf, kaf, zaf, wgf_r, bgf_r, False, nc, *pre_f)
        _gla_block_pre(qab, kab, zab, wgb_r, bgb_r, True, nc, *pre_b)
        tri_f, tri_b = _tri_masks()

        def one(tri, pre, v_r, o_r, s_r, st, ci):
            qd_s, ki_s, ks_s, dec_s = pre
            rows = pl.ds(pl.multiple_of(ci * GLA_CHUNK, GLA_CHUNK), GLA_CHUNK)
            dec = dec_s[ci]
            heads = range(GLA_HEADS)
            lanes = [slice(HEAD_PAD * h, HEAD_PAD * (h + 1)) for h in heads]
            qd = [qd_s[rows, sl] for sl in lanes]
            v = [v_r[rows, sl] for sl in lanes]
            s_t = [st[h] for h in heads]
            a = [_dot_nt(qd[h], ki_s[rows, lanes[h]]) for h in heads]
            carried = [_dot_nt(qd[h], _mx(s_t[h])) for h in heads]
            grown = [_dot_tn(v[h], ks_s[rows, lanes[h]]) for h in heads]
            a = [_mx(jnp.where(tri, a[h], 0.0)) for h in heads]
            inner = [_dot(a[h], v[h]) for h in heads]
            for h in heads:
                s_r[ci, h] = s_t[h]
                o_r[rows, lanes[h]] = inner[h] + carried[h]
                st[h] = s_t[h] * dec[:, lanes[h]] + grown[h]

        def loop(t, carry):
            one(tri_f, pre_f, vaf, of_r, sf_r, st_f, t)
            one(tri_b, pre_b, vab, ob_r, sb_r, st_b, nc - 1 - t)
            return carry

        lax.fori_loop(0, nc, loop, 0, unroll=True)

    fwd = lambda i: (i, 0)
    bwd = lambda i: (nb - 1 - i, 0)
    ins = lambda m: [pl.BlockSpec((br, hw), m), pl.BlockSpec((br, hw), m),
                     pl.BlockSpec((br, hw), m), pl.BlockSpec((br, 128), m)]
    wspecs = [_full_spec((128, hw)), _full_spec((1, hw))] * 2
    s_shape = (nc, GLA_HEADS, HEAD_PAD, HEAD_PAD)
    pre_scratch = [pltpu.VMEM((br, hw), MXU_DTYPE)] * 3 + [pltpu.VMEM((nc, 1, hw), F32)]
    return pl.pallas_call(
        body, name="gla_fwd", grid=(nb,),
        in_specs=ins(fwd) + ins(bwd) + wspecs,
        out_specs=[pl.BlockSpec((br, hw), fwd), pl.BlockSpec((br, hw), bwd),
                   pl.BlockSpec(s_shape, lambda i: (i, 0, 0, 0)),
                   pl.BlockSpec(s_shape, lambda i: (nb - 1 - i, 0, 0, 0))],
        out_shape=[jax.ShapeDtypeStruct((L, hw), F32), jax.ShapeDtypeStruct((L, hw), F32),
                   jax.ShapeDtypeStruct((n_chunks,) + s_shape[1:], F32),
                   jax.ShapeDtypeStruct((n_chunks,) + s_shape[1:], F32)],
        scratch_shapes=[pltpu.VMEM(s_shape[1:], F32), pltpu.VMEM(s_shape[1:], F32), pre_scratch, pre_scratch],
        compiler_params=_params(("arbitrary",), VMEM_BIG),
    )(qa, ka, va, za, qa, ka, va, za, wgf, bgf, wgb, bgb)


def _gla_bwd_call(qa, ka, va, za, do, sf, sb, wgf, bgf, wgb, bgb, dep=None):
    L = qa.shape[0]
    br = min(256, L)
    nb, nc = L // br, br // GLA_CHUNK
    hw = GLA_HEADS * HEAD_PAD

    def body(qaf, kaf, vaf, zaf, dof, sf_r, qab, kab, vab, zab, dob, sb_r, wgf_r, bgf_r, wgb_r, bgb_r,
             dqf, dkf, dvf, dzf, dwf, dbf, dqb, dkb, dvb, dzb, dwb, dbb, gt_f, gt_b, pre_f, pre_b):
        @pl.when(pl.program_id(0) == 0)
        def _():
            for ref in (gt_f, gt_b, dwf, dbf, dwb, dbb):
                ref[...] = jnp.zeros_like(ref)

        _gla_block_pre(qaf, kaf, zaf, wgf_r, bgf_r, False, nc, *pre_f[:4], keep=pre_f[4:8])
        _gla_block_pre(qab, kab, zab, wgb_r, bgb_r, True, nc, *pre_b[:4], keep=pre_b[4:8])
        tri_f, tri_b = _tri_masks()
        row_w = lax.broadcasted_iota(jnp.int32, (GLA_CHUNK, HEAD_PAD), 0)

        def one(rev, pre, q_r, k_r, v_r, do_r, s_r, dq_r, dk_r, dv_r, gt, ci):
            qd_s, ki_s, ks_s, dec_s, _, eb_s, enb_s, elb_s, db_s = pre
            tri = tri_b if rev else tri_f
            last_row = 0 if rev else GLA_CHUNK - 1
            rows = pl.ds(pl.multiple_of(ci * GLA_CHUNK, GLA_CHUNK), GLA_CHUNK)
            dec = dec_s[ci]
            heads = range(GLA_HEADS)
            lanes = [slice(HEAD_PAD * h, HEAD_PAD * (h + 1)) for h in heads]
            qd = [qd_s[rows, sl] for sl in lanes]
            ki = [ki_s[rows, sl] for sl in lanes]
            ks = [ks_s[rows, sl] for sl in lanes]
            v = [v_r[rows, sl] for sl in lanes]
            do_h = [_mx(do_r[rows, sl]) for sl in lanes]
            s_t = [s_r[ci, h] for h in heads]
            g_t = [gt[h] for h in heads]
            g_m = [_mx(g_t[h]) for h in heads]
            a = [_dot_nt(qd[h], ki[h]) for h in heads]
            da = [_dot_nt(do_h[h], v[h]) for h in heads]
            dv_carried = [_dot_nt(ks[h], g_m[h]) for h in heads]
            dqd_carried = [_dot(do_h[h], _mx(s_t[h])) for h in heads]
            dks = [_dot(v[h], g_m[h]) for h in heads]
            g_grown = [_dot_tn(do_h[h], qd[h]) for h in heads]
            a = [_mx(jnp.where(tri, a[h], 0.0)) for h in heads]
            da = [_mx(jnp.where(tri, da[h], 0.0)) for h in heads]
            dv_inner = [_dot_tn(a[h], do_h[h]) for h in heads]
            dqd_inner = [_dot(da[h], ki[h]) for h in heads]
            dki = [_dot_tn(da[h], qd[h]) for h in heads]
            for h in heads:
                sl = lanes[h]
                dv_r[rows, sl] = dv_inner[h] + dv_carried[h]
                ddec = jnp.sum(g_t[h] * s_t[h], axis=0, keepdims=True)
                gt[h] = g_t[h] * dec[:, sl] + g_grown[h]
                dq = (dqd_inner[h] + dqd_carried[h]) * eb_s[rows, sl] * 0.125
                dk_state = dks[h] * elb_s[rows, sl]
                dk = dki[h] * enb_s[rows, sl] + dk_state
                dq_r[rows, sl] = dq
                dk_r[rows, sl] = dk
                k = k_r[rows, sl]
                dblast = jnp.sum(dk_state * k, axis=0, keepdims=True) + dec[:, sl] * ddec
                db_s[rows, sl] = q_r[rows, sl] * dq - k * dk + jnp.where(row_w == last_row, dblast, 0.0)

        def loop(t, carry):
            one(False, pre_f, qaf, kaf, vaf, dof, sf_r, dqf, dkf, dvf, gt_f, nc - 1 - t)
            one(True, pre_b, qab, kab, vab, dob, sb_r, dqb, dkb, dvb, gt_b, t)
            return carry

        lax.fori_loop(0, nc, loop, 0, unroll=True)

        def gate_grads(rev, pre, z_r, w_r, dz_r, dw_r, dbias_r):
            g_s, db_s = pre[4], pre[8]
            back_m = _mx((tri_f if rev else tri_b).astype(F32))
            db = db_s[...]
            dla = jnp.concatenate([_chunk_sums(back_m, db[GLA_CHUNK * c:GLA_CHUNK * (c + 1)]) for c in range(nc)],
                                  axis=0)
            dg = dla * (1.0 / GLA_GATE_NORM) * (1.0 / (1.0 + jnp.exp(g_s[...])))
            dg_m = _mx(dg)
            dz_r[...] = _dot_nt(dg_m, w_r[...])
            dw_r[...] += _dot_tn(_mx(z_r[...]), dg_m)
            dbias_r[...] += jnp.sum(dg, axis=0, keepdims=True)

        gate_grads(False, pre_f, zaf, wgf_r, dzf, dwf, dbf)
        gate_grads(True, pre_b, zab, wgb_r, dzb, dwb, dbb)

    last_first = lambda i: (nb - 1 - i, 0)
    first_last = lambda i: (i, 0)
    s_shape = (nc, GLA_HEADS, HEAD_PAD, HEAD_PAD)

    def ins(m):
        return [pl.BlockSpec((br, hw), m), pl.BlockSpec((br, hw), m), pl.BlockSpec((br, hw), m),
                pl.BlockSpec((br, 128), m), pl.BlockSpec((br, hw), m),
                pl.BlockSpec(s_shape, lambda i: m(i) + (0, 0))]

    def outs(m):
        return [pl.BlockSpec((br, hw), m), pl.BlockSpec((br, hw), m), pl.BlockSpec((br, hw), m),
                pl.BlockSpec((br, 128), m), _full_spec((128, hw)), _full_spec((1, hw))]

    out_shape = [jax.ShapeDtypeStruct((L, hw), F32)] * 3 + [
        jax.ShapeDtypeStruct((L, 128), F32), jax.ShapeDtypeStruct((128, hw), F32),
        jax.ShapeDtypeStruct((1, hw), F32)]
    wspecs = [_full_spec((128, hw)), _full_spec((1, hw))] * 2
    body, extra, extra_specs = _after(body, 16, dep)
    pre_scratch = ([pltpu.VMEM((br, hw), MXU_DTYPE)] * 3 + [pltpu.VMEM((nc, 1, hw), F32)]
                   + [pltpu.VMEM((br, hw), F32)] * 5)
    return pl.pallas_call(
        body, name="gla_bwd", grid=(nb,),
        in_specs=ins(last_first) + ins(first_last) + wspecs + extra_specs,
        out_specs=outs(last_first) + outs(first_last),
        out_shape=out_shape + out_shape,
        scratch_shapes=[pltpu.VMEM(s_shape[1:], F32), pltpu.VMEM(s_shape[1:], F32), pre_scratch, pre_scratch],
        compiler_params=_params(("arbitrary",), VMEM_BIG),
    )(qa, ka, va, za, do, sf, qa, ka, va, za, do, sb, wgf, bgf, wgb, bgb, *extra)


def _t5_buckets(rel):
    nb = REL_BUCKETS // 2
    ret = (rel > 0).astype(np.int32) * nb
    n = np.abs(rel)
    max_exact = nb // 2
    large = max_exact + (np.log(np.maximum(n, 1).astype(np.float32) / max_exact)
                         / math.log(REL_MAX_DIST / max_exact) * (nb - max_exact)).astype(np.int32)
    large = np.minimum(large, nb - 1)
    return ret + np.where(n < max_exact, n, large)


SWA_GROUP = SWA_Q_HEADS // SWA_KV_HEADS
SWA_SPAN = 3 * SWA_BLOCK
SWA_GROUP_LANES = SWA_GROUP * SWA_BLOCK


def _band_buckets():
    s = np.arange(SWA_SPAN)[:, None]
    c = np.arange(SWA_BLOCK)[None, :]
    return _t5_buckets(s - SWA_BLOCK - c).astype(np.int32)


def _swa_valid(n, seq_len):
    key_pos = (n - 1) * SWA_BLOCK + lax.broadcasted_iota(jnp.int32, (SWA_SPAN, 1), 0)
    return (key_pos >= 0) & (key_pos < seq_len)


def _swa_sink_row(sink_r, kv):
    lane = lax.broadcasted_iota(jnp.int32, (1, SWA_GROUP_LANES), 1)
    row = jnp.full((1, SWA_GROUP_LANES), sink_r[kv * SWA_GROUP], F32)
    for g in range(1, SWA_GROUP):
        row = jnp.where(lane >= g * SWA_BLOCK, sink_r[kv * SWA_GROUP + g], row)
    return row


def _swa_group(ref, kv):
    first = kv * SWA_GROUP
    return jnp.concatenate([ref[:, HEAD_PAD * h:HEAD_PAD * (h + 1)] for h in range(first, first + SWA_GROUP)],
                           axis=0)


def _swa_softmax(scores, bias_t, sink_row, valid):
    st = scores + bias_t
    st = jnp.where(valid, st, -1e30)
    m = jnp.maximum(jnp.max(st, axis=0, keepdims=True), sink_row)
    p = jnp.exp(st - m)
    e_sink = jnp.exp(sink_row - m)
    inv = 1.0 / (jnp.sum(p, axis=0, keepdims=True) + e_sink)
    return p * inv, e_sink * inv


def _swa_fwd_call(qs, ks, vs, bias, sink, dep=None):
    L = qs.shape[0]

    def body(q_r, k_r, v_r, bias_r, sink_r, o_r):
        n = pl.program_id(0)
        span = pl.ds(pl.multiple_of(n * SWA_BLOCK, SWA_BLOCK), SWA_SPAN)
        valid = _swa_valid(n, L)
        groups = range(SWA_KV_HEADS)
        lanes = [slice(HEAD_PAD * kv, HEAD_PAD * (kv + 1)) for kv in groups]
        scores = [_dot_nt(k_r[span, lanes[kv]], _swa_group(q_r, kv)) for kv in groups]
        probs = [_swa_softmax(scores[kv], bias_r[kv], _swa_sink_row(sink_r, kv), valid)[0] for kv in groups]
        low = _low_half(SWA_BLOCK)
        for kv in groups:
            og = _dot_tn(_mx(probs[kv]), v_r[span, lanes[kv]])
            for pair in range(SWA_GROUP // 2):
                even = og[2 * SWA_BLOCK * pair:2 * SWA_BLOCK * pair + SWA_BLOCK]
                odd = og[2 * SWA_BLOCK * pair + SWA_BLOCK:2 * SWA_BLOCK * (pair + 1)]
                first = HEAD_PAD * (kv * SWA_GROUP // 2 + pair)
                o_r[:, first:first + HEAD_PAD] = jnp.where(low, even, pltpu.roll(odd, 64, 1)).astype(o_r.dtype)

    qw = SWA_Q_HEADS * HEAD_PAD
    body, extra, extra_specs = _after(body, 5, dep)
    return pl.pallas_call(
        body, name="swa_fwd", grid=(L // SWA_BLOCK,),
        in_specs=[_row_spec(SWA_BLOCK, qw), _vmem_spec(), _vmem_spec(), _vmem_spec(),
                  pl.BlockSpec(memory_space=pltpu.SMEM)] + extra_specs,
        out_specs=_row_spec(SWA_BLOCK, qw // 2),
        out_shape=jax.ShapeDtypeStruct((L, qw // 2), MXU_DTYPE),
        compiler_params=_params(("arbitrary",), VMEM_BIG),
    )(qs, ks, vs, bias, sink, *extra)


def _swa_bwd_call(qs, ks, vs, bias, sink, do, dep=None):
    L = qs.shape[0]
    qw = SWA_Q_HEADS * HEAD_PAD
    kw = SWA_KV_HEADS * HEAD_PAD

    def body(q_r, k_r, v_r, bias_r, sink_r, do_r, dq_r, dk_r, dv_r, dbias_r, dsink_r):
        n = pl.program_id(0)

        @pl.when(n == 0)
        def _():
            for ref in (dk_r, dv_r, dbias_r, dsink_r):
                ref[...] = jnp.zeros_like(ref)

        span = pl.ds(pl.multiple_of(n * SWA_BLOCK, SWA_BLOCK), SWA_SPAN)
        valid = _swa_valid(n, L)
        groups = range(SWA_KV_HEADS)
        lanes = [slice(HEAD_PAD * kv, HEAD_PAD * (kv + 1)) for kv in groups]
        kk = [k_r[span, sl] for sl in lanes]
        vv = [v_r[span, sl] for sl in lanes]
        qg = [_swa_group(q_r, kv) for kv in groups]
        dog = [_swa_group(do_r, kv) for kv in groups]
        scores = [_dot_nt(kk[kv], qg[kv]) for kv in groups]
        dp = [_dot_nt(vv[kv], dog[kv]) for kv in groups]
        probs = [_swa_softmax(scores[kv], bias_r[kv], _swa_sink_row(sink_r, kv), valid) for kv in groups]
        ds_m, pn_m = [], []
        for kv in groups:
            pn, p_sink = probs[kv]
            delta = jnp.sum(pn * dp[kv], axis=0, keepdims=True)
            ds = pn * (dp[kv] - delta)
            dsink_r[kv] -= p_sink * delta
            dbias_r[kv] += ds
            ds_m.append(_mx(ds))
            pn_m.append(_mx(pn))
        dqg = [_dot_tn(ds_m[kv], kk[kv]) * 0.125 for kv in groups]
        dkk = [_dot(ds_m[kv], qg[kv]) for kv in groups]
        dvv = [_dot(pn_m[kv], dog[kv]) for kv in groups]
        for kv in groups:
            for g in range(SWA_GROUP):
                h = kv * SWA_GROUP + g
                dq_r[:, HEAD_PAD * h:HEAD_PAD * (h + 1)] = dqg[kv][SWA_BLOCK * g:SWA_BLOCK * (g + 1)]
            dk_r[span, lanes[kv]] += dkk[kv]
            dv_r[span, lanes[kv]] += dvv[kv]

    body, extra, extra_specs = _after(body, 6, dep)
    return pl.pallas_call(
        body, name="swa_bwd", grid=(L // SWA_BLOCK,),
        in_specs=[_row_spec(SWA_BLOCK, qw), _vmem_spec(), _vmem_spec(), _vmem_spec(),
                  pl.BlockSpec(memory_space=pltpu.SMEM), _row_spec(SWA_BLOCK, qw)] + extra_specs,
        out_specs=[_row_spec(SWA_BLOCK, qw), _vmem_spec(), _vmem_spec(), _vmem_spec(), _vmem_spec()],
        out_shape=[jax.ShapeDtypeStruct((L, qw), F32),
                   jax.ShapeDtypeStruct((L + 2 * SWA_BLOCK, kw), F32),
                   jax.ShapeDtypeStruct((L + 2 * SWA_BLOCK, kw), F32),
                   jax.ShapeDtypeStruct((SWA_KV_HEADS, SWA_SPAN, SWA_GROUP_LANES), F32),
                   jax.ShapeDtypeStruct((SWA_KV_HEADS, 1, SWA_GROUP_LANES), F32)],
        compiler_params=_params(("arbitrary",), VMEM_BIG),
    )(qs, ks, vs, bias, sink, do, *extra)


def _bias_call(rel_bias, buckets):
    def body(t_r, bk_r, o_r):
        bk = bk_r[...]
        s = lax.broadcasted_iota(jnp.int32, bk.shape, 0)
        c = lax.broadcasted_iota(jnp.int32, bk.shape, 1)
        in_band = jnp.abs(s - SWA_BLOCK - c) <= SWA_BLOCK
        for h in range(SWA_Q_HEADS):
            acc = jnp.zeros(bk.shape, F32)
            for b in range(REL_BUCKETS):
                acc = jnp.where(bk == b, t_r[b, h], acc)
            g = h % SWA_GROUP
            o_r[h // SWA_GROUP, :, SWA_BLOCK * g:SWA_BLOCK * (g + 1)] = jnp.where(in_band, acc, -1e30)

    return pl.pallas_call(
        body, name="band_bias",
        in_specs=[pl.BlockSpec(memory_space=pltpu.SMEM), _vmem_spec()], out_specs=_vmem_spec(),
        out_shape=jax.ShapeDtypeStruct((SWA_KV_HEADS, SWA_SPAN, SWA_GROUP_LANES), F32),
    )(rel_bias, buckets)


def _relbias_call(dbias, dsink, buckets, dep=None):
    def body(db_r, ds_r, bk_r, o_r, os_r):
        bk = bk_r[...]
        rowi = lax.broadcasted_iota(jnp.int32, (REL_BUCKETS, 128), 0)
        lanei = lax.broadcasted_iota(jnp.int32, (REL_BUCKETS, 128), 1)
        lane1 = lax.broadcasted_iota(jnp.int32, (1, 128), 1)
        acc = jnp.zeros((REL_BUCKETS, 128), F32)
        acc_sink = jnp.zeros((1, 128), F32)
        for h in range(SWA_Q_HEADS):
            kv, g = h // SWA_GROUP, h % SWA_GROUP
            lanes = slice(SWA_BLOCK * g, SWA_BLOCK * (g + 1))
            part = db_r[kv, :, lanes]
            for b in range(REL_BUCKETS):
                s = jnp.sum(jnp.where(bk == b, part, 0.0))
                acc = acc + jnp.where((rowi == b) & (lanei == h), s, 0.0)
            acc_sink = acc_sink + jnp.where(lane1 == h, jnp.sum(ds_r[kv, :, lanes]), 0.0)
        o_r[...] = acc
        os_r[...] = acc_sink

    body, extra, extra_specs = _after(body, 3, dep)
    return pl.pallas_call(
        body, name="relbias_grad",
        in_specs=[_vmem_spec()] * 3 + extra_specs, out_specs=[_vmem_spec()] * 2,
        out_shape=[jax.ShapeDtypeStruct((REL_BUCKETS, 128), F32), jax.ShapeDtypeStruct((1, 128), F32)],
    )(dbias, dsink, buckets, *extra)


def _mix_call(o_f, o_b, ga, o_s, x, gn, w_out_p, g_post, g_pre2):
    L = x.shape[0]
    tm = min(256, L)
    hw = GLA_HEADS * HEAD_PAD

    def body(of_r, ob_r, ga_r, os_r, x_r, gn_r, w_r, gp_r, g2_r, cat_r, mix_r, h1_r, n2_r):
        gn_v = gn_r[...]
        for h in range(GLA_HEADS):
            sl = slice(HEAD_PAD * h, HEAD_PAD * (h + 1))
            oh = of_r[:, sl] + ob_r[:, sl]
            on = oh * _rms_r(oh) * gn_v
            gate = ga_r[:, sl]
            cat_r[:, sl] = (on * (gate * jax.nn.sigmoid(gate))).astype(cat_r.dtype)
        os_v = os_r[...]
        cat_r[:, hw:] = os_v
        mix = _dot(cat_r[:, :hw], w_r[:hw, :]) + _dot(os_v, w_r[hw:, :])
        mix_r[...] = mix
        h1 = x_r[...] + mix * _rms_r(mix) * gp_r[...]
        h1_r[...] = h1
        n2_r[...] = (h1 * _rms_r(h1) * g2_r[...]).astype(n2_r.dtype)

    return pl.pallas_call(
        body, name="mix_fwd", grid=(L // tm,),
        in_specs=[_row_spec(tm, hw), _row_spec(tm, hw), _row_spec(tm, hw), _row_spec(tm, OUT_PAD - hw),
                  _row_spec(tm, D_MODEL), _full_spec((1, HEAD_PAD)), _vmem_spec(),
                  _full_spec((1, D_MODEL)), _full_spec((1, D_MODEL))],
        out_specs=[_row_spec(tm, OUT_PAD), _row_spec(tm, D_MODEL), _row_spec(tm, D_MODEL), _row_spec(tm, D_MODEL)],
        out_shape=[jax.ShapeDtypeStruct((L, OUT_PAD), MXU_DTYPE), jax.ShapeDtypeStruct((L, D_MODEL), F32),
                   jax.ShapeDtypeStruct((L, D_MODEL), F32), jax.ShapeDtypeStruct((L, D_MODEL), MXU_DTYPE)],
        compiler_params=_params(("arbitrary",), VMEM_BIG),
    )(o_f, o_b, ga, o_s, x, gn, w_out_p, g_post, g_pre2)


def _mlp_fwd_call(n2, h1, tgt, w_ud, g_post):
    L = n2.shape[0]
    tm = min(512, L)
    blk = D_FF // N_CHIPS

    def body(n2_r, h1_r, t_r, w_r, g_r, a_r, rz_r, dh2_r, dff_r, loss_r, dg_r):
        @pl.when(pl.program_id(0) == 0)
        def _():
            loss_r[...] = jnp.zeros_like(loss_r)
            dg_r[...] = jnp.zeros_like(dg_r)

        n2v = n2_r[...]
        ff = jnp.zeros((tm, D_MODEL), F32)
        for j in range(N_CHIPS):
            sl = slice(blk * j, blk * (j + 1))
            rz = jnp.maximum(_dot(n2v, w_r[j, 0]), 0.0)
            a = _mx(rz * rz)
            rz_r[:, sl] = rz.astype(rz_r.dtype)
            a_r[:, sl] = a
            ff = ff + _dot(a, w_r[j, 1])
        g = g_r[...]
        r = _rms_r(ff)
        err = h1_r[...] + ff * r * g - t_r[...]
        loss_r[...] += 0.5 * jnp.sum(err * err) / D_MODEL
        dh2 = err * (1.0 / D_MODEL)
        dh2_r[...] = dh2
        dff, dg = _rms_bwd(ff, r, g, dh2)
        dff_r[...] = dff.astype(dff_r.dtype)
        dg_r[...] += dg

    return pl.pallas_call(
        body, name="mlp_fwd", grid=(L // tm,),
        in_specs=[_row_spec(tm, D_MODEL), _row_spec(tm, D_MODEL), _row_spec(tm, D_MODEL),
                  _vmem_spec(), _full_spec((1, D_MODEL))],
        out_specs=[_row_spec(tm, D_FF), _row_spec(tm, D_FF), _row_spec(tm, D_MODEL), _row_spec(tm, D_MODEL),
                   _full_spec((1, 128)), _full_spec((1, D_MODEL))],
        out_shape=[jax.ShapeDtypeStruct((L, D_FF), MXU_DTYPE), jax.ShapeDtypeStruct((L, D_FF), MXU_DTYPE),
                   jax.ShapeDtypeStruct((L, D_MODEL), F32), jax.ShapeDtypeStruct((L, D_MODEL), MXU_DTYPE),
                   jax.ShapeDtypeStruct((1, 128), F32), jax.ShapeDtypeStruct((1, D_MODEL), F32)],
        compiler_params=_params(("arbitrary",), VMEM_BIG),
    )(n2, h1, tgt, w_ud, g_post)


def _mlp_bwd_call(dff, rz, w_ud):
    L = dff.shape[0]
    tm = min(512, L)
    blk = D_FF // N_CHIPS

    def body(dff_r, rz_r, w_r, dz_r, dn2_r):
        dffv = dff_r[...]
        dn2 = jnp.zeros((tm, D_MODEL), F32)
        for j in range(N_CHIPS):
            sl = slice(blk * j, blk * (j + 1))
            dz = _mx(_dot_nt(dffv, w_r[j, 1]) * 2.0 * rz_r[:, sl].astype(F32))
            dz_r[:, sl] = dz
            dn2 = dn2 + _dot_nt(dz, w_r[j, 0])
        dn2_r[...] = dn2

    return pl.pallas_call(
        body, name="mlp_bwd", grid=(L // tm,),
        in_specs=[_row_spec(tm, D_MODEL), _row_spec(tm, D_FF), _vmem_spec()],
        out_specs=[_row_spec(tm, D_FF), _row_spec(tm, D_MODEL)],
        out_shape=[jax.ShapeDtypeStruct((L, D_FF), MXU_DTYPE), jax.ShapeDtypeStruct((L, D_MODEL), F32)],
        compiler_params=_params(("arbitrary",), VMEM_BIG),
    )(dff, rz, w_ud)


def _mlp_wgrad_call(a, dff, n2, dz):
    L = a.shape[0]
    tf = 512
    per = (D_FF // N_CHIPS) // tf

    def body(a_r, dff_r, n2_r, dz_r, dwd_r, dwu_r):
        dwd_r[...] = _dot_tn(a_r[...], dff_r[...])
        dwu_r[...] = _dot_tn(n2_r[...], dz_r[...])

    return pl.pallas_call(
        body, name="mlp_wgrad", grid=(D_FF // tf,),
        in_specs=[pl.BlockSpec((L, tf), lambda j: (0, j)), _vmem_spec(), _vmem_spec(),
                  pl.BlockSpec((L, tf), lambda j: (0, j))],
        out_specs=[pl.BlockSpec((tf, D_MODEL), lambda j: (j, 0)),
                   pl.BlockSpec((None, D_MODEL, tf), lambda j: (j // per, 0, j % per))],
        out_shape=[jax.ShapeDtypeStruct((D_FF, D_MODEL), F32),
                   jax.ShapeDtypeStruct((N_CHIPS, D_MODEL, D_FF // N_CHIPS), F32)],
        compiler_params=_params(("arbitrary",), VMEM_BIG),
    )(a, dff, n2, dz)


def _mix_bwd_call(dn2, dh2, h1, mix, cat, o_f, o_b, ga, gn, g_post, g_pre2, w_out_p):
    L = dn2.shape[0]
    tm = min(256, L)
    hw = GLA_HEADS * HEAD_PAD

    def body(dn2_r, dh2_r, h1_r, mix_r, cat_r, of_r, ob_r, ga_r, gn_r, gp_r, g2_r, w_r,
             dh1_r, do_r, dga_r, dos_r, dw_r, dg2_r, dgp_r, dgn_r):
        @pl.when(pl.program_id(0) == 0)
        def _():
            for ref in (dw_r, dg2_r, dgp_r, dgn_r):
                ref[...] = jnp.zeros_like(ref)

        h1 = h1_r[...]
        dx2, dg2 = _rms_bwd(h1, _rms_r(h1), g2_r[...], dn2_r[...])
        dh1 = dh2_r[...] + dx2
        dh1_r[...] = dh1
        dg2_r[...] += dg2
        mix = mix_r[...]
        dmix, dgp = _rms_bwd(mix, _rms_r(mix), gp_r[...], dh1)
        dgp_r[...] += dgp
        dmix_m = _mx(dmix)
        dw_r[...] += _dot_tn(cat_r[...], dmix_m)
        dcat = _dot_nt(dmix_m, w_r[...])
        dos_r[...] = _spread_heads(dcat[:, hw:]).astype(dos_r.dtype)
        gn_v = gn_r[...]
        dgn = jnp.zeros((1, HEAD_PAD), F32)
        for h in range(GLA_HEADS):
            sl = slice(HEAD_PAD * h, HEAD_PAD * (h + 1))
            oh = of_r[:, sl] + ob_r[:, sl]
            rr = _rms_r(oh)
            xh = oh * rr
            gate = ga_r[:, sl]
            sg = jax.nn.sigmoid(gate)
            silu = gate * sg
            doa = dcat[:, sl]
            dga_r[:, sl] = doa * (xh * gn_v) * (sg + silu * (1.0 - sg))
            don = doa * silu
            gd = don * gn_v
            do_r[:, sl] = rr * (gd - xh * jnp.mean(gd * xh, axis=-1, keepdims=True))
            dgn = dgn + jnp.sum(don * xh, axis=0, keepdims=True)
        dgn_r[...] += dgn

    return pl.pallas_call(
        body, name="mix_bwd", grid=(L // tm,),
        in_specs=[_row_spec(tm, D_MODEL)] * 4 + [_row_spec(tm, OUT_PAD)] + [_row_spec(tm, hw)] * 3
        + [_full_spec((1, HEAD_PAD)), _full_spec((1, D_MODEL)), _full_spec((1, D_MODEL)), _vmem_spec()],
        out_specs=[_row_spec(tm, D_MODEL), _row_spec(tm, hw), _row_spec(tm, hw),
                   _row_spec(tm, SWA_Q_HEADS * HEAD_PAD),
                   _full_spec((OUT_PAD, D_MODEL)), _full_spec((1, D_MODEL)), _full_spec((1, D_MODEL)),
                   _full_spec((1, HEAD_PAD))],
        out_shape=[jax.ShapeDtypeStruct((L, D_MODEL), F32), jax.ShapeDtypeStruct((L, hw), F32),
                   jax.ShapeDtypeStruct((L, hw), F32), jax.ShapeDtypeStruct((L, SWA_Q_HEADS * HEAD_PAD), MXU_DTYPE),
                   jax.ShapeDtypeStruct((OUT_PAD, D_MODEL), F32), jax.ShapeDtypeStruct((1, D_MODEL), F32),
                   jax.ShapeDtypeStruct((1, D_MODEL), F32), jax.ShapeDtypeStruct((1, HEAD_PAD), F32)],
        compiler_params=_params(("arbitrary",), VMEM_BIG),
    )(dn2, dh2, h1, mix, cat, o_f, o_b, ga, gn, g_post, g_pre2, w_out_p)


def _in_bwd_call(x, dh1, g_pre, w_in_t, pairs, singles, halos, dep=None):
    L = x.shape[0]
    tm = min(256, L)
    n_pair, n_single, n_halo = len(pairs), len(singles), len(halos)
    groups = [c for c, _ in pairs] + [c for c, _ in singles] + [c for c, _ in halos]

    def body(*refs):
        x_r, dh1_r, g_r, w_r = refs[:4]
        pair_refs = refs[4:4 + 2 * n_pair]
        single_refs = refs[4 + 2 * n_pair:4 + 2 * n_pair + n_single]
        halo_refs = refs[4 + 2 * n_pair + n_single:4 + 2 * n_pair + n_single + n_halo]
        dx_r, dw_r, dg_r = refs[4 + 2 * n_pair + n_single + n_halo:]

        @pl.when(pl.program_id(0) == 0)
        def _():
            dw_r[...] = jnp.zeros_like(dw_r)
            dg_r[...] = jnp.zeros_like(dg_r)

        xv = x_r[...]
        r = _rms_r(xv)
        g = g_r[...]
        u = _mx(xv * r * g)
        vals = [pair_refs[2 * i][...] + pair_refs[2 * i + 1][...] for i in range(n_pair)]
        vals += [ref[...].astype(F32) for ref in single_refs]
        inner = pl.ds(pl.multiple_of(pl.program_id(0) * tm + SWA_BLOCK, SWA_BLOCK), tm)
        vals += [ref[inner, :] for ref in halo_refs]
        ds = [_mx(_squeeze_heads(val) if heads else val) for (_, _, heads), val in zip(groups, vals)]
        du = jnp.zeros((tm, D_MODEL), F32)
        for (first, rows, _), d in zip(groups, ds):
            du = du + _dot(d, w_r[first:first + rows, :])
        for (first, rows, _), d in zip(groups, ds):
            dw_r[first:first + rows, :] += _dot_tn(d, u)
        dx, dg = _rms_bwd(xv, r, g, du)
        dx_r[...] = dh1_r[...] + dx
        dg_r[...] += dg

    arrays = [a for _, pr in pairs for a in pr] + [a for _, a in singles]
    specs = [_row_spec(tm, a.shape[1]) for a in arrays] + [_vmem_spec()] * n_halo
    arrays += [a for _, a in halos]
    body, extra, extra_specs = _after(body, 4 + len(arrays), dep)
    return pl.pallas_call(
        body, name="in_bwd", grid=(L // tm,),
        in_specs=[_row_spec(tm, D_MODEL), _row_spec(tm, D_MODEL), _full_spec((1, D_MODEL)), _vmem_spec()] + specs
        + extra_specs,
        out_specs=[_row_spec(tm, D_MODEL), _full_spec((IN_COLS, D_MODEL)), _full_spec((1, D_MODEL))],
        out_shape=[jax.ShapeDtypeStruct((L, D_MODEL), F32), jax.ShapeDtypeStruct((IN_COLS, D_MODEL), F32),
                   jax.ShapeDtypeStruct((1, D_MODEL), F32)],
        compiler_params=_params(("arbitrary",), VMEM_BIG),
    )(x, dh1, g_pre, w_in_t, *arrays, *extra)


def _adamw_math(w, g, m, v):
    m = ADAM_B1 * m + (1.0 - ADAM_B1) * g
    v = ADAM_B2 * v + (1.0 - ADAM_B2) * (g * g)
    m_hat = m / (1.0 - ADAM_B1 ** ADAM_STEP)
    v_hat = v / (1.0 - ADAM_B2 ** ADAM_STEP)
    delta = -ADAM_LR * (m_hat / (jnp.sqrt(v_hat) + ADAM_EPS) + ADAM_WD * w)
    return delta, m, v


def _adamw_call(w, g, m, v, name, dep=None):
    rows, cols = w.shape
    tr = min(256, rows)

    def body(w_r, g_r, m_r, v_r, d_r, nm_r, nv_r):
        d_r[...], nm_r[...], nv_r[...] = _adamw_math(w_r[...], g_r[...], m_r[...], v_r[...])

    if rows % tr == 0:
        spec, steps = _row_spec(tr, cols), rows // tr
    else:
        spec, steps = pl.BlockSpec((rows, 256), lambda i: (0, i)), cols // 256
    body, extra, extra_specs = _after(body, 4, dep)
    return pl.pallas_call(
        body, name=name, grid=(steps,),
        in_specs=[spec] * 4 + extra_specs, out_specs=[spec] * 3,
        out_shape=[jax.ShapeDtypeStruct(w.shape, F32)] * 3,
        compiler_params=_params(("arbitrary",)),
    )(w, g, m, v, *extra)


def _position():
    return lax.axis_index("x"), lax.axis_index("y"), lax.axis_index("c")


def _other_chips(x, y):
    return [(1 - x, y), (x, 1 - y), (1 - x, 1 - y)]


ROWS, COLS = -2, -1


def _half(ref, which, axis):
    size = ref.shape[axis] // 2
    span = pl.ds(pl.multiple_of(which * size, 16 if axis == ROWS else 128), size)
    index = [slice(None)] * len(ref.shape)
    index[axis] = span
    return ref.at[tuple(index)]


def _first_gather_call(shards, axes):
    n = len(shards)

    def body(*refs):
        srcs, outs = refs[:n], refs[n:2 * n]
        send_sems, recv_sems, local_sems = refs[2 * n:]
        x, y, c = _position()
        sibling = (x, y, 1 - c)
        chips = _other_chips(x, y)
        local = [pltpu.make_async_copy(srcs[a], outs[a].at[2 * x + y], local_sems.at[a]) for a in range(n)]
        for cp in local:
            cp.start()

        def copy(a, k, block, to, src=None):
            px, py, pc = block
            dst = _half(outs[a].at[2 * px + py], pc, axes[a])
            return pltpu.make_async_remote_copy(
                src_ref=dst if src is None else src, dst_ref=dst, send_sem=send_sems.at[6 * a + k],
                recv_sem=recv_sems.at[6 * a + k], device_id=to, device_id_type=MESH_ID)

        first, passed = [], []
        for a in range(n):
            my_half = _half(srcs[a], c, axes[a])
            first += [copy(a, j, (x, y, c), (*chip, c), src=my_half) for j, chip in enumerate(chips)]
        for cp in first:
            cp.start()
        for a in range(n):
            for j, chip in enumerate(chips):
                copy(a, j, (*chip, c), (x, y, c)).wait_recv()
                passed.append(copy(a, 3 + j, (*chip, c), sibling))
                passed[-1].start()
        for a in range(n):
            for j, chip in enumerate(chips):
                copy(a, 3 + j, (*chip, 1 - c), (x, y, c)).wait_recv()
        for cp in first + passed:
            cp.wait_send()
        for cp in local:
            cp.wait()

    return pl.pallas_call(
        body, name="first_gather",
        in_specs=[_any_spec()] * n, out_specs=[_any_spec()] * n,
        out_shape=[jax.ShapeDtypeStruct((N_CHIPS,) + s.shape, s.dtype) for s in shards],
        scratch_shapes=[pltpu.SemaphoreType.DMA((6 * n,)), pltpu.SemaphoreType.DMA((6 * n,)),
                        pltpu.SemaphoreType.DMA((n,))],
    )(*shards)


def _split_start(name, arrays, n_copies, plan):
    n = len(arrays)

    def body(*refs):
        ins, send_sems, recv_sems, token = refs[:n], refs[n], refs[n + 1], refs[-1]
        for k, (src, dst, to, _) in enumerate(plan(ins)):
            pltpu.make_async_remote_copy(src_ref=src, dst_ref=dst, send_sem=send_sems.at[k],
                                         recv_sem=recv_sems.at[k], device_id=to, device_id_type=MESH_ID).start()
        token[...] = jnp.zeros_like(token)

    hbm = pl.BlockSpec(memory_space=pltpu.HBM)
    sem = pl.BlockSpec(memory_space=pltpu.SEMAPHORE)
    out = pl.pallas_call(
        body, name=name,
        out_shape=(pltpu.SemaphoreType.DMA((n_copies,)), pltpu.SemaphoreType.DMA((n_copies,)))
        + tuple(pltpu.HBM(a.shape, a.dtype) for a in arrays) + (jax.ShapeDtypeStruct((8, 128), F32),),
        in_specs=[hbm] * n, out_specs=(sem, sem) + (hbm,) * n + (_vmem_spec(),),
        input_output_aliases={i: 2 + i for i in range(n)},
        compiler_params=pltpu.CompilerParams(has_side_effects=pltpu.SideEffectType.DATAFLOW_SIDE_EFFECTING),
    )(*[pltpu.with_memory_space_constraint(a, pltpu.HBM) for a in arrays])
    return (out[0], out[1], tuple(out[2:2 + n])), out[-1]


def _split_wait(name, handle, n_copies, plan, after):
    send_sems, recv_sems, arrays = handle
    n = len(arrays)

    def body(*refs):
        ins, s_sems, r_sems = refs[:n], refs[n], refs[n + 1]
        for k, (src, dst, to, landed) in enumerate(plan(ins)):
            cp = pltpu.make_async_remote_copy(src_ref=src, dst_ref=landed, send_sem=s_sems.at[k],
                                              recv_sem=r_sems.at[k], device_id=to, device_id_type=MESH_ID)
            cp.wait_send()
            cp.wait_recv()

    hbm = pl.BlockSpec(memory_space=pltpu.HBM)
    sem = pl.BlockSpec(memory_space=pltpu.SEMAPHORE)
    out = pl.pallas_call(
        body, name=name,
        out_shape=tuple(pltpu.HBM(a.shape, a.dtype) for a in arrays),
        in_specs=[hbm] * n + [sem, sem, _any_spec()], out_specs=(hbm,) * n,
        input_output_aliases={i: i for i in range(n)},
        compiler_params=pltpu.CompilerParams(has_side_effects=pltpu.SideEffectType.DATAFLOW_SIDE_EFFECTING),
    )(*arrays, send_sems, recv_sems, after)
    return tuple(out)


def _gather_plans(axes):
    n = len(axes)

    def stage_one(refs):
        x, y, c = _position()
        copies = []
        for a, axis in enumerate(axes):
            for px, py in _other_chips(x, y):
                copies.append((_half(refs[a], c, axis), _half(refs[n + a].at[2 * x + y], c, axis),
                               (px, py, c), _half(refs[n + a].at[2 * px + py], c, axis)))
        return copies

    def stage_two(refs):
        x, y, c = _position()
        copies = []
        for a, axis in enumerate(axes):
            for px, py in _other_chips(x, y):
                piece = _half(refs[n + a].at[2 * px + py], c, axis)
                copies.append((piece, piece, (x, y, 1 - c), _half(refs[n + a].at[2 * px + py], 1 - c, axis)))
        return copies

    return stage_one, stage_two


def _pair_swap_plan(axes):
    n = len(axes)

    def plan(refs):
        x, y, c = _position()
        return [(_half(refs[a], 1 - c, axes[a]), refs[n + a], (x, y, 1 - c), refs[n + a]) for a in range(n)]

    return plan


def _chip_swap_plan(n):
    def plan(refs):
        x, y, c = _position()
        copies = []
        for a in range(n):
            for j, (px, py) in enumerate(_other_chips(x, y)):
                copies.append((refs[a].at[2 * px + py], refs[n + a].at[j], (px, py, c), refs[n + a].at[j]))
        return copies

    return plan


def _pair_join_plan(axes):
    def plan(refs):
        x, y, c = _position()
        copies = []
        for a, axis in enumerate(axes):
            mine = _half(refs[a], c, axis)
            copies.append((mine, mine, (x, y, 1 - c), _half(refs[a], 1 - c, axis)))
        return copies

    return plan


def _pair_add_call(g, got, pos, name, axis):
    rows, cols = got.shape[1], got.shape[2]
    tr = min(512, rows) if axis == ROWS else rows
    nblk = rows // tr
    if axis == ROWS:
        mine = lambda j, i, p: (j, p[1] * nblk + i, 0)
    else:
        mine = lambda j, i, p: (j, 0, p[1])

    def body(pos_r, g_r, got_r, o_r):
        o_r[...] = (g_r[...] + got_r[...]).astype(o_r.dtype)

    return pl.pallas_call(
        body, name=name,
        grid_spec=pltpu.PrefetchScalarGridSpec(
            num_scalar_prefetch=1, grid=(N_CHIPS, nblk),
            in_specs=[pl.BlockSpec((None, tr, cols), mine),
                      pl.BlockSpec((None, tr, cols), lambda j, i, p: (j, i, 0))],
            out_specs=pl.BlockSpec((None, tr, cols), lambda j, i, p: (j, i, 0))),
        out_shape=jax.ShapeDtypeStruct(got.shape, COMM_DTYPE),
        compiler_params=_params(("arbitrary", "arbitrary"), VMEM_BIG),
    )(pos, g, got)


def _chip_add_call(hsum, got, pos, name, axis):
    rows, cols = hsum.shape[1], hsum.shape[2]
    tr = min(512, rows) if axis == ROWS else rows
    nblk = rows // tr
    if axis == ROWS:
        out_shape, mine = (2 * rows, cols), (lambda i, p: (p[1] * nblk + i, 0))
    else:
        out_shape, mine = (rows, 2 * cols), (lambda i, p: (0, p[1]))

    def body(pos_r, own_r, got_r, o_r):
        acc = own_r[...].astype(F32)
        for j in range(3):
            acc = acc + got_r[j].astype(F32)
        o_r[...] = acc

    return pl.pallas_call(
        body, name=name,
        grid_spec=pltpu.PrefetchScalarGridSpec(
            num_scalar_prefetch=1, grid=(nblk,),
            in_specs=[pl.BlockSpec((None, tr, cols), lambda i, p: (p[0], i, 0)),
                      pl.BlockSpec((3, tr, cols), lambda i, p: (0, i, 0))],
            out_specs=pl.BlockSpec((tr, cols), mine)),
        out_shape=jax.ShapeDtypeStruct(out_shape, F32),
        compiler_params=_params(("arbitrary",), VMEM_BIG),
    )(pos, hsum, got)


SMALL_NAMES = ("norm_mix_pre", "norm_mix_post", "norm_mlp_pre", "norm_mlp_post", "b_gate_fwd", "b_gate_bwd",
               "gla_norm", "swa_sink", "rel_bias")


def _small_update_call(grads, gate_grads, params, dep=None):
    n_dev = 8
    n_small = len(SMALL_NAMES)
    wmv = [t for p in params for t in p]
    shapes = [p[0].shape for p in params]

    def body(*refs):
        g_refs = refs[:n_small + 3]
        wmv_refs = refs[n_small + 3:n_small + 3 + 3 * n_small]
        n_in = n_small + 3 + 3 * n_small
        out_refs = refs[n_in:n_in + 4 * n_small + 3]
        pack_a, pack_b, all_a, all_b, send_sems, recv_sems = refs[n_in + 4 * n_small + 3:]
        x, y, c = _position()
        me = 4 * x + 2 * y + c
        pack_a[...] = jnp.zeros_like(pack_a)
        pack_b[...] = jnp.zeros_like(pack_b)
        for i in range(4):
            pack_a[i:i + 1, :] = g_refs[i][...]
        pack_a[4:5, 0:256] = g_refs[4][...]
        pack_a[5:6, 0:256] = g_refs[5][...]
        pack_a[6:7, 0:128] = g_refs[6][...]
        pack_a[7:8, 0:128] = g_refs[7][...]
        pack_a[7:8, 128:256] = g_refs[11][...]
        pack_b[0:32, 0:128] = g_refs[8][...]
        pack_b[32:48, :] = g_refs[9][...]
        pack_b[48:64, :] = g_refs[10][...]
        all_a[me] = pack_a[...]
        all_b[me] = pack_b[...]
        copies = []
        for k in range(1, n_dev):
            fx, fy, fc = (k >> 2) & 1, (k >> 1) & 1, k & 1
            to = (1 - x if fx else x, 1 - y if fy else y, 1 - c if fc else c)
            for t, (pack, dst) in enumerate(((pack_a, all_a), (pack_b, all_b))):
                copies.append(pltpu.make_async_remote_copy(
                    src_ref=pack, dst_ref=dst.at[me], send_sem=send_sems.at[2 * (k - 1) + t],
                    recv_sem=recv_sems.at[2 * (k - 1) + t], device_id=to, device_id_type=MESH_ID))
        for cp in copies:
            cp.start()
        for cp in copies:
            cp.wait()
        sum_a, sum_b = all_a[0], all_b[0]
        for d in range(1, n_dev):
            sum_a = sum_a + all_a[d]
            sum_b = sum_b + all_b[d]
        gsum = [sum_a[0:1], sum_a[1:2], sum_a[2:3], sum_a[3:4], sum_a[4:5, 0:256], sum_a[5:6, 0:256],
                sum_a[6:7, 0:128], sum_a[7:8, 0:SWA_Q_HEADS], sum_b[0:32, 0:SWA_Q_HEADS]]
        for i in range(n_small):
            w_r, m_r, v_r = wmv_refs[3 * i:3 * i + 3]
            delta, new_m, new_v = _adamw_math(w_r[...], gsum[i], m_r[...], v_r[...])
            out_refs[4 * i][...] = gsum[i]
            out_refs[4 * i + 1][...] = delta
            out_refs[4 * i + 2][...] = new_m
            out_refs[4 * i + 3][...] = new_v
        out_refs[4 * n_small][...] = sum_b[32:48]
        out_refs[4 * n_small + 1][...] = sum_b[48:64]
        out_refs[4 * n_small + 2][...] = sum_a[7:8, 128:256]

    n_in = n_small + 3 + 3 * n_small
    body, extra, extra_specs = _after(body, n_in, dep)
    out_shape = [jax.ShapeDtypeStruct(s, F32) for s in shapes for _ in range(4)]
    out_shape += [jax.ShapeDtypeStruct((GLA_GATE_RANK, 256), F32)] * 2 + [jax.ShapeDtypeStruct((1, 128), F32)]
    out = pl.pallas_call(
        body, name="small_update",
        in_specs=[_vmem_spec()] * n_in + extra_specs, out_specs=[_vmem_spec()] * len(out_shape),
        out_shape=out_shape,
        scratch_shapes=[pltpu.VMEM((8, D_MODEL), F32), pltpu.VMEM((64, 256), F32),
                        pltpu.VMEM((n_dev, 8, D_MODEL), F32), pltpu.VMEM((n_dev, 64, 256), F32),
                        pltpu.SemaphoreType.DMA((2 * (n_dev - 1),)), pltpu.SemaphoreType.DMA((2 * (n_dev - 1),))],
    )(*grads, *gate_grads, *wmv, *extra)
    per_name = [tuple(out[4 * i:4 * i + 4]) for i in range(n_small)]
    return per_name, out[4 * n_small], out[4 * n_small + 1], out[4 * n_small + 2]


def _pad_heads(t, n_heads, axis=-1):
    axis = axis % t.ndim
    shape = t.shape
    t = t.reshape(shape[:axis] + (n_heads, 64) + shape[axis + 1:])
    pad = [(0, 0)] * t.ndim
    pad[axis + 1] = (0, HEAD_PAD - 64)
    return jnp.pad(t, pad).reshape(shape[:axis] + (n_heads * HEAD_PAD,) + shape[axis + 1:])


def _unpad_heads(t, n_heads, axis=-1):
    axis = axis % t.ndim
    shape = t.shape
    t = t.reshape(shape[:axis] + (n_heads, HEAD_PAD) + shape[axis + 1:])
    t = lax.slice_in_dim(t, 0, 64, axis=axis + 1)
    return t.reshape(shape[:axis] + (n_heads * 64,) + shape[axis + 1:])


def _pad_gate(w, first_row):
    return jnp.pad(_pad_heads(w, 4), ((first_row, 128 - GLA_GATE_RANK - first_row), (0, 0)))


def _own_slot(shard, chip):
    zone = lax.empty((N_CHIPS,) + shard.shape, shard.dtype)
    return lax.dynamic_update_slice(zone, shard[None], (chip,) + (0,) * shard.ndim)


def _reduce_to_owners(grads, axes, pos, tag, overlap):
    n = len(grads)

    def half_shape(g, axis):
        return (N_CHIPS, g.shape[1] // 2, g.shape[2]) if axis == ROWS else (N_CHIPS, g.shape[1], g.shape[2] // 2)

    lands = [lax.empty(half_shape(g, axis), F32) for g, axis in zip(grads, axes)]
    handle, token = _split_start(tag + "_pair_start", list(grads) + lands, n, _pair_swap_plan(axes))
    got = _split_wait(tag + "_pair_wait", handle, n, _pair_swap_plan(axes), overlap[0](token))
    sums = [_pair_add_call(got[a], got[n + a], pos, f"{tag}_pair_add{a}", axes[a]) for a in range(n)]
    lands = [lax.empty((3,) + s.shape[1:], s.dtype) for s in sums]
    handle, token = _split_start(tag + "_chip_start", sums + lands, 3 * n, _chip_swap_plan(n))
    got = _split_wait(tag + "_chip_wait", handle, 3 * n, _chip_swap_plan(n), overlap[1](token))
    halves = [_chip_add_call(got[a], got[n + a], pos, f"{tag}_chip_add{a}", axes[a]) for a in range(n)]
    handle, token = _split_start(tag + "_join_start", halves, n, _pair_join_plan(axes))
    return _split_wait(tag + "_join_wait", handle, n, _pair_join_plan(axes), overlap[2](token))


def kernel(x, norm_mix_pre, w_in, w_gate_up_fwd, b_gate_fwd, w_gate_up_bwd, b_gate_bwd, gla_norm, swa_sink, rel_bias, w_out, norm_mix_post, norm_mlp_pre, w_up, w_down, norm_mlp_post, loss_target, m_norm_mix_pre, m_w_in, m_w_gate_up_fwd, m_b_gate_fwd, m_w_gate_up_bwd, m_b_gate_bwd, m_gla_norm, m_swa_sink, m_rel_bias, m_w_out, m_norm_mix_post, m_norm_mlp_pre, m_w_up, m_w_down, m_norm_mlp_post, v_norm_mix_pre, v_w_in, v_w_gate_up_fwd, v_b_gate_fwd, v_w_gate_up_bwd, v_b_gate_bwd, v_gla_norm, v_swa_sink, v_rel_bias, v_w_out, v_norm_mix_post, v_norm_mlp_pre, v_w_up, v_w_down, v_norm_mlp_post):
    given = dict(locals())
    cx, cy, cc = _position()
    chip = (2 * cx + cy).astype(jnp.int32)
    pos = jnp.stack([chip, cc.astype(jnp.int32)])
    seq, tgt = x[0], loss_target[0]
    L = seq.shape[0]

    gates = jnp.concatenate([w_gate_up_fwd[0], w_gate_up_bwd[0]], axis=0).astype(COMM_DTYPE)
    all_in, all_gates = _first_gather_call([w_in[0].T.astype(COMM_DTYPE), gates], [COLS, ROWS])
    rest = [w_out[0].astype(COMM_DTYPE), jnp.stack([w_up[0], w_down[0]]).astype(COMM_DTYPE)]
    stage_one, stage_two = _gather_plans([ROWS, ROWS])
    handle, token = _split_start("gather_chip_start", rest + [_own_slot(s, chip) for s in rest] + [all_gates], 6,
                                 stage_one)

    w_in_t = _mx(all_in.reshape(IN_COLS, D_MODEL))
    gates_full = jnp.concatenate([all_gates[j] for j in range(N_CHIPS)], axis=1)
    wgf_p = _mx(_pad_gate(gates_full[:GLA_GATE_RANK], 0))
    wgb_p = _mx(_pad_gate(gates_full[GLA_GATE_RANK:], GLA_GATE_RANK))
    bf_p, bb_p = _pad_heads(b_gate_fwd, 4), _pad_heads(b_gate_bwd, 4)
    buckets = jnp.asarray(_band_buckets())
    bias = _bias_call(rel_bias, buckets)
    sink1 = swa_sink.reshape(SWA_Q_HEADS)

    qa, ka, va, ga, qs, ks, vs, za = _proj_call(seq, norm_mix_pre, w_in_t, dep=token)
    halo = ((SWA_BLOCK, SWA_BLOCK), (0, 0))
    ks_p, vs_p = jnp.pad(ks, halo), jnp.pad(vs, halo)
    o_f, o_b, s_f, s_b = _gla_fwd_call(qa, ka, va, za, wgf_p, bf_p, wgb_p, bb_p)
    arrays = _split_wait("gather_chip_wait", handle, 6, stage_one, o_f)
    handle, token = _split_start("gather_pair_start", list(arrays), 6, stage_two)
    o_s = _swa_fwd_call(qs, ks_p, vs_p, bias, sink1, dep=token)
    arrays = _split_wait("gather_pair_wait", handle, 6, stage_two, o_s)
    w_out_full = _mx(arrays[2].reshape(N_CHIPS * R_OUT, D_MODEL))
    w_ud = _mx(arrays[3])
    cat, mix, h1, n2 = _mix_call(o_f, o_b, ga, o_s, seq, gla_norm, w_out_full, norm_mix_post, norm_mlp_pre)
    a, rz, dh2, dff, loss, d_post2 = _mlp_fwd_call(n2, h1, tgt, w_ud, norm_mlp_post)

    dz, dn2 = _mlp_bwd_call(dff, rz, w_ud)
    dw_down, dw_up4 = _mlp_wgrad_call(a, dff, n2, dz)
    dh1, do, dga, dos, dw_out, d_pre2, d_post, d_gn = _mix_bwd_call(
        dn2, dh2, h1, mix, cat, o_f, o_b, ga, gla_norm, norm_mix_post, norm_mlp_pre, w_out_full)
    done = {}

    def swa_backward(tok):
        done["swa"] = _swa_bwd_call(qs, ks_p, vs_p, bias, sink1, dos, dep=tok)
        return done["swa"][0]

    def gla_in_backward(tok):
        done["gla"] = _gla_bwd_call(qa, ka, va, za, do, s_f, s_b, wgf_p, bf_p, wgb_p, bb_p, dep=tok)
        dqf, dkf, dvf, dzf, _, _, dqb, dkb, dvb, dzb, _, _ = done["gla"]
        dqs, dks_p, dvs_p, _, _ = done["swa"]
        done["in"] = _in_bwd_call(
            seq, dh1, norm_mix_pre, w_in_t,
            pairs=[(T_QA, (dqf, dqb)), (T_KA, (dkf, dkb)), (T_VA, (dvf, dvb)), (T_ZA, (dzf, dzb))],
            singles=[(T_GA, dga), (T_QS, dqs)], halos=[(T_KS, dks_p), (T_VS, dvs_p)])
        return done["in"][0]

    def bias_backward(tok):
        done["rel"] = _relbias_call(done["swa"][3], done["swa"][4], buckets, dep=tok)
        return done["rel"][0]

    g_up, g_down, g_out = _reduce_to_owners(
        [dw_up4, dw_down.reshape(N_CHIPS, R_DOWN, D_MODEL), dw_out.reshape(N_CHIPS, R_OUT, D_MODEL)],
        [ROWS, ROWS, ROWS], pos, "mlp", [swa_backward, gla_in_backward, bias_backward])
    dx, dw_in_t, d_pre = done["in"]
    dwf, dbf, dwb, dbb = done["gla"][4], done["gla"][5], done["gla"][10], done["gla"][11]
    drel, dsink = done["rel"]

    small_grads = [d_pre, d_post, d_pre2, d_post2, _unpad_heads(dbf, 4), _unpad_heads(dbb, 4), d_gn, dsink, drel]
    gate_grads = [_unpad_heads(dwf[:GLA_GATE_RANK], 4), _unpad_heads(dwb[GLA_GATE_RANK:2 * GLA_GATE_RANK], 4)]
    small_params = [(given[n], given["m_" + n], given["v_" + n]) for n in SMALL_NAMES]
    upd = {}

    def update_up(tok):
        upd["w_up"] = (g_up,) + tuple(_adamw_call(w_up[0], g_up, m_w_up[0], v_w_up[0], "adamw_w_up", dep=tok))
        return upd["w_up"][1]

    def update_small(tok):
        per_name, gf_sum, gb_sum, upd["loss"] = _small_update_call(small_grads, gate_grads + [loss], small_params,
                                                                   dep=tok)
        upd.update(dict(zip(SMALL_NAMES, per_name)))
        for name, total in (("w_gate_up_fwd", gf_sum), ("w_gate_up_bwd", gb_sum)):
            g = lax.dynamic_slice(total, (0, chip * 64), (GLA_GATE_RANK, 64))
            upd[name] = (g,) + tuple(_adamw_call(given[name][0], g, given["m_" + name][0], given["v_" + name][0],
                                                 "adamw_" + name))
        upd["w_down"] = (g_down,) + tuple(
            _adamw_call(w_down[0], g_down, m_w_down[0], v_w_down[0], "adamw_w_down", dep=gf_sum))
        return upd["w_down"][1]

    def update_out(tok):
        upd["w_out"] = (g_out,) + tuple(_adamw_call(w_out[0], g_out, m_w_out[0], v_w_out[0], "adamw_w_out", dep=tok))
        return upd["w_out"][1]

    (g_in_t,) = _reduce_to_owners([dw_in_t.reshape(N_CHIPS, R_IN, D_MODEL)], [COLS], pos, "in",
                                  [update_up, update_small, update_out])
    in_t = (g_in_t,) + tuple(_adamw_call(w_in[0].T, g_in_t, m_w_in[0].T, v_w_in[0].T, "adamw_w_in"))
    upd["w_in"] = tuple(t.T for t in in_t)

    big = ("w_in", "w_gate_up_fwd", "w_gate_up_bwd", "w_out", "w_up", "w_down")
    names = ["norm_mix_pre", "w_in", "w_gate_up_fwd", "b_gate_fwd", "w_gate_up_bwd", "b_gate_bwd", "gla_norm",
             "swa_sink", "rel_bias", "w_out", "norm_mix_post", "norm_mlp_pre", "w_up", "w_down", "norm_mlp_post"]
    outs = [upd["loss"][0, 0], dx[None]]
    for kind in range(4):
        outs += [upd[n][kind][None] if n in big else upd[n][kind] for n in names]
    return tuple(outs)
```

```python
import math

import numpy as np
import jax
import jax.numpy as jnp
from jax import lax
from jax.experimental import pallas as pl
from jax.experimental.pallas import tpu as pltpu

F32 = jnp.float32
MXU_DTYPE = jnp.bfloat16
COMM_DTYPE = jnp.bfloat16

D_MODEL = 1024
D_FF = 4096
N_CHIPS = 4
GLA_HEADS = 4
GLA_CHUNK = 64
GLA_GATE_RANK = 16
GLA_GATE_NORM = 16.0
SWA_Q_HEADS = 8
SWA_KV_HEADS = 2
SWA_BLOCK = 128
REL_BUCKETS = 32
REL_MAX_DIST = 128
NORM_EPS = 1e-6
HEAD_PAD = 128

ADAM_LR = 0.001
ADAM_B1 = 0.9
ADAM_B2 = 0.999
ADAM_EPS = 1e-08
ADAM_WD = 0.01
ADAM_STEP = 10

OUT_PAD = 1024

R_IN, R_OUT, R_UP, R_DOWN = 584, 256, 1024, 1024

VMEM_BIG = 56 * 1024 * 1024
MESH_AXES = ("x", "y", "c")
MESH_ID = pl.DeviceIdType.MESH


def _mx(a):
    return a.astype(MXU_DTYPE)


def _dot(a, b):
    return jnp.dot(a, b, preferred_element_type=F32)


def _dot_nt(a, b):
    return lax.dot_general(a, b, (((1,), (1,)), ((), ())), preferred_element_type=F32)


def _dot_tn(a, b):
    return lax.dot_general(a, b, (((0,), (0,)), ((), ())), preferred_element_type=F32)


def _rms_r(x):
    return lax.rsqrt(jnp.mean(x * x, axis=-1, keepdims=True) + NORM_EPS)


def _rms_bwd(x, r, g, dy):
    xh = x * r
    gdy = dy * g
    dx = r * (gdy - xh * jnp.mean(gdy * xh, axis=-1, keepdims=True))
    return dx, jnp.sum(dy * xh, axis=0, keepdims=True)


def _low_half(rows):
    return lax.broadcasted_iota(jnp.int32, (rows, HEAD_PAD), 1) < 64


def _spread_heads(x):
    low = _low_half(x.shape[0])
    parts = []
    for p in range(x.shape[1] // HEAD_PAD):
        pair = x[:, HEAD_PAD * p:HEAD_PAD * (p + 1)]
        parts += [jnp.where(low, pair, 0.0), jnp.where(low, pltpu.roll(pair, 64, 1), 0.0)]
    return jnp.concatenate(parts, axis=1)


def _squeeze_heads(x):
    low = _low_half(x.shape[0])
    parts = []
    for p in range(x.shape[1] // (2 * HEAD_PAD)):
        even = x[:, 2 * HEAD_PAD * p:2 * HEAD_PAD * p + HEAD_PAD]
        odd = x[:, 2 * HEAD_PAD * p + HEAD_PAD:2 * HEAD_PAD * (p + 1)]
        parts.append(jnp.where(low, even, pltpu.roll(odd, 64, 1)))
    return parts[0] if len(parts) == 1 else jnp.concatenate(parts, axis=1)


def _params(sem=None, vmem=None):
    kw = {}
    if sem is not None:
        kw["dimension_semantics"] = sem
    if vmem is not None:
        kw["vmem_limit_bytes"] = vmem
    return pltpu.CompilerParams(**kw)


def _vmem_spec():
    return pl.BlockSpec(memory_space=pltpu.VMEM)


def _whole_spec(shape):
    return pl.BlockSpec(shape, lambda: (0,) * len(shape))


def _row_spec(tm, width):
    return pl.BlockSpec((tm, width), lambda i: (i, 0))


def _full_spec(shape):
    return pl.BlockSpec(shape, lambda i: (0,) * len(shape))


def _any_spec():
    return pl.BlockSpec(memory_space=pl.ANY)


def _after(body, n_in, dep):
    if dep is None:
        return body, [], []
    return (lambda *refs: body(*refs[:n_in], *refs[n_in + 1:])), [dep], [_any_spec()]


T_QA, T_KA, T_VA, T_GA = (0, 256, 4), (256, 256, 4), (512, 512, 0), (1024, 512, 0)
T_QS, T_KS, T_VS = (1568, 512, 8), (2080, 128, 2), (2208, 128, 2)
T_ZA = (1536, 128, 0)
ZA_COLS = 2 * GLA_GATE_RANK
IN_COLS = 2336


def _proj_call(x, g_pre, w_in_t, dep=None):
    L = x.shape[0]
    tm = min(512, L)
    groups = [(T_QA, F32), (T_KA, F32), (T_VA, MXU_DTYPE), (T_GA, F32),
              (T_QS, MXU_DTYPE), (T_KS, MXU_DTYPE), (T_VS, MXU_DTYPE), (T_ZA, F32)]
    widths = [rows * (2 if heads else 1) for (_, rows, heads), _ in groups]

    def body(x_ref, g_ref, w_ref, *outs):
        xv = x_ref[...]
        u = _mx(xv * _rms_r(xv) * g_ref[...])
        for ref, (grp, _) in zip(outs, groups):
            first, rows, heads = grp
            val = _dot_nt(u, w_ref[first:first + rows, :])
            if heads:
                val = _spread_heads(val)
            if grp is T_ZA:
                val = jnp.where(lax.broadcasted_iota(jnp.int32, val.shape, 1) < ZA_COLS, val, 0.0)
            if grp is T_QS:
                val = val * 0.125
            ref[...] = val.astype(ref.dtype)

    body, extra, extra_specs = _after(body, 3, dep)
    return pl.pallas_call(
        body, name="proj_fwd", grid=(L // tm,),
        in_specs=[_row_spec(tm, D_MODEL), _full_spec((1, D_MODEL)), _vmem_spec()] + extra_specs,
        out_specs=[_row_spec(tm, w) for w in widths],
        out_shape=[jax.ShapeDtypeStruct((L, w), dt) for w, (_, dt) in zip(widths, groups)],
        compiler_params=_params(("arbitrary",), VMEM_BIG),
    )(x, g_pre, w_in_t, *extra)


def _tri_masks():
    row = lax.broadcasted_iota(jnp.int32, (GLA_CHUNK, GLA_CHUNK), 0)
    col = lax.broadcasted_iota(jnp.int32, (GLA_CHUNK, GLA_CHUNK), 1)
    return row >= col, row <= col


def _chunk_sums(tri_m, x):
    hi = _mx(x)
    rest = x - hi.astype(F32)
    mid = _mx(rest)
    lo = _mx(rest - mid.astype(F32))
    return _dot(tri_m, hi) + _dot(tri_m, mid) + _dot(tri_m, lo)


def _gla_block_pre(q_r, k_r, z_r, w_r, b_r, rev, nc, qd_s, ki_s, ks_s, dec_s, keep=None):
    tri_f, tri_b = _tri_masks()
    tri_m = _mx((tri_b if rev else tri_f).astype(F32))
    g = _dot(_mx(z_r[...]), w_r[...]) + b_r[...]
    la = (jnp.minimum(g, 0.0) - jnp.log(1.0 + jnp.exp(-jnp.abs(g)))) / GLA_GATE_NORM
    sums, lasts = [], []
    for c in range(nc):
        b_c = _chunk_sums(tri_m, la[GLA_CHUNK * c:GLA_CHUNK * (c + 1)])
        blast = b_c[0:1] if rev else b_c[GLA_CHUNK - 1:GLA_CHUNK]
        dec_s[c] = jnp.exp(blast)
        sums.append(b_c)
        lasts.append(jnp.broadcast_to(blast, b_c.shape))
    b = jnp.concatenate(sums, axis=0)
    eb = jnp.exp(b)
    enb = jnp.exp(-b)
    elb = jnp.exp(jnp.concatenate(lasts, axis=0) - b)
    k = k_r[...]
    qd_s[...] = (q_r[...] * 0.125 * eb).astype(qd_s.dtype)
    ki_s[...] = (k * enb).astype(ki_s.dtype)
    ks_s[...] = (k * elb).astype(ks_s.dtype)
    if keep is not None:
        for ref, val in zip(keep, (g, eb, enb, elb)):
            ref[...] = val


def _gla_fwd_call(qa, ka, va, za, wgf, bgf, wgb, bgb):
    L = qa.shape[0]
    br = min(512, L)
    nb, nc, n_chunks = L // br, br // GLA_CHUNK, L // GLA_CHUNK
    hw = GLA_HEADS * HEAD_PAD

    def body(qaf, kaf, vaf, zaf, qab, kab, vab, zab, wgf_r, bgf_r, wgb_r, bgb_r,
             of_r, ob_r, sf_r, sb_r, st_f, st_b, pre_f, pre_b):
        @pl.when(pl.program_id(0) == 0)
        def _():
            st_f[...] = jnp.zeros_like(st_f)
            st_b[...] = jnp.zeros_like(st_b)

        _gla_block_pre(qaf, kaf, zaf, wgf_r, bgf_r, False, nc, *pre_f)
        _gla_block_pre(qab, kab, zab, wgb_r, bgb_r, True, nc, *pre_b)
        tri_f, tri_b = _tri_masks()

        def one(tri, pre, v_r, o_r, s_r, st, ci):
            qd_s, ki_s, ks_s, dec_s = pre
            rows = pl.ds(pl.multiple_of(ci * GLA_CHUNK, GLA_CHUNK), GLA_CHUNK)
            dec = dec_s[ci]
            heads = range(GLA_HEADS)
            lanes = [slice(HEAD_PAD * h, HEAD_PAD * (h + 1)) for h in heads]
            qd = [qd_s[rows, sl] for sl in lanes]
            v = [v_r[rows, sl] for sl in lanes]
            s_t = [st[h] for h in heads]
            a = [_dot_nt(qd[h], ki_s[rows, lanes[h]]) for h in heads]
            carried = [_dot_nt(qd[h], _mx(s_t[h])) for h in heads]
            grown = [_dot_tn(v[h], ks_s[rows, lanes[h]]) for h in heads]
            a = [_mx(jnp.where(tri, a[h], 0.0)) for h in heads]
            inner = [_dot(a[h], v[h]) for h in heads]
            for h in heads:
                s_r[ci, h] = s_t[h]
                o_r[rows, lanes[h]] = inner[h] + carried[h]
                st[h] = s_t[h] * dec[:, lanes[h]] + grown[h]

        def loop(t, carry):
            one(tri_f, pre_f, vaf, of_r, sf_r, st_f, t)
            one(tri_b, pre_b, vab, ob_r, sb_r, st_b, nc - 1 - t)
            return carry

        lax.fori_loop(0, nc, loop, 0, unroll=True)

    fwd = lambda i: (i, 0)
    bwd = lambda i: (nb - 1 - i, 0)
    ins = lambda m: [pl.BlockSpec((br, hw), m), pl.BlockSpec((br, hw), m),
                     pl.BlockSpec((br, hw), m), pl.BlockSpec((br, 128), m)]
    wspecs = [_full_spec((128, hw)), _full_spec((1, hw))] * 2
    s_shape = (nc, GLA_HEADS, HEAD_PAD, HEAD_PAD)
    pre_scratch = [pltpu.VMEM((br, hw), MXU_DTYPE)] * 3 + [pltpu.VMEM((nc, 1, hw), F32)]
    return pl.pallas_call(
        body, name="gla_fwd", grid=(nb,),
        in_specs=ins(fwd) + ins(bwd) + wspecs,
        out_specs=[pl.BlockSpec((br, hw), fwd), pl.BlockSpec((br, hw), bwd),
                   pl.BlockSpec(s_shape, lambda i: (i, 0, 0, 0)),
                   pl.BlockSpec(s_shape, lambda i: (nb - 1 - i, 0, 0, 0))],
        out_shape=[jax.ShapeDtypeStruct((L, hw), F32), jax.ShapeDtypeStruct((L, hw), F32),
                   jax.ShapeDtypeStruct((n_chunks,) + s_shape[1:], F32),
                   jax.ShapeDtypeStruct((n_chunks,) + s_shape[1:], F32)],
        scratch_shapes=[pltpu.VMEM(s_shape[1:], F32), pltpu.VMEM(s_shape[1:], F32), pre_scratch, pre_scratch],
        compiler_params=_params(("arbitrary",), VMEM_BIG),
    )(qa, ka, va, za, qa, ka, va, za, wgf, bgf, wgb, bgb)


def _gla_bwd_call(qa, ka, va, za, do, sf, sb, wgf, bgf, wgb, bgb, dep=None):
    L = qa.shape[0]
    br = min(256, L)
    nb, nc = L // br, br // GLA_CHUNK
    hw = GLA_HEADS * HEAD_PAD

    def body(qaf, kaf, vaf, zaf, dof, sf_r, qab, kab, vab, zab, dob, sb_r, wgf_r, bgf_r, wgb_r, bgb_r,
             dqf, dkf, dvf, dzf, dwf, dbf, dqb, dkb, dvb, dzb, dwb, dbb, gt_f, gt_b, pre_f, pre_b):
        @pl.when(pl.program_id(0) == 0)
        def _():
            for ref in (gt_f, gt_b, dwf, dbf, dwb, dbb):
                ref[...] = jnp.zeros_like(ref)

        _gla_block_pre(qaf, kaf, zaf, wgf_r, bgf_r, False, nc, *pre_f[:4], keep=pre_f[4:8])
        _gla_block_pre(qab, kab, zab, wgb_r, bgb_r, True, nc, *pre_b[:4], keep=pre_b[4:8])
        tri_f, tri_b = _tri_masks()
        row_w = lax.broadcasted_iota(jnp.int32, (GLA_CHUNK, HEAD_PAD), 0)

        def one(rev, pre, q_r, k_r, v_r, do_r, s_r, dq_r, dk_r, dv_r, gt, ci):
            qd_s, ki_s, ks_s, dec_s, _, eb_s, enb_s, elb_s, db_s = pre
            tri = tri_b if rev else tri_f
            last_row = 0 if rev else GLA_CHUNK - 1
            rows = pl.ds(pl.multiple_of(ci * GLA_CHUNK, GLA_CHUNK), GLA_CHUNK)
            dec = dec_s[ci]
            heads = range(GLA_HEADS)
            lanes = [slice(HEAD_PAD * h, HEAD_PAD * (h + 1)) for h in heads]
            qd = [qd_s[rows, sl] for sl in lanes]
            ki = [ki_s[rows, sl] for sl in lanes]
            ks = [ks_s[rows, sl] for sl in lanes]
            v = [v_r[rows, sl] for sl in lanes]
            do_h = [_mx(do_r[rows, sl]) for sl in lanes]
            s_t = [s_r[ci, h] for h in heads]
            g_t = [gt[h] for h in heads]
            g_m = [_mx(g_t[h]) for h in heads]
            a = [_dot_nt(qd[h], ki[h]) for h in heads]
            da = [_dot_nt(do_h[h], v[h]) for h in heads]
            dv_carried = [_dot_nt(ks[h], g_m[h]) for h in heads]
            dqd_carried = [_dot(do_h[h], _mx(s_t[h])) for h in heads]
            dks = [_dot(v[h], g_m[h]) for h in heads]
            g_grown = [_dot_tn(do_h[h], qd[h]) for h in heads]
            a = [_mx(jnp.where(tri, a[h], 0.0)) for h in heads]
            da = [_mx(jnp.where(tri, da[h], 0.0)) for h in heads]
            dv_inner = [_dot_tn(a[h], do_h[h]) for h in heads]
            dqd_inner = [_dot(da[h], ki[h]) for h in heads]
            dki = [_dot_tn(da[h], qd[h]) for h in heads]
            for h in heads:
                sl = lanes[h]
                dv_r[rows, sl] = dv_inner[h] + dv_carried[h]
                ddec = jnp.sum(g_t[h] * s_t[h], axis=0, keepdims=True)
                gt[h] = g_t[h] * dec[:, sl] + g_grown[h]
                dq = (dqd_inner[h] + dqd_carried[h]) * eb_s[rows, sl] * 0.125
                dk_state = dks[h] * elb_s[rows, sl]
                dk = dki[h] * enb_s[rows, sl] + dk_state
                dq_r[rows, sl] = dq
                dk_r[rows, sl] = dk
                k = k_r[rows, sl]
                dblast = jnp.sum(dk_state * k, axis=0, keepdims=True) + dec[:, sl] * ddec
                db_s[rows, sl] = q_r[rows, sl] * dq - k * dk + jnp.where(row_w == last_row, dblast, 0.0)

        def loop(t, carry):
            one(False, pre_f, qaf, kaf, vaf, dof, sf_r, dqf, dkf, dvf, gt_f, nc - 1 - t)
            one(True, pre_b, qab, kab, vab, dob, sb_r, dqb, dkb, dvb, gt_b, t)
            return carry

        lax.fori_loop(0, nc, loop, 0, unroll=True)

        def gate_grads(rev, pre, z_r, w_r, dz_r, dw_r, dbias_r):
            g_s, db_s = pre[4], pre[8]
            back_m = _mx((tri_f if rev else tri_b).astype(F32))
            db = db_s[...]
            dla = jnp.concatenate([_chunk_sums(back_m, db[GLA_CHUNK * c:GLA_CHUNK * (c + 1)]) for c in range(nc)],
                                  axis=0)
            dg = dla * (1.0 / GLA_GATE_NORM) * (1.0 / (1.0 + jnp.exp(g_s[...])))
            dg_m = _mx(dg)
            dz_r[...] = _dot_nt(dg_m, w_r[...])
            dw_r[...] += _dot_tn(_mx(z_r[...]), dg_m)
            dbias_r[...] += jnp.sum(dg, axis=0, keepdims=True)

        gate_grads(False, pre_f, zaf, wgf_r, dzf, dwf, dbf)
        gate_grads(True, pre_b, zab, wgb_r, dzb, dwb, dbb)

    last_first = lambda i: (nb - 1 - i, 0)
    first_last = lambda i: (i, 0)
    s_shape = (nc, GLA_HEADS, HEAD_PAD, HEAD_PAD)

    def ins(m):
        return [pl.BlockSpec((br, hw), m), pl.BlockSpec((br, hw), m), pl.BlockSpec((br, hw), m),
                pl.BlockSpec((br, 128), m), pl.BlockSpec((br, hw), m),
                pl.BlockSpec(s_shape, lambda i: m(i) + (0, 0))]

    def outs(m):
        return [pl.BlockSpec((br, hw), m), pl.BlockSpec((br, hw), m), pl.BlockSpec((br, hw), m),
                pl.BlockSpec((br, 128), m), _full_spec((128, hw)), _full_spec((1, hw))]

    out_shape = [jax.ShapeDtypeStruct((L, hw), F32)] * 3 + [
        jax.ShapeDtypeStruct((L, 128), F32), jax.ShapeDtypeStruct((128, hw), F32),
        jax.ShapeDtypeStruct((1, hw), F32)]
    wspecs = [_full_spec((128, hw)), _full_spec((1, hw))] * 2
    body, extra, extra_specs = _after(body, 16, dep)
    pre_scratch = ([pltpu.VMEM((br, hw), MXU_DTYPE)] * 3 + [pltpu.VMEM((nc, 1, hw), F32)]
                   + [pltpu.VMEM((br, hw), F32)] * 5)
    return pl.pallas_call(
        body, name="gla_bwd", grid=(nb,),
        in_specs=ins(last_first) + ins(first_last) + wspecs + extra_specs,
        out_specs=outs(last_first) + outs(first_last),
        out_shape=out_shape + out_shape,
        scratch_shapes=[pltpu.VMEM(s_shape[1:], F32), pltpu.VMEM(s_shape[1:], F32), pre_scratch, pre_scratch],
        compiler_params=_params(("arbitrary",), VMEM_BIG),
    )(qa, ka, va, za, do, sf, qa, ka, va, za, do, sb, wgf, bgf, wgb, bgb, *extra)


def _t5_buckets(rel):
    nb = REL_BUCKETS // 2
    ret = (rel > 0).astype(np.int32) * nb
    n = np.abs(rel)
    max_exact = nb // 2
    large = max_exact + (np.log(np.maximum(n, 1).astype(np.float32) / max_exact)
                         / math.log(REL_MAX_DIST / max_exact) * (nb - max_exact)).astype(np.int32)
    large = np.minimum(large, nb - 1)
    return ret + np.where(n < max_exact, n, large)


SWA_GROUP = SWA_Q_HEADS // SWA_KV_HEADS
SWA_SPAN = 3 * SWA_BLOCK
SWA_GROUP_LANES = SWA_GROUP * SWA_BLOCK


def _band_buckets():
    s = np.arange(SWA_SPAN)[:, None]
    c = np.arange(SWA_BLOCK)[None, :]
    return _t5_buckets(s - SWA_BLOCK - c).astype(np.int32)


def _swa_valid(n, seq_len):
    key_pos = (n - 1) * SWA_BLOCK + lax.broadcasted_iota(jnp.int32, (SWA_SPAN, 1), 0)
    return (key_pos >= 0) & (key_pos < seq_len)


def _swa_sink_row(sink_r, kv):
    lane = lax.broadcasted_iota(jnp.int32, (1, SWA_GROUP_LANES), 1)
    row = jnp.full((1, SWA_GROUP_LANES), sink_r[kv * SWA_GROUP], F32)
    for g in range(1, SWA_GROUP):
        row = jnp.where(lane >= g * SWA_BLOCK, sink_r[kv * SWA_GROUP + g], row)
    return row


def _swa_group(ref, kv):
    first = kv * SWA_GROUP
    return jnp.concatenate([ref[:, HEAD_PAD * h:HEAD_PAD * (h + 1)] for h in range(first, first + SWA_GROUP)],
                           axis=0)


def _swa_softmax(scores, bias_t, sink_row, valid):
    st = scores + bias_t
    st = jnp.where(valid, st, -1e30)
    m = jnp.maximum(jnp.max(st, axis=0, keepdims=True), sink_row)
    p = jnp.exp(st - m)
    e_sink = jnp.exp(sink_row - m)
    inv = 1.0 / (jnp.sum(p, axis=0, keepdims=True) + e_sink)
    return p * inv, e_sink * inv


def _swa_fwd_call(qs, ks, vs, bias, sink, dep=None):
    L = qs.shape[0]

    def body(q_r, k_r, v_r, bias_r, sink_r, o_r):
        n = pl.program_id(0)
        span = pl.ds(pl.multiple_of(n * SWA_BLOCK, SWA_BLOCK), SWA_SPAN)
        valid = _swa_valid(n, L)
        groups = range(SWA_KV_HEADS)
        lanes = [slice(HEAD_PAD * kv, HEAD_PAD * (kv + 1)) for kv in groups]
        scores = [_dot_nt(k_r[span, lanes[kv]], _swa_group(q_r, kv)) for kv in groups]
        probs = [_swa_softmax(scores[kv], bias_r[kv], _swa_sink_row(sink_r, kv), valid)[0] for kv in groups]
        low = _low_half(SWA_BLOCK)
        for kv in groups:
            og = _dot_tn(_mx(probs[kv]), v_r[span, lanes[kv]])
            for pair in range(SWA_GROUP // 2):
                even = og[2 * SWA_BLOCK * pair:2 * SWA_BLOCK * pair + SWA_BLOCK]
                odd = og[2 * SWA_BLOCK * pair + SWA_BLOCK:2 * SWA_BLOCK * (pair + 1)]
                first = HEAD_PAD * (kv * SWA_GROUP // 2 + pair)
                o_r[:, first:first + HEAD_PAD] = jnp.where(low, even, pltpu.roll(odd, 64, 1)).astype(o_r.dtype)

    qw = SWA_Q_HEADS * HEAD_PAD
    body, extra, extra_specs = _after(body, 5, dep)
    return pl.pallas_call(
        body, name="swa_fwd", grid=(L // SWA_BLOCK,),
        in_specs=[_row_spec(SWA_BLOCK, qw), _vmem_spec(), _vmem_spec(), _vmem_spec(),
                  pl.BlockSpec(memory_space=pltpu.SMEM)] + extra_specs,
        out_specs=_row_spec(SWA_BLOCK, qw // 2),
        out_shape=jax.ShapeDtypeStruct((L, qw // 2), MXU_DTYPE),
        compiler_params=_params(("arbitrary",), VMEM_BIG),
    )(qs, ks, vs, bias, sink, *extra)


def _swa_bwd_call(qs, ks, vs, bias, sink, do, dep=None):
    L = qs.shape[0]
    qw = SWA_Q_HEADS * HEAD_PAD
    kw = SWA_KV_HEADS * HEAD_PAD

    def body(q_r, k_r, v_r, bias_r, sink_r, do_r, dq_r, dk_r, dv_r, dbias_r, dsink_r):
        n = pl.program_id(0)

        @pl.when(n == 0)
        def _():
            for ref in (dk_r, dv_r, dbias_r, dsink_r):
                ref[...] = jnp.zeros_like(ref)

        span = pl.ds(pl.multiple_of(n * SWA_BLOCK, SWA_BLOCK), SWA_SPAN)
        valid = _swa_valid(n, L)
        groups = range(SWA_KV_HEADS)
        lanes = [slice(HEAD_PAD * kv, HEAD_PAD * (kv + 1)) for kv in groups]
        kk = [k_r[span, sl] for sl in lanes]
        vv = [v_r[span, sl] for sl in lanes]
        qg = [_swa_group(q_r, kv) for kv in groups]
        dog = [_swa_group(do_r, kv) for kv in groups]
        scores = [_dot_nt(kk[kv], qg[kv]) for kv in groups]
        dp = [_dot_nt(vv[kv], dog[kv]) for kv in groups]
        probs = [_swa_softmax(scores[kv], bias_r[kv], _swa_sink_row(sink_r, kv), valid) for kv in groups]
        ds_m, pn_m = [], []
        for kv in groups:
            pn, p_sink = probs[kv]
            delta = jnp.sum(pn * dp[kv], axis=0, keepdims=True)
            ds = pn * (dp[kv] - delta)
            dsink_r[kv] -= p_sink * delta
            dbias_r[kv] += ds
            ds_m.append(_mx(ds))
            pn_m.append(_mx(pn))
        dqg = [_dot_tn(ds_m[kv], kk[kv]) * 0.125 for kv in groups]
        dkk = [_dot(ds_m[kv], qg[kv]) for kv in groups]
        dvv = [_dot(pn_m[kv], dog[kv]) for kv in groups]
        for kv in groups:
            for g in range(SWA_GROUP):
                h = kv * SWA_GROUP + g
                dq_r[:, HEAD_PAD * h:HEAD_PAD * (h + 1)] = dqg[kv][SWA_BLOCK * g:SWA_BLOCK * (g + 1)]
            dk_r[span, lanes[kv]] += dkk[kv]
            dv_r[span, lanes[kv]] += dvv[kv]

    body, extra, extra_specs = _after(body, 6, dep)
    return pl.pallas_call(
        body, name="swa_bwd", grid=(L // SWA_BLOCK,),
        in_specs=[_row_spec(SWA_BLOCK, qw), _vmem_spec(), _vmem_spec(), _vmem_spec(),
                  pl.BlockSpec(memory_space=pltpu.SMEM), _row_spec(SWA_BLOCK, qw)] + extra_specs,
        out_specs=[_row_spec(SWA_BLOCK, qw), _vmem_spec(), _vmem_spec(), _vmem_spec(), _vmem_spec()],
        out_shape=[jax.ShapeDtypeStruct((L, qw), F32),
                   jax.ShapeDtypeStruct((L + 2 * SWA_BLOCK, kw), F32),
                   jax.ShapeDtypeStruct((L + 2 * SWA_BLOCK, kw), F32),
                   jax.ShapeDtypeStruct((SWA_KV_HEADS, SWA_SPAN, SWA_GROUP_LANES), F32),
                   jax.ShapeDtypeStruct((SWA_KV_HEADS, 1, SWA_GROUP_LANES), F32)],
        compiler_params=_params(("arbitrary",), VMEM_BIG),
    )(qs, ks, vs, bias, sink, do, *extra)


def _bias_call(rel_bias, buckets):
    def body(t_r, bk_r, o_r):
        bk = bk_r[...]
        s = lax.broadcasted_iota(jnp.int32, bk.shape, 0)
        c = lax.broadcasted_iota(jnp.int32, bk.shape, 1)
        in_band = jnp.abs(s - SWA_BLOCK - c) <= SWA_BLOCK
        for h in range(SWA_Q_HEADS):
            acc = jnp.zeros(bk.shape, F32)
            for b in range(REL_BUCKETS):
                acc = jnp.where(bk == b, t_r[b, h], acc)
            g = h % SWA_GROUP
            o_r[h // SWA_GROUP, :, SWA_BLOCK * g:SWA_BLOCK * (g + 1)] = jnp.where(in_band, acc, -1e30)

    return pl.pallas_call(
        body, name="band_bias",
        in_specs=[pl.BlockSpec(memory_space=pltpu.SMEM), _vmem_spec()], out_specs=_vmem_spec(),
        out_shape=jax.ShapeDtypeStruct((SWA_KV_HEADS, SWA_SPAN, SWA_GROUP_LANES), F32),
    )(rel_bias, buckets)


def _relbias_call(dbias, dsink, buckets, dep=None):
    def body(db_r, ds_r, bk_r, o_r, os_r):
        bk = bk_r[...]
        rowi = lax.broadcasted_iota(jnp.int32, (REL_BUCKETS, 128), 0)
        lanei = lax.broadcasted_iota(jnp.int32, (REL_BUCKETS, 128), 1)
        lane1 = lax.broadcasted_iota(jnp.int32, (1, 128), 1)
        acc = jnp.zeros((REL_BUCKETS, 128), F32)
        acc_sink = jnp.zeros((1, 128), F32)
        for h in range(SWA_Q_HEADS):
            kv, g = h // SWA_GROUP, h % SWA_GROUP
            lanes = slice(SWA_BLOCK * g, SWA_BLOCK * (g + 1))
            part = db_r[kv, :, lanes]
            for b in range(REL_BUCKETS):
                s = jnp.sum(jnp.where(bk == b, part, 0.0))
                acc = acc + jnp.where((rowi == b) & (lanei == h), s, 0.0)
            acc_sink = acc_sink + jnp.where(lane1 == h, jnp.sum(ds_r[kv, :, lanes]), 0.0)
        o_r[...] = acc
        os_r[...] = acc_sink

    body, extra, extra_specs = _after(body, 3, dep)
    return pl.pallas_call(
        body, name="relbias_grad",
        in_specs=[_vmem_spec()] * 3 + extra_specs, out_specs=[_vmem_spec()] * 2,
        out_shape=[jax.ShapeDtypeStruct((REL_BUCKETS, 128), F32), jax.ShapeDtypeStruct((1, 128), F32)],
    )(dbias, dsink, buckets, *extra)


def _mix_call(o_f, o_b, ga, o_s, x, gn, w_out_p, g_post, g_pre2):
    L = x.shape[0]
    tm = min(512, L)
    hw = GLA_HEADS * HEAD_PAD

    def body(of_r, ob_r, ga_r, os_r, x_r, gn_r, w_r, gp_r, g2_r, cat_r, mix_r, h1_r, n2_r):
        gn_v = gn_r[...]
        for h in range(GLA_HEADS):
            sl = slice(HEAD_PAD * h, HEAD_PAD * (h + 1))
            oh = of_r[:, sl] + ob_r[:, sl]
            on = oh * _rms_r(oh) * gn_v
            gate = ga_r[:, sl]
            cat_r[:, sl] = (on * (gate * jax.nn.sigmoid(gate))).astype(cat_r.dtype)
        os_v = os_r[...]
        cat_r[:, hw:] = os_v
        mix = _dot(cat_r[:, :hw], w_r[:hw, :]) + _dot(os_v, w_r[hw:, :])
        mix_r[...] = mix
        h1 = x_r[...] + mix * _rms_r(mix) * gp_r[...]
        h1_r[...] = h1
        n2_r[...] = (h1 * _rms_r(h1) * g2_r[...]).astype(n2_r.dtype)

    return pl.pallas_call(
        body, name="mix_fwd", grid=(L // tm,),
        in_specs=[_row_spec(tm, hw), _row_spec(tm, hw), _row_spec(tm, hw), _row_spec(tm, OUT_PAD - hw),
                  _row_spec(tm, D_MODEL), _full_spec((1, HEAD_PAD)), _vmem_spec(),
                  _full_spec((1, D_MODEL)), _full_spec((1, D_MODEL))],
        out_specs=[_row_spec(tm, OUT_PAD), _row_spec(tm, D_MODEL), _row_spec(tm, D_MODEL), _row_spec(tm, D_MODEL)],
        out_shape=[jax.ShapeDtypeStruct((L, OUT_PAD), MXU_DTYPE), jax.ShapeDtypeStruct((L, D_MODEL), F32),
                   jax.ShapeDtypeStruct((L, D_MODEL), F32), jax.ShapeDtypeStruct((L, D_MODEL), MXU_DTYPE)],
        compiler_params=_params(("arbitrary",), VMEM_BIG),
    )(o_f, o_b, ga, o_s, x, gn, w_out_p, g_post, g_pre2)


def _mlp_fwd_call(n2, h1, tgt, w_ud, g_post):
    L = n2.shape[0]
    tm = min(512, L)
    blk = D_FF // N_CHIPS

    def body(n2_r, h1_r, t_r, w_r, g_r, a_r, rz_r, dh2_r, dff_r, loss_r, dg_r):
        @pl.when(pl.program_id(0) == 0)
        def _():
            loss_r[...] = jnp.zeros_like(loss_r)
            dg_r[...] = jnp.zeros_like(dg_r)

        n2v = n2_r[...]
        ff = jnp.zeros((tm, D_MODEL), F32)
        for j in range(N_CHIPS):
            sl = slice(blk * j, blk * (j + 1))
            rz = jnp.maximum(_dot(n2v, w_r[j, 0]), 0.0)
            a = _mx(rz * rz)
            rz_r[:, sl] = rz.astype(rz_r.dtype)
            a_r[:, sl] = a
            ff = ff + _dot(a, w_r[j, 1])
        g = g_r[...]
        r = _rms_r(ff)
        err = h1_r[...] + ff * r * g - t_r[...]
        loss_r[...] += 0.5 * jnp.sum(err * err) / D_MODEL
        dh2 = err * (1.0 / D_MODEL)
        dh2_r[...] = dh2
        dff, dg = _rms_bwd(ff, r, g, dh2)
        dff_r[...] = dff.astype(dff_r.dtype)
        dg_r[...] += dg

    return pl.pallas_call(
        body, name="mlp_fwd", grid=(L // tm,),
        in_specs=[_row_spec(tm, D_MODEL), _row_spec(tm, D_MODEL), _row_spec(tm, D_MODEL),
                  _vmem_spec(), _full_spec((1, D_MODEL))],
        out_specs=[_row_spec(tm, D_FF), _row_spec(tm, D_FF), _row_spec(tm, D_MODEL), _row_spec(tm, D_MODEL),
                   _full_spec((1, 128)), _full_spec((1, D_MODEL))],
        out_shape=[jax.ShapeDtypeStruct((L, D_FF), MXU_DTYPE), jax.ShapeDtypeStruct((L, D_FF), MXU_DTYPE),
                   jax.ShapeDtypeStruct((L, D_MODEL), F32), jax.ShapeDtypeStruct((L, D_MODEL), MXU_DTYPE),
                   jax.ShapeDtypeStruct((1, 128), F32), jax.ShapeDtypeStruct((1, D_MODEL), F32)],
        compiler_params=_params(("arbitrary",), VMEM_BIG),
    )(n2, h1, tgt, w_ud, g_post)


def _mlp_bwd_call(dff, rz, w_ud):
    L = dff.shape[0]
    tm = min(512, L)
    blk = D_FF // N_CHIPS

    def body(dff_r, rz_r, w_r, dz_r, dn2_r):
        dffv = dff_r[...]
        dn2 = jnp.zeros((tm, D_MODEL), F32)
        for j in range(N_CHIPS):
            sl = slice(blk * j, blk * (j + 1))
            dz = _mx(_dot_nt(dffv, w_r[j, 1]) * 2.0 * rz_r[:, sl].astype(F32))
            dz_r[:, sl] = dz
            dn2 = dn2 + _dot_nt(dz, w_r[j, 0])
        dn2_r[...] = dn2

    return pl.pallas_call(
        body, name="mlp_bwd", grid=(L // tm,),
        in_specs=[_row_spec(tm, D_MODEL), _row_spec(tm, D_FF), _vmem_spec()],
        out_specs=[_row_spec(tm, D_FF), _row_spec(tm, D_MODEL)],
        out_shape=[jax.ShapeDtypeStruct((L, D_FF), MXU_DTYPE), jax.ShapeDtypeStruct((L, D_MODEL), F32)],
        compiler_params=_params(("arbitrary",), VMEM_BIG),
    )(dff, rz, w_ud)


def _mlp_wgrad_call(a, dff, n2, dz):
    L = a.shape[0]
    tf = 512
    per = (D_FF // N_CHIPS) // tf

    def body(a_r, dff_r, n2_r, dz_r, dwd_r, dwu_r):
        dwd_r[...] = _dot_tn(a_r[...], dff_r[...])
        dwu_r[...] = _dot_tn(n2_r[...], dz_r[...])

    return pl.pallas_call(
        body, name="mlp_wgrad", grid=(D_FF // tf,),
        in_specs=[pl.BlockSpec((L, tf), lambda j: (0, j)), _vmem_spec(), _vmem_spec(),
                  pl.BlockSpec((L, tf), lambda j: (0, j))],
        out_specs=[pl.BlockSpec((tf, D_MODEL), lambda j: (j, 0)),
                   pl.BlockSpec((None, D_MODEL, tf), lambda j: (j // per, 0, j % per))],
        out_shape=[jax.ShapeDtypeStruct((D_FF, D_MODEL), F32),
                   jax.ShapeDtypeStruct((N_CHIPS, D_MODEL, D_FF // N_CHIPS), F32)],
        compiler_params=_params(("arbitrary",), VMEM_BIG),
    )(a, dff, n2, dz)


def _mix_bwd_call(dn2, dh2, h1, mix, cat, o_f, o_b, ga, gn, g_post, g_pre2, w_out_p):
    L = dn2.shape[0]
    tm = min(512, L)
    hw = GLA_HEADS * HEAD_PAD

    def body(dn2_r, dh2_r, h1_r, mix_r, cat_r, of_r, ob_r, ga_r, gn_r, gp_r, g2_r, w_r,
             dh1_r, do_r, dga_r, dos_r, dw_r, dg2_r, dgp_r, dgn_r):
        @pl.when(pl.program_id(0) == 0)
        def _():
            for ref in (dw_r, dg2_r, dgp_r, dgn_r):
                ref[...] = jnp.zeros_like(ref)

        h1 = h1_r[...]
        dx2, dg2 = _rms_bwd(h1, _rms_r(h1), g2_r[...], dn2_r[...])
        dh1 = dh2_r[...] + dx2
        dh1_r[...] = dh1
        dg2_r[...] += dg2
        mix = mix_r[...]
        dmix, dgp = _rms_bwd(mix, _rms_r(mix), gp_r[...], dh1)
        dgp_r[...] += dgp
        dmix_m = _mx(dmix)
        dw_r[...] += _dot_tn(cat_r[...], dmix_m)
        dcat = _dot_nt(dmix_m, w_r[...])
        dos_r[...] = _spread_heads(dcat[:, hw:]).astype(dos_r.dtype)
        gn_v = gn_r[...]
        dgn = jnp.zeros((1, HEAD_PAD), F32)
        for h in range(GLA_HEADS):
            sl = slice(HEAD_PAD * h, HEAD_PAD * (h + 1))
            oh = of_r[:, sl] + ob_r[:, sl]
            rr = _rms_r(oh)
            xh = oh * rr
            gate = ga_r[:, sl]
            sg = jax.nn.sigmoid(gate)
            silu = gate * sg
            doa = dcat[:, sl]
            dga_r[:, sl] = doa * (xh * gn_v) * (sg + silu * (1.0 - sg))
            don = doa * silu
            gd = don * gn_v
            do_r[:, sl] = rr * (gd - xh * jnp.mean(gd * xh, axis=-1, keepdims=True))
            dgn = dgn + jnp.sum(don * xh, axis=0, keepdims=True)
        dgn_r[...] += dgn

    return pl.pallas_call(
        body, name="mix_bwd", grid=(L // tm,),
        in_specs=[_row_spec(tm, D_MODEL)] * 4 + [_row_spec(tm, OUT_PAD)] + [_row_spec(tm, hw)] * 3
        + [_full_spec((1, HEAD_PAD)), _full_spec((1, D_MODEL)), _full_spec((1, D_MODEL)), _vmem_spec()],
        out_specs=[_row_spec(tm, D_MODEL), _row_spec(tm, hw), _row_spec(tm, hw),
                   _row_spec(tm, SWA_Q_HEADS * HEAD_PAD),
                   _full_spec((OUT_PAD, D_MODEL)), _full_spec((1, D_MODEL)), _full_spec((1, D_MODEL)),
                   _full_spec((1, HEAD_PAD))],
        out_shape=[jax.ShapeDtypeStruct((L, D_MODEL), F32), jax.ShapeDtypeStruct((L, hw), F32),
                   jax.ShapeDtypeStruct((L, hw), F32), jax.ShapeDtypeStruct((L, SWA_Q_HEADS * HEAD_PAD), MXU_DTYPE),
                   jax.ShapeDtypeStruct((OUT_PAD, D_MODEL), F32), jax.ShapeDtypeStruct((1, D_MODEL), F32),
                   jax.ShapeDtypeStruct((1, D_MODEL), F32), jax.ShapeDtypeStruct((1, HEAD_PAD), F32)],
        compiler_params=_params(("arbitrary",), VMEM_BIG),
    )(dn2, dh2, h1, mix, cat, o_f, o_b, ga, gn, g_post, g_pre2, w_out_p)


def _in_bwd_call(x, dh1, g_pre, w_in_t, pairs, singles, halos, dep=None):
    L = x.shape[0]
    tm = min(512, L)
    per = tm // SWA_BLOCK
    n_pair, n_single, n_halo = len(pairs), len(singles), len(halos)
    groups = [c for c, _ in pairs] + [c for c, _ in singles] + [c for c, _ in halos]

    def body(*refs):
        x_r, dh1_r, g_r, w_r = refs[:4]
        pair_refs = refs[4:4 + 2 * n_pair]
        single_refs = refs[4 + 2 * n_pair:4 + 2 * n_pair + n_single]
        halo_refs = refs[4 + 2 * n_pair + n_single:4 + 2 * n_pair + n_single + per * n_halo]
        dx_r, dw_r, dg_r = refs[4 + 2 * n_pair + n_single + per * n_halo:]

        @pl.when(pl.program_id(0) == 0)
        def _():
            dw_r[...] = jnp.zeros_like(dw_r)
            dg_r[...] = jnp.zeros_like(dg_r)

        xv = x_r[...]
        r = _rms_r(xv)
        g = g_r[...]
        u = _mx(xv * r * g)
        vals = [pair_refs[2 * i][...] + pair_refs[2 * i + 1][...] for i in range(n_pair)]
        vals += [ref[...].astype(F32) for ref in single_refs]
        vals += [jnp.concatenate([ref[...] for ref in halo_refs[per * i:per * (i + 1)]], axis=0)
                 for i in range(n_halo)]
        ds = [_mx(_squeeze_heads(val) if heads else val) for (_, _, heads), val in zip(groups, vals)]
        du = jnp.zeros((tm, D_MODEL), F32)
        for (first, rows, _), d in zip(groups, ds):
            du = du + _dot(d, w_r[first:first + rows, :])
        for (first, rows, _), d in zip(groups, ds):
            dw_r[first:first + rows, :] += _dot_tn(d, u)
        dx, dg = _rms_bwd(xv, r, g, du)
        dx_r[...] = dh1_r[...] + dx
        dg_r[...] += dg

    arrays = [a for _, pr in pairs for a in pr] + [a for _, a in singles]
    specs = [_row_spec(tm, a.shape[1]) for a in arrays]
    for _, a in halos:
        specs += [pl.BlockSpec((SWA_BLOCK, a.shape[1]), lambda i, j=j: (per * i + 1 + j, 0)) for j in range(per)]
        arrays += [a] * per
    body, extra, extra_specs = _after(body, 4 + len(arrays), dep)
    return pl.pallas_call(
        body, name="in_bwd", grid=(L // tm,),
        in_specs=[_row_spec(tm, D_MODEL), _row_spec(tm, D_MODEL), _full_spec((1, D_MODEL)), _vmem_spec()] + specs
        + extra_specs,
        out_specs=[_row_spec(tm, D_MODEL), _full_spec((IN_COLS, D_MODEL)), _full_spec((1, D_MODEL))],
        out_shape=[jax.ShapeDtypeStruct((L, D_MODEL), F32), jax.ShapeDtypeStruct((IN_COLS, D_MODEL), F32),
                   jax.ShapeDtypeStruct((1, D_MODEL), F32)],
        compiler_params=_params(("arbitrary",), VMEM_BIG),
    )(x, dh1, g_pre, w_in_t, *arrays, *extra)


def _adamw_math(w, g, m, v):
    m = ADAM_B1 * m + (1.0 - ADAM_B1) * g
    v = ADAM_B2 * v + (1.0 - ADAM_B2) * (g * g)
    m_hat = m / (1.0 - ADAM_B1 ** ADAM_STEP)
    v_hat = v / (1.0 - ADAM_B2 ** ADAM_STEP)
    delta = -ADAM_LR * (m_hat / (jnp.sqrt(v_hat) + ADAM_EPS) + ADAM_WD * w)
    return delta, m, v


def _adamw_call(w, g, m, v, name, dep=None):
    rows, cols = w.shape
    tr = min(256, rows)

    def body(w_r, g_r, m_r, v_r, d_r, nm_r, nv_r):
        d_r[...], nm_r[...], nv_r[...] = _adamw_math(w_r[...], g_r[...], m_r[...], v_r[...])

    if rows % tr == 0:
        spec, steps = _row_spec(tr, cols), rows // tr
    else:
        spec, steps = pl.BlockSpec((rows, 256), lambda i: (0, i)), cols // 256
    body, extra, extra_specs = _after(body, 4, dep)
    return pl.pallas_call(
        body, name=name, grid=(steps,),
        in_specs=[spec] * 4 + extra_specs, out_specs=[spec] * 3,
        out_shape=[jax.ShapeDtypeStruct(w.shape, F32)] * 3,
        compiler_params=_params(("arbitrary",)),
    )(w, g, m, v, *extra)


def _position():
    return lax.axis_index("x"), lax.axis_index("y"), lax.axis_index("c")


def _other_chips(x, y):
    return [(1 - x, y), (x, 1 - y), (1 - x, 1 - y)]


ROWS, COLS = -2, -1


def _half(ref, which, axis):
    size = ref.shape[axis] // 2
    span = pl.ds(pl.multiple_of(which * size, 16 if axis == ROWS else 128), size)
    index = [slice(None)] * len(ref.shape)
    index[axis] = span
    return ref.at[tuple(index)]


def _first_gather_call(shards, axes):
    n = len(shards)

    def body(*refs):
        srcs, outs = refs[:n], refs[n:2 * n]
        send_sems, recv_sems, local_sems = refs[2 * n:]
        x, y, c = _position()
        sibling = (x, y, 1 - c)
        chips = _other_chips(x, y)
        local = [pltpu.make_async_copy(srcs[a], outs[a].at[2 * x + y], local_sems.at[a]) for a in range(n)]
        for cp in local:
            cp.start()

        def copy(a, k, block, to, src=None):
            px, py, pc = block
            dst = _half(outs[a].at[2 * px + py], pc, axes[a])
            return pltpu.make_async_remote_copy(
                src_ref=dst if src is None else src, dst_ref=dst, send_sem=send_sems.at[6 * a + k],
                recv_sem=recv_sems.at[6 * a + k], device_id=to, device_id_type=MESH_ID)

        first, passed = [], []
        for a in range(n):
            my_half = _half(srcs[a], c, axes[a])
            first += [copy(a, j, (x, y, c), (*chip, c), src=my_half) for j, chip in enumerate(chips)]
        for cp in first:
            cp.start()
        for a in range(n):
            for j, chip in enumerate(chips):
                copy(a, j, (*chip, c), (x, y, c)).wait_recv()
                passed.append(copy(a, 3 + j, (*chip, c), sibling))
                passed[-1].start()
        for a in range(n):
            for j, chip in enumerate(chips):
                copy(a, 3 + j, (*chip, 1 - c), (x, y, c)).wait_recv()
        for cp in first + passed:
            cp.wait_send()
        for cp in local:
            cp.wait()

    return pl.pallas_call(
        body, name="first_gather",
        in_specs=[_any_spec()] * n, out_specs=[_any_spec()] * n,
        out_shape=[jax.ShapeDtypeStruct((N_CHIPS,) + s.shape, s.dtype) for s in shards],
        scratch_shapes=[pltpu.SemaphoreType.DMA((6 * n,)), pltpu.SemaphoreType.DMA((6 * n,)),
                        pltpu.SemaphoreType.DMA((n,))],
    )(*shards)


def _split_start(name, arrays, n_copies, plan):
    n = len(arrays)

    def body(*refs):
        ins, send_sems, recv_sems, token = refs[:n], refs[n], refs[n + 1], refs[-1]
        for k, (src, dst, to, _) in enumerate(plan(ins)):
            pltpu.make_async_remote_copy(src_ref=src, dst_ref=dst, send_sem=send_sems.at[k],
                                         recv_sem=recv_sems.at[k], device_id=to, device_id_type=MESH_ID).start()
        token[...] = jnp.zeros_like(token)

    hbm = pl.BlockSpec(memory_space=pltpu.HBM)
    sem = pl.BlockSpec(memory_space=pltpu.SEMAPHORE)
    out = pl.pallas_call(
        body, name=name,
        out_shape=(pltpu.SemaphoreType.DMA((n_copies,)), pltpu.SemaphoreType.DMA((n_copies,)))
        + tuple(pltpu.HBM(a.shape, a.dtype) for a in arrays) + (jax.ShapeDtypeStruct((8, 128), F32),),
        in_specs=[hbm] * n, out_specs=(sem, sem) + (hbm,) * n + (_vmem_spec(),),
        input_output_aliases={i: 2 + i for i in range(n)},
        compiler_params=pltpu.CompilerParams(has_side_effects=pltpu.SideEffectType.DATAFLOW_SIDE_EFFECTING),
    )(*[pltpu.with_memory_space_constraint(a, pltpu.HBM) for a in arrays])
    return (out[0], out[1], tuple(out[2:2 + n])), out[-1]


def _split_wait(name, handle, n_copies, plan, after):
    send_sems, recv_sems, arrays = handle
    n = len(arrays)

    def body(*refs):
        ins, s_sems, r_sems = refs[:n], refs[n], refs[n + 1]
        for k, (src, dst, to, landed) in enumerate(plan(ins)):
            cp = pltpu.make_async_remote_copy(src_ref=src, dst_ref=landed, send_sem=s_sems.at[k],
                                              recv_sem=r_sems.at[k], device_id=to, device_id_type=MESH_ID)
            cp.wait_send()
            cp.wait_recv()

    hbm = pl.BlockSpec(memory_space=pltpu.HBM)
    sem = pl.BlockSpec(memory_space=pltpu.SEMAPHORE)
    out = pl.pallas_call(
        body, name=name,
        out_shape=tuple(pltpu.HBM(a.shape, a.dtype) for a in arrays),
        in_specs=[hbm] * n + [sem, sem, _any_spec()], out_specs=(hbm,) * n,
        input_output_aliases={i: i for i in range(n)},
        compiler_params=pltpu.CompilerParams(has_side_effects=pltpu.SideEffectType.DATAFLOW_SIDE_EFFECTING),
    )(*arrays, send_sems, recv_sems, after)
    return tuple(out)


def _gather_plans(axes):
    n = len(axes)

    def stage_one(refs):
        x, y, c = _position()
        copies = []
        for a, axis in enumerate(axes):
            for px, py in _other_chips(x, y):
                copies.append((_half(refs[a], c, axis), _half(refs[n + a].at[2 * x + y], c, axis),
                               (px, py, c), _half(refs[n + a].at[2 * px + py], c, axis)))
        return copies

    def stage_two(refs):
        x, y, c = _position()
        copies = []
        for a, axis in enumerate(axes):
            for px, py in _other_chips(x, y):
                piece = _half(refs[n + a].at[2 * px + py], c, axis)
                copies.append((piece, piece, (x, y, 1 - c), _half(refs[n + a].at[2 * px + py], 1 - c, axis)))
        return copies

    return stage_one, stage_two


def _pair_swap_plan(axes):
    n = len(axes)

    def plan(refs):
        x, y, c = _position()
        return [(_half(refs[a], 1 - c, axes[a]), refs[n + a], (x, y, 1 - c), refs[n + a]) for a in range(n)]

    return plan


def _chip_swap_plan(n):
    def plan(refs):
        x, y, c = _position()
        copies = []
        for a in range(n):
            for j, (px, py) in enumerate(_other_chips(x, y)):
                copies.append((refs[a].at[2 * px + py], refs[n + a].at[j], (px, py, c), refs[n + a].at[j]))
        return copies

    return plan


def _pair_join_plan(axes):
    def plan(refs):
        x, y, c = _position()
        copies = []
        for a, axis in enumerate(axes):
            mine = _half(refs[a], c, axis)
            copies.append((mine, mine, (x, y, 1 - c), _half(refs[a], 1 - c, axis)))
        return copies

    return plan


def _pair_add_call(g, got, pos, name, axis):
    rows, cols = got.shape[1], got.shape[2]
    tr = min(512, rows) if axis == ROWS else rows
    nblk = rows // tr
    if axis == ROWS:
        mine = lambda j, i, p: (j, p[1] * nblk + i, 0)
    else:
        mine = lambda j, i, p: (j, 0, p[1])

    def body(pos_r, g_r, got_r, o_r):
        o_r[...] = (g_r[...] + got_r[...]).astype(o_r.dtype)

    return pl.pallas_call(
        body, name=name,
        grid_spec=pltpu.PrefetchScalarGridSpec(
            num_scalar_prefetch=1, grid=(N_CHIPS, nblk),
            in_specs=[pl.BlockSpec((None, tr, cols), mine),
                      pl.BlockSpec((None, tr, cols), lambda j, i, p: (j, i, 0))],
            out_specs=pl.BlockSpec((None, tr, cols), lambda j, i, p: (j, i, 0))),
        out_shape=jax.ShapeDtypeStruct(got.shape, COMM_DTYPE),
        compiler_params=_params(("arbitrary", "arbitrary"), VMEM_BIG),
    )(pos, g, got)


def _chip_add_call(hsum, got, pos, name, axis):
    rows, cols = hsum.shape[1], hsum.shape[2]
    tr = min(512, rows) if axis == ROWS else rows
    nblk = rows // tr
    if axis == ROWS:
        out_shape, mine = (2 * rows, cols), (lambda i, p: (p[1] * nblk + i, 0))
    else:
        out_shape, mine = (rows, 2 * cols), (lambda i, p: (0, p[1]))

    def body(pos_r, own_r, got_r, o_r):
        acc = own_r[...].astype(F32)
        for j in range(3):
            acc = acc + got_r[j].astype(F32)
        o_r[...] = acc

    return pl.pallas_call(
        body, name=name,
        grid_spec=pltpu.PrefetchScalarGridSpec(
            num_scalar_prefetch=1, grid=(nblk,),
            in_specs=[pl.BlockSpec((None, tr, cols), lambda i, p: (p[0], i, 0)),
                      pl.BlockSpec((3, tr, cols), lambda i, p: (0, i, 0))],
            out_specs=pl.BlockSpec((tr, cols), mine)),
        out_shape=jax.ShapeDtypeStruct(out_shape, F32),
        compiler_params=_params(("arbitrary",), VMEM_BIG),
    )(pos, hsum, got)


SMALL_NAMES = ("norm_mix_pre", "norm_mix_post", "norm_mlp_pre", "norm_mlp_post", "b_gate_fwd", "b_gate_bwd",
               "gla_norm", "swa_sink", "rel_bias")


def _small_update_call(grads, gate_grads, params, dep=None):
    n_dev = 8
    n_small = len(SMALL_NAMES)
    wmv = [t for p in params for t in p]
    shapes = [p[0].shape for p in params]

    def body(*refs):
        g_refs = refs[:n_small + 3]
        wmv_refs = refs[n_small + 3:n_small + 3 + 3 * n_small]
        n_in = n_small + 3 + 3 * n_small
        out_refs = refs[n_in:n_in + 4 * n_small + 3]
        pack_a, pack_b, all_a, all_b, send_sems, recv_sems = refs[n_in + 4 * n_small + 3:]
        x, y, c = _position()
        me = 4 * x + 2 * y + c
        pack_a[...] = jnp.zeros_like(pack_a)
        pack_b[...] = jnp.zeros_like(pack_b)
        for i in range(4):
            pack_a[i:i + 1, :] = g_refs[i][...]
        pack_a[4:5, 0:256] = g_refs[4][...]
        pack_a[5:6, 0:256] = g_refs[5][...]
        pack_a[6:7, 0:128] = g_refs[6][...]
        pack_a[7:8, 0:128] = g_refs[7][...]
        pack_a[7:8, 128:256] = g_refs[11][...]
        pack_b[0:32, 0:128] = g_refs[8][...]
        pack_b[32:48, :] = g_refs[9][...]
        pack_b[48:64, :] = g_refs[10][...]
        all_a[me] = pack_a[...]
        all_b[me] = pack_b[...]
        copies = []
        for k in range(1, n_dev):
            fx, fy, fc = (k >> 2) & 1, (k >> 1) & 1, k & 1
            to = (1 - x if fx else x, 1 - y if fy else y, 1 - c if fc else c)
            for t, (pack, dst) in enumerate(((pack_a, all_a), (pack_b, all_b))):
                copies.append(pltpu.make_async_remote_copy(
                    src_ref=pack, dst_ref=dst.at[me], send_sem=send_sems.at[2 * (k - 1) + t],
                    recv_sem=recv_sems.at[2 * (k - 1) + t], device_id=to, device_id_type=MESH_ID))
        for cp in copies:
            cp.start()
        for cp in copies:
            cp.wait()
        sum_a, sum_b = all_a[0], all_b[0]
        for d in range(1, n_dev):
            sum_a = sum_a + all_a[d]
            sum_b = sum_b + all_b[d]
        gsum = [sum_a[0:1], sum_a[1:2], sum_a[2:3], sum_a[3:4], sum_a[4:5, 0:256], sum_a[5:6, 0:256],
                sum_a[6:7, 0:128], sum_a[7:8, 0:SWA_Q_HEADS], sum_b[0:32, 0:SWA_Q_HEADS]]
        for i in range(n_small):
            w_r, m_r, v_r = wmv_refs[3 * i:3 * i + 3]
            delta, new_m, new_v = _adamw_math(w_r[...], gsum[i], m_r[...], v_r[...])
            out_refs[4 * i][...] = gsum[i]
            out_refs[4 * i + 1][...] = delta
            out_refs[4 * i + 2][...] = new_m
            out_refs[4 * i + 3][...] = new_v
        out_refs[4 * n_small][...] = sum_b[32:48]
        out_refs[4 * n_small + 1][...] = sum_b[48:64]
        out_refs[4 * n_small + 2][...] = sum_a[7:8, 128:256]

    n_in = n_small + 3 + 3 * n_small
    body, extra, extra_specs = _after(body, n_in, dep)
    out_shape = [jax.ShapeDtypeStruct(s, F32) for s in shapes for _ in range(4)]
    out_shape += [jax.ShapeDtypeStruct((GLA_GATE_RANK, 256), F32)] * 2 + [jax.ShapeDtypeStruct((1, 128), F32)]
    out = pl.pallas_call(
        body, name="small_update",
        in_specs=[_whole_spec(a.shape) for a in list(grads) + list(gate_grads) + wmv] + extra_specs,
        out_specs=[_whole_spec(s.shape) for s in out_shape],
        out_shape=out_shape,
        scratch_shapes=[pltpu.VMEM((8, D_MODEL), F32), pltpu.VMEM((64, 256), F32),
                        pltpu.VMEM((n_dev, 8, D_MODEL), F32), pltpu.VMEM((n_dev, 64, 256), F32),
                        pltpu.SemaphoreType.DMA((2 * (n_dev - 1),)), pltpu.SemaphoreType.DMA((2 * (n_dev - 1),))],
    )(*grads, *gate_grads, *wmv, *extra)
    per_name = [tuple(out[4 * i:4 * i + 4]) for i in range(n_small)]
    return per_name, out[4 * n_small], out[4 * n_small + 1], out[4 * n_small + 2]


def _pad_heads(t, n_heads, axis=-1):
    axis = axis % t.ndim
    shape = t.shape
    t = t.reshape(shape[:axis] + (n_heads, 64) + shape[axis + 1:])
    pad = [(0, 0)] * t.ndim
    pad[axis + 1] = (0, HEAD_PAD - 64)
    return jnp.pad(t, pad).reshape(shape[:axis] + (n_heads * HEAD_PAD,) + shape[axis + 1:])


def _unpad_heads(t, n_heads, axis=-1):
    axis = axis % t.ndim
    shape = t.shape
    t = t.reshape(shape[:axis] + (n_heads, HEAD_PAD) + shape[axis + 1:])
    t = lax.slice_in_dim(t, 0, 64, axis=axis + 1)
    return t.reshape(shape[:axis] + (n_heads * 64,) + shape[axis + 1:])


def _pad_gate(w, first_row):
    return jnp.pad(_pad_heads(w, 4), ((first_row, 128 - GLA_GATE_RANK - first_row), (0, 0)))


def _own_slot(shard, chip):
    zone = lax.empty((N_CHIPS,) + shard.shape, shard.dtype)
    return lax.dynamic_update_slice(zone, shard[None], (chip,) + (0,) * shard.ndim)


def _reduce_to_owners(grads, axes, pos, tag, overlap):
    n = len(grads)

    def half_shape(g, axis):
        return (N_CHIPS, g.shape[1] // 2, g.shape[2]) if axis == ROWS else (N_CHIPS, g.shape[1], g.shape[2] // 2)

    lands = [lax.empty(half_shape(g, axis), F32) for g, axis in zip(grads, axes)]
    handle, token = _split_start(tag + "_pair_start", list(grads) + lands, n, _pair_swap_plan(axes))
    got = _split_wait(tag + "_pair_wait", handle, n, _pair_swap_plan(axes), overlap[0](token))
    sums = [_pair_add_call(got[a], got[n + a], pos, f"{tag}_pair_add{a}", axes[a]) for a in range(n)]
    lands = [lax.empty((3,) + s.shape[1:], s.dtype) for s in sums]
    handle, token = _split_start(tag + "_chip_start", sums + lands, 3 * n, _chip_swap_plan(n))
    got = _split_wait(tag + "_chip_wait", handle, 3 * n, _chip_swap_plan(n), overlap[1](token))
    halves = [_chip_add_call(got[a], got[n + a], pos, f"{tag}_chip_add{a}", axes[a]) for a in range(n)]
    handle, token = _split_start(tag + "_join_start", halves, n, _pair_join_plan(axes))
    return _split_wait(tag + "_join_wait", handle, n, _pair_join_plan(axes), overlap[2](token))


def kernel(x, norm_mix_pre, w_in, w_gate_up_fwd, b_gate_fwd, w_gate_up_bwd, b_gate_bwd, gla_norm, swa_sink, rel_bias, w_out, norm_mix_post, norm_mlp_pre, w_up, w_down, norm_mlp_post, loss_target, m_norm_mix_pre, m_w_in, m_w_gate_up_fwd, m_b_gate_fwd, m_w_gate_up_bwd, m_b_gate_bwd, m_gla_norm, m_swa_sink, m_rel_bias, m_w_out, m_norm_mix_post, m_norm_mlp_pre, m_w_up, m_w_down, m_norm_mlp_post, v_norm_mix_pre, v_w_in, v_w_gate_up_fwd, v_b_gate_fwd, v_w_gate_up_bwd, v_b_gate_bwd, v_gla_norm, v_swa_sink, v_rel_bias, v_w_out, v_norm_mix_post, v_norm_mlp_pre, v_w_up, v_w_down, v_norm_mlp_post):
    given = dict(locals())
    cx, cy, cc = _position()
    chip = (2 * cx + cy).astype(jnp.int32)
    pos = jnp.stack([chip, cc.astype(jnp.int32)])
    seq, tgt = x[0], loss_target[0]
    L = seq.shape[0]

    gates = jnp.concatenate([w_gate_up_fwd[0], w_gate_up_bwd[0]], axis=0).astype(COMM_DTYPE)
    all_in, all_gates = _first_gather_call([w_in[0].T.astype(COMM_DTYPE), gates], [COLS, ROWS])
    rest = [w_out[0].astype(COMM_DTYPE), jnp.stack([w_up[0], w_down[0]]).astype(COMM_DTYPE)]
    stage_one, stage_two = _gather_plans([ROWS, ROWS])
    handle, token = _split_start("gather_chip_start", rest + [_own_slot(s, chip) for s in rest] + [all_gates], 6,
                                 stage_one)

    w_in_t = _mx(all_in.reshape(IN_COLS, D_MODEL))
    gates_full = jnp.concatenate([all_gates[j] for j in range(N_CHIPS)], axis=1)
    wgf_p = _mx(_pad_gate(gates_full[:GLA_GATE_RANK], 0))
    wgb_p = _mx(_pad_gate(gates_full[GLA_GATE_RANK:], GLA_GATE_RANK))
    bf_p, bb_p = _pad_heads(b_gate_fwd, 4), _pad_heads(b_gate_bwd, 4)
    buckets = jnp.asarray(_band_buckets())
    bias = _bias_call(rel_bias, buckets)
    sink1 = swa_sink.reshape(SWA_Q_HEADS)

    qa, ka, va, ga, qs, ks, vs, za = _proj_call(seq, norm_mix_pre, w_in_t, dep=token)
    halo = ((SWA_BLOCK, SWA_BLOCK), (0, 0))
    ks_p, vs_p = jnp.pad(ks, halo), jnp.pad(vs, halo)
    o_f, o_b, s_f, s_b = _gla_fwd_call(qa, ka, va, za, wgf_p, bf_p, wgb_p, bb_p)
    arrays = _split_wait("gather_chip_wait", handle, 6, stage_one, o_f)
    handle, token = _split_start("gather_pair_start", list(arrays), 6, stage_two)
    o_s = _swa_fwd_call(qs, ks_p, vs_p, bias, sink1, dep=token)
    arrays = _split_wait("gather_pair_wait", handle, 6, stage_two, o_s)
    w_out_full = _mx(arrays[2].reshape(N_CHIPS * R_OUT, D_MODEL))
    w_ud = _mx(arrays[3])
    cat, mix, h1, n2 = _mix_call(o_f, o_b, ga, o_s, seq, gla_norm, w_out_full, norm_mix_post, norm_mlp_pre)
    a, rz, dh2, dff, loss, d_post2 = _mlp_fwd_call(n2, h1, tgt, w_ud, norm_mlp_post)

    dz, dn2 = _mlp_bwd_call(dff, rz, w_ud)
    dw_down, dw_up4 = _mlp_wgrad_call(a, dff, n2, dz)
    dh1, do, dga, dos, dw_out, d_pre2, d_post, d_gn = _mix_bwd_call(
        dn2, dh2, h1, mix, cat, o_f, o_b, ga, gla_norm, norm_mix_post, norm_mlp_pre, w_out_full)
    done = {}

    def swa_backward(tok):
        done["swa"] = _swa_bwd_call(qs, ks_p, vs_p, bias, sink1, dos, dep=tok)
        return done["swa"][0]

    def gla_in_backward(tok):
        done["gla"] = _gla_bwd_call(qa, ka, va, za, do, s_f, s_b, wgf_p, bf_p, wgb_p, bb_p, dep=tok)
        dqf, dkf, dvf, dzf, _, _, dqb, dkb, dvb, dzb, _, _ = done["gla"]
        dqs, dks_p, dvs_p, _, _ = done["swa"]
        done["in"] = _in_bwd_call(
            seq, dh1, norm_mix_pre, w_in_t,
            pairs=[(T_QA, (dqf, dqb)), (T_KA, (dkf, dkb)), (T_VA, (dvf, dvb)), (T_ZA, (dzf, dzb))],
            singles=[(T_GA, dga), (T_QS, dqs)], halos=[(T_KS, dks_p), (T_VS, dvs_p)])
        return done["in"][0]

    def bias_backward(tok):
        done["rel"] = _relbias_call(done["swa"][3], done["swa"][4], buckets, dep=tok)
        return done["rel"][0]

    g_up, g_down, g_out = _reduce_to_owners(
        [dw_up4, dw_down.reshape(N_CHIPS, R_DOWN, D_MODEL), dw_out.reshape(N_CHIPS, R_OUT, D_MODEL)],
        [ROWS, ROWS, ROWS], pos, "mlp", [swa_backward, gla_in_backward, bias_backward])
    dx, dw_in_t, d_pre = done["in"]
    dwf, dbf, dwb, dbb = done["gla"][4], done["gla"][5], done["gla"][10], done["gla"][11]
    drel, dsink = done["rel"]

    small_grads = [d_pre, d_post, d_pre2, d_post2, _unpad_heads(dbf, 4), _unpad_heads(dbb, 4), d_gn, dsink, drel]
    gate_grads = [_unpad_heads(dwf[:GLA_GATE_RANK], 4), _unpad_heads(dwb[GLA_GATE_RANK:2 * GLA_GATE_RANK], 4)]
    small_params = [(given[n], given["m_" + n], given["v_" + n]) for n in SMALL_NAMES]
    upd = {}

    def update_up(tok):
        upd["w_up"] = (g_up,) + tuple(_adamw_call(w_up[0], g_up, m_w_up[0], v_w_up[0], "adamw_w_up", dep=tok))
        return upd["w_up"][1]

    def update_small(tok):
        per_name, gf_sum, gb_sum, upd["loss"] = _small_update_call(small_grads, gate_grads + [loss], small_params,
                                                                   dep=tok)
        upd.update(dict(zip(SMALL_NAMES, per_name)))
        for name, total in (("w_gate_up_fwd", gf_sum), ("w_gate_up_bwd", gb_sum)):
            g = lax.dynamic_slice(total, (0, chip * 64), (GLA_GATE_RANK, 64))
            upd[name] = (g,) + tuple(_adamw_call(given[name][0], g, given["m_" + name][0], given["v_" + name][0],
                                                 "adamw_" + name))
        upd["w_down"] = (g_down,) + tuple(
            _adamw_call(w_down[0], g_down, m_w_down[0], v_w_down[0], "adamw_w_down", dep=gf_sum))
        return upd["w_down"][1]

    def update_out(tok):
        upd["w_out"] = (g_out,) + tuple(_adamw_call(w_out[0], g_out, m_w_out[0], v_w_out[0], "adamw_w_out", dep=tok))
        return upd["w_out"][1]

    (g_in_t,) = _reduce_to_owners([dw_in_t.reshape(N_CHIPS, R_IN, D_MODEL)], [COLS], pos, "in",
                                  [update_up, update_small, update_out])
    in_t = (g_in_t,) + tuple(_adamw_call(w_in[0].T, g_in_t, m_w_in[0].T, v_w_in[0].T, "adamw_w_in"))
    upd["w_in"] = tuple(t.T for t in in_t)

    big = ("w_in", "w_gate_up_fwd", "w_gate_up_bwd", "w_out", "w_up", "w_down")
    names = ["norm_mix_pre", "w_in", "w_gate_up_fwd", "b_gate_fwd", "w_gate_up_bwd", "b_gate_bwd", "gla_norm",
             "swa_sink", "rel_bias", "w_out", "norm_mix_post", "norm_mlp_pre", "w_up", "w_down", "norm_mlp_post"]
    outs = [upd["loss"][0, 0], dx[None]]
    for kind in range(4):
        outs += [upd[n][kind][None] if n in big else upd[n][kind] for n in names]
    return tuple(outs)
```

```python
import math

import numpy as np
import jax
import jax.numpy as jnp
from jax import lax
from jax.experimental import pallas as pl
from jax.experimental.pallas import tpu as pltpu

F32 = jnp.float32
MXU_DTYPE = jnp.bfloat16
COMM_DTYPE = jnp.bfloat16

D_MODEL = 1024
D_FF = 4096
N_CHIPS = 4
GLA_HEADS = 4
GLA_CHUNK = 64
GLA_GATE_RANK = 16
GLA_GATE_NORM = 16.0
SWA_Q_HEADS = 8
SWA_KV_HEADS = 2
SWA_BLOCK = 128
REL_BUCKETS = 32
REL_MAX_DIST = 128
NORM_EPS = 1e-6
HEAD_PAD = 128

ADAM_LR = 0.001
ADAM_B1 = 0.9
ADAM_B2 = 0.999
ADAM_EPS = 1e-08
ADAM_WD = 0.01
ADAM_STEP = 10

OUT_PAD = 1024

R_IN, R_OUT, R_UP, R_DOWN = 584, 256, 1024, 1024

VMEM_BIG = 56 * 1024 * 1024
MESH_AXES = ("x", "y", "c")
MESH_ID = pl.DeviceIdType.MESH


def _mx(a):
    return a.astype(MXU_DTYPE)


def _dot(a, b):
    return jnp.dot(a, b, preferred_element_type=F32)


def _dot_nt(a, b):
    return lax.dot_general(a, b, (((1,), (1,)), ((), ())), preferred_element_type=F32)


def _dot_tn(a, b):
    return lax.dot_general(a, b, (((0,), (0,)), ((), ())), preferred_element_type=F32)


def _rms_r(x):
    return lax.rsqrt(jnp.mean(x * x, axis=-1, keepdims=True) + NORM_EPS)


def _rms_bwd(x, r, g, dy):
    xh = x * r
    gdy = dy * g
    dx = r * (gdy - xh * jnp.mean(gdy * xh, axis=-1, keepdims=True))
    return dx, jnp.sum(dy * xh, axis=0, keepdims=True)


def _low_half(rows):
    return lax.broadcasted_iota(jnp.int32, (rows, HEAD_PAD), 1) < 64


def _spread_heads(x):
    low = _low_half(x.shape[0])
    parts = []
    for p in range(x.shape[1] // HEAD_PAD):
        pair = x[:, HEAD_PAD * p:HEAD_PAD * (p + 1)]
        parts += [jnp.where(low, pair, 0.0), jnp.where(low, pltpu.roll(pair, 64, 1), 0.0)]
    return jnp.concatenate(parts, axis=1)


def _squeeze_heads(x):
    low = _low_half(x.shape[0])
    parts = []
    for p in range(x.shape[1] // (2 * HEAD_PAD)):
        even = x[:, 2 * HEAD_PAD * p:2 * HEAD_PAD * p + HEAD_PAD]
        odd = x[:, 2 * HEAD_PAD * p + HEAD_PAD:2 * HEAD_PAD * (p + 1)]
        parts.append(jnp.where(low, even, pltpu.roll(odd, 64, 1)))
    return parts[0] if len(parts) == 1 else jnp.concatenate(parts, axis=1)


def _params(sem=None, vmem=None):
    kw = {}
    if sem is not None:
        kw["dimension_semantics"] = sem
    if vmem is not None:
        kw["vmem_limit_bytes"] = vmem
    return pltpu.CompilerParams(**kw)


def _vmem_spec():
    return pl.BlockSpec(memory_space=pltpu.VMEM)


def _whole_spec(shape):
    return pl.BlockSpec(shape, lambda: (0,) * len(shape))


def _row_spec(tm, width):
    return pl.BlockSpec((tm, width), lambda i: (i, 0))


def _full_spec(shape):
    return pl.BlockSpec(shape, lambda i: (0,) * len(shape))


def _any_spec():
    return pl.BlockSpec(memory_space=pl.ANY)


def _after(body, n_in, dep):
    if dep is None:
        return body, [], []
    return (lambda *refs: body(*refs[:n_in], *refs[n_in + 1:])), [dep], [_any_spec()]


T_QA, T_KA, T_VA, T_GA = (0, 256, 4), (256, 256, 4), (512, 512, 0), (1024, 512, 0)
T_QS, T_KS, T_VS = (1568, 512, 8), (2080, 128, 2), (2208, 128, 2)
T_ZA = (1536, 128, 0)
ZA_COLS = 2 * GLA_GATE_RANK
IN_COLS = 2336


def _side_by_side(group):
    return group[0], group[1], 0


def _proj_call(x, g_pre, w_in_t, dep=None):
    L = x.shape[0]
    tm = min(512, L)
    groups = [(T_QA, F32), (T_KA, F32), (T_VA, MXU_DTYPE), (T_GA, F32),
              (T_QS, MXU_DTYPE), (T_KS, MXU_DTYPE), (T_VS, MXU_DTYPE), (T_ZA, F32)]
    widths = [rows * (2 if heads else 1) for (_, rows, heads), _ in groups]

    def body(x_ref, g_ref, w_ref, *outs):
        xv = x_ref[...]
        u = _mx(xv * _rms_r(xv) * g_ref[...])
        for ref, (grp, _) in zip(outs, groups):
            first, rows, heads = grp
            val = _dot_nt(u, w_ref[first:first + rows, :])
            if heads:
                val = _spread_heads(val)
            if grp is T_ZA:
                val = jnp.where(lax.broadcasted_iota(jnp.int32, val.shape, 1) < ZA_COLS, val, 0.0)
            if grp is T_QS:
                val = val * 0.125
            ref[...] = val.astype(ref.dtype)

    body, extra, extra_specs = _after(body, 3, dep)
    return pl.pallas_call(
        body, name="proj_fwd", grid=(L // tm,),
        in_specs=[_row_spec(tm, D_MODEL), _full_spec((1, D_MODEL)), _vmem_spec()] + extra_specs,
        out_specs=[_row_spec(tm, w) for w in widths],
        out_shape=[jax.ShapeDtypeStruct((L, w), dt) for w, (_, dt) in zip(widths, groups)],
        compiler_params=_params(("arbitrary",), VMEM_BIG),
    )(x, g_pre, w_in_t, *extra)


def _tri_masks():
    row = lax.broadcasted_iota(jnp.int32, (GLA_CHUNK, GLA_CHUNK), 0)
    col = lax.broadcasted_iota(jnp.int32, (GLA_CHUNK, GLA_CHUNK), 1)
    return row >= col, row <= col


def _chunk_sums(tri_m, x):
    hi = _mx(x)
    rest = x - hi.astype(F32)
    mid = _mx(rest)
    lo = _mx(rest - mid.astype(F32))
    return _dot(tri_m, hi) + _dot(tri_m, mid) + _dot(tri_m, lo)


def _gla_block_pre(q_r, k_r, z_r, w_r, b_r, rev, nc, qd_s, ki_s, ks_s, dec_s, keep=None):
    tri_f, tri_b = _tri_masks()
    tri_m = _mx((tri_b if rev else tri_f).astype(F32))
    g = _dot(_mx(z_r[...]), w_r[...]) + b_r[...]
    la = (jnp.minimum(g, 0.0) - jnp.log(1.0 + jnp.exp(-jnp.abs(g)))) / GLA_GATE_NORM
    sums, lasts = [], []
    for c in range(nc):
        b_c = _chunk_sums(tri_m, la[GLA_CHUNK * c:GLA_CHUNK * (c + 1)])
        blast = b_c[0:1] if rev else b_c[GLA_CHUNK - 1:GLA_CHUNK]
        dec_s[c] = jnp.exp(blast)
        sums.append(b_c)
        lasts.append(jnp.broadcast_to(blast, b_c.shape))
    b = jnp.concatenate(sums, axis=0)
    eb = jnp.exp(b)
    enb = jnp.exp(-b)
    elb = jnp.exp(jnp.concatenate(lasts, axis=0) - b)
    k = k_r[...]
    qd_s[...] = (q_r[...] * 0.125 * eb).astype(qd_s.dtype)
    ki_s[...] = (k * enb).astype(ki_s.dtype)
    ks_s[...] = (k * elb).astype(ks_s.dtype)
    if keep is not None:
        for ref, val in zip(keep, (g, eb, enb, elb)):
            ref[...] = val


def _gla_fwd_call(qa, ka, va, za, wgf, bgf, wgb, bgb):
    L = qa.shape[0]
    br = min(512, L)
    nb, nc, n_chunks = L // br, br // GLA_CHUNK, L // GLA_CHUNK
    hw = GLA_HEADS * HEAD_PAD

    def body(qaf, kaf, vaf, zaf, qab, kab, vab, zab, wgf_r, bgf_r, wgb_r, bgb_r,
             of_r, ob_r, sf_r, sb_r, st_f, st_b, pre_f, pre_b):
        @pl.when(pl.program_id(0) == 0)
        def _():
            st_f[...] = jnp.zeros_like(st_f)
            st_b[...] = jnp.zeros_like(st_b)

        _gla_block_pre(qaf, kaf, zaf, wgf_r, bgf_r, False, nc, *pre_f)
        _gla_block_pre(qab, kab, zab, wgb_r, bgb_r, True, nc, *pre_b)
        tri_f, tri_b = _tri_masks()

        def one(tri, pre, v_r, o_r, s_r, st, ci):
            qd_s, ki_s, ks_s, dec_s = pre
            rows = pl.ds(pl.multiple_of(ci * GLA_CHUNK, GLA_CHUNK), GLA_CHUNK)
            dec = dec_s[ci]
            heads = range(GLA_HEADS)
            lanes = [slice(HEAD_PAD * h, HEAD_PAD * (h + 1)) for h in heads]
            qd = [qd_s[rows, sl] for sl in lanes]
            v = [v_r[rows, sl] for sl in lanes]
            s_t = [st[h] for h in heads]
            a = [_dot_nt(qd[h], ki_s[rows, lanes[h]]) for h in heads]
            carried = [_dot_nt(qd[h], _mx(s_t[h])) for h in heads]
            grown = [_dot_tn(v[h], ks_s[rows, lanes[h]]) for h in heads]
            a = [_mx(jnp.where(tri, a[h], 0.0)) for h in heads]
            inner = [_dot(a[h], v[h]) for h in heads]
            for h in heads:
                s_r[ci, h] = s_t[h].astype(s_r.dtype)
                o_r[rows, lanes[h]] = inner[h] + carried[h]
                st[h] = s_t[h] * dec[:, lanes[h]] + grown[h]

        def loop(t, carry):
            one(tri_f, pre_f, vaf, of_r, sf_r, st_f, t)
            one(tri_b, pre_b, vab, ob_r, sb_r, st_b, nc - 1 - t)
            return carry

        lax.fori_loop(0, nc, loop, 0, unroll=True)

    fwd = lambda i: (i, 0)
    bwd = lambda i: (nb - 1 - i, 0)
    ins = lambda m: [pl.BlockSpec((br, hw), m), pl.BlockSpec((br, hw), m),
                     pl.BlockSpec((br, hw), m), pl.BlockSpec((br, 128), m)]
    wspecs = [_full_spec((128, hw)), _full_spec((1, hw))] * 2
    s_shape = (nc, GLA_HEADS, HEAD_PAD, HEAD_PAD)
    pre_scratch = [pltpu.VMEM((br, hw), MXU_DTYPE)] * 3 + [pltpu.VMEM((nc, 1, hw), F32)]
    return pl.pallas_call(
        body, name="gla_fwd", grid=(nb,),
        in_specs=ins(fwd) + ins(bwd) + wspecs,
        out_specs=[pl.BlockSpec((br, hw), fwd), pl.BlockSpec((br, hw), bwd),
                   pl.BlockSpec(s_shape, lambda i: (i, 0, 0, 0)),
                   pl.BlockSpec(s_shape, lambda i: (nb - 1 - i, 0, 0, 0))],
        out_shape=[jax.ShapeDtypeStruct((L, hw), F32), jax.ShapeDtypeStruct((L, hw), F32),
                   jax.ShapeDtypeStruct((n_chunks,) + s_shape[1:], MXU_DTYPE),
                   jax.ShapeDtypeStruct((n_chunks,) + s_shape[1:], MXU_DTYPE)],
        scratch_shapes=[pltpu.VMEM(s_shape[1:], F32), pltpu.VMEM(s_shape[1:], F32), pre_scratch, pre_scratch],
        compiler_params=_params(("arbitrary",), VMEM_BIG),
    )(qa, ka, va, za, qa, ka, va, za, wgf, bgf, wgb, bgb)


def _gla_bwd_call(qa, ka, va, za, do, sf, sb, wgf, bgf, wgb, bgb, dep=None):
    L = qa.shape[0]
    br = min(256, L)
    nb, nc = L // br, br // GLA_CHUNK
    hw = GLA_HEADS * HEAD_PAD

    def body(qaf, kaf, vaf, zaf, dof, sf_r, qab, kab, vab, zab, dob, sb_r, wgf_r, bgf_r, wgb_r, bgb_r,
             dqf, dkf, dvf, dzf, dwf, dbf, dqb, dkb, dvb, dzb, dwb, dbb, gt_f, gt_b, pre_f, pre_b):
        @pl.when(pl.program_id(0) == 0)
        def _():
            for ref in (gt_f, gt_b, dwf, dbf, dwb, dbb):
                ref[...] = jnp.zeros_like(ref)

        _gla_block_pre(qaf, kaf, zaf, wgf_r, bgf_r, False, nc, *pre_f[:4], keep=pre_f[4:8])
        _gla_block_pre(qab, kab, zab, wgb_r, bgb_r, True, nc, *pre_b[:4], keep=pre_b[4:8])
        tri_f, tri_b = _tri_masks()
        row_w = lax.broadcasted_iota(jnp.int32, (GLA_CHUNK, HEAD_PAD), 0)

        def one(rev, pre, q_r, k_r, v_r, do_r, s_r, dq_r, dk_r, dv_r, gt, ci):
            qd_s, ki_s, ks_s, dec_s, _, eb_s, enb_s, elb_s, db_s = pre
            tri = tri_b if rev else tri_f
            last_row = 0 if rev else GLA_CHUNK - 1
            rows = pl.ds(pl.multiple_of(ci * GLA_CHUNK, GLA_CHUNK), GLA_CHUNK)
            dec = dec_s[ci]
            heads = range(GLA_HEADS)
            lanes = [slice(HEAD_PAD * h, HEAD_PAD * (h + 1)) for h in heads]
            qd = [qd_s[rows, sl] for sl in lanes]
            ki = [ki_s[rows, sl] for sl in lanes]
            ks = [ks_s[rows, sl] for sl in lanes]
            v = [v_r[rows, sl] for sl in lanes]
            do_h = [_mx(do_r[rows, sl]) for sl in lanes]
            s_t = [s_r[ci, h] for h in heads]
            g_t = [gt[h] for h in heads]
            g_m = [_mx(g_t[h]) for h in heads]
            a = [_dot_nt(qd[h], ki[h]) for h in heads]
            da = [_dot_nt(do_h[h], v[h]) for h in heads]
            dv_carried = [_dot_nt(ks[h], g_m[h]) for h in heads]
            dqd_carried = [_dot(do_h[h], _mx(s_t[h])) for h in heads]
            dks = [_dot(v[h], g_m[h]) for h in heads]
            g_grown = [_dot_tn(do_h[h], qd[h]) for h in heads]
            a = [_mx(jnp.where(tri, a[h], 0.0)) for h in heads]
            da = [_mx(jnp.where(tri, da[h], 0.0)) for h in heads]
            dv_inner = [_dot_tn(a[h], do_h[h]) for h in heads]
            dqd_inner = [_dot(da[h], ki[h]) for h in heads]
            dki = [_dot_tn(da[h], qd[h]) for h in heads]
            dq, dk = [], []
            for h in heads:
                sl = lanes[h]
                dv_r[rows, sl] = (dv_inner[h] + dv_carried[h]).astype(dv_r.dtype)
                ddec = jnp.sum(g_t[h] * s_t[h].astype(F32), axis=0, keepdims=True)
                gt[h] = g_t[h] * dec[:, sl] + g_grown[h]
                dq.append((dqd_inner[h] + dqd_carried[h]) * eb_s[rows, sl] * 0.125)
                dk_state = dks[h] * elb_s[rows, sl]
                dk.append(dki[h] * enb_s[rows, sl] + dk_state)
                k = k_r[rows, sl]
                dblast = jnp.sum(dk_state * k, axis=0, keepdims=True) + dec[:, sl] * ddec
                db_s[rows, sl] = q_r[rows, sl] * dq[h] - k * dk[h] + jnp.where(row_w == last_row, dblast, 0.0)
            low = _low_half(GLA_CHUNK)
            for pair in range(GLA_HEADS // 2):
                psl = slice(HEAD_PAD * pair, HEAD_PAD * (pair + 1))
                for ref, val in ((dq_r, dq), (dk_r, dk)):
                    both = jnp.where(low, val[2 * pair], pltpu.roll(val[2 * pair + 1], 64, 1))
                    ref[rows, psl] = both.astype(ref.dtype)

        def loop(t, carry):
            one(False, pre_f, qaf, kaf, vaf, dof, sf_r, dqf, dkf, dvf, gt_f, nc - 1 - t)
            one(True, pre_b, qab, kab, vab, dob, sb_r, dqb, dkb, dvb, gt_b, t)
            return carry

        lax.fori_loop(0, nc, loop, 0, unroll=True)

        def gate_grads(rev, pre, z_r, w_r, dz_r, dw_r, dbias_r):
            g_s, db_s = pre[4], pre[8]
            back_m = _mx((tri_f if rev else tri_b).astype(F32))
            db = db_s[...]
            dla = jnp.concatenate([_chunk_sums(back_m, db[GLA_CHUNK * c:GLA_CHUNK * (c + 1)]) for c in range(nc)],
                                  axis=0)
            dg = dla * (1.0 / GLA_GATE_NORM) * (1.0 / (1.0 + jnp.exp(g_s[...])))
            dg_m = _mx(dg)
            dz_r[...] = _dot_nt(dg_m, w_r[...])
            dw_r[...] += _dot_tn(_mx(z_r[...]), dg_m)
            dbias_r[...] += jnp.sum(dg, axis=0, keepdims=True)

        gate_grads(False, pre_f, zaf, wgf_r, dzf, dwf, dbf)
        gate_grads(True, pre_b, zab, wgb_r, dzb, dwb, dbb)

    last_first = lambda i: (nb - 1 - i, 0)
    first_last = lambda i: (i, 0)
    s_shape = (nc, GLA_HEADS, HEAD_PAD, HEAD_PAD)

    def ins(m):
        return [pl.BlockSpec((br, hw), m), pl.BlockSpec((br, hw), m), pl.BlockSpec((br, hw), m),
                pl.BlockSpec((br, 128), m), pl.BlockSpec((br, hw), m),
                pl.BlockSpec(s_shape, lambda i: m(i) + (0, 0))]

    def outs(m):
        return [pl.BlockSpec((br, hw // 2), m), pl.BlockSpec((br, hw // 2), m), pl.BlockSpec((br, hw), m),
                pl.BlockSpec((br, 128), m), _full_spec((128, hw)), _full_spec((1, hw))]

    out_shape = [jax.ShapeDtypeStruct((L, hw // 2), MXU_DTYPE)] * 2 + [
        jax.ShapeDtypeStruct((L, hw), MXU_DTYPE),
        jax.ShapeDtypeStruct((L, 128), F32), jax.ShapeDtypeStruct((128, hw), F32),
        jax.ShapeDtypeStruct((1, hw), F32)]
    wspecs = [_full_spec((128, hw)), _full_spec((1, hw))] * 2
    body, extra, extra_specs = _after(body, 16, dep)
    pre_scratch = ([pltpu.VMEM((br, hw), MXU_DTYPE)] * 3 + [pltpu.VMEM((nc, 1, hw), F32)]
                   + [pltpu.VMEM((br, hw), F32)] * 5)
    return pl.pallas_call(
        body, name="gla_bwd", grid=(nb,),
        in_specs=ins(last_first) + ins(first_last) + wspecs + extra_specs,
        out_specs=outs(last_first) + outs(first_last),
        out_shape=out_shape + out_shape,
        scratch_shapes=[pltpu.VMEM(s_shape[1:], F32), pltpu.VMEM(s_shape[1:], F32), pre_scratch, pre_scratch],
        compiler_params=_params(("arbitrary",), VMEM_BIG),
    )(qa, ka, va, za, do, sf, qa, ka, va, za, do, sb, wgf, bgf, wgb, bgb, *extra)


def _t5_buckets(rel):
    nb = REL_BUCKETS // 2
    ret = (rel > 0).astype(np.int32) * nb
    n = np.abs(rel)
    max_exact = nb // 2
    large = max_exact + (np.log(np.maximum(n, 1).astype(np.float32) / max_exact)
                         / math.log(REL_MAX_DIST / max_exact) * (nb - max_exact)).astype(np.int32)
    large = np.minimum(large, nb - 1)
    return ret + np.where(n < max_exact, n, large)


SWA_GROUP = SWA_Q_HEADS // SWA_KV_HEADS
SWA_SPAN = 3 * SWA_BLOCK
SWA_GROUP_LANES = SWA_GROUP * SWA_BLOCK


def _band_buckets():
    s = np.arange(SWA_SPAN)[:, None]
    c = np.arange(SWA_BLOCK)[None, :]
    return _t5_buckets(s - SWA_BLOCK - c).astype(np.int32)


def _swa_valid(n, seq_len):
    key_pos = (n - 1) * SWA_BLOCK + lax.broadcasted_iota(jnp.int32, (SWA_SPAN, 1), 0)
    return (key_pos >= 0) & (key_pos < seq_len)


def _swa_sink_row(sink_r, kv):
    lane = lax.broadcasted_iota(jnp.int32, (1, SWA_GROUP_LANES), 1)
    row = jnp.full((1, SWA_GROUP_LANES), sink_r[kv * SWA_GROUP], F32)
    for g in range(1, SWA_GROUP):
        row = jnp.where(lane >= g * SWA_BLOCK, sink_r[kv * SWA_GROUP + g], row)
    return row


def _swa_group(ref, kv):
    first = kv * SWA_GROUP
    return jnp.concatenate([ref[:, HEAD_PAD * h:HEAD_PAD * (h + 1)] for h in range(first, first + SWA_GROUP)],
                           axis=0)


def _swa_softmax(scores, bias_t, sink_row, valid):
    st = jnp.where(valid, scores + bias_t, -1e30)
    m = jnp.maximum(jnp.max(st, axis=0, keepdims=True), sink_row)
    p = jnp.exp(st - m)
    e_sink = jnp.exp(sink_row - m)
    inv = 1.0 / (jnp.sum(p, axis=0, keepdims=True) + e_sink)
    return p * inv, e_sink * inv


def _swa_fwd_call(qs, ks, vs, bias, sink, dep=None):
    L = qs.shape[0]

    def body(q_r, k_r, v_r, bias_r, sink_r, o_r):
        n = pl.program_id(0)
        span = pl.ds(pl.multiple_of(n * SWA_BLOCK, SWA_BLOCK), SWA_SPAN)
        valid = _swa_valid(n, L)
        groups = range(SWA_KV_HEADS)
        lanes = [slice(HEAD_PAD * kv, HEAD_PAD * (kv + 1)) for kv in groups]
        scores = [_dot_nt(k_r[span, lanes[kv]], _swa_group(q_r, kv)) for kv in groups]
        probs = [_swa_softmax(scores[kv], bias_r[kv], _swa_sink_row(sink_r, kv), valid)[0] for kv in groups]
        low = _low_half(SWA_BLOCK)
        for kv in groups:
            og = _dot_tn(_mx(probs[kv]), v_r[span, lanes[kv]])
            for pair in range(SWA_GROUP // 2):
                even = og[2 * SWA_BLOCK * pair:2 * SWA_BLOCK * pair + SWA_BLOCK]
                odd = og[2 * SWA_BLOCK * pair + SWA_BLOCK:2 * SWA_BLOCK * (pair + 1)]
                first = HEAD_PAD * (kv * SWA_GROUP // 2 + pair)
                o_r[:, first:first + HEAD_PAD] = jnp.where(low, even, pltpu.roll(odd, 64, 1)).astype(o_r.dtype)

    qw = SWA_Q_HEADS * HEAD_PAD
    body, extra, extra_specs = _after(body, 5, dep)
    return pl.pallas_call(
        body, name="swa_fwd", grid=(L // SWA_BLOCK,),
        in_specs=[_row_spec(SWA_BLOCK, qw), _vmem_spec(), _vmem_spec(), _vmem_spec(),
                  pl.BlockSpec(memory_space=pltpu.SMEM)] + extra_specs,
        out_specs=_row_spec(SWA_BLOCK, qw // 2),
        out_shape=jax.ShapeDtypeStruct((L, qw // 2), MXU_DTYPE),
        compiler_params=_params(("arbitrary",), VMEM_BIG),
    )(qs, ks, vs, bias, sink, *extra)


def _swa_bwd_call(qs, ks, vs, bias, sink, do, dep=None):
    L = qs.shape[0]
    qw = SWA_Q_HEADS * HEAD_PAD
    kw = SWA_KV_HEADS * HEAD_PAD

    def body(q_r, k_r, v_r, bias_r, sink_r, do_r, dq_r, dk_r, dv_r, dbias_r, dsink_r):
        n = pl.program_id(0)

        @pl.when(n == 0)
        def _():
            for ref in (dk_r, dv_r, dbias_r, dsink_r):
                ref[...] = jnp.zeros_like(ref)

        span = pl.ds(pl.multiple_of(n * SWA_BLOCK, SWA_BLOCK), SWA_SPAN)
        valid = _swa_valid(n, L)
        groups = range(SWA_KV_HEADS)
        lanes = [slice(HEAD_PAD * kv, HEAD_PAD * (kv + 1)) for kv in groups]
        kk = [k_r[span, sl] for sl in lanes]
        vv = [v_r[span, sl] for sl in lanes]
        qg = [_swa_group(q_r, kv) for kv in groups]
        dog = [_swa_group(do_r, kv) for kv in groups]
        scores = [_dot_nt(kk[kv], qg[kv]) for kv in groups]
        dp = [_dot_nt(vv[kv], dog[kv]) for kv in groups]
        probs = [_swa_softmax(scores[kv], bias_r[kv], _swa_sink_row(sink_r, kv), valid) for kv in groups]
        ds_m, pn_m = [], []
        for kv in groups:
            pn, p_sink = probs[kv]
            delta = jnp.sum(pn * dp[kv], axis=0, keepdims=True)
            ds = pn * (dp[kv] - delta)
            dsink_r[kv] -= p_sink * delta
            dbias_r[kv] += ds
            ds_m.append(_mx(ds))
            pn_m.append(_mx(pn))
        dqg = [_dot_tn(ds_m[kv], kk[kv]) * 0.125 for kv in groups]
        dkk = [_dot(ds_m[kv], qg[kv]) for kv in groups]
        dvv = [_dot(pn_m[kv], dog[kv]) for kv in groups]
        low = _low_half(SWA_BLOCK)
        for kv in groups:
            for pair in range(SWA_GROUP // 2):
                even = dqg[kv][2 * SWA_BLOCK * pair:2 * SWA_BLOCK * pair + SWA_BLOCK]
                odd = dqg[kv][2 * SWA_BLOCK * pair + SWA_BLOCK:2 * SWA_BLOCK * (pair + 1)]
                first = HEAD_PAD * (kv * SWA_GROUP // 2 + pair)
                dq_r[:, first:first + HEAD_PAD] = jnp.where(low, even, pltpu.roll(odd, 64, 1)).astype(dq_r.dtype)
            dk_r[span, lanes[kv]] += dkk[kv]
            dv_r[span, lanes[kv]] += dvv[kv]

    body, extra, extra_specs = _after(body, 6, dep)
    return pl.pallas_call(
        body, name="swa_bwd", grid=(L // SWA_BLOCK,),
        in_specs=[_row_spec(SWA_BLOCK, qw), _vmem_spec(), _vmem_spec(), _vmem_spec(),
                  pl.BlockSpec(memory_space=pltpu.SMEM), _row_spec(SWA_BLOCK, qw)] + extra_specs,
        out_specs=[_row_spec(SWA_BLOCK, qw // 2), _vmem_spec(), _vmem_spec(), _vmem_spec(), _vmem_spec()],
        out_shape=[jax.ShapeDtypeStruct((L, qw // 2), MXU_DTYPE),
                   jax.ShapeDtypeStruct((L + 2 * SWA_BLOCK, kw), F32),
                   jax.ShapeDtypeStruct((L + 2 * SWA_BLOCK, kw), F32),
                   jax.ShapeDtypeStruct((SWA_KV_HEADS, SWA_SPAN, SWA_GROUP_LANES), F32),
                   jax.ShapeDtypeStruct((SWA_KV_HEADS, 1, SWA_GROUP_LANES), F32)],
        compiler_params=_params(("arbitrary",), VMEM_BIG),
    )(qs, ks, vs, bias, sink, do, *extra)


def _bias_call(rel_bias, buckets):
    def body(t_r, bk_r, o_r):
        bk = bk_r[...]
        s = lax.broadcasted_iota(jnp.int32, bk.shape, 0)
        c = lax.broadcasted_iota(jnp.int32, bk.shape, 1)
        in_band = jnp.abs(s - SWA_BLOCK - c) <= SWA_BLOCK
        for h in range(SWA_Q_HEADS):
            acc = jnp.zeros(bk.shape, F32)
            for b in range(REL_BUCKETS):
                acc = jnp.where(bk == b, t_r[b, h], acc)
            g = h % SWA_GROUP
            o_r[h // SWA_GROUP, :, SWA_BLOCK * g:SWA_BLOCK * (g + 1)] = jnp.where(in_band, acc, -1e30)

    return pl.pallas_call(
        body, name="band_bias",
        in_specs=[pl.BlockSpec(memory_space=pltpu.SMEM), _vmem_spec()], out_specs=_vmem_spec(),
        out_shape=jax.ShapeDtypeStruct((SWA_KV_HEADS, SWA_SPAN, SWA_GROUP_LANES), F32),
    )(rel_bias, buckets)


def _relbias_call(dbias, dsink, buckets, dep=None):
    def body(db_r, ds_r, bk_r, o_r, os_r):
        bk = bk_r[...]
        rowi = lax.broadcasted_iota(jnp.int32, (REL_BUCKETS, 128), 0)
        lanei = lax.broadcasted_iota(jnp.int32, (REL_BUCKETS, 128), 1)
        lane1 = lax.broadcasted_iota(jnp.int32, (1, 128), 1)
        acc = jnp.zeros((REL_BUCKETS, 128), F32)
        acc_sink = jnp.zeros((1, 128), F32)
        for h in range(SWA_Q_HEADS):
            kv, g = h // SWA_GROUP, h % SWA_GROUP
            lanes = slice(SWA_BLOCK * g, SWA_BLOCK * (g + 1))
            part = db_r[kv, :, lanes]
            for b in range(REL_BUCKETS):
                s = jnp.sum(jnp.where(bk == b, part, 0.0))
                acc = acc + jnp.where((rowi == b) & (lanei == h), s, 0.0)
            acc_sink = acc_sink + jnp.where(lane1 == h, jnp.sum(ds_r[kv, :, lanes]), 0.0)
        o_r[...] = acc
        os_r[...] = acc_sink

    body, extra, extra_specs = _after(body, 3, dep)
    return pl.pallas_call(
        body, name="relbias_grad",
        in_specs=[_vmem_spec()] * 3 + extra_specs, out_specs=[_vmem_spec()] * 2,
        out_shape=[jax.ShapeDtypeStruct((REL_BUCKETS, 128), F32), jax.ShapeDtypeStruct((1, 128), F32)],
    )(dbias, dsink, buckets, *extra)


def _mix_call(o_f, o_b, ga, o_s, x, gn, w_out_p, g_post, g_pre2):
    L = x.shape[0]
    tm = min(512, L)
    hw = GLA_HEADS * HEAD_PAD

    def body(of_r, ob_r, ga_r, os_r, x_r, gn_r, w_r, gp_r, g2_r, cat_r, mix_r, h1_r, n2_r):
        gn_v = gn_r[...]
        for h in range(GLA_HEADS):
            sl = slice(HEAD_PAD * h, HEAD_PAD * (h + 1))
            oh = of_r[:, sl] + ob_r[:, sl]
            on = oh * _rms_r(oh) * gn_v
            gate = ga_r[:, sl]
            cat_r[:, sl] = (on * (gate * jax.nn.sigmoid(gate))).astype(cat_r.dtype)
        os_v = os_r[...]
        cat_r[:, hw:] = os_v
        mix = _dot(cat_r[:, :hw], w_r[:hw, :]) + _dot(os_v, w_r[hw:, :])
        mix_r[...] = mix
        h1 = x_r[...] + mix * _rms_r(mix) * gp_r[...]
        h1_r[...] = h1
        n2_r[...] = (h1 * _rms_r(h1) * g2_r[...]).astype(n2_r.dtype)

    return pl.pallas_call(
        body, name="mix_fwd", grid=(L // tm,),
        in_specs=[_row_spec(tm, hw), _row_spec(tm, hw), _row_spec(tm, hw), _row_spec(tm, OUT_PAD - hw),
                  _row_spec(tm, D_MODEL), _full_spec((1, HEAD_PAD)), _vmem_spec(),
                  _full_spec((1, D_MODEL)), _full_spec((1, D_MODEL))],
        out_specs=[_row_spec(tm, OUT_PAD), _row_spec(tm, D_MODEL), _row_spec(tm, D_MODEL), _row_spec(tm, D_MODEL)],
        out_shape=[jax.ShapeDtypeStruct((L, OUT_PAD), MXU_DTYPE), jax.ShapeDtypeStruct((L, D_MODEL), F32),
                   jax.ShapeDtypeStruct((L, D_MODEL), F32), jax.ShapeDtypeStruct((L, D_MODEL), MXU_DTYPE)],
        compiler_params=_params(("arbitrary",), VMEM_BIG),
    )(o_f, o_b, ga, o_s, x, gn, w_out_p, g_post, g_pre2)


def _mlp_fwd_call(n2, h1, tgt, w_ud, g_post):
    L = n2.shape[0]
    tm = min(512, L)
    blk = D_FF // N_CHIPS

    def body(n2_r, h1_r, t_r, w_r, g_r, a_r, rz_r, dh2_r, dff_r, loss_r, dg_r):
        @pl.when(pl.program_id(0) == 0)
        def _():
            loss_r[...] = jnp.zeros_like(loss_r)
            dg_r[...] = jnp.zeros_like(dg_r)

        n2v = n2_r[...]
        ff = jnp.zeros((tm, D_MODEL), F32)
        for j in range(N_CHIPS):
            sl = slice(blk * j, blk * (j + 1))
            rz = jnp.maximum(_dot(n2v, w_r[j, 0]), 0.0)
            a = _mx(rz * rz)
            rz_r[:, sl] = rz.astype(rz_r.dtype)
            a_r[:, sl] = a
            ff = ff + _dot(a, w_r[j, 1])
        g = g_r[...]
        r = _rms_r(ff)
        err = h1_r[...] + ff * r * g - t_r[...]
        loss_r[...] += 0.5 * jnp.sum(err * err) / D_MODEL
        dh2 = err * (1.0 / D_MODEL)
        dh2_r[...] = dh2
        dff, dg = _rms_bwd(ff, r, g, dh2)
        dff_r[...] = dff.astype(dff_r.dtype)
        dg_r[...] += dg

    return pl.pallas_call(
        body, name="mlp_fwd", grid=(L // tm,),
        in_specs=[_row_spec(tm, D_MODEL), _row_spec(tm, D_MODEL), _row_spec(tm, D_MODEL),
                  _vmem_spec(), _full_spec((1, D_MODEL))],
        out_specs=[_row_spec(tm, D_FF), _row_spec(tm, D_FF), _row_spec(tm, D_MODEL), _row_spec(tm, D_MODEL),
                   _full_spec((1, 128)), _full_spec((1, D_MODEL))],
        out_shape=[jax.ShapeDtypeStruct((L, D_FF), MXU_DTYPE), jax.ShapeDtypeStruct((L, D_FF), MXU_DTYPE),
                   jax.ShapeDtypeStruct((L, D_MODEL), F32), jax.ShapeDtypeStruct((L, D_MODEL), MXU_DTYPE),
                   jax.ShapeDtypeStruct((1, 128), F32), jax.ShapeDtypeStruct((1, D_MODEL), F32)],
        compiler_params=_params(("arbitrary",), VMEM_BIG),
    )(n2, h1, tgt, w_ud, g_post)


def _mlp_bwd_call(dff, rz, w_ud):
    L = dff.shape[0]
    tm = min(512, L)
    blk = D_FF // N_CHIPS

    def body(dff_r, rz_r, w_r, dz_r, dn2_r):
        dffv = dff_r[...]
        dn2 = jnp.zeros((tm, D_MODEL), F32)
        for j in range(N_CHIPS):
            sl = slice(blk * j, blk * (j + 1))
            dz = _mx(_dot_nt(dffv, w_r[j, 1]) * 2.0 * rz_r[:, sl].astype(F32))
            dz_r[:, sl] = dz
            dn2 = dn2 + _dot_nt(dz, w_r[j, 0])
        dn2_r[...] = dn2

    return pl.pallas_call(
        body, name="mlp_bwd", grid=(L // tm,),
        in_specs=[_row_spec(tm, D_MODEL), _row_spec(tm, D_FF), _vmem_spec()],
        out_specs=[_row_spec(tm, D_FF), _row_spec(tm, D_MODEL)],
        out_shape=[jax.ShapeDtypeStruct((L, D_FF), MXU_DTYPE), jax.ShapeDtypeStruct((L, D_MODEL), F32)],
        compiler_params=_params(("arbitrary",), VMEM_BIG),
    )(dff, rz, w_ud)


def _mlp_wgrad_call(a, dff, n2, dz):
    L = a.shape[0]
    tf = 512
    per = (D_FF // N_CHIPS) // tf

    def body(a_r, dff_r, n2_r, dz_r, dwd_r, dwu_r):
        dwd_r[...] = _dot_tn(a_r[...], dff_r[...])
        dwu_r[...] = _dot_tn(n2_r[...], dz_r[...])

    return pl.pallas_call(
        body, name="mlp_wgrad", grid=(D_FF // tf,),
        in_specs=[pl.BlockSpec((L, tf), lambda j: (0, j)), _vmem_spec(), _vmem_spec(),
                  pl.BlockSpec((L, tf), lambda j: (0, j))],
        out_specs=[pl.BlockSpec((tf, D_MODEL), lambda j: (j, 0)),
                   pl.BlockSpec((None, D_MODEL, tf), lambda j: (j // per, 0, j % per))],
        out_shape=[jax.ShapeDtypeStruct((D_FF, D_MODEL), F32),
                   jax.ShapeDtypeStruct((N_CHIPS, D_MODEL, D_FF // N_CHIPS), F32)],
        compiler_params=_params(("arbitrary",), VMEM_BIG),
    )(a, dff, n2, dz)


def _mix_bwd_call(dn2, dh2, h1, mix, cat, o_f, o_b, ga, gn, g_post, g_pre2, w_out_p):
    L = dn2.shape[0]
    tm = min(512, L)
    hw = GLA_HEADS * HEAD_PAD

    def body(dn2_r, dh2_r, h1_r, mix_r, cat_r, of_r, ob_r, ga_r, gn_r, gp_r, g2_r, w_r,
             dh1_r, do_r, dga_r, dos_r, dw_r, dg2_r, dgp_r, dgn_r):
        @pl.when(pl.program_id(0) == 0)
        def _():
            for ref in (dw_r, dg2_r, dgp_r, dgn_r):
                ref[...] = jnp.zeros_like(ref)

        parts = [slice(start, start + min(256, tm)) for start in range(0, tm, 256)]
        dmix_m = []
        for rs in parts:
            h1 = h1_r[rs, :]
            dx2, dg2 = _rms_bwd(h1, _rms_r(h1), g2_r[...], dn2_r[rs, :])
            dh1 = dh2_r[rs, :] + dx2
            dh1_r[rs, :] = dh1
            dg2_r[...] += dg2
            mix = mix_r[rs, :]
            dmix, dgp = _rms_bwd(mix, _rms_r(mix), gp_r[...], dh1)
            dgp_r[...] += dgp
            dmix_m.append(_mx(dmix))
        dcat = [_dot_nt(d, w_r[...]) for d in dmix_m]
        for rs, d in zip(parts, dmix_m):
            dw_r[...] += _dot_tn(cat_r[rs, :], d)
        gn_v = gn_r[...]
        dgn = jnp.zeros((1, HEAD_PAD), F32)
        for rs, dc in zip(parts, dcat):
            dos_r[rs, :] = _spread_heads(dc[:, hw:]).astype(dos_r.dtype)
            for h in range(GLA_HEADS):
                sl = slice(HEAD_PAD * h, HEAD_PAD * (h + 1))
                oh = of_r[rs, sl] + ob_r[rs, sl]
                rr = _rms_r(oh)
                xh = oh * rr
                gate = ga_r[rs, sl]
                sg = jax.nn.sigmoid(gate)
                silu = gate * sg
                doa = dc[:, sl]
                dga_r[rs, sl] = (doa * (xh * gn_v) * (sg + silu * (1.0 - sg))).astype(dga_r.dtype)
                don = doa * silu
                gd = don * gn_v
                do_r[rs, sl] = rr * (gd - xh * jnp.mean(gd * xh, axis=-1, keepdims=True))
                dgn = dgn + jnp.sum(don * xh, axis=0, keepdims=True)
        dgn_r[...] += dgn

    return pl.pallas_call(
        body, name="mix_bwd", grid=(L // tm,),
        in_specs=[_row_spec(tm, D_MODEL)] * 4 + [_row_spec(tm, OUT_PAD)] + [_row_spec(tm, hw)] * 3
        + [_full_spec((1, HEAD_PAD)), _full_spec((1, D_MODEL)), _full_spec((1, D_MODEL)), _vmem_spec()],
        out_specs=[_row_spec(tm, D_MODEL), _row_spec(tm, hw), _row_spec(tm, hw),
                   _row_spec(tm, SWA_Q_HEADS * HEAD_PAD),
                   _full_spec((OUT_PAD, D_MODEL)), _full_spec((1, D_MODEL)), _full_spec((1, D_MODEL)),
                   _full_spec((1, HEAD_PAD))],
        out_shape=[jax.ShapeDtypeStruct((L, D_MODEL), F32), jax.ShapeDtypeStruct((L, hw), F32),
                   jax.ShapeDtypeStruct((L, hw), MXU_DTYPE),
                   jax.ShapeDtypeStruct((L, SWA_Q_HEADS * HEAD_PAD), MXU_DTYPE),
                   jax.ShapeDtypeStruct((OUT_PAD, D_MODEL), F32), jax.ShapeDtypeStruct((1, D_MODEL), F32),
                   jax.ShapeDtypeStruct((1, D_MODEL), F32), jax.ShapeDtypeStruct((1, HEAD_PAD), F32)],
        compiler_params=_params(("arbitrary",), VMEM_BIG),
    )(dn2, dh2, h1, mix, cat, o_f, o_b, ga, gn, g_post, g_pre2, w_out_p)


def _in_bwd_call(x, dh1, g_pre, w_in_t, pairs, singles, halos, dep=None):
    L = x.shape[0]
    tm = min(512, L)
    per = tm // SWA_BLOCK
    n_pair, n_single, n_halo = len(pairs), len(singles), len(halos)
    groups = [c for c, _ in pairs] + [c for c, _ in singles] + [c for c, _ in halos]

    def body(*refs):
        x_r, dh1_r, g_r, w_r = refs[:4]
        pair_refs = refs[4:4 + 2 * n_pair]
        single_refs = refs[4 + 2 * n_pair:4 + 2 * n_pair + n_single]
        halo_refs = refs[4 + 2 * n_pair + n_single:4 + 2 * n_pair + n_single + per * n_halo]
        dx_r, dw_r, dg_r = refs[4 + 2 * n_pair + n_single + per * n_halo:]

        @pl.when(pl.program_id(0) == 0)
        def _():
            dw_r[...] = jnp.zeros_like(dw_r)
            dg_r[...] = jnp.zeros_like(dg_r)

        xv = x_r[...]
        r = _rms_r(xv)
        g = g_r[...]
        u = _mx(xv * r * g)
        vals = [pair_refs[2 * i][...].astype(F32) + pair_refs[2 * i + 1][...].astype(F32) for i in range(n_pair)]
        vals += [ref[...].astype(F32) for ref in single_refs]
        vals += [jnp.concatenate([ref[...] for ref in halo_refs[per * i:per * (i + 1)]], axis=0)
                 for i in range(n_halo)]
        ds = [_mx(_squeeze_heads(val) if heads else val) for (_, _, heads), val in zip(groups, vals)]
        du = jnp.zeros((tm, D_MODEL), F32)
        for (first, rows, _), d in zip(groups, ds):
            du = du + _dot(d, w_r[first:first + rows, :])
        for (first, rows, _), d in zip(groups, ds):
            dw_r[first:first + rows, :] += _dot_tn(d, u)
        dx, dg = _rms_bwd(xv, r, g, du)
        dx_r[...] = dh1_r[...] + dx
        dg_r[...] += dg

    arrays = [a for _, pr in pairs for a in pr] + [a for _, a in singles]
    specs = [_row_spec(tm, a.shape[1]) for a in arrays]
    for _, a in halos:
        specs += [pl.BlockSpec((SWA_BLOCK, a.shape[1]), lambda i, j=j: (per * i + 1 + j, 0)) for j in range(per)]
        arrays += [a] * per
    body, extra, extra_specs = _after(body, 4 + len(arrays), dep)
    return pl.pallas_call(
        body, name="in_bwd", grid=(L // tm,),
        in_specs=[_row_spec(tm, D_MODEL), _row_spec(tm, D_MODEL), _full_spec((1, D_MODEL)), _vmem_spec()] + specs
        + extra_specs,
        out_specs=[_row_spec(tm, D_MODEL), _full_spec((IN_COLS, D_MODEL)), _full_spec((1, D_MODEL))],
        out_shape=[jax.ShapeDtypeStruct((L, D_MODEL), F32), jax.ShapeDtypeStruct((IN_COLS, D_MODEL), F32),
                   jax.ShapeDtypeStruct((1, D_MODEL), F32)],
        compiler_params=_params(("arbitrary",), VMEM_BIG),
    )(x, dh1, g_pre, w_in_t, *arrays, *extra)


def _adamw_math(w, g, m, v):
    m = ADAM_B1 * m + (1.0 - ADAM_B1) * g
    v = ADAM_B2 * v + (1.0 - ADAM_B2) * (g * g)
    m_hat = m / (1.0 - ADAM_B1 ** ADAM_STEP)
    v_hat = v / (1.0 - ADAM_B2 ** ADAM_STEP)
    delta = -ADAM_LR * (m_hat / (jnp.sqrt(v_hat) + ADAM_EPS) + ADAM_WD * w)
    return delta, m, v


def _adamw_call(w, g, m, v, name, dep=None):
    rows, cols = w.shape
    tr = min(256, rows)

    def body(w_r, g_r, m_r, v_r, d_r, nm_r, nv_r):
        d_r[...], nm_r[...], nv_r[...] = _adamw_math(w_r[...], g_r[...], m_r[...], v_r[...])

    if rows % tr == 0:
        spec, steps = _row_spec(tr, cols), rows // tr
    else:
        spec, steps = pl.BlockSpec((rows, 256), lambda i: (0, i)), cols // 256
    body, extra, extra_specs = _after(body, 4, dep)
    return pl.pallas_call(
        body, name=name, grid=(steps,),
        in_specs=[spec] * 4 + extra_specs, out_specs=[spec] * 3,
        out_shape=[jax.ShapeDtypeStruct(w.shape, F32)] * 3,
        compiler_params=_params(("arbitrary",)),
    )(w, g, m, v, *extra)


def _position():
    return lax.axis_index("x"), lax.axis_index("y"), lax.axis_index("c")


def _other_chips(x, y):
    return [(1 - x, y), (x, 1 - y), (1 - x, 1 - y)]


ROWS, COLS = -2, -1


def _half(ref, which, axis):
    size = ref.shape[axis] // 2
    span = pl.ds(pl.multiple_of(which * size, 16 if axis == ROWS else 128), size)
    index = [slice(None)] * len(ref.shape)
    index[axis] = span
    return ref.at[tuple(index)]


def _first_gather_call(shards, axes):
    n = len(shards)

    def body(*refs):
        srcs, outs = refs[:n], refs[n:2 * n]
        send_sems, recv_sems, local_sems = refs[2 * n:]
        x, y, c = _position()
        sibling = (x, y, 1 - c)
        chips = _other_chips(x, y)
        local = [pltpu.make_async_copy(srcs[a], outs[a].at[2 * x + y], local_sems.at[a]) for a in range(n)]
        for cp in local:
            cp.start()

        def copy(a, k, block, to, src=None):
            px, py, pc = block
            dst = _half(outs[a].at[2 * px + py], pc, axes[a])
            return pltpu.make_async_remote_copy(
                src_ref=dst if src is None else src, dst_ref=dst, send_sem=send_sems.at[6 * a + k],
                recv_sem=recv_sems.at[6 * a + k], device_id=to, device_id_type=MESH_ID)

        first, passed = [], []
        for a in range(n):
            my_half = _half(srcs[a], c, axes[a])
            first += [copy(a, j, (x, y, c), (*chip, c), src=my_half) for j, chip in enumerate(chips)]
        for cp in first:
            cp.start()
        for a in range(n):
            for j, chip in enumerate(chips):
                copy(a, j, (*chip, c), (x, y, c)).wait_recv()
                passed.append(copy(a, 3 + j, (*chip, c), sibling))
                passed[-1].start()
        for a in range(n):
            for j, chip in enumerate(chips):
                copy(a, 3 + j, (*chip, 1 - c), (x, y, c)).wait_recv()
        for cp in first + passed:
            cp.wait_send()
        for cp in local:
            cp.wait()

    return pl.pallas_call(
        body, name="first_gather",
        in_specs=[_any_spec()] * n, out_specs=[_any_spec()] * n,
        out_shape=[jax.ShapeDtypeStruct((N_CHIPS,) + s.shape, s.dtype) for s in shards],
        scratch_shapes=[pltpu.SemaphoreType.DMA((6 * n,)), pltpu.SemaphoreType.DMA((6 * n,)),
                        pltpu.SemaphoreType.DMA((n,))],
    )(*shards)


def _split_start(name, arrays, n_copies, plan):
    n = len(arrays)

    def body(*refs):
        ins, send_sems, recv_sems, token = refs[:n], refs[n], refs[n + 1], refs[-1]
        for k, (src, dst, to, _) in enumerate(plan(ins)):
            pltpu.make_async_remote_copy(src_ref=src, dst_ref=dst, send_sem=send_sems.at[k],
                                         recv_sem=recv_sems.at[k], device_id=to, device_id_type=MESH_ID).start()
        token[...] = jnp.zeros_like(token)

    hbm = pl.BlockSpec(memory_space=pltpu.HBM)
    sem = pl.BlockSpec(memory_space=pltpu.SEMAPHORE)
    out = pl.pallas_call(
        body, name=name,
        out_shape=(pltpu.SemaphoreType.DMA((n_copies,)), pltpu.SemaphoreType.DMA((n_copies,)))
        + tuple(pltpu.HBM(a.shape, a.dtype) for a in arrays) + (jax.ShapeDtypeStruct((8, 128), F32),),
        in_specs=[hbm] * n, out_specs=(sem, sem) + (hbm,) * n + (_vmem_spec(),),
        input_output_aliases={i: 2 + i for i in range(n)},
        compiler_params=pltpu.CompilerParams(has_side_effects=pltpu.SideEffectType.DATAFLOW_SIDE_EFFECTING),
    )(*[pltpu.with_memory_space_constraint(a, pltpu.HBM) for a in arrays])
    return (out[0], out[1], tuple(out[2:2 + n])), out[-1]


def _split_wait(name, handle, n_copies, plan, after):
    send_sems, recv_sems, arrays = handle
    n = len(arrays)

    def body(*refs):
        ins, s_sems, r_sems = refs[:n], refs[n], refs[n + 1]
        for k, (src, dst, to, landed) in enumerate(plan(ins)):
            cp = pltpu.make_async_remote_copy(src_ref=src, dst_ref=landed, send_sem=s_sems.at[k],
                                              recv_sem=r_sems.at[k], device_id=to, device_id_type=MESH_ID)
            cp.wait_send()
            cp.wait_recv()

    hbm = pl.BlockSpec(memory_space=pltpu.HBM)
    sem = pl.BlockSpec(memory_space=pltpu.SEMAPHORE)
    out = pl.pallas_call(
        body, name=name,
        out_shape=tuple(pltpu.HBM(a.shape, a.dtype) for a in arrays),
        in_specs=[hbm] * n + [sem, sem, _any_spec()], out_specs=(hbm,) * n,
        input_output_aliases={i: i for i in range(n)},
        compiler_params=pltpu.CompilerParams(has_side_effects=pltpu.SideEffectType.DATAFLOW_SIDE_EFFECTING),
    )(*arrays, send_sems, recv_sems, after)
    return tuple(out)


def _gather_plans(axes):
    n = len(axes)

    def stage_one(refs):
        x, y, c = _position()
        copies = []
        for a, axis in enumerate(axes):
            for px, py in _other_chips(x, y):
                copies.append((_half(refs[a], c, axis), _half(refs[n + a].at[2 * x + y], c, axis),
                               (px, py, c), _half(refs[n + a].at[2 * px + py], c, axis)))
        return copies

    def stage_two(refs):
        x, y, c = _position()
        copies = []
        for a, axis in enumerate(axes):
            for px, py in _other_chips(x, y):
                piece = _half(refs[n + a].at[2 * px + py], c, axis)
                copies.append((piece, piece, (x, y, 1 - c), _half(refs[n + a].at[2 * px + py], 1 - c, axis)))
        return copies

    return stage_one, stage_two


def _pair_swap_plan(axes):
    n = len(axes)

    def plan(refs):
        x, y, c = _position()
        return [(_half(refs[a], 1 - c, axes[a]), refs[n + a], (x, y, 1 - c), refs[n + a]) for a in range(n)]

    return plan


def _chip_swap_plan(n):
    def plan(refs):
        x, y, c = _position()
        copies = []
        for a in range(n):
            for j, (px, py) in enumerate(_other_chips(x, y)):
                copies.append((refs[a].at[2 * px + py], refs[n + a].at[j], (px, py, c), refs[n + a].at[j]))
        return copies

    return plan


def _pair_join_plan(axes):
    def plan(refs):
        x, y, c = _position()
        copies = []
        for a, axis in enumerate(axes):
            mine = _half(refs[a], c, axis)
            copies.append((mine, mine, (x, y, 1 - c), _half(refs[a], 1 - c, axis)))
        return copies

    return plan


def _pair_add_call(g, got, pos, name, axis):
    rows, cols = got.shape[1], got.shape[2]
    tr = min(512, rows) if axis == ROWS else rows
    nblk = rows // tr
    if axis == ROWS:
        mine = lambda j, i, p: (j, p[1] * nblk + i, 0)
    else:
        mine = lambda j, i, p: (j, 0, p[1])

    def body(pos_r, g_r, got_r, o_r):
        o_r[...] = (g_r[...] + got_r[...]).astype(o_r.dtype)

    return pl.pallas_call(
        body, name=name,
        grid_spec=pltpu.PrefetchScalarGridSpec(
            num_scalar_prefetch=1, grid=(N_CHIPS, nblk),
            in_specs=[pl.BlockSpec((None, tr, cols), mine),
                      pl.BlockSpec((None, tr, cols), lambda j, i, p: (j, i, 0))],
            out_specs=pl.BlockSpec((None, tr, cols), lambda j, i, p: (j, i, 0))),
        out_shape=jax.ShapeDtypeStruct(got.shape, COMM_DTYPE),
        compiler_params=_params(("arbitrary", "arbitrary"), VMEM_BIG),
    )(pos, g, got)


def _chip_add_call(hsum, got, pos, name, axis):
    rows, cols = hsum.shape[1], hsum.shape[2]
    tr = min(512, rows) if axis == ROWS else rows
    nblk = rows // tr
    if axis == ROWS:
        out_shape, mine = (2 * rows, cols), (lambda i, p: (p[1] * nblk + i, 0))
    else:
        out_shape, mine = (rows, 2 * cols), (lambda i, p: (0, p[1]))

    def body(pos_r, own_r, got_r, o_r):
        acc = own_r[...].astype(F32)
        for j in range(3):
            acc = acc + got_r[j].astype(F32)
        o_r[...] = acc

    return pl.pallas_call(
        body, name=name,
        grid_spec=pltpu.PrefetchScalarGridSpec(
            num_scalar_prefetch=1, grid=(nblk,),
            in_specs=[pl.BlockSpec((None, tr, cols), lambda i, p: (p[0], i, 0)),
                      pl.BlockSpec((3, tr, cols), lambda i, p: (0, i, 0))],
            out_specs=pl.BlockSpec((tr, cols), mine)),
        out_shape=jax.ShapeDtypeStruct(out_shape, F32),
        compiler_params=_params(("arbitrary",), VMEM_BIG),
    )(pos, hsum, got)


SMALL_NAMES = ("norm_mix_pre", "norm_mix_post", "norm_mlp_pre", "norm_mlp_post", "b_gate_fwd", "b_gate_bwd",
               "gla_norm", "swa_sink", "rel_bias")


def _small_update_call(grads, gate_grads, params, dep=None):
    n_dev = 8
    n_small = len(SMALL_NAMES)
    wmv = [t for p in params for t in p]
    shapes = [p[0].shape for p in params]

    def body(*refs):
        g_refs = refs[:n_small + 3]
        wmv_refs = refs[n_small + 3:n_small + 3 + 3 * n_small]
        n_in = n_small + 3 + 3 * n_small
        out_refs = refs[n_in:n_in + 4 * n_small + 3]
        pack_a, pack_b, all_a, all_b, send_sems, recv_sems = refs[n_in + 4 * n_small + 3:]
        x, y, c = _position()
        me = 4 * x + 2 * y + c
        pack_a[...] = jnp.zeros_like(pack_a)
        pack_b[...] = jnp.zeros_like(pack_b)
        for i in range(4):
            pack_a[i:i + 1, :] = g_refs[i][...]
        pack_a[4:5, 0:256] = g_refs[4][...]
        pack_a[5:6, 0:256] = g_refs[5][...]
        pack_a[6:7, 0:128] = g_refs[6][...]
        pack_a[7:8, 0:128] = g_refs[7][...]
        pack_a[7:8, 128:256] = g_refs[11][...]
        pack_b[0:32, 0:128] = g_refs[8][...]
        pack_b[32:48, :] = g_refs[9][...]
        pack_b[48:64, :] = g_refs[10][...]
        all_a[me] = pack_a[...]
        all_b[me] = pack_b[...]
        copies = []
        for k in range(1, n_dev):
            fx, fy, fc = (k >> 2) & 1, (k >> 1) & 1, k & 1
            to = (1 - x if fx else x, 1 - y if fy else y, 1 - c if fc else c)
            for t, (pack, dst) in enumerate(((pack_a, all_a), (pack_b, all_b))):
                copies.append(pltpu.make_async_remote_copy(
                    src_ref=pack, dst_ref=dst.at[me], send_sem=send_sems.at[2 * (k - 1) + t],
                    recv_sem=recv_sems.at[2 * (k - 1) + t], device_id=to, device_id_type=MESH_ID))
        for cp in copies:
            cp.start()
        for cp in copies:
            cp.wait()
        sum_a, sum_b = all_a[0], all_b[0]
        for d in range(1, n_dev):
            sum_a = sum_a + all_a[d]
            sum_b = sum_b + all_b[d]
        gsum = [sum_a[0:1], sum_a[1:2], sum_a[2:3], sum_a[3:4], sum_a[4:5, 0:256], sum_a[5:6, 0:256],
                sum_a[6:7, 0:128], sum_a[7:8, 0:SWA_Q_HEADS], sum_b[0:32, 0:SWA_Q_HEADS]]
        for i in range(n_small):
            w_r, m_r, v_r = wmv_refs[3 * i:3 * i + 3]
            delta, new_m, new_v = _adamw_math(w_r[...], gsum[i], m_r[...], v_r[...])
            out_refs[4 * i][...] = gsum[i]
            out_refs[4 * i + 1][...] = delta
            out_refs[4 * i + 2][...] = new_m
            out_refs[4 * i + 3][...] = new_v
        out_refs[4 * n_small][...] = sum_b[32:48]
        out_refs[4 * n_small + 1][...] = sum_b[48:64]
        out_refs[4 * n_small + 2][...] = sum_a[7:8, 128:256]

    n_in = n_small + 3 + 3 * n_small
    body, extra, extra_specs = _after(body, n_in, dep)
    out_shape = [jax.ShapeDtypeStruct(s, F32) for s in shapes for _ in range(4)]
    out_shape += [jax.ShapeDtypeStruct((GLA_GATE_RANK, 256), F32)] * 2 + [jax.ShapeDtypeStruct((1, 128), F32)]
    out = pl.pallas_call(
        body, name="small_update",
        in_specs=[_whole_spec(a.shape) for a in list(grads) + list(gate_grads) + wmv] + extra_specs,
        out_specs=[_whole_spec(s.shape) for s in out_shape],
        out_shape=out_shape,
        scratch_shapes=[pltpu.VMEM((8, D_MODEL), F32), pltpu.VMEM((64, 256), F32),
                        pltpu.VMEM((n_dev, 8, D_MODEL), F32), pltpu.VMEM((n_dev, 64, 256), F32),
                        pltpu.SemaphoreType.DMA((2 * (n_dev - 1),)), pltpu.SemaphoreType.DMA((2 * (n_dev - 1),))],
    )(*grads, *gate_grads, *wmv, *extra)
    per_name = [tuple(out[4 * i:4 * i + 4]) for i in range(n_small)]
    return per_name, out[4 * n_small], out[4 * n_small + 1], out[4 * n_small + 2]


def _pad_heads(t, n_heads, axis=-1):
    axis = axis % t.ndim
    shape = t.shape
    t = t.reshape(shape[:axis] + (n_heads, 64) + shape[axis + 1:])
    pad = [(0, 0)] * t.ndim
    pad[axis + 1] = (0, HEAD_PAD - 64)
    return jnp.pad(t, pad).reshape(shape[:axis] + (n_heads * HEAD_PAD,) + shape[axis + 1:])


def _unpad_heads(t, n_heads, axis=-1):
    axis = axis % t.ndim
    shape = t.shape
    t = t.reshape(shape[:axis] + (n_heads, HEAD_PAD) + shape[axis + 1:])
    t = lax.slice_in_dim(t, 0, 64, axis=axis + 1)
    return t.reshape(shape[:axis] + (n_heads * 64,) + shape[axis + 1:])


def _pad_gate(w, first_row):
    return jnp.pad(_pad_heads(w, 4), ((first_row, 128 - GLA_GATE_RANK - first_row), (0, 0)))


def _own_slot(shard, chip):
    zone = lax.empty((N_CHIPS,) + shard.shape, shard.dtype)
    return lax.dynamic_update_slice(zone, shard[None], (chip,) + (0,) * shard.ndim)


def _reduce_to_owners(grads, axes, pos, tag, overlap):
    n = len(grads)

    def half_shape(g, axis):
        return (N_CHIPS, g.shape[1] // 2, g.shape[2]) if axis == ROWS else (N_CHIPS, g.shape[1], g.shape[2] // 2)

    lands = [lax.empty(half_shape(g, axis), F32) for g, axis in zip(grads, axes)]
    handle, token = _split_start(tag + "_pair_start", list(grads) + lands, n, _pair_swap_plan(axes))
    got = _split_wait(tag + "_pair_wait", handle, n, _pair_swap_plan(axes), overlap[0](token))
    sums = [_pair_add_call(got[a], got[n + a], pos, f"{tag}_pair_add{a}", axes[a]) for a in range(n)]
    lands = [lax.empty((3,) + s.shape[1:], s.dtype) for s in sums]
    handle, token = _split_start(tag + "_chip_start", sums + lands, 3 * n, _chip_swap_plan(n))
    got = _split_wait(tag + "_chip_wait", handle, 3 * n, _chip_swap_plan(n), overlap[1](token))
    halves = [_chip_add_call(got[a], got[n + a], pos, f"{tag}_chip_add{a}", axes[a]) for a in range(n)]
    handle, token = _split_start(tag + "_join_start", halves, n, _pair_join_plan(axes))
    return _split_wait(tag + "_join_wait", handle, n, _pair_join_plan(axes), overlap[2](token))


def kernel(x, norm_mix_pre, w_in, w_gate_up_fwd, b_gate_fwd, w_gate_up_bwd, b_gate_bwd, gla_norm, swa_sink, rel_bias, w_out, norm_mix_post, norm_mlp_pre, w_up, w_down, norm_mlp_post, loss_target, m_norm_mix_pre, m_w_in, m_w_gate_up_fwd, m_b_gate_fwd, m_w_gate_up_bwd, m_b_gate_bwd, m_gla_norm, m_swa_sink, m_rel_bias, m_w_out, m_norm_mix_post, m_norm_mlp_pre, m_w_up, m_w_down, m_norm_mlp_post, v_norm_mix_pre, v_w_in, v_w_gate_up_fwd, v_b_gate_fwd, v_w_gate_up_bwd, v_b_gate_bwd, v_gla_norm, v_swa_sink, v_rel_bias, v_w_out, v_norm_mix_post, v_norm_mlp_pre, v_w_up, v_w_down, v_norm_mlp_post):
    given = dict(locals())
    cx, cy, cc = _position()
    chip = (2 * cx + cy).astype(jnp.int32)
    pos = jnp.stack([chip, cc.astype(jnp.int32)])
    seq, tgt = x[0], loss_target[0]
    L = seq.shape[0]

    gates = jnp.concatenate([w_gate_up_fwd[0], w_gate_up_bwd[0]], axis=0).astype(COMM_DTYPE)
    all_in, all_gates = _first_gather_call([w_in[0].T.astype(COMM_DTYPE), gates], [COLS, ROWS])
    rest = [w_out[0].astype(COMM_DTYPE), jnp.stack([w_up[0], w_down[0]]).astype(COMM_DTYPE)]
    stage_one, stage_two = _gather_plans([ROWS, ROWS])
    handle, token = _split_start("gather_chip_start", rest + [_own_slot(s, chip) for s in rest] + [all_gates], 6,
                                 stage_one)

    w_in_t = _mx(all_in.reshape(IN_COLS, D_MODEL))
    gates_full = jnp.concatenate([all_gates[j] for j in range(N_CHIPS)], axis=1)
    wgf_p = _mx(_pad_gate(gates_full[:GLA_GATE_RANK], 0))
    wgb_p = _mx(_pad_gate(gates_full[GLA_GATE_RANK:], GLA_GATE_RANK))
    bf_p, bb_p = _pad_heads(b_gate_fwd, 4), _pad_heads(b_gate_bwd, 4)
    buckets = jnp.asarray(_band_buckets())
    bias = _bias_call(rel_bias, buckets)
    sink1 = swa_sink.reshape(SWA_Q_HEADS)

    qa, ka, va, ga, qs, ks, vs, za = _proj_call(seq, norm_mix_pre, w_in_t, dep=token)
    halo = ((SWA_BLOCK, SWA_BLOCK), (0, 0))
    ks_p, vs_p = jnp.pad(ks, halo), jnp.pad(vs, halo)
    o_f, o_b, s_f, s_b = _gla_fwd_call(qa, ka, va, za, wgf_p, bf_p, wgb_p, bb_p)
    arrays = _split_wait("gather_chip_wait", handle, 6, stage_one, o_f)
    handle, token = _split_start("gather_pair_start", list(arrays), 6, stage_two)
    o_s = _swa_fwd_call(qs, ks_p, vs_p, bias, sink1, dep=token)
    arrays = _split_wait("gather_pair_wait", handle, 6, stage_two, o_s)
    w_out_full = _mx(arrays[2].reshape(N_CHIPS * R_OUT, D_MODEL))
    w_ud = _mx(arrays[3])
    cat, mix, h1, n2 = _mix_call(o_f, o_b, ga, o_s, seq, gla_norm, w_out_full, norm_mix_post, norm_mlp_pre)
    a, rz, dh2, dff, loss, d_post2 = _mlp_fwd_call(n2, h1, tgt, w_ud, norm_mlp_post)

    dz, dn2 = _mlp_bwd_call(dff, rz, w_ud)
    dw_down, dw_up4 = _mlp_wgrad_call(a, dff, n2, dz)
    dh1, do, dga, dos, dw_out, d_pre2, d_post, d_gn = _mix_bwd_call(
        dn2, dh2, h1, mix, cat, o_f, o_b, ga, gla_norm, norm_mix_post, norm_mlp_pre, w_out_full)
    done = {}

    def swa_backward(tok):
        done["swa"] = _swa_bwd_call(qs, ks_p, vs_p, bias, sink1, dos, dep=tok)
        return done["swa"][0]

    def gla_in_backward(tok):
        done["gla"] = _gla_bwd_call(qa, ka, va, za, do, s_f, s_b, wgf_p, bf_p, wgb_p, bb_p, dep=tok)
        dqf, dkf, dvf, dzf, _, _, dqb, dkb, dvb, dzb, _, _ = done["gla"]
        dqs, dks_p, dvs_p, _, _ = done["swa"]
        done["in"] = _in_bwd_call(
            seq, dh1, norm_mix_pre, w_in_t,
            pairs=[(_side_by_side(T_QA), (dqf, dqb)), (_side_by_side(T_KA), (dkf, dkb)), (T_VA, (dvf, dvb)),
                   (T_ZA, (dzf, dzb))],
            singles=[(T_GA, dga), (_side_by_side(T_QS), dqs)], halos=[(T_KS, dks_p), (T_VS, dvs_p)])
        return done["in"][0]

    def bias_backward(tok):
        done["rel"] = _relbias_call(done["swa"][3], done["swa"][4], buckets, dep=tok)
        return done["rel"][0]

    g_up, g_down, g_out = _reduce_to_owners(
        [dw_up4, dw_down.reshape(N_CHIPS, R_DOWN, D_MODEL), dw_out.reshape(N_CHIPS, R_OUT, D_MODEL)],
        [ROWS, ROWS, ROWS], pos, "mlp", [swa_backward, gla_in_backward, bias_backward])
    dx, dw_in_t, d_pre = done["in"]
    dwf, dbf, dwb, dbb = done["gla"][4], done["gla"][5], done["gla"][10], done["gla"][11]
    drel, dsink = done["rel"]

    small_grads = [d_pre, d_post, d_pre2, d_post2, _unpad_heads(dbf, 4), _unpad_heads(dbb, 4), d_gn, dsink, drel]
    gate_grads = [_unpad_heads(dwf[:GLA_GATE_RANK], 4), _unpad_heads(dwb[GLA_GATE_RANK:2 * GLA_GATE_RANK], 4)]
    small_params = [(given[n], given["m_" + n], given["v_" + n]) for n in SMALL_NAMES]
    upd = {}

    def update_up(tok):
        upd["w_up"] = (g_up,) + tuple(_adamw_call(w_up[0], g_up, m_w_up[0], v_w_up[0], "adamw_w_up", dep=tok))
        return upd["w_up"][1]

    def update_small(tok):
        per_name, gf_sum, gb_sum, upd["loss"] = _small_update_call(small_grads, gate_grads + [loss], small_params,
                                                                   dep=tok)
        upd.update(dict(zip(SMALL_NAMES, per_name)))
        for name, total in (("w_gate_up_fwd", gf_sum), ("w_gate_up_bwd", gb_sum)):
            g = lax.dynamic_slice(total, (0, chip * 64), (GLA_GATE_RANK, 64))
            upd[name] = (g,) + tuple(_adamw_call(given[name][0], g, given["m_" + name][0], given["v_" + name][0],
                                                 "adamw_" + name))
        upd["w_down"] = (g_down,) + tuple(
            _adamw_call(w_down[0], g_down, m_w_down[0], v_w_down[0], "adamw_w_down", dep=gf_sum))
        return upd["w_down"][1]

    def update_out(tok):
        upd["w_out"] = (g_out,) + tuple(_adamw_call(w_out[0], g_out, m_w_out[0], v_w_out[0], "adamw_w_out", dep=tok))
        return upd["w_out"][1]

    (g_in_t,) = _reduce_to_owners([dw_in_t.reshape(N_CHIPS, R_IN, D_MODEL)], [COLS], pos, "in",
                                  [update_up, update_small, update_out])
    in_t = (g_in_t,) + tuple(_adamw_call(w_in[0].T, g_in_t, m_w_in[0].T, v_w_in[0].T, "adamw_w_in"))
    upd["w_in"] = tuple(t.T for t in in_t)

    big = ("w_in", "w_gate_up_fwd", "w_gate_up_bwd", "w_out", "w_up", "w_down")
    names = ["norm_mix_pre", "w_in", "w_gate_up_fwd", "b_gate_fwd", "w_gate_up_bwd", "b_gate_bwd", "gla_norm",
             "swa_sink", "rel_bias", "w_out", "norm_mix_post", "norm_mlp_pre", "w_up", "w_down", "norm_mlp_post"]
    outs = [upd["loss"][0, 0], dx[None]]
    for kind in range(4):
        outs += [upd[n][kind][None] if n in big else upd[n][kind] for n in names]
    return tuple(outs)
```

```python
import math

import numpy as np
import jax
import jax.numpy as jnp
from jax import lax
from jax.experimental import pallas as pl
from jax.experimental.pallas import tpu as pltpu

F32 = jnp.float32
MXU_DTYPE = jnp.bfloat16
COMM_DTYPE = jnp.bfloat16

D_MODEL = 1024
D_FF = 4096
N_CHIPS = 4
GLA_HEADS = 4
GLA_CHUNK = 64
GLA_GATE_RANK = 16
GLA_GATE_NORM = 16.0
SWA_Q_HEADS = 8
SWA_KV_HEADS = 2
SWA_BLOCK = 128
REL_BUCKETS = 32
REL_MAX_DIST = 128
NORM_EPS = 1e-6
HEAD_PAD = 128

ADAM_LR = 0.001
ADAM_B1 = 0.9
ADAM_B2 = 0.999
ADAM_EPS = 1e-08
ADAM_WD = 0.01
ADAM_STEP = 10

OUT_PAD = 1024

R_IN, R_OUT, R_UP, R_DOWN = 584, 256, 1024, 1024

VMEM_BIG = 56 * 1024 * 1024
MESH_AXES = ("x", "y", "c")
MESH_ID = pl.DeviceIdType.MESH


def _mx(a):
    return a.astype(MXU_DTYPE)


def _dot(a, b):
    return jnp.dot(a, b, preferred_element_type=F32)


def _dot_nt(a, b):
    return lax.dot_general(a, b, (((1,), (1,)), ((), ())), preferred_element_type=F32)


def _dot_tn(a, b):
    return lax.dot_general(a, b, (((0,), (0,)), ((), ())), preferred_element_type=F32)


def _rms_r(x):
    return lax.rsqrt(jnp.mean(x * x, axis=-1, keepdims=True) + NORM_EPS)


def _rms_bwd(x, r, g, dy):
    xh = x * r
    gdy = dy * g
    dx = r * (gdy - xh * jnp.mean(gdy * xh, axis=-1, keepdims=True))
    return dx, jnp.sum(dy * xh, axis=0, keepdims=True)


def _low_half(rows):
    return lax.broadcasted_iota(jnp.int32, (rows, HEAD_PAD), 1) < 64


def _spread_heads(x):
    low = _low_half(x.shape[0])
    parts = []
    for p in range(x.shape[1] // HEAD_PAD):
        pair = x[:, HEAD_PAD * p:HEAD_PAD * (p + 1)]
        parts += [jnp.where(low, pair, 0.0), jnp.where(low, pltpu.roll(pair, 64, 1), 0.0)]
    return jnp.concatenate(parts, axis=1)


def _squeeze_heads(x):
    low = _low_half(x.shape[0])
    parts = []
    for p in range(x.shape[1] // (2 * HEAD_PAD)):
        even = x[:, 2 * HEAD_PAD * p:2 * HEAD_PAD * p + HEAD_PAD]
        odd = x[:, 2 * HEAD_PAD * p + HEAD_PAD:2 * HEAD_PAD * (p + 1)]
        parts.append(jnp.where(low, even, pltpu.roll(odd, 64, 1)))
    return parts[0] if len(parts) == 1 else jnp.concatenate(parts, axis=1)


def _params(sem=None, vmem=None):
    kw = {}
    if sem is not None:
        kw["dimension_semantics"] = sem
    if vmem is not None:
        kw["vmem_limit_bytes"] = vmem
    return pltpu.CompilerParams(**kw)


def _vmem_spec():
    return pl.BlockSpec(memory_space=pltpu.VMEM)


def _whole_spec(shape):
    return pl.BlockSpec(shape, lambda: (0,) * len(shape))


def _row_spec(tm, width):
    return pl.BlockSpec((tm, width), lambda i: (i, 0))


def _full_spec(shape):
    return pl.BlockSpec(shape, lambda i: (0,) * len(shape))


def _any_spec():
    return pl.BlockSpec(memory_space=pl.ANY)


def _after(body, n_in, dep):
    if dep is None:
        return body, [], []
    return (lambda *refs: body(*refs[:n_in], *refs[n_in + 1:])), [dep], [_any_spec()]


T_QA, T_KA, T_VA, T_GA = (0, 256, 4), (256, 256, 4), (512, 512, 0), (1024, 512, 0)
T_QS, T_KS, T_VS = (1568, 512, 8), (2080, 128, 2), (2208, 128, 2)
T_ZA = (1536, 128, 0)
ZA_COLS = 2 * GLA_GATE_RANK
IN_COLS = 2336


def _side_by_side(group):
    return group[0], group[1], 0


def _proj_call(x, g_pre, w_in_t, dep=None):
    L = x.shape[0]
    tm = min(512, L)
    groups = [(T_QA, F32), (T_KA, F32), (T_VA, MXU_DTYPE), (T_GA, F32),
              (T_QS, MXU_DTYPE), (T_KS, MXU_DTYPE), (T_VS, MXU_DTYPE), (T_ZA, F32)]
    widths = [rows * (2 if heads else 1) for (_, rows, heads), _ in groups]

    def body(x_ref, g_ref, w_ref, *outs):
        xv = x_ref[...]
        u = _mx(xv * _rms_r(xv) * g_ref[...])
        for ref, (grp, _) in zip(outs, groups):
            first, rows, heads = grp
            val = _dot_nt(u, w_ref[first:first + rows, :])
            if heads:
                val = _spread_heads(val)
            if grp is T_ZA:
                val = jnp.where(lax.broadcasted_iota(jnp.int32, val.shape, 1) < ZA_COLS, val, 0.0)
            if grp is T_QS:
                val = val * 0.125
            ref[...] = val.astype(ref.dtype)

    body, extra, extra_specs = _after(body, 3, dep)
    return pl.pallas_call(
        body, name="proj_fwd", grid=(L // tm,),
        in_specs=[_row_spec(tm, D_MODEL), _full_spec((1, D_MODEL)), _vmem_spec()] + extra_specs,
        out_specs=[_row_spec(tm, w) for w in widths],
        out_shape=[jax.ShapeDtypeStruct((L, w), dt) for w, (_, dt) in zip(widths, groups)],
        compiler_params=_params(("arbitrary",), VMEM_BIG),
    )(x, g_pre, w_in_t, *extra)


def _tri_masks():
    row = lax.broadcasted_iota(jnp.int32, (GLA_CHUNK, GLA_CHUNK), 0)
    col = lax.broadcasted_iota(jnp.int32, (GLA_CHUNK, GLA_CHUNK), 1)
    return row >= col, row <= col


def _chunk_sums(tri_m, x):
    hi = _mx(x)
    rest = x - hi.astype(F32)
    mid = _mx(rest)
    lo = _mx(rest - mid.astype(F32))
    return _dot(tri_m, hi) + _dot(tri_m, mid) + _dot(tri_m, lo)


def _gla_block_pre(q_r, k_r, z_r, w_r, b_r, rev, nc, qd_s, ki_s, ks_s, dec_s, keep=None):
    tri_f, tri_b = _tri_masks()
    tri_m = _mx((tri_b if rev else tri_f).astype(F32))
    g = _dot(_mx(z_r[...]), w_r[...]) + b_r[...]
    la = (jnp.minimum(g, 0.0) - jnp.log(1.0 + jnp.exp(-jnp.abs(g)))) / GLA_GATE_NORM
    sums, lasts = [], []
    for c in range(nc):
        b_c = _chunk_sums(tri_m, la[GLA_CHUNK * c:GLA_CHUNK * (c + 1)])
        blast = b_c[0:1] if rev else b_c[GLA_CHUNK - 1:GLA_CHUNK]
        dec_s[c] = jnp.exp(blast)
        sums.append(b_c)
        lasts.append(jnp.broadcast_to(blast, b_c.shape))
    b = jnp.concatenate(sums, axis=0)
    eb = jnp.exp(b)
    enb = jnp.exp(-b)
    elb = jnp.exp(jnp.concatenate(lasts, axis=0) - b)
    k = k_r[...]
    qd_s[...] = (q_r[...] * 0.125 * eb).astype(qd_s.dtype)
    ki_s[...] = (k * enb).astype(ki_s.dtype)
    ks_s[...] = (k * elb).astype(ks_s.dtype)
    if keep is not None:
        for ref, val in zip(keep, (g, eb, enb, elb)):
            ref[...] = val


def _gla_fwd_call(qa, ka, va, za, wgf, bgf, wgb, bgb):
    L = qa.shape[0]
    br = min(512, L)
    nb, nc, n_chunks = L // br, br // GLA_CHUNK, L // GLA_CHUNK
    hw = GLA_HEADS * HEAD_PAD

    def body(qaf, kaf, vaf, zaf, qab, kab, vab, zab, wgf_r, bgf_r, wgb_r, bgb_r,
             of_r, ob_r, sf_r, sb_r, st_f, st_b, pre_f, pre_b):
        @pl.when(pl.program_id(0) == 0)
        def _():
            st_f[...] = jnp.zeros_like(st_f)
            st_b[...] = jnp.zeros_like(st_b)

        _gla_block_pre(qaf, kaf, zaf, wgf_r, bgf_r, False, nc, *pre_f)
        _gla_block_pre(qab, kab, zab, wgb_r, bgb_r, True, nc, *pre_b)
        tri_f, tri_b = _tri_masks()

        def one(tri, pre, v_r, o_r, s_r, st, ci):
            qd_s, ki_s, ks_s, dec_s = pre
            rows = pl.ds(pl.multiple_of(ci * GLA_CHUNK, GLA_CHUNK), GLA_CHUNK)
            dec = dec_s[ci]
            heads = range(GLA_HEADS)
            lanes = [slice(HEAD_PAD * h, HEAD_PAD * (h + 1)) for h in heads]
            qd = [qd_s[rows, sl] for sl in lanes]
            v = [v_r[rows, sl] for sl in lanes]
            s_t = [st[h] for h in heads]
            a = [_dot_nt(qd[h], ki_s[rows, lanes[h]]) for h in heads]
            carried = [_dot_nt(qd[h], _mx(s_t[h])) for h in heads]
            grown = [_dot_tn(v[h], ks_s[rows, lanes[h]]) for h in heads]
            a = [_mx(jnp.where(tri, a[h], 0.0)) for h in heads]
            inner = [_dot(a[h], v[h]) for h in heads]
            for h in heads:
                s_r[ci, h] = s_t[h].astype(s_r.dtype)
                o_r[rows, lanes[h]] = inner[h] + carried[h]
                st[h] = s_t[h] * dec[:, lanes[h]] + grown[h]

        def loop(t, carry):
            one(tri_f, pre_f, vaf, of_r, sf_r, st_f, t)
            one(tri_b, pre_b, vab, ob_r, sb_r, st_b, nc - 1 - t)
            return carry

        lax.fori_loop(0, nc, loop, 0, unroll=True)

    fwd = lambda i: (i, 0)
    bwd = lambda i: (nb - 1 - i, 0)
    ins = lambda m: [pl.BlockSpec((br, hw), m), pl.BlockSpec((br, hw), m),
                     pl.BlockSpec((br, hw), m), pl.BlockSpec((br, 128), m)]
    wspecs = [_full_spec((128, hw)), _full_spec((1, hw))] * 2
    s_shape = (nc, GLA_HEADS, HEAD_PAD, HEAD_PAD)
    pre_scratch = [pltpu.VMEM((br, hw), MXU_DTYPE)] * 3 + [pltpu.VMEM((nc, 1, hw), F32)]
    return pl.pallas_call(
        body, name="gla_fwd", grid=(nb,),
        in_specs=ins(fwd) + ins(bwd) + wspecs,
        out_specs=[pl.BlockSpec((br, hw), fwd), pl.BlockSpec((br, hw), bwd),
                   pl.BlockSpec(s_shape, lambda i: (i, 0, 0, 0)),
                   pl.BlockSpec(s_shape, lambda i: (nb - 1 - i, 0, 0, 0))],
        out_shape=[jax.ShapeDtypeStruct((L, hw), F32), jax.ShapeDtypeStruct((L, hw), F32),
                   jax.ShapeDtypeStruct((n_chunks,) + s_shape[1:], MXU_DTYPE),
                   jax.ShapeDtypeStruct((n_chunks,) + s_shape[1:], MXU_DTYPE)],
        scratch_shapes=[pltpu.VMEM(s_shape[1:], F32), pltpu.VMEM(s_shape[1:], F32), pre_scratch, pre_scratch],
        compiler_params=_params(("arbitrary",), VMEM_BIG),
    )(qa, ka, va, za, qa, ka, va, za, wgf, bgf, wgb, bgb)


def _gla_bwd_call(qa, ka, va, za, do, sf, sb, wgf, bgf, wgb, bgb, dep=None):
    L = qa.shape[0]
    br = min(512, L)
    nb, nc = L // br, br // GLA_CHUNK
    hw = GLA_HEADS * HEAD_PAD

    def body(qaf, kaf, vaf, zaf, dof, sf_r, qab, kab, vab, zab, dob, sb_r, wgf_r, bgf_r, wgb_r, bgb_r,
             dqf, dkf, dvf, dzf, dwf, dbf, dqb, dkb, dvb, dzb, dwb, dbb, gt_f, gt_b, pre_f, pre_b):
        @pl.when(pl.program_id(0) == 0)
        def _():
            for ref in (gt_f, gt_b, dwf, dbf, dwb, dbb):
                ref[...] = jnp.zeros_like(ref)

        _gla_block_pre(qaf, kaf, zaf, wgf_r, bgf_r, False, nc, *pre_f[:4], keep=pre_f[4:8])
        _gla_block_pre(qab, kab, zab, wgb_r, bgb_r, True, nc, *pre_b[:4], keep=pre_b[4:8])
        tri_f, tri_b = _tri_masks()
        row_w = lax.broadcasted_iota(jnp.int32, (GLA_CHUNK, HEAD_PAD), 0)

        def one(rev, pre, q_r, k_r, v_r, do_r, s_r, dq_r, dk_r, dv_r, gt, ci):
            qd_s, ki_s, ks_s, dec_s, _, eb_s, enb_s, elb_s, db_s = pre
            tri = tri_b if rev else tri_f
            last_row = 0 if rev else GLA_CHUNK - 1
            rows = pl.ds(pl.multiple_of(ci * GLA_CHUNK, GLA_CHUNK), GLA_CHUNK)
            dec = dec_s[ci]
            heads = range(GLA_HEADS)
            lanes = [slice(HEAD_PAD * h, HEAD_PAD * (h + 1)) for h in heads]
            qd = [qd_s[rows, sl] for sl in lanes]
            ki = [ki_s[rows, sl] for sl in lanes]
            ks = [ks_s[rows, sl] for sl in lanes]
            v = [v_r[rows, sl] for sl in lanes]
            do_h = [_mx(do_r[rows, sl]) for sl in lanes]
            s_t = [s_r[ci, h] for h in heads]
            g_t = [gt[h] for h in heads]
            g_m = [_mx(g_t[h]) for h in heads]
            a = [_dot_nt(qd[h], ki[h]) for h in heads]
            da = [_dot_nt(do_h[h], v[h]) for h in heads]
            dv_carried = [_dot_nt(ks[h], g_m[h]) for h in heads]
            dqd_carried = [_dot(do_h[h], _mx(s_t[h])) for h in heads]
            dks = [_dot(v[h], g_m[h]) for h in heads]
            g_grown = [_dot_tn(do_h[h], qd[h]) for h in heads]
            a = [_mx(jnp.where(tri, a[h], 0.0)) for h in heads]
            da = [_mx(jnp.where(tri, da[h], 0.0)) for h in heads]
            dv_inner = [_dot_tn(a[h], do_h[h]) for h in heads]
            dqd_inner = [_dot(da[h], ki[h]) for h in heads]
            dki = [_dot_tn(da[h], qd[h]) for h in heads]
            dq, dk = [], []
            for h in heads:
                sl = lanes[h]
                dv_r[rows, sl] = (dv_inner[h] + dv_carried[h]).astype(dv_r.dtype)
                ddec = jnp.sum(g_t[h] * s_t[h].astype(F32), axis=0, keepdims=True)
                gt[h] = g_t[h] * dec[:, sl] + g_grown[h]
                dq.append((dqd_inner[h] + dqd_carried[h]) * eb_s[rows, sl] * 0.125)
                dk_state = dks[h] * elb_s[rows, sl]
                dk.append(dki[h] * enb_s[rows, sl] + dk_state)
                k = k_r[rows, sl]
                dblast = jnp.sum(dk_state * k, axis=0, keepdims=True) + dec[:, sl] * ddec
                db_s[rows, sl] = q_r[rows, sl] * dq[h] - k * dk[h] + jnp.where(row_w == last_row, dblast, 0.0)
            low = _low_half(GLA_CHUNK)
            for pair in range(GLA_HEADS // 2):
                psl = slice(HEAD_PAD * pair, HEAD_PAD * (pair + 1))
                for ref, val in ((dq_r, dq), (dk_r, dk)):
                    both = jnp.where(low, val[2 * pair], pltpu.roll(val[2 * pair + 1], 64, 1))
                    ref[rows, psl] = both.astype(ref.dtype)

        def loop(t, carry):
            one(False, pre_f, qaf, kaf, vaf, dof, sf_r, dqf, dkf, dvf, gt_f, nc - 1 - t)
            one(True, pre_b, qab, kab, vab, dob, sb_r, dqb, dkb, dvb, gt_b, t)
            return carry

        lax.fori_loop(0, nc, loop, 0, unroll=True)

        def gate_grads(rev, pre, z_r, w_r, dz_r, dw_r, dbias_r):
            g_s, db_s = pre[4], pre[8]
            back_m = _mx((tri_f if rev else tri_b).astype(F32))
            db = db_s[...]
            dla = jnp.concatenate([_chunk_sums(back_m, db[GLA_CHUNK * c:GLA_CHUNK * (c + 1)]) for c in range(nc)],
                                  axis=0)
            dg = dla * (1.0 / GLA_GATE_NORM) * (1.0 / (1.0 + jnp.exp(g_s[...])))
            dg_m = _mx(dg)
            dz_r[...] = _dot_nt(dg_m, w_r[...])
            dw_r[...] += _dot_tn(_mx(z_r[...]), dg_m)
            dbias_r[...] += jnp.sum(dg, axis=0, keepdims=True)

        gate_grads(False, pre_f, zaf, wgf_r, dzf, dwf, dbf)
        gate_grads(True, pre_b, zab, wgb_r, dzb, dwb, dbb)

    last_first = lambda i: (nb - 1 - i, 0)
    first_last = lambda i: (i, 0)
    s_shape = (nc, GLA_HEADS, HEAD_PAD, HEAD_PAD)

    def ins(m):
        return [pl.BlockSpec((br, hw), m), pl.BlockSpec((br, hw), m), pl.BlockSpec((br, hw), m),
                pl.BlockSpec((br, 128), m), pl.BlockSpec((br, hw), m),
                pl.BlockSpec(s_shape, lambda i: m(i) + (0, 0))]

    def outs(m):
        return [pl.BlockSpec((br, hw // 2), m), pl.BlockSpec((br, hw // 2), m), pl.BlockSpec((br, hw), m),
                pl.BlockSpec((br, 128), m), _full_spec((128, hw)), _full_spec((1, hw))]

    out_shape = [jax.ShapeDtypeStruct((L, hw // 2), MXU_DTYPE)] * 2 + [
        jax.ShapeDtypeStruct((L, hw), MXU_DTYPE),
        jax.ShapeDtypeStruct((L, 128), F32), jax.ShapeDtypeStruct((128, hw), F32),
        jax.ShapeDtypeStruct((1, hw), F32)]
    wspecs = [_full_spec((128, hw)), _full_spec((1, hw))] * 2
    body, extra, extra_specs = _after(body, 16, dep)
    pre_scratch = ([pltpu.VMEM((br, hw), MXU_DTYPE)] * 3 + [pltpu.VMEM((nc, 1, hw), F32)]
                   + [pltpu.VMEM((br, hw), F32)] * 5)
    return pl.pallas_call(
        body, name="gla_bwd", grid=(nb,),
        in_specs=ins(last_first) + ins(first_last) + wspecs + extra_specs,
        out_specs=outs(last_first) + outs(first_last),
        out_shape=out_shape + out_shape,
        scratch_shapes=[pltpu.VMEM(s_shape[1:], F32), pltpu.VMEM(s_shape[1:], F32), pre_scratch, pre_scratch],
        compiler_params=_params(("arbitrary",), VMEM_BIG),
    )(qa, ka, va, za, do, sf, qa, ka, va, za, do, sb, wgf, bgf, wgb, bgb, *extra)


def _t5_buckets(rel):
    nb = REL_BUCKETS // 2
    ret = (rel > 0).astype(np.int32) * nb
    n = np.abs(rel)
    max_exact = nb // 2
    large = max_exact + (np.log(np.maximum(n, 1).astype(np.float32) / max_exact)
                         / math.log(REL_MAX_DIST / max_exact) * (nb - max_exact)).astype(np.int32)
    large = np.minimum(large, nb - 1)
    return ret + np.where(n < max_exact, n, large)


SWA_GROUP = SWA_Q_HEADS // SWA_KV_HEADS
SWA_SPAN = 3 * SWA_BLOCK
SWA_GROUP_LANES = SWA_GROUP * SWA_BLOCK


def _band_buckets():
    s = np.arange(SWA_SPAN)[:, None]
    c = np.arange(SWA_BLOCK)[None, :]
    return _t5_buckets(s - SWA_BLOCK - c).astype(np.int32)


def _swa_valid(n, seq_len):
    key_pos = (n - 1) * SWA_BLOCK + lax.broadcasted_iota(jnp.int32, (SWA_SPAN, 1), 0)
    return (key_pos >= 0) & (key_pos < seq_len)


def _swa_sink_row(sink_r, kv):
    lane = lax.broadcasted_iota(jnp.int32, (1, SWA_GROUP_LANES), 1)
    row = jnp.full((1, SWA_GROUP_LANES), sink_r[kv * SWA_GROUP], F32)
    for g in range(1, SWA_GROUP):
        row = jnp.where(lane >= g * SWA_BLOCK, sink_r[kv * SWA_GROUP + g], row)
    return row


def _swa_group(ref, kv):
    first = kv * SWA_GROUP
    return jnp.concatenate([ref[:, HEAD_PAD * h:HEAD_PAD * (h + 1)] for h in range(first, first + SWA_GROUP)],
                           axis=0)


def _swa_softmax(scores, bias_t, sink_row, valid):
    st = jnp.where(valid, scores + bias_t, -1e30)
    m = jnp.maximum(jnp.max(st, axis=0, keepdims=True), sink_row)
    p = jnp.exp(st - m)
    e_sink = jnp.exp(sink_row - m)
    inv = 1.0 / (jnp.sum(p, axis=0, keepdims=True) + e_sink)
    return p * inv, e_sink * inv


def _swa_fwd_call(qs, ks, vs, bias, sink, dep=None):
    L = qs.shape[0]

    def body(q_r, k_r, v_r, bias_r, sink_r, o_r):
        n = pl.program_id(0)
        span = pl.ds(pl.multiple_of(n * SWA_BLOCK, SWA_BLOCK), SWA_SPAN)
        valid = _swa_valid(n, L)
        groups = range(SWA_KV_HEADS)
        lanes = [slice(HEAD_PAD * kv, HEAD_PAD * (kv + 1)) for kv in groups]
        scores = [_dot_nt(k_r[span, lanes[kv]], _swa_group(q_r, kv)) for kv in groups]
        probs = [_swa_softmax(scores[kv], bias_r[kv], _swa_sink_row(sink_r, kv), valid)[0] for kv in groups]
        low = _low_half(SWA_BLOCK)
        for kv in groups:
            og = _dot_tn(_mx(probs[kv]), v_r[span, lanes[kv]])
            for pair in range(SWA_GROUP // 2):
                even = og[2 * SWA_BLOCK * pair:2 * SWA_BLOCK * pair + SWA_BLOCK]
                odd = og[2 * SWA_BLOCK * pair + SWA_BLOCK:2 * SWA_BLOCK * (pair + 1)]
                first = HEAD_PAD * (kv * SWA_GROUP // 2 + pair)
                o_r[:, first:first + HEAD_PAD] = jnp.where(low, even, pltpu.roll(odd, 64, 1)).astype(o_r.dtype)

    qw = SWA_Q_HEADS * HEAD_PAD
    body, extra, extra_specs = _after(body, 5, dep)
    return pl.pallas_call(
        body, name="swa_fwd", grid=(L // SWA_BLOCK,),
        in_specs=[_row_spec(SWA_BLOCK, qw), _vmem_spec(), _vmem_spec(), _vmem_spec(),
                  pl.BlockSpec(memory_space=pltpu.SMEM)] + extra_specs,
        out_specs=_row_spec(SWA_BLOCK, qw // 2),
        out_shape=jax.ShapeDtypeStruct((L, qw // 2), MXU_DTYPE),
        compiler_params=_params(("arbitrary",), VMEM_BIG),
    )(qs, ks, vs, bias, sink, *extra)


def _swa_bwd_call(qs, ks, vs, bias, sink, do, dep=None):
    L = qs.shape[0]
    qw = SWA_Q_HEADS * HEAD_PAD
    kw = SWA_KV_HEADS * HEAD_PAD

    def body(q_r, k_r, v_r, bias_r, sink_r, do_r, dq_r, dk_r, dv_r, dbias_r, dsink_r):
        n = pl.program_id(0)

        @pl.when(n == 0)
        def _():
            for ref in (dk_r, dv_r, dbias_r, dsink_r):
                ref[...] = jnp.zeros_like(ref)

        span = pl.ds(pl.multiple_of(n * SWA_BLOCK, SWA_BLOCK), SWA_SPAN)
        valid = _swa_valid(n, L)
        groups = range(SWA_KV_HEADS)
        lanes = [slice(HEAD_PAD * kv, HEAD_PAD * (kv + 1)) for kv in groups]
        kk = [k_r[span, sl] for sl in lanes]
        vv = [v_r[span, sl] for sl in lanes]
        qg = [_swa_group(q_r, kv) for kv in groups]
        dog = [_swa_group(do_r, kv) for kv in groups]
        scores = [_dot_nt(kk[kv], qg[kv]) for kv in groups]
        dp = [_dot_nt(vv[kv], dog[kv]) for kv in groups]
        probs = [_swa_softmax(scores[kv], bias_r[kv], _swa_sink_row(sink_r, kv), valid) for kv in groups]
        ds_m, pn_m = [], []
        for kv in groups:
            pn, p_sink = probs[kv]
            delta = jnp.sum(pn * dp[kv], axis=0, keepdims=True)
            ds = pn * (dp[kv] - delta)
            dsink_r[kv] -= p_sink * delta
            dbias_r[kv] += ds
            ds_m.append(_mx(ds))
            pn_m.append(_mx(pn))
        dqg = [_dot_tn(ds_m[kv], kk[kv]) * 0.125 for kv in groups]
        dkk = [_dot(ds_m[kv], qg[kv]) for kv in groups]
        dvv = [_dot(pn_m[kv], dog[kv]) for kv in groups]
        low = _low_half(SWA_BLOCK)
        for kv in groups:
            for pair in range(SWA_GROUP // 2):
                even = dqg[kv][2 * SWA_BLOCK * pair:2 * SWA_BLOCK * pair + SWA_BLOCK]
                odd = dqg[kv][2 * SWA_BLOCK * pair + SWA_BLOCK:2 * SWA_BLOCK * (pair + 1)]
                first = HEAD_PAD * (kv * SWA_GROUP // 2 + pair)
                dq_r[:, first:first + HEAD_PAD] = jnp.where(low, even, pltpu.roll(odd, 64, 1)).astype(dq_r.dtype)
            dk_r[span, lanes[kv]] += dkk[kv]
            dv_r[span, lanes[kv]] += dvv[kv]

    body, extra, extra_specs = _after(body, 6, dep)
    return pl.pallas_call(
        body, name="swa_bwd", grid=(L // SWA_BLOCK,),
        in_specs=[_row_spec(SWA_BLOCK, qw), _vmem_spec(), _vmem_spec(), _vmem_spec(),
                  pl.BlockSpec(memory_space=pltpu.SMEM), _row_spec(SWA_BLOCK, qw)] + extra_specs,
        out_specs=[_row_spec(SWA_BLOCK, qw // 2), _vmem_spec(), _vmem_spec(), _vmem_spec(), _vmem_spec()],
        out_shape=[jax.ShapeDtypeStruct((L, qw // 2), MXU_DTYPE),
                   jax.ShapeDtypeStruct((L + 2 * SWA_BLOCK, kw), F32),
                   jax.ShapeDtypeStruct((L + 2 * SWA_BLOCK, kw), F32),
                   jax.ShapeDtypeStruct((SWA_KV_HEADS, SWA_SPAN, SWA_GROUP_LANES), F32),
                   jax.ShapeDtypeStruct((SWA_KV_HEADS, 1, SWA_GROUP_LANES), F32)],
        compiler_params=_params(("arbitrary",), VMEM_BIG),
    )(qs, ks, vs, bias, sink, do, *extra)


def _bias_call(rel_bias, buckets):
    def body(t_r, bk_r, o_r):
        bk = bk_r[...]
        s = lax.broadcasted_iota(jnp.int32, bk.shape, 0)
        c = lax.broadcasted_iota(jnp.int32, bk.shape, 1)
        in_band = jnp.abs(s - SWA_BLOCK - c) <= SWA_BLOCK
        for h in range(SWA_Q_HEADS):
            acc = jnp.zeros(bk.shape, F32)
            for b in range(REL_BUCKETS):
                acc = jnp.where(bk == b, t_r[b, h], acc)
            g = h % SWA_GROUP
            o_r[h // SWA_GROUP, :, SWA_BLOCK * g:SWA_BLOCK * (g + 1)] = jnp.where(in_band, acc, -1e30)

    return pl.pallas_call(
        body, name="band_bias",
        in_specs=[pl.BlockSpec(memory_space=pltpu.SMEM), _vmem_spec()], out_specs=_vmem_spec(),
        out_shape=jax.ShapeDtypeStruct((SWA_KV_HEADS, SWA_SPAN, SWA_GROUP_LANES), F32),
    )(rel_bias, buckets)


def _relbias_call(dbias, dsink, buckets, dep=None):
    def body(db_r, ds_r, bk_r, o_r, os_r):
        bk = bk_r[...]
        rowi = lax.broadcasted_iota(jnp.int32, (REL_BUCKETS, 128), 0)
        lanei = lax.broadcasted_iota(jnp.int32, (REL_BUCKETS, 128), 1)
        lane1 = lax.broadcasted_iota(jnp.int32, (1, 128), 1)
        acc = jnp.zeros((REL_BUCKETS, 128), F32)
        acc_sink = jnp.zeros((1, 128), F32)
        for h in range(SWA_Q_HEADS):
            kv, g = h // SWA_GROUP, h % SWA_GROUP
            lanes = slice(SWA_BLOCK * g, SWA_BLOCK * (g + 1))
            part = db_r[kv, :, lanes]
            for b in range(REL_BUCKETS):
                s = jnp.sum(jnp.where(bk == b, part, 0.0))
                acc = acc + jnp.where((rowi == b) & (lanei == h), s, 0.0)
            acc_sink = acc_sink + jnp.where(lane1 == h, jnp.sum(ds_r[kv, :, lanes]), 0.0)
        o_r[...] = acc
        os_r[...] = acc_sink

    body, extra, extra_specs = _after(body, 3, dep)
    return pl.pallas_call(
        body, name="relbias_grad",
        in_specs=[_vmem_spec()] * 3 + extra_specs, out_specs=[_vmem_spec()] * 2,
        out_shape=[jax.ShapeDtypeStruct((REL_BUCKETS, 128), F32), jax.ShapeDtypeStruct((1, 128), F32)],
    )(dbias, dsink, buckets, *extra)


def _mix_call(o_f, o_b, ga, o_s, x, gn, w_out_p, g_post, g_pre2):
    L = x.shape[0]
    tm = min(512, L)
    hw = GLA_HEADS * HEAD_PAD

    def body(of_r, ob_r, ga_r, os_r, x_r, gn_r, w_r, gp_r, g2_r, cat_r, mix_r, h1_r, n2_r):
        gn_v = gn_r[...]
        for h in range(GLA_HEADS):
            sl = slice(HEAD_PAD * h, HEAD_PAD * (h + 1))
            oh = of_r[:, sl] + ob_r[:, sl]
            on = oh * _rms_r(oh) * gn_v
            gate = ga_r[:, sl]
            cat_r[:, sl] = (on * (gate * jax.nn.sigmoid(gate))).astype(cat_r.dtype)
        os_v = os_r[...]
        cat_r[:, hw:] = os_v
        mix = _dot(cat_r[:, :hw], w_r[:hw, :]) + _dot(os_v, w_r[hw:, :])
        mix_r[...] = mix
        h1 = x_r[...] + mix * _rms_r(mix) * gp_r[...]
        h1_r[...] = h1
        n2_r[...] = (h1 * _rms_r(h1) * g2_r[...]).astype(n2_r.dtype)

    return pl.pallas_call(
        body, name="mix_fwd", grid=(L // tm,),
        in_specs=[_row_spec(tm, hw), _row_spec(tm, hw), _row_spec(tm, hw), _row_spec(tm, OUT_PAD - hw),
                  _row_spec(tm, D_MODEL), _full_spec((1, HEAD_PAD)), _vmem_spec(),
                  _full_spec((1, D_MODEL)), _full_spec((1, D_MODEL))],
        out_specs=[_row_spec(tm, OUT_PAD), _row_spec(tm, D_MODEL), _row_spec(tm, D_MODEL), _row_spec(tm, D_MODEL)],
        out_shape=[jax.ShapeDtypeStruct((L, OUT_PAD), MXU_DTYPE), jax.ShapeDtypeStruct((L, D_MODEL), F32),
                   jax.ShapeDtypeStruct((L, D_MODEL), F32), jax.ShapeDtypeStruct((L, D_MODEL), MXU_DTYPE)],
        compiler_params=_params(("arbitrary",), VMEM_BIG),
    )(o_f, o_b, ga, o_s, x, gn, w_out_p, g_post, g_pre2)


def _mlp_fwd_call(n2, h1, tgt, w_ud, g_post):
    L = n2.shape[0]
    tm = min(512, L)
    blk = D_FF // N_CHIPS

    def body(n2_r, h1_r, t_r, w_r, g_r, a_r, rz_r, dh2_r, dff_r, loss_r, dg_r):
        @pl.when(pl.program_id(0) == 0)
        def _():
            loss_r[...] = jnp.zeros_like(loss_r)
            dg_r[...] = jnp.zeros_like(dg_r)

        n2v = n2_r[...]
        ff = jnp.zeros((tm, D_MODEL), F32)
        for j in range(N_CHIPS):
            sl = slice(blk * j, blk * (j + 1))
            rz = jnp.maximum(_dot(n2v, w_r[j, 0]), 0.0)
            a = _mx(rz * rz)
            rz_r[:, sl] = rz.astype(rz_r.dtype)
            a_r[:, sl] = a
            ff = ff + _dot(a, w_r[j, 1])
        g = g_r[...]
        r = _rms_r(ff)
        err = h1_r[...] + ff * r * g - t_r[...]
        loss_r[...] += 0.5 * jnp.sum(err * err) / D_MODEL
        dh2 = err * (1.0 / D_MODEL)
        dh2_r[...] = dh2
        dff, dg = _rms_bwd(ff, r, g, dh2)
        dff_r[...] = dff.astype(dff_r.dtype)
        dg_r[...] += dg

    return pl.pallas_call(
        body, name="mlp_fwd", grid=(L // tm,),
        in_specs=[_row_spec(tm, D_MODEL), _row_spec(tm, D_MODEL), _row_spec(tm, D_MODEL),
                  _vmem_spec(), _full_spec((1, D_MODEL))],
        out_specs=[_row_spec(tm, D_FF), _row_spec(tm, D_FF), _row_spec(tm, D_MODEL), _row_spec(tm, D_MODEL),
                   _full_spec((1, 128)), _full_spec((1, D_MODEL))],
        out_shape=[jax.ShapeDtypeStruct((L, D_FF), MXU_DTYPE), jax.ShapeDtypeStruct((L, D_FF), MXU_DTYPE),
                   jax.ShapeDtypeStruct((L, D_MODEL), F32), jax.ShapeDtypeStruct((L, D_MODEL), MXU_DTYPE),
                   jax.ShapeDtypeStruct((1, 128), F32), jax.ShapeDtypeStruct((1, D_MODEL), F32)],
        compiler_params=_params(("arbitrary",), VMEM_BIG),
    )(n2, h1, tgt, w_ud, g_post)


def _mlp_bwd_call(dff, rz, w_ud):
    L = dff.shape[0]
    tm = min(512, L)
    blk = D_FF // N_CHIPS

    def body(dff_r, rz_r, w_r, dz_r, dn2_r):
        dffv = dff_r[...]
        dn2 = jnp.zeros((tm, D_MODEL), F32)
        for j in range(N_CHIPS):
            sl = slice(blk * j, blk * (j + 1))
            dz = _mx(_dot_nt(dffv, w_r[j, 1]) * 2.0 * rz_r[:, sl].astype(F32))
            dz_r[:, sl] = dz
            dn2 = dn2 + _dot_nt(dz, w_r[j, 0])
        dn2_r[...] = dn2

    return pl.pallas_call(
        body, name="mlp_bwd", grid=(L // tm,),
        in_specs=[_row_spec(tm, D_MODEL), _row_spec(tm, D_FF), _vmem_spec()],
        out_specs=[_row_spec(tm, D_FF), _row_spec(tm, D_MODEL)],
        out_shape=[jax.ShapeDtypeStruct((L, D_FF), MXU_DTYPE), jax.ShapeDtypeStruct((L, D_MODEL), F32)],
        compiler_params=_params(("arbitrary",), VMEM_BIG),
    )(dff, rz, w_ud)


def _mlp_wgrad_call(a, dff, n2, dz):
    L = a.shape[0]
    tf = 512
    per = (D_FF // N_CHIPS) // tf

    def body(a_r, dff_r, n2_r, dz_r, dwd_r, dwu_r):
        dwd_r[...] = _dot_tn(a_r[...], dff_r[...])
        dwu_r[...] = _dot_tn(n2_r[...], dz_r[...])

    return pl.pallas_call(
        body, name="mlp_wgrad", grid=(D_FF // tf,),
        in_specs=[pl.BlockSpec((L, tf), lambda j: (0, j)), _vmem_spec(), _vmem_spec(),
                  pl.BlockSpec((L, tf), lambda j: (0, j))],
        out_specs=[pl.BlockSpec((tf, D_MODEL), lambda j: (j, 0)),
                   pl.BlockSpec((None, D_MODEL, tf), lambda j: (j // per, 0, j % per))],
        out_shape=[jax.ShapeDtypeStruct((D_FF, D_MODEL), F32),
                   jax.ShapeDtypeStruct((N_CHIPS, D_MODEL, D_FF // N_CHIPS), F32)],
        compiler_params=_params(("arbitrary",), VMEM_BIG),
    )(a, dff, n2, dz)


def _mix_bwd_call(dn2, dh2, h1, mix, cat, o_f, o_b, ga, gn, g_post, g_pre2, w_out_p):
    L = dn2.shape[0]
    tm = min(512, L)
    hw = GLA_HEADS * HEAD_PAD

    def body(dn2_r, dh2_r, h1_r, mix_r, cat_r, of_r, ob_r, ga_r, gn_r, gp_r, g2_r, w_r,
             dh1_r, do_r, dga_r, dos_r, dw_r, dg2_r, dgp_r, dgn_r):
        @pl.when(pl.program_id(0) == 0)
        def _():
            for ref in (dw_r, dg2_r, dgp_r, dgn_r):
                ref[...] = jnp.zeros_like(ref)

        parts = [slice(start, start + min(256, tm)) for start in range(0, tm, 256)]
        dmix_m = []
        for rs in parts:
            h1 = h1_r[rs, :]
            dx2, dg2 = _rms_bwd(h1, _rms_r(h1), g2_r[...], dn2_r[rs, :])
            dh1 = dh2_r[rs, :] + dx2
            dh1_r[rs, :] = dh1
            dg2_r[...] += dg2
            mix = mix_r[rs, :]
            dmix, dgp = _rms_bwd(mix, _rms_r(mix), gp_r[...], dh1)
            dgp_r[...] += dgp
            dmix_m.append(_mx(dmix))
        dcat = [_dot_nt(d, w_r[...]) for d in dmix_m]
        for rs, d in zip(parts, dmix_m):
            dw_r[...] += _dot_tn(cat_r[rs, :], d)
        gn_v = gn_r[...]
        dgn = jnp.zeros((1, HEAD_PAD), F32)
        for rs, dc in zip(parts, dcat):
            dos_r[rs, :] = _spread_heads(dc[:, hw:]).astype(dos_r.dtype)
            for h in range(GLA_HEADS):
                sl = slice(HEAD_PAD * h, HEAD_PAD * (h + 1))
                oh = of_r[rs, sl] + ob_r[rs, sl]
                rr = _rms_r(oh)
                xh = oh * rr
                gate = ga_r[rs, sl]
                sg = jax.nn.sigmoid(gate)
                silu = gate * sg
                doa = dc[:, sl]
                dga_r[rs, sl] = (doa * (xh * gn_v) * (sg + silu * (1.0 - sg))).astype(dga_r.dtype)
                don = doa * silu
                gd = don * gn_v
                do_r[rs, sl] = rr * (gd - xh * jnp.mean(gd * xh, axis=-1, keepdims=True))
                dgn = dgn + jnp.sum(don * xh, axis=0, keepdims=True)
        dgn_r[...] += dgn

    return pl.pallas_call(
        body, name="mix_bwd", grid=(L // tm,),
        in_specs=[_row_spec(tm, D_MODEL)] * 4 + [_row_spec(tm, OUT_PAD)] + [_row_spec(tm, hw)] * 3
        + [_full_spec((1, HEAD_PAD)), _full_spec((1, D_MODEL)), _full_spec((1, D_MODEL)), _vmem_spec()],
        out_specs=[_row_spec(tm, D_MODEL), _row_spec(tm, hw), _row_spec(tm, hw),
                   _row_spec(tm, SWA_Q_HEADS * HEAD_PAD),
                   _full_spec((OUT_PAD, D_MODEL)), _full_spec((1, D_MODEL)), _full_spec((1, D_MODEL)),
                   _full_spec((1, HEAD_PAD))],
        out_shape=[jax.ShapeDtypeStruct((L, D_MODEL), F32), jax.ShapeDtypeStruct((L, hw), F32),
                   jax.ShapeDtypeStruct((L, hw), MXU_DTYPE),
                   jax.ShapeDtypeStruct((L, SWA_Q_HEADS * HEAD_PAD), MXU_DTYPE),
                   jax.ShapeDtypeStruct((OUT_PAD, D_MODEL), F32), jax.ShapeDtypeStruct((1, D_MODEL), F32),
                   jax.ShapeDtypeStruct((1, D_MODEL), F32), jax.ShapeDtypeStruct((1, HEAD_PAD), F32)],
        compiler_params=_params(("arbitrary",), VMEM_BIG),
    )(dn2, dh2, h1, mix, cat, o_f, o_b, ga, gn, g_post, g_pre2, w_out_p)


def _in_bwd_call(x, dh1, g_pre, w_in_t, pairs, singles, halos, dep=None):
    L = x.shape[0]
    tm = min(512, L)
    per = tm // SWA_BLOCK
    n_pair, n_single, n_halo = len(pairs), len(singles), len(halos)
    groups = [c for c, _ in pairs] + [c for c, _ in singles] + [c for c, _ in halos]

    def body(*refs):
        x_r, dh1_r, g_r, w_r = refs[:4]
        pair_refs = refs[4:4 + 2 * n_pair]
        single_refs = refs[4 + 2 * n_pair:4 + 2 * n_pair + n_single]
        halo_refs = refs[4 + 2 * n_pair + n_single:4 + 2 * n_pair + n_single + per * n_halo]
        dx_r, dw_r, dg_r = refs[4 + 2 * n_pair + n_single + per * n_halo:]

        @pl.when(pl.program_id(0) == 0)
        def _():
            dw_r[...] = jnp.zeros_like(dw_r)
            dg_r[...] = jnp.zeros_like(dg_r)

        xv = x_r[...]
        r = _rms_r(xv)
        g = g_r[...]
        u = _mx(xv * r * g)
        vals = [pair_refs[2 * i][...].astype(F32) + pair_refs[2 * i + 1][...].astype(F32) for i in range(n_pair)]
        vals += [ref[...].astype(F32) for ref in single_refs]
        vals += [jnp.concatenate([ref[...] for ref in halo_refs[per * i:per * (i + 1)]], axis=0)
                 for i in range(n_halo)]
        ds = [_mx(_squeeze_heads(val) if heads else val) for (_, _, heads), val in zip(groups, vals)]
        du = jnp.zeros((tm, D_MODEL), F32)
        for (first, rows, _), d in zip(groups, ds):
            du = du + _dot(d, w_r[first:first + rows, :])
        for (first, rows, _), d in zip(groups, ds):
            dw_r[first:first + rows, :] += _dot_tn(d, u)
        dx, dg = _rms_bwd(xv, r, g, du)
        dx_r[...] = dh1_r[...] + dx
        dg_r[...] += dg

    arrays = [a for _, pr in pairs for a in pr] + [a for _, a in singles]
    specs = [_row_spec(tm, a.shape[1]) for a in arrays]
    for _, a in halos:
        specs += [pl.BlockSpec((SWA_BLOCK, a.shape[1]), lambda i, j=j: (per * i + 1 + j, 0)) for j in range(per)]
        arrays += [a] * per
    body, extra, extra_specs = _after(body, 4 + len(arrays), dep)
    return pl.pallas_call(
        body, name="in_bwd", grid=(L // tm,),
        in_specs=[_row_spec(tm, D_MODEL), _row_spec(tm, D_MODEL), _full_spec((1, D_MODEL)), _vmem_spec()] + specs
        + extra_specs,
        out_specs=[_row_spec(tm, D_MODEL), _full_spec((IN_COLS, D_MODEL)), _full_spec((1, D_MODEL))],
        out_shape=[jax.ShapeDtypeStruct((L, D_MODEL), F32), jax.ShapeDtypeStruct((IN_COLS, D_MODEL), F32),
                   jax.ShapeDtypeStruct((1, D_MODEL), F32)],
        compiler_params=_params(("arbitrary",), VMEM_BIG),
    )(x, dh1, g_pre, w_in_t, *arrays, *extra)


def _adamw_math(w, g, m, v):
    m = ADAM_B1 * m + (1.0 - ADAM_B1) * g
    v = ADAM_B2 * v + (1.0 - ADAM_B2) * (g * g)
    m_hat = m / (1.0 - ADAM_B1 ** ADAM_STEP)
    v_hat = v / (1.0 - ADAM_B2 ** ADAM_STEP)
    delta = -ADAM_LR * (m_hat / (jnp.sqrt(v_hat) + ADAM_EPS) + ADAM_WD * w)
    return delta, m, v


def _adamw_call(w, g, m, v, name, dep=None):
    rows, cols = w.shape
    tr = min(256, rows)

    def body(w_r, g_r, m_r, v_r, d_r, nm_r, nv_r):
        d_r[...], nm_r[...], nv_r[...] = _adamw_math(w_r[...], g_r[...], m_r[...], v_r[...])

    if rows % tr == 0:
        spec, steps = _row_spec(tr, cols), rows // tr
    else:
        spec, steps = pl.BlockSpec((rows, 256), lambda i: (0, i)), cols // 256
    body, extra, extra_specs = _after(body, 4, dep)
    return pl.pallas_call(
        body, name=name, grid=(steps,),
        in_specs=[spec] * 4 + extra_specs, out_specs=[spec] * 3,
        out_shape=[jax.ShapeDtypeStruct(w.shape, F32)] * 3,
        compiler_params=_params(("arbitrary",)),
    )(w, g, m, v, *extra)


def _position():
    return lax.axis_index("x"), lax.axis_index("y"), lax.axis_index("c")


def _other_chips(x, y):
    return [(1 - x, y), (x, 1 - y), (1 - x, 1 - y)]


ROWS, COLS = -2, -1


def _half(ref, which, axis):
    size = ref.shape[axis] // 2
    span = pl.ds(pl.multiple_of(which * size, 16 if axis == ROWS else 128), size)
    index = [slice(None)] * len(ref.shape)
    index[axis] = span
    return ref.at[tuple(index)]


def _first_gather_call(shards, axes):
    n = len(shards)

    def body(*refs):
        srcs, outs = refs[:n], refs[n:2 * n]
        send_sems, recv_sems, local_sems = refs[2 * n:]
        x, y, c = _position()
        sibling = (x, y, 1 - c)
        chips = _other_chips(x, y)
        local = [pltpu.make_async_copy(srcs[a], outs[a].at[2 * x + y], local_sems.at[a]) for a in range(n)]
        for cp in local:
            cp.start()

        def copy(a, k, block, to, src=None):
            px, py, pc = block
            dst = _half(outs[a].at[2 * px + py], pc, axes[a])
            return pltpu.make_async_remote_copy(
                src_ref=dst if src is None else src, dst_ref=dst, send_sem=send_sems.at[6 * a + k],
                recv_sem=recv_sems.at[6 * a + k], device_id=to, device_id_type=MESH_ID)

        first, passed = [], []
        for a in range(n):
            my_half = _half(srcs[a], c, axes[a])
            first += [copy(a, j, (x, y, c), (*chip, c), src=my_half) for j, chip in enumerate(chips)]
        for cp in first:
            cp.start()
        for a in range(n):
            for j, chip in enumerate(chips):
                copy(a, j, (*chip, c), (x, y, c)).wait_recv()
                passed.append(copy(a, 3 + j, (*chip, c), sibling))
                passed[-1].start()
        for a in range(n):
            for j, chip in enumerate(chips):
                copy(a, 3 + j, (*chip, 1 - c), (x, y, c)).wait_recv()
        for cp in first + passed:
            cp.wait_send()
        for cp in local:
            cp.wait()

    return pl.pallas_call(
        body, name="first_gather",
        in_specs=[_any_spec()] * n, out_specs=[_any_spec()] * n,
        out_shape=[jax.ShapeDtypeStruct((N_CHIPS,) + s.shape, s.dtype) for s in shards],
        scratch_shapes=[pltpu.SemaphoreType.DMA((6 * n,)), pltpu.SemaphoreType.DMA((6 * n,)),
                        pltpu.SemaphoreType.DMA((n,))],
    )(*shards)


def _split_start(name, arrays, n_copies, plan):
    n = len(arrays)

    def body(*refs):
        ins, send_sems, recv_sems, token = refs[:n], refs[n], refs[n + 1], refs[-1]
        for k, (src, dst, to, _) in enumerate(plan(ins)):
            pltpu.make_async_remote_copy(src_ref=src, dst_ref=dst, send_sem=send_sems.at[k],
                                         recv_sem=recv_sems.at[k], device_id=to, device_id_type=MESH_ID).start()
        token[...] = jnp.zeros_like(token)

    hbm = pl.BlockSpec(memory_space=pltpu.HBM)
    sem = pl.BlockSpec(memory_space=pltpu.SEMAPHORE)
    out = pl.pallas_call(
        body, name=name,
        out_shape=(pltpu.SemaphoreType.DMA((n_copies,)), pltpu.SemaphoreType.DMA((n_copies,)))
        + tuple(pltpu.HBM(a.shape, a.dtype) for a in arrays) + (jax.ShapeDtypeStruct((8, 128), F32),),
        in_specs=[hbm] * n, out_specs=(sem, sem) + (hbm,) * n + (_vmem_spec(),),
        input_output_aliases={i: 2 + i for i in range(n)},
        compiler_params=pltpu.CompilerParams(has_side_effects=pltpu.SideEffectType.DATAFLOW_SIDE_EFFECTING),
    )(*[pltpu.with_memory_space_constraint(a, pltpu.HBM) for a in arrays])
    return (out[0], out[1], tuple(out[2:2 + n])), out[-1]


def _split_wait(name, handle, n_copies, plan, after):
    send_sems, recv_sems, arrays = handle
    n = len(arrays)

    def body(*refs):
        ins, s_sems, r_sems = refs[:n], refs[n], refs[n + 1]
        for k, (src, dst, to, landed) in enumerate(plan(ins)):
            cp = pltpu.make_async_remote_copy(src_ref=src, dst_ref=landed, send_sem=s_sems.at[k],
                                              recv_sem=r_sems.at[k], device_id=to, device_id_type=MESH_ID)
            cp.wait_send()
            cp.wait_recv()

    hbm = pl.BlockSpec(memory_space=pltpu.HBM)
    sem = pl.BlockSpec(memory_space=pltpu.SEMAPHORE)
    out = pl.pallas_call(
        body, name=name,
        out_shape=tuple(pltpu.HBM(a.shape, a.dtype) for a in arrays),
        in_specs=[hbm] * n + [sem, sem, _any_spec()], out_specs=(hbm,) * n,
        input_output_aliases={i: i for i in range(n)},
        compiler_params=pltpu.CompilerParams(has_side_effects=pltpu.SideEffectType.DATAFLOW_SIDE_EFFECTING),
    )(*arrays, send_sems, recv_sems, after)
    return tuple(out)


def _gather_plans(axes):
    n = len(axes)

    def stage_one(refs):
        x, y, c = _position()
        copies = []
        for a, axis in enumerate(axes):
            for px, py in _other_chips(x, y):
                copies.append((_half(refs[a], c, axis), _half(refs[n + a].at[2 * x + y], c, axis),
                               (px, py, c), _half(refs[n + a].at[2 * px + py], c, axis)))
        return copies

    def stage_two(refs):
        x, y, c = _position()
        copies = []
        for a, axis in enumerate(axes):
            for px, py in _other_chips(x, y):
                piece = _half(refs[n + a].at[2 * px + py], c, axis)
                copies.append((piece, piece, (x, y, 1 - c), _half(refs[n + a].at[2 * px + py], 1 - c, axis)))
        return copies

    return stage_one, stage_two


def _pair_swap_plan(axes):
    n = len(axes)

    def plan(refs):
        x, y, c = _position()
        return [(_half(refs[a], 1 - c, axes[a]), refs[n + a], (x, y, 1 - c), refs[n + a]) for a in range(n)]

    return plan


def _chip_swap_plan(n):
    def plan(refs):
        x, y, c = _position()
        copies = []
        for a in range(n):
            for j, (px, py) in enumerate(_other_chips(x, y)):
                copies.append((refs[a].at[2 * px + py], refs[n + a].at[j], (px, py, c), refs[n + a].at[j]))
        return copies

    return plan


def _pair_join_plan(axes):
    def plan(refs):
        x, y, c = _position()
        copies = []
        for a, axis in enumerate(axes):
            mine = _half(refs[a], c, axis)
            copies.append((mine, mine, (x, y, 1 - c), _half(refs[a], 1 - c, axis)))
        return copies

    return plan


def _pair_add_call(g, got, pos, name, axis):
    rows, cols = got.shape[1], got.shape[2]
    tr = min(512, rows) if axis == ROWS else rows
    nblk = rows // tr
    if axis == ROWS:
        mine = lambda j, i, p: (j, p[1] * nblk + i, 0)
    else:
        mine = lambda j, i, p: (j, 0, p[1])

    def body(pos_r, g_r, got_r, o_r):
        o_r[...] = (g_r[...] + got_r[...]).astype(o_r.dtype)

    return pl.pallas_call(
        body, name=name,
        grid_spec=pltpu.PrefetchScalarGridSpec(
            num_scalar_prefetch=1, grid=(N_CHIPS, nblk),
            in_specs=[pl.BlockSpec((None, tr, cols), mine),
                      pl.BlockSpec((None, tr, cols), lambda j, i, p: (j, i, 0))],
            out_specs=pl.BlockSpec((None, tr, cols), lambda j, i, p: (j, i, 0))),
        out_shape=jax.ShapeDtypeStruct(got.shape, COMM_DTYPE),
        compiler_params=_params(("arbitrary", "arbitrary"), VMEM_BIG),
    )(pos, g, got)


def _chip_add_call(hsum, got, pos, name, axis):
    rows, cols = hsum.shape[1], hsum.shape[2]
    tr = min(512, rows) if axis == ROWS else rows
    nblk = rows // tr
    if axis == ROWS:
        out_shape, mine = (2 * rows, cols), (lambda i, p: (p[1] * nblk + i, 0))
    else:
        out_shape, mine = (rows, 2 * cols), (lambda i, p: (0, p[1]))

    def body(pos_r, own_r, got_r, o_r):
        acc = own_r[...].astype(F32)
        for j in range(3):
            acc = acc + got_r[j].astype(F32)
        o_r[...] = acc

    return pl.pallas_call(
        body, name=name,
        grid_spec=pltpu.PrefetchScalarGridSpec(
            num_scalar_prefetch=1, grid=(nblk,),
            in_specs=[pl.BlockSpec((None, tr, cols), lambda i, p: (p[0], i, 0)),
                      pl.BlockSpec((3, tr, cols), lambda i, p: (0, i, 0))],
            out_specs=pl.BlockSpec((tr, cols), mine)),
        out_shape=jax.ShapeDtypeStruct(out_shape, F32),
        compiler_params=_params(("arbitrary",), VMEM_BIG),
    )(pos, hsum, got)


SMALL_NAMES = ("norm_mix_pre", "norm_mix_post", "norm_mlp_pre", "norm_mlp_post", "b_gate_fwd", "b_gate_bwd",
               "gla_norm", "swa_sink", "rel_bias")


def _small_update_call(grads, gate_grads, params, dep=None):
    n_dev = 8
    n_small = len(SMALL_NAMES)
    wmv = [t for p in params for t in p]
    shapes = [p[0].shape for p in params]

    def body(*refs):
        g_refs = refs[:n_small + 3]
        wmv_refs = refs[n_small + 3:n_small + 3 + 3 * n_small]
        n_in = n_small + 3 + 3 * n_small
        out_refs = refs[n_in:n_in + 4 * n_small + 3]
        pack_a, pack_b, all_a, all_b, send_sems, recv_sems = refs[n_in + 4 * n_small + 3:]
        x, y, c = _position()
        me = 4 * x + 2 * y + c
        pack_a[...] = jnp.zeros_like(pack_a)
        pack_b[...] = jnp.zeros_like(pack_b)
        for i in range(4):
            pack_a[i:i + 1, :] = g_refs[i][...]
        pack_a[4:5, 0:256] = g_refs[4][...]
        pack_a[5:6, 0:256] = g_refs[5][...]
        pack_a[6:7, 0:128] = g_refs[6][...]
        pack_a[7:8, 0:128] = g_refs[7][...]
        pack_a[7:8, 128:256] = g_refs[11][...]
        pack_b[0:32, 0:128] = g_refs[8][...]
        pack_b[32:48, :] = g_refs[9][...]
        pack_b[48:64, :] = g_refs[10][...]
        all_a[me] = pack_a[...]
        all_b[me] = pack_b[...]
        copies = []
        for k in range(1, n_dev):
            fx, fy, fc = (k >> 2) & 1, (k >> 1) & 1, k & 1
            to = (1 - x if fx else x, 1 - y if fy else y, 1 - c if fc else c)
            for t, (pack, dst) in enumerate(((pack_a, all_a), (pack_b, all_b))):
                copies.append(pltpu.make_async_remote_copy(
                    src_ref=pack, dst_ref=dst.at[me], send_sem=send_sems.at[2 * (k - 1) + t],
                    recv_sem=recv_sems.at[2 * (k - 1) + t], device_id=to, device_id_type=MESH_ID))
        for cp in copies:
            cp.start()
        for cp in copies:
            cp.wait()
        sum_a, sum_b = all_a[0], all_b[0]
        for d in range(1, n_dev):
            sum_a = sum_a + all_a[d]
            sum_b = sum_b + all_b[d]
        gsum = [sum_a[0:1], sum_a[1:2], sum_a[2:3], sum_a[3:4], sum_a[4:5, 0:256], sum_a[5:6, 0:256],
                sum_a[6:7, 0:128], sum_a[7:8, 0:SWA_Q_HEADS], sum_b[0:32, 0:SWA_Q_HEADS]]
        for i in range(n_small):
            w_r, m_r, v_r = wmv_refs[3 * i:3 * i + 3]
            delta, new_m, new_v = _adamw_math(w_r[...], gsum[i], m_r[...], v_r[...])
            out_refs[4 * i][...] = gsum[i]
            out_refs[4 * i + 1][...] = delta
            out_refs[4 * i + 2][...] = new_m
            out_refs[4 * i + 3][...] = new_v
        out_refs[4 * n_small][...] = sum_b[32:48]
        out_refs[4 * n_small + 1][...] = sum_b[48:64]
        out_refs[4 * n_small + 2][...] = sum_a[7:8, 128:256]

    n_in = n_small + 3 + 3 * n_small
    body, extra, extra_specs = _after(body, n_in, dep)
    out_shape = [jax.ShapeDtypeStruct(s, F32) for s in shapes for _ in range(4)]
    out_shape += [jax.ShapeDtypeStruct((GLA_GATE_RANK, 256), F32)] * 2 + [jax.ShapeDtypeStruct((1, 128), F32)]
    out = pl.pallas_call(
        body, name="small_update",
        in_specs=[_whole_spec(a.shape) for a in list(grads) + list(gate_grads) + wmv] + extra_specs,
        out_specs=[_whole_spec(s.shape) for s in out_shape],
        out_shape=out_shape,
        scratch_shapes=[pltpu.VMEM((8, D_MODEL), F32), pltpu.VMEM((64, 256), F32),
                        pltpu.VMEM((n_dev, 8, D_MODEL), F32), pltpu.VMEM((n_dev, 64, 256), F32),
                        pltpu.SemaphoreType.DMA((2 * (n_dev - 1),)), pltpu.SemaphoreType.DMA((2 * (n_dev - 1),))],
    )(*grads, *gate_grads, *wmv, *extra)
    per_name = [tuple(out[4 * i:4 * i + 4]) for i in range(n_small)]
    return per_name, out[4 * n_small], out[4 * n_small + 1], out[4 * n_small + 2]


def _pad_heads(t, n_heads, axis=-1):
    axis = axis % t.ndim
    shape = t.shape
    t = t.reshape(shape[:axis] + (n_heads, 64) + shape[axis + 1:])
    pad = [(0, 0)] * t.ndim
    pad[axis + 1] = (0, HEAD_PAD - 64)
    return jnp.pad(t, pad).reshape(shape[:axis] + (n_heads * HEAD_PAD,) + shape[axis + 1:])


def _unpad_heads(t, n_heads, axis=-1):
    axis = axis % t.ndim
    shape = t.shape
    t = t.reshape(shape[:axis] + (n_heads, HEAD_PAD) + shape[axis + 1:])
    t = lax.slice_in_dim(t, 0, 64, axis=axis + 1)
    return t.reshape(shape[:axis] + (n_heads * 64,) + shape[axis + 1:])


def _pad_gate(w, first_row):
    return jnp.pad(_pad_heads(w, 4), ((first_row, 128 - GLA_GATE_RANK - first_row), (0, 0)))


def _own_slot(shard, chip):
    zone = lax.empty((N_CHIPS,) + shard.shape, shard.dtype)
    return lax.dynamic_update_slice(zone, shard[None], (chip,) + (0,) * shard.ndim)


def _reduce_to_owners(grads, axes, pos, tag, overlap):
    n = len(grads)

    def half_shape(g, axis):
        return (N_CHIPS, g.shape[1] // 2, g.shape[2]) if axis == ROWS else (N_CHIPS, g.shape[1], g.shape[2] // 2)

    lands = [lax.empty(half_shape(g, axis), F32) for g, axis in zip(grads, axes)]
    handle, token = _split_start(tag + "_pair_start", list(grads) + lands, n, _pair_swap_plan(axes))
    got = _split_wait(tag + "_pair_wait", handle, n, _pair_swap_plan(axes), overlap[0](token))
    sums = [_pair_add_call(got[a], got[n + a], pos, f"{tag}_pair_add{a}", axes[a]) for a in range(n)]
    lands = [lax.empty((3,) + s.shape[1:], s.dtype) for s in sums]
    handle, token = _split_start(tag + "_chip_start", sums + lands, 3 * n, _chip_swap_plan(n))
    got = _split_wait(tag + "_chip_wait", handle, 3 * n, _chip_swap_plan(n), overlap[1](token))
    halves = [_chip_add_call(got[a], got[n + a], pos, f"{tag}_chip_add{a}", axes[a]) for a in range(n)]
    handle, token = _split_start(tag + "_join_start", halves, n, _pair_join_plan(axes))
    return _split_wait(tag + "_join_wait", handle, n, _pair_join_plan(axes), overlap[2](token))


def kernel(x, norm_mix_pre, w_in, w_gate_up_fwd, b_gate_fwd, w_gate_up_bwd, b_gate_bwd, gla_norm, swa_sink, rel_bias, w_out, norm_mix_post, norm_mlp_pre, w_up, w_down, norm_mlp_post, loss_target, m_norm_mix_pre, m_w_in, m_w_gate_up_fwd, m_b_gate_fwd, m_w_gate_up_bwd, m_b_gate_bwd, m_gla_norm, m_swa_sink, m_rel_bias, m_w_out, m_norm_mix_post, m_norm_mlp_pre, m_w_up, m_w_down, m_norm_mlp_post, v_norm_mix_pre, v_w_in, v_w_gate_up_fwd, v_b_gate_fwd, v_w_gate_up_bwd, v_b_gate_bwd, v_gla_norm, v_swa_sink, v_rel_bias, v_w_out, v_norm_mix_post, v_norm_mlp_pre, v_w_up, v_w_down, v_norm_mlp_post):
    given = dict(locals())
    cx, cy, cc = _position()
    chip = (2 * cx + cy).astype(jnp.int32)
    pos = jnp.stack([chip, cc.astype(jnp.int32)])
    seq, tgt = x[0], loss_target[0]
    L = seq.shape[0]

    gates = jnp.concatenate([w_gate_up_fwd[0], w_gate_up_bwd[0]], axis=0).astype(COMM_DTYPE)
    all_in, all_gates = _first_gather_call([w_in[0].T.astype(COMM_DTYPE), gates], [COLS, ROWS])
    rest = [w_out[0].astype(COMM_DTYPE), jnp.stack([w_up[0], w_down[0]]).astype(COMM_DTYPE)]
    stage_one, stage_two = _gather_plans([ROWS, ROWS])
    handle, token = _split_start("gather_chip_start", rest + [_own_slot(s, chip) for s in rest] + [all_gates], 6,
                                 stage_one)

    w_in_t = _mx(all_in.reshape(IN_COLS, D_MODEL))
    gates_full = jnp.concatenate([all_gates[j] for j in range(N_CHIPS)], axis=1)
    wgf_p = _mx(_pad_gate(gates_full[:GLA_GATE_RANK], 0))
    wgb_p = _mx(_pad_gate(gates_full[GLA_GATE_RANK:], GLA_GATE_RANK))
    bf_p, bb_p = _pad_heads(b_gate_fwd, 4), _pad_heads(b_gate_bwd, 4)
    buckets = jnp.asarray(_band_buckets())
    bias = _bias_call(rel_bias, buckets)
    sink1 = swa_sink.reshape(SWA_Q_HEADS)

    qa, ka, va, ga, qs, ks, vs, za = _proj_call(seq, norm_mix_pre, w_in_t, dep=token)
    halo = ((SWA_BLOCK, SWA_BLOCK), (0, 0))
    ks_p, vs_p = jnp.pad(ks, halo), jnp.pad(vs, halo)
    o_f, o_b, s_f, s_b = _gla_fwd_call(qa, ka, va, za, wgf_p, bf_p, wgb_p, bb_p)
    arrays = _split_wait("gather_chip_wait", handle, 6, stage_one, o_f)
    handle, token = _split_start("gather_pair_start", list(arrays), 6, stage_two)
    o_s = _swa_fwd_call(qs, ks_p, vs_p, bias, sink1, dep=token)
    arrays = _split_wait("gather_pair_wait", handle, 6, stage_two, o_s)
    w_out_full = _mx(arrays[2].reshape(N_CHIPS * R_OUT, D_MODEL))
    w_ud = _mx(arrays[3])
    cat, mix, h1, n2 = _mix_call(o_f, o_b, ga, o_s, seq, gla_norm, w_out_full, norm_mix_post, norm_mlp_pre)
    a, rz, dh2, dff, loss, d_post2 = _mlp_fwd_call(n2, h1, tgt, w_ud, norm_mlp_post)

    dz, dn2 = _mlp_bwd_call(dff, rz, w_ud)
    dw_down, dw_up4 = _mlp_wgrad_call(a, dff, n2, dz)
    dh1, do, dga, dos, dw_out, d_pre2, d_post, d_gn = _mix_bwd_call(
        dn2, dh2, h1, mix, cat, o_f, o_b, ga, gla_norm, norm_mix_post, norm_mlp_pre, w_out_full)
    done = {}

    def swa_backward(tok):
        done["swa"] = _swa_bwd_call(qs, ks_p, vs_p, bias, sink1, dos, dep=tok)
        return done["swa"][0]

    def gla_in_backward(tok):
        done["gla"] = _gla_bwd_call(qa, ka, va, za, do, s_f, s_b, wgf_p, bf_p, wgb_p, bb_p, dep=tok)
        dqf, dkf, dvf, dzf, _, _, dqb, dkb, dvb, dzb, _, _ = done["gla"]
        dqs, dks_p, dvs_p, _, _ = done["swa"]
        done["in"] = _in_bwd_call(
            seq, dh1, norm_mix_pre, w_in_t,
            pairs=[(_side_by_side(T_QA), (dqf, dqb)), (_side_by_side(T_KA), (dkf, dkb)), (T_VA, (dvf, dvb)),
                   (T_ZA, (dzf, dzb))],
            singles=[(T_GA, dga), (_side_by_side(T_QS), dqs)], halos=[(T_KS, dks_p), (T_VS, dvs_p)])
        return done["in"][0]

    def bias_backward(tok):
        done["rel"] = _relbias_call(done["swa"][3], done["swa"][4], buckets, dep=tok)
        return done["rel"][0]

    g_up, g_down, g_out = _reduce_to_owners(
        [dw_up4, dw_down.reshape(N_CHIPS, R_DOWN, D_MODEL), dw_out.reshape(N_CHIPS, R_OUT, D_MODEL)],
        [ROWS, ROWS, ROWS], pos, "mlp", [swa_backward, gla_in_backward, bias_backward])
    dx, dw_in_t, d_pre = done["in"]
    dwf, dbf, dwb, dbb = done["gla"][4], done["gla"][5], done["gla"][10], done["gla"][11]
    drel, dsink = done["rel"]

    small_grads = [d_pre, d_post, d_pre2, d_post2, _unpad_heads(dbf, 4), _unpad_heads(dbb, 4), d_gn, dsink, drel]
    gate_grads = [_unpad_heads(dwf[:GLA_GATE_RANK], 4), _unpad_heads(dwb[GLA_GATE_RANK:2 * GLA_GATE_RANK], 4)]
    small_params = [(given[n], given["m_" + n], given["v_" + n]) for n in SMALL_NAMES]
    upd = {}

    def update_up(tok):
        upd["w_up"] = (g_up,) + tuple(_adamw_call(w_up[0], g_up, m_w_up[0], v_w_up[0], "adamw_w_up", dep=tok))
        return upd["w_up"][1]

    def update_small(tok):
        per_name, gf_sum, gb_sum, upd["loss"] = _small_update_call(small_grads, gate_grads + [loss], small_params,
                                                                   dep=tok)
        upd.update(dict(zip(SMALL_NAMES, per_name)))
        for name, total in (("w_gate_up_fwd", gf_sum), ("w_gate_up_bwd", gb_sum)):
            g = lax.dynamic_slice(total, (0, chip * 64), (GLA_GATE_RANK, 64))
            upd[name] = (g,) + tuple(_adamw_call(given[name][0], g, given["m_" + name][0], given["v_" + name][0],
                                                 "adamw_" + name))
        upd["w_down"] = (g_down,) + tuple(
            _adamw_call(w_down[0], g_down, m_w_down[0], v_w_down[0], "adamw_w_down", dep=gf_sum))
        return upd["w_down"][1]

    def update_out(tok):
        upd["w_out"] = (g_out,) + tuple(_adamw_call(w_out[0], g_out, m_w_out[0], v_w_out[0], "adamw_w_out", dep=tok))
        return upd["w_out"][1]

    (g_in_t,) = _reduce_to_owners([dw_in_t.reshape(N_CHIPS, R_IN, D_MODEL)], [COLS], pos, "in",
                                  [update_up, update_small, update_out])
    in_t = (g_in_t,) + tuple(_adamw_call(w_in[0].T, g_in_t, m_w_in[0].T, v_w_in[0].T, "adamw_w_in"))
    upd["w_in"] = tuple(t.T for t in in_t)

    big = ("w_in", "w_gate_up_fwd", "w_gate_up_bwd", "w_out", "w_up", "w_down")
    names = ["norm_mix_pre", "w_in", "w_gate_up_fwd", "b_gate_fwd", "w_gate_up_bwd", "b_gate_bwd", "gla_norm",
             "swa_sink", "rel_bias", "w_out", "norm_mix_post", "norm_mlp_pre", "w_up", "w_down", "norm_mlp_post"]
    outs = [upd["loss"][0, 0], dx[None]]
    for kind in range(4):
        outs += [upd[n][kind][None] if n in big else upd[n][kind] for n in names]
    return tuple(outs)
```

```python
import math

import numpy as np
import jax
import jax.numpy as jnp
from jax import lax
from jax.experimental import pallas as pl
from jax.experimental.pallas import tpu as pltpu

F32 = jnp.float32
MXU_DTYPE = jnp.bfloat16
COMM_DTYPE = jnp.bfloat16

D_MODEL = 1024
D_FF = 4096
N_CHIPS = 4
GLA_HEADS = 4
GLA_CHUNK = 64
GLA_GATE_RANK = 16
GLA_GATE_NORM = 16.0
SWA_Q_HEADS = 8
SWA_KV_HEADS = 2
SWA_BLOCK = 128
REL_BUCKETS = 32
REL_MAX_DIST = 128
NORM_EPS = 1e-6
HEAD_PAD = 128

ADAM_LR = 0.001
ADAM_B1 = 0.9
ADAM_B2 = 0.999
ADAM_EPS = 1e-08
ADAM_WD = 0.01
ADAM_STEP = 10

OUT_PAD = 1024

R_IN, R_OUT, R_UP, R_DOWN = 584, 256, 1024, 1024

VMEM_BIG = 56 * 1024 * 1024
MESH_AXES = ("x", "y", "c")
MESH_ID = pl.DeviceIdType.MESH


def _mx(a):
    return a.astype(MXU_DTYPE)


def _dot(a, b):
    return jnp.dot(a, b, preferred_element_type=F32)


def _dot_nt(a, b):
    return lax.dot_general(a, b, (((1,), (1,)), ((), ())), preferred_element_type=F32)


def _dot_tn(a, b):
    return lax.dot_general(a, b, (((0,), (0,)), ((), ())), preferred_element_type=F32)


def _rms_r(x):
    return lax.rsqrt(jnp.mean(x * x, axis=-1, keepdims=True) + NORM_EPS)


def _rms_bwd(x, r, g, dy):
    xh = x * r
    gdy = dy * g
    dx = r * (gdy - xh * jnp.mean(gdy * xh, axis=-1, keepdims=True))
    return dx, jnp.sum(dy * xh, axis=0, keepdims=True)


def _low_half(rows):
    return lax.broadcasted_iota(jnp.int32, (rows, HEAD_PAD), 1) < 64


def _spread_heads(x):
    low = _low_half(x.shape[0])
    parts = []
    for p in range(x.shape[1] // HEAD_PAD):
        pair = x[:, HEAD_PAD * p:HEAD_PAD * (p + 1)]
        parts += [jnp.where(low, pair, 0.0), jnp.where(low, pltpu.roll(pair, 64, 1), 0.0)]
    return jnp.concatenate(parts, axis=1)


def _squeeze_heads(x):
    low = _low_half(x.shape[0])
    parts = []
    for p in range(x.shape[1] // (2 * HEAD_PAD)):
        even = x[:, 2 * HEAD_PAD * p:2 * HEAD_PAD * p + HEAD_PAD]
        odd = x[:, 2 * HEAD_PAD * p + HEAD_PAD:2 * HEAD_PAD * (p + 1)]
        parts.append(jnp.where(low, even, pltpu.roll(odd, 64, 1)))
    return parts[0] if len(parts) == 1 else jnp.concatenate(parts, axis=1)


def _params(sem=None, vmem=None):
    kw = {}
    if sem is not None:
        kw["dimension_semantics"] = sem
    if vmem is not None:
        kw["vmem_limit_bytes"] = vmem
    return pltpu.CompilerParams(**kw)


def _vmem_spec():
    return pl.BlockSpec(memory_space=pltpu.VMEM)


def _whole_spec(shape):
    return pl.BlockSpec(shape, lambda: (0,) * len(shape))


def _row_spec(tm, width):
    return pl.BlockSpec((tm, width), lambda i: (i, 0))


def _full_spec(shape):
    return pl.BlockSpec(shape, lambda i: (0,) * len(shape))


def _any_spec():
    return pl.BlockSpec(memory_space=pl.ANY)


def _after(body, n_in, dep):
    if dep is None:
        return body, [], []
    return (lambda *refs: body(*refs[:n_in], *refs[n_in + 1:])), [dep], [_any_spec()]


T_QA, T_KA, T_VA, T_GA = (0, 256, 4), (256, 256, 4), (512, 512, 0), (1024, 512, 0)
T_QS, T_KS, T_VS = (1568, 512, 8), (2080, 128, 2), (2208, 128, 2)
T_ZA = (1536, 128, 0)
ZA_COLS = 2 * GLA_GATE_RANK
IN_COLS = 2336


def _side_by_side(group):
    return group[0], group[1], 0


def _proj_call(x, g_pre, w_in_t, dep=None):
    L = x.shape[0]
    tm = min(512, L)
    groups = [(T_QA, F32), (T_KA, F32), (T_VA, MXU_DTYPE), (T_GA, F32),
              (T_QS, MXU_DTYPE), (T_KS, MXU_DTYPE), (T_VS, MXU_DTYPE), (T_ZA, F32)]
    widths = [rows * (2 if heads else 1) for (_, rows, heads), _ in groups]

    def body(x_ref, g_ref, w_ref, *outs):
        xv = x_ref[...]
        u = _mx(xv * _rms_r(xv) * g_ref[...])
        for ref, (grp, _) in zip(outs, groups):
            first, rows, heads = grp
            val = _dot_nt(u, w_ref[first:first + rows, :])
            if heads:
                val = _spread_heads(val)
            if grp is T_ZA:
                val = jnp.where(lax.broadcasted_iota(jnp.int32, val.shape, 1) < ZA_COLS, val, 0.0)
            if grp is T_QS:
                val = val * 0.125
            ref[...] = val.astype(ref.dtype)

    body, extra, extra_specs = _after(body, 3, dep)
    return pl.pallas_call(
        body, name="proj_fwd", grid=(L // tm,),
        in_specs=[_row_spec(tm, D_MODEL), _full_spec((1, D_MODEL)), _vmem_spec()] + extra_specs,
        out_specs=[_row_spec(tm, w) for w in widths],
        out_shape=[jax.ShapeDtypeStruct((L, w), dt) for w, (_, dt) in zip(widths, groups)],
        compiler_params=_params(("arbitrary",), VMEM_BIG),
    )(x, g_pre, w_in_t, *extra)


def _tri_masks():
    row = lax.broadcasted_iota(jnp.int32, (GLA_CHUNK, GLA_CHUNK), 0)
    col = lax.broadcasted_iota(jnp.int32, (GLA_CHUNK, GLA_CHUNK), 1)
    return row >= col, row <= col


def _chunk_sums(tri_m, x):
    hi = _mx(x)
    rest = x - hi.astype(F32)
    mid = _mx(rest)
    lo = _mx(rest - mid.astype(F32))
    return _dot(tri_m, hi) + _dot(tri_m, mid) + _dot(tri_m, lo)


def _gla_block_pre(q_r, k_r, z_r, w_r, b_r, rev, nc, qd_s, ki_s, ks_s, dec_s, keep=None):
    tri_f, tri_b = _tri_masks()
    tri_m = _mx((tri_b if rev else tri_f).astype(F32))
    g = _dot(_mx(z_r[...]), w_r[...]) + b_r[...]
    la = (jnp.minimum(g, 0.0) - jnp.log(1.0 + jnp.exp(-jnp.abs(g)))) / GLA_GATE_NORM
    sums, lasts = [], []
    for c in range(nc):
        b_c = _chunk_sums(tri_m, la[GLA_CHUNK * c:GLA_CHUNK * (c + 1)])
        blast = b_c[0:1] if rev else b_c[GLA_CHUNK - 1:GLA_CHUNK]
        dec_s[c] = jnp.exp(blast)
        sums.append(b_c)
        lasts.append(jnp.broadcast_to(blast, b_c.shape))
    b = jnp.concatenate(sums, axis=0)
    eb = jnp.exp(b)
    enb = jnp.exp(-b)
    elb = jnp.exp(jnp.concatenate(lasts, axis=0) - b)
    k = k_r[...]
    qd_s[...] = (q_r[...] * 0.125 * eb).astype(qd_s.dtype)
    ki_s[...] = (k * enb).astype(ki_s.dtype)
    ks_s[...] = (k * elb).astype(ks_s.dtype)
    if keep is not None:
        for ref, val in zip(keep, (g, eb, enb, elb)):
            ref[...] = val


def _gla_fwd_call(qa, ka, va, za, wgf, bgf, wgb, bgb):
    L = qa.shape[0]
    br = min(512, L)
    nb, nc, n_chunks = L // br, br // GLA_CHUNK, L // GLA_CHUNK
    hw = GLA_HEADS * HEAD_PAD

    def body(qaf, kaf, vaf, zaf, qab, kab, vab, zab, wgf_r, bgf_r, wgb_r, bgb_r,
             of_r, ob_r, sf_r, sb_r, st_f, st_b, pre_f, pre_b):
        @pl.when(pl.program_id(0) == 0)
        def _():
            st_f[...] = jnp.zeros_like(st_f)
            st_b[...] = jnp.zeros_like(st_b)

        _gla_block_pre(qaf, kaf, zaf, wgf_r, bgf_r, False, nc, *pre_f)
        _gla_block_pre(qab, kab, zab, wgb_r, bgb_r, True, nc, *pre_b)
        tri_f, tri_b = _tri_masks()

        def one(tri, pre, v_r, o_r, s_r, st, ci):
            qd_s, ki_s, ks_s, dec_s = pre
            rows = pl.ds(pl.multiple_of(ci * GLA_CHUNK, GLA_CHUNK), GLA_CHUNK)
            dec = dec_s[ci]
            heads = range(GLA_HEADS)
            lanes = [slice(HEAD_PAD * h, HEAD_PAD * (h + 1)) for h in heads]
            qd = [qd_s[rows, sl] for sl in lanes]
            v = [v_r[rows, sl] for sl in lanes]
            s_t = [st[h] for h in heads]
            a = [_dot_nt(qd[h], ki_s[rows, lanes[h]]) for h in heads]
            carried = [_dot_nt(qd[h], _mx(s_t[h])) for h in heads]
            grown = [_dot_tn(v[h], ks_s[rows, lanes[h]]) for h in heads]
            a = [_mx(jnp.where(tri, a[h], 0.0)) for h in heads]
            inner = [_dot(a[h], v[h]) for h in heads]
            for h in heads:
                s_r[ci, h] = s_t[h].astype(s_r.dtype)
                o_r[rows, lanes[h]] = inner[h] + carried[h]
                st[h] = s_t[h] * dec[:, lanes[h]] + grown[h]

        def loop(t, carry):
            one(tri_f, pre_f, vaf, of_r, sf_r, st_f, t)
            one(tri_b, pre_b, vab, ob_r, sb_r, st_b, nc - 1 - t)
            return carry

        lax.fori_loop(0, nc, loop, 0, unroll=True)

    fwd = lambda i: (i, 0)
    bwd = lambda i: (nb - 1 - i, 0)
    ins = lambda m: [pl.BlockSpec((br, hw), m), pl.BlockSpec((br, hw), m),
                     pl.BlockSpec((br, hw), m), pl.BlockSpec((br, 128), m)]
    wspecs = [_full_spec((128, hw)), _full_spec((1, hw))] * 2
    s_shape = (nc, GLA_HEADS, HEAD_PAD, HEAD_PAD)
    pre_scratch = [pltpu.VMEM((br, hw), MXU_DTYPE)] * 3 + [pltpu.VMEM((nc, 1, hw), F32)]
    return pl.pallas_call(
        body, name="gla_fwd", grid=(nb,),
        in_specs=ins(fwd) + ins(bwd) + wspecs,
        out_specs=[pl.BlockSpec((br, hw), fwd), pl.BlockSpec((br, hw), bwd),
                   pl.BlockSpec(s_shape, lambda i: (i, 0, 0, 0)),
                   pl.BlockSpec(s_shape, lambda i: (nb - 1 - i, 0, 0, 0))],
        out_shape=[jax.ShapeDtypeStruct((L, hw), F32), jax.ShapeDtypeStruct((L, hw), F32),
                   jax.ShapeDtypeStruct((n_chunks,) + s_shape[1:], MXU_DTYPE),
                   jax.ShapeDtypeStruct((n_chunks,) + s_shape[1:], MXU_DTYPE)],
        scratch_shapes=[pltpu.VMEM(s_shape[1:], F32), pltpu.VMEM(s_shape[1:], F32), pre_scratch, pre_scratch],
        compiler_params=_params(("arbitrary",), VMEM_BIG),
    )(qa, ka, va, za, qa, ka, va, za, wgf, bgf, wgb, bgb)


def _gla_bwd_call(qa, ka, va, za, do, sf, sb, wgf, bgf, wgb, bgb, dep=None):
    L = qa.shape[0]
    br = min(512, L)
    nb, nc = L // br, br // GLA_CHUNK
    hw = GLA_HEADS * HEAD_PAD

    def body(qaf, kaf, vaf, zaf, dof, sf_r, qab, kab, vab, zab, dob, sb_r, wgf_r, bgf_r, wgb_r, bgb_r,
             dqf, dkf, dvf, dzf, dwf, dbf, dqb, dkb, dvb, dzb, dwb, dbb, gt_f, gt_b, pre_f, pre_b):
        @pl.when(pl.program_id(0) == 0)
        def _():
            for ref in (gt_f, gt_b, dwf, dbf, dwb, dbb):
                ref[...] = jnp.zeros_like(ref)

        _gla_block_pre(qaf, kaf, zaf, wgf_r, bgf_r, False, nc, *pre_f[:4], keep=pre_f[4:8])
        _gla_block_pre(qab, kab, zab, wgb_r, bgb_r, True, nc, *pre_b[:4], keep=pre_b[4:8])
        tri_f, tri_b = _tri_masks()
        row_w = lax.broadcasted_iota(jnp.int32, (GLA_CHUNK, HEAD_PAD), 0)

        def one(rev, pre, q_r, k_r, v_r, do_r, s_r, dq_r, dk_r, dv_r, gt, ci):
            qd_s, ki_s, ks_s, dec_s, _, eb_s, enb_s, elb_s, db_s = pre
            tri = tri_b if rev else tri_f
            last_row = 0 if rev else GLA_CHUNK - 1
            rows = pl.ds(pl.multiple_of(ci * GLA_CHUNK, GLA_CHUNK), GLA_CHUNK)
            dec = dec_s[ci]
            heads = range(GLA_HEADS)
            lanes = [slice(HEAD_PAD * h, HEAD_PAD * (h + 1)) for h in heads]
            qd = [qd_s[rows, sl] for sl in lanes]
            ki = [ki_s[rows, sl] for sl in lanes]
            ks = [ks_s[rows, sl] for sl in lanes]
            v = [v_r[rows, sl] for sl in lanes]
            do_h = [_mx(do_r[rows, sl]) for sl in lanes]
            s_t = [s_r[ci, h] for h in heads]
            g_t = [gt[h] for h in heads]
            g_m = [_mx(g_t[h]) for h in heads]
            a = [_dot_nt(qd[h], ki[h]) for h in heads]
            da = [_dot_nt(do_h[h], v[h]) for h in heads]
            dv_carried = [_dot_nt(ks[h], g_m[h]) for h in heads]
            dqd_carried = [_dot(do_h[h], _mx(s_t[h])) for h in heads]
            dks = [_dot(v[h], g_m[h]) for h in heads]
            g_grown = [_dot_tn(do_h[h], qd[h]) for h in heads]
            a = [_mx(jnp.where(tri, a[h], 0.0)) for h in heads]
            da = [_mx(jnp.where(tri, da[h], 0.0)) for h in heads]
            dv_inner = [_dot_tn(a[h], do_h[h]) for h in heads]
            dqd_inner = [_dot(da[h], ki[h]) for h in heads]
            dki = [_dot_tn(da[h], qd[h]) for h in heads]
            dq, dk = [], []
            for h in heads:
                sl = lanes[h]
                dv_r[rows, sl] = (dv_inner[h] + dv_carried[h]).astype(dv_r.dtype)
                ddec = jnp.sum(g_t[h] * s_t[h].astype(F32), axis=0, keepdims=True)
                gt[h] = g_t[h] * dec[:, sl] + g_grown[h]
                dq.append((dqd_inner[h] + dqd_carried[h]) * eb_s[rows, sl] * 0.125)
                dk_state = dks[h] * elb_s[rows, sl]
                dk.append(dki[h] * enb_s[rows, sl] + dk_state)
                k = k_r[rows, sl]
                dblast = jnp.sum(dk_state * k, axis=0, keepdims=True) + dec[:, sl] * ddec
                db_s[rows, sl] = q_r[rows, sl] * dq[h] - k * dk[h] + jnp.where(row_w == last_row, dblast, 0.0)
            low = _low_half(GLA_CHUNK)
            for pair in range(GLA_HEADS // 2):
                psl = slice(HEAD_PAD * pair, HEAD_PAD * (pair + 1))
                for ref, val in ((dq_r, dq), (dk_r, dk)):
                    both = jnp.where(low, val[2 * pair], pltpu.roll(val[2 * pair + 1], 64, 1))
                    ref[rows, psl] = both.astype(ref.dtype)

        def loop(t, carry):
            one(False, pre_f, qaf, kaf, vaf, dof, sf_r, dqf, dkf, dvf, gt_f, nc - 1 - t)
            one(True, pre_b, qab, kab, vab, dob, sb_r, dqb, dkb, dvb, gt_b, t)
            return carry

        lax.fori_loop(0, nc, loop, 0, unroll=True)

        def gate_grads(rev, pre, z_r, w_r, dz_r, dw_r, dbias_r):
            g_s, db_s = pre[4], pre[8]
            back_m = _mx((tri_f if rev else tri_b).astype(F32))
            db = db_s[...]
            dla = jnp.concatenate([_chunk_sums(back_m, db[GLA_CHUNK * c:GLA_CHUNK * (c + 1)]) for c in range(nc)],
                                  axis=0)
            dg = dla * (1.0 / GLA_GATE_NORM) * (1.0 / (1.0 + jnp.exp(g_s[...])))
            dg_m = _mx(dg)
            dz_r[...] = _dot_nt(dg_m, w_r[...])
            dw_r[...] += _dot_tn(_mx(z_r[...]), dg_m)
            dbias_r[...] += jnp.sum(dg, axis=0, keepdims=True)

        gate_grads(False, pre_f, zaf, wgf_r, dzf, dwf, dbf)
        gate_grads(True, pre_b, zab, wgb_r, dzb, dwb, dbb)

    last_first = lambda i: (nb - 1 - i, 0)
    first_last = lambda i: (i, 0)
    s_shape = (nc, GLA_HEADS, HEAD_PAD, HEAD_PAD)

    def ins(m):
        return [pl.BlockSpec((br, hw), m), pl.BlockSpec((br, hw), m), pl.BlockSpec((br, hw), m),
                pl.BlockSpec((br, 128), m), pl.BlockSpec((br, hw), m),
                pl.BlockSpec(s_shape, lambda i: m(i) + (0, 0))]

    def outs(m):
        return [pl.BlockSpec((br, hw // 2), m), pl.BlockSpec((br, hw // 2), m), pl.BlockSpec((br, hw), m),
                pl.BlockSpec((br, 128), m), _full_spec((128, hw)), _full_spec((1, hw))]

    out_shape = [jax.ShapeDtypeStruct((L, hw // 2), MXU_DTYPE)] * 2 + [
        jax.ShapeDtypeStruct((L, hw), MXU_DTYPE),
        jax.ShapeDtypeStruct((L, 128), F32), jax.ShapeDtypeStruct((128, hw), F32),
        jax.ShapeDtypeStruct((1, hw), F32)]
    wspecs = [_full_spec((128, hw)), _full_spec((1, hw))] * 2
    body, extra, extra_specs = _after(body, 16, dep)
    pre_scratch = ([pltpu.VMEM((br, hw), MXU_DTYPE)] * 3 + [pltpu.VMEM((nc, 1, hw), F32)]
                   + [pltpu.VMEM((br, hw), F32)] * 5)
    return pl.pallas_call(
        body, name="gla_bwd", grid=(nb,),
        in_specs=ins(last_first) + ins(first_last) + wspecs + extra_specs,
        out_specs=outs(last_first) + outs(first_last),
        out_shape=out_shape + out_shape,
        scratch_shapes=[pltpu.VMEM(s_shape[1:], F32), pltpu.VMEM(s_shape[1:], F32), pre_scratch, pre_scratch],
        compiler_params=_params(("arbitrary",), VMEM_BIG),
    )(qa, ka, va, za, do, sf, qa, ka, va, za, do, sb, wgf, bgf, wgb, bgb, *extra)


def _t5_buckets(rel):
    nb = REL_BUCKETS // 2
    ret = (rel > 0).astype(np.int32) * nb
    n = np.abs(rel)
    max_exact = nb // 2
    large = max_exact + (np.log(np.maximum(n, 1).astype(np.float32) / max_exact)
                         / math.log(REL_MAX_DIST / max_exact) * (nb - max_exact)).astype(np.int32)
    large = np.minimum(large, nb - 1)
    return ret + np.where(n < max_exact, n, large)


SWA_GROUP = SWA_Q_HEADS // SWA_KV_HEADS
SWA_SPAN = 3 * SWA_BLOCK
SWA_GROUP_LANES = SWA_GROUP * SWA_BLOCK


def _band_buckets():
    s = np.arange(SWA_SPAN)[:, None]
    c = np.arange(SWA_BLOCK)[None, :]
    return _t5_buckets(s - SWA_BLOCK - c).astype(np.int32)


def _swa_valid(n, seq_len):
    key_pos = (n - 1) * SWA_BLOCK + lax.broadcasted_iota(jnp.int32, (SWA_SPAN, 1), 0)
    return (key_pos >= 0) & (key_pos < seq_len)


def _swa_sink_row(sink_r, kv):
    lane = lax.broadcasted_iota(jnp.int32, (1, SWA_GROUP_LANES), 1)
    row = jnp.full((1, SWA_GROUP_LANES), sink_r[kv * SWA_GROUP], F32)
    for g in range(1, SWA_GROUP):
        row = jnp.where(lane >= g * SWA_BLOCK, sink_r[kv * SWA_GROUP + g], row)
    return row


def _swa_group(ref, kv):
    first = kv * SWA_GROUP
    return jnp.concatenate([ref[:, HEAD_PAD * h:HEAD_PAD * (h + 1)] for h in range(first, first + SWA_GROUP)],
                           axis=0)


def _swa_softmax(scores, bias_t, sink_row, valid):
    st = jnp.where(valid, scores + bias_t, -1e30)
    m = jnp.maximum(jnp.max(st, axis=0, keepdims=True), sink_row)
    p = jnp.exp(st - m)
    e_sink = jnp.exp(sink_row - m)
    inv = 1.0 / (jnp.sum(p, axis=0, keepdims=True) + e_sink)
    return p * inv, e_sink * inv


def _swa_fwd_call(qs, ks, vs, bias, sink, dep=None):
    L = qs.shape[0]

    def body(q_r, k_r, v_r, bias_r, sink_r, o_r):
        n = pl.program_id(0)
        span = pl.ds(pl.multiple_of(n * SWA_BLOCK, SWA_BLOCK), SWA_SPAN)
        valid = _swa_valid(n, L)
        groups = range(SWA_KV_HEADS)
        lanes = [slice(HEAD_PAD * kv, HEAD_PAD * (kv + 1)) for kv in groups]
        scores = [_dot_nt(k_r[span, lanes[kv]], _swa_group(q_r, kv)) for kv in groups]
        probs = [_swa_softmax(scores[kv], bias_r[kv], _swa_sink_row(sink_r, kv), valid)[0] for kv in groups]
        low = _low_half(SWA_BLOCK)
        for kv in groups:
            og = _dot_tn(_mx(probs[kv]), v_r[span, lanes[kv]])
            for pair in range(SWA_GROUP // 2):
                even = og[2 * SWA_BLOCK * pair:2 * SWA_BLOCK * pair + SWA_BLOCK]
                odd = og[2 * SWA_BLOCK * pair + SWA_BLOCK:2 * SWA_BLOCK * (pair + 1)]
                first = HEAD_PAD * (kv * SWA_GROUP // 2 + pair)
                o_r[:, first:first + HEAD_PAD] = jnp.where(low, even, pltpu.roll(odd, 64, 1)).astype(o_r.dtype)

    qw = SWA_Q_HEADS * HEAD_PAD
    body, extra, extra_specs = _after(body, 5, dep)
    return pl.pallas_call(
        body, name="swa_fwd", grid=(L // SWA_BLOCK,),
        in_specs=[_row_spec(SWA_BLOCK, qw), _vmem_spec(), _vmem_spec(), _vmem_spec(),
                  pl.BlockSpec(memory_space=pltpu.SMEM)] + extra_specs,
        out_specs=_row_spec(SWA_BLOCK, qw // 2),
        out_shape=jax.ShapeDtypeStruct((L, qw // 2), MXU_DTYPE),
        compiler_params=_params(("arbitrary",), VMEM_BIG),
    )(qs, ks, vs, bias, sink, *extra)


def _swa_bwd_call(qs, ks, vs, bias, sink, do, dep=None):
    L = qs.shape[0]
    qw = SWA_Q_HEADS * HEAD_PAD
    kw = SWA_KV_HEADS * HEAD_PAD

    def body(q_r, k_r, v_r, bias_r, sink_r, do_r, dq_r, dk_r, dv_r, dbias_r, dsink_r):
        n = pl.program_id(0)

        @pl.when(n == 0)
        def _():
            for ref in (dk_r, dv_r, dbias_r, dsink_r):
                ref[...] = jnp.zeros_like(ref)

        span = pl.ds(pl.multiple_of(n * SWA_BLOCK, SWA_BLOCK), SWA_SPAN)
        valid = _swa_valid(n, L)
        groups = range(SWA_KV_HEADS)
        lanes = [slice(HEAD_PAD * kv, HEAD_PAD * (kv + 1)) for kv in groups]
        kk = [k_r[span, sl] for sl in lanes]
        vv = [v_r[span, sl] for sl in lanes]
        qg = [_swa_group(q_r, kv) for kv in groups]
        dog = [_swa_group(do_r, kv) for kv in groups]
        scores = [_dot_nt(kk[kv], qg[kv]) for kv in groups]
        dp = [_dot_nt(vv[kv], dog[kv]) for kv in groups]
        probs = [_swa_softmax(scores[kv], bias_r[kv], _swa_sink_row(sink_r, kv), valid) for kv in groups]
        ds_m, pn_m = [], []
        for kv in groups:
            pn, p_sink = probs[kv]
            delta = jnp.sum(pn * dp[kv], axis=0, keepdims=True)
            ds = pn * (dp[kv] - delta)
            dsink_r[kv] -= p_sink * delta
            dbias_r[kv] += ds
            ds_m.append(_mx(ds))
            pn_m.append(_mx(pn))
        dqg = [_dot_tn(ds_m[kv], kk[kv]) * 0.125 for kv in groups]
        dkk = [_dot(ds_m[kv], qg[kv]) for kv in groups]
        dvv = [_dot(pn_m[kv], dog[kv]) for kv in groups]
        low = _low_half(SWA_BLOCK)
        for kv in groups:
            for pair in range(SWA_GROUP // 2):
                even = dqg[kv][2 * SWA_BLOCK * pair:2 * SWA_BLOCK * pair + SWA_BLOCK]
                odd = dqg[kv][2 * SWA_BLOCK * pair + SWA_BLOCK:2 * SWA_BLOCK * (pair + 1)]
                first = HEAD_PAD * (kv * SWA_GROUP // 2 + pair)
                dq_r[:, first:first + HEAD_PAD] = jnp.where(low, even, pltpu.roll(odd, 64, 1)).astype(dq_r.dtype)
            dk_r[span, lanes[kv]] += dkk[kv]
            dv_r[span, lanes[kv]] += dvv[kv]

    body, extra, extra_specs = _after(body, 6, dep)
    return pl.pallas_call(
        body, name="swa_bwd", grid=(L // SWA_BLOCK,),
        in_specs=[_row_spec(SWA_BLOCK, qw), _vmem_spec(), _vmem_spec(), _vmem_spec(),
                  pl.BlockSpec(memory_space=pltpu.SMEM), _row_spec(SWA_BLOCK, qw)] + extra_specs,
        out_specs=[_row_spec(SWA_BLOCK, qw // 2), _vmem_spec(), _vmem_spec(), _vmem_spec(), _vmem_spec()],
        out_shape=[jax.ShapeDtypeStruct((L, qw // 2), MXU_DTYPE),
                   jax.ShapeDtypeStruct((L + 2 * SWA_BLOCK, kw), F32),
                   jax.ShapeDtypeStruct((L + 2 * SWA_BLOCK, kw), F32),
                   jax.ShapeDtypeStruct((SWA_KV_HEADS, SWA_SPAN, SWA_GROUP_LANES), F32),
                   jax.ShapeDtypeStruct((SWA_KV_HEADS, 1, SWA_GROUP_LANES), F32)],
        compiler_params=_params(("arbitrary",), VMEM_BIG),
    )(qs, ks, vs, bias, sink, do, *extra)


def _bias_call(rel_bias, buckets):
    def body(t_r, bk_r, o_r):
        bk = bk_r[...]
        s = lax.broadcasted_iota(jnp.int32, bk.shape, 0)
        c = lax.broadcasted_iota(jnp.int32, bk.shape, 1)
        in_band = jnp.abs(s - SWA_BLOCK - c) <= SWA_BLOCK
        for h in range(SWA_Q_HEADS):
            acc = jnp.zeros(bk.shape, F32)
            for b in range(REL_BUCKETS):
                acc = jnp.where(bk == b, t_r[b, h], acc)
            g = h % SWA_GROUP
            o_r[h // SWA_GROUP, :, SWA_BLOCK * g:SWA_BLOCK * (g + 1)] = jnp.where(in_band, acc, -1e30)

    return pl.pallas_call(
        body, name="band_bias",
        in_specs=[pl.BlockSpec(memory_space=pltpu.SMEM), _vmem_spec()], out_specs=_vmem_spec(),
        out_shape=jax.ShapeDtypeStruct((SWA_KV_HEADS, SWA_SPAN, SWA_GROUP_LANES), F32),
    )(rel_bias, buckets)


def _relbias_call(dbias, dsink, buckets, dep=None):
    def body(db_r, ds_r, bk_r, o_r, os_r):
        bk = bk_r[...]
        rowi = lax.broadcasted_iota(jnp.int32, (REL_BUCKETS, 128), 0)
        lanei = lax.broadcasted_iota(jnp.int32, (REL_BUCKETS, 128), 1)
        lane1 = lax.broadcasted_iota(jnp.int32, (1, 128), 1)
        acc = jnp.zeros((REL_BUCKETS, 128), F32)
        acc_sink = jnp.zeros((1, 128), F32)
        for h in range(SWA_Q_HEADS):
            kv, g = h // SWA_GROUP, h % SWA_GROUP
            lanes = slice(SWA_BLOCK * g, SWA_BLOCK * (g + 1))
            part = db_r[kv, :, lanes]
            for b in range(REL_BUCKETS):
                s = jnp.sum(jnp.where(bk == b, part, 0.0))
                acc = acc + jnp.where((rowi == b) & (lanei == h), s, 0.0)
            acc_sink = acc_sink + jnp.where(lane1 == h, jnp.sum(ds_r[kv, :, lanes]), 0.0)
        o_r[...] = acc
        os_r[...] = acc_sink

    body, extra, extra_specs = _after(body, 3, dep)
    return pl.pallas_call(
        body, name="relbias_grad",
        in_specs=[_vmem_spec()] * 3 + extra_specs, out_specs=[_vmem_spec()] * 2,
        out_shape=[jax.ShapeDtypeStruct((REL_BUCKETS, 128), F32), jax.ShapeDtypeStruct((1, 128), F32)],
    )(dbias, dsink, buckets, *extra)


def _mix_call(o_f, o_b, ga, o_s, x, gn, w_out_p, g_post, g_pre2, dep=None):
    L = x.shape[0]
    tm = min(512, L)
    hw = GLA_HEADS * HEAD_PAD

    def body(of_r, ob_r, ga_r, os_r, x_r, gn_r, w_r, gp_r, g2_r, cat_r, mix_r, h1_r, n2_r):
        gn_v = gn_r[...]
        for h in range(GLA_HEADS):
            sl = slice(HEAD_PAD * h, HEAD_PAD * (h + 1))
            oh = of_r[:, sl] + ob_r[:, sl]
            on = oh * _rms_r(oh) * gn_v
            gate = ga_r[:, sl]
            cat_r[:, sl] = (on * (gate * jax.nn.sigmoid(gate))).astype(cat_r.dtype)
        os_v = os_r[...]
        cat_r[:, hw:] = os_v
        mix = _dot(cat_r[:, :hw], w_r[:hw, :]) + _dot(os_v, w_r[hw:, :])
        mix_r[...] = mix
        h1 = x_r[...] + mix * _rms_r(mix) * gp_r[...]
        h1_r[...] = h1
        n2_r[...] = (h1 * _rms_r(h1) * g2_r[...]).astype(n2_r.dtype)

    body, extra, extra_specs = _after(body, 9, dep)
    return pl.pallas_call(
        body, name="mix_fwd", grid=(L // tm,),
        in_specs=[_row_spec(tm, hw), _row_spec(tm, hw), _row_spec(tm, hw), _row_spec(tm, OUT_PAD - hw),
                  _row_spec(tm, D_MODEL), _full_spec((1, HEAD_PAD)), _vmem_spec(),
                  _full_spec((1, D_MODEL)), _full_spec((1, D_MODEL))] + extra_specs,
        out_specs=[_row_spec(tm, OUT_PAD), _row_spec(tm, D_MODEL), _row_spec(tm, D_MODEL), _row_spec(tm, D_MODEL)],
        out_shape=[jax.ShapeDtypeStruct((L, OUT_PAD), MXU_DTYPE), jax.ShapeDtypeStruct((L, D_MODEL), F32),
                   jax.ShapeDtypeStruct((L, D_MODEL), F32), jax.ShapeDtypeStruct((L, D_MODEL), MXU_DTYPE)],
        compiler_params=_params(("arbitrary",), VMEM_BIG),
    )(o_f, o_b, ga, o_s, x, gn, w_out_p, g_post, g_pre2, *extra)


def _mlp_fwd_call(n2, h1, tgt, w_ud, g_post):
    L = n2.shape[0]
    tm = min(512, L)
    blk = D_FF // N_CHIPS

    def body(n2_r, h1_r, t_r, w_r, g_r, a_r, rz_r, dh2_r, dff_r, loss_r, dg_r):
        @pl.when(pl.program_id(0) == 0)
        def _():
            loss_r[...] = jnp.zeros_like(loss_r)
            dg_r[...] = jnp.zeros_like(dg_r)

        n2v = n2_r[...]
        ff = jnp.zeros((tm, D_MODEL), F32)
        for j in range(N_CHIPS):
            sl = slice(blk * j, blk * (j + 1))
            rz = jnp.maximum(_dot(n2v, w_r[j, 0]), 0.0)
            a = _mx(rz * rz)
            rz_r[:, sl] = rz.astype(rz_r.dtype)
            a_r[:, sl] = a
            ff = ff + _dot(a, w_r[j, 1])
        g = g_r[...]
        r = _rms_r(ff)
        err = h1_r[...] + ff * r * g - t_r[...]
        loss_r[...] += 0.5 * jnp.sum(err * err) / D_MODEL
        dh2 = err * (1.0 / D_MODEL)
        dh2_r[...] = dh2
        dff, dg = _rms_bwd(ff, r, g, dh2)
        dff_r[...] = dff.astype(dff_r.dtype)
        dg_r[...] += dg

    return pl.pallas_call(
        body, name="mlp_fwd", grid=(L // tm,),
        in_specs=[_row_spec(tm, D_MODEL), _row_spec(tm, D_MODEL), _row_spec(tm, D_MODEL),
                  _vmem_spec(), _full_spec((1, D_MODEL))],
        out_specs=[_row_spec(tm, D_FF), _row_spec(tm, D_FF), _row_spec(tm, D_MODEL), _row_spec(tm, D_MODEL),
                   _full_spec((1, 128)), _full_spec((1, D_MODEL))],
        out_shape=[jax.ShapeDtypeStruct((L, D_FF), MXU_DTYPE), jax.ShapeDtypeStruct((L, D_FF), MXU_DTYPE),
                   jax.ShapeDtypeStruct((L, D_MODEL), F32), jax.ShapeDtypeStruct((L, D_MODEL), MXU_DTYPE),
                   jax.ShapeDtypeStruct((1, 128), F32), jax.ShapeDtypeStruct((1, D_MODEL), F32)],
        compiler_params=_params(("arbitrary",), VMEM_BIG),
    )(n2, h1, tgt, w_ud, g_post)


def _mlp_bwd_call(dff, rz, w_ud):
    L = dff.shape[0]
    tm = min(512, L)
    blk = D_FF // N_CHIPS

    def body(dff_r, rz_r, w_r, dz_r, dn2_r):
        dffv = dff_r[...]
        dn2 = jnp.zeros((tm, D_MODEL), F32)
        for j in range(N_CHIPS):
            sl = slice(blk * j, blk * (j + 1))
            dz = _mx(_dot_nt(dffv, w_r[j, 1]) * 2.0 * rz_r[:, sl].astype(F32))
            dz_r[:, sl] = dz
            dn2 = dn2 + _dot_nt(dz, w_r[j, 0])
        dn2_r[...] = dn2

    return pl.pallas_call(
        body, name="mlp_bwd", grid=(L // tm,),
        in_specs=[_row_spec(tm, D_MODEL), _row_spec(tm, D_FF), _vmem_spec()],
        out_specs=[_row_spec(tm, D_FF), _row_spec(tm, D_MODEL)],
        out_shape=[jax.ShapeDtypeStruct((L, D_FF), MXU_DTYPE), jax.ShapeDtypeStruct((L, D_MODEL), F32)],
        compiler_params=_params(("arbitrary",), VMEM_BIG),
    )(dff, rz, w_ud)


def _mlp_wgrad_call(a, dff, n2, dz):
    L = a.shape[0]
    tf = 512
    per = (D_FF // N_CHIPS) // tf

    def body(a_r, dff_r, n2_r, dz_r, dwd_r, dwu_r):
        dwd_r[...] = _dot_tn(a_r[...], dff_r[...])
        dwu_r[...] = _dot_tn(n2_r[...], dz_r[...])

    return pl.pallas_call(
        body, name="mlp_wgrad", grid=(D_FF // tf,),
        in_specs=[pl.BlockSpec((L, tf), lambda j: (0, j)), _vmem_spec(), _vmem_spec(),
                  pl.BlockSpec((L, tf), lambda j: (0, j))],
        out_specs=[pl.BlockSpec((tf, D_MODEL), lambda j: (j, 0)),
                   pl.BlockSpec((None, D_MODEL, tf), lambda j: (j // per, 0, j % per))],
        out_shape=[jax.ShapeDtypeStruct((D_FF, D_MODEL), F32),
                   jax.ShapeDtypeStruct((N_CHIPS, D_MODEL, D_FF // N_CHIPS), F32)],
        compiler_params=_params(("arbitrary",), VMEM_BIG),
    )(a, dff, n2, dz)


def _mix_bwd_call(dn2, dh2, h1, mix, cat, o_f, o_b, ga, gn, g_post, g_pre2, w_out_p):
    L = dn2.shape[0]
    tm = min(512, L)
    hw = GLA_HEADS * HEAD_PAD

    def body(dn2_r, dh2_r, h1_r, mix_r, cat_r, of_r, ob_r, ga_r, gn_r, gp_r, g2_r, w_r,
             dh1_r, do_r, dga_r, dos_r, dw_r, dg2_r, dgp_r, dgn_r):
        @pl.when(pl.program_id(0) == 0)
        def _():
            for ref in (dw_r, dg2_r, dgp_r, dgn_r):
                ref[...] = jnp.zeros_like(ref)

        parts = [slice(start, start + min(256, tm)) for start in range(0, tm, 256)]
        dmix_m = []
        for rs in parts:
            h1 = h1_r[rs, :]
            dx2, dg2 = _rms_bwd(h1, _rms_r(h1), g2_r[...], dn2_r[rs, :])
            dh1 = dh2_r[rs, :] + dx2
            dh1_r[rs, :] = dh1
            dg2_r[...] += dg2
            mix = mix_r[rs, :]
            dmix, dgp = _rms_bwd(mix, _rms_r(mix), gp_r[...], dh1)
            dgp_r[...] += dgp
            dmix_m.append(_mx(dmix))
        dcat = [_dot_nt(d, w_r[...]) for d in dmix_m]
        for rs, d in zip(parts, dmix_m):
            dw_r[...] += _dot_tn(cat_r[rs, :], d)
        gn_v = gn_r[...]
        dgn = jnp.zeros((1, HEAD_PAD), F32)
        for rs, dc in zip(parts, dcat):
            dos_r[rs, :] = _spread_heads(dc[:, hw:]).astype(dos_r.dtype)
            for h in range(GLA_HEADS):
                sl = slice(HEAD_PAD * h, HEAD_PAD * (h + 1))
                oh = of_r[rs, sl] + ob_r[rs, sl]
                rr = _rms_r(oh)
                xh = oh * rr
                gate = ga_r[rs, sl]
                sg = jax.nn.sigmoid(gate)
                silu = gate * sg
                doa = dc[:, sl]
                dga_r[rs, sl] = (doa * (xh * gn_v) * (sg + silu * (1.0 - sg))).astype(dga_r.dtype)
                don = doa * silu
                gd = don * gn_v
                do_r[rs, sl] = rr * (gd - xh * jnp.mean(gd * xh, axis=-1, keepdims=True))
                dgn = dgn + jnp.sum(don * xh, axis=0, keepdims=True)
        dgn_r[...] += dgn

    return pl.pallas_call(
        body, name="mix_bwd", grid=(L // tm,),
        in_specs=[_row_spec(tm, D_MODEL)] * 4 + [_row_spec(tm, OUT_PAD)] + [_row_spec(tm, hw)] * 3
        + [_full_spec((1, HEAD_PAD)), _full_spec((1, D_MODEL)), _full_spec((1, D_MODEL)), _vmem_spec()],
        out_specs=[_row_spec(tm, D_MODEL), _row_spec(tm, hw), _row_spec(tm, hw),
                   _row_spec(tm, SWA_Q_HEADS * HEAD_PAD),
                   _full_spec((OUT_PAD, D_MODEL)), _full_spec((1, D_MODEL)), _full_spec((1, D_MODEL)),
                   _full_spec((1, HEAD_PAD))],
        out_shape=[jax.ShapeDtypeStruct((L, D_MODEL), F32), jax.ShapeDtypeStruct((L, hw), F32),
                   jax.ShapeDtypeStruct((L, hw), MXU_DTYPE),
                   jax.ShapeDtypeStruct((L, SWA_Q_HEADS * HEAD_PAD), MXU_DTYPE),
                   jax.ShapeDtypeStruct((OUT_PAD, D_MODEL), F32), jax.ShapeDtypeStruct((1, D_MODEL), F32),
                   jax.ShapeDtypeStruct((1, D_MODEL), F32), jax.ShapeDtypeStruct((1, HEAD_PAD), F32)],
        compiler_params=_params(("arbitrary",), VMEM_BIG),
    )(dn2, dh2, h1, mix, cat, o_f, o_b, ga, gn, g_post, g_pre2, w_out_p)


def _in_bwd_call(x, dh1, g_pre, w_in_t, pairs, singles, halos, dep=None):
    L = x.shape[0]
    tm = min(512, L)
    per = tm // SWA_BLOCK
    n_pair, n_single, n_halo = len(pairs), len(singles), len(halos)
    groups = [c for c, _ in pairs] + [c for c, _ in singles] + [c for c, _ in halos]

    def body(*refs):
        x_r, dh1_r, g_r, w_r = refs[:4]
        pair_refs = refs[4:4 + 2 * n_pair]
        single_refs = refs[4 + 2 * n_pair:4 + 2 * n_pair + n_single]
        halo_refs = refs[4 + 2 * n_pair + n_single:4 + 2 * n_pair + n_single + per * n_halo]
        dx_r, dw_r, dg_r = refs[4 + 2 * n_pair + n_single + per * n_halo:]

        @pl.when(pl.program_id(0) == 0)
        def _():
            dw_r[...] = jnp.zeros_like(dw_r)
            dg_r[...] = jnp.zeros_like(dg_r)

        xv = x_r[...]
        r = _rms_r(xv)
        g = g_r[...]
        u = _mx(xv * r * g)
        vals = [pair_refs[2 * i][...].astype(F32) + pair_refs[2 * i + 1][...].astype(F32) for i in range(n_pair)]
        vals += [ref[...].astype(F32) for ref in single_refs]
        vals += [jnp.concatenate([ref[...] for ref in halo_refs[per * i:per * (i + 1)]], axis=0)
                 for i in range(n_halo)]
        ds = [_mx(_squeeze_heads(val) if heads else val) for (_, _, heads), val in zip(groups, vals)]
        du = jnp.zeros((tm, D_MODEL), F32)
        for (first, rows, _), d in zip(groups, ds):
            du = du + _dot(d, w_r[first:first + rows, :])
        for (first, rows, _), d in zip(groups, ds):
            dw_r[first:first + rows, :] += _dot_tn(d, u)
        dx, dg = _rms_bwd(xv, r, g, du)
        dx_r[...] = dh1_r[...] + dx
        dg_r[...] += dg

    arrays = [a for _, pr in pairs for a in pr] + [a for _, a in singles]
    specs = [_row_spec(tm, a.shape[1]) for a in arrays]
    for _, a in halos:
        specs += [pl.BlockSpec((SWA_BLOCK, a.shape[1]), lambda i, j=j: (per * i + 1 + j, 0)) for j in range(per)]
        arrays += [a] * per
    body, extra, extra_specs = _after(body, 4 + len(arrays), dep)
    return pl.pallas_call(
        body, name="in_bwd", grid=(L // tm,),
        in_specs=[_row_spec(tm, D_MODEL), _row_spec(tm, D_MODEL), _full_spec((1, D_MODEL)), _vmem_spec()] + specs
        + extra_specs,
        out_specs=[_row_spec(tm, D_MODEL), _full_spec((IN_COLS, D_MODEL)), _full_spec((1, D_MODEL))],
        out_shape=[jax.ShapeDtypeStruct((L, D_MODEL), F32), jax.ShapeDtypeStruct((IN_COLS, D_MODEL), F32),
                   jax.ShapeDtypeStruct((1, D_MODEL), F32)],
        compiler_params=_params(("arbitrary",), VMEM_BIG),
    )(x, dh1, g_pre, w_in_t, *arrays, *extra)


def _adamw_math(w, g, m, v):
    m = ADAM_B1 * m + (1.0 - ADAM_B1) * g
    v = ADAM_B2 * v + (1.0 - ADAM_B2) * (g * g)
    m_hat = m / (1.0 - ADAM_B1 ** ADAM_STEP)
    v_hat = v / (1.0 - ADAM_B2 ** ADAM_STEP)
    delta = -ADAM_LR * (m_hat / (jnp.sqrt(v_hat) + ADAM_EPS) + ADAM_WD * w)
    return delta, m, v


def _adamw_call(w, g, m, v, name, dep=None):
    rows, cols = w.shape
    tr = min(256, rows)

    def body(w_r, g_r, m_r, v_r, d_r, nm_r, nv_r):
        d_r[...], nm_r[...], nv_r[...] = _adamw_math(w_r[...], g_r[...], m_r[...], v_r[...])

    if rows % tr == 0:
        spec, steps = _row_spec(tr, cols), rows // tr
    else:
        spec, steps = pl.BlockSpec((rows, 256), lambda i: (0, i)), cols // 256
    body, extra, extra_specs = _after(body, 4, dep)
    return pl.pallas_call(
        body, name=name, grid=(steps,),
        in_specs=[spec] * 4 + extra_specs, out_specs=[spec] * 3,
        out_shape=[jax.ShapeDtypeStruct(w.shape, F32)] * 3,
        compiler_params=_params(("arbitrary",)),
    )(w, g, m, v, *extra)


def _position():
    return lax.axis_index("x"), lax.axis_index("y"), lax.axis_index("c")


def _other_chips(x, y):
    return [(1 - x, y), (x, 1 - y), (1 - x, 1 - y)]


ROWS, COLS = -2, -1


def _half(ref, which, axis):
    size = ref.shape[axis] // 2
    span = pl.ds(pl.multiple_of(which * size, 16 if axis == ROWS else 128), size)
    index = [slice(None)] * len(ref.shape)
    index[axis] = span
    return ref.at[tuple(index)]


def _first_gather_call(shards, axes):
    n = len(shards)

    def body(*refs):
        srcs, outs = refs[:n], refs[n:2 * n]
        send_sems, recv_sems, local_sems = refs[2 * n:]
        x, y, c = _position()
        sibling = (x, y, 1 - c)
        chips = _other_chips(x, y)
        local = [pltpu.make_async_copy(srcs[a], outs[a].at[2 * x + y], local_sems.at[a]) for a in range(n)]
        for cp in local:
            cp.start()

        def copy(a, k, block, to, src=None):
            px, py, pc = block
            dst = _half(outs[a].at[2 * px + py], pc, axes[a])
            return pltpu.make_async_remote_copy(
                src_ref=dst if src is None else src, dst_ref=dst, send_sem=send_sems.at[6 * a + k],
                recv_sem=recv_sems.at[6 * a + k], device_id=to, device_id_type=MESH_ID)

        first, passed = [], []
        for a in range(n):
            my_half = _half(srcs[a], c, axes[a])
            first += [copy(a, j, (x, y, c), (*chip, c), src=my_half) for j, chip in enumerate(chips)]
        for cp in first:
            cp.start()
        for a in range(n):
            for j, chip in enumerate(chips):
                copy(a, j, (*chip, c), (x, y, c)).wait_recv()
                passed.append(copy(a, 3 + j, (*chip, c), sibling))
                passed[-1].start()
        for a in range(n):
            for j, chip in enumerate(chips):
                copy(a, 3 + j, (*chip, 1 - c), (x, y, c)).wait_recv()
        for cp in first + passed:
            cp.wait_send()
        for cp in local:
            cp.wait()

    return pl.pallas_call(
        body, name="first_gather",
        in_specs=[_any_spec()] * n, out_specs=[_any_spec()] * n,
        out_shape=[jax.ShapeDtypeStruct((N_CHIPS,) + s.shape, s.dtype) for s in shards],
        scratch_shapes=[pltpu.SemaphoreType.DMA((6 * n,)), pltpu.SemaphoreType.DMA((6 * n,)),
                        pltpu.SemaphoreType.DMA((n,))],
    )(*shards)


def _split_start(name, arrays, n_copies, plan):
    n = len(arrays)

    def body(*refs):
        ins, send_sems, recv_sems, token = refs[:n], refs[n], refs[n + 1], refs[-1]
        for k, (src, dst, to, _) in enumerate(plan(ins)):
            pltpu.make_async_remote_copy(src_ref=src, dst_ref=dst, send_sem=send_sems.at[k],
                                         recv_sem=recv_sems.at[k], device_id=to, device_id_type=MESH_ID).start()
        token[...] = jnp.zeros_like(token)

    hbm = pl.BlockSpec(memory_space=pltpu.HBM)
    sem = pl.BlockSpec(memory_space=pltpu.SEMAPHORE)
    out = pl.pallas_call(
        body, name=name,
        out_shape=(pltpu.SemaphoreType.DMA((n_copies,)), pltpu.SemaphoreType.DMA((n_copies,)))
        + tuple(pltpu.HBM(a.shape, a.dtype) for a in arrays) + (jax.ShapeDtypeStruct((8, 128), F32),),
        in_specs=[hbm] * n, out_specs=(sem, sem) + (hbm,) * n + (_vmem_spec(),),
        input_output_aliases={i: 2 + i for i in range(n)},
        compiler_params=pltpu.CompilerParams(has_side_effects=pltpu.SideEffectType.DATAFLOW_SIDE_EFFECTING),
    )(*[pltpu.with_memory_space_constraint(a, pltpu.HBM) for a in arrays])
    return (out[0], out[1], tuple(out[2:2 + n])), out[-1]


def _split_wait(name, handle, n_copies, plan, after):
    send_sems, recv_sems, arrays = handle
    n = len(arrays)

    def body(*refs):
        ins, s_sems, r_sems = refs[:n], refs[n], refs[n + 1]
        for k, (src, dst, to, landed) in enumerate(plan(ins)):
            cp = pltpu.make_async_remote_copy(src_ref=src, dst_ref=landed, send_sem=s_sems.at[k],
                                              recv_sem=r_sems.at[k], device_id=to, device_id_type=MESH_ID)
            cp.wait_send()
            cp.wait_recv()

    hbm = pl.BlockSpec(memory_space=pltpu.HBM)
    sem = pl.BlockSpec(memory_space=pltpu.SEMAPHORE)
    out = pl.pallas_call(
        body, name=name,
        out_shape=tuple(pltpu.HBM(a.shape, a.dtype) for a in arrays),
        in_specs=[hbm] * n + [sem, sem, _any_spec()], out_specs=(hbm,) * n,
        input_output_aliases={i: i for i in range(n)},
        compiler_params=pltpu.CompilerParams(has_side_effects=pltpu.SideEffectType.DATAFLOW_SIDE_EFFECTING),
    )(*arrays, send_sems, recv_sems, after)
    return tuple(out)


def _gather_plans(axes):
    n = len(axes)

    def stage_one(refs):
        x, y, c = _position()
        copies = []
        for a, axis in enumerate(axes):
            for px, py in _other_chips(x, y):
                copies.append((_half(refs[a], c, axis), _half(refs[n + a].at[2 * x + y], c, axis),
                               (px, py, c), _half(refs[n + a].at[2 * px + py], c, axis)))
        return copies

    def stage_two(refs):
        x, y, c = _position()
        copies = []
        for a, axis in enumerate(axes):
            for px, py in _other_chips(x, y):
                piece = _half(refs[n + a].at[2 * px + py], c, axis)
                copies.append((piece, piece, (x, y, 1 - c), _half(refs[n + a].at[2 * px + py], 1 - c, axis)))
        return copies

    return stage_one, stage_two


def _pair_swap_plan(axes):
    n = len(axes)

    def plan(refs):
        x, y, c = _position()
        return [(_half(refs[a], 1 - c, axes[a]), refs[n + a], (x, y, 1 - c), refs[n + a]) for a in range(n)]

    return plan


def _chip_swap_plan(n):
    def plan(refs):
        x, y, c = _position()
        copies = []
        for a in range(n):
            for j, (px, py) in enumerate(_other_chips(x, y)):
                copies.append((refs[a].at[2 * px + py], refs[n + a].at[j], (px, py, c), refs[n + a].at[j]))
        return copies

    return plan


def _pair_join_plan(axes):
    def plan(refs):
        x, y, c = _position()
        copies = []
        for a, axis in enumerate(axes):
            mine = _half(refs[a], c, axis)
            copies.append((mine, mine, (x, y, 1 - c), _half(refs[a], 1 - c, axis)))
        return copies

    return plan


def _pair_add_call(g, got, pos, name, axis):
    rows, cols = got.shape[1], got.shape[2]
    tr = min(512, rows) if axis == ROWS else rows
    nblk = rows // tr
    if axis == ROWS:
        mine = lambda j, i, p: (j, p[1] * nblk + i, 0)
    else:
        mine = lambda j, i, p: (j, 0, p[1])

    def body(pos_r, g_r, got_r, o_r):
        o_r[...] = (g_r[...] + got_r[...]).astype(o_r.dtype)

    return pl.pallas_call(
        body, name=name,
        grid_spec=pltpu.PrefetchScalarGridSpec(
            num_scalar_prefetch=1, grid=(N_CHIPS, nblk),
            in_specs=[pl.BlockSpec((None, tr, cols), mine),
                      pl.BlockSpec((None, tr, cols), lambda j, i, p: (j, i, 0))],
            out_specs=pl.BlockSpec((None, tr, cols), lambda j, i, p: (j, i, 0))),
        out_shape=jax.ShapeDtypeStruct(got.shape, COMM_DTYPE),
        compiler_params=_params(("arbitrary", "arbitrary"), VMEM_BIG),
    )(pos, g, got)


def _chip_add_call(hsum, got, pos, name, axis):
    rows, cols = hsum.shape[1], hsum.shape[2]
    tr = min(512, rows) if axis == ROWS else rows
    nblk = rows // tr
    if axis == ROWS:
        out_shape, mine = (2 * rows, cols), (lambda i, p: (p[1] * nblk + i, 0))
    else:
        out_shape, mine = (rows, 2 * cols), (lambda i, p: (0, p[1]))

    def body(pos_r, own_r, got_r, o_r):
        acc = own_r[...].astype(F32)
        for j in range(3):
            acc = acc + got_r[j].astype(F32)
        o_r[...] = acc

    return pl.pallas_call(
        body, name=name,
        grid_spec=pltpu.PrefetchScalarGridSpec(
            num_scalar_prefetch=1, grid=(nblk,),
            in_specs=[pl.BlockSpec((None, tr, cols), lambda i, p: (p[0], i, 0)),
                      pl.BlockSpec((3, tr, cols), lambda i, p: (0, i, 0))],
            out_specs=pl.BlockSpec((tr, cols), mine)),
        out_shape=jax.ShapeDtypeStruct(out_shape, F32),
        compiler_params=_params(("arbitrary",), VMEM_BIG),
    )(pos, hsum, got)


SMALL_NAMES = ("norm_mix_pre", "norm_mix_post", "norm_mlp_pre", "norm_mlp_post", "b_gate_fwd", "b_gate_bwd",
               "gla_norm", "swa_sink", "rel_bias")


def _small_update_call(grads, gate_grads, params, dep=None):
    n_dev = 8
    n_small = len(SMALL_NAMES)
    wmv = [t for p in params for t in p]
    shapes = [p[0].shape for p in params]

    def body(*refs):
        g_refs = refs[:n_small + 3]
        wmv_refs = refs[n_small + 3:n_small + 3 + 3 * n_small]
        n_in = n_small + 3 + 3 * n_small
        out_refs = refs[n_in:n_in + 4 * n_small + 3]
        pack_a, pack_b, all_a, all_b, send_sems, recv_sems = refs[n_in + 4 * n_small + 3:]
        x, y, c = _position()
        me = 4 * x + 2 * y + c
        pack_a[...] = jnp.zeros_like(pack_a)
        pack_b[...] = jnp.zeros_like(pack_b)
        for i in range(4):
            pack_a[i:i + 1, :] = g_refs[i][...]
        pack_a[4:5, 0:256] = g_refs[4][...]
        pack_a[5:6, 0:256] = g_refs[5][...]
        pack_a[6:7, 0:128] = g_refs[6][...]
        pack_a[7:8, 0:128] = g_refs[7][...]
        pack_a[7:8, 128:256] = g_refs[11][...]
        pack_b[0:32, 0:128] = g_refs[8][...]
        pack_b[32:48, :] = g_refs[9][...]
        pack_b[48:64, :] = g_refs[10][...]
        all_a[me] = pack_a[...]
        all_b[me] = pack_b[...]
        copies = []
        for k in range(1, n_dev):
            fx, fy, fc = (k >> 2) & 1, (k >> 1) & 1, k & 1
            to = (1 - x if fx else x, 1 - y if fy else y, 1 - c if fc else c)
            for t, (pack, dst) in enumerate(((pack_a, all_a), (pack_b, all_b))):
                copies.append(pltpu.make_async_remote_copy(
                    src_ref=pack, dst_ref=dst.at[me], send_sem=send_sems.at[2 * (k - 1) + t],
                    recv_sem=recv_sems.at[2 * (k - 1) + t], device_id=to, device_id_type=MESH_ID))
        for cp in copies:
            cp.start()
        for cp in copies:
            cp.wait()
        sum_a, sum_b = all_a[0], all_b[0]
        for d in range(1, n_dev):
            sum_a = sum_a + all_a[d]
            sum_b = sum_b + all_b[d]
        gsum = [sum_a[0:1], sum_a[1:2], sum_a[2:3], sum_a[3:4], sum_a[4:5, 0:256], sum_a[5:6, 0:256],
                sum_a[6:7, 0:128], sum_a[7:8, 0:SWA_Q_HEADS], sum_b[0:32, 0:SWA_Q_HEADS]]
        for i in range(n_small):
            w_r, m_r, v_r = wmv_refs[3 * i:3 * i + 3]
            delta, new_m, new_v = _adamw_math(w_r[...], gsum[i], m_r[...], v_r[...])
            out_refs[4 * i][...] = gsum[i]
            out_refs[4 * i + 1][...] = delta
            out_refs[4 * i + 2][...] = new_m
            out_refs[4 * i + 3][...] = new_v
        out_refs[4 * n_small][...] = sum_b[32:48]
        out_refs[4 * n_small + 1][...] = sum_b[48:64]
        out_refs[4 * n_small + 2][...] = sum_a[7:8, 128:256]

    n_in = n_small + 3 + 3 * n_small
    body, extra, extra_specs = _after(body, n_in, dep)
    out_shape = [jax.ShapeDtypeStruct(s, F32) for s in shapes for _ in range(4)]
    out_shape += [jax.ShapeDtypeStruct((GLA_GATE_RANK, 256), F32)] * 2 + [jax.ShapeDtypeStruct((1, 128), F32)]
    out = pl.pallas_call(
        body, name="small_update",
        in_specs=[_whole_spec(a.shape) for a in list(grads) + list(gate_grads) + wmv] + extra_specs,
        out_specs=[_whole_spec(s.shape) for s in out_shape],
        out_shape=out_shape,
        scratch_shapes=[pltpu.VMEM((8, D_MODEL), F32), pltpu.VMEM((64, 256), F32),
                        pltpu.VMEM((n_dev, 8, D_MODEL), F32), pltpu.VMEM((n_dev, 64, 256), F32),
                        pltpu.SemaphoreType.DMA((2 * (n_dev - 1),)), pltpu.SemaphoreType.DMA((2 * (n_dev - 1),))],
    )(*grads, *gate_grads, *wmv, *extra)
    per_name = [tuple(out[4 * i:4 * i + 4]) for i in range(n_small)]
    return per_name, out[4 * n_small], out[4 * n_small + 1], out[4 * n_small + 2]


def _pad_heads(t, n_heads, axis=-1):
    axis = axis % t.ndim
    shape = t.shape
    t = t.reshape(shape[:axis] + (n_heads, 64) + shape[axis + 1:])
    pad = [(0, 0)] * t.ndim
    pad[axis + 1] = (0, HEAD_PAD - 64)
    return jnp.pad(t, pad).reshape(shape[:axis] + (n_heads * HEAD_PAD,) + shape[axis + 1:])


def _unpad_heads(t, n_heads, axis=-1):
    axis = axis % t.ndim
    shape = t.shape
    t = t.reshape(shape[:axis] + (n_heads, HEAD_PAD) + shape[axis + 1:])
    t = lax.slice_in_dim(t, 0, 64, axis=axis + 1)
    return t.reshape(shape[:axis] + (n_heads * 64,) + shape[axis + 1:])


def _pad_gate(w, first_row):
    return jnp.pad(_pad_heads(w, 4), ((first_row, 128 - GLA_GATE_RANK - first_row), (0, 0)))


def _own_slot(shard, chip):
    zone = lax.empty((N_CHIPS,) + shard.shape, shard.dtype)
    return lax.dynamic_update_slice(zone, shard[None], (chip,) + (0,) * shard.ndim)


def _reduce_to_owners(grads, axes, pos, tag, overlap):
    n = len(grads)

    def half_shape(g, axis):
        return (N_CHIPS, g.shape[1] // 2, g.shape[2]) if axis == ROWS else (N_CHIPS, g.shape[1], g.shape[2] // 2)

    lands = [lax.empty(half_shape(g, axis), F32) for g, axis in zip(grads, axes)]
    handle, token = _split_start(tag + "_pair_start", list(grads) + lands, n, _pair_swap_plan(axes))
    got = _split_wait(tag + "_pair_wait", handle, n, _pair_swap_plan(axes), overlap[0](token))
    sums = [_pair_add_call(got[a], got[n + a], pos, f"{tag}_pair_add{a}", axes[a]) for a in range(n)]
    lands = [lax.empty((3,) + s.shape[1:], s.dtype) for s in sums]
    handle, token = _split_start(tag + "_chip_start", sums + lands, 3 * n, _chip_swap_plan(n))
    got = _split_wait(tag + "_chip_wait", handle, 3 * n, _chip_swap_plan(n), overlap[1](token))
    halves = [_chip_add_call(got[a], got[n + a], pos, f"{tag}_chip_add{a}", axes[a]) for a in range(n)]
    handle, token = _split_start(tag + "_join_start", halves, n, _pair_join_plan(axes))
    return _split_wait(tag + "_join_wait", handle, n, _pair_join_plan(axes), overlap[2](token))


def kernel(x, norm_mix_pre, w_in, w_gate_up_fwd, b_gate_fwd, w_gate_up_bwd, b_gate_bwd, gla_norm, swa_sink, rel_bias, w_out, norm_mix_post, norm_mlp_pre, w_up, w_down, norm_mlp_post, loss_target, m_norm_mix_pre, m_w_in, m_w_gate_up_fwd, m_b_gate_fwd, m_w_gate_up_bwd, m_b_gate_bwd, m_gla_norm, m_swa_sink, m_rel_bias, m_w_out, m_norm_mix_post, m_norm_mlp_pre, m_w_up, m_w_down, m_norm_mlp_post, v_norm_mix_pre, v_w_in, v_w_gate_up_fwd, v_b_gate_fwd, v_w_gate_up_bwd, v_b_gate_bwd, v_gla_norm, v_swa_sink, v_rel_bias, v_w_out, v_norm_mix_post, v_norm_mlp_pre, v_w_up, v_w_down, v_norm_mlp_post):
    given = dict(locals())
    cx, cy, cc = _position()
    chip = (2 * cx + cy).astype(jnp.int32)
    pos = jnp.stack([chip, cc.astype(jnp.int32)])
    seq, tgt = x[0], loss_target[0]
    L = seq.shape[0]

    gates = jnp.concatenate([w_gate_up_fwd[0], w_gate_up_bwd[0]], axis=0).astype(COMM_DTYPE)
    all_in, all_gates, all_out = _first_gather_call(
        [w_in[0].T.astype(COMM_DTYPE), gates, w_out[0].astype(COMM_DTYPE)], [COLS, ROWS, ROWS])
    up_down = jnp.stack([w_up[0], w_down[0]]).astype(COMM_DTYPE)
    stage_one, stage_two = _gather_plans([ROWS])
    handle, token = _split_start("gather_chip_start", [up_down, _own_slot(up_down, chip), all_gates], 3, stage_one)

    w_in_t = _mx(all_in.reshape(IN_COLS, D_MODEL))
    w_out_full = _mx(all_out.reshape(N_CHIPS * R_OUT, D_MODEL))
    gates_full = jnp.concatenate([all_gates[j] for j in range(N_CHIPS)], axis=1)
    wgf_p = _mx(_pad_gate(gates_full[:GLA_GATE_RANK], 0))
    wgb_p = _mx(_pad_gate(gates_full[GLA_GATE_RANK:], GLA_GATE_RANK))
    bf_p, bb_p = _pad_heads(b_gate_fwd, 4), _pad_heads(b_gate_bwd, 4)
    buckets = jnp.asarray(_band_buckets())
    bias = _bias_call(rel_bias, buckets)
    sink1 = swa_sink.reshape(SWA_Q_HEADS)

    qa, ka, va, ga, qs, ks, vs, za = _proj_call(seq, norm_mix_pre, w_in_t, dep=token)
    halo = ((SWA_BLOCK, SWA_BLOCK), (0, 0))
    ks_p, vs_p = jnp.pad(ks, halo), jnp.pad(vs, halo)
    o_f, o_b, s_f, s_b = _gla_fwd_call(qa, ka, va, za, wgf_p, bf_p, wgb_p, bb_p)
    o_s = _swa_fwd_call(qs, ks_p, vs_p, bias, sink1)
    arrays = _split_wait("gather_chip_wait", handle, 3, stage_one, o_s)
    handle, token = _split_start("gather_pair_start", list(arrays), 3, stage_two)
    cat, mix, h1, n2 = _mix_call(o_f, o_b, ga, o_s, seq, gla_norm, w_out_full, norm_mix_post, norm_mlp_pre,
                                 dep=token)
    arrays = _split_wait("gather_pair_wait", handle, 3, stage_two, n2)
    w_ud = _mx(arrays[1])
    a, rz, dh2, dff, loss, d_post2 = _mlp_fwd_call(n2, h1, tgt, w_ud, norm_mlp_post)

    dz, dn2 = _mlp_bwd_call(dff, rz, w_ud)
    dw_down, dw_up4 = _mlp_wgrad_call(a, dff, n2, dz)
    dh1, do, dga, dos, dw_out, d_pre2, d_post, d_gn = _mix_bwd_call(
        dn2, dh2, h1, mix, cat, o_f, o_b, ga, gla_norm, norm_mix_post, norm_mlp_pre, w_out_full)
    done = {}

    def swa_backward(tok):
        done["swa"] = _swa_bwd_call(qs, ks_p, vs_p, bias, sink1, dos, dep=tok)
        return done["swa"][0]

    def gla_in_backward(tok):
        done["gla"] = _gla_bwd_call(qa, ka, va, za, do, s_f, s_b, wgf_p, bf_p, wgb_p, bb_p, dep=tok)
        dqf, dkf, dvf, dzf, _, _, dqb, dkb, dvb, dzb, _, _ = done["gla"]
        dqs, dks_p, dvs_p, _, _ = done["swa"]
        done["in"] = _in_bwd_call(
            seq, dh1, norm_mix_pre, w_in_t,
            pairs=[(_side_by_side(T_QA), (dqf, dqb)), (_side_by_side(T_KA), (dkf, dkb)), (T_VA, (dvf, dvb)),
                   (T_ZA, (dzf, dzb))],
            singles=[(T_GA, dga), (_side_by_side(T_QS), dqs)], halos=[(T_KS, dks_p), (T_VS, dvs_p)])
        return done["in"][0]

    def bias_backward(tok):
        done["rel"] = _relbias_call(done["swa"][3], done["swa"][4], buckets, dep=tok)
        return done["rel"][0]

    g_up, g_down, g_out = _reduce_to_owners(
        [dw_up4, dw_down.reshape(N_CHIPS, R_DOWN, D_MODEL), dw_out.reshape(N_CHIPS, R_OUT, D_MODEL)],
        [ROWS, ROWS, ROWS], pos, "mlp", [swa_backward, gla_in_backward, bias_backward])
    dx, dw_in_t, d_pre = done["in"]
    dwf, dbf, dwb, dbb = done["gla"][4], done["gla"][5], done["gla"][10], done["gla"][11]
    drel, dsink = done["rel"]

    small_grads = [d_pre, d_post, d_pre2, d_post2, _unpad_heads(dbf, 4), _unpad_heads(dbb, 4), d_gn, dsink, drel]
    gate_grads = [_unpad_heads(dwf[:GLA_GATE_RANK], 4), _unpad_heads(dwb[GLA_GATE_RANK:2 * GLA_GATE_RANK], 4)]
    small_params = [(given[n], given["m_" + n], given["v_" + n]) for n in SMALL_NAMES]
    upd = {}

    def update_up(tok):
        upd["w_up"] = (g_up,) + tuple(_adamw_call(w_up[0], g_up, m_w_up[0], v_w_up[0], "adamw_w_up", dep=tok))
        return upd["w_up"][1]

    def update_small(tok):
        per_name, gf_sum, gb_sum, upd["loss"] = _small_update_call(small_grads, gate_grads + [loss], small_params,
                                                                   dep=tok)
        upd.update(dict(zip(SMALL_NAMES, per_name)))
        for name, total in (("w_gate_up_fwd", gf_sum), ("w_gate_up_bwd", gb_sum)):
            g = lax.dynamic_slice(total, (0, chip * 64), (GLA_GATE_RANK, 64))
            upd[name] = (g,) + tuple(_adamw_call(given[name][0], g, given["m_" + name][0], given["v_" + name][0],
                                                 "adamw_" + name))
        upd["w_down"] = (g_down,) + tuple(
            _adamw_call(w_down[0], g_down, m_w_down[0], v_w_down[0], "adamw_w_down", dep=gf_sum))
        return upd["w_down"][1]

    def update_out(tok):
        upd["w_out"] = (g_out,) + tuple(_adamw_call(w_out[0], g_out, m_w_out[0], v_w_out[0], "adamw_w_out", dep=tok))
        return upd["w_out"][1]

    (g_in_t,) = _reduce_to_owners([dw_in_t.reshape(N_CHIPS, R_IN, D_MODEL)], [COLS], pos, "in",
                                  [update_up, update_small, update_out])
    in_t = (g_in_t,) + tuple(_adamw_call(w_in[0].T, g_in_t, m_w_in[0].T, v_w_in[0].T, "adamw_w_in"))
    upd["w_in"] = tuple(t.T for t in in_t)

    big = ("w_in", "w_gate_up_fwd", "w_gate_up_bwd", "w_out", "w_up", "w_down")
    names = ["norm_mix_pre", "w_in", "w_gate_up_fwd", "b_gate_fwd", "w_gate_up_bwd", "b_gate_bwd", "gla_norm",
             "swa_sink", "rel_bias", "w_out", "norm_mix_post", "norm_mlp_pre", "w_up", "w_down", "norm_mlp_post"]
    outs = [upd["loss"][0, 0], dx[None]]
    for kind in range(4):
        outs += [upd[n][kind][None] if n in big else upd[n][kind] for n in names]
    return tuple(outs)
```

```python
import math

import numpy as np
import jax
import jax.numpy as jnp
from jax import lax
from jax.experimental import pallas as pl
from jax.experimental.pallas import tpu as pltpu

F32 = jnp.float32
MXU_DTYPE = jnp.bfloat16
COMM_DTYPE = jnp.bfloat16

D_MODEL = 1024
D_FF = 4096
N_CHIPS = 4
GLA_HEADS = 4
GLA_CHUNK = 64
GLA_GATE_RANK = 16
GLA_GATE_NORM = 16.0
SWA_Q_HEADS = 8
SWA_KV_HEADS = 2
SWA_BLOCK = 128
REL_BUCKETS = 32
REL_MAX_DIST = 128
NORM_EPS = 1e-6
HEAD_PAD = 128

ADAM_LR = 0.001
ADAM_B1 = 0.9
ADAM_B2 = 0.999
ADAM_EPS = 1e-08
ADAM_WD = 0.01
ADAM_STEP = 10

OUT_PAD = 1024

R_IN, R_OUT, R_UP, R_DOWN = 584, 256, 1024, 1024

VMEM_BIG = 56 * 1024 * 1024
MESH_AXES = ("x", "y", "c")
MESH_ID = pl.DeviceIdType.MESH


def _mx(a):
    return a.astype(MXU_DTYPE)


def _dot(a, b):
    return jnp.dot(a, b, preferred_element_type=F32)


def _dot_nt(a, b):
    return lax.dot_general(a, b, (((1,), (1,)), ((), ())), preferred_element_type=F32)


def _dot_tn(a, b):
    return lax.dot_general(a, b, (((0,), (0,)), ((), ())), preferred_element_type=F32)


def _rms_r(x):
    return lax.rsqrt(jnp.mean(x * x, axis=-1, keepdims=True) + NORM_EPS)


def _rms_bwd(x, r, g, dy):
    xh = x * r
    gdy = dy * g
    dx = r * (gdy - xh * jnp.mean(gdy * xh, axis=-1, keepdims=True))
    return dx, jnp.sum(dy * xh, axis=0, keepdims=True)


def _low_half(rows):
    return lax.broadcasted_iota(jnp.int32, (rows, HEAD_PAD), 1) < 64


def _spread_heads(x):
    low = _low_half(x.shape[0])
    parts = []
    for p in range(x.shape[1] // HEAD_PAD):
        pair = x[:, HEAD_PAD * p:HEAD_PAD * (p + 1)]
        parts += [jnp.where(low, pair, 0.0), jnp.where(low, pltpu.roll(pair, 64, 1), 0.0)]
    return jnp.concatenate(parts, axis=1)


def _squeeze_heads(x):
    low = _low_half(x.shape[0])
    parts = []
    for p in range(x.shape[1] // (2 * HEAD_PAD)):
        even = x[:, 2 * HEAD_PAD * p:2 * HEAD_PAD * p + HEAD_PAD]
        odd = x[:, 2 * HEAD_PAD * p + HEAD_PAD:2 * HEAD_PAD * (p + 1)]
        parts.append(jnp.where(low, even, pltpu.roll(odd, 64, 1)))
    return parts[0] if len(parts) == 1 else jnp.concatenate(parts, axis=1)


def _params(sem=None, vmem=None):
    kw = {}
    if sem is not None:
        kw["dimension_semantics"] = sem
    if vmem is not None:
        kw["vmem_limit_bytes"] = vmem
    return pltpu.CompilerParams(**kw)


def _vmem_spec():
    return pl.BlockSpec(memory_space=pltpu.VMEM)


def _whole_spec(shape):
    return pl.BlockSpec(shape, lambda: (0,) * len(shape))


def _row_spec(tm, width):
    return pl.BlockSpec((tm, width), lambda i: (i, 0))


def _full_spec(shape):
    return pl.BlockSpec(shape, lambda i: (0,) * len(shape))


def _any_spec():
    return pl.BlockSpec(memory_space=pl.ANY)


def _after(body, n_in, dep):
    if dep is None:
        return body, [], []
    return (lambda *refs: body(*refs[:n_in], *refs[n_in + 1:])), [dep], [_any_spec()]


T_QA, T_KA, T_VA, T_GA = (0, 256, 4), (256, 256, 4), (512, 512, 0), (1024, 512, 0)
T_QS, T_KS, T_VS = (1568, 512, 8), (2080, 128, 2), (2208, 128, 2)
T_ZA = (1536, 128, 0)
ZA_COLS = 2 * GLA_GATE_RANK
IN_COLS = 2336


def _side_by_side(group):
    return group[0], group[1], 0


def _proj_call(x, g_pre, w_in_t, dep=None):
    L = x.shape[0]
    tm = min(512, L)
    groups = [(T_QA, F32), (T_KA, F32), (T_VA, MXU_DTYPE), (T_GA, F32),
              (T_QS, MXU_DTYPE), (T_KS, MXU_DTYPE), (T_VS, MXU_DTYPE), (T_ZA, F32)]
    widths = [rows * (2 if heads else 1) for (_, rows, heads), _ in groups]

    def body(x_ref, g_ref, w_ref, *outs):
        xv = x_ref[...]
        u = _mx(xv * _rms_r(xv) * g_ref[...])
        for ref, (grp, _) in zip(outs, groups):
            first, rows, heads = grp
            val = _dot_nt(u, w_ref[first:first + rows, :])
            if heads:
                val = _spread_heads(val)
            if grp is T_ZA:
                val = jnp.where(lax.broadcasted_iota(jnp.int32, val.shape, 1) < ZA_COLS, val, 0.0)
            if grp is T_QS:
                val = val * 0.125
            ref[...] = val.astype(ref.dtype)

    body, extra, extra_specs = _after(body, 3, dep)
    return pl.pallas_call(
        body, name="proj_fwd", grid=(L // tm,),
        in_specs=[_row_spec(tm, D_MODEL), _full_spec((1, D_MODEL)), _vmem_spec()] + extra_specs,
        out_specs=[_row_spec(tm, w) for w in widths],
        out_shape=[jax.ShapeDtypeStruct((L, w), dt) for w, (_, dt) in zip(widths, groups)],
        compiler_params=_params(("arbitrary",), VMEM_BIG),
    )(x, g_pre, w_in_t, *extra)


def _tri_masks():
    row = lax.broadcasted_iota(jnp.int32, (GLA_CHUNK, GLA_CHUNK), 0)
    col = lax.broadcasted_iota(jnp.int32, (GLA_CHUNK, GLA_CHUNK), 1)
    return row >= col, row <= col


def _chunk_sums(tri_m, x):
    hi = _mx(x)
    rest = x - hi.astype(F32)
    mid = _mx(rest)
    lo = _mx(rest - mid.astype(F32))
    return _dot(tri_m, hi) + _dot(tri_m, mid) + _dot(tri_m, lo)


def _gla_block_pre(q_r, k_r, z_r, w_r, b_r, rev, nc, qd_s, ki_s, ks_s, dec_s, keep=None):
    tri_f, tri_b = _tri_masks()
    tri_m = _mx((tri_b if rev else tri_f).astype(F32))
    g = _dot(_mx(z_r[...]), w_r[...]) + b_r[...]
    la = (jnp.minimum(g, 0.0) - jnp.log(1.0 + jnp.exp(-jnp.abs(g)))) / GLA_GATE_NORM
    sums, lasts = [], []
    for c in range(nc):
        b_c = _chunk_sums(tri_m, la[GLA_CHUNK * c:GLA_CHUNK * (c + 1)])
        blast = b_c[0:1] if rev else b_c[GLA_CHUNK - 1:GLA_CHUNK]
        dec_s[c] = jnp.exp(blast)
        sums.append(b_c)
        lasts.append(jnp.broadcast_to(blast, b_c.shape))
    b = jnp.concatenate(sums, axis=0)
    eb = jnp.exp(b)
    enb = jnp.exp(-b)
    elb = jnp.exp(jnp.concatenate(lasts, axis=0) - b)
    k = k_r[...]
    qd_s[...] = (q_r[...] * 0.125 * eb).astype(qd_s.dtype)
    ki_s[...] = (k * enb).astype(ki_s.dtype)
    ks_s[...] = (k * elb).astype(ks_s.dtype)
    if keep is not None:
        for ref, val in zip(keep, (g, eb, enb, elb)):
            ref[...] = val


def _gla_fwd_call(qa, ka, va, za, wgf, bgf, wgb, bgb):
    L = qa.shape[0]
    br = min(512, L)
    nb, nc, n_chunks = L // br, br // GLA_CHUNK, L // GLA_CHUNK
    hw = GLA_HEADS * HEAD_PAD

    def body(qaf, kaf, vaf, zaf, qab, kab, vab, zab, wgf_r, bgf_r, wgb_r, bgb_r,
             of_r, ob_r, sf_r, sb_r, st_f, st_b, pre_f, pre_b):
        @pl.when(pl.program_id(0) == 0)
        def _():
            st_f[...] = jnp.zeros_like(st_f)
            st_b[...] = jnp.zeros_like(st_b)

        _gla_block_pre(qaf, kaf, zaf, wgf_r, bgf_r, False, nc, *pre_f)
        _gla_block_pre(qab, kab, zab, wgb_r, bgb_r, True, nc, *pre_b)
        tri_f, tri_b = _tri_masks()

        def one(tri, pre, v_r, o_r, s_r, st, ci):
            qd_s, ki_s, ks_s, dec_s = pre
            rows = pl.ds(pl.multiple_of(ci * GLA_CHUNK, GLA_CHUNK), GLA_CHUNK)
            dec = dec_s[ci]
            heads = range(GLA_HEADS)
            lanes = [slice(HEAD_PAD * h, HEAD_PAD * (h + 1)) for h in heads]
            qd = [qd_s[rows, sl] for sl in lanes]
            v = [v_r[rows, sl] for sl in lanes]
            s_t = [st[h] for h in heads]
            a = [_dot_nt(qd[h], ki_s[rows, lanes[h]]) for h in heads]
            carried = [_dot_nt(qd[h], _mx(s_t[h])) for h in heads]
            grown = [_dot_tn(v[h], ks_s[rows, lanes[h]]) for h in heads]
            a = [_mx(jnp.where(tri, a[h], 0.0)) for h in heads]
            inner = [_dot(a[h], v[h]) for h in heads]
            for h in heads:
                s_r[ci, h] = s_t[h].astype(s_r.dtype)
                o_r[rows, lanes[h]] = inner[h] + carried[h]
                st[h] = s_t[h] * dec[:, lanes[h]] + grown[h]

        def loop(t, carry):
            one(tri_f, pre_f, vaf, of_r, sf_r, st_f, t)
            one(tri_b, pre_b, vab, ob_r, sb_r, st_b, nc - 1 - t)
            return carry

        lax.fori_loop(0, nc, loop, 0, unroll=True)

    fwd = lambda i: (i, 0)
    bwd = lambda i: (nb - 1 - i, 0)
    ins = lambda m: [pl.BlockSpec((br, hw), m), pl.BlockSpec((br, hw), m),
                     pl.BlockSpec((br, hw), m), pl.BlockSpec((br, 128), m)]
    wspecs = [_full_spec((128, hw)), _full_spec((1, hw))] * 2
    s_shape = (nc, GLA_HEADS, HEAD_PAD, HEAD_PAD)
    pre_scratch = [pltpu.VMEM((br, hw), MXU_DTYPE)] * 3 + [pltpu.VMEM((nc, 1, hw), F32)]
    return pl.pallas_call(
        body, name="gla_fwd", grid=(nb,),
        in_specs=ins(fwd) + ins(bwd) + wspecs,
        out_specs=[pl.BlockSpec((br, hw), fwd), pl.BlockSpec((br, hw), bwd),
                   pl.BlockSpec(s_shape, lambda i: (i, 0, 0, 0)),
                   pl.BlockSpec(s_shape, lambda i: (nb - 1 - i, 0, 0, 0))],
        out_shape=[jax.ShapeDtypeStruct((L, hw), F32), jax.ShapeDtypeStruct((L, hw), F32),
                   jax.ShapeDtypeStruct((n_chunks,) + s_shape[1:], MXU_DTYPE),
                   jax.ShapeDtypeStruct((n_chunks,) + s_shape[1:], MXU_DTYPE)],
        scratch_shapes=[pltpu.VMEM(s_shape[1:], F32), pltpu.VMEM(s_shape[1:], F32), pre_scratch, pre_scratch],
        compiler_params=_params(("arbitrary",), VMEM_BIG),
    )(qa, ka, va, za, qa, ka, va, za, wgf, bgf, wgb, bgb)


def _gla_bwd_call(qa, ka, va, za, do, sf, sb, wgf, bgf, wgb, bgb, dep=None):
    L = qa.shape[0]
    br = min(512, L)
    nb, nc = L // br, br // GLA_CHUNK
    hw = GLA_HEADS * HEAD_PAD

    def body(qaf, kaf, vaf, zaf, dof, sf_r, qab, kab, vab, zab, dob, sb_r, wgf_r, bgf_r, wgb_r, bgb_r,
             dqf, dkf, dvf, dzf, dwf, dbf, dqb, dkb, dvb, dzb, dwb, dbb, gt_f, gt_b, pre_f, pre_b):
        @pl.when(pl.program_id(0) == 0)
        def _():
            for ref in (gt_f, gt_b, dwf, dbf, dwb, dbb):
                ref[...] = jnp.zeros_like(ref)

        _gla_block_pre(qaf, kaf, zaf, wgf_r, bgf_r, False, nc, *pre_f[:4], keep=pre_f[4:8])
        _gla_block_pre(qab, kab, zab, wgb_r, bgb_r, True, nc, *pre_b[:4], keep=pre_b[4:8])
        tri_f, tri_b = _tri_masks()
        row_w = lax.broadcasted_iota(jnp.int32, (GLA_CHUNK, HEAD_PAD), 0)

        def one(rev, pre, q_r, k_r, v_r, do_r, s_r, dq_r, dk_r, dv_r, gt, ci):
            qd_s, ki_s, ks_s, dec_s, _, eb_s, enb_s, elb_s, db_s = pre
            tri = tri_b if rev else tri_f
            last_row = 0 if rev else GLA_CHUNK - 1
            rows = pl.ds(pl.multiple_of(ci * GLA_CHUNK, GLA_CHUNK), GLA_CHUNK)
            dec = dec_s[ci]
            heads = range(GLA_HEADS)
            lanes = [slice(HEAD_PAD * h, HEAD_PAD * (h + 1)) for h in heads]
            qd = [qd_s[rows, sl] for sl in lanes]
            ki = [ki_s[rows, sl] for sl in lanes]
            ks = [ks_s[rows, sl] for sl in lanes]
            v = [v_r[rows, sl] for sl in lanes]
            do_h = [_mx(do_r[rows, sl]) for sl in lanes]
            s_t = [s_r[ci, h] for h in heads]
            g_t = [gt[h] for h in heads]
            g_m = [_mx(g_t[h]) for h in heads]
            a = [_dot_nt(qd[h], ki[h]) for h in heads]
            da = [_dot_nt(do_h[h], v[h]) for h in heads]
            dv_carried = [_dot_nt(ks[h], g_m[h]) for h in heads]
            dqd_carried = [_dot(do_h[h], _mx(s_t[h])) for h in heads]
            dks = [_dot(v[h], g_m[h]) for h in heads]
            g_grown = [_dot_tn(do_h[h], qd[h]) for h in heads]
            a = [_mx(jnp.where(tri, a[h], 0.0)) for h in heads]
            da = [_mx(jnp.where(tri, da[h], 0.0)) for h in heads]
            dv_inner = [_dot_tn(a[h], do_h[h]) for h in heads]
            dqd_inner = [_dot(da[h], ki[h]) for h in heads]
            dki = [_dot_tn(da[h], qd[h]) for h in heads]
            dq, dk = [], []
            for h in heads:
                sl = lanes[h]
                dv_r[rows, sl] = (dv_inner[h] + dv_carried[h]).astype(dv_r.dtype)
                ddec = jnp.sum(g_t[h] * s_t[h].astype(F32), axis=0, keepdims=True)
                gt[h] = g_t[h] * dec[:, sl] + g_grown[h]
                dq.append((dqd_inner[h] + dqd_carried[h]) * eb_s[rows, sl] * 0.125)
                dk_state = dks[h] * elb_s[rows, sl]
                dk.append(dki[h] * enb_s[rows, sl] + dk_state)
                k = k_r[rows, sl]
                dblast = jnp.sum(dk_state * k, axis=0, keepdims=True) + dec[:, sl] * ddec
                db_s[rows, sl] = q_r[rows, sl] * dq[h] - k * dk[h] + jnp.where(row_w == last_row, dblast, 0.0)
            low = _low_half(GLA_CHUNK)
            for pair in range(GLA_HEADS // 2):
                psl = slice(HEAD_PAD * pair, HEAD_PAD * (pair + 1))
                for ref, val in ((dq_r, dq), (dk_r, dk)):
                    both = jnp.where(low, val[2 * pair], pltpu.roll(val[2 * pair + 1], 64, 1))
                    ref[rows, psl] = both.astype(ref.dtype)

        def loop(t, carry):
            one(False, pre_f, qaf, kaf, vaf, dof, sf_r, dqf, dkf, dvf, gt_f, nc - 1 - t)
            one(True, pre_b, qab, kab, vab, dob, sb_r, dqb, dkb, dvb, gt_b, t)
            return carry

        lax.fori_loop(0, nc, loop, 0, unroll=True)

        def gate_grads(rev, pre, z_r, w_r, dz_r, dw_r, dbias_r):
            g_s, db_s = pre[4], pre[8]
            back_m = _mx((tri_f if rev else tri_b).astype(F32))
            db = db_s[...]
            dla = jnp.concatenate([_chunk_sums(back_m, db[GLA_CHUNK * c:GLA_CHUNK * (c + 1)]) for c in range(nc)],
                                  axis=0)
            dg = dla * (1.0 / GLA_GATE_NORM) * (1.0 / (1.0 + jnp.exp(g_s[...])))
            dg_m = _mx(dg)
            dz_r[...] = _dot_nt(dg_m, w_r[...])
            dw_r[...] += _dot_tn(_mx(z_r[...]), dg_m)
            dbias_r[...] += jnp.sum(dg, axis=0, keepdims=True)

        gate_grads(False, pre_f, zaf, wgf_r, dzf, dwf, dbf)
        gate_grads(True, pre_b, zab, wgb_r, dzb, dwb, dbb)

    last_first = lambda i: (nb - 1 - i, 0)
    first_last = lambda i: (i, 0)
    s_shape = (nc, GLA_HEADS, HEAD_PAD, HEAD_PAD)

    def ins(m):
        return [pl.BlockSpec((br, hw), m), pl.BlockSpec((br, hw), m), pl.BlockSpec((br, hw), m),
                pl.BlockSpec((br, 128), m), pl.BlockSpec((br, hw), m),
                pl.BlockSpec(s_shape, lambda i: m(i) + (0, 0))]

    def outs(m):
        return [pl.BlockSpec((br, hw // 2), m), pl.BlockSpec((br, hw // 2), m), pl.BlockSpec((br, hw), m),
                pl.BlockSpec((br, 128), m), _full_spec((128, hw)), _full_spec((1, hw))]

    out_shape = [jax.ShapeDtypeStruct((L, hw // 2), MXU_DTYPE)] * 2 + [
        jax.ShapeDtypeStruct((L, hw), MXU_DTYPE),
        jax.ShapeDtypeStruct((L, 128), F32), jax.ShapeDtypeStruct((128, hw), F32),
        jax.ShapeDtypeStruct((1, hw), F32)]
    wspecs = [_full_spec((128, hw)), _full_spec((1, hw))] * 2
    body, extra, extra_specs = _after(body, 16, dep)
    pre_scratch = ([pltpu.VMEM((br, hw), MXU_DTYPE)] * 3 + [pltpu.VMEM((nc, 1, hw), F32)]
                   + [pltpu.VMEM((br, hw), F32)] * 5)
    return pl.pallas_call(
        body, name="gla_bwd", grid=(nb,),
        in_specs=ins(last_first) + ins(first_last) + wspecs + extra_specs,
        out_specs=outs(last_first) + outs(first_last),
        out_shape=out_shape + out_shape,
        scratch_shapes=[pltpu.VMEM(s_shape[1:], F32), pltpu.VMEM(s_shape[1:], F32), pre_scratch, pre_scratch],
        compiler_params=_params(("arbitrary",), VMEM_BIG),
    )(qa, ka, va, za, do, sf, qa, ka, va, za, do, sb, wgf, bgf, wgb, bgb, *extra)


def _t5_buckets(rel):
    nb = REL_BUCKETS // 2
    ret = (rel > 0).astype(np.int32) * nb
    n = np.abs(rel)
    max_exact = nb // 2
    large = max_exact + (np.log(np.maximum(n, 1).astype(np.float32) / max_exact)
                         / math.log(REL_MAX_DIST / max_exact) * (nb - max_exact)).astype(np.int32)
    large = np.minimum(large, nb - 1)
    return ret + np.where(n < max_exact, n, large)


SWA_GROUP = SWA_Q_HEADS // SWA_KV_HEADS
SWA_SPAN = 3 * SWA_BLOCK
SWA_GROUP_LANES = SWA_GROUP * SWA_BLOCK


def _band_buckets():
    s = np.arange(SWA_SPAN)[:, None]
    c = np.arange(SWA_BLOCK)[None, :]
    return _t5_buckets(s - SWA_BLOCK - c).astype(np.int32)


def _swa_valid(n, seq_len):
    key_pos = (n - 1) * SWA_BLOCK + lax.broadcasted_iota(jnp.int32, (SWA_SPAN, 1), 0)
    return (key_pos >= 0) & (key_pos < seq_len)


def _swa_sink_row(sink_r, kv):
    lane = lax.broadcasted_iota(jnp.int32, (1, SWA_GROUP_LANES), 1)
    row = jnp.full((1, SWA_GROUP_LANES), sink_r[kv * SWA_GROUP], F32)
    for g in range(1, SWA_GROUP):
        row = jnp.where(lane >= g * SWA_BLOCK, sink_r[kv * SWA_GROUP + g], row)
    return row


def _swa_group(ref, kv):
    first = kv * SWA_GROUP
    return jnp.concatenate([ref[:, HEAD_PAD * h:HEAD_PAD * (h + 1)] for h in range(first, first + SWA_GROUP)],
                           axis=0)


def _swa_softmax(scores, bias_t, sink_row, valid):
    st = jnp.where(valid, scores + bias_t, -1e30)
    m = jnp.maximum(jnp.max(st, axis=0, keepdims=True), sink_row)
    p = jnp.exp(st - m)
    e_sink = jnp.exp(sink_row - m)
    inv = 1.0 / (jnp.sum(p, axis=0, keepdims=True) + e_sink)
    return p * inv, e_sink * inv


def _swa_fwd_call(qs, ks, vs, bias, sink, dep=None):
    L = qs.shape[0]

    def body(q_r, k_r, v_r, bias_r, sink_r, o_r):
        n = pl.program_id(0)
        span = pl.ds(pl.multiple_of(n * SWA_BLOCK, SWA_BLOCK), SWA_SPAN)
        valid = _swa_valid(n, L)
        groups = range(SWA_KV_HEADS)
        lanes = [slice(HEAD_PAD * kv, HEAD_PAD * (kv + 1)) for kv in groups]
        scores = [_dot_nt(k_r[span, lanes[kv]], _swa_group(q_r, kv)) for kv in groups]
        probs = [_swa_softmax(scores[kv], bias_r[kv], _swa_sink_row(sink_r, kv), valid)[0] for kv in groups]
        low = _low_half(SWA_BLOCK)
        for kv in groups:
            og = _dot_tn(_mx(probs[kv]), v_r[span, lanes[kv]])
            for pair in range(SWA_GROUP // 2):
                even = og[2 * SWA_BLOCK * pair:2 * SWA_BLOCK * pair + SWA_BLOCK]
                odd = og[2 * SWA_BLOCK * pair + SWA_BLOCK:2 * SWA_BLOCK * (pair + 1)]
                first = HEAD_PAD * (kv * SWA_GROUP // 2 + pair)
                o_r[:, first:first + HEAD_PAD] = jnp.where(low, even, pltpu.roll(odd, 64, 1)).astype(o_r.dtype)

    qw = SWA_Q_HEADS * HEAD_PAD
    body, extra, extra_specs = _after(body, 5, dep)
    return pl.pallas_call(
        body, name="swa_fwd", grid=(L // SWA_BLOCK,),
        in_specs=[_row_spec(SWA_BLOCK, qw), _vmem_spec(), _vmem_spec(), _vmem_spec(),
                  pl.BlockSpec(memory_space=pltpu.SMEM)] + extra_specs,
        out_specs=_row_spec(SWA_BLOCK, qw // 2),
        out_shape=jax.ShapeDtypeStruct((L, qw // 2), MXU_DTYPE),
        compiler_params=_params(("arbitrary",), VMEM_BIG),
    )(qs, ks, vs, bias, sink, *extra)


def _swa_bwd_call(qs, ks, vs, bias, sink, do, dep=None):
    L = qs.shape[0]
    qw = SWA_Q_HEADS * HEAD_PAD
    kw = SWA_KV_HEADS * HEAD_PAD

    def body(q_r, k_r, v_r, bias_r, sink_r, do_r, dq_r, dk_r, dv_r, dbias_r, dsink_r):
        n = pl.program_id(0)

        @pl.when(n == 0)
        def _():
            for ref in (dk_r, dv_r, dbias_r, dsink_r):
                ref[...] = jnp.zeros_like(ref)

        span = pl.ds(pl.multiple_of(n * SWA_BLOCK, SWA_BLOCK), SWA_SPAN)
        valid = _swa_valid(n, L)
        groups = range(SWA_KV_HEADS)
        lanes = [slice(HEAD_PAD * kv, HEAD_PAD * (kv + 1)) for kv in groups]
        kk = [k_r[span, sl] for sl in lanes]
        vv = [v_r[span, sl] for sl in lanes]
        qg = [_swa_group(q_r, kv) for kv in groups]
        dog = [_swa_group(do_r, kv) for kv in groups]
        scores = [_dot_nt(kk[kv], qg[kv]) for kv in groups]
        dp = [_dot_nt(vv[kv], dog[kv]) for kv in groups]
        probs = [_swa_softmax(scores[kv], bias_r[kv], _swa_sink_row(sink_r, kv), valid) for kv in groups]
        ds_m, pn_m = [], []
        for kv in groups:
            pn, p_sink = probs[kv]
            delta = jnp.sum(pn * dp[kv], axis=0, keepdims=True)
            ds = pn * (dp[kv] - delta)
            dsink_r[kv] -= p_sink * delta
            dbias_r[kv] += ds
            ds_m.append(_mx(ds))
            pn_m.append(_mx(pn))
        dqg = [_dot_tn(ds_m[kv], kk[kv]) * 0.125 for kv in groups]
        dkk = [_dot(ds_m[kv], qg[kv]) for kv in groups]
        dvv = [_dot(pn_m[kv], dog[kv]) for kv in groups]
        low = _low_half(SWA_BLOCK)
        for kv in groups:
            for pair in range(SWA_GROUP // 2):
                even = dqg[kv][2 * SWA_BLOCK * pair:2 * SWA_BLOCK * pair + SWA_BLOCK]
                odd = dqg[kv][2 * SWA_BLOCK * pair + SWA_BLOCK:2 * SWA_BLOCK * (pair + 1)]
                first = HEAD_PAD * (kv * SWA_GROUP // 2 + pair)
                dq_r[:, first:first + HEAD_PAD] = jnp.where(low, even, pltpu.roll(odd, 64, 1)).astype(dq_r.dtype)
            dk_r[span, lanes[kv]] += dkk[kv]
            dv_r[span, lanes[kv]] += dvv[kv]

    body, extra, extra_specs = _after(body, 6, dep)
    return pl.pallas_call(
        body, name="swa_bwd", grid=(L // SWA_BLOCK,),
        in_specs=[_row_spec(SWA_BLOCK, qw), _vmem_spec(), _vmem_spec(), _vmem_spec(),
                  pl.BlockSpec(memory_space=pltpu.SMEM), _row_spec(SWA_BLOCK, qw)] + extra_specs,
        out_specs=[_row_spec(SWA_BLOCK, qw // 2), _vmem_spec(), _vmem_spec(), _vmem_spec(), _vmem_spec()],
        out_shape=[jax.ShapeDtypeStruct((L, qw // 2), MXU_DTYPE),
                   jax.ShapeDtypeStruct((L + 2 * SWA_BLOCK, kw), F32),
                   jax.ShapeDtypeStruct((L + 2 * SWA_BLOCK, kw), F32),
                   jax.ShapeDtypeStruct((SWA_KV_HEADS, SWA_SPAN, SWA_GROUP_LANES), F32),
                   jax.ShapeDtypeStruct((SWA_KV_HEADS, 1, SWA_GROUP_LANES), F32)],
        compiler_params=_params(("arbitrary",), VMEM_BIG),
    )(qs, ks, vs, bias, sink, do, *extra)


def _bias_call(rel_bias, buckets):
    def body(t_r, bk_r, o_r):
        bk = bk_r[...]
        s = lax.broadcasted_iota(jnp.int32, bk.shape, 0)
        c = lax.broadcasted_iota(jnp.int32, bk.shape, 1)
        in_band = jnp.abs(s - SWA_BLOCK - c) <= SWA_BLOCK
        for h in range(SWA_Q_HEADS):
            acc = jnp.zeros(bk.shape, F32)
            for b in range(REL_BUCKETS):
                acc = jnp.where(bk == b, t_r[b, h], acc)
            g = h % SWA_GROUP
            o_r[h // SWA_GROUP, :, SWA_BLOCK * g:SWA_BLOCK * (g + 1)] = jnp.where(in_band, acc, -1e30)

    return pl.pallas_call(
        body, name="band_bias",
        in_specs=[pl.BlockSpec(memory_space=pltpu.SMEM), _vmem_spec()], out_specs=_vmem_spec(),
        out_shape=jax.ShapeDtypeStruct((SWA_KV_HEADS, SWA_SPAN, SWA_GROUP_LANES), F32),
    )(rel_bias, buckets)


def _relbias_call(dbias, dsink, buckets, dep=None):
    def body(db_r, ds_r, bk_r, o_r, os_r):
        bk = bk_r[...]
        rowi = lax.broadcasted_iota(jnp.int32, (REL_BUCKETS, 128), 0)
        lanei = lax.broadcasted_iota(jnp.int32, (REL_BUCKETS, 128), 1)
        lane1 = lax.broadcasted_iota(jnp.int32, (1, 128), 1)
        acc = jnp.zeros((REL_BUCKETS, 128), F32)
        acc_sink = jnp.zeros((1, 128), F32)
        for h in range(SWA_Q_HEADS):
            kv, g = h // SWA_GROUP, h % SWA_GROUP
            lanes = slice(SWA_BLOCK * g, SWA_BLOCK * (g + 1))
            part = db_r[kv, :, lanes]
            for b in range(REL_BUCKETS):
                s = jnp.sum(jnp.where(bk == b, part, 0.0))
                acc = acc + jnp.where((rowi == b) & (lanei == h), s, 0.0)
            acc_sink = acc_sink + jnp.where(lane1 == h, jnp.sum(ds_r[kv, :, lanes]), 0.0)
        o_r[...] = acc
        os_r[...] = acc_sink

    body, extra, extra_specs = _after(body, 3, dep)
    return pl.pallas_call(
        body, name="relbias_grad",
        in_specs=[_vmem_spec()] * 3 + extra_specs, out_specs=[_vmem_spec()] * 2,
        out_shape=[jax.ShapeDtypeStruct((REL_BUCKETS, 128), F32), jax.ShapeDtypeStruct((1, 128), F32)],
    )(dbias, dsink, buckets, *extra)


def _mix_call(o_f, o_b, ga, o_s, x, gn, w_out_p, g_post, g_pre2, dep=None):
    L = x.shape[0]
    tm = min(512, L)
    hw = GLA_HEADS * HEAD_PAD

    def body(of_r, ob_r, ga_r, os_r, x_r, gn_r, w_r, gp_r, g2_r, cat_r, mix_r, h1_r, n2_r):
        gn_v = gn_r[...]
        for h in range(GLA_HEADS):
            sl = slice(HEAD_PAD * h, HEAD_PAD * (h + 1))
            oh = of_r[:, sl] + ob_r[:, sl]
            on = oh * _rms_r(oh) * gn_v
            gate = ga_r[:, sl]
            cat_r[:, sl] = (on * (gate * jax.nn.sigmoid(gate))).astype(cat_r.dtype)
        os_v = os_r[...]
        cat_r[:, hw:] = os_v
        mix = _dot(cat_r[:, :hw], w_r[:hw, :]) + _dot(os_v, w_r[hw:, :])
        mix_r[...] = mix
        h1 = x_r[...] + mix * _rms_r(mix) * gp_r[...]
        h1_r[...] = h1
        n2_r[...] = (h1 * _rms_r(h1) * g2_r[...]).astype(n2_r.dtype)

    body, extra, extra_specs = _after(body, 9, dep)
    return pl.pallas_call(
        body, name="mix_fwd", grid=(L // tm,),
        in_specs=[_row_spec(tm, hw), _row_spec(tm, hw), _row_spec(tm, hw), _row_spec(tm, OUT_PAD - hw),
                  _row_spec(tm, D_MODEL), _full_spec((1, HEAD_PAD)), _vmem_spec(),
                  _full_spec((1, D_MODEL)), _full_spec((1, D_MODEL))] + extra_specs,
        out_specs=[_row_spec(tm, OUT_PAD), _row_spec(tm, D_MODEL), _row_spec(tm, D_MODEL), _row_spec(tm, D_MODEL)],
        out_shape=[jax.ShapeDtypeStruct((L, OUT_PAD), MXU_DTYPE), jax.ShapeDtypeStruct((L, D_MODEL), F32),
                   jax.ShapeDtypeStruct((L, D_MODEL), F32), jax.ShapeDtypeStruct((L, D_MODEL), MXU_DTYPE)],
        compiler_params=_params(("arbitrary",), VMEM_BIG),
    )(o_f, o_b, ga, o_s, x, gn, w_out_p, g_post, g_pre2, *extra)


def _mlp_fwd_call(n2, h1, tgt, w_ud, g_post):
    L = n2.shape[0]
    tm = min(512, L)
    blk = D_FF // N_CHIPS

    def body(n2_r, h1_r, t_r, w_r, g_r, a_r, rz_r, dh2_r, dff_r, loss_r, dg_r):
        @pl.when(pl.program_id(0) == 0)
        def _():
            loss_r[...] = jnp.zeros_like(loss_r)
            dg_r[...] = jnp.zeros_like(dg_r)

        n2v = n2_r[...]
        ff = jnp.zeros((tm, D_MODEL), F32)
        for j in range(N_CHIPS):
            sl = slice(blk * j, blk * (j + 1))
            rz = jnp.maximum(_dot(n2v, w_r[j, 0]), 0.0)
            a = _mx(rz * rz)
            rz_r[:, sl] = rz.astype(rz_r.dtype)
            a_r[:, sl] = a
            ff = ff + _dot(a, w_r[j, 1])
        g = g_r[...]
        r = _rms_r(ff)
        err = h1_r[...] + ff * r * g - t_r[...]
        loss_r[...] += 0.5 * jnp.sum(err * err) / D_MODEL
        dh2 = err * (1.0 / D_MODEL)
        dh2_r[...] = dh2
        dff, dg = _rms_bwd(ff, r, g, dh2)
        dff_r[...] = dff.astype(dff_r.dtype)
        dg_r[...] += dg

    return pl.pallas_call(
        body, name="mlp_fwd", grid=(L // tm,),
        in_specs=[_row_spec(tm, D_MODEL), _row_spec(tm, D_MODEL), _row_spec(tm, D_MODEL),
                  _vmem_spec(), _full_spec((1, D_MODEL))],
        out_specs=[_row_spec(tm, D_FF), _row_spec(tm, D_FF), _row_spec(tm, D_MODEL), _row_spec(tm, D_MODEL),
                   _full_spec((1, 128)), _full_spec((1, D_MODEL))],
        out_shape=[jax.ShapeDtypeStruct((L, D_FF), MXU_DTYPE), jax.ShapeDtypeStruct((L, D_FF), MXU_DTYPE),
                   jax.ShapeDtypeStruct((L, D_MODEL), F32), jax.ShapeDtypeStruct((L, D_MODEL), MXU_DTYPE),
                   jax.ShapeDtypeStruct((1, 128), F32), jax.ShapeDtypeStruct((1, D_MODEL), F32)],
        compiler_params=_params(("arbitrary",), VMEM_BIG),
    )(n2, h1, tgt, w_ud, g_post)


def _mlp_bwd_call(dff, rz, w_ud):
    L = dff.shape[0]
    tm = min(512, L)
    blk = D_FF // N_CHIPS

    def body(dff_r, rz_r, w_r, dz_r, dn2_r):
        dffv = dff_r[...]
        dn2 = jnp.zeros((tm, D_MODEL), F32)
        for j in range(N_CHIPS):
            sl = slice(blk * j, blk * (j + 1))
            dz = _mx(_dot_nt(dffv, w_r[j, 1]) * 2.0 * rz_r[:, sl].astype(F32))
            dz_r[:, sl] = dz
            dn2 = dn2 + _dot_nt(dz, w_r[j, 0])
        dn2_r[...] = dn2

    return pl.pallas_call(
        body, name="mlp_bwd", grid=(L // tm,),
        in_specs=[_row_spec(tm, D_MODEL), _row_spec(tm, D_FF), _vmem_spec()],
        out_specs=[_row_spec(tm, D_FF), _row_spec(tm, D_MODEL)],
        out_shape=[jax.ShapeDtypeStruct((L, D_FF), MXU_DTYPE), jax.ShapeDtypeStruct((L, D_MODEL), F32)],
        compiler_params=_params(("arbitrary",), VMEM_BIG),
    )(dff, rz, w_ud)


def _mlp_wgrad_call(a, dff, n2, dz):
    L = a.shape[0]
    tf = 512
    per = (D_FF // N_CHIPS) // tf

    def body(a_r, dff_r, n2_r, dz_r, dwd_r, dwu_r):
        dwd_r[...] = _dot_tn(a_r[...], dff_r[...])
        dwu_r[...] = _dot_tn(n2_r[...], dz_r[...])

    return pl.pallas_call(
        body, name="mlp_wgrad", grid=(D_FF // tf,),
        in_specs=[pl.BlockSpec((L, tf), lambda j: (0, j)), _vmem_spec(), _vmem_spec(),
                  pl.BlockSpec((L, tf), lambda j: (0, j))],
        out_specs=[pl.BlockSpec((tf, D_MODEL), lambda j: (j, 0)),
                   pl.BlockSpec((None, D_MODEL, tf), lambda j: (j // per, 0, j % per))],
        out_shape=[jax.ShapeDtypeStruct((D_FF, D_MODEL), F32),
                   jax.ShapeDtypeStruct((N_CHIPS, D_MODEL, D_FF // N_CHIPS), F32)],
        compiler_params=_params(("arbitrary",), VMEM_BIG),
    )(a, dff, n2, dz)


def _mix_bwd_call(dn2, dh2, h1, mix, cat, o_f, o_b, ga, gn, g_post, g_pre2, w_out_p):
    L = dn2.shape[0]
    tm = min(512, L)
    hw = GLA_HEADS * HEAD_PAD

    def body(dn2_r, dh2_r, h1_r, mix_r, cat_r, of_r, ob_r, ga_r, gn_r, gp_r, g2_r, w_r,
             dh1_r, do_r, dga_r, dos_r, dw_r, dg2_r, dgp_r, dgn_r):
        @pl.when(pl.program_id(0) == 0)
        def _():
            for ref in (dw_r, dg2_r, dgp_r, dgn_r):
                ref[...] = jnp.zeros_like(ref)

        parts = [slice(start, start + min(256, tm)) for start in range(0, tm, 256)]
        dmix_m = []
        for rs in parts:
            h1 = h1_r[rs, :]
            dx2, dg2 = _rms_bwd(h1, _rms_r(h1), g2_r[...], dn2_r[rs, :])
            dh1 = dh2_r[rs, :] + dx2
            dh1_r[rs, :] = dh1
            dg2_r[...] += dg2
            mix = mix_r[rs, :]
            dmix, dgp = _rms_bwd(mix, _rms_r(mix), gp_r[...], dh1)
            dgp_r[...] += dgp
            dmix_m.append(_mx(dmix))
        dcat = [_dot_nt(d, w_r[...]) for d in dmix_m]
        for rs, d in zip(parts, dmix_m):
            dw_r[...] += _dot_tn(cat_r[rs, :], d)
        gn_v = gn_r[...]
        dgn = jnp.zeros((1, HEAD_PAD), F32)
        for rs, dc in zip(parts, dcat):
            dos_r[rs, :] = _spread_heads(dc[:, hw:]).astype(dos_r.dtype)
            for h in range(GLA_HEADS):
                sl = slice(HEAD_PAD * h, HEAD_PAD * (h + 1))
                oh = of_r[rs, sl] + ob_r[rs, sl]
                rr = _rms_r(oh)
                xh = oh * rr
                gate = ga_r[rs, sl]
                sg = jax.nn.sigmoid(gate)
                silu = gate * sg
                doa = dc[:, sl]
                dga_r[rs, sl] = (doa * (xh * gn_v) * (sg + silu * (1.0 - sg))).astype(dga_r.dtype)
                don = doa * silu
                gd = don * gn_v
                do_r[rs, sl] = rr * (gd - xh * jnp.mean(gd * xh, axis=-1, keepdims=True))
                dgn = dgn + jnp.sum(don * xh, axis=0, keepdims=True)
        dgn_r[...] += dgn

    return pl.pallas_call(
        body, name="mix_bwd", grid=(L // tm,),
        in_specs=[_row_spec(tm, D_MODEL)] * 4 + [_row_spec(tm, OUT_PAD)] + [_row_spec(tm, hw)] * 3
        + [_full_spec((1, HEAD_PAD)), _full_spec((1, D_MODEL)), _full_spec((1, D_MODEL)), _vmem_spec()],
        out_specs=[_row_spec(tm, D_MODEL), _row_spec(tm, hw), _row_spec(tm, hw),
                   _row_spec(tm, SWA_Q_HEADS * HEAD_PAD),
                   _full_spec((OUT_PAD, D_MODEL)), _full_spec((1, D_MODEL)), _full_spec((1, D_MODEL)),
                   _full_spec((1, HEAD_PAD))],
        out_shape=[jax.ShapeDtypeStruct((L, D_MODEL), F32), jax.ShapeDtypeStruct((L, hw), F32),
                   jax.ShapeDtypeStruct((L, hw), MXU_DTYPE),
                   jax.ShapeDtypeStruct((L, SWA_Q_HEADS * HEAD_PAD), MXU_DTYPE),
                   jax.ShapeDtypeStruct((OUT_PAD, D_MODEL), F32), jax.ShapeDtypeStruct((1, D_MODEL), F32),
                   jax.ShapeDtypeStruct((1, D_MODEL), F32), jax.ShapeDtypeStruct((1, HEAD_PAD), F32)],
        compiler_params=_params(("arbitrary",), VMEM_BIG),
    )(dn2, dh2, h1, mix, cat, o_f, o_b, ga, gn, g_post, g_pre2, w_out_p)


def _in_bwd_call(x, dh1, g_pre, w_in_t, pairs, singles, halos, dep=None):
    L = x.shape[0]
    tm = min(512, L)
    per = tm // SWA_BLOCK
    n_pair, n_single, n_halo = len(pairs), len(singles), len(halos)
    groups = [c for c, _ in pairs] + [c for c, _ in singles] + [c for c, _ in halos]

    def body(*refs):
        x_r, dh1_r, g_r, w_r = refs[:4]
        pair_refs = refs[4:4 + 2 * n_pair]
        single_refs = refs[4 + 2 * n_pair:4 + 2 * n_pair + n_single]
        halo_refs = refs[4 + 2 * n_pair + n_single:4 + 2 * n_pair + n_single + per * n_halo]
        dx_r, dw_r, dg_r = refs[4 + 2 * n_pair + n_single + per * n_halo:]

        @pl.when(pl.program_id(0) == 0)
        def _():
            dw_r[...] = jnp.zeros_like(dw_r)
            dg_r[...] = jnp.zeros_like(dg_r)

        xv = x_r[...]
        r = _rms_r(xv)
        g = g_r[...]
        u = _mx(xv * r * g)
        vals = [pair_refs[2 * i][...].astype(F32) + pair_refs[2 * i + 1][...].astype(F32) for i in range(n_pair)]
        vals += [ref[...].astype(F32) for ref in single_refs]
        vals += [jnp.concatenate([ref[...] for ref in halo_refs[per * i:per * (i + 1)]], axis=0)
                 for i in range(n_halo)]
        ds = [_mx(_squeeze_heads(val) if heads else val) for (_, _, heads), val in zip(groups, vals)]
        du = jnp.zeros((tm, D_MODEL), F32)
        for (first, rows, _), d in zip(groups, ds):
            du = du + _dot(d, w_r[first:first + rows, :])
        for (first, rows, _), d in zip(groups, ds):
            dw_r[first:first + rows, :] += _dot_tn(d, u)
        dx, dg = _rms_bwd(xv, r, g, du)
        dx_r[...] = dh1_r[...] + dx
        dg_r[...] += dg

    arrays = [a for _, pr in pairs for a in pr] + [a for _, a in singles]
    specs = [_row_spec(tm, a.shape[1]) for a in arrays]
    for _, a in halos:
        specs += [pl.BlockSpec((SWA_BLOCK, a.shape[1]), lambda i, j=j: (per * i + 1 + j, 0)) for j in range(per)]
        arrays += [a] * per
    body, extra, extra_specs = _after(body, 4 + len(arrays), dep)
    return pl.pallas_call(
        body, name="in_bwd", grid=(L // tm,),
        in_specs=[_row_spec(tm, D_MODEL), _row_spec(tm, D_MODEL), _full_spec((1, D_MODEL)), _vmem_spec()] + specs
        + extra_specs,
        out_specs=[_row_spec(tm, D_MODEL), _full_spec((IN_COLS, D_MODEL)), _full_spec((1, D_MODEL))],
        out_shape=[jax.ShapeDtypeStruct((L, D_MODEL), F32), jax.ShapeDtypeStruct((IN_COLS, D_MODEL), F32),
                   jax.ShapeDtypeStruct((1, D_MODEL), F32)],
        compiler_params=_params(("arbitrary",), VMEM_BIG),
    )(x, dh1, g_pre, w_in_t, *arrays, *extra)


def _adamw_math(w, g, m, v):
    m = ADAM_B1 * m + (1.0 - ADAM_B1) * g
    v = ADAM_B2 * v + (1.0 - ADAM_B2) * (g * g)
    m_hat = m / (1.0 - ADAM_B1 ** ADAM_STEP)
    v_hat = v / (1.0 - ADAM_B2 ** ADAM_STEP)
    delta = -ADAM_LR * (m_hat / (jnp.sqrt(v_hat) + ADAM_EPS) + ADAM_WD * w)
    return delta, m, v


def _adamw_call(w, g, m, v, name, dep=None):
    rows, cols = w.shape
    tr = min(256, rows)

    def body(w_r, g_r, m_r, v_r, d_r, nm_r, nv_r):
        d_r[...], nm_r[...], nv_r[...] = _adamw_math(w_r[...], g_r[...], m_r[...], v_r[...])

    if rows % tr == 0:
        spec, steps = _row_spec(tr, cols), rows // tr
    else:
        spec, steps = pl.BlockSpec((rows, 256), lambda i: (0, i)), cols // 256
    body, extra, extra_specs = _after(body, 4, dep)
    return pl.pallas_call(
        body, name=name, grid=(steps,),
        in_specs=[spec] * 4 + extra_specs, out_specs=[spec] * 3,
        out_shape=[jax.ShapeDtypeStruct(w.shape, F32)] * 3,
        compiler_params=_params(("arbitrary",)),
    )(w, g, m, v, *extra)


def _position():
    return lax.axis_index("x"), lax.axis_index("y"), lax.axis_index("c")


def _other_chips(x, y):
    return [(1 - x, y), (x, 1 - y), (1 - x, 1 - y)]


ROWS, COLS = -2, -1


def _half(ref, which, axis):
    size = ref.shape[axis] // 2
    span = pl.ds(pl.multiple_of(which * size, 16 if axis == ROWS else 128), size)
    index = [slice(None)] * len(ref.shape)
    index[axis] = span
    return ref.at[tuple(index)]


def _quarter(ref, half, which, axis):
    size = ref.shape[axis] // 4
    span = pl.ds(pl.multiple_of((2 * half + which) * size, 16 if axis == ROWS else 128), size)
    index = [slice(None)] * len(ref.shape)
    index[axis] = span
    return ref.at[tuple(index)]


def _first_gather_call(shards, axes, routed):
    n = len(shards)
    per = 7

    def body(*refs):
        srcs, outs = refs[:n], refs[n:2 * n]
        send_sems, recv_sems, local_sems = refs[2 * n:]
        x, y, c = _position()
        me, sibling = (x, y, c), (x, y, 1 - c)
        x_side, y_side, across = _other_chips(x, y)
        local = [pltpu.make_async_copy(srcs[a], outs[a].at[2 * x + y], local_sems.at[a]) for a in range(n)]
        for cp in local:
            cp.start()

        def copy(a, k, dst, to, src=None):
            return pltpu.make_async_remote_copy(
                src_ref=dst if src is None else src, dst_ref=dst, send_sem=send_sems.at[per * a + k],
                recv_sem=recv_sems.at[per * a + k], device_id=to, device_id_type=MESH_ID)

        def half(a, chip, pc):
            return _half(outs[a].at[2 * chip[0] + chip[1]], pc, axes[a])

        def quarter(a, chip, q):
            return _quarter(outs[a].at[2 * chip[0] + chip[1]], c, q, axes[a])

        sends = []
        for a in range(n):
            mine = _half(srcs[a], c, axes[a])
            targets = (x_side, y_side) if routed[a] else (x_side, y_side, across)
            sends += [copy(a, j, half(a, (x, y), c), (*chip, c), src=mine) for j, chip in enumerate(targets)]
        for cp in sends:
            cp.start()
        for a in range(n):
            for j, chip in enumerate((x_side, y_side)):
                copy(a, j, half(a, chip, c), me).wait_recv()
                if routed[a]:
                    other = (y_side, x_side)[j]
                    sends.append(copy(a, 2 + j, quarter(a, chip, j), (*other, c)))
                    sends[-1].start()
                sends.append(copy(a, 4 + j, half(a, chip, c), sibling))
                sends[-1].start()
        for a in range(n):
            if routed[a]:
                for j in range(2):
                    copy(a, 2 + j, quarter(a, across, j), me).wait_recv()
            else:
                copy(a, 2, half(a, across, c), me).wait_recv()
            sends.append(copy(a, 6, half(a, across, c), sibling))
            sends[-1].start()
        for a in range(n):
            for k, chip in ((4, x_side), (5, y_side), (6, across)):
                copy(a, k, half(a, chip, 1 - c), me).wait_recv()
        for cp in sends:
            cp.wait_send()
        for cp in local:
            cp.wait()

    return pl.pallas_call(
        body, name="first_gather",
        in_specs=[_any_spec()] * n, out_specs=[_any_spec()] * n,
        out_shape=[jax.ShapeDtypeStruct((N_CHIPS,) + s.shape, s.dtype) for s in shards],
        scratch_shapes=[pltpu.SemaphoreType.DMA((per * n,)), pltpu.SemaphoreType.DMA((per * n,)),
                        pltpu.SemaphoreType.DMA((n,))],
    )(*shards)


def _split_start(name, arrays, n_copies, plan):
    n = len(arrays)

    def body(*refs):
        ins, send_sems, recv_sems, token = refs[:n], refs[n], refs[n + 1], refs[-1]
        for k, (src, dst, to, _) in enumerate(plan(ins)):
            pltpu.make_async_remote_copy(src_ref=src, dst_ref=dst, send_sem=send_sems.at[k],
                                         recv_sem=recv_sems.at[k], device_id=to, device_id_type=MESH_ID).start()
        token[...] = jnp.zeros_like(token)

    hbm = pl.BlockSpec(memory_space=pltpu.HBM)
    sem = pl.BlockSpec(memory_space=pltpu.SEMAPHORE)
    out = pl.pallas_call(
        body, name=name,
        out_shape=(pltpu.SemaphoreType.DMA((n_copies,)), pltpu.SemaphoreType.DMA((n_copies,)))
        + tuple(pltpu.HBM(a.shape, a.dtype) for a in arrays) + (jax.ShapeDtypeStruct((8, 128), F32),),
        in_specs=[hbm] * n, out_specs=(sem, sem) + (hbm,) * n + (_vmem_spec(),),
        input_output_aliases={i: 2 + i for i in range(n)},
        compiler_params=pltpu.CompilerParams(has_side_effects=pltpu.SideEffectType.DATAFLOW_SIDE_EFFECTING),
    )(*[pltpu.with_memory_space_constraint(a, pltpu.HBM) for a in arrays])
    return (out[0], out[1], tuple(out[2:2 + n])), out[-1]


def _split_wait(name, handle, n_copies, plan, after):
    send_sems, recv_sems, arrays = handle
    n = len(arrays)

    def body(*refs):
        ins, s_sems, r_sems = refs[:n], refs[n], refs[n + 1]
        for k, (src, dst, to, landed) in enumerate(plan(ins)):
            cp = pltpu.make_async_remote_copy(src_ref=src, dst_ref=landed, send_sem=s_sems.at[k],
                                              recv_sem=r_sems.at[k], device_id=to, device_id_type=MESH_ID)
            cp.wait_send()
            cp.wait_recv()

    hbm = pl.BlockSpec(memory_space=pltpu.HBM)
    sem = pl.BlockSpec(memory_space=pltpu.SEMAPHORE)
    out = pl.pallas_call(
        body, name=name,
        out_shape=tuple(pltpu.HBM(a.shape, a.dtype) for a in arrays),
        in_specs=[hbm] * n + [sem, sem, _any_spec()], out_specs=(hbm,) * n,
        input_output_aliases={i: i for i in range(n)},
        compiler_params=pltpu.CompilerParams(has_side_effects=pltpu.SideEffectType.DATAFLOW_SIDE_EFFECTING),
    )(*arrays, send_sems, recv_sems, after)
    return tuple(out)


def _gather_plans(axes):
    n = len(axes)

    def stage_one(refs):
        x, y, c = _position()
        copies = []
        for a, axis in enumerate(axes):
            for px, py in _other_chips(x, y):
                copies.append((_half(refs[a], c, axis), _half(refs[n + a].at[2 * x + y], c, axis),
                               (px, py, c), _half(refs[n + a].at[2 * px + py], c, axis)))
        return copies

    def stage_two(refs):
        x, y, c = _position()
        copies = []
        for a, axis in enumerate(axes):
            for px, py in _other_chips(x, y):
                piece = _half(refs[n + a].at[2 * px + py], c, axis)
                copies.append((piece, piece, (x, y, 1 - c), _half(refs[n + a].at[2 * px + py], 1 - c, axis)))
        return copies

    return stage_one, stage_two


def _pair_swap_plan(axes):
    n = len(axes)

    def plan(refs):
        x, y, c = _position()
        return [(_half(refs[a], 1 - c, axes[a]), refs[n + a], (x, y, 1 - c), refs[n + a]) for a in range(n)]

    return plan


def _chip_swap_plan(n):
    def plan(refs):
        x, y, c = _position()
        copies = []
        for a in range(n):
            for j, (px, py) in enumerate(_other_chips(x, y)):
                copies.append((refs[a].at[2 * px + py], refs[n + a].at[j], (px, py, c), refs[n + a].at[j]))
        return copies

    return plan


def _pair_join_plan(axes):
    def plan(refs):
        x, y, c = _position()
        copies = []
        for a, axis in enumerate(axes):
            mine = _half(refs[a], c, axis)
            copies.append((mine, mine, (x, y, 1 - c), _half(refs[a], 1 - c, axis)))
        return copies

    return plan


def _pair_add_call(g, got, pos, name, axis):
    rows, cols = got.shape[1], got.shape[2]
    tr = min(512, rows) if axis == ROWS else rows
    nblk = rows // tr
    if axis == ROWS:
        mine = lambda j, i, p: (j, p[1] * nblk + i, 0)
    else:
        mine = lambda j, i, p: (j, 0, p[1])

    def body(pos_r, g_r, got_r, o_r):
        o_r[...] = (g_r[...] + got_r[...]).astype(o_r.dtype)

    return pl.pallas_call(
        body, name=name,
        grid_spec=pltpu.PrefetchScalarGridSpec(
            num_scalar_prefetch=1, grid=(N_CHIPS, nblk),
            in_specs=[pl.BlockSpec((None, tr, cols), mine),
                      pl.BlockSpec((None, tr, cols), lambda j, i, p: (j, i, 0))],
            out_specs=pl.BlockSpec((None, tr, cols), lambda j, i, p: (j, i, 0))),
        out_shape=jax.ShapeDtypeStruct(got.shape, COMM_DTYPE),
        compiler_params=_params(("arbitrary", "arbitrary"), VMEM_BIG),
    )(pos, g, got)


def _chip_add_call(hsum, got, pos, name, axis):
    rows, cols = hsum.shape[1], hsum.shape[2]
    tr = min(512, rows) if axis == ROWS else rows
    nblk = rows // tr
    if axis == ROWS:
        out_shape, mine = (2 * rows, cols), (lambda i, p: (p[1] * nblk + i, 0))
    else:
        out_shape, mine = (rows, 2 * cols), (lambda i, p: (0, p[1]))

    def body(pos_r, own_r, got_r, o_r):
        acc = own_r[...].astype(F32)
        for j in range(3):
            acc = acc + got_r[j].astype(F32)
        o_r[...] = acc

    return pl.pallas_call(
        body, name=name,
        grid_spec=pltpu.PrefetchScalarGridSpec(
            num_scalar_prefetch=1, grid=(nblk,),
            in_specs=[pl.BlockSpec((None, tr, cols), lambda i, p: (p[0], i, 0)),
                      pl.BlockSpec((3, tr, cols), lambda i, p: (0, i, 0))],
            out_specs=pl.BlockSpec((tr, cols), mine)),
        out_shape=jax.ShapeDtypeStruct(out_shape, F32),
        compiler_params=_params(("arbitrary",), VMEM_BIG),
    )(pos, hsum, got)


SMALL_NAMES = ("norm_mix_pre", "norm_mix_post", "norm_mlp_pre", "norm_mlp_post", "b_gate_fwd", "b_gate_bwd",
               "gla_norm", "swa_sink", "rel_bias")


def _small_update_call(grads, gate_grads, params, dep=None):
    n_dev = 8
    n_small = len(SMALL_NAMES)
    wmv = [t for p in params for t in p]
    shapes = [p[0].shape for p in params]

    def body(*refs):
        g_refs = refs[:n_small + 3]
        wmv_refs = refs[n_small + 3:n_small + 3 + 3 * n_small]
        n_in = n_small + 3 + 3 * n_small
        out_refs = refs[n_in:n_in + 4 * n_small + 3]
        pack_a, pack_b, all_a, all_b, send_sems, recv_sems = refs[n_in + 4 * n_small + 3:]
        x, y, c = _position()
        me = 4 * x + 2 * y + c
        pack_a[...] = jnp.zeros_like(pack_a)
        pack_b[...] = jnp.zeros_like(pack_b)
        for i in range(4):
            pack_a[i:i + 1, :] = g_refs[i][...]
        pack_a[4:5, 0:256] = g_refs[4][...]
        pack_a[5:6, 0:256] = g_refs[5][...]
        pack_a[6:7, 0:128] = g_refs[6][...]
        pack_a[7:8, 0:128] = g_refs[7][...]
        pack_a[7:8, 128:256] = g_refs[11][...]
        pack_b[0:32, 0:128] = g_refs[8][...]
        pack_b[32:48, :] = g_refs[9][...]
        pack_b[48:64, :] = g_refs[10][...]
        all_a[me] = pack_a[...]
        all_b[me] = pack_b[...]
        copies = []
        for k in range(1, n_dev):
            fx, fy, fc = (k >> 2) & 1, (k >> 1) & 1, k & 1
            to = (1 - x if fx else x, 1 - y if fy else y, 1 - c if fc else c)
            for t, (pack, dst) in enumerate(((pack_a, all_a), (pack_b, all_b))):
                copies.append(pltpu.make_async_remote_copy(
                    src_ref=pack, dst_ref=dst.at[me], send_sem=send_sems.at[2 * (k - 1) + t],
                    recv_sem=recv_sems.at[2 * (k - 1) + t], device_id=to, device_id_type=MESH_ID))
        for cp in copies:
            cp.start()
        for cp in copies:
            cp.wait()
        sum_a, sum_b = all_a[0], all_b[0]
        for d in range(1, n_dev):
            sum_a = sum_a + all_a[d]
            sum_b = sum_b + all_b[d]
        gsum = [sum_a[0:1], sum_a[1:2], sum_a[2:3], sum_a[3:4], sum_a[4:5, 0:256], sum_a[5:6, 0:256],
                sum_a[6:7, 0:128], sum_a[7:8, 0:SWA_Q_HEADS], sum_b[0:32, 0:SWA_Q_HEADS]]
        for i in range(n_small):
            w_r, m_r, v_r = wmv_refs[3 * i:3 * i + 3]
            delta, new_m, new_v = _adamw_math(w_r[...], gsum[i], m_r[...], v_r[...])
            out_refs[4 * i][...] = gsum[i]
            out_refs[4 * i + 1][...] = delta
            out_refs[4 * i + 2][...] = new_m
            out_refs[4 * i + 3][...] = new_v
        out_refs[4 * n_small][...] = sum_b[32:48]
        out_refs[4 * n_small + 1][...] = sum_b[48:64]
        out_refs[4 * n_small + 2][...] = sum_a[7:8, 128:256]

    n_in = n_small + 3 + 3 * n_small
    body, extra, extra_specs = _after(body, n_in, dep)
    out_shape = [jax.ShapeDtypeStruct(s, F32) for s in shapes for _ in range(4)]
    out_shape += [jax.ShapeDtypeStruct((GLA_GATE_RANK, 256), F32)] * 2 + [jax.ShapeDtypeStruct((1, 128), F32)]
    out = pl.pallas_call(
        body, name="small_update",
        in_specs=[_whole_spec(a.shape) for a in list(grads) + list(gate_grads) + wmv] + extra_specs,
        out_specs=[_whole_spec(s.shape) for s in out_shape],
        out_shape=out_shape,
        scratch_shapes=[pltpu.VMEM((8, D_MODEL), F32), pltpu.VMEM((64, 256), F32),
                        pltpu.VMEM((n_dev, 8, D_MODEL), F32), pltpu.VMEM((n_dev, 64, 256), F32),
                        pltpu.SemaphoreType.DMA((2 * (n_dev - 1),)), pltpu.SemaphoreType.DMA((2 * (n_dev - 1),))],
    )(*grads, *gate_grads, *wmv, *extra)
    per_name = [tuple(out[4 * i:4 * i + 4]) for i in range(n_small)]
    return per_name, out[4 * n_small], out[4 * n_small + 1], out[4 * n_small + 2]


def _pad_heads(t, n_heads, axis=-1):
    axis = axis % t.ndim
    shape = t.shape
    t = t.reshape(shape[:axis] + (n_heads, 64) + shape[axis + 1:])
    pad = [(0, 0)] * t.ndim
    pad[axis + 1] = (0, HEAD_PAD - 64)
    return jnp.pad(t, pad).reshape(shape[:axis] + (n_heads * HEAD_PAD,) + shape[axis + 1:])


def _unpad_heads(t, n_heads, axis=-1):
    axis = axis % t.ndim
    shape = t.shape
    t = t.reshape(shape[:axis] + (n_heads, HEAD_PAD) + shape[axis + 1:])
    t = lax.slice_in_dim(t, 0, 64, axis=axis + 1)
    return t.reshape(shape[:axis] + (n_heads * 64,) + shape[axis + 1:])


def _pad_gate(w, first_row):
    return jnp.pad(_pad_heads(w, 4), ((first_row, 128 - GLA_GATE_RANK - first_row), (0, 0)))


def _own_slot(shard, chip):
    zone = lax.empty((N_CHIPS,) + shard.shape, shard.dtype)
    return lax.dynamic_update_slice(zone, shard[None], (chip,) + (0,) * shard.ndim)


def _reduce_to_owners(grads, axes, pos, tag, overlap):
    n = len(grads)

    def half_shape(g, axis):
        return (N_CHIPS, g.shape[1] // 2, g.shape[2]) if axis == ROWS else (N_CHIPS, g.shape[1], g.shape[2] // 2)

    lands = [lax.empty(half_shape(g, axis), F32) for g, axis in zip(grads, axes)]
    handle, token = _split_start(tag + "_pair_start", list(grads) + lands, n, _pair_swap_plan(axes))
    got = _split_wait(tag + "_pair_wait", handle, n, _pair_swap_plan(axes), overlap[0](token))
    sums = [_pair_add_call(got[a], got[n + a], pos, f"{tag}_pair_add{a}", axes[a]) for a in range(n)]
    lands = [lax.empty((3,) + s.shape[1:], s.dtype) for s in sums]
    handle, token = _split_start(tag + "_chip_start", sums + lands, 3 * n, _chip_swap_plan(n))
    got = _split_wait(tag + "_chip_wait", handle, 3 * n, _chip_swap_plan(n), overlap[1](token))
    halves = [_chip_add_call(got[a], got[n + a], pos, f"{tag}_chip_add{a}", axes[a]) for a in range(n)]
    handle, token = _split_start(tag + "_join_start", halves, n, _pair_join_plan(axes))
    return _split_wait(tag + "_join_wait", handle, n, _pair_join_plan(axes), overlap[2](token))


def kernel(x, norm_mix_pre, w_in, w_gate_up_fwd, b_gate_fwd, w_gate_up_bwd, b_gate_bwd, gla_norm, swa_sink, rel_bias, w_out, norm_mix_post, norm_mlp_pre, w_up, w_down, norm_mlp_post, loss_target, m_norm_mix_pre, m_w_in, m_w_gate_up_fwd, m_b_gate_fwd, m_w_gate_up_bwd, m_b_gate_bwd, m_gla_norm, m_swa_sink, m_rel_bias, m_w_out, m_norm_mix_post, m_norm_mlp_pre, m_w_up, m_w_down, m_norm_mlp_post, v_norm_mix_pre, v_w_in, v_w_gate_up_fwd, v_b_gate_fwd, v_w_gate_up_bwd, v_b_gate_bwd, v_gla_norm, v_swa_sink, v_rel_bias, v_w_out, v_norm_mix_post, v_norm_mlp_pre, v_w_up, v_w_down, v_norm_mlp_post):
    given = dict(locals())
    cx, cy, cc = _position()
    chip = (2 * cx + cy).astype(jnp.int32)
    pos = jnp.stack([chip, cc.astype(jnp.int32)])
    seq, tgt = x[0], loss_target[0]
    L = seq.shape[0]

    gates = jnp.concatenate([w_gate_up_fwd[0], w_gate_up_bwd[0]], axis=0).astype(COMM_DTYPE)
    all_in, all_gates = _first_gather_call([w_in[0].T.astype(COMM_DTYPE), gates], [COLS, ROWS], [True, False])
    rest = [w_out[0].astype(COMM_DTYPE), jnp.stack([w_up[0], w_down[0]]).astype(COMM_DTYPE)]
    stage_one, stage_two = _gather_plans([ROWS, ROWS])
    handle, token = _split_start("gather_chip_start", rest + [_own_slot(s, chip) for s in rest] + [all_gates], 6,
                                 stage_one)

    w_in_t = _mx(all_in.reshape(IN_COLS, D_MODEL))
    gates_full = jnp.concatenate([all_gates[j] for j in range(N_CHIPS)], axis=1)
    wgf_p = _mx(_pad_gate(gates_full[:GLA_GATE_RANK], 0))
    wgb_p = _mx(_pad_gate(gates_full[GLA_GATE_RANK:], GLA_GATE_RANK))
    bf_p, bb_p = _pad_heads(b_gate_fwd, 4), _pad_heads(b_gate_bwd, 4)
    buckets = jnp.asarray(_band_buckets())
    bias = _bias_call(rel_bias, buckets)
    sink1 = swa_sink.reshape(SWA_Q_HEADS)

    qa, ka, va, ga, qs, ks, vs, za = _proj_call(seq, norm_mix_pre, w_in_t, dep=token)
    halo = ((SWA_BLOCK, SWA_BLOCK), (0, 0))
    ks_p, vs_p = jnp.pad(ks, halo), jnp.pad(vs, halo)
    o_f, o_b, s_f, s_b = _gla_fwd_call(qa, ka, va, za, wgf_p, bf_p, wgb_p, bb_p)
    arrays = _split_wait("gather_chip_wait", handle, 6, stage_one, o_f)
    handle, token = _split_start("gather_pair_start", list(arrays), 6, stage_two)
    o_s = _swa_fwd_call(qs, ks_p, vs_p, bias, sink1, dep=token)
    arrays = _split_wait("gather_pair_wait", handle, 6, stage_two, o_s)
    w_out_full = _mx(arrays[2].reshape(N_CHIPS * R_OUT, D_MODEL))
    w_ud = _mx(arrays[3])
    cat, mix, h1, n2 = _mix_call(o_f, o_b, ga, o_s, seq, gla_norm, w_out_full, norm_mix_post, norm_mlp_pre)
    a, rz, dh2, dff, loss, d_post2 = _mlp_fwd_call(n2, h1, tgt, w_ud, norm_mlp_post)

    dz, dn2 = _mlp_bwd_call(dff, rz, w_ud)
    dw_down, dw_up4 = _mlp_wgrad_call(a, dff, n2, dz)
    dh1, do, dga, dos, dw_out, d_pre2, d_post, d_gn = _mix_bwd_call(
        dn2, dh2, h1, mix, cat, o_f, o_b, ga, gla_norm, norm_mix_post, norm_mlp_pre, w_out_full)
    done = {}

    def swa_backward(tok):
        done["swa"] = _swa_bwd_call(qs, ks_p, vs_p, bias, sink1, dos, dep=tok)
        return done["swa"][0]

    def gla_in_backward(tok):
        done["gla"] = _gla_bwd_call(qa, ka, va, za, do, s_f, s_b, wgf_p, bf_p, wgb_p, bb_p, dep=tok)
        dqf, dkf, dvf, dzf, _, _, dqb, dkb, dvb, dzb, _, _ = done["gla"]
        dqs, dks_p, dvs_p, _, _ = done["swa"]
        done["in"] = _in_bwd_call(
            seq, dh1, norm_mix_pre, w_in_t,
            pairs=[(_side_by_side(T_QA), (dqf, dqb)), (_side_by_side(T_KA), (dkf, dkb)), (T_VA, (dvf, dvb)),
                   (T_ZA, (dzf, dzb))],
            singles=[(T_GA, dga), (_side_by_side(T_QS), dqs)], halos=[(T_KS, dks_p), (T_VS, dvs_p)])
        return done["in"][0]

    def bias_backward(tok):
        done["rel"] = _relbias_call(done["swa"][3], done["swa"][4], buckets, dep=tok)
        return done["rel"][0]

    g_up, g_down, g_out = _reduce_to_owners(
        [dw_up4, dw_down.reshape(N_CHIPS, R_DOWN, D_MODEL), dw_out.reshape(N_CHIPS, R_OUT, D_MODEL)],
        [ROWS, ROWS, ROWS], pos, "mlp", [swa_backward, gla_in_backward, bias_backward])
    dx, dw_in_t, d_pre = done["in"]
    dwf, dbf, dwb, dbb = done["gla"][4], done["gla"][5], done["gla"][10], done["gla"][11]
    drel, dsink = done["rel"]

    small_grads = [d_pre, d_post, d_pre2, d_post2, _unpad_heads(dbf, 4), _unpad_heads(dbb, 4), d_gn, dsink, drel]
    gate_grads = [_unpad_heads(dwf[:GLA_GATE_RANK], 4), _unpad_heads(dwb[GLA_GATE_RANK:2 * GLA_GATE_RANK], 4)]
    small_params = [(given[n], given["m_" + n], given["v_" + n]) for n in SMALL_NAMES]
    upd = {}

    def update_up(tok):
        upd["w_up"] = (g_up,) + tuple(_adamw_call(w_up[0], g_up, m_w_up[0], v_w_up[0], "adamw_w_up", dep=tok))
        return upd["w_up"][1]

    def update_small(tok):
        per_name, gf_sum, gb_sum, upd["loss"] = _small_update_call(small_grads, gate_grads + [loss], small_params,
                                                                   dep=tok)
        upd.update(dict(zip(SMALL_NAMES, per_name)))
        for name, total in (("w_gate_up_fwd", gf_sum), ("w_gate_up_bwd", gb_sum)):
            g = lax.dynamic_slice(total, (0, chip * 64), (GLA_GATE_RANK, 64))
            upd[name] = (g,) + tuple(_adamw_call(given[name][0], g, given["m_" + name][0], given["v_" + name][0],
                                                 "adamw_" + name))
        upd["w_down"] = (g_down,) + tuple(
            _adamw_call(w_down[0], g_down, m_w_down[0], v_w_down[0], "adamw_w_down", dep=gf_sum))
        return upd["w_down"][1]

    def update_out(tok):
        upd["w_out"] = (g_out,) + tuple(_adamw_call(w_out[0], g_out, m_w_out[0], v_w_out[0], "adamw_w_out", dep=tok))
        return upd["w_out"][1]

    (g_in_t,) = _reduce_to_owners([dw_in_t.reshape(N_CHIPS, R_IN, D_MODEL)], [COLS], pos, "in",
                                  [update_up, update_small, update_out])
    in_t = (g_in_t,) + tuple(_adamw_call(w_in[0].T, g_in_t, m_w_in[0].T, v_w_in[0].T, "adamw_w_in"))
    upd["w_in"] = tuple(t.T for t in in_t)

    big = ("w_in", "w_gate_up_fwd", "w_gate_up_bwd", "w_out", "w_up", "w_down")
    names = ["norm_mix_pre", "w_in", "w_gate_up_fwd", "b_gate_fwd", "w_gate_up_bwd", "b_gate_bwd", "gla_norm",
             "swa_sink", "rel_bias", "w_out", "norm_mix_post", "norm_mlp_pre", "w_up", "w_down", "norm_mlp_post"]
    outs = [upd["loss"][0, 0], dx[None]]
    for kind in range(4):
        outs += [upd[n][kind][None] if n in big else upd[n][kind] for n in names]
    return tuple(outs)
```

```python
import math

import numpy as np
import jax
import jax.numpy as jnp
from jax import lax
from jax.experimental import pallas as pl
from jax.experimental.pallas import tpu as pltpu

F32 = jnp.float32
MXU_DTYPE = jnp.bfloat16
COMM_DTYPE = jnp.bfloat16

D_MODEL = 1024
D_FF = 4096
N_CHIPS = 4
GLA_HEADS = 4
GLA_CHUNK = 64
GLA_GATE_RANK = 16
GLA_GATE_NORM = 16.0
SWA_Q_HEADS = 8
SWA_KV_HEADS = 2
SWA_BLOCK = 128
REL_BUCKETS = 32
REL_MAX_DIST = 128
NORM_EPS = 1e-6
HEAD_PAD = 128

ADAM_LR = 0.001
ADAM_B1 = 0.9
ADAM_B2 = 0.999
ADAM_EPS = 1e-08
ADAM_WD = 0.01
ADAM_STEP = 10

OUT_PAD = 1024

R_IN, R_OUT, R_UP, R_DOWN = 584, 256, 1024, 1024

VMEM_BIG = 56 * 1024 * 1024
MESH_AXES = ("x", "y", "c")
MESH_ID = pl.DeviceIdType.MESH


def _mx(a):
    return a.astype(MXU_DTYPE)


def _dot(a, b):
    return jnp.dot(a, b, preferred_element_type=F32)


def _dot_nt(a, b):
    return lax.dot_general(a, b, (((1,), (1,)), ((), ())), preferred_element_type=F32)


def _dot_tn(a, b):
    return lax.dot_general(a, b, (((0,), (0,)), ((), ())), preferred_element_type=F32)


def _rms_r(x):
    return lax.rsqrt(jnp.mean(x * x, axis=-1, keepdims=True) + NORM_EPS)


def _rms_bwd(x, r, g, dy):
    xh = x * r
    gdy = dy * g
    dx = r * (gdy - xh * jnp.mean(gdy * xh, axis=-1, keepdims=True))
    return dx, jnp.sum(dy * xh, axis=0, keepdims=True)


def _low_half(rows):
    return lax.broadcasted_iota(jnp.int32, (rows, HEAD_PAD), 1) < 64


def _spread_heads(x):
    low = _low_half(x.shape[0])
    parts = []
    for p in range(x.shape[1] // HEAD_PAD):
        pair = x[:, HEAD_PAD * p:HEAD_PAD * (p + 1)]
        parts += [jnp.where(low, pair, 0.0), jnp.where(low, pltpu.roll(pair, 64, 1), 0.0)]
    return jnp.concatenate(parts, axis=1)


def _squeeze_heads(x):
    low = _low_half(x.shape[0])
    parts = []
    for p in range(x.shape[1] // (2 * HEAD_PAD)):
        even = x[:, 2 * HEAD_PAD * p:2 * HEAD_PAD * p + HEAD_PAD]
        odd = x[:, 2 * HEAD_PAD * p + HEAD_PAD:2 * HEAD_PAD * (p + 1)]
        parts.append(jnp.where(low, even, pltpu.roll(odd, 64, 1)))
    return parts[0] if len(parts) == 1 else jnp.concatenate(parts, axis=1)


def _params(sem=None, vmem=None):
    kw = {}
    if sem is not None:
        kw["dimension_semantics"] = sem
    if vmem is not None:
        kw["vmem_limit_bytes"] = vmem
    return pltpu.CompilerParams(**kw)


def _vmem_spec():
    return pl.BlockSpec(memory_space=pltpu.VMEM)


def _whole_spec(shape):
    return pl.BlockSpec(shape, lambda: (0,) * len(shape))


def _row_spec(tm, width):
    return pl.BlockSpec((tm, width), lambda i: (i, 0))


def _full_spec(shape):
    return pl.BlockSpec(shape, lambda i: (0,) * len(shape))


def _any_spec():
    return pl.BlockSpec(memory_space=pl.ANY)


def _after(body, n_in, dep):
    if dep is None:
        return body, [], []
    return (lambda *refs: body(*refs[:n_in], *refs[n_in + 1:])), [dep], [_any_spec()]


T_QA, T_KA, T_VA, T_GA = (0, 256, 4), (256, 256, 4), (512, 512, 0), (1024, 512, 0)
T_QS, T_KS, T_VS = (1568, 512, 8), (2080, 128, 2), (2208, 128, 2)
T_ZA = (1536, 128, 0)
ZA_COLS = 2 * GLA_GATE_RANK
IN_COLS = 2336


def _side_by_side(group):
    return group[0], group[1], 0


def _proj_call(x, g_pre, w_in_t, dep=None):
    L = x.shape[0]
    tm = min(512, L)
    groups = [(T_QA, F32), (T_KA, F32), (T_VA, MXU_DTYPE), (T_GA, F32),
              (T_QS, MXU_DTYPE), (T_KS, MXU_DTYPE), (T_VS, MXU_DTYPE), (T_ZA, F32)]
    widths = [rows * (2 if heads else 1) for (_, rows, heads), _ in groups]

    def body(x_ref, g_ref, w_ref, *outs):
        xv = x_ref[...]
        u = _mx(xv * _rms_r(xv) * g_ref[...])
        for ref, (grp, _) in zip(outs, groups):
            first, rows, heads = grp
            val = _dot_nt(u, w_ref[first:first + rows, :])
            if heads:
                val = _spread_heads(val)
            if grp is T_ZA:
                val = jnp.where(lax.broadcasted_iota(jnp.int32, val.shape, 1) < ZA_COLS, val, 0.0)
            if grp is T_QS:
                val = val * 0.125
            ref[...] = val.astype(ref.dtype)

    body, extra, extra_specs = _after(body, 3, dep)
    return pl.pallas_call(
        body, name="proj_fwd", grid=(L // tm,),
        in_specs=[_row_spec(tm, D_MODEL), _full_spec((1, D_MODEL)), _vmem_spec()] + extra_specs,
        out_specs=[_row_spec(tm, w) for w in widths],
        out_shape=[jax.ShapeDtypeStruct((L, w), dt) for w, (_, dt) in zip(widths, groups)],
        compiler_params=_params(("arbitrary",), VMEM_BIG),
    )(x, g_pre, w_in_t, *extra)


def _tri_masks():
    row = lax.broadcasted_iota(jnp.int32, (GLA_CHUNK, GLA_CHUNK), 0)
    col = lax.broadcasted_iota(jnp.int32, (GLA_CHUNK, GLA_CHUNK), 1)
    return row >= col, row <= col


def _chunk_sums(tri_m, x):
    hi = _mx(x)
    rest = x - hi.astype(F32)
    mid = _mx(rest)
    lo = _mx(rest - mid.astype(F32))
    return _dot(tri_m, hi) + _dot(tri_m, mid) + _dot(tri_m, lo)


def _gla_block_pre(q_r, k_r, z_r, w_r, b_r, rev, nc, qd_s, ki_s, ks_s, dec_s, keep=None):
    tri_f, tri_b = _tri_masks()
    tri_m = _mx((tri_b if rev else tri_f).astype(F32))
    g = _dot(_mx(z_r[...]), w_r[...]) + b_r[...]
    la = (jnp.minimum(g, 0.0) - jnp.log(1.0 + jnp.exp(-jnp.abs(g)))) / GLA_GATE_NORM
    sums, lasts = [], []
    for c in range(nc):
        b_c = _chunk_sums(tri_m, la[GLA_CHUNK * c:GLA_CHUNK * (c + 1)])
        blast = b_c[0:1] if rev else b_c[GLA_CHUNK - 1:GLA_CHUNK]
        dec_s[c] = jnp.exp(blast)
        sums.append(b_c)
        lasts.append(jnp.broadcast_to(blast, b_c.shape))
    b = jnp.concatenate(sums, axis=0)
    eb = jnp.exp(b)
    enb = jnp.exp(-b)
    elb = jnp.exp(jnp.concatenate(lasts, axis=0) - b)
    k = k_r[...]
    qd_s[...] = (q_r[...] * 0.125 * eb).astype(qd_s.dtype)
    ki_s[...] = (k * enb).astype(ki_s.dtype)
    ks_s[...] = (k * elb).astype(ks_s.dtype)
    if keep is not None:
        for ref, val in zip(keep, (g, eb, enb, elb)):
            ref[...] = val


def _gla_fwd_call(qa, ka, va, za, wgf, bgf, wgb, bgb):
    L = qa.shape[0]
    br = min(512, L)
    nb, nc, n_chunks = L // br, br // GLA_CHUNK, L // GLA_CHUNK
    hw = GLA_HEADS * HEAD_PAD

    def body(qaf, kaf, vaf, zaf, qab, kab, vab, zab, wgf_r, bgf_r, wgb_r, bgb_r,
             of_r, ob_r, sf_r, sb_r, st_f, st_b, pre_f, pre_b):
        @pl.when(pl.program_id(0) == 0)
        def _():
            st_f[...] = jnp.zeros_like(st_f)
            st_b[...] = jnp.zeros_like(st_b)

        _gla_block_pre(qaf, kaf, zaf, wgf_r, bgf_r, False, nc, *pre_f)
        _gla_block_pre(qab, kab, zab, wgb_r, bgb_r, True, nc, *pre_b)
        tri_f, tri_b = _tri_masks()

        def one(tri, pre, v_r, o_r, s_r, st, ci):
            qd_s, ki_s, ks_s, dec_s = pre
            rows = pl.ds(pl.multiple_of(ci * GLA_CHUNK, GLA_CHUNK), GLA_CHUNK)
            dec = dec_s[ci]
            heads = range(GLA_HEADS)
            lanes = [slice(HEAD_PAD * h, HEAD_PAD * (h + 1)) for h in heads]
            qd = [qd_s[rows, sl] for sl in lanes]
            v = [v_r[rows, sl] for sl in lanes]
            s_t = [st[h] for h in heads]
            a = [_dot_nt(qd[h], ki_s[rows, lanes[h]]) for h in heads]
            carried = [_dot_nt(qd[h], _mx(s_t[h])) for h in heads]
            grown = [_dot_tn(v[h], ks_s[rows, lanes[h]]) for h in heads]
            a = [_mx(jnp.where(tri, a[h], 0.0)) for h in heads]
            inner = [_dot(a[h], v[h]) for h in heads]
            for h in heads:
                s_r[ci, h] = s_t[h].astype(s_r.dtype)
                o_r[rows, lanes[h]] = inner[h] + carried[h]
                st[h] = s_t[h] * dec[:, lanes[h]] + grown[h]

        def loop(t, carry):
            one(tri_f, pre_f, vaf, of_r, sf_r, st_f, t)
            one(tri_b, pre_b, vab, ob_r, sb_r, st_b, nc - 1 - t)
            return carry

        lax.fori_loop(0, nc, loop, 0, unroll=True)

    fwd = lambda i: (i, 0)
    bwd = lambda i: (nb - 1 - i, 0)
    ins = lambda m: [pl.BlockSpec((br, hw), m), pl.BlockSpec((br, hw), m),
                     pl.BlockSpec((br, hw), m), pl.BlockSpec((br, 128), m)]
    wspecs = [_full_spec((128, hw)), _full_spec((1, hw))] * 2
    s_shape = (nc, GLA_HEADS, HEAD_PAD, HEAD_PAD)
    pre_scratch = [pltpu.VMEM((br, hw), MXU_DTYPE)] * 3 + [pltpu.VMEM((nc, 1, hw), F32)]
    return pl.pallas_call(
        body, name="gla_fwd", grid=(nb,),
        in_specs=ins(fwd) + ins(bwd) + wspecs,
        out_specs=[pl.BlockSpec((br, hw), fwd), pl.BlockSpec((br, hw), bwd),
                   pl.BlockSpec(s_shape, lambda i: (i, 0, 0, 0)),
                   pl.BlockSpec(s_shape, lambda i: (nb - 1 - i, 0, 0, 0))],
        out_shape=[jax.ShapeDtypeStruct((L, hw), F32), jax.ShapeDtypeStruct((L, hw), F32),
                   jax.ShapeDtypeStruct((n_chunks,) + s_shape[1:], MXU_DTYPE),
                   jax.ShapeDtypeStruct((n_chunks,) + s_shape[1:], MXU_DTYPE)],
        scratch_shapes=[pltpu.VMEM(s_shape[1:], F32), pltpu.VMEM(s_shape[1:], F32), pre_scratch, pre_scratch],
        compiler_params=_params(("arbitrary",), VMEM_BIG),
    )(qa, ka, va, za, qa, ka, va, za, wgf, bgf, wgb, bgb)


def _gla_bwd_call(qa, ka, va, za, do, sf, sb, wgf, bgf, wgb, bgb, dep=None):
    L = qa.shape[0]
    br = min(512, L)
    nb, nc = L // br, br // GLA_CHUNK
    hw = GLA_HEADS * HEAD_PAD

    def body(qaf, kaf, vaf, zaf, dof, sf_r, qab, kab, vab, zab, dob, sb_r, wgf_r, bgf_r, wgb_r, bgb_r,
             dqf, dkf, dvf, dzf, dwf, dbf, dqb, dkb, dvb, dzb, dwb, dbb, gt_f, gt_b, pre_f, pre_b):
        @pl.when(pl.program_id(0) == 0)
        def _():
            for ref in (gt_f, gt_b, dwf, dbf, dwb, dbb):
                ref[...] = jnp.zeros_like(ref)

        _gla_block_pre(qaf, kaf, zaf, wgf_r, bgf_r, False, nc, *pre_f[:4], keep=pre_f[4:8])
        _gla_block_pre(qab, kab, zab, wgb_r, bgb_r, True, nc, *pre_b[:4], keep=pre_b[4:8])
        tri_f, tri_b = _tri_masks()
        row_w = lax.broadcasted_iota(jnp.int32, (GLA_CHUNK, HEAD_PAD), 0)

        def one(rev, pre, q_r, k_r, v_r, do_r, s_r, dq_r, dk_r, dv_r, gt, ci):
            qd_s, ki_s, ks_s, dec_s, _, eb_s, enb_s, elb_s, db_s = pre
            tri = tri_b if rev else tri_f
            last_row = 0 if rev else GLA_CHUNK - 1
            rows = pl.ds(pl.multiple_of(ci * GLA_CHUNK, GLA_CHUNK), GLA_CHUNK)
            dec = dec_s[ci]
            heads = range(GLA_HEADS)
            lanes = [slice(HEAD_PAD * h, HEAD_PAD * (h + 1)) for h in heads]
            qd = [qd_s[rows, sl] for sl in lanes]
            ki = [ki_s[rows, sl] for sl in lanes]
            ks = [ks_s[rows, sl] for sl in lanes]
            v = [v_r[rows, sl] for sl in lanes]
            do_h = [_mx(do_r[rows, sl]) for sl in lanes]
            s_t = [s_r[ci, h] for h in heads]
            g_t = [gt[h] for h in heads]
            g_m = [_mx(g_t[h]) for h in heads]
            a = [_dot_nt(qd[h], ki[h]) for h in heads]
            da = [_dot_nt(do_h[h], v[h]) for h in heads]
            dv_carried = [_dot_nt(ks[h], g_m[h]) for h in heads]
            dqd_carried = [_dot(do_h[h], _mx(s_t[h])) for h in heads]
            dks = [_dot(v[h], g_m[h]) for h in heads]
            g_grown = [_dot_tn(do_h[h], qd[h]) for h in heads]
            a = [_mx(jnp.where(tri, a[h], 0.0)) for h in heads]
            da = [_mx(jnp.where(tri, da[h], 0.0)) for h in heads]
            dv_inner = [_dot_tn(a[h], do_h[h]) for h in heads]
            dqd_inner = [_dot(da[h], ki[h]) for h in heads]
            dki = [_dot_tn(da[h], qd[h]) for h in heads]
            dq, dk = [], []
            for h in heads:
                sl = lanes[h]
                dv_r[rows, sl] = (dv_inner[h] + dv_carried[h]).astype(dv_r.dtype)
                ddec = jnp.sum(g_t[h] * s_t[h].astype(F32), axis=0, keepdims=True)
                gt[h] = g_t[h] * dec[:, sl] + g_grown[h]
                dq.append((dqd_inner[h] + dqd_carried[h]) * eb_s[rows, sl] * 0.125)
                dk_state = dks[h] * elb_s[rows, sl]
                dk.append(dki[h] * enb_s[rows, sl] + dk_state)
                k = k_r[rows, sl]
                dblast = jnp.sum(dk_state * k, axis=0, keepdims=True) + dec[:, sl] * ddec
                db_s[rows, sl] = q_r[rows, sl] * dq[h] - k * dk[h] + jnp.where(row_w == last_row, dblast, 0.0)
            low = _low_half(GLA_CHUNK)
            for pair in range(GLA_HEADS // 2):
                psl = slice(HEAD_PAD * pair, HEAD_PAD * (pair + 1))
                for ref, val in ((dq_r, dq), (dk_r, dk)):
                    both = jnp.where(low, val[2 * pair], pltpu.roll(val[2 * pair + 1], 64, 1))
                    ref[rows, psl] = both.astype(ref.dtype)

        def loop(t, carry):
            one(False, pre_f, qaf, kaf, vaf, dof, sf_r, dqf, dkf, dvf, gt_f, nc - 1 - t)
            one(True, pre_b, qab, kab, vab, dob, sb_r, dqb, dkb, dvb, gt_b, t)
            return carry

        lax.fori_loop(0, nc, loop, 0, unroll=True)

        def gate_grads(rev, pre, z_r, w_r, dz_r, dw_r, dbias_r):
            g_s, db_s = pre[4], pre[8]
            back_m = _mx((tri_f if rev else tri_b).astype(F32))
            db = db_s[...]
            dla = jnp.concatenate([_chunk_sums(back_m, db[GLA_CHUNK * c:GLA_CHUNK * (c + 1)]) for c in range(nc)],
                                  axis=0)
            dg = dla * (1.0 / GLA_GATE_NORM) * (1.0 / (1.0 + jnp.exp(g_s[...])))
            dg_m = _mx(dg)
            dz_r[...] = _dot_nt(dg_m, w_r[...])
            dw_r[...] += _dot_tn(_mx(z_r[...]), dg_m)
            dbias_r[...] += jnp.sum(dg, axis=0, keepdims=True)

        gate_grads(False, pre_f, zaf, wgf_r, dzf, dwf, dbf)
        gate_grads(True, pre_b, zab, wgb_r, dzb, dwb, dbb)

    last_first = lambda i: (nb - 1 - i, 0)
    first_last = lambda i: (i, 0)
    s_shape = (nc, GLA_HEADS, HEAD_PAD, HEAD_PAD)

    def ins(m):
        return [pl.BlockSpec((br, hw), m), pl.BlockSpec((br, hw), m), pl.BlockSpec((br, hw), m),
                pl.BlockSpec((br, 128), m), pl.BlockSpec((br, hw), m),
                pl.BlockSpec(s_shape, lambda i: m(i) + (0, 0))]

    def outs(m):
        return [pl.BlockSpec((br, hw // 2), m), pl.BlockSpec((br, hw // 2), m), pl.BlockSpec((br, hw), m),
                pl.BlockSpec((br, 128), m), _full_spec((128, hw)), _full_spec((1, hw))]

    out_shape = [jax.ShapeDtypeStruct((L, hw // 2), MXU_DTYPE)] * 2 + [
        jax.ShapeDtypeStruct((L, hw), MXU_DTYPE),
        jax.ShapeDtypeStruct((L, 128), F32), jax.ShapeDtypeStruct((128, hw), F32),
        jax.ShapeDtypeStruct((1, hw), F32)]
    wspecs = [_full_spec((128, hw)), _full_spec((1, hw))] * 2
    body, extra, extra_specs = _after(body, 16, dep)
    pre_scratch = ([pltpu.VMEM((br, hw), MXU_DTYPE)] * 3 + [pltpu.VMEM((nc, 1, hw), F32)]
                   + [pltpu.VMEM((br, hw), F32)] * 5)
    return pl.pallas_call(
        body, name="gla_bwd", grid=(nb,),
        in_specs=ins(last_first) + ins(first_last) + wspecs + extra_specs,
        out_specs=outs(last_first) + outs(first_last),
        out_shape=out_shape + out_shape,
        scratch_shapes=[pltpu.VMEM(s_shape[1:], F32), pltpu.VMEM(s_shape[1:], F32), pre_scratch, pre_scratch],
        compiler_params=_params(("arbitrary",), VMEM_BIG),
    )(qa, ka, va, za, do, sf, qa, ka, va, za, do, sb, wgf, bgf, wgb, bgb, *extra)


def _t5_buckets(rel):
    nb = REL_BUCKETS // 2
    ret = (rel > 0).astype(np.int32) * nb
    n = np.abs(rel)
    max_exact = nb // 2
    large = max_exact + (np.log(np.maximum(n, 1).astype(np.float32) / max_exact)
                         / math.log(REL_MAX_DIST / max_exact) * (nb - max_exact)).astype(np.int32)
    large = np.minimum(large, nb - 1)
    return ret + np.where(n < max_exact, n, large)


SWA_GROUP = SWA_Q_HEADS // SWA_KV_HEADS
SWA_SPAN = 3 * SWA_BLOCK
SWA_GROUP_LANES = SWA_GROUP * SWA_BLOCK


def _band_buckets():
    s = np.arange(SWA_SPAN)[:, None]
    c = np.arange(SWA_BLOCK)[None, :]
    return _t5_buckets(s - SWA_BLOCK - c).astype(np.int32)


def _swa_valid(n, seq_len):
    key_pos = (n - 1) * SWA_BLOCK + lax.broadcasted_iota(jnp.int32, (SWA_SPAN, 1), 0)
    return (key_pos >= 0) & (key_pos < seq_len)


def _swa_sink_row(sink_r, kv):
    lane = lax.broadcasted_iota(jnp.int32, (1, SWA_GROUP_LANES), 1)
    row = jnp.full((1, SWA_GROUP_LANES), sink_r[kv * SWA_GROUP], F32)
    for g in range(1, SWA_GROUP):
        row = jnp.where(lane >= g * SWA_BLOCK, sink_r[kv * SWA_GROUP + g], row)
    return row


def _swa_group(ref, kv):
    first = kv * SWA_GROUP
    return jnp.concatenate([ref[:, HEAD_PAD * h:HEAD_PAD * (h + 1)] for h in range(first, first + SWA_GROUP)],
                           axis=0)


def _swa_softmax(scores, bias_t, sink_row, valid):
    st = jnp.where(valid, scores + bias_t, -1e30)
    m = jnp.maximum(jnp.max(st, axis=0, keepdims=True), sink_row)
    p = jnp.exp(st - m)
    e_sink = jnp.exp(sink_row - m)
    inv = 1.0 / (jnp.sum(p, axis=0, keepdims=True) + e_sink)
    return p * inv, e_sink * inv


def _swa_fwd_call(qs, ks, vs, bias, sink, dep=None):
    L = qs.shape[0]

    def body(q_r, k_r, v_r, bias_r, sink_r, o_r):
        n = pl.program_id(0)
        span = pl.ds(pl.multiple_of(n * SWA_BLOCK, SWA_BLOCK), SWA_SPAN)
        valid = _swa_valid(n, L)
        groups = range(SWA_KV_HEADS)
        lanes = [slice(HEAD_PAD * kv, HEAD_PAD * (kv + 1)) for kv in groups]
        scores = [_dot_nt(k_r[span, lanes[kv]], _swa_group(q_r, kv)) for kv in groups]
        probs = [_swa_softmax(scores[kv], bias_r[kv], _swa_sink_row(sink_r, kv), valid)[0] for kv in groups]
        low = _low_half(SWA_BLOCK)
        for kv in groups:
            og = _dot_tn(_mx(probs[kv]), v_r[span, lanes[kv]])
            for pair in range(SWA_GROUP // 2):
                even = og[2 * SWA_BLOCK * pair:2 * SWA_BLOCK * pair + SWA_BLOCK]
                odd = og[2 * SWA_BLOCK * pair + SWA_BLOCK:2 * SWA_BLOCK * (pair + 1)]
                first = HEAD_PAD * (kv * SWA_GROUP // 2 + pair)
                o_r[:, first:first + HEAD_PAD] = jnp.where(low, even, pltpu.roll(odd, 64, 1)).astype(o_r.dtype)

    qw = SWA_Q_HEADS * HEAD_PAD
    body, extra, extra_specs = _after(body, 5, dep)
    return pl.pallas_call(
        body, name="swa_fwd", grid=(L // SWA_BLOCK,),
        in_specs=[_row_spec(SWA_BLOCK, qw), _vmem_spec(), _vmem_spec(), _vmem_spec(),
                  pl.BlockSpec(memory_space=pltpu.SMEM)] + extra_specs,
        out_specs=_row_spec(SWA_BLOCK, qw // 2),
        out_shape=jax.ShapeDtypeStruct((L, qw // 2), MXU_DTYPE),
        compiler_params=_params(("arbitrary",), VMEM_BIG),
    )(qs, ks, vs, bias, sink, *extra)


def _swa_bwd_call(qs, ks, vs, bias, sink, do, dep=None):
    L = qs.shape[0]
    qw = SWA_Q_HEADS * HEAD_PAD
    kw = SWA_KV_HEADS * HEAD_PAD

    def body(q_r, k_r, v_r, bias_r, sink_r, do_r, dq_r, dk_r, dv_r, dbias_r, dsink_r):
        n = pl.program_id(0)

        @pl.when(n == 0)
        def _():
            for ref in (dk_r, dv_r, dbias_r, dsink_r):
                ref[...] = jnp.zeros_like(ref)

        span = pl.ds(pl.multiple_of(n * SWA_BLOCK, SWA_BLOCK), SWA_SPAN)
        valid = _swa_valid(n, L)
        groups = range(SWA_KV_HEADS)
        lanes = [slice(HEAD_PAD * kv, HEAD_PAD * (kv + 1)) for kv in groups]
        kk = [k_r[span, sl] for sl in lanes]
        vv = [v_r[span, sl] for sl in lanes]
        qg = [_swa_group(q_r, kv) for kv in groups]
        dog = [_swa_group(do_r, kv) for kv in groups]
        scores = [_dot_nt(kk[kv], qg[kv]) for kv in groups]
        dp = [_dot_nt(vv[kv], dog[kv]) for kv in groups]
        probs = [_swa_softmax(scores[kv], bias_r[kv], _swa_sink_row(sink_r, kv), valid) for kv in groups]
        ds_m, pn_m = [], []
        for kv in groups:
            pn, p_sink = probs[kv]
            delta = jnp.sum(pn * dp[kv], axis=0, keepdims=True)
            ds = pn * (dp[kv] - delta)
            dsink_r[kv] -= p_sink * delta
            dbias_r[kv] += ds
            ds_m.append(_mx(ds))
            pn_m.append(_mx(pn))
        dqg = [_dot_tn(ds_m[kv], kk[kv]) * 0.125 for kv in groups]
        dkk = [_dot(ds_m[kv], qg[kv]) for kv in groups]
        dvv = [_dot(pn_m[kv], dog[kv]) for kv in groups]
        low = _low_half(SWA_BLOCK)
        for kv in groups:
            for pair in range(SWA_GROUP // 2):
                even = dqg[kv][2 * SWA_BLOCK * pair:2 * SWA_BLOCK * pair + SWA_BLOCK]
                odd = dqg[kv][2 * SWA_BLOCK * pair + SWA_BLOCK:2 * SWA_BLOCK * (pair + 1)]
                first = HEAD_PAD * (kv * SWA_GROUP // 2 + pair)
                dq_r[:, first:first + HEAD_PAD] = jnp.where(low, even, pltpu.roll(odd, 64, 1)).astype(dq_r.dtype)
            dk_r[span, lanes[kv]] += dkk[kv]
            dv_r[span, lanes[kv]] += dvv[kv]

    body, extra, extra_specs = _after(body, 6, dep)
    return pl.pallas_call(
        body, name="swa_bwd", grid=(L // SWA_BLOCK,),
        in_specs=[_row_spec(SWA_BLOCK, qw), _vmem_spec(), _vmem_spec(), _vmem_spec(),
                  pl.BlockSpec(memory_space=pltpu.SMEM), _row_spec(SWA_BLOCK, qw)] + extra_specs,
        out_specs=[_row_spec(SWA_BLOCK, qw // 2), _vmem_spec(), _vmem_spec(), _vmem_spec(), _vmem_spec()],
        out_shape=[jax.ShapeDtypeStruct((L, qw // 2), MXU_DTYPE),
                   jax.ShapeDtypeStruct((L + 2 * SWA_BLOCK, kw), F32),
                   jax.ShapeDtypeStruct((L + 2 * SWA_BLOCK, kw), F32),
                   jax.ShapeDtypeStruct((SWA_KV_HEADS, SWA_SPAN, SWA_GROUP_LANES), F32),
                   jax.ShapeDtypeStruct((SWA_KV_HEADS, 1, SWA_GROUP_LANES), F32)],
        compiler_params=_params(("arbitrary",), VMEM_BIG),
    )(qs, ks, vs, bias, sink, do, *extra)


def _bias_call(rel_bias, buckets):
    def body(t_r, bk_r, o_r):
        bk = bk_r[...]
        s = lax.broadcasted_iota(jnp.int32, bk.shape, 0)
        c = lax.broadcasted_iota(jnp.int32, bk.shape, 1)
        in_band = jnp.abs(s - SWA_BLOCK - c) <= SWA_BLOCK
        for h in range(SWA_Q_HEADS):
            acc = jnp.zeros(bk.shape, F32)
            for b in range(REL_BUCKETS):
                acc = jnp.where(bk == b, t_r[b, h], acc)
            g = h % SWA_GROUP
            o_r[h // SWA_GROUP, :, SWA_BLOCK * g:SWA_BLOCK * (g + 1)] = jnp.where(in_band, acc, -1e30)

    return pl.pallas_call(
        body, name="band_bias",
        in_specs=[pl.BlockSpec(memory_space=pltpu.SMEM), _vmem_spec()], out_specs=_vmem_spec(),
        out_shape=jax.ShapeDtypeStruct((SWA_KV_HEADS, SWA_SPAN, SWA_GROUP_LANES), F32),
    )(rel_bias, buckets)


def _relbias_call(dbias, dsink, buckets, dep=None):
    def body(db_r, ds_r, bk_r, o_r, os_r):
        bk = bk_r[...]
        rowi = lax.broadcasted_iota(jnp.int32, (REL_BUCKETS, 128), 0)
        lanei = lax.broadcasted_iota(jnp.int32, (REL_BUCKETS, 128), 1)
        lane1 = lax.broadcasted_iota(jnp.int32, (1, 128), 1)
        acc = jnp.zeros((REL_BUCKETS, 128), F32)
        acc_sink = jnp.zeros((1, 128), F32)
        for h in range(SWA_Q_HEADS):
            kv, g = h // SWA_GROUP, h % SWA_GROUP
            lanes = slice(SWA_BLOCK * g, SWA_BLOCK * (g + 1))
            part = db_r[kv, :, lanes]
            for b in range(REL_BUCKETS):
                s = jnp.sum(jnp.where(bk == b, part, 0.0))
                acc = acc + jnp.where((rowi == b) & (lanei == h), s, 0.0)
            acc_sink = acc_sink + jnp.where(lane1 == h, jnp.sum(ds_r[kv, :, lanes]), 0.0)
        o_r[...] = acc
        os_r[...] = acc_sink

    body, extra, extra_specs = _after(body, 3, dep)
    return pl.pallas_call(
        body, name="relbias_grad",
        in_specs=[_vmem_spec()] * 3 + extra_specs, out_specs=[_vmem_spec()] * 2,
        out_shape=[jax.ShapeDtypeStruct((REL_BUCKETS, 128), F32), jax.ShapeDtypeStruct((1, 128), F32)],
    )(dbias, dsink, buckets, *extra)


def _mix_call(o_f, o_b, ga, o_s, x, gn, w_out_p, g_post, g_pre2, dep=None):
    L = x.shape[0]
    tm = min(512, L)
    hw = GLA_HEADS * HEAD_PAD

    def body(of_r, ob_r, ga_r, os_r, x_r, gn_r, w_r, gp_r, g2_r, cat_r, mix_r, h1_r, n2_r):
        gn_v = gn_r[...]
        for h in range(GLA_HEADS):
            sl = slice(HEAD_PAD * h, HEAD_PAD * (h + 1))
            oh = of_r[:, sl] + ob_r[:, sl]
            on = oh * _rms_r(oh) * gn_v
            gate = ga_r[:, sl]
            cat_r[:, sl] = (on * (gate * jax.nn.sigmoid(gate))).astype(cat_r.dtype)
        os_v = os_r[...]
        cat_r[:, hw:] = os_v
        mix = _dot(cat_r[:, :hw], w_r[:hw, :]) + _dot(os_v, w_r[hw:, :])
        mix_r[...] = mix
        h1 = x_r[...] + mix * _rms_r(mix) * gp_r[...]
        h1_r[...] = h1
        n2_r[...] = (h1 * _rms_r(h1) * g2_r[...]).astype(n2_r.dtype)

    body, extra, extra_specs = _after(body, 9, dep)
    return pl.pallas_call(
        body, name="mix_fwd", grid=(L // tm,),
        in_specs=[_row_spec(tm, hw), _row_spec(tm, hw), _row_spec(tm, hw), _row_spec(tm, OUT_PAD - hw),
                  _row_spec(tm, D_MODEL), _full_spec((1, HEAD_PAD)), _vmem_spec(),
                  _full_spec((1, D_MODEL)), _full_spec((1, D_MODEL))] + extra_specs,
        out_specs=[_row_spec(tm, OUT_PAD), _row_spec(tm, D_MODEL), _row_spec(tm, D_MODEL), _row_spec(tm, D_MODEL)],
        out_shape=[jax.ShapeDtypeStruct((L, OUT_PAD), MXU_DTYPE), jax.ShapeDtypeStruct((L, D_MODEL), F32),
                   jax.ShapeDtypeStruct((L, D_MODEL), F32), jax.ShapeDtypeStruct((L, D_MODEL), MXU_DTYPE)],
        compiler_params=_params(("arbitrary",), VMEM_BIG),
    )(o_f, o_b, ga, o_s, x, gn, w_out_p, g_post, g_pre2, *extra)


def _mlp_fwd_call(n2, h1, tgt, w_ud, g_post):
    L = n2.shape[0]
    tm = min(512, L)
    blk = D_FF // N_CHIPS

    def body(n2_r, h1_r, t_r, w_r, g_r, a_r, rz_r, dh2_r, dff_r, loss_r, dg_r):
        @pl.when(pl.program_id(0) == 0)
        def _():
            loss_r[...] = jnp.zeros_like(loss_r)
            dg_r[...] = jnp.zeros_like(dg_r)

        n2v = n2_r[...]
        ff = jnp.zeros((tm, D_MODEL), F32)
        for j in range(N_CHIPS):
            sl = slice(blk * j, blk * (j + 1))
            rz = jnp.maximum(_dot(n2v, w_r[j, 0]), 0.0)
            a = _mx(rz * rz)
            rz_r[:, sl] = rz.astype(rz_r.dtype)
            a_r[:, sl] = a
            ff = ff + _dot(a, w_r[j, 1])
        g = g_r[...]
        r = _rms_r(ff)
        err = h1_r[...] + ff * r * g - t_r[...]
        loss_r[...] += 0.5 * jnp.sum(err * err) / D_MODEL
        dh2 = err * (1.0 / D_MODEL)
        dh2_r[...] = dh2
        dff, dg = _rms_bwd(ff, r, g, dh2)
        dff_r[...] = dff.astype(dff_r.dtype)
        dg_r[...] += dg

    return pl.pallas_call(
        body, name="mlp_fwd", grid=(L // tm,),
        in_specs=[_row_spec(tm, D_MODEL), _row_spec(tm, D_MODEL), _row_spec(tm, D_MODEL),
                  _vmem_spec(), _full_spec((1, D_MODEL))],
        out_specs=[_row_spec(tm, D_FF), _row_spec(tm, D_FF), _row_spec(tm, D_MODEL), _row_spec(tm, D_MODEL),
                   _full_spec((1, 128)), _full_spec((1, D_MODEL))],
        out_shape=[jax.ShapeDtypeStruct((L, D_FF), MXU_DTYPE), jax.ShapeDtypeStruct((L, D_FF), MXU_DTYPE),
                   jax.ShapeDtypeStruct((L, D_MODEL), F32), jax.ShapeDtypeStruct((L, D_MODEL), MXU_DTYPE),
                   jax.ShapeDtypeStruct((1, 128), F32), jax.ShapeDtypeStruct((1, D_MODEL), F32)],
        compiler_params=_params(("arbitrary",), VMEM_BIG),
    )(n2, h1, tgt, w_ud, g_post)


def _mlp_bwd_call(dff, rz, w_ud):
    L = dff.shape[0]
    tm = min(512, L)
    blk = D_FF // N_CHIPS

    def body(dff_r, rz_r, w_r, dz_r, dn2_r):
        dffv = dff_r[...]
        dn2 = jnp.zeros((tm, D_MODEL), F32)
        for j in range(N_CHIPS):
            sl = slice(blk * j, blk * (j + 1))
            dz = _mx(_dot_nt(dffv, w_r[j, 1]) * 2.0 * rz_r[:, sl].astype(F32))
            dz_r[:, sl] = dz
            dn2 = dn2 + _dot_nt(dz, w_r[j, 0])
        dn2_r[...] = dn2

    return pl.pallas_call(
        body, name="mlp_bwd", grid=(L // tm,),
        in_specs=[_row_spec(tm, D_MODEL), _row_spec(tm, D_FF), _vmem_spec()],
        out_specs=[_row_spec(tm, D_FF), _row_spec(tm, D_MODEL)],
        out_shape=[jax.ShapeDtypeStruct((L, D_FF), MXU_DTYPE), jax.ShapeDtypeStruct((L, D_MODEL), F32)],
        compiler_params=_params(("arbitrary",), VMEM_BIG),
    )(dff, rz, w_ud)


def _mlp_wgrad_call(a, dff, n2, dz):
    L = a.shape[0]
    tf = 512
    per = (D_FF // N_CHIPS) // tf

    def body(a_r, dff_r, n2_r, dz_r, dwd_r, dwu_r):
        dwd_r[...] = _dot_tn(a_r[...], dff_r[...])
        dwu_r[...] = _dot_tn(n2_r[...], dz_r[...])

    return pl.pallas_call(
        body, name="mlp_wgrad", grid=(D_FF // tf,),
        in_specs=[pl.BlockSpec((L, tf), lambda j: (0, j)), _vmem_spec(), _vmem_spec(),
                  pl.BlockSpec((L, tf), lambda j: (0, j))],
        out_specs=[pl.BlockSpec((tf, D_MODEL), lambda j: (j, 0)),
                   pl.BlockSpec((None, D_MODEL, tf), lambda j: (j // per, 0, j % per))],
        out_shape=[jax.ShapeDtypeStruct((D_FF, D_MODEL), F32),
                   jax.ShapeDtypeStruct((N_CHIPS, D_MODEL, D_FF // N_CHIPS), F32)],
        compiler_params=_params(("arbitrary",), VMEM_BIG),
    )(a, dff, n2, dz)


def _mix_bwd_call(dn2, dh2, h1, mix, cat, o_f, o_b, ga, gn, g_post, g_pre2, w_out_p):
    L = dn2.shape[0]
    tm = min(512, L)
    hw = GLA_HEADS * HEAD_PAD

    def body(dn2_r, dh2_r, h1_r, mix_r, cat_r, of_r, ob_r, ga_r, gn_r, gp_r, g2_r, w_r,
             dh1_r, do_r, dga_r, dos_r, dw_r, dg2_r, dgp_r, dgn_r):
        @pl.when(pl.program_id(0) == 0)
        def _():
            for ref in (dw_r, dg2_r, dgp_r, dgn_r):
                ref[...] = jnp.zeros_like(ref)

        parts = [slice(start, start + min(256, tm)) for start in range(0, tm, 256)]
        dmix_m = []
        for rs in parts:
            h1 = h1_r[rs, :]
            dx2, dg2 = _rms_bwd(h1, _rms_r(h1), g2_r[...], dn2_r[rs, :])
            dh1 = dh2_r[rs, :] + dx2
            dh1_r[rs, :] = dh1
            dg2_r[...] += dg2
            mix = mix_r[rs, :]
            dmix, dgp = _rms_bwd(mix, _rms_r(mix), gp_r[...], dh1)
            dgp_r[...] += dgp
            dmix_m.append(_mx(dmix))
        dcat = [_dot_nt(d, w_r[...]) for d in dmix_m]
        for rs, d in zip(parts, dmix_m):
            dw_r[...] += _dot_tn(cat_r[rs, :], d)
        gn_v = gn_r[...]
        dgn = jnp.zeros((1, HEAD_PAD), F32)
        for rs, dc in zip(parts, dcat):
            dos_r[rs, :] = _spread_heads(dc[:, hw:]).astype(dos_r.dtype)
            for h in range(GLA_HEADS):
                sl = slice(HEAD_PAD * h, HEAD_PAD * (h + 1))
                oh = of_r[rs, sl] + ob_r[rs, sl]
                rr = _rms_r(oh)
                xh = oh * rr
                gate = ga_r[rs, sl]
                sg = jax.nn.sigmoid(gate)
                silu = gate * sg
                doa = dc[:, sl]
                dga_r[rs, sl] = (doa * (xh * gn_v) * (sg + silu * (1.0 - sg))).astype(dga_r.dtype)
                don = doa * silu
                gd = don * gn_v
                do_r[rs, sl] = rr * (gd - xh * jnp.mean(gd * xh, axis=-1, keepdims=True))
                dgn = dgn + jnp.sum(don * xh, axis=0, keepdims=True)
        dgn_r[...] += dgn

    return pl.pallas_call(
        body, name="mix_bwd", grid=(L // tm,),
        in_specs=[_row_spec(tm, D_MODEL)] * 4 + [_row_spec(tm, OUT_PAD)] + [_row_spec(tm, hw)] * 3
        + [_full_spec((1, HEAD_PAD)), _full_spec((1, D_MODEL)), _full_spec((1, D_MODEL)), _vmem_spec()],
        out_specs=[_row_spec(tm, D_MODEL), _row_spec(tm, hw), _row_spec(tm, hw),
                   _row_spec(tm, SWA_Q_HEADS * HEAD_PAD),
                   _full_spec((OUT_PAD, D_MODEL)), _full_spec((1, D_MODEL)), _full_spec((1, D_MODEL)),
                   _full_spec((1, HEAD_PAD))],
        out_shape=[jax.ShapeDtypeStruct((L, D_MODEL), F32), jax.ShapeDtypeStruct((L, hw), F32),
                   jax.ShapeDtypeStruct((L, hw), MXU_DTYPE),
                   jax.ShapeDtypeStruct((L, SWA_Q_HEADS * HEAD_PAD), MXU_DTYPE),
                   jax.ShapeDtypeStruct((OUT_PAD, D_MODEL), F32), jax.ShapeDtypeStruct((1, D_MODEL), F32),
                   jax.ShapeDtypeStruct((1, D_MODEL), F32), jax.ShapeDtypeStruct((1, HEAD_PAD), F32)],
        compiler_params=_params(("arbitrary",), VMEM_BIG),
    )(dn2, dh2, h1, mix, cat, o_f, o_b, ga, gn, g_post, g_pre2, w_out_p)


def _in_bwd_call(x, dh1, g_pre, w_in_t, pairs, singles, halos, dep=None):
    L = x.shape[0]
    tm = min(512, L)
    per = tm // SWA_BLOCK
    n_pair, n_single, n_halo = len(pairs), len(singles), len(halos)
    groups = [c for c, _ in pairs] + [c for c, _ in singles] + [c for c, _ in halos]

    def body(*refs):
        x_r, dh1_r, g_r, w_r = refs[:4]
        pair_refs = refs[4:4 + 2 * n_pair]
        single_refs = refs[4 + 2 * n_pair:4 + 2 * n_pair + n_single]
        halo_refs = refs[4 + 2 * n_pair + n_single:4 + 2 * n_pair + n_single + per * n_halo]
        dx_r, dw_r, dg_r = refs[4 + 2 * n_pair + n_single + per * n_halo:]

        @pl.when(pl.program_id(0) == 0)
        def _():
            dw_r[...] = jnp.zeros_like(dw_r)
            dg_r[...] = jnp.zeros_like(dg_r)

        xv = x_r[...]
        r = _rms_r(xv)
        g = g_r[...]
        u = _mx(xv * r * g)
        vals = [pair_refs[2 * i][...].astype(F32) + pair_refs[2 * i + 1][...].astype(F32) for i in range(n_pair)]
        vals += [ref[...].astype(F32) for ref in single_refs]
        vals += [jnp.concatenate([ref[...] for ref in halo_refs[per * i:per * (i + 1)]], axis=0)
                 for i in range(n_halo)]
        ds = [_mx(_squeeze_heads(val) if heads else val) for (_, _, heads), val in zip(groups, vals)]
        du = jnp.zeros((tm, D_MODEL), F32)
        for (first, rows, _), d in zip(groups, ds):
            du = du + _dot(d, w_r[first:first + rows, :])
        for (first, rows, _), d in zip(groups, ds):
            dw_r[first:first + rows, :] += _dot_tn(d, u)
        dx, dg = _rms_bwd(xv, r, g, du)
        dx_r[...] = dh1_r[...] + dx
        dg_r[...] += dg

    arrays = [a for _, pr in pairs for a in pr] + [a for _, a in singles]
    specs = [_row_spec(tm, a.shape[1]) for a in arrays]
    for _, a in halos:
        specs += [pl.BlockSpec((SWA_BLOCK, a.shape[1]), lambda i, j=j: (per * i + 1 + j, 0)) for j in range(per)]
        arrays += [a] * per
    body, extra, extra_specs = _after(body, 4 + len(arrays), dep)
    return pl.pallas_call(
        body, name="in_bwd", grid=(L // tm,),
        in_specs=[_row_spec(tm, D_MODEL), _row_spec(tm, D_MODEL), _full_spec((1, D_MODEL)), _vmem_spec()] + specs
        + extra_specs,
        out_specs=[_row_spec(tm, D_MODEL), _full_spec((IN_COLS, D_MODEL)), _full_spec((1, D_MODEL))],
        out_shape=[jax.ShapeDtypeStruct((L, D_MODEL), F32), jax.ShapeDtypeStruct((IN_COLS, D_MODEL), F32),
                   jax.ShapeDtypeStruct((1, D_MODEL), F32)],
        compiler_params=_params(("arbitrary",), VMEM_BIG),
    )(x, dh1, g_pre, w_in_t, *arrays, *extra)


def _adamw_math(w, g, m, v):
    m = ADAM_B1 * m + (1.0 - ADAM_B1) * g
    v = ADAM_B2 * v + (1.0 - ADAM_B2) * (g * g)
    m_hat = m / (1.0 - ADAM_B1 ** ADAM_STEP)
    v_hat = v / (1.0 - ADAM_B2 ** ADAM_STEP)
    delta = -ADAM_LR * (m_hat / (jnp.sqrt(v_hat) + ADAM_EPS) + ADAM_WD * w)
    return delta, m, v


def _adamw_call(w, g, m, v, name, dep=None):
    rows, cols = w.shape
    tr = min(256, rows)

    def body(w_r, g_r, m_r, v_r, d_r, nm_r, nv_r):
        d_r[...], nm_r[...], nv_r[...] = _adamw_math(w_r[...], g_r[...], m_r[...], v_r[...])

    if rows % tr == 0:
        spec, steps = _row_spec(tr, cols), rows // tr
    else:
        spec, steps = pl.BlockSpec((rows, 256), lambda i: (0, i)), cols // 256
    body, extra, extra_specs = _after(body, 4, dep)
    return pl.pallas_call(
        body, name=name, grid=(steps,),
        in_specs=[spec] * 4 + extra_specs, out_specs=[spec] * 3,
        out_shape=[jax.ShapeDtypeStruct(w.shape, F32)] * 3,
        compiler_params=_params(("arbitrary",)),
    )(w, g, m, v, *extra)


def _position():
    return lax.axis_index("x"), lax.axis_index("y"), lax.axis_index("c")


def _other_chips(x, y):
    return [(1 - x, y), (x, 1 - y), (1 - x, 1 - y)]


ROWS, COLS = -2, -1


def _half(ref, which, axis):
    size = ref.shape[axis] // 2
    span = pl.ds(pl.multiple_of(which * size, 16 if axis == ROWS else 128), size)
    index = [slice(None)] * len(ref.shape)
    index[axis] = span
    return ref.at[tuple(index)]


def _quarter(ref, half, which, axis):
    size = ref.shape[axis] // 4
    span = pl.ds(pl.multiple_of((2 * half + which) * size, 16 if axis == ROWS else 128), size)
    index = [slice(None)] * len(ref.shape)
    index[axis] = span
    return ref.at[tuple(index)]


def _first_gather_call(shards, axes, routed):
    n = len(shards)
    per = 7

    def body(*refs):
        srcs, outs = refs[:n], refs[n:2 * n]
        send_sems, recv_sems, local_sems = refs[2 * n:]
        x, y, c = _position()
        me, sibling = (x, y, c), (x, y, 1 - c)
        x_side, y_side, across = _other_chips(x, y)
        local = [pltpu.make_async_copy(srcs[a], outs[a].at[2 * x + y], local_sems.at[a]) for a in range(n)]
        for cp in local:
            cp.start()

        def copy(a, k, dst, to, src=None):
            return pltpu.make_async_remote_copy(
                src_ref=dst if src is None else src, dst_ref=dst, send_sem=send_sems.at[per * a + k],
                recv_sem=recv_sems.at[per * a + k], device_id=to, device_id_type=MESH_ID)

        def half(a, chip, pc):
            return _half(outs[a].at[2 * chip[0] + chip[1]], pc, axes[a])

        def quarter(a, chip, q):
            return _quarter(outs[a].at[2 * chip[0] + chip[1]], c, q, axes[a])

        sends = []
        for a in range(n):
            mine = _half(srcs[a], c, axes[a])
            targets = (x_side, y_side) if routed[a] else (x_side, y_side, across)
            sends += [copy(a, j, half(a, (x, y), c), (*chip, c), src=mine) for j, chip in enumerate(targets)]
        for cp in sends:
            cp.start()
        for a in range(n):
            for j, chip in enumerate((x_side, y_side)):
                copy(a, j, half(a, chip, c), me).wait_recv()
                if routed[a]:
                    other = (y_side, x_side)[j]
                    sends.append(copy(a, 2 + j, quarter(a, chip, j), (*other, c)))
                    sends[-1].start()
                sends.append(copy(a, 4 + j, half(a, chip, c), sibling))
                sends[-1].start()
        for a in range(n):
            if routed[a]:
                for j in range(2):
                    copy(a, 2 + j, quarter(a, across, j), me).wait_recv()
            else:
                copy(a, 2, half(a, across, c), me).wait_recv()
            sends.append(copy(a, 6, half(a, across, c), sibling))
            sends[-1].start()
        for a in range(n):
            for k, chip in ((4, x_side), (5, y_side), (6, across)):
                copy(a, k, half(a, chip, 1 - c), me).wait_recv()
        for cp in sends:
            cp.wait_send()
        for cp in local:
            cp.wait()

    return pl.pallas_call(
        body, name="first_gather",
        in_specs=[_any_spec()] * n, out_specs=[_any_spec()] * n,
        out_shape=[jax.ShapeDtypeStruct((N_CHIPS,) + s.shape, s.dtype) for s in shards],
        scratch_shapes=[pltpu.SemaphoreType.DMA((per * n,)), pltpu.SemaphoreType.DMA((per * n,)),
                        pltpu.SemaphoreType.DMA((n,))],
    )(*shards)


def _split_start(name, arrays, n_copies, plan):
    n = len(arrays)

    def body(*refs):
        ins, send_sems, recv_sems, token = refs[:n], refs[n], refs[n + 1], refs[-1]
        for k, (src, dst, to, _) in enumerate(plan(ins)):
            pltpu.make_async_remote_copy(src_ref=src, dst_ref=dst, send_sem=send_sems.at[k],
                                         recv_sem=recv_sems.at[k], device_id=to, device_id_type=MESH_ID).start()
        token[...] = jnp.zeros_like(token)

    hbm = pl.BlockSpec(memory_space=pltpu.HBM)
    sem = pl.BlockSpec(memory_space=pltpu.SEMAPHORE)
    out = pl.pallas_call(
        body, name=name,
        out_shape=(pltpu.SemaphoreType.DMA((n_copies,)), pltpu.SemaphoreType.DMA((n_copies,)))
        + tuple(pltpu.HBM(a.shape, a.dtype) for a in arrays) + (jax.ShapeDtypeStruct((8, 128), F32),),
        in_specs=[hbm] * n, out_specs=(sem, sem) + (hbm,) * n + (_vmem_spec(),),
        input_output_aliases={i: 2 + i for i in range(n)},
        compiler_params=pltpu.CompilerParams(has_side_effects=pltpu.SideEffectType.DATAFLOW_SIDE_EFFECTING),
    )(*[pltpu.with_memory_space_constraint(a, pltpu.HBM) for a in arrays])
    return (out[0], out[1], tuple(out[2:2 + n])), out[-1]


def _split_wait(name, handle, n_copies, plan, after):
    send_sems, recv_sems, arrays = handle
    n = len(arrays)

    def body(*refs):
        ins, s_sems, r_sems = refs[:n], refs[n], refs[n + 1]
        for k, (src, dst, to, landed) in enumerate(plan(ins)):
            cp = pltpu.make_async_remote_copy(src_ref=src, dst_ref=landed, send_sem=s_sems.at[k],
                                              recv_sem=r_sems.at[k], device_id=to, device_id_type=MESH_ID)
            cp.wait_send()
            cp.wait_recv()

    hbm = pl.BlockSpec(memory_space=pltpu.HBM)
    sem = pl.BlockSpec(memory_space=pltpu.SEMAPHORE)
    out = pl.pallas_call(
        body, name=name,
        out_shape=tuple(pltpu.HBM(a.shape, a.dtype) for a in arrays),
        in_specs=[hbm] * n + [sem, sem, _any_spec()], out_specs=(hbm,) * n,
        input_output_aliases={i: i for i in range(n)},
        compiler_params=pltpu.CompilerParams(has_side_effects=pltpu.SideEffectType.DATAFLOW_SIDE_EFFECTING),
    )(*arrays, send_sems, recv_sems, after)
    return tuple(out)


def _gather_plans(axes):
    n = len(axes)

    def stage_one(refs):
        x, y, c = _position()
        copies = []
        for a, axis in enumerate(axes):
            for px, py in _other_chips(x, y):
                copies.append((_half(refs[a], c, axis), _half(refs[n + a].at[2 * x + y], c, axis),
                               (px, py, c), _half(refs[n + a].at[2 * px + py], c, axis)))
        return copies

    def stage_two(refs):
        x, y, c = _position()
        copies = []
        for a, axis in enumerate(axes):
            for px, py in _other_chips(x, y):
                piece = _half(refs[n + a].at[2 * px + py], c, axis)
                copies.append((piece, piece, (x, y, 1 - c), _half(refs[n + a].at[2 * px + py], 1 - c, axis)))
        return copies

    return stage_one, stage_two


def _pair_swap_plan(axes):
    n = len(axes)

    def plan(refs):
        x, y, c = _position()
        return [(_half(refs[a], 1 - c, axes[a]), refs[n + a], (x, y, 1 - c), refs[n + a]) for a in range(n)]

    return plan


def _chip_swap_plan(n):
    def plan(refs):
        x, y, c = _position()
        copies = []
        for a in range(n):
            for j, (px, py) in enumerate(_other_chips(x, y)):
                copies.append((refs[a].at[2 * px + py], refs[n + a].at[j], (px, py, c), refs[n + a].at[j]))
        return copies

    return plan


def _pair_join_plan(axes):
    def plan(refs):
        x, y, c = _position()
        copies = []
        for a, axis in enumerate(axes):
            mine = _half(refs[a], c, axis)
            copies.append((mine, mine, (x, y, 1 - c), _half(refs[a], 1 - c, axis)))
        return copies

    return plan


def _pair_add_call(g, got, pos, name, axis):
    rows, cols = got.shape[1], got.shape[2]
    tr = min(512, rows) if axis == ROWS else rows
    nblk = rows // tr
    if axis == ROWS:
        mine = lambda j, i, p: (j, p[1] * nblk + i, 0)
    else:
        mine = lambda j, i, p: (j, 0, p[1])

    def body(pos_r, g_r, got_r, o_r):
        o_r[...] = (g_r[...] + got_r[...]).astype(o_r.dtype)

    return pl.pallas_call(
        body, name=name,
        grid_spec=pltpu.PrefetchScalarGridSpec(
            num_scalar_prefetch=1, grid=(N_CHIPS, nblk),
            in_specs=[pl.BlockSpec((None, tr, cols), mine),
                      pl.BlockSpec((None, tr, cols), lambda j, i, p: (j, i, 0))],
            out_specs=pl.BlockSpec((None, tr, cols), lambda j, i, p: (j, i, 0))),
        out_shape=jax.ShapeDtypeStruct(got.shape, COMM_DTYPE),
        compiler_params=_params(("arbitrary", "arbitrary"), VMEM_BIG),
    )(pos, g, got)


def _chip_add_call(hsum, got, pos, name, axis):
    rows, cols = hsum.shape[1], hsum.shape[2]
    tr = min(512, rows) if axis == ROWS else rows
    nblk = rows // tr
    if axis == ROWS:
        out_shape, mine = (2 * rows, cols), (lambda i, p: (p[1] * nblk + i, 0))
    else:
        out_shape, mine = (rows, 2 * cols), (lambda i, p: (0, p[1]))

    def body(pos_r, own_r, got_r, o_r):
        acc = own_r[...].astype(F32)
        for j in range(3):
            acc = acc + got_r[j].astype(F32)
        o_r[...] = acc

    return pl.pallas_call(
        body, name=name,
        grid_spec=pltpu.PrefetchScalarGridSpec(
            num_scalar_prefetch=1, grid=(nblk,),
            in_specs=[pl.BlockSpec((None, tr, cols), lambda i, p: (p[0], i, 0)),
                      pl.BlockSpec((3, tr, cols), lambda i, p: (0, i, 0))],
            out_specs=pl.BlockSpec((tr, cols), mine)),
        out_shape=jax.ShapeDtypeStruct(out_shape, F32),
        compiler_params=_params(("arbitrary",), VMEM_BIG),
    )(pos, hsum, got)


SMALL_NAMES = ("norm_mix_pre", "norm_mix_post", "norm_mlp_pre", "norm_mlp_post", "b_gate_fwd", "b_gate_bwd",
               "gla_norm", "swa_sink", "rel_bias")


N_DEVICES = 8


def _small_pack_call(grads, extras):
    operands = list(grads) + list(extras)

    def body(*refs):
        g_refs, (all_a, all_b) = refs[:len(operands)], refs[len(operands):]
        x, y, c = _position()
        me = 4 * x + 2 * y + c
        all_a[me] = jnp.zeros(all_a.shape[1:], F32)
        all_b[me] = jnp.zeros(all_b.shape[1:], F32)
        for i in range(4):
            all_a[me, i:i + 1, :] = g_refs[i][...]
        all_a[me, 4:5, 0:256] = g_refs[4][...]
        all_a[me, 5:6, 0:256] = g_refs[5][...]
        all_a[me, 6:7, 0:128] = g_refs[6][...]
        all_a[me, 7:8, 0:128] = g_refs[7][...]
        all_a[me, 7:8, 128:256] = g_refs[11][...]
        all_b[me, 0:32, 0:128] = g_refs[8][...]
        all_b[me, 32:48, :] = g_refs[9][...]
        all_b[me, 48:64, :] = g_refs[10][...]

    out_shape = [jax.ShapeDtypeStruct((N_DEVICES, 8, D_MODEL), F32), jax.ShapeDtypeStruct((N_DEVICES, 64, 256), F32)]
    return pl.pallas_call(
        body, name="small_pack",
        in_specs=[_whole_spec(a.shape) for a in operands], out_specs=[_whole_spec(s.shape) for s in out_shape],
        out_shape=out_shape,
    )(*operands)


def _everyone_plan(n):
    def plan(refs):
        x, y, c = _position()
        copies = []
        for k in range(1, N_DEVICES):
            px = 1 - x if (k >> 2) & 1 else x
            py = 1 - y if (k >> 1) & 1 else y
            pc = 1 - c if k & 1 else c
            for a in range(n):
                mine = refs[a].at[4 * x + 2 * y + c]
                copies.append((mine, mine, (px, py, pc), refs[a].at[4 * px + 2 * py + pc]))
        return copies

    return plan


def _small_adamw_call(all_a, all_b, params):
    n_small = len(SMALL_NAMES)
    wmv = [t for p in params for t in p]
    shapes = [p[0].shape for p in params]

    def body(*refs):
        all_a, all_b = refs[:2]
        wmv_refs = refs[2:2 + 3 * n_small]
        out_refs = refs[2 + 3 * n_small:]
        sum_a, sum_b = all_a[0], all_b[0]
        for d in range(1, N_DEVICES):
            sum_a = sum_a + all_a[d]
            sum_b = sum_b + all_b[d]
        gsum = [sum_a[0:1], sum_a[1:2], sum_a[2:3], sum_a[3:4], sum_a[4:5, 0:256], sum_a[5:6, 0:256],
                sum_a[6:7, 0:128], sum_a[7:8, 0:SWA_Q_HEADS], sum_b[0:32, 0:SWA_Q_HEADS]]
        for i in range(n_small):
            w_r, m_r, v_r = wmv_refs[3 * i:3 * i + 3]
            delta, new_m, new_v = _adamw_math(w_r[...], gsum[i], m_r[...], v_r[...])
            out_refs[4 * i][...] = gsum[i]
            out_refs[4 * i + 1][...] = delta
            out_refs[4 * i + 2][...] = new_m
            out_refs[4 * i + 3][...] = new_v
        out_refs[4 * n_small][...] = sum_b[32:48]
        out_refs[4 * n_small + 1][...] = sum_b[48:64]
        out_refs[4 * n_small + 2][...] = sum_a[7:8, 128:256]

    out_shape = [jax.ShapeDtypeStruct(s, F32) for s in shapes for _ in range(4)]
    out_shape += [jax.ShapeDtypeStruct((GLA_GATE_RANK, 256), F32)] * 2 + [jax.ShapeDtypeStruct((1, 128), F32)]
    out = pl.pallas_call(
        body, name="small_adamw",
        in_specs=[_whole_spec(a.shape) for a in [all_a, all_b] + wmv],
        out_specs=[_whole_spec(s.shape) for s in out_shape],
        out_shape=out_shape,
    )(all_a, all_b, *wmv)
    per_name = [tuple(out[4 * i:4 * i + 4]) for i in range(n_small)]
    return per_name, out[4 * n_small], out[4 * n_small + 1], out[4 * n_small + 2]


def _pad_heads(t, n_heads, axis=-1):
    axis = axis % t.ndim
    shape = t.shape
    t = t.reshape(shape[:axis] + (n_heads, 64) + shape[axis + 1:])
    pad = [(0, 0)] * t.ndim
    pad[axis + 1] = (0, HEAD_PAD - 64)
    return jnp.pad(t, pad).reshape(shape[:axis] + (n_heads * HEAD_PAD,) + shape[axis + 1:])


def _unpad_heads(t, n_heads, axis=-1):
    axis = axis % t.ndim
    shape = t.shape
    t = t.reshape(shape[:axis] + (n_heads, HEAD_PAD) + shape[axis + 1:])
    t = lax.slice_in_dim(t, 0, 64, axis=axis + 1)
    return t.reshape(shape[:axis] + (n_heads * 64,) + shape[axis + 1:])


def _pad_gate(w, first_row):
    return jnp.pad(_pad_heads(w, 4), ((first_row, 128 - GLA_GATE_RANK - first_row), (0, 0)))


def _own_slot(shard, chip):
    zone = lax.empty((N_CHIPS,) + shard.shape, shard.dtype)
    return lax.dynamic_update_slice(zone, shard[None], (chip,) + (0,) * shard.ndim)


def _reduce_to_owners(grads, axes, pos, tag, overlap):
    n = len(grads)

    def half_shape(g, axis):
        return (N_CHIPS, g.shape[1] // 2, g.shape[2]) if axis == ROWS else (N_CHIPS, g.shape[1], g.shape[2] // 2)

    lands = [lax.empty(half_shape(g, axis), F32) for g, axis in zip(grads, axes)]
    handle, token = _split_start(tag + "_pair_start", list(grads) + lands, n, _pair_swap_plan(axes))
    got = _split_wait(tag + "_pair_wait", handle, n, _pair_swap_plan(axes), overlap[0](token))
    sums = [_pair_add_call(got[a], got[n + a], pos, f"{tag}_pair_add{a}", axes[a]) for a in range(n)]
    lands = [lax.empty((3,) + s.shape[1:], s.dtype) for s in sums]
    handle, token = _split_start(tag + "_chip_start", sums + lands, 3 * n, _chip_swap_plan(n))
    got = _split_wait(tag + "_chip_wait", handle, 3 * n, _chip_swap_plan(n), overlap[1](token))
    halves = [_chip_add_call(got[a], got[n + a], pos, f"{tag}_chip_add{a}", axes[a]) for a in range(n)]
    handle, token = _split_start(tag + "_join_start", halves, n, _pair_join_plan(axes))
    return _split_wait(tag + "_join_wait", handle, n, _pair_join_plan(axes), overlap[2](token))


def kernel(x, norm_mix_pre, w_in, w_gate_up_fwd, b_gate_fwd, w_gate_up_bwd, b_gate_bwd, gla_norm, swa_sink, rel_bias, w_out, norm_mix_post, norm_mlp_pre, w_up, w_down, norm_mlp_post, loss_target, m_norm_mix_pre, m_w_in, m_w_gate_up_fwd, m_b_gate_fwd, m_w_gate_up_bwd, m_b_gate_bwd, m_gla_norm, m_swa_sink, m_rel_bias, m_w_out, m_norm_mix_post, m_norm_mlp_pre, m_w_up, m_w_down, m_norm_mlp_post, v_norm_mix_pre, v_w_in, v_w_gate_up_fwd, v_b_gate_fwd, v_w_gate_up_bwd, v_b_gate_bwd, v_gla_norm, v_swa_sink, v_rel_bias, v_w_out, v_norm_mix_post, v_norm_mlp_pre, v_w_up, v_w_down, v_norm_mlp_post):
    given = dict(locals())
    cx, cy, cc = _position()
    chip = (2 * cx + cy).astype(jnp.int32)
    pos = jnp.stack([chip, cc.astype(jnp.int32)])
    seq, tgt = x[0], loss_target[0]
    L = seq.shape[0]

    gates = jnp.concatenate([w_gate_up_fwd[0], w_gate_up_bwd[0]], axis=0).astype(COMM_DTYPE)
    all_in, all_gates = _first_gather_call([w_in[0].T.astype(COMM_DTYPE), gates], [COLS, ROWS], [True, False])
    rest = [w_out[0].astype(COMM_DTYPE), jnp.stack([w_up[0], w_down[0]]).astype(COMM_DTYPE)]
    stage_one, stage_two = _gather_plans([ROWS, ROWS])
    handle, token = _split_start("gather_chip_start", rest + [_own_slot(s, chip) for s in rest] + [all_gates], 6,
                                 stage_one)

    w_in_t = _mx(all_in.reshape(IN_COLS, D_MODEL))
    gates_full = jnp.concatenate([all_gates[j] for j in range(N_CHIPS)], axis=1)
    wgf_p = _mx(_pad_gate(gates_full[:GLA_GATE_RANK], 0))
    wgb_p = _mx(_pad_gate(gates_full[GLA_GATE_RANK:], GLA_GATE_RANK))
    bf_p, bb_p = _pad_heads(b_gate_fwd, 4), _pad_heads(b_gate_bwd, 4)
    buckets = jnp.asarray(_band_buckets())
    bias = _bias_call(rel_bias, buckets)
    sink1 = swa_sink.reshape(SWA_Q_HEADS)

    qa, ka, va, ga, qs, ks, vs, za = _proj_call(seq, norm_mix_pre, w_in_t, dep=token)
    halo = ((SWA_BLOCK, SWA_BLOCK), (0, 0))
    ks_p, vs_p = jnp.pad(ks, halo), jnp.pad(vs, halo)
    o_f, o_b, s_f, s_b = _gla_fwd_call(qa, ka, va, za, wgf_p, bf_p, wgb_p, bb_p)
    arrays = _split_wait("gather_chip_wait", handle, 6, stage_one, o_f)
    handle, token = _split_start("gather_pair_start", list(arrays), 6, stage_two)
    o_s = _swa_fwd_call(qs, ks_p, vs_p, bias, sink1, dep=token)
    arrays = _split_wait("gather_pair_wait", handle, 6, stage_two, o_s)
    w_out_full = _mx(arrays[2].reshape(N_CHIPS * R_OUT, D_MODEL))
    w_ud = _mx(arrays[3])
    cat, mix, h1, n2 = _mix_call(o_f, o_b, ga, o_s, seq, gla_norm, w_out_full, norm_mix_post, norm_mlp_pre)
    a, rz, dh2, dff, loss, d_post2 = _mlp_fwd_call(n2, h1, tgt, w_ud, norm_mlp_post)

    dz, dn2 = _mlp_bwd_call(dff, rz, w_ud)
    dw_down, dw_up4 = _mlp_wgrad_call(a, dff, n2, dz)
    dh1, do, dga, dos, dw_out, d_pre2, d_post, d_gn = _mix_bwd_call(
        dn2, dh2, h1, mix, cat, o_f, o_b, ga, gla_norm, norm_mix_post, norm_mlp_pre, w_out_full)
    done = {}

    def swa_backward(tok):
        done["swa"] = _swa_bwd_call(qs, ks_p, vs_p, bias, sink1, dos, dep=tok)
        return done["swa"][0]

    def gla_in_backward(tok):
        done["gla"] = _gla_bwd_call(qa, ka, va, za, do, s_f, s_b, wgf_p, bf_p, wgb_p, bb_p, dep=tok)
        dqf, dkf, dvf, dzf, _, _, dqb, dkb, dvb, dzb, _, _ = done["gla"]
        dqs, dks_p, dvs_p, _, _ = done["swa"]
        done["in"] = _in_bwd_call(
            seq, dh1, norm_mix_pre, w_in_t,
            pairs=[(_side_by_side(T_QA), (dqf, dqb)), (_side_by_side(T_KA), (dkf, dkb)), (T_VA, (dvf, dvb)),
                   (T_ZA, (dzf, dzb))],
            singles=[(T_GA, dga), (_side_by_side(T_QS), dqs)], halos=[(T_KS, dks_p), (T_VS, dvs_p)])
        return done["in"][0]

    def bias_backward(tok):
        done["rel"] = _relbias_call(done["swa"][3], done["swa"][4], buckets, dep=tok)
        return done["rel"][0]

    g_up, g_down, g_out = _reduce_to_owners(
        [dw_up4, dw_down.reshape(N_CHIPS, R_DOWN, D_MODEL), dw_out.reshape(N_CHIPS, R_OUT, D_MODEL)],
        [ROWS, ROWS, ROWS], pos, "mlp", [swa_backward, gla_in_backward, bias_backward])
    dx, dw_in_t, d_pre = done["in"]
    dwf, dbf, dwb, dbb = done["gla"][4], done["gla"][5], done["gla"][10], done["gla"][11]
    drel, dsink = done["rel"]

    small_grads = [d_pre, d_post, d_pre2, d_post2, _unpad_heads(dbf, 4), _unpad_heads(dbb, 4), d_gn, dsink, drel]
    gate_grads = [_unpad_heads(dwf[:GLA_GATE_RANK], 4), _unpad_heads(dwb[GLA_GATE_RANK:2 * GLA_GATE_RANK], 4)]
    small_params = [(given[n], given["m_" + n], given["v_" + n]) for n in SMALL_NAMES]
    upd = {}

    everyone = _everyone_plan(2)
    small_handle, small_token = _split_start(
        "small_start", list(_small_pack_call(small_grads, gate_grads + [loss])), 2 * (N_DEVICES - 1), everyone)

    def update_up(tok):
        upd["w_up"] = (g_up,) + tuple(_adamw_call(w_up[0], g_up, m_w_up[0], v_w_up[0], "adamw_w_up",
                                                  dep=tok + small_token))
        return upd["w_up"][1]

    def update_small(tok):
        all_a, all_b = _split_wait("small_wait", small_handle, 2 * (N_DEVICES - 1), everyone, tok)
        per_name, gf_sum, gb_sum, upd["loss"] = _small_adamw_call(all_a, all_b, small_params)
        upd.update(dict(zip(SMALL_NAMES, per_name)))
        for name, total in (("w_gate_up_fwd", gf_sum), ("w_gate_up_bwd", gb_sum)):
            g = lax.dynamic_slice(total, (0, chip * 64), (GLA_GATE_RANK, 64))
            upd[name] = (g,) + tuple(_adamw_call(given[name][0], g, given["m_" + name][0], given["v_" + name][0],
                                                 "adamw_" + name))
        upd["w_down"] = (g_down,) + tuple(
            _adamw_call(w_down[0], g_down, m_w_down[0], v_w_down[0], "adamw_w_down", dep=gf_sum))
        return upd["w_down"][1]

    def update_out(tok):
        upd["w_out"] = (g_out,) + tuple(_adamw_call(w_out[0], g_out, m_w_out[0], v_w_out[0], "adamw_w_out", dep=tok))
        return upd["w_out"][1]

    (g_in_t,) = _reduce_to_owners([dw_in_t.reshape(N_CHIPS, R_IN, D_MODEL)], [COLS], pos, "in",
                                  [update_up, update_small, update_out])
    in_t = (g_in_t,) + tuple(_adamw_call(w_in[0].T, g_in_t, m_w_in[0].T, v_w_in[0].T, "adamw_w_in"))
    upd["w_in"] = tuple(t.T for t in in_t)

    big = ("w_in", "w_gate_up_fwd", "w_gate_up_bwd", "w_out", "w_up", "w_down")
    names = ["norm_mix_pre", "w_in", "w_gate_up_fwd", "b_gate_fwd", "w_gate_up_bwd", "b_gate_bwd", "gla_norm",
             "swa_sink", "rel_bias", "w_out", "norm_mix_post", "norm_mlp_pre", "w_up", "w_down", "norm_mlp_post"]
    outs = [upd["loss"][0, 0], dx[None]]
    for kind in range(4):
        outs += [upd[n][kind][None] if n in big else upd[n][kind] for n in names]
    return tuple(outs)
```

```python
import math

import numpy as np
import jax
import jax.numpy as jnp
from jax import lax
from jax.experimental import pallas as pl
from jax.experimental.pallas import tpu as pltpu

F32 = jnp.float32
MXU_DTYPE = jnp.bfloat16
COMM_DTYPE = jnp.bfloat16

D_MODEL = 1024
D_FF = 4096
N_CHIPS = 4
GLA_HEADS = 4
GLA_CHUNK = 64
GLA_GATE_RANK = 16
GLA_GATE_NORM = 16.0
SWA_Q_HEADS = 8
SWA_KV_HEADS = 2
SWA_BLOCK = 128
REL_BUCKETS = 32
REL_MAX_DIST = 128
NORM_EPS = 1e-6
HEAD_PAD = 128

ADAM_LR = 0.001
ADAM_B1 = 0.9
ADAM_B2 = 0.999
ADAM_EPS = 1e-08
ADAM_WD = 0.01
ADAM_STEP = 10

OUT_PAD = 1024

R_IN, R_OUT, R_UP, R_DOWN = 584, 256, 1024, 1024

VMEM_BIG = 56 * 1024 * 1024
MESH_AXES = ("x", "y", "c")
MESH_ID = pl.DeviceIdType.MESH


def _mx(a):
    return a.astype(MXU_DTYPE)


def _dot(a, b):
    return jnp.dot(a, b, preferred_element_type=F32)


def _dot_nt(a, b):
    return lax.dot_general(a, b, (((1,), (1,)), ((), ())), preferred_element_type=F32)


def _dot_tn(a, b):
    return lax.dot_general(a, b, (((0,), (0,)), ((), ())), preferred_element_type=F32)


def _rms_r(x):
    return lax.rsqrt(jnp.mean(x * x, axis=-1, keepdims=True) + NORM_EPS)


def _rms_bwd(x, r, g, dy):
    xh = x * r
    gdy = dy * g
    dx = r * (gdy - xh * jnp.mean(gdy * xh, axis=-1, keepdims=True))
    return dx, jnp.sum(dy * xh, axis=0, keepdims=True)


def _low_half(rows):
    return lax.broadcasted_iota(jnp.int32, (rows, HEAD_PAD), 1) < 64


def _spread_heads(x):
    low = _low_half(x.shape[0])
    parts = []
    for p in range(x.shape[1] // HEAD_PAD):
        pair = x[:, HEAD_PAD * p:HEAD_PAD * (p + 1)]
        parts += [jnp.where(low, pair, 0.0), jnp.where(low, pltpu.roll(pair, 64, 1), 0.0)]
    return jnp.concatenate(parts, axis=1)


def _squeeze_heads(x):
    low = _low_half(x.shape[0])
    parts = []
    for p in range(x.shape[1] // (2 * HEAD_PAD)):
        even = x[:, 2 * HEAD_PAD * p:2 * HEAD_PAD * p + HEAD_PAD]
        odd = x[:, 2 * HEAD_PAD * p + HEAD_PAD:2 * HEAD_PAD * (p + 1)]
        parts.append(jnp.where(low, even, pltpu.roll(odd, 64, 1)))
    return parts[0] if len(parts) == 1 else jnp.concatenate(parts, axis=1)


def _params(sem=None, vmem=None):
    kw = {}
    if sem is not None:
        kw["dimension_semantics"] = sem
    if vmem is not None:
        kw["vmem_limit_bytes"] = vmem
    return pltpu.CompilerParams(**kw)


def _vmem_spec():
    return pl.BlockSpec(memory_space=pltpu.VMEM)


def _whole_spec(shape):
    return pl.BlockSpec(shape, lambda: (0,) * len(shape))


def _row_spec(tm, width):
    return pl.BlockSpec((tm, width), lambda i: (i, 0))


def _full_spec(shape):
    return pl.BlockSpec(shape, lambda i: (0,) * len(shape))


def _any_spec():
    return pl.BlockSpec(memory_space=pl.ANY)


def _after(body, n_in, dep):
    if dep is None:
        return body, [], []
    return (lambda *refs: body(*refs[:n_in], *refs[n_in + 1:])), [dep], [_any_spec()]


T_QA, T_KA, T_VA, T_GA = (0, 256, 4), (256, 256, 4), (512, 512, 0), (1024, 512, 0)
T_QS, T_KS, T_VS = (1568, 512, 8), (2080, 128, 2), (2208, 128, 2)
T_ZA = (1536, 128, 0)
ZA_COLS = 2 * GLA_GATE_RANK
IN_COLS = 2336


def _side_by_side(group):
    return group[0], group[1], 0


def _proj_call(x, g_pre, w_in_t, dep=None):
    L = x.shape[0]
    tm = min(512, L)
    groups = [(T_QA, F32), (T_KA, F32), (T_VA, MXU_DTYPE), (T_GA, F32),
              (T_QS, MXU_DTYPE), (T_KS, MXU_DTYPE), (T_VS, MXU_DTYPE), (T_ZA, F32)]
    widths = [rows * (2 if heads else 1) for (_, rows, heads), _ in groups]

    def body(x_ref, g_ref, w_ref, *outs):
        xv = x_ref[...]
        u = _mx(xv * _rms_r(xv) * g_ref[...])
        for ref, (grp, _) in zip(outs, groups):
            first, rows, heads = grp
            val = _dot_nt(u, w_ref[first:first + rows, :])
            if heads:
                val = _spread_heads(val)
            if grp is T_ZA:
                val = jnp.where(lax.broadcasted_iota(jnp.int32, val.shape, 1) < ZA_COLS, val, 0.0)
            if grp is T_QS:
                val = val * 0.125
            ref[...] = val.astype(ref.dtype)

    body, extra, extra_specs = _after(body, 3, dep)
    return pl.pallas_call(
        body, name="proj_fwd", grid=(L // tm,),
        in_specs=[_row_spec(tm, D_MODEL), _full_spec((1, D_MODEL)), _vmem_spec()] + extra_specs,
        out_specs=[_row_spec(tm, w) for w in widths],
        out_shape=[jax.ShapeDtypeStruct((L, w), dt) for w, (_, dt) in zip(widths, groups)],
        compiler_params=_params(("arbitrary",), VMEM_BIG),
    )(x, g_pre, w_in_t, *extra)


def _tri_masks():
    row = lax.broadcasted_iota(jnp.int32, (GLA_CHUNK, GLA_CHUNK), 0)
    col = lax.broadcasted_iota(jnp.int32, (GLA_CHUNK, GLA_CHUNK), 1)
    return row >= col, row <= col


def _chunk_sums(tri_m, x):
    hi = _mx(x)
    rest = x - hi.astype(F32)
    mid = _mx(rest)
    lo = _mx(rest - mid.astype(F32))
    return _dot(tri_m, hi) + _dot(tri_m, mid) + _dot(tri_m, lo)


def _gla_block_pre(q_r, k_r, z_r, w_r, b_r, rev, nc, qd_s, ki_s, ks_s, dec_s, keep=None):
    tri_f, tri_b = _tri_masks()
    tri_m = _mx((tri_b if rev else tri_f).astype(F32))
    g = _dot(_mx(z_r[...]), w_r[...]) + b_r[...]
    la = (jnp.minimum(g, 0.0) - jnp.log(1.0 + jnp.exp(-jnp.abs(g)))) / GLA_GATE_NORM
    sums, lasts = [], []
    for c in range(nc):
        b_c = _chunk_sums(tri_m, la[GLA_CHUNK * c:GLA_CHUNK * (c + 1)])
        blast = b_c[0:1] if rev else b_c[GLA_CHUNK - 1:GLA_CHUNK]
        dec_s[c] = jnp.exp(blast)
        sums.append(b_c)
        lasts.append(jnp.broadcast_to(blast, b_c.shape))
    b = jnp.concatenate(sums, axis=0)
    eb = jnp.exp(b)
    enb = jnp.exp(-b)
    elb = jnp.exp(jnp.concatenate(lasts, axis=0) - b)
    k = k_r[...]
    qd_s[...] = (q_r[...] * 0.125 * eb).astype(qd_s.dtype)
    ki_s[...] = (k * enb).astype(ki_s.dtype)
    ks_s[...] = (k * elb).astype(ks_s.dtype)
    if keep is not None:
        for ref, val in zip(keep, (g, eb, enb, elb)):
            ref[...] = val


def _gla_fwd_call(qa, ka, va, za, wgf, bgf, wgb, bgb):
    L = qa.shape[0]
    br = min(512, L)
    nb, nc, n_chunks = L // br, br // GLA_CHUNK, L // GLA_CHUNK
    hw = GLA_HEADS * HEAD_PAD

    def body(qaf, kaf, vaf, zaf, qab, kab, vab, zab, wgf_r, bgf_r, wgb_r, bgb_r,
             of_r, ob_r, sf_r, sb_r, st_f, st_b, pre_f, pre_b):
        @pl.when(pl.program_id(0) == 0)
        def _():
            st_f[...] = jnp.zeros_like(st_f)
            st_b[...] = jnp.zeros_like(st_b)

        _gla_block_pre(qaf, kaf, zaf, wgf_r, bgf_r, False, nc, *pre_f)
        _gla_block_pre(qab, kab, zab, wgb_r, bgb_r, True, nc, *pre_b)
        tri_f, tri_b = _tri_masks()

        def one(tri, pre, v_r, o_r, s_r, st, ci):
            qd_s, ki_s, ks_s, dec_s = pre
            rows = pl.ds(pl.multiple_of(ci * GLA_CHUNK, GLA_CHUNK), GLA_CHUNK)
            dec = dec_s[ci]
            heads = range(GLA_HEADS)
            lanes = [slice(HEAD_PAD * h, HEAD_PAD * (h + 1)) for h in heads]
            qd = [qd_s[rows, sl] for sl in lanes]
            v = [v_r[rows, sl] for sl in lanes]
            s_t = [st[h] for h in heads]
            a = [_dot_nt(qd[h], ki_s[rows, lanes[h]]) for h in heads]
            carried = [_dot_nt(qd[h], _mx(s_t[h])) for h in heads]
            grown = [_dot_tn(v[h], ks_s[rows, lanes[h]]) for h in heads]
            a = [_mx(jnp.where(tri, a[h], 0.0)) for h in heads]
            inner = [_dot(a[h], v[h]) for h in heads]
            for h in heads:
                s_r[ci, h] = s_t[h].astype(s_r.dtype)
                o_r[rows, lanes[h]] = inner[h] + carried[h]
                st[h] = s_t[h] * dec[:, lanes[h]] + grown[h]

        def loop(t, carry):
            one(tri_f, pre_f, vaf, of_r, sf_r, st_f, t)
            one(tri_b, pre_b, vab, ob_r, sb_r, st_b, nc - 1 - t)
            return carry

        lax.fori_loop(0, nc, loop, 0, unroll=True)

    fwd = lambda i: (i, 0)
    bwd = lambda i: (nb - 1 - i, 0)
    ins = lambda m: [pl.BlockSpec((br, hw), m), pl.BlockSpec((br, hw), m),
                     pl.BlockSpec((br, hw), m), pl.BlockSpec((br, 128), m)]
    wspecs = [_full_spec((128, hw)), _full_spec((1, hw))] * 2
    s_shape = (nc, GLA_HEADS, HEAD_PAD, HEAD_PAD)
    pre_scratch = [pltpu.VMEM((br, hw), MXU_DTYPE)] * 3 + [pltpu.VMEM((nc, 1, hw), F32)]
    return pl.pallas_call(
        body, name="gla_fwd", grid=(nb,),
        in_specs=ins(fwd) + ins(bwd) + wspecs,
        out_specs=[pl.BlockSpec((br, hw), fwd), pl.BlockSpec((br, hw), bwd),
                   pl.BlockSpec(s_shape, lambda i: (i, 0, 0, 0)),
                   pl.BlockSpec(s_shape, lambda i: (nb - 1 - i, 0, 0, 0))],
        out_shape=[jax.ShapeDtypeStruct((L, hw), F32), jax.ShapeDtypeStruct((L, hw), F32),
                   jax.ShapeDtypeStruct((n_chunks,) + s_shape[1:], MXU_DTYPE),
                   jax.ShapeDtypeStruct((n_chunks,) + s_shape[1:], MXU_DTYPE)],
        scratch_shapes=[pltpu.VMEM(s_shape[1:], F32), pltpu.VMEM(s_shape[1:], F32), pre_scratch, pre_scratch],
        compiler_params=_params(("arbitrary",), VMEM_BIG),
    )(qa, ka, va, za, qa, ka, va, za, wgf, bgf, wgb, bgb)


def _gla_bwd_call(qa, ka, va, za, do, sf, sb, wgf, bgf, wgb, bgb, dep=None):
    L = qa.shape[0]
    br = min(512, L)
    nb, nc = L // br, br // GLA_CHUNK
    hw = GLA_HEADS * HEAD_PAD

    def body(qaf, kaf, vaf, zaf, dof, sf_r, qab, kab, vab, zab, dob, sb_r, wgf_r, bgf_r, wgb_r, bgb_r,
             dqf, dkf, dvf, dzf, dwf, dbf, dqb, dkb, dvb, dzb, dwb, dbb, gt_f, gt_b, pre_f, pre_b):
        @pl.when(pl.program_id(0) == 0)
        def _():
            for ref in (gt_f, gt_b, dwf, dbf, dwb, dbb):
                ref[...] = jnp.zeros_like(ref)

        _gla_block_pre(qaf, kaf, zaf, wgf_r, bgf_r, False, nc, *pre_f[:4], keep=pre_f[4:8])
        _gla_block_pre(qab, kab, zab, wgb_r, bgb_r, True, nc, *pre_b[:4], keep=pre_b[4:8])
        tri_f, tri_b = _tri_masks()
        row_w = lax.broadcasted_iota(jnp.int32, (GLA_CHUNK, HEAD_PAD), 0)

        def one(rev, pre, q_r, k_r, v_r, do_r, s_r, dq_r, dk_r, dv_r, gt, ci):
            qd_s, ki_s, ks_s, dec_s, _, eb_s, enb_s, elb_s, db_s = pre
            tri = tri_b if rev else tri_f
            last_row = 0 if rev else GLA_CHUNK - 1
            rows = pl.ds(pl.multiple_of(ci * GLA_CHUNK, GLA_CHUNK), GLA_CHUNK)
            dec = dec_s[ci]
            heads = range(GLA_HEADS)
            lanes = [slice(HEAD_PAD * h, HEAD_PAD * (h + 1)) for h in heads]
            qd = [qd_s[rows, sl] for sl in lanes]
            ki = [ki_s[rows, sl] for sl in lanes]
            ks = [ks_s[rows, sl] for sl in lanes]
            v = [v_r[rows, sl] for sl in lanes]
            do_h = [_mx(do_r[rows, sl]) for sl in lanes]
            s_t = [s_r[ci, h] for h in heads]
            g_t = [gt[h] for h in heads]
            g_m = [_mx(g_t[h]) for h in heads]
            a = [_dot_nt(qd[h], ki[h]) for h in heads]
            da = [_dot_nt(do_h[h], v[h]) for h in heads]
            dv_carried = [_dot_nt(ks[h], g_m[h]) for h in heads]
            dqd_carried = [_dot(do_h[h], _mx(s_t[h])) for h in heads]
            dks = [_dot(v[h], g_m[h]) for h in heads]
            g_grown = [_dot_tn(do_h[h], qd[h]) for h in heads]
            a = [_mx(jnp.where(tri, a[h], 0.0)) for h in heads]
            da = [_mx(jnp.where(tri, da[h], 0.0)) for h in heads]
            dv_inner = [_dot_tn(a[h], do_h[h]) for h in heads]
            dqd_inner = [_dot(da[h], ki[h]) for h in heads]
            dki = [_dot_tn(da[h], qd[h]) for h in heads]
            dq, dk = [], []
            for h in heads:
                sl = lanes[h]
                dv_r[rows, sl] = (dv_inner[h] + dv_carried[h]).astype(dv_r.dtype)
                ddec = jnp.sum(g_t[h] * s_t[h].astype(F32), axis=0, keepdims=True)
                gt[h] = g_t[h] * dec[:, sl] + g_grown[h]
                dq.append((dqd_inner[h] + dqd_carried[h]) * eb_s[rows, sl] * 0.125)
                dk_state = dks[h] * elb_s[rows, sl]
                dk.append(dki[h] * enb_s[rows, sl] + dk_state)
                k = k_r[rows, sl]
                dblast = jnp.sum(dk_state * k, axis=0, keepdims=True) + dec[:, sl] * ddec
                db_s[rows, sl] = q_r[rows, sl] * dq[h] - k * dk[h] + jnp.where(row_w == last_row, dblast, 0.0)
            low = _low_half(GLA_CHUNK)
            for pair in range(GLA_HEADS // 2):
                psl = slice(HEAD_PAD * pair, HEAD_PAD * (pair + 1))
                for ref, val in ((dq_r, dq), (dk_r, dk)):
                    both = jnp.where(low, val[2 * pair], pltpu.roll(val[2 * pair + 1], 64, 1))
                    ref[rows, psl] = both.astype(ref.dtype)

        def loop(t, carry):
            one(False, pre_f, qaf, kaf, vaf, dof, sf_r, dqf, dkf, dvf, gt_f, nc - 1 - t)
            one(True, pre_b, qab, kab, vab, dob, sb_r, dqb, dkb, dvb, gt_b, t)
            return carry

        lax.fori_loop(0, nc, loop, 0, unroll=True)

        def gate_grads(rev, pre, z_r, w_r, dz_r, dw_r, dbias_r):
            g_s, db_s = pre[4], pre[8]
            back_m = _mx((tri_f if rev else tri_b).astype(F32))
            db = db_s[...]
            dla = jnp.concatenate([_chunk_sums(back_m, db[GLA_CHUNK * c:GLA_CHUNK * (c + 1)]) for c in range(nc)],
                                  axis=0)
            dg = dla * (1.0 / GLA_GATE_NORM) * (1.0 / (1.0 + jnp.exp(g_s[...])))
            dg_m = _mx(dg)
            dz_r[...] = _dot_nt(dg_m, w_r[...])
            dw_r[...] += _dot_tn(_mx(z_r[...]), dg_m)
            dbias_r[...] += jnp.sum(dg, axis=0, keepdims=True)

        gate_grads(False, pre_f, zaf, wgf_r, dzf, dwf, dbf)
        gate_grads(True, pre_b, zab, wgb_r, dzb, dwb, dbb)

    last_first = lambda i: (nb - 1 - i, 0)
    first_last = lambda i: (i, 0)
    s_shape = (nc, GLA_HEADS, HEAD_PAD, HEAD_PAD)

    def ins(m):
        return [pl.BlockSpec((br, hw), m), pl.BlockSpec((br, hw), m), pl.BlockSpec((br, hw), m),
                pl.BlockSpec((br, 128), m), pl.BlockSpec((br, hw), m),
                pl.BlockSpec(s_shape, lambda i: m(i) + (0, 0))]

    def outs(m):
        return [pl.BlockSpec((br, hw // 2), m), pl.BlockSpec((br, hw // 2), m), pl.BlockSpec((br, hw), m),
                pl.BlockSpec((br, 128), m), _full_spec((128, hw)), _full_spec((1, hw))]

    out_shape = [jax.ShapeDtypeStruct((L, hw // 2), MXU_DTYPE)] * 2 + [
        jax.ShapeDtypeStruct((L, hw), MXU_DTYPE),
        jax.ShapeDtypeStruct((L, 128), F32), jax.ShapeDtypeStruct((128, hw), F32),
        jax.ShapeDtypeStruct((1, hw), F32)]
    wspecs = [_full_spec((128, hw)), _full_spec((1, hw))] * 2
    body, extra, extra_specs = _after(body, 16, dep)
    pre_scratch = ([pltpu.VMEM((br, hw), MXU_DTYPE)] * 3 + [pltpu.VMEM((nc, 1, hw), F32)]
                   + [pltpu.VMEM((br, hw), F32)] * 5)
    return pl.pallas_call(
        body, name="gla_bwd", grid=(nb,),
        in_specs=ins(last_first) + ins(first_last) + wspecs + extra_specs,
        out_specs=outs(last_first) + outs(first_last),
        out_shape=out_shape + out_shape,
        scratch_shapes=[pltpu.VMEM(s_shape[1:], F32), pltpu.VMEM(s_shape[1:], F32), pre_scratch, pre_scratch],
        compiler_params=_params(("arbitrary",), VMEM_BIG),
    )(qa, ka, va, za, do, sf, qa, ka, va, za, do, sb, wgf, bgf, wgb, bgb, *extra)


def _t5_buckets(rel):
    nb = REL_BUCKETS // 2
    ret = (rel > 0).astype(np.int32) * nb
    n = np.abs(rel)
    max_exact = nb // 2
    large = max_exact + (np.log(np.maximum(n, 1).astype(np.float32) / max_exact)
                         / math.log(REL_MAX_DIST / max_exact) * (nb - max_exact)).astype(np.int32)
    large = np.minimum(large, nb - 1)
    return ret + np.where(n < max_exact, n, large)


SWA_GROUP = SWA_Q_HEADS // SWA_KV_HEADS
SWA_SPAN = 3 * SWA_BLOCK
SWA_GROUP_LANES = SWA_GROUP * SWA_BLOCK


def _band_buckets():
    s = np.arange(SWA_SPAN)[:, None]
    c = np.arange(SWA_BLOCK)[None, :]
    return _t5_buckets(s - SWA_BLOCK - c).astype(np.int32)


def _swa_valid(n, seq_len):
    key_pos = (n - 1) * SWA_BLOCK + lax.broadcasted_iota(jnp.int32, (SWA_SPAN, 1), 0)
    return (key_pos >= 0) & (key_pos < seq_len)


def _swa_sink_row(sink_r, kv):
    lane = lax.broadcasted_iota(jnp.int32, (1, SWA_GROUP_LANES), 1)
    row = jnp.full((1, SWA_GROUP_LANES), sink_r[kv * SWA_GROUP], F32)
    for g in range(1, SWA_GROUP):
        row = jnp.where(lane >= g * SWA_BLOCK, sink_r[kv * SWA_GROUP + g], row)
    return row


def _swa_group(ref, kv):
    first = kv * SWA_GROUP
    return jnp.concatenate([ref[:, HEAD_PAD * h:HEAD_PAD * (h + 1)] for h in range(first, first + SWA_GROUP)],
                           axis=0)


def _swa_softmax(scores, bias_t, sink_row, valid):
    st = jnp.where(valid, scores + bias_t, -1e30)
    m = jnp.maximum(jnp.max(st, axis=0, keepdims=True), sink_row)
    p = jnp.exp(st - m)
    e_sink = jnp.exp(sink_row - m)
    inv = 1.0 / (jnp.sum(p, axis=0, keepdims=True) + e_sink)
    return p * inv, e_sink * inv


def _swa_fwd_call(qs, ks, vs, bias, sink, dep=None):
    L = qs.shape[0]

    def body(q_r, k_r, v_r, bias_r, sink_r, o_r):
        n = pl.program_id(0)
        span = pl.ds(pl.multiple_of(n * SWA_BLOCK, SWA_BLOCK), SWA_SPAN)
        valid = _swa_valid(n, L)
        groups = range(SWA_KV_HEADS)
        lanes = [slice(HEAD_PAD * kv, HEAD_PAD * (kv + 1)) for kv in groups]
        scores = [_dot_nt(k_r[span, lanes[kv]], _swa_group(q_r, kv)) for kv in groups]
        probs = [_swa_softmax(scores[kv], bias_r[kv], _swa_sink_row(sink_r, kv), valid)[0] for kv in groups]
        low = _low_half(SWA_BLOCK)
        for kv in groups:
            og = _dot_tn(_mx(probs[kv]), v_r[span, lanes[kv]])
            for pair in range(SWA_GROUP // 2):
                even = og[2 * SWA_BLOCK * pair:2 * SWA_BLOCK * pair + SWA_BLOCK]
                odd = og[2 * SWA_BLOCK * pair + SWA_BLOCK:2 * SWA_BLOCK * (pair + 1)]
                first = HEAD_PAD * (kv * SWA_GROUP // 2 + pair)
                o_r[:, first:first + HEAD_PAD] = jnp.where(low, even, pltpu.roll(odd, 64, 1)).astype(o_r.dtype)

    qw = SWA_Q_HEADS * HEAD_PAD
    body, extra, extra_specs = _after(body, 5, dep)
    return pl.pallas_call(
        body, name="swa_fwd", grid=(L // SWA_BLOCK,),
        in_specs=[_row_spec(SWA_BLOCK, qw), _vmem_spec(), _vmem_spec(), _vmem_spec(),
                  pl.BlockSpec(memory_space=pltpu.SMEM)] + extra_specs,
        out_specs=_row_spec(SWA_BLOCK, qw // 2),
        out_shape=jax.ShapeDtypeStruct((L, qw // 2), MXU_DTYPE),
        compiler_params=_params(("arbitrary",), VMEM_BIG),
    )(qs, ks, vs, bias, sink, *extra)


def _swa_bwd_call(qs, ks, vs, bias, sink, do, dep=None):
    L = qs.shape[0]
    qw = SWA_Q_HEADS * HEAD_PAD
    kw = SWA_KV_HEADS * HEAD_PAD

    def body(q_r, k_r, v_r, bias_r, sink_r, do_r, dq_r, dk_r, dv_r, dbias_r, dsink_r):
        n = pl.program_id(0)

        @pl.when(n == 0)
        def _():
            for ref in (dk_r, dv_r, dbias_r, dsink_r):
                ref[...] = jnp.zeros_like(ref)

        span = pl.ds(pl.multiple_of(n * SWA_BLOCK, SWA_BLOCK), SWA_SPAN)
        valid = _swa_valid(n, L)
        groups = range(SWA_KV_HEADS)
        lanes = [slice(HEAD_PAD * kv, HEAD_PAD * (kv + 1)) for kv in groups]
        kk = [k_r[span, sl] for sl in lanes]
        vv = [v_r[span, sl] for sl in lanes]
        qg = [_swa_group(q_r, kv) for kv in groups]
        dog = [_swa_group(do_r, kv) for kv in groups]
        scores = [_dot_nt(kk[kv], qg[kv]) for kv in groups]
        dp = [_dot_nt(vv[kv], dog[kv]) for kv in groups]
        probs = [_swa_softmax(scores[kv], bias_r[kv], _swa_sink_row(sink_r, kv), valid) for kv in groups]
        ds_m, pn_m = [], []
        for kv in groups:
            pn, p_sink = probs[kv]
            delta = jnp.sum(pn * dp[kv], axis=0, keepdims=True)
            ds = pn * (dp[kv] - delta)
            dsink_r[kv] -= p_sink * delta
            dbias_r[kv] += ds
            ds_m.append(_mx(ds))
            pn_m.append(_mx(pn))
        dqg = [_dot_tn(ds_m[kv], kk[kv]) * 0.125 for kv in groups]
        dkk = [_dot(ds_m[kv], qg[kv]) for kv in groups]
        dvv = [_dot(pn_m[kv], dog[kv]) for kv in groups]
        low = _low_half(SWA_BLOCK)
        for kv in groups:
            for pair in range(SWA_GROUP // 2):
                even = dqg[kv][2 * SWA_BLOCK * pair:2 * SWA_BLOCK * pair + SWA_BLOCK]
                odd = dqg[kv][2 * SWA_BLOCK * pair + SWA_BLOCK:2 * SWA_BLOCK * (pair + 1)]
                first = HEAD_PAD * (kv * SWA_GROUP // 2 + pair)
                dq_r[:, first:first + HEAD_PAD] = jnp.where(low, even, pltpu.roll(odd, 64, 1)).astype(dq_r.dtype)
            dk_r[span, lanes[kv]] += dkk[kv]
            dv_r[span, lanes[kv]] += dvv[kv]

    body, extra, extra_specs = _after(body, 6, dep)
    return pl.pallas_call(
        body, name="swa_bwd", grid=(L // SWA_BLOCK,),
        in_specs=[_row_spec(SWA_BLOCK, qw), _vmem_spec(), _vmem_spec(), _vmem_spec(),
                  pl.BlockSpec(memory_space=pltpu.SMEM), _row_spec(SWA_BLOCK, qw)] + extra_specs,
        out_specs=[_row_spec(SWA_BLOCK, qw // 2), _vmem_spec(), _vmem_spec(), _vmem_spec(), _vmem_spec()],
        out_shape=[jax.ShapeDtypeStruct((L, qw // 2), MXU_DTYPE),
                   jax.ShapeDtypeStruct((L + 2 * SWA_BLOCK, kw), F32),
                   jax.ShapeDtypeStruct((L + 2 * SWA_BLOCK, kw), F32),
                   jax.ShapeDtypeStruct((SWA_KV_HEADS, SWA_SPAN, SWA_GROUP_LANES), F32),
                   jax.ShapeDtypeStruct((SWA_KV_HEADS, 1, SWA_GROUP_LANES), F32)],
        compiler_params=_params(("arbitrary",), VMEM_BIG),
    )(qs, ks, vs, bias, sink, do, *extra)


def _bias_call(rel_bias, buckets):
    def body(t_r, bk_r, o_r):
        bk = bk_r[...]
        s = lax.broadcasted_iota(jnp.int32, bk.shape, 0)
        c = lax.broadcasted_iota(jnp.int32, bk.shape, 1)
        in_band = jnp.abs(s - SWA_BLOCK - c) <= SWA_BLOCK
        for h in range(SWA_Q_HEADS):
            acc = jnp.zeros(bk.shape, F32)
            for b in range(REL_BUCKETS):
                acc = jnp.where(bk == b, t_r[b, h], acc)
            g = h % SWA_GROUP
            o_r[h // SWA_GROUP, :, SWA_BLOCK * g:SWA_BLOCK * (g + 1)] = jnp.where(in_band, acc, -1e30)

    return pl.pallas_call(
        body, name="band_bias",
        in_specs=[pl.BlockSpec(memory_space=pltpu.SMEM), _vmem_spec()], out_specs=_vmem_spec(),
        out_shape=jax.ShapeDtypeStruct((SWA_KV_HEADS, SWA_SPAN, SWA_GROUP_LANES), F32),
    )(rel_bias, buckets)


def _relbias_call(dbias, dsink, buckets, dep=None):
    def body(db_r, ds_r, bk_r, o_r, os_r):
        bk = bk_r[...]
        rowi = lax.broadcasted_iota(jnp.int32, (REL_BUCKETS, 128), 0)
        lanei = lax.broadcasted_iota(jnp.int32, (REL_BUCKETS, 128), 1)
        lane1 = lax.broadcasted_iota(jnp.int32, (1, 128), 1)
        acc = jnp.zeros((REL_BUCKETS, 128), F32)
        acc_sink = jnp.zeros((1, 128), F32)
        for h in range(SWA_Q_HEADS):
            kv, g = h // SWA_GROUP, h % SWA_GROUP
            lanes = slice(SWA_BLOCK * g, SWA_BLOCK * (g + 1))
            part = db_r[kv, :, lanes]
            for b in range(REL_BUCKETS):
                s = jnp.sum(jnp.where(bk == b, part, 0.0))
                acc = acc + jnp.where((rowi == b) & (lanei == h), s, 0.0)
            acc_sink = acc_sink + jnp.where(lane1 == h, jnp.sum(ds_r[kv, :, lanes]), 0.0)
        o_r[...] = acc
        os_r[...] = acc_sink

    body, extra, extra_specs = _after(body, 3, dep)
    return pl.pallas_call(
        body, name="relbias_grad",
        in_specs=[_vmem_spec()] * 3 + extra_specs, out_specs=[_vmem_spec()] * 2,
        out_shape=[jax.ShapeDtypeStruct((REL_BUCKETS, 128), F32), jax.ShapeDtypeStruct((1, 128), F32)],
    )(dbias, dsink, buckets, *extra)


def _mix_call(o_f, o_b, ga, o_s, x, gn, w_out_p, g_post, g_pre2, dep=None):
    L = x.shape[0]
    tm = min(512, L)
    hw = GLA_HEADS * HEAD_PAD

    def body(of_r, ob_r, ga_r, os_r, x_r, gn_r, w_r, gp_r, g2_r, cat_r, mix_r, h1_r, n2_r):
        gn_v = gn_r[...]
        for h in range(GLA_HEADS):
            sl = slice(HEAD_PAD * h, HEAD_PAD * (h + 1))
            oh = of_r[:, sl] + ob_r[:, sl]
            on = oh * _rms_r(oh) * gn_v
            gate = ga_r[:, sl]
            cat_r[:, sl] = (on * (gate * jax.nn.sigmoid(gate))).astype(cat_r.dtype)
        os_v = os_r[...]
        cat_r[:, hw:] = os_v
        mix = _dot(cat_r[:, :hw], w_r[:hw, :]) + _dot(os_v, w_r[hw:, :])
        mix_r[...] = mix
        h1 = x_r[...] + mix * _rms_r(mix) * gp_r[...]
        h1_r[...] = h1
        n2_r[...] = (h1 * _rms_r(h1) * g2_r[...]).astype(n2_r.dtype)

    body, extra, extra_specs = _after(body, 9, dep)
    return pl.pallas_call(
        body, name="mix_fwd", grid=(L // tm,),
        in_specs=[_row_spec(tm, hw), _row_spec(tm, hw), _row_spec(tm, hw), _row_spec(tm, OUT_PAD - hw),
                  _row_spec(tm, D_MODEL), _full_spec((1, HEAD_PAD)), _vmem_spec(),
                  _full_spec((1, D_MODEL)), _full_spec((1, D_MODEL))] + extra_specs,
        out_specs=[_row_spec(tm, OUT_PAD), _row_spec(tm, D_MODEL), _row_spec(tm, D_MODEL), _row_spec(tm, D_MODEL)],
        out_shape=[jax.ShapeDtypeStruct((L, OUT_PAD), MXU_DTYPE), jax.ShapeDtypeStruct((L, D_MODEL), F32),
                   jax.ShapeDtypeStruct((L, D_MODEL), F32), jax.ShapeDtypeStruct((L, D_MODEL), MXU_DTYPE)],
        compiler_params=_params(("arbitrary",), VMEM_BIG),
    )(o_f, o_b, ga, o_s, x, gn, w_out_p, g_post, g_pre2, *extra)


def _mlp_fwd_call(n2, h1, tgt, w_ud, g_post):
    L = n2.shape[0]
    tm = min(512, L)
    blk = D_FF // N_CHIPS

    def body(n2_r, h1_r, t_r, w_r, g_r, a_r, rz_r, dh2_r, dff_r, loss_r, dg_r):
        @pl.when(pl.program_id(0) == 0)
        def _():
            loss_r[...] = jnp.zeros_like(loss_r)
            dg_r[...] = jnp.zeros_like(dg_r)

        n2v = n2_r[...]
        ff = jnp.zeros((tm, D_MODEL), F32)
        for j in range(N_CHIPS):
            sl = slice(blk * j, blk * (j + 1))
            rz = jnp.maximum(_dot(n2v, w_r[j, 0]), 0.0)
            a = _mx(rz * rz)
            rz_r[:, sl] = rz.astype(rz_r.dtype)
            a_r[:, sl] = a
            ff = ff + _dot(a, w_r[j, 1])
        g = g_r[...]
        r = _rms_r(ff)
        err = h1_r[...] + ff * r * g - t_r[...]
        loss_r[...] += 0.5 * jnp.sum(err * err) / D_MODEL
        dh2 = err * (1.0 / D_MODEL)
        dh2_r[...] = dh2
        dff, dg = _rms_bwd(ff, r, g, dh2)
        dff_r[...] = dff.astype(dff_r.dtype)
        dg_r[...] += dg

    return pl.pallas_call(
        body, name="mlp_fwd", grid=(L // tm,),
        in_specs=[_row_spec(tm, D_MODEL), _row_spec(tm, D_MODEL), _row_spec(tm, D_MODEL),
                  _vmem_spec(), _full_spec((1, D_MODEL))],
        out_specs=[_row_spec(tm, D_FF), _row_spec(tm, D_FF), _row_spec(tm, D_MODEL), _row_spec(tm, D_MODEL),
                   _full_spec((1, 128)), _full_spec((1, D_MODEL))],
        out_shape=[jax.ShapeDtypeStruct((L, D_FF), MXU_DTYPE), jax.ShapeDtypeStruct((L, D_FF), MXU_DTYPE),
                   jax.ShapeDtypeStruct((L, D_MODEL), F32), jax.ShapeDtypeStruct((L, D_MODEL), MXU_DTYPE),
                   jax.ShapeDtypeStruct((1, 128), F32), jax.ShapeDtypeStruct((1, D_MODEL), F32)],
        compiler_params=_params(("arbitrary",), VMEM_BIG),
    )(n2, h1, tgt, w_ud, g_post)


def _mlp_bwd_call(dff, rz, w_ud):
    L = dff.shape[0]
    tm = min(512, L)
    blk = D_FF // N_CHIPS

    def body(dff_r, rz_r, w_r, dz_r, dn2_r):
        dffv = dff_r[...]
        dn2 = jnp.zeros((tm, D_MODEL), F32)
        for j in range(N_CHIPS):
            sl = slice(blk * j, blk * (j + 1))
            dz = _mx(_dot_nt(dffv, w_r[j, 1]) * 2.0 * rz_r[:, sl].astype(F32))
            dz_r[:, sl] = dz
            dn2 = dn2 + _dot_nt(dz, w_r[j, 0])
        dn2_r[...] = dn2

    return pl.pallas_call(
        body, name="mlp_bwd", grid=(L // tm,),
        in_specs=[_row_spec(tm, D_MODEL), _row_spec(tm, D_FF), _vmem_spec()],
        out_specs=[_row_spec(tm, D_FF), _row_spec(tm, D_MODEL)],
        out_shape=[jax.ShapeDtypeStruct((L, D_FF), MXU_DTYPE), jax.ShapeDtypeStruct((L, D_MODEL), F32)],
        compiler_params=_params(("arbitrary",), VMEM_BIG),
    )(dff, rz, w_ud)


def _mlp_wgrad_call(a, dff, n2, dz):
    L = a.shape[0]
    tf = 512
    per = (D_FF // N_CHIPS) // tf

    def body(a_r, dff_r, n2_r, dz_r, dwd_r, dwu_r):
        dwd_r[...] = _dot_tn(a_r[...], dff_r[...])
        dwu_r[...] = _dot_tn(n2_r[...], dz_r[...])

    return pl.pallas_call(
        body, name="mlp_wgrad", grid=(D_FF // tf,),
        in_specs=[pl.BlockSpec((L, tf), lambda j: (0, j)), _vmem_spec(), _vmem_spec(),
                  pl.BlockSpec((L, tf), lambda j: (0, j))],
        out_specs=[pl.BlockSpec((tf, D_MODEL), lambda j: (j, 0)),
                   pl.BlockSpec((None, D_MODEL, tf), lambda j: (j // per, 0, j % per))],
        out_shape=[jax.ShapeDtypeStruct((D_FF, D_MODEL), F32),
                   jax.ShapeDtypeStruct((N_CHIPS, D_MODEL, D_FF // N_CHIPS), F32)],
        compiler_params=_params(("arbitrary",), VMEM_BIG),
    )(a, dff, n2, dz)


def _mix_bwd_call(dn2, dh2, h1, mix, cat, o_f, o_b, ga, gn, g_post, g_pre2, w_out_p):
    L = dn2.shape[0]
    tm = min(512, L)
    hw = GLA_HEADS * HEAD_PAD

    def body(dn2_r, dh2_r, h1_r, mix_r, cat_r, of_r, ob_r, ga_r, gn_r, gp_r, g2_r, w_r,
             dh1_r, do_r, dga_r, dos_r, dw_r, dg2_r, dgp_r, dgn_r):
        @pl.when(pl.program_id(0) == 0)
        def _():
            for ref in (dw_r, dg2_r, dgp_r, dgn_r):
                ref[...] = jnp.zeros_like(ref)

        parts = [slice(start, start + min(256, tm)) for start in range(0, tm, 256)]
        dmix_m = []
        for rs in parts:
            h1 = h1_r[rs, :]
            dx2, dg2 = _rms_bwd(h1, _rms_r(h1), g2_r[...], dn2_r[rs, :])
            dh1 = dh2_r[rs, :] + dx2
            dh1_r[rs, :] = dh1
            dg2_r[...] += dg2
            mix = mix_r[rs, :]
            dmix, dgp = _rms_bwd(mix, _rms_r(mix), gp_r[...], dh1)
            dgp_r[...] += dgp
            dmix_m.append(_mx(dmix))
        dcat = [_dot_nt(d, w_r[...]) for d in dmix_m]
        for rs, d in zip(parts, dmix_m):
            dw_r[...] += _dot_tn(cat_r[rs, :], d)
        gn_v = gn_r[...]
        dgn = jnp.zeros((1, HEAD_PAD), F32)
        for rs, dc in zip(parts, dcat):
            dos_r[rs, :] = _spread_heads(dc[:, hw:]).astype(dos_r.dtype)
            for h in range(GLA_HEADS):
                sl = slice(HEAD_PAD * h, HEAD_PAD * (h + 1))
                oh = of_r[rs, sl] + ob_r[rs, sl]
                rr = _rms_r(oh)
                xh = oh * rr
                gate = ga_r[rs, sl]
                sg = jax.nn.sigmoid(gate)
                silu = gate * sg
                doa = dc[:, sl]
                dga_r[rs, sl] = (doa * (xh * gn_v) * (sg + silu * (1.0 - sg))).astype(dga_r.dtype)
                don = doa * silu
                gd = don * gn_v
                do_r[rs, sl] = rr * (gd - xh * jnp.mean(gd * xh, axis=-1, keepdims=True))
                dgn = dgn + jnp.sum(don * xh, axis=0, keepdims=True)
        dgn_r[...] += dgn

    return pl.pallas_call(
        body, name="mix_bwd", grid=(L // tm,),
        in_specs=[_row_spec(tm, D_MODEL)] * 4 + [_row_spec(tm, OUT_PAD)] + [_row_spec(tm, hw)] * 3
        + [_full_spec((1, HEAD_PAD)), _full_spec((1, D_MODEL)), _full_spec((1, D_MODEL)), _vmem_spec()],
        out_specs=[_row_spec(tm, D_MODEL), _row_spec(tm, hw), _row_spec(tm, hw),
                   _row_spec(tm, SWA_Q_HEADS * HEAD_PAD),
                   _full_spec((OUT_PAD, D_MODEL)), _full_spec((1, D_MODEL)), _full_spec((1, D_MODEL)),
                   _full_spec((1, HEAD_PAD))],
        out_shape=[jax.ShapeDtypeStruct((L, D_MODEL), F32), jax.ShapeDtypeStruct((L, hw), F32),
                   jax.ShapeDtypeStruct((L, hw), MXU_DTYPE),
                   jax.ShapeDtypeStruct((L, SWA_Q_HEADS * HEAD_PAD), MXU_DTYPE),
                   jax.ShapeDtypeStruct((OUT_PAD, D_MODEL), F32), jax.ShapeDtypeStruct((1, D_MODEL), F32),
                   jax.ShapeDtypeStruct((1, D_MODEL), F32), jax.ShapeDtypeStruct((1, HEAD_PAD), F32)],
        compiler_params=_params(("arbitrary",), VMEM_BIG),
    )(dn2, dh2, h1, mix, cat, o_f, o_b, ga, gn, g_post, g_pre2, w_out_p)


def _in_bwd_call(x, dh1, g_pre, w_in_t, pairs, singles, halos, dep=None):
    L = x.shape[0]
    tm = min(512, L)
    per = tm // SWA_BLOCK
    n_pair, n_single, n_halo = len(pairs), len(singles), len(halos)
    groups = [c for c, _ in pairs] + [c for c, _ in singles] + [c for c, _ in halos]

    def body(*refs):
        x_r, dh1_r, g_r, w_r = refs[:4]
        pair_refs = refs[4:4 + 2 * n_pair]
        single_refs = refs[4 + 2 * n_pair:4 + 2 * n_pair + n_single]
        halo_refs = refs[4 + 2 * n_pair + n_single:4 + 2 * n_pair + n_single + per * n_halo]
        dx_r, dw_r, dg_r = refs[4 + 2 * n_pair + n_single + per * n_halo:]

        @pl.when(pl.program_id(0) == 0)
        def _():
            dw_r[...] = jnp.zeros_like(dw_r)
            dg_r[...] = jnp.zeros_like(dg_r)

        xv = x_r[...]
        r = _rms_r(xv)
        g = g_r[...]
        u = _mx(xv * r * g)
        vals = [pair_refs[2 * i][...].astype(F32) + pair_refs[2 * i + 1][...].astype(F32) for i in range(n_pair)]
        vals += [ref[...].astype(F32) for ref in single_refs]
        vals += [jnp.concatenate([ref[...] for ref in halo_refs[per * i:per * (i + 1)]], axis=0)
                 for i in range(n_halo)]
        ds = [_mx(_squeeze_heads(val) if heads else val) for (_, _, heads), val in zip(groups, vals)]
        du = jnp.zeros((tm, D_MODEL), F32)
        for (first, rows, _), d in zip(groups, ds):
            du = du + _dot(d, w_r[first:first + rows, :])
        for (first, rows, _), d in zip(groups, ds):
            dw_r[first:first + rows, :] += _dot_tn(d, u)
        dx, dg = _rms_bwd(xv, r, g, du)
        dx_r[...] = dh1_r[...] + dx
        dg_r[...] += dg

    arrays = [a for _, pr in pairs for a in pr] + [a for _, a in singles]
    specs = [_row_spec(tm, a.shape[1]) for a in arrays]
    for _, a in halos:
        specs += [pl.BlockSpec((SWA_BLOCK, a.shape[1]), lambda i, j=j: (per * i + 1 + j, 0)) for j in range(per)]
        arrays += [a] * per
    body, extra, extra_specs = _after(body, 4 + len(arrays), dep)
    return pl.pallas_call(
        body, name="in_bwd", grid=(L // tm,),
        in_specs=[_row_spec(tm, D_MODEL), _row_spec(tm, D_MODEL), _full_spec((1, D_MODEL)), _vmem_spec()] + specs
        + extra_specs,
        out_specs=[_row_spec(tm, D_MODEL), _full_spec((IN_COLS, D_MODEL)), _full_spec((1, D_MODEL))],
        out_shape=[jax.ShapeDtypeStruct((L, D_MODEL), F32), jax.ShapeDtypeStruct((IN_COLS, D_MODEL), F32),
                   jax.ShapeDtypeStruct((1, D_MODEL), F32)],
        compiler_params=_params(("arbitrary",), VMEM_BIG),
    )(x, dh1, g_pre, w_in_t, *arrays, *extra)


def _adamw_math(w, g, m, v):
    m = ADAM_B1 * m + (1.0 - ADAM_B1) * g
    v = ADAM_B2 * v + (1.0 - ADAM_B2) * (g * g)
    m_hat = m / (1.0 - ADAM_B1 ** ADAM_STEP)
    v_hat = v / (1.0 - ADAM_B2 ** ADAM_STEP)
    delta = -ADAM_LR * (m_hat / (jnp.sqrt(v_hat) + ADAM_EPS) + ADAM_WD * w)
    return delta, m, v


def _adamw_call(w, g, m, v, name, dep=None):
    rows, cols = w.shape
    tr = min(256, rows)

    def body(w_r, g_r, m_r, v_r, d_r, nm_r, nv_r):
        d_r[...], nm_r[...], nv_r[...] = _adamw_math(w_r[...], g_r[...], m_r[...], v_r[...])

    if rows % tr == 0:
        spec, steps = _row_spec(tr, cols), rows // tr
    else:
        spec, steps = pl.BlockSpec((rows, 256), lambda i: (0, i)), cols // 256
    body, extra, extra_specs = _after(body, 4, dep)
    return pl.pallas_call(
        body, name=name, grid=(steps,),
        in_specs=[spec] * 4 + extra_specs, out_specs=[spec] * 3,
        out_shape=[jax.ShapeDtypeStruct(w.shape, F32)] * 3,
        compiler_params=_params(("arbitrary",)),
    )(w, g, m, v, *extra)


def _position():
    return lax.axis_index("x"), lax.axis_index("y"), lax.axis_index("c")


def _other_chips(x, y):
    return [(1 - x, y), (x, 1 - y), (1 - x, 1 - y)]


ROWS, COLS = -2, -1


def _half(ref, which, axis):
    size = ref.shape[axis] // 2
    span = pl.ds(pl.multiple_of(which * size, 16 if axis == ROWS else 128), size)
    index = [slice(None)] * len(ref.shape)
    index[axis] = span
    return ref.at[tuple(index)]


def _quarter(ref, half, which, axis):
    size = ref.shape[axis] // 4
    span = pl.ds(pl.multiple_of((2 * half + which) * size, 16 if axis == ROWS else 128), size)
    index = [slice(None)] * len(ref.shape)
    index[axis] = span
    return ref.at[tuple(index)]


def _first_gather_call(shards, axes, routed):
    n = len(shards)
    per = 7

    def body(*refs):
        srcs, outs = refs[:n], refs[n:2 * n]
        send_sems, recv_sems, local_sems = refs[2 * n:]
        x, y, c = _position()
        me, sibling = (x, y, c), (x, y, 1 - c)
        x_side, y_side, across = _other_chips(x, y)
        local = [pltpu.make_async_copy(srcs[a], outs[a].at[2 * x + y], local_sems.at[a]) for a in range(n)]
        for cp in local:
            cp.start()

        def copy(a, k, dst, to, src=None):
            return pltpu.make_async_remote_copy(
                src_ref=dst if src is None else src, dst_ref=dst, send_sem=send_sems.at[per * a + k],
                recv_sem=recv_sems.at[per * a + k], device_id=to, device_id_type=MESH_ID)

        def half(a, chip, pc):
            return _half(outs[a].at[2 * chip[0] + chip[1]], pc, axes[a])

        def quarter(a, chip, q):
            return _quarter(outs[a].at[2 * chip[0] + chip[1]], c, q, axes[a])

        sends = []
        for a in range(n):
            mine = _half(srcs[a], c, axes[a])
            targets = (x_side, y_side) if routed[a] else (x_side, y_side, across)
            sends += [copy(a, j, half(a, (x, y), c), (*chip, c), src=mine) for j, chip in enumerate(targets)]
        for cp in sends:
            cp.start()
        for a in range(n):
            for j, chip in enumerate((x_side, y_side)):
                copy(a, j, half(a, chip, c), me).wait_recv()
                if routed[a]:
                    other = (y_side, x_side)[j]
                    sends.append(copy(a, 2 + j, quarter(a, chip, j), (*other, c)))
                    sends[-1].start()
                sends.append(copy(a, 4 + j, half(a, chip, c), sibling))
                sends[-1].start()
        for a in range(n):
            if routed[a]:
                for j in range(2):
                    copy(a, 2 + j, quarter(a, across, j), me).wait_recv()
            else:
                copy(a, 2, half(a, across, c), me).wait_recv()
            sends.append(copy(a, 6, half(a, across, c), sibling))
            sends[-1].start()
        for a in range(n):
            for k, chip in ((4, x_side), (5, y_side), (6, across)):
                copy(a, k, half(a, chip, 1 - c), me).wait_recv()
        for cp in sends:
            cp.wait_send()
        for cp in local:
            cp.wait()

    return pl.pallas_call(
        body, name="first_gather",
        in_specs=[_any_spec()] * n, out_specs=[_any_spec()] * n,
        out_shape=[jax.ShapeDtypeStruct((N_CHIPS,) + s.shape, s.dtype) for s in shards],
        scratch_shapes=[pltpu.SemaphoreType.DMA((per * n,)), pltpu.SemaphoreType.DMA((per * n,)),
                        pltpu.SemaphoreType.DMA((n,))],
    )(*shards)


def _split_start(name, arrays, n_copies, plan):
    n = len(arrays)

    def body(*refs):
        ins, send_sems, recv_sems, token = refs[:n], refs[n], refs[n + 1], refs[-1]
        for k, (src, dst, to, _) in enumerate(plan(ins)):
            pltpu.make_async_remote_copy(src_ref=src, dst_ref=dst, send_sem=send_sems.at[k],
                                         recv_sem=recv_sems.at[k], device_id=to, device_id_type=MESH_ID).start()
        token[...] = jnp.zeros_like(token)

    hbm = pl.BlockSpec(memory_space=pltpu.HBM)
    sem = pl.BlockSpec(memory_space=pltpu.SEMAPHORE)
    out = pl.pallas_call(
        body, name=name,
        out_shape=(pltpu.SemaphoreType.DMA((n_copies,)), pltpu.SemaphoreType.DMA((n_copies,)))
        + tuple(pltpu.HBM(a.shape, a.dtype) for a in arrays) + (jax.ShapeDtypeStruct((8, 128), F32),),
        in_specs=[hbm] * n, out_specs=(sem, sem) + (hbm,) * n + (_vmem_spec(),),
        input_output_aliases={i: 2 + i for i in range(n)},
        compiler_params=pltpu.CompilerParams(has_side_effects=pltpu.SideEffectType.DATAFLOW_SIDE_EFFECTING),
    )(*[pltpu.with_memory_space_constraint(a, pltpu.HBM) for a in arrays])
    return (out[0], out[1], tuple(out[2:2 + n])), out[-1]


def _split_wait(name, handle, n_copies, plan, after):
    send_sems, recv_sems, arrays = handle
    n = len(arrays)

    def body(*refs):
        ins, s_sems, r_sems = refs[:n], refs[n], refs[n + 1]
        for k, (src, dst, to, landed) in enumerate(plan(ins)):
            cp = pltpu.make_async_remote_copy(src_ref=src, dst_ref=landed, send_sem=s_sems.at[k],
                                              recv_sem=r_sems.at[k], device_id=to, device_id_type=MESH_ID)
            cp.wait_send()
            cp.wait_recv()

    hbm = pl.BlockSpec(memory_space=pltpu.HBM)
    sem = pl.BlockSpec(memory_space=pltpu.SEMAPHORE)
    out = pl.pallas_call(
        body, name=name,
        out_shape=tuple(pltpu.HBM(a.shape, a.dtype) for a in arrays),
        in_specs=[hbm] * n + [sem, sem, _any_spec()], out_specs=(hbm,) * n,
        input_output_aliases={i: i for i in range(n)},
        compiler_params=pltpu.CompilerParams(has_side_effects=pltpu.SideEffectType.DATAFLOW_SIDE_EFFECTING),
    )(*arrays, send_sems, recv_sems, after)
    return tuple(out)


def _gather_plans(axes):
    n = len(axes)

    def stage_one(refs):
        x, y, c = _position()
        copies = []
        for a, axis in enumerate(axes):
            for px, py in _other_chips(x, y):
                copies.append((_half(refs[a], c, axis), _half(refs[n + a].at[2 * x + y], c, axis),
                               (px, py, c), _half(refs[n + a].at[2 * px + py], c, axis)))
        return copies

    def stage_two(refs):
        x, y, c = _position()
        copies = []
        for a, axis in enumerate(axes):
            for px, py in _other_chips(x, y):
                piece = _half(refs[n + a].at[2 * px + py], c, axis)
                copies.append((piece, piece, (x, y, 1 - c), _half(refs[n + a].at[2 * px + py], 1 - c, axis)))
        return copies

    return stage_one, stage_two


def _pair_swap_plan(axes):
    n = len(axes)

    def plan(refs):
        x, y, c = _position()
        return [(_half(refs[a], 1 - c, axes[a]), refs[n + a], (x, y, 1 - c), refs[n + a]) for a in range(n)]

    return plan


def _chip_swap_plan(n):
    def plan(refs):
        x, y, c = _position()
        copies = []
        for a in range(n):
            for j, (px, py) in enumerate(_other_chips(x, y)):
                copies.append((refs[a].at[2 * px + py], refs[n + a].at[j], (px, py, c), refs[n + a].at[j]))
        return copies

    return plan


def _pair_join_plan(axes):
    def plan(refs):
        x, y, c = _position()
        copies = []
        for a, axis in enumerate(axes):
            mine = _half(refs[a], c, axis)
            copies.append((mine, mine, (x, y, 1 - c), _half(refs[a], 1 - c, axis)))
        return copies

    return plan


def _pair_add_call(g, got, pos, name, axis):
    rows, cols = got.shape[1], got.shape[2]
    tr = min(512, rows) if axis == ROWS else rows
    nblk = rows // tr
    if axis == ROWS:
        mine = lambda j, i, p: (j, p[1] * nblk + i, 0)
    else:
        mine = lambda j, i, p: (j, 0, p[1])

    def body(pos_r, g_r, got_r, o_r):
        o_r[...] = (g_r[...] + got_r[...]).astype(o_r.dtype)

    return pl.pallas_call(
        body, name=name,
        grid_spec=pltpu.PrefetchScalarGridSpec(
            num_scalar_prefetch=1, grid=(N_CHIPS, nblk),
            in_specs=[pl.BlockSpec((None, tr, cols), mine),
                      pl.BlockSpec((None, tr, cols), lambda j, i, p: (j, i, 0))],
            out_specs=pl.BlockSpec((None, tr, cols), lambda j, i, p: (j, i, 0))),
        out_shape=jax.ShapeDtypeStruct(got.shape, COMM_DTYPE),
        compiler_params=_params(("arbitrary", "arbitrary"), VMEM_BIG),
    )(pos, g, got)


def _chip_add_call(hsum, got, pos, name, axis):
    rows, cols = hsum.shape[1], hsum.shape[2]
    tr = min(512, rows) if axis == ROWS else rows
    nblk = rows // tr
    if axis == ROWS:
        out_shape, mine = (2 * rows, cols), (lambda i, p: (p[1] * nblk + i, 0))
    else:
        out_shape, mine = (rows, 2 * cols), (lambda i, p: (0, p[1]))

    def body(pos_r, own_r, got_r, o_r):
        acc = own_r[...].astype(F32)
        for j in range(3):
            acc = acc + got_r[j].astype(F32)
        o_r[...] = acc

    return pl.pallas_call(
        body, name=name,
        grid_spec=pltpu.PrefetchScalarGridSpec(
            num_scalar_prefetch=1, grid=(nblk,),
            in_specs=[pl.BlockSpec((None, tr, cols), lambda i, p: (p[0], i, 0)),
                      pl.BlockSpec((3, tr, cols), lambda i, p: (0, i, 0))],
            out_specs=pl.BlockSpec((tr, cols), mine)),
        out_shape=jax.ShapeDtypeStruct(out_shape, F32),
        compiler_params=_params(("arbitrary",), VMEM_BIG),
    )(pos, hsum, got)


SMALL_NAMES = ("norm_mix_pre", "norm_mix_post", "norm_mlp_pre", "norm_mlp_post", "b_gate_fwd", "b_gate_bwd",
               "gla_norm", "swa_sink", "rel_bias")


N_DEVICES = 8


def _small_pack_call(grads, extras):
    operands = list(grads) + list(extras)

    def body(*refs):
        g_refs, (all_a, all_b) = refs[:len(operands)], refs[len(operands):]
        x, y, c = _position()
        me = 4 * x + 2 * y + c
        all_a[me] = jnp.zeros(all_a.shape[1:], F32)
        all_b[me] = jnp.zeros(all_b.shape[1:], F32)
        for i in range(4):
            all_a[me, i:i + 1, :] = g_refs[i][...]
        all_a[me, 4:5, 0:256] = g_refs[4][...]
        all_a[me, 5:6, 0:256] = g_refs[5][...]
        all_a[me, 6:7, 0:128] = g_refs[6][...]
        all_a[me, 7:8, 0:128] = g_refs[7][...]
        all_a[me, 7:8, 128:256] = g_refs[11][...]
        all_b[me, 0:32, 0:128] = g_refs[8][...]
        all_b[me, 32:48, :] = g_refs[9][...]
        all_b[me, 48:64, :] = g_refs[10][...]

    out_shape = [jax.ShapeDtypeStruct((N_DEVICES, 8, D_MODEL), F32), jax.ShapeDtypeStruct((N_DEVICES, 64, 256), F32)]
    return pl.pallas_call(
        body, name="small_pack",
        in_specs=[_whole_spec(a.shape) for a in operands], out_specs=[_whole_spec(s.shape) for s in out_shape],
        out_shape=out_shape,
    )(*operands)


def _everyone_plan(n):
    def plan(refs):
        x, y, c = _position()
        copies = []
        for k in range(1, N_DEVICES):
            px = 1 - x if (k >> 2) & 1 else x
            py = 1 - y if (k >> 1) & 1 else y
            pc = 1 - c if k & 1 else c
            for a in range(n):
                mine = refs[a].at[4 * x + 2 * y + c]
                copies.append((mine, mine, (px, py, pc), refs[a].at[4 * px + 2 * py + pc]))
        return copies

    return plan


def _small_adamw_call(all_a, all_b, params):
    n_small = len(SMALL_NAMES)
    wmv = [t for p in params for t in p]
    shapes = [p[0].shape for p in params]

    def body(*refs):
        all_a, all_b = refs[:2]
        wmv_refs = refs[2:2 + 3 * n_small]
        out_refs = refs[2 + 3 * n_small:]
        sum_a, sum_b = all_a[0], all_b[0]
        for d in range(1, N_DEVICES):
            sum_a = sum_a + all_a[d]
            sum_b = sum_b + all_b[d]
        gsum = [sum_a[0:1], sum_a[1:2], sum_a[2:3], sum_a[3:4], sum_a[4:5, 0:256], sum_a[5:6, 0:256],
                sum_a[6:7, 0:128], sum_a[7:8, 0:SWA_Q_HEADS], sum_b[0:32, 0:SWA_Q_HEADS]]
        for i in range(n_small):
            w_r, m_r, v_r = wmv_refs[3 * i:3 * i + 3]
            delta, new_m, new_v = _adamw_math(w_r[...], gsum[i], m_r[...], v_r[...])
            out_refs[4 * i][...] = gsum[i]
            out_refs[4 * i + 1][...] = delta
            out_refs[4 * i + 2][...] = new_m
            out_refs[4 * i + 3][...] = new_v
        out_refs[4 * n_small][...] = sum_b[32:48]
        out_refs[4 * n_small + 1][...] = sum_b[48:64]
        out_refs[4 * n_small + 2][...] = sum_a[7:8, 128:256]

    out_shape = [jax.ShapeDtypeStruct(s, F32) for s in shapes for _ in range(4)]
    out_shape += [jax.ShapeDtypeStruct((GLA_GATE_RANK, 256), F32)] * 2 + [jax.ShapeDtypeStruct((1, 128), F32)]
    out = pl.pallas_call(
        body, name="small_adamw",
        in_specs=[_whole_spec(a.shape) for a in [all_a, all_b] + wmv],
        out_specs=[_whole_spec(s.shape) for s in out_shape],
        out_shape=out_shape,
    )(all_a, all_b, *wmv)
    per_name = [tuple(out[4 * i:4 * i + 4]) for i in range(n_small)]
    return per_name, out[4 * n_small], out[4 * n_small + 1], out[4 * n_small + 2]


def _pad_heads(t, n_heads, axis=-1):
    axis = axis % t.ndim
    shape = t.shape
    t = t.reshape(shape[:axis] + (n_heads, 64) + shape[axis + 1:])
    pad = [(0, 0)] * t.ndim
    pad[axis + 1] = (0, HEAD_PAD - 64)
    return jnp.pad(t, pad).reshape(shape[:axis] + (n_heads * HEAD_PAD,) + shape[axis + 1:])


def _unpad_heads(t, n_heads, axis=-1):
    axis = axis % t.ndim
    shape = t.shape
    t = t.reshape(shape[:axis] + (n_heads, HEAD_PAD) + shape[axis + 1:])
    t = lax.slice_in_dim(t, 0, 64, axis=axis + 1)
    return t.reshape(shape[:axis] + (n_heads * 64,) + shape[axis + 1:])


def _pad_gate(w, first_row):
    return jnp.pad(_pad_heads(w, 4), ((first_row, 128 - GLA_GATE_RANK - first_row), (0, 0)))


def _own_slot(shard, chip):
    zone = lax.empty((N_CHIPS,) + shard.shape, shard.dtype)
    return lax.dynamic_update_slice(zone, shard[None], (chip,) + (0,) * shard.ndim)


def _reduce_to_owners(grads, axes, pos, tag, overlap):
    n = len(grads)

    def half_shape(g, axis):
        return (N_CHIPS, g.shape[1] // 2, g.shape[2]) if axis == ROWS else (N_CHIPS, g.shape[1], g.shape[2] // 2)

    lands = [lax.empty(half_shape(g, axis), F32) for g, axis in zip(grads, axes)]
    handle, token = _split_start(tag + "_pair_start", list(grads) + lands, n, _pair_swap_plan(axes))
    got = _split_wait(tag + "_pair_wait", handle, n, _pair_swap_plan(axes), overlap[0](token))
    sums = [_pair_add_call(got[a], got[n + a], pos, f"{tag}_pair_add{a}", axes[a]) for a in range(n)]
    lands = [lax.empty((3,) + s.shape[1:], s.dtype) for s in sums]
    handle, token = _split_start(tag + "_chip_start", sums + lands, 3 * n, _chip_swap_plan(n))
    got = _split_wait(tag + "_chip_wait", handle, 3 * n, _chip_swap_plan(n), overlap[1](token))
    halves = [_chip_add_call(got[a], got[n + a], pos, f"{tag}_chip_add{a}", axes[a]) for a in range(n)]
    handle, token = _split_start(tag + "_join_start", halves, n, _pair_join_plan(axes))
    return _split_wait(tag + "_join_wait", handle, n, _pair_join_plan(axes), overlap[2](token))


def kernel(x, norm_mix_pre, w_in, w_gate_up_fwd, b_gate_fwd, w_gate_up_bwd, b_gate_bwd, gla_norm, swa_sink, rel_bias, w_out, norm_mix_post, norm_mlp_pre, w_up, w_down, norm_mlp_post, loss_target, m_norm_mix_pre, m_w_in, m_w_gate_up_fwd, m_b_gate_fwd, m_w_gate_up_bwd, m_b_gate_bwd, m_gla_norm, m_swa_sink, m_rel_bias, m_w_out, m_norm_mix_post, m_norm_mlp_pre, m_w_up, m_w_down, m_norm_mlp_post, v_norm_mix_pre, v_w_in, v_w_gate_up_fwd, v_b_gate_fwd, v_w_gate_up_bwd, v_b_gate_bwd, v_gla_norm, v_swa_sink, v_rel_bias, v_w_out, v_norm_mix_post, v_norm_mlp_pre, v_w_up, v_w_down, v_norm_mlp_post):
    given = dict(locals())
    cx, cy, cc = _position()
    chip = (2 * cx + cy).astype(jnp.int32)
    pos = jnp.stack([chip, cc.astype(jnp.int32)])
    seq, tgt = x[0], loss_target[0]
    L = seq.shape[0]

    gates = jnp.concatenate([w_gate_up_fwd[0], w_gate_up_bwd[0]], axis=0).astype(COMM_DTYPE)
    all_in, all_gates = _first_gather_call([w_in[0].T.astype(COMM_DTYPE), gates], [COLS, ROWS], [True, False])
    rest = [w_out[0].astype(COMM_DTYPE), jnp.stack([w_up[0], w_down[0]]).astype(COMM_DTYPE)]
    stage_one, stage_two = _gather_plans([ROWS, ROWS])
    handle, token = _split_start("gather_chip_start", rest + [_own_slot(s, chip) for s in rest] + [all_gates], 6,
                                 stage_one)

    w_in_t = _mx(all_in.reshape(IN_COLS, D_MODEL))
    gates_full = jnp.concatenate([all_gates[j] for j in range(N_CHIPS)], axis=1)
    wgf_p = _mx(_pad_gate(gates_full[:GLA_GATE_RANK], 0))
    wgb_p = _mx(_pad_gate(gates_full[GLA_GATE_RANK:], GLA_GATE_RANK))
    bf_p, bb_p = _pad_heads(b_gate_fwd, 4), _pad_heads(b_gate_bwd, 4)
    buckets = jnp.asarray(_band_buckets())
    bias = _bias_call(rel_bias, buckets)
    sink1 = swa_sink.reshape(SWA_Q_HEADS)

    qa, ka, va, ga, qs, ks, vs, za = _proj_call(seq, norm_mix_pre, w_in_t, dep=token)
    halo = ((SWA_BLOCK, SWA_BLOCK), (0, 0))
    ks_p, vs_p = jnp.pad(ks, halo), jnp.pad(vs, halo)
    o_f, o_b, s_f, s_b = _gla_fwd_call(qa, ka, va, za, wgf_p, bf_p, wgb_p, bb_p)
    arrays = _split_wait("gather_chip_wait", handle, 6, stage_one, o_f)
    handle, token = _split_start("gather_pair_start", list(arrays), 6, stage_two)
    o_s = _swa_fwd_call(qs, ks_p, vs_p, bias, sink1, dep=token)
    arrays = _split_wait("gather_pair_wait", handle, 6, stage_two, o_s)
    w_out_full = _mx(arrays[2].reshape(N_CHIPS * R_OUT, D_MODEL))
    w_ud = _mx(arrays[3])
    cat, mix, h1, n2 = _mix_call(o_f, o_b, ga, o_s, seq, gla_norm, w_out_full, norm_mix_post, norm_mlp_pre)
    a, rz, dh2, dff, loss, d_post2 = _mlp_fwd_call(n2, h1, tgt, w_ud, norm_mlp_post)

    dz, dn2 = _mlp_bwd_call(dff, rz, w_ud)
    dw_down, dw_up4 = _mlp_wgrad_call(a, dff, n2, dz)
    dh1, do, dga, dos, dw_out, d_pre2, d_post, d_gn = _mix_bwd_call(
        dn2, dh2, h1, mix, cat, o_f, o_b, ga, gla_norm, norm_mix_post, norm_mlp_pre, w_out_full)
    done = {}

    def swa_backward(tok):
        done["swa"] = _swa_bwd_call(qs, ks_p, vs_p, bias, sink1, dos, dep=tok)
        return done["swa"][0]

    def gla_in_backward(tok):
        done["gla"] = _gla_bwd_call(qa, ka, va, za, do, s_f, s_b, wgf_p, bf_p, wgb_p, bb_p, dep=tok)
        dqf, dkf, dvf, dzf, _, _, dqb, dkb, dvb, dzb, _, _ = done["gla"]
        dqs, dks_p, dvs_p, _, _ = done["swa"]
        done["in"] = _in_bwd_call(
            seq, dh1, norm_mix_pre, w_in_t,
            pairs=[(_side_by_side(T_QA), (dqf, dqb)), (_side_by_side(T_KA), (dkf, dkb)), (T_VA, (dvf, dvb)),
                   (T_ZA, (dzf, dzb))],
            singles=[(T_GA, dga), (_side_by_side(T_QS), dqs)], halos=[(T_KS, dks_p), (T_VS, dvs_p)])
        return done["in"][0]

    def bias_backward(tok):
        done["rel"] = _relbias_call(done["swa"][3], done["swa"][4], buckets, dep=tok)
        return done["rel"][0]

    g_up, g_down, g_out = _reduce_to_owners(
        [dw_up4, dw_down.reshape(N_CHIPS, R_DOWN, D_MODEL), dw_out.reshape(N_CHIPS, R_OUT, D_MODEL)],
        [ROWS, ROWS, ROWS], pos, "mlp", [swa_backward, gla_in_backward, bias_backward])
    dx, dw_in_t, d_pre = done["in"]
    dwf, dbf, dwb, dbb = done["gla"][4], done["gla"][5], done["gla"][10], done["gla"][11]
    drel, dsink = done["rel"]

    small_grads = [d_pre, d_post, d_pre2, d_post2, _unpad_heads(dbf, 4), _unpad_heads(dbb, 4), d_gn, dsink, drel]
    gate_grads = [_unpad_heads(dwf[:GLA_GATE_RANK], 4), _unpad_heads(dwb[GLA_GATE_RANK:2 * GLA_GATE_RANK], 4)]
    small_params = [(given[n], given["m_" + n], given["v_" + n]) for n in SMALL_NAMES]
    upd = {}

    everyone = _everyone_plan(2)
    small_handle, small_token = _split_start(
        "small_start", list(_small_pack_call(small_grads, gate_grads + [loss])), 2 * (N_DEVICES - 1), everyone)

    def update_out(tok):
        upd["w_out"] = (g_out,) + tuple(_adamw_call(w_out[0], g_out, m_w_out[0], v_w_out[0], "adamw_w_out",
                                                    dep=tok + small_token))
        return upd["w_out"][1]

    def update_mlp(tok):
        upd["w_up"] = (g_up,) + tuple(_adamw_call(w_up[0], g_up, m_w_up[0], v_w_up[0], "adamw_w_up", dep=tok))
        upd["w_down"] = (g_down,) + tuple(
            _adamw_call(w_down[0], g_down, m_w_down[0], v_w_down[0], "adamw_w_down", dep=upd["w_up"][1]))
        all_a, all_b = _split_wait("small_wait", small_handle, 2 * (N_DEVICES - 1), everyone, upd["w_down"][1])
        per_name, done["gf_sum"], done["gb_sum"], upd["loss"] = _small_adamw_call(all_a, all_b, small_params)
        upd.update(dict(zip(SMALL_NAMES, per_name)))
        return per_name[0][1]

    def update_gates(tok):
        for name, total in (("w_gate_up_fwd", done["gf_sum"]), ("w_gate_up_bwd", done["gb_sum"])):
            g = lax.dynamic_slice(total, (0, chip * 64), (GLA_GATE_RANK, 64))
            upd[name] = (g,) + tuple(_adamw_call(given[name][0], g, given["m_" + name][0], given["v_" + name][0],
                                                 "adamw_" + name, dep=tok))
        return upd["w_gate_up_bwd"][1]

    (g_in_t,) = _reduce_to_owners([dw_in_t.reshape(N_CHIPS, R_IN, D_MODEL)], [COLS], pos, "in",
                                  [update_out, update_mlp, update_gates])
    in_t = (g_in_t,) + tuple(_adamw_call(w_in[0].T, g_in_t, m_w_in[0].T, v_w_in[0].T, "adamw_w_in"))
    upd["w_in"] = tuple(t.T for t in in_t)

    big = ("w_in", "w_gate_up_fwd", "w_gate_up_bwd", "w_out", "w_up", "w_down")
    names = ["norm_mix_pre", "w_in", "w_gate_up_fwd", "b_gate_fwd", "w_gate_up_bwd", "b_gate_bwd", "gla_norm",
             "swa_sink", "rel_bias", "w_out", "norm_mix_post", "norm_mlp_pre", "w_up", "w_down", "norm_mlp_post"]
    outs = [upd["loss"][0, 0], dx[None]]
    for kind in range(4):
        outs += [upd[n][kind][None] if n in big else upd[n][kind] for n in names]
    return tuple(outs)
```

```python
import math

import numpy as np
import jax
import jax.numpy as jnp
from jax import lax
from jax.experimental import pallas as pl
from jax.experimental.pallas import tpu as pltpu

F32 = jnp.float32
MXU_DTYPE = jnp.bfloat16
COMM_DTYPE = jnp.bfloat16

D_MODEL = 1024
D_FF = 4096
N_CHIPS = 4
GLA_HEADS = 4
GLA_CHUNK = 64
GLA_GATE_RANK = 16
GLA_GATE_NORM = 16.0
SWA_Q_HEADS = 8
SWA_KV_HEADS = 2
SWA_BLOCK = 128
REL_BUCKETS = 32
REL_MAX_DIST = 128
NORM_EPS = 1e-6
HEAD_PAD = 128

ADAM_LR = 0.001
ADAM_B1 = 0.9
ADAM_B2 = 0.999
ADAM_EPS = 1e-08
ADAM_WD = 0.01
ADAM_STEP = 10

OUT_PAD = 1024

R_IN, R_OUT, R_UP, R_DOWN = 584, 256, 1024, 1024

VMEM_BIG = 56 * 1024 * 1024
MESH_AXES = ("x", "y", "c")
MESH_ID = pl.DeviceIdType.MESH


def _mx(a):
    return a.astype(MXU_DTYPE)


def _dot(a, b):
    return jnp.dot(a, b, preferred_element_type=F32)


def _dot_nt(a, b):
    return lax.dot_general(a, b, (((1,), (1,)), ((), ())), preferred_element_type=F32)


def _dot_tn(a, b):
    return lax.dot_general(a, b, (((0,), (0,)), ((), ())), preferred_element_type=F32)


def _rms_r(x):
    return lax.rsqrt(jnp.mean(x * x, axis=-1, keepdims=True) + NORM_EPS)


def _rms_bwd(x, r, g, dy):
    xh = x * r
    gdy = dy * g
    dx = r * (gdy - xh * jnp.mean(gdy * xh, axis=-1, keepdims=True))
    return dx, jnp.sum(dy * xh, axis=0, keepdims=True)


def _low_half(rows):
    return lax.broadcasted_iota(jnp.int32, (rows, HEAD_PAD), 1) < 64


def _spread_heads(x):
    low = _low_half(x.shape[0])
    parts = []
    for p in range(x.shape[1] // HEAD_PAD):
        pair = x[:, HEAD_PAD * p:HEAD_PAD * (p + 1)]
        parts += [jnp.where(low, pair, 0.0), jnp.where(low, pltpu.roll(pair, 64, 1), 0.0)]
    return jnp.concatenate(parts, axis=1)


def _squeeze_heads(x):
    low = _low_half(x.shape[0])
    parts = []
    for p in range(x.shape[1] // (2 * HEAD_PAD)):
        even = x[:, 2 * HEAD_PAD * p:2 * HEAD_PAD * p + HEAD_PAD]
        odd = x[:, 2 * HEAD_PAD * p + HEAD_PAD:2 * HEAD_PAD * (p + 1)]
        parts.append(jnp.where(low, even, pltpu.roll(odd, 64, 1)))
    return parts[0] if len(parts) == 1 else jnp.concatenate(parts, axis=1)


def _params(sem=None, vmem=None):
    kw = {}
    if sem is not None:
        kw["dimension_semantics"] = sem
    if vmem is not None:
        kw["vmem_limit_bytes"] = vmem
    return pltpu.CompilerParams(**kw)


def _vmem_spec():
    return pl.BlockSpec(memory_space=pltpu.VMEM)


def _whole_spec(shape):
    return pl.BlockSpec(shape, lambda: (0,) * len(shape))


def _row_spec(tm, width):
    return pl.BlockSpec((tm, width), lambda i: (i, 0))


def _full_spec(shape):
    return pl.BlockSpec(shape, lambda i: (0,) * len(shape))


def _any_spec():
    return pl.BlockSpec(memory_space=pl.ANY)


def _after(body, n_in, dep):
    if dep is None:
        return body, [], []
    return (lambda *refs: body(*refs[:n_in], *refs[n_in + 1:])), [dep], [_any_spec()]


T_QA, T_KA, T_VA, T_GA = (0, 256, 4), (256, 256, 4), (512, 512, 0), (1024, 512, 0)
T_QS, T_KS, T_VS = (1568, 512, 8), (2080, 128, 2), (2208, 128, 2)
T_ZA = (1536, 128, 0)
ZA_COLS = 2 * GLA_GATE_RANK
IN_COLS = 2336


def _side_by_side(group):
    return group[0], group[1], 0


def _proj_call(x, g_pre, w_in_t, dep=None):
    L = x.shape[0]
    tm = min(512, L)
    groups = [(T_QA, F32), (T_KA, F32), (T_VA, MXU_DTYPE), (T_GA, F32),
              (T_QS, MXU_DTYPE), (T_KS, MXU_DTYPE), (T_VS, MXU_DTYPE), (T_ZA, F32)]
    widths = [rows * (2 if heads else 1) for (_, rows, heads), _ in groups]

    def body(x_ref, g_ref, w_ref, *outs):
        xv = x_ref[...]
        u = _mx(xv * _rms_r(xv) * g_ref[...])
        for ref, (grp, _) in zip(outs, groups):
            first, rows, heads = grp
            val = _dot_nt(u, w_ref[first:first + rows, :])
            if heads:
                val = _spread_heads(val)
            if grp is T_ZA:
                val = jnp.where(lax.broadcasted_iota(jnp.int32, val.shape, 1) < ZA_COLS, val, 0.0)
            if grp is T_QS:
                val = val * 0.125
            ref[...] = val.astype(ref.dtype)

    body, extra, extra_specs = _after(body, 3, dep)
    return pl.pallas_call(
        body, name="proj_fwd", grid=(L // tm,),
        in_specs=[_row_spec(tm, D_MODEL), _full_spec((1, D_MODEL)), _vmem_spec()] + extra_specs,
        out_specs=[_row_spec(tm, w) for w in widths],
        out_shape=[jax.ShapeDtypeStruct((L, w), dt) for w, (_, dt) in zip(widths, groups)],
        compiler_params=_params(("arbitrary",), VMEM_BIG),
    )(x, g_pre, w_in_t, *extra)


def _tri_masks():
    row = lax.broadcasted_iota(jnp.int32, (GLA_CHUNK, GLA_CHUNK), 0)
    col = lax.broadcasted_iota(jnp.int32, (GLA_CHUNK, GLA_CHUNK), 1)
    return row >= col, row <= col


def _chunk_sums(tri_m, x):
    hi = _mx(x)
    rest = x - hi.astype(F32)
    mid = _mx(rest)
    lo = _mx(rest - mid.astype(F32))
    return _dot(tri_m, hi) + _dot(tri_m, mid) + _dot(tri_m, lo)


def _gla_block_pre(q_r, k_r, z_r, w_r, b_r, rev, nc, qd_s, ki_s, ks_s, dec_s, keep=None):
    tri_f, tri_b = _tri_masks()
    tri_m = _mx((tri_b if rev else tri_f).astype(F32))
    g = _dot(_mx(z_r[...]), w_r[...]) + b_r[...]
    la = (jnp.minimum(g, 0.0) - jnp.log(1.0 + jnp.exp(-jnp.abs(g)))) / GLA_GATE_NORM
    sums, lasts = [], []
    for c in range(nc):
        b_c = _chunk_sums(tri_m, la[GLA_CHUNK * c:GLA_CHUNK * (c + 1)])
        blast = b_c[0:1] if rev else b_c[GLA_CHUNK - 1:GLA_CHUNK]
        dec_s[c] = jnp.exp(blast)
        sums.append(b_c)
        lasts.append(jnp.broadcast_to(blast, b_c.shape))
    b = jnp.concatenate(sums, axis=0)
    eb = jnp.exp(b)
    enb = jnp.exp(-b)
    elb = jnp.exp(jnp.concatenate(lasts, axis=0) - b)
    k = k_r[...]
    qd_s[...] = (q_r[...] * 0.125 * eb).astype(qd_s.dtype)
    ki_s[...] = (k * enb).astype(ki_s.dtype)
    ks_s[...] = (k * elb).astype(ks_s.dtype)
    if keep is not None:
        for ref, val in zip(keep, (g, eb, enb, elb)):
            ref[...] = val


def _gla_fwd_call(qa, ka, va, za, wgf, bgf, wgb, bgb):
    L = qa.shape[0]
    br = min(512, L)
    nb, nc, n_chunks = L // br, br // GLA_CHUNK, L // GLA_CHUNK
    hw = GLA_HEADS * HEAD_PAD

    def body(qaf, kaf, vaf, zaf, qab, kab, vab, zab, wgf_r, bgf_r, wgb_r, bgb_r,
             of_r, ob_r, sf_r, sb_r, st_f, st_b, pre_f, pre_b):
        @pl.when(pl.program_id(0) == 0)
        def _():
            st_f[...] = jnp.zeros_like(st_f)
            st_b[...] = jnp.zeros_like(st_b)

        _gla_block_pre(qaf, kaf, zaf, wgf_r, bgf_r, False, nc, *pre_f)
        _gla_block_pre(qab, kab, zab, wgb_r, bgb_r, True, nc, *pre_b)
        tri_f, tri_b = _tri_masks()

        def one(tri, pre, v_r, o_r, s_r, st, ci):
            qd_s, ki_s, ks_s, dec_s = pre
            rows = pl.ds(pl.multiple_of(ci * GLA_CHUNK, GLA_CHUNK), GLA_CHUNK)
            dec = dec_s[ci]
            heads = range(GLA_HEADS)
            lanes = [slice(HEAD_PAD * h, HEAD_PAD * (h + 1)) for h in heads]
            qd = [qd_s[rows, sl] for sl in lanes]
            v = [v_r[rows, sl] for sl in lanes]
            s_t = [st[h] for h in heads]
            a = [_dot_nt(qd[h], ki_s[rows, lanes[h]]) for h in heads]
            carried = [_dot_nt(qd[h], _mx(s_t[h])) for h in heads]
            grown = [_dot_tn(v[h], ks_s[rows, lanes[h]]) for h in heads]
            a = [_mx(jnp.where(tri, a[h], 0.0)) for h in heads]
            inner = [_dot(a[h], v[h]) for h in heads]
            for h in heads:
                s_r[ci, h] = s_t[h].astype(s_r.dtype)
                o_r[rows, lanes[h]] = inner[h] + carried[h]
                st[h] = s_t[h] * dec[:, lanes[h]] + grown[h]

        def loop(t, carry):
            one(tri_f, pre_f, vaf, of_r, sf_r, st_f, t)
            one(tri_b, pre_b, vab, ob_r, sb_r, st_b, nc - 1 - t)
            return carry

        lax.fori_loop(0, nc, loop, 0, unroll=True)

    fwd = lambda i: (i, 0)
    bwd = lambda i: (nb - 1 - i, 0)
    ins = lambda m: [pl.BlockSpec((br, hw), m), pl.BlockSpec((br, hw), m),
                     pl.BlockSpec((br, hw), m), pl.BlockSpec((br, 128), m)]
    wspecs = [_full_spec((128, hw)), _full_spec((1, hw))] * 2
    s_shape = (nc, GLA_HEADS, HEAD_PAD, HEAD_PAD)
    pre_scratch = [pltpu.VMEM((br, hw), MXU_DTYPE)] * 3 + [pltpu.VMEM((nc, 1, hw), F32)]
    return pl.pallas_call(
        body, name="gla_fwd", grid=(nb,),
        in_specs=ins(fwd) + ins(bwd) + wspecs,
        out_specs=[pl.BlockSpec((br, hw), fwd), pl.BlockSpec((br, hw), bwd),
                   pl.BlockSpec(s_shape, lambda i: (i, 0, 0, 0)),
                   pl.BlockSpec(s_shape, lambda i: (nb - 1 - i, 0, 0, 0))],
        out_shape=[jax.ShapeDtypeStruct((L, hw), F32), jax.ShapeDtypeStruct((L, hw), F32),
                   jax.ShapeDtypeStruct((n_chunks,) + s_shape[1:], MXU_DTYPE),
                   jax.ShapeDtypeStruct((n_chunks,) + s_shape[1:], MXU_DTYPE)],
        scratch_shapes=[pltpu.VMEM(s_shape[1:], F32), pltpu.VMEM(s_shape[1:], F32), pre_scratch, pre_scratch],
        compiler_params=_params(("arbitrary",), VMEM_BIG),
    )(qa, ka, va, za, qa, ka, va, za, wgf, bgf, wgb, bgb)


def _gla_bwd_call(qa, ka, va, za, do, sf, sb, wgf, bgf, wgb, bgb, dep=None):
    L = qa.shape[0]
    br = min(512, L)
    nb, nc = L // br, br // GLA_CHUNK
    hw = GLA_HEADS * HEAD_PAD

    def body(qaf, kaf, vaf, zaf, dof, sf_r, qab, kab, vab, zab, dob, sb_r, wgf_r, bgf_r, wgb_r, bgb_r,
             dqf, dkf, dvf, dzf, dwf, dbf, dqb, dkb, dvb, dzb, dwb, dbb, gt_f, gt_b, pre_f, pre_b):
        @pl.when(pl.program_id(0) == 0)
        def _():
            for ref in (gt_f, gt_b, dwf, dbf, dwb, dbb):
                ref[...] = jnp.zeros_like(ref)

        _gla_block_pre(qaf, kaf, zaf, wgf_r, bgf_r, False, nc, *pre_f[:4], keep=pre_f[4:8])
        _gla_block_pre(qab, kab, zab, wgb_r, bgb_r, True, nc, *pre_b[:4], keep=pre_b[4:8])
        tri_f, tri_b = _tri_masks()
        row_w = lax.broadcasted_iota(jnp.int32, (GLA_CHUNK, HEAD_PAD), 0)

        def one(rev, pre, q_r, k_r, v_r, do_r, s_r, dq_r, dk_r, dv_r, gt, ci):
            qd_s, ki_s, ks_s, dec_s, _, eb_s, enb_s, elb_s, db_s = pre
            tri = tri_b if rev else tri_f
            last_row = 0 if rev else GLA_CHUNK - 1
            rows = pl.ds(pl.multiple_of(ci * GLA_CHUNK, GLA_CHUNK), GLA_CHUNK)
            dec = dec_s[ci]
            heads = range(GLA_HEADS)
            lanes = [slice(HEAD_PAD * h, HEAD_PAD * (h + 1)) for h in heads]
            qd = [qd_s[rows, sl] for sl in lanes]
            ki = [ki_s[rows, sl] for sl in lanes]
            ks = [ks_s[rows, sl] for sl in lanes]
            v = [v_r[rows, sl] for sl in lanes]
            do_h = [_mx(do_r[rows, sl]) for sl in lanes]
            s_t = [s_r[ci, h] for h in heads]
            g_t = [gt[h] for h in heads]
            g_m = [_mx(g_t[h]) for h in heads]
            a = [_dot_nt(qd[h], ki[h]) for h in heads]
            da = [_dot_nt(do_h[h], v[h]) for h in heads]
            dv_carried = [_dot_nt(ks[h], g_m[h]) for h in heads]
            dqd_carried = [_dot(do_h[h], _mx(s_t[h])) for h in heads]
            dks = [_dot(v[h], g_m[h]) for h in heads]
            g_grown = [_dot_tn(do_h[h], qd[h]) for h in heads]
            a = [_mx(jnp.where(tri, a[h], 0.0)) for h in heads]
            da = [_mx(jnp.where(tri, da[h], 0.0)) for h in heads]
            dv_inner = [_dot_tn(a[h], do_h[h]) for h in heads]
            dqd_inner = [_dot(da[h], ki[h]) for h in heads]
            dki = [_dot_tn(da[h], qd[h]) for h in heads]
            dq, dk = [], []
            for h in heads:
                sl = lanes[h]
                dv_r[rows, sl] = (dv_inner[h] + dv_carried[h]).astype(dv_r.dtype)
                ddec = jnp.sum(g_t[h] * s_t[h].astype(F32), axis=0, keepdims=True)
                gt[h] = g_t[h] * dec[:, sl] + g_grown[h]
                dq.append((dqd_inner[h] + dqd_carried[h]) * eb_s[rows, sl] * 0.125)
                dk_state = dks[h] * elb_s[rows, sl]
                dk.append(dki[h] * enb_s[rows, sl] + dk_state)
                k = k_r[rows, sl]
                dblast = jnp.sum(dk_state * k, axis=0, keepdims=True) + dec[:, sl] * ddec
                db_s[rows, sl] = q_r[rows, sl] * dq[h] - k * dk[h] + jnp.where(row_w == last_row, dblast, 0.0)
            low = _low_half(GLA_CHUNK)
            for pair in range(GLA_HEADS // 2):
                psl = slice(HEAD_PAD * pair, HEAD_PAD * (pair + 1))
                for ref, val in ((dq_r, dq), (dk_r, dk)):
                    both = jnp.where(low, val[2 * pair], pltpu.roll(val[2 * pair + 1], 64, 1))
                    ref[rows, psl] = both.astype(ref.dtype)

        def loop(t, carry):
            one(False, pre_f, qaf, kaf, vaf, dof, sf_r, dqf, dkf, dvf, gt_f, nc - 1 - t)
            one(True, pre_b, qab, kab, vab, dob, sb_r, dqb, dkb, dvb, gt_b, t)
            return carry

        lax.fori_loop(0, nc, loop, 0, unroll=True)

        def gate_grads(rev, pre, z_r, w_r, dz_r, dw_r, dbias_r):
            g_s, db_s = pre[4], pre[8]
            back_m = _mx((tri_f if rev else tri_b).astype(F32))
            db = db_s[...]
            dla = jnp.concatenate([_chunk_sums(back_m, db[GLA_CHUNK * c:GLA_CHUNK * (c + 1)]) for c in range(nc)],
                                  axis=0)
            dg = dla * (1.0 / GLA_GATE_NORM) * (1.0 / (1.0 + jnp.exp(g_s[...])))
            dg_m = _mx(dg)
            dz_r[...] = _dot_nt(dg_m, w_r[...])
            dw_r[...] += _dot_tn(_mx(z_r[...]), dg_m)
            dbias_r[...] += jnp.sum(dg, axis=0, keepdims=True)

        gate_grads(False, pre_f, zaf, wgf_r, dzf, dwf, dbf)
        gate_grads(True, pre_b, zab, wgb_r, dzb, dwb, dbb)

    last_first = lambda i: (nb - 1 - i, 0)
    first_last = lambda i: (i, 0)
    s_shape = (nc, GLA_HEADS, HEAD_PAD, HEAD_PAD)

    def ins(m):
        return [pl.BlockSpec((br, hw), m), pl.BlockSpec((br, hw), m), pl.BlockSpec((br, hw), m),
                pl.BlockSpec((br, 128), m), pl.BlockSpec((br, hw), m),
                pl.BlockSpec(s_shape, lambda i: m(i) + (0, 0))]

    def outs(m):
        return [pl.BlockSpec((br, hw // 2), m), pl.BlockSpec((br, hw // 2), m), pl.BlockSpec((br, hw), m),
                pl.BlockSpec((br, 128), m), _full_spec((128, hw)), _full_spec((1, hw))]

    out_shape = [jax.ShapeDtypeStruct((L, hw // 2), MXU_DTYPE)] * 2 + [
        jax.ShapeDtypeStruct((L, hw), MXU_DTYPE),
        jax.ShapeDtypeStruct((L, 128), F32), jax.ShapeDtypeStruct((128, hw), F32),
        jax.ShapeDtypeStruct((1, hw), F32)]
    wspecs = [_full_spec((128, hw)), _full_spec((1, hw))] * 2
    body, extra, extra_specs = _after(body, 16, dep)
    pre_scratch = ([pltpu.VMEM((br, hw), MXU_DTYPE)] * 3 + [pltpu.VMEM((nc, 1, hw), F32)]
                   + [pltpu.VMEM((br, hw), F32)] * 5)
    return pl.pallas_call(
        body, name="gla_bwd", grid=(nb,),
        in_specs=ins(last_first) + ins(first_last) + wspecs + extra_specs,
        out_specs=outs(last_first) + outs(first_last),
        out_shape=out_shape + out_shape,
        scratch_shapes=[pltpu.VMEM(s_shape[1:], F32), pltpu.VMEM(s_shape[1:], F32), pre_scratch, pre_scratch],
        compiler_params=_params(("arbitrary",), VMEM_BIG),
    )(qa, ka, va, za, do, sf, qa, ka, va, za, do, sb, wgf, bgf, wgb, bgb, *extra)


def _t5_buckets(rel):
    nb = REL_BUCKETS // 2
    ret = (rel > 0).astype(np.int32) * nb
    n = np.abs(rel)
    max_exact = nb // 2
    large = max_exact + (np.log(np.maximum(n, 1).astype(np.float32) / max_exact)
                         / math.log(REL_MAX_DIST / max_exact) * (nb - max_exact)).astype(np.int32)
    large = np.minimum(large, nb - 1)
    return ret + np.where(n < max_exact, n, large)


SWA_GROUP = SWA_Q_HEADS // SWA_KV_HEADS
SWA_SPAN = 3 * SWA_BLOCK
SWA_GROUP_LANES = SWA_GROUP * SWA_BLOCK


def _band_buckets():
    s = np.arange(SWA_SPAN)[:, None]
    c = np.arange(SWA_BLOCK)[None, :]
    return _t5_buckets(s - SWA_BLOCK - c).astype(np.int32)


def _swa_valid(n, seq_len):
    key_pos = (n - 1) * SWA_BLOCK + lax.broadcasted_iota(jnp.int32, (SWA_SPAN, 1), 0)
    return (key_pos >= 0) & (key_pos < seq_len)


def _swa_sink_row(sink_r, kv):
    lane = lax.broadcasted_iota(jnp.int32, (1, SWA_GROUP_LANES), 1)
    row = jnp.full((1, SWA_GROUP_LANES), sink_r[kv * SWA_GROUP], F32)
    for g in range(1, SWA_GROUP):
        row = jnp.where(lane >= g * SWA_BLOCK, sink_r[kv * SWA_GROUP + g], row)
    return row


def _swa_group(ref, kv):
    first = kv * SWA_GROUP
    return jnp.concatenate([ref[:, HEAD_PAD * h:HEAD_PAD * (h + 1)] for h in range(first, first + SWA_GROUP)],
                           axis=0)


def _swa_softmax(scores, bias_t, sink_row, valid):
    st = jnp.where(valid, scores + bias_t, -1e30)
    m = jnp.maximum(jnp.max(st, axis=0, keepdims=True), sink_row)
    p = jnp.exp(st - m)
    e_sink = jnp.exp(sink_row - m)
    inv = 1.0 / (jnp.sum(p, axis=0, keepdims=True) + e_sink)
    return p * inv, e_sink * inv


def _swa_fwd_call(qs, ks, vs, bias, sink, dep=None):
    L = qs.shape[0]

    def body(q_r, k_r, v_r, bias_r, sink_r, o_r):
        n = pl.program_id(0)
        span = pl.ds(pl.multiple_of(n * SWA_BLOCK, SWA_BLOCK), SWA_SPAN)
        valid = _swa_valid(n, L)
        groups = range(SWA_KV_HEADS)
        lanes = [slice(HEAD_PAD * kv, HEAD_PAD * (kv + 1)) for kv in groups]
        scores = [_dot_nt(k_r[span, lanes[kv]], _swa_group(q_r, kv)) for kv in groups]
        probs = [_swa_softmax(scores[kv], bias_r[kv], _swa_sink_row(sink_r, kv), valid)[0] for kv in groups]
        low = _low_half(SWA_BLOCK)
        for kv in groups:
            og = _dot_tn(_mx(probs[kv]), v_r[span, lanes[kv]])
            for pair in range(SWA_GROUP // 2):
                even = og[2 * SWA_BLOCK * pair:2 * SWA_BLOCK * pair + SWA_BLOCK]
                odd = og[2 * SWA_BLOCK * pair + SWA_BLOCK:2 * SWA_BLOCK * (pair + 1)]
                first = HEAD_PAD * (kv * SWA_GROUP // 2 + pair)
                o_r[:, first:first + HEAD_PAD] = jnp.where(low, even, pltpu.roll(odd, 64, 1)).astype(o_r.dtype)

    qw = SWA_Q_HEADS * HEAD_PAD
    body, extra, extra_specs = _after(body, 5, dep)
    return pl.pallas_call(
        body, name="swa_fwd", grid=(L // SWA_BLOCK,),
        in_specs=[_row_spec(SWA_BLOCK, qw), _vmem_spec(), _vmem_spec(), _vmem_spec(),
                  pl.BlockSpec(memory_space=pltpu.SMEM)] + extra_specs,
        out_specs=_row_spec(SWA_BLOCK, qw // 2),
        out_shape=jax.ShapeDtypeStruct((L, qw // 2), MXU_DTYPE),
        compiler_params=_params(("arbitrary",), VMEM_BIG),
    )(qs, ks, vs, bias, sink, *extra)


def _swa_bwd_call(qs, ks, vs, bias, sink, do, dep=None):
    L = qs.shape[0]
    qw = SWA_Q_HEADS * HEAD_PAD
    kw = SWA_KV_HEADS * HEAD_PAD

    def body(q_r, k_r, v_r, bias_r, sink_r, do_r, dq_r, dk_r, dv_r, dbias_r, dsink_r):
        n = pl.program_id(0)

        @pl.when(n == 0)
        def _():
            for ref in (dk_r, dv_r, dbias_r, dsink_r):
                ref[...] = jnp.zeros_like(ref)

        span = pl.ds(pl.multiple_of(n * SWA_BLOCK, SWA_BLOCK), SWA_SPAN)
        valid = _swa_valid(n, L)
        groups = range(SWA_KV_HEADS)
        lanes = [slice(HEAD_PAD * kv, HEAD_PAD * (kv + 1)) for kv in groups]
        kk = [k_r[span, sl] for sl in lanes]
        vv = [v_r[span, sl] for sl in lanes]
        qg = [_swa_group(q_r, kv) for kv in groups]
        dog = [_swa_group(do_r, kv) for kv in groups]
        scores = [_dot_nt(kk[kv], qg[kv]) for kv in groups]
        dp = [_dot_nt(vv[kv], dog[kv]) for kv in groups]
        probs = [_swa_softmax(scores[kv], bias_r[kv], _swa_sink_row(sink_r, kv), valid) for kv in groups]
        ds_m, pn_m = [], []
        for kv in groups:
            pn, p_sink = probs[kv]
            delta = jnp.sum(pn * dp[kv], axis=0, keepdims=True)
            ds = pn * (dp[kv] - delta)
            dsink_r[kv] -= p_sink * delta
            dbias_r[kv] += ds
            ds_m.append(_mx(ds))
            pn_m.append(_mx(pn))
        dqg = [_dot_tn(ds_m[kv], kk[kv]) * 0.125 for kv in groups]
        dkk = [_dot(ds_m[kv], qg[kv]) for kv in groups]
        dvv = [_dot(pn_m[kv], dog[kv]) for kv in groups]
        low = _low_half(SWA_BLOCK)
        for kv in groups:
            for pair in range(SWA_GROUP // 2):
                even = dqg[kv][2 * SWA_BLOCK * pair:2 * SWA_BLOCK * pair + SWA_BLOCK]
                odd = dqg[kv][2 * SWA_BLOCK * pair + SWA_BLOCK:2 * SWA_BLOCK * (pair + 1)]
                first = HEAD_PAD * (kv * SWA_GROUP // 2 + pair)
                dq_r[:, first:first + HEAD_PAD] = jnp.where(low, even, pltpu.roll(odd, 64, 1)).astype(dq_r.dtype)
            dk_r[span, lanes[kv]] += dkk[kv]
            dv_r[span, lanes[kv]] += dvv[kv]

    body, extra, extra_specs = _after(body, 6, dep)
    return pl.pallas_call(
        body, name="swa_bwd", grid=(L // SWA_BLOCK,),
        in_specs=[_row_spec(SWA_BLOCK, qw), _vmem_spec(), _vmem_spec(), _vmem_spec(),
                  pl.BlockSpec(memory_space=pltpu.SMEM), _row_spec(SWA_BLOCK, qw)] + extra_specs,
        out_specs=[_row_spec(SWA_BLOCK, qw // 2), _vmem_spec(), _vmem_spec(), _vmem_spec(), _vmem_spec()],
        out_shape=[jax.ShapeDtypeStruct((L, qw // 2), MXU_DTYPE),
                   jax.ShapeDtypeStruct((L + 2 * SWA_BLOCK, kw), F32),
                   jax.ShapeDtypeStruct((L + 2 * SWA_BLOCK, kw), F32),
                   jax.ShapeDtypeStruct((SWA_KV_HEADS, SWA_SPAN, SWA_GROUP_LANES), F32),
                   jax.ShapeDtypeStruct((SWA_KV_HEADS, 1, SWA_GROUP_LANES), F32)],
        compiler_params=_params(("arbitrary",), VMEM_BIG),
    )(qs, ks, vs, bias, sink, do, *extra)


def _bias_call(rel_bias, buckets, dep=None):
    def body(t_r, bk_r, o_r):
        bk = bk_r[...]
        s = lax.broadcasted_iota(jnp.int32, bk.shape, 0)
        c = lax.broadcasted_iota(jnp.int32, bk.shape, 1)
        in_band = jnp.abs(s - SWA_BLOCK - c) <= SWA_BLOCK
        for h in range(SWA_Q_HEADS):
            acc = jnp.zeros(bk.shape, F32)
            for b in range(REL_BUCKETS):
                acc = jnp.where(bk == b, t_r[b, h], acc)
            g = h % SWA_GROUP
            o_r[h // SWA_GROUP, :, SWA_BLOCK * g:SWA_BLOCK * (g + 1)] = jnp.where(in_band, acc, -1e30)

    body, extra, extra_specs = _after(body, 2, dep)
    return pl.pallas_call(
        body, name="band_bias",
        in_specs=[pl.BlockSpec(memory_space=pltpu.SMEM), _vmem_spec()] + extra_specs, out_specs=_vmem_spec(),
        out_shape=jax.ShapeDtypeStruct((SWA_KV_HEADS, SWA_SPAN, SWA_GROUP_LANES), F32),
    )(rel_bias, buckets, *extra)


def _relbias_call(dbias, dsink, buckets, dep=None):
    def body(db_r, ds_r, bk_r, o_r, os_r):
        bk = bk_r[...]
        rowi = lax.broadcasted_iota(jnp.int32, (REL_BUCKETS, 128), 0)
        lanei = lax.broadcasted_iota(jnp.int32, (REL_BUCKETS, 128), 1)
        lane1 = lax.broadcasted_iota(jnp.int32, (1, 128), 1)
        acc = jnp.zeros((REL_BUCKETS, 128), F32)
        acc_sink = jnp.zeros((1, 128), F32)
        for h in range(SWA_Q_HEADS):
            kv, g = h // SWA_GROUP, h % SWA_GROUP
            lanes = slice(SWA_BLOCK * g, SWA_BLOCK * (g + 1))
            part = db_r[kv, :, lanes]
            for b in range(REL_BUCKETS):
                s = jnp.sum(jnp.where(bk == b, part, 0.0))
                acc = acc + jnp.where((rowi == b) & (lanei == h), s, 0.0)
            acc_sink = acc_sink + jnp.where(lane1 == h, jnp.sum(ds_r[kv, :, lanes]), 0.0)
        o_r[...] = acc
        os_r[...] = acc_sink

    body, extra, extra_specs = _after(body, 3, dep)
    return pl.pallas_call(
        body, name="relbias_grad",
        in_specs=[_vmem_spec()] * 3 + extra_specs, out_specs=[_vmem_spec()] * 2,
        out_shape=[jax.ShapeDtypeStruct((REL_BUCKETS, 128), F32), jax.ShapeDtypeStruct((1, 128), F32)],
    )(dbias, dsink, buckets, *extra)


def _mix_call(o_f, o_b, ga, o_s, x, gn, w_out_p, g_post, g_pre2, dep=None):
    L = x.shape[0]
    tm = min(512, L)
    hw = GLA_HEADS * HEAD_PAD

    def body(of_r, ob_r, ga_r, os_r, x_r, gn_r, w_r, gp_r, g2_r, cat_r, mix_r, h1_r, n2_r):
        gn_v = gn_r[...]
        for h in range(GLA_HEADS):
            sl = slice(HEAD_PAD * h, HEAD_PAD * (h + 1))
            oh = of_r[:, sl] + ob_r[:, sl]
            on = oh * _rms_r(oh) * gn_v
            gate = ga_r[:, sl]
            cat_r[:, sl] = (on * (gate * jax.nn.sigmoid(gate))).astype(cat_r.dtype)
        os_v = os_r[...]
        cat_r[:, hw:] = os_v
        mix = _dot(cat_r[:, :hw], w_r[:hw, :]) + _dot(os_v, w_r[hw:, :])
        mix_r[...] = mix
        h1 = x_r[...] + mix * _rms_r(mix) * gp_r[...]
        h1_r[...] = h1
        n2_r[...] = (h1 * _rms_r(h1) * g2_r[...]).astype(n2_r.dtype)

    body, extra, extra_specs = _after(body, 9, dep)
    return pl.pallas_call(
        body, name="mix_fwd", grid=(L // tm,),
        in_specs=[_row_spec(tm, hw), _row_spec(tm, hw), _row_spec(tm, hw), _row_spec(tm, OUT_PAD - hw),
                  _row_spec(tm, D_MODEL), _full_spec((1, HEAD_PAD)), _vmem_spec(),
                  _full_spec((1, D_MODEL)), _full_spec((1, D_MODEL))] + extra_specs,
        out_specs=[_row_spec(tm, OUT_PAD), _row_spec(tm, D_MODEL), _row_spec(tm, D_MODEL), _row_spec(tm, D_MODEL)],
        out_shape=[jax.ShapeDtypeStruct((L, OUT_PAD), MXU_DTYPE), jax.ShapeDtypeStruct((L, D_MODEL), F32),
                   jax.ShapeDtypeStruct((L, D_MODEL), F32), jax.ShapeDtypeStruct((L, D_MODEL), MXU_DTYPE)],
        compiler_params=_params(("arbitrary",), VMEM_BIG),
    )(o_f, o_b, ga, o_s, x, gn, w_out_p, g_post, g_pre2, *extra)


def _mlp_fwd_call(n2, h1, tgt, w_ud, g_post):
    L = n2.shape[0]
    tm = min(512, L)
    blk = D_FF // N_CHIPS

    def body(n2_r, h1_r, t_r, w_r, g_r, a_r, rz_r, dh2_r, dff_r, loss_r, dg_r):
        @pl.when(pl.program_id(0) == 0)
        def _():
            loss_r[...] = jnp.zeros_like(loss_r)
            dg_r[...] = jnp.zeros_like(dg_r)

        n2v = n2_r[...]
        ff = jnp.zeros((tm, D_MODEL), F32)
        for j in range(N_CHIPS):
            sl = slice(blk * j, blk * (j + 1))
            rz = jnp.maximum(_dot(n2v, w_r[j, 0]), 0.0)
            a = _mx(rz * rz)
            rz_r[:, sl] = rz.astype(rz_r.dtype)
            a_r[:, sl] = a
            ff = ff + _dot(a, w_r[j, 1])
        g = g_r[...]
        r = _rms_r(ff)
        err = h1_r[...] + ff * r * g - t_r[...]
        loss_r[...] += 0.5 * jnp.sum(err * err) / D_MODEL
        dh2 = err * (1.0 / D_MODEL)
        dh2_r[...] = dh2
        dff, dg = _rms_bwd(ff, r, g, dh2)
        dff_r[...] = dff.astype(dff_r.dtype)
        dg_r[...] += dg

    return pl.pallas_call(
        body, name="mlp_fwd", grid=(L // tm,),
        in_specs=[_row_spec(tm, D_MODEL), _row_spec(tm, D_MODEL), _row_spec(tm, D_MODEL),
                  _vmem_spec(), _full_spec((1, D_MODEL))],
        out_specs=[_row_spec(tm, D_FF), _row_spec(tm, D_FF), _row_spec(tm, D_MODEL), _row_spec(tm, D_MODEL),
                   _full_spec((1, 128)), _full_spec((1, D_MODEL))],
        out_shape=[jax.ShapeDtypeStruct((L, D_FF), MXU_DTYPE), jax.ShapeDtypeStruct((L, D_FF), MXU_DTYPE),
                   jax.ShapeDtypeStruct((L, D_MODEL), F32), jax.ShapeDtypeStruct((L, D_MODEL), MXU_DTYPE),
                   jax.ShapeDtypeStruct((1, 128), F32), jax.ShapeDtypeStruct((1, D_MODEL), F32)],
        compiler_params=_params(("arbitrary",), VMEM_BIG),
    )(n2, h1, tgt, w_ud, g_post)


def _mlp_bwd_call(dff, rz, w_ud):
    L = dff.shape[0]
    tm = min(512, L)
    blk = D_FF // N_CHIPS

    def body(dff_r, rz_r, w_r, dz_r, dn2_r):
        dffv = dff_r[...]
        dn2 = jnp.zeros((tm, D_MODEL), F32)
        for j in range(N_CHIPS):
            sl = slice(blk * j, blk * (j + 1))
            dz = _mx(_dot_nt(dffv, w_r[j, 1]) * 2.0 * rz_r[:, sl].astype(F32))
            dz_r[:, sl] = dz
            dn2 = dn2 + _dot_nt(dz, w_r[j, 0])
        dn2_r[...] = dn2

    return pl.pallas_call(
        body, name="mlp_bwd", grid=(L // tm,),
        in_specs=[_row_spec(tm, D_MODEL), _row_spec(tm, D_FF), _vmem_spec()],
        out_specs=[_row_spec(tm, D_FF), _row_spec(tm, D_MODEL)],
        out_shape=[jax.ShapeDtypeStruct((L, D_FF), MXU_DTYPE), jax.ShapeDtypeStruct((L, D_MODEL), F32)],
        compiler_params=_params(("arbitrary",), VMEM_BIG),
    )(dff, rz, w_ud)


def _mlp_wgrad_call(a, dff, n2, dz):
    L = a.shape[0]
    tf = 512
    per = (D_FF // N_CHIPS) // tf

    def body(a_r, dff_r, n2_r, dz_r, dwd_r, dwu_r):
        dwd_r[...] = _dot_tn(a_r[...], dff_r[...])
        dwu_r[...] = _dot_tn(n2_r[...], dz_r[...])

    return pl.pallas_call(
        body, name="mlp_wgrad", grid=(D_FF // tf,),
        in_specs=[pl.BlockSpec((L, tf), lambda j: (0, j)), _vmem_spec(), _vmem_spec(),
                  pl.BlockSpec((L, tf), lambda j: (0, j))],
        out_specs=[pl.BlockSpec((tf, D_MODEL), lambda j: (j, 0)),
                   pl.BlockSpec((None, D_MODEL, tf), lambda j: (j // per, 0, j % per))],
        out_shape=[jax.ShapeDtypeStruct((D_FF, D_MODEL), F32),
                   jax.ShapeDtypeStruct((N_CHIPS, D_MODEL, D_FF // N_CHIPS), F32)],
        compiler_params=_params(("arbitrary",), VMEM_BIG),
    )(a, dff, n2, dz)


def _mix_bwd_call(dn2, dh2, h1, mix, cat, o_f, o_b, ga, gn, g_post, g_pre2, w_out_p):
    L = dn2.shape[0]
    tm = min(512, L)
    hw = GLA_HEADS * HEAD_PAD

    def body(dn2_r, dh2_r, h1_r, mix_r, cat_r, of_r, ob_r, ga_r, gn_r, gp_r, g2_r, w_r,
             dh1_r, do_r, dga_r, dos_r, dw_r, dg2_r, dgp_r, dgn_r):
        @pl.when(pl.program_id(0) == 0)
        def _():
            for ref in (dw_r, dg2_r, dgp_r, dgn_r):
                ref[...] = jnp.zeros_like(ref)

        parts = [slice(start, start + min(256, tm)) for start in range(0, tm, 256)]
        dmix_m = []
        for rs in parts:
            h1 = h1_r[rs, :]
            dx2, dg2 = _rms_bwd(h1, _rms_r(h1), g2_r[...], dn2_r[rs, :])
            dh1 = dh2_r[rs, :] + dx2
            dh1_r[rs, :] = dh1
            dg2_r[...] += dg2
            mix = mix_r[rs, :]
            dmix, dgp = _rms_bwd(mix, _rms_r(mix), gp_r[...], dh1)
            dgp_r[...] += dgp
            dmix_m.append(_mx(dmix))
        dcat = [_dot_nt(d, w_r[...]) for d in dmix_m]
        for rs, d in zip(parts, dmix_m):
            dw_r[...] += _dot_tn(cat_r[rs, :], d)
        gn_v = gn_r[...]
        dgn = jnp.zeros((1, HEAD_PAD), F32)
        for rs, dc in zip(parts, dcat):
            dos_r[rs, :] = _spread_heads(dc[:, hw:]).astype(dos_r.dtype)
            for h in range(GLA_HEADS):
                sl = slice(HEAD_PAD * h, HEAD_PAD * (h + 1))
                oh = of_r[rs, sl] + ob_r[rs, sl]
                rr = _rms_r(oh)
                xh = oh * rr
                gate = ga_r[rs, sl]
                sg = jax.nn.sigmoid(gate)
                silu = gate * sg
                doa = dc[:, sl]
                dga_r[rs, sl] = (doa * (xh * gn_v) * (sg + silu * (1.0 - sg))).astype(dga_r.dtype)
                don = doa * silu
                gd = don * gn_v
                do_r[rs, sl] = rr * (gd - xh * jnp.mean(gd * xh, axis=-1, keepdims=True))
                dgn = dgn + jnp.sum(don * xh, axis=0, keepdims=True)
        dgn_r[...] += dgn

    return pl.pallas_call(
        body, name="mix_bwd", grid=(L // tm,),
        in_specs=[_row_spec(tm, D_MODEL)] * 4 + [_row_spec(tm, OUT_PAD)] + [_row_spec(tm, hw)] * 3
        + [_full_spec((1, HEAD_PAD)), _full_spec((1, D_MODEL)), _full_spec((1, D_MODEL)), _vmem_spec()],
        out_specs=[_row_spec(tm, D_MODEL), _row_spec(tm, hw), _row_spec(tm, hw),
                   _row_spec(tm, SWA_Q_HEADS * HEAD_PAD),
                   _full_spec((OUT_PAD, D_MODEL)), _full_spec((1, D_MODEL)), _full_spec((1, D_MODEL)),
                   _full_spec((1, HEAD_PAD))],
        out_shape=[jax.ShapeDtypeStruct((L, D_MODEL), F32), jax.ShapeDtypeStruct((L, hw), F32),
                   jax.ShapeDtypeStruct((L, hw), MXU_DTYPE),
                   jax.ShapeDtypeStruct((L, SWA_Q_HEADS * HEAD_PAD), MXU_DTYPE),
                   jax.ShapeDtypeStruct((OUT_PAD, D_MODEL), F32), jax.ShapeDtypeStruct((1, D_MODEL), F32),
                   jax.ShapeDtypeStruct((1, D_MODEL), F32), jax.ShapeDtypeStruct((1, HEAD_PAD), F32)],
        compiler_params=_params(("arbitrary",), VMEM_BIG),
    )(dn2, dh2, h1, mix, cat, o_f, o_b, ga, gn, g_post, g_pre2, w_out_p)


def _in_bwd_call(x, dh1, g_pre, w_in_t, pairs, singles, halos, dep=None):
    L = x.shape[0]
    tm = min(512, L)
    per = tm // SWA_BLOCK
    n_pair, n_single, n_halo = len(pairs), len(singles), len(halos)
    groups = [c for c, _ in pairs] + [c for c, _ in singles] + [c for c, _ in halos]

    def body(*refs):
        x_r, dh1_r, g_r, w_r = refs[:4]
        pair_refs = refs[4:4 + 2 * n_pair]
        single_refs = refs[4 + 2 * n_pair:4 + 2 * n_pair + n_single]
        halo_refs = refs[4 + 2 * n_pair + n_single:4 + 2 * n_pair + n_single + per * n_halo]
        dx_r, dw_r, dg_r = refs[4 + 2 * n_pair + n_single + per * n_halo:]

        @pl.when(pl.program_id(0) == 0)
        def _():
            dw_r[...] = jnp.zeros_like(dw_r)
            dg_r[...] = jnp.zeros_like(dg_r)

        xv = x_r[...]
        r = _rms_r(xv)
        g = g_r[...]
        u = _mx(xv * r * g)
        vals = [pair_refs[2 * i][...].astype(F32) + pair_refs[2 * i + 1][...].astype(F32) for i in range(n_pair)]
        vals += [ref[...].astype(F32) for ref in single_refs]
        vals += [jnp.concatenate([ref[...] for ref in halo_refs[per * i:per * (i + 1)]], axis=0)
                 for i in range(n_halo)]
        ds = [_mx(_squeeze_heads(val) if heads else val) for (_, _, heads), val in zip(groups, vals)]
        du = jnp.zeros((tm, D_MODEL), F32)
        for (first, rows, _), d in zip(groups, ds):
            du = du + _dot(d, w_r[first:first + rows, :])
        for (first, rows, _), d in zip(groups, ds):
            dw_r[first:first + rows, :] += _dot_tn(d, u)
        dx, dg = _rms_bwd(xv, r, g, du)
        dx_r[...] = dh1_r[...] + dx
        dg_r[...] += dg

    arrays = [a for _, pr in pairs for a in pr] + [a for _, a in singles]
    specs = [_row_spec(tm, a.shape[1]) for a in arrays]
    for _, a in halos:
        specs += [pl.BlockSpec((SWA_BLOCK, a.shape[1]), lambda i, j=j: (per * i + 1 + j, 0)) for j in range(per)]
        arrays += [a] * per
    body, extra, extra_specs = _after(body, 4 + len(arrays), dep)
    return pl.pallas_call(
        body, name="in_bwd", grid=(L // tm,),
        in_specs=[_row_spec(tm, D_MODEL), _row_spec(tm, D_MODEL), _full_spec((1, D_MODEL)), _vmem_spec()] + specs
        + extra_specs,
        out_specs=[_row_spec(tm, D_MODEL), _full_spec((IN_COLS, D_MODEL)), _full_spec((1, D_MODEL))],
        out_shape=[jax.ShapeDtypeStruct((L, D_MODEL), F32), jax.ShapeDtypeStruct((IN_COLS, D_MODEL), F32),
                   jax.ShapeDtypeStruct((1, D_MODEL), F32)],
        compiler_params=_params(("arbitrary",), VMEM_BIG),
    )(x, dh1, g_pre, w_in_t, *arrays, *extra)


def _adamw_math(w, g, m, v):
    m = ADAM_B1 * m + (1.0 - ADAM_B1) * g
    v = ADAM_B2 * v + (1.0 - ADAM_B2) * (g * g)
    m_hat = m / (1.0 - ADAM_B1 ** ADAM_STEP)
    v_hat = v / (1.0 - ADAM_B2 ** ADAM_STEP)
    delta = -ADAM_LR * (m_hat / (jnp.sqrt(v_hat) + ADAM_EPS) + ADAM_WD * w)
    return delta, m, v


def _adamw_call(w, g, m, v, name, dep=None):
    rows, cols = w.shape
    tr = min(256, rows)

    def body(w_r, g_r, m_r, v_r, d_r, nm_r, nv_r):
        d_r[...], nm_r[...], nv_r[...] = _adamw_math(w_r[...], g_r[...], m_r[...], v_r[...])

    if rows % tr == 0:
        spec, steps = _row_spec(tr, cols), rows // tr
    else:
        spec, steps = pl.BlockSpec((rows, 256), lambda i: (0, i)), cols // 256
    body, extra, extra_specs = _after(body, 4, dep)
    return pl.pallas_call(
        body, name=name, grid=(steps,),
        in_specs=[spec] * 4 + extra_specs, out_specs=[spec] * 3,
        out_shape=[jax.ShapeDtypeStruct(w.shape, F32)] * 3,
        compiler_params=_params(("arbitrary",)),
    )(w, g, m, v, *extra)


def _position():
    return lax.axis_index("x"), lax.axis_index("y"), lax.axis_index("c")


def _other_chips(x, y):
    return [(1 - x, y), (x, 1 - y), (1 - x, 1 - y)]


ROWS, COLS = -2, -1


def _half(ref, which, axis):
    size = ref.shape[axis] // 2
    span = pl.ds(pl.multiple_of(which * size, 16 if axis == ROWS else 128), size)
    index = [slice(None)] * len(ref.shape)
    index[axis] = span
    return ref.at[tuple(index)]


def _quarter(ref, half, which, axis):
    size = ref.shape[axis] // 4
    span = pl.ds(pl.multiple_of((2 * half + which) * size, 16 if axis == ROWS else 128), size)
    index = [slice(None)] * len(ref.shape)
    index[axis] = span
    return ref.at[tuple(index)]


def _first_gather_call(shards, axes, routed):
    n = len(shards)
    per = 7

    def body(*refs):
        srcs, outs = refs[:n], refs[n:2 * n]
        send_sems, recv_sems, local_sems = refs[2 * n:]
        x, y, c = _position()
        me, sibling = (x, y, c), (x, y, 1 - c)
        x_side, y_side, across = _other_chips(x, y)
        local = [pltpu.make_async_copy(srcs[a], outs[a].at[2 * x + y], local_sems.at[a]) for a in range(n)]
        for cp in local:
            cp.start()

        def copy(a, k, dst, to, src=None):
            return pltpu.make_async_remote_copy(
                src_ref=dst if src is None else src, dst_ref=dst, send_sem=send_sems.at[per * a + k],
                recv_sem=recv_sems.at[per * a + k], device_id=to, device_id_type=MESH_ID)

        def half(a, chip, pc):
            return _half(outs[a].at[2 * chip[0] + chip[1]], pc, axes[a])

        def quarter(a, chip, q):
            return _quarter(outs[a].at[2 * chip[0] + chip[1]], c, q, axes[a])

        sends = []
        for a in range(n):
            mine = _half(srcs[a], c, axes[a])
            targets = (x_side, y_side) if routed[a] else (x_side, y_side, across)
            sends += [copy(a, j, half(a, (x, y), c), (*chip, c), src=mine) for j, chip in enumerate(targets)]
        for cp in sends:
            cp.start()
        for a in range(n):
            for j, chip in enumerate((x_side, y_side)):
                copy(a, j, half(a, chip, c), me).wait_recv()
                if routed[a]:
                    other = (y_side, x_side)[j]
                    sends.append(copy(a, 2 + j, quarter(a, chip, j), (*other, c)))
                    sends[-1].start()
                sends.append(copy(a, 4 + j, half(a, chip, c), sibling))
                sends[-1].start()
        for a in range(n):
            if routed[a]:
                for j in range(2):
                    copy(a, 2 + j, quarter(a, across, j), me).wait_recv()
            else:
                copy(a, 2, half(a, across, c), me).wait_recv()
            sends.append(copy(a, 6, half(a, across, c), sibling))
            sends[-1].start()
        for a in range(n):
            for k, chip in ((4, x_side), (5, y_side), (6, across)):
                copy(a, k, half(a, chip, 1 - c), me).wait_recv()
        for cp in sends:
            cp.wait_send()
        for cp in local:
            cp.wait()

    return pl.pallas_call(
        body, name="first_gather",
        in_specs=[_any_spec()] * n, out_specs=[_any_spec()] * n,
        out_shape=[jax.ShapeDtypeStruct((N_CHIPS,) + s.shape, s.dtype) for s in shards],
        scratch_shapes=[pltpu.SemaphoreType.DMA((per * n,)), pltpu.SemaphoreType.DMA((per * n,)),
                        pltpu.SemaphoreType.DMA((n,))],
    )(*shards)


def _split_start(name, arrays, n_copies, plan):
    n = len(arrays)

    def body(*refs):
        ins, send_sems, recv_sems, token = refs[:n], refs[n], refs[n + 1], refs[-1]
        for k, (src, dst, to, _) in enumerate(plan(ins)):
            pltpu.make_async_remote_copy(src_ref=src, dst_ref=dst, send_sem=send_sems.at[k],
                                         recv_sem=recv_sems.at[k], device_id=to, device_id_type=MESH_ID).start()
        token[...] = jnp.zeros_like(token)

    hbm = pl.BlockSpec(memory_space=pltpu.HBM)
    sem = pl.BlockSpec(memory_space=pltpu.SEMAPHORE)
    out = pl.pallas_call(
        body, name=name,
        out_shape=(pltpu.SemaphoreType.DMA((n_copies,)), pltpu.SemaphoreType.DMA((n_copies,)))
        + tuple(pltpu.HBM(a.shape, a.dtype) for a in arrays) + (jax.ShapeDtypeStruct((8, 128), F32),),
        in_specs=[hbm] * n, out_specs=(sem, sem) + (hbm,) * n + (_vmem_spec(),),
        input_output_aliases={i: 2 + i for i in range(n)},
        compiler_params=pltpu.CompilerParams(has_side_effects=pltpu.SideEffectType.DATAFLOW_SIDE_EFFECTING),
    )(*[pltpu.with_memory_space_constraint(a, pltpu.HBM) for a in arrays])
    return (out[0], out[1], tuple(out[2:2 + n])), out[-1]


def _split_wait(name, handle, n_copies, plan, after):
    send_sems, recv_sems, arrays = handle
    n = len(arrays)

    def body(*refs):
        ins, s_sems, r_sems = refs[:n], refs[n], refs[n + 1]
        for k, (src, dst, to, landed) in enumerate(plan(ins)):
            cp = pltpu.make_async_remote_copy(src_ref=src, dst_ref=landed, send_sem=s_sems.at[k],
                                              recv_sem=r_sems.at[k], device_id=to, device_id_type=MESH_ID)
            cp.wait_send()
            cp.wait_recv()

    hbm = pl.BlockSpec(memory_space=pltpu.HBM)
    sem = pl.BlockSpec(memory_space=pltpu.SEMAPHORE)
    out = pl.pallas_call(
        body, name=name,
        out_shape=tuple(pltpu.HBM(a.shape, a.dtype) for a in arrays),
        in_specs=[hbm] * n + [sem, sem, _any_spec()], out_specs=(hbm,) * n,
        input_output_aliases={i: i for i in range(n)},
        compiler_params=pltpu.CompilerParams(has_side_effects=pltpu.SideEffectType.DATAFLOW_SIDE_EFFECTING),
    )(*arrays, send_sems, recv_sems, after)
    return tuple(out)


def _gather_plans(axes):
    n = len(axes)

    def stage_one(refs):
        x, y, c = _position()
        copies = []
        for a, axis in enumerate(axes):
            for px, py in _other_chips(x, y):
                copies.append((_half(refs[a], c, axis), _half(refs[n + a].at[2 * x + y], c, axis),
                               (px, py, c), _half(refs[n + a].at[2 * px + py], c, axis)))
        return copies

    def stage_two(refs):
        x, y, c = _position()
        copies = []
        for a, axis in enumerate(axes):
            for px, py in _other_chips(x, y):
                piece = _half(refs[n + a].at[2 * px + py], c, axis)
                copies.append((piece, piece, (x, y, 1 - c), _half(refs[n + a].at[2 * px + py], 1 - c, axis)))
        return copies

    return stage_one, stage_two


def _pair_swap_plan(axes):
    n = len(axes)

    def plan(refs):
        x, y, c = _position()
        return [(_half(refs[a], 1 - c, axes[a]), refs[n + a], (x, y, 1 - c), refs[n + a]) for a in range(n)]

    return plan


def _chip_swap_plan(n):
    def plan(refs):
        x, y, c = _position()
        copies = []
        for a in range(n):
            for j, (px, py) in enumerate(_other_chips(x, y)):
                copies.append((refs[a].at[2 * px + py], refs[n + a].at[j], (px, py, c), refs[n + a].at[j]))
        return copies

    return plan


def _pair_join_plan(axes):
    def plan(refs):
        x, y, c = _position()
        copies = []
        for a, axis in enumerate(axes):
            mine = _half(refs[a], c, axis)
            copies.append((mine, mine, (x, y, 1 - c), _half(refs[a], 1 - c, axis)))
        return copies

    return plan


def _pair_add_call(g, got, pos, name, axis):
    rows, cols = got.shape[1], got.shape[2]
    tr = min(512, rows) if axis == ROWS else rows
    nblk = rows // tr
    if axis == ROWS:
        mine = lambda j, i, p: (j, p[1] * nblk + i, 0)
    else:
        mine = lambda j, i, p: (j, 0, p[1])

    def body(pos_r, g_r, got_r, o_r):
        o_r[...] = (g_r[...] + got_r[...]).astype(o_r.dtype)

    return pl.pallas_call(
        body, name=name,
        grid_spec=pltpu.PrefetchScalarGridSpec(
            num_scalar_prefetch=1, grid=(N_CHIPS, nblk),
            in_specs=[pl.BlockSpec((None, tr, cols), mine),
                      pl.BlockSpec((None, tr, cols), lambda j, i, p: (j, i, 0))],
            out_specs=pl.BlockSpec((None, tr, cols), lambda j, i, p: (j, i, 0))),
        out_shape=jax.ShapeDtypeStruct(got.shape, COMM_DTYPE),
        compiler_params=_params(("arbitrary", "arbitrary"), VMEM_BIG),
    )(pos, g, got)


def _chip_add_call(hsum, got, pos, name, axis):
    rows, cols = hsum.shape[1], hsum.shape[2]
    tr = min(512, rows) if axis == ROWS else rows
    nblk = rows // tr
    if axis == ROWS:
        out_shape, mine = (2 * rows, cols), (lambda i, p: (p[1] * nblk + i, 0))
    else:
        out_shape, mine = (rows, 2 * cols), (lambda i, p: (0, p[1]))

    def body(pos_r, own_r, got_r, o_r):
        acc = own_r[...].astype(F32)
        for j in range(3):
            acc = acc + got_r[j].astype(F32)
        o_r[...] = acc

    return pl.pallas_call(
        body, name=name,
        grid_spec=pltpu.PrefetchScalarGridSpec(
            num_scalar_prefetch=1, grid=(nblk,),
            in_specs=[pl.BlockSpec((None, tr, cols), lambda i, p: (p[0], i, 0)),
                      pl.BlockSpec((3, tr, cols), lambda i, p: (0, i, 0))],
            out_specs=pl.BlockSpec((tr, cols), mine)),
        out_shape=jax.ShapeDtypeStruct(out_shape, F32),
        compiler_params=_params(("arbitrary",), VMEM_BIG),
    )(pos, hsum, got)


SMALL_NAMES = ("norm_mix_pre", "norm_mix_post", "norm_mlp_pre", "norm_mlp_post", "b_gate_fwd", "b_gate_bwd",
               "gla_norm", "swa_sink", "rel_bias")


N_DEVICES = 8


def _small_pack_call(grads, extras):
    operands = list(grads) + list(extras)

    def body(*refs):
        g_refs, (all_a, all_b) = refs[:len(operands)], refs[len(operands):]
        x, y, c = _position()
        me = 4 * x + 2 * y + c
        all_a[me] = jnp.zeros(all_a.shape[1:], F32)
        all_b[me] = jnp.zeros(all_b.shape[1:], F32)
        for i in range(4):
            all_a[me, i:i + 1, :] = g_refs[i][...]
        all_a[me, 4:5, 0:256] = g_refs[4][...]
        all_a[me, 5:6, 0:256] = g_refs[5][...]
        all_a[me, 6:7, 0:128] = g_refs[6][...]
        all_a[me, 7:8, 0:128] = g_refs[7][...]
        all_a[me, 7:8, 128:256] = g_refs[11][...]
        all_b[me, 0:32, 0:128] = g_refs[8][...]
        all_b[me, 32:48, :] = g_refs[9][...]
        all_b[me, 48:64, :] = g_refs[10][...]

    out_shape = [jax.ShapeDtypeStruct((N_DEVICES, 8, D_MODEL), F32), jax.ShapeDtypeStruct((N_DEVICES, 64, 256), F32)]
    return pl.pallas_call(
        body, name="small_pack",
        in_specs=[_whole_spec(a.shape) for a in operands], out_specs=[_whole_spec(s.shape) for s in out_shape],
        out_shape=out_shape,
    )(*operands)


def _everyone_plan(n):
    def plan(refs):
        x, y, c = _position()
        copies = []
        for k in range(1, N_DEVICES):
            px = 1 - x if (k >> 2) & 1 else x
            py = 1 - y if (k >> 1) & 1 else y
            pc = 1 - c if k & 1 else c
            for a in range(n):
                mine = refs[a].at[4 * x + 2 * y + c]
                copies.append((mine, mine, (px, py, pc), refs[a].at[4 * px + 2 * py + pc]))
        return copies

    return plan


def _small_adamw_call(all_a, all_b, params):
    n_small = len(SMALL_NAMES)
    wmv = [t for p in params for t in p]
    shapes = [p[0].shape for p in params]

    def body(*refs):
        all_a, all_b = refs[:2]
        wmv_refs = refs[2:2 + 3 * n_small]
        out_refs = refs[2 + 3 * n_small:]
        sum_a, sum_b = all_a[0], all_b[0]
        for d in range(1, N_DEVICES):
            sum_a = sum_a + all_a[d]
            sum_b = sum_b + all_b[d]
        gsum = [sum_a[0:1], sum_a[1:2], sum_a[2:3], sum_a[3:4], sum_a[4:5, 0:256], sum_a[5:6, 0:256],
                sum_a[6:7, 0:128], sum_a[7:8, 0:SWA_Q_HEADS], sum_b[0:32, 0:SWA_Q_HEADS]]
        for i in range(n_small):
            w_r, m_r, v_r = wmv_refs[3 * i:3 * i + 3]
            delta, new_m, new_v = _adamw_math(w_r[...], gsum[i], m_r[...], v_r[...])
            out_refs[4 * i][...] = gsum[i]
            out_refs[4 * i + 1][...] = delta
            out_refs[4 * i + 2][...] = new_m
            out_refs[4 * i + 3][...] = new_v
        out_refs[4 * n_small][...] = sum_b[32:48]
        out_refs[4 * n_small + 1][...] = sum_b[48:64]
        out_refs[4 * n_small + 2][...] = sum_a[7:8, 128:256]

    out_shape = [jax.ShapeDtypeStruct(s, F32) for s in shapes for _ in range(4)]
    out_shape += [jax.ShapeDtypeStruct((GLA_GATE_RANK, 256), F32)] * 2 + [jax.ShapeDtypeStruct((1, 128), F32)]
    out = pl.pallas_call(
        body, name="small_adamw",
        in_specs=[_whole_spec(a.shape) for a in [all_a, all_b] + wmv],
        out_specs=[_whole_spec(s.shape) for s in out_shape],
        out_shape=out_shape,
    )(all_a, all_b, *wmv)
    per_name = [tuple(out[4 * i:4 * i + 4]) for i in range(n_small)]
    return per_name, out[4 * n_small], out[4 * n_small + 1], out[4 * n_small + 2]


def _pad_heads(t, n_heads, axis=-1):
    axis = axis % t.ndim
    shape = t.shape
    t = t.reshape(shape[:axis] + (n_heads, 64) + shape[axis + 1:])
    pad = [(0, 0)] * t.ndim
    pad[axis + 1] = (0, HEAD_PAD - 64)
    return jnp.pad(t, pad).reshape(shape[:axis] + (n_heads * HEAD_PAD,) + shape[axis + 1:])


def _unpad_heads(t, n_heads, axis=-1):
    axis = axis % t.ndim
    shape = t.shape
    t = t.reshape(shape[:axis] + (n_heads, HEAD_PAD) + shape[axis + 1:])
    t = lax.slice_in_dim(t, 0, 64, axis=axis + 1)
    return t.reshape(shape[:axis] + (n_heads * 64,) + shape[axis + 1:])


def _pad_gate(w, first_row):
    return jnp.pad(_pad_heads(w, 4), ((first_row, 128 - GLA_GATE_RANK - first_row), (0, 0)))


def _own_slot(shard, chip):
    zone = lax.empty((N_CHIPS,) + shard.shape, shard.dtype)
    return lax.dynamic_update_slice(zone, shard[None], (chip,) + (0,) * shard.ndim)


def _reduce_to_owners(grads, axes, pos, tag, overlap):
    n = len(grads)

    def half_shape(g, axis):
        return (N_CHIPS, g.shape[1] // 2, g.shape[2]) if axis == ROWS else (N_CHIPS, g.shape[1], g.shape[2] // 2)

    lands = [lax.empty(half_shape(g, axis), F32) for g, axis in zip(grads, axes)]
    handle, token = _split_start(tag + "_pair_start", list(grads) + lands, n, _pair_swap_plan(axes))
    got = _split_wait(tag + "_pair_wait", handle, n, _pair_swap_plan(axes), overlap[0](token))
    sums = [_pair_add_call(got[a], got[n + a], pos, f"{tag}_pair_add{a}", axes[a]) for a in range(n)]
    lands = [lax.empty((3,) + s.shape[1:], s.dtype) for s in sums]
    handle, token = _split_start(tag + "_chip_start", sums + lands, 3 * n, _chip_swap_plan(n))
    got = _split_wait(tag + "_chip_wait", handle, 3 * n, _chip_swap_plan(n), overlap[1](token))
    halves = [_chip_add_call(got[a], got[n + a], pos, f"{tag}_chip_add{a}", axes[a]) for a in range(n)]
    handle, token = _split_start(tag + "_join_start", halves, n, _pair_join_plan(axes))
    return _split_wait(tag + "_join_wait", handle, n, _pair_join_plan(axes), overlap[2](token))


def kernel(x, norm_mix_pre, w_in, w_gate_up_fwd, b_gate_fwd, w_gate_up_bwd, b_gate_bwd, gla_norm, swa_sink, rel_bias, w_out, norm_mix_post, norm_mlp_pre, w_up, w_down, norm_mlp_post, loss_target, m_norm_mix_pre, m_w_in, m_w_gate_up_fwd, m_b_gate_fwd, m_w_gate_up_bwd, m_b_gate_bwd, m_gla_norm, m_swa_sink, m_rel_bias, m_w_out, m_norm_mix_post, m_norm_mlp_pre, m_w_up, m_w_down, m_norm_mlp_post, v_norm_mix_pre, v_w_in, v_w_gate_up_fwd, v_b_gate_fwd, v_w_gate_up_bwd, v_b_gate_bwd, v_gla_norm, v_swa_sink, v_rel_bias, v_w_out, v_norm_mix_post, v_norm_mlp_pre, v_w_up, v_w_down, v_norm_mlp_post):
    given = dict(locals())
    cx, cy, cc = _position()
    chip = (2 * cx + cy).astype(jnp.int32)
    pos = jnp.stack([chip, cc.astype(jnp.int32)])
    seq, tgt = x[0], loss_target[0]
    L = seq.shape[0]

    gates = jnp.concatenate([w_gate_up_fwd[0], w_gate_up_bwd[0]], axis=0).astype(COMM_DTYPE)
    all_in, all_gates = _first_gather_call([w_in[0].T.astype(COMM_DTYPE), gates], [COLS, ROWS], [True, False])
    rest = [w_out[0].astype(COMM_DTYPE), jnp.stack([w_up[0], w_down[0]]).astype(COMM_DTYPE)]
    stage_one, stage_two = _gather_plans([ROWS, ROWS])
    handle, token = _split_start("gather_chip_start", rest + [_own_slot(s, chip) for s in rest] + [all_gates], 6,
                                 stage_one)

    w_in_t = _mx(all_in.reshape(IN_COLS, D_MODEL))
    gates_full = jnp.concatenate([all_gates[j] for j in range(N_CHIPS)], axis=1)
    wgf_p = _mx(_pad_gate(gates_full[:GLA_GATE_RANK], 0))
    wgb_p = _mx(_pad_gate(gates_full[GLA_GATE_RANK:], GLA_GATE_RANK))
    bf_p, bb_p = _pad_heads(b_gate_fwd, 4), _pad_heads(b_gate_bwd, 4)
    buckets = jnp.asarray(_band_buckets())
    sink1 = swa_sink.reshape(SWA_Q_HEADS)

    qa, ka, va, ga, qs, ks, vs, za = _proj_call(seq, norm_mix_pre, w_in_t, dep=token)
    halo = ((SWA_BLOCK, SWA_BLOCK), (0, 0))
    ks_p, vs_p = jnp.pad(ks, halo), jnp.pad(vs, halo)
    o_f, o_b, s_f, s_b = _gla_fwd_call(qa, ka, va, za, wgf_p, bf_p, wgb_p, bb_p)
    bias = _bias_call(rel_bias, buckets, dep=o_f)
    arrays = _split_wait("gather_chip_wait", handle, 6, stage_one, bias)
    handle, token = _split_start("gather_pair_start", list(arrays), 6, stage_two)
    o_s = _swa_fwd_call(qs, ks_p, vs_p, bias, sink1, dep=token)
    arrays = _split_wait("gather_pair_wait", handle, 6, stage_two, o_s)
    w_out_full = _mx(arrays[2].reshape(N_CHIPS * R_OUT, D_MODEL))
    w_ud = _mx(arrays[3])
    cat, mix, h1, n2 = _mix_call(o_f, o_b, ga, o_s, seq, gla_norm, w_out_full, norm_mix_post, norm_mlp_pre)
    a, rz, dh2, dff, loss, d_post2 = _mlp_fwd_call(n2, h1, tgt, w_ud, norm_mlp_post)

    dz, dn2 = _mlp_bwd_call(dff, rz, w_ud)
    dw_down, dw_up4 = _mlp_wgrad_call(a, dff, n2, dz)
    dh1, do, dga, dos, dw_out, d_pre2, d_post, d_gn = _mix_bwd_call(
        dn2, dh2, h1, mix, cat, o_f, o_b, ga, gla_norm, norm_mix_post, norm_mlp_pre, w_out_full)
    done = {}

    def swa_backward(tok):
        done["swa"] = _swa_bwd_call(qs, ks_p, vs_p, bias, sink1, dos, dep=tok)
        return done["swa"][0]

    def gla_in_backward(tok):
        done["gla"] = _gla_bwd_call(qa, ka, va, za, do, s_f, s_b, wgf_p, bf_p, wgb_p, bb_p, dep=tok)
        dqf, dkf, dvf, dzf, _, _, dqb, dkb, dvb, dzb, _, _ = done["gla"]
        dqs, dks_p, dvs_p, _, _ = done["swa"]
        done["in"] = _in_bwd_call(
            seq, dh1, norm_mix_pre, w_in_t,
            pairs=[(_side_by_side(T_QA), (dqf, dqb)), (_side_by_side(T_KA), (dkf, dkb)), (T_VA, (dvf, dvb)),
                   (T_ZA, (dzf, dzb))],
            singles=[(T_GA, dga), (_side_by_side(T_QS), dqs)], halos=[(T_KS, dks_p), (T_VS, dvs_p)])
        return done["in"][0]

    def bias_backward(tok):
        done["rel"] = _relbias_call(done["swa"][3], done["swa"][4], buckets, dep=tok)
        return done["rel"][0]

    g_up, g_down, g_out = _reduce_to_owners(
        [dw_up4, dw_down.reshape(N_CHIPS, R_DOWN, D_MODEL), dw_out.reshape(N_CHIPS, R_OUT, D_MODEL)],
        [ROWS, ROWS, ROWS], pos, "mlp", [swa_backward, gla_in_backward, bias_backward])
    dx, dw_in_t, d_pre = done["in"]
    dwf, dbf, dwb, dbb = done["gla"][4], done["gla"][5], done["gla"][10], done["gla"][11]
    drel, dsink = done["rel"]

    small_grads = [d_pre, d_post, d_pre2, d_post2, _unpad_heads(dbf, 4), _unpad_heads(dbb, 4), d_gn, dsink, drel]
    gate_grads = [_unpad_heads(dwf[:GLA_GATE_RANK], 4), _unpad_heads(dwb[GLA_GATE_RANK:2 * GLA_GATE_RANK], 4)]
    small_params = [(given[n], given["m_" + n], given["v_" + n]) for n in SMALL_NAMES]
    upd = {}

    everyone = _everyone_plan(2)
    small_handle, small_token = _split_start(
        "small_start", list(_small_pack_call(small_grads, gate_grads + [loss])), 2 * (N_DEVICES - 1), everyone)

    def update_out(tok):
        upd["w_out"] = (g_out,) + tuple(_adamw_call(w_out[0], g_out, m_w_out[0], v_w_out[0], "adamw_w_out",
                                                    dep=tok + small_token))
        return upd["w_out"][1]

    def update_mlp(tok):
        upd["w_up"] = (g_up,) + tuple(_adamw_call(w_up[0], g_up, m_w_up[0], v_w_up[0], "adamw_w_up", dep=tok))
        upd["w_down"] = (g_down,) + tuple(
            _adamw_call(w_down[0], g_down, m_w_down[0], v_w_down[0], "adamw_w_down", dep=upd["w_up"][1]))
        all_a, all_b = _split_wait("small_wait", small_handle, 2 * (N_DEVICES - 1), everyone, upd["w_down"][1])
        per_name, done["gf_sum"], done["gb_sum"], upd["loss"] = _small_adamw_call(all_a, all_b, small_params)
        upd.update(dict(zip(SMALL_NAMES, per_name)))
        return per_name[0][1]

    def update_gates(tok):
        for name, total in (("w_gate_up_fwd", done["gf_sum"]), ("w_gate_up_bwd", done["gb_sum"])):
            g = lax.dynamic_slice(total, (0, chip * 64), (GLA_GATE_RANK, 64))
            upd[name] = (g,) + tuple(_adamw_call(given[name][0], g, given["m_" + name][0], given["v_" + name][0],
                                                 "adamw_" + name, dep=tok))
        return upd["w_gate_up_bwd"][1]

    (g_in_t,) = _reduce_to_owners([dw_in_t.reshape(N_CHIPS, R_IN, D_MODEL)], [COLS], pos, "in",
                                  [update_out, update_mlp, update_gates])
    in_t = (g_in_t,) + tuple(_adamw_call(w_in[0].T, g_in_t, m_w_in[0].T, v_w_in[0].T, "adamw_w_in"))
    upd["w_in"] = tuple(t.T for t in in_t)

    big = ("w_in", "w_gate_up_fwd", "w_gate_up_bwd", "w_out", "w_up", "w_down")
    names = ["norm_mix_pre", "w_in", "w_gate_up_fwd", "b_gate_fwd", "w_gate_up_bwd", "b_gate_bwd", "gla_norm",
             "swa_sink", "rel_bias", "w_out", "norm_mix_post", "norm_mlp_pre", "w_up", "w_down", "norm_mlp_post"]
    outs = [upd["loss"][0, 0], dx[None]]
    for kind in range(4):
        outs += [upd[n][kind][None] if n in big else upd[n][kind] for n in names]
    return tuple(outs)
```

```python
import math

import numpy as np
import jax
import jax.numpy as jnp
from jax import lax
from jax.experimental import pallas as pl
from jax.experimental.pallas import tpu as pltpu

F32 = jnp.float32
MXU_DTYPE = jnp.bfloat16
COMM_DTYPE = jnp.bfloat16

D_MODEL = 1024
D_FF = 4096
N_CHIPS = 4
GLA_HEADS = 4
GLA_CHUNK = 64
GLA_GATE_RANK = 16
GLA_GATE_NORM = 16.0
SWA_Q_HEADS = 8
SWA_KV_HEADS = 2
SWA_BLOCK = 128
REL_BUCKETS = 32
REL_MAX_DIST = 128
NORM_EPS = 1e-6
HEAD_PAD = 128

ADAM_LR = 0.001
ADAM_B1 = 0.9
ADAM_B2 = 0.999
ADAM_EPS = 1e-08
ADAM_WD = 0.01
ADAM_STEP = 10

OUT_PAD = 1024

R_IN, R_OUT, R_UP, R_DOWN = 584, 256, 1024, 1024

VMEM_BIG = 56 * 1024 * 1024
MESH_AXES = ("x", "y", "c")
MESH_ID = pl.DeviceIdType.MESH


def _mx(a):
    return a.astype(MXU_DTYPE)


def _dot(a, b):
    return jnp.dot(a, b, preferred_element_type=F32)


def _dot_nt(a, b):
    return lax.dot_general(a, b, (((1,), (1,)), ((), ())), preferred_element_type=F32)


def _dot_tn(a, b):
    return lax.dot_general(a, b, (((0,), (0,)), ((), ())), preferred_element_type=F32)


def _rms_r(x):
    return lax.rsqrt(jnp.mean(x * x, axis=-1, keepdims=True) + NORM_EPS)


def _rms_bwd(x, r, g, dy):
    xh = x * r
    gdy = dy * g
    dx = r * (gdy - xh * jnp.mean(gdy * xh, axis=-1, keepdims=True))
    return dx, jnp.sum(dy * xh, axis=0, keepdims=True)


def _low_half(rows):
    return lax.broadcasted_iota(jnp.int32, (rows, HEAD_PAD), 1) < 64


def _spread_heads(x):
    low = _low_half(x.shape[0])
    parts = []
    for p in range(x.shape[1] // HEAD_PAD):
        pair = x[:, HEAD_PAD * p:HEAD_PAD * (p + 1)]
        parts += [jnp.where(low, pair, 0.0), jnp.where(low, pltpu.roll(pair, 64, 1), 0.0)]
    return jnp.concatenate(parts, axis=1)


def _squeeze_heads(x):
    low = _low_half(x.shape[0])
    parts = []
    for p in range(x.shape[1] // (2 * HEAD_PAD)):
        even = x[:, 2 * HEAD_PAD * p:2 * HEAD_PAD * p + HEAD_PAD]
        odd = x[:, 2 * HEAD_PAD * p + HEAD_PAD:2 * HEAD_PAD * (p + 1)]
        parts.append(jnp.where(low, even, pltpu.roll(odd, 64, 1)))
    return parts[0] if len(parts) == 1 else jnp.concatenate(parts, axis=1)


def _params(sem=None, vmem=None):
    kw = {}
    if sem is not None:
        kw["dimension_semantics"] = sem
    if vmem is not None:
        kw["vmem_limit_bytes"] = vmem
    return pltpu.CompilerParams(**kw)


def _vmem_spec():
    return pl.BlockSpec(memory_space=pltpu.VMEM)


def _whole_spec(shape):
    return pl.BlockSpec(shape, lambda: (0,) * len(shape))


def _row_spec(tm, width):
    return pl.BlockSpec((tm, width), lambda i: (i, 0))


def _full_spec(shape):
    return pl.BlockSpec(shape, lambda i: (0,) * len(shape))


def _any_spec():
    return pl.BlockSpec(memory_space=pl.ANY)


def _after(body, n_in, dep):
    if dep is None:
        return body, [], []
    return (lambda *refs: body(*refs[:n_in], *refs[n_in + 1:])), [dep], [_any_spec()]


T_QA, T_KA, T_VA, T_GA = (0, 256, 4), (256, 256, 4), (512, 512, 0), (1024, 512, 0)
T_QS, T_KS, T_VS = (1568, 512, 8), (2080, 128, 2), (2208, 128, 2)
T_ZA = (1536, 128, 0)
ZA_COLS = 2 * GLA_GATE_RANK
IN_COLS = 2336


def _side_by_side(group):
    return group[0], group[1], 0


def _proj_call(x, g_pre, w_in_t, dep=None):
    L = x.shape[0]
    tm = min(512, L)
    groups = [(T_QA, F32), (T_KA, F32), (T_VA, MXU_DTYPE), (T_GA, F32),
              (T_QS, MXU_DTYPE), (T_KS, MXU_DTYPE), (T_VS, MXU_DTYPE), (T_ZA, F32)]
    widths = [rows * (2 if heads else 1) for (_, rows, heads), _ in groups]

    def body(x_ref, g_ref, w_ref, *outs):
        xv = x_ref[...]
        u = _mx(xv * _rms_r(xv) * g_ref[...])
        for ref, (grp, _) in zip(outs, groups):
            first, rows, heads = grp
            val = _dot_nt(u, w_ref[first:first + rows, :])
            if heads:
                val = _spread_heads(val)
            if grp is T_ZA:
                val = jnp.where(lax.broadcasted_iota(jnp.int32, val.shape, 1) < ZA_COLS, val, 0.0)
            if grp is T_QS:
                val = val * 0.125
            ref[...] = val.astype(ref.dtype)

    body, extra, extra_specs = _after(body, 3, dep)
    return pl.pallas_call(
        body, name="proj_fwd", grid=(L // tm,),
        in_specs=[_row_spec(tm, D_MODEL), _full_spec((1, D_MODEL)), _vmem_spec()] + extra_specs,
        out_specs=[_row_spec(tm, w) for w in widths],
        out_shape=[jax.ShapeDtypeStruct((L, w), dt) for w, (_, dt) in zip(widths, groups)],
        compiler_params=_params(("arbitrary",), VMEM_BIG),
    )(x, g_pre, w_in_t, *extra)


def _tri_masks():
    row = lax.broadcasted_iota(jnp.int32, (GLA_CHUNK, GLA_CHUNK), 0)
    col = lax.broadcasted_iota(jnp.int32, (GLA_CHUNK, GLA_CHUNK), 1)
    return row >= col, row <= col


def _chunk_sums(tri_m, x):
    hi = _mx(x)
    rest = x - hi.astype(F32)
    mid = _mx(rest)
    lo = _mx(rest - mid.astype(F32))
    return _dot(tri_m, hi) + _dot(tri_m, mid) + _dot(tri_m, lo)


def _gla_block_pre(q_r, k_r, z_r, w_r, b_r, rev, nc, qd_s, ki_s, ks_s, dec_s, keep=None):
    tri_f, tri_b = _tri_masks()
    tri_m = _mx((tri_b if rev else tri_f).astype(F32))
    g = _dot(_mx(z_r[...]), w_r[...]) + b_r[...]
    la = (jnp.minimum(g, 0.0) - jnp.log(1.0 + jnp.exp(-jnp.abs(g)))) / GLA_GATE_NORM
    sums, lasts = [], []
    for c in range(nc):
        b_c = _chunk_sums(tri_m, la[GLA_CHUNK * c:GLA_CHUNK * (c + 1)])
        blast = b_c[0:1] if rev else b_c[GLA_CHUNK - 1:GLA_CHUNK]
        dec_s[c] = jnp.exp(blast)
        sums.append(b_c)
        lasts.append(jnp.broadcast_to(blast, b_c.shape))
    b = jnp.concatenate(sums, axis=0)
    eb = jnp.exp(b)
    enb = jnp.exp(-b)
    elb = jnp.exp(jnp.concatenate(lasts, axis=0) - b)
    k = k_r[...]
    qd_s[...] = (q_r[...] * 0.125 * eb).astype(qd_s.dtype)
    ki_s[...] = (k * enb).astype(ki_s.dtype)
    ks_s[...] = (k * elb).astype(ks_s.dtype)
    if keep is not None:
        for ref, val in zip(keep, (g, eb, enb, elb)):
            ref[...] = val


def _gla_fwd_call(qa, ka, va, za, wgf, bgf, wgb, bgb):
    L = qa.shape[0]
    br = min(512, L)
    nb, nc, n_chunks = L // br, br // GLA_CHUNK, L // GLA_CHUNK
    hw = GLA_HEADS * HEAD_PAD

    def body(qaf, kaf, vaf, zaf, qab, kab, vab, zab, wgf_r, bgf_r, wgb_r, bgb_r,
             of_r, ob_r, sf_r, sb_r, st_f, st_b, pre_f, pre_b):
        @pl.when(pl.program_id(0) == 0)
        def _():
            st_f[...] = jnp.zeros_like(st_f)
            st_b[...] = jnp.zeros_like(st_b)

        _gla_block_pre(qaf, kaf, zaf, wgf_r, bgf_r, False, nc, *pre_f)
        _gla_block_pre(qab, kab, zab, wgb_r, bgb_r, True, nc, *pre_b)
        tri_f, tri_b = _tri_masks()

        def one(tri, pre, v_r, o_r, s_r, st, ci):
            qd_s, ki_s, ks_s, dec_s = pre
            rows = pl.ds(pl.multiple_of(ci * GLA_CHUNK, GLA_CHUNK), GLA_CHUNK)
            dec = dec_s[ci]
            heads = range(GLA_HEADS)
            lanes = [slice(HEAD_PAD * h, HEAD_PAD * (h + 1)) for h in heads]
            qd = [qd_s[rows, sl] for sl in lanes]
            v = [v_r[rows, sl] for sl in lanes]
            s_t = [st[h] for h in heads]
            a = [_dot_nt(qd[h], ki_s[rows, lanes[h]]) for h in heads]
            carried = [_dot_nt(qd[h], _mx(s_t[h])) for h in heads]
            grown = [_dot_tn(v[h], ks_s[rows, lanes[h]]) for h in heads]
            a = [_mx(jnp.where(tri, a[h], 0.0)) for h in heads]
            inner = [_dot(a[h], v[h]) for h in heads]
            for h in heads:
                s_r[ci, h] = s_t[h].astype(s_r.dtype)
                o_r[rows, lanes[h]] = inner[h] + carried[h]
                st[h] = s_t[h] * dec[:, lanes[h]] + grown[h]

        def loop(t, carry):
            one(tri_f, pre_f, vaf, of_r, sf_r, st_f, t)
            one(tri_b, pre_b, vab, ob_r, sb_r, st_b, nc - 1 - t)
            return carry

        lax.fori_loop(0, nc, loop, 0, unroll=True)

    fwd = lambda i: (i, 0)
    bwd = lambda i: (nb - 1 - i, 0)
    ins = lambda m: [pl.BlockSpec((br, hw), m), pl.BlockSpec((br, hw), m),
                     pl.BlockSpec((br, hw), m), pl.BlockSpec((br, 128), m)]
    wspecs = [_full_spec((128, hw)), _full_spec((1, hw))] * 2
    s_shape = (nc, GLA_HEADS, HEAD_PAD, HEAD_PAD)
    pre_scratch = [pltpu.VMEM((br, hw), MXU_DTYPE)] * 3 + [pltpu.VMEM((nc, 1, hw), F32)]
    return pl.pallas_call(
        body, name="gla_fwd", grid=(nb,),
        in_specs=ins(fwd) + ins(bwd) + wspecs,
        out_specs=[pl.BlockSpec((br, hw), fwd), pl.BlockSpec((br, hw), bwd),
                   pl.BlockSpec(s_shape, lambda i: (i, 0, 0, 0)),
                   pl.BlockSpec(s_shape, lambda i: (nb - 1 - i, 0, 0, 0))],
        out_shape=[jax.ShapeDtypeStruct((L, hw), F32), jax.ShapeDtypeStruct((L, hw), F32),
                   jax.ShapeDtypeStruct((n_chunks,) + s_shape[1:], MXU_DTYPE),
                   jax.ShapeDtypeStruct((n_chunks,) + s_shape[1:], MXU_DTYPE)],
        scratch_shapes=[pltpu.VMEM(s_shape[1:], F32), pltpu.VMEM(s_shape[1:], F32), pre_scratch, pre_scratch],
        compiler_params=_params(("arbitrary",), VMEM_BIG),
    )(qa, ka, va, za, qa, ka, va, za, wgf, bgf, wgb, bgb)


def _gla_bwd_call(qa, ka, va, za, do, sf, sb, wgf, bgf, wgb, bgb, dep=None):
    L = qa.shape[0]
    br = min(512, L)
    nb, nc = L // br, br // GLA_CHUNK
    hw = GLA_HEADS * HEAD_PAD

    def body(qaf, kaf, vaf, zaf, dof, sf_r, qab, kab, vab, zab, dob, sb_r, wgf_r, bgf_r, wgb_r, bgb_r,
             dqf, dkf, dvf, dzf, dwf, dbf, dqb, dkb, dvb, dzb, dwb, dbb, gt_f, gt_b, pre_f, pre_b):
        @pl.when(pl.program_id(0) == 0)
        def _():
            for ref in (gt_f, gt_b, dwf, dbf, dwb, dbb):
                ref[...] = jnp.zeros_like(ref)

        _gla_block_pre(qaf, kaf, zaf, wgf_r, bgf_r, False, nc, *pre_f[:4], keep=pre_f[4:8])
        _gla_block_pre(qab, kab, zab, wgb_r, bgb_r, True, nc, *pre_b[:4], keep=pre_b[4:8])
        tri_f, tri_b = _tri_masks()
        row_w = lax.broadcasted_iota(jnp.int32, (GLA_CHUNK, HEAD_PAD), 0)

        def one(rev, pre, q_r, k_r, v_r, do_r, s_r, dq_r, dk_r, dv_r, gt, ci):
            qd_s, ki_s, ks_s, dec_s, _, eb_s, enb_s, elb_s, db_s = pre
            tri = tri_b if rev else tri_f
            last_row = 0 if rev else GLA_CHUNK - 1
            rows = pl.ds(pl.multiple_of(ci * GLA_CHUNK, GLA_CHUNK), GLA_CHUNK)
            dec = dec_s[ci]
            heads = range(GLA_HEADS)
            lanes = [slice(HEAD_PAD * h, HEAD_PAD * (h + 1)) for h in heads]
            qd = [qd_s[rows, sl] for sl in lanes]
            ki = [ki_s[rows, sl] for sl in lanes]
            ks = [ks_s[rows, sl] for sl in lanes]
            v = [v_r[rows, sl] for sl in lanes]
            do_h = [_mx(do_r[rows, sl]) for sl in lanes]
            s_t = [s_r[ci, h] for h in heads]
            g_t = [gt[h] for h in heads]
            g_m = [_mx(g_t[h]) for h in heads]
            a = [_dot_nt(qd[h], ki[h]) for h in heads]
            da = [_dot_nt(do_h[h], v[h]) for h in heads]
            dv_carried = [_dot_nt(ks[h], g_m[h]) for h in heads]
            dqd_carried = [_dot(do_h[h], _mx(s_t[h])) for h in heads]
            dks = [_dot(v[h], g_m[h]) for h in heads]
            g_grown = [_dot_tn(do_h[h], qd[h]) for h in heads]
            a = [_mx(jnp.where(tri, a[h], 0.0)) for h in heads]
            da = [_mx(jnp.where(tri, da[h], 0.0)) for h in heads]
            dv_inner = [_dot_tn(a[h], do_h[h]) for h in heads]
            dqd_inner = [_dot(da[h], ki[h]) for h in heads]
            dki = [_dot_tn(da[h], qd[h]) for h in heads]
            dq, dk = [], []
            for h in heads:
                sl = lanes[h]
                dv_r[rows, sl] = (dv_inner[h] + dv_carried[h]).astype(dv_r.dtype)
                ddec = jnp.sum(g_t[h] * s_t[h].astype(F32), axis=0, keepdims=True)
                gt[h] = g_t[h] * dec[:, sl] + g_grown[h]
                dq.append((dqd_inner[h] + dqd_carried[h]) * eb_s[rows, sl] * 0.125)
                dk_state = dks[h] * elb_s[rows, sl]
                dk.append(dki[h] * enb_s[rows, sl] + dk_state)
                k = k_r[rows, sl]
                dblast = jnp.sum(dk_state * k, axis=0, keepdims=True) + dec[:, sl] * ddec
                db_s[rows, sl] = q_r[rows, sl] * dq[h] - k * dk[h] + jnp.where(row_w == last_row, dblast, 0.0)
            low = _low_half(GLA_CHUNK)
            for pair in range(GLA_HEADS // 2):
                psl = slice(HEAD_PAD * pair, HEAD_PAD * (pair + 1))
                for ref, val in ((dq_r, dq), (dk_r, dk)):
                    both = jnp.where(low, val[2 * pair], pltpu.roll(val[2 * pair + 1], 64, 1))
                    ref[rows, psl] = both.astype(ref.dtype)

        def loop(t, carry):
            one(False, pre_f, qaf, kaf, vaf, dof, sf_r, dqf, dkf, dvf, gt_f, nc - 1 - t)
            one(True, pre_b, qab, kab, vab, dob, sb_r, dqb, dkb, dvb, gt_b, t)
            return carry

        lax.fori_loop(0, nc, loop, 0, unroll=True)

        def gate_grads(rev, pre, z_r, w_r, dz_r, dw_r, dbias_r):
            g_s, db_s = pre[4], pre[8]
            back_m = _mx((tri_f if rev else tri_b).astype(F32))
            db = db_s[...]
            dla = jnp.concatenate([_chunk_sums(back_m, db[GLA_CHUNK * c:GLA_CHUNK * (c + 1)]) for c in range(nc)],
                                  axis=0)
            dg = dla * (1.0 / GLA_GATE_NORM) * (1.0 / (1.0 + jnp.exp(g_s[...])))
            dg_m = _mx(dg)
            dz_r[...] = _dot_nt(dg_m, w_r[...])
            dw_r[...] += _dot_tn(_mx(z_r[...]), dg_m)
            dbias_r[...] += jnp.sum(dg, axis=0, keepdims=True)

        gate_grads(False, pre_f, zaf, wgf_r, dzf, dwf, dbf)
        gate_grads(True, pre_b, zab, wgb_r, dzb, dwb, dbb)

    last_first = lambda i: (nb - 1 - i, 0)
    first_last = lambda i: (i, 0)
    s_shape = (nc, GLA_HEADS, HEAD_PAD, HEAD_PAD)

    def ins(m):
        return [pl.BlockSpec((br, hw), m), pl.BlockSpec((br, hw), m), pl.BlockSpec((br, hw), m),
                pl.BlockSpec((br, 128), m), pl.BlockSpec((br, hw), m),
                pl.BlockSpec(s_shape, lambda i: m(i) + (0, 0))]

    def outs(m):
        return [pl.BlockSpec((br, hw // 2), m), pl.BlockSpec((br, hw // 2), m), pl.BlockSpec((br, hw), m),
                pl.BlockSpec((br, 128), m), _full_spec((128, hw)), _full_spec((1, hw))]

    out_shape = [jax.ShapeDtypeStruct((L, hw // 2), MXU_DTYPE)] * 2 + [
        jax.ShapeDtypeStruct((L, hw), MXU_DTYPE),
        jax.ShapeDtypeStruct((L, 128), F32), jax.ShapeDtypeStruct((128, hw), F32),
        jax.ShapeDtypeStruct((1, hw), F32)]
    wspecs = [_full_spec((128, hw)), _full_spec((1, hw))] * 2
    body, extra, extra_specs = _after(body, 16, dep)
    pre_scratch = ([pltpu.VMEM((br, hw), MXU_DTYPE)] * 3 + [pltpu.VMEM((nc, 1, hw), F32)]
                   + [pltpu.VMEM((br, hw), F32)] * 5)
    return pl.pallas_call(
        body, name="gla_bwd", grid=(nb,),
        in_specs=ins(last_first) + ins(first_last) + wspecs + extra_specs,
        out_specs=outs(last_first) + outs(first_last),
        out_shape=out_shape + out_shape,
        scratch_shapes=[pltpu.VMEM(s_shape[1:], F32), pltpu.VMEM(s_shape[1:], F32), pre_scratch, pre_scratch],
        compiler_params=_params(("arbitrary",), VMEM_BIG),
    )(qa, ka, va, za, do, sf, qa, ka, va, za, do, sb, wgf, bgf, wgb, bgb, *extra)


def _t5_buckets(rel):
    nb = REL_BUCKETS // 2
    ret = (rel > 0).astype(np.int32) * nb
    n = np.abs(rel)
    max_exact = nb // 2
    large = max_exact + (np.log(np.maximum(n, 1).astype(np.float32) / max_exact)
                         / math.log(REL_MAX_DIST / max_exact) * (nb - max_exact)).astype(np.int32)
    large = np.minimum(large, nb - 1)
    return ret + np.where(n < max_exact, n, large)


SWA_GROUP = SWA_Q_HEADS // SWA_KV_HEADS
SWA_SPAN = 3 * SWA_BLOCK
SWA_GROUP_LANES = SWA_GROUP * SWA_BLOCK


def _band_buckets():
    s = np.arange(SWA_SPAN)[:, None]
    c = np.arange(SWA_BLOCK)[None, :]
    return _t5_buckets(s - SWA_BLOCK - c).astype(np.int32)


def _swa_valid(n, seq_len):
    key_pos = (n - 1) * SWA_BLOCK + lax.broadcasted_iota(jnp.int32, (SWA_SPAN, 1), 0)
    return (key_pos >= 0) & (key_pos < seq_len)


def _swa_sink_row(sink_r, kv):
    lane = lax.broadcasted_iota(jnp.int32, (1, SWA_GROUP_LANES), 1)
    row = jnp.full((1, SWA_GROUP_LANES), sink_r[kv * SWA_GROUP], F32)
    for g in range(1, SWA_GROUP):
        row = jnp.where(lane >= g * SWA_BLOCK, sink_r[kv * SWA_GROUP + g], row)
    return row


def _swa_group(ref, kv):
    first = kv * SWA_GROUP
    return jnp.concatenate([ref[:, HEAD_PAD * h:HEAD_PAD * (h + 1)] for h in range(first, first + SWA_GROUP)],
                           axis=0)


def _swa_softmax(scores, bias_t, sink_row, valid):
    st = jnp.where(valid, scores + bias_t, -1e30)
    m = jnp.maximum(jnp.max(st, axis=0, keepdims=True), sink_row)
    p = jnp.exp(st - m)
    e_sink = jnp.exp(sink_row - m)
    inv = 1.0 / (jnp.sum(p, axis=0, keepdims=True) + e_sink)
    return p * inv, e_sink * inv


def _swa_fwd_call(qs, ks, vs, bias, sink, dep=None):
    L = qs.shape[0]

    def body(q_r, k_r, v_r, bias_r, sink_r, o_r):
        n = pl.program_id(0)
        span = pl.ds(pl.multiple_of(n * SWA_BLOCK, SWA_BLOCK), SWA_SPAN)
        valid = _swa_valid(n, L)
        groups = range(SWA_KV_HEADS)
        lanes = [slice(HEAD_PAD * kv, HEAD_PAD * (kv + 1)) for kv in groups]
        scores = [_dot_nt(k_r[span, lanes[kv]], _swa_group(q_r, kv)) for kv in groups]
        probs = [_swa_softmax(scores[kv], bias_r[kv], _swa_sink_row(sink_r, kv), valid)[0] for kv in groups]
        low = _low_half(SWA_BLOCK)
        for kv in groups:
            og = _dot_tn(_mx(probs[kv]), v_r[span, lanes[kv]])
            for pair in range(SWA_GROUP // 2):
                even = og[2 * SWA_BLOCK * pair:2 * SWA_BLOCK * pair + SWA_BLOCK]
                odd = og[2 * SWA_BLOCK * pair + SWA_BLOCK:2 * SWA_BLOCK * (pair + 1)]
                first = HEAD_PAD * (kv * SWA_GROUP // 2 + pair)
                o_r[:, first:first + HEAD_PAD] = jnp.where(low, even, pltpu.roll(odd, 64, 1)).astype(o_r.dtype)

    qw = SWA_Q_HEADS * HEAD_PAD
    body, extra, extra_specs = _after(body, 5, dep)
    return pl.pallas_call(
        body, name="swa_fwd", grid=(L // SWA_BLOCK,),
        in_specs=[_row_spec(SWA_BLOCK, qw), _vmem_spec(), _vmem_spec(), _vmem_spec(),
                  pl.BlockSpec(memory_space=pltpu.SMEM)] + extra_specs,
        out_specs=_row_spec(SWA_BLOCK, qw // 2),
        out_shape=jax.ShapeDtypeStruct((L, qw // 2), MXU_DTYPE),
        compiler_params=_params(("arbitrary",), VMEM_BIG),
    )(qs, ks, vs, bias, sink, *extra)


def _swa_bwd_call(qs, ks, vs, bias, sink, do, dep=None):
    L = qs.shape[0]
    qw = SWA_Q_HEADS * HEAD_PAD
    kw = SWA_KV_HEADS * HEAD_PAD

    def body(q_r, k_r, v_r, bias_r, sink_r, do_r, dq_r, dk_r, dv_r, dbias_r, dsink_r):
        n = pl.program_id(0)

        @pl.when(n == 0)
        def _():
            for ref in (dk_r, dv_r, dbias_r, dsink_r):
                ref[...] = jnp.zeros_like(ref)

        span = pl.ds(pl.multiple_of(n * SWA_BLOCK, SWA_BLOCK), SWA_SPAN)
        valid = _swa_valid(n, L)
        groups = range(SWA_KV_HEADS)
        lanes = [slice(HEAD_PAD * kv, HEAD_PAD * (kv + 1)) for kv in groups]
        kk = [k_r[span, sl] for sl in lanes]
        vv = [v_r[span, sl] for sl in lanes]
        qg = [_swa_group(q_r, kv) for kv in groups]
        dog = [_swa_group(do_r, kv) for kv in groups]
        scores = [_dot_nt(kk[kv], qg[kv]) for kv in groups]
        dp = [_dot_nt(vv[kv], dog[kv]) for kv in groups]
        probs = [_swa_softmax(scores[kv], bias_r[kv], _swa_sink_row(sink_r, kv), valid) for kv in groups]
        ds_m, pn_m = [], []
        for kv in groups:
            pn, p_sink = probs[kv]
            delta = jnp.sum(pn * dp[kv], axis=0, keepdims=True)
            ds = pn * (dp[kv] - delta)
            dsink_r[kv] -= p_sink * delta
            dbias_r[kv] += ds
            ds_m.append(_mx(ds))
            pn_m.append(_mx(pn))
        dqg = [_dot_tn(ds_m[kv], kk[kv]) * 0.125 for kv in groups]
        dkk = [_dot(ds_m[kv], qg[kv]) for kv in groups]
        dvv = [_dot(pn_m[kv], dog[kv]) for kv in groups]
        low = _low_half(SWA_BLOCK)
        for kv in groups:
            for pair in range(SWA_GROUP // 2):
                even = dqg[kv][2 * SWA_BLOCK * pair:2 * SWA_BLOCK * pair + SWA_BLOCK]
                odd = dqg[kv][2 * SWA_BLOCK * pair + SWA_BLOCK:2 * SWA_BLOCK * (pair + 1)]
                first = HEAD_PAD * (kv * SWA_GROUP // 2 + pair)
                dq_r[:, first:first + HEAD_PAD] = jnp.where(low, even, pltpu.roll(odd, 64, 1)).astype(dq_r.dtype)
            dk_r[span, lanes[kv]] += dkk[kv]
            dv_r[span, lanes[kv]] += dvv[kv]

    body, extra, extra_specs = _after(body, 6, dep)
    return pl.pallas_call(
        body, name="swa_bwd", grid=(L // SWA_BLOCK,),
        in_specs=[_row_spec(SWA_BLOCK, qw), _vmem_spec(), _vmem_spec(), _vmem_spec(),
                  pl.BlockSpec(memory_space=pltpu.SMEM), _row_spec(SWA_BLOCK, qw)] + extra_specs,
        out_specs=[_row_spec(SWA_BLOCK, qw // 2), _vmem_spec(), _vmem_spec(), _vmem_spec(), _vmem_spec()],
        out_shape=[jax.ShapeDtypeStruct((L, qw // 2), MXU_DTYPE),
                   jax.ShapeDtypeStruct((L + 2 * SWA_BLOCK, kw), F32),
                   jax.ShapeDtypeStruct((L + 2 * SWA_BLOCK, kw), F32),
                   jax.ShapeDtypeStruct((SWA_KV_HEADS, SWA_SPAN, SWA_GROUP_LANES), F32),
                   jax.ShapeDtypeStruct((SWA_KV_HEADS, 1, SWA_GROUP_LANES), F32)],
        compiler_params=_params(("arbitrary",), VMEM_BIG),
    )(qs, ks, vs, bias, sink, do, *extra)


def _bias_call(rel_bias, buckets, dep=None):
    def body(t_r, bk_r, o_r):
        bk = bk_r[...]
        s = lax.broadcasted_iota(jnp.int32, bk.shape, 0)
        c = lax.broadcasted_iota(jnp.int32, bk.shape, 1)
        in_band = jnp.abs(s - SWA_BLOCK - c) <= SWA_BLOCK
        for h in range(SWA_Q_HEADS):
            acc = jnp.zeros(bk.shape, F32)
            for b in range(REL_BUCKETS):
                acc = jnp.where(bk == b, t_r[b, h], acc)
            g = h % SWA_GROUP
            o_r[h // SWA_GROUP, :, SWA_BLOCK * g:SWA_BLOCK * (g + 1)] = jnp.where(in_band, acc, -1e30)

    body, extra, extra_specs = _after(body, 2, dep)
    return pl.pallas_call(
        body, name="band_bias",
        in_specs=[pl.BlockSpec(memory_space=pltpu.SMEM), _vmem_spec()] + extra_specs, out_specs=_vmem_spec(),
        out_shape=jax.ShapeDtypeStruct((SWA_KV_HEADS, SWA_SPAN, SWA_GROUP_LANES), F32),
    )(rel_bias, buckets, *extra)


def _relbias_call(dbias, dsink, buckets, dep=None):
    def body(db_r, ds_r, bk_r, o_r, os_r):
        bk = bk_r[...]
        rowi = lax.broadcasted_iota(jnp.int32, (REL_BUCKETS, 128), 0)
        lanei = lax.broadcasted_iota(jnp.int32, (REL_BUCKETS, 128), 1)
        lane1 = lax.broadcasted_iota(jnp.int32, (1, 128), 1)
        acc = jnp.zeros((REL_BUCKETS, 128), F32)
        acc_sink = jnp.zeros((1, 128), F32)
        for h in range(SWA_Q_HEADS):
            kv, g = h // SWA_GROUP, h % SWA_GROUP
            lanes = slice(SWA_BLOCK * g, SWA_BLOCK * (g + 1))
            part = db_r[kv, :, lanes]
            for b in range(REL_BUCKETS):
                s = jnp.sum(jnp.where(bk == b, part, 0.0))
                acc = acc + jnp.where((rowi == b) & (lanei == h), s, 0.0)
            acc_sink = acc_sink + jnp.where(lane1 == h, jnp.sum(ds_r[kv, :, lanes]), 0.0)
        o_r[...] = acc
        os_r[...] = acc_sink

    body, extra, extra_specs = _after(body, 3, dep)
    return pl.pallas_call(
        body, name="relbias_grad",
        in_specs=[_vmem_spec()] * 3 + extra_specs, out_specs=[_vmem_spec()] * 2,
        out_shape=[jax.ShapeDtypeStruct((REL_BUCKETS, 128), F32), jax.ShapeDtypeStruct((1, 128), F32)],
    )(dbias, dsink, buckets, *extra)


def _mix_call(o_f, o_b, ga, o_s, x, gn, w_out_p, g_post, g_pre2, dep=None):
    L = x.shape[0]
    tm = min(512, L)
    hw = GLA_HEADS * HEAD_PAD

    def body(of_r, ob_r, ga_r, os_r, x_r, gn_r, w_r, gp_r, g2_r, cat_r, mix_r, h1_r, n2_r):
        gn_v = gn_r[...]
        for h in range(GLA_HEADS):
            sl = slice(HEAD_PAD * h, HEAD_PAD * (h + 1))
            oh = of_r[:, sl] + ob_r[:, sl]
            on = oh * _rms_r(oh) * gn_v
            gate = ga_r[:, sl]
            cat_r[:, sl] = (on * (gate * jax.nn.sigmoid(gate))).astype(cat_r.dtype)
        os_v = os_r[...]
        cat_r[:, hw:] = os_v
        mix = _dot(cat_r[:, :hw], w_r[:hw, :]) + _dot(os_v, w_r[hw:, :])
        mix_r[...] = mix
        h1 = x_r[...] + mix * _rms_r(mix) * gp_r[...]
        h1_r[...] = h1
        n2_r[...] = (h1 * _rms_r(h1) * g2_r[...]).astype(n2_r.dtype)

    body, extra, extra_specs = _after(body, 9, dep)
    return pl.pallas_call(
        body, name="mix_fwd", grid=(L // tm,),
        in_specs=[_row_spec(tm, hw), _row_spec(tm, hw), _row_spec(tm, hw), _row_spec(tm, OUT_PAD - hw),
                  _row_spec(tm, D_MODEL), _full_spec((1, HEAD_PAD)), _vmem_spec(),
                  _full_spec((1, D_MODEL)), _full_spec((1, D_MODEL))] + extra_specs,
        out_specs=[_row_spec(tm, OUT_PAD), _row_spec(tm, D_MODEL), _row_spec(tm, D_MODEL), _row_spec(tm, D_MODEL)],
        out_shape=[jax.ShapeDtypeStruct((L, OUT_PAD), MXU_DTYPE), jax.ShapeDtypeStruct((L, D_MODEL), F32),
                   jax.ShapeDtypeStruct((L, D_MODEL), F32), jax.ShapeDtypeStruct((L, D_MODEL), MXU_DTYPE)],
        compiler_params=_params(("arbitrary",), VMEM_BIG),
    )(o_f, o_b, ga, o_s, x, gn, w_out_p, g_post, g_pre2, *extra)


def _mlp_fwd_call(n2, h1, tgt, w_ud, g_post):
    L = n2.shape[0]
    tm = min(512, L)
    blk = D_FF // N_CHIPS

    def body(n2_r, h1_r, t_r, w_r, g_r, a_r, rz_r, dh2_r, dff_r, loss_r, dg_r):
        @pl.when(pl.program_id(0) == 0)
        def _():
            loss_r[...] = jnp.zeros_like(loss_r)
            dg_r[...] = jnp.zeros_like(dg_r)

        n2v = n2_r[...]
        ff = jnp.zeros((tm, D_MODEL), F32)
        for j in range(N_CHIPS):
            sl = slice(blk * j, blk * (j + 1))
            rz = jnp.maximum(_dot(n2v, w_r[j, 0]), 0.0)
            a = _mx(rz * rz)
            rz_r[:, sl] = rz.astype(rz_r.dtype)
            a_r[:, sl] = a
            ff = ff + _dot(a, w_r[j, 1])
        g = g_r[...]
        r = _rms_r(ff)
        err = h1_r[...] + ff * r * g - t_r[...]
        loss_r[...] += 0.5 * jnp.sum(err * err) / D_MODEL
        dh2 = err * (1.0 / D_MODEL)
        dh2_r[...] = dh2
        dff, dg = _rms_bwd(ff, r, g, dh2)
        dff_r[...] = dff.astype(dff_r.dtype)
        dg_r[...] += dg

    return pl.pallas_call(
        body, name="mlp_fwd", grid=(L // tm,),
        in_specs=[_row_spec(tm, D_MODEL), _row_spec(tm, D_MODEL), _row_spec(tm, D_MODEL),
                  _vmem_spec(), _full_spec((1, D_MODEL))],
        out_specs=[_row_spec(tm, D_FF), _row_spec(tm, D_FF), _row_spec(tm, D_MODEL), _row_spec(tm, D_MODEL),
                   _full_spec((1, 128)), _full_spec((1, D_MODEL))],
        out_shape=[jax.ShapeDtypeStruct((L, D_FF), MXU_DTYPE), jax.ShapeDtypeStruct((L, D_FF), MXU_DTYPE),
                   jax.ShapeDtypeStruct((L, D_MODEL), F32), jax.ShapeDtypeStruct((L, D_MODEL), MXU_DTYPE),
                   jax.ShapeDtypeStruct((1, 128), F32), jax.ShapeDtypeStruct((1, D_MODEL), F32)],
        compiler_params=_params(("arbitrary",), VMEM_BIG),
    )(n2, h1, tgt, w_ud, g_post)


def _mlp_bwd_call(dff, rz, w_ud):
    L = dff.shape[0]
    tm = min(512, L)
    blk = D_FF // N_CHIPS

    def body(dff_r, rz_r, w_r, dz_r, dn2_r):
        dffv = dff_r[...]
        dn2 = jnp.zeros((tm, D_MODEL), F32)
        for j in range(N_CHIPS):
            sl = slice(blk * j, blk * (j + 1))
            dz = _mx(_dot_nt(dffv, w_r[j, 1]) * 2.0 * rz_r[:, sl].astype(F32))
            dz_r[:, sl] = dz
            dn2 = dn2 + _dot_nt(dz, w_r[j, 0])
        dn2_r[...] = dn2

    return pl.pallas_call(
        body, name="mlp_bwd", grid=(L // tm,),
        in_specs=[_row_spec(tm, D_MODEL), _row_spec(tm, D_FF), _vmem_spec()],
        out_specs=[_row_spec(tm, D_FF), _row_spec(tm, D_MODEL)],
        out_shape=[jax.ShapeDtypeStruct((L, D_FF), MXU_DTYPE), jax.ShapeDtypeStruct((L, D_MODEL), F32)],
        compiler_params=_params(("arbitrary",), VMEM_BIG),
    )(dff, rz, w_ud)


def _mlp_wgrad_call(a, dff, n2, dz):
    L = a.shape[0]
    tf = 512
    per = (D_FF // N_CHIPS) // tf

    def body(a_r, dff_r, n2_r, dz_r, dwd_r, dwu_r):
        dwd_r[...] = _dot_tn(a_r[...], dff_r[...])
        dwu_r[...] = _dot_tn(n2_r[...], dz_r[...])

    return pl.pallas_call(
        body, name="mlp_wgrad", grid=(D_FF // tf,),
        in_specs=[pl.BlockSpec((L, tf), lambda j: (0, j)), _vmem_spec(), _vmem_spec(),
                  pl.BlockSpec((L, tf), lambda j: (0, j))],
        out_specs=[pl.BlockSpec((tf, D_MODEL), lambda j: (j, 0)),
                   pl.BlockSpec((None, D_MODEL, tf), lambda j: (j // per, 0, j % per))],
        out_shape=[jax.ShapeDtypeStruct((D_FF, D_MODEL), F32),
                   jax.ShapeDtypeStruct((N_CHIPS, D_MODEL, D_FF // N_CHIPS), F32)],
        compiler_params=_params(("arbitrary",), VMEM_BIG),
    )(a, dff, n2, dz)


def _mix_bwd_call(dn2, dh2, h1, mix, cat, o_f, o_b, ga, gn, g_post, g_pre2, w_out_p):
    L = dn2.shape[0]
    tm = min(512, L)
    hw = GLA_HEADS * HEAD_PAD

    def body(dn2_r, dh2_r, h1_r, mix_r, cat_r, of_r, ob_r, ga_r, gn_r, gp_r, g2_r, w_r,
             dh1_r, do_r, dga_r, dos_r, dw_r, dg2_r, dgp_r, dgn_r):
        @pl.when(pl.program_id(0) == 0)
        def _():
            for ref in (dw_r, dg2_r, dgp_r, dgn_r):
                ref[...] = jnp.zeros_like(ref)

        parts = [slice(start, start + min(256, tm)) for start in range(0, tm, 256)]
        dmix_m = []
        for rs in parts:
            h1 = h1_r[rs, :]
            dx2, dg2 = _rms_bwd(h1, _rms_r(h1), g2_r[...], dn2_r[rs, :])
            dh1 = dh2_r[rs, :] + dx2
            dh1_r[rs, :] = dh1
            dg2_r[...] += dg2
            mix = mix_r[rs, :]
            dmix, dgp = _rms_bwd(mix, _rms_r(mix), gp_r[...], dh1)
            dgp_r[...] += dgp
            dmix_m.append(_mx(dmix))
        dcat = [_dot_nt(d, w_r[...]) for d in dmix_m]
        for rs, d in zip(parts, dmix_m):
            dw_r[...] += _dot_tn(cat_r[rs, :], d)
        gn_v = gn_r[...]
        dgn = jnp.zeros((1, HEAD_PAD), F32)
        for rs, dc in zip(parts, dcat):
            dos_r[rs, :] = _spread_heads(dc[:, hw:]).astype(dos_r.dtype)
            for h in range(GLA_HEADS):
                sl = slice(HEAD_PAD * h, HEAD_PAD * (h + 1))
                oh = of_r[rs, sl] + ob_r[rs, sl]
                rr = _rms_r(oh)
                xh = oh * rr
                gate = ga_r[rs, sl]
                sg = jax.nn.sigmoid(gate)
                silu = gate * sg
                doa = dc[:, sl]
                dga_r[rs, sl] = (doa * (xh * gn_v) * (sg + silu * (1.0 - sg))).astype(dga_r.dtype)
                don = doa * silu
                gd = don * gn_v
                do_r[rs, sl] = rr * (gd - xh * jnp.mean(gd * xh, axis=-1, keepdims=True))
                dgn = dgn + jnp.sum(don * xh, axis=0, keepdims=True)
        dgn_r[...] += dgn

    return pl.pallas_call(
        body, name="mix_bwd", grid=(L // tm,),
        in_specs=[_row_spec(tm, D_MODEL)] * 4 + [_row_spec(tm, OUT_PAD)] + [_row_spec(tm, hw)] * 3
        + [_full_spec((1, HEAD_PAD)), _full_spec((1, D_MODEL)), _full_spec((1, D_MODEL)), _vmem_spec()],
        out_specs=[_row_spec(tm, D_MODEL), _row_spec(tm, hw), _row_spec(tm, hw),
                   _row_spec(tm, SWA_Q_HEADS * HEAD_PAD),
                   _full_spec((OUT_PAD, D_MODEL)), _full_spec((1, D_MODEL)), _full_spec((1, D_MODEL)),
                   _full_spec((1, HEAD_PAD))],
        out_shape=[jax.ShapeDtypeStruct((L, D_MODEL), F32), jax.ShapeDtypeStruct((L, hw), F32),
                   jax.ShapeDtypeStruct((L, hw), MXU_DTYPE),
                   jax.ShapeDtypeStruct((L, SWA_Q_HEADS * HEAD_PAD), MXU_DTYPE),
                   jax.ShapeDtypeStruct((OUT_PAD, D_MODEL), F32), jax.ShapeDtypeStruct((1, D_MODEL), F32),
                   jax.ShapeDtypeStruct((1, D_MODEL), F32), jax.ShapeDtypeStruct((1, HEAD_PAD), F32)],
        compiler_params=_params(("arbitrary",), VMEM_BIG),
    )(dn2, dh2, h1, mix, cat, o_f, o_b, ga, gn, g_post, g_pre2, w_out_p)


def _in_bwd_call(x, dh1, g_pre, w_in_t, pairs, singles, halos, dep=None):
    L = x.shape[0]
    tm = min(512, L)
    per = tm // SWA_BLOCK
    n_pair, n_single, n_halo = len(pairs), len(singles), len(halos)
    groups = [c for c, _ in pairs] + [c for c, _ in singles] + [c for c, _ in halos]

    def body(*refs):
        x_r, dh1_r, g_r, w_r = refs[:4]
        pair_refs = refs[4:4 + 2 * n_pair]
        single_refs = refs[4 + 2 * n_pair:4 + 2 * n_pair + n_single]
        halo_refs = refs[4 + 2 * n_pair + n_single:4 + 2 * n_pair + n_single + per * n_halo]
        dx_r, dw_r, dg_r = refs[4 + 2 * n_pair + n_single + per * n_halo:]

        @pl.when(pl.program_id(0) == 0)
        def _():
            dw_r[...] = jnp.zeros_like(dw_r)
            dg_r[...] = jnp.zeros_like(dg_r)

        xv = x_r[...]
        r = _rms_r(xv)
        g = g_r[...]
        u = _mx(xv * r * g)
        vals = [pair_refs[2 * i][...].astype(F32) + pair_refs[2 * i + 1][...].astype(F32) for i in range(n_pair)]
        vals += [ref[...].astype(F32) for ref in single_refs]
        vals += [jnp.concatenate([ref[...] for ref in halo_refs[per * i:per * (i + 1)]], axis=0)
                 for i in range(n_halo)]
        ds = [_mx(_squeeze_heads(val) if heads else val) for (_, _, heads), val in zip(groups, vals)]
        du = jnp.zeros((tm, D_MODEL), F32)
        for (first, rows, _), d in zip(groups, ds):
            du = du + _dot(d, w_r[first:first + rows, :])
        for (first, rows, _), d in zip(groups, ds):
            dw_r[first:first + rows, :] += _dot_tn(d, u)
        dx, dg = _rms_bwd(xv, r, g, du)
        dx_r[...] = dh1_r[...] + dx
        dg_r[...] += dg

    arrays = [a for _, pr in pairs for a in pr] + [a for _, a in singles]
    specs = [_row_spec(tm, a.shape[1]) for a in arrays]
    for _, a in halos:
        specs += [pl.BlockSpec((SWA_BLOCK, a.shape[1]), lambda i, j=j: (per * i + 1 + j, 0)) for j in range(per)]
        arrays += [a] * per
    body, extra, extra_specs = _after(body, 4 + len(arrays), dep)
    return pl.pallas_call(
        body, name="in_bwd", grid=(L // tm,),
        in_specs=[_row_spec(tm, D_MODEL), _row_spec(tm, D_MODEL), _full_spec((1, D_MODEL)), _vmem_spec()] + specs
        + extra_specs,
        out_specs=[_row_spec(tm, D_MODEL), _full_spec((IN_COLS, D_MODEL)), _full_spec((1, D_MODEL))],
        out_shape=[jax.ShapeDtypeStruct((L, D_MODEL), F32), jax.ShapeDtypeStruct((IN_COLS, D_MODEL), F32),
                   jax.ShapeDtypeStruct((1, D_MODEL), F32)],
        compiler_params=_params(("arbitrary",), VMEM_BIG),
    )(x, dh1, g_pre, w_in_t, *arrays, *extra)


def _adamw_math(w, g, m, v):
    m = ADAM_B1 * m + (1.0 - ADAM_B1) * g
    v = ADAM_B2 * v + (1.0 - ADAM_B2) * (g * g)
    m_hat = m / (1.0 - ADAM_B1 ** ADAM_STEP)
    v_hat = v / (1.0 - ADAM_B2 ** ADAM_STEP)
    delta = -ADAM_LR * (m_hat / (jnp.sqrt(v_hat) + ADAM_EPS) + ADAM_WD * w)
    return delta, m, v


def _adamw_call(w, g, m, v, name, dep=None):
    rows, cols = w.shape
    tr = min(256, rows)

    def body(w_r, g_r, m_r, v_r, d_r, nm_r, nv_r):
        d_r[...], nm_r[...], nv_r[...] = _adamw_math(w_r[...], g_r[...], m_r[...], v_r[...])

    if rows % tr == 0:
        spec, steps = _row_spec(tr, cols), rows // tr
    else:
        spec, steps = pl.BlockSpec((rows, 256), lambda i: (0, i)), cols // 256
    body, extra, extra_specs = _after(body, 4, dep)
    return pl.pallas_call(
        body, name=name, grid=(steps,),
        in_specs=[spec] * 4 + extra_specs, out_specs=[spec] * 3,
        out_shape=[jax.ShapeDtypeStruct(w.shape, F32)] * 3,
        compiler_params=_params(("arbitrary",)),
    )(w, g, m, v, *extra)


def _position():
    return lax.axis_index("x"), lax.axis_index("y"), lax.axis_index("c")


def _other_chips(x, y):
    return [(1 - x, y), (x, 1 - y), (1 - x, 1 - y)]


ROWS, COLS = -2, -1


def _half(ref, which, axis):
    size = ref.shape[axis] // 2
    span = pl.ds(pl.multiple_of(which * size, 16 if axis == ROWS else 128), size)
    index = [slice(None)] * len(ref.shape)
    index[axis] = span
    return ref.at[tuple(index)]


def _quarter(ref, half, which, axis):
    size = ref.shape[axis] // 4
    span = pl.ds(pl.multiple_of((2 * half + which) * size, 16 if axis == ROWS else 128), size)
    index = [slice(None)] * len(ref.shape)
    index[axis] = span
    return ref.at[tuple(index)]


def _first_gather_call(shards, axes, routed):
    n = len(shards)
    per = 7

    def body(*refs):
        srcs, outs = refs[:n], refs[n:2 * n]
        send_sems, recv_sems, local_sems = refs[2 * n:]
        x, y, c = _position()
        me, sibling = (x, y, c), (x, y, 1 - c)
        x_side, y_side, across = _other_chips(x, y)
        local = [pltpu.make_async_copy(srcs[a], outs[a].at[2 * x + y], local_sems.at[a]) for a in range(n)]
        for cp in local:
            cp.start()

        def copy(a, k, dst, to, src=None):
            return pltpu.make_async_remote_copy(
                src_ref=dst if src is None else src, dst_ref=dst, send_sem=send_sems.at[per * a + k],
                recv_sem=recv_sems.at[per * a + k], device_id=to, device_id_type=MESH_ID)

        def half(a, chip, pc):
            return _half(outs[a].at[2 * chip[0] + chip[1]], pc, axes[a])

        def quarter(a, chip, q):
            return _quarter(outs[a].at[2 * chip[0] + chip[1]], c, q, axes[a])

        sends = []
        for a in range(n):
            mine = _half(srcs[a], c, axes[a])
            targets = (x_side, y_side) if routed[a] else (x_side, y_side, across)
            sends += [copy(a, j, half(a, (x, y), c), (*chip, c), src=mine) for j, chip in enumerate(targets)]
        for cp in sends:
            cp.start()
        for a in range(n):
            for j, chip in enumerate((x_side, y_side)):
                copy(a, j, half(a, chip, c), me).wait_recv()
                if routed[a]:
                    other = (y_side, x_side)[j]
                    sends.append(copy(a, 2 + j, quarter(a, chip, j), (*other, c)))
                    sends[-1].start()
                sends.append(copy(a, 4 + j, half(a, chip, c), sibling))
                sends[-1].start()
        for a in range(n):
            if routed[a]:
                for j in range(2):
                    copy(a, 2 + j, quarter(a, across, j), me).wait_recv()
            else:
                copy(a, 2, half(a, across, c), me).wait_recv()
            sends.append(copy(a, 6, half(a, across, c), sibling))
            sends[-1].start()
        for a in range(n):
            for k, chip in ((4, x_side), (5, y_side), (6, across)):
                copy(a, k, half(a, chip, 1 - c), me).wait_recv()
        for cp in sends:
            cp.wait_send()
        for cp in local:
            cp.wait()

    return pl.pallas_call(
        body, name="first_gather",
        in_specs=[_any_spec()] * n, out_specs=[_any_spec()] * n,
        out_shape=[jax.ShapeDtypeStruct((N_CHIPS,) + s.shape, s.dtype) for s in shards],
        scratch_shapes=[pltpu.SemaphoreType.DMA((per * n,)), pltpu.SemaphoreType.DMA((per * n,)),
                        pltpu.SemaphoreType.DMA((n,))],
    )(*shards)


def _split_start(name, arrays, n_copies, plan):
    n = len(arrays)

    def body(*refs):
        ins, send_sems, recv_sems, token = refs[:n], refs[n], refs[n + 1], refs[-1]
        for k, (src, dst, to, _) in enumerate(plan(ins)):
            pltpu.make_async_remote_copy(src_ref=src, dst_ref=dst, send_sem=send_sems.at[k],
                                         recv_sem=recv_sems.at[k], device_id=to, device_id_type=MESH_ID).start()
        token[...] = jnp.zeros_like(token)

    hbm = pl.BlockSpec(memory_space=pltpu.HBM)
    sem = pl.BlockSpec(memory_space=pltpu.SEMAPHORE)
    out = pl.pallas_call(
        body, name=name,
        out_shape=(pltpu.SemaphoreType.DMA((n_copies,)), pltpu.SemaphoreType.DMA((n_copies,)))
        + tuple(pltpu.HBM(a.shape, a.dtype) for a in arrays) + (jax.ShapeDtypeStruct((8, 128), F32),),
        in_specs=[hbm] * n, out_specs=(sem, sem) + (hbm,) * n + (_vmem_spec(),),
        input_output_aliases={i: 2 + i for i in range(n)},
        compiler_params=pltpu.CompilerParams(has_side_effects=pltpu.SideEffectType.DATAFLOW_SIDE_EFFECTING),
    )(*[pltpu.with_memory_space_constraint(a, pltpu.HBM) for a in arrays])
    return (out[0], out[1], tuple(out[2:2 + n])), out[-1]


def _split_wait(name, handle, n_copies, plan, after):
    send_sems, recv_sems, arrays = handle
    n = len(arrays)

    def body(*refs):
        ins, s_sems, r_sems = refs[:n], refs[n], refs[n + 1]
        for k, (src, dst, to, landed) in enumerate(plan(ins)):
            cp = pltpu.make_async_remote_copy(src_ref=src, dst_ref=landed, send_sem=s_sems.at[k],
                                              recv_sem=r_sems.at[k], device_id=to, device_id_type=MESH_ID)
            cp.wait_send()
            cp.wait_recv()

    hbm = pl.BlockSpec(memory_space=pltpu.HBM)
    sem = pl.BlockSpec(memory_space=pltpu.SEMAPHORE)
    out = pl.pallas_call(
        body, name=name,
        out_shape=tuple(pltpu.HBM(a.shape, a.dtype) for a in arrays),
        in_specs=[hbm] * n + [sem, sem, _any_spec()], out_specs=(hbm,) * n,
        input_output_aliases={i: i for i in range(n)},
        compiler_params=pltpu.CompilerParams(has_side_effects=pltpu.SideEffectType.DATAFLOW_SIDE_EFFECTING),
    )(*arrays, send_sems, recv_sems, after)
    return tuple(out)


def _gather_plans(axes):
    n = len(axes)

    def stage_one(refs):
        x, y, c = _position()
        copies = []
        for a, axis in enumerate(axes):
            for px, py in _other_chips(x, y):
                copies.append((_half(refs[a], c, axis), _half(refs[n + a].at[2 * x + y], c, axis),
                               (px, py, c), _half(refs[n + a].at[2 * px + py], c, axis)))
        return copies

    def stage_two(refs):
        x, y, c = _position()
        copies = []
        for a, axis in enumerate(axes):
            for px, py in _other_chips(x, y):
                piece = _half(refs[n + a].at[2 * px + py], c, axis)
                copies.append((piece, piece, (x, y, 1 - c), _half(refs[n + a].at[2 * px + py], 1 - c, axis)))
        return copies

    return stage_one, stage_two


def _pair_swap_plan(axes):
    n = len(axes)

    def plan(refs):
        x, y, c = _position()
        return [(_half(refs[a], 1 - c, axes[a]), refs[n + a], (x, y, 1 - c), refs[n + a]) for a in range(n)]

    return plan


def _chip_swap_plan(n):
    def plan(refs):
        x, y, c = _position()
        copies = []
        for a in range(n):
            for j, (px, py) in enumerate(_other_chips(x, y)):
                copies.append((refs[a].at[2 * px + py], refs[n + a].at[j], (px, py, c), refs[n + a].at[j]))
        return copies

    return plan


def _pair_join_plan(axes):
    def plan(refs):
        x, y, c = _position()
        copies = []
        for a, axis in enumerate(axes):
            mine = _half(refs[a], c, axis)
            copies.append((mine, mine, (x, y, 1 - c), _half(refs[a], 1 - c, axis)))
        return copies

    return plan


def _pair_add_call(gs, gots, pos, name, axes):
    n = len(gs)

    def body(pos_r, *refs):
        for g_r, got_r, o_r in zip(refs[:n], refs[n:2 * n], refs[2 * n:]):
            o_r[...] = (g_r[...] + got_r[...]).astype(o_r.dtype)

    def mine(axis):
        return (lambda j, p: (j, p[1], 0)) if axis == ROWS else (lambda j, p: (j, 0, p[1]))

    blocks = [(None,) + got.shape[1:] for got in gots]
    return pl.pallas_call(
        body, name=name,
        grid_spec=pltpu.PrefetchScalarGridSpec(
            num_scalar_prefetch=1, grid=(N_CHIPS,),
            in_specs=[pl.BlockSpec(blk, mine(axis)) for blk, axis in zip(blocks, axes)]
            + [pl.BlockSpec(blk, lambda j, p: (j, 0, 0)) for blk in blocks],
            out_specs=[pl.BlockSpec(blk, lambda j, p: (j, 0, 0)) for blk in blocks]),
        out_shape=[jax.ShapeDtypeStruct(got.shape, COMM_DTYPE) for got in gots],
        compiler_params=_params(("arbitrary",), VMEM_BIG),
    )(pos, *gs, *gots)


def _chip_add_call(hsums, gots, pos, name, axes):
    n = len(hsums)
    steps = 2

    def body(pos_r, *refs):
        for own_r, got_r, o_r in zip(refs[:n], refs[n:2 * n], refs[2 * n:]):
            acc = own_r[...].astype(F32)
            for j in range(3):
                acc = acc + got_r[j].astype(F32)
            o_r[...] = acc

    in_specs, got_specs, out_specs, out_shape = [], [], [], []
    for h, axis in zip(hsums, axes):
        if axis == ROWS:
            rows, cols = h.shape[1] // steps, h.shape[2]
            in_specs.append(pl.BlockSpec((None, rows, cols), lambda i, p: (p[0], i, 0)))
            got_specs.append(pl.BlockSpec((3, rows, cols), lambda i, p: (0, i, 0)))
            out_specs.append(pl.BlockSpec((rows, cols), lambda i, p: (p[1] * steps + i, 0)))
            out_shape.append(jax.ShapeDtypeStruct((2 * h.shape[1], cols), F32))
        else:
            rows, cols = h.shape[1], h.shape[2] // steps
            in_specs.append(pl.BlockSpec((None, rows, cols), lambda i, p: (p[0], 0, i)))
            got_specs.append(pl.BlockSpec((3, rows, cols), lambda i, p: (0, 0, i)))
            out_specs.append(pl.BlockSpec((rows, cols), lambda i, p: (0, p[1] * steps + i)))
            out_shape.append(jax.ShapeDtypeStruct((rows, 2 * h.shape[2]), F32))
    return pl.pallas_call(
        body, name=name,
        grid_spec=pltpu.PrefetchScalarGridSpec(
            num_scalar_prefetch=1, grid=(steps,), in_specs=in_specs + got_specs, out_specs=out_specs),
        out_shape=out_shape,
        compiler_params=_params(("arbitrary",), VMEM_BIG),
    )(pos, *hsums, *gots)


SMALL_NAMES = ("norm_mix_pre", "norm_mix_post", "norm_mlp_pre", "norm_mlp_post", "b_gate_fwd", "b_gate_bwd",
               "gla_norm", "swa_sink", "rel_bias")


N_DEVICES = 8


def _small_pack_call(grads, extras):
    operands = list(grads) + list(extras)

    def body(*refs):
        g_refs, (all_a, all_b) = refs[:len(operands)], refs[len(operands):]
        x, y, c = _position()
        me = 4 * x + 2 * y + c
        all_a[me] = jnp.zeros(all_a.shape[1:], F32)
        all_b[me] = jnp.zeros(all_b.shape[1:], F32)
        for i in range(4):
            all_a[me, i:i + 1, :] = g_refs[i][...]
        all_a[me, 4:5, 0:256] = g_refs[4][...]
        all_a[me, 5:6, 0:256] = g_refs[5][...]
        all_a[me, 6:7, 0:128] = g_refs[6][...]
        all_a[me, 7:8, 0:128] = g_refs[7][...]
        all_a[me, 7:8, 128:256] = g_refs[11][...]
        all_b[me, 0:32, 0:128] = g_refs[8][...]
        all_b[me, 32:48, :] = g_refs[9][...]
        all_b[me, 48:64, :] = g_refs[10][...]

    out_shape = [jax.ShapeDtypeStruct((N_DEVICES, 8, D_MODEL), F32), jax.ShapeDtypeStruct((N_DEVICES, 64, 256), F32)]
    return pl.pallas_call(
        body, name="small_pack",
        in_specs=[_whole_spec(a.shape) for a in operands], out_specs=[_whole_spec(s.shape) for s in out_shape],
        out_shape=out_shape,
    )(*operands)


def _everyone_plan(n):
    def plan(refs):
        x, y, c = _position()
        copies = []
        for k in range(1, N_DEVICES):
            px = 1 - x if (k >> 2) & 1 else x
            py = 1 - y if (k >> 1) & 1 else y
            pc = 1 - c if k & 1 else c
            for a in range(n):
                mine = refs[a].at[4 * x + 2 * y + c]
                copies.append((mine, mine, (px, py, pc), refs[a].at[4 * px + 2 * py + pc]))
        return copies

    return plan


def _small_adamw_call(all_a, all_b, params):
    n_small = len(SMALL_NAMES)
    wmv = [t for p in params for t in p]
    shapes = [p[0].shape for p in params]

    def body(*refs):
        all_a, all_b = refs[:2]
        wmv_refs = refs[2:2 + 3 * n_small]
        out_refs = refs[2 + 3 * n_small:]
        sum_a, sum_b = all_a[0], all_b[0]
        for d in range(1, N_DEVICES):
            sum_a = sum_a + all_a[d]
            sum_b = sum_b + all_b[d]
        gsum = [sum_a[0:1], sum_a[1:2], sum_a[2:3], sum_a[3:4], sum_a[4:5, 0:256], sum_a[5:6, 0:256],
                sum_a[6:7, 0:128], sum_a[7:8, 0:SWA_Q_HEADS], sum_b[0:32, 0:SWA_Q_HEADS]]
        for i in range(n_small):
            w_r, m_r, v_r = wmv_refs[3 * i:3 * i + 3]
            delta, new_m, new_v = _adamw_math(w_r[...], gsum[i], m_r[...], v_r[...])
            out_refs[4 * i][...] = gsum[i]
            out_refs[4 * i + 1][...] = delta
            out_refs[4 * i + 2][...] = new_m
            out_refs[4 * i + 3][...] = new_v
        out_refs[4 * n_small][...] = sum_b[32:48]
        out_refs[4 * n_small + 1][...] = sum_b[48:64]
        out_refs[4 * n_small + 2][...] = sum_a[7:8, 128:256]

    out_shape = [jax.ShapeDtypeStruct(s, F32) for s in shapes for _ in range(4)]
    out_shape += [jax.ShapeDtypeStruct((GLA_GATE_RANK, 256), F32)] * 2 + [jax.ShapeDtypeStruct((1, 128), F32)]
    out = pl.pallas_call(
        body, name="small_adamw",
        in_specs=[_whole_spec(a.shape) for a in [all_a, all_b] + wmv],
        out_specs=[_whole_spec(s.shape) for s in out_shape],
        out_shape=out_shape,
    )(all_a, all_b, *wmv)
    per_name = [tuple(out[4 * i:4 * i + 4]) for i in range(n_small)]
    return per_name, out[4 * n_small], out[4 * n_small + 1], out[4 * n_small + 2]


def _pad_heads(t, n_heads, axis=-1):
    axis = axis % t.ndim
    shape = t.shape
    t = t.reshape(shape[:axis] + (n_heads, 64) + shape[axis + 1:])
    pad = [(0, 0)] * t.ndim
    pad[axis + 1] = (0, HEAD_PAD - 64)
    return jnp.pad(t, pad).reshape(shape[:axis] + (n_heads * HEAD_PAD,) + shape[axis + 1:])


def _unpad_heads(t, n_heads, axis=-1):
    axis = axis % t.ndim
    shape = t.shape
    t = t.reshape(shape[:axis] + (n_heads, HEAD_PAD) + shape[axis + 1:])
    t = lax.slice_in_dim(t, 0, 64, axis=axis + 1)
    return t.reshape(shape[:axis] + (n_heads * 64,) + shape[axis + 1:])


def _pad_gate(w, first_row):
    return jnp.pad(_pad_heads(w, 4), ((first_row, 128 - GLA_GATE_RANK - first_row), (0, 0)))


def _own_slot(shard, chip):
    zone = lax.empty((N_CHIPS,) + shard.shape, shard.dtype)
    return lax.dynamic_update_slice(zone, shard[None], (chip,) + (0,) * shard.ndim)


def _reduce_to_owners(grads, axes, pos, tag, overlap):
    n = len(grads)

    def half_shape(g, axis):
        return (N_CHIPS, g.shape[1] // 2, g.shape[2]) if axis == ROWS else (N_CHIPS, g.shape[1], g.shape[2] // 2)

    lands = [lax.empty(half_shape(g, axis), F32) for g, axis in zip(grads, axes)]
    handle, token = _split_start(tag + "_pair_start", list(grads) + lands, n, _pair_swap_plan(axes))
    got = _split_wait(tag + "_pair_wait", handle, n, _pair_swap_plan(axes), overlap[0](token))
    sums = list(_pair_add_call(got[:n], got[n:], pos, tag + "_pair_add", axes))
    lands = [lax.empty((3,) + s.shape[1:], s.dtype) for s in sums]
    handle, token = _split_start(tag + "_chip_start", sums + lands, 3 * n, _chip_swap_plan(n))
    got = _split_wait(tag + "_chip_wait", handle, 3 * n, _chip_swap_plan(n), overlap[1](token))
    halves = list(_chip_add_call(got[:n], got[n:], pos, tag + "_chip_add", axes))
    handle, token = _split_start(tag + "_join_start", halves, n, _pair_join_plan(axes))
    return _split_wait(tag + "_join_wait", handle, n, _pair_join_plan(axes), overlap[2](token))


def kernel(x, norm_mix_pre, w_in, w_gate_up_fwd, b_gate_fwd, w_gate_up_bwd, b_gate_bwd, gla_norm, swa_sink, rel_bias, w_out, norm_mix_post, norm_mlp_pre, w_up, w_down, norm_mlp_post, loss_target, m_norm_mix_pre, m_w_in, m_w_gate_up_fwd, m_b_gate_fwd, m_w_gate_up_bwd, m_b_gate_bwd, m_gla_norm, m_swa_sink, m_rel_bias, m_w_out, m_norm_mix_post, m_norm_mlp_pre, m_w_up, m_w_down, m_norm_mlp_post, v_norm_mix_pre, v_w_in, v_w_gate_up_fwd, v_b_gate_fwd, v_w_gate_up_bwd, v_b_gate_bwd, v_gla_norm, v_swa_sink, v_rel_bias, v_w_out, v_norm_mix_post, v_norm_mlp_pre, v_w_up, v_w_down, v_norm_mlp_post):
    given = dict(locals())
    cx, cy, cc = _position()
    chip = (2 * cx + cy).astype(jnp.int32)
    pos = jnp.stack([chip, cc.astype(jnp.int32)])
    seq, tgt = x[0], loss_target[0]
    L = seq.shape[0]

    gates = jnp.concatenate([w_gate_up_fwd[0], w_gate_up_bwd[0]], axis=0).astype(COMM_DTYPE)
    all_in, all_gates = _first_gather_call([w_in[0].T.astype(COMM_DTYPE), gates], [COLS, ROWS], [True, False])
    rest = [w_out[0].astype(COMM_DTYPE), jnp.stack([w_up[0], w_down[0]]).astype(COMM_DTYPE)]
    stage_one, stage_two = _gather_plans([ROWS, ROWS])
    handle, token = _split_start("gather_chip_start", rest + [_own_slot(s, chip) for s in rest] + [all_gates], 6,
                                 stage_one)

    w_in_t = _mx(all_in.reshape(IN_COLS, D_MODEL))
    gates_full = jnp.concatenate([all_gates[j] for j in range(N_CHIPS)], axis=1)
    wgf_p = _mx(_pad_gate(gates_full[:GLA_GATE_RANK], 0))
    wgb_p = _mx(_pad_gate(gates_full[GLA_GATE_RANK:], GLA_GATE_RANK))
    bf_p, bb_p = _pad_heads(b_gate_fwd, 4), _pad_heads(b_gate_bwd, 4)
    buckets = jnp.asarray(_band_buckets())
    sink1 = swa_sink.reshape(SWA_Q_HEADS)

    qa, ka, va, ga, qs, ks, vs, za = _proj_call(seq, norm_mix_pre, w_in_t, dep=token)
    halo = ((SWA_BLOCK, SWA_BLOCK), (0, 0))
    ks_p, vs_p = jnp.pad(ks, halo), jnp.pad(vs, halo)
    o_f, o_b, s_f, s_b = _gla_fwd_call(qa, ka, va, za, wgf_p, bf_p, wgb_p, bb_p)
    bias = _bias_call(rel_bias, buckets, dep=o_f)
    arrays = _split_wait("gather_chip_wait", handle, 6, stage_one, bias)
    handle, token = _split_start("gather_pair_start", list(arrays), 6, stage_two)
    o_s = _swa_fwd_call(qs, ks_p, vs_p, bias, sink1, dep=token)
    arrays = _split_wait("gather_pair_wait", handle, 6, stage_two, o_s)
    w_out_full = _mx(arrays[2].reshape(N_CHIPS * R_OUT, D_MODEL))
    w_ud = _mx(arrays[3])
    cat, mix, h1, n2 = _mix_call(o_f, o_b, ga, o_s, seq, gla_norm, w_out_full, norm_mix_post, norm_mlp_pre)
    a, rz, dh2, dff, loss, d_post2 = _mlp_fwd_call(n2, h1, tgt, w_ud, norm_mlp_post)

    dz, dn2 = _mlp_bwd_call(dff, rz, w_ud)
    dw_down, dw_up4 = _mlp_wgrad_call(a, dff, n2, dz)
    dh1, do, dga, dos, dw_out, d_pre2, d_post, d_gn = _mix_bwd_call(
        dn2, dh2, h1, mix, cat, o_f, o_b, ga, gla_norm, norm_mix_post, norm_mlp_pre, w_out_full)
    done = {}

    def swa_backward(tok):
        done["swa"] = _swa_bwd_call(qs, ks_p, vs_p, bias, sink1, dos, dep=tok)
        return done["swa"][0]

    def gla_in_backward(tok):
        done["gla"] = _gla_bwd_call(qa, ka, va, za, do, s_f, s_b, wgf_p, bf_p, wgb_p, bb_p, dep=tok)
        dqf, dkf, dvf, dzf, _, _, dqb, dkb, dvb, dzb, _, _ = done["gla"]
        dqs, dks_p, dvs_p, _, _ = done["swa"]
        done["in"] = _in_bwd_call(
            seq, dh1, norm_mix_pre, w_in_t,
            pairs=[(_side_by_side(T_QA), (dqf, dqb)), (_side_by_side(T_KA), (dkf, dkb)), (T_VA, (dvf, dvb)),
                   (T_ZA, (dzf, dzb))],
            singles=[(T_GA, dga), (_side_by_side(T_QS), dqs)], halos=[(T_KS, dks_p), (T_VS, dvs_p)])
        return done["in"][0]

    def bias_backward(tok):
        done["rel"] = _relbias_call(done["swa"][3], done["swa"][4], buckets, dep=tok)
        return done["rel"][0]

    g_up, g_down, g_out = _reduce_to_owners(
        [dw_up4, dw_down.reshape(N_CHIPS, R_DOWN, D_MODEL), dw_out.reshape(N_CHIPS, R_OUT, D_MODEL)],
        [ROWS, ROWS, ROWS], pos, "mlp", [swa_backward, gla_in_backward, bias_backward])
    dx, dw_in_t, d_pre = done["in"]
    dwf, dbf, dwb, dbb = done["gla"][4], done["gla"][5], done["gla"][10], done["gla"][11]
    drel, dsink = done["rel"]

    small_grads = [d_pre, d_post, d_pre2, d_post2, _unpad_heads(dbf, 4), _unpad_heads(dbb, 4), d_gn, dsink, drel]
    gate_grads = [_unpad_heads(dwf[:GLA_GATE_RANK], 4), _unpad_heads(dwb[GLA_GATE_RANK:2 * GLA_GATE_RANK], 4)]
    small_params = [(given[n], given["m_" + n], given["v_" + n]) for n in SMALL_NAMES]
    upd = {}

    everyone = _everyone_plan(2)
    small_handle, small_token = _split_start(
        "small_start", list(_small_pack_call(small_grads, gate_grads + [loss])), 2 * (N_DEVICES - 1), everyone)

    def update_out(tok):
        upd["w_out"] = (g_out,) + tuple(_adamw_call(w_out[0], g_out, m_w_out[0], v_w_out[0], "adamw_w_out",
                                                    dep=tok + small_token))
        return upd["w_out"][1]

    def update_mlp(tok):
        upd["w_up"] = (g_up,) + tuple(_adamw_call(w_up[0], g_up, m_w_up[0], v_w_up[0], "adamw_w_up", dep=tok))
        upd["w_down"] = (g_down,) + tuple(
            _adamw_call(w_down[0], g_down, m_w_down[0], v_w_down[0], "adamw_w_down", dep=upd["w_up"][1]))
        all_a, all_b = _split_wait("small_wait", small_handle, 2 * (N_DEVICES - 1), everyone, upd["w_down"][1])
        per_name, done["gf_sum"], done["gb_sum"], upd["loss"] = _small_adamw_call(all_a, all_b, small_params)
        upd.update(dict(zip(SMALL_NAMES, per_name)))
        return per_name[0][1]

    def update_gates(tok):
        for name, total in (("w_gate_up_fwd", done["gf_sum"]), ("w_gate_up_bwd", done["gb_sum"])):
            g = lax.dynamic_slice(total, (0, chip * 64), (GLA_GATE_RANK, 64))
            upd[name] = (g,) + tuple(_adamw_call(given[name][0], g, given["m_" + name][0], given["v_" + name][0],
                                                 "adamw_" + name, dep=tok))
        return upd["w_gate_up_bwd"][1]

    (g_in_t,) = _reduce_to_owners([dw_in_t.reshape(N_CHIPS, R_IN, D_MODEL)], [COLS], pos, "in",
                                  [update_out, update_mlp, update_gates])
    in_t = (g_in_t,) + tuple(_adamw_call(w_in[0].T, g_in_t, m_w_in[0].T, v_w_in[0].T, "adamw_w_in"))
    upd["w_in"] = tuple(t.T for t in in_t)

    big = ("w_in", "w_gate_up_fwd", "w_gate_up_bwd", "w_out", "w_up", "w_down")
    names = ["norm_mix_pre", "w_in", "w_gate_up_fwd", "b_gate_fwd", "w_gate_up_bwd", "b_gate_bwd", "gla_norm",
             "swa_sink", "rel_bias", "w_out", "norm_mix_post", "norm_mlp_pre", "w_up", "w_down", "norm_mlp_post"]
    outs = [upd["loss"][0, 0], dx[None]]
    for kind in range(4):
        outs += [upd[n][kind][None] if n in big else upd[n][kind] for n in names]
    return tuple(outs)
```

```python
import math

import numpy as np
import jax
import jax.numpy as jnp
from jax import lax
from jax.experimental import pallas as pl
from jax.experimental.pallas import tpu as pltpu

F32 = jnp.float32
MXU_DTYPE = jnp.bfloat16
COMM_DTYPE = jnp.bfloat16

D_MODEL = 1024
D_FF = 4096
N_CHIPS = 4
GLA_HEADS = 4
GLA_CHUNK = 64
GLA_GATE_RANK = 16
GLA_GATE_NORM = 16.0
SWA_Q_HEADS = 8
SWA_KV_HEADS = 2
SWA_BLOCK = 128
REL_BUCKETS = 32
REL_MAX_DIST = 128
NORM_EPS = 1e-6
HEAD_PAD = 128

ADAM_LR = 0.001
ADAM_B1 = 0.9
ADAM_B2 = 0.999
ADAM_EPS = 1e-08
ADAM_WD = 0.01
ADAM_STEP = 10

OUT_PAD = 1024

R_IN, R_OUT, R_UP, R_DOWN = 584, 256, 1024, 1024

VMEM_BIG = 56 * 1024 * 1024
MESH_AXES = ("x", "y", "c")
MESH_ID = pl.DeviceIdType.MESH


def _mx(a):
    return a.astype(MXU_DTYPE)


def _dot(a, b):
    return jnp.dot(a, b, preferred_element_type=F32)


def _dot_nt(a, b):
    return lax.dot_general(a, b, (((1,), (1,)), ((), ())), preferred_element_type=F32)


def _dot_tn(a, b):
    return lax.dot_general(a, b, (((0,), (0,)), ((), ())), preferred_element_type=F32)


def _rms_r(x):
    return lax.rsqrt(jnp.mean(x * x, axis=-1, keepdims=True) + NORM_EPS)


def _rms_bwd(x, r, g, dy):
    xh = x * r
    gdy = dy * g
    dx = r * (gdy - xh * jnp.mean(gdy * xh, axis=-1, keepdims=True))
    return dx, jnp.sum(dy * xh, axis=0, keepdims=True)


def _low_half(rows):
    return lax.broadcasted_iota(jnp.int32, (rows, HEAD_PAD), 1) < 64


def _spread_heads(x):
    low = _low_half(x.shape[0])
    parts = []
    for p in range(x.shape[1] // HEAD_PAD):
        pair = x[:, HEAD_PAD * p:HEAD_PAD * (p + 1)]
        parts += [jnp.where(low, pair, 0.0), jnp.where(low, pltpu.roll(pair, 64, 1), 0.0)]
    return jnp.concatenate(parts, axis=1)


def _squeeze_heads(x):
    low = _low_half(x.shape[0])
    parts = []
    for p in range(x.shape[1] // (2 * HEAD_PAD)):
        even = x[:, 2 * HEAD_PAD * p:2 * HEAD_PAD * p + HEAD_PAD]
        odd = x[:, 2 * HEAD_PAD * p + HEAD_PAD:2 * HEAD_PAD * (p + 1)]
        parts.append(jnp.where(low, even, pltpu.roll(odd, 64, 1)))
    return parts[0] if len(parts) == 1 else jnp.concatenate(parts, axis=1)


def _params(sem=None, vmem=None):
    kw = {}
    if sem is not None:
        kw["dimension_semantics"] = sem
    if vmem is not None:
        kw["vmem_limit_bytes"] = vmem
    return pltpu.CompilerParams(**kw)


def _vmem_spec():
    return pl.BlockSpec(memory_space=pltpu.VMEM)


def _whole_spec(shape):
    return pl.BlockSpec(shape, lambda: (0,) * len(shape))


def _row_spec(tm, width):
    return pl.BlockSpec((tm, width), lambda i: (i, 0))


def _full_spec(shape):
    return pl.BlockSpec(shape, lambda i: (0,) * len(shape))


def _any_spec():
    return pl.BlockSpec(memory_space=pl.ANY)


def _after(body, n_in, dep):
    if dep is None:
        return body, [], []
    return (lambda *refs: body(*refs[:n_in], *refs[n_in + 1:])), [dep], [_any_spec()]


T_QA, T_KA, T_VA, T_GA = (0, 256, 4), (256, 256, 4), (512, 512, 0), (1024, 512, 0)
T_QS, T_KS, T_VS = (1568, 512, 8), (2080, 128, 2), (2208, 128, 2)
T_ZA = (1536, 128, 0)
ZA_COLS = 2 * GLA_GATE_RANK
IN_COLS = 2336


def _side_by_side(group):
    return group[0], group[1], 0


def _proj_call(x, g_pre, w_in_t, dep=None):
    L = x.shape[0]
    tm = min(512, L)
    groups = [(T_QA, F32), (T_KA, F32), (T_VA, MXU_DTYPE), (T_GA, F32),
              (T_QS, MXU_DTYPE), (T_KS, MXU_DTYPE), (T_VS, MXU_DTYPE), (T_ZA, F32)]
    widths = [rows * (2 if heads else 1) for (_, rows, heads), _ in groups]

    def body(x_ref, g_ref, w_ref, *outs):
        xv = x_ref[...]
        u = _mx(xv * _rms_r(xv) * g_ref[...])
        for ref, (grp, _) in zip(outs, groups):
            first, rows, heads = grp
            val = _dot_nt(u, w_ref[first:first + rows, :])
            if heads:
                val = _spread_heads(val)
            if grp is T_ZA:
                val = jnp.where(lax.broadcasted_iota(jnp.int32, val.shape, 1) < ZA_COLS, val, 0.0)
            if grp is T_QS:
                val = val * 0.125
            ref[...] = val.astype(ref.dtype)

    body, extra, extra_specs = _after(body, 3, dep)
    return pl.pallas_call(
        body, name="proj_fwd", grid=(L // tm,),
        in_specs=[_row_spec(tm, D_MODEL), _full_spec((1, D_MODEL)), _vmem_spec()] + extra_specs,
        out_specs=[_row_spec(tm, w) for w in widths],
        out_shape=[jax.ShapeDtypeStruct((L, w), dt) for w, (_, dt) in zip(widths, groups)],
        compiler_params=_params(("arbitrary",), VMEM_BIG),
    )(x, g_pre, w_in_t, *extra)


def _tri_masks():
    row = lax.broadcasted_iota(jnp.int32, (GLA_CHUNK, GLA_CHUNK), 0)
    col = lax.broadcasted_iota(jnp.int32, (GLA_CHUNK, GLA_CHUNK), 1)
    return row >= col, row <= col


def _chunk_sums(tri_m, x):
    hi = _mx(x)
    rest = x - hi.astype(F32)
    mid = _mx(rest)
    lo = _mx(rest - mid.astype(F32))
    return _dot(tri_m, hi) + _dot(tri_m, mid) + _dot(tri_m, lo)


def _gla_block_pre(q_r, k_r, z_r, w_r, b_r, rev, nc, qd_s, ki_s, ks_s, dec_s, keep=None):
    tri_f, tri_b = _tri_masks()
    tri_m = _mx((tri_b if rev else tri_f).astype(F32))
    g = _dot(_mx(z_r[...]), w_r[...]) + b_r[...]
    la = (jnp.minimum(g, 0.0) - jnp.log(1.0 + jnp.exp(-jnp.abs(g)))) / GLA_GATE_NORM
    sums, lasts = [], []
    for c in range(nc):
        b_c = _chunk_sums(tri_m, la[GLA_CHUNK * c:GLA_CHUNK * (c + 1)])
        blast = b_c[0:1] if rev else b_c[GLA_CHUNK - 1:GLA_CHUNK]
        dec_s[c] = jnp.exp(blast)
        sums.append(b_c)
        lasts.append(jnp.broadcast_to(blast, b_c.shape))
    b = jnp.concatenate(sums, axis=0)
    eb = jnp.exp(b)
    enb = jnp.exp(-b)
    elb = jnp.exp(jnp.concatenate(lasts, axis=0) - b)
    k = k_r[...]
    qd_s[...] = (q_r[...] * 0.125 * eb).astype(qd_s.dtype)
    ki_s[...] = (k * enb).astype(ki_s.dtype)
    ks_s[...] = (k * elb).astype(ks_s.dtype)
    if keep is not None:
        for ref, val in zip(keep, (g, eb, enb, elb)):
            ref[...] = val


def _gla_fwd_call(qa, ka, va, za, wgf, bgf, wgb, bgb):
    L = qa.shape[0]
    br = min(512, L)
    nb, nc, n_chunks = L // br, br // GLA_CHUNK, L // GLA_CHUNK
    hw = GLA_HEADS * HEAD_PAD

    def body(qaf, kaf, vaf, zaf, qab, kab, vab, zab, wgf_r, bgf_r, wgb_r, bgb_r,
             of_r, ob_r, sf_r, sb_r, st_f, st_b, pre_f, pre_b):
        @pl.when(pl.program_id(0) == 0)
        def _():
            st_f[...] = jnp.zeros_like(st_f)
            st_b[...] = jnp.zeros_like(st_b)

        _gla_block_pre(qaf, kaf, zaf, wgf_r, bgf_r, False, nc, *pre_f)
        _gla_block_pre(qab, kab, zab, wgb_r, bgb_r, True, nc, *pre_b)
        tri_f, tri_b = _tri_masks()

        def one(tri, pre, v_r, o_r, s_r, st, ci):
            qd_s, ki_s, ks_s, dec_s = pre
            rows = pl.ds(pl.multiple_of(ci * GLA_CHUNK, GLA_CHUNK), GLA_CHUNK)
            dec = dec_s[ci]
            heads = range(GLA_HEADS)
            lanes = [slice(HEAD_PAD * h, HEAD_PAD * (h + 1)) for h in heads]
            qd = [qd_s[rows, sl] for sl in lanes]
            v = [v_r[rows, sl] for sl in lanes]
            s_t = [st[h] for h in heads]
            a = [_dot_nt(qd[h], ki_s[rows, lanes[h]]) for h in heads]
            carried = [_dot_nt(qd[h], _mx(s_t[h])) for h in heads]
            grown = [_dot_tn(v[h], ks_s[rows, lanes[h]]) for h in heads]
            a = [_mx(jnp.where(tri, a[h], 0.0)) for h in heads]
            inner = [_dot(a[h], v[h]) for h in heads]
            for h in heads:
                s_r[ci, h] = s_t[h].astype(s_r.dtype)
                o_r[rows, lanes[h]] = inner[h] + carried[h]
                st[h] = s_t[h] * dec[:, lanes[h]] + grown[h]

        def loop(t, carry):
            one(tri_f, pre_f, vaf, of_r, sf_r, st_f, t)
            one(tri_b, pre_b, vab, ob_r, sb_r, st_b, nc - 1 - t)
            return carry

        lax.fori_loop(0, nc, loop, 0, unroll=True)

    fwd = lambda i: (i, 0)
    bwd = lambda i: (nb - 1 - i, 0)
    ins = lambda m: [pl.BlockSpec((br, hw), m), pl.BlockSpec((br, hw), m),
                     pl.BlockSpec((br, hw), m), pl.BlockSpec((br, 128), m)]
    wspecs = [_full_spec((128, hw)), _full_spec((1, hw))] * 2
    s_shape = (nc, GLA_HEADS, HEAD_PAD, HEAD_PAD)
    pre_scratch = [pltpu.VMEM((br, hw), MXU_DTYPE)] * 3 + [pltpu.VMEM((nc, 1, hw), F32)]
    return pl.pallas_call(
        body, name="gla_fwd", grid=(nb,),
        in_specs=ins(fwd) + ins(bwd) + wspecs,
        out_specs=[pl.BlockSpec((br, hw), fwd), pl.BlockSpec((br, hw), bwd),
                   pl.BlockSpec(s_shape, lambda i: (i, 0, 0, 0)),
                   pl.BlockSpec(s_shape, lambda i: (nb - 1 - i, 0, 0, 0))],
        out_shape=[jax.ShapeDtypeStruct((L, hw), F32), jax.ShapeDtypeStruct((L, hw), F32),
                   jax.ShapeDtypeStruct((n_chunks,) + s_shape[1:], MXU_DTYPE),
                   jax.ShapeDtypeStruct((n_chunks,) + s_shape[1:], MXU_DTYPE)],
        scratch_shapes=[pltpu.VMEM(s_shape[1:], F32), pltpu.VMEM(s_shape[1:], F32), pre_scratch, pre_scratch],
        compiler_params=_params(("arbitrary",), VMEM_BIG),
    )(qa, ka, va, za, qa, ka, va, za, wgf, bgf, wgb, bgb)


def _gla_bwd_call(qa, ka, va, za, do, sf, sb, wgf, bgf, wgb, bgb, dep=None):
    L = qa.shape[0]
    br = min(512, L)
    nb, nc = L // br, br // GLA_CHUNK
    hw = GLA_HEADS * HEAD_PAD

    def body(qaf, kaf, vaf, zaf, dof, sf_r, qab, kab, vab, zab, dob, sb_r, wgf_r, bgf_r, wgb_r, bgb_r,
             dqf, dkf, dvf, dzf, dwf, dbf, dqb, dkb, dvb, dzb, dwb, dbb, gt_f, gt_b, pre_f, pre_b):
        @pl.when(pl.program_id(0) == 0)
        def _():
            for ref in (gt_f, gt_b, dwf, dbf, dwb, dbb):
                ref[...] = jnp.zeros_like(ref)

        _gla_block_pre(qaf, kaf, zaf, wgf_r, bgf_r, False, nc, *pre_f[:4], keep=pre_f[4:8])
        _gla_block_pre(qab, kab, zab, wgb_r, bgb_r, True, nc, *pre_b[:4], keep=pre_b[4:8])
        tri_f, tri_b = _tri_masks()
        row_w = lax.broadcasted_iota(jnp.int32, (GLA_CHUNK, HEAD_PAD), 0)

        def one(rev, pre, q_r, k_r, v_r, do_r, s_r, dq_r, dk_r, dv_r, gt, ci):
            qd_s, ki_s, ks_s, dec_s, _, eb_s, enb_s, elb_s, db_s = pre
            tri = tri_b if rev else tri_f
            last_row = 0 if rev else GLA_CHUNK - 1
            rows = pl.ds(pl.multiple_of(ci * GLA_CHUNK, GLA_CHUNK), GLA_CHUNK)
            dec = dec_s[ci]
            heads = range(GLA_HEADS)
            lanes = [slice(HEAD_PAD * h, HEAD_PAD * (h + 1)) for h in heads]
            qd = [qd_s[rows, sl] for sl in lanes]
            ki = [ki_s[rows, sl] for sl in lanes]
            ks = [ks_s[rows, sl] for sl in lanes]
            v = [v_r[rows, sl] for sl in lanes]
            do_h = [_mx(do_r[rows, sl]) for sl in lanes]
            s_t = [s_r[ci, h] for h in heads]
            g_t = [gt[h] for h in heads]
            g_m = [_mx(g_t[h]) for h in heads]
            a = [_dot_nt(qd[h], ki[h]) for h in heads]
            da = [_dot_nt(do_h[h], v[h]) for h in heads]
            dv_carried = [_dot_nt(ks[h], g_m[h]) for h in heads]
            dqd_carried = [_dot(do_h[h], _mx(s_t[h])) for h in heads]
            dks = [_dot(v[h], g_m[h]) for h in heads]
            g_grown = [_dot_tn(do_h[h], qd[h]) for h in heads]
            a = [_mx(jnp.where(tri, a[h], 0.0)) for h in heads]
            da = [_mx(jnp.where(tri, da[h], 0.0)) for h in heads]
            dv_inner = [_dot_tn(a[h], do_h[h]) for h in heads]
            dqd_inner = [_dot(da[h], ki[h]) for h in heads]
            dki = [_dot_tn(da[h], qd[h]) for h in heads]
            dq, dk = [], []
            for h in heads:
                sl = lanes[h]
                dv_r[rows, sl] = (dv_inner[h] + dv_carried[h]).astype(dv_r.dtype)
                ddec = jnp.sum(g_t[h] * s_t[h].astype(F32), axis=0, keepdims=True)
                gt[h] = g_t[h] * dec[:, sl] + g_grown[h]
                dq.append((dqd_inner[h] + dqd_carried[h]) * eb_s[rows, sl] * 0.125)
                dk_state = dks[h] * elb_s[rows, sl]
                dk.append(dki[h] * enb_s[rows, sl] + dk_state)
                k = k_r[rows, sl]
                dblast = jnp.sum(dk_state * k, axis=0, keepdims=True) + dec[:, sl] * ddec
                db_s[rows, sl] = q_r[rows, sl] * dq[h] - k * dk[h] + jnp.where(row_w == last_row, dblast, 0.0)
            low = _low_half(GLA_CHUNK)
            for pair in range(GLA_HEADS // 2):
                psl = slice(HEAD_PAD * pair, HEAD_PAD * (pair + 1))
                for ref, val in ((dq_r, dq), (dk_r, dk)):
                    both = jnp.where(low, val[2 * pair], pltpu.roll(val[2 * pair + 1], 64, 1))
                    ref[rows, psl] = both.astype(ref.dtype)

        def loop(t, carry):
            one(False, pre_f, qaf, kaf, vaf, dof, sf_r, dqf, dkf, dvf, gt_f, nc - 1 - t)
            one(True, pre_b, qab, kab, vab, dob, sb_r, dqb, dkb, dvb, gt_b, t)
            return carry

        lax.fori_loop(0, nc, loop, 0, unroll=True)

        def gate_grads(rev, pre, z_r, w_r, dz_r, dw_r, dbias_r):
            g_s, db_s = pre[4], pre[8]
            back_m = _mx((tri_f if rev else tri_b).astype(F32))
            db = db_s[...]
            dla = jnp.concatenate([_chunk_sums(back_m, db[GLA_CHUNK * c:GLA_CHUNK * (c + 1)]) for c in range(nc)],
                                  axis=0)
            dg = dla * (1.0 / GLA_GATE_NORM) * (1.0 / (1.0 + jnp.exp(g_s[...])))
            dg_m = _mx(dg)
            dz_r[...] = _dot_nt(dg_m, w_r[...])
            dw_r[...] += _dot_tn(_mx(z_r[...]), dg_m)
            dbias_r[...] += jnp.sum(dg, axis=0, keepdims=True)

        gate_grads(False, pre_f, zaf, wgf_r, dzf, dwf, dbf)
        gate_grads(True, pre_b, zab, wgb_r, dzb, dwb, dbb)

    last_first = lambda i: (nb - 1 - i, 0)
    first_last = lambda i: (i, 0)
    s_shape = (nc, GLA_HEADS, HEAD_PAD, HEAD_PAD)

    def ins(m):
        return [pl.BlockSpec((br, hw), m), pl.BlockSpec((br, hw), m), pl.BlockSpec((br, hw), m),
                pl.BlockSpec((br, 128), m), pl.BlockSpec((br, hw), m),
                pl.BlockSpec(s_shape, lambda i: m(i) + (0, 0))]

    def outs(m):
        return [pl.BlockSpec((br, hw // 2), m), pl.BlockSpec((br, hw // 2), m), pl.BlockSpec((br, hw), m),
                pl.BlockSpec((br, 128), m), _full_spec((128, hw)), _full_spec((1, hw))]

    out_shape = [jax.ShapeDtypeStruct((L, hw // 2), MXU_DTYPE)] * 2 + [
        jax.ShapeDtypeStruct((L, hw), MXU_DTYPE),
        jax.ShapeDtypeStruct((L, 128), F32), jax.ShapeDtypeStruct((128, hw), F32),
        jax.ShapeDtypeStruct((1, hw), F32)]
    wspecs = [_full_spec((128, hw)), _full_spec((1, hw))] * 2
    body, extra, extra_specs = _after(body, 16, dep)
    pre_scratch = ([pltpu.VMEM((br, hw), MXU_DTYPE)] * 3 + [pltpu.VMEM((nc, 1, hw), F32)]
                   + [pltpu.VMEM((br, hw), F32)] * 5)
    return pl.pallas_call(
        body, name="gla_bwd", grid=(nb,),
        in_specs=ins(last_first) + ins(first_last) + wspecs + extra_specs,
        out_specs=outs(last_first) + outs(first_last),
        out_shape=out_shape + out_shape,
        scratch_shapes=[pltpu.VMEM(s_shape[1:], F32), pltpu.VMEM(s_shape[1:], F32), pre_scratch, pre_scratch],
        compiler_params=_params(("arbitrary",), VMEM_BIG),
    )(qa, ka, va, za, do, sf, qa, ka, va, za, do, sb, wgf, bgf, wgb, bgb, *extra)


def _t5_buckets(rel):
    nb = REL_BUCKETS // 2
    ret = (rel > 0).astype(np.int32) * nb
    n = np.abs(rel)
    max_exact = nb // 2
    large = max_exact + (np.log(np.maximum(n, 1).astype(np.float32) / max_exact)
                         / math.log(REL_MAX_DIST / max_exact) * (nb - max_exact)).astype(np.int32)
    large = np.minimum(large, nb - 1)
    return ret + np.where(n < max_exact, n, large)


SWA_GROUP = SWA_Q_HEADS // SWA_KV_HEADS
SWA_SPAN = 3 * SWA_BLOCK
SWA_GROUP_LANES = SWA_GROUP * SWA_BLOCK


def _band_buckets():
    s = np.arange(SWA_SPAN)[:, None]
    c = np.arange(SWA_BLOCK)[None, :]
    return _t5_buckets(s - SWA_BLOCK - c).astype(np.int32)


def _swa_valid(n, seq_len):
    key_pos = (n - 1) * SWA_BLOCK + lax.broadcasted_iota(jnp.int32, (SWA_SPAN, 1), 0)
    return (key_pos >= 0) & (key_pos < seq_len)


def _swa_sink_row(sink_r, kv):
    lane = lax.broadcasted_iota(jnp.int32, (1, SWA_GROUP_LANES), 1)
    row = jnp.full((1, SWA_GROUP_LANES), sink_r[kv * SWA_GROUP], F32)
    for g in range(1, SWA_GROUP):
        row = jnp.where(lane >= g * SWA_BLOCK, sink_r[kv * SWA_GROUP + g], row)
    return row


SWA_STEP_BLOCKS = 2


def _swa_group(ref, kv, rows):
    first = kv * SWA_GROUP
    return jnp.concatenate([ref[rows, HEAD_PAD * h:HEAD_PAD * (h + 1)] for h in range(first, first + SWA_GROUP)],
                           axis=0)


def _swa_softmax(scores, bias_t, sink_row, valid):
    st = jnp.where(valid, scores + bias_t, -1e30)
    m = jnp.maximum(jnp.max(st, axis=0, keepdims=True), sink_row)
    p = jnp.exp(st - m)
    e_sink = jnp.exp(sink_row - m)
    inv = 1.0 / (jnp.sum(p, axis=0, keepdims=True) + e_sink)
    return p * inv, e_sink * inv


def _swa_fwd_call(qs, ks, vs, bias, sink, dep=None):
    L = qs.shape[0]

    def block(n, rows, q_r, k_r, v_r, bias_r, sink_r, o_r):
        span = pl.ds(pl.multiple_of(n * SWA_BLOCK, SWA_BLOCK), SWA_SPAN)
        valid = _swa_valid(n, L)
        groups = range(SWA_KV_HEADS)
        lanes = [slice(HEAD_PAD * kv, HEAD_PAD * (kv + 1)) for kv in groups]
        scores = [_dot_nt(k_r[span, lanes[kv]], _swa_group(q_r, kv, rows)) for kv in groups]
        probs = [_swa_softmax(scores[kv], bias_r[kv], _swa_sink_row(sink_r, kv), valid)[0] for kv in groups]
        low = _low_half(SWA_BLOCK)
        for kv in groups:
            og = _dot_tn(_mx(probs[kv]), v_r[span, lanes[kv]])
            for pair in range(SWA_GROUP // 2):
                even = og[2 * SWA_BLOCK * pair:2 * SWA_BLOCK * pair + SWA_BLOCK]
                odd = og[2 * SWA_BLOCK * pair + SWA_BLOCK:2 * SWA_BLOCK * (pair + 1)]
                first = HEAD_PAD * (kv * SWA_GROUP // 2 + pair)
                o_r[rows, first:first + HEAD_PAD] = jnp.where(low, even, pltpu.roll(odd, 64, 1)).astype(o_r.dtype)

    def body(*refs):
        for j in range(SWA_STEP_BLOCKS):
            block(SWA_STEP_BLOCKS * pl.program_id(0) + j, slice(SWA_BLOCK * j, SWA_BLOCK * (j + 1)), *refs)

    qw = SWA_Q_HEADS * HEAD_PAD
    tm = SWA_STEP_BLOCKS * SWA_BLOCK
    body, extra, extra_specs = _after(body, 5, dep)
    return pl.pallas_call(
        body, name="swa_fwd", grid=(L // tm,),
        in_specs=[_row_spec(tm, qw), _vmem_spec(), _vmem_spec(), _vmem_spec(),
                  pl.BlockSpec(memory_space=pltpu.SMEM)] + extra_specs,
        out_specs=_row_spec(tm, qw // 2),
        out_shape=jax.ShapeDtypeStruct((L, qw // 2), MXU_DTYPE),
        compiler_params=_params(("arbitrary",), VMEM_BIG),
    )(qs, ks, vs, bias, sink, *extra)


def _swa_bwd_call(qs, ks, vs, bias, sink, do, dep=None):
    L = qs.shape[0]
    qw = SWA_Q_HEADS * HEAD_PAD
    kw = SWA_KV_HEADS * HEAD_PAD

    def body(*refs):
        dk_r, dv_r, dbias_r, dsink_r = refs[7:]

        @pl.when(pl.program_id(0) == 0)
        def _():
            for ref in (dk_r, dv_r, dbias_r, dsink_r):
                ref[...] = jnp.zeros_like(ref)

        for j in range(SWA_STEP_BLOCKS):
            block(SWA_STEP_BLOCKS * pl.program_id(0) + j, slice(SWA_BLOCK * j, SWA_BLOCK * (j + 1)), *refs)

    def block(n, rows, q_r, k_r, v_r, bias_r, sink_r, do_r, dq_r, dk_r, dv_r, dbias_r, dsink_r):
        span = pl.ds(pl.multiple_of(n * SWA_BLOCK, SWA_BLOCK), SWA_SPAN)
        valid = _swa_valid(n, L)
        groups = range(SWA_KV_HEADS)
        lanes = [slice(HEAD_PAD * kv, HEAD_PAD * (kv + 1)) for kv in groups]
        kk = [k_r[span, sl] for sl in lanes]
        vv = [v_r[span, sl] for sl in lanes]
        qg = [_swa_group(q_r, kv, rows) for kv in groups]
        dog = [_swa_group(do_r, kv, rows) for kv in groups]
        scores = [_dot_nt(kk[kv], qg[kv]) for kv in groups]
        dp = [_dot_nt(vv[kv], dog[kv]) for kv in groups]
        probs = [_swa_softmax(scores[kv], bias_r[kv], _swa_sink_row(sink_r, kv), valid) for kv in groups]
        ds_m, pn_m = [], []
        for kv in groups:
            pn, p_sink = probs[kv]
            delta = jnp.sum(pn * dp[kv], axis=0, keepdims=True)
            ds = pn * (dp[kv] - delta)
            dsink_r[kv] -= p_sink * delta
            dbias_r[kv] += ds
            ds_m.append(_mx(ds))
            pn_m.append(_mx(pn))
        dqg = [_dot_tn(ds_m[kv], kk[kv]) * 0.125 for kv in groups]
        dkk = [_dot(ds_m[kv], qg[kv]) for kv in groups]
        dvv = [_dot(pn_m[kv], dog[kv]) for kv in groups]
        low = _low_half(SWA_BLOCK)
        for kv in groups:
            for pair in range(SWA_GROUP // 2):
                even = dqg[kv][2 * SWA_BLOCK * pair:2 * SWA_BLOCK * pair + SWA_BLOCK]
                odd = dqg[kv][2 * SWA_BLOCK * pair + SWA_BLOCK:2 * SWA_BLOCK * (pair + 1)]
                first = HEAD_PAD * (kv * SWA_GROUP // 2 + pair)
                dq_r[rows, first:first + HEAD_PAD] = jnp.where(low, even, pltpu.roll(odd, 64, 1)).astype(dq_r.dtype)
            dk_r[span, lanes[kv]] += dkk[kv]
            dv_r[span, lanes[kv]] += dvv[kv]

    tm = SWA_STEP_BLOCKS * SWA_BLOCK
    body, extra, extra_specs = _after(body, 6, dep)
    return pl.pallas_call(
        body, name="swa_bwd", grid=(L // tm,),
        in_specs=[_row_spec(tm, qw), _vmem_spec(), _vmem_spec(), _vmem_spec(),
                  pl.BlockSpec(memory_space=pltpu.SMEM), _row_spec(tm, qw)] + extra_specs,
        out_specs=[_row_spec(tm, qw // 2), _vmem_spec(), _vmem_spec(), _vmem_spec(), _vmem_spec()],
        out_shape=[jax.ShapeDtypeStruct((L, qw // 2), MXU_DTYPE),
                   jax.ShapeDtypeStruct((L + 2 * SWA_BLOCK, kw), F32),
                   jax.ShapeDtypeStruct((L + 2 * SWA_BLOCK, kw), F32),
                   jax.ShapeDtypeStruct((SWA_KV_HEADS, SWA_SPAN, SWA_GROUP_LANES), F32),
                   jax.ShapeDtypeStruct((SWA_KV_HEADS, 1, SWA_GROUP_LANES), F32)],
        compiler_params=_params(("arbitrary",), VMEM_BIG),
    )(qs, ks, vs, bias, sink, do, *extra)


def _bias_call(rel_bias, buckets, dep=None):
    def body(t_r, bk_r, o_r):
        bk = bk_r[...]
        s = lax.broadcasted_iota(jnp.int32, bk.shape, 0)
        c = lax.broadcasted_iota(jnp.int32, bk.shape, 1)
        in_band = jnp.abs(s - SWA_BLOCK - c) <= SWA_BLOCK
        for h in range(SWA_Q_HEADS):
            acc = jnp.zeros(bk.shape, F32)
            for b in range(REL_BUCKETS):
                acc = jnp.where(bk == b, t_r[b, h], acc)
            g = h % SWA_GROUP
            o_r[h // SWA_GROUP, :, SWA_BLOCK * g:SWA_BLOCK * (g + 1)] = jnp.where(in_band, acc, -1e30)

    body, extra, extra_specs = _after(body, 2, dep)
    return pl.pallas_call(
        body, name="band_bias",
        in_specs=[pl.BlockSpec(memory_space=pltpu.SMEM), _vmem_spec()] + extra_specs, out_specs=_vmem_spec(),
        out_shape=jax.ShapeDtypeStruct((SWA_KV_HEADS, SWA_SPAN, SWA_GROUP_LANES), F32),
    )(rel_bias, buckets, *extra)


def _relbias_call(dbias, dsink, buckets, dep=None):
    def body(db_r, ds_r, bk_r, o_r, os_r):
        bk = bk_r[...]
        rowi = lax.broadcasted_iota(jnp.int32, (REL_BUCKETS, 128), 0)
        lanei = lax.broadcasted_iota(jnp.int32, (REL_BUCKETS, 128), 1)
        lane1 = lax.broadcasted_iota(jnp.int32, (1, 128), 1)
        acc = jnp.zeros((REL_BUCKETS, 128), F32)
        acc_sink = jnp.zeros((1, 128), F32)
        for h in range(SWA_Q_HEADS):
            kv, g = h // SWA_GROUP, h % SWA_GROUP
            lanes = slice(SWA_BLOCK * g, SWA_BLOCK * (g + 1))
            part = db_r[kv, :, lanes]
            for b in range(REL_BUCKETS):
                s = jnp.sum(jnp.where(bk == b, part, 0.0))
                acc = acc + jnp.where((rowi == b) & (lanei == h), s, 0.0)
            acc_sink = acc_sink + jnp.where(lane1 == h, jnp.sum(ds_r[kv, :, lanes]), 0.0)
        o_r[...] = acc
        os_r[...] = acc_sink

    body, extra, extra_specs = _after(body, 3, dep)
    return pl.pallas_call(
        body, name="relbias_grad",
        in_specs=[_vmem_spec()] * 3 + extra_specs, out_specs=[_vmem_spec()] * 2,
        out_shape=[jax.ShapeDtypeStruct((REL_BUCKETS, 128), F32), jax.ShapeDtypeStruct((1, 128), F32)],
    )(dbias, dsink, buckets, *extra)


def _mix_call(o_f, o_b, ga, o_s, x, gn, w_out_p, g_post, g_pre2, dep=None):
    L = x.shape[0]
    tm = min(512, L)
    hw = GLA_HEADS * HEAD_PAD

    def body(of_r, ob_r, ga_r, os_r, x_r, gn_r, w_r, gp_r, g2_r, cat_r, mix_r, h1_r, n2_r):
        gn_v = gn_r[...]
        for h in range(GLA_HEADS):
            sl = slice(HEAD_PAD * h, HEAD_PAD * (h + 1))
            oh = of_r[:, sl] + ob_r[:, sl]
            on = oh * _rms_r(oh) * gn_v
            gate = ga_r[:, sl]
            cat_r[:, sl] = (on * (gate * jax.nn.sigmoid(gate))).astype(cat_r.dtype)
        os_v = os_r[...]
        cat_r[:, hw:] = os_v
        mix = _dot(cat_r[:, :hw], w_r[:hw, :]) + _dot(os_v, w_r[hw:, :])
        mix_r[...] = mix
        h1 = x_r[...] + mix * _rms_r(mix) * gp_r[...]
        h1_r[...] = h1
        n2_r[...] = (h1 * _rms_r(h1) * g2_r[...]).astype(n2_r.dtype)

    body, extra, extra_specs = _after(body, 9, dep)
    return pl.pallas_call(
        body, name="mix_fwd", grid=(L // tm,),
        in_specs=[_row_spec(tm, hw), _row_spec(tm, hw), _row_spec(tm, hw), _row_spec(tm, OUT_PAD - hw),
                  _row_spec(tm, D_MODEL), _full_spec((1, HEAD_PAD)), _vmem_spec(),
                  _full_spec((1, D_MODEL)), _full_spec((1, D_MODEL))] + extra_specs,
        out_specs=[_row_spec(tm, OUT_PAD), _row_spec(tm, D_MODEL), _row_spec(tm, D_MODEL), _row_spec(tm, D_MODEL)],
        out_shape=[jax.ShapeDtypeStruct((L, OUT_PAD), MXU_DTYPE), jax.ShapeDtypeStruct((L, D_MODEL), F32),
                   jax.ShapeDtypeStruct((L, D_MODEL), F32), jax.ShapeDtypeStruct((L, D_MODEL), MXU_DTYPE)],
        compiler_params=_params(("arbitrary",), VMEM_BIG),
    )(o_f, o_b, ga, o_s, x, gn, w_out_p, g_post, g_pre2, *extra)


def _mlp_fwd_call(n2, h1, tgt, w_ud, g_post):
    L = n2.shape[0]
    tm = min(512, L)
    blk = D_FF // N_CHIPS

    def body(n2_r, h1_r, t_r, w_r, g_r, a_r, rz_r, dh2_r, dff_r, loss_r, dg_r):
        @pl.when(pl.program_id(0) == 0)
        def _():
            loss_r[...] = jnp.zeros_like(loss_r)
            dg_r[...] = jnp.zeros_like(dg_r)

        n2v = n2_r[...]
        ff = jnp.zeros((tm, D_MODEL), F32)
        for j in range(N_CHIPS):
            sl = slice(blk * j, blk * (j + 1))
            rz = jnp.maximum(_dot(n2v, w_r[j, 0]), 0.0)
            a = _mx(rz * rz)
            rz_r[:, sl] = rz.astype(rz_r.dtype)
            a_r[:, sl] = a
            ff = ff + _dot(a, w_r[j, 1])
        g = g_r[...]
        r = _rms_r(ff)
        err = h1_r[...] + ff * r * g - t_r[...]
        loss_r[...] += 0.5 * jnp.sum(err * err) / D_MODEL
        dh2 = err * (1.0 / D_MODEL)
        dh2_r[...] = dh2
        dff, dg = _rms_bwd(ff, r, g, dh2)
        dff_r[...] = dff.astype(dff_r.dtype)
        dg_r[...] += dg

    return pl.pallas_call(
        body, name="mlp_fwd", grid=(L // tm,),
        in_specs=[_row_spec(tm, D_MODEL), _row_spec(tm, D_MODEL), _row_spec(tm, D_MODEL),
                  _vmem_spec(), _full_spec((1, D_MODEL))],
        out_specs=[_row_spec(tm, D_FF), _row_spec(tm, D_FF), _row_spec(tm, D_MODEL), _row_spec(tm, D_MODEL),
                   _full_spec((1, 128)), _full_spec((1, D_MODEL))],
        out_shape=[jax.ShapeDtypeStruct((L, D_FF), MXU_DTYPE), jax.ShapeDtypeStruct((L, D_FF), MXU_DTYPE),
                   jax.ShapeDtypeStruct((L, D_MODEL), F32), jax.ShapeDtypeStruct((L, D_MODEL), MXU_DTYPE),
                   jax.ShapeDtypeStruct((1, 128), F32), jax.ShapeDtypeStruct((1, D_MODEL), F32)],
        compiler_params=_params(("arbitrary",), VMEM_BIG),
    )(n2, h1, tgt, w_ud, g_post)


def _mlp_bwd_call(dff, rz, w_ud):
    L = dff.shape[0]
    tm = min(512, L)
    blk = D_FF // N_CHIPS

    def body(dff_r, rz_r, w_r, dz_r, dn2_r):
        dffv = dff_r[...]
        dn2 = jnp.zeros((tm, D_MODEL), F32)
        for j in range(N_CHIPS):
            sl = slice(blk * j, blk * (j + 1))
            dz = _mx(_dot_nt(dffv, w_r[j, 1]) * 2.0 * rz_r[:, sl].astype(F32))
            dz_r[:, sl] = dz
            dn2 = dn2 + _dot_nt(dz, w_r[j, 0])
        dn2_r[...] = dn2

    return pl.pallas_call(
        body, name="mlp_bwd", grid=(L // tm,),
        in_specs=[_row_spec(tm, D_MODEL), _row_spec(tm, D_FF), _vmem_spec()],
        out_specs=[_row_spec(tm, D_FF), _row_spec(tm, D_MODEL)],
        out_shape=[jax.ShapeDtypeStruct((L, D_FF), MXU_DTYPE), jax.ShapeDtypeStruct((L, D_MODEL), F32)],
        compiler_params=_params(("arbitrary",), VMEM_BIG),
    )(dff, rz, w_ud)


def _mlp_wgrad_call(a, dff, n2, dz):
    L = a.shape[0]
    tf = 512
    per = (D_FF // N_CHIPS) // tf

    def body(a_r, dff_r, n2_r, dz_r, dwd_r, dwu_r):
        dwd_r[...] = _dot_tn(a_r[...], dff_r[...])
        dwu_r[...] = _dot_tn(n2_r[...], dz_r[...])

    return pl.pallas_call(
        body, name="mlp_wgrad", grid=(D_FF // tf,),
        in_specs=[pl.BlockSpec((L, tf), lambda j: (0, j)), _vmem_spec(), _vmem_spec(),
                  pl.BlockSpec((L, tf), lambda j: (0, j))],
        out_specs=[pl.BlockSpec((tf, D_MODEL), lambda j: (j, 0)),
                   pl.BlockSpec((None, D_MODEL, tf), lambda j: (j // per, 0, j % per))],
        out_shape=[jax.ShapeDtypeStruct((D_FF, D_MODEL), F32),
                   jax.ShapeDtypeStruct((N_CHIPS, D_MODEL, D_FF // N_CHIPS), F32)],
        compiler_params=_params(("arbitrary",), VMEM_BIG),
    )(a, dff, n2, dz)


def _mix_bwd_call(dn2, dh2, h1, mix, cat, o_f, o_b, ga, gn, g_post, g_pre2, w_out_p):
    L = dn2.shape[0]
    tm = min(512, L)
    hw = GLA_HEADS * HEAD_PAD

    def body(dn2_r, dh2_r, h1_r, mix_r, cat_r, of_r, ob_r, ga_r, gn_r, gp_r, g2_r, w_r,
             dh1_r, do_r, dga_r, dos_r, dw_r, dg2_r, dgp_r, dgn_r):
        @pl.when(pl.program_id(0) == 0)
        def _():
            for ref in (dw_r, dg2_r, dgp_r, dgn_r):
                ref[...] = jnp.zeros_like(ref)

        parts = [slice(start, start + min(256, tm)) for start in range(0, tm, 256)]
        dmix_m = []
        for rs in parts:
            h1 = h1_r[rs, :]
            dx2, dg2 = _rms_bwd(h1, _rms_r(h1), g2_r[...], dn2_r[rs, :])
            dh1 = dh2_r[rs, :] + dx2
            dh1_r[rs, :] = dh1
            dg2_r[...] += dg2
            mix = mix_r[rs, :]
            dmix, dgp = _rms_bwd(mix, _rms_r(mix), gp_r[...], dh1)
            dgp_r[...] += dgp
            dmix_m.append(_mx(dmix))
        dcat = [_dot_nt(d, w_r[...]) for d in dmix_m]
        for rs, d in zip(parts, dmix_m):
            dw_r[...] += _dot_tn(cat_r[rs, :], d)
        gn_v = gn_r[...]
        dgn = jnp.zeros((1, HEAD_PAD), F32)
        for rs, dc in zip(parts, dcat):
            dos_r[rs, :] = _spread_heads(dc[:, hw:]).astype(dos_r.dtype)
            for h in range(GLA_HEADS):
                sl = slice(HEAD_PAD * h, HEAD_PAD * (h + 1))
                oh = of_r[rs, sl] + ob_r[rs, sl]
                rr = _rms_r(oh)
                xh = oh * rr
                gate = ga_r[rs, sl]
                sg = jax.nn.sigmoid(gate)
                silu = gate * sg
                doa = dc[:, sl]
                dga_r[rs, sl] = (doa * (xh * gn_v) * (sg + silu * (1.0 - sg))).astype(dga_r.dtype)
                don = doa * silu
                gd = don * gn_v
                do_r[rs, sl] = rr * (gd - xh * jnp.mean(gd * xh, axis=-1, keepdims=True))
                dgn = dgn + jnp.sum(don * xh, axis=0, keepdims=True)
        dgn_r[...] += dgn

    return pl.pallas_call(
        body, name="mix_bwd", grid=(L // tm,),
        in_specs=[_row_spec(tm, D_MODEL)] * 4 + [_row_spec(tm, OUT_PAD)] + [_row_spec(tm, hw)] * 3
        + [_full_spec((1, HEAD_PAD)), _full_spec((1, D_MODEL)), _full_spec((1, D_MODEL)), _vmem_spec()],
        out_specs=[_row_spec(tm, D_MODEL), _row_spec(tm, hw), _row_spec(tm, hw),
                   _row_spec(tm, SWA_Q_HEADS * HEAD_PAD),
                   _full_spec((OUT_PAD, D_MODEL)), _full_spec((1, D_MODEL)), _full_spec((1, D_MODEL)),
                   _full_spec((1, HEAD_PAD))],
        out_shape=[jax.ShapeDtypeStruct((L, D_MODEL), F32), jax.ShapeDtypeStruct((L, hw), F32),
                   jax.ShapeDtypeStruct((L, hw), MXU_DTYPE),
                   jax.ShapeDtypeStruct((L, SWA_Q_HEADS * HEAD_PAD), MXU_DTYPE),
                   jax.ShapeDtypeStruct((OUT_PAD, D_MODEL), F32), jax.ShapeDtypeStruct((1, D_MODEL), F32),
                   jax.ShapeDtypeStruct((1, D_MODEL), F32), jax.ShapeDtypeStruct((1, HEAD_PAD), F32)],
        compiler_params=_params(("arbitrary",), VMEM_BIG),
    )(dn2, dh2, h1, mix, cat, o_f, o_b, ga, gn, g_post, g_pre2, w_out_p)


def _in_bwd_call(x, dh1, g_pre, w_in_t, pairs, singles, halos, dep=None):
    L = x.shape[0]
    tm = min(512, L)
    per = tm // SWA_BLOCK
    n_pair, n_single, n_halo = len(pairs), len(singles), len(halos)
    groups = [c for c, _ in pairs] + [c for c, _ in singles] + [c for c, _ in halos]

    def body(*refs):
        x_r, dh1_r, g_r, w_r = refs[:4]
        pair_refs = refs[4:4 + 2 * n_pair]
        single_refs = refs[4 + 2 * n_pair:4 + 2 * n_pair + n_single]
        halo_refs = refs[4 + 2 * n_pair + n_single:4 + 2 * n_pair + n_single + per * n_halo]
        dx_r, dw_r, dg_r = refs[4 + 2 * n_pair + n_single + per * n_halo:]

        @pl.when(pl.program_id(0) == 0)
        def _():
            dw_r[...] = jnp.zeros_like(dw_r)
            dg_r[...] = jnp.zeros_like(dg_r)

        xv = x_r[...]
        r = _rms_r(xv)
        g = g_r[...]
        u = _mx(xv * r * g)
        vals = [pair_refs[2 * i][...].astype(F32) + pair_refs[2 * i + 1][...].astype(F32) for i in range(n_pair)]
        vals += [ref[...].astype(F32) for ref in single_refs]
        vals += [jnp.concatenate([ref[...] for ref in halo_refs[per * i:per * (i + 1)]], axis=0)
                 for i in range(n_halo)]
        ds = [_mx(_squeeze_heads(val) if heads else val) for (_, _, heads), val in zip(groups, vals)]
        du = jnp.zeros((tm, D_MODEL), F32)
        for (first, rows, _), d in zip(groups, ds):
            du = du + _dot(d, w_r[first:first + rows, :])
        for (first, rows, _), d in zip(groups, ds):
            dw_r[first:first + rows, :] += _dot_tn(d, u)
        dx, dg = _rms_bwd(xv, r, g, du)
        dx_r[...] = dh1_r[...] + dx
        dg_r[...] += dg

    arrays = [a for _, pr in pairs for a in pr] + [a for _, a in singles]
    specs = [_row_spec(tm, a.shape[1]) for a in arrays]
    for _, a in halos:
        specs += [pl.BlockSpec((SWA_BLOCK, a.shape[1]), lambda i, j=j: (per * i + 1 + j, 0)) for j in range(per)]
        arrays += [a] * per
    body, extra, extra_specs = _after(body, 4 + len(arrays), dep)
    return pl.pallas_call(
        body, name="in_bwd", grid=(L // tm,),
        in_specs=[_row_spec(tm, D_MODEL), _row_spec(tm, D_MODEL), _full_spec((1, D_MODEL)), _vmem_spec()] + specs
        + extra_specs,
        out_specs=[_row_spec(tm, D_MODEL), _full_spec((IN_COLS, D_MODEL)), _full_spec((1, D_MODEL))],
        out_shape=[jax.ShapeDtypeStruct((L, D_MODEL), F32), jax.ShapeDtypeStruct((IN_COLS, D_MODEL), F32),
                   jax.ShapeDtypeStruct((1, D_MODEL), F32)],
        compiler_params=_params(("arbitrary",), VMEM_BIG),
    )(x, dh1, g_pre, w_in_t, *arrays, *extra)


def _adamw_math(w, g, m, v):
    m = ADAM_B1 * m + (1.0 - ADAM_B1) * g
    v = ADAM_B2 * v + (1.0 - ADAM_B2) * (g * g)
    m_hat = m / (1.0 - ADAM_B1 ** ADAM_STEP)
    v_hat = v / (1.0 - ADAM_B2 ** ADAM_STEP)
    delta = -ADAM_LR * (m_hat / (jnp.sqrt(v_hat) + ADAM_EPS) + ADAM_WD * w)
    return delta, m, v


def _adamw_call(w, g, m, v, name, dep=None):
    rows, cols = w.shape
    tr = min(256, rows)

    def body(w_r, g_r, m_r, v_r, d_r, nm_r, nv_r):
        d_r[...], nm_r[...], nv_r[...] = _adamw_math(w_r[...], g_r[...], m_r[...], v_r[...])

    if rows % tr == 0:
        spec, steps = _row_spec(tr, cols), rows // tr
    else:
        spec, steps = pl.BlockSpec((rows, 256), lambda i: (0, i)), cols // 256
    body, extra, extra_specs = _after(body, 4, dep)
    return pl.pallas_call(
        body, name=name, grid=(steps,),
        in_specs=[spec] * 4 + extra_specs, out_specs=[spec] * 3,
        out_shape=[jax.ShapeDtypeStruct(w.shape, F32)] * 3,
        compiler_params=_params(("arbitrary",)),
    )(w, g, m, v, *extra)


def _position():
    return lax.axis_index("x"), lax.axis_index("y"), lax.axis_index("c")


def _other_chips(x, y):
    return [(1 - x, y), (x, 1 - y), (1 - x, 1 - y)]


ROWS, COLS = -2, -1


def _half(ref, which, axis):
    size = ref.shape[axis] // 2
    span = pl.ds(pl.multiple_of(which * size, 16 if axis == ROWS else 128), size)
    index = [slice(None)] * len(ref.shape)
    index[axis] = span
    return ref.at[tuple(index)]


def _quarter(ref, half, which, axis):
    size = ref.shape[axis] // 4
    span = pl.ds(pl.multiple_of((2 * half + which) * size, 16 if axis == ROWS else 128), size)
    index = [slice(None)] * len(ref.shape)
    index[axis] = span
    return ref.at[tuple(index)]


def _first_gather_call(shards, axes, routed):
    n = len(shards)
    per = 7

    def body(*refs):
        srcs, outs = refs[:n], refs[n:2 * n]
        send_sems, recv_sems, local_sems = refs[2 * n:]
        x, y, c = _position()
        me, sibling = (x, y, c), (x, y, 1 - c)
        x_side, y_side, across = _other_chips(x, y)
        local = [pltpu.make_async_copy(srcs[a], outs[a].at[2 * x + y], local_sems.at[a]) for a in range(n)]
        for cp in local:
            cp.start()

        def copy(a, k, dst, to, src=None):
            return pltpu.make_async_remote_copy(
                src_ref=dst if src is None else src, dst_ref=dst, send_sem=send_sems.at[per * a + k],
                recv_sem=recv_sems.at[per * a + k], device_id=to, device_id_type=MESH_ID)

        def half(a, chip, pc):
            return _half(outs[a].at[2 * chip[0] + chip[1]], pc, axes[a])

        def quarter(a, chip, q):
            return _quarter(outs[a].at[2 * chip[0] + chip[1]], c, q, axes[a])

        sends = []
        for a in range(n):
            mine = _half(srcs[a], c, axes[a])
            targets = (x_side, y_side) if routed[a] else (x_side, y_side, across)
            sends += [copy(a, j, half(a, (x, y), c), (*chip, c), src=mine) for j, chip in enumerate(targets)]
        for cp in sends:
            cp.start()
        for a in range(n):
            for j, chip in enumerate((x_side, y_side)):
                copy(a, j, half(a, chip, c), me).wait_recv()
                if routed[a]:
                    other = (y_side, x_side)[j]
                    sends.append(copy(a, 2 + j, quarter(a, chip, j), (*other, c)))
                    sends[-1].start()
                sends.append(copy(a, 4 + j, half(a, chip, c), sibling))
                sends[-1].start()
        for a in range(n):
            if routed[a]:
                for j in range(2):
                    copy(a, 2 + j, quarter(a, across, j), me).wait_recv()
            else:
                copy(a, 2, half(a, across, c), me).wait_recv()
            sends.append(copy(a, 6, half(a, across, c), sibling))
            sends[-1].start()
        for a in range(n):
            for k, chip in ((4, x_side), (5, y_side), (6, across)):
                copy(a, k, half(a, chip, 1 - c), me).wait_recv()
        for cp in sends:
            cp.wait_send()
        for cp in local:
            cp.wait()

    return pl.pallas_call(
        body, name="first_gather",
        in_specs=[_any_spec()] * n, out_specs=[_any_spec()] * n,
        out_shape=[jax.ShapeDtypeStruct((N_CHIPS,) + s.shape, s.dtype) for s in shards],
        scratch_shapes=[pltpu.SemaphoreType.DMA((per * n,)), pltpu.SemaphoreType.DMA((per * n,)),
                        pltpu.SemaphoreType.DMA((n,))],
    )(*shards)


def _split_start(name, arrays, n_copies, plan):
    n = len(arrays)

    def body(*refs):
        ins, send_sems, recv_sems, token = refs[:n], refs[n], refs[n + 1], refs[-1]
        for k, (src, dst, to, _) in enumerate(plan(ins)):
            pltpu.make_async_remote_copy(src_ref=src, dst_ref=dst, send_sem=send_sems.at[k],
                                         recv_sem=recv_sems.at[k], device_id=to, device_id_type=MESH_ID).start()
        token[...] = jnp.zeros_like(token)

    hbm = pl.BlockSpec(memory_space=pltpu.HBM)
    sem = pl.BlockSpec(memory_space=pltpu.SEMAPHORE)
    out = pl.pallas_call(
        body, name=name,
        out_shape=(pltpu.SemaphoreType.DMA((n_copies,)), pltpu.SemaphoreType.DMA((n_copies,)))
        + tuple(pltpu.HBM(a.shape, a.dtype) for a in arrays) + (jax.ShapeDtypeStruct((8, 128), F32),),
        in_specs=[hbm] * n, out_specs=(sem, sem) + (hbm,) * n + (_vmem_spec(),),
        input_output_aliases={i: 2 + i for i in range(n)},
        compiler_params=pltpu.CompilerParams(has_side_effects=pltpu.SideEffectType.DATAFLOW_SIDE_EFFECTING),
    )(*[pltpu.with_memory_space_constraint(a, pltpu.HBM) for a in arrays])
    return (out[0], out[1], tuple(out[2:2 + n])), out[-1]


def _split_wait(name, handle, n_copies, plan, after):
    send_sems, recv_sems, arrays = handle
    n = len(arrays)

    def body(*refs):
        ins, s_sems, r_sems = refs[:n], refs[n], refs[n + 1]
        for k, (src, dst, to, landed) in enumerate(plan(ins)):
            cp = pltpu.make_async_remote_copy(src_ref=src, dst_ref=landed, send_sem=s_sems.at[k],
                                              recv_sem=r_sems.at[k], device_id=to, device_id_type=MESH_ID)
            cp.wait_send()
            cp.wait_recv()

    hbm = pl.BlockSpec(memory_space=pltpu.HBM)
    sem = pl.BlockSpec(memory_space=pltpu.SEMAPHORE)
    out = pl.pallas_call(
        body, name=name,
        out_shape=tuple(pltpu.HBM(a.shape, a.dtype) for a in arrays),
        in_specs=[hbm] * n + [sem, sem, _any_spec()], out_specs=(hbm,) * n,
        input_output_aliases={i: i for i in range(n)},
        compiler_params=pltpu.CompilerParams(has_side_effects=pltpu.SideEffectType.DATAFLOW_SIDE_EFFECTING),
    )(*arrays, send_sems, recv_sems, after)
    return tuple(out)


def _gather_plans(axes):
    n = len(axes)

    def stage_one(refs):
        x, y, c = _position()
        copies = []
        for a, axis in enumerate(axes):
            for px, py in _other_chips(x, y):
                copies.append((_half(refs[a], c, axis), _half(refs[n + a].at[2 * x + y], c, axis),
                               (px, py, c), _half(refs[n + a].at[2 * px + py], c, axis)))
        return copies

    def stage_two(refs):
        x, y, c = _position()
        copies = []
        for a, axis in enumerate(axes):
            for px, py in _other_chips(x, y):
                piece = _half(refs[n + a].at[2 * px + py], c, axis)
                copies.append((piece, piece, (x, y, 1 - c), _half(refs[n + a].at[2 * px + py], 1 - c, axis)))
        return copies

    return stage_one, stage_two


def _pair_swap_plan(axes):
    n = len(axes)

    def plan(refs):
        x, y, c = _position()
        return [(_half(refs[a], 1 - c, axes[a]), refs[n + a], (x, y, 1 - c), refs[n + a]) for a in range(n)]

    return plan


def _chip_swap_plan(n):
    def plan(refs):
        x, y, c = _position()
        copies = []
        for a in range(n):
            for j, (px, py) in enumerate(_other_chips(x, y)):
                copies.append((refs[a].at[2 * px + py], refs[n + a].at[j], (px, py, c), refs[n + a].at[j]))
        return copies

    return plan


def _pair_join_plan(axes):
    def plan(refs):
        x, y, c = _position()
        copies = []
        for a, axis in enumerate(axes):
            mine = _half(refs[a], c, axis)
            copies.append((mine, mine, (x, y, 1 - c), _half(refs[a], 1 - c, axis)))
        return copies

    return plan


def _pair_add_call(gs, gots, pos, name, axes):
    n = len(gs)

    def body(pos_r, *refs):
        for g_r, got_r, o_r in zip(refs[:n], refs[n:2 * n], refs[2 * n:]):
            o_r[...] = (g_r[...] + got_r[...]).astype(o_r.dtype)

    def mine(axis):
        return (lambda j, p: (j, p[1], 0)) if axis == ROWS else (lambda j, p: (j, 0, p[1]))

    blocks = [(None,) + got.shape[1:] for got in gots]
    return pl.pallas_call(
        body, name=name,
        grid_spec=pltpu.PrefetchScalarGridSpec(
            num_scalar_prefetch=1, grid=(N_CHIPS,),
            in_specs=[pl.BlockSpec(blk, mine(axis)) for blk, axis in zip(blocks, axes)]
            + [pl.BlockSpec(blk, lambda j, p: (j, 0, 0)) for blk in blocks],
            out_specs=[pl.BlockSpec(blk, lambda j, p: (j, 0, 0)) for blk in blocks]),
        out_shape=[jax.ShapeDtypeStruct(got.shape, COMM_DTYPE) for got in gots],
        compiler_params=_params(("arbitrary",), VMEM_BIG),
    )(pos, *gs, *gots)


def _chip_add_call(hsums, gots, pos, name, axes):
    n = len(hsums)
    steps = 2

    def body(pos_r, *refs):
        for own_r, got_r, o_r in zip(refs[:n], refs[n:2 * n], refs[2 * n:]):
            acc = own_r[...].astype(F32)
            for j in range(3):
                acc = acc + got_r[j].astype(F32)
            o_r[...] = acc

    in_specs, got_specs, out_specs, out_shape = [], [], [], []
    for h, axis in zip(hsums, axes):
        if axis == ROWS:
            rows, cols = h.shape[1] // steps, h.shape[2]
            in_specs.append(pl.BlockSpec((None, rows, cols), lambda i, p: (p[0], i, 0)))
            got_specs.append(pl.BlockSpec((3, rows, cols), lambda i, p: (0, i, 0)))
            out_specs.append(pl.BlockSpec((rows, cols), lambda i, p: (p[1] * steps + i, 0)))
            out_shape.append(jax.ShapeDtypeStruct((2 * h.shape[1], cols), F32))
        else:
            rows, cols = h.shape[1], h.shape[2] // steps
            in_specs.append(pl.BlockSpec((None, rows, cols), lambda i, p: (p[0], 0, i)))
            got_specs.append(pl.BlockSpec((3, rows, cols), lambda i, p: (0, 0, i)))
            out_specs.append(pl.BlockSpec((rows, cols), lambda i, p: (0, p[1] * steps + i)))
            out_shape.append(jax.ShapeDtypeStruct((rows, 2 * h.shape[2]), F32))
    return pl.pallas_call(
        body, name=name,
        grid_spec=pltpu.PrefetchScalarGridSpec(
            num_scalar_prefetch=1, grid=(steps,), in_specs=in_specs + got_specs, out_specs=out_specs),
        out_shape=out_shape,
        compiler_params=_params(("arbitrary",), VMEM_BIG),
    )(pos, *hsums, *gots)


SMALL_NAMES = ("norm_mix_pre", "norm_mix_post", "norm_mlp_pre", "norm_mlp_post", "b_gate_fwd", "b_gate_bwd",
               "gla_norm", "swa_sink", "rel_bias")


N_DEVICES = 8


def _small_pack_call(grads, extras):
    operands = list(grads) + list(extras)

    def body(*refs):
        g_refs, (all_a, all_b) = refs[:len(operands)], refs[len(operands):]
        x, y, c = _position()
        me = 4 * x + 2 * y + c
        all_a[me] = jnp.zeros(all_a.shape[1:], F32)
        all_b[me] = jnp.zeros(all_b.shape[1:], F32)
        for i in range(4):
            all_a[me, i:i + 1, :] = g_refs[i][...]
        all_a[me, 4:5, 0:256] = g_refs[4][...]
        all_a[me, 5:6, 0:256] = g_refs[5][...]
        all_a[me, 6:7, 0:128] = g_refs[6][...]
        all_a[me, 7:8, 0:128] = g_refs[7][...]
        all_a[me, 7:8, 128:256] = g_refs[11][...]
        all_b[me, 0:32, 0:128] = g_refs[8][...]
        all_b[me, 32:48, :] = g_refs[9][...]
        all_b[me, 48:64, :] = g_refs[10][...]

    out_shape = [jax.ShapeDtypeStruct((N_DEVICES, 8, D_MODEL), F32), jax.ShapeDtypeStruct((N_DEVICES, 64, 256), F32)]
    return pl.pallas_call(
        body, name="small_pack",
        in_specs=[_whole_spec(a.shape) for a in operands], out_specs=[_whole_spec(s.shape) for s in out_shape],
        out_shape=out_shape,
    )(*operands)


def _everyone_plan(n):
    def plan(refs):
        x, y, c = _position()
        copies = []
        for k in range(1, N_DEVICES):
            px = 1 - x if (k >> 2) & 1 else x
            py = 1 - y if (k >> 1) & 1 else y
            pc = 1 - c if k & 1 else c
            for a in range(n):
                mine = refs[a].at[4 * x + 2 * y + c]
                copies.append((mine, mine, (px, py, pc), refs[a].at[4 * px + 2 * py + pc]))
        return copies

    return plan


def _small_adamw_call(all_a, all_b, params):
    n_small = len(SMALL_NAMES)
    wmv = [t for p in params for t in p]
    shapes = [p[0].shape for p in params]

    def body(*refs):
        all_a, all_b = refs[:2]
        wmv_refs = refs[2:2 + 3 * n_small]
        out_refs = refs[2 + 3 * n_small:]
        sum_a, sum_b = all_a[0], all_b[0]
        for d in range(1, N_DEVICES):
            sum_a = sum_a + all_a[d]
            sum_b = sum_b + all_b[d]
        gsum = [sum_a[0:1], sum_a[1:2], sum_a[2:3], sum_a[3:4], sum_a[4:5, 0:256], sum_a[5:6, 0:256],
                sum_a[6:7, 0:128], sum_a[7:8, 0:SWA_Q_HEADS], sum_b[0:32, 0:SWA_Q_HEADS]]
        for i in range(n_small):
            w_r, m_r, v_r = wmv_refs[3 * i:3 * i + 3]
            delta, new_m, new_v = _adamw_math(w_r[...], gsum[i], m_r[...], v_r[...])
            out_refs[4 * i][...] = gsum[i]
            out_refs[4 * i + 1][...] = delta
            out_refs[4 * i + 2][...] = new_m
            out_refs[4 * i + 3][...] = new_v
        out_refs[4 * n_small][...] = sum_b[32:48]
        out_refs[4 * n_small + 1][...] = sum_b[48:64]
        out_refs[4 * n_small + 2][...] = sum_a[7:8, 128:256]

    out_shape = [jax.ShapeDtypeStruct(s, F32) for s in shapes for _ in range(4)]
    out_shape += [jax.ShapeDtypeStruct((GLA_GATE_RANK, 256), F32)] * 2 + [jax.ShapeDtypeStruct((1, 128), F32)]
    out = pl.pallas_call(
        body, name="small_adamw",
        in_specs=[_whole_spec(a.shape) for a in [all_a, all_b] + wmv],
        out_specs=[_whole_spec(s.shape) for s in out_shape],
        out_shape=out_shape,
    )(all_a, all_b, *wmv)
    per_name = [tuple(out[4 * i:4 * i + 4]) for i in range(n_small)]
    return per_name, out[4 * n_small], out[4 * n_small + 1], out[4 * n_small + 2]


def _pad_heads(t, n_heads, axis=-1):
    axis = axis % t.ndim
    shape = t.shape
    t = t.reshape(shape[:axis] + (n_heads, 64) + shape[axis + 1:])
    pad = [(0, 0)] * t.ndim
    pad[axis + 1] = (0, HEAD_PAD - 64)
    return jnp.pad(t, pad).reshape(shape[:axis] + (n_heads * HEAD_PAD,) + shape[axis + 1:])


def _unpad_heads(t, n_heads, axis=-1):
    axis = axis % t.ndim
    shape = t.shape
    t = t.reshape(shape[:axis] + (n_heads, HEAD_PAD) + shape[axis + 1:])
    t = lax.slice_in_dim(t, 0, 64, axis=axis + 1)
    return t.reshape(shape[:axis] + (n_heads * 64,) + shape[axis + 1:])


def _pad_gate(w, first_row):
    return jnp.pad(_pad_heads(w, 4), ((first_row, 128 - GLA_GATE_RANK - first_row), (0, 0)))


def _own_slot(shard, chip):
    zone = lax.empty((N_CHIPS,) + shard.shape, shard.dtype)
    return lax.dynamic_update_slice(zone, shard[None], (chip,) + (0,) * shard.ndim)


def _reduce_to_owners(grads, axes, pos, tag, overlap):
    n = len(grads)

    def half_shape(g, axis):
        return (N_CHIPS, g.shape[1] // 2, g.shape[2]) if axis == ROWS else (N_CHIPS, g.shape[1], g.shape[2] // 2)

    lands = [lax.empty(half_shape(g, axis), F32) for g, axis in zip(grads, axes)]
    handle, token = _split_start(tag + "_pair_start", list(grads) + lands, n, _pair_swap_plan(axes))
    got = _split_wait(tag + "_pair_wait", handle, n, _pair_swap_plan(axes), overlap[0](token))
    sums = list(_pair_add_call(got[:n], got[n:], pos, tag + "_pair_add", axes))
    lands = [lax.empty((3,) + s.shape[1:], s.dtype) for s in sums]
    handle, token = _split_start(tag + "_chip_start", sums + lands, 3 * n, _chip_swap_plan(n))
    got = _split_wait(tag + "_chip_wait", handle, 3 * n, _chip_swap_plan(n), overlap[1](token))
    halves = list(_chip_add_call(got[:n], got[n:], pos, tag + "_chip_add", axes))
    handle, token = _split_start(tag + "_join_start", halves, n, _pair_join_plan(axes))
    return _split_wait(tag + "_join_wait", handle, n, _pair_join_plan(axes), overlap[2](token))


def kernel(x, norm_mix_pre, w_in, w_gate_up_fwd, b_gate_fwd, w_gate_up_bwd, b_gate_bwd, gla_norm, swa_sink, rel_bias, w_out, norm_mix_post, norm_mlp_pre, w_up, w_down, norm_mlp_post, loss_target, m_norm_mix_pre, m_w_in, m_w_gate_up_fwd, m_b_gate_fwd, m_w_gate_up_bwd, m_b_gate_bwd, m_gla_norm, m_swa_sink, m_rel_bias, m_w_out, m_norm_mix_post, m_norm_mlp_pre, m_w_up, m_w_down, m_norm_mlp_post, v_norm_mix_pre, v_w_in, v_w_gate_up_fwd, v_b_gate_fwd, v_w_gate_up_bwd, v_b_gate_bwd, v_gla_norm, v_swa_sink, v_rel_bias, v_w_out, v_norm_mix_post, v_norm_mlp_pre, v_w_up, v_w_down, v_norm_mlp_post):
    given = dict(locals())
    cx, cy, cc = _position()
    chip = (2 * cx + cy).astype(jnp.int32)
    pos = jnp.stack([chip, cc.astype(jnp.int32)])
    seq, tgt = x[0], loss_target[0]
    L = seq.shape[0]

    gates = jnp.concatenate([w_gate_up_fwd[0], w_gate_up_bwd[0]], axis=0).astype(COMM_DTYPE)
    all_in, all_gates = _first_gather_call([w_in[0].T.astype(COMM_DTYPE), gates], [COLS, ROWS], [True, False])
    rest = [w_out[0].astype(COMM_DTYPE), jnp.stack([w_up[0], w_down[0]]).astype(COMM_DTYPE)]
    stage_one, stage_two = _gather_plans([ROWS, ROWS])
    handle, token = _split_start("gather_chip_start", rest + [_own_slot(s, chip) for s in rest] + [all_gates], 6,
                                 stage_one)

    w_in_t = _mx(all_in.reshape(IN_COLS, D_MODEL))
    gates_full = jnp.concatenate([all_gates[j] for j in range(N_CHIPS)], axis=1)
    wgf_p = _mx(_pad_gate(gates_full[:GLA_GATE_RANK], 0))
    wgb_p = _mx(_pad_gate(gates_full[GLA_GATE_RANK:], GLA_GATE_RANK))
    bf_p, bb_p = _pad_heads(b_gate_fwd, 4), _pad_heads(b_gate_bwd, 4)
    buckets = jnp.asarray(_band_buckets())
    sink1 = swa_sink.reshape(SWA_Q_HEADS)

    qa, ka, va, ga, qs, ks, vs, za = _proj_call(seq, norm_mix_pre, w_in_t, dep=token)
    halo = ((SWA_BLOCK, SWA_BLOCK), (0, 0))
    ks_p, vs_p = jnp.pad(ks, halo), jnp.pad(vs, halo)
    o_f, o_b, s_f, s_b = _gla_fwd_call(qa, ka, va, za, wgf_p, bf_p, wgb_p, bb_p)
    bias = _bias_call(rel_bias, buckets, dep=o_f)
    arrays = _split_wait("gather_chip_wait", handle, 6, stage_one, bias)
    handle, token = _split_start("gather_pair_start", list(arrays), 6, stage_two)
    o_s = _swa_fwd_call(qs, ks_p, vs_p, bias, sink1, dep=token)
    arrays = _split_wait("gather_pair_wait", handle, 6, stage_two, o_s)
    w_out_full = _mx(arrays[2].reshape(N_CHIPS * R_OUT, D_MODEL))
    w_ud = _mx(arrays[3])
    cat, mix, h1, n2 = _mix_call(o_f, o_b, ga, o_s, seq, gla_norm, w_out_full, norm_mix_post, norm_mlp_pre)
    a, rz, dh2, dff, loss, d_post2 = _mlp_fwd_call(n2, h1, tgt, w_ud, norm_mlp_post)

    dz, dn2 = _mlp_bwd_call(dff, rz, w_ud)
    dw_down, dw_up4 = _mlp_wgrad_call(a, dff, n2, dz)
    dh1, do, dga, dos, dw_out, d_pre2, d_post, d_gn = _mix_bwd_call(
        dn2, dh2, h1, mix, cat, o_f, o_b, ga, gla_norm, norm_mix_post, norm_mlp_pre, w_out_full)
    done = {}

    def swa_backward(tok):
        done["swa"] = _swa_bwd_call(qs, ks_p, vs_p, bias, sink1, dos, dep=tok)
        return done["swa"][0]

    def gla_in_backward(tok):
        done["gla"] = _gla_bwd_call(qa, ka, va, za, do, s_f, s_b, wgf_p, bf_p, wgb_p, bb_p, dep=tok)
        dqf, dkf, dvf, dzf, _, _, dqb, dkb, dvb, dzb, _, _ = done["gla"]
        dqs, dks_p, dvs_p, _, _ = done["swa"]
        done["in"] = _in_bwd_call(
            seq, dh1, norm_mix_pre, w_in_t,
            pairs=[(_side_by_side(T_QA), (dqf, dqb)), (_side_by_side(T_KA), (dkf, dkb)), (T_VA, (dvf, dvb)),
                   (T_ZA, (dzf, dzb))],
            singles=[(T_GA, dga), (_side_by_side(T_QS), dqs)], halos=[(T_KS, dks_p), (T_VS, dvs_p)])
        return done["in"][0]

    def bias_backward(tok):
        done["rel"] = _relbias_call(done["swa"][3], done["swa"][4], buckets, dep=tok)
        return done["rel"][0]

    g_up, g_down, g_out = _reduce_to_owners(
        [dw_up4, dw_down.reshape(N_CHIPS, R_DOWN, D_MODEL), dw_out.reshape(N_CHIPS, R_OUT, D_MODEL)],
        [ROWS, ROWS, ROWS], pos, "mlp", [swa_backward, gla_in_backward, bias_backward])
    dx, dw_in_t, d_pre = done["in"]
    dwf, dbf, dwb, dbb = done["gla"][4], done["gla"][5], done["gla"][10], done["gla"][11]
    drel, dsink = done["rel"]

    small_grads = [d_pre, d_post, d_pre2, d_post2, _unpad_heads(dbf, 4), _unpad_heads(dbb, 4), d_gn, dsink, drel]
    gate_grads = [_unpad_heads(dwf[:GLA_GATE_RANK], 4), _unpad_heads(dwb[GLA_GATE_RANK:2 * GLA_GATE_RANK], 4)]
    small_params = [(given[n], given["m_" + n], given["v_" + n]) for n in SMALL_NAMES]
    upd = {}

    everyone = _everyone_plan(2)
    small_handle, small_token = _split_start(
        "small_start", list(_small_pack_call(small_grads, gate_grads + [loss])), 2 * (N_DEVICES - 1), everyone)

    def update_out(tok):
        upd["w_out"] = (g_out,) + tuple(_adamw_call(w_out[0], g_out, m_w_out[0], v_w_out[0], "adamw_w_out",
                                                    dep=tok + small_token))
        return upd["w_out"][1]

    def update_mlp(tok):
        upd["w_up"] = (g_up,) + tuple(_adamw_call(w_up[0], g_up, m_w_up[0], v_w_up[0], "adamw_w_up", dep=tok))
        upd["w_down"] = (g_down,) + tuple(
            _adamw_call(w_down[0], g_down, m_w_down[0], v_w_down[0], "adamw_w_down", dep=upd["w_up"][1]))
        all_a, all_b = _split_wait("small_wait", small_handle, 2 * (N_DEVICES - 1), everyone, upd["w_down"][1])
        per_name, done["gf_sum"], done["gb_sum"], upd["loss"] = _small_adamw_call(all_a, all_b, small_params)
        upd.update(dict(zip(SMALL_NAMES, per_name)))
        return per_name[0][1]

    def update_gates(tok):
        for name, total in (("w_gate_up_fwd", done["gf_sum"]), ("w_gate_up_bwd", done["gb_sum"])):
            g = lax.dynamic_slice(total, (0, chip * 64), (GLA_GATE_RANK, 64))
            upd[name] = (g,) + tuple(_adamw_call(given[name][0], g, given["m_" + name][0], given["v_" + name][0],
                                                 "adamw_" + name, dep=tok))
        return upd["w_gate_up_bwd"][1]

    (g_in_t,) = _reduce_to_owners([dw_in_t.reshape(N_CHIPS, R_IN, D_MODEL)], [COLS], pos, "in",
                                  [update_out, update_mlp, update_gates])
    in_t = (g_in_t,) + tuple(_adamw_call(w_in[0].T, g_in_t, m_w_in[0].T, v_w_in[0].T, "adamw_w_in"))
    upd["w_in"] = tuple(t.T for t in in_t)

    big = ("w_in", "w_gate_up_fwd", "w_gate_up_bwd", "w_out", "w_up", "w_down")
    names = ["norm_mix_pre", "w_in", "w_gate_up_fwd", "b_gate_fwd", "w_gate_up_bwd", "b_gate_bwd", "gla_norm",
             "swa_sink", "rel_bias", "w_out", "norm_mix_post", "norm_mlp_pre", "w_up", "w_down", "norm_mlp_post"]
    outs = [upd["loss"][0, 0], dx[None]]
    for kind in range(4):
        outs += [upd[n][kind][None] if n in big else upd[n][kind] for n in names]
    return tuple(outs)
```

```python
import math

import numpy as np
import jax
import jax.numpy as jnp
from jax import lax
from jax.experimental import pallas as pl
from jax.experimental.pallas import tpu as pltpu

F32 = jnp.float32
MXU_DTYPE = jnp.bfloat16
COMM_DTYPE = jnp.bfloat16

D_MODEL = 1024
D_FF = 4096
N_CHIPS = 4
GLA_HEADS = 4
GLA_CHUNK = 64
GLA_GATE_RANK = 16
GLA_GATE_NORM = 16.0
SWA_Q_HEADS = 8
SWA_KV_HEADS = 2
SWA_BLOCK = 128
REL_BUCKETS = 32
REL_MAX_DIST = 128
NORM_EPS = 1e-6
HEAD_PAD = 128

ADAM_LR = 0.001
ADAM_B1 = 0.9
ADAM_B2 = 0.999
ADAM_EPS = 1e-08
ADAM_WD = 0.01
ADAM_STEP = 10

OUT_PAD = 1024

R_IN, R_OUT, R_UP, R_DOWN = 584, 256, 1024, 1024

VMEM_BIG = 56 * 1024 * 1024
MESH_AXES = ("x", "y", "c")
MESH_ID = pl.DeviceIdType.MESH


def _mx(a):
    return a.astype(MXU_DTYPE)


def _dot(a, b):
    return jnp.dot(a, b, preferred_element_type=F32)


def _dot_nt(a, b):
    return lax.dot_general(a, b, (((1,), (1,)), ((), ())), preferred_element_type=F32)


def _dot_tn(a, b):
    return lax.dot_general(a, b, (((0,), (0,)), ((), ())), preferred_element_type=F32)


def _rms_r(x):
    return lax.rsqrt(jnp.mean(x * x, axis=-1, keepdims=True) + NORM_EPS)


def _rms_bwd(x, r, g, dy):
    xh = x * r
    gdy = dy * g
    dx = r * (gdy - xh * jnp.mean(gdy * xh, axis=-1, keepdims=True))
    return dx, jnp.sum(dy * xh, axis=0, keepdims=True)


def _low_half(rows):
    return lax.broadcasted_iota(jnp.int32, (rows, HEAD_PAD), 1) < 64


def _spread_heads(x):
    low = _low_half(x.shape[0])
    parts = []
    for p in range(x.shape[1] // HEAD_PAD):
        pair = x[:, HEAD_PAD * p:HEAD_PAD * (p + 1)]
        parts += [jnp.where(low, pair, 0.0), jnp.where(low, pltpu.roll(pair, 64, 1), 0.0)]
    return jnp.concatenate(parts, axis=1)


def _squeeze_heads(x):
    low = _low_half(x.shape[0])
    parts = []
    for p in range(x.shape[1] // (2 * HEAD_PAD)):
        even = x[:, 2 * HEAD_PAD * p:2 * HEAD_PAD * p + HEAD_PAD]
        odd = x[:, 2 * HEAD_PAD * p + HEAD_PAD:2 * HEAD_PAD * (p + 1)]
        parts.append(jnp.where(low, even, pltpu.roll(odd, 64, 1)))
    return parts[0] if len(parts) == 1 else jnp.concatenate(parts, axis=1)


def _params(sem=None, vmem=None):
    kw = {}
    if sem is not None:
        kw["dimension_semantics"] = sem
    if vmem is not None:
        kw["vmem_limit_bytes"] = vmem
    return pltpu.CompilerParams(**kw)


def _vmem_spec():
    return pl.BlockSpec(memory_space=pltpu.VMEM)


def _whole_spec(shape):
    return pl.BlockSpec(shape, lambda: (0,) * len(shape))


def _row_spec(tm, width):
    return pl.BlockSpec((tm, width), lambda i: (i, 0))


def _full_spec(shape):
    return pl.BlockSpec(shape, lambda i: (0,) * len(shape))


def _any_spec():
    return pl.BlockSpec(memory_space=pl.ANY)


def _after(body, n_in, dep):
    if dep is None:
        return body, [], []
    return (lambda *refs: body(*refs[:n_in], *refs[n_in + 1:])), [dep], [_any_spec()]


T_QA, T_KA, T_VA, T_GA = (0, 256, 4), (256, 256, 4), (512, 512, 0), (1024, 512, 0)
T_QS, T_KS, T_VS = (1568, 512, 8), (2080, 128, 2), (2208, 128, 2)
T_ZA = (1536, 128, 0)
ZA_COLS = 2 * GLA_GATE_RANK
IN_COLS = 2336


def _side_by_side(group):
    return group[0], group[1], 0


def _proj_call(x, g_pre, w_in_t, dep=None):
    L = x.shape[0]
    tm = min(512, L)
    groups = [(T_QA, F32), (T_KA, F32), (T_VA, MXU_DTYPE), (T_GA, F32),
              (T_QS, MXU_DTYPE), (T_KS, MXU_DTYPE), (T_VS, MXU_DTYPE), (T_ZA, F32)]
    widths = [rows * (2 if heads else 1) for (_, rows, heads), _ in groups]

    def body(x_ref, g_ref, w_ref, *outs):
        xv = x_ref[...]
        u = _mx(xv * _rms_r(xv) * g_ref[...])
        for ref, (grp, _) in zip(outs, groups):
            first, rows, heads = grp
            val = _dot_nt(u, w_ref[first:first + rows, :])
            if heads:
                val = _spread_heads(val)
            if grp is T_ZA:
                val = jnp.where(lax.broadcasted_iota(jnp.int32, val.shape, 1) < ZA_COLS, val, 0.0)
            if grp is T_QS:
                val = val * 0.125
            ref[...] = val.astype(ref.dtype)

    body, extra, extra_specs = _after(body, 3, dep)
    return pl.pallas_call(
        body, name="proj_fwd", grid=(L // tm,),
        in_specs=[_row_spec(tm, D_MODEL), _full_spec((1, D_MODEL)), _vmem_spec()] + extra_specs,
        out_specs=[_row_spec(tm, w) for w in widths],
        out_shape=[jax.ShapeDtypeStruct((L, w), dt) for w, (_, dt) in zip(widths, groups)],
        compiler_params=_params(("arbitrary",), VMEM_BIG),
    )(x, g_pre, w_in_t, *extra)


def _tri_masks():
    row = lax.broadcasted_iota(jnp.int32, (GLA_CHUNK, GLA_CHUNK), 0)
    col = lax.broadcasted_iota(jnp.int32, (GLA_CHUNK, GLA_CHUNK), 1)
    return row >= col, row <= col


def _chunk_sums(tri_m, x):
    hi = _mx(x)
    rest = x - hi.astype(F32)
    mid = _mx(rest)
    lo = _mx(rest - mid.astype(F32))
    return _dot(tri_m, hi) + _dot(tri_m, mid) + _dot(tri_m, lo)


def _gla_block_pre(q_r, k_r, z_r, w_r, b_r, rev, nc, qd_s, ki_s, ks_s, dec_s, keep=None):
    tri_f, tri_b = _tri_masks()
    tri_m = _mx((tri_b if rev else tri_f).astype(F32))
    g = _dot(_mx(z_r[...]), w_r[...]) + b_r[...]
    la = (jnp.minimum(g, 0.0) - jnp.log(1.0 + jnp.exp(-jnp.abs(g)))) / GLA_GATE_NORM
    sums, lasts = [], []
    for c in range(nc):
        b_c = _chunk_sums(tri_m, la[GLA_CHUNK * c:GLA_CHUNK * (c + 1)])
        blast = b_c[0:1] if rev else b_c[GLA_CHUNK - 1:GLA_CHUNK]
        dec_s[c] = jnp.exp(blast)
        sums.append(b_c)
        lasts.append(jnp.broadcast_to(blast, b_c.shape))
    b = jnp.concatenate(sums, axis=0)
    eb = jnp.exp(b)
    enb = jnp.exp(-b)
    elb = jnp.exp(jnp.concatenate(lasts, axis=0) - b)
    k = k_r[...]
    qd_s[...] = (q_r[...] * 0.125 * eb).astype(qd_s.dtype)
    ki_s[...] = (k * enb).astype(ki_s.dtype)
    ks_s[...] = (k * elb).astype(ks_s.dtype)
    if keep is not None:
        for ref, val in zip(keep, (g, eb, enb, elb)):
            ref[...] = val


def _gla_fwd_call(qa, ka, va, za, wgf, bgf, wgb, bgb):
    L = qa.shape[0]
    br = min(512, L)
    nb, nc, n_chunks = L // br, br // GLA_CHUNK, L // GLA_CHUNK
    hw = GLA_HEADS * HEAD_PAD

    def body(qaf, kaf, vaf, zaf, qab, kab, vab, zab, wgf_r, bgf_r, wgb_r, bgb_r,
             of_r, ob_r, sf_r, sb_r, st_f, st_b, pre_f, pre_b):
        @pl.when(pl.program_id(0) == 0)
        def _():
            st_f[...] = jnp.zeros_like(st_f)
            st_b[...] = jnp.zeros_like(st_b)

        _gla_block_pre(qaf, kaf, zaf, wgf_r, bgf_r, False, nc, *pre_f)
        _gla_block_pre(qab, kab, zab, wgb_r, bgb_r, True, nc, *pre_b)
        tri_f, tri_b = _tri_masks()

        def one(tri, pre, v_r, o_r, s_r, st, ci):
            qd_s, ki_s, ks_s, dec_s = pre
            rows = pl.ds(pl.multiple_of(ci * GLA_CHUNK, GLA_CHUNK), GLA_CHUNK)
            dec = dec_s[ci]
            heads = range(GLA_HEADS)
            lanes = [slice(HEAD_PAD * h, HEAD_PAD * (h + 1)) for h in heads]
            qd = [qd_s[rows, sl] for sl in lanes]
            v = [v_r[rows, sl] for sl in lanes]
            s_t = [st[h] for h in heads]
            a = [_dot_nt(qd[h], ki_s[rows, lanes[h]]) for h in heads]
            carried = [_dot_nt(qd[h], _mx(s_t[h])) for h in heads]
            grown = [_dot_tn(v[h], ks_s[rows, lanes[h]]) for h in heads]
            a = [_mx(jnp.where(tri, a[h], 0.0)) for h in heads]
            inner = [_dot(a[h], v[h]) for h in heads]
            for h in heads:
                s_r[ci, h] = s_t[h].astype(s_r.dtype)
                o_r[rows, lanes[h]] = inner[h] + carried[h]
                st[h] = s_t[h] * dec[:, lanes[h]] + grown[h]

        def loop(t, carry):
            one(tri_f, pre_f, vaf, of_r, sf_r, st_f, t)
            one(tri_b, pre_b, vab, ob_r, sb_r, st_b, nc - 1 - t)
            return carry

        lax.fori_loop(0, nc, loop, 0, unroll=True)

    fwd = lambda i: (i, 0)
    bwd = lambda i: (nb - 1 - i, 0)
    ins = lambda m: [pl.BlockSpec((br, hw), m), pl.BlockSpec((br, hw), m),
                     pl.BlockSpec((br, hw), m), pl.BlockSpec((br, 128), m)]
    wspecs = [_full_spec((128, hw)), _full_spec((1, hw))] * 2
    s_shape = (nc, GLA_HEADS, HEAD_PAD, HEAD_PAD)
    pre_scratch = [pltpu.VMEM((br, hw), MXU_DTYPE)] * 3 + [pltpu.VMEM((nc, 1, hw), F32)]
    return pl.pallas_call(
        body, name="gla_fwd", grid=(nb,),
        in_specs=ins(fwd) + ins(bwd) + wspecs,
        out_specs=[pl.BlockSpec((br, hw), fwd), pl.BlockSpec((br, hw), bwd),
                   pl.BlockSpec(s_shape, lambda i: (i, 0, 0, 0)),
                   pl.BlockSpec(s_shape, lambda i: (nb - 1 - i, 0, 0, 0))],
        out_shape=[jax.ShapeDtypeStruct((L, hw), F32), jax.ShapeDtypeStruct((L, hw), F32),
                   jax.ShapeDtypeStruct((n_chunks,) + s_shape[1:], MXU_DTYPE),
                   jax.ShapeDtypeStruct((n_chunks,) + s_shape[1:], MXU_DTYPE)],
        scratch_shapes=[pltpu.VMEM(s_shape[1:], F32), pltpu.VMEM(s_shape[1:], F32), pre_scratch, pre_scratch],
        compiler_params=_params(("arbitrary",), VMEM_BIG),
    )(qa, ka, va, za, qa, ka, va, za, wgf, bgf, wgb, bgb)


def _gla_bwd_call(qa, ka, va, za, do, sf, sb, wgf, bgf, wgb, bgb, dep=None):
    L = qa.shape[0]
    br = min(512, L)
    nb, nc = L // br, br // GLA_CHUNK
    hw = GLA_HEADS * HEAD_PAD

    def body(qaf, kaf, vaf, zaf, dof, sf_r, qab, kab, vab, zab, dob, sb_r, wgf_r, bgf_r, wgb_r, bgb_r,
             dqf, dkf, dvf, dzf, dwf, dbf, dqb, dkb, dvb, dzb, dwb, dbb, gt_f, gt_b, pre_f, pre_b):
        @pl.when(pl.program_id(0) == 0)
        def _():
            for ref in (gt_f, gt_b, dwf, dbf, dwb, dbb):
                ref[...] = jnp.zeros_like(ref)

        _gla_block_pre(qaf, kaf, zaf, wgf_r, bgf_r, False, nc, *pre_f[:4], keep=pre_f[4:8])
        _gla_block_pre(qab, kab, zab, wgb_r, bgb_r, True, nc, *pre_b[:4], keep=pre_b[4:8])
        tri_f, tri_b = _tri_masks()
        row_w = lax.broadcasted_iota(jnp.int32, (GLA_CHUNK, HEAD_PAD), 0)

        def one(rev, pre, q_r, k_r, v_r, do_r, s_r, dq_r, dk_r, dv_r, gt, ci):
            qd_s, ki_s, ks_s, dec_s, _, eb_s, enb_s, elb_s, db_s = pre
            tri = tri_b if rev else tri_f
            last_row = 0 if rev else GLA_CHUNK - 1
            rows = pl.ds(pl.multiple_of(ci * GLA_CHUNK, GLA_CHUNK), GLA_CHUNK)
            dec = dec_s[ci]
            heads = range(GLA_HEADS)
            lanes = [slice(HEAD_PAD * h, HEAD_PAD * (h + 1)) for h in heads]
            qd = [qd_s[rows, sl] for sl in lanes]
            ki = [ki_s[rows, sl] for sl in lanes]
            ks = [ks_s[rows, sl] for sl in lanes]
            v = [v_r[rows, sl] for sl in lanes]
            do_h = [_mx(do_r[rows, sl]) for sl in lanes]
            s_t = [s_r[ci, h] for h in heads]
            g_t = [gt[h] for h in heads]
            g_m = [_mx(g_t[h]) for h in heads]
            a = [_dot_nt(qd[h], ki[h]) for h in heads]
            da = [_dot_nt(do_h[h], v[h]) for h in heads]
            dv_carried = [_dot_nt(ks[h], g_m[h]) for h in heads]
            dqd_carried = [_dot(do_h[h], _mx(s_t[h])) for h in heads]
            dks = [_dot(v[h], g_m[h]) for h in heads]
            g_grown = [_dot_tn(do_h[h], qd[h]) for h in heads]
            a = [_mx(jnp.where(tri, a[h], 0.0)) for h in heads]
            da = [_mx(jnp.where(tri, da[h], 0.0)) for h in heads]
            dv_inner = [_dot_tn(a[h], do_h[h]) for h in heads]
            dqd_inner = [_dot(da[h], ki[h]) for h in heads]
            dki = [_dot_tn(da[h], qd[h]) for h in heads]
            dq, dk = [], []
            for h in heads:
                sl = lanes[h]
                dv_r[rows, sl] = (dv_inner[h] + dv_carried[h]).astype(dv_r.dtype)
                ddec = jnp.sum(g_t[h] * s_t[h].astype(F32), axis=0, keepdims=True)
                gt[h] = g_t[h] * dec[:, sl] + g_grown[h]
                dq.append((dqd_inner[h] + dqd_carried[h]) * eb_s[rows, sl] * 0.125)
                dk_state = dks[h] * elb_s[rows, sl]
                dk.append(dki[h] * enb_s[rows, sl] + dk_state)
                k = k_r[rows, sl]
                dblast = jnp.sum(dk_state * k, axis=0, keepdims=True) + dec[:, sl] * ddec
                db_s[rows, sl] = q_r[rows, sl] * dq[h] - k * dk[h] + jnp.where(row_w == last_row, dblast, 0.0)
            low = _low_half(GLA_CHUNK)
            for pair in range(GLA_HEADS // 2):
                psl = slice(HEAD_PAD * pair, HEAD_PAD * (pair + 1))
                for ref, val in ((dq_r, dq), (dk_r, dk)):
                    both = jnp.where(low, val[2 * pair], pltpu.roll(val[2 * pair + 1], 64, 1))
                    ref[rows, psl] = both.astype(ref.dtype)

        def loop(t, carry):
            one(False, pre_f, qaf, kaf, vaf, dof, sf_r, dqf, dkf, dvf, gt_f, nc - 1 - t)
            one(True, pre_b, qab, kab, vab, dob, sb_r, dqb, dkb, dvb, gt_b, t)
            return carry

        lax.fori_loop(0, nc, loop, 0, unroll=True)

        def gate_grads(rev, pre, z_r, w_r, dz_r, dw_r, dbias_r):
            g_s, db_s = pre[4], pre[8]
            back_m = _mx((tri_f if rev else tri_b).astype(F32))
            db = db_s[...]
            dla = jnp.concatenate([_chunk_sums(back_m, db[GLA_CHUNK * c:GLA_CHUNK * (c + 1)]) for c in range(nc)],
                                  axis=0)
            dg = dla * (1.0 / GLA_GATE_NORM) * (1.0 / (1.0 + jnp.exp(g_s[...])))
            dg_m = _mx(dg)
            dz_r[...] = _dot_nt(dg_m, w_r[...])
            dw_r[...] += _dot_tn(_mx(z_r[...]), dg_m)
            dbias_r[...] += jnp.sum(dg, axis=0, keepdims=True)

        gate_grads(False, pre_f, zaf, wgf_r, dzf, dwf, dbf)
        gate_grads(True, pre_b, zab, wgb_r, dzb, dwb, dbb)

    last_first = lambda i: (nb - 1 - i, 0)
    first_last = lambda i: (i, 0)
    s_shape = (nc, GLA_HEADS, HEAD_PAD, HEAD_PAD)

    def ins(m):
        return [pl.BlockSpec((br, hw), m), pl.BlockSpec((br, hw), m), pl.BlockSpec((br, hw), m),
                pl.BlockSpec((br, 128), m), pl.BlockSpec((br, hw), m),
                pl.BlockSpec(s_shape, lambda i: m(i) + (0, 0))]

    def outs(m):
        return [pl.BlockSpec((br, hw // 2), m), pl.BlockSpec((br, hw // 2), m), pl.BlockSpec((br, hw), m),
                pl.BlockSpec((br, 128), m), _full_spec((128, hw)), _full_spec((1, hw))]

    out_shape = [jax.ShapeDtypeStruct((L, hw // 2), MXU_DTYPE)] * 2 + [
        jax.ShapeDtypeStruct((L, hw), MXU_DTYPE),
        jax.ShapeDtypeStruct((L, 128), F32), jax.ShapeDtypeStruct((128, hw), F32),
        jax.ShapeDtypeStruct((1, hw), F32)]
    wspecs = [_full_spec((128, hw)), _full_spec((1, hw))] * 2
    body, extra, extra_specs = _after(body, 16, dep)
    pre_scratch = ([pltpu.VMEM((br, hw), MXU_DTYPE)] * 3 + [pltpu.VMEM((nc, 1, hw), F32)]
                   + [pltpu.VMEM((br, hw), F32)] * 5)
    return pl.pallas_call(
        body, name="gla_bwd", grid=(nb,),
        in_specs=ins(last_first) + ins(first_last) + wspecs + extra_specs,
        out_specs=outs(last_first) + outs(first_last),
        out_shape=out_shape + out_shape,
        scratch_shapes=[pltpu.VMEM(s_shape[1:], F32), pltpu.VMEM(s_shape[1:], F32), pre_scratch, pre_scratch],
        compiler_params=_params(("arbitrary",), VMEM_BIG),
    )(qa, ka, va, za, do, sf, qa, ka, va, za, do, sb, wgf, bgf, wgb, bgb, *extra)


def _t5_buckets(rel):
    nb = REL_BUCKETS // 2
    ret = (rel > 0).astype(np.int32) * nb
    n = np.abs(rel)
    max_exact = nb // 2
    large = max_exact + (np.log(np.maximum(n, 1).astype(np.float32) / max_exact)
                         / math.log(REL_MAX_DIST / max_exact) * (nb - max_exact)).astype(np.int32)
    large = np.minimum(large, nb - 1)
    return ret + np.where(n < max_exact, n, large)


SWA_GROUP = SWA_Q_HEADS // SWA_KV_HEADS
SWA_SPAN = 3 * SWA_BLOCK
SWA_GROUP_LANES = SWA_GROUP * SWA_BLOCK


def _band_buckets():
    s = np.arange(SWA_SPAN)[:, None]
    c = np.arange(SWA_BLOCK)[None, :]
    return _t5_buckets(s - SWA_BLOCK - c).astype(np.int32)


def _swa_valid(n, seq_len):
    key_pos = (n - 1) * SWA_BLOCK + lax.broadcasted_iota(jnp.int32, (SWA_SPAN, 1), 0)
    return (key_pos >= 0) & (key_pos < seq_len)


def _swa_sink_row(sink_r, kv):
    lane = lax.broadcasted_iota(jnp.int32, (1, SWA_GROUP_LANES), 1)
    row = jnp.full((1, SWA_GROUP_LANES), sink_r[kv * SWA_GROUP], F32)
    for g in range(1, SWA_GROUP):
        row = jnp.where(lane >= g * SWA_BLOCK, sink_r[kv * SWA_GROUP + g], row)
    return row


SWA_STEP_BLOCKS = 4


def _swa_group(ref, kv, rows):
    first = kv * SWA_GROUP
    return jnp.concatenate([ref[rows, HEAD_PAD * h:HEAD_PAD * (h + 1)] for h in range(first, first + SWA_GROUP)],
                           axis=0)


def _swa_softmax(scores, bias_t, sink_row, valid):
    st = jnp.where(valid, scores + bias_t, -1e30)
    m = jnp.maximum(jnp.max(st, axis=0, keepdims=True), sink_row)
    p = jnp.exp(st - m)
    e_sink = jnp.exp(sink_row - m)
    inv = 1.0 / (jnp.sum(p, axis=0, keepdims=True) + e_sink)
    return p * inv, e_sink * inv


def _swa_fwd_call(qs, ks, vs, bias, sink, dep=None):
    L = qs.shape[0]

    def block(n, rows, q_r, k_r, v_r, bias_r, sink_r, o_r):
        span = pl.ds(pl.multiple_of(n * SWA_BLOCK, SWA_BLOCK), SWA_SPAN)
        valid = _swa_valid(n, L)
        groups = range(SWA_KV_HEADS)
        lanes = [slice(HEAD_PAD * kv, HEAD_PAD * (kv + 1)) for kv in groups]
        scores = [_dot_nt(k_r[span, lanes[kv]], _swa_group(q_r, kv, rows)) for kv in groups]
        probs = [_swa_softmax(scores[kv], bias_r[kv], _swa_sink_row(sink_r, kv), valid)[0] for kv in groups]
        low = _low_half(SWA_BLOCK)
        for kv in groups:
            og = _dot_tn(_mx(probs[kv]), v_r[span, lanes[kv]])
            for pair in range(SWA_GROUP // 2):
                even = og[2 * SWA_BLOCK * pair:2 * SWA_BLOCK * pair + SWA_BLOCK]
                odd = og[2 * SWA_BLOCK * pair + SWA_BLOCK:2 * SWA_BLOCK * (pair + 1)]
                first = HEAD_PAD * (kv * SWA_GROUP // 2 + pair)
                o_r[rows, first:first + HEAD_PAD] = jnp.where(low, even, pltpu.roll(odd, 64, 1)).astype(o_r.dtype)

    def body(*refs):
        for j in range(SWA_STEP_BLOCKS):
            block(SWA_STEP_BLOCKS * pl.program_id(0) + j, slice(SWA_BLOCK * j, SWA_BLOCK * (j + 1)), *refs)

    qw = SWA_Q_HEADS * HEAD_PAD
    tm = SWA_STEP_BLOCKS * SWA_BLOCK
    body, extra, extra_specs = _after(body, 5, dep)
    return pl.pallas_call(
        body, name="swa_fwd", grid=(L // tm,),
        in_specs=[_row_spec(tm, qw), _vmem_spec(), _vmem_spec(), _vmem_spec(),
                  pl.BlockSpec(memory_space=pltpu.SMEM)] + extra_specs,
        out_specs=_row_spec(tm, qw // 2),
        out_shape=jax.ShapeDtypeStruct((L, qw // 2), MXU_DTYPE),
        compiler_params=_params(("arbitrary",), VMEM_BIG),
    )(qs, ks, vs, bias, sink, *extra)


def _swa_bwd_call(qs, ks, vs, bias, sink, do, dep=None):
    L = qs.shape[0]
    qw = SWA_Q_HEADS * HEAD_PAD
    kw = SWA_KV_HEADS * HEAD_PAD

    def body(*refs):
        dk_r, dv_r, dbias_r, dsink_r = refs[7:]

        @pl.when(pl.program_id(0) == 0)
        def _():
            for ref in (dk_r, dv_r, dbias_r, dsink_r):
                ref[...] = jnp.zeros_like(ref)

        for j in range(SWA_STEP_BLOCKS):
            block(SWA_STEP_BLOCKS * pl.program_id(0) + j, slice(SWA_BLOCK * j, SWA_BLOCK * (j + 1)), *refs)

    def block(n, rows, q_r, k_r, v_r, bias_r, sink_r, do_r, dq_r, dk_r, dv_r, dbias_r, dsink_r):
        span = pl.ds(pl.multiple_of(n * SWA_BLOCK, SWA_BLOCK), SWA_SPAN)
        valid = _swa_valid(n, L)
        groups = range(SWA_KV_HEADS)
        lanes = [slice(HEAD_PAD * kv, HEAD_PAD * (kv + 1)) for kv in groups]
        kk = [k_r[span, sl] for sl in lanes]
        vv = [v_r[span, sl] for sl in lanes]
        qg = [_swa_group(q_r, kv, rows) for kv in groups]
        dog = [_swa_group(do_r, kv, rows) for kv in groups]
        scores = [_dot_nt(kk[kv], qg[kv]) for kv in groups]
        dp = [_dot_nt(vv[kv], dog[kv]) for kv in groups]
        probs = [_swa_softmax(scores[kv], bias_r[kv], _swa_sink_row(sink_r, kv), valid) for kv in groups]
        ds_m, pn_m = [], []
        for kv in groups:
            pn, p_sink = probs[kv]
            delta = jnp.sum(pn * dp[kv], axis=0, keepdims=True)
            ds = pn * (dp[kv] - delta)
            dsink_r[kv] -= p_sink * delta
            dbias_r[kv] += ds
            ds_m.append(_mx(ds))
            pn_m.append(_mx(pn))
        dqg = [_dot_tn(ds_m[kv], kk[kv]) * 0.125 for kv in groups]
        dkk = [_dot(ds_m[kv], qg[kv]) for kv in groups]
        dvv = [_dot(pn_m[kv], dog[kv]) for kv in groups]
        low = _low_half(SWA_BLOCK)
        for kv in groups:
            for pair in range(SWA_GROUP // 2):
                even = dqg[kv][2 * SWA_BLOCK * pair:2 * SWA_BLOCK * pair + SWA_BLOCK]
                odd = dqg[kv][2 * SWA_BLOCK * pair + SWA_BLOCK:2 * SWA_BLOCK * (pair + 1)]
                first = HEAD_PAD * (kv * SWA_GROUP // 2 + pair)
                dq_r[rows, first:first + HEAD_PAD] = jnp.where(low, even, pltpu.roll(odd, 64, 1)).astype(dq_r.dtype)
            dk_r[span, lanes[kv]] += dkk[kv]
            dv_r[span, lanes[kv]] += dvv[kv]

    tm = SWA_STEP_BLOCKS * SWA_BLOCK
    body, extra, extra_specs = _after(body, 6, dep)
    return pl.pallas_call(
        body, name="swa_bwd", grid=(L // tm,),
        in_specs=[_row_spec(tm, qw), _vmem_spec(), _vmem_spec(), _vmem_spec(),
                  pl.BlockSpec(memory_space=pltpu.SMEM), _row_spec(tm, qw)] + extra_specs,
        out_specs=[_row_spec(tm, qw // 2), _vmem_spec(), _vmem_spec(), _vmem_spec(), _vmem_spec()],
        out_shape=[jax.ShapeDtypeStruct((L, qw // 2), MXU_DTYPE),
                   jax.ShapeDtypeStruct((L + 2 * SWA_BLOCK, kw), F32),
                   jax.ShapeDtypeStruct((L + 2 * SWA_BLOCK, kw), F32),
                   jax.ShapeDtypeStruct((SWA_KV_HEADS, SWA_SPAN, SWA_GROUP_LANES), F32),
                   jax.ShapeDtypeStruct((SWA_KV_HEADS, 1, SWA_GROUP_LANES), F32)],
        compiler_params=_params(("arbitrary",), VMEM_BIG),
    )(qs, ks, vs, bias, sink, do, *extra)


def _bias_call(rel_bias, buckets, dep=None):
    def body(t_r, bk_r, o_r):
        bk = bk_r[...]
        s = lax.broadcasted_iota(jnp.int32, bk.shape, 0)
        c = lax.broadcasted_iota(jnp.int32, bk.shape, 1)
        in_band = jnp.abs(s - SWA_BLOCK - c) <= SWA_BLOCK
        for h in range(SWA_Q_HEADS):
            acc = jnp.zeros(bk.shape, F32)
            for b in range(REL_BUCKETS):
                acc = jnp.where(bk == b, t_r[b, h], acc)
            g = h % SWA_GROUP
            o_r[h // SWA_GROUP, :, SWA_BLOCK * g:SWA_BLOCK * (g + 1)] = jnp.where(in_band, acc, -1e30)

    body, extra, extra_specs = _after(body, 2, dep)
    return pl.pallas_call(
        body, name="band_bias",
        in_specs=[pl.BlockSpec(memory_space=pltpu.SMEM), _vmem_spec()] + extra_specs, out_specs=_vmem_spec(),
        out_shape=jax.ShapeDtypeStruct((SWA_KV_HEADS, SWA_SPAN, SWA_GROUP_LANES), F32),
    )(rel_bias, buckets, *extra)


def _relbias_call(dbias, dsink, buckets, dep=None):
    def body(db_r, ds_r, bk_r, o_r, os_r):
        bk = bk_r[...]
        rowi = lax.broadcasted_iota(jnp.int32, (REL_BUCKETS, 128), 0)
        lanei = lax.broadcasted_iota(jnp.int32, (REL_BUCKETS, 128), 1)
        lane1 = lax.broadcasted_iota(jnp.int32, (1, 128), 1)
        acc = jnp.zeros((REL_BUCKETS, 128), F32)
        acc_sink = jnp.zeros((1, 128), F32)
        for h in range(SWA_Q_HEADS):
            kv, g = h // SWA_GROUP, h % SWA_GROUP
            lanes = slice(SWA_BLOCK * g, SWA_BLOCK * (g + 1))
            part = db_r[kv, :, lanes]
            for b in range(REL_BUCKETS):
                s = jnp.sum(jnp.where(bk == b, part, 0.0))
                acc = acc + jnp.where((rowi == b) & (lanei == h), s, 0.0)
            acc_sink = acc_sink + jnp.where(lane1 == h, jnp.sum(ds_r[kv, :, lanes]), 0.0)
        o_r[...] = acc
        os_r[...] = acc_sink

    body, extra, extra_specs = _after(body, 3, dep)
    return pl.pallas_call(
        body, name="relbias_grad",
        in_specs=[_vmem_spec()] * 3 + extra_specs, out_specs=[_vmem_spec()] * 2,
        out_shape=[jax.ShapeDtypeStruct((REL_BUCKETS, 128), F32), jax.ShapeDtypeStruct((1, 128), F32)],
    )(dbias, dsink, buckets, *extra)


def _mix_call(o_f, o_b, ga, o_s, x, gn, w_out_p, g_post, g_pre2, dep=None):
    L = x.shape[0]
    tm = min(512, L)
    hw = GLA_HEADS * HEAD_PAD

    def body(of_r, ob_r, ga_r, os_r, x_r, gn_r, w_r, gp_r, g2_r, cat_r, mix_r, h1_r, n2_r):
        gn_v = gn_r[...]
        for h in range(GLA_HEADS):
            sl = slice(HEAD_PAD * h, HEAD_PAD * (h + 1))
            oh = of_r[:, sl] + ob_r[:, sl]
            on = oh * _rms_r(oh) * gn_v
            gate = ga_r[:, sl]
            cat_r[:, sl] = (on * (gate * jax.nn.sigmoid(gate))).astype(cat_r.dtype)
        os_v = os_r[...]
        cat_r[:, hw:] = os_v
        mix = _dot(cat_r[:, :hw], w_r[:hw, :]) + _dot(os_v, w_r[hw:, :])
        mix_r[...] = mix
        h1 = x_r[...] + mix * _rms_r(mix) * gp_r[...]
        h1_r[...] = h1
        n2_r[...] = (h1 * _rms_r(h1) * g2_r[...]).astype(n2_r.dtype)

    body, extra, extra_specs = _after(body, 9, dep)
    return pl.pallas_call(
        body, name="mix_fwd", grid=(L // tm,),
        in_specs=[_row_spec(tm, hw), _row_spec(tm, hw), _row_spec(tm, hw), _row_spec(tm, OUT_PAD - hw),
                  _row_spec(tm, D_MODEL), _full_spec((1, HEAD_PAD)), _vmem_spec(),
                  _full_spec((1, D_MODEL)), _full_spec((1, D_MODEL))] + extra_specs,
        out_specs=[_row_spec(tm, OUT_PAD), _row_spec(tm, D_MODEL), _row_spec(tm, D_MODEL), _row_spec(tm, D_MODEL)],
        out_shape=[jax.ShapeDtypeStruct((L, OUT_PAD), MXU_DTYPE), jax.ShapeDtypeStruct((L, D_MODEL), F32),
                   jax.ShapeDtypeStruct((L, D_MODEL), F32), jax.ShapeDtypeStruct((L, D_MODEL), MXU_DTYPE)],
        compiler_params=_params(("arbitrary",), VMEM_BIG),
    )(o_f, o_b, ga, o_s, x, gn, w_out_p, g_post, g_pre2, *extra)


def _mlp_fwd_call(n2, h1, tgt, w_ud, g_post):
    L = n2.shape[0]
    tm = min(512, L)
    blk = D_FF // N_CHIPS

    def body(n2_r, h1_r, t_r, w_r, g_r, a_r, rz_r, dh2_r, dff_r, loss_r, dg_r):
        @pl.when(pl.program_id(0) == 0)
        def _():
            loss_r[...] = jnp.zeros_like(loss_r)
            dg_r[...] = jnp.zeros_like(dg_r)

        n2v = n2_r[...]
        ff = jnp.zeros((tm, D_MODEL), F32)
        for j in range(N_CHIPS):
            sl = slice(blk * j, blk * (j + 1))
            rz = jnp.maximum(_dot(n2v, w_r[j, 0]), 0.0)
            a = _mx(rz * rz)
            rz_r[:, sl] = rz.astype(rz_r.dtype)
            a_r[:, sl] = a
            ff = ff + _dot(a, w_r[j, 1])
        g = g_r[...]
        r = _rms_r(ff)
        err = h1_r[...] + ff * r * g - t_r[...]
        loss_r[...] += 0.5 * jnp.sum(err * err) / D_MODEL
        dh2 = err * (1.0 / D_MODEL)
        dh2_r[...] = dh2
        dff, dg = _rms_bwd(ff, r, g, dh2)
        dff_r[...] = dff.astype(dff_r.dtype)
        dg_r[...] += dg

    return pl.pallas_call(
        body, name="mlp_fwd", grid=(L // tm,),
        in_specs=[_row_spec(tm, D_MODEL), _row_spec(tm, D_MODEL), _row_spec(tm, D_MODEL),
                  _vmem_spec(), _full_spec((1, D_MODEL))],
        out_specs=[_row_spec(tm, D_FF), _row_spec(tm, D_FF), _row_spec(tm, D_MODEL), _row_spec(tm, D_MODEL),
                   _full_spec((1, 128)), _full_spec((1, D_MODEL))],
        out_shape=[jax.ShapeDtypeStruct((L, D_FF), MXU_DTYPE), jax.ShapeDtypeStruct((L, D_FF), MXU_DTYPE),
                   jax.ShapeDtypeStruct((L, D_MODEL), F32), jax.ShapeDtypeStruct((L, D_MODEL), MXU_DTYPE),
                   jax.ShapeDtypeStruct((1, 128), F32), jax.ShapeDtypeStruct((1, D_MODEL), F32)],
        compiler_params=_params(("arbitrary",), VMEM_BIG),
    )(n2, h1, tgt, w_ud, g_post)


def _mlp_bwd_call(dff, rz, w_ud):
    L = dff.shape[0]
    tm = min(512, L)
    blk = D_FF // N_CHIPS

    def body(dff_r, rz_r, w_r, dz_r, dn2_r):
        dffv = dff_r[...]
        dn2 = jnp.zeros((tm, D_MODEL), F32)
        for j in range(N_CHIPS):
            sl = slice(blk * j, blk * (j + 1))
            dz = _mx(_dot_nt(dffv, w_r[j, 1]) * 2.0 * rz_r[:, sl].astype(F32))
            dz_r[:, sl] = dz
            dn2 = dn2 + _dot_nt(dz, w_r[j, 0])
        dn2_r[...] = dn2

    return pl.pallas_call(
        body, name="mlp_bwd", grid=(L // tm,),
        in_specs=[_row_spec(tm, D_MODEL), _row_spec(tm, D_FF), _vmem_spec()],
        out_specs=[_row_spec(tm, D_FF), _row_spec(tm, D_MODEL)],
        out_shape=[jax.ShapeDtypeStruct((L, D_FF), MXU_DTYPE), jax.ShapeDtypeStruct((L, D_MODEL), F32)],
        compiler_params=_params(("arbitrary",), VMEM_BIG),
    )(dff, rz, w_ud)


def _mlp_wgrad_call(a, dff, n2, dz):
    L = a.shape[0]
    tf = 512
    per = (D_FF // N_CHIPS) // tf

    def body(a_r, dff_r, n2_r, dz_r, dwd_r, dwu_r):
        dwd_r[...] = _dot_tn(a_r[...], dff_r[...])
        dwu_r[...] = _dot_tn(n2_r[...], dz_r[...])

    return pl.pallas_call(
        body, name="mlp_wgrad", grid=(D_FF // tf,),
        in_specs=[pl.BlockSpec((L, tf), lambda j: (0, j)), _vmem_spec(), _vmem_spec(),
                  pl.BlockSpec((L, tf), lambda j: (0, j))],
        out_specs=[pl.BlockSpec((tf, D_MODEL), lambda j: (j, 0)),
                   pl.BlockSpec((None, D_MODEL, tf), lambda j: (j // per, 0, j % per))],
        out_shape=[jax.ShapeDtypeStruct((D_FF, D_MODEL), F32),
                   jax.ShapeDtypeStruct((N_CHIPS, D_MODEL, D_FF // N_CHIPS), F32)],
        compiler_params=_params(("arbitrary",), VMEM_BIG),
    )(a, dff, n2, dz)


def _mix_bwd_call(dn2, dh2, h1, mix, cat, o_f, o_b, ga, gn, g_post, g_pre2, w_out_p):
    L = dn2.shape[0]
    tm = min(512, L)
    hw = GLA_HEADS * HEAD_PAD

    def body(dn2_r, dh2_r, h1_r, mix_r, cat_r, of_r, ob_r, ga_r, gn_r, gp_r, g2_r, w_r,
             dh1_r, do_r, dga_r, dos_r, dw_r, dg2_r, dgp_r, dgn_r):
        @pl.when(pl.program_id(0) == 0)
        def _():
            for ref in (dw_r, dg2_r, dgp_r, dgn_r):
                ref[...] = jnp.zeros_like(ref)

        parts = [slice(start, start + min(256, tm)) for start in range(0, tm, 256)]
        dmix_m = []
        for rs in parts:
            h1 = h1_r[rs, :]
            dx2, dg2 = _rms_bwd(h1, _rms_r(h1), g2_r[...], dn2_r[rs, :])
            dh1 = dh2_r[rs, :] + dx2
            dh1_r[rs, :] = dh1
            dg2_r[...] += dg2
            mix = mix_r[rs, :]
            dmix, dgp = _rms_bwd(mix, _rms_r(mix), gp_r[...], dh1)
            dgp_r[...] += dgp
            dmix_m.append(_mx(dmix))
        dcat = [_dot_nt(d, w_r[...]) for d in dmix_m]
        for rs, d in zip(parts, dmix_m):
            dw_r[...] += _dot_tn(cat_r[rs, :], d)
        gn_v = gn_r[...]
        dgn = jnp.zeros((1, HEAD_PAD), F32)
        for rs, dc in zip(parts, dcat):
            dos_r[rs, :] = _spread_heads(dc[:, hw:]).astype(dos_r.dtype)
            for h in range(GLA_HEADS):
                sl = slice(HEAD_PAD * h, HEAD_PAD * (h + 1))
                oh = of_r[rs, sl] + ob_r[rs, sl]
                rr = _rms_r(oh)
                xh = oh * rr
                gate = ga_r[rs, sl]
                sg = jax.nn.sigmoid(gate)
                silu = gate * sg
                doa = dc[:, sl]
                dga_r[rs, sl] = (doa * (xh * gn_v) * (sg + silu * (1.0 - sg))).astype(dga_r.dtype)
                don = doa * silu
                gd = don * gn_v
                do_r[rs, sl] = rr * (gd - xh * jnp.mean(gd * xh, axis=-1, keepdims=True))
                dgn = dgn + jnp.sum(don * xh, axis=0, keepdims=True)
        dgn_r[...] += dgn

    return pl.pallas_call(
        body, name="mix_bwd", grid=(L // tm,),
        in_specs=[_row_spec(tm, D_MODEL)] * 4 + [_row_spec(tm, OUT_PAD)] + [_row_spec(tm, hw)] * 3
        + [_full_spec((1, HEAD_PAD)), _full_spec((1, D_MODEL)), _full_spec((1, D_MODEL)), _vmem_spec()],
        out_specs=[_row_spec(tm, D_MODEL), _row_spec(tm, hw), _row_spec(tm, hw),
                   _row_spec(tm, SWA_Q_HEADS * HEAD_PAD),
                   _full_spec((OUT_PAD, D_MODEL)), _full_spec((1, D_MODEL)), _full_spec((1, D_MODEL)),
                   _full_spec((1, HEAD_PAD))],
        out_shape=[jax.ShapeDtypeStruct((L, D_MODEL), F32), jax.ShapeDtypeStruct((L, hw), F32),
                   jax.ShapeDtypeStruct((L, hw), MXU_DTYPE),
                   jax.ShapeDtypeStruct((L, SWA_Q_HEADS * HEAD_PAD), MXU_DTYPE),
                   jax.ShapeDtypeStruct((OUT_PAD, D_MODEL), F32), jax.ShapeDtypeStruct((1, D_MODEL), F32),
                   jax.ShapeDtypeStruct((1, D_MODEL), F32), jax.ShapeDtypeStruct((1, HEAD_PAD), F32)],
        compiler_params=_params(("arbitrary",), VMEM_BIG),
    )(dn2, dh2, h1, mix, cat, o_f, o_b, ga, gn, g_post, g_pre2, w_out_p)


def _in_bwd_call(x, dh1, g_pre, w_in_t, pairs, singles, halos, dep=None):
    L = x.shape[0]
    tm = min(512, L)
    per = tm // SWA_BLOCK
    n_pair, n_single, n_halo = len(pairs), len(singles), len(halos)
    groups = [c for c, _ in pairs] + [c for c, _ in singles] + [c for c, _ in halos]

    def body(*refs):
        x_r, dh1_r, g_r, w_r = refs[:4]
        pair_refs = refs[4:4 + 2 * n_pair]
        single_refs = refs[4 + 2 * n_pair:4 + 2 * n_pair + n_single]
        halo_refs = refs[4 + 2 * n_pair + n_single:4 + 2 * n_pair + n_single + per * n_halo]
        dx_r, dw_r, dg_r = refs[4 + 2 * n_pair + n_single + per * n_halo:]

        @pl.when(pl.program_id(0) == 0)
        def _():
            dw_r[...] = jnp.zeros_like(dw_r)
            dg_r[...] = jnp.zeros_like(dg_r)

        xv = x_r[...]
        r = _rms_r(xv)
        g = g_r[...]
        u = _mx(xv * r * g)
        vals = [pair_refs[2 * i][...].astype(F32) + pair_refs[2 * i + 1][...].astype(F32) for i in range(n_pair)]
        vals += [ref[...].astype(F32) for ref in single_refs]
        vals += [jnp.concatenate([ref[...] for ref in halo_refs[per * i:per * (i + 1)]], axis=0)
                 for i in range(n_halo)]
        ds = [_mx(_squeeze_heads(val) if heads else val) for (_, _, heads), val in zip(groups, vals)]
        du = jnp.zeros((tm, D_MODEL), F32)
        for (first, rows, _), d in zip(groups, ds):
            du = du + _dot(d, w_r[first:first + rows, :])
        for (first, rows, _), d in zip(groups, ds):
            dw_r[first:first + rows, :] += _dot_tn(d, u)
        dx, dg = _rms_bwd(xv, r, g, du)
        dx_r[...] = dh1_r[...] + dx
        dg_r[...] += dg

    arrays = [a for _, pr in pairs for a in pr] + [a for _, a in singles]
    specs = [_row_spec(tm, a.shape[1]) for a in arrays]
    for _, a in halos:
        specs += [pl.BlockSpec((SWA_BLOCK, a.shape[1]), lambda i, j=j: (per * i + 1 + j, 0)) for j in range(per)]
        arrays += [a] * per
    body, extra, extra_specs = _after(body, 4 + len(arrays), dep)
    return pl.pallas_call(
        body, name="in_bwd", grid=(L // tm,),
        in_specs=[_row_spec(tm, D_MODEL), _row_spec(tm, D_MODEL), _full_spec((1, D_MODEL)), _vmem_spec()] + specs
        + extra_specs,
        out_specs=[_row_spec(tm, D_MODEL), _full_spec((IN_COLS, D_MODEL)), _full_spec((1, D_MODEL))],
        out_shape=[jax.ShapeDtypeStruct((L, D_MODEL), F32), jax.ShapeDtypeStruct((IN_COLS, D_MODEL), F32),
                   jax.ShapeDtypeStruct((1, D_MODEL), F32)],
        compiler_params=_params(("arbitrary",), VMEM_BIG),
    )(x, dh1, g_pre, w_in_t, *arrays, *extra)


def _adamw_math(w, g, m, v):
    m = ADAM_B1 * m + (1.0 - ADAM_B1) * g
    v = ADAM_B2 * v + (1.0 - ADAM_B2) * (g * g)
    m_hat = m / (1.0 - ADAM_B1 ** ADAM_STEP)
    v_hat = v / (1.0 - ADAM_B2 ** ADAM_STEP)
    delta = -ADAM_LR * (m_hat / (jnp.sqrt(v_hat) + ADAM_EPS) + ADAM_WD * w)
    return delta, m, v


def _adamw_call(w, g, m, v, name, dep=None):
    rows, cols = w.shape
    tr = min(256, rows)

    def body(w_r, g_r, m_r, v_r, d_r, nm_r, nv_r):
        d_r[...], nm_r[...], nv_r[...] = _adamw_math(w_r[...], g_r[...], m_r[...], v_r[...])

    if rows % tr == 0:
        spec, steps = _row_spec(tr, cols), rows // tr
    else:
        spec, steps = pl.BlockSpec((rows, 256), lambda i: (0, i)), cols // 256
    body, extra, extra_specs = _after(body, 4, dep)
    return pl.pallas_call(
        body, name=name, grid=(steps,),
        in_specs=[spec] * 4 + extra_specs, out_specs=[spec] * 3,
        out_shape=[jax.ShapeDtypeStruct(w.shape, F32)] * 3,
        compiler_params=_params(("arbitrary",)),
    )(w, g, m, v, *extra)


def _position():
    return lax.axis_index("x"), lax.axis_index("y"), lax.axis_index("c")


def _other_chips(x, y):
    return [(1 - x, y), (x, 1 - y), (1 - x, 1 - y)]


ROWS, COLS = -2, -1


def _half(ref, which, axis):
    size = ref.shape[axis] // 2
    span = pl.ds(pl.multiple_of(which * size, 16 if axis == ROWS else 128), size)
    index = [slice(None)] * len(ref.shape)
    index[axis] = span
    return ref.at[tuple(index)]


def _quarter(ref, half, which, axis):
    size = ref.shape[axis] // 4
    span = pl.ds(pl.multiple_of((2 * half + which) * size, 16 if axis == ROWS else 128), size)
    index = [slice(None)] * len(ref.shape)
    index[axis] = span
    return ref.at[tuple(index)]


def _first_gather_call(shards, axes, routed):
    n = len(shards)
    per = 7

    def body(*refs):
        srcs, outs = refs[:n], refs[n:2 * n]
        send_sems, recv_sems, local_sems = refs[2 * n:]
        x, y, c = _position()
        me, sibling = (x, y, c), (x, y, 1 - c)
        x_side, y_side, across = _other_chips(x, y)
        local = [pltpu.make_async_copy(srcs[a], outs[a].at[2 * x + y], local_sems.at[a]) for a in range(n)]
        for cp in local:
            cp.start()

        def copy(a, k, dst, to, src=None):
            return pltpu.make_async_remote_copy(
                src_ref=dst if src is None else src, dst_ref=dst, send_sem=send_sems.at[per * a + k],
                recv_sem=recv_sems.at[per * a + k], device_id=to, device_id_type=MESH_ID)

        def half(a, chip, pc):
            return _half(outs[a].at[2 * chip[0] + chip[1]], pc, axes[a])

        def quarter(a, chip, q):
            return _quarter(outs[a].at[2 * chip[0] + chip[1]], c, q, axes[a])

        sends = []
        for a in range(n):
            mine = _half(srcs[a], c, axes[a])
            targets = (x_side, y_side) if routed[a] else (x_side, y_side, across)
            sends += [copy(a, j, half(a, (x, y), c), (*chip, c), src=mine) for j, chip in enumerate(targets)]
        for cp in sends:
            cp.start()
        for a in range(n):
            for j, chip in enumerate((x_side, y_side)):
                copy(a, j, half(a, chip, c), me).wait_recv()
                if routed[a]:
                    other = (y_side, x_side)[j]
                    sends.append(copy(a, 2 + j, quarter(a, chip, j), (*other, c)))
                    sends[-1].start()
                sends.append(copy(a, 4 + j, half(a, chip, c), sibling))
                sends[-1].start()
        for a in range(n):
            if routed[a]:
                for j in range(2):
                    copy(a, 2 + j, quarter(a, across, j), me).wait_recv()
            else:
                copy(a, 2, half(a, across, c), me).wait_recv()
            sends.append(copy(a, 6, half(a, across, c), sibling))
            sends[-1].start()
        for a in range(n):
            for k, chip in ((4, x_side), (5, y_side), (6, across)):
                copy(a, k, half(a, chip, 1 - c), me).wait_recv()
        for cp in sends:
            cp.wait_send()
        for cp in local:
            cp.wait()

    return pl.pallas_call(
        body, name="first_gather",
        in_specs=[_any_spec()] * n, out_specs=[_any_spec()] * n,
        out_shape=[jax.ShapeDtypeStruct((N_CHIPS,) + s.shape, s.dtype) for s in shards],
        scratch_shapes=[pltpu.SemaphoreType.DMA((per * n,)), pltpu.SemaphoreType.DMA((per * n,)),
                        pltpu.SemaphoreType.DMA((n,))],
    )(*shards)


def _split_start(name, arrays, n_copies, plan):
    n = len(arrays)

    def body(*refs):
        ins, send_sems, recv_sems, token = refs[:n], refs[n], refs[n + 1], refs[-1]
        for k, (src, dst, to, _) in enumerate(plan(ins)):
            pltpu.make_async_remote_copy(src_ref=src, dst_ref=dst, send_sem=send_sems.at[k],
                                         recv_sem=recv_sems.at[k], device_id=to, device_id_type=MESH_ID).start()
        token[...] = jnp.zeros_like(token)

    hbm = pl.BlockSpec(memory_space=pltpu.HBM)
    sem = pl.BlockSpec(memory_space=pltpu.SEMAPHORE)
    out = pl.pallas_call(
        body, name=name,
        out_shape=(pltpu.SemaphoreType.DMA((n_copies,)), pltpu.SemaphoreType.DMA((n_copies,)))
        + tuple(pltpu.HBM(a.shape, a.dtype) for a in arrays) + (jax.ShapeDtypeStruct((8, 128), F32),),
        in_specs=[hbm] * n, out_specs=(sem, sem) + (hbm,) * n + (_vmem_spec(),),
        input_output_aliases={i: 2 + i for i in range(n)},
        compiler_params=pltpu.CompilerParams(has_side_effects=pltpu.SideEffectType.DATAFLOW_SIDE_EFFECTING),
    )(*[pltpu.with_memory_space_constraint(a, pltpu.HBM) for a in arrays])
    return (out[0], out[1], tuple(out[2:2 + n])), out[-1]


def _split_wait(name, handle, n_copies, plan, after):
    send_sems, recv_sems, arrays = handle
    n = len(arrays)

    def body(*refs):
        ins, s_sems, r_sems = refs[:n], refs[n], refs[n + 1]
        for k, (src, dst, to, landed) in enumerate(plan(ins)):
            cp = pltpu.make_async_remote_copy(src_ref=src, dst_ref=landed, send_sem=s_sems.at[k],
                                              recv_sem=r_sems.at[k], device_id=to, device_id_type=MESH_ID)
            cp.wait_send()
            cp.wait_recv()

    hbm = pl.BlockSpec(memory_space=pltpu.HBM)
    sem = pl.BlockSpec(memory_space=pltpu.SEMAPHORE)
    out = pl.pallas_call(
        body, name=name,
        out_shape=tuple(pltpu.HBM(a.shape, a.dtype) for a in arrays),
        in_specs=[hbm] * n + [sem, sem, _any_spec()], out_specs=(hbm,) * n,
        input_output_aliases={i: i for i in range(n)},
        compiler_params=pltpu.CompilerParams(has_side_effects=pltpu.SideEffectType.DATAFLOW_SIDE_EFFECTING),
    )(*arrays, send_sems, recv_sems, after)
    return tuple(out)


def _gather_plans(axes):
    n = len(axes)

    def stage_one(refs):
        x, y, c = _position()
        copies = []
        for a, axis in enumerate(axes):
            for px, py in _other_chips(x, y):
                copies.append((_half(refs[a], c, axis), _half(refs[n + a].at[2 * x + y], c, axis),
                               (px, py, c), _half(refs[n + a].at[2 * px + py], c, axis)))
        return copies

    def stage_two(refs):
        x, y, c = _position()
        copies = []
        for a, axis in enumerate(axes):
            for px, py in _other_chips(x, y):
                piece = _half(refs[n + a].at[2 * px + py], c, axis)
                copies.append((piece, piece, (x, y, 1 - c), _half(refs[n + a].at[2 * px + py], 1 - c, axis)))
        return copies

    return stage_one, stage_two


def _pair_swap_plan(axes):
    n = len(axes)

    def plan(refs):
        x, y, c = _position()
        return [(_half(refs[a], 1 - c, axes[a]), refs[n + a], (x, y, 1 - c), refs[n + a]) for a in range(n)]

    return plan


def _chip_swap_plan(n):
    def plan(refs):
        x, y, c = _position()
        copies = []
        for a in range(n):
            for j, (px, py) in enumerate(_other_chips(x, y)):
                copies.append((refs[a].at[2 * px + py], refs[n + a].at[j], (px, py, c), refs[n + a].at[j]))
        return copies

    return plan


def _pair_join_plan(axes):
    def plan(refs):
        x, y, c = _position()
        copies = []
        for a, axis in enumerate(axes):
            mine = _half(refs[a], c, axis)
            copies.append((mine, mine, (x, y, 1 - c), _half(refs[a], 1 - c, axis)))
        return copies

    return plan


def _pair_add_call(gs, gots, pos, name, axes):
    n = len(gs)

    def body(pos_r, *refs):
        for g_r, got_r, o_r in zip(refs[:n], refs[n:2 * n], refs[2 * n:]):
            o_r[...] = (g_r[...] + got_r[...]).astype(o_r.dtype)

    def mine(axis):
        return (lambda j, p: (j, p[1], 0)) if axis == ROWS else (lambda j, p: (j, 0, p[1]))

    blocks = [(None,) + got.shape[1:] for got in gots]
    return pl.pallas_call(
        body, name=name,
        grid_spec=pltpu.PrefetchScalarGridSpec(
            num_scalar_prefetch=1, grid=(N_CHIPS,),
            in_specs=[pl.BlockSpec(blk, mine(axis)) for blk, axis in zip(blocks, axes)]
            + [pl.BlockSpec(blk, lambda j, p: (j, 0, 0)) for blk in blocks],
            out_specs=[pl.BlockSpec(blk, lambda j, p: (j, 0, 0)) for blk in blocks]),
        out_shape=[jax.ShapeDtypeStruct(got.shape, COMM_DTYPE) for got in gots],
        compiler_params=_params(("arbitrary",), VMEM_BIG),
    )(pos, *gs, *gots)


def _chip_add_call(hsums, gots, pos, name, axes):
    n = len(hsums)
    steps = 2

    def body(pos_r, *refs):
        for own_r, got_r, o_r in zip(refs[:n], refs[n:2 * n], refs[2 * n:]):
            acc = own_r[...].astype(F32)
            for j in range(3):
                acc = acc + got_r[j].astype(F32)
            o_r[...] = acc

    in_specs, got_specs, out_specs, out_shape = [], [], [], []
    for h, axis in zip(hsums, axes):
        if axis == ROWS:
            rows, cols = h.shape[1] // steps, h.shape[2]
            in_specs.append(pl.BlockSpec((None, rows, cols), lambda i, p: (p[0], i, 0)))
            got_specs.append(pl.BlockSpec((3, rows, cols), lambda i, p: (0, i, 0)))
            out_specs.append(pl.BlockSpec((rows, cols), lambda i, p: (p[1] * steps + i, 0)))
            out_shape.append(jax.ShapeDtypeStruct((2 * h.shape[1], cols), F32))
        else:
            rows, cols = h.shape[1], h.shape[2] // steps
            in_specs.append(pl.BlockSpec((None, rows, cols), lambda i, p: (p[0], 0, i)))
            got_specs.append(pl.BlockSpec((3, rows, cols), lambda i, p: (0, 0, i)))
            out_specs.append(pl.BlockSpec((rows, cols), lambda i, p: (0, p[1] * steps + i)))
            out_shape.append(jax.ShapeDtypeStruct((rows, 2 * h.shape[2]), F32))
    return pl.pallas_call(
        body, name=name,
        grid_spec=pltpu.PrefetchScalarGridSpec(
            num_scalar_prefetch=1, grid=(steps,), in_specs=in_specs + got_specs, out_specs=out_specs),
        out_shape=out_shape,
        compiler_params=_params(("arbitrary",), VMEM_BIG),
    )(pos, *hsums, *gots)


SMALL_NAMES = ("norm_mix_pre", "norm_mix_post", "norm_mlp_pre", "norm_mlp_post", "b_gate_fwd", "b_gate_bwd",
               "gla_norm", "swa_sink", "rel_bias")


N_DEVICES = 8


def _small_pack_call(grads, extras):
    operands = list(grads) + list(extras)

    def body(*refs):
        g_refs, (all_a, all_b) = refs[:len(operands)], refs[len(operands):]
        x, y, c = _position()
        me = 4 * x + 2 * y + c
        all_a[me] = jnp.zeros(all_a.shape[1:], F32)
        all_b[me] = jnp.zeros(all_b.shape[1:], F32)
        for i in range(4):
            all_a[me, i:i + 1, :] = g_refs[i][...]
        all_a[me, 4:5, 0:256] = g_refs[4][...]
        all_a[me, 5:6, 0:256] = g_refs[5][...]
        all_a[me, 6:7, 0:128] = g_refs[6][...]
        all_a[me, 7:8, 0:128] = g_refs[7][...]
        all_a[me, 7:8, 128:256] = g_refs[11][...]
        all_b[me, 0:32, 0:128] = g_refs[8][...]
        all_b[me, 32:48, :] = g_refs[9][...]
        all_b[me, 48:64, :] = g_refs[10][...]

    out_shape = [jax.ShapeDtypeStruct((N_DEVICES, 8, D_MODEL), F32), jax.ShapeDtypeStruct((N_DEVICES, 64, 256), F32)]
    return pl.pallas_call(
        body, name="small_pack",
        in_specs=[_whole_spec(a.shape) for a in operands], out_specs=[_whole_spec(s.shape) for s in out_shape],
        out_shape=out_shape,
    )(*operands)


def _everyone_plan(n):
    def plan(refs):
        x, y, c = _position()
        copies = []
        for k in range(1, N_DEVICES):
            px = 1 - x if (k >> 2) & 1 else x
            py = 1 - y if (k >> 1) & 1 else y
            pc = 1 - c if k & 1 else c
            for a in range(n):
                mine = refs[a].at[4 * x + 2 * y + c]
                copies.append((mine, mine, (px, py, pc), refs[a].at[4 * px + 2 * py + pc]))
        return copies

    return plan


def _small_adamw_call(all_a, all_b, params):
    n_small = len(SMALL_NAMES)
    wmv = [t for p in params for t in p]
    shapes = [p[0].shape for p in params]

    def body(*refs):
        all_a, all_b = refs[:2]
        wmv_refs = refs[2:2 + 3 * n_small]
        out_refs = refs[2 + 3 * n_small:]
        sum_a, sum_b = all_a[0], all_b[0]
        for d in range(1, N_DEVICES):
            sum_a = sum_a + all_a[d]
            sum_b = sum_b + all_b[d]
        gsum = [sum_a[0:1], sum_a[1:2], sum_a[2:3], sum_a[3:4], sum_a[4:5, 0:256], sum_a[5:6, 0:256],
                sum_a[6:7, 0:128], sum_a[7:8, 0:SWA_Q_HEADS], sum_b[0:32, 0:SWA_Q_HEADS]]
        for i in range(n_small):
            w_r, m_r, v_r = wmv_refs[3 * i:3 * i + 3]
            delta, new_m, new_v = _adamw_math(w_r[...], gsum[i], m_r[...], v_r[...])
            out_refs[4 * i][...] = gsum[i]
            out_refs[4 * i + 1][...] = delta
            out_refs[4 * i + 2][...] = new_m
            out_refs[4 * i + 3][...] = new_v
        out_refs[4 * n_small][...] = sum_b[32:48]
        out_refs[4 * n_small + 1][...] = sum_b[48:64]
        out_refs[4 * n_small + 2][...] = sum_a[7:8, 128:256]

    out_shape = [jax.ShapeDtypeStruct(s, F32) for s in shapes for _ in range(4)]
    out_shape += [jax.ShapeDtypeStruct((GLA_GATE_RANK, 256), F32)] * 2 + [jax.ShapeDtypeStruct((1, 128), F32)]
    out = pl.pallas_call(
        body, name="small_adamw",
        in_specs=[_whole_spec(a.shape) for a in [all_a, all_b] + wmv],
        out_specs=[_whole_spec(s.shape) for s in out_shape],
        out_shape=out_shape,
    )(all_a, all_b, *wmv)
    per_name = [tuple(out[4 * i:4 * i + 4]) for i in range(n_small)]
    return per_name, out[4 * n_small], out[4 * n_small + 1], out[4 * n_small + 2]


def _pad_heads(t, n_heads, axis=-1):
    axis = axis % t.ndim
    shape = t.shape
    t = t.reshape(shape[:axis] + (n_heads, 64) + shape[axis + 1:])
    pad = [(0, 0)] * t.ndim
    pad[axis + 1] = (0, HEAD_PAD - 64)
    return jnp.pad(t, pad).reshape(shape[:axis] + (n_heads * HEAD_PAD,) + shape[axis + 1:])


def _unpad_heads(t, n_heads, axis=-1):
    axis = axis % t.ndim
    shape = t.shape
    t = t.reshape(shape[:axis] + (n_heads, HEAD_PAD) + shape[axis + 1:])
    t = lax.slice_in_dim(t, 0, 64, axis=axis + 1)
    return t.reshape(shape[:axis] + (n_heads * 64,) + shape[axis + 1:])


def _pad_gate(w, first_row):
    return jnp.pad(_pad_heads(w, 4), ((first_row, 128 - GLA_GATE_RANK - first_row), (0, 0)))


def _own_slot(shard, chip):
    zone = lax.empty((N_CHIPS,) + shard.shape, shard.dtype)
    return lax.dynamic_update_slice(zone, shard[None], (chip,) + (0,) * shard.ndim)


def _reduce_to_owners(grads, axes, pos, tag, overlap):
    n = len(grads)

    def half_shape(g, axis):
        return (N_CHIPS, g.shape[1] // 2, g.shape[2]) if axis == ROWS else (N_CHIPS, g.shape[1], g.shape[2] // 2)

    lands = [lax.empty(half_shape(g, axis), F32) for g, axis in zip(grads, axes)]
    handle, token = _split_start(tag + "_pair_start", list(grads) + lands, n, _pair_swap_plan(axes))
    got = _split_wait(tag + "_pair_wait", handle, n, _pair_swap_plan(axes), overlap[0](token))
    sums = list(_pair_add_call(got[:n], got[n:], pos, tag + "_pair_add", axes))
    lands = [lax.empty((3,) + s.shape[1:], s.dtype) for s in sums]
    handle, token = _split_start(tag + "_chip_start", sums + lands, 3 * n, _chip_swap_plan(n))
    got = _split_wait(tag + "_chip_wait", handle, 3 * n, _chip_swap_plan(n), overlap[1](token))
    halves = list(_chip_add_call(got[:n], got[n:], pos, tag + "_chip_add", axes))
    handle, token = _split_start(tag + "_join_start", halves, n, _pair_join_plan(axes))
    return _split_wait(tag + "_join_wait", handle, n, _pair_join_plan(axes), overlap[2](token))


def kernel(x, norm_mix_pre, w_in, w_gate_up_fwd, b_gate_fwd, w_gate_up_bwd, b_gate_bwd, gla_norm, swa_sink, rel_bias, w_out, norm_mix_post, norm_mlp_pre, w_up, w_down, norm_mlp_post, loss_target, m_norm_mix_pre, m_w_in, m_w_gate_up_fwd, m_b_gate_fwd, m_w_gate_up_bwd, m_b_gate_bwd, m_gla_norm, m_swa_sink, m_rel_bias, m_w_out, m_norm_mix_post, m_norm_mlp_pre, m_w_up, m_w_down, m_norm_mlp_post, v_norm_mix_pre, v_w_in, v_w_gate_up_fwd, v_b_gate_fwd, v_w_gate_up_bwd, v_b_gate_bwd, v_gla_norm, v_swa_sink, v_rel_bias, v_w_out, v_norm_mix_post, v_norm_mlp_pre, v_w_up, v_w_down, v_norm_mlp_post):
    given = dict(locals())
    cx, cy, cc = _position()
    chip = (2 * cx + cy).astype(jnp.int32)
    pos = jnp.stack([chip, cc.astype(jnp.int32)])
    seq, tgt = x[0], loss_target[0]
    L = seq.shape[0]

    gates = jnp.concatenate([w_gate_up_fwd[0], w_gate_up_bwd[0]], axis=0).astype(COMM_DTYPE)
    all_in, all_gates = _first_gather_call([w_in[0].T.astype(COMM_DTYPE), gates], [COLS, ROWS], [True, False])
    rest = [w_out[0].astype(COMM_DTYPE), jnp.stack([w_up[0], w_down[0]]).astype(COMM_DTYPE)]
    stage_one, stage_two = _gather_plans([ROWS, ROWS])
    handle, token = _split_start("gather_chip_start", rest + [_own_slot(s, chip) for s in rest] + [all_gates], 6,
                                 stage_one)

    w_in_t = _mx(all_in.reshape(IN_COLS, D_MODEL))
    gates_full = jnp.concatenate([all_gates[j] for j in range(N_CHIPS)], axis=1)
    wgf_p = _mx(_pad_gate(gates_full[:GLA_GATE_RANK], 0))
    wgb_p = _mx(_pad_gate(gates_full[GLA_GATE_RANK:], GLA_GATE_RANK))
    bf_p, bb_p = _pad_heads(b_gate_fwd, 4), _pad_heads(b_gate_bwd, 4)
    buckets = jnp.asarray(_band_buckets())
    sink1 = swa_sink.reshape(SWA_Q_HEADS)

    qa, ka, va, ga, qs, ks, vs, za = _proj_call(seq, norm_mix_pre, w_in_t, dep=token)
    halo = ((SWA_BLOCK, SWA_BLOCK), (0, 0))
    ks_p, vs_p = jnp.pad(ks, halo), jnp.pad(vs, halo)
    o_f, o_b, s_f, s_b = _gla_fwd_call(qa, ka, va, za, wgf_p, bf_p, wgb_p, bb_p)
    bias = _bias_call(rel_bias, buckets, dep=o_f)
    arrays = _split_wait("gather_chip_wait", handle, 6, stage_one, bias)
    handle, token = _split_start("gather_pair_start", list(arrays), 6, stage_two)
    o_s = _swa_fwd_call(qs, ks_p, vs_p, bias, sink1, dep=token)
    arrays = _split_wait("gather_pair_wait", handle, 6, stage_two, o_s)
    w_out_full = _mx(arrays[2].reshape(N_CHIPS * R_OUT, D_MODEL))
    w_ud = _mx(arrays[3])
    cat, mix, h1, n2 = _mix_call(o_f, o_b, ga, o_s, seq, gla_norm, w_out_full, norm_mix_post, norm_mlp_pre)
    a, rz, dh2, dff, loss, d_post2 = _mlp_fwd_call(n2, h1, tgt, w_ud, norm_mlp_post)

    dz, dn2 = _mlp_bwd_call(dff, rz, w_ud)
    dw_down, dw_up4 = _mlp_wgrad_call(a, dff, n2, dz)
    dh1, do, dga, dos, dw_out, d_pre2, d_post, d_gn = _mix_bwd_call(
        dn2, dh2, h1, mix, cat, o_f, o_b, ga, gla_norm, norm_mix_post, norm_mlp_pre, w_out_full)
    done = {}

    def swa_backward(tok):
        done["swa"] = _swa_bwd_call(qs, ks_p, vs_p, bias, sink1, dos, dep=tok)
        return done["swa"][0]

    def gla_in_backward(tok):
        done["gla"] = _gla_bwd_call(qa, ka, va, za, do, s_f, s_b, wgf_p, bf_p, wgb_p, bb_p, dep=tok)
        dqf, dkf, dvf, dzf, _, _, dqb, dkb, dvb, dzb, _, _ = done["gla"]
        dqs, dks_p, dvs_p, _, _ = done["swa"]
        done["in"] = _in_bwd_call(
            seq, dh1, norm_mix_pre, w_in_t,
            pairs=[(_side_by_side(T_QA), (dqf, dqb)), (_side_by_side(T_KA), (dkf, dkb)), (T_VA, (dvf, dvb)),
                   (T_ZA, (dzf, dzb))],
            singles=[(T_GA, dga), (_side_by_side(T_QS), dqs)], halos=[(T_KS, dks_p), (T_VS, dvs_p)])
        return done["in"][0]

    def bias_backward(tok):
        done["rel"] = _relbias_call(done["swa"][3], done["swa"][4], buckets, dep=tok)
        return done["rel"][0]

    g_up, g_down, g_out = _reduce_to_owners(
        [dw_up4, dw_down.reshape(N_CHIPS, R_DOWN, D_MODEL), dw_out.reshape(N_CHIPS, R_OUT, D_MODEL)],
        [ROWS, ROWS, ROWS], pos, "mlp", [swa_backward, gla_in_backward, bias_backward])
    dx, dw_in_t, d_pre = done["in"]
    dwf, dbf, dwb, dbb = done["gla"][4], done["gla"][5], done["gla"][10], done["gla"][11]
    drel, dsink = done["rel"]

    small_grads = [d_pre, d_post, d_pre2, d_post2, _unpad_heads(dbf, 4), _unpad_heads(dbb, 4), d_gn, dsink, drel]
    gate_grads = [_unpad_heads(dwf[:GLA_GATE_RANK], 4), _unpad_heads(dwb[GLA_GATE_RANK:2 * GLA_GATE_RANK], 4)]
    small_params = [(given[n], given["m_" + n], given["v_" + n]) for n in SMALL_NAMES]
    upd = {}

    everyone = _everyone_plan(2)
    small_handle, small_token = _split_start(
        "small_start", list(_small_pack_call(small_grads, gate_grads + [loss])), 2 * (N_DEVICES - 1), everyone)

    def update_out(tok):
        upd["w_out"] = (g_out,) + tuple(_adamw_call(w_out[0], g_out, m_w_out[0], v_w_out[0], "adamw_w_out",
                                                    dep=tok + small_token))
        return upd["w_out"][1]

    def update_mlp(tok):
        upd["w_up"] = (g_up,) + tuple(_adamw_call(w_up[0], g_up, m_w_up[0], v_w_up[0], "adamw_w_up", dep=tok))
        upd["w_down"] = (g_down,) + tuple(
            _adamw_call(w_down[0], g_down, m_w_down[0], v_w_down[0], "adamw_w_down", dep=upd["w_up"][1]))
        all_a, all_b = _split_wait("small_wait", small_handle, 2 * (N_DEVICES - 1), everyone, upd["w_down"][1])
        per_name, done["gf_sum"], done["gb_sum"], upd["loss"] = _small_adamw_call(all_a, all_b, small_params)
        upd.update(dict(zip(SMALL_NAMES, per_name)))
        return per_name[0][1]

    def update_gates(tok):
        for name, total in (("w_gate_up_fwd", done["gf_sum"]), ("w_gate_up_bwd", done["gb_sum"])):
            g = lax.dynamic_slice(total, (0, chip * 64), (GLA_GATE_RANK, 64))
            upd[name] = (g,) + tuple(_adamw_call(given[name][0], g, given["m_" + name][0], given["v_" + name][0],
                                                 "adamw_" + name, dep=tok))
        return upd["w_gate_up_bwd"][1]

    (g_in_t,) = _reduce_to_owners([dw_in_t.reshape(N_CHIPS, R_IN, D_MODEL)], [COLS], pos, "in",
                                  [update_out, update_mlp, update_gates])
    in_t = (g_in_t,) + tuple(_adamw_call(w_in[0].T, g_in_t, m_w_in[0].T, v_w_in[0].T, "adamw_w_in"))
    upd["w_in"] = tuple(t.T for t in in_t)

    big = ("w_in", "w_gate_up_fwd", "w_gate_up_bwd", "w_out", "w_up", "w_down")
    names = ["norm_mix_pre", "w_in", "w_gate_up_fwd", "b_gate_fwd", "w_gate_up_bwd", "b_gate_bwd", "gla_norm",
             "swa_sink", "rel_bias", "w_out", "norm_mix_post", "norm_mlp_pre", "w_up", "w_down", "norm_mlp_post"]
    outs = [upd["loss"][0, 0], dx[None]]
    for kind in range(4):
        outs += [upd[n][kind][None] if n in big else upd[n][kind] for n in names]
    return tuple(outs)
```

```python
import math

import numpy as np
import jax
import jax.numpy as jnp
from jax import lax
from jax.experimental import pallas as pl
from jax.experimental.pallas import tpu as pltpu

F32 = jnp.float32
MXU_DTYPE = jnp.bfloat16
COMM_DTYPE = jnp.bfloat16

D_MODEL = 1024
D_FF = 4096
N_CHIPS = 4
GLA_HEADS = 4
GLA_CHUNK = 64
GLA_GATE_RANK = 16
GLA_GATE_NORM = 16.0
SWA_Q_HEADS = 8
SWA_KV_HEADS = 2
SWA_BLOCK = 128
REL_BUCKETS = 32
REL_MAX_DIST = 128
NORM_EPS = 1e-6
HEAD_PAD = 128

ADAM_LR = 0.001
ADAM_B1 = 0.9
ADAM_B2 = 0.999
ADAM_EPS = 1e-08
ADAM_WD = 0.01
ADAM_STEP = 10

OUT_PAD = 1024

R_IN, R_OUT, R_UP, R_DOWN = 584, 256, 1024, 1024

VMEM_BIG = 56 * 1024 * 1024
MESH_AXES = ("x", "y", "c")
MESH_ID = pl.DeviceIdType.MESH


def _mx(a):
    return a.astype(MXU_DTYPE)


def _dot(a, b):
    return jnp.dot(a, b, preferred_element_type=F32)


def _dot_nt(a, b):
    return lax.dot_general(a, b, (((1,), (1,)), ((), ())), preferred_element_type=F32)


def _dot_tn(a, b):
    return lax.dot_general(a, b, (((0,), (0,)), ((), ())), preferred_element_type=F32)


def _rms_r(x):
    return lax.rsqrt(jnp.mean(x * x, axis=-1, keepdims=True) + NORM_EPS)


def _rms_bwd(x, r, g, dy):
    xh = x * r
    gdy = dy * g
    dx = r * (gdy - xh * jnp.mean(gdy * xh, axis=-1, keepdims=True))
    return dx, jnp.sum(dy * xh, axis=0, keepdims=True)


def _low_half(rows):
    return lax.broadcasted_iota(jnp.int32, (rows, HEAD_PAD), 1) < 64


def _spread_heads(x):
    low = _low_half(x.shape[0])
    parts = []
    for p in range(x.shape[1] // HEAD_PAD):
        pair = x[:, HEAD_PAD * p:HEAD_PAD * (p + 1)]
        parts += [jnp.where(low, pair, 0.0), jnp.where(low, pltpu.roll(pair, 64, 1), 0.0)]
    return jnp.concatenate(parts, axis=1)


def _squeeze_heads(x):
    low = _low_half(x.shape[0])
    parts = []
    for p in range(x.shape[1] // (2 * HEAD_PAD)):
        even = x[:, 2 * HEAD_PAD * p:2 * HEAD_PAD * p + HEAD_PAD]
        odd = x[:, 2 * HEAD_PAD * p + HEAD_PAD:2 * HEAD_PAD * (p + 1)]
        parts.append(jnp.where(low, even, pltpu.roll(odd, 64, 1)))
    return parts[0] if len(parts) == 1 else jnp.concatenate(parts, axis=1)


def _params(sem=None, vmem=None):
    kw = {}
    if sem is not None:
        kw["dimension_semantics"] = sem
    if vmem is not None:
        kw["vmem_limit_bytes"] = vmem
    return pltpu.CompilerParams(**kw)


def _vmem_spec():
    return pl.BlockSpec(memory_space=pltpu.VMEM)


def _whole_spec(shape):
    return pl.BlockSpec(shape, lambda: (0,) * len(shape))


def _row_spec(tm, width):
    return pl.BlockSpec((tm, width), lambda i: (i, 0))


def _full_spec(shape):
    return pl.BlockSpec(shape, lambda i: (0,) * len(shape))


def _any_spec():
    return pl.BlockSpec(memory_space=pl.ANY)


def _after(body, n_in, dep):
    if dep is None:
        return body, [], []
    return (lambda *refs: body(*refs[:n_in], *refs[n_in + 1:])), [dep], [_any_spec()]


T_QA, T_KA, T_VA, T_GA = (0, 256, 4), (256, 256, 4), (512, 512, 0), (1024, 512, 0)
T_QS, T_KS, T_VS = (1568, 512, 8), (2080, 128, 2), (2208, 128, 2)
T_ZA = (1536, 128, 0)
ZA_COLS = 2 * GLA_GATE_RANK
IN_COLS = 2336


def _side_by_side(group):
    return group[0], group[1], 0


def _proj_call(x, g_pre, w_in_t, dep=None):
    L = x.shape[0]
    tm = min(512, L)
    groups = [(T_QA, F32), (T_KA, F32), (T_VA, MXU_DTYPE), (T_GA, F32),
              (T_QS, MXU_DTYPE), (T_KS, MXU_DTYPE), (T_VS, MXU_DTYPE), (T_ZA, F32)]
    widths = [rows * (2 if heads else 1) for (_, rows, heads), _ in groups]

    def body(x_ref, g_ref, w_ref, *outs):
        xv = x_ref[...]
        u = _mx(xv * _rms_r(xv) * g_ref[...])
        for ref, (grp, _) in zip(outs, groups):
            first, rows, heads = grp
            val = _dot_nt(u, w_ref[first:first + rows, :])
            if heads:
                val = _spread_heads(val)
            if grp is T_ZA:
                val = jnp.where(lax.broadcasted_iota(jnp.int32, val.shape, 1) < ZA_COLS, val, 0.0)
            if grp is T_QS:
                val = val * 0.125
            ref[...] = val.astype(ref.dtype)

    body, extra, extra_specs = _after(body, 3, dep)
    return pl.pallas_call(
        body, name="proj_fwd", grid=(L // tm,),
        in_specs=[_row_spec(tm, D_MODEL), _full_spec((1, D_MODEL)), _vmem_spec()] + extra_specs,
        out_specs=[_row_spec(tm, w) for w in widths],
        out_shape=[jax.ShapeDtypeStruct((L, w), dt) for w, (_, dt) in zip(widths, groups)],
        compiler_params=_params(("arbitrary",), VMEM_BIG),
    )(x, g_pre, w_in_t, *extra)


def _tri_masks():
    row = lax.broadcasted_iota(jnp.int32, (GLA_CHUNK, GLA_CHUNK), 0)
    col = lax.broadcasted_iota(jnp.int32, (GLA_CHUNK, GLA_CHUNK), 1)
    return row >= col, row <= col


def _chunk_sums(tri_m, x):
    hi = _mx(x)
    rest = x - hi.astype(F32)
    mid = _mx(rest)
    lo = _mx(rest - mid.astype(F32))
    return _dot(tri_m, hi) + _dot(tri_m, mid) + _dot(tri_m, lo)


def _gla_block_pre(q_r, k_r, z_r, w_r, b_r, rev, nc, qd_s, ki_s, ks_s, dec_s, keep=None):
    tri_f, tri_b = _tri_masks()
    tri_m = _mx((tri_b if rev else tri_f).astype(F32))
    g = _dot(_mx(z_r[...]), w_r[...]) + b_r[...]
    la = (jnp.minimum(g, 0.0) - jnp.log(1.0 + jnp.exp(-jnp.abs(g)))) / GLA_GATE_NORM
    sums, lasts = [], []
    for c in range(nc):
        b_c = _chunk_sums(tri_m, la[GLA_CHUNK * c:GLA_CHUNK * (c + 1)])
        blast = b_c[0:1] if rev else b_c[GLA_CHUNK - 1:GLA_CHUNK]
        dec_s[c] = jnp.exp(blast)
        sums.append(b_c)
        lasts.append(jnp.broadcast_to(blast, b_c.shape))
    b = jnp.concatenate(sums, axis=0)
    eb = jnp.exp(b)
    enb = jnp.exp(-b)
    elb = jnp.exp(jnp.concatenate(lasts, axis=0) - b)
    k = k_r[...]
    qd_s[...] = (q_r[...] * 0.125 * eb).astype(qd_s.dtype)
    ki_s[...] = (k * enb).astype(ki_s.dtype)
    ks_s[...] = (k * elb).astype(ks_s.dtype)
    if keep is not None:
        for ref, val in zip(keep, (g, eb, enb, elb)):
            ref[...] = val


def _gla_fwd_call(qa, ka, va, za, wgf, bgf, wgb, bgb):
    L = qa.shape[0]
    br = min(512, L)
    nb, nc, n_chunks = L // br, br // GLA_CHUNK, L // GLA_CHUNK
    hw = GLA_HEADS * HEAD_PAD

    def body(qaf, kaf, vaf, zaf, qab, kab, vab, zab, wgf_r, bgf_r, wgb_r, bgb_r,
             of_r, ob_r, sf_r, sb_r, st_f, st_b, pre_f, pre_b):
        @pl.when(pl.program_id(0) == 0)
        def _():
            st_f[...] = jnp.zeros_like(st_f)
            st_b[...] = jnp.zeros_like(st_b)

        _gla_block_pre(qaf, kaf, zaf, wgf_r, bgf_r, False, nc, *pre_f)
        _gla_block_pre(qab, kab, zab, wgb_r, bgb_r, True, nc, *pre_b)
        tri_f, tri_b = _tri_masks()

        def one(tri, pre, v_r, o_r, s_r, st, ci):
            qd_s, ki_s, ks_s, dec_s = pre
            rows = pl.ds(pl.multiple_of(ci * GLA_CHUNK, GLA_CHUNK), GLA_CHUNK)
            dec = dec_s[ci]
            heads = range(GLA_HEADS)
            lanes = [slice(HEAD_PAD * h, HEAD_PAD * (h + 1)) for h in heads]
            qd = [qd_s[rows, sl] for sl in lanes]
            v = [v_r[rows, sl] for sl in lanes]
            s_t = [st[h] for h in heads]
            a = [_dot_nt(qd[h], ki_s[rows, lanes[h]]) for h in heads]
            carried = [_dot_nt(qd[h], _mx(s_t[h])) for h in heads]
            grown = [_dot_tn(v[h], ks_s[rows, lanes[h]]) for h in heads]
            a = [_mx(jnp.where(tri, a[h], 0.0)) for h in heads]
            inner = [_dot(a[h], v[h]) for h in heads]
            for h in heads:
                s_r[ci, h] = s_t[h].astype(s_r.dtype)
                o_r[rows, lanes[h]] = inner[h] + carried[h]
                st[h] = s_t[h] * dec[:, lanes[h]] + grown[h]

        def loop(t, carry):
            one(tri_f, pre_f, vaf, of_r, sf_r, st_f, t)
            one(tri_b, pre_b, vab, ob_r, sb_r, st_b, nc - 1 - t)
            return carry

        lax.fori_loop(0, nc, loop, 0, unroll=True)

    fwd = lambda i: (i, 0)
    bwd = lambda i: (nb - 1 - i, 0)
    ins = lambda m: [pl.BlockSpec((br, hw), m), pl.BlockSpec((br, hw), m),
                     pl.BlockSpec((br, hw), m), pl.BlockSpec((br, 128), m)]
    wspecs = [_full_spec((128, hw)), _full_spec((1, hw))] * 2
    s_shape = (nc, GLA_HEADS, HEAD_PAD, HEAD_PAD)
    pre_scratch = [pltpu.VMEM((br, hw), MXU_DTYPE)] * 3 + [pltpu.VMEM((nc, 1, hw), F32)]
    return pl.pallas_call(
        body, name="gla_fwd", grid=(nb,),
        in_specs=ins(fwd) + ins(bwd) + wspecs,
        out_specs=[pl.BlockSpec((br, hw), fwd), pl.BlockSpec((br, hw), bwd),
                   pl.BlockSpec(s_shape, lambda i: (i, 0, 0, 0)),
                   pl.BlockSpec(s_shape, lambda i: (nb - 1 - i, 0, 0, 0))],
        out_shape=[jax.ShapeDtypeStruct((L, hw), F32), jax.ShapeDtypeStruct((L, hw), F32),
                   jax.ShapeDtypeStruct((n_chunks,) + s_shape[1:], MXU_DTYPE),
                   jax.ShapeDtypeStruct((n_chunks,) + s_shape[1:], MXU_DTYPE)],
        scratch_shapes=[pltpu.VMEM(s_shape[1:], F32), pltpu.VMEM(s_shape[1:], F32), pre_scratch, pre_scratch],
        compiler_params=_params(("arbitrary",), VMEM_BIG),
    )(qa, ka, va, za, qa, ka, va, za, wgf, bgf, wgb, bgb)


def _gla_bwd_call(qa, ka, va, za, do, sf, sb, wgf, bgf, wgb, bgb, dep=None):
    L = qa.shape[0]
    br = min(512, L)
    nb, nc = L // br, br // GLA_CHUNK
    hw = GLA_HEADS * HEAD_PAD

    def body(qaf, kaf, vaf, zaf, dof, sf_r, qab, kab, vab, zab, dob, sb_r, wgf_r, bgf_r, wgb_r, bgb_r,
             dqf, dkf, dvf, dzf, dwf, dbf, dqb, dkb, dvb, dzb, dwb, dbb, gt_f, gt_b, pre_f, pre_b):
        @pl.when(pl.program_id(0) == 0)
        def _():
            for ref in (gt_f, gt_b, dwf, dbf, dwb, dbb):
                ref[...] = jnp.zeros_like(ref)

        _gla_block_pre(qaf, kaf, zaf, wgf_r, bgf_r, False, nc, *pre_f[:4], keep=pre_f[4:8])
        _gla_block_pre(qab, kab, zab, wgb_r, bgb_r, True, nc, *pre_b[:4], keep=pre_b[4:8])
        tri_f, tri_b = _tri_masks()
        row_w = lax.broadcasted_iota(jnp.int32, (GLA_CHUNK, HEAD_PAD), 0)

        def one(rev, pre, q_r, k_r, v_r, do_r, s_r, dq_r, dk_r, dv_r, gt, ci):
            qd_s, ki_s, ks_s, dec_s, _, eb_s, enb_s, elb_s, db_s = pre
            tri = tri_b if rev else tri_f
            last_row = 0 if rev else GLA_CHUNK - 1
            rows = pl.ds(pl.multiple_of(ci * GLA_CHUNK, GLA_CHUNK), GLA_CHUNK)
            dec = dec_s[ci]
            heads = range(GLA_HEADS)
            lanes = [slice(HEAD_PAD * h, HEAD_PAD * (h + 1)) for h in heads]
            qd = [qd_s[rows, sl] for sl in lanes]
            ki = [ki_s[rows, sl] for sl in lanes]
            ks = [ks_s[rows, sl] for sl in lanes]
            v = [v_r[rows, sl] for sl in lanes]
            do_h = [_mx(do_r[rows, sl]) for sl in lanes]
            s_t = [s_r[ci, h] for h in heads]
            g_t = [gt[h] for h in heads]
            g_m = [_mx(g_t[h]) for h in heads]
            a = [_dot_nt(qd[h], ki[h]) for h in heads]
            da = [_dot_nt(do_h[h], v[h]) for h in heads]
            dv_carried = [_dot_nt(ks[h], g_m[h]) for h in heads]
            dqd_carried = [_dot(do_h[h], _mx(s_t[h])) for h in heads]
            dks = [_dot(v[h], g_m[h]) for h in heads]
            g_grown = [_dot_tn(do_h[h], qd[h]) for h in heads]
            a = [_mx(jnp.where(tri, a[h], 0.0)) for h in heads]
            da = [_mx(jnp.where(tri, da[h], 0.0)) for h in heads]
            dv_inner = [_dot_tn(a[h], do_h[h]) for h in heads]
            dqd_inner = [_dot(da[h], ki[h]) for h in heads]
            dki = [_dot_tn(da[h], qd[h]) for h in heads]
            dq, dk = [], []
            for h in heads:
                sl = lanes[h]
                dv_r[rows, sl] = (dv_inner[h] + dv_carried[h]).astype(dv_r.dtype)
                ddec = jnp.sum(g_t[h] * s_t[h].astype(F32), axis=0, keepdims=True)
                gt[h] = g_t[h] * dec[:, sl] + g_grown[h]
                dq.append((dqd_inner[h] + dqd_carried[h]) * eb_s[rows, sl] * 0.125)
                dk_state = dks[h] * elb_s[rows, sl]
                dk.append(dki[h] * enb_s[rows, sl] + dk_state)
                k = k_r[rows, sl]
                dblast = jnp.sum(dk_state * k, axis=0, keepdims=True) + dec[:, sl] * ddec
                db_s[rows, sl] = q_r[rows, sl] * dq[h] - k * dk[h] + jnp.where(row_w == last_row, dblast, 0.0)
            low = _low_half(GLA_CHUNK)
            for pair in range(GLA_HEADS // 2):
                psl = slice(HEAD_PAD * pair, HEAD_PAD * (pair + 1))
                for ref, val in ((dq_r, dq), (dk_r, dk)):
                    both = jnp.where(low, val[2 * pair], pltpu.roll(val[2 * pair + 1], 64, 1))
                    ref[rows, psl] = both.astype(ref.dtype)

        def loop(t, carry):
            one(False, pre_f, qaf, kaf, vaf, dof, sf_r, dqf, dkf, dvf, gt_f, nc - 1 - t)
            one(True, pre_b, qab, kab, vab, dob, sb_r, dqb, dkb, dvb, gt_b, t)
            return carry

        lax.fori_loop(0, nc, loop, 0, unroll=True)

        def gate_grads(rev, pre, z_r, w_r, dz_r, dw_r, dbias_r):
            g_s, db_s = pre[4], pre[8]
            back_m = _mx((tri_f if rev else tri_b).astype(F32))
            db = db_s[...]
            dla = jnp.concatenate([_chunk_sums(back_m, db[GLA_CHUNK * c:GLA_CHUNK * (c + 1)]) for c in range(nc)],
                                  axis=0)
            dg = dla * (1.0 / GLA_GATE_NORM) * (1.0 / (1.0 + jnp.exp(g_s[...])))
            dg_m = _mx(dg)
            dz_r[...] = _dot_nt(dg_m, w_r[...])
            dw_r[...] += _dot_tn(_mx(z_r[...]), dg_m)
            dbias_r[...] += jnp.sum(dg, axis=0, keepdims=True)

        gate_grads(False, pre_f, zaf, wgf_r, dzf, dwf, dbf)
        gate_grads(True, pre_b, zab, wgb_r, dzb, dwb, dbb)

    last_first = lambda i: (nb - 1 - i, 0)
    first_last = lambda i: (i, 0)
    s_shape = (nc, GLA_HEADS, HEAD_PAD, HEAD_PAD)

    def ins(m):
        return [pl.BlockSpec((br, hw), m), pl.BlockSpec((br, hw), m), pl.BlockSpec((br, hw), m),
                pl.BlockSpec((br, 128), m), pl.BlockSpec((br, hw), m),
                pl.BlockSpec(s_shape, lambda i: m(i) + (0, 0))]

    def outs(m):
        return [pl.BlockSpec((br, hw // 2), m), pl.BlockSpec((br, hw // 2), m), pl.BlockSpec((br, hw), m),
                pl.BlockSpec((br, 128), m), _full_spec((128, hw)), _full_spec((1, hw))]

    out_shape = [jax.ShapeDtypeStruct((L, hw // 2), MXU_DTYPE)] * 2 + [
        jax.ShapeDtypeStruct((L, hw), MXU_DTYPE),
        jax.ShapeDtypeStruct((L, 128), F32), jax.ShapeDtypeStruct((128, hw), F32),
        jax.ShapeDtypeStruct((1, hw), F32)]
    wspecs = [_full_spec((128, hw)), _full_spec((1, hw))] * 2
    body, extra, extra_specs = _after(body, 16, dep)
    pre_scratch = ([pltpu.VMEM((br, hw), MXU_DTYPE)] * 3 + [pltpu.VMEM((nc, 1, hw), F32)]
                   + [pltpu.VMEM((br, hw), F32)] * 5)
    return pl.pallas_call(
        body, name="gla_bwd", grid=(nb,),
        in_specs=ins(last_first) + ins(first_last) + wspecs + extra_specs,
        out_specs=outs(last_first) + outs(first_last),
        out_shape=out_shape + out_shape,
        scratch_shapes=[pltpu.VMEM(s_shape[1:], F32), pltpu.VMEM(s_shape[1:], F32), pre_scratch, pre_scratch],
        compiler_params=_params(("arbitrary",), VMEM_BIG),
    )(qa, ka, va, za, do, sf, qa, ka, va, za, do, sb, wgf, bgf, wgb, bgb, *extra)


def _t5_buckets(rel):
    nb = REL_BUCKETS // 2
    ret = (rel > 0).astype(np.int32) * nb
    n = np.abs(rel)
    max_exact = nb // 2
    large = max_exact + (np.log(np.maximum(n, 1).astype(np.float32) / max_exact)
                         / math.log(REL_MAX_DIST / max_exact) * (nb - max_exact)).astype(np.int32)
    large = np.minimum(large, nb - 1)
    return ret + np.where(n < max_exact, n, large)


SWA_GROUP = SWA_Q_HEADS // SWA_KV_HEADS
SWA_SPAN = 3 * SWA_BLOCK
SWA_GROUP_LANES = SWA_GROUP * SWA_BLOCK


def _band_buckets():
    s = np.arange(SWA_SPAN)[:, None]
    c = np.arange(SWA_BLOCK)[None, :]
    return _t5_buckets(s - SWA_BLOCK - c).astype(np.int32)


def _swa_valid(n, seq_len):
    key_pos = (n - 1) * SWA_BLOCK + lax.broadcasted_iota(jnp.int32, (SWA_SPAN, 1), 0)
    return (key_pos >= 0) & (key_pos < seq_len)


def _swa_sink_row(sink_r, kv):
    lane = lax.broadcasted_iota(jnp.int32, (1, SWA_GROUP_LANES), 1)
    row = jnp.full((1, SWA_GROUP_LANES), sink_r[kv * SWA_GROUP], F32)
    for g in range(1, SWA_GROUP):
        row = jnp.where(lane >= g * SWA_BLOCK, sink_r[kv * SWA_GROUP + g], row)
    return row


SWA_STEP_BLOCKS = 4


def _swa_group(ref, kv, rows):
    first = kv * SWA_GROUP
    return jnp.concatenate([ref[rows, HEAD_PAD * h:HEAD_PAD * (h + 1)] for h in range(first, first + SWA_GROUP)],
                           axis=0)


def _swa_softmax(scores, bias_t, sink_row, valid):
    st = jnp.where(valid, scores + bias_t, -1e30)
    m = jnp.maximum(jnp.max(st, axis=0, keepdims=True), sink_row)
    p = jnp.exp(st - m)
    e_sink = jnp.exp(sink_row - m)
    inv = 1.0 / (jnp.sum(p, axis=0, keepdims=True) + e_sink)
    return p * inv, e_sink * inv


def _swa_fwd_call(qs, ks, vs, bias, sink, dep=None):
    L = qs.shape[0]

    def block(n, rows, q_r, k_r, v_r, bias_r, sink_r, o_r):
        span = pl.ds(pl.multiple_of(n * SWA_BLOCK, SWA_BLOCK), SWA_SPAN)
        valid = _swa_valid(n, L)
        groups = range(SWA_KV_HEADS)
        lanes = [slice(HEAD_PAD * kv, HEAD_PAD * (kv + 1)) for kv in groups]
        scores = [_dot_nt(k_r[span, lanes[kv]], _swa_group(q_r, kv, rows)) for kv in groups]
        probs = [_swa_softmax(scores[kv], bias_r[kv], _swa_sink_row(sink_r, kv), valid)[0] for kv in groups]
        low = _low_half(SWA_BLOCK)
        for kv in groups:
            og = _dot_tn(_mx(probs[kv]), v_r[span, lanes[kv]])
            for pair in range(SWA_GROUP // 2):
                even = og[2 * SWA_BLOCK * pair:2 * SWA_BLOCK * pair + SWA_BLOCK]
                odd = og[2 * SWA_BLOCK * pair + SWA_BLOCK:2 * SWA_BLOCK * (pair + 1)]
                first = HEAD_PAD * (kv * SWA_GROUP // 2 + pair)
                o_r[rows, first:first + HEAD_PAD] = jnp.where(low, even, pltpu.roll(odd, 64, 1)).astype(o_r.dtype)

    def body(*refs):
        for j in range(SWA_STEP_BLOCKS):
            block(SWA_STEP_BLOCKS * pl.program_id(0) + j, slice(SWA_BLOCK * j, SWA_BLOCK * (j + 1)), *refs)

    qw = SWA_Q_HEADS * HEAD_PAD
    tm = SWA_STEP_BLOCKS * SWA_BLOCK
    body, extra, extra_specs = _after(body, 5, dep)
    return pl.pallas_call(
        body, name="swa_fwd", grid=(L // tm,),
        in_specs=[_row_spec(tm, qw), _vmem_spec(), _vmem_spec(), _vmem_spec(),
                  pl.BlockSpec(memory_space=pltpu.SMEM)] + extra_specs,
        out_specs=_row_spec(tm, qw // 2),
        out_shape=jax.ShapeDtypeStruct((L, qw // 2), MXU_DTYPE),
        compiler_params=_params(("arbitrary",), VMEM_BIG),
    )(qs, ks, vs, bias, sink, *extra)


def _swa_bwd_call(qs, ks, vs, bias, sink, do, dep=None):
    L = qs.shape[0]
    qw = SWA_Q_HEADS * HEAD_PAD
    kw = SWA_KV_HEADS * HEAD_PAD

    def body(*refs):
        dk_r, dv_r, dbias_r, dsink_r = refs[7:]

        @pl.when(pl.program_id(0) == 0)
        def _():
            for ref in (dk_r, dv_r, dbias_r, dsink_r):
                ref[...] = jnp.zeros_like(ref)

        for j in range(SWA_STEP_BLOCKS):
            block(SWA_STEP_BLOCKS * pl.program_id(0) + j, slice(SWA_BLOCK * j, SWA_BLOCK * (j + 1)), *refs)

    def block(n, rows, q_r, k_r, v_r, bias_r, sink_r, do_r, dq_r, dk_r, dv_r, dbias_r, dsink_r):
        span = pl.ds(pl.multiple_of(n * SWA_BLOCK, SWA_BLOCK), SWA_SPAN)
        valid = _swa_valid(n, L)
        groups = range(SWA_KV_HEADS)
        lanes = [slice(HEAD_PAD * kv, HEAD_PAD * (kv + 1)) for kv in groups]
        kk = [k_r[span, sl] for sl in lanes]
        vv = [v_r[span, sl] for sl in lanes]
        qg = [_swa_group(q_r, kv, rows) for kv in groups]
        dog = [_swa_group(do_r, kv, rows) for kv in groups]
        scores = [_dot_nt(kk[kv], qg[kv]) for kv in groups]
        dp = [_dot_nt(vv[kv], dog[kv]) for kv in groups]
        probs = [_swa_softmax(scores[kv], bias_r[kv], _swa_sink_row(sink_r, kv), valid) for kv in groups]
        ds_m, pn_m = [], []
        for kv in groups:
            pn, p_sink = probs[kv]
            delta = jnp.sum(pn * dp[kv], axis=0, keepdims=True)
            ds = pn * (dp[kv] - delta)
            dsink_r[kv] -= p_sink * delta
            dbias_r[kv] += ds
            ds_m.append(_mx(ds))
            pn_m.append(_mx(pn))
        dqg = [_dot_tn(ds_m[kv], kk[kv]) * 0.125 for kv in groups]
        dkk = [_dot(ds_m[kv], qg[kv]) for kv in groups]
        dvv = [_dot(pn_m[kv], dog[kv]) for kv in groups]
        low = _low_half(SWA_BLOCK)
        for kv in groups:
            for pair in range(SWA_GROUP // 2):
                even = dqg[kv][2 * SWA_BLOCK * pair:2 * SWA_BLOCK * pair + SWA_BLOCK]
                odd = dqg[kv][2 * SWA_BLOCK * pair + SWA_BLOCK:2 * SWA_BLOCK * (pair + 1)]
                first = HEAD_PAD * (kv * SWA_GROUP // 2 + pair)
                dq_r[rows, first:first + HEAD_PAD] = jnp.where(low, even, pltpu.roll(odd, 64, 1)).astype(dq_r.dtype)
            dk_r[span, lanes[kv]] += dkk[kv]
            dv_r[span, lanes[kv]] += dvv[kv]

    tm = SWA_STEP_BLOCKS * SWA_BLOCK
    body, extra, extra_specs = _after(body, 6, dep)
    return pl.pallas_call(
        body, name="swa_bwd", grid=(L // tm,),
        in_specs=[_row_spec(tm, qw), _vmem_spec(), _vmem_spec(), _vmem_spec(),
                  pl.BlockSpec(memory_space=pltpu.SMEM), _row_spec(tm, qw)] + extra_specs,
        out_specs=[_row_spec(tm, qw // 2), _vmem_spec(), _vmem_spec(), _vmem_spec(), _vmem_spec()],
        out_shape=[jax.ShapeDtypeStruct((L, qw // 2), MXU_DTYPE),
                   jax.ShapeDtypeStruct((L + 2 * SWA_BLOCK, kw), F32),
                   jax.ShapeDtypeStruct((L + 2 * SWA_BLOCK, kw), F32),
                   jax.ShapeDtypeStruct((SWA_KV_HEADS, SWA_SPAN, SWA_GROUP_LANES), F32),
                   jax.ShapeDtypeStruct((SWA_KV_HEADS, 1, SWA_GROUP_LANES), F32)],
        compiler_params=_params(("arbitrary",), VMEM_BIG),
    )(qs, ks, vs, bias, sink, do, *extra)


def _bias_call(rel_bias, buckets, dep=None):
    def body(t_r, bk_r, o_r):
        bk = bk_r[...]
        s = lax.broadcasted_iota(jnp.int32, bk.shape, 0)
        c = lax.broadcasted_iota(jnp.int32, bk.shape, 1)
        in_band = jnp.abs(s - SWA_BLOCK - c) <= SWA_BLOCK
        for h in range(SWA_Q_HEADS):
            acc = jnp.zeros(bk.shape, F32)
            for b in range(REL_BUCKETS):
                acc = jnp.where(bk == b, t_r[b, h], acc)
            g = h % SWA_GROUP
            o_r[h // SWA_GROUP, :, SWA_BLOCK * g:SWA_BLOCK * (g + 1)] = jnp.where(in_band, acc, -1e30)

    body, extra, extra_specs = _after(body, 2, dep)
    return pl.pallas_call(
        body, name="band_bias",
        in_specs=[pl.BlockSpec(memory_space=pltpu.SMEM), _vmem_spec()] + extra_specs, out_specs=_vmem_spec(),
        out_shape=jax.ShapeDtypeStruct((SWA_KV_HEADS, SWA_SPAN, SWA_GROUP_LANES), F32),
    )(rel_bias, buckets, *extra)


def _relbias_call(dbias, dsink, buckets, dep=None):
    def body(db_r, ds_r, bk_r, o_r, os_r):
        bk = bk_r[...]
        rowi = lax.broadcasted_iota(jnp.int32, (REL_BUCKETS, 128), 0)
        lanei = lax.broadcasted_iota(jnp.int32, (REL_BUCKETS, 128), 1)
        lane1 = lax.broadcasted_iota(jnp.int32, (1, 128), 1)
        acc = jnp.zeros((REL_BUCKETS, 128), F32)
        acc_sink = jnp.zeros((1, 128), F32)
        for h in range(SWA_Q_HEADS):
            kv, g = h // SWA_GROUP, h % SWA_GROUP
            lanes = slice(SWA_BLOCK * g, SWA_BLOCK * (g + 1))
            part = db_r[kv, :, lanes]
            for b in range(REL_BUCKETS):
                s = jnp.sum(jnp.where(bk == b, part, 0.0))
                acc = acc + jnp.where((rowi == b) & (lanei == h), s, 0.0)
            acc_sink = acc_sink + jnp.where(lane1 == h, jnp.sum(ds_r[kv, :, lanes]), 0.0)
        o_r[...] = acc
        os_r[...] = acc_sink

    body, extra, extra_specs = _after(body, 3, dep)
    return pl.pallas_call(
        body, name="relbias_grad",
        in_specs=[_vmem_spec()] * 3 + extra_specs, out_specs=[_vmem_spec()] * 2,
        out_shape=[jax.ShapeDtypeStruct((REL_BUCKETS, 128), F32), jax.ShapeDtypeStruct((1, 128), F32)],
    )(dbias, dsink, buckets, *extra)


def _mix_call(o_f, o_b, ga, o_s, x, gn, w_out_p, g_post, g_pre2, dep=None):
    L = x.shape[0]
    tm = min(512, L)
    hw = GLA_HEADS * HEAD_PAD

    def body(of_r, ob_r, ga_r, os_r, x_r, gn_r, w_r, gp_r, g2_r, cat_r, mix_r, h1_r, n2_r):
        gn_v = gn_r[...]
        for h in range(GLA_HEADS):
            sl = slice(HEAD_PAD * h, HEAD_PAD * (h + 1))
            oh = of_r[:, sl] + ob_r[:, sl]
            on = oh * _rms_r(oh) * gn_v
            gate = ga_r[:, sl]
            cat_r[:, sl] = (on * (gate * jax.nn.sigmoid(gate))).astype(cat_r.dtype)
        os_v = os_r[...]
        cat_r[:, hw:] = os_v
        mix = _dot(cat_r[:, :hw], w_r[:hw, :]) + _dot(os_v, w_r[hw:, :])
        mix_r[...] = mix
        h1 = x_r[...] + mix * _rms_r(mix) * gp_r[...]
        h1_r[...] = h1
        n2_r[...] = (h1 * _rms_r(h1) * g2_r[...]).astype(n2_r.dtype)

    body, extra, extra_specs = _after(body, 9, dep)
    return pl.pallas_call(
        body, name="mix_fwd", grid=(L // tm,),
        in_specs=[_row_spec(tm, hw), _row_spec(tm, hw), _row_spec(tm, hw), _row_spec(tm, OUT_PAD - hw),
                  _row_spec(tm, D_MODEL), _full_spec((1, HEAD_PAD)), _vmem_spec(),
                  _full_spec((1, D_MODEL)), _full_spec((1, D_MODEL))] + extra_specs,
        out_specs=[_row_spec(tm, OUT_PAD), _row_spec(tm, D_MODEL), _row_spec(tm, D_MODEL), _row_spec(tm, D_MODEL)],
        out_shape=[jax.ShapeDtypeStruct((L, OUT_PAD), MXU_DTYPE), jax.ShapeDtypeStruct((L, D_MODEL), F32),
                   jax.ShapeDtypeStruct((L, D_MODEL), F32), jax.ShapeDtypeStruct((L, D_MODEL), MXU_DTYPE)],
        compiler_params=_params(("arbitrary",), VMEM_BIG),
    )(o_f, o_b, ga, o_s, x, gn, w_out_p, g_post, g_pre2, *extra)


def _mlp_fwd_call(n2, h1, tgt, w_ud, g_post):
    L = n2.shape[0]
    tm = min(512, L)
    blk = D_FF // N_CHIPS

    def body(n2_r, h1_r, t_r, w_r, g_r, a_r, rz_r, dh2_r, dff_r, loss_r, dg_r):
        @pl.when(pl.program_id(0) == 0)
        def _():
            loss_r[...] = jnp.zeros_like(loss_r)
            dg_r[...] = jnp.zeros_like(dg_r)

        n2v = n2_r[...]
        ff = jnp.zeros((tm, D_MODEL), F32)
        for j in range(N_CHIPS):
            sl = slice(blk * j, blk * (j + 1))
            rz = jnp.maximum(_dot(n2v, w_r[j, 0]), 0.0)
            a = _mx(rz * rz)
            rz_r[:, sl] = rz.astype(rz_r.dtype)
            a_r[:, sl] = a
            ff = ff + _dot(a, w_r[j, 1])
        g = g_r[...]
        r = _rms_r(ff)
        err = h1_r[...] + ff * r * g - t_r[...]
        loss_r[...] += 0.5 * jnp.sum(err * err) / D_MODEL
        dh2 = err * (1.0 / D_MODEL)
        dh2_r[...] = dh2
        dff, dg = _rms_bwd(ff, r, g, dh2)
        dff_r[...] = dff.astype(dff_r.dtype)
        dg_r[...] += dg

    return pl.pallas_call(
        body, name="mlp_fwd", grid=(L // tm,),
        in_specs=[_row_spec(tm, D_MODEL), _row_spec(tm, D_MODEL), _row_spec(tm, D_MODEL),
                  _vmem_spec(), _full_spec((1, D_MODEL))],
        out_specs=[_row_spec(tm, D_FF), _row_spec(tm, D_FF), _row_spec(tm, D_MODEL), _row_spec(tm, D_MODEL),
                   _full_spec((1, 128)), _full_spec((1, D_MODEL))],
        out_shape=[jax.ShapeDtypeStruct((L, D_FF), MXU_DTYPE), jax.ShapeDtypeStruct((L, D_FF), MXU_DTYPE),
                   jax.ShapeDtypeStruct((L, D_MODEL), F32), jax.ShapeDtypeStruct((L, D_MODEL), MXU_DTYPE),
                   jax.ShapeDtypeStruct((1, 128), F32), jax.ShapeDtypeStruct((1, D_MODEL), F32)],
        compiler_params=_params(("arbitrary",), VMEM_BIG),
    )(n2, h1, tgt, w_ud, g_post)


def _mlp_bwd_call(dff, rz, w_ud):
    L = dff.shape[0]
    tm = min(512, L)
    blk = D_FF // N_CHIPS

    def body(dff_r, rz_r, w_r, dz_r, dn2_r):
        dffv = dff_r[...]
        dn2 = jnp.zeros((tm, D_MODEL), F32)
        for j in range(N_CHIPS):
            sl = slice(blk * j, blk * (j + 1))
            dz = _mx(_dot_nt(dffv, w_r[j, 1]) * 2.0 * rz_r[:, sl].astype(F32))
            dz_r[:, sl] = dz
            dn2 = dn2 + _dot_nt(dz, w_r[j, 0])
        dn2_r[...] = dn2

    return pl.pallas_call(
        body, name="mlp_bwd", grid=(L // tm,),
        in_specs=[_row_spec(tm, D_MODEL), _row_spec(tm, D_FF), _vmem_spec()],
        out_specs=[_row_spec(tm, D_FF), _row_spec(tm, D_MODEL)],
        out_shape=[jax.ShapeDtypeStruct((L, D_FF), MXU_DTYPE), jax.ShapeDtypeStruct((L, D_MODEL), F32)],
        compiler_params=_params(("arbitrary",), VMEM_BIG),
    )(dff, rz, w_ud)


def _mlp_wgrad_call(a, dff, n2, dz):
    L = a.shape[0]
    tf = 512
    per = (D_FF // N_CHIPS) // tf

    def body(a_r, dff_r, n2_r, dz_r, dwd_r, dwu_r):
        dwd_r[...] = _dot_tn(a_r[...], dff_r[...])
        dwu_r[...] = _dot_tn(n2_r[...], dz_r[...])

    return pl.pallas_call(
        body, name="mlp_wgrad", grid=(D_FF // tf,),
        in_specs=[pl.BlockSpec((L, tf), lambda j: (0, j)), _vmem_spec(), _vmem_spec(),
                  pl.BlockSpec((L, tf), lambda j: (0, j))],
        out_specs=[pl.BlockSpec((tf, D_MODEL), lambda j: (j, 0)),
                   pl.BlockSpec((None, D_MODEL, tf), lambda j: (j // per, 0, j % per))],
        out_shape=[jax.ShapeDtypeStruct((D_FF, D_MODEL), F32),
                   jax.ShapeDtypeStruct((N_CHIPS, D_MODEL, D_FF // N_CHIPS), F32)],
        compiler_params=_params(("arbitrary",), VMEM_BIG),
    )(a, dff, n2, dz)


def _mix_bwd_call(dn2, dh2, h1, mix, cat, o_f, o_b, ga, gn, g_post, g_pre2, w_out_p):
    L = dn2.shape[0]
    tm = min(512, L)
    hw = GLA_HEADS * HEAD_PAD

    def body(dn2_r, dh2_r, h1_r, mix_r, cat_r, of_r, ob_r, ga_r, gn_r, gp_r, g2_r, w_r,
             dh1_r, do_r, dga_r, dos_r, dw_r, dg2_r, dgp_r, dgn_r):
        @pl.when(pl.program_id(0) == 0)
        def _():
            for ref in (dw_r, dg2_r, dgp_r, dgn_r):
                ref[...] = jnp.zeros_like(ref)

        parts = [slice(start, start + min(256, tm)) for start in range(0, tm, 256)]
        dmix_m = []
        for rs in parts:
            h1 = h1_r[rs, :]
            dx2, dg2 = _rms_bwd(h1, _rms_r(h1), g2_r[...], dn2_r[rs, :])
            dh1 = dh2_r[rs, :] + dx2
            dh1_r[rs, :] = dh1
            dg2_r[...] += dg2
            mix = mix_r[rs, :]
            dmix, dgp = _rms_bwd(mix, _rms_r(mix), gp_r[...], dh1)
            dgp_r[...] += dgp
            dmix_m.append(_mx(dmix))
        dcat = [_dot_nt(d, w_r[...]) for d in dmix_m]
        for rs, d in zip(parts, dmix_m):
            dw_r[...] += _dot_tn(cat_r[rs, :], d)
        gn_v = gn_r[...]
        dgn = jnp.zeros((1, HEAD_PAD), F32)
        for rs, dc in zip(parts, dcat):
            dos_r[rs, :] = _spread_heads(dc[:, hw:]).astype(dos_r.dtype)
            for h in range(GLA_HEADS):
                sl = slice(HEAD_PAD * h, HEAD_PAD * (h + 1))
                oh = of_r[rs, sl] + ob_r[rs, sl]
                rr = _rms_r(oh)
                xh = oh * rr
                gate = ga_r[rs, sl]
                sg = jax.nn.sigmoid(gate)
                silu = gate * sg
                doa = dc[:, sl]
                dga_r[rs, sl] = (doa * (xh * gn_v) * (sg + silu * (1.0 - sg))).astype(dga_r.dtype)
                don = doa * silu
                gd = don * gn_v
                do_r[rs, sl] = rr * (gd - xh * jnp.mean(gd * xh, axis=-1, keepdims=True))
                dgn = dgn + jnp.sum(don * xh, axis=0, keepdims=True)
        dgn_r[...] += dgn

    return pl.pallas_call(
        body, name="mix_bwd", grid=(L // tm,),
        in_specs=[_row_spec(tm, D_MODEL)] * 4 + [_row_spec(tm, OUT_PAD)] + [_row_spec(tm, hw)] * 3
        + [_full_spec((1, HEAD_PAD)), _full_spec((1, D_MODEL)), _full_spec((1, D_MODEL)), _vmem_spec()],
        out_specs=[_row_spec(tm, D_MODEL), _row_spec(tm, hw), _row_spec(tm, hw),
                   _row_spec(tm, SWA_Q_HEADS * HEAD_PAD),
                   _full_spec((OUT_PAD, D_MODEL)), _full_spec((1, D_MODEL)), _full_spec((1, D_MODEL)),
                   _full_spec((1, HEAD_PAD))],
        out_shape=[jax.ShapeDtypeStruct((L, D_MODEL), F32), jax.ShapeDtypeStruct((L, hw), F32),
                   jax.ShapeDtypeStruct((L, hw), MXU_DTYPE),
                   jax.ShapeDtypeStruct((L, SWA_Q_HEADS * HEAD_PAD), MXU_DTYPE),
                   jax.ShapeDtypeStruct((OUT_PAD, D_MODEL), F32), jax.ShapeDtypeStruct((1, D_MODEL), F32),
                   jax.ShapeDtypeStruct((1, D_MODEL), F32), jax.ShapeDtypeStruct((1, HEAD_PAD), F32)],
        compiler_params=_params(("arbitrary",), VMEM_BIG),
    )(dn2, dh2, h1, mix, cat, o_f, o_b, ga, gn, g_post, g_pre2, w_out_p)


def _in_bwd_call(x, dh1, g_pre, w_in_t, pairs, singles, halos, dep=None):
    L = x.shape[0]
    tm = min(512, L)
    per = tm // SWA_BLOCK
    n_pair, n_single, n_halo = len(pairs), len(singles), len(halos)
    groups = [c for c, _ in pairs] + [c for c, _ in singles] + [c for c, _ in halos]

    def body(*refs):
        x_r, dh1_r, g_r, w_r = refs[:4]
        pair_refs = refs[4:4 + 2 * n_pair]
        single_refs = refs[4 + 2 * n_pair:4 + 2 * n_pair + n_single]
        halo_refs = refs[4 + 2 * n_pair + n_single:4 + 2 * n_pair + n_single + per * n_halo]
        dx_r, dw_r, dg_r = refs[4 + 2 * n_pair + n_single + per * n_halo:]

        @pl.when(pl.program_id(0) == 0)
        def _():
            dw_r[...] = jnp.zeros_like(dw_r)
            dg_r[...] = jnp.zeros_like(dg_r)

        xv = x_r[...]
        r = _rms_r(xv)
        g = g_r[...]
        u = _mx(xv * r * g)
        vals = [pair_refs[2 * i][...].astype(F32) + pair_refs[2 * i + 1][...].astype(F32) for i in range(n_pair)]
        vals += [ref[...].astype(F32) for ref in single_refs]
        vals += [jnp.concatenate([ref[...] for ref in halo_refs[per * i:per * (i + 1)]], axis=0)
                 for i in range(n_halo)]
        ds = [_mx(_squeeze_heads(val) if heads else val) for (_, _, heads), val in zip(groups, vals)]
        du = jnp.zeros((tm, D_MODEL), F32)
        for (first, rows, _), d in zip(groups, ds):
            du = du + _dot(d, w_r[first:first + rows, :])
        for (first, rows, _), d in zip(groups, ds):
            dw_r[first:first + rows, :] += _dot_tn(d, u)
        dx, dg = _rms_bwd(xv, r, g, du)
        dx_r[...] = dh1_r[...] + dx
        dg_r[...] += dg

    arrays = [a for _, pr in pairs for a in pr] + [a for _, a in singles]
    specs = [_row_spec(tm, a.shape[1]) for a in arrays]
    for _, a in halos:
        specs += [pl.BlockSpec((SWA_BLOCK, a.shape[1]), lambda i, j=j: (per * i + 1 + j, 0)) for j in range(per)]
        arrays += [a] * per
    body, extra, extra_specs = _after(body, 4 + len(arrays), dep)
    return pl.pallas_call(
        body, name="in_bwd", grid=(L // tm,),
        in_specs=[_row_spec(tm, D_MODEL), _row_spec(tm, D_MODEL), _full_spec((1, D_MODEL)), _vmem_spec()] + specs
        + extra_specs,
        out_specs=[_row_spec(tm, D_MODEL), _full_spec((IN_COLS, D_MODEL)), _full_spec((1, D_MODEL))],
        out_shape=[jax.ShapeDtypeStruct((L, D_MODEL), F32), jax.ShapeDtypeStruct((IN_COLS, D_MODEL), F32),
                   jax.ShapeDtypeStruct((1, D_MODEL), F32)],
        compiler_params=_params(("arbitrary",), VMEM_BIG),
    )(x, dh1, g_pre, w_in_t, *arrays, *extra)


def _adamw_math(w, g, m, v):
    m = ADAM_B1 * m + (1.0 - ADAM_B1) * g
    v = ADAM_B2 * v + (1.0 - ADAM_B2) * (g * g)
    m_hat = m / (1.0 - ADAM_B1 ** ADAM_STEP)
    v_hat = v / (1.0 - ADAM_B2 ** ADAM_STEP)
    delta = -ADAM_LR * (m_hat / (jnp.sqrt(v_hat) + ADAM_EPS) + ADAM_WD * w)
    return delta, m, v


def _adamw_call(w, g, m, v, name, dep=None):
    rows, cols = w.shape
    tr = min(256, rows)

    def body(w_r, g_r, m_r, v_r, d_r, nm_r, nv_r):
        d_r[...], nm_r[...], nv_r[...] = _adamw_math(w_r[...], g_r[...], m_r[...], v_r[...])

    if rows % tr == 0:
        spec, steps = _row_spec(tr, cols), rows // tr
    else:
        spec, steps = pl.BlockSpec((rows, 256), lambda i: (0, i)), cols // 256
    body, extra, extra_specs = _after(body, 4, dep)
    return pl.pallas_call(
        body, name=name, grid=(steps,),
        in_specs=[spec] * 4 + extra_specs, out_specs=[spec] * 3,
        out_shape=[jax.ShapeDtypeStruct(w.shape, F32)] * 3,
        compiler_params=_params(("arbitrary",)),
    )(w, g, m, v, *extra)


def _position():
    return lax.axis_index("x"), lax.axis_index("y"), lax.axis_index("c")


def _other_chips(x, y):
    return [(1 - x, y), (x, 1 - y), (1 - x, 1 - y)]


ROWS, COLS = -2, -1


def _half(ref, which, axis):
    size = ref.shape[axis] // 2
    span = pl.ds(pl.multiple_of(which * size, 16 if axis == ROWS else 128), size)
    index = [slice(None)] * len(ref.shape)
    index[axis] = span
    return ref.at[tuple(index)]


def _quarter(ref, half, which, axis):
    size = ref.shape[axis] // 4
    span = pl.ds(pl.multiple_of((2 * half + which) * size, 16 if axis == ROWS else 128), size)
    index = [slice(None)] * len(ref.shape)
    index[axis] = span
    return ref.at[tuple(index)]


def _first_gather_call(shards, axes, routed):
    n = len(shards)
    per = 7

    def body(*refs):
        srcs, outs = refs[:n], refs[n:2 * n]
        send_sems, recv_sems, local_sems = refs[2 * n:]
        x, y, c = _position()
        me, sibling = (x, y, c), (x, y, 1 - c)
        x_side, y_side, across = _other_chips(x, y)
        local = [pltpu.make_async_copy(srcs[a], outs[a].at[2 * x + y], local_sems.at[a]) for a in range(n)]
        for cp in local:
            cp.start()

        def copy(a, k, dst, to, src=None):
            return pltpu.make_async_remote_copy(
                src_ref=dst if src is None else src, dst_ref=dst, send_sem=send_sems.at[per * a + k],
                recv_sem=recv_sems.at[per * a + k], device_id=to, device_id_type=MESH_ID)

        def half(a, chip, pc):
            return _half(outs[a].at[2 * chip[0] + chip[1]], pc, axes[a])

        def quarter(a, chip, q):
            return _quarter(outs[a].at[2 * chip[0] + chip[1]], c, q, axes[a])

        sends = []
        for a in range(n):
            mine = _half(srcs[a], c, axes[a])
            targets = (x_side, y_side) if routed[a] else (x_side, y_side, across)
            sends += [copy(a, j, half(a, (x, y), c), (*chip, c), src=mine) for j, chip in enumerate(targets)]
        for cp in sends:
            cp.start()
        for a in range(n):
            for j, chip in enumerate((x_side, y_side)):
                copy(a, j, half(a, chip, c), me).wait_recv()
                if routed[a]:
                    other = (y_side, x_side)[j]
                    sends.append(copy(a, 2 + j, quarter(a, chip, j), (*other, c)))
                    sends[-1].start()
                sends.append(copy(a, 4 + j, half(a, chip, c), sibling))
                sends[-1].start()
        for a in range(n):
            if routed[a]:
                for j in range(2):
                    copy(a, 2 + j, quarter(a, across, j), me).wait_recv()
            else:
                copy(a, 2, half(a, across, c), me).wait_recv()
            sends.append(copy(a, 6, half(a, across, c), sibling))
            sends[-1].start()
        for a in range(n):
            for k, chip in ((4, x_side), (5, y_side), (6, across)):
                copy(a, k, half(a, chip, 1 - c), me).wait_recv()
        for cp in sends:
            cp.wait_send()
        for cp in local:
            cp.wait()

    return pl.pallas_call(
        body, name="first_gather",
        in_specs=[_any_spec()] * n, out_specs=[_any_spec()] * n,
        out_shape=[jax.ShapeDtypeStruct((N_CHIPS,) + s.shape, s.dtype) for s in shards],
        scratch_shapes=[pltpu.SemaphoreType.DMA((per * n,)), pltpu.SemaphoreType.DMA((per * n,)),
                        pltpu.SemaphoreType.DMA((n,))],
    )(*shards)


PAIR_PEERS, CHIP_PEERS = 1, 2


def _peers(which):
    x, y, c = _position()
    if which == PAIR_PEERS:
        return [(x, y, 1 - c)]
    return [(px, py, c) for px, py in _other_chips(x, y)]


def _split_start(name, arrays, n_copies, plan, peers=None):
    n = len(arrays)

    def body(*refs):
        ins, send_sems, recv_sems, token = refs[:n], refs[n], refs[n + 1], refs[-1]
        if peers is not None:
            barrier = pltpu.get_barrier_semaphore()
            targets = _peers(peers)
            for target in targets:
                pl.semaphore_signal(barrier, inc=1, device_id=target, device_id_type=MESH_ID)
            pl.semaphore_wait(barrier, len(targets))
        for k, (src, dst, to, _) in enumerate(plan(ins)):
            pltpu.make_async_remote_copy(src_ref=src, dst_ref=dst, send_sem=send_sems.at[k],
                                         recv_sem=recv_sems.at[k], device_id=to, device_id_type=MESH_ID).start()
        token[...] = jnp.zeros_like(token)

    hbm = pl.BlockSpec(memory_space=pltpu.HBM)
    sem = pl.BlockSpec(memory_space=pltpu.SEMAPHORE)
    out = pl.pallas_call(
        body, name=name,
        out_shape=(pltpu.SemaphoreType.DMA((n_copies,)), pltpu.SemaphoreType.DMA((n_copies,)))
        + tuple(pltpu.HBM(a.shape, a.dtype) for a in arrays) + (jax.ShapeDtypeStruct((8, 128), F32),),
        in_specs=[hbm] * n, out_specs=(sem, sem) + (hbm,) * n + (_vmem_spec(),),
        input_output_aliases={i: 2 + i for i in range(n)},
        compiler_params=pltpu.CompilerParams(has_side_effects=pltpu.SideEffectType.DATAFLOW_SIDE_EFFECTING,
                                             collective_id=peers),
    )(*[pltpu.with_memory_space_constraint(a, pltpu.HBM) for a in arrays])
    return (out[0], out[1], tuple(out[2:2 + n])), out[-1]


def _split_wait(name, handle, n_copies, plan, after):
    send_sems, recv_sems, arrays = handle
    n = len(arrays)

    def body(*refs):
        ins, s_sems, r_sems = refs[:n], refs[n], refs[n + 1]
        for k, (src, dst, to, landed) in enumerate(plan(ins)):
            cp = pltpu.make_async_remote_copy(src_ref=src, dst_ref=landed, send_sem=s_sems.at[k],
                                              recv_sem=r_sems.at[k], device_id=to, device_id_type=MESH_ID)
            cp.wait_send()
            cp.wait_recv()

    hbm = pl.BlockSpec(memory_space=pltpu.HBM)
    sem = pl.BlockSpec(memory_space=pltpu.SEMAPHORE)
    out = pl.pallas_call(
        body, name=name,
        out_shape=tuple(pltpu.HBM(a.shape, a.dtype) for a in arrays),
        in_specs=[hbm] * n + [sem, sem, _any_spec()], out_specs=(hbm,) * n,
        input_output_aliases={i: i for i in range(n)},
        compiler_params=pltpu.CompilerParams(has_side_effects=pltpu.SideEffectType.DATAFLOW_SIDE_EFFECTING),
    )(*arrays, send_sems, recv_sems, after)
    return tuple(out)


def _gather_plans(axes):
    n = len(axes)

    def stage_one(refs):
        x, y, c = _position()
        copies = []
        for a, axis in enumerate(axes):
            for px, py in _other_chips(x, y):
                copies.append((_half(refs[a], c, axis), _half(refs[n + a].at[2 * x + y], c, axis),
                               (px, py, c), _half(refs[n + a].at[2 * px + py], c, axis)))
        return copies

    def stage_two(refs):
        x, y, c = _position()
        copies = []
        for a, axis in enumerate(axes):
            for px, py in _other_chips(x, y):
                piece = _half(refs[n + a].at[2 * px + py], c, axis)
                copies.append((piece, piece, (x, y, 1 - c), _half(refs[n + a].at[2 * px + py], 1 - c, axis)))
        return copies

    return stage_one, stage_two


def _pair_swap_plan(axes):
    n = len(axes)

    def plan(refs):
        x, y, c = _position()
        return [(_half(refs[a], 1 - c, axes[a]), refs[n + a], (x, y, 1 - c), refs[n + a]) for a in range(n)]

    return plan


def _chip_swap_plan(n):
    def plan(refs):
        x, y, c = _position()
        copies = []
        for a in range(n):
            for j, (px, py) in enumerate(_other_chips(x, y)):
                copies.append((refs[a].at[2 * px + py], refs[n + a].at[j], (px, py, c), refs[n + a].at[j]))
        return copies

    return plan


def _pair_join_plan(axes):
    def plan(refs):
        x, y, c = _position()
        copies = []
        for a, axis in enumerate(axes):
            mine = _half(refs[a], c, axis)
            copies.append((mine, mine, (x, y, 1 - c), _half(refs[a], 1 - c, axis)))
        return copies

    return plan


def _pair_add_call(gs, gots, pos, name, axes):
    n = len(gs)

    def body(pos_r, *refs):
        for g_r, got_r, o_r in zip(refs[:n], refs[n:2 * n], refs[2 * n:]):
            o_r[...] = (g_r[...] + got_r[...]).astype(o_r.dtype)

    def mine(axis):
        return (lambda j, p: (j, p[1], 0)) if axis == ROWS else (lambda j, p: (j, 0, p[1]))

    blocks = [(None,) + got.shape[1:] for got in gots]
    return pl.pallas_call(
        body, name=name,
        grid_spec=pltpu.PrefetchScalarGridSpec(
            num_scalar_prefetch=1, grid=(N_CHIPS,),
            in_specs=[pl.BlockSpec(blk, mine(axis)) for blk, axis in zip(blocks, axes)]
            + [pl.BlockSpec(blk, lambda j, p: (j, 0, 0)) for blk in blocks],
            out_specs=[pl.BlockSpec(blk, lambda j, p: (j, 0, 0)) for blk in blocks]),
        out_shape=[jax.ShapeDtypeStruct(got.shape, COMM_DTYPE) for got in gots],
        compiler_params=_params(("arbitrary",), VMEM_BIG),
    )(pos, *gs, *gots)


def _chip_add_call(hsums, gots, pos, name, axes):
    n = len(hsums)
    steps = 2

    def body(pos_r, *refs):
        for own_r, got_r, o_r in zip(refs[:n], refs[n:2 * n], refs[2 * n:]):
            acc = own_r[...].astype(F32)
            for j in range(3):
                acc = acc + got_r[j].astype(F32)
            o_r[...] = acc

    in_specs, got_specs, out_specs, out_shape = [], [], [], []
    for h, axis in zip(hsums, axes):
        if axis == ROWS:
            rows, cols = h.shape[1] // steps, h.shape[2]
            in_specs.append(pl.BlockSpec((None, rows, cols), lambda i, p: (p[0], i, 0)))
            got_specs.append(pl.BlockSpec((3, rows, cols), lambda i, p: (0, i, 0)))
            out_specs.append(pl.BlockSpec((rows, cols), lambda i, p: (p[1] * steps + i, 0)))
            out_shape.append(jax.ShapeDtypeStruct((2 * h.shape[1], cols), F32))
        else:
            rows, cols = h.shape[1], h.shape[2] // steps
            in_specs.append(pl.BlockSpec((None, rows, cols), lambda i, p: (p[0], 0, i)))
            got_specs.append(pl.BlockSpec((3, rows, cols), lambda i, p: (0, 0, i)))
            out_specs.append(pl.BlockSpec((rows, cols), lambda i, p: (0, p[1] * steps + i)))
            out_shape.append(jax.ShapeDtypeStruct((rows, 2 * h.shape[2]), F32))
    return pl.pallas_call(
        body, name=name,
        grid_spec=pltpu.PrefetchScalarGridSpec(
            num_scalar_prefetch=1, grid=(steps,), in_specs=in_specs + got_specs, out_specs=out_specs),
        out_shape=out_shape,
        compiler_params=_params(("arbitrary",), VMEM_BIG),
    )(pos, *hsums, *gots)


SMALL_NAMES = ("norm_mix_pre", "norm_mix_post", "norm_mlp_pre", "norm_mlp_post", "b_gate_fwd", "b_gate_bwd",
               "gla_norm", "swa_sink", "rel_bias")


N_DEVICES = 8


def _small_pack_call(grads, extras):
    operands = list(grads) + list(extras)

    def body(*refs):
        g_refs, (all_a, all_b) = refs[:len(operands)], refs[len(operands):]
        x, y, c = _position()
        me = 4 * x + 2 * y + c
        all_a[me] = jnp.zeros(all_a.shape[1:], F32)
        all_b[me] = jnp.zeros(all_b.shape[1:], F32)
        for i in range(4):
            all_a[me, i:i + 1, :] = g_refs[i][...]
        all_a[me, 4:5, 0:256] = g_refs[4][...]
        all_a[me, 5:6, 0:256] = g_refs[5][...]
        all_a[me, 6:7, 0:128] = g_refs[6][...]
        all_a[me, 7:8, 0:128] = g_refs[7][...]
        all_a[me, 7:8, 128:256] = g_refs[11][...]
        all_b[me, 0:32, 0:128] = g_refs[8][...]
        all_b[me, 32:48, :] = g_refs[9][...]
        all_b[me, 48:64, :] = g_refs[10][...]

    out_shape = [jax.ShapeDtypeStruct((N_DEVICES, 8, D_MODEL), F32), jax.ShapeDtypeStruct((N_DEVICES, 64, 256), F32)]
    return pl.pallas_call(
        body, name="small_pack",
        in_specs=[_whole_spec(a.shape) for a in operands], out_specs=[_whole_spec(s.shape) for s in out_shape],
        out_shape=out_shape,
    )(*operands)


def _everyone_plan(n):
    def plan(refs):
        x, y, c = _position()
        copies = []
        for k in range(1, N_DEVICES):
            px = 1 - x if (k >> 2) & 1 else x
            py = 1 - y if (k >> 1) & 1 else y
            pc = 1 - c if k & 1 else c
            for a in range(n):
                mine = refs[a].at[4 * x + 2 * y + c]
                copies.append((mine, mine, (px, py, pc), refs[a].at[4 * px + 2 * py + pc]))
        return copies

    return plan


def _small_adamw_call(all_a, all_b, params):
    n_small = len(SMALL_NAMES)
    wmv = [t for p in params for t in p]
    shapes = [p[0].shape for p in params]

    def body(*refs):
        all_a, all_b = refs[:2]
        wmv_refs = refs[2:2 + 3 * n_small]
        out_refs = refs[2 + 3 * n_small:]
        sum_a, sum_b = all_a[0], all_b[0]
        for d in range(1, N_DEVICES):
            sum_a = sum_a + all_a[d]
            sum_b = sum_b + all_b[d]
        gsum = [sum_a[0:1], sum_a[1:2], sum_a[2:3], sum_a[3:4], sum_a[4:5, 0:256], sum_a[5:6, 0:256],
                sum_a[6:7, 0:128], sum_a[7:8, 0:SWA_Q_HEADS], sum_b[0:32, 0:SWA_Q_HEADS]]
        for i in range(n_small):
            w_r, m_r, v_r = wmv_refs[3 * i:3 * i + 3]
            delta, new_m, new_v = _adamw_math(w_r[...], gsum[i], m_r[...], v_r[...])
            out_refs[4 * i][...] = gsum[i]
            out_refs[4 * i + 1][...] = delta
            out_refs[4 * i + 2][...] = new_m
            out_refs[4 * i + 3][...] = new_v
        out_refs[4 * n_small][...] = sum_b[32:48]
        out_refs[4 * n_small + 1][...] = sum_b[48:64]
        out_refs[4 * n_small + 2][...] = sum_a[7:8, 128:256]

    out_shape = [jax.ShapeDtypeStruct(s, F32) for s in shapes for _ in range(4)]
    out_shape += [jax.ShapeDtypeStruct((GLA_GATE_RANK, 256), F32)] * 2 + [jax.ShapeDtypeStruct((1, 128), F32)]
    out = pl.pallas_call(
        body, name="small_adamw",
        in_specs=[_whole_spec(a.shape) for a in [all_a, all_b] + wmv],
        out_specs=[_whole_spec(s.shape) for s in out_shape],
        out_shape=out_shape,
    )(all_a, all_b, *wmv)
    per_name = [tuple(out[4 * i:4 * i + 4]) for i in range(n_small)]
    return per_name, out[4 * n_small], out[4 * n_small + 1], out[4 * n_small + 2]


def _pad_heads(t, n_heads, axis=-1):
    axis = axis % t.ndim
    shape = t.shape
    t = t.reshape(shape[:axis] + (n_heads, 64) + shape[axis + 1:])
    pad = [(0, 0)] * t.ndim
    pad[axis + 1] = (0, HEAD_PAD - 64)
    return jnp.pad(t, pad).reshape(shape[:axis] + (n_heads * HEAD_PAD,) + shape[axis + 1:])


def _unpad_heads(t, n_heads, axis=-1):
    axis = axis % t.ndim
    shape = t.shape
    t = t.reshape(shape[:axis] + (n_heads, HEAD_PAD) + shape[axis + 1:])
    t = lax.slice_in_dim(t, 0, 64, axis=axis + 1)
    return t.reshape(shape[:axis] + (n_heads * 64,) + shape[axis + 1:])


def _pad_gate(w, first_row):
    return jnp.pad(_pad_heads(w, 4), ((first_row, 128 - GLA_GATE_RANK - first_row), (0, 0)))


def _own_slot(shard, chip):
    zone = lax.empty((N_CHIPS,) + shard.shape, shard.dtype)
    return lax.dynamic_update_slice(zone, shard[None], (chip,) + (0,) * shard.ndim)


def _reduce_to_owners(grads, axes, pos, tag, overlap):
    n = len(grads)

    def half_shape(g, axis):
        return (N_CHIPS, g.shape[1] // 2, g.shape[2]) if axis == ROWS else (N_CHIPS, g.shape[1], g.shape[2] // 2)

    lands = [lax.empty(half_shape(g, axis), F32) for g, axis in zip(grads, axes)]
    handle, token = _split_start(tag + "_pair_start", list(grads) + lands, n, _pair_swap_plan(axes), PAIR_PEERS)
    got = _split_wait(tag + "_pair_wait", handle, n, _pair_swap_plan(axes), overlap[0](token))
    sums = list(_pair_add_call(got[:n], got[n:], pos, tag + "_pair_add", axes))
    lands = [lax.empty((3,) + s.shape[1:], s.dtype) for s in sums]
    handle, token = _split_start(tag + "_chip_start", sums + lands, 3 * n, _chip_swap_plan(n), CHIP_PEERS)
    got = _split_wait(tag + "_chip_wait", handle, 3 * n, _chip_swap_plan(n), overlap[1](token))
    halves = list(_chip_add_call(got[:n], got[n:], pos, tag + "_chip_add", axes))
    handle, token = _split_start(tag + "_join_start", halves, n, _pair_join_plan(axes), PAIR_PEERS)
    return _split_wait(tag + "_join_wait", handle, n, _pair_join_plan(axes), overlap[2](token))


def kernel(x, norm_mix_pre, w_in, w_gate_up_fwd, b_gate_fwd, w_gate_up_bwd, b_gate_bwd, gla_norm, swa_sink, rel_bias, w_out, norm_mix_post, norm_mlp_pre, w_up, w_down, norm_mlp_post, loss_target, m_norm_mix_pre, m_w_in, m_w_gate_up_fwd, m_b_gate_fwd, m_w_gate_up_bwd, m_b_gate_bwd, m_gla_norm, m_swa_sink, m_rel_bias, m_w_out, m_norm_mix_post, m_norm_mlp_pre, m_w_up, m_w_down, m_norm_mlp_post, v_norm_mix_pre, v_w_in, v_w_gate_up_fwd, v_b_gate_fwd, v_w_gate_up_bwd, v_b_gate_bwd, v_gla_norm, v_swa_sink, v_rel_bias, v_w_out, v_norm_mix_post, v_norm_mlp_pre, v_w_up, v_w_down, v_norm_mlp_post):
    given = dict(locals())
    cx, cy, cc = _position()
    chip = (2 * cx + cy).astype(jnp.int32)
    pos = jnp.stack([chip, cc.astype(jnp.int32)])
    seq, tgt = x[0], loss_target[0]
    L = seq.shape[0]

    gates = jnp.concatenate([w_gate_up_fwd[0], w_gate_up_bwd[0]], axis=0).astype(COMM_DTYPE)
    all_in, all_gates = _first_gather_call([w_in[0].T.astype(COMM_DTYPE), gates], [COLS, ROWS], [True, False])
    rest = [w_out[0].astype(COMM_DTYPE), jnp.stack([w_up[0], w_down[0]]).astype(COMM_DTYPE)]
    stage_one, stage_two = _gather_plans([ROWS, ROWS])
    handle, token = _split_start("gather_chip_start", rest + [_own_slot(s, chip) for s in rest] + [all_gates], 6,
                                 stage_one, CHIP_PEERS)

    w_in_t = _mx(all_in.reshape(IN_COLS, D_MODEL))
    gates_full = jnp.concatenate([all_gates[j] for j in range(N_CHIPS)], axis=1)
    wgf_p = _mx(_pad_gate(gates_full[:GLA_GATE_RANK], 0))
    wgb_p = _mx(_pad_gate(gates_full[GLA_GATE_RANK:], GLA_GATE_RANK))
    bf_p, bb_p = _pad_heads(b_gate_fwd, 4), _pad_heads(b_gate_bwd, 4)
    buckets = jnp.asarray(_band_buckets())
    sink1 = swa_sink.reshape(SWA_Q_HEADS)

    qa, ka, va, ga, qs, ks, vs, za = _proj_call(seq, norm_mix_pre, w_in_t, dep=token)
    halo = ((SWA_BLOCK, SWA_BLOCK), (0, 0))
    ks_p, vs_p = jnp.pad(ks, halo), jnp.pad(vs, halo)
    o_f, o_b, s_f, s_b = _gla_fwd_call(qa, ka, va, za, wgf_p, bf_p, wgb_p, bb_p)
    bias = _bias_call(rel_bias, buckets, dep=o_f)
    arrays = _split_wait("gather_chip_wait", handle, 6, stage_one, bias)
    handle, token = _split_start("gather_pair_start", list(arrays), 6, stage_two, PAIR_PEERS)
    o_s = _swa_fwd_call(qs, ks_p, vs_p, bias, sink1, dep=token)
    arrays = _split_wait("gather_pair_wait", handle, 6, stage_two, o_s)
    w_out_full = _mx(arrays[2].reshape(N_CHIPS * R_OUT, D_MODEL))
    w_ud = _mx(arrays[3])
    cat, mix, h1, n2 = _mix_call(o_f, o_b, ga, o_s, seq, gla_norm, w_out_full, norm_mix_post, norm_mlp_pre)
    a, rz, dh2, dff, loss, d_post2 = _mlp_fwd_call(n2, h1, tgt, w_ud, norm_mlp_post)

    dz, dn2 = _mlp_bwd_call(dff, rz, w_ud)
    dw_down, dw_up4 = _mlp_wgrad_call(a, dff, n2, dz)
    dh1, do, dga, dos, dw_out, d_pre2, d_post, d_gn = _mix_bwd_call(
        dn2, dh2, h1, mix, cat, o_f, o_b, ga, gla_norm, norm_mix_post, norm_mlp_pre, w_out_full)
    done = {}

    def swa_backward(tok):
        done["swa"] = _swa_bwd_call(qs, ks_p, vs_p, bias, sink1, dos, dep=tok)
        return done["swa"][0]

    def gla_in_backward(tok):
        done["gla"] = _gla_bwd_call(qa, ka, va, za, do, s_f, s_b, wgf_p, bf_p, wgb_p, bb_p, dep=tok)
        dqf, dkf, dvf, dzf, _, _, dqb, dkb, dvb, dzb, _, _ = done["gla"]
        dqs, dks_p, dvs_p, _, _ = done["swa"]
        done["in"] = _in_bwd_call(
            seq, dh1, norm_mix_pre, w_in_t,
            pairs=[(_side_by_side(T_QA), (dqf, dqb)), (_side_by_side(T_KA), (dkf, dkb)), (T_VA, (dvf, dvb)),
                   (T_ZA, (dzf, dzb))],
            singles=[(T_GA, dga), (_side_by_side(T_QS), dqs)], halos=[(T_KS, dks_p), (T_VS, dvs_p)])
        return done["in"][0]

    def bias_backward(tok):
        done["rel"] = _relbias_call(done["swa"][3], done["swa"][4], buckets, dep=tok)
        return done["rel"][0]

    g_up, g_down, g_out = _reduce_to_owners(
        [dw_up4, dw_down.reshape(N_CHIPS, R_DOWN, D_MODEL), dw_out.reshape(N_CHIPS, R_OUT, D_MODEL)],
        [ROWS, ROWS, ROWS], pos, "mlp", [swa_backward, gla_in_backward, bias_backward])
    dx, dw_in_t, d_pre = done["in"]
    dwf, dbf, dwb, dbb = done["gla"][4], done["gla"][5], done["gla"][10], done["gla"][11]
    drel, dsink = done["rel"]

    small_grads = [d_pre, d_post, d_pre2, d_post2, _unpad_heads(dbf, 4), _unpad_heads(dbb, 4), d_gn, dsink, drel]
    gate_grads = [_unpad_heads(dwf[:GLA_GATE_RANK], 4), _unpad_heads(dwb[GLA_GATE_RANK:2 * GLA_GATE_RANK], 4)]
    small_params = [(given[n], given["m_" + n], given["v_" + n]) for n in SMALL_NAMES]
    upd = {}

    everyone = _everyone_plan(2)
    small_handle, small_token = _split_start(
        "small_start", list(_small_pack_call(small_grads, gate_grads + [loss])), 2 * (N_DEVICES - 1), everyone)

    def update_out(tok):
        upd["w_out"] = (g_out,) + tuple(_adamw_call(w_out[0], g_out, m_w_out[0], v_w_out[0], "adamw_w_out",
                                                    dep=tok + small_token))
        return upd["w_out"][1]

    def update_mlp(tok):
        upd["w_up"] = (g_up,) + tuple(_adamw_call(w_up[0], g_up, m_w_up[0], v_w_up[0], "adamw_w_up", dep=tok))
        upd["w_down"] = (g_down,) + tuple(
            _adamw_call(w_down[0], g_down, m_w_down[0], v_w_down[0], "adamw_w_down", dep=upd["w_up"][1]))
        all_a, all_b = _split_wait("small_wait", small_handle, 2 * (N_DEVICES - 1), everyone, upd["w_down"][1])
        per_name, done["gf_sum"], done["gb_sum"], upd["loss"] = _small_adamw_call(all_a, all_b, small_params)
        upd.update(dict(zip(SMALL_NAMES, per_name)))
        return per_name[0][1]

    def update_gates(tok):
        for name, total in (("w_gate_up_fwd", done["gf_sum"]), ("w_gate_up_bwd", done["gb_sum"])):
            g = lax.dynamic_slice(total, (0, chip * 64), (GLA_GATE_RANK, 64))
            upd[name] = (g,) + tuple(_adamw_call(given[name][0], g, given["m_" + name][0], given["v_" + name][0],
                                                 "adamw_" + name, dep=tok))
        return upd["w_gate_up_bwd"][1]

    (g_in_t,) = _reduce_to_owners([dw_in_t.reshape(N_CHIPS, R_IN, D_MODEL)], [COLS], pos, "in",
                                  [update_out, update_mlp, update_gates])
    in_t = (g_in_t,) + tuple(_adamw_call(w_in[0].T, g_in_t, m_w_in[0].T, v_w_in[0].T, "adamw_w_in"))
    upd["w_in"] = tuple(t.T for t in in_t)

    big = ("w_in", "w_gate_up_fwd", "w_gate_up_bwd", "w_out", "w_up", "w_down")
    names = ["norm_mix_pre", "w_in", "w_gate_up_fwd", "b_gate_fwd", "w_gate_up_bwd", "b_gate_bwd", "gla_norm",
             "swa_sink", "rel_bias", "w_out", "norm_mix_post", "norm_mlp_pre", "w_up", "w_down", "norm_mlp_post"]
    outs = [upd["loss"][0, 0], dx[None]]
    for kind in range(4):
        outs += [upd[n][kind][None] if n in big else upd[n][kind] for n in names]
    return tuple(outs)
```

```python
import math

import numpy as np
import jax
import jax.numpy as jnp
from jax import lax
from jax.experimental import pallas as pl
from jax.experimental.pallas import tpu as pltpu

F32 = jnp.float32
MXU_DTYPE = jnp.bfloat16
COMM_DTYPE = jnp.bfloat16

D_MODEL = 1024
D_FF = 4096
N_CHIPS = 4
GLA_HEADS = 4
GLA_CHUNK = 64
GLA_GATE_RANK = 16
GLA_GATE_NORM = 16.0
SWA_Q_HEADS = 8
SWA_KV_HEADS = 2
SWA_BLOCK = 128
REL_BUCKETS = 32
REL_MAX_DIST = 128
NORM_EPS = 1e-6
HEAD_PAD = 128

ADAM_LR = 0.001
ADAM_B1 = 0.9
ADAM_B2 = 0.999
ADAM_EPS = 1e-08
ADAM_WD = 0.01
ADAM_STEP = 10

OUT_PAD = 1024

R_IN, R_OUT, R_UP, R_DOWN = 584, 256, 1024, 1024

VMEM_BIG = 56 * 1024 * 1024
MESH_AXES = ("x", "y", "c")
MESH_ID = pl.DeviceIdType.MESH


def _mx(a):
    return a.astype(MXU_DTYPE)


def _dot(a, b):
    return jnp.dot(a, b, preferred_element_type=F32)


def _dot_nt(a, b):
    return lax.dot_general(a, b, (((1,), (1,)), ((), ())), preferred_element_type=F32)


def _dot_tn(a, b):
    return lax.dot_general(a, b, (((0,), (0,)), ((), ())), preferred_element_type=F32)


def _rms_r(x):
    return lax.rsqrt(jnp.mean(x * x, axis=-1, keepdims=True) + NORM_EPS)


def _rms_bwd(x, r, g, dy):
    xh = x * r
    gdy = dy * g
    dx = r * (gdy - xh * jnp.mean(gdy * xh, axis=-1, keepdims=True))
    return dx, jnp.sum(dy * xh, axis=0, keepdims=True)


def _low_half(rows):
    return lax.broadcasted_iota(jnp.int32, (rows, HEAD_PAD), 1) < 64


def _spread_heads(x):
    low = _low_half(x.shape[0])
    parts = []
    for p in range(x.shape[1] // HEAD_PAD):
        pair = x[:, HEAD_PAD * p:HEAD_PAD * (p + 1)]
        parts += [jnp.where(low, pair, 0.0), jnp.where(low, pltpu.roll(pair, 64, 1), 0.0)]
    return jnp.concatenate(parts, axis=1)


def _squeeze_heads(x):
    low = _low_half(x.shape[0])
    parts = []
    for p in range(x.shape[1] // (2 * HEAD_PAD)):
        even = x[:, 2 * HEAD_PAD * p:2 * HEAD_PAD * p + HEAD_PAD]
        odd = x[:, 2 * HEAD_PAD * p + HEAD_PAD:2 * HEAD_PAD * (p + 1)]
        parts.append(jnp.where(low, even, pltpu.roll(odd, 64, 1)))
    return parts[0] if len(parts) == 1 else jnp.concatenate(parts, axis=1)


def _params(sem=None, vmem=None):
    kw = {}
    if sem is not None:
        kw["dimension_semantics"] = sem
    if vmem is not None:
        kw["vmem_limit_bytes"] = vmem
    return pltpu.CompilerParams(**kw)


def _vmem_spec():
    return pl.BlockSpec(memory_space=pltpu.VMEM)


def _whole_spec(shape):
    return pl.BlockSpec(shape, lambda: (0,) * len(shape))


def _row_spec(tm, width):
    return pl.BlockSpec((tm, width), lambda i: (i, 0))


def _full_spec(shape):
    return pl.BlockSpec(shape, lambda i: (0,) * len(shape))


def _any_spec():
    return pl.BlockSpec(memory_space=pl.ANY)


def _after(body, n_in, dep):
    if dep is None:
        return body, [], []
    return (lambda *refs: body(*refs[:n_in], *refs[n_in + 1:])), [dep], [_any_spec()]


T_QA, T_KA, T_VA, T_GA = (0, 256, 4), (256, 256, 4), (512, 512, 0), (1024, 512, 0)
T_QS, T_KS, T_VS = (1568, 512, 8), (2080, 128, 2), (2208, 128, 2)
T_ZA = (1536, 128, 0)
ZA_COLS = 2 * GLA_GATE_RANK
IN_COLS = 2336


def _side_by_side(group):
    return group[0], group[1], 0


def _proj_call(x, g_pre, w_in_t, dep=None):
    L = x.shape[0]
    tm = min(512, L)
    groups = [(T_QA, F32), (T_KA, F32), (T_VA, MXU_DTYPE), (T_GA, F32),
              (T_QS, MXU_DTYPE), (T_KS, MXU_DTYPE), (T_VS, MXU_DTYPE), (T_ZA, F32)]
    widths = [rows * (2 if heads else 1) for (_, rows, heads), _ in groups]

    def body(x_ref, g_ref, w_ref, *outs):
        xv = x_ref[...]
        u = _mx(xv * _rms_r(xv) * g_ref[...])
        for ref, (grp, _) in zip(outs, groups):
            first, rows, heads = grp
            val = _dot_nt(u, w_ref[first:first + rows, :])
            if heads:
                val = _spread_heads(val)
            if grp is T_ZA:
                val = jnp.where(lax.broadcasted_iota(jnp.int32, val.shape, 1) < ZA_COLS, val, 0.0)
            if grp is T_QS:
                val = val * 0.125
            ref[...] = val.astype(ref.dtype)

    body, extra, extra_specs = _after(body, 3, dep)
    return pl.pallas_call(
        body, name="proj_fwd", grid=(L // tm,),
        in_specs=[_row_spec(tm, D_MODEL), _full_spec((1, D_MODEL)), _vmem_spec()] + extra_specs,
        out_specs=[_row_spec(tm, w) for w in widths],
        out_shape=[jax.ShapeDtypeStruct((L, w), dt) for w, (_, dt) in zip(widths, groups)],
        compiler_params=_params(("arbitrary",), VMEM_BIG),
    )(x, g_pre, w_in_t, *extra)


def _tri_masks():
    row = lax.broadcasted_iota(jnp.int32, (GLA_CHUNK, GLA_CHUNK), 0)
    col = lax.broadcasted_iota(jnp.int32, (GLA_CHUNK, GLA_CHUNK), 1)
    return row >= col, row <= col


def _chunk_sums(tri_m, x):
    hi = _mx(x)
    rest = x - hi.astype(F32)
    mid = _mx(rest)
    lo = _mx(rest - mid.astype(F32))
    return _dot(tri_m, hi) + _dot(tri_m, mid) + _dot(tri_m, lo)


def _gla_block_pre(q_r, k_r, z_r, w_r, b_r, rev, nc, qd_s, ki_s, ks_s, dec_s, keep=None):
    tri_f, tri_b = _tri_masks()
    tri_m = _mx((tri_b if rev else tri_f).astype(F32))
    g = _dot(_mx(z_r[...]), w_r[...]) + b_r[...]
    la = (jnp.minimum(g, 0.0) - jnp.log(1.0 + jnp.exp(-jnp.abs(g)))) * (1.0 / GLA_GATE_NORM)
    sums, lasts = [], []
    for c in range(nc):
        b_c = _chunk_sums(tri_m, la[GLA_CHUNK * c:GLA_CHUNK * (c + 1)])
        blast = b_c[0:1] if rev else b_c[GLA_CHUNK - 1:GLA_CHUNK]
        dec_s[c] = _spread_heads(jnp.exp(blast))
        sums.append(b_c)
        lasts.append(jnp.broadcast_to(blast, b_c.shape))
    b = jnp.concatenate(sums, axis=0)
    eb = jnp.exp(b)
    enb = jnp.exp(-b)
    elb = jnp.exp(jnp.concatenate(lasts, axis=0) - b)
    q, k = _squeeze_heads(q_r[...]), _squeeze_heads(k_r[...])
    qd_s[...] = _spread_heads(q * 0.125 * eb).astype(qd_s.dtype)
    ki_s[...] = _spread_heads(k * enb).astype(ki_s.dtype)
    ks_s[...] = _spread_heads(k * elb).astype(ks_s.dtype)
    if keep is not None:
        keep[0][...] = g
        for ref, val in zip(keep[1:], (eb, enb, elb)):
            ref[...] = _spread_heads(val)


def _gla_fwd_call(qa, ka, va, za, wgf, bgf, wgb, bgb):
    L = qa.shape[0]
    br = min(512, L)
    nb, nc, n_chunks = L // br, br // GLA_CHUNK, L // GLA_CHUNK
    hw = GLA_HEADS * HEAD_PAD

    def body(qaf, kaf, vaf, zaf, qab, kab, vab, zab, wgf_r, bgf_r, wgb_r, bgb_r,
             of_r, ob_r, sf_r, sb_r, st_f, st_b, pre_f, pre_b):
        @pl.when(pl.program_id(0) == 0)
        def _():
            st_f[...] = jnp.zeros_like(st_f)
            st_b[...] = jnp.zeros_like(st_b)

        _gla_block_pre(qaf, kaf, zaf, wgf_r, bgf_r, False, nc, *pre_f)
        _gla_block_pre(qab, kab, zab, wgb_r, bgb_r, True, nc, *pre_b)
        tri_f, tri_b = _tri_masks()

        def one(tri, pre, v_r, o_r, s_r, st, ci):
            qd_s, ki_s, ks_s, dec_s = pre
            rows = pl.ds(pl.multiple_of(ci * GLA_CHUNK, GLA_CHUNK), GLA_CHUNK)
            dec = dec_s[ci]
            heads = range(GLA_HEADS)
            lanes = [slice(HEAD_PAD * h, HEAD_PAD * (h + 1)) for h in heads]
            qd = [qd_s[rows, sl] for sl in lanes]
            v = [v_r[rows, sl] for sl in lanes]
            s_t = [st[h] for h in heads]
            a = [_dot_nt(qd[h], ki_s[rows, lanes[h]]) for h in heads]
            carried = [_dot_nt(qd[h], _mx(s_t[h])) for h in heads]
            grown = [_dot_tn(v[h], ks_s[rows, lanes[h]]) for h in heads]
            a = [_mx(jnp.where(tri, a[h], 0.0)) for h in heads]
            inner = [_dot(a[h], v[h]) for h in heads]
            for h in heads:
                s_r[ci, h] = s_t[h].astype(s_r.dtype)
                o_r[rows, lanes[h]] = inner[h] + carried[h]
                st[h] = s_t[h] * dec[:, lanes[h]] + grown[h]

        def loop(t, carry):
            one(tri_f, pre_f, vaf, of_r, sf_r, st_f, t)
            one(tri_b, pre_b, vab, ob_r, sb_r, st_b, nc - 1 - t)
            return carry

        lax.fori_loop(0, nc, loop, 0, unroll=True)

    fwd = lambda i: (i, 0)
    bwd = lambda i: (nb - 1 - i, 0)
    ins = lambda m: [pl.BlockSpec((br, hw), m), pl.BlockSpec((br, hw), m),
                     pl.BlockSpec((br, hw), m), pl.BlockSpec((br, 128), m)]
    wspecs = [_full_spec((128, hw // 2)), _full_spec((1, hw // 2))] * 2
    s_shape = (nc, GLA_HEADS, HEAD_PAD, HEAD_PAD)
    pre_scratch = [pltpu.VMEM((br, hw), MXU_DTYPE)] * 3 + [pltpu.VMEM((nc, 1, hw), F32)]
    return pl.pallas_call(
        body, name="gla_fwd", grid=(nb,),
        in_specs=ins(fwd) + ins(bwd) + wspecs,
        out_specs=[pl.BlockSpec((br, hw), fwd), pl.BlockSpec((br, hw), bwd),
                   pl.BlockSpec(s_shape, lambda i: (i, 0, 0, 0)),
                   pl.BlockSpec(s_shape, lambda i: (nb - 1 - i, 0, 0, 0))],
        out_shape=[jax.ShapeDtypeStruct((L, hw), F32), jax.ShapeDtypeStruct((L, hw), F32),
                   jax.ShapeDtypeStruct((n_chunks,) + s_shape[1:], MXU_DTYPE),
                   jax.ShapeDtypeStruct((n_chunks,) + s_shape[1:], MXU_DTYPE)],
        scratch_shapes=[pltpu.VMEM(s_shape[1:], F32), pltpu.VMEM(s_shape[1:], F32), pre_scratch, pre_scratch],
        compiler_params=_params(("arbitrary",), VMEM_BIG),
    )(qa, ka, va, za, qa, ka, va, za, wgf, bgf, wgb, bgb)


def _gla_bwd_call(qa, ka, va, za, do, sf, sb, wgf, bgf, wgb, bgb, dep=None):
    L = qa.shape[0]
    br = min(512, L)
    nb, nc = L // br, br // GLA_CHUNK
    hw = GLA_HEADS * HEAD_PAD

    def body(qaf, kaf, vaf, zaf, dof, sf_r, qab, kab, vab, zab, dob, sb_r, wgf_r, bgf_r, wgb_r, bgb_r,
             dqf, dkf, dvf, dzf, dwf, dbf, dqb, dkb, dvb, dzb, dwb, dbb, gt_f, gt_b, pre_f, pre_b):
        @pl.when(pl.program_id(0) == 0)
        def _():
            for ref in (gt_f, gt_b, dwf, dbf, dwb, dbb):
                ref[...] = jnp.zeros_like(ref)

        _gla_block_pre(qaf, kaf, zaf, wgf_r, bgf_r, False, nc, *pre_f[:4], keep=pre_f[4:8])
        _gla_block_pre(qab, kab, zab, wgb_r, bgb_r, True, nc, *pre_b[:4], keep=pre_b[4:8])
        tri_f, tri_b = _tri_masks()
        row_w = lax.broadcasted_iota(jnp.int32, (GLA_CHUNK, HEAD_PAD), 0)

        def one(rev, pre, q_r, k_r, v_r, do_r, s_r, dq_r, dk_r, dv_r, gt, ci):
            qd_s, ki_s, ks_s, dec_s, _, eb_s, enb_s, elb_s, db_s = pre
            tri = tri_b if rev else tri_f
            last_row = 0 if rev else GLA_CHUNK - 1
            rows = pl.ds(pl.multiple_of(ci * GLA_CHUNK, GLA_CHUNK), GLA_CHUNK)
            dec = dec_s[ci]
            heads = range(GLA_HEADS)
            lanes = [slice(HEAD_PAD * h, HEAD_PAD * (h + 1)) for h in heads]
            qd = [qd_s[rows, sl] for sl in lanes]
            ki = [ki_s[rows, sl] for sl in lanes]
            ks = [ks_s[rows, sl] for sl in lanes]
            v = [v_r[rows, sl] for sl in lanes]
            do_h = [_mx(do_r[rows, sl]) for sl in lanes]
            s_t = [s_r[ci, h] for h in heads]
            g_t = [gt[h] for h in heads]
            g_m = [_mx(g_t[h]) for h in heads]
            a = [_dot_nt(qd[h], ki[h]) for h in heads]
            da = [_dot_nt(do_h[h], v[h]) for h in heads]
            dv_carried = [_dot_nt(ks[h], g_m[h]) for h in heads]
            dqd_carried = [_dot(do_h[h], _mx(s_t[h])) for h in heads]
            dks = [_dot(v[h], g_m[h]) for h in heads]
            g_grown = [_dot_tn(do_h[h], qd[h]) for h in heads]
            a = [_mx(jnp.where(tri, a[h], 0.0)) for h in heads]
            da = [_mx(jnp.where(tri, da[h], 0.0)) for h in heads]
            dv_inner = [_dot_tn(a[h], do_h[h]) for h in heads]
            dqd_inner = [_dot(da[h], ki[h]) for h in heads]
            dki = [_dot_tn(da[h], qd[h]) for h in heads]
            dq, dk = [], []
            for h in heads:
                sl = lanes[h]
                dv_r[rows, sl] = (dv_inner[h] + dv_carried[h]).astype(dv_r.dtype)
                ddec = jnp.sum(g_t[h] * s_t[h].astype(F32), axis=0, keepdims=True)
                gt[h] = g_t[h] * dec[:, sl] + g_grown[h]
                dq.append((dqd_inner[h] + dqd_carried[h]) * eb_s[rows, sl] * 0.125)
                dk_state = dks[h] * elb_s[rows, sl]
                dk.append(dki[h] * enb_s[rows, sl] + dk_state)
                k = k_r[rows, sl]
                dblast = jnp.sum(dk_state * k, axis=0, keepdims=True) + dec[:, sl] * ddec
                db_s[rows, sl] = q_r[rows, sl] * dq[h] - k * dk[h] + jnp.where(row_w == last_row, dblast, 0.0)
            low = _low_half(GLA_CHUNK)
            for pair in range(GLA_HEADS // 2):
                psl = slice(HEAD_PAD * pair, HEAD_PAD * (pair + 1))
                for ref, val in ((dq_r, dq), (dk_r, dk)):
                    both = jnp.where(low, val[2 * pair], pltpu.roll(val[2 * pair + 1], 64, 1))
                    ref[rows, psl] = both.astype(ref.dtype)

        def loop(t, carry):
            one(False, pre_f, qaf, kaf, vaf, dof, sf_r, dqf, dkf, dvf, gt_f, nc - 1 - t)
            one(True, pre_b, qab, kab, vab, dob, sb_r, dqb, dkb, dvb, gt_b, t)
            return carry

        lax.fori_loop(0, nc, loop, 0, unroll=True)

        def gate_grads(rev, pre, z_r, w_r, dz_r, dw_r, dbias_r):
            g_s, db_s = pre[4], pre[8]
            back_m = _mx((tri_f if rev else tri_b).astype(F32))
            db = _squeeze_heads(db_s[...])
            dla = jnp.concatenate([_chunk_sums(back_m, db[GLA_CHUNK * c:GLA_CHUNK * (c + 1)]) for c in range(nc)],
                                  axis=0)
            dg = dla * (1.0 / GLA_GATE_NORM) * (1.0 / (1.0 + jnp.exp(g_s[...])))
            dg_m = _mx(dg)
            dz_r[...] = _dot_nt(dg_m, w_r[...])
            dw_r[...] += _dot_tn(_mx(z_r[...]), dg_m)
            dbias_r[...] += jnp.sum(dg, axis=0, keepdims=True)

        gate_grads(False, pre_f, zaf, wgf_r, dzf, dwf, dbf)
        gate_grads(True, pre_b, zab, wgb_r, dzb, dwb, dbb)

    last_first = lambda i: (nb - 1 - i, 0)
    first_last = lambda i: (i, 0)
    s_shape = (nc, GLA_HEADS, HEAD_PAD, HEAD_PAD)

    def ins(m):
        return [pl.BlockSpec((br, hw), m), pl.BlockSpec((br, hw), m), pl.BlockSpec((br, hw), m),
                pl.BlockSpec((br, 128), m), pl.BlockSpec((br, hw), m),
                pl.BlockSpec(s_shape, lambda i: m(i) + (0, 0))]

    def outs(m):
        return [pl.BlockSpec((br, hw // 2), m), pl.BlockSpec((br, hw // 2), m), pl.BlockSpec((br, hw), m),
                pl.BlockSpec((br, 128), m), _full_spec((128, hw // 2)), _full_spec((1, hw // 2))]

    out_shape = [jax.ShapeDtypeStruct((L, hw // 2), MXU_DTYPE)] * 2 + [
        jax.ShapeDtypeStruct((L, hw), MXU_DTYPE),
        jax.ShapeDtypeStruct((L, 128), F32), jax.ShapeDtypeStruct((128, hw // 2), F32),
        jax.ShapeDtypeStruct((1, hw // 2), F32)]
    wspecs = [_full_spec((128, hw // 2)), _full_spec((1, hw // 2))] * 2
    body, extra, extra_specs = _after(body, 16, dep)
    pre_scratch = ([pltpu.VMEM((br, hw), MXU_DTYPE)] * 3 + [pltpu.VMEM((nc, 1, hw), F32)]
                   + [pltpu.VMEM((br, hw // 2), F32)] + [pltpu.VMEM((br, hw), F32)] * 4)
    return pl.pallas_call(
        body, name="gla_bwd", grid=(nb,),
        in_specs=ins(last_first) + ins(first_last) + wspecs + extra_specs,
        out_specs=outs(last_first) + outs(first_last),
        out_shape=out_shape + out_shape,
        scratch_shapes=[pltpu.VMEM(s_shape[1:], F32), pltpu.VMEM(s_shape[1:], F32), pre_scratch, pre_scratch],
        compiler_params=_params(("arbitrary",), VMEM_BIG),
    )(qa, ka, va, za, do, sf, qa, ka, va, za, do, sb, wgf, bgf, wgb, bgb, *extra)


def _t5_buckets(rel):
    nb = REL_BUCKETS // 2
    ret = (rel > 0).astype(np.int32) * nb
    n = np.abs(rel)
    max_exact = nb // 2
    large = max_exact + (np.log(np.maximum(n, 1).astype(np.float32) / max_exact)
                         / math.log(REL_MAX_DIST / max_exact) * (nb - max_exact)).astype(np.int32)
    large = np.minimum(large, nb - 1)
    return ret + np.where(n < max_exact, n, large)


SWA_GROUP = SWA_Q_HEADS // SWA_KV_HEADS
SWA_SPAN = 3 * SWA_BLOCK
SWA_GROUP_LANES = SWA_GROUP * SWA_BLOCK


def _band_buckets():
    s = np.arange(SWA_SPAN)[:, None]
    c = np.arange(SWA_BLOCK)[None, :]
    return _t5_buckets(s - SWA_BLOCK - c).astype(np.int32)


def _swa_valid(n, seq_len):
    key_pos = (n - 1) * SWA_BLOCK + lax.broadcasted_iota(jnp.int32, (SWA_SPAN, 1), 0)
    return (key_pos >= 0) & (key_pos < seq_len)


def _swa_sink_row(sink_r, kv):
    lane = lax.broadcasted_iota(jnp.int32, (1, SWA_GROUP_LANES), 1)
    row = jnp.full((1, SWA_GROUP_LANES), sink_r[kv * SWA_GROUP], F32)
    for g in range(1, SWA_GROUP):
        row = jnp.where(lane >= g * SWA_BLOCK, sink_r[kv * SWA_GROUP + g], row)
    return row


SWA_STEP_BLOCKS = 4


def _swa_group(ref, kv, rows):
    first = kv * SWA_GROUP
    return jnp.concatenate([ref[rows, HEAD_PAD * h:HEAD_PAD * (h + 1)] for h in range(first, first + SWA_GROUP)],
                           axis=0)


def _swa_softmax(scores, bias_t, sink_row, valid):
    st = jnp.where(valid, scores + bias_t, -1e30)
    m = jnp.maximum(jnp.max(st, axis=0, keepdims=True), sink_row)
    p = jnp.exp(st - m)
    e_sink = jnp.exp(sink_row - m)
    inv = 1.0 / (jnp.sum(p, axis=0, keepdims=True) + e_sink)
    return p * inv, e_sink * inv


def _swa_fwd_call(qs, ks, vs, bias, sink, dep=None):
    L = qs.shape[0]

    def block(n, rows, q_r, k_r, v_r, bias_r, sink_r, o_r):
        span = pl.ds(pl.multiple_of(n * SWA_BLOCK, SWA_BLOCK), SWA_SPAN)
        valid = _swa_valid(n, L)
        groups = range(SWA_KV_HEADS)
        lanes = [slice(HEAD_PAD * kv, HEAD_PAD * (kv + 1)) for kv in groups]
        scores = [_dot_nt(k_r[span, lanes[kv]], _swa_group(q_r, kv, rows)) for kv in groups]
        probs = [_swa_softmax(scores[kv], bias_r[kv], _swa_sink_row(sink_r, kv), valid)[0] for kv in groups]
        low = _low_half(SWA_BLOCK)
        for kv in groups:
            og = _dot_tn(_mx(probs[kv]), v_r[span, lanes[kv]])
            for pair in range(SWA_GROUP // 2):
                even = og[2 * SWA_BLOCK * pair:2 * SWA_BLOCK * pair + SWA_BLOCK]
                odd = og[2 * SWA_BLOCK * pair + SWA_BLOCK:2 * SWA_BLOCK * (pair + 1)]
                first = HEAD_PAD * (kv * SWA_GROUP // 2 + pair)
                o_r[rows, first:first + HEAD_PAD] = jnp.where(low, even, pltpu.roll(odd, 64, 1)).astype(o_r.dtype)

    def body(*refs):
        for j in range(SWA_STEP_BLOCKS):
            block(SWA_STEP_BLOCKS * pl.program_id(0) + j, slice(SWA_BLOCK * j, SWA_BLOCK * (j + 1)), *refs)

    qw = SWA_Q_HEADS * HEAD_PAD
    tm = SWA_STEP_BLOCKS * SWA_BLOCK
    body, extra, extra_specs = _after(body, 5, dep)
    return pl.pallas_call(
        body, name="swa_fwd", grid=(L // tm,),
        in_specs=[_row_spec(tm, qw), _vmem_spec(), _vmem_spec(), _vmem_spec(),
                  pl.BlockSpec(memory_space=pltpu.SMEM)] + extra_specs,
        out_specs=_row_spec(tm, qw // 2),
        out_shape=jax.ShapeDtypeStruct((L, qw // 2), MXU_DTYPE),
        compiler_params=_params(("arbitrary",), VMEM_BIG),
    )(qs, ks, vs, bias, sink, *extra)


def _swa_bwd_call(qs, ks, vs, bias, sink, do, dep=None):
    L = qs.shape[0]
    qw = SWA_Q_HEADS * HEAD_PAD
    kw = SWA_KV_HEADS * HEAD_PAD

    def body(*refs):
        dk_r, dv_r, dbias_r, dsink_r = refs[7:]

        @pl.when(pl.program_id(0) == 0)
        def _():
            for ref in (dk_r, dv_r, dbias_r, dsink_r):
                ref[...] = jnp.zeros_like(ref)

        for j in range(SWA_STEP_BLOCKS):
            block(SWA_STEP_BLOCKS * pl.program_id(0) + j, slice(SWA_BLOCK * j, SWA_BLOCK * (j + 1)), *refs)

    def block(n, rows, q_r, k_r, v_r, bias_r, sink_r, do_r, dq_r, dk_r, dv_r, dbias_r, dsink_r):
        span = pl.ds(pl.multiple_of(n * SWA_BLOCK, SWA_BLOCK), SWA_SPAN)
        valid = _swa_valid(n, L)
        groups = range(SWA_KV_HEADS)
        lanes = [slice(HEAD_PAD * kv, HEAD_PAD * (kv + 1)) for kv in groups]
        kk = [k_r[span, sl] for sl in lanes]
        vv = [v_r[span, sl] for sl in lanes]
        qg = [_swa_group(q_r, kv, rows) for kv in groups]
        dog = [_swa_group(do_r, kv, rows) for kv in groups]
        scores = [_dot_nt(kk[kv], qg[kv]) for kv in groups]
        dp = [_dot_nt(vv[kv], dog[kv]) for kv in groups]
        probs = [_swa_softmax(scores[kv], bias_r[kv], _swa_sink_row(sink_r, kv), valid) for kv in groups]
        ds_m, pn_m = [], []
        for kv in groups:
            pn, p_sink = probs[kv]
            delta = jnp.sum(pn * dp[kv], axis=0, keepdims=True)
            ds = pn * (dp[kv] - delta)
            dsink_r[kv] -= p_sink * delta
            dbias_r[kv] += ds
            ds_m.append(_mx(ds))
            pn_m.append(_mx(pn))
        dqg = [_dot_tn(ds_m[kv], kk[kv]) * 0.125 for kv in groups]
        dkk = [_dot(ds_m[kv], qg[kv]) for kv in groups]
        dvv = [_dot(pn_m[kv], dog[kv]) for kv in groups]
        low = _low_half(SWA_BLOCK)
        for kv in groups:
            for pair in range(SWA_GROUP // 2):
                even = dqg[kv][2 * SWA_BLOCK * pair:2 * SWA_BLOCK * pair + SWA_BLOCK]
                odd = dqg[kv][2 * SWA_BLOCK * pair + SWA_BLOCK:2 * SWA_BLOCK * (pair + 1)]
                first = HEAD_PAD * (kv * SWA_GROUP // 2 + pair)
                dq_r[rows, first:first + HEAD_PAD] = jnp.where(low, even, pltpu.roll(odd, 64, 1)).astype(dq_r.dtype)
            dk_r[span, lanes[kv]] += dkk[kv]
            dv_r[span, lanes[kv]] += dvv[kv]

    tm = SWA_STEP_BLOCKS * SWA_BLOCK
    body, extra, extra_specs = _after(body, 6, dep)
    return pl.pallas_call(
        body, name="swa_bwd", grid=(L // tm,),
        in_specs=[_row_spec(tm, qw), _vmem_spec(), _vmem_spec(), _vmem_spec(),
                  pl.BlockSpec(memory_space=pltpu.SMEM), _row_spec(tm, qw)] + extra_specs,
        out_specs=[_row_spec(tm, qw // 2), _vmem_spec(), _vmem_spec(), _vmem_spec(), _vmem_spec()],
        out_shape=[jax.ShapeDtypeStruct((L, qw // 2), MXU_DTYPE),
                   jax.ShapeDtypeStruct((L + 2 * SWA_BLOCK, kw), F32),
                   jax.ShapeDtypeStruct((L + 2 * SWA_BLOCK, kw), F32),
                   jax.ShapeDtypeStruct((SWA_KV_HEADS, SWA_SPAN, SWA_GROUP_LANES), F32),
                   jax.ShapeDtypeStruct((SWA_KV_HEADS, 1, SWA_GROUP_LANES), F32)],
        compiler_params=_params(("arbitrary",), VMEM_BIG),
    )(qs, ks, vs, bias, sink, do, *extra)


def _bias_call(rel_bias, buckets, dep=None):
    def body(t_r, bk_r, o_r):
        bk = bk_r[...]
        s = lax.broadcasted_iota(jnp.int32, bk.shape, 0)
        c = lax.broadcasted_iota(jnp.int32, bk.shape, 1)
        in_band = jnp.abs(s - SWA_BLOCK - c) <= SWA_BLOCK
        for h in range(SWA_Q_HEADS):
            acc = jnp.zeros(bk.shape, F32)
            for b in range(REL_BUCKETS):
                acc = jnp.where(bk == b, t_r[b, h], acc)
            g = h % SWA_GROUP
            o_r[h // SWA_GROUP, :, SWA_BLOCK * g:SWA_BLOCK * (g + 1)] = jnp.where(in_band, acc, -1e30)

    body, extra, extra_specs = _after(body, 2, dep)
    return pl.pallas_call(
        body, name="band_bias",
        in_specs=[pl.BlockSpec(memory_space=pltpu.SMEM), _vmem_spec()] + extra_specs, out_specs=_vmem_spec(),
        out_shape=jax.ShapeDtypeStruct((SWA_KV_HEADS, SWA_SPAN, SWA_GROUP_LANES), F32),
    )(rel_bias, buckets, *extra)


def _relbias_call(dbias, dsink, buckets, dep=None):
    def body(db_r, ds_r, bk_r, o_r, os_r):
        bk = bk_r[...]
        rowi = lax.broadcasted_iota(jnp.int32, (REL_BUCKETS, 128), 0)
        lanei = lax.broadcasted_iota(jnp.int32, (REL_BUCKETS, 128), 1)
        lane1 = lax.broadcasted_iota(jnp.int32, (1, 128), 1)
        acc = jnp.zeros((REL_BUCKETS, 128), F32)
        acc_sink = jnp.zeros((1, 128), F32)
        for h in range(SWA_Q_HEADS):
            kv, g = h // SWA_GROUP, h % SWA_GROUP
            lanes = slice(SWA_BLOCK * g, SWA_BLOCK * (g + 1))
            part = db_r[kv, :, lanes]
            for b in range(REL_BUCKETS):
                s = jnp.sum(jnp.where(bk == b, part, 0.0))
                acc = acc + jnp.where((rowi == b) & (lanei == h), s, 0.0)
            acc_sink = acc_sink + jnp.where(lane1 == h, jnp.sum(ds_r[kv, :, lanes]), 0.0)
        o_r[...] = acc
        os_r[...] = acc_sink

    body, extra, extra_specs = _after(body, 3, dep)
    return pl.pallas_call(
        body, name="relbias_grad",
        in_specs=[_vmem_spec()] * 3 + extra_specs, out_specs=[_vmem_spec()] * 2,
        out_shape=[jax.ShapeDtypeStruct((REL_BUCKETS, 128), F32), jax.ShapeDtypeStruct((1, 128), F32)],
    )(dbias, dsink, buckets, *extra)


def _mix_call(o_f, o_b, ga, o_s, x, gn, w_out_p, g_post, g_pre2, dep=None):
    L = x.shape[0]
    tm = min(512, L)
    hw = GLA_HEADS * HEAD_PAD

    def body(of_r, ob_r, ga_r, os_r, x_r, gn_r, w_r, gp_r, g2_r, cat_r, mix_r, h1_r, n2_r):
        gn_v = gn_r[...]
        for h in range(GLA_HEADS):
            sl = slice(HEAD_PAD * h, HEAD_PAD * (h + 1))
            oh = of_r[:, sl] + ob_r[:, sl]
            on = oh * _rms_r(oh) * gn_v
            gate = ga_r[:, sl]
            cat_r[:, sl] = (on * (gate * jax.nn.sigmoid(gate))).astype(cat_r.dtype)
        os_v = os_r[...]
        cat_r[:, hw:] = os_v
        mix = _dot(cat_r[:, :hw], w_r[:hw, :]) + _dot(os_v, w_r[hw:, :])
        mix_r[...] = mix
        h1 = x_r[...] + mix * _rms_r(mix) * gp_r[...]
        h1_r[...] = h1
        n2_r[...] = (h1 * _rms_r(h1) * g2_r[...]).astype(n2_r.dtype)

    body, extra, extra_specs = _after(body, 9, dep)
    return pl.pallas_call(
        body, name="mix_fwd", grid=(L // tm,),
        in_specs=[_row_spec(tm, hw), _row_spec(tm, hw), _row_spec(tm, hw), _row_spec(tm, OUT_PAD - hw),
                  _row_spec(tm, D_MODEL), _full_spec((1, HEAD_PAD)), _vmem_spec(),
                  _full_spec((1, D_MODEL)), _full_spec((1, D_MODEL))] + extra_specs,
        out_specs=[_row_spec(tm, OUT_PAD), _row_spec(tm, D_MODEL), _row_spec(tm, D_MODEL), _row_spec(tm, D_MODEL)],
        out_shape=[jax.ShapeDtypeStruct((L, OUT_PAD), MXU_DTYPE), jax.ShapeDtypeStruct((L, D_MODEL), F32),
                   jax.ShapeDtypeStruct((L, D_MODEL), F32), jax.ShapeDtypeStruct((L, D_MODEL), MXU_DTYPE)],
        compiler_params=_params(("arbitrary",), VMEM_BIG),
    )(o_f, o_b, ga, o_s, x, gn, w_out_p, g_post, g_pre2, *extra)


def _mlp_fwd_call(n2, h1, tgt, w_ud, g_post):
    L = n2.shape[0]
    tm = min(512, L)
    blk = D_FF // N_CHIPS

    def body(n2_r, h1_r, t_r, w_r, g_r, a_r, rz_r, dh2_r, dff_r, loss_r, dg_r):
        @pl.when(pl.program_id(0) == 0)
        def _():
            loss_r[...] = jnp.zeros_like(loss_r)
            dg_r[...] = jnp.zeros_like(dg_r)

        n2v = n2_r[...]
        ff = jnp.zeros((tm, D_MODEL), F32)
        for j in range(N_CHIPS):
            sl = slice(blk * j, blk * (j + 1))
            rz = jnp.maximum(_dot(n2v, w_r[j, 0]), 0.0)
            a = _mx(rz * rz)
            rz_r[:, sl] = rz.astype(rz_r.dtype)
            a_r[:, sl] = a
            ff = ff + _dot(a, w_r[j, 1])
        g = g_r[...]
        r = _rms_r(ff)
        err = h1_r[...] + ff * r * g - t_r[...]
        loss_r[...] += 0.5 * jnp.sum(err * err) / D_MODEL
        dh2 = err * (1.0 / D_MODEL)
        dh2_r[...] = dh2
        dff, dg = _rms_bwd(ff, r, g, dh2)
        dff_r[...] = dff.astype(dff_r.dtype)
        dg_r[...] += dg

    return pl.pallas_call(
        body, name="mlp_fwd", grid=(L // tm,),
        in_specs=[_row_spec(tm, D_MODEL), _row_spec(tm, D_MODEL), _row_spec(tm, D_MODEL),
                  _vmem_spec(), _full_spec((1, D_MODEL))],
        out_specs=[_row_spec(tm, D_FF), _row_spec(tm, D_FF), _row_spec(tm, D_MODEL), _row_spec(tm, D_MODEL),
                   _full_spec((1, 128)), _full_spec((1, D_MODEL))],
        out_shape=[jax.ShapeDtypeStruct((L, D_FF), MXU_DTYPE), jax.ShapeDtypeStruct((L, D_FF), MXU_DTYPE),
                   jax.ShapeDtypeStruct((L, D_MODEL), F32), jax.ShapeDtypeStruct((L, D_MODEL), MXU_DTYPE),
                   jax.ShapeDtypeStruct((1, 128), F32), jax.ShapeDtypeStruct((1, D_MODEL), F32)],
        compiler_params=_params(("arbitrary",), VMEM_BIG),
    )(n2, h1, tgt, w_ud, g_post)


def _mlp_bwd_call(dff, rz, w_ud):
    L = dff.shape[0]
    tm = min(512, L)
    blk = D_FF // N_CHIPS

    def body(dff_r, rz_r, w_r, dz_r, dn2_r):
        dffv = dff_r[...]
        dn2 = jnp.zeros((tm, D_MODEL), F32)
        for j in range(N_CHIPS):
            sl = slice(blk * j, blk * (j + 1))
            dz = _mx(_dot_nt(dffv, w_r[j, 1]) * 2.0 * rz_r[:, sl].astype(F32))
            dz_r[:, sl] = dz
            dn2 = dn2 + _dot_nt(dz, w_r[j, 0])
        dn2_r[...] = dn2

    return pl.pallas_call(
        body, name="mlp_bwd", grid=(L // tm,),
        in_specs=[_row_spec(tm, D_MODEL), _row_spec(tm, D_FF), _vmem_spec()],
        out_specs=[_row_spec(tm, D_FF), _row_spec(tm, D_MODEL)],
        out_shape=[jax.ShapeDtypeStruct((L, D_FF), MXU_DTYPE), jax.ShapeDtypeStruct((L, D_MODEL), F32)],
        compiler_params=_params(("arbitrary",), VMEM_BIG),
    )(dff, rz, w_ud)


def _mlp_wgrad_call(a, dff, n2, dz):
    L = a.shape[0]
    tf = 512
    per = (D_FF // N_CHIPS) // tf

    def body(a_r, dff_r, n2_r, dz_r, dwd_r, dwu_r):
        dwd_r[...] = _dot_tn(a_r[...], dff_r[...])
        dwu_r[...] = _dot_tn(n2_r[...], dz_r[...])

    return pl.pallas_call(
        body, name="mlp_wgrad", grid=(D_FF // tf,),
        in_specs=[pl.BlockSpec((L, tf), lambda j: (0, j)), _vmem_spec(), _vmem_spec(),
                  pl.BlockSpec((L, tf), lambda j: (0, j))],
        out_specs=[pl.BlockSpec((tf, D_MODEL), lambda j: (j, 0)),
                   pl.BlockSpec((None, D_MODEL, tf), lambda j: (j // per, 0, j % per))],
        out_shape=[jax.ShapeDtypeStruct((D_FF, D_MODEL), F32),
                   jax.ShapeDtypeStruct((N_CHIPS, D_MODEL, D_FF // N_CHIPS), F32)],
        compiler_params=_params(("arbitrary",), VMEM_BIG),
    )(a, dff, n2, dz)


def _mix_bwd_call(dn2, dh2, h1, mix, cat, o_f, o_b, ga, gn, g_post, g_pre2, w_out_p):
    L = dn2.shape[0]
    tm = min(512, L)
    hw = GLA_HEADS * HEAD_PAD

    def body(dn2_r, dh2_r, h1_r, mix_r, cat_r, of_r, ob_r, ga_r, gn_r, gp_r, g2_r, w_r,
             dh1_r, do_r, dga_r, dos_r, dw_r, dg2_r, dgp_r, dgn_r):
        @pl.when(pl.program_id(0) == 0)
        def _():
            for ref in (dw_r, dg2_r, dgp_r, dgn_r):
                ref[...] = jnp.zeros_like(ref)

        parts = [slice(start, start + min(256, tm)) for start in range(0, tm, 256)]
        dmix_m = []
        for rs in parts:
            h1 = h1_r[rs, :]
            dx2, dg2 = _rms_bwd(h1, _rms_r(h1), g2_r[...], dn2_r[rs, :])
            dh1 = dh2_r[rs, :] + dx2
            dh1_r[rs, :] = dh1
            dg2_r[...] += dg2
            mix = mix_r[rs, :]
            dmix, dgp = _rms_bwd(mix, _rms_r(mix), gp_r[...], dh1)
            dgp_r[...] += dgp
            dmix_m.append(_mx(dmix))
        dcat = [_dot_nt(d, w_r[...]) for d in dmix_m]
        for rs, d in zip(parts, dmix_m):
            dw_r[...] += _dot_tn(cat_r[rs, :], d)
        gn_v = gn_r[...]
        dgn = jnp.zeros((1, HEAD_PAD), F32)
        for rs, dc in zip(parts, dcat):
            dos_r[rs, :] = _spread_heads(dc[:, hw:]).astype(dos_r.dtype)
            for h in range(GLA_HEADS):
                sl = slice(HEAD_PAD * h, HEAD_PAD * (h + 1))
                oh = of_r[rs, sl] + ob_r[rs, sl]
                rr = _rms_r(oh)
                xh = oh * rr
                gate = ga_r[rs, sl]
                sg = jax.nn.sigmoid(gate)
                silu = gate * sg
                doa = dc[:, sl]
                dga_r[rs, sl] = (doa * (xh * gn_v) * (sg + silu * (1.0 - sg))).astype(dga_r.dtype)
                don = doa * silu
                gd = don * gn_v
                do_r[rs, sl] = rr * (gd - xh * jnp.mean(gd * xh, axis=-1, keepdims=True))
                dgn = dgn + jnp.sum(don * xh, axis=0, keepdims=True)
        dgn_r[...] += dgn

    return pl.pallas_call(
        body, name="mix_bwd", grid=(L // tm,),
        in_specs=[_row_spec(tm, D_MODEL)] * 4 + [_row_spec(tm, OUT_PAD)] + [_row_spec(tm, hw)] * 3
        + [_full_spec((1, HEAD_PAD)), _full_spec((1, D_MODEL)), _full_spec((1, D_MODEL)), _vmem_spec()],
        out_specs=[_row_spec(tm, D_MODEL), _row_spec(tm, hw), _row_spec(tm, hw),
                   _row_spec(tm, SWA_Q_HEADS * HEAD_PAD),
                   _full_spec((OUT_PAD, D_MODEL)), _full_spec((1, D_MODEL)), _full_spec((1, D_MODEL)),
                   _full_spec((1, HEAD_PAD))],
        out_shape=[jax.ShapeDtypeStruct((L, D_MODEL), F32), jax.ShapeDtypeStruct((L, hw), F32),
                   jax.ShapeDtypeStruct((L, hw), MXU_DTYPE),
                   jax.ShapeDtypeStruct((L, SWA_Q_HEADS * HEAD_PAD), MXU_DTYPE),
                   jax.ShapeDtypeStruct((OUT_PAD, D_MODEL), F32), jax.ShapeDtypeStruct((1, D_MODEL), F32),
                   jax.ShapeDtypeStruct((1, D_MODEL), F32), jax.ShapeDtypeStruct((1, HEAD_PAD), F32)],
        compiler_params=_params(("arbitrary",), VMEM_BIG),
    )(dn2, dh2, h1, mix, cat, o_f, o_b, ga, gn, g_post, g_pre2, w_out_p)


def _in_bwd_call(x, dh1, g_pre, w_in_t, pairs, singles, halos, dep=None):
    L = x.shape[0]
    tm = min(512, L)
    per = tm // SWA_BLOCK
    n_pair, n_single, n_halo = len(pairs), len(singles), len(halos)
    groups = [c for c, _ in pairs] + [c for c, _ in singles] + [c for c, _ in halos]

    def body(*refs):
        x_r, dh1_r, g_r, w_r = refs[:4]
        pair_refs = refs[4:4 + 2 * n_pair]
        single_refs = refs[4 + 2 * n_pair:4 + 2 * n_pair + n_single]
        halo_refs = refs[4 + 2 * n_pair + n_single:4 + 2 * n_pair + n_single + per * n_halo]
        dx_r, dw_r, dg_r = refs[4 + 2 * n_pair + n_single + per * n_halo:]

        @pl.when(pl.program_id(0) == 0)
        def _():
            dw_r[...] = jnp.zeros_like(dw_r)
            dg_r[...] = jnp.zeros_like(dg_r)

        xv = x_r[...]
        r = _rms_r(xv)
        g = g_r[...]
        u = _mx(xv * r * g)
        vals = [pair_refs[2 * i][...].astype(F32) + pair_refs[2 * i + 1][...].astype(F32) for i in range(n_pair)]
        vals += [ref[...].astype(F32) for ref in single_refs]
        vals += [jnp.concatenate([ref[...] for ref in halo_refs[per * i:per * (i + 1)]], axis=0)
                 for i in range(n_halo)]
        ds = [_mx(_squeeze_heads(val) if heads else val) for (_, _, heads), val in zip(groups, vals)]
        du = jnp.zeros((tm, D_MODEL), F32)
        for (first, rows, _), d in zip(groups, ds):
            du = du + _dot(d, w_r[first:first + rows, :])
        for (first, rows, _), d in zip(groups, ds):
            dw_r[first:first + rows, :] += _dot_tn(d, u)
        dx, dg = _rms_bwd(xv, r, g, du)
        dx_r[...] = dh1_r[...] + dx
        dg_r[...] += dg

    arrays = [a for _, pr in pairs for a in pr] + [a for _, a in singles]
    specs = [_row_spec(tm, a.shape[1]) for a in arrays]
    for _, a in halos:
        specs += [pl.BlockSpec((SWA_BLOCK, a.shape[1]), lambda i, j=j: (per * i + 1 + j, 0)) for j in range(per)]
        arrays += [a] * per
    body, extra, extra_specs = _after(body, 4 + len(arrays), dep)
    return pl.pallas_call(
        body, name="in_bwd", grid=(L // tm,),
        in_specs=[_row_spec(tm, D_MODEL), _row_spec(tm, D_MODEL), _full_spec((1, D_MODEL)), _vmem_spec()] + specs
        + extra_specs,
        out_specs=[_row_spec(tm, D_MODEL), _full_spec((IN_COLS, D_MODEL)), _full_spec((1, D_MODEL))],
        out_shape=[jax.ShapeDtypeStruct((L, D_MODEL), F32), jax.ShapeDtypeStruct((IN_COLS, D_MODEL), F32),
                   jax.ShapeDtypeStruct((1, D_MODEL), F32)],
        compiler_params=_params(("arbitrary",), VMEM_BIG),
    )(x, dh1, g_pre, w_in_t, *arrays, *extra)


def _adamw_math(w, g, m, v):
    m = ADAM_B1 * m + (1.0 - ADAM_B1) * g
    v = ADAM_B2 * v + (1.0 - ADAM_B2) * (g * g)
    m_hat = m / (1.0 - ADAM_B1 ** ADAM_STEP)
    v_hat = v / (1.0 - ADAM_B2 ** ADAM_STEP)
    delta = -ADAM_LR * (m_hat / (jnp.sqrt(v_hat) + ADAM_EPS) + ADAM_WD * w)
    return delta, m, v


def _adamw_call(w, g, m, v, name, dep=None):
    rows, cols = w.shape
    tr = min(256, rows)

    def body(w_r, g_r, m_r, v_r, d_r, nm_r, nv_r):
        d_r[...], nm_r[...], nv_r[...] = _adamw_math(w_r[...], g_r[...], m_r[...], v_r[...])

    if rows % tr == 0:
        spec, steps = _row_spec(tr, cols), rows // tr
    else:
        spec, steps = pl.BlockSpec((rows, 256), lambda i: (0, i)), cols // 256
    body, extra, extra_specs = _after(body, 4, dep)
    return pl.pallas_call(
        body, name=name, grid=(steps,),
        in_specs=[spec] * 4 + extra_specs, out_specs=[spec] * 3,
        out_shape=[jax.ShapeDtypeStruct(w.shape, F32)] * 3,
        compiler_params=_params(("arbitrary",)),
    )(w, g, m, v, *extra)


def _position():
    return lax.axis_index("x"), lax.axis_index("y"), lax.axis_index("c")


def _other_chips(x, y):
    return [(1 - x, y), (x, 1 - y), (1 - x, 1 - y)]


ROWS, COLS = -2, -1


def _half(ref, which, axis):
    size = ref.shape[axis] // 2
    span = pl.ds(pl.multiple_of(which * size, 16 if axis == ROWS else 128), size)
    index = [slice(None)] * len(ref.shape)
    index[axis] = span
    return ref.at[tuple(index)]


def _quarter(ref, half, which, axis):
    size = ref.shape[axis] // 4
    span = pl.ds(pl.multiple_of((2 * half + which) * size, 16 if axis == ROWS else 128), size)
    index = [slice(None)] * len(ref.shape)
    index[axis] = span
    return ref.at[tuple(index)]


def _first_gather_call(shards, axes, routed):
    n = len(shards)
    per = 7

    def body(*refs):
        srcs, outs = refs[:n], refs[n:2 * n]
        send_sems, recv_sems, local_sems = refs[2 * n:]
        x, y, c = _position()
        me, sibling = (x, y, c), (x, y, 1 - c)
        x_side, y_side, across = _other_chips(x, y)
        local = [pltpu.make_async_copy(srcs[a], outs[a].at[2 * x + y], local_sems.at[a]) for a in range(n)]
        for cp in local:
            cp.start()

        def copy(a, k, dst, to, src=None):
            return pltpu.make_async_remote_copy(
                src_ref=dst if src is None else src, dst_ref=dst, send_sem=send_sems.at[per * a + k],
                recv_sem=recv_sems.at[per * a + k], device_id=to, device_id_type=MESH_ID)

        def half(a, chip, pc):
            return _half(outs[a].at[2 * chip[0] + chip[1]], pc, axes[a])

        def quarter(a, chip, q):
            return _quarter(outs[a].at[2 * chip[0] + chip[1]], c, q, axes[a])

        sends = []
        for a in range(n):
            mine = _half(srcs[a], c, axes[a])
            targets = (x_side, y_side) if routed[a] else (x_side, y_side, across)
            sends += [copy(a, j, half(a, (x, y), c), (*chip, c), src=mine) for j, chip in enumerate(targets)]
        for cp in sends:
            cp.start()
        for a in range(n):
            for j, chip in enumerate((x_side, y_side)):
                copy(a, j, half(a, chip, c), me).wait_recv()
                if routed[a]:
                    other = (y_side, x_side)[j]
                    sends.append(copy(a, 2 + j, quarter(a, chip, j), (*other, c)))
                    sends[-1].start()
                sends.append(copy(a, 4 + j, half(a, chip, c), sibling))
                sends[-1].start()
        for a in range(n):
            if routed[a]:
                for j in range(2):
                    copy(a, 2 + j, quarter(a, across, j), me).wait_recv()
            else:
                copy(a, 2, half(a, across, c), me).wait_recv()
            sends.append(copy(a, 6, half(a, across, c), sibling))
            sends[-1].start()
        for a in range(n):
            for k, chip in ((4, x_side), (5, y_side), (6, across)):
                copy(a, k, half(a, chip, 1 - c), me).wait_recv()
        for cp in sends:
            cp.wait_send()
        for cp in local:
            cp.wait()

    return pl.pallas_call(
        body, name="first_gather",
        in_specs=[_any_spec()] * n, out_specs=[_any_spec()] * n,
        out_shape=[jax.ShapeDtypeStruct((N_CHIPS,) + s.shape, s.dtype) for s in shards],
        scratch_shapes=[pltpu.SemaphoreType.DMA((per * n,)), pltpu.SemaphoreType.DMA((per * n,)),
                        pltpu.SemaphoreType.DMA((n,))],
    )(*shards)


PAIR_PEERS, CHIP_PEERS = 1, 2


def _peers(which):
    x, y, c = _position()
    if which == PAIR_PEERS:
        return [(x, y, 1 - c)]
    return [(px, py, c) for px, py in _other_chips(x, y)]


def _split_start(name, arrays, n_copies, plan, peers=None):
    n = len(arrays)

    def body(*refs):
        ins, send_sems, recv_sems, token = refs[:n], refs[n], refs[n + 1], refs[-1]
        if peers is not None:
            barrier = pltpu.get_barrier_semaphore()
            targets = _peers(peers)
            for target in targets:
                pl.semaphore_signal(barrier, inc=1, device_id=target, device_id_type=MESH_ID)
            pl.semaphore_wait(barrier, len(targets))
        for k, (src, dst, to, _) in enumerate(plan(ins)):
            pltpu.make_async_remote_copy(src_ref=src, dst_ref=dst, send_sem=send_sems.at[k],
                                         recv_sem=recv_sems.at[k], device_id=to, device_id_type=MESH_ID).start()
        token[...] = jnp.zeros_like(token)

    hbm = pl.BlockSpec(memory_space=pltpu.HBM)
    sem = pl.BlockSpec(memory_space=pltpu.SEMAPHORE)
    out = pl.pallas_call(
        body, name=name,
        out_shape=(pltpu.SemaphoreType.DMA((n_copies,)), pltpu.SemaphoreType.DMA((n_copies,)))
        + tuple(pltpu.HBM(a.shape, a.dtype) for a in arrays) + (jax.ShapeDtypeStruct((8, 128), F32),),
        in_specs=[hbm] * n, out_specs=(sem, sem) + (hbm,) * n + (_vmem_spec(),),
        input_output_aliases={i: 2 + i for i in range(n)},
        compiler_params=pltpu.CompilerParams(has_side_effects=pltpu.SideEffectType.DATAFLOW_SIDE_EFFECTING,
                                             collective_id=peers),
    )(*[pltpu.with_memory_space_constraint(a, pltpu.HBM) for a in arrays])
    return (out[0], out[1], tuple(out[2:2 + n])), out[-1]


def _split_wait(name, handle, n_copies, plan, after):
    send_sems, recv_sems, arrays = handle
    n = len(arrays)

    def body(*refs):
        ins, s_sems, r_sems = refs[:n], refs[n], refs[n + 1]
        for k, (src, dst, to, landed) in enumerate(plan(ins)):
            cp = pltpu.make_async_remote_copy(src_ref=src, dst_ref=landed, send_sem=s_sems.at[k],
                                              recv_sem=r_sems.at[k], device_id=to, device_id_type=MESH_ID)
            cp.wait_send()
            cp.wait_recv()

    hbm = pl.BlockSpec(memory_space=pltpu.HBM)
    sem = pl.BlockSpec(memory_space=pltpu.SEMAPHORE)
    out = pl.pallas_call(
        body, name=name,
        out_shape=tuple(pltpu.HBM(a.shape, a.dtype) for a in arrays),
        in_specs=[hbm] * n + [sem, sem, _any_spec()], out_specs=(hbm,) * n,
        input_output_aliases={i: i for i in range(n)},
        compiler_params=pltpu.CompilerParams(has_side_effects=pltpu.SideEffectType.DATAFLOW_SIDE_EFFECTING),
    )(*arrays, send_sems, recv_sems, after)
    return tuple(out)


def _gather_plans(axes):
    n = len(axes)

    def stage_one(refs):
        x, y, c = _position()
        copies = []
        for a, axis in enumerate(axes):
            for px, py in _other_chips(x, y):
                copies.append((_half(refs[a], c, axis), _half(refs[n + a].at[2 * x + y], c, axis),
                               (px, py, c), _half(refs[n + a].at[2 * px + py], c, axis)))
        return copies

    def stage_two(refs):
        x, y, c = _position()
        copies = []
        for a, axis in enumerate(axes):
            for px, py in _other_chips(x, y):
                piece = _half(refs[n + a].at[2 * px + py], c, axis)
                copies.append((piece, piece, (x, y, 1 - c), _half(refs[n + a].at[2 * px + py], 1 - c, axis)))
        return copies

    return stage_one, stage_two


def _pair_swap_plan(axes):
    n = len(axes)

    def plan(refs):
        x, y, c = _position()
        return [(_half(refs[a], 1 - c, axes[a]), refs[n + a], (x, y, 1 - c), refs[n + a]) for a in range(n)]

    return plan


def _chip_swap_plan(n):
    def plan(refs):
        x, y, c = _position()
        copies = []
        for a in range(n):
            for j, (px, py) in enumerate(_other_chips(x, y)):
                copies.append((refs[a].at[2 * px + py], refs[n + a].at[j], (px, py, c), refs[n + a].at[j]))
        return copies

    return plan


def _pair_join_plan(axes):
    def plan(refs):
        x, y, c = _position()
        copies = []
        for a, axis in enumerate(axes):
            mine = _half(refs[a], c, axis)
            copies.append((mine, mine, (x, y, 1 - c), _half(refs[a], 1 - c, axis)))
        return copies

    return plan


def _pair_add_call(gs, gots, pos, name, axes):
    n = len(gs)

    def body(pos_r, *refs):
        for g_r, got_r, o_r in zip(refs[:n], refs[n:2 * n], refs[2 * n:]):
            o_r[...] = (g_r[...] + got_r[...]).astype(o_r.dtype)

    def mine(axis):
        return (lambda j, p: (j, p[1], 0)) if axis == ROWS else (lambda j, p: (j, 0, p[1]))

    blocks = [(None,) + got.shape[1:] for got in gots]
    return pl.pallas_call(
        body, name=name,
        grid_spec=pltpu.PrefetchScalarGridSpec(
            num_scalar_prefetch=1, grid=(N_CHIPS,),
            in_specs=[pl.BlockSpec(blk, mine(axis)) for blk, axis in zip(blocks, axes)]
            + [pl.BlockSpec(blk, lambda j, p: (j, 0, 0)) for blk in blocks],
            out_specs=[pl.BlockSpec(blk, lambda j, p: (j, 0, 0)) for blk in blocks]),
        out_shape=[jax.ShapeDtypeStruct(got.shape, COMM_DTYPE) for got in gots],
        compiler_params=_params(("arbitrary",), VMEM_BIG),
    )(pos, *gs, *gots)


def _chip_add_call(hsums, gots, pos, name, axes):
    n = len(hsums)
    steps = 2

    def body(pos_r, *refs):
        for own_r, got_r, o_r in zip(refs[:n], refs[n:2 * n], refs[2 * n:]):
            acc = own_r[...].astype(F32)
            for j in range(3):
                acc = acc + got_r[j].astype(F32)
            o_r[...] = acc

    in_specs, got_specs, out_specs, out_shape = [], [], [], []
    for h, axis in zip(hsums, axes):
        if axis == ROWS:
            rows, cols = h.shape[1] // steps, h.shape[2]
            in_specs.append(pl.BlockSpec((None, rows, cols), lambda i, p: (p[0], i, 0)))
            got_specs.append(pl.BlockSpec((3, rows, cols), lambda i, p: (0, i, 0)))
            out_specs.append(pl.BlockSpec((rows, cols), lambda i, p: (p[1] * steps + i, 0)))
            out_shape.append(jax.ShapeDtypeStruct((2 * h.shape[1], cols), F32))
        else:
            rows, cols = h.shape[1], h.shape[2] // steps
            in_specs.append(pl.BlockSpec((None, rows, cols), lambda i, p: (p[0], 0, i)))
            got_specs.append(pl.BlockSpec((3, rows, cols), lambda i, p: (0, 0, i)))
            out_specs.append(pl.BlockSpec((rows, cols), lambda i, p: (0, p[1] * steps + i)))
            out_shape.append(jax.ShapeDtypeStruct((rows, 2 * h.shape[2]), F32))
    return pl.pallas_call(
        body, name=name,
        grid_spec=pltpu.PrefetchScalarGridSpec(
            num_scalar_prefetch=1, grid=(steps,), in_specs=in_specs + got_specs, out_specs=out_specs),
        out_shape=out_shape,
        compiler_params=_params(("arbitrary",), VMEM_BIG),
    )(pos, *hsums, *gots)


SMALL_NAMES = ("norm_mix_pre", "norm_mix_post", "norm_mlp_pre", "norm_mlp_post", "b_gate_fwd", "b_gate_bwd",
               "gla_norm", "swa_sink", "rel_bias")


N_DEVICES = 8


def _small_pack_call(grads, extras):
    operands = list(grads) + list(extras)

    def body(*refs):
        g_refs, (all_a, all_b) = refs[:len(operands)], refs[len(operands):]
        x, y, c = _position()
        me = 4 * x + 2 * y + c
        all_a[me] = jnp.zeros(all_a.shape[1:], F32)
        all_b[me] = jnp.zeros(all_b.shape[1:], F32)
        for i in range(4):
            all_a[me, i:i + 1, :] = g_refs[i][...]
        all_a[me, 4:5, 0:256] = g_refs[4][...]
        all_a[me, 5:6, 0:256] = g_refs[5][...]
        all_a[me, 6:7, 0:128] = g_refs[6][...]
        all_a[me, 7:8, 0:128] = g_refs[7][...]
        all_a[me, 7:8, 128:256] = g_refs[11][...]
        all_b[me, 0:32, 0:128] = g_refs[8][...]
        all_b[me, 32:48, :] = g_refs[9][...]
        all_b[me, 48:64, :] = g_refs[10][...]

    out_shape = [jax.ShapeDtypeStruct((N_DEVICES, 8, D_MODEL), F32), jax.ShapeDtypeStruct((N_DEVICES, 64, 256), F32)]
    return pl.pallas_call(
        body, name="small_pack",
        in_specs=[_whole_spec(a.shape) for a in operands], out_specs=[_whole_spec(s.shape) for s in out_shape],
        out_shape=out_shape,
    )(*operands)


def _everyone_plan(n):
    def plan(refs):
        x, y, c = _position()
        copies = []
        for k in range(1, N_DEVICES):
            px = 1 - x if (k >> 2) & 1 else x
            py = 1 - y if (k >> 1) & 1 else y
            pc = 1 - c if k & 1 else c
            for a in range(n):
                mine = refs[a].at[4 * x + 2 * y + c]
                copies.append((mine, mine, (px, py, pc), refs[a].at[4 * px + 2 * py + pc]))
        return copies

    return plan


def _small_adamw_call(all_a, all_b, params):
    n_small = len(SMALL_NAMES)
    wmv = [t for p in params for t in p]
    shapes = [p[0].shape for p in params]

    def body(*refs):
        all_a, all_b = refs[:2]
        wmv_refs = refs[2:2 + 3 * n_small]
        out_refs = refs[2 + 3 * n_small:]
        sum_a, sum_b = all_a[0], all_b[0]
        for d in range(1, N_DEVICES):
            sum_a = sum_a + all_a[d]
            sum_b = sum_b + all_b[d]
        gsum = [sum_a[0:1], sum_a[1:2], sum_a[2:3], sum_a[3:4], sum_a[4:5, 0:256], sum_a[5:6, 0:256],
                sum_a[6:7, 0:128], sum_a[7:8, 0:SWA_Q_HEADS], sum_b[0:32, 0:SWA_Q_HEADS]]
        for i in range(n_small):
            w_r, m_r, v_r = wmv_refs[3 * i:3 * i + 3]
            delta, new_m, new_v = _adamw_math(w_r[...], gsum[i], m_r[...], v_r[...])
            out_refs[4 * i][...] = gsum[i]
            out_refs[4 * i + 1][...] = delta
            out_refs[4 * i + 2][...] = new_m
            out_refs[4 * i + 3][...] = new_v
        out_refs[4 * n_small][...] = sum_b[32:48]
        out_refs[4 * n_small + 1][...] = sum_b[48:64]
        out_refs[4 * n_small + 2][...] = sum_a[7:8, 128:256]

    out_shape = [jax.ShapeDtypeStruct(s, F32) for s in shapes for _ in range(4)]
    out_shape += [jax.ShapeDtypeStruct((GLA_GATE_RANK, 256), F32)] * 2 + [jax.ShapeDtypeStruct((1, 128), F32)]
    out = pl.pallas_call(
        body, name="small_adamw",
        in_specs=[_whole_spec(a.shape) for a in [all_a, all_b] + wmv],
        out_specs=[_whole_spec(s.shape) for s in out_shape],
        out_shape=out_shape,
    )(all_a, all_b, *wmv)
    per_name = [tuple(out[4 * i:4 * i + 4]) for i in range(n_small)]
    return per_name, out[4 * n_small], out[4 * n_small + 1], out[4 * n_small + 2]


def _pad_gate(w, first_row):
    return jnp.pad(w, ((first_row, 128 - GLA_GATE_RANK - first_row), (0, 0)))


def _own_slot(shard, chip):
    zone = lax.empty((N_CHIPS,) + shard.shape, shard.dtype)
    return lax.dynamic_update_slice(zone, shard[None], (chip,) + (0,) * shard.ndim)


def _reduce_to_owners(grads, axes, pos, tag, overlap):
    n = len(grads)

    def half_shape(g, axis):
        return (N_CHIPS, g.shape[1] // 2, g.shape[2]) if axis == ROWS else (N_CHIPS, g.shape[1], g.shape[2] // 2)

    lands = [lax.empty(half_shape(g, axis), F32) for g, axis in zip(grads, axes)]
    handle, token = _split_start(tag + "_pair_start", list(grads) + lands, n, _pair_swap_plan(axes), PAIR_PEERS)
    got = _split_wait(tag + "_pair_wait", handle, n, _pair_swap_plan(axes), overlap[0](token))
    sums = list(_pair_add_call(got[:n], got[n:], pos, tag + "_pair_add", axes))
    lands = [lax.empty((3,) + s.shape[1:], s.dtype) for s in sums]
    handle, token = _split_start(tag + "_chip_start", sums + lands, 3 * n, _chip_swap_plan(n), CHIP_PEERS)
    got = _split_wait(tag + "_chip_wait", handle, 3 * n, _chip_swap_plan(n), overlap[1](token))
    halves = list(_chip_add_call(got[:n], got[n:], pos, tag + "_chip_add", axes))
    handle, token = _split_start(tag + "_join_start", halves, n, _pair_join_plan(axes), PAIR_PEERS)
    return _split_wait(tag + "_join_wait", handle, n, _pair_join_plan(axes), overlap[2](token))


def kernel(x, norm_mix_pre, w_in, w_gate_up_fwd, b_gate_fwd, w_gate_up_bwd, b_gate_bwd, gla_norm, swa_sink, rel_bias, w_out, norm_mix_post, norm_mlp_pre, w_up, w_down, norm_mlp_post, loss_target, m_norm_mix_pre, m_w_in, m_w_gate_up_fwd, m_b_gate_fwd, m_w_gate_up_bwd, m_b_gate_bwd, m_gla_norm, m_swa_sink, m_rel_bias, m_w_out, m_norm_mix_post, m_norm_mlp_pre, m_w_up, m_w_down, m_norm_mlp_post, v_norm_mix_pre, v_w_in, v_w_gate_up_fwd, v_b_gate_fwd, v_w_gate_up_bwd, v_b_gate_bwd, v_gla_norm, v_swa_sink, v_rel_bias, v_w_out, v_norm_mix_post, v_norm_mlp_pre, v_w_up, v_w_down, v_norm_mlp_post):
    given = dict(locals())
    cx, cy, cc = _position()
    chip = (2 * cx + cy).astype(jnp.int32)
    pos = jnp.stack([chip, cc.astype(jnp.int32)])
    seq, tgt = x[0], loss_target[0]
    L = seq.shape[0]

    gates = jnp.concatenate([w_gate_up_fwd[0], w_gate_up_bwd[0]], axis=0).astype(COMM_DTYPE)
    all_in, all_gates = _first_gather_call([w_in[0].T.astype(COMM_DTYPE), gates], [COLS, ROWS], [True, False])
    rest = [w_out[0].astype(COMM_DTYPE), jnp.stack([w_up[0], w_down[0]]).astype(COMM_DTYPE)]
    stage_one, stage_two = _gather_plans([ROWS, ROWS])
    handle, token = _split_start("gather_chip_start", rest + [_own_slot(s, chip) for s in rest] + [all_gates], 6,
                                 stage_one, CHIP_PEERS)

    w_in_t = _mx(all_in.reshape(IN_COLS, D_MODEL))
    gates_full = jnp.concatenate([all_gates[j] for j in range(N_CHIPS)], axis=1)
    wgf_p = _mx(_pad_gate(gates_full[:GLA_GATE_RANK], 0))
    wgb_p = _mx(_pad_gate(gates_full[GLA_GATE_RANK:], GLA_GATE_RANK))
    bf_p, bb_p = b_gate_fwd, b_gate_bwd
    buckets = jnp.asarray(_band_buckets())
    sink1 = swa_sink.reshape(SWA_Q_HEADS)

    qa, ka, va, ga, qs, ks, vs, za = _proj_call(seq, norm_mix_pre, w_in_t, dep=token)
    halo = ((SWA_BLOCK, SWA_BLOCK), (0, 0))
    ks_p, vs_p = jnp.pad(ks, halo), jnp.pad(vs, halo)
    o_f, o_b, s_f, s_b = _gla_fwd_call(qa, ka, va, za, wgf_p, bf_p, wgb_p, bb_p)
    bias = _bias_call(rel_bias, buckets, dep=o_f)
    arrays = _split_wait("gather_chip_wait", handle, 6, stage_one, bias)
    handle, token = _split_start("gather_pair_start", list(arrays), 6, stage_two, PAIR_PEERS)
    o_s = _swa_fwd_call(qs, ks_p, vs_p, bias, sink1, dep=token)
    arrays = _split_wait("gather_pair_wait", handle, 6, stage_two, o_s)
    w_out_full = _mx(arrays[2].reshape(N_CHIPS * R_OUT, D_MODEL))
    w_ud = _mx(arrays[3])
    cat, mix, h1, n2 = _mix_call(o_f, o_b, ga, o_s, seq, gla_norm, w_out_full, norm_mix_post, norm_mlp_pre)
    a, rz, dh2, dff, loss, d_post2 = _mlp_fwd_call(n2, h1, tgt, w_ud, norm_mlp_post)

    dz, dn2 = _mlp_bwd_call(dff, rz, w_ud)
    dw_down, dw_up4 = _mlp_wgrad_call(a, dff, n2, dz)
    dh1, do, dga, dos, dw_out, d_pre2, d_post, d_gn = _mix_bwd_call(
        dn2, dh2, h1, mix, cat, o_f, o_b, ga, gla_norm, norm_mix_post, norm_mlp_pre, w_out_full)
    done = {}

    def swa_backward(tok):
        done["swa"] = _swa_bwd_call(qs, ks_p, vs_p, bias, sink1, dos, dep=tok)
        return done["swa"][0]

    def gla_in_backward(tok):
        done["gla"] = _gla_bwd_call(qa, ka, va, za, do, s_f, s_b, wgf_p, bf_p, wgb_p, bb_p, dep=tok)
        dqf, dkf, dvf, dzf, _, _, dqb, dkb, dvb, dzb, _, _ = done["gla"]
        dqs, dks_p, dvs_p, _, _ = done["swa"]
        done["in"] = _in_bwd_call(
            seq, dh1, norm_mix_pre, w_in_t,
            pairs=[(_side_by_side(T_QA), (dqf, dqb)), (_side_by_side(T_KA), (dkf, dkb)), (T_VA, (dvf, dvb)),
                   (T_ZA, (dzf, dzb))],
            singles=[(T_GA, dga), (_side_by_side(T_QS), dqs)], halos=[(T_KS, dks_p), (T_VS, dvs_p)])
        return done["in"][0]

    def bias_backward(tok):
        done["rel"] = _relbias_call(done["swa"][3], done["swa"][4], buckets, dep=tok)
        return done["rel"][0]

    g_up, g_down, g_out = _reduce_to_owners(
        [dw_up4, dw_down.reshape(N_CHIPS, R_DOWN, D_MODEL), dw_out.reshape(N_CHIPS, R_OUT, D_MODEL)],
        [ROWS, ROWS, ROWS], pos, "mlp", [swa_backward, gla_in_backward, bias_backward])
    dx, dw_in_t, d_pre = done["in"]
    dwf, dbf, dwb, dbb = done["gla"][4], done["gla"][5], done["gla"][10], done["gla"][11]
    drel, dsink = done["rel"]

    small_grads = [d_pre, d_post, d_pre2, d_post2, dbf, dbb, d_gn, dsink, drel]
    gate_grads = [dwf[:GLA_GATE_RANK], dwb[GLA_GATE_RANK:2 * GLA_GATE_RANK]]
    small_params = [(given[n], given["m_" + n], given["v_" + n]) for n in SMALL_NAMES]
    upd = {}

    everyone = _everyone_plan(2)
    small_handle, small_token = _split_start(
        "small_start", list(_small_pack_call(small_grads, gate_grads + [loss])), 2 * (N_DEVICES - 1), everyone)

    def update_out(tok):
        upd["w_out"] = (g_out,) + tuple(_adamw_call(w_out[0], g_out, m_w_out[0], v_w_out[0], "adamw_w_out",
                                                    dep=tok + small_token))
        return upd["w_out"][1]

    def update_mlp(tok):
        upd["w_up"] = (g_up,) + tuple(_adamw_call(w_up[0], g_up, m_w_up[0], v_w_up[0], "adamw_w_up", dep=tok))
        upd["w_down"] = (g_down,) + tuple(
            _adamw_call(w_down[0], g_down, m_w_down[0], v_w_down[0], "adamw_w_down", dep=upd["w_up"][1]))
        all_a, all_b = _split_wait("small_wait", small_handle, 2 * (N_DEVICES - 1), everyone, upd["w_down"][1])
        per_name, done["gf_sum"], done["gb_sum"], upd["loss"] = _small_adamw_call(all_a, all_b, small_params)
        upd.update(dict(zip(SMALL_NAMES, per_name)))
        return per_name[0][1]

    def update_gates(tok):
        for name, total in (("w_gate_up_fwd", done["gf_sum"]), ("w_gate_up_bwd", done["gb_sum"])):
            g = lax.dynamic_slice(total, (0, chip * 64), (GLA_GATE_RANK, 64))
            upd[name] = (g,) + tuple(_adamw_call(given[name][0], g, given["m_" + name][0], given["v_" + name][0],
                                                 "adamw_" + name, dep=tok))
        return upd["w_gate_up_bwd"][1]

    (g_in_t,) = _reduce_to_owners([dw_in_t.reshape(N_CHIPS, R_IN, D_MODEL)], [COLS], pos, "in",
                                  [update_out, update_mlp, update_gates])
    in_t = (g_in_t,) + tuple(_adamw_call(w_in[0].T, g_in_t, m_w_in[0].T, v_w_in[0].T, "adamw_w_in"))
    upd["w_in"] = tuple(t.T for t in in_t)

    big = ("w_in", "w_gate_up_fwd", "w_gate_up_bwd", "w_out", "w_up", "w_down")
    names = ["norm_mix_pre", "w_in", "w_gate_up_fwd", "b_gate_fwd", "w_gate_up_bwd", "b_gate_bwd", "gla_norm",
             "swa_sink", "rel_bias", "w_out", "norm_mix_post", "norm_mlp_pre", "w_up", "w_down", "norm_mlp_post"]
    outs = [upd["loss"][0, 0], dx[None]]
    for kind in range(4):
        outs += [upd[n][kind][None] if n in big else upd[n][kind] for n in names]
    return tuple(outs)
```

```python
import math

import numpy as np
import jax
import jax.numpy as jnp
from jax import lax
from jax.experimental import pallas as pl
from jax.experimental.pallas import tpu as pltpu

F32 = jnp.float32
MXU_DTYPE = jnp.bfloat16
COMM_DTYPE = jnp.bfloat16

D_MODEL = 1024
D_FF = 4096
N_CHIPS = 4
GLA_HEADS = 4
GLA_CHUNK = 64
GLA_GATE_RANK = 16
GLA_GATE_NORM = 16.0
SWA_Q_HEADS = 8
SWA_KV_HEADS = 2
SWA_BLOCK = 128
REL_BUCKETS = 32
REL_MAX_DIST = 128
NORM_EPS = 1e-6
HEAD_PAD = 128

ADAM_LR = 0.001
ADAM_B1 = 0.9
ADAM_B2 = 0.999
ADAM_EPS = 1e-08
ADAM_WD = 0.01
ADAM_STEP = 10

OUT_PAD = 1024

R_IN, R_OUT, R_DOWN = 584, 256, 1024

VMEM_BIG = 56 * 1024 * 1024
MESH_ID = pl.DeviceIdType.MESH


def _mx(a):
    return a.astype(MXU_DTYPE)


def _dot(a, b):
    return jnp.dot(a, b, preferred_element_type=F32)


def _dot_nt(a, b):
    return lax.dot_general(a, b, (((1,), (1,)), ((), ())), preferred_element_type=F32)


def _dot_tn(a, b):
    return lax.dot_general(a, b, (((0,), (0,)), ((), ())), preferred_element_type=F32)


def _rms_r(x):
    return lax.rsqrt(jnp.mean(x * x, axis=-1, keepdims=True) + NORM_EPS)


def _rms_bwd(x, r, g, dy):
    xh = x * r
    gdy = dy * g
    dx = r * (gdy - xh * jnp.mean(gdy * xh, axis=-1, keepdims=True))
    return dx, jnp.sum(dy * xh, axis=0, keepdims=True)


def _low_half(rows):
    return lax.broadcasted_iota(jnp.int32, (rows, HEAD_PAD), 1) < 64


def _spread_heads(x):
    low = _low_half(x.shape[0])
    parts = []
    for p in range(x.shape[1] // HEAD_PAD):
        pair = x[:, HEAD_PAD * p:HEAD_PAD * (p + 1)]
        parts += [jnp.where(low, pair, 0.0), jnp.where(low, pltpu.roll(pair, 64, 1), 0.0)]
    return jnp.concatenate(parts, axis=1)


def _squeeze_heads(x):
    low = _low_half(x.shape[0])
    parts = []
    for p in range(x.shape[1] // (2 * HEAD_PAD)):
        even = x[:, 2 * HEAD_PAD * p:2 * HEAD_PAD * p + HEAD_PAD]
        odd = x[:, 2 * HEAD_PAD * p + HEAD_PAD:2 * HEAD_PAD * (p + 1)]
        parts.append(jnp.where(low, even, pltpu.roll(odd, 64, 1)))
    return parts[0] if len(parts) == 1 else jnp.concatenate(parts, axis=1)


def _params(sem=None, vmem=None):
    kw = {}
    if sem is not None:
        kw["dimension_semantics"] = sem
    if vmem is not None:
        kw["vmem_limit_bytes"] = vmem
    return pltpu.CompilerParams(**kw)


def _vmem_spec():
    return pl.BlockSpec(memory_space=pltpu.VMEM)


def _whole_spec(shape):
    return pl.BlockSpec(shape, lambda: (0,) * len(shape))


def _row_spec(tm, width):
    return pl.BlockSpec((tm, width), lambda i: (i, 0))


def _full_spec(shape):
    return pl.BlockSpec(shape, lambda i: (0,) * len(shape))


def _any_spec():
    return pl.BlockSpec(memory_space=pl.ANY)


def _after(body, n_in, dep):
    if dep is None:
        return body, [], []
    return (lambda *refs: body(*refs[:n_in], *refs[n_in + 1:])), [dep], [_any_spec()]


T_QA, T_KA, T_VA, T_GA = (0, 256, 4), (256, 256, 4), (512, 512, 0), (1024, 512, 0)
T_QS, T_KS, T_VS = (1568, 512, 8), (2080, 128, 2), (2208, 128, 2)
T_ZA = (1536, 128, 0)
ZA_COLS = 2 * GLA_GATE_RANK
IN_COLS = 2336


def _side_by_side(group):
    return group[0], group[1], 0


def _proj_call(x, g_pre, w_in_t, dep=None):
    L = x.shape[0]
    tm = min(512, L)
    groups = [(T_QA, F32), (T_KA, F32), (T_VA, MXU_DTYPE), (T_GA, F32),
              (T_QS, MXU_DTYPE), (T_KS, MXU_DTYPE), (T_VS, MXU_DTYPE), (T_ZA, F32)]
    widths = [rows * (2 if heads else 1) for (_, rows, heads), _ in groups]

    def body(x_ref, g_ref, w_ref, *outs):
        xv = x_ref[...]
        u = _mx(xv * _rms_r(xv) * g_ref[...])
        for ref, (grp, _) in zip(outs, groups):
            first, rows, heads = grp
            val = _dot_nt(u, w_ref[first:first + rows, :])
            if heads:
                val = _spread_heads(val)
            if grp is T_ZA:
                val = jnp.where(lax.broadcasted_iota(jnp.int32, val.shape, 1) < ZA_COLS, val, 0.0)
            if grp is T_QS:
                val = val * 0.125
            ref[...] = val.astype(ref.dtype)

    body, extra, extra_specs = _after(body, 3, dep)
    return pl.pallas_call(
        body, name="proj_fwd", grid=(L // tm,),
        in_specs=[_row_spec(tm, D_MODEL), _full_spec((1, D_MODEL)), _vmem_spec()] + extra_specs,
        out_specs=[_row_spec(tm, w) for w in widths],
        out_shape=[jax.ShapeDtypeStruct((L, w), dt) for w, (_, dt) in zip(widths, groups)],
        compiler_params=_params(("arbitrary",), VMEM_BIG),
    )(x, g_pre, w_in_t, *extra)


def _tri_masks():
    row = lax.broadcasted_iota(jnp.int32, (GLA_CHUNK, GLA_CHUNK), 0)
    col = lax.broadcasted_iota(jnp.int32, (GLA_CHUNK, GLA_CHUNK), 1)
    return row >= col, row <= col


def _chunk_sums(tri_m, x):
    hi = _mx(x)
    rest = x - hi.astype(F32)
    mid = _mx(rest)
    lo = _mx(rest - mid.astype(F32))
    return _dot(tri_m, hi) + _dot(tri_m, mid) + _dot(tri_m, lo)


def _gla_block_pre(q_r, k_r, z_r, w_r, b_r, rev, nc, qd_s, ki_s, ks_s, dec_s, keep=None):
    tri_f, tri_b = _tri_masks()
    tri_m = _mx((tri_b if rev else tri_f).astype(F32))
    g = _dot(_mx(z_r[...]), w_r[...]) + b_r[...]
    la = (jnp.minimum(g, 0.0) - jnp.log(1.0 + jnp.exp(-jnp.abs(g)))) * (1.0 / GLA_GATE_NORM)
    sums, lasts = [], []
    for c in range(nc):
        b_c = _chunk_sums(tri_m, la[GLA_CHUNK * c:GLA_CHUNK * (c + 1)])
        blast = b_c[0:1] if rev else b_c[GLA_CHUNK - 1:GLA_CHUNK]
        dec_s[c] = _spread_heads(jnp.exp(blast))
        sums.append(b_c)
        lasts.append(jnp.broadcast_to(blast, b_c.shape))
    b = jnp.concatenate(sums, axis=0)
    eb = jnp.exp(b)
    enb = jnp.exp(-b)
    elb = jnp.exp(jnp.concatenate(lasts, axis=0) - b)
    q, k = _squeeze_heads(q_r[...]), _squeeze_heads(k_r[...])
    qd_s[...] = _spread_heads(q * 0.125 * eb).astype(qd_s.dtype)
    ki_s[...] = _spread_heads(k * enb).astype(ki_s.dtype)
    ks_s[...] = _spread_heads(k * elb).astype(ks_s.dtype)
    if keep is not None:
        keep[0][...] = g
        for ref, val in zip(keep[1:], (eb, enb, elb)):
            ref[...] = _spread_heads(val)


def _gla_fwd_call(qa, ka, va, za, wgf, bgf, wgb, bgb):
    L = qa.shape[0]
    br = min(512, L)
    nb, nc, n_chunks = L // br, br // GLA_CHUNK, L // GLA_CHUNK
    hw = GLA_HEADS * HEAD_PAD

    def body(qaf, kaf, vaf, zaf, qab, kab, vab, zab, wgf_r, bgf_r, wgb_r, bgb_r,
             of_r, ob_r, sf_r, sb_r, st_f, st_b, pre_f, pre_b):
        @pl.when(pl.program_id(0) == 0)
        def _():
            st_f[...] = jnp.zeros_like(st_f)
            st_b[...] = jnp.zeros_like(st_b)

        _gla_block_pre(qaf, kaf, zaf, wgf_r, bgf_r, False, nc, *pre_f)
        _gla_block_pre(qab, kab, zab, wgb_r, bgb_r, True, nc, *pre_b)
        tri_f, tri_b = _tri_masks()

        def one(tri, pre, v_r, o_r, s_r, st, ci):
            qd_s, ki_s, ks_s, dec_s = pre
            rows = pl.ds(pl.multiple_of(ci * GLA_CHUNK, GLA_CHUNK), GLA_CHUNK)
            dec = dec_s[ci]
            heads = range(GLA_HEADS)
            lanes = [slice(HEAD_PAD * h, HEAD_PAD * (h + 1)) for h in heads]
            qd = [qd_s[rows, sl] for sl in lanes]
            v = [v_r[rows, sl] for sl in lanes]
            s_t = [st[h] for h in heads]
            a = [_dot_nt(qd[h], ki_s[rows, lanes[h]]) for h in heads]
            carried = [_dot_nt(qd[h], _mx(s_t[h])) for h in heads]
            grown = [_dot_tn(v[h], ks_s[rows, lanes[h]]) for h in heads]
            a = [_mx(jnp.where(tri, a[h], 0.0)) for h in heads]
            inner = [_dot(a[h], v[h]) for h in heads]
            for h in heads:
                s_r[ci, h] = s_t[h].astype(s_r.dtype)
                o_r[rows, lanes[h]] = inner[h] + carried[h]
                st[h] = s_t[h] * dec[:, lanes[h]] + grown[h]

        def loop(t, carry):
            one(tri_f, pre_f, vaf, of_r, sf_r, st_f, t)
            one(tri_b, pre_b, vab, ob_r, sb_r, st_b, nc - 1 - t)
            return carry

        lax.fori_loop(0, nc, loop, 0, unroll=True)

    fwd = lambda i: (i, 0)
    bwd = lambda i: (nb - 1 - i, 0)
    ins = lambda m: [pl.BlockSpec((br, hw), m), pl.BlockSpec((br, hw), m),
                     pl.BlockSpec((br, hw), m), pl.BlockSpec((br, 128), m)]
    wspecs = [_full_spec((128, hw // 2)), _full_spec((1, hw // 2))] * 2
    s_shape = (nc, GLA_HEADS, HEAD_PAD, HEAD_PAD)
    pre_scratch = [pltpu.VMEM((br, hw), MXU_DTYPE)] * 3 + [pltpu.VMEM((nc, 1, hw), F32)]
    return pl.pallas_call(
        body, name="gla_fwd", grid=(nb,),
        in_specs=ins(fwd) + ins(bwd) + wspecs,
        out_specs=[pl.BlockSpec((br, hw), fwd), pl.BlockSpec((br, hw), bwd),
                   pl.BlockSpec(s_shape, lambda i: (i, 0, 0, 0)),
                   pl.BlockSpec(s_shape, lambda i: (nb - 1 - i, 0, 0, 0))],
        out_shape=[jax.ShapeDtypeStruct((L, hw), F32), jax.ShapeDtypeStruct((L, hw), F32),
                   jax.ShapeDtypeStruct((n_chunks,) + s_shape[1:], MXU_DTYPE),
                   jax.ShapeDtypeStruct((n_chunks,) + s_shape[1:], MXU_DTYPE)],
        scratch_shapes=[pltpu.VMEM(s_shape[1:], F32), pltpu.VMEM(s_shape[1:], F32), pre_scratch, pre_scratch],
        compiler_params=_params(("arbitrary",), VMEM_BIG),
    )(qa, ka, va, za, qa, ka, va, za, wgf, bgf, wgb, bgb)


def _gla_bwd_call(qa, ka, va, za, do, sf, sb, wgf, bgf, wgb, bgb, dep=None):
    L = qa.shape[0]
    br = min(512, L)
    nb, nc = L // br, br // GLA_CHUNK
    hw = GLA_HEADS * HEAD_PAD

    def body(qaf, kaf, vaf, zaf, dof, sf_r, qab, kab, vab, zab, dob, sb_r, wgf_r, bgf_r, wgb_r, bgb_r,
             dqf, dkf, dvf, dzf, dwf, dbf, dqb, dkb, dvb, dzb, dwb, dbb, gt_f, gt_b, pre_f, pre_b):
        @pl.when(pl.program_id(0) == 0)
        def _():
            for ref in (gt_f, gt_b, dwf, dbf, dwb, dbb):
                ref[...] = jnp.zeros_like(ref)

        _gla_block_pre(qaf, kaf, zaf, wgf_r, bgf_r, False, nc, *pre_f[:4], keep=pre_f[4:8])
        _gla_block_pre(qab, kab, zab, wgb_r, bgb_r, True, nc, *pre_b[:4], keep=pre_b[4:8])
        tri_f, tri_b = _tri_masks()
        row_w = lax.broadcasted_iota(jnp.int32, (GLA_CHUNK, HEAD_PAD), 0)

        def one(rev, pre, q_r, k_r, v_r, do_r, s_r, dq_r, dk_r, dv_r, gt, ci):
            qd_s, ki_s, ks_s, dec_s, _, eb_s, enb_s, elb_s, db_s = pre
            tri = tri_b if rev else tri_f
            last_row = 0 if rev else GLA_CHUNK - 1
            rows = pl.ds(pl.multiple_of(ci * GLA_CHUNK, GLA_CHUNK), GLA_CHUNK)
            dec = dec_s[ci]
            heads = range(GLA_HEADS)
            lanes = [slice(HEAD_PAD * h, HEAD_PAD * (h + 1)) for h in heads]
            qd = [qd_s[rows, sl] for sl in lanes]
            ki = [ki_s[rows, sl] for sl in lanes]
            ks = [ks_s[rows, sl] for sl in lanes]
            v = [v_r[rows, sl] for sl in lanes]
            do_h = [_mx(do_r[rows, sl]) for sl in lanes]
            s_t = [s_r[ci, h] for h in heads]
            g_t = [gt[h] for h in heads]
            g_m = [_mx(g_t[h]) for h in heads]
            a = [_dot_nt(qd[h], ki[h]) for h in heads]
            da = [_dot_nt(do_h[h], v[h]) for h in heads]
            dv_carried = [_dot_nt(ks[h], g_m[h]) for h in heads]
            dqd_carried = [_dot(do_h[h], _mx(s_t[h])) for h in heads]
            dks = [_dot(v[h], g_m[h]) for h in heads]
            g_grown = [_dot_tn(do_h[h], qd[h]) for h in heads]
            a = [_mx(jnp.where(tri, a[h], 0.0)) for h in heads]
            da = [_mx(jnp.where(tri, da[h], 0.0)) for h in heads]
            dv_inner = [_dot_tn(a[h], do_h[h]) for h in heads]
            dqd_inner = [_dot(da[h], ki[h]) for h in heads]
            dki = [_dot_tn(da[h], qd[h]) for h in heads]
            dq, dk = [], []
            for h in heads:
                sl = lanes[h]
                dv_r[rows, sl] = (dv_inner[h] + dv_carried[h]).astype(dv_r.dtype)
                ddec = jnp.sum(g_t[h] * s_t[h].astype(F32), axis=0, keepdims=True)
                gt[h] = g_t[h] * dec[:, sl] + g_grown[h]
                dq.append((dqd_inner[h] + dqd_carried[h]) * eb_s[rows, sl] * 0.125)
                dk_state = dks[h] * elb_s[rows, sl]
                dk.append(dki[h] * enb_s[rows, sl] + dk_state)
                k = k_r[rows, sl]
                dblast = jnp.sum(dk_state * k, axis=0, keepdims=True) + dec[:, sl] * ddec
                db_s[rows, sl] = q_r[rows, sl] * dq[h] - k * dk[h] + jnp.where(row_w == last_row, dblast, 0.0)
            low = _low_half(GLA_CHUNK)
            for pair in range(GLA_HEADS // 2):
                psl = slice(HEAD_PAD * pair, HEAD_PAD * (pair + 1))
                for ref, val in ((dq_r, dq), (dk_r, dk)):
                    both = jnp.where(low, val[2 * pair], pltpu.roll(val[2 * pair + 1], 64, 1))
                    ref[rows, psl] = both.astype(ref.dtype)

        def loop(t, carry):
            one(False, pre_f, qaf, kaf, vaf, dof, sf_r, dqf, dkf, dvf, gt_f, nc - 1 - t)
            one(True, pre_b, qab, kab, vab, dob, sb_r, dqb, dkb, dvb, gt_b, t)
            return carry

        lax.fori_loop(0, nc, loop, 0, unroll=True)

        def gate_grads(rev, pre, z_r, w_r, dz_r, dw_r, dbias_r):
            g_s, db_s = pre[4], pre[8]
            back_m = _mx((tri_f if rev else tri_b).astype(F32))
            db = _squeeze_heads(db_s[...])
            dla = jnp.concatenate([_chunk_sums(back_m, db[GLA_CHUNK * c:GLA_CHUNK * (c + 1)]) for c in range(nc)],
                                  axis=0)
            dg = dla * (1.0 / GLA_GATE_NORM) * (1.0 / (1.0 + jnp.exp(g_s[...])))
            dg_m = _mx(dg)
            dz_r[...] = _dot_nt(dg_m, w_r[...])
            dw_r[...] += _dot_tn(_mx(z_r[...]), dg_m)
            dbias_r[...] += jnp.sum(dg, axis=0, keepdims=True)

        gate_grads(False, pre_f, zaf, wgf_r, dzf, dwf, dbf)
        gate_grads(True, pre_b, zab, wgb_r, dzb, dwb, dbb)

    last_first = lambda i: (nb - 1 - i, 0)
    first_last = lambda i: (i, 0)
    s_shape = (nc, GLA_HEADS, HEAD_PAD, HEAD_PAD)

    def ins(m):
        return [pl.BlockSpec((br, hw), m), pl.BlockSpec((br, hw), m), pl.BlockSpec((br, hw), m),
                pl.BlockSpec((br, 128), m), pl.BlockSpec((br, hw), m),
                pl.BlockSpec(s_shape, lambda i: m(i) + (0, 0))]

    def outs(m):
        return [pl.BlockSpec((br, hw // 2), m), pl.BlockSpec((br, hw // 2), m), pl.BlockSpec((br, hw), m),
                pl.BlockSpec((br, 128), m), _full_spec((128, hw // 2)), _full_spec((1, hw // 2))]

    out_shape = [jax.ShapeDtypeStruct((L, hw // 2), MXU_DTYPE)] * 2 + [
        jax.ShapeDtypeStruct((L, hw), MXU_DTYPE),
        jax.ShapeDtypeStruct((L, 128), F32), jax.ShapeDtypeStruct((128, hw // 2), F32),
        jax.ShapeDtypeStruct((1, hw // 2), F32)]
    wspecs = [_full_spec((128, hw // 2)), _full_spec((1, hw // 2))] * 2
    body, extra, extra_specs = _after(body, 16, dep)
    pre_scratch = ([pltpu.VMEM((br, hw), MXU_DTYPE)] * 3 + [pltpu.VMEM((nc, 1, hw), F32)]
                   + [pltpu.VMEM((br, hw // 2), F32)] + [pltpu.VMEM((br, hw), F32)] * 4)
    return pl.pallas_call(
        body, name="gla_bwd", grid=(nb,),
        in_specs=ins(last_first) + ins(first_last) + wspecs + extra_specs,
        out_specs=outs(last_first) + outs(first_last),
        out_shape=out_shape + out_shape,
        scratch_shapes=[pltpu.VMEM(s_shape[1:], F32), pltpu.VMEM(s_shape[1:], F32), pre_scratch, pre_scratch],
        compiler_params=_params(("arbitrary",), VMEM_BIG),
    )(qa, ka, va, za, do, sf, qa, ka, va, za, do, sb, wgf, bgf, wgb, bgb, *extra)


def _t5_buckets(rel):
    nb = REL_BUCKETS // 2
    ret = (rel > 0).astype(np.int32) * nb
    n = np.abs(rel)
    max_exact = nb // 2
    large = max_exact + (np.log(np.maximum(n, 1).astype(np.float32) / max_exact)
                         / math.log(REL_MAX_DIST / max_exact) * (nb - max_exact)).astype(np.int32)
    large = np.minimum(large, nb - 1)
    return ret + np.where(n < max_exact, n, large)


SWA_GROUP = SWA_Q_HEADS // SWA_KV_HEADS
SWA_SPAN = 3 * SWA_BLOCK
SWA_GROUP_LANES = SWA_GROUP * SWA_BLOCK


def _band_buckets():
    s = np.arange(SWA_SPAN)[:, None]
    c = np.arange(SWA_BLOCK)[None, :]
    return _t5_buckets(s - SWA_BLOCK - c).astype(np.int32)


def _swa_valid(n, seq_len):
    key_pos = (n - 1) * SWA_BLOCK + lax.broadcasted_iota(jnp.int32, (SWA_SPAN, 1), 0)
    return (key_pos >= 0) & (key_pos < seq_len)


def _swa_sink_row(sink_r, kv):
    lane = lax.broadcasted_iota(jnp.int32, (1, SWA_GROUP_LANES), 1)
    row = jnp.full((1, SWA_GROUP_LANES), sink_r[kv * SWA_GROUP], F32)
    for g in range(1, SWA_GROUP):
        row = jnp.where(lane >= g * SWA_BLOCK, sink_r[kv * SWA_GROUP + g], row)
    return row


SWA_STEP_BLOCKS = 4


def _swa_group(ref, kv, rows):
    first = kv * SWA_GROUP
    return jnp.concatenate([ref[rows, HEAD_PAD * h:HEAD_PAD * (h + 1)] for h in range(first, first + SWA_GROUP)],
                           axis=0)


def _swa_softmax(scores, bias_t, sink_row, valid):
    st = jnp.where(valid, scores + bias_t, -1e30)
    m = jnp.maximum(jnp.max(st, axis=0, keepdims=True), sink_row)
    p = jnp.exp(st - m)
    e_sink = jnp.exp(sink_row - m)
    inv = 1.0 / (jnp.sum(p, axis=0, keepdims=True) + e_sink)
    return p * inv, e_sink * inv


def _swa_fwd_call(qs, ks, vs, bias, sink, dep=None):
    L = qs.shape[0]

    def block(n, rows, q_r, k_r, v_r, bias_r, sink_r, o_r):
        span = pl.ds(pl.multiple_of(n * SWA_BLOCK, SWA_BLOCK), SWA_SPAN)
        valid = _swa_valid(n, L)
        groups = range(SWA_KV_HEADS)
        lanes = [slice(HEAD_PAD * kv, HEAD_PAD * (kv + 1)) for kv in groups]
        scores = [_dot_nt(k_r[span, lanes[kv]], _swa_group(q_r, kv, rows)) for kv in groups]
        probs = [_swa_softmax(scores[kv], bias_r[kv], _swa_sink_row(sink_r, kv), valid)[0] for kv in groups]
        low = _low_half(SWA_BLOCK)
        for kv in groups:
            og = _dot_tn(_mx(probs[kv]), v_r[span, lanes[kv]])
            for pair in range(SWA_GROUP // 2):
                even = og[2 * SWA_BLOCK * pair:2 * SWA_BLOCK * pair + SWA_BLOCK]
                odd = og[2 * SWA_BLOCK * pair + SWA_BLOCK:2 * SWA_BLOCK * (pair + 1)]
                first = HEAD_PAD * (kv * SWA_GROUP // 2 + pair)
                o_r[rows, first:first + HEAD_PAD] = jnp.where(low, even, pltpu.roll(odd, 64, 1)).astype(o_r.dtype)

    def body(*refs):
        for j in range(SWA_STEP_BLOCKS):
            block(SWA_STEP_BLOCKS * pl.program_id(0) + j, slice(SWA_BLOCK * j, SWA_BLOCK * (j + 1)), *refs)

    qw = SWA_Q_HEADS * HEAD_PAD
    tm = SWA_STEP_BLOCKS * SWA_BLOCK
    body, extra, extra_specs = _after(body, 5, dep)
    return pl.pallas_call(
        body, name="swa_fwd", grid=(L // tm,),
        in_specs=[_row_spec(tm, qw), _vmem_spec(), _vmem_spec(), _vmem_spec(),
                  pl.BlockSpec(memory_space=pltpu.SMEM)] + extra_specs,
        out_specs=_row_spec(tm, qw // 2),
        out_shape=jax.ShapeDtypeStruct((L, qw // 2), MXU_DTYPE),
        compiler_params=_params(("arbitrary",), VMEM_BIG),
    )(qs, ks, vs, bias, sink, *extra)


def _swa_bwd_call(qs, ks, vs, bias, sink, do, dep=None):
    L = qs.shape[0]
    qw = SWA_Q_HEADS * HEAD_PAD
    kw = SWA_KV_HEADS * HEAD_PAD

    def body(*refs):
        dk_r, dv_r, dbias_r, dsink_r = refs[7:]

        @pl.when(pl.program_id(0) == 0)
        def _():
            for ref in (dk_r, dv_r, dbias_r, dsink_r):
                ref[...] = jnp.zeros_like(ref)

        for j in range(SWA_STEP_BLOCKS):
            block(SWA_STEP_BLOCKS * pl.program_id(0) + j, slice(SWA_BLOCK * j, SWA_BLOCK * (j + 1)), *refs)

    def block(n, rows, q_r, k_r, v_r, bias_r, sink_r, do_r, dq_r, dk_r, dv_r, dbias_r, dsink_r):
        span = pl.ds(pl.multiple_of(n * SWA_BLOCK, SWA_BLOCK), SWA_SPAN)
        valid = _swa_valid(n, L)
        groups = range(SWA_KV_HEADS)
        lanes = [slice(HEAD_PAD * kv, HEAD_PAD * (kv + 1)) for kv in groups]
        kk = [k_r[span, sl] for sl in lanes]
        vv = [v_r[span, sl] for sl in lanes]
        qg = [_swa_group(q_r, kv, rows) for kv in groups]
        dog = [_swa_group(do_r, kv, rows) for kv in groups]
        scores = [_dot_nt(kk[kv], qg[kv]) for kv in groups]
        dp = [_dot_nt(vv[kv], dog[kv]) for kv in groups]
        probs = [_swa_softmax(scores[kv], bias_r[kv], _swa_sink_row(sink_r, kv), valid) for kv in groups]
        ds_m, pn_m = [], []
        for kv in groups:
            pn, p_sink = probs[kv]
            delta = jnp.sum(pn * dp[kv], axis=0, keepdims=True)
            ds = pn * (dp[kv] - delta)
            dsink_r[kv] -= p_sink * delta
            dbias_r[kv] += ds
            ds_m.append(_mx(ds))
            pn_m.append(_mx(pn))
        dqg = [_dot_tn(ds_m[kv], kk[kv]) * 0.125 for kv in groups]
        dkk = [_dot(ds_m[kv], qg[kv]) for kv in groups]
        dvv = [_dot(pn_m[kv], dog[kv]) for kv in groups]
        low = _low_half(SWA_BLOCK)
        for kv in groups:
            for pair in range(SWA_GROUP // 2):
                even = dqg[kv][2 * SWA_BLOCK * pair:2 * SWA_BLOCK * pair + SWA_BLOCK]
                odd = dqg[kv][2 * SWA_BLOCK * pair + SWA_BLOCK:2 * SWA_BLOCK * (pair + 1)]
                first = HEAD_PAD * (kv * SWA_GROUP // 2 + pair)
                dq_r[rows, first:first + HEAD_PAD] = jnp.where(low, even, pltpu.roll(odd, 64, 1)).astype(dq_r.dtype)
            dk_r[span, lanes[kv]] += dkk[kv]
            dv_r[span, lanes[kv]] += dvv[kv]

    tm = SWA_STEP_BLOCKS * SWA_BLOCK
    body, extra, extra_specs = _after(body, 6, dep)
    return pl.pallas_call(
        body, name="swa_bwd", grid=(L // tm,),
        in_specs=[_row_spec(tm, qw), _vmem_spec(), _vmem_spec(), _vmem_spec(),
                  pl.BlockSpec(memory_space=pltpu.SMEM), _row_spec(tm, qw)] + extra_specs,
        out_specs=[_row_spec(tm, qw // 2), _vmem_spec(), _vmem_spec(), _vmem_spec(), _vmem_spec()],
        out_shape=[jax.ShapeDtypeStruct((L, qw // 2), MXU_DTYPE),
                   jax.ShapeDtypeStruct((L + 2 * SWA_BLOCK, kw), F32),
                   jax.ShapeDtypeStruct((L + 2 * SWA_BLOCK, kw), F32),
                   jax.ShapeDtypeStruct((SWA_KV_HEADS, SWA_SPAN, SWA_GROUP_LANES), F32),
                   jax.ShapeDtypeStruct((SWA_KV_HEADS, 1, SWA_GROUP_LANES), F32)],
        compiler_params=_params(("arbitrary",), VMEM_BIG),
    )(qs, ks, vs, bias, sink, do, *extra)


def _bias_call(rel_bias, buckets, dep=None):
    def body(t_r, bk_r, o_r):
        bk = bk_r[...]
        s = lax.broadcasted_iota(jnp.int32, bk.shape, 0)
        c = lax.broadcasted_iota(jnp.int32, bk.shape, 1)
        in_band = jnp.abs(s - SWA_BLOCK - c) <= SWA_BLOCK
        for h in range(SWA_Q_HEADS):
            acc = jnp.zeros(bk.shape, F32)
            for b in range(REL_BUCKETS):
                acc = jnp.where(bk == b, t_r[b, h], acc)
            g = h % SWA_GROUP
            o_r[h // SWA_GROUP, :, SWA_BLOCK * g:SWA_BLOCK * (g + 1)] = jnp.where(in_band, acc, -1e30)

    body, extra, extra_specs = _after(body, 2, dep)
    return pl.pallas_call(
        body, name="band_bias",
        in_specs=[pl.BlockSpec(memory_space=pltpu.SMEM), _vmem_spec()] + extra_specs, out_specs=_vmem_spec(),
        out_shape=jax.ShapeDtypeStruct((SWA_KV_HEADS, SWA_SPAN, SWA_GROUP_LANES), F32),
    )(rel_bias, buckets, *extra)


def _relbias_call(dbias, dsink, buckets, dep=None):
    def body(db_r, ds_r, bk_r, o_r, os_r):
        bk = bk_r[...]
        rowi = lax.broadcasted_iota(jnp.int32, (REL_BUCKETS, 128), 0)
        lanei = lax.broadcasted_iota(jnp.int32, (REL_BUCKETS, 128), 1)
        lane1 = lax.broadcasted_iota(jnp.int32, (1, 128), 1)
        acc = jnp.zeros((REL_BUCKETS, 128), F32)
        acc_sink = jnp.zeros((1, 128), F32)
        for h in range(SWA_Q_HEADS):
            kv, g = h // SWA_GROUP, h % SWA_GROUP
            lanes = slice(SWA_BLOCK * g, SWA_BLOCK * (g + 1))
            part = db_r[kv, :, lanes]
            for b in range(REL_BUCKETS):
                s = jnp.sum(jnp.where(bk == b, part, 0.0))
                acc = acc + jnp.where((rowi == b) & (lanei == h), s, 0.0)
            acc_sink = acc_sink + jnp.where(lane1 == h, jnp.sum(ds_r[kv, :, lanes]), 0.0)
        o_r[...] = acc
        os_r[...] = acc_sink

    body, extra, extra_specs = _after(body, 3, dep)
    return pl.pallas_call(
        body, name="relbias_grad",
        in_specs=[_vmem_spec()] * 3 + extra_specs, out_specs=[_vmem_spec()] * 2,
        out_shape=[jax.ShapeDtypeStruct((REL_BUCKETS, 128), F32), jax.ShapeDtypeStruct((1, 128), F32)],
    )(dbias, dsink, buckets, *extra)


def _mix_call(o_f, o_b, ga, o_s, x, gn, w_out_p, g_post, g_pre2, dep=None):
    L = x.shape[0]
    tm = min(512, L)
    hw = GLA_HEADS * HEAD_PAD

    def body(of_r, ob_r, ga_r, os_r, x_r, gn_r, w_r, gp_r, g2_r, cat_r, mix_r, h1_r, n2_r):
        gn_v = gn_r[...]
        for h in range(GLA_HEADS):
            sl = slice(HEAD_PAD * h, HEAD_PAD * (h + 1))
            oh = of_r[:, sl] + ob_r[:, sl]
            on = oh * _rms_r(oh) * gn_v
            gate = ga_r[:, sl]
            cat_r[:, sl] = (on * (gate * jax.nn.sigmoid(gate))).astype(cat_r.dtype)
        os_v = os_r[...]
        cat_r[:, hw:] = os_v
        mix = _dot(cat_r[:, :hw], w_r[:hw, :]) + _dot(os_v, w_r[hw:, :])
        mix_r[...] = mix
        h1 = x_r[...] + mix * _rms_r(mix) * gp_r[...]
        h1_r[...] = h1
        n2_r[...] = (h1 * _rms_r(h1) * g2_r[...]).astype(n2_r.dtype)

    body, extra, extra_specs = _after(body, 9, dep)
    return pl.pallas_call(
        body, name="mix_fwd", grid=(L // tm,),
        in_specs=[_row_spec(tm, hw), _row_spec(tm, hw), _row_spec(tm, hw), _row_spec(tm, OUT_PAD - hw),
                  _row_spec(tm, D_MODEL), _full_spec((1, HEAD_PAD)), _vmem_spec(),
                  _full_spec((1, D_MODEL)), _full_spec((1, D_MODEL))] + extra_specs,
        out_specs=[_row_spec(tm, OUT_PAD), _row_spec(tm, D_MODEL), _row_spec(tm, D_MODEL), _row_spec(tm, D_MODEL)],
        out_shape=[jax.ShapeDtypeStruct((L, OUT_PAD), MXU_DTYPE), jax.ShapeDtypeStruct((L, D_MODEL), F32),
                   jax.ShapeDtypeStruct((L, D_MODEL), F32), jax.ShapeDtypeStruct((L, D_MODEL), MXU_DTYPE)],
        compiler_params=_params(("arbitrary",), VMEM_BIG),
    )(o_f, o_b, ga, o_s, x, gn, w_out_p, g_post, g_pre2, *extra)


def _mlp_fwd_call(n2, h1, tgt, w_ud, g_post):
    L = n2.shape[0]
    tm = min(512, L)
    blk = D_FF // N_CHIPS

    def body(n2_r, h1_r, t_r, w_r, g_r, a_r, rz_r, dh2_r, dff_r, loss_r, dg_r):
        @pl.when(pl.program_id(0) == 0)
        def _():
            loss_r[...] = jnp.zeros_like(loss_r)
            dg_r[...] = jnp.zeros_like(dg_r)

        n2v = n2_r[...]
        ff = jnp.zeros((tm, D_MODEL), F32)
        for j in range(N_CHIPS):
            sl = slice(blk * j, blk * (j + 1))
            rz = jnp.maximum(_dot(n2v, w_r[j, 0]), 0.0)
            a = _mx(rz * rz)
            rz_r[:, sl] = rz.astype(rz_r.dtype)
            a_r[:, sl] = a
            ff = ff + _dot(a, w_r[j, 1])
        g = g_r[...]
        r = _rms_r(ff)
        err = h1_r[...] + ff * r * g - t_r[...]
        loss_r[...] += 0.5 * jnp.sum(err * err) / D_MODEL
        dh2 = err * (1.0 / D_MODEL)
        dh2_r[...] = dh2
        dff, dg = _rms_bwd(ff, r, g, dh2)
        dff_r[...] = dff.astype(dff_r.dtype)
        dg_r[...] += dg

    return pl.pallas_call(
        body, name="mlp_fwd", grid=(L // tm,),
        in_specs=[_row_spec(tm, D_MODEL), _row_spec(tm, D_MODEL), _row_spec(tm, D_MODEL),
                  _vmem_spec(), _full_spec((1, D_MODEL))],
        out_specs=[_row_spec(tm, D_FF), _row_spec(tm, D_FF), _row_spec(tm, D_MODEL), _row_spec(tm, D_MODEL),
                   _full_spec((1, 128)), _full_spec((1, D_MODEL))],
        out_shape=[jax.ShapeDtypeStruct((L, D_FF), MXU_DTYPE), jax.ShapeDtypeStruct((L, D_FF), MXU_DTYPE),
                   jax.ShapeDtypeStruct((L, D_MODEL), F32), jax.ShapeDtypeStruct((L, D_MODEL), MXU_DTYPE),
                   jax.ShapeDtypeStruct((1, 128), F32), jax.ShapeDtypeStruct((1, D_MODEL), F32)],
        compiler_params=_params(("arbitrary",), VMEM_BIG),
    )(n2, h1, tgt, w_ud, g_post)


def _mlp_bwd_call(dff, rz, w_ud):
    L = dff.shape[0]
    tm = min(512, L)
    blk = D_FF // N_CHIPS

    def body(dff_r, rz_r, w_r, dz_r, dn2_r):
        dffv = dff_r[...]
        dn2 = jnp.zeros((tm, D_MODEL), F32)
        for j in range(N_CHIPS):
            sl = slice(blk * j, blk * (j + 1))
            dz = _mx(_dot_nt(dffv, w_r[j, 1]) * 2.0 * rz_r[:, sl].astype(F32))
            dz_r[:, sl] = dz
            dn2 = dn2 + _dot_nt(dz, w_r[j, 0])
        dn2_r[...] = dn2

    return pl.pallas_call(
        body, name="mlp_bwd", grid=(L // tm,),
        in_specs=[_row_spec(tm, D_MODEL), _row_spec(tm, D_FF), _vmem_spec()],
        out_specs=[_row_spec(tm, D_FF), _row_spec(tm, D_MODEL)],
        out_shape=[jax.ShapeDtypeStruct((L, D_FF), MXU_DTYPE), jax.ShapeDtypeStruct((L, D_MODEL), F32)],
        compiler_params=_params(("arbitrary",), VMEM_BIG),
    )(dff, rz, w_ud)


def _mlp_wgrad_call(a, dff, n2, dz):
    L = a.shape[0]
    tf = 512
    per = (D_FF // N_CHIPS) // tf

    def body(a_r, dff_r, n2_r, dz_r, dwd_r, dwu_r):
        dwd_r[...] = _dot_tn(a_r[...], dff_r[...])
        dwu_r[...] = _dot_tn(n2_r[...], dz_r[...])

    return pl.pallas_call(
        body, name="mlp_wgrad", grid=(D_FF // tf,),
        in_specs=[pl.BlockSpec((L, tf), lambda j: (0, j)), _vmem_spec(), _vmem_spec(),
                  pl.BlockSpec((L, tf), lambda j: (0, j))],
        out_specs=[pl.BlockSpec((tf, D_MODEL), lambda j: (j, 0)),
                   pl.BlockSpec((None, D_MODEL, tf), lambda j: (j // per, 0, j % per))],
        out_shape=[jax.ShapeDtypeStruct((D_FF, D_MODEL), F32),
                   jax.ShapeDtypeStruct((N_CHIPS, D_MODEL, D_FF // N_CHIPS), F32)],
        compiler_params=_params(("arbitrary",), VMEM_BIG),
    )(a, dff, n2, dz)


def _mix_bwd_call(dn2, dh2, h1, mix, cat, o_f, o_b, ga, gn, g_post, g_pre2, w_out_p):
    L = dn2.shape[0]
    tm = min(512, L)
    hw = GLA_HEADS * HEAD_PAD

    def body(dn2_r, dh2_r, h1_r, mix_r, cat_r, of_r, ob_r, ga_r, gn_r, gp_r, g2_r, w_r,
             dh1_r, do_r, dga_r, dos_r, dw_r, dg2_r, dgp_r, dgn_r):
        @pl.when(pl.program_id(0) == 0)
        def _():
            for ref in (dw_r, dg2_r, dgp_r, dgn_r):
                ref[...] = jnp.zeros_like(ref)

        parts = [slice(start, start + min(256, tm)) for start in range(0, tm, 256)]
        dmix_m = []
        for rs in parts:
            h1 = h1_r[rs, :]
            dx2, dg2 = _rms_bwd(h1, _rms_r(h1), g2_r[...], dn2_r[rs, :])
            dh1 = dh2_r[rs, :] + dx2
            dh1_r[rs, :] = dh1
            dg2_r[...] += dg2
            mix = mix_r[rs, :]
            dmix, dgp = _rms_bwd(mix, _rms_r(mix), gp_r[...], dh1)
            dgp_r[...] += dgp
            dmix_m.append(_mx(dmix))
        dcat = [_dot_nt(d, w_r[...]) for d in dmix_m]
        for rs, d in zip(parts, dmix_m):
            dw_r[...] += _dot_tn(cat_r[rs, :], d)
        gn_v = gn_r[...]
        dgn = jnp.zeros((1, HEAD_PAD), F32)
        for rs, dc in zip(parts, dcat):
            dos_r[rs, :] = _spread_heads(dc[:, hw:]).astype(dos_r.dtype)
            for h in range(GLA_HEADS):
                sl = slice(HEAD_PAD * h, HEAD_PAD * (h + 1))
                oh = of_r[rs, sl] + ob_r[rs, sl]
                rr = _rms_r(oh)
                xh = oh * rr
                gate = ga_r[rs, sl]
                sg = jax.nn.sigmoid(gate)
                silu = gate * sg
                doa = dc[:, sl]
                dga_r[rs, sl] = (doa * (xh * gn_v) * (sg + silu * (1.0 - sg))).astype(dga_r.dtype)
                don = doa * silu
                gd = don * gn_v
                do_r[rs, sl] = rr * (gd - xh * jnp.mean(gd * xh, axis=-1, keepdims=True))
                dgn = dgn + jnp.sum(don * xh, axis=0, keepdims=True)
        dgn_r[...] += dgn

    return pl.pallas_call(
        body, name="mix_bwd", grid=(L // tm,),
        in_specs=[_row_spec(tm, D_MODEL)] * 4 + [_row_spec(tm, OUT_PAD)] + [_row_spec(tm, hw)] * 3
        + [_full_spec((1, HEAD_PAD)), _full_spec((1, D_MODEL)), _full_spec((1, D_MODEL)), _vmem_spec()],
        out_specs=[_row_spec(tm, D_MODEL), _row_spec(tm, hw), _row_spec(tm, hw),
                   _row_spec(tm, SWA_Q_HEADS * HEAD_PAD),
                   _full_spec((OUT_PAD, D_MODEL)), _full_spec((1, D_MODEL)), _full_spec((1, D_MODEL)),
                   _full_spec((1, HEAD_PAD))],
        out_shape=[jax.ShapeDtypeStruct((L, D_MODEL), F32), jax.ShapeDtypeStruct((L, hw), F32),
                   jax.ShapeDtypeStruct((L, hw), MXU_DTYPE),
                   jax.ShapeDtypeStruct((L, SWA_Q_HEADS * HEAD_PAD), MXU_DTYPE),
                   jax.ShapeDtypeStruct((OUT_PAD, D_MODEL), F32), jax.ShapeDtypeStruct((1, D_MODEL), F32),
                   jax.ShapeDtypeStruct((1, D_MODEL), F32), jax.ShapeDtypeStruct((1, HEAD_PAD), F32)],
        compiler_params=_params(("arbitrary",), VMEM_BIG),
    )(dn2, dh2, h1, mix, cat, o_f, o_b, ga, gn, g_post, g_pre2, w_out_p)


def _in_bwd_call(x, dh1, g_pre, w_in_t, pairs, singles, halos, dep=None):
    L = x.shape[0]
    tm = min(512, L)
    per = tm // SWA_BLOCK
    n_pair, n_single, n_halo = len(pairs), len(singles), len(halos)
    groups = [c for c, _ in pairs] + [c for c, _ in singles] + [c for c, _ in halos]

    def body(*refs):
        x_r, dh1_r, g_r, w_r = refs[:4]
        pair_refs = refs[4:4 + 2 * n_pair]
        single_refs = refs[4 + 2 * n_pair:4 + 2 * n_pair + n_single]
        halo_refs = refs[4 + 2 * n_pair + n_single:4 + 2 * n_pair + n_single + per * n_halo]
        dx_r, dw_r, dg_r = refs[4 + 2 * n_pair + n_single + per * n_halo:]

        @pl.when(pl.program_id(0) == 0)
        def _():
            dw_r[...] = jnp.zeros_like(dw_r)
            dg_r[...] = jnp.zeros_like(dg_r)

        xv = x_r[...]
        r = _rms_r(xv)
        g = g_r[...]
        u = _mx(xv * r * g)
        vals = [pair_refs[2 * i][...].astype(F32) + pair_refs[2 * i + 1][...].astype(F32) for i in range(n_pair)]
        vals += [ref[...].astype(F32) for ref in single_refs]
        vals += [jnp.concatenate([ref[...] for ref in halo_refs[per * i:per * (i + 1)]], axis=0)
                 for i in range(n_halo)]
        ds = [_mx(_squeeze_heads(val) if heads else val) for (_, _, heads), val in zip(groups, vals)]
        du = jnp.zeros((tm, D_MODEL), F32)
        for (first, rows, _), d in zip(groups, ds):
            du = du + _dot(d, w_r[first:first + rows, :])
        for (first, rows, _), d in zip(groups, ds):
            dw_r[first:first + rows, :] += _dot_tn(d, u)
        dx, dg = _rms_bwd(xv, r, g, du)
        dx_r[...] = dh1_r[...] + dx
        dg_r[...] += dg

    arrays = [a for _, pr in pairs for a in pr] + [a for _, a in singles]
    specs = [_row_spec(tm, a.shape[1]) for a in arrays]
    for _, a in halos:
        specs += [pl.BlockSpec((SWA_BLOCK, a.shape[1]), lambda i, j=j: (per * i + 1 + j, 0)) for j in range(per)]
        arrays += [a] * per
    body, extra, extra_specs = _after(body, 4 + len(arrays), dep)
    return pl.pallas_call(
        body, name="in_bwd", grid=(L // tm,),
        in_specs=[_row_spec(tm, D_MODEL), _row_spec(tm, D_MODEL), _full_spec((1, D_MODEL)), _vmem_spec()] + specs
        + extra_specs,
        out_specs=[_row_spec(tm, D_MODEL), _full_spec((IN_COLS, D_MODEL)), _full_spec((1, D_MODEL))],
        out_shape=[jax.ShapeDtypeStruct((L, D_MODEL), F32), jax.ShapeDtypeStruct((IN_COLS, D_MODEL), F32),
                   jax.ShapeDtypeStruct((1, D_MODEL), F32)],
        compiler_params=_params(("arbitrary",), VMEM_BIG),
    )(x, dh1, g_pre, w_in_t, *arrays, *extra)


def _adamw_math(w, g, m, v):
    m = ADAM_B1 * m + (1.0 - ADAM_B1) * g
    v = ADAM_B2 * v + (1.0 - ADAM_B2) * (g * g)
    m_hat = m / (1.0 - ADAM_B1 ** ADAM_STEP)
    v_hat = v / (1.0 - ADAM_B2 ** ADAM_STEP)
    delta = -ADAM_LR * (m_hat / (jnp.sqrt(v_hat) + ADAM_EPS) + ADAM_WD * w)
    return delta, m, v


def _adamw_call(w, g, m, v, name, dep=None):
    rows, cols = w.shape
    tr = min(256, rows)

    def body(w_r, g_r, m_r, v_r, g_out_r, d_r, nm_r, nv_r):
        g = g_r[...]
        g_out_r[...] = g
        d_r[...], nm_r[...], nv_r[...] = _adamw_math(w_r[...], g, m_r[...], v_r[...])

    if rows % tr == 0:
        spec, steps = _row_spec(tr, cols), rows // tr
    else:
        spec, steps = pl.BlockSpec((rows, 256), lambda i: (0, i)), cols // 256
    body, extra, extra_specs = _after(body, 4, dep)
    return pl.pallas_call(
        body, name=name, grid=(steps,),
        in_specs=[spec] * 4 + extra_specs, out_specs=[spec] * 4,
        out_shape=[jax.ShapeDtypeStruct(w.shape, F32)] * 4,
        compiler_params=_params(("arbitrary",)),
    )(w, g, m, v, *extra)


def _position():
    return lax.axis_index("x"), lax.axis_index("y"), lax.axis_index("c")


def _other_chips(x, y):
    return [(1 - x, y), (x, 1 - y), (1 - x, 1 - y)]


ROWS, COLS = -2, -1


def _half(ref, which, axis):
    size = ref.shape[axis] // 2
    span = pl.ds(pl.multiple_of(which * size, 16 if axis == ROWS else 128), size)
    index = [slice(None)] * len(ref.shape)
    index[axis] = span
    return ref.at[tuple(index)]


def _quarter(ref, half, which, axis):
    size = ref.shape[axis] // 4
    span = pl.ds(pl.multiple_of((2 * half + which) * size, 16 if axis == ROWS else 128), size)
    index = [slice(None)] * len(ref.shape)
    index[axis] = span
    return ref.at[tuple(index)]


def _first_gather_call(shards, axes, routed):
    n = len(shards)
    per = 7

    def body(*refs):
        srcs, outs = refs[:n], refs[n:2 * n]
        send_sems, recv_sems, local_sems = refs[2 * n:]
        x, y, c = _position()
        me, sibling = (x, y, c), (x, y, 1 - c)
        x_side, y_side, across = _other_chips(x, y)
        local = [pltpu.make_async_copy(srcs[a], outs[a].at[2 * x + y], local_sems.at[a]) for a in range(n)]
        for cp in local:
            cp.start()

        def copy(a, k, dst, to, src=None):
            return pltpu.make_async_remote_copy(
                src_ref=dst if src is None else src, dst_ref=dst, send_sem=send_sems.at[per * a + k],
                recv_sem=recv_sems.at[per * a + k], device_id=to, device_id_type=MESH_ID)

        def half(a, chip, pc):
            return _half(outs[a].at[2 * chip[0] + chip[1]], pc, axes[a])

        def quarter(a, chip, q):
            return _quarter(outs[a].at[2 * chip[0] + chip[1]], c, q, axes[a])

        sends = []
        for a in range(n):
            mine = _half(srcs[a], c, axes[a])
            targets = (x_side, y_side) if routed[a] else (x_side, y_side, across)
            sends += [copy(a, j, half(a, (x, y), c), (*chip, c), src=mine) for j, chip in enumerate(targets)]
        for cp in sends:
            cp.start()
        for a in range(n):
            for j, chip in enumerate((x_side, y_side)):
                copy(a, j, half(a, chip, c), me).wait_recv()
                if routed[a]:
                    other = (y_side, x_side)[j]
                    sends.append(copy(a, 2 + j, quarter(a, chip, j), (*other, c)))
                    sends[-1].start()
                sends.append(copy(a, 4 + j, half(a, chip, c), sibling))
                sends[-1].start()
        for a in range(n):
            if routed[a]:
                for j in range(2):
                    copy(a, 2 + j, quarter(a, across, j), me).wait_recv()
            else:
                copy(a, 2, half(a, across, c), me).wait_recv()
            sends.append(copy(a, 6, half(a, across, c), sibling))
            sends[-1].start()
        for a in range(n):
            for k, chip in ((4, x_side), (5, y_side), (6, across)):
                copy(a, k, half(a, chip, 1 - c), me).wait_recv()
        for cp in sends:
            cp.wait_send()
        for cp in local:
            cp.wait()

    return pl.pallas_call(
        body, name="first_gather",
        in_specs=[_any_spec()] * n, out_specs=[_any_spec()] * n,
        out_shape=[jax.ShapeDtypeStruct((N_CHIPS,) + s.shape, s.dtype) for s in shards],
        scratch_shapes=[pltpu.SemaphoreType.DMA((per * n,)), pltpu.SemaphoreType.DMA((per * n,)),
                        pltpu.SemaphoreType.DMA((n,))],
    )(*shards)


PAIR_PEERS, CHIP_PEERS = 1, 2


def _peers(which):
    x, y, c = _position()
    if which == PAIR_PEERS:
        return [(x, y, 1 - c)]
    return [(px, py, c) for px, py in _other_chips(x, y)]


def _split_start(name, arrays, n_copies, plan, peers=None):
    n = len(arrays)

    def body(*refs):
        ins, send_sems, recv_sems, token = refs[:n], refs[n], refs[n + 1], refs[-1]
        if peers is not None:
            barrier = pltpu.get_barrier_semaphore()
            targets = _peers(peers)
            for target in targets:
                pl.semaphore_signal(barrier, inc=1, device_id=target, device_id_type=MESH_ID)
            pl.semaphore_wait(barrier, len(targets))
        for k, (src, dst, to, _) in enumerate(plan(ins)):
            pltpu.make_async_remote_copy(src_ref=src, dst_ref=dst, send_sem=send_sems.at[k],
                                         recv_sem=recv_sems.at[k], device_id=to, device_id_type=MESH_ID).start()
        token[...] = jnp.zeros_like(token)

    hbm = pl.BlockSpec(memory_space=pltpu.HBM)
    sem = pl.BlockSpec(memory_space=pltpu.SEMAPHORE)
    out = pl.pallas_call(
        body, name=name,
        out_shape=(pltpu.SemaphoreType.DMA((n_copies,)), pltpu.SemaphoreType.DMA((n_copies,)))
        + tuple(pltpu.HBM(a.shape, a.dtype) for a in arrays) + (jax.ShapeDtypeStruct((8, 128), F32),),
        in_specs=[hbm] * n, out_specs=(sem, sem) + (hbm,) * n + (_vmem_spec(),),
        input_output_aliases={i: 2 + i for i in range(n)},
        compiler_params=pltpu.CompilerParams(has_side_effects=pltpu.SideEffectType.DATAFLOW_SIDE_EFFECTING,
                                             collective_id=peers),
    )(*[pltpu.with_memory_space_constraint(a, pltpu.HBM) for a in arrays])
    return (out[0], out[1], tuple(out[2:2 + n])), out[-1]


def _split_wait(name, handle, n_copies, plan, after):
    send_sems, recv_sems, arrays = handle
    n = len(arrays)

    def body(*refs):
        ins, s_sems, r_sems = refs[:n], refs[n], refs[n + 1]
        for k, (src, dst, to, landed) in enumerate(plan(ins)):
            cp = pltpu.make_async_remote_copy(src_ref=src, dst_ref=landed, send_sem=s_sems.at[k],
                                              recv_sem=r_sems.at[k], device_id=to, device_id_type=MESH_ID)
            cp.wait_send()
            cp.wait_recv()

    hbm = pl.BlockSpec(memory_space=pltpu.HBM)
    sem = pl.BlockSpec(memory_space=pltpu.SEMAPHORE)
    out = pl.pallas_call(
        body, name=name,
        out_shape=tuple(pltpu.HBM(a.shape, a.dtype) for a in arrays),
        in_specs=[hbm] * n + [sem, sem, _any_spec()], out_specs=(hbm,) * n,
        input_output_aliases={i: i for i in range(n)},
        compiler_params=pltpu.CompilerParams(has_side_effects=pltpu.SideEffectType.DATAFLOW_SIDE_EFFECTING),
    )(*arrays, send_sems, recv_sems, after)
    return tuple(out)


def _gather_plans(axes):
    n = len(axes)

    def stage_one(refs):
        x, y, c = _position()
        copies = []
        for a, axis in enumerate(axes):
            for px, py in _other_chips(x, y):
                copies.append((_half(refs[a], c, axis), _half(refs[n + a].at[2 * x + y], c, axis),
                               (px, py, c), _half(refs[n + a].at[2 * px + py], c, axis)))
        return copies

    def stage_two(refs):
        x, y, c = _position()
        copies = []
        for a, axis in enumerate(axes):
            for px, py in _other_chips(x, y):
                piece = _half(refs[n + a].at[2 * px + py], c, axis)
                copies.append((piece, piece, (x, y, 1 - c), _half(refs[n + a].at[2 * px + py], 1 - c, axis)))
        return copies

    return stage_one, stage_two


def _pair_swap_plan(axes):
    n = len(axes)

    def plan(refs):
        x, y, c = _position()
        return [(_half(refs[a], 1 - c, axes[a]), refs[n + a], (x, y, 1 - c), refs[n + a]) for a in range(n)]

    return plan


def _chip_swap_plan(n):
    def plan(refs):
        x, y, c = _position()
        copies = []
        for a in range(n):
            for j, (px, py) in enumerate(_other_chips(x, y)):
                copies.append((refs[a].at[2 * px + py], refs[n + a].at[j], (px, py, c), refs[n + a].at[j]))
        return copies

    return plan


def _pair_join_plan(axes):
    def plan(refs):
        x, y, c = _position()
        copies = []
        for a, axis in enumerate(axes):
            mine = _half(refs[a], c, axis)
            copies.append((mine, mine, (x, y, 1 - c), _half(refs[a], 1 - c, axis)))
        return copies

    return plan


def _pair_add_call(gs, gots, pos, name, axes):
    n = len(gs)

    def body(pos_r, *refs):
        for g_r, got_r, o_r in zip(refs[:n], refs[n:2 * n], refs[2 * n:]):
            o_r[...] = (g_r[...] + got_r[...]).astype(o_r.dtype)

    def mine(axis):
        return (lambda j, p: (j, p[1], 0)) if axis == ROWS else (lambda j, p: (j, 0, p[1]))

    blocks = [(None,) + got.shape[1:] for got in gots]
    return pl.pallas_call(
        body, name=name,
        grid_spec=pltpu.PrefetchScalarGridSpec(
            num_scalar_prefetch=1, grid=(N_CHIPS,),
            in_specs=[pl.BlockSpec(blk, mine(axis)) for blk, axis in zip(blocks, axes)]
            + [pl.BlockSpec(blk, lambda j, p: (j, 0, 0)) for blk in blocks],
            out_specs=[pl.BlockSpec(blk, lambda j, p: (j, 0, 0)) for blk in blocks]),
        out_shape=[jax.ShapeDtypeStruct(got.shape, COMM_DTYPE) for got in gots],
        compiler_params=_params(("arbitrary",), VMEM_BIG),
    )(pos, *gs, *gots)


def _chip_add_call(hsums, gots, pos, name, axes):
    n = len(hsums)
    steps = 2

    def body(pos_r, *refs):
        for own_r, got_r, o_r in zip(refs[:n], refs[n:2 * n], refs[2 * n:]):
            acc = own_r[...].astype(F32)
            for j in range(3):
                acc = acc + got_r[j].astype(F32)
            o_r[...] = acc

    in_specs, got_specs, out_specs, out_shape = [], [], [], []
    for h, axis in zip(hsums, axes):
        if axis == ROWS:
            rows, cols = h.shape[1] // steps, h.shape[2]
            in_specs.append(pl.BlockSpec((None, rows, cols), lambda i, p: (p[0], i, 0)))
            got_specs.append(pl.BlockSpec((3, rows, cols), lambda i, p: (0, i, 0)))
            out_specs.append(pl.BlockSpec((rows, cols), lambda i, p: (p[1] * steps + i, 0)))
            out_shape.append(jax.ShapeDtypeStruct((2 * h.shape[1], cols), F32))
        else:
            rows, cols = h.shape[1], h.shape[2] // steps
            in_specs.append(pl.BlockSpec((None, rows, cols), lambda i, p: (p[0], 0, i)))
            got_specs.append(pl.BlockSpec((3, rows, cols), lambda i, p: (0, 0, i)))
            out_specs.append(pl.BlockSpec((rows, cols), lambda i, p: (0, p[1] * steps + i)))
            out_shape.append(jax.ShapeDtypeStruct((rows, 2 * h.shape[2]), F32))
    return pl.pallas_call(
        body, name=name,
        grid_spec=pltpu.PrefetchScalarGridSpec(
            num_scalar_prefetch=1, grid=(steps,), in_specs=in_specs + got_specs, out_specs=out_specs),
        out_shape=out_shape,
        compiler_params=_params(("arbitrary",), VMEM_BIG),
    )(pos, *hsums, *gots)


SMALL_NAMES = ("norm_mix_pre", "norm_mix_post", "norm_mlp_pre", "norm_mlp_post", "b_gate_fwd", "b_gate_bwd",
               "gla_norm", "swa_sink", "rel_bias")


N_DEVICES = 8


def _small_pack_call(grads, extras):
    operands = list(grads) + list(extras)

    def body(*refs):
        g_refs, (all_a, all_b) = refs[:len(operands)], refs[len(operands):]
        x, y, c = _position()
        me = 4 * x + 2 * y + c
        all_a[me] = jnp.zeros(all_a.shape[1:], F32)
        all_b[me] = jnp.zeros(all_b.shape[1:], F32)
        for i in range(4):
            all_a[me, i:i + 1, :] = g_refs[i][...]
        all_a[me, 4:5, 0:256] = g_refs[4][...]
        all_a[me, 5:6, 0:256] = g_refs[5][...]
        all_a[me, 6:7, 0:128] = g_refs[6][...]
        all_a[me, 7:8, 0:128] = g_refs[7][...]
        all_a[me, 7:8, 128:256] = g_refs[11][...]
        all_b[me, 0:32, 0:128] = g_refs[8][...]
        all_b[me, 32:48, :] = g_refs[9][...]
        all_b[me, 48:64, :] = g_refs[10][...]

    out_shape = [jax.ShapeDtypeStruct((N_DEVICES, 8, D_MODEL), F32), jax.ShapeDtypeStruct((N_DEVICES, 64, 256), F32)]
    return pl.pallas_call(
        body, name="small_pack",
        in_specs=[_whole_spec(a.shape) for a in operands], out_specs=[_whole_spec(s.shape) for s in out_shape],
        out_shape=out_shape,
    )(*operands)


def _everyone_plan(n):
    def plan(refs):
        x, y, c = _position()
        copies = []
        for k in range(1, N_DEVICES):
            px = 1 - x if (k >> 2) & 1 else x
            py = 1 - y if (k >> 1) & 1 else y
            pc = 1 - c if k & 1 else c
            for a in range(n):
                mine = refs[a].at[4 * x + 2 * y + c]
                copies.append((mine, mine, (px, py, pc), refs[a].at[4 * px + 2 * py + pc]))
        return copies

    return plan


def _small_adamw_call(all_a, all_b, params):
    n_small = len(SMALL_NAMES)
    wmv = [t for p in params for t in p]
    shapes = [p[0].shape for p in params]

    def body(*refs):
        all_a, all_b = refs[:2]
        wmv_refs = refs[2:2 + 3 * n_small]
        out_refs = refs[2 + 3 * n_small:]
        sum_a, sum_b = all_a[0], all_b[0]
        for d in range(1, N_DEVICES):
            sum_a = sum_a + all_a[d]
            sum_b = sum_b + all_b[d]
        gsum = [sum_a[0:1], sum_a[1:2], sum_a[2:3], sum_a[3:4], sum_a[4:5, 0:256], sum_a[5:6, 0:256],
                sum_a[6:7, 0:128], sum_a[7:8, 0:SWA_Q_HEADS], sum_b[0:32, 0:SWA_Q_HEADS]]
        for i in range(n_small):
            w_r, m_r, v_r = wmv_refs[3 * i:3 * i + 3]
            delta, new_m, new_v = _adamw_math(w_r[...], gsum[i], m_r[...], v_r[...])
            out_refs[4 * i][...] = gsum[i]
            out_refs[4 * i + 1][...] = delta
            out_refs[4 * i + 2][...] = new_m
            out_refs[4 * i + 3][...] = new_v
        out_refs[4 * n_small][...] = sum_b[32:48]
        out_refs[4 * n_small + 1][...] = sum_b[48:64]
        out_refs[4 * n_small + 2][...] = sum_a[7:8, 128:256]

    out_shape = [jax.ShapeDtypeStruct(s, F32) for s in shapes for _ in range(4)]
    out_shape += [jax.ShapeDtypeStruct((GLA_GATE_RANK, 256), F32)] * 2 + [jax.ShapeDtypeStruct((1, 128), F32)]
    out = pl.pallas_call(
        body, name="small_adamw",
        in_specs=[_whole_spec(a.shape) for a in [all_a, all_b] + wmv],
        out_specs=[_whole_spec(s.shape) for s in out_shape],
        out_shape=out_shape,
    )(all_a, all_b, *wmv)
    per_name = [tuple(out[4 * i:4 * i + 4]) for i in range(n_small)]
    return per_name, out[4 * n_small], out[4 * n_small + 1], out[4 * n_small + 2]


def _pad_gate(w, first_row):
    return jnp.pad(w, ((first_row, 128 - GLA_GATE_RANK - first_row), (0, 0)))


def _own_slot(shard, chip):
    zone = lax.empty((N_CHIPS,) + shard.shape, shard.dtype)
    return lax.dynamic_update_slice(zone, shard[None], (chip,) + (0,) * shard.ndim)


def _reduce_to_owners(grads, axes, pos, tag, overlap):
    n = len(grads)

    def half_shape(g, axis):
        return (N_CHIPS, g.shape[1] // 2, g.shape[2]) if axis == ROWS else (N_CHIPS, g.shape[1], g.shape[2] // 2)

    lands = [lax.empty(half_shape(g, axis), F32) for g, axis in zip(grads, axes)]
    handle, token = _split_start(tag + "_pair_start", list(grads) + lands, n, _pair_swap_plan(axes), PAIR_PEERS)
    got = _split_wait(tag + "_pair_wait", handle, n, _pair_swap_plan(axes), overlap[0](token))
    sums = list(_pair_add_call(got[:n], got[n:], pos, tag + "_pair_add", axes))
    lands = [lax.empty((3,) + s.shape[1:], s.dtype) for s in sums]
    handle, token = _split_start(tag + "_chip_start", sums + lands, 3 * n, _chip_swap_plan(n), CHIP_PEERS)
    got = _split_wait(tag + "_chip_wait", handle, 3 * n, _chip_swap_plan(n), overlap[1](token))
    halves = list(_chip_add_call(got[:n], got[n:], pos, tag + "_chip_add", axes))
    handle, token = _split_start(tag + "_join_start", halves, n, _pair_join_plan(axes), PAIR_PEERS)
    return _split_wait(tag + "_join_wait", handle, n, _pair_join_plan(axes), overlap[2](token))


def kernel(x, norm_mix_pre, w_in, w_gate_up_fwd, b_gate_fwd, w_gate_up_bwd, b_gate_bwd, gla_norm, swa_sink, rel_bias, w_out, norm_mix_post, norm_mlp_pre, w_up, w_down, norm_mlp_post, loss_target, m_norm_mix_pre, m_w_in, m_w_gate_up_fwd, m_b_gate_fwd, m_w_gate_up_bwd, m_b_gate_bwd, m_gla_norm, m_swa_sink, m_rel_bias, m_w_out, m_norm_mix_post, m_norm_mlp_pre, m_w_up, m_w_down, m_norm_mlp_post, v_norm_mix_pre, v_w_in, v_w_gate_up_fwd, v_b_gate_fwd, v_w_gate_up_bwd, v_b_gate_bwd, v_gla_norm, v_swa_sink, v_rel_bias, v_w_out, v_norm_mix_post, v_norm_mlp_pre, v_w_up, v_w_down, v_norm_mlp_post):
    given = dict(locals())
    cx, cy, cc = _position()
    chip = (2 * cx + cy).astype(jnp.int32)
    pos = jnp.stack([chip, cc.astype(jnp.int32)])
    seq, tgt = x[0], loss_target[0]

    gates = jnp.concatenate([w_gate_up_fwd[0], w_gate_up_bwd[0]], axis=0).astype(COMM_DTYPE)
    all_in, all_gates = _first_gather_call([w_in[0].T.astype(COMM_DTYPE), gates], [COLS, ROWS], [True, False])
    rest = [w_out[0].astype(COMM_DTYPE), jnp.stack([w_up[0], w_down[0]]).astype(COMM_DTYPE)]
    stage_one, stage_two = _gather_plans([ROWS, ROWS])
    handle, token = _split_start("gather_chip_start", rest + [_own_slot(s, chip) for s in rest] + [all_gates], 6,
                                 stage_one, CHIP_PEERS)

    w_in_t = _mx(all_in.reshape(IN_COLS, D_MODEL))
    gates_full = jnp.concatenate([all_gates[j] for j in range(N_CHIPS)], axis=1)
    wgf_p = _mx(_pad_gate(gates_full[:GLA_GATE_RANK], 0))
    wgb_p = _mx(_pad_gate(gates_full[GLA_GATE_RANK:], GLA_GATE_RANK))
    bf_p, bb_p = b_gate_fwd, b_gate_bwd
    buckets = jnp.asarray(_band_buckets())
    sink1 = swa_sink.reshape(SWA_Q_HEADS)

    qa, ka, va, ga, qs, ks, vs, za = _proj_call(seq, norm_mix_pre, w_in_t, dep=token)
    halo = ((SWA_BLOCK, SWA_BLOCK), (0, 0))
    ks_p, vs_p = jnp.pad(ks, halo), jnp.pad(vs, halo)
    o_f, o_b, s_f, s_b = _gla_fwd_call(qa, ka, va, za, wgf_p, bf_p, wgb_p, bb_p)
    bias = _bias_call(rel_bias, buckets, dep=o_f)
    arrays = _split_wait("gather_chip_wait", handle, 6, stage_one, bias)
    handle, token = _split_start("gather_pair_start", list(arrays), 6, stage_two, PAIR_PEERS)
    o_s = _swa_fwd_call(qs, ks_p, vs_p, bias, sink1, dep=token)
    arrays = _split_wait("gather_pair_wait", handle, 6, stage_two, o_s)
    w_out_full = _mx(arrays[2].reshape(N_CHIPS * R_OUT, D_MODEL))
    w_ud = _mx(arrays[3])
    cat, mix, h1, n2 = _mix_call(o_f, o_b, ga, o_s, seq, gla_norm, w_out_full, norm_mix_post, norm_mlp_pre)
    a, rz, dh2, dff, loss, d_post2 = _mlp_fwd_call(n2, h1, tgt, w_ud, norm_mlp_post)

    dz, dn2 = _mlp_bwd_call(dff, rz, w_ud)
    dw_down, dw_up4 = _mlp_wgrad_call(a, dff, n2, dz)
    dh1, do, dga, dos, dw_out, d_pre2, d_post, d_gn = _mix_bwd_call(
        dn2, dh2, h1, mix, cat, o_f, o_b, ga, gla_norm, norm_mix_post, norm_mlp_pre, w_out_full)
    done = {}

    def swa_backward(tok):
        done["swa"] = _swa_bwd_call(qs, ks_p, vs_p, bias, sink1, dos, dep=tok)
        return done["swa"][0]

    def gla_in_backward(tok):
        done["gla"] = _gla_bwd_call(qa, ka, va, za, do, s_f, s_b, wgf_p, bf_p, wgb_p, bb_p, dep=tok)
        dqf, dkf, dvf, dzf, _, _, dqb, dkb, dvb, dzb, _, _ = done["gla"]
        dqs, dks_p, dvs_p, _, _ = done["swa"]
        done["in"] = _in_bwd_call(
            seq, dh1, norm_mix_pre, w_in_t,
            pairs=[(_side_by_side(T_QA), (dqf, dqb)), (_side_by_side(T_KA), (dkf, dkb)), (T_VA, (dvf, dvb)),
                   (T_ZA, (dzf, dzb))],
            singles=[(T_GA, dga), (_side_by_side(T_QS), dqs)], halos=[(T_KS, dks_p), (T_VS, dvs_p)])
        return done["in"][0]

    def bias_backward(tok):
        done["rel"] = _relbias_call(done["swa"][3], done["swa"][4], buckets, dep=tok)
        return done["rel"][0]

    g_up, g_down, g_out = _reduce_to_owners(
        [dw_up4, dw_down.reshape(N_CHIPS, R_DOWN, D_MODEL), dw_out.reshape(N_CHIPS, R_OUT, D_MODEL)],
        [ROWS, ROWS, ROWS], pos, "mlp", [swa_backward, gla_in_backward, bias_backward])
    dx, dw_in_t, d_pre = done["in"]
    dwf, dbf, dwb, dbb = done["gla"][4], done["gla"][5], done["gla"][10], done["gla"][11]
    drel, dsink = done["rel"]

    small_grads = [d_pre, d_post, d_pre2, d_post2, dbf, dbb, d_gn, dsink, drel]
    gate_grads = [dwf[:GLA_GATE_RANK], dwb[GLA_GATE_RANK:2 * GLA_GATE_RANK]]
    small_params = [(given[n], given["m_" + n], given["v_" + n]) for n in SMALL_NAMES]
    upd = {}

    everyone = _everyone_plan(2)
    small_handle, small_token = _split_start(
        "small_start", list(_small_pack_call(small_grads, gate_grads + [loss])), 2 * (N_DEVICES - 1), everyone)

    def update_out(tok):
        upd["w_out"] = tuple(_adamw_call(w_out[0], g_out, m_w_out[0], v_w_out[0], "adamw_w_out",
                                         dep=tok + small_token))
        return upd["w_out"][1]

    def update_mlp(tok):
        upd["w_up"] = tuple(_adamw_call(w_up[0], g_up, m_w_up[0], v_w_up[0], "adamw_w_up", dep=tok))
        upd["w_down"] = tuple(
            _adamw_call(w_down[0], g_down, m_w_down[0], v_w_down[0], "adamw_w_down", dep=upd["w_up"][1]))
        all_a, all_b = _split_wait("small_wait", small_handle, 2 * (N_DEVICES - 1), everyone, upd["w_down"][1])
        per_name, done["gf_sum"], done["gb_sum"], upd["loss"] = _small_adamw_call(all_a, all_b, small_params)
        upd.update(dict(zip(SMALL_NAMES, per_name)))
        return per_name[0][1]

    def update_gates(tok):
        for name, total in (("w_gate_up_fwd", done["gf_sum"]), ("w_gate_up_bwd", done["gb_sum"])):
            g = lax.dynamic_slice(total, (0, chip * 64), (GLA_GATE_RANK, 64))
            upd[name] = tuple(_adamw_call(given[name][0], g, given["m_" + name][0], given["v_" + name][0],
                                          "adamw_" + name, dep=tok))
        return upd["w_gate_up_bwd"][1]

    (g_in_t,) = _reduce_to_owners([dw_in_t.reshape(N_CHIPS, R_IN, D_MODEL)], [COLS], pos, "in",
                                  [update_out, update_mlp, update_gates])
    upd["w_in"] = tuple(t.T for t in _adamw_call(w_in[0].T, g_in_t, m_w_in[0].T, v_w_in[0].T, "adamw_w_in"))

    big = ("w_in", "w_gate_up_fwd", "w_gate_up_bwd", "w_out", "w_up", "w_down")
    names = ["norm_mix_pre", "w_in", "w_gate_up_fwd", "b_gate_fwd", "w_gate_up_bwd", "b_gate_bwd", "gla_norm",
             "swa_sink", "rel_bias", "w_out", "norm_mix_post", "norm_mlp_pre", "w_up", "w_down", "norm_mlp_post"]
    outs = [upd["loss"][0, 0], dx[None]]
    for kind in range(4):
        outs += [upd[n][kind][None] if n in big else upd[n][kind] for n in names]
    return tuple(outs)
```

```python
import math

import numpy as np
import jax
import jax.numpy as jnp
from jax import lax
from jax.experimental import pallas as pl
from jax.experimental.pallas import tpu as pltpu

F32 = jnp.float32
MXU_DTYPE = jnp.bfloat16
COMM_DTYPE = jnp.bfloat16

D_MODEL = 1024
D_FF = 4096
N_CHIPS = 4
GLA_HEADS = 4
GLA_CHUNK = 64
GLA_GATE_RANK = 16
GLA_GATE_NORM = 16.0
SWA_Q_HEADS = 8
SWA_KV_HEADS = 2
SWA_BLOCK = 128
REL_BUCKETS = 32
REL_MAX_DIST = 128
NORM_EPS = 1e-6
HEAD_PAD = 128

ADAM_LR = 0.001
ADAM_B1 = 0.9
ADAM_B2 = 0.999
ADAM_EPS = 1e-08
ADAM_WD = 0.01
ADAM_STEP = 10

OUT_PAD = 1024

R_IN, R_OUT, R_DOWN = 584, 256, 1024

VMEM_BIG = 56 * 1024 * 1024
MESH_ID = pl.DeviceIdType.MESH


def _mx(a):
    return a.astype(MXU_DTYPE)


def _dot(a, b):
    return jnp.dot(a, b, preferred_element_type=F32)


def _dot_nt(a, b):
    return lax.dot_general(a, b, (((1,), (1,)), ((), ())), preferred_element_type=F32)


def _dot_tn(a, b):
    return lax.dot_general(a, b, (((0,), (0,)), ((), ())), preferred_element_type=F32)


def _rms_r(x):
    return lax.rsqrt(jnp.mean(x * x, axis=-1, keepdims=True) + NORM_EPS)


def _rms_bwd(x, r, g, dy):
    xh = x * r
    gdy = dy * g
    dx = r * (gdy - xh * jnp.mean(gdy * xh, axis=-1, keepdims=True))
    return dx, jnp.sum(dy * xh, axis=0, keepdims=True)


def _low_half(rows):
    return lax.broadcasted_iota(jnp.int32, (rows, HEAD_PAD), 1) < 64


def _spread_heads(x):
    low = _low_half(x.shape[0])
    parts = []
    for p in range(x.shape[1] // HEAD_PAD):
        pair = x[:, HEAD_PAD * p:HEAD_PAD * (p + 1)]
        parts += [jnp.where(low, pair, 0.0), jnp.where(low, pltpu.roll(pair, 64, 1), 0.0)]
    return jnp.concatenate(parts, axis=1)


def _squeeze_heads(x):
    low = _low_half(x.shape[0])
    parts = []
    for p in range(x.shape[1] // (2 * HEAD_PAD)):
        even = x[:, 2 * HEAD_PAD * p:2 * HEAD_PAD * p + HEAD_PAD]
        odd = x[:, 2 * HEAD_PAD * p + HEAD_PAD:2 * HEAD_PAD * (p + 1)]
        parts.append(jnp.where(low, even, pltpu.roll(odd, 64, 1)))
    return parts[0] if len(parts) == 1 else jnp.concatenate(parts, axis=1)


def _params(sem=None, vmem=None):
    kw = {}
    if sem is not None:
        kw["dimension_semantics"] = sem
    if vmem is not None:
        kw["vmem_limit_bytes"] = vmem
    return pltpu.CompilerParams(**kw)


def _vmem_spec():
    return pl.BlockSpec(memory_space=pltpu.VMEM)


def _whole_spec(shape):
    return pl.BlockSpec(shape, lambda: (0,) * len(shape))


def _row_spec(tm, width):
    return pl.BlockSpec((tm, width), lambda i: (i, 0))


def _full_spec(shape):
    return pl.BlockSpec(shape, lambda i: (0,) * len(shape))


def _any_spec():
    return pl.BlockSpec(memory_space=pl.ANY)


def _after(body, n_in, dep):
    if dep is None:
        return body, [], []
    return (lambda *refs: body(*refs[:n_in], *refs[n_in + 1:])), [dep], [_any_spec()]


T_QA, T_KA, T_VA, T_GA = (0, 256, 4), (256, 256, 4), (512, 512, 0), (1024, 512, 0)
T_QS, T_KS, T_VS = (1568, 512, 8), (2080, 128, 2), (2208, 128, 2)
T_ZA = (1536, 128, 0)
ZA_COLS = 2 * GLA_GATE_RANK
IN_COLS = 2336


def _side_by_side(group):
    return group[0], group[1], 0


def _proj_call(x, g_pre, w_in_t, dep=None):
    L = x.shape[0]
    tm = min(512, L)
    groups = [(T_QA, F32), (T_KA, F32), (T_VA, MXU_DTYPE), (T_GA, F32),
              (T_QS, MXU_DTYPE), (T_KS, MXU_DTYPE), (T_VS, MXU_DTYPE), (T_ZA, F32)]
    widths = [rows * (2 if heads else 1) for (_, rows, heads), _ in groups]

    def body(x_ref, g_ref, w_ref, *outs):
        xv = x_ref[...]
        u = _mx(xv * _rms_r(xv) * g_ref[...])
        for ref, (grp, _) in zip(outs, groups):
            first, rows, heads = grp
            val = _dot_nt(u, w_ref[first:first + rows, :])
            if heads:
                val = _spread_heads(val)
            if grp is T_ZA:
                val = jnp.where(lax.broadcasted_iota(jnp.int32, val.shape, 1) < ZA_COLS, val, 0.0)
            if grp is T_QS:
                val = val * 0.125
            ref[...] = val.astype(ref.dtype)

    body, extra, extra_specs = _after(body, 3, dep)
    return pl.pallas_call(
        body, name="proj_fwd", grid=(L // tm,),
        in_specs=[_row_spec(tm, D_MODEL), _full_spec((1, D_MODEL)), _vmem_spec()] + extra_specs,
        out_specs=[_row_spec(tm, w) for w in widths],
        out_shape=[jax.ShapeDtypeStruct((L, w), dt) for w, (_, dt) in zip(widths, groups)],
        compiler_params=_params(("arbitrary",), VMEM_BIG),
    )(x, g_pre, w_in_t, *extra)


def _tri_masks():
    row = lax.broadcasted_iota(jnp.int32, (GLA_CHUNK, GLA_CHUNK), 0)
    col = lax.broadcasted_iota(jnp.int32, (GLA_CHUNK, GLA_CHUNK), 1)
    return row >= col, row <= col


def _chunk_sums(tri_m, x):
    hi = _mx(x)
    rest = x - hi.astype(F32)
    mid = _mx(rest)
    lo = _mx(rest - mid.astype(F32))
    return _dot(tri_m, hi) + _dot(tri_m, mid) + _dot(tri_m, lo)


def _gla_block_pre(q_r, k_r, z_r, w_r, b_r, rev, nc, qd_s, ki_s, ks_s, dec_s, keep=None):
    tri_f, tri_b = _tri_masks()
    tri_m = _mx((tri_b if rev else tri_f).astype(F32))
    g = _dot(_mx(z_r[...]), w_r[...]) + b_r[...]
    la = (jnp.minimum(g, 0.0) - jnp.log(1.0 + jnp.exp(-jnp.abs(g)))) * (1.0 / GLA_GATE_NORM)
    sums, lasts = [], []
    for c in range(nc):
        b_c = _chunk_sums(tri_m, la[GLA_CHUNK * c:GLA_CHUNK * (c + 1)])
        blast = b_c[0:1] if rev else b_c[GLA_CHUNK - 1:GLA_CHUNK]
        dec_s[c] = _spread_heads(jnp.exp(blast))
        sums.append(b_c)
        lasts.append(jnp.broadcast_to(blast, b_c.shape))
    b = jnp.concatenate(sums, axis=0)
    eb = jnp.exp(b)
    enb = jnp.exp(-b)
    elb = jnp.exp(jnp.concatenate(lasts, axis=0) - b)
    q, k = _squeeze_heads(q_r[...]), _squeeze_heads(k_r[...])
    qd_s[...] = _spread_heads(q * 0.125 * eb).astype(qd_s.dtype)
    ki_s[...] = _spread_heads(k * enb).astype(ki_s.dtype)
    ks_s[...] = _spread_heads(k * elb).astype(ks_s.dtype)
    if keep is not None:
        keep[0][...] = g
        for ref, val in zip(keep[1:], (eb, enb, elb)):
            ref[...] = _spread_heads(val)


def _gla_fwd_call(qa, ka, va, za, wgf, bgf, wgb, bgb):
    L = qa.shape[0]
    br = min(512, L)
    nb, nc, n_chunks = L // br, br // GLA_CHUNK, L // GLA_CHUNK
    hw = GLA_HEADS * HEAD_PAD

    def body(qaf, kaf, vaf, zaf, qab, kab, vab, zab, wgf_r, bgf_r, wgb_r, bgb_r,
             of_r, ob_r, sf_r, sb_r, st_f, st_b, pre_f, pre_b):
        @pl.when(pl.program_id(0) == 0)
        def _():
            st_f[...] = jnp.zeros_like(st_f)
            st_b[...] = jnp.zeros_like(st_b)

        _gla_block_pre(qaf, kaf, zaf, wgf_r, bgf_r, False, nc, *pre_f)
        _gla_block_pre(qab, kab, zab, wgb_r, bgb_r, True, nc, *pre_b)
        tri_f, tri_b = _tri_masks()

        def one(tri, pre, v_r, o_r, s_r, st, ci):
            qd_s, ki_s, ks_s, dec_s = pre
            rows = pl.ds(pl.multiple_of(ci * GLA_CHUNK, GLA_CHUNK), GLA_CHUNK)
            dec = dec_s[ci]
            heads = range(GLA_HEADS)
            lanes = [slice(HEAD_PAD * h, HEAD_PAD * (h + 1)) for h in heads]
            qd = [qd_s[rows, sl] for sl in lanes]
            v = [v_r[rows, sl] for sl in lanes]
            s_t = [st[h] for h in heads]
            a = [_dot_nt(qd[h], ki_s[rows, lanes[h]]) for h in heads]
            carried = [_dot_nt(qd[h], _mx(s_t[h])) for h in heads]
            grown = [_dot_tn(v[h], ks_s[rows, lanes[h]]) for h in heads]
            a = [_mx(jnp.where(tri, a[h], 0.0)) for h in heads]
            inner = [_dot(a[h], v[h]) for h in heads]
            for h in heads:
                s_r[ci, h] = s_t[h].astype(s_r.dtype)
                o_r[rows, lanes[h]] = inner[h] + carried[h]
                st[h] = s_t[h] * dec[:, lanes[h]] + grown[h]

        def loop(t, carry):
            one(tri_f, pre_f, vaf, of_r, sf_r, st_f, t)
            one(tri_b, pre_b, vab, ob_r, sb_r, st_b, nc - 1 - t)
            return carry

        lax.fori_loop(0, nc, loop, 0, unroll=True)

    fwd = lambda i: (i, 0)
    bwd = lambda i: (nb - 1 - i, 0)
    ins = lambda m: [pl.BlockSpec((br, hw), m), pl.BlockSpec((br, hw), m),
                     pl.BlockSpec((br, hw), m), pl.BlockSpec((br, 128), m)]
    wspecs = [_full_spec((128, hw // 2)), _full_spec((1, hw // 2))] * 2
    s_shape = (nc, GLA_HEADS, HEAD_PAD, HEAD_PAD)
    pre_scratch = [pltpu.VMEM((br, hw), MXU_DTYPE)] * 3 + [pltpu.VMEM((nc, 1, hw), F32)]
    return pl.pallas_call(
        body, name="gla_fwd", grid=(nb,),
        in_specs=ins(fwd) + ins(bwd) + wspecs,
        out_specs=[pl.BlockSpec((br, hw), fwd), pl.BlockSpec((br, hw), bwd),
                   pl.BlockSpec(s_shape, lambda i: (i, 0, 0, 0)),
                   pl.BlockSpec(s_shape, lambda i: (nb - 1 - i, 0, 0, 0))],
        out_shape=[jax.ShapeDtypeStruct((L, hw), F32), jax.ShapeDtypeStruct((L, hw), F32),
                   jax.ShapeDtypeStruct((n_chunks,) + s_shape[1:], MXU_DTYPE),
                   jax.ShapeDtypeStruct((n_chunks,) + s_shape[1:], MXU_DTYPE)],
        scratch_shapes=[pltpu.VMEM(s_shape[1:], F32), pltpu.VMEM(s_shape[1:], F32), pre_scratch, pre_scratch],
        compiler_params=_params(("arbitrary",), VMEM_BIG),
    )(qa, ka, va, za, qa, ka, va, za, wgf, bgf, wgb, bgb)


def _gla_bwd_call(qa, ka, va, za, do, sf, sb, wgf, bgf, wgb, bgb, dep=None):
    L = qa.shape[0]
    br = min(512, L)
    nb, nc = L // br, br // GLA_CHUNK
    hw = GLA_HEADS * HEAD_PAD

    def body(qaf, kaf, vaf, zaf, dof, sf_r, qab, kab, vab, zab, dob, sb_r, wgf_r, bgf_r, wgb_r, bgb_r,
             dqf, dkf, dvf, dzf, dwf, dbf, dqb, dkb, dvb, dzb, dwb, dbb, gt_f, gt_b, pre_f, pre_b):
        @pl.when(pl.program_id(0) == 0)
        def _():
            for ref in (gt_f, gt_b, dwf, dbf, dwb, dbb):
                ref[...] = jnp.zeros_like(ref)

        _gla_block_pre(qaf, kaf, zaf, wgf_r, bgf_r, False, nc, *pre_f[:4], keep=pre_f[4:8])
        _gla_block_pre(qab, kab, zab, wgb_r, bgb_r, True, nc, *pre_b[:4], keep=pre_b[4:8])
        tri_f, tri_b = _tri_masks()
        row_w = lax.broadcasted_iota(jnp.int32, (GLA_CHUNK, HEAD_PAD), 0)

        def one(rev, pre, q_r, k_r, v_r, do_r, s_r, dq_r, dk_r, dv_r, gt, ci):
            qd_s, ki_s, ks_s, dec_s, _, eb_s, enb_s, elb_s, db_s = pre
            tri = tri_b if rev else tri_f
            last_row = 0 if rev else GLA_CHUNK - 1
            rows = pl.ds(pl.multiple_of(ci * GLA_CHUNK, GLA_CHUNK), GLA_CHUNK)
            dec = dec_s[ci]
            heads = range(GLA_HEADS)
            lanes = [slice(HEAD_PAD * h, HEAD_PAD * (h + 1)) for h in heads]
            qd = [qd_s[rows, sl] for sl in lanes]
            ki = [ki_s[rows, sl] for sl in lanes]
            ks = [ks_s[rows, sl] for sl in lanes]
            v = [v_r[rows, sl] for sl in lanes]
            do_h = [_mx(do_r[rows, sl]) for sl in lanes]
            s_t = [s_r[ci, h] for h in heads]
            g_t = [gt[h] for h in heads]
            g_m = [_mx(g_t[h]) for h in heads]
            a = [_dot_nt(qd[h], ki[h]) for h in heads]
            da = [_dot_nt(do_h[h], v[h]) for h in heads]
            dv_carried = [_dot_nt(ks[h], g_m[h]) for h in heads]
            dqd_carried = [_dot(do_h[h], _mx(s_t[h])) for h in heads]
            dks = [_dot(v[h], g_m[h]) for h in heads]
            g_grown = [_dot_tn(do_h[h], qd[h]) for h in heads]
            a = [_mx(jnp.where(tri, a[h], 0.0)) for h in heads]
            da = [_mx(jnp.where(tri, da[h], 0.0)) for h in heads]
            dv_inner = [_dot_tn(a[h], do_h[h]) for h in heads]
            dqd_inner = [_dot(da[h], ki[h]) for h in heads]
            dki = [_dot_tn(da[h], qd[h]) for h in heads]
            dq, dk = [], []
            for h in heads:
                sl = lanes[h]
                dv_r[rows, sl] = (dv_inner[h] + dv_carried[h]).astype(dv_r.dtype)
                ddec = jnp.sum(g_t[h] * s_t[h].astype(F32), axis=0, keepdims=True)
                gt[h] = g_t[h] * dec[:, sl] + g_grown[h]
                dq.append((dqd_inner[h] + dqd_carried[h]) * eb_s[rows, sl] * 0.125)
                dk_state = dks[h] * elb_s[rows, sl]
                dk.append(dki[h] * enb_s[rows, sl] + dk_state)
                k = k_r[rows, sl]
                dblast = jnp.sum(dk_state * k, axis=0, keepdims=True) + dec[:, sl] * ddec
                db_s[rows, sl] = q_r[rows, sl] * dq[h] - k * dk[h] + jnp.where(row_w == last_row, dblast, 0.0)
            low = _low_half(GLA_CHUNK)
            for pair in range(GLA_HEADS // 2):
                psl = slice(HEAD_PAD * pair, HEAD_PAD * (pair + 1))
                for ref, val in ((dq_r, dq), (dk_r, dk)):
                    both = jnp.where(low, val[2 * pair], pltpu.roll(val[2 * pair + 1], 64, 1))
                    ref[rows, psl] = both.astype(ref.dtype)

        def loop(t, carry):
            one(False, pre_f, qaf, kaf, vaf, dof, sf_r, dqf, dkf, dvf, gt_f, nc - 1 - t)
            one(True, pre_b, qab, kab, vab, dob, sb_r, dqb, dkb, dvb, gt_b, t)
            return carry

        lax.fori_loop(0, nc, loop, 0, unroll=True)

        def gate_grads(rev, pre, z_r, w_r, dz_r, dw_r, dbias_r):
            g_s, db_s = pre[4], pre[8]
            back_m = _mx((tri_f if rev else tri_b).astype(F32))
            db = _squeeze_heads(db_s[...])
            dla = jnp.concatenate([_chunk_sums(back_m, db[GLA_CHUNK * c:GLA_CHUNK * (c + 1)]) for c in range(nc)],
                                  axis=0)
            dg = dla * (1.0 / GLA_GATE_NORM) * (1.0 / (1.0 + jnp.exp(g_s[...])))
            dg_m = _mx(dg)
            dz_r[...] = _dot_nt(dg_m, w_r[...])
            dw_r[...] += _dot_tn(_mx(z_r[...]), dg_m)
            dbias_r[...] += jnp.sum(dg, axis=0, keepdims=True)

        gate_grads(False, pre_f, zaf, wgf_r, dzf, dwf, dbf)
        gate_grads(True, pre_b, zab, wgb_r, dzb, dwb, dbb)

    last_first = lambda i: (nb - 1 - i, 0)
    first_last = lambda i: (i, 0)
    s_shape = (nc, GLA_HEADS, HEAD_PAD, HEAD_PAD)

    def ins(m):
        return [pl.BlockSpec((br, hw), m), pl.BlockSpec((br, hw), m), pl.BlockSpec((br, hw), m),
                pl.BlockSpec((br, 128), m), pl.BlockSpec((br, hw), m),
                pl.BlockSpec(s_shape, lambda i: m(i) + (0, 0))]

    def outs(m):
        return [pl.BlockSpec((br, hw // 2), m), pl.BlockSpec((br, hw // 2), m), pl.BlockSpec((br, hw), m),
                pl.BlockSpec((br, 128), m), _full_spec((128, hw // 2)), _full_spec((1, hw // 2))]

    out_shape = [jax.ShapeDtypeStruct((L, hw // 2), MXU_DTYPE)] * 2 + [
        jax.ShapeDtypeStruct((L, hw), MXU_DTYPE),
        jax.ShapeDtypeStruct((L, 128), F32), jax.ShapeDtypeStruct((128, hw // 2), F32),
        jax.ShapeDtypeStruct((1, hw // 2), F32)]
    wspecs = [_full_spec((128, hw // 2)), _full_spec((1, hw // 2))] * 2
    body, extra, extra_specs = _after(body, 16, dep)
    pre_scratch = ([pltpu.VMEM((br, hw), MXU_DTYPE)] * 3 + [pltpu.VMEM((nc, 1, hw), F32)]
                   + [pltpu.VMEM((br, hw // 2), F32)] + [pltpu.VMEM((br, hw), F32)] * 4)
    return pl.pallas_call(
        body, name="gla_bwd", grid=(nb,),
        in_specs=ins(last_first) + ins(first_last) + wspecs + extra_specs,
        out_specs=outs(last_first) + outs(first_last),
        out_shape=out_shape + out_shape,
        scratch_shapes=[pltpu.VMEM(s_shape[1:], F32), pltpu.VMEM(s_shape[1:], F32), pre_scratch, pre_scratch],
        compiler_params=_params(("arbitrary",), VMEM_BIG),
    )(qa, ka, va, za, do, sf, qa, ka, va, za, do, sb, wgf, bgf, wgb, bgb, *extra)


def _t5_buckets(rel):
    nb = REL_BUCKETS // 2
    ret = (rel > 0).astype(np.int32) * nb
    n = np.abs(rel)
    max_exact = nb // 2
    large = max_exact + (np.log(np.maximum(n, 1).astype(np.float32) / max_exact)
                         / math.log(REL_MAX_DIST / max_exact) * (nb - max_exact)).astype(np.int32)
    large = np.minimum(large, nb - 1)
    return ret + np.where(n < max_exact, n, large)


SWA_GROUP = SWA_Q_HEADS // SWA_KV_HEADS
SWA_SPAN = 3 * SWA_BLOCK
SWA_GROUP_LANES = SWA_GROUP * SWA_BLOCK


def _band_buckets():
    s = np.arange(SWA_SPAN)[:, None]
    c = np.arange(SWA_BLOCK)[None, :]
    return _t5_buckets(s - SWA_BLOCK - c).astype(np.int32)


def _swa_valid(n, seq_len):
    key_pos = (n - 1) * SWA_BLOCK + lax.broadcasted_iota(jnp.int32, (SWA_SPAN, 1), 0)
    return (key_pos >= 0) & (key_pos < seq_len)


def _swa_sink_row(sink_r, kv):
    lane = lax.broadcasted_iota(jnp.int32, (1, SWA_GROUP_LANES), 1)
    row = jnp.full((1, SWA_GROUP_LANES), sink_r[kv * SWA_GROUP], F32)
    for g in range(1, SWA_GROUP):
        row = jnp.where(lane >= g * SWA_BLOCK, sink_r[kv * SWA_GROUP + g], row)
    return row


SWA_STEP_BLOCKS = 4


def _swa_group(ref, kv, rows):
    first = kv * SWA_GROUP
    return jnp.concatenate([ref[rows, HEAD_PAD * h:HEAD_PAD * (h + 1)] for h in range(first, first + SWA_GROUP)],
                           axis=0)


def _swa_softmax(scores, bias_t, sink_row, valid):
    st = jnp.where(valid, scores + bias_t, -1e30)
    m = jnp.maximum(jnp.max(st, axis=0, keepdims=True), sink_row)
    p = jnp.exp(st - m)
    e_sink = jnp.exp(sink_row - m)
    inv = 1.0 / (jnp.sum(p, axis=0, keepdims=True) + e_sink)
    return p * inv, e_sink * inv


def _swa_fwd_call(qs, ks, vs, bias, sink, dep=None):
    L = qs.shape[0]

    def block(n, rows, q_r, k_r, v_r, bias_r, sink_r, o_r):
        span = pl.ds(pl.multiple_of(n * SWA_BLOCK, SWA_BLOCK), SWA_SPAN)
        valid = _swa_valid(n, L)
        groups = range(SWA_KV_HEADS)
        lanes = [slice(HEAD_PAD * kv, HEAD_PAD * (kv + 1)) for kv in groups]
        scores = [_dot_nt(k_r[span, lanes[kv]], _swa_group(q_r, kv, rows)) for kv in groups]
        probs = [_swa_softmax(scores[kv], bias_r[kv], _swa_sink_row(sink_r, kv), valid)[0] for kv in groups]
        low = _low_half(SWA_BLOCK)
        for kv in groups:
            og = _dot_tn(_mx(probs[kv]), v_r[span, lanes[kv]])
            for pair in range(SWA_GROUP // 2):
                even = og[2 * SWA_BLOCK * pair:2 * SWA_BLOCK * pair + SWA_BLOCK]
                odd = og[2 * SWA_BLOCK * pair + SWA_BLOCK:2 * SWA_BLOCK * (pair + 1)]
                first = HEAD_PAD * (kv * SWA_GROUP // 2 + pair)
                o_r[rows, first:first + HEAD_PAD] = jnp.where(low, even, pltpu.roll(odd, 64, 1)).astype(o_r.dtype)

    def body(*refs):
        for j in range(SWA_STEP_BLOCKS):
            block(SWA_STEP_BLOCKS * pl.program_id(0) + j, slice(SWA_BLOCK * j, SWA_BLOCK * (j + 1)), *refs)

    qw = SWA_Q_HEADS * HEAD_PAD
    tm = SWA_STEP_BLOCKS * SWA_BLOCK
    body, extra, extra_specs = _after(body, 5, dep)
    return pl.pallas_call(
        body, name="swa_fwd", grid=(L // tm,),
        in_specs=[_row_spec(tm, qw), _vmem_spec(), _vmem_spec(), _vmem_spec(),
                  pl.BlockSpec(memory_space=pltpu.SMEM)] + extra_specs,
        out_specs=_row_spec(tm, qw // 2),
        out_shape=jax.ShapeDtypeStruct((L, qw // 2), MXU_DTYPE),
        compiler_params=_params(("arbitrary",), VMEM_BIG),
    )(qs, ks, vs, bias, sink, *extra)


def _swa_bwd_call(qs, ks, vs, bias, sink, do, dep=None):
    L = qs.shape[0]
    qw = SWA_Q_HEADS * HEAD_PAD
    kw = SWA_KV_HEADS * HEAD_PAD

    def body(*refs):
        dk_r, dv_r, dbias_r, dsink_r = refs[7:]

        @pl.when(pl.program_id(0) == 0)
        def _():
            for ref in (dk_r, dv_r, dbias_r, dsink_r):
                ref[...] = jnp.zeros_like(ref)

        for j in range(SWA_STEP_BLOCKS):
            block(SWA_STEP_BLOCKS * pl.program_id(0) + j, slice(SWA_BLOCK * j, SWA_BLOCK * (j + 1)), *refs)

    def block(n, rows, q_r, k_r, v_r, bias_r, sink_r, do_r, dq_r, dk_r, dv_r, dbias_r, dsink_r):
        span = pl.ds(pl.multiple_of(n * SWA_BLOCK, SWA_BLOCK), SWA_SPAN)
        valid = _swa_valid(n, L)
        groups = range(SWA_KV_HEADS)
        lanes = [slice(HEAD_PAD * kv, HEAD_PAD * (kv + 1)) for kv in groups]
        kk = [k_r[span, sl] for sl in lanes]
        vv = [v_r[span, sl] for sl in lanes]
        qg = [_swa_group(q_r, kv, rows) for kv in groups]
        dog = [_swa_group(do_r, kv, rows) for kv in groups]
        scores = [_dot_nt(kk[kv], qg[kv]) for kv in groups]
        dp = [_dot_nt(vv[kv], dog[kv]) for kv in groups]
        probs = [_swa_softmax(scores[kv], bias_r[kv], _swa_sink_row(sink_r, kv), valid) for kv in groups]
        ds_m, pn_m = [], []
        for kv in groups:
            pn, p_sink = probs[kv]
            delta = jnp.sum(pn * dp[kv], axis=0, keepdims=True)
            ds = pn * (dp[kv] - delta)
            dsink_r[kv] -= p_sink * delta
            dbias_r[kv] += ds
            ds_m.append(_mx(ds))
            pn_m.append(_mx(pn))
        dqg = [_dot_tn(ds_m[kv], kk[kv]) * 0.125 for kv in groups]
        dkk = [_dot(ds_m[kv], qg[kv]) for kv in groups]
        dvv = [_dot(pn_m[kv], dog[kv]) for kv in groups]
        low = _low_half(SWA_BLOCK)
        for kv in groups:
            for pair in range(SWA_GROUP // 2):
                even = dqg[kv][2 * SWA_BLOCK * pair:2 * SWA_BLOCK * pair + SWA_BLOCK]
                odd = dqg[kv][2 * SWA_BLOCK * pair + SWA_BLOCK:2 * SWA_BLOCK * (pair + 1)]
                first = HEAD_PAD * (kv * SWA_GROUP // 2 + pair)
                dq_r[rows, first:first + HEAD_PAD] = jnp.where(low, even, pltpu.roll(odd, 64, 1)).astype(dq_r.dtype)
            dk_r[span, lanes[kv]] += dkk[kv]
            dv_r[span, lanes[kv]] += dvv[kv]

    tm = SWA_STEP_BLOCKS * SWA_BLOCK
    body, extra, extra_specs = _after(body, 6, dep)
    return pl.pallas_call(
        body, name="swa_bwd", grid=(L // tm,),
        in_specs=[_row_spec(tm, qw), _vmem_spec(), _vmem_spec(), _vmem_spec(),
                  pl.BlockSpec(memory_space=pltpu.SMEM), _row_spec(tm, qw)] + extra_specs,
        out_specs=[_row_spec(tm, qw // 2), _vmem_spec(), _vmem_spec(), _vmem_spec(), _vmem_spec()],
        out_shape=[jax.ShapeDtypeStruct((L, qw // 2), MXU_DTYPE),
                   jax.ShapeDtypeStruct((L + 2 * SWA_BLOCK, kw), F32),
                   jax.ShapeDtypeStruct((L + 2 * SWA_BLOCK, kw), F32),
                   jax.ShapeDtypeStruct((SWA_KV_HEADS, SWA_SPAN, SWA_GROUP_LANES), F32),
                   jax.ShapeDtypeStruct((SWA_KV_HEADS, 1, SWA_GROUP_LANES), F32)],
        compiler_params=_params(("arbitrary",), VMEM_BIG),
    )(qs, ks, vs, bias, sink, do, *extra)


def _bias_call(rel_bias, buckets, dep=None):
    def body(t_r, bk_r, o_r):
        bk = bk_r[...]
        s = lax.broadcasted_iota(jnp.int32, bk.shape, 0)
        c = lax.broadcasted_iota(jnp.int32, bk.shape, 1)
        in_band = jnp.abs(s - SWA_BLOCK - c) <= SWA_BLOCK
        for h in range(SWA_Q_HEADS):
            acc = jnp.zeros(bk.shape, F32)
            for b in range(REL_BUCKETS):
                acc = jnp.where(bk == b, t_r[b, h], acc)
            g = h % SWA_GROUP
            o_r[h // SWA_GROUP, :, SWA_BLOCK * g:SWA_BLOCK * (g + 1)] = jnp.where(in_band, acc, -1e30)

    body, extra, extra_specs = _after(body, 2, dep)
    return pl.pallas_call(
        body, name="band_bias",
        in_specs=[pl.BlockSpec(memory_space=pltpu.SMEM), _vmem_spec()] + extra_specs, out_specs=_vmem_spec(),
        out_shape=jax.ShapeDtypeStruct((SWA_KV_HEADS, SWA_SPAN, SWA_GROUP_LANES), F32),
    )(rel_bias, buckets, *extra)


def _relbias_call(dbias, dsink, buckets, dep=None):
    def body(db_r, ds_r, bk_r, o_r, os_r):
        bk = bk_r[...]
        rowi = lax.broadcasted_iota(jnp.int32, (REL_BUCKETS, 128), 0)
        lanei = lax.broadcasted_iota(jnp.int32, (REL_BUCKETS, 128), 1)
        lane1 = lax.broadcasted_iota(jnp.int32, (1, 128), 1)
        acc = jnp.zeros((REL_BUCKETS, 128), F32)
        acc_sink = jnp.zeros((1, 128), F32)
        heads = [(h // SWA_GROUP, slice(SWA_BLOCK * (h % SWA_GROUP), SWA_BLOCK * (h % SWA_GROUP + 1)))
                 for h in range(SWA_Q_HEADS)]
        for b in range(REL_BUCKETS):
            in_bucket = bk == b
            for h, (kv, lanes) in enumerate(heads):
                s = jnp.sum(jnp.where(in_bucket, db_r[kv, :, lanes], 0.0))
                acc = acc + jnp.where((rowi == b) & (lanei == h), s, 0.0)
        for h, (kv, lanes) in enumerate(heads):
            acc_sink = acc_sink + jnp.where(lane1 == h, jnp.sum(ds_r[kv, :, lanes]), 0.0)
        o_r[...] = acc
        os_r[...] = acc_sink

    body, extra, extra_specs = _after(body, 3, dep)
    return pl.pallas_call(
        body, name="relbias_grad",
        in_specs=[_vmem_spec()] * 3 + extra_specs, out_specs=[_vmem_spec()] * 2,
        out_shape=[jax.ShapeDtypeStruct((REL_BUCKETS, 128), F32), jax.ShapeDtypeStruct((1, 128), F32)],
    )(dbias, dsink, buckets, *extra)


def _mix_call(o_f, o_b, ga, o_s, x, gn, w_out_p, g_post, g_pre2, dep=None):
    L = x.shape[0]
    tm = min(512, L)
    hw = GLA_HEADS * HEAD_PAD

    def body(of_r, ob_r, ga_r, os_r, x_r, gn_r, w_r, gp_r, g2_r, cat_r, mix_r, h1_r, n2_r):
        gn_v = gn_r[...]
        for h in range(GLA_HEADS):
            sl = slice(HEAD_PAD * h, HEAD_PAD * (h + 1))
            oh = of_r[:, sl] + ob_r[:, sl]
            on = oh * _rms_r(oh) * gn_v
            gate = ga_r[:, sl]
            cat_r[:, sl] = (on * (gate * jax.nn.sigmoid(gate))).astype(cat_r.dtype)
        os_v = os_r[...]
        cat_r[:, hw:] = os_v
        mix = _dot(cat_r[:, :hw], w_r[:hw, :]) + _dot(os_v, w_r[hw:, :])
        mix_r[...] = mix
        h1 = x_r[...] + mix * _rms_r(mix) * gp_r[...]
        h1_r[...] = h1
        n2_r[...] = (h1 * _rms_r(h1) * g2_r[...]).astype(n2_r.dtype)

    body, extra, extra_specs = _after(body, 9, dep)
    return pl.pallas_call(
        body, name="mix_fwd", grid=(L // tm,),
        in_specs=[_row_spec(tm, hw), _row_spec(tm, hw), _row_spec(tm, hw), _row_spec(tm, OUT_PAD - hw),
                  _row_spec(tm, D_MODEL), _full_spec((1, HEAD_PAD)), _vmem_spec(),
                  _full_spec((1, D_MODEL)), _full_spec((1, D_MODEL))] + extra_specs,
        out_specs=[_row_spec(tm, OUT_PAD), _row_spec(tm, D_MODEL), _row_spec(tm, D_MODEL), _row_spec(tm, D_MODEL)],
        out_shape=[jax.ShapeDtypeStruct((L, OUT_PAD), MXU_DTYPE), jax.ShapeDtypeStruct((L, D_MODEL), F32),
                   jax.ShapeDtypeStruct((L, D_MODEL), F32), jax.ShapeDtypeStruct((L, D_MODEL), MXU_DTYPE)],
        compiler_params=_params(("arbitrary",), VMEM_BIG),
    )(o_f, o_b, ga, o_s, x, gn, w_out_p, g_post, g_pre2, *extra)


def _mlp_fwd_call(n2, h1, tgt, w_ud, g_post):
    L = n2.shape[0]
    tm = min(512, L)
    blk = D_FF // N_CHIPS

    def body(n2_r, h1_r, t_r, w_r, g_r, a_r, rz_r, dh2_r, dff_r, loss_r, dg_r):
        @pl.when(pl.program_id(0) == 0)
        def _():
            loss_r[...] = jnp.zeros_like(loss_r)
            dg_r[...] = jnp.zeros_like(dg_r)

        n2v = n2_r[...]
        ff = jnp.zeros((tm, D_MODEL), F32)
        for j in range(N_CHIPS):
            sl = slice(blk * j, blk * (j + 1))
            rz = jnp.maximum(_dot(n2v, w_r[j, 0]), 0.0)
            a = _mx(rz * rz)
            rz_r[:, sl] = rz.astype(rz_r.dtype)
            a_r[:, sl] = a
            ff = ff + _dot(a, w_r[j, 1])
        g = g_r[...]
        r = _rms_r(ff)
        err = h1_r[...] + ff * r * g - t_r[...]
        loss_r[...] += 0.5 * jnp.sum(err * err) / D_MODEL
        dh2 = err * (1.0 / D_MODEL)
        dh2_r[...] = dh2
        dff, dg = _rms_bwd(ff, r, g, dh2)
        dff_r[...] = dff.astype(dff_r.dtype)
        dg_r[...] += dg

    return pl.pallas_call(
        body, name="mlp_fwd", grid=(L // tm,),
        in_specs=[_row_spec(tm, D_MODEL), _row_spec(tm, D_MODEL), _row_spec(tm, D_MODEL),
                  _vmem_spec(), _full_spec((1, D_MODEL))],
        out_specs=[_row_spec(tm, D_FF), _row_spec(tm, D_FF), _row_spec(tm, D_MODEL), _row_spec(tm, D_MODEL),
                   _full_spec((1, 128)), _full_spec((1, D_MODEL))],
        out_shape=[jax.ShapeDtypeStruct((L, D_FF), MXU_DTYPE), jax.ShapeDtypeStruct((L, D_FF), MXU_DTYPE),
                   jax.ShapeDtypeStruct((L, D_MODEL), F32), jax.ShapeDtypeStruct((L, D_MODEL), MXU_DTYPE),
                   jax.ShapeDtypeStruct((1, 128), F32), jax.ShapeDtypeStruct((1, D_MODEL), F32)],
        compiler_params=_params(("arbitrary",), VMEM_BIG),
    )(n2, h1, tgt, w_ud, g_post)


def _mlp_bwd_call(dff, rz, w_ud):
    L = dff.shape[0]
    tm = min(512, L)
    blk = D_FF // N_CHIPS

    def body(dff_r, rz_r, w_r, dz_r, dn2_r):
        dffv = dff_r[...]
        dn2 = jnp.zeros((tm, D_MODEL), F32)
        for j in range(N_CHIPS):
            sl = slice(blk * j, blk * (j + 1))
            dz = _mx(_dot_nt(dffv, w_r[j, 1]) * 2.0 * rz_r[:, sl].astype(F32))
            dz_r[:, sl] = dz
            dn2 = dn2 + _dot_nt(dz, w_r[j, 0])
        dn2_r[...] = dn2

    return pl.pallas_call(
        body, name="mlp_bwd", grid=(L // tm,),
        in_specs=[_row_spec(tm, D_MODEL), _row_spec(tm, D_FF), _vmem_spec()],
        out_specs=[_row_spec(tm, D_FF), _row_spec(tm, D_MODEL)],
        out_shape=[jax.ShapeDtypeStruct((L, D_FF), MXU_DTYPE), jax.ShapeDtypeStruct((L, D_MODEL), F32)],
        compiler_params=_params(("arbitrary",), VMEM_BIG),
    )(dff, rz, w_ud)


def _mlp_wgrad_call(a, dff, n2, dz):
    L = a.shape[0]
    tf = 512
    per = (D_FF // N_CHIPS) // tf

    def body(a_r, dff_r, n2_r, dz_r, dwd_r, dwu_r):
        dwd_r[...] = _dot_tn(a_r[...], dff_r[...])
        dwu_r[...] = _dot_tn(n2_r[...], dz_r[...])

    return pl.pallas_call(
        body, name="mlp_wgrad", grid=(D_FF // tf,),
        in_specs=[pl.BlockSpec((L, tf), lambda j: (0, j)), _vmem_spec(), _vmem_spec(),
                  pl.BlockSpec((L, tf), lambda j: (0, j))],
        out_specs=[pl.BlockSpec((tf, D_MODEL), lambda j: (j, 0)),
                   pl.BlockSpec((None, D_MODEL, tf), lambda j: (j // per, 0, j % per))],
        out_shape=[jax.ShapeDtypeStruct((D_FF, D_MODEL), F32),
                   jax.ShapeDtypeStruct((N_CHIPS, D_MODEL, D_FF // N_CHIPS), F32)],
        compiler_params=_params(("arbitrary",), VMEM_BIG),
    )(a, dff, n2, dz)


def _mix_bwd_call(dn2, dh2, h1, mix, cat, o_f, o_b, ga, gn, g_post, g_pre2, w_out_p):
    L = dn2.shape[0]
    tm = min(512, L)
    hw = GLA_HEADS * HEAD_PAD

    def body(dn2_r, dh2_r, h1_r, mix_r, cat_r, of_r, ob_r, ga_r, gn_r, gp_r, g2_r, w_r,
             dh1_r, do_r, dga_r, dos_r, dw_r, dg2_r, dgp_r, dgn_r):
        @pl.when(pl.program_id(0) == 0)
        def _():
            for ref in (dw_r, dg2_r, dgp_r, dgn_r):
                ref[...] = jnp.zeros_like(ref)

        parts = [slice(start, start + min(256, tm)) for start in range(0, tm, 256)]
        dmix_m = []
        for rs in parts:
            h1 = h1_r[rs, :]
            dx2, dg2 = _rms_bwd(h1, _rms_r(h1), g2_r[...], dn2_r[rs, :])
            dh1 = dh2_r[rs, :] + dx2
            dh1_r[rs, :] = dh1
            dg2_r[...] += dg2
            mix = mix_r[rs, :]
            dmix, dgp = _rms_bwd(mix, _rms_r(mix), gp_r[...], dh1)
            dgp_r[...] += dgp
            dmix_m.append(_mx(dmix))
        dcat = [_dot_nt(d, w_r[...]) for d in dmix_m]
        for rs, d in zip(parts, dmix_m):
            dw_r[...] += _dot_tn(cat_r[rs, :], d)
        gn_v = gn_r[...]
        dgn = jnp.zeros((1, HEAD_PAD), F32)
        for rs, dc in zip(parts, dcat):
            dos_r[rs, :] = _spread_heads(dc[:, hw:]).astype(dos_r.dtype)
            for h in range(GLA_HEADS):
                sl = slice(HEAD_PAD * h, HEAD_PAD * (h + 1))
                oh = of_r[rs, sl] + ob_r[rs, sl]
                rr = _rms_r(oh)
                xh = oh * rr
                gate = ga_r[rs, sl]
                sg = jax.nn.sigmoid(gate)
                silu = gate * sg
                doa = dc[:, sl]
                dga_r[rs, sl] = (doa * (xh * gn_v) * (sg + silu * (1.0 - sg))).astype(dga_r.dtype)
                don = doa * silu
                gd = don * gn_v
                do_r[rs, sl] = rr * (gd - xh * jnp.mean(gd * xh, axis=-1, keepdims=True))
                dgn = dgn + jnp.sum(don * xh, axis=0, keepdims=True)
        dgn_r[...] += dgn

    return pl.pallas_call(
        body, name="mix_bwd", grid=(L // tm,),
        in_specs=[_row_spec(tm, D_MODEL)] * 4 + [_row_spec(tm, OUT_PAD)] + [_row_spec(tm, hw)] * 3
        + [_full_spec((1, HEAD_PAD)), _full_spec((1, D_MODEL)), _full_spec((1, D_MODEL)), _vmem_spec()],
        out_specs=[_row_spec(tm, D_MODEL), _row_spec(tm, hw), _row_spec(tm, hw),
                   _row_spec(tm, SWA_Q_HEADS * HEAD_PAD),
                   _full_spec((OUT_PAD, D_MODEL)), _full_spec((1, D_MODEL)), _full_spec((1, D_MODEL)),
                   _full_spec((1, HEAD_PAD))],
        out_shape=[jax.ShapeDtypeStruct((L, D_MODEL), F32), jax.ShapeDtypeStruct((L, hw), F32),
                   jax.ShapeDtypeStruct((L, hw), MXU_DTYPE),
                   jax.ShapeDtypeStruct((L, SWA_Q_HEADS * HEAD_PAD), MXU_DTYPE),
                   jax.ShapeDtypeStruct((OUT_PAD, D_MODEL), F32), jax.ShapeDtypeStruct((1, D_MODEL), F32),
                   jax.ShapeDtypeStruct((1, D_MODEL), F32), jax.ShapeDtypeStruct((1, HEAD_PAD), F32)],
        compiler_params=_params(("arbitrary",), VMEM_BIG),
    )(dn2, dh2, h1, mix, cat, o_f, o_b, ga, gn, g_post, g_pre2, w_out_p)


def _in_bwd_call(x, dh1, g_pre, w_in_t, pairs, singles, halos, dep=None):
    L = x.shape[0]
    tm = min(512, L)
    per = tm // SWA_BLOCK
    n_pair, n_single, n_halo = len(pairs), len(singles), len(halos)
    groups = [c for c, _ in pairs] + [c for c, _ in singles] + [c for c, _ in halos]

    def body(*refs):
        x_r, dh1_r, g_r, w_r = refs[:4]
        pair_refs = refs[4:4 + 2 * n_pair]
        single_refs = refs[4 + 2 * n_pair:4 + 2 * n_pair + n_single]
        halo_refs = refs[4 + 2 * n_pair + n_single:4 + 2 * n_pair + n_single + per * n_halo]
        dx_r, dw_r, dg_r = refs[4 + 2 * n_pair + n_single + per * n_halo:]

        @pl.when(pl.program_id(0) == 0)
        def _():
            dw_r[...] = jnp.zeros_like(dw_r)
            dg_r[...] = jnp.zeros_like(dg_r)

        xv = x_r[...]
        r = _rms_r(xv)
        g = g_r[...]
        u = _mx(xv * r * g)
        vals = [pair_refs[2 * i][...].astype(F32) + pair_refs[2 * i + 1][...].astype(F32) for i in range(n_pair)]
        vals += [ref[...].astype(F32) for ref in single_refs]
        vals += [jnp.concatenate([ref[...] for ref in halo_refs[per * i:per * (i + 1)]], axis=0)
                 for i in range(n_halo)]
        ds = [_mx(_squeeze_heads(val) if heads else val) for (_, _, heads), val in zip(groups, vals)]
        du = jnp.zeros((tm, D_MODEL), F32)
        for (first, rows, _), d in zip(groups, ds):
            du = du + _dot(d, w_r[first:first + rows, :])
        for (first, rows, _), d in zip(groups, ds):
            dw_r[first:first + rows, :] += _dot_tn(d, u)
        dx, dg = _rms_bwd(xv, r, g, du)
        dx_r[...] = dh1_r[...] + dx
        dg_r[...] += dg

    arrays = [a for _, pr in pairs for a in pr] + [a for _, a in singles]
    specs = [_row_spec(tm, a.shape[1]) for a in arrays]
    for _, a in halos:
        specs += [pl.BlockSpec((SWA_BLOCK, a.shape[1]), lambda i, j=j: (per * i + 1 + j, 0)) for j in range(per)]
        arrays += [a] * per
    body, extra, extra_specs = _after(body, 4 + len(arrays), dep)
    return pl.pallas_call(
        body, name="in_bwd", grid=(L // tm,),
        in_specs=[_row_spec(tm, D_MODEL), _row_spec(tm, D_MODEL), _full_spec((1, D_MODEL)), _vmem_spec()] + specs
        + extra_specs,
        out_specs=[_row_spec(tm, D_MODEL), _full_spec((IN_COLS, D_MODEL)), _full_spec((1, D_MODEL))],
        out_shape=[jax.ShapeDtypeStruct((L, D_MODEL), F32), jax.ShapeDtypeStruct((IN_COLS, D_MODEL), F32),
                   jax.ShapeDtypeStruct((1, D_MODEL), F32)],
        compiler_params=_params(("arbitrary",), VMEM_BIG),
    )(x, dh1, g_pre, w_in_t, *arrays, *extra)


def _adamw_math(w, g, m, v):
    m = ADAM_B1 * m + (1.0 - ADAM_B1) * g
    v = ADAM_B2 * v + (1.0 - ADAM_B2) * (g * g)
    m_hat = m / (1.0 - ADAM_B1 ** ADAM_STEP)
    v_hat = v / (1.0 - ADAM_B2 ** ADAM_STEP)
    delta = -ADAM_LR * (m_hat / (jnp.sqrt(v_hat) + ADAM_EPS) + ADAM_WD * w)
    return delta, m, v


def _adamw_call(w, g, m, v, name, dep=None):
    rows, cols = w.shape
    tr = min(256, rows)

    def body(w_r, g_r, m_r, v_r, g_out_r, d_r, nm_r, nv_r):
        g = g_r[...]
        g_out_r[...] = g
        d_r[...], nm_r[...], nv_r[...] = _adamw_math(w_r[...], g, m_r[...], v_r[...])

    if rows % tr == 0:
        spec, steps = _row_spec(tr, cols), rows // tr
    else:
        spec, steps = pl.BlockSpec((rows, 256), lambda i: (0, i)), cols // 256
    body, extra, extra_specs = _after(body, 4, dep)
    return pl.pallas_call(
        body, name=name, grid=(steps,),
        in_specs=[spec] * 4 + extra_specs, out_specs=[spec] * 4,
        out_shape=[jax.ShapeDtypeStruct(w.shape, F32)] * 4,
        compiler_params=_params(("arbitrary",)),
    )(w, g, m, v, *extra)


def _position():
    return lax.axis_index("x"), lax.axis_index("y"), lax.axis_index("c")


def _other_chips(x, y):
    return [(1 - x, y), (x, 1 - y), (1 - x, 1 - y)]


ROWS, COLS = -2, -1


def _half(ref, which, axis):
    size = ref.shape[axis] // 2
    span = pl.ds(pl.multiple_of(which * size, 16 if axis == ROWS else 128), size)
    index = [slice(None)] * len(ref.shape)
    index[axis] = span
    return ref.at[tuple(index)]


def _quarter(ref, half, which, axis):
    size = ref.shape[axis] // 4
    span = pl.ds(pl.multiple_of((2 * half + which) * size, 16 if axis == ROWS else 128), size)
    index = [slice(None)] * len(ref.shape)
    index[axis] = span
    return ref.at[tuple(index)]


def _first_gather_call(shards, axes, routed):
    n = len(shards)
    per = 7

    def body(*refs):
        srcs, outs = refs[:n], refs[n:2 * n]
        send_sems, recv_sems, local_sems = refs[2 * n:]
        x, y, c = _position()
        me, sibling = (x, y, c), (x, y, 1 - c)
        x_side, y_side, across = _other_chips(x, y)
        local = [pltpu.make_async_copy(srcs[a], outs[a].at[2 * x + y], local_sems.at[a]) for a in range(n)]
        for cp in local:
            cp.start()

        def copy(a, k, dst, to, src=None):
            return pltpu.make_async_remote_copy(
                src_ref=dst if src is None else src, dst_ref=dst, send_sem=send_sems.at[per * a + k],
                recv_sem=recv_sems.at[per * a + k], device_id=to, device_id_type=MESH_ID)

        def half(a, chip, pc):
            return _half(outs[a].at[2 * chip[0] + chip[1]], pc, axes[a])

        def quarter(a, chip, q):
            return _quarter(outs[a].at[2 * chip[0] + chip[1]], c, q, axes[a])

        sends = []
        for a in range(n):
            mine = _half(srcs[a], c, axes[a])
            targets = (x_side, y_side) if routed[a] else (x_side, y_side, across)
            sends += [copy(a, j, half(a, (x, y), c), (*chip, c), src=mine) for j, chip in enumerate(targets)]
        for cp in sends:
            cp.start()
        for a in range(n):
            for j, chip in enumerate((x_side, y_side)):
                copy(a, j, half(a, chip, c), me).wait_recv()
                if routed[a]:
                    other = (y_side, x_side)[j]
                    sends.append(copy(a, 2 + j, quarter(a, chip, j), (*other, c)))
                    sends[-1].start()
                sends.append(copy(a, 4 + j, half(a, chip, c), sibling))
                sends[-1].start()
        for a in range(n):
            if routed[a]:
                for j in range(2):
                    copy(a, 2 + j, quarter(a, across, j), me).wait_recv()
            else:
                copy(a, 2, half(a, across, c), me).wait_recv()
            sends.append(copy(a, 6, half(a, across, c), sibling))
            sends[-1].start()
        for a in range(n):
            for k, chip in ((4, x_side), (5, y_side), (6, across)):
                copy(a, k, half(a, chip, 1 - c), me).wait_recv()
        for cp in sends:
            cp.wait_send()
        for cp in local:
            cp.wait()

    return pl.pallas_call(
        body, name="first_gather",
        in_specs=[_any_spec()] * n, out_specs=[_any_spec()] * n,
        out_shape=[jax.ShapeDtypeStruct((N_CHIPS,) + s.shape, s.dtype) for s in shards],
        scratch_shapes=[pltpu.SemaphoreType.DMA((per * n,)), pltpu.SemaphoreType.DMA((per * n,)),
                        pltpu.SemaphoreType.DMA((n,))],
    )(*shards)


PAIR_PEERS, CHIP_PEERS = 1, 2


def _peers(which):
    x, y, c = _position()
    if which == PAIR_PEERS:
        return [(x, y, 1 - c)]
    return [(px, py, c) for px, py in _other_chips(x, y)]


def _split_start(name, arrays, n_copies, plan, peers=None):
    n = len(arrays)

    def body(*refs):
        ins, send_sems, recv_sems, token = refs[:n], refs[n], refs[n + 1], refs[-1]
        if peers is not None:
            barrier = pltpu.get_barrier_semaphore()
            targets = _peers(peers)
            for target in targets:
                pl.semaphore_signal(barrier, inc=1, device_id=target, device_id_type=MESH_ID)
            pl.semaphore_wait(barrier, len(targets))
        for k, (src, dst, to, _) in enumerate(plan(ins)):
            pltpu.make_async_remote_copy(src_ref=src, dst_ref=dst, send_sem=send_sems.at[k],
                                         recv_sem=recv_sems.at[k], device_id=to, device_id_type=MESH_ID).start()
        token[...] = jnp.zeros_like(token)

    hbm = pl.BlockSpec(memory_space=pltpu.HBM)
    sem = pl.BlockSpec(memory_space=pltpu.SEMAPHORE)
    out = pl.pallas_call(
        body, name=name,
        out_shape=(pltpu.SemaphoreType.DMA((n_copies,)), pltpu.SemaphoreType.DMA((n_copies,)))
        + tuple(pltpu.HBM(a.shape, a.dtype) for a in arrays) + (jax.ShapeDtypeStruct((8, 128), F32),),
        in_specs=[hbm] * n, out_specs=(sem, sem) + (hbm,) * n + (_vmem_spec(),),
        input_output_aliases={i: 2 + i for i in range(n)},
        compiler_params=pltpu.CompilerParams(has_side_effects=pltpu.SideEffectType.DATAFLOW_SIDE_EFFECTING,
                                             collective_id=peers),
    )(*[pltpu.with_memory_space_constraint(a, pltpu.HBM) for a in arrays])
    return (out[0], out[1], tuple(out[2:2 + n])), out[-1]


def _split_wait(name, handle, n_copies, plan, after):
    send_sems, recv_sems, arrays = handle
    n = len(arrays)

    def body(*refs):
        ins, s_sems, r_sems = refs[:n], refs[n], refs[n + 1]
        for k, (src, dst, to, landed) in enumerate(plan(ins)):
            cp = pltpu.make_async_remote_copy(src_ref=src, dst_ref=landed, send_sem=s_sems.at[k],
                                              recv_sem=r_sems.at[k], device_id=to, device_id_type=MESH_ID)
            cp.wait_send()
            cp.wait_recv()

    hbm = pl.BlockSpec(memory_space=pltpu.HBM)
    sem = pl.BlockSpec(memory_space=pltpu.SEMAPHORE)
    out = pl.pallas_call(
        body, name=name,
        out_shape=tuple(pltpu.HBM(a.shape, a.dtype) for a in arrays),
        in_specs=[hbm] * n + [sem, sem, _any_spec()], out_specs=(hbm,) * n,
        input_output_aliases={i: i for i in range(n)},
        compiler_params=pltpu.CompilerParams(has_side_effects=pltpu.SideEffectType.DATAFLOW_SIDE_EFFECTING),
    )(*arrays, send_sems, recv_sems, after)
    return tuple(out)


def _gather_plans(axes):
    n = len(axes)

    def stage_one(refs):
        x, y, c = _position()
        copies = []
        for a, axis in enumerate(axes):
            for px, py in _other_chips(x, y):
                copies.append((_half(refs[a], c, axis), _half(refs[n + a].at[2 * x + y], c, axis),
                               (px, py, c), _half(refs[n + a].at[2 * px + py], c, axis)))
        return copies

    def stage_two(refs):
        x, y, c = _position()
        copies = []
        for a, axis in enumerate(axes):
            for px, py in _other_chips(x, y):
                piece = _half(refs[n + a].at[2 * px + py], c, axis)
                copies.append((piece, piece, (x, y, 1 - c), _half(refs[n + a].at[2 * px + py], 1 - c, axis)))
        return copies

    return stage_one, stage_two


def _pair_swap_plan(axes):
    n = len(axes)

    def plan(refs):
        x, y, c = _position()
        return [(_half(refs[a], 1 - c, axes[a]), refs[n + a], (x, y, 1 - c), refs[n + a]) for a in range(n)]

    return plan


def _chip_swap_plan(n):
    def plan(refs):
        x, y, c = _position()
        copies = []
        for a in range(n):
            for j, (px, py) in enumerate(_other_chips(x, y)):
                copies.append((refs[a].at[2 * px + py], refs[n + a].at[j], (px, py, c), refs[n + a].at[j]))
        return copies

    return plan


def _pair_join_plan(axes):
    def plan(refs):
        x, y, c = _position()
        copies = []
        for a, axis in enumerate(axes):
            mine = _half(refs[a], c, axis)
            copies.append((mine, mine, (x, y, 1 - c), _half(refs[a], 1 - c, axis)))
        return copies

    return plan


def _pair_add_call(gs, gots, pos, name, axes):
    n = len(gs)

    def body(pos_r, *refs):
        for g_r, got_r, o_r in zip(refs[:n], refs[n:2 * n], refs[2 * n:]):
            o_r[...] = (g_r[...] + got_r[...]).astype(o_r.dtype)

    def mine(axis):
        return (lambda j, p: (j, p[1], 0)) if axis == ROWS else (lambda j, p: (j, 0, p[1]))

    blocks = [(None,) + got.shape[1:] for got in gots]
    return pl.pallas_call(
        body, name=name,
        grid_spec=pltpu.PrefetchScalarGridSpec(
            num_scalar_prefetch=1, grid=(N_CHIPS,),
            in_specs=[pl.BlockSpec(blk, mine(axis)) for blk, axis in zip(blocks, axes)]
            + [pl.BlockSpec(blk, lambda j, p: (j, 0, 0)) for blk in blocks],
            out_specs=[pl.BlockSpec(blk, lambda j, p: (j, 0, 0)) for blk in blocks]),
        out_shape=[jax.ShapeDtypeStruct(got.shape, COMM_DTYPE) for got in gots],
        compiler_params=_params(("arbitrary",), VMEM_BIG),
    )(pos, *gs, *gots)


def _chip_add_call(hsums, gots, pos, name, axes):
    n = len(hsums)
    steps = 2

    def body(pos_r, *refs):
        for own_r, got_r, o_r in zip(refs[:n], refs[n:2 * n], refs[2 * n:]):
            acc = own_r[...].astype(F32)
            for j in range(3):
                acc = acc + got_r[j].astype(F32)
            o_r[...] = acc

    in_specs, got_specs, out_specs, out_shape = [], [], [], []
    for h, axis in zip(hsums, axes):
        if axis == ROWS:
            rows, cols = h.shape[1] // steps, h.shape[2]
            in_specs.append(pl.BlockSpec((None, rows, cols), lambda i, p: (p[0], i, 0)))
            got_specs.append(pl.BlockSpec((3, rows, cols), lambda i, p: (0, i, 0)))
            out_specs.append(pl.BlockSpec((rows, cols), lambda i, p: (p[1] * steps + i, 0)))
            out_shape.append(jax.ShapeDtypeStruct((2 * h.shape[1], cols), F32))
        else:
            rows, cols = h.shape[1], h.shape[2] // steps
            in_specs.append(pl.BlockSpec((None, rows, cols), lambda i, p: (p[0], 0, i)))
            got_specs.append(pl.BlockSpec((3, rows, cols), lambda i, p: (0, 0, i)))
            out_specs.append(pl.BlockSpec((rows, cols), lambda i, p: (0, p[1] * steps + i)))
            out_shape.append(jax.ShapeDtypeStruct((rows, 2 * h.shape[2]), F32))
    return pl.pallas_call(
        body, name=name,
        grid_spec=pltpu.PrefetchScalarGridSpec(
            num_scalar_prefetch=1, grid=(steps,), in_specs=in_specs + got_specs, out_specs=out_specs),
        out_shape=out_shape,
        compiler_params=_params(("arbitrary",), VMEM_BIG),
    )(pos, *hsums, *gots)


SMALL_NAMES = ("norm_mix_pre", "norm_mix_post", "norm_mlp_pre", "norm_mlp_post", "b_gate_fwd", "b_gate_bwd",
               "gla_norm", "swa_sink", "rel_bias")


N_DEVICES = 8


def _small_pack_call(grads, extras):
    operands = list(grads) + list(extras)

    def body(*refs):
        g_refs, (all_a, all_b) = refs[:len(operands)], refs[len(operands):]
        x, y, c = _position()
        me = 4 * x + 2 * y + c
        all_a[me] = jnp.zeros(all_a.shape[1:], F32)
        all_b[me] = jnp.zeros(all_b.shape[1:], F32)
        for i in range(4):
            all_a[me, i:i + 1, :] = g_refs[i][...]
        all_a[me, 4:5, 0:256] = g_refs[4][...]
        all_a[me, 5:6, 0:256] = g_refs[5][...]
        all_a[me, 6:7, 0:128] = g_refs[6][...]
        all_a[me, 7:8, 0:128] = g_refs[7][...]
        all_a[me, 7:8, 128:256] = g_refs[11][...]
        all_b[me, 0:32, 0:128] = g_refs[8][...]
        all_b[me, 32:48, :] = g_refs[9][...]
        all_b[me, 48:64, :] = g_refs[10][...]

    out_shape = [jax.ShapeDtypeStruct((N_DEVICES, 8, D_MODEL), F32), jax.ShapeDtypeStruct((N_DEVICES, 64, 256), F32)]
    return pl.pallas_call(
        body, name="small_pack",
        in_specs=[_whole_spec(a.shape) for a in operands], out_specs=[_whole_spec(s.shape) for s in out_shape],
        out_shape=out_shape,
    )(*operands)


def _everyone_plan(n):
    def plan(refs):
        x, y, c = _position()
        copies = []
        for k in range(1, N_DEVICES):
            px = 1 - x if (k >> 2) & 1 else x
            py = 1 - y if (k >> 1) & 1 else y
            pc = 1 - c if k & 1 else c
            for a in range(n):
                mine = refs[a].at[4 * x + 2 * y + c]
                copies.append((mine, mine, (px, py, pc), refs[a].at[4 * px + 2 * py + pc]))
        return copies

    return plan


def _small_adamw_call(all_a, all_b, params):
    n_small = len(SMALL_NAMES)
    wmv = [t for p in params for t in p]
    shapes = [p[0].shape for p in params]

    def body(*refs):
        all_a, all_b = refs[:2]
        wmv_refs = refs[2:2 + 3 * n_small]
        out_refs = refs[2 + 3 * n_small:]
        sum_a, sum_b = all_a[0], all_b[0]
        for d in range(1, N_DEVICES):
            sum_a = sum_a + all_a[d]
            sum_b = sum_b + all_b[d]
        gsum = [sum_a[0:1], sum_a[1:2], sum_a[2:3], sum_a[3:4], sum_a[4:5, 0:256], sum_a[5:6, 0:256],
                sum_a[6:7, 0:128], sum_a[7:8, 0:SWA_Q_HEADS], sum_b[0:32, 0:SWA_Q_HEADS]]
        for i in range(n_small):
            w_r, m_r, v_r = wmv_refs[3 * i:3 * i + 3]
            delta, new_m, new_v = _adamw_math(w_r[...], gsum[i], m_r[...], v_r[...])
            out_refs[4 * i][...] = gsum[i]
            out_refs[4 * i + 1][...] = delta
            out_refs[4 * i + 2][...] = new_m
            out_refs[4 * i + 3][...] = new_v
        out_refs[4 * n_small][...] = sum_b[32:48]
        out_refs[4 * n_small + 1][...] = sum_b[48:64]
        out_refs[4 * n_small + 2][...] = sum_a[7:8, 128:256]

    out_shape = [jax.ShapeDtypeStruct(s, F32) for s in shapes for _ in range(4)]
    out_shape += [jax.ShapeDtypeStruct((GLA_GATE_RANK, 256), F32)] * 2 + [jax.ShapeDtypeStruct((1, 128), F32)]
    out = pl.pallas_call(
        body, name="small_adamw",
        in_specs=[_whole_spec(a.shape) for a in [all_a, all_b] + wmv],
        out_specs=[_whole_spec(s.shape) for s in out_shape],
        out_shape=out_shape,
    )(all_a, all_b, *wmv)
    per_name = [tuple(out[4 * i:4 * i + 4]) for i in range(n_small)]
    return per_name, out[4 * n_small], out[4 * n_small + 1], out[4 * n_small + 2]


def _pad_gate(w, first_row):
    return jnp.pad(w, ((first_row, 128 - GLA_GATE_RANK - first_row), (0, 0)))


def _own_slot(shard, chip):
    zone = lax.empty((N_CHIPS,) + shard.shape, shard.dtype)
    return lax.dynamic_update_slice(zone, shard[None], (chip,) + (0,) * shard.ndim)


def _reduce_to_owners(grads, axes, pos, tag, overlap):
    n = len(grads)

    def half_shape(g, axis):
        return (N_CHIPS, g.shape[1] // 2, g.shape[2]) if axis == ROWS else (N_CHIPS, g.shape[1], g.shape[2] // 2)

    lands = [lax.empty(half_shape(g, axis), F32) for g, axis in zip(grads, axes)]
    handle, token = _split_start(tag + "_pair_start", list(grads) + lands, n, _pair_swap_plan(axes), PAIR_PEERS)
    got = _split_wait(tag + "_pair_wait", handle, n, _pair_swap_plan(axes), overlap[0](token))
    sums = list(_pair_add_call(got[:n], got[n:], pos, tag + "_pair_add", axes))
    lands = [lax.empty((3,) + s.shape[1:], s.dtype) for s in sums]
    handle, token = _split_start(tag + "_chip_start", sums + lands, 3 * n, _chip_swap_plan(n), CHIP_PEERS)
    got = _split_wait(tag + "_chip_wait", handle, 3 * n, _chip_swap_plan(n), overlap[1](token))
    halves = list(_chip_add_call(got[:n], got[n:], pos, tag + "_chip_add", axes))
    handle, token = _split_start(tag + "_join_start", halves, n, _pair_join_plan(axes), PAIR_PEERS)
    return _split_wait(tag + "_join_wait", handle, n, _pair_join_plan(axes), overlap[2](token))


def kernel(x, norm_mix_pre, w_in, w_gate_up_fwd, b_gate_fwd, w_gate_up_bwd, b_gate_bwd, gla_norm, swa_sink, rel_bias, w_out, norm_mix_post, norm_mlp_pre, w_up, w_down, norm_mlp_post, loss_target, m_norm_mix_pre, m_w_in, m_w_gate_up_fwd, m_b_gate_fwd, m_w_gate_up_bwd, m_b_gate_bwd, m_gla_norm, m_swa_sink, m_rel_bias, m_w_out, m_norm_mix_post, m_norm_mlp_pre, m_w_up, m_w_down, m_norm_mlp_post, v_norm_mix_pre, v_w_in, v_w_gate_up_fwd, v_b_gate_fwd, v_w_gate_up_bwd, v_b_gate_bwd, v_gla_norm, v_swa_sink, v_rel_bias, v_w_out, v_norm_mix_post, v_norm_mlp_pre, v_w_up, v_w_down, v_norm_mlp_post):
    given = dict(locals())
    cx, cy, cc = _position()
    chip = (2 * cx + cy).astype(jnp.int32)
    pos = jnp.stack([chip, cc.astype(jnp.int32)])
    seq, tgt = x[0], loss_target[0]

    gates = jnp.concatenate([w_gate_up_fwd[0], w_gate_up_bwd[0]], axis=0).astype(COMM_DTYPE)
    all_in, all_gates = _first_gather_call([w_in[0].T.astype(COMM_DTYPE), gates], [COLS, ROWS], [True, False])
    rest = [w_out[0].astype(COMM_DTYPE), jnp.stack([w_up[0], w_down[0]]).astype(COMM_DTYPE)]
    stage_one, stage_two = _gather_plans([ROWS, ROWS])
    handle, token = _split_start("gather_chip_start", rest + [_own_slot(s, chip) for s in rest] + [all_gates], 6,
                                 stage_one, CHIP_PEERS)

    w_in_t = _mx(all_in.reshape(IN_COLS, D_MODEL))
    gates_full = jnp.concatenate([all_gates[j] for j in range(N_CHIPS)], axis=1)
    wgf_p = _mx(_pad_gate(gates_full[:GLA_GATE_RANK], 0))
    wgb_p = _mx(_pad_gate(gates_full[GLA_GATE_RANK:], GLA_GATE_RANK))
    bf_p, bb_p = b_gate_fwd, b_gate_bwd
    buckets = jnp.asarray(_band_buckets())
    sink1 = swa_sink.reshape(SWA_Q_HEADS)

    qa, ka, va, ga, qs, ks, vs, za = _proj_call(seq, norm_mix_pre, w_in_t, dep=token)
    halo = ((SWA_BLOCK, SWA_BLOCK), (0, 0))
    ks_p, vs_p = jnp.pad(ks, halo), jnp.pad(vs, halo)
    o_f, o_b, s_f, s_b = _gla_fwd_call(qa, ka, va, za, wgf_p, bf_p, wgb_p, bb_p)
    bias = _bias_call(rel_bias, buckets, dep=o_f)
    arrays = _split_wait("gather_chip_wait", handle, 6, stage_one, bias)
    handle, token = _split_start("gather_pair_start", list(arrays), 6, stage_two, PAIR_PEERS)
    o_s = _swa_fwd_call(qs, ks_p, vs_p, bias, sink1, dep=token)
    arrays = _split_wait("gather_pair_wait", handle, 6, stage_two, o_s)
    w_out_full = _mx(arrays[2].reshape(N_CHIPS * R_OUT, D_MODEL))
    w_ud = _mx(arrays[3])
    cat, mix, h1, n2 = _mix_call(o_f, o_b, ga, o_s, seq, gla_norm, w_out_full, norm_mix_post, norm_mlp_pre)
    a, rz, dh2, dff, loss, d_post2 = _mlp_fwd_call(n2, h1, tgt, w_ud, norm_mlp_post)

    dz, dn2 = _mlp_bwd_call(dff, rz, w_ud)
    dw_down, dw_up4 = _mlp_wgrad_call(a, dff, n2, dz)
    dh1, do, dga, dos, dw_out, d_pre2, d_post, d_gn = _mix_bwd_call(
        dn2, dh2, h1, mix, cat, o_f, o_b, ga, gla_norm, norm_mix_post, norm_mlp_pre, w_out_full)
    done = {}

    def swa_backward(tok):
        done["swa"] = _swa_bwd_call(qs, ks_p, vs_p, bias, sink1, dos, dep=tok)
        return done["swa"][0]

    def gla_in_backward(tok):
        done["gla"] = _gla_bwd_call(qa, ka, va, za, do, s_f, s_b, wgf_p, bf_p, wgb_p, bb_p, dep=tok)
        dqf, dkf, dvf, dzf, _, _, dqb, dkb, dvb, dzb, _, _ = done["gla"]
        dqs, dks_p, dvs_p, _, _ = done["swa"]
        done["in"] = _in_bwd_call(
            seq, dh1, norm_mix_pre, w_in_t,
            pairs=[(_side_by_side(T_QA), (dqf, dqb)), (_side_by_side(T_KA), (dkf, dkb)), (T_VA, (dvf, dvb)),
                   (T_ZA, (dzf, dzb))],
            singles=[(T_GA, dga), (_side_by_side(T_QS), dqs)], halos=[(T_KS, dks_p), (T_VS, dvs_p)])
        return done["in"][0]

    def bias_backward(tok):
        done["rel"] = _relbias_call(done["swa"][3], done["swa"][4], buckets, dep=tok)
        return done["rel"][0]

    g_up, g_down, g_out = _reduce_to_owners(
        [dw_up4, dw_down.reshape(N_CHIPS, R_DOWN, D_MODEL), dw_out.reshape(N_CHIPS, R_OUT, D_MODEL)],
        [ROWS, ROWS, ROWS], pos, "mlp", [swa_backward, gla_in_backward, bias_backward])
    dx, dw_in_t, d_pre = done["in"]
    dwf, dbf, dwb, dbb = done["gla"][4], done["gla"][5], done["gla"][10], done["gla"][11]
    drel, dsink = done["rel"]

    small_grads = [d_pre, d_post, d_pre2, d_post2, dbf, dbb, d_gn, dsink, drel]
    gate_grads = [dwf[:GLA_GATE_RANK], dwb[GLA_GATE_RANK:2 * GLA_GATE_RANK]]
    small_params = [(given[n], given["m_" + n], given["v_" + n]) for n in SMALL_NAMES]
    upd = {}

    everyone = _everyone_plan(2)
    small_handle, small_token = _split_start(
        "small_start", list(_small_pack_call(small_grads, gate_grads + [loss])), 2 * (N_DEVICES - 1), everyone)

    def update_out(tok):
        upd["w_out"] = tuple(_adamw_call(w_out[0], g_out, m_w_out[0], v_w_out[0], "adamw_w_out",
                                         dep=tok + small_token))
        return upd["w_out"][1]

    def update_mlp(tok):
        upd["w_up"] = tuple(_adamw_call(w_up[0], g_up, m_w_up[0], v_w_up[0], "adamw_w_up", dep=tok))
        upd["w_down"] = tuple(
            _adamw_call(w_down[0], g_down, m_w_down[0], v_w_down[0], "adamw_w_down", dep=upd["w_up"][1]))
        all_a, all_b = _split_wait("small_wait", small_handle, 2 * (N_DEVICES - 1), everyone, upd["w_down"][1])
        per_name, done["gf_sum"], done["gb_sum"], upd["loss"] = _small_adamw_call(all_a, all_b, small_params)
        upd.update(dict(zip(SMALL_NAMES, per_name)))
        return per_name[0][1]

    def update_gates(tok):
        for name, total in (("w_gate_up_fwd", done["gf_sum"]), ("w_gate_up_bwd", done["gb_sum"])):
            g = lax.dynamic_slice(total, (0, chip * 64), (GLA_GATE_RANK, 64))
            upd[name] = tuple(_adamw_call(given[name][0], g, given["m_" + name][0], given["v_" + name][0],
                                          "adamw_" + name, dep=tok))
        return upd["w_gate_up_bwd"][1]

    (g_in_t,) = _reduce_to_owners([dw_in_t.reshape(N_CHIPS, R_IN, D_MODEL)], [COLS], pos, "in",
                                  [update_out, update_mlp, update_gates])
    upd["w_in"] = tuple(t.T for t in _adamw_call(w_in[0].T, g_in_t, m_w_in[0].T, v_w_in[0].T, "adamw_w_in"))

    big = ("w_in", "w_gate_up_fwd", "w_gate_up_bwd", "w_out", "w_up", "w_down")
    names = ["norm_mix_pre", "w_in", "w_gate_up_fwd", "b_gate_fwd", "w_gate_up_bwd", "b_gate_bwd", "gla_norm",
             "swa_sink", "rel_bias", "w_out", "norm_mix_post", "norm_mlp_pre", "w_up", "w_down", "norm_mlp_post"]
    outs = [upd["loss"][0, 0], dx[None]]
    for kind in range(4):
        outs += [upd[n][kind][None] if n in big else upd[n][kind] for n in names]
    return tuple(outs)
```

```python
import math

import numpy as np
import jax
import jax.numpy as jnp
from jax import lax
from jax.experimental import pallas as pl
from jax.experimental.pallas import tpu as pltpu

F32 = jnp.float32
MXU_DTYPE = jnp.bfloat16
COMM_DTYPE = jnp.bfloat16

D_MODEL = 1024
D_FF = 4096
N_CHIPS = 4
GLA_HEADS = 4
GLA_CHUNK = 64
GLA_GATE_RANK = 16
GLA_GATE_NORM = 16.0
SWA_Q_HEADS = 8
SWA_KV_HEADS = 2
SWA_BLOCK = 128
REL_BUCKETS = 32
REL_MAX_DIST = 128
NORM_EPS = 1e-6
HEAD_PAD = 128

ADAM_LR = 0.001
ADAM_B1 = 0.9
ADAM_B2 = 0.999
ADAM_EPS = 1e-08
ADAM_WD = 0.01
ADAM_STEP = 10

OUT_PAD = 1024

R_IN, R_OUT, R_DOWN = 584, 256, 1024

VMEM_BIG = 56 * 1024 * 1024
MESH_ID = pl.DeviceIdType.MESH


def _mx(a):
    return a.astype(MXU_DTYPE)


def _dot(a, b):
    return jnp.dot(a, b, preferred_element_type=F32)


def _dot_nt(a, b):
    return lax.dot_general(a, b, (((1,), (1,)), ((), ())), preferred_element_type=F32)


def _dot_tn(a, b):
    return lax.dot_general(a, b, (((0,), (0,)), ((), ())), preferred_element_type=F32)


def _rms_r(x):
    return lax.rsqrt(jnp.mean(x * x, axis=-1, keepdims=True) + NORM_EPS)


def _rms_bwd(x, r, g, dy):
    xh = x * r
    gdy = dy * g
    dx = r * (gdy - xh * jnp.mean(gdy * xh, axis=-1, keepdims=True))
    return dx, jnp.sum(dy * xh, axis=0, keepdims=True)


def _low_half(rows):
    return lax.broadcasted_iota(jnp.int32, (rows, HEAD_PAD), 1) < 64


def _spread_heads(x):
    low = _low_half(x.shape[0])
    parts = []
    for p in range(x.shape[1] // HEAD_PAD):
        pair = x[:, HEAD_PAD * p:HEAD_PAD * (p + 1)]
        parts += [jnp.where(low, pair, 0.0), jnp.where(low, pltpu.roll(pair, 64, 1), 0.0)]
    return jnp.concatenate(parts, axis=1)


def _squeeze_heads(x):
    low = _low_half(x.shape[0])
    parts = []
    for p in range(x.shape[1] // (2 * HEAD_PAD)):
        even = x[:, 2 * HEAD_PAD * p:2 * HEAD_PAD * p + HEAD_PAD]
        odd = x[:, 2 * HEAD_PAD * p + HEAD_PAD:2 * HEAD_PAD * (p + 1)]
        parts.append(jnp.where(low, even, pltpu.roll(odd, 64, 1)))
    return parts[0] if len(parts) == 1 else jnp.concatenate(parts, axis=1)


def _params(sem=None, vmem=None):
    kw = {}
    if sem is not None:
        kw["dimension_semantics"] = sem
    if vmem is not None:
        kw["vmem_limit_bytes"] = vmem
    return pltpu.CompilerParams(**kw)


def _vmem_spec():
    return pl.BlockSpec(memory_space=pltpu.VMEM)


def _whole_spec(shape):
    return pl.BlockSpec(shape, lambda: (0,) * len(shape))


def _row_spec(tm, width):
    return pl.BlockSpec((tm, width), lambda i: (i, 0))


def _full_spec(shape):
    return pl.BlockSpec(shape, lambda i: (0,) * len(shape))


def _any_spec():
    return pl.BlockSpec(memory_space=pl.ANY)


def _after(body, n_in, dep):
    if dep is None:
        return body, [], []
    return (lambda *refs: body(*refs[:n_in], *refs[n_in + 1:])), [dep], [_any_spec()]


T_QA, T_KA, T_VA, T_GA = (0, 256, 4), (256, 256, 4), (512, 512, 0), (1024, 512, 0)
T_QS, T_KS, T_VS = (1568, 512, 8), (2080, 128, 2), (2208, 128, 2)
T_ZA = (1536, 128, 0)
ZA_COLS = 2 * GLA_GATE_RANK
IN_COLS = 2336


def _side_by_side(group):
    return group[0], group[1], 0


def _proj_call(x, g_pre, w_in_t, dep=None):
    L = x.shape[0]
    tm = min(512, L)
    groups = [(T_QA, F32), (T_KA, F32), (T_VA, MXU_DTYPE), (T_GA, F32),
              (T_QS, MXU_DTYPE), (T_KS, MXU_DTYPE), (T_VS, MXU_DTYPE), (T_ZA, F32)]
    widths = [rows * (2 if heads else 1) for (_, rows, heads), _ in groups]

    def body(x_ref, g_ref, w_ref, *outs):
        xv = x_ref[...]
        u = _mx(xv * _rms_r(xv) * g_ref[...])
        for ref, (grp, _) in zip(outs, groups):
            first, rows, heads = grp
            val = _dot_nt(u, w_ref[first:first + rows, :])
            if heads:
                val = _spread_heads(val)
            if grp is T_ZA:
                val = jnp.where(lax.broadcasted_iota(jnp.int32, val.shape, 1) < ZA_COLS, val, 0.0)
            if grp is T_QS:
                val = val * 0.125
            ref[...] = val.astype(ref.dtype)

    body, extra, extra_specs = _after(body, 3, dep)
    return pl.pallas_call(
        body, name="proj_fwd", grid=(L // tm,),
        in_specs=[_row_spec(tm, D_MODEL), _full_spec((1, D_MODEL)), _vmem_spec()] + extra_specs,
        out_specs=[_row_spec(tm, w) for w in widths],
        out_shape=[jax.ShapeDtypeStruct((L, w), dt) for w, (_, dt) in zip(widths, groups)],
        compiler_params=_params(("arbitrary",), VMEM_BIG),
    )(x, g_pre, w_in_t, *extra)


def _tri_masks():
    row = lax.broadcasted_iota(jnp.int32, (GLA_CHUNK, GLA_CHUNK), 0)
    col = lax.broadcasted_iota(jnp.int32, (GLA_CHUNK, GLA_CHUNK), 1)
    return row >= col, row <= col


def _chunk_sums(tri_m, x):
    hi = _mx(x)
    rest = x - hi.astype(F32)
    mid = _mx(rest)
    lo = _mx(rest - mid.astype(F32))
    return _dot(tri_m, hi) + _dot(tri_m, mid) + _dot(tri_m, lo)


def _gla_block_pre(q_r, k_r, z_r, w_r, b_r, rev, nc, qd_s, ki_s, ks_s, dec_s, keep=None):
    tri_f, tri_b = _tri_masks()
    tri_m = _mx((tri_b if rev else tri_f).astype(F32))
    g = _dot(_mx(z_r[...]), w_r[...]) + b_r[...]
    la = (jnp.minimum(g, 0.0) - jnp.log(1.0 + jnp.exp(-jnp.abs(g)))) * (1.0 / GLA_GATE_NORM)
    sums, lasts = [], []
    for c in range(nc):
        b_c = _chunk_sums(tri_m, la[GLA_CHUNK * c:GLA_CHUNK * (c + 1)])
        blast = b_c[0:1] if rev else b_c[GLA_CHUNK - 1:GLA_CHUNK]
        dec_s[c] = _spread_heads(jnp.exp(blast))
        sums.append(b_c)
        lasts.append(jnp.broadcast_to(blast, b_c.shape))
    b = jnp.concatenate(sums, axis=0)
    eb = jnp.exp(b)
    enb = jnp.exp(-b)
    elb = jnp.exp(jnp.concatenate(lasts, axis=0) - b)
    q, k = _squeeze_heads(q_r[...]), _squeeze_heads(k_r[...])
    qd_s[...] = _spread_heads(q * 0.125 * eb).astype(qd_s.dtype)
    ki_s[...] = _spread_heads(k * enb).astype(ki_s.dtype)
    ks_s[...] = _spread_heads(k * elb).astype(ks_s.dtype)
    if keep is not None:
        keep[0][...] = g
        for ref, val in zip(keep[1:], (eb, enb, elb)):
            ref[...] = _spread_heads(val)


def _gla_fwd_call(qa, ka, va, za, wgf, bgf, wgb, bgb):
    L = qa.shape[0]
    br = min(512, L)
    nb, nc, n_chunks = L // br, br // GLA_CHUNK, L // GLA_CHUNK
    hw = GLA_HEADS * HEAD_PAD

    def body(qaf, kaf, vaf, zaf, qab, kab, vab, zab, wgf_r, bgf_r, wgb_r, bgb_r,
             of_r, ob_r, sf_r, sb_r, st_f, st_b, pre_f, pre_b):
        @pl.when(pl.program_id(0) == 0)
        def _():
            st_f[...] = jnp.zeros_like(st_f)
            st_b[...] = jnp.zeros_like(st_b)

        _gla_block_pre(qaf, kaf, zaf, wgf_r, bgf_r, False, nc, *pre_f)
        _gla_block_pre(qab, kab, zab, wgb_r, bgb_r, True, nc, *pre_b)
        tri_f, tri_b = _tri_masks()

        def one(tri, pre, v_r, o_r, s_r, st, ci):
            qd_s, ki_s, ks_s, dec_s = pre
            rows = pl.ds(pl.multiple_of(ci * GLA_CHUNK, GLA_CHUNK), GLA_CHUNK)
            dec = dec_s[ci]
            heads = range(GLA_HEADS)
            lanes = [slice(HEAD_PAD * h, HEAD_PAD * (h + 1)) for h in heads]
            qd = [qd_s[rows, sl] for sl in lanes]
            v = [v_r[rows, sl] for sl in lanes]
            s_t = [st[h] for h in heads]
            a = [_dot_nt(qd[h], ki_s[rows, lanes[h]]) for h in heads]
            carried = [_dot_nt(qd[h], _mx(s_t[h])) for h in heads]
            grown = [_dot_tn(v[h], ks_s[rows, lanes[h]]) for h in heads]
            a = [_mx(jnp.where(tri, a[h], 0.0)) for h in heads]
            inner = [_dot(a[h], v[h]) for h in heads]
            for h in heads:
                s_r[ci, h] = s_t[h].astype(s_r.dtype)
                o_r[rows, lanes[h]] = inner[h] + carried[h]
                st[h] = s_t[h] * dec[:, lanes[h]] + grown[h]

        def loop(t, carry):
            one(tri_f, pre_f, vaf, of_r, sf_r, st_f, t)
            one(tri_b, pre_b, vab, ob_r, sb_r, st_b, nc - 1 - t)
            return carry

        lax.fori_loop(0, nc, loop, 0, unroll=True)

    fwd = lambda i: (i, 0)
    bwd = lambda i: (nb - 1 - i, 0)
    ins = lambda m: [pl.BlockSpec((br, hw), m), pl.BlockSpec((br, hw), m),
                     pl.BlockSpec((br, hw), m), pl.BlockSpec((br, 128), m)]
    wspecs = [_full_spec((128, hw // 2)), _full_spec((1, hw // 2))] * 2
    s_shape = (nc, GLA_HEADS, HEAD_PAD, HEAD_PAD)
    pre_scratch = [pltpu.VMEM((br, hw), MXU_DTYPE)] * 3 + [pltpu.VMEM((nc, 1, hw), F32)]
    return pl.pallas_call(
        body, name="gla_fwd", grid=(nb,),
        in_specs=ins(fwd) + ins(bwd) + wspecs,
        out_specs=[pl.BlockSpec((br, hw), fwd), pl.BlockSpec((br, hw), bwd),
                   pl.BlockSpec(s_shape, lambda i: (i, 0, 0, 0)),
                   pl.BlockSpec(s_shape, lambda i: (nb - 1 - i, 0, 0, 0))],
        out_shape=[jax.ShapeDtypeStruct((L, hw), F32), jax.ShapeDtypeStruct((L, hw), F32),
                   jax.ShapeDtypeStruct((n_chunks,) + s_shape[1:], MXU_DTYPE),
                   jax.ShapeDtypeStruct((n_chunks,) + s_shape[1:], MXU_DTYPE)],
        scratch_shapes=[pltpu.VMEM(s_shape[1:], F32), pltpu.VMEM(s_shape[1:], F32), pre_scratch, pre_scratch],
        compiler_params=_params(("arbitrary",), VMEM_BIG),
    )(qa, ka, va, za, qa, ka, va, za, wgf, bgf, wgb, bgb)


def _gla_bwd_call(qa, ka, va, za, do, sf, sb, wgf, bgf, wgb, bgb, dep=None):
    L = qa.shape[0]
    br = min(512, L)
    nb, nc = L // br, br // GLA_CHUNK
    hw = GLA_HEADS * HEAD_PAD

    def body(qaf, kaf, vaf, zaf, dof, sf_r, qab, kab, vab, zab, dob, sb_r, wgf_r, bgf_r, wgb_r, bgb_r,
             dqf, dkf, dvf, dzf, dwf, dbf, dqb, dkb, dvb, dzb, dwb, dbb, gt_f, gt_b, pre_f, pre_b):
        @pl.when(pl.program_id(0) == 0)
        def _():
            for ref in (gt_f, gt_b, dwf, dbf, dwb, dbb):
                ref[...] = jnp.zeros_like(ref)

        _gla_block_pre(qaf, kaf, zaf, wgf_r, bgf_r, False, nc, *pre_f[:4], keep=pre_f[4:8])
        _gla_block_pre(qab, kab, zab, wgb_r, bgb_r, True, nc, *pre_b[:4], keep=pre_b[4:8])
        tri_f, tri_b = _tri_masks()
        row_w = lax.broadcasted_iota(jnp.int32, (GLA_CHUNK, HEAD_PAD), 0)

        def one(rev, pre, q_r, k_r, v_r, do_r, s_r, dq_r, dk_r, dv_r, gt, ci):
            qd_s, ki_s, ks_s, dec_s, _, eb_s, enb_s, elb_s, db_s = pre
            tri = tri_b if rev else tri_f
            last_row = 0 if rev else GLA_CHUNK - 1
            rows = pl.ds(pl.multiple_of(ci * GLA_CHUNK, GLA_CHUNK), GLA_CHUNK)
            dec = dec_s[ci]
            heads = range(GLA_HEADS)
            lanes = [slice(HEAD_PAD * h, HEAD_PAD * (h + 1)) for h in heads]
            qd = [qd_s[rows, sl] for sl in lanes]
            ki = [ki_s[rows, sl] for sl in lanes]
            ks = [ks_s[rows, sl] for sl in lanes]
            v = [v_r[rows, sl] for sl in lanes]
            do_h = [_mx(do_r[rows, sl]) for sl in lanes]
            s_t = [s_r[ci, h] for h in heads]
            g_t = [gt[h] for h in heads]
            g_m = [_mx(g_t[h]) for h in heads]
            a = [_dot_nt(qd[h], ki[h]) for h in heads]
            da = [_dot_nt(do_h[h], v[h]) for h in heads]
            dv_carried = [_dot_nt(ks[h], g_m[h]) for h in heads]
            dqd_carried = [_dot(do_h[h], _mx(s_t[h])) for h in heads]
            dks = [_dot(v[h], g_m[h]) for h in heads]
            g_grown = [_dot_tn(do_h[h], qd[h]) for h in heads]
            a = [_mx(jnp.where(tri, a[h], 0.0)) for h in heads]
            da = [_mx(jnp.where(tri, da[h], 0.0)) for h in heads]
            dv_inner = [_dot_tn(a[h], do_h[h]) for h in heads]
            dqd_inner = [_dot(da[h], ki[h]) for h in heads]
            dki = [_dot_tn(da[h], qd[h]) for h in heads]
            dq, dk = [], []
            for h in heads:
                sl = lanes[h]
                dv_r[rows, sl] = (dv_inner[h] + dv_carried[h]).astype(dv_r.dtype)
                ddec = jnp.sum(g_t[h] * s_t[h].astype(F32), axis=0, keepdims=True)
                gt[h] = g_t[h] * dec[:, sl] + g_grown[h]
                dq.append((dqd_inner[h] + dqd_carried[h]) * eb_s[rows, sl] * 0.125)
                dk_state = dks[h] * elb_s[rows, sl]
                dk.append(dki[h] * enb_s[rows, sl] + dk_state)
                k = k_r[rows, sl]
                dblast = jnp.sum(dk_state * k, axis=0, keepdims=True) + dec[:, sl] * ddec
                db_s[rows, sl] = q_r[rows, sl] * dq[h] - k * dk[h] + jnp.where(row_w == last_row, dblast, 0.0)
            low = _low_half(GLA_CHUNK)
            for pair in range(GLA_HEADS // 2):
                psl = slice(HEAD_PAD * pair, HEAD_PAD * (pair + 1))
                for ref, val in ((dq_r, dq), (dk_r, dk)):
                    both = jnp.where(low, val[2 * pair], pltpu.roll(val[2 * pair + 1], 64, 1))
                    ref[rows, psl] = both.astype(ref.dtype)

        def loop(t, carry):
            one(False, pre_f, qaf, kaf, vaf, dof, sf_r, dqf, dkf, dvf, gt_f, nc - 1 - t)
            one(True, pre_b, qab, kab, vab, dob, sb_r, dqb, dkb, dvb, gt_b, t)
            return carry

        lax.fori_loop(0, nc, loop, 0, unroll=True)

        def gate_grads(rev, pre, z_r, w_r, dz_r, dw_r, dbias_r):
            g_s, db_s = pre[4], pre[8]
            back_m = _mx((tri_f if rev else tri_b).astype(F32))
            db = _squeeze_heads(db_s[...])
            dla = jnp.concatenate([_chunk_sums(back_m, db[GLA_CHUNK * c:GLA_CHUNK * (c + 1)]) for c in range(nc)],
                                  axis=0)
            dg = dla * (1.0 / GLA_GATE_NORM) * (1.0 / (1.0 + jnp.exp(g_s[...])))
            dg_m = _mx(dg)
            dz_r[...] = _dot_nt(dg_m, w_r[...])
            dw_r[...] += _dot_tn(_mx(z_r[...]), dg_m)
            dbias_r[...] += jnp.sum(dg, axis=0, keepdims=True)

        gate_grads(False, pre_f, zaf, wgf_r, dzf, dwf, dbf)
        gate_grads(True, pre_b, zab, wgb_r, dzb, dwb, dbb)

    last_first = lambda i: (nb - 1 - i, 0)
    first_last = lambda i: (i, 0)
    s_shape = (nc, GLA_HEADS, HEAD_PAD, HEAD_PAD)

    def ins(m):
        return [pl.BlockSpec((br, hw), m), pl.BlockSpec((br, hw), m), pl.BlockSpec((br, hw), m),
                pl.BlockSpec((br, 128), m), pl.BlockSpec((br, hw), m),
                pl.BlockSpec(s_shape, lambda i: m(i) + (0, 0))]

    def outs(m):
        return [pl.BlockSpec((br, hw // 2), m), pl.BlockSpec((br, hw // 2), m), pl.BlockSpec((br, hw), m),
                pl.BlockSpec((br, 128), m), _full_spec((128, hw // 2)), _full_spec((1, hw // 2))]

    out_shape = [jax.ShapeDtypeStruct((L, hw // 2), MXU_DTYPE)] * 2 + [
        jax.ShapeDtypeStruct((L, hw), MXU_DTYPE),
        jax.ShapeDtypeStruct((L, 128), F32), jax.ShapeDtypeStruct((128, hw // 2), F32),
        jax.ShapeDtypeStruct((1, hw // 2), F32)]
    wspecs = [_full_spec((128, hw // 2)), _full_spec((1, hw // 2))] * 2
    body, extra, extra_specs = _after(body, 16, dep)
    pre_scratch = ([pltpu.VMEM((br, hw), MXU_DTYPE)] * 3 + [pltpu.VMEM((nc, 1, hw), F32)]
                   + [pltpu.VMEM((br, hw // 2), F32)] + [pltpu.VMEM((br, hw), F32)] * 4)
    return pl.pallas_call(
        body, name="gla_bwd", grid=(nb,),
        in_specs=ins(last_first) + ins(first_last) + wspecs + extra_specs,
        out_specs=outs(last_first) + outs(first_last),
        out_shape=out_shape + out_shape,
        scratch_shapes=[pltpu.VMEM(s_shape[1:], F32), pltpu.VMEM(s_shape[1:], F32), pre_scratch, pre_scratch],
        compiler_params=_params(("arbitrary",), VMEM_BIG),
    )(qa, ka, va, za, do, sf, qa, ka, va, za, do, sb, wgf, bgf, wgb, bgb, *extra)


def _t5_buckets(rel):
    nb = REL_BUCKETS // 2
    ret = (rel > 0).astype(np.int32) * nb
    n = np.abs(rel)
    max_exact = nb // 2
    large = max_exact + (np.log(np.maximum(n, 1).astype(np.float32) / max_exact)
                         / math.log(REL_MAX_DIST / max_exact) * (nb - max_exact)).astype(np.int32)
    large = np.minimum(large, nb - 1)
    return ret + np.where(n < max_exact, n, large)


SWA_GROUP = SWA_Q_HEADS // SWA_KV_HEADS
SWA_SPAN = 3 * SWA_BLOCK
SWA_GROUP_LANES = SWA_GROUP * SWA_BLOCK


def _band_buckets():
    s = np.arange(SWA_SPAN)[:, None]
    c = np.arange(SWA_BLOCK)[None, :]
    return _t5_buckets(s - SWA_BLOCK - c).astype(np.int32)


def _swa_valid(n, seq_len):
    key_pos = (n - 1) * SWA_BLOCK + lax.broadcasted_iota(jnp.int32, (SWA_SPAN, 1), 0)
    return (key_pos >= 0) & (key_pos < seq_len)


def _swa_sink_row(sink_r, kv):
    lane = lax.broadcasted_iota(jnp.int32, (1, SWA_GROUP_LANES), 1)
    row = jnp.full((1, SWA_GROUP_LANES), sink_r[kv * SWA_GROUP], F32)
    for g in range(1, SWA_GROUP):
        row = jnp.where(lane >= g * SWA_BLOCK, sink_r[kv * SWA_GROUP + g], row)
    return row


SWA_STEP_BLOCKS = 8


def _swa_group(ref, kv, rows):
    first = kv * SWA_GROUP
    return jnp.concatenate([ref[rows, HEAD_PAD * h:HEAD_PAD * (h + 1)] for h in range(first, first + SWA_GROUP)],
                           axis=0)


def _swa_softmax(scores, bias_t, sink_row, valid):
    st = jnp.where(valid, scores + bias_t, -1e30)
    m = jnp.maximum(jnp.max(st, axis=0, keepdims=True), sink_row)
    p = jnp.exp(st - m)
    e_sink = jnp.exp(sink_row - m)
    inv = 1.0 / (jnp.sum(p, axis=0, keepdims=True) + e_sink)
    return p * inv, e_sink * inv


def _swa_fwd_call(qs, ks, vs, bias, sink, dep=None):
    L = qs.shape[0]

    def block(n, rows, q_r, k_r, v_r, bias_r, sink_r, o_r):
        span = pl.ds(pl.multiple_of(n * SWA_BLOCK, SWA_BLOCK), SWA_SPAN)
        valid = _swa_valid(n, L)
        groups = range(SWA_KV_HEADS)
        lanes = [slice(HEAD_PAD * kv, HEAD_PAD * (kv + 1)) for kv in groups]
        scores = [_dot_nt(k_r[span, lanes[kv]], _swa_group(q_r, kv, rows)) for kv in groups]
        probs = [_swa_softmax(scores[kv], bias_r[kv], _swa_sink_row(sink_r, kv), valid)[0] for kv in groups]
        low = _low_half(SWA_BLOCK)
        for kv in groups:
            og = _dot_tn(_mx(probs[kv]), v_r[span, lanes[kv]])
            for pair in range(SWA_GROUP // 2):
                even = og[2 * SWA_BLOCK * pair:2 * SWA_BLOCK * pair + SWA_BLOCK]
                odd = og[2 * SWA_BLOCK * pair + SWA_BLOCK:2 * SWA_BLOCK * (pair + 1)]
                first = HEAD_PAD * (kv * SWA_GROUP // 2 + pair)
                o_r[rows, first:first + HEAD_PAD] = jnp.where(low, even, pltpu.roll(odd, 64, 1)).astype(o_r.dtype)

    def body(*refs):
        for j in range(SWA_STEP_BLOCKS):
            block(SWA_STEP_BLOCKS * pl.program_id(0) + j, slice(SWA_BLOCK * j, SWA_BLOCK * (j + 1)), *refs)

    qw = SWA_Q_HEADS * HEAD_PAD
    tm = SWA_STEP_BLOCKS * SWA_BLOCK
    body, extra, extra_specs = _after(body, 5, dep)
    return pl.pallas_call(
        body, name="swa_fwd", grid=(L // tm,),
        in_specs=[_row_spec(tm, qw), _vmem_spec(), _vmem_spec(), _vmem_spec(),
                  pl.BlockSpec(memory_space=pltpu.SMEM)] + extra_specs,
        out_specs=_row_spec(tm, qw // 2),
        out_shape=jax.ShapeDtypeStruct((L, qw // 2), MXU_DTYPE),
        compiler_params=_params(("arbitrary",), VMEM_BIG),
    )(qs, ks, vs, bias, sink, *extra)


def _swa_bwd_call(qs, ks, vs, bias, sink, do, dep=None):
    L = qs.shape[0]
    qw = SWA_Q_HEADS * HEAD_PAD
    kw = SWA_KV_HEADS * HEAD_PAD

    def body(*refs):
        dk_r, dv_r, dbias_r, dsink_r = refs[7:]

        @pl.when(pl.program_id(0) == 0)
        def _():
            for ref in (dk_r, dv_r, dbias_r, dsink_r):
                ref[...] = jnp.zeros_like(ref)

        for j in range(SWA_STEP_BLOCKS):
            block(SWA_STEP_BLOCKS * pl.program_id(0) + j, slice(SWA_BLOCK * j, SWA_BLOCK * (j + 1)), *refs)

    def block(n, rows, q_r, k_r, v_r, bias_r, sink_r, do_r, dq_r, dk_r, dv_r, dbias_r, dsink_r):
        span = pl.ds(pl.multiple_of(n * SWA_BLOCK, SWA_BLOCK), SWA_SPAN)
        valid = _swa_valid(n, L)
        groups = range(SWA_KV_HEADS)
        lanes = [slice(HEAD_PAD * kv, HEAD_PAD * (kv + 1)) for kv in groups]
        kk = [k_r[span, sl] for sl in lanes]
        vv = [v_r[span, sl] for sl in lanes]
        qg = [_swa_group(q_r, kv, rows) for kv in groups]
        dog = [_swa_group(do_r, kv, rows) for kv in groups]
        scores = [_dot_nt(kk[kv], qg[kv]) for kv in groups]
        dp = [_dot_nt(vv[kv], dog[kv]) for kv in groups]
        probs = [_swa_softmax(scores[kv], bias_r[kv], _swa_sink_row(sink_r, kv), valid) for kv in groups]
        ds_m, pn_m = [], []
        for kv in groups:
            pn, p_sink = probs[kv]
            delta = jnp.sum(pn * dp[kv], axis=0, keepdims=True)
            ds = pn * (dp[kv] - delta)
            dsink_r[kv] -= p_sink * delta
            dbias_r[kv] += ds
            ds_m.append(_mx(ds))
            pn_m.append(_mx(pn))
        dqg = [_dot_tn(ds_m[kv], kk[kv]) * 0.125 for kv in groups]
        dkk = [_dot(ds_m[kv], qg[kv]) for kv in groups]
        dvv = [_dot(pn_m[kv], dog[kv]) for kv in groups]
        low = _low_half(SWA_BLOCK)
        for kv in groups:
            for pair in range(SWA_GROUP // 2):
                even = dqg[kv][2 * SWA_BLOCK * pair:2 * SWA_BLOCK * pair + SWA_BLOCK]
                odd = dqg[kv][2 * SWA_BLOCK * pair + SWA_BLOCK:2 * SWA_BLOCK * (pair + 1)]
                first = HEAD_PAD * (kv * SWA_GROUP // 2 + pair)
                dq_r[rows, first:first + HEAD_PAD] = jnp.where(low, even, pltpu.roll(odd, 64, 1)).astype(dq_r.dtype)
            dk_r[span, lanes[kv]] += dkk[kv]
            dv_r[span, lanes[kv]] += dvv[kv]

    tm = SWA_STEP_BLOCKS * SWA_BLOCK
    body, extra, extra_specs = _after(body, 6, dep)
    return pl.pallas_call(
        body, name="swa_bwd", grid=(L // tm,),
        in_specs=[_row_spec(tm, qw), _vmem_spec(), _vmem_spec(), _vmem_spec(),
                  pl.BlockSpec(memory_space=pltpu.SMEM), _row_spec(tm, qw)] + extra_specs,
        out_specs=[_row_spec(tm, qw // 2), _vmem_spec(), _vmem_spec(), _vmem_spec(), _vmem_spec()],
        out_shape=[jax.ShapeDtypeStruct((L, qw // 2), MXU_DTYPE),
                   jax.ShapeDtypeStruct((L + 2 * SWA_BLOCK, kw), F32),
                   jax.ShapeDtypeStruct((L + 2 * SWA_BLOCK, kw), F32),
                   jax.ShapeDtypeStruct((SWA_KV_HEADS, SWA_SPAN, SWA_GROUP_LANES), F32),
                   jax.ShapeDtypeStruct((SWA_KV_HEADS, 1, SWA_GROUP_LANES), F32)],
        compiler_params=_params(("arbitrary",), VMEM_BIG),
    )(qs, ks, vs, bias, sink, do, *extra)


def _bias_call(rel_bias, buckets, dep=None):
    def body(t_r, bk_r, o_r):
        bk = bk_r[...]
        s = lax.broadcasted_iota(jnp.int32, bk.shape, 0)
        c = lax.broadcasted_iota(jnp.int32, bk.shape, 1)
        in_band = jnp.abs(s - SWA_BLOCK - c) <= SWA_BLOCK
        for h in range(SWA_Q_HEADS):
            acc = jnp.zeros(bk.shape, F32)
            for b in range(REL_BUCKETS):
                acc = jnp.where(bk == b, t_r[b, h], acc)
            g = h % SWA_GROUP
            o_r[h // SWA_GROUP, :, SWA_BLOCK * g:SWA_BLOCK * (g + 1)] = jnp.where(in_band, acc, -1e30)

    body, extra, extra_specs = _after(body, 2, dep)
    return pl.pallas_call(
        body, name="band_bias",
        in_specs=[pl.BlockSpec(memory_space=pltpu.SMEM), _vmem_spec()] + extra_specs, out_specs=_vmem_spec(),
        out_shape=jax.ShapeDtypeStruct((SWA_KV_HEADS, SWA_SPAN, SWA_GROUP_LANES), F32),
    )(rel_bias, buckets, *extra)


def _relbias_call(dbias, dsink, buckets, dep=None):
    def body(db_r, ds_r, bk_r, o_r, os_r):
        bk = bk_r[...]
        rowi = lax.broadcasted_iota(jnp.int32, (REL_BUCKETS, 128), 0)
        lanei = lax.broadcasted_iota(jnp.int32, (REL_BUCKETS, 128), 1)
        lane1 = lax.broadcasted_iota(jnp.int32, (1, 128), 1)
        acc = jnp.zeros((REL_BUCKETS, 128), F32)
        acc_sink = jnp.zeros((1, 128), F32)
        heads = [(h // SWA_GROUP, slice(SWA_BLOCK * (h % SWA_GROUP), SWA_BLOCK * (h % SWA_GROUP + 1)))
                 for h in range(SWA_Q_HEADS)]
        for b in range(REL_BUCKETS):
            in_bucket = bk == b
            for h, (kv, lanes) in enumerate(heads):
                s = jnp.sum(jnp.where(in_bucket, db_r[kv, :, lanes], 0.0))
                acc = acc + jnp.where((rowi == b) & (lanei == h), s, 0.0)
        for h, (kv, lanes) in enumerate(heads):
            acc_sink = acc_sink + jnp.where(lane1 == h, jnp.sum(ds_r[kv, :, lanes]), 0.0)
        o_r[...] = acc
        os_r[...] = acc_sink

    body, extra, extra_specs = _after(body, 3, dep)
    return pl.pallas_call(
        body, name="relbias_grad",
        in_specs=[_vmem_spec()] * 3 + extra_specs, out_specs=[_vmem_spec()] * 2,
        out_shape=[jax.ShapeDtypeStruct((REL_BUCKETS, 128), F32), jax.ShapeDtypeStruct((1, 128), F32)],
    )(dbias, dsink, buckets, *extra)


def _mix_call(o_f, o_b, ga, o_s, x, gn, w_out_p, g_post, g_pre2, dep=None):
    L = x.shape[0]
    tm = min(512, L)
    hw = GLA_HEADS * HEAD_PAD

    def body(of_r, ob_r, ga_r, os_r, x_r, gn_r, w_r, gp_r, g2_r, cat_r, mix_r, h1_r, n2_r):
        gn_v = gn_r[...]
        for h in range(GLA_HEADS):
            sl = slice(HEAD_PAD * h, HEAD_PAD * (h + 1))
            oh = of_r[:, sl] + ob_r[:, sl]
            on = oh * _rms_r(oh) * gn_v
            gate = ga_r[:, sl]
            cat_r[:, sl] = (on * (gate * jax.nn.sigmoid(gate))).astype(cat_r.dtype)
        os_v = os_r[...]
        cat_r[:, hw:] = os_v
        mix = _dot(cat_r[:, :hw], w_r[:hw, :]) + _dot(os_v, w_r[hw:, :])
        mix_r[...] = mix
        h1 = x_r[...] + mix * _rms_r(mix) * gp_r[...]
        h1_r[...] = h1
        n2_r[...] = (h1 * _rms_r(h1) * g2_r[...]).astype(n2_r.dtype)

    body, extra, extra_specs = _after(body, 9, dep)
    return pl.pallas_call(
        body, name="mix_fwd", grid=(L // tm,),
        in_specs=[_row_spec(tm, hw), _row_spec(tm, hw), _row_spec(tm, hw), _row_spec(tm, OUT_PAD - hw),
                  _row_spec(tm, D_MODEL), _full_spec((1, HEAD_PAD)), _vmem_spec(),
                  _full_spec((1, D_MODEL)), _full_spec((1, D_MODEL))] + extra_specs,
        out_specs=[_row_spec(tm, OUT_PAD), _row_spec(tm, D_MODEL), _row_spec(tm, D_MODEL), _row_spec(tm, D_MODEL)],
        out_shape=[jax.ShapeDtypeStruct((L, OUT_PAD), MXU_DTYPE), jax.ShapeDtypeStruct((L, D_MODEL), F32),
                   jax.ShapeDtypeStruct((L, D_MODEL), F32), jax.ShapeDtypeStruct((L, D_MODEL), MXU_DTYPE)],
        compiler_params=_params(("arbitrary",), VMEM_BIG),
    )(o_f, o_b, ga, o_s, x, gn, w_out_p, g_post, g_pre2, *extra)


def _mlp_fwd_call(n2, h1, tgt, w_ud, g_post):
    L = n2.shape[0]
    tm = min(512, L)
    blk = D_FF // N_CHIPS

    def body(n2_r, h1_r, t_r, w_r, g_r, a_r, rz_r, dh2_r, dff_r, loss_r, dg_r):
        @pl.when(pl.program_id(0) == 0)
        def _():
            loss_r[...] = jnp.zeros_like(loss_r)
            dg_r[...] = jnp.zeros_like(dg_r)

        n2v = n2_r[...]
        ff = jnp.zeros((tm, D_MODEL), F32)
        for j in range(N_CHIPS):
            sl = slice(blk * j, blk * (j + 1))
            rz = jnp.maximum(_dot(n2v, w_r[j, 0]), 0.0)
            a = _mx(rz * rz)
            rz_r[:, sl] = rz.astype(rz_r.dtype)
            a_r[:, sl] = a
            ff = ff + _dot(a, w_r[j, 1])
        g = g_r[...]
        r = _rms_r(ff)
        err = h1_r[...] + ff * r * g - t_r[...]
        loss_r[...] += 0.5 * jnp.sum(err * err) / D_MODEL
        dh2 = err * (1.0 / D_MODEL)
        dh2_r[...] = dh2
        dff, dg = _rms_bwd(ff, r, g, dh2)
        dff_r[...] = dff.astype(dff_r.dtype)
        dg_r[...] += dg

    return pl.pallas_call(
        body, name="mlp_fwd", grid=(L // tm,),
        in_specs=[_row_spec(tm, D_MODEL), _row_spec(tm, D_MODEL), _row_spec(tm, D_MODEL),
                  _vmem_spec(), _full_spec((1, D_MODEL))],
        out_specs=[_row_spec(tm, D_FF), _row_spec(tm, D_FF), _row_spec(tm, D_MODEL), _row_spec(tm, D_MODEL),
                   _full_spec((1, 128)), _full_spec((1, D_MODEL))],
        out_shape=[jax.ShapeDtypeStruct((L, D_FF), MXU_DTYPE), jax.ShapeDtypeStruct((L, D_FF), MXU_DTYPE),
                   jax.ShapeDtypeStruct((L, D_MODEL), F32), jax.ShapeDtypeStruct((L, D_MODEL), MXU_DTYPE),
                   jax.ShapeDtypeStruct((1, 128), F32), jax.ShapeDtypeStruct((1, D_MODEL), F32)],
        compiler_params=_params(("arbitrary",), VMEM_BIG),
    )(n2, h1, tgt, w_ud, g_post)


def _mlp_bwd_call(dff, rz, w_ud):
    L = dff.shape[0]
    tm = min(512, L)
    blk = D_FF // N_CHIPS

    def body(dff_r, rz_r, w_r, dz_r, dn2_r):
        dffv = dff_r[...]
        dn2 = jnp.zeros((tm, D_MODEL), F32)
        for j in range(N_CHIPS):
            sl = slice(blk * j, blk * (j + 1))
            dz = _mx(_dot_nt(dffv, w_r[j, 1]) * 2.0 * rz_r[:, sl].astype(F32))
            dz_r[:, sl] = dz
            dn2 = dn2 + _dot_nt(dz, w_r[j, 0])
        dn2_r[...] = dn2

    return pl.pallas_call(
        body, name="mlp_bwd", grid=(L // tm,),
        in_specs=[_row_spec(tm, D_MODEL), _row_spec(tm, D_FF), _vmem_spec()],
        out_specs=[_row_spec(tm, D_FF), _row_spec(tm, D_MODEL)],
        out_shape=[jax.ShapeDtypeStruct((L, D_FF), MXU_DTYPE), jax.ShapeDtypeStruct((L, D_MODEL), F32)],
        compiler_params=_params(("arbitrary",), VMEM_BIG),
    )(dff, rz, w_ud)


def _mlp_wgrad_call(a, dff, n2, dz):
    L = a.shape[0]
    tf = 512
    per = (D_FF // N_CHIPS) // tf

    def body(a_r, dff_r, n2_r, dz_r, dwd_r, dwu_r):
        dwd_r[...] = _dot_tn(a_r[...], dff_r[...])
        dwu_r[...] = _dot_tn(n2_r[...], dz_r[...])

    return pl.pallas_call(
        body, name="mlp_wgrad", grid=(D_FF // tf,),
        in_specs=[pl.BlockSpec((L, tf), lambda j: (0, j)), _vmem_spec(), _vmem_spec(),
                  pl.BlockSpec((L, tf), lambda j: (0, j))],
        out_specs=[pl.BlockSpec((tf, D_MODEL), lambda j: (j, 0)),
                   pl.BlockSpec((None, D_MODEL, tf), lambda j: (j // per, 0, j % per))],
        out_shape=[jax.ShapeDtypeStruct((D_FF, D_MODEL), F32),
                   jax.ShapeDtypeStruct((N_CHIPS, D_MODEL, D_FF // N_CHIPS), F32)],
        compiler_params=_params(("arbitrary",), VMEM_BIG),
    )(a, dff, n2, dz)


def _mix_bwd_call(dn2, dh2, h1, mix, cat, o_f, o_b, ga, gn, g_post, g_pre2, w_out_p):
    L = dn2.shape[0]
    tm = min(512, L)
    hw = GLA_HEADS * HEAD_PAD

    def body(dn2_r, dh2_r, h1_r, mix_r, cat_r, of_r, ob_r, ga_r, gn_r, gp_r, g2_r, w_r,
             dh1_r, do_r, dga_r, dos_r, dw_r, dg2_r, dgp_r, dgn_r):
        @pl.when(pl.program_id(0) == 0)
        def _():
            for ref in (dw_r, dg2_r, dgp_r, dgn_r):
                ref[...] = jnp.zeros_like(ref)

        parts = [slice(start, start + min(256, tm)) for start in range(0, tm, 256)]
        dmix_m = []
        for rs in parts:
            h1 = h1_r[rs, :]
            dx2, dg2 = _rms_bwd(h1, _rms_r(h1), g2_r[...], dn2_r[rs, :])
            dh1 = dh2_r[rs, :] + dx2
            dh1_r[rs, :] = dh1
            dg2_r[...] += dg2
            mix = mix_r[rs, :]
            dmix, dgp = _rms_bwd(mix, _rms_r(mix), gp_r[...], dh1)
            dgp_r[...] += dgp
            dmix_m.append(_mx(dmix))
        dcat = [_dot_nt(d, w_r[...]) for d in dmix_m]
        for rs, d in zip(parts, dmix_m):
            dw_r[...] += _dot_tn(cat_r[rs, :], d)
        gn_v = gn_r[...]
        dgn = jnp.zeros((1, HEAD_PAD), F32)
        for rs, dc in zip(parts, dcat):
            dos_r[rs, :] = _spread_heads(dc[:, hw:]).astype(dos_r.dtype)
            for h in range(GLA_HEADS):
                sl = slice(HEAD_PAD * h, HEAD_PAD * (h + 1))
                oh = of_r[rs, sl] + ob_r[rs, sl]
                rr = _rms_r(oh)
                xh = oh * rr
                gate = ga_r[rs, sl]
                sg = jax.nn.sigmoid(gate)
                silu = gate * sg
                doa = dc[:, sl]
                dga_r[rs, sl] = (doa * (xh * gn_v) * (sg + silu * (1.0 - sg))).astype(dga_r.dtype)
                don = doa * silu
                gd = don * gn_v
                do_r[rs, sl] = rr * (gd - xh * jnp.mean(gd * xh, axis=-1, keepdims=True))
                dgn = dgn + jnp.sum(don * xh, axis=0, keepdims=True)
        dgn_r[...] += dgn

    return pl.pallas_call(
        body, name="mix_bwd", grid=(L // tm,),
        in_specs=[_row_spec(tm, D_MODEL)] * 4 + [_row_spec(tm, OUT_PAD)] + [_row_spec(tm, hw)] * 3
        + [_full_spec((1, HEAD_PAD)), _full_spec((1, D_MODEL)), _full_spec((1, D_MODEL)), _vmem_spec()],
        out_specs=[_row_spec(tm, D_MODEL), _row_spec(tm, hw), _row_spec(tm, hw),
                   _row_spec(tm, SWA_Q_HEADS * HEAD_PAD),
                   _full_spec((OUT_PAD, D_MODEL)), _full_spec((1, D_MODEL)), _full_spec((1, D_MODEL)),
                   _full_spec((1, HEAD_PAD))],
        out_shape=[jax.ShapeDtypeStruct((L, D_MODEL), F32), jax.ShapeDtypeStruct((L, hw), F32),
                   jax.ShapeDtypeStruct((L, hw), MXU_DTYPE),
                   jax.ShapeDtypeStruct((L, SWA_Q_HEADS * HEAD_PAD), MXU_DTYPE),
                   jax.ShapeDtypeStruct((OUT_PAD, D_MODEL), F32), jax.ShapeDtypeStruct((1, D_MODEL), F32),
                   jax.ShapeDtypeStruct((1, D_MODEL), F32), jax.ShapeDtypeStruct((1, HEAD_PAD), F32)],
        compiler_params=_params(("arbitrary",), VMEM_BIG),
    )(dn2, dh2, h1, mix, cat, o_f, o_b, ga, gn, g_post, g_pre2, w_out_p)


def _in_bwd_call(x, dh1, g_pre, w_in_t, pairs, singles, halos, dep=None):
    L = x.shape[0]
    tm = min(512, L)
    per = tm // SWA_BLOCK
    n_pair, n_single, n_halo = len(pairs), len(singles), len(halos)
    groups = [c for c, _ in pairs] + [c for c, _ in singles] + [c for c, _ in halos]

    def body(*refs):
        x_r, dh1_r, g_r, w_r = refs[:4]
        pair_refs = refs[4:4 + 2 * n_pair]
        single_refs = refs[4 + 2 * n_pair:4 + 2 * n_pair + n_single]
        halo_refs = refs[4 + 2 * n_pair + n_single:4 + 2 * n_pair + n_single + per * n_halo]
        dx_r, dw_r, dg_r = refs[4 + 2 * n_pair + n_single + per * n_halo:]

        @pl.when(pl.program_id(0) == 0)
        def _():
            dw_r[...] = jnp.zeros_like(dw_r)
            dg_r[...] = jnp.zeros_like(dg_r)

        xv = x_r[...]
        r = _rms_r(xv)
        g = g_r[...]
        u = _mx(xv * r * g)
        vals = [pair_refs[2 * i][...].astype(F32) + pair_refs[2 * i + 1][...].astype(F32) for i in range(n_pair)]
        vals += [ref[...].astype(F32) for ref in single_refs]
        vals += [jnp.concatenate([ref[...] for ref in halo_refs[per * i:per * (i + 1)]], axis=0)
                 for i in range(n_halo)]
        ds = [_mx(_squeeze_heads(val) if heads else val) for (_, _, heads), val in zip(groups, vals)]
        du = jnp.zeros((tm, D_MODEL), F32)
        for (first, rows, _), d in zip(groups, ds):
            du = du + _dot(d, w_r[first:first + rows, :])
        for (first, rows, _), d in zip(groups, ds):
            dw_r[first:first + rows, :] += _dot_tn(d, u)
        dx, dg = _rms_bwd(xv, r, g, du)
        dx_r[...] = dh1_r[...] + dx
        dg_r[...] += dg

    arrays = [a for _, pr in pairs for a in pr] + [a for _, a in singles]
    specs = [_row_spec(tm, a.shape[1]) for a in arrays]
    for _, a in halos:
        specs += [pl.BlockSpec((SWA_BLOCK, a.shape[1]), lambda i, j=j: (per * i + 1 + j, 0)) for j in range(per)]
        arrays += [a] * per
    body, extra, extra_specs = _after(body, 4 + len(arrays), dep)
    return pl.pallas_call(
        body, name="in_bwd", grid=(L // tm,),
        in_specs=[_row_spec(tm, D_MODEL), _row_spec(tm, D_MODEL), _full_spec((1, D_MODEL)), _vmem_spec()] + specs
        + extra_specs,
        out_specs=[_row_spec(tm, D_MODEL), _full_spec((IN_COLS, D_MODEL)), _full_spec((1, D_MODEL))],
        out_shape=[jax.ShapeDtypeStruct((L, D_MODEL), F32), jax.ShapeDtypeStruct((IN_COLS, D_MODEL), F32),
                   jax.ShapeDtypeStruct((1, D_MODEL), F32)],
        compiler_params=_params(("arbitrary",), VMEM_BIG),
    )(x, dh1, g_pre, w_in_t, *arrays, *extra)


def _adamw_math(w, g, m, v):
    m = ADAM_B1 * m + (1.0 - ADAM_B1) * g
    v = ADAM_B2 * v + (1.0 - ADAM_B2) * (g * g)
    m_hat = m / (1.0 - ADAM_B1 ** ADAM_STEP)
    v_hat = v / (1.0 - ADAM_B2 ** ADAM_STEP)
    delta = -ADAM_LR * (m_hat / (jnp.sqrt(v_hat) + ADAM_EPS) + ADAM_WD * w)
    return delta, m, v


def _adamw_call(w, g, m, v, name, dep=None):
    rows, cols = w.shape
    tr = min(256, rows)

    def body(w_r, g_r, m_r, v_r, g_out_r, d_r, nm_r, nv_r):
        g = g_r[...]
        g_out_r[...] = g
        d_r[...], nm_r[...], nv_r[...] = _adamw_math(w_r[...], g, m_r[...], v_r[...])

    if rows % tr == 0:
        spec, steps = _row_spec(tr, cols), rows // tr
    else:
        spec, steps = pl.BlockSpec((rows, 256), lambda i: (0, i)), cols // 256
    body, extra, extra_specs = _after(body, 4, dep)
    return pl.pallas_call(
        body, name=name, grid=(steps,),
        in_specs=[spec] * 4 + extra_specs, out_specs=[spec] * 4,
        out_shape=[jax.ShapeDtypeStruct(w.shape, F32)] * 4,
        compiler_params=_params(("arbitrary",)),
    )(w, g, m, v, *extra)


def _position():
    return lax.axis_index("x"), lax.axis_index("y"), lax.axis_index("c")


def _other_chips(x, y):
    return [(1 - x, y), (x, 1 - y), (1 - x, 1 - y)]


ROWS, COLS = -2, -1


def _half(ref, which, axis):
    size = ref.shape[axis] // 2
    span = pl.ds(pl.multiple_of(which * size, 16 if axis == ROWS else 128), size)
    index = [slice(None)] * len(ref.shape)
    index[axis] = span
    return ref.at[tuple(index)]


def _quarter(ref, half, which, axis):
    size = ref.shape[axis] // 4
    span = pl.ds(pl.multiple_of((2 * half + which) * size, 16 if axis == ROWS else 128), size)
    index = [slice(None)] * len(ref.shape)
    index[axis] = span
    return ref.at[tuple(index)]


def _first_gather_call(shards, axes, routed):
    n = len(shards)
    per = 7

    def body(*refs):
        srcs, outs = refs[:n], refs[n:2 * n]
        send_sems, recv_sems, local_sems = refs[2 * n:]
        x, y, c = _position()
        me, sibling = (x, y, c), (x, y, 1 - c)
        x_side, y_side, across = _other_chips(x, y)
        local = [pltpu.make_async_copy(srcs[a], outs[a].at[2 * x + y], local_sems.at[a]) for a in range(n)]
        for cp in local:
            cp.start()

        def copy(a, k, dst, to, src=None):
            return pltpu.make_async_remote_copy(
                src_ref=dst if src is None else src, dst_ref=dst, send_sem=send_sems.at[per * a + k],
                recv_sem=recv_sems.at[per * a + k], device_id=to, device_id_type=MESH_ID)

        def half(a, chip, pc):
            return _half(outs[a].at[2 * chip[0] + chip[1]], pc, axes[a])

        def quarter(a, chip, q):
            return _quarter(outs[a].at[2 * chip[0] + chip[1]], c, q, axes[a])

        sends = []
        for a in range(n):
            mine = _half(srcs[a], c, axes[a])
            targets = (x_side, y_side) if routed[a] else (x_side, y_side, across)
            sends += [copy(a, j, half(a, (x, y), c), (*chip, c), src=mine) for j, chip in enumerate(targets)]
        for cp in sends:
            cp.start()
        for a in range(n):
            for j, chip in enumerate((x_side, y_side)):
                copy(a, j, half(a, chip, c), me).wait_recv()
                if routed[a]:
                    other = (y_side, x_side)[j]
                    sends.append(copy(a, 2 + j, quarter(a, chip, j), (*other, c)))
                    sends[-1].start()
                sends.append(copy(a, 4 + j, half(a, chip, c), sibling))
                sends[-1].start()
        for a in range(n):
            if routed[a]:
                for j in range(2):
                    copy(a, 2 + j, quarter(a, across, j), me).wait_recv()
            else:
                copy(a, 2, half(a, across, c), me).wait_recv()
            sends.append(copy(a, 6, half(a, across, c), sibling))
            sends[-1].start()
        for a in range(n):
            for k, chip in ((4, x_side), (5, y_side), (6, across)):
                copy(a, k, half(a, chip, 1 - c), me).wait_recv()
        for cp in sends:
            cp.wait_send()
        for cp in local:
            cp.wait()

    return pl.pallas_call(
        body, name="first_gather",
        in_specs=[_any_spec()] * n, out_specs=[_any_spec()] * n,
        out_shape=[jax.ShapeDtypeStruct((N_CHIPS,) + s.shape, s.dtype) for s in shards],
        scratch_shapes=[pltpu.SemaphoreType.DMA((per * n,)), pltpu.SemaphoreType.DMA((per * n,)),
                        pltpu.SemaphoreType.DMA((n,))],
    )(*shards)


PAIR_PEERS, CHIP_PEERS = 1, 2


def _peers(which):
    x, y, c = _position()
    if which == PAIR_PEERS:
        return [(x, y, 1 - c)]
    return [(px, py, c) for px, py in _other_chips(x, y)]


def _split_start(name, arrays, n_copies, plan, peers=None):
    n = len(arrays)

    def body(*refs):
        ins, send_sems, recv_sems, token = refs[:n], refs[n], refs[n + 1], refs[-1]
        if peers is not None:
            barrier = pltpu.get_barrier_semaphore()
            targets = _peers(peers)
            for target in targets:
                pl.semaphore_signal(barrier, inc=1, device_id=target, device_id_type=MESH_ID)
            pl.semaphore_wait(barrier, len(targets))
        for k, (src, dst, to, _) in enumerate(plan(ins)):
            pltpu.make_async_remote_copy(src_ref=src, dst_ref=dst, send_sem=send_sems.at[k],
                                         recv_sem=recv_sems.at[k], device_id=to, device_id_type=MESH_ID).start()
        token[...] = jnp.zeros_like(token)

    hbm = pl.BlockSpec(memory_space=pltpu.HBM)
    sem = pl.BlockSpec(memory_space=pltpu.SEMAPHORE)
    out = pl.pallas_call(
        body, name=name,
        out_shape=(pltpu.SemaphoreType.DMA((n_copies,)), pltpu.SemaphoreType.DMA((n_copies,)))
        + tuple(pltpu.HBM(a.shape, a.dtype) for a in arrays) + (jax.ShapeDtypeStruct((8, 128), F32),),
        in_specs=[hbm] * n, out_specs=(sem, sem) + (hbm,) * n + (_vmem_spec(),),
        input_output_aliases={i: 2 + i for i in range(n)},
        compiler_params=pltpu.CompilerParams(has_side_effects=pltpu.SideEffectType.DATAFLOW_SIDE_EFFECTING,
                                             collective_id=peers),
    )(*[pltpu.with_memory_space_constraint(a, pltpu.HBM) for a in arrays])
    return (out[0], out[1], tuple(out[2:2 + n])), out[-1]


def _split_wait(name, handle, n_copies, plan, after):
    send_sems, recv_sems, arrays = handle
    n = len(arrays)

    def body(*refs):
        ins, s_sems, r_sems = refs[:n], refs[n], refs[n + 1]
        for k, (src, dst, to, landed) in enumerate(plan(ins)):
            cp = pltpu.make_async_remote_copy(src_ref=src, dst_ref=landed, send_sem=s_sems.at[k],
                                              recv_sem=r_sems.at[k], device_id=to, device_id_type=MESH_ID)
            cp.wait_send()
            cp.wait_recv()

    hbm = pl.BlockSpec(memory_space=pltpu.HBM)
    sem = pl.BlockSpec(memory_space=pltpu.SEMAPHORE)
    out = pl.pallas_call(
        body, name=name,
        out_shape=tuple(pltpu.HBM(a.shape, a.dtype) for a in arrays),
        in_specs=[hbm] * n + [sem, sem, _any_spec()], out_specs=(hbm,) * n,
        input_output_aliases={i: i for i in range(n)},
        compiler_params=pltpu.CompilerParams(has_side_effects=pltpu.SideEffectType.DATAFLOW_SIDE_EFFECTING),
    )(*arrays, send_sems, recv_sems, after)
    return tuple(out)


def _gather_plans(axes):
    n = len(axes)

    def stage_one(refs):
        x, y, c = _position()
        copies = []
        for a, axis in enumerate(axes):
            for px, py in _other_chips(x, y):
                copies.append((_half(refs[a], c, axis), _half(refs[n + a].at[2 * x + y], c, axis),
                               (px, py, c), _half(refs[n + a].at[2 * px + py], c, axis)))
        return copies

    def stage_two(refs):
        x, y, c = _position()
        copies = []
        for a, axis in enumerate(axes):
            for px, py in _other_chips(x, y):
                piece = _half(refs[n + a].at[2 * px + py], c, axis)
                copies.append((piece, piece, (x, y, 1 - c), _half(refs[n + a].at[2 * px + py], 1 - c, axis)))
        return copies

    return stage_one, stage_two


def _pair_swap_plan(axes):
    n = len(axes)

    def plan(refs):
        x, y, c = _position()
        return [(_half(refs[a], 1 - c, axes[a]), refs[n + a], (x, y, 1 - c), refs[n + a]) for a in range(n)]

    return plan


def _chip_swap_plan(n):
    def plan(refs):
        x, y, c = _position()
        copies = []
        for a in range(n):
            for j, (px, py) in enumerate(_other_chips(x, y)):
                copies.append((refs[a].at[2 * px + py], refs[n + a].at[j], (px, py, c), refs[n + a].at[j]))
        return copies

    return plan


def _pair_join_plan(axes):
    def plan(refs):
        x, y, c = _position()
        copies = []
        for a, axis in enumerate(axes):
            mine = _half(refs[a], c, axis)
            copies.append((mine, mine, (x, y, 1 - c), _half(refs[a], 1 - c, axis)))
        return copies

    return plan


def _pair_add_call(gs, gots, pos, name, axes):
    n = len(gs)

    def body(pos_r, *refs):
        for g_r, got_r, o_r in zip(refs[:n], refs[n:2 * n], refs[2 * n:]):
            o_r[...] = (g_r[...] + got_r[...]).astype(o_r.dtype)

    def mine(axis):
        return (lambda j, p: (j, p[1], 0)) if axis == ROWS else (lambda j, p: (j, 0, p[1]))

    blocks = [(None,) + got.shape[1:] for got in gots]
    return pl.pallas_call(
        body, name=name,
        grid_spec=pltpu.PrefetchScalarGridSpec(
            num_scalar_prefetch=1, grid=(N_CHIPS,),
            in_specs=[pl.BlockSpec(blk, mine(axis)) for blk, axis in zip(blocks, axes)]
            + [pl.BlockSpec(blk, lambda j, p: (j, 0, 0)) for blk in blocks],
            out_specs=[pl.BlockSpec(blk, lambda j, p: (j, 0, 0)) for blk in blocks]),
        out_shape=[jax.ShapeDtypeStruct(got.shape, COMM_DTYPE) for got in gots],
        compiler_params=_params(("arbitrary",), VMEM_BIG),
    )(pos, *gs, *gots)


def _chip_add_call(hsums, gots, pos, name, axes):
    n = len(hsums)
    steps = 2

    def body(pos_r, *refs):
        for own_r, got_r, o_r in zip(refs[:n], refs[n:2 * n], refs[2 * n:]):
            acc = own_r[...].astype(F32)
            for j in range(3):
                acc = acc + got_r[j].astype(F32)
            o_r[...] = acc

    in_specs, got_specs, out_specs, out_shape = [], [], [], []
    for h, axis in zip(hsums, axes):
        if axis == ROWS:
            rows, cols = h.shape[1] // steps, h.shape[2]
            in_specs.append(pl.BlockSpec((None, rows, cols), lambda i, p: (p[0], i, 0)))
            got_specs.append(pl.BlockSpec((3, rows, cols), lambda i, p: (0, i, 0)))
            out_specs.append(pl.BlockSpec((rows, cols), lambda i, p: (p[1] * steps + i, 0)))
            out_shape.append(jax.ShapeDtypeStruct((2 * h.shape[1], cols), F32))
        else:
            rows, cols = h.shape[1], h.shape[2] // steps
            in_specs.append(pl.BlockSpec((None, rows, cols), lambda i, p: (p[0], 0, i)))
            got_specs.append(pl.BlockSpec((3, rows, cols), lambda i, p: (0, 0, i)))
            out_specs.append(pl.BlockSpec((rows, cols), lambda i, p: (0, p[1] * steps + i)))
            out_shape.append(jax.ShapeDtypeStruct((rows, 2 * h.shape[2]), F32))
    return pl.pallas_call(
        body, name=name,
        grid_spec=pltpu.PrefetchScalarGridSpec(
            num_scalar_prefetch=1, grid=(steps,), in_specs=in_specs + got_specs, out_specs=out_specs),
        out_shape=out_shape,
        compiler_params=_params(("arbitrary",), VMEM_BIG),
    )(pos, *hsums, *gots)


SMALL_NAMES = ("norm_mix_pre", "norm_mix_post", "norm_mlp_pre", "norm_mlp_post", "b_gate_fwd", "b_gate_bwd",
               "gla_norm", "swa_sink", "rel_bias")


N_DEVICES = 8


def _small_pack_call(grads, extras):
    operands = list(grads) + list(extras)

    def body(*refs):
        g_refs, (all_a, all_b) = refs[:len(operands)], refs[len(operands):]
        x, y, c = _position()
        me = 4 * x + 2 * y + c
        all_a[me] = jnp.zeros(all_a.shape[1:], F32)
        all_b[me] = jnp.zeros(all_b.shape[1:], F32)
        for i in range(4):
            all_a[me, i:i + 1, :] = g_refs[i][...]
        all_a[me, 4:5, 0:256] = g_refs[4][...]
        all_a[me, 5:6, 0:256] = g_refs[5][...]
        all_a[me, 6:7, 0:128] = g_refs[6][...]
        all_a[me, 7:8, 0:128] = g_refs[7][...]
        all_a[me, 7:8, 128:256] = g_refs[11][...]
        all_b[me, 0:32, 0:128] = g_refs[8][...]
        all_b[me, 32:48, :] = g_refs[9][...]
        all_b[me, 48:64, :] = g_refs[10][...]

    out_shape = [jax.ShapeDtypeStruct((N_DEVICES, 8, D_MODEL), F32), jax.ShapeDtypeStruct((N_DEVICES, 64, 256), F32)]
    return pl.pallas_call(
        body, name="small_pack",
        in_specs=[_whole_spec(a.shape) for a in operands], out_specs=[_whole_spec(s.shape) for s in out_shape],
        out_shape=out_shape,
    )(*operands)


def _everyone_plan(n):
    def plan(refs):
        x, y, c = _position()
        copies = []
        for k in range(1, N_DEVICES):
            px = 1 - x if (k >> 2) & 1 else x
            py = 1 - y if (k >> 1) & 1 else y
            pc = 1 - c if k & 1 else c
            for a in range(n):
                mine = refs[a].at[4 * x + 2 * y + c]
                copies.append((mine, mine, (px, py, pc), refs[a].at[4 * px + 2 * py + pc]))
        return copies

    return plan


def _small_adamw_call(all_a, all_b, params):
    n_small = len(SMALL_NAMES)
    wmv = [t for p in params for t in p]
    shapes = [p[0].shape for p in params]

    def body(*refs):
        all_a, all_b = refs[:2]
        wmv_refs = refs[2:2 + 3 * n_small]
        out_refs = refs[2 + 3 * n_small:]
        sum_a, sum_b = all_a[0], all_b[0]
        for d in range(1, N_DEVICES):
            sum_a = sum_a + all_a[d]
            sum_b = sum_b + all_b[d]
        gsum = [sum_a[0:1], sum_a[1:2], sum_a[2:3], sum_a[3:4], sum_a[4:5, 0:256], sum_a[5:6, 0:256],
                sum_a[6:7, 0:128], sum_a[7:8, 0:SWA_Q_HEADS], sum_b[0:32, 0:SWA_Q_HEADS]]
        for i in range(n_small):
            w_r, m_r, v_r = wmv_refs[3 * i:3 * i + 3]
            delta, new_m, new_v = _adamw_math(w_r[...], gsum[i], m_r[...], v_r[...])
            out_refs[4 * i][...] = gsum[i]
            out_refs[4 * i + 1][...] = delta
            out_refs[4 * i + 2][...] = new_m
            out_refs[4 * i + 3][...] = new_v
        out_refs[4 * n_small][...] = sum_b[32:48]
        out_refs[4 * n_small + 1][...] = sum_b[48:64]
        out_refs[4 * n_small + 2][...] = sum_a[7:8, 128:256]

    out_shape = [jax.ShapeDtypeStruct(s, F32) for s in shapes for _ in range(4)]
    out_shape += [jax.ShapeDtypeStruct((GLA_GATE_RANK, 256), F32)] * 2 + [jax.ShapeDtypeStruct((1, 128), F32)]
    out = pl.pallas_call(
        body, name="small_adamw",
        in_specs=[_whole_spec(a.shape) for a in [all_a, all_b] + wmv],
        out_specs=[_whole_spec(s.shape) for s in out_shape],
        out_shape=out_shape,
    )(all_a, all_b, *wmv)
    per_name = [tuple(out[4 * i:4 * i + 4]) for i in range(n_small)]
    return per_name, out[4 * n_small], out[4 * n_small + 1], out[4 * n_small + 2]


def _pad_gate(w, first_row):
    return jnp.pad(w, ((first_row, 128 - GLA_GATE_RANK - first_row), (0, 0)))


def _own_slot(shard, chip):
    zone = lax.empty((N_CHIPS,) + shard.shape, shard.dtype)
    return lax.dynamic_update_slice(zone, shard[None], (chip,) + (0,) * shard.ndim)


def _reduce_to_owners(grads, axes, pos, tag, overlap):
    n = len(grads)

    def half_shape(g, axis):
        return (N_CHIPS, g.shape[1] // 2, g.shape[2]) if axis == ROWS else (N_CHIPS, g.shape[1], g.shape[2] // 2)

    lands = [lax.empty(half_shape(g, axis), F32) for g, axis in zip(grads, axes)]
    handle, token = _split_start(tag + "_pair_start", list(grads) + lands, n, _pair_swap_plan(axes), PAIR_PEERS)
    got = _split_wait(tag + "_pair_wait", handle, n, _pair_swap_plan(axes), overlap[0](token))
    sums = list(_pair_add_call(got[:n], got[n:], pos, tag + "_pair_add", axes))
    lands = [lax.empty((3,) + s.shape[1:], s.dtype) for s in sums]
    handle, token = _split_start(tag + "_chip_start", sums + lands, 3 * n, _chip_swap_plan(n), CHIP_PEERS)
    got = _split_wait(tag + "_chip_wait", handle, 3 * n, _chip_swap_plan(n), overlap[1](token))
    halves = list(_chip_add_call(got[:n], got[n:], pos, tag + "_chip_add", axes))
    handle, token = _split_start(tag + "_join_start", halves, n, _pair_join_plan(axes), PAIR_PEERS)
    return _split_wait(tag + "_join_wait", handle, n, _pair_join_plan(axes), overlap[2](token))


def kernel(x, norm_mix_pre, w_in, w_gate_up_fwd, b_gate_fwd, w_gate_up_bwd, b_gate_bwd, gla_norm, swa_sink, rel_bias, w_out, norm_mix_post, norm_mlp_pre, w_up, w_down, norm_mlp_post, loss_target, m_norm_mix_pre, m_w_in, m_w_gate_up_fwd, m_b_gate_fwd, m_w_gate_up_bwd, m_b_gate_bwd, m_gla_norm, m_swa_sink, m_rel_bias, m_w_out, m_norm_mix_post, m_norm_mlp_pre, m_w_up, m_w_down, m_norm_mlp_post, v_norm_mix_pre, v_w_in, v_w_gate_up_fwd, v_b_gate_fwd, v_w_gate_up_bwd, v_b_gate_bwd, v_gla_norm, v_swa_sink, v_rel_bias, v_w_out, v_norm_mix_post, v_norm_mlp_pre, v_w_up, v_w_down, v_norm_mlp_post):
    given = dict(locals())
    cx, cy, cc = _position()
    chip = (2 * cx + cy).astype(jnp.int32)
    pos = jnp.stack([chip, cc.astype(jnp.int32)])
    seq, tgt = x[0], loss_target[0]

    gates = jnp.concatenate([w_gate_up_fwd[0], w_gate_up_bwd[0]], axis=0).astype(COMM_DTYPE)
    all_in, all_gates = _first_gather_call([w_in[0].T.astype(COMM_DTYPE), gates], [COLS, ROWS], [True, False])
    rest = [w_out[0].astype(COMM_DTYPE), jnp.stack([w_up[0], w_down[0]]).astype(COMM_DTYPE)]
    stage_one, stage_two = _gather_plans([ROWS, ROWS])
    handle, token = _split_start("gather_chip_start", rest + [_own_slot(s, chip) for s in rest] + [all_gates], 6,
                                 stage_one, CHIP_PEERS)

    w_in_t = _mx(all_in.reshape(IN_COLS, D_MODEL))
    gates_full = jnp.concatenate([all_gates[j] for j in range(N_CHIPS)], axis=1)
    wgf_p = _mx(_pad_gate(gates_full[:GLA_GATE_RANK], 0))
    wgb_p = _mx(_pad_gate(gates_full[GLA_GATE_RANK:], GLA_GATE_RANK))
    bf_p, bb_p = b_gate_fwd, b_gate_bwd
    buckets = jnp.asarray(_band_buckets())
    sink1 = swa_sink.reshape(SWA_Q_HEADS)

    qa, ka, va, ga, qs, ks, vs, za = _proj_call(seq, norm_mix_pre, w_in_t, dep=token)
    halo = ((SWA_BLOCK, SWA_BLOCK), (0, 0))
    ks_p, vs_p = jnp.pad(ks, halo), jnp.pad(vs, halo)
    o_f, o_b, s_f, s_b = _gla_fwd_call(qa, ka, va, za, wgf_p, bf_p, wgb_p, bb_p)
    bias = _bias_call(rel_bias, buckets, dep=o_f)
    arrays = _split_wait("gather_chip_wait", handle, 6, stage_one, bias)
    handle, token = _split_start("gather_pair_start", list(arrays), 6, stage_two, PAIR_PEERS)
    o_s = _swa_fwd_call(qs, ks_p, vs_p, bias, sink1, dep=token)
    arrays = _split_wait("gather_pair_wait", handle, 6, stage_two, o_s)
    w_out_full = _mx(arrays[2].reshape(N_CHIPS * R_OUT, D_MODEL))
    w_ud = _mx(arrays[3])
    cat, mix, h1, n2 = _mix_call(o_f, o_b, ga, o_s, seq, gla_norm, w_out_full, norm_mix_post, norm_mlp_pre)
    a, rz, dh2, dff, loss, d_post2 = _mlp_fwd_call(n2, h1, tgt, w_ud, norm_mlp_post)

    dz, dn2 = _mlp_bwd_call(dff, rz, w_ud)
    dw_down, dw_up4 = _mlp_wgrad_call(a, dff, n2, dz)
    dh1, do, dga, dos, dw_out, d_pre2, d_post, d_gn = _mix_bwd_call(
        dn2, dh2, h1, mix, cat, o_f, o_b, ga, gla_norm, norm_mix_post, norm_mlp_pre, w_out_full)
    done = {}

    def swa_backward(tok):
        done["swa"] = _swa_bwd_call(qs, ks_p, vs_p, bias, sink1, dos, dep=tok)
        return done["swa"][0]

    def gla_in_backward(tok):
        done["gla"] = _gla_bwd_call(qa, ka, va, za, do, s_f, s_b, wgf_p, bf_p, wgb_p, bb_p, dep=tok)
        dqf, dkf, dvf, dzf, _, _, dqb, dkb, dvb, dzb, _, _ = done["gla"]
        dqs, dks_p, dvs_p, _, _ = done["swa"]
        done["in"] = _in_bwd_call(
            seq, dh1, norm_mix_pre, w_in_t,
            pairs=[(_side_by_side(T_QA), (dqf, dqb)), (_side_by_side(T_KA), (dkf, dkb)), (T_VA, (dvf, dvb)),
                   (T_ZA, (dzf, dzb))],
            singles=[(T_GA, dga), (_side_by_side(T_QS), dqs)], halos=[(T_KS, dks_p), (T_VS, dvs_p)])
        return done["in"][0]

    def bias_backward(tok):
        done["rel"] = _relbias_call(done["swa"][3], done["swa"][4], buckets, dep=tok)
        return done["rel"][0]

    g_up, g_down, g_out = _reduce_to_owners(
        [dw_up4, dw_down.reshape(N_CHIPS, R_DOWN, D_MODEL), dw_out.reshape(N_CHIPS, R_OUT, D_MODEL)],
        [ROWS, ROWS, ROWS], pos, "mlp", [swa_backward, gla_in_backward, bias_backward])
    dx, dw_in_t, d_pre = done["in"]
    dwf, dbf, dwb, dbb = done["gla"][4], done["gla"][5], done["gla"][10], done["gla"][11]
    drel, dsink = done["rel"]

    small_grads = [d_pre, d_post, d_pre2, d_post2, dbf, dbb, d_gn, dsink, drel]
    gate_grads = [dwf[:GLA_GATE_RANK], dwb[GLA_GATE_RANK:2 * GLA_GATE_RANK]]
    small_params = [(given[n], given["m_" + n], given["v_" + n]) for n in SMALL_NAMES]
    upd = {}

    everyone = _everyone_plan(2)
    small_handle, small_token = _split_start(
        "small_start", list(_small_pack_call(small_grads, gate_grads + [loss])), 2 * (N_DEVICES - 1), everyone)

    def update_out(tok):
        upd["w_out"] = tuple(_adamw_call(w_out[0], g_out, m_w_out[0], v_w_out[0], "adamw_w_out",
                                         dep=tok + small_token))
        return upd["w_out"][1]

    def update_mlp(tok):
        upd["w_up"] = tuple(_adamw_call(w_up[0], g_up, m_w_up[0], v_w_up[0], "adamw_w_up", dep=tok))
        upd["w_down"] = tuple(
            _adamw_call(w_down[0], g_down, m_w_down[0], v_w_down[0], "adamw_w_down", dep=upd["w_up"][1]))
        all_a, all_b = _split_wait("small_wait", small_handle, 2 * (N_DEVICES - 1), everyone, upd["w_down"][1])
        per_name, done["gf_sum"], done["gb_sum"], upd["loss"] = _small_adamw_call(all_a, all_b, small_params)
        upd.update(dict(zip(SMALL_NAMES, per_name)))
        return per_name[0][1]

    def update_gates(tok):
        for name, total in (("w_gate_up_fwd", done["gf_sum"]), ("w_gate_up_bwd", done["gb_sum"])):
            g = lax.dynamic_slice(total, (0, chip * 64), (GLA_GATE_RANK, 64))
            upd[name] = tuple(_adamw_call(given[name][0], g, given["m_" + name][0], given["v_" + name][0],
                                          "adamw_" + name, dep=tok))
        return upd["w_gate_up_bwd"][1]

    (g_in_t,) = _reduce_to_owners([dw_in_t.reshape(N_CHIPS, R_IN, D_MODEL)], [COLS], pos, "in",
                                  [update_out, update_mlp, update_gates])
    upd["w_in"] = tuple(t.T for t in _adamw_call(w_in[0].T, g_in_t, m_w_in[0].T, v_w_in[0].T, "adamw_w_in"))

    big = ("w_in", "w_gate_up_fwd", "w_gate_up_bwd", "w_out", "w_up", "w_down")
    names = ["norm_mix_pre", "w_in", "w_gate_up_fwd", "b_gate_fwd", "w_gate_up_bwd", "b_gate_bwd", "gla_norm",
             "swa_sink", "rel_bias", "w_out", "norm_mix_post", "norm_mlp_pre", "w_up", "w_down", "norm_mlp_post"]
    outs = [upd["loss"][0, 0], dx[None]]
    for kind in range(4):
        outs += [upd[n][kind][None] if n in big else upd[n][kind] for n in names]
    return tuple(outs)
```

```python
import math

import numpy as np
import jax
import jax.numpy as jnp
from jax import lax
from jax.experimental import pallas as pl
from jax.experimental.pallas import tpu as pltpu

F32 = jnp.float32
MXU_DTYPE = jnp.bfloat16
COMM_DTYPE = jnp.bfloat16

D_MODEL = 1024
D_FF = 4096
N_CHIPS = 4
GLA_HEADS = 4
GLA_CHUNK = 64
GLA_GATE_RANK = 16
GLA_GATE_NORM = 16.0
SWA_Q_HEADS = 8
SWA_KV_HEADS = 2
SWA_BLOCK = 128
REL_BUCKETS = 32
REL_MAX_DIST = 128
NORM_EPS = 1e-6
HEAD_PAD = 128

ADAM_LR = 0.001
ADAM_B1 = 0.9
ADAM_B2 = 0.999
ADAM_EPS = 1e-08
ADAM_WD = 0.01
ADAM_STEP = 10

OUT_PAD = 1024

R_IN, R_OUT, R_DOWN = 584, 256, 1024

VMEM_BIG = 56 * 1024 * 1024
MESH_ID = pl.DeviceIdType.MESH


def _mx(a):
    return a.astype(MXU_DTYPE)


def _dot(a, b):
    return jnp.dot(a, b, preferred_element_type=F32)


def _dot_nt(a, b):
    return lax.dot_general(a, b, (((1,), (1,)), ((), ())), preferred_element_type=F32)


def _dot_tn(a, b):
    return lax.dot_general(a, b, (((0,), (0,)), ((), ())), preferred_element_type=F32)


def _rms_r(x):
    return lax.rsqrt(jnp.mean(x * x, axis=-1, keepdims=True) + NORM_EPS)


def _rms_bwd(x, r, g, dy):
    xh = x * r
    gdy = dy * g
    dx = r * (gdy - xh * jnp.mean(gdy * xh, axis=-1, keepdims=True))
    return dx, jnp.sum(dy * xh, axis=0, keepdims=True)


def _low_half(rows):
    return lax.broadcasted_iota(jnp.int32, (rows, HEAD_PAD), 1) < 64


def _spread_heads(x):
    low = _low_half(x.shape[0])
    parts = []
    for p in range(x.shape[1] // HEAD_PAD):
        pair = x[:, HEAD_PAD * p:HEAD_PAD * (p + 1)]
        parts += [jnp.where(low, pair, 0.0), jnp.where(low, pltpu.roll(pair, 64, 1), 0.0)]
    return jnp.concatenate(parts, axis=1)


def _squeeze_heads(x):
    low = _low_half(x.shape[0])
    parts = []
    for p in range(x.shape[1] // (2 * HEAD_PAD)):
        even = x[:, 2 * HEAD_PAD * p:2 * HEAD_PAD * p + HEAD_PAD]
        odd = x[:, 2 * HEAD_PAD * p + HEAD_PAD:2 * HEAD_PAD * (p + 1)]
        parts.append(jnp.where(low, even, pltpu.roll(odd, 64, 1)))
    return parts[0] if len(parts) == 1 else jnp.concatenate(parts, axis=1)


def _params(sem=None, vmem=None):
    kw = {}
    if sem is not None:
        kw["dimension_semantics"] = sem
    if vmem is not None:
        kw["vmem_limit_bytes"] = vmem
    return pltpu.CompilerParams(**kw)


def _vmem_spec():
    return pl.BlockSpec(memory_space=pltpu.VMEM)


def _whole_spec(shape):
    return pl.BlockSpec(shape, lambda: (0,) * len(shape))


def _row_spec(tm, width):
    return pl.BlockSpec((tm, width), lambda i: (i, 0))


def _full_spec(shape):
    return pl.BlockSpec(shape, lambda i: (0,) * len(shape))


def _any_spec():
    return pl.BlockSpec(memory_space=pl.ANY)


def _after(body, n_in, dep):
    if dep is None:
        return body, [], []
    return (lambda *refs: body(*refs[:n_in], *refs[n_in + 1:])), [dep], [_any_spec()]


T_QA, T_KA, T_VA, T_GA = (0, 256, 4), (256, 256, 4), (512, 512, 0), (1024, 512, 0)
T_QS, T_KS, T_VS = (1568, 512, 8), (2080, 128, 2), (2208, 128, 2)
T_ZA = (1536, 128, 0)
ZA_COLS = 2 * GLA_GATE_RANK
IN_COLS = 2336


def _side_by_side(group):
    return group[0], group[1], 0


def _proj_call(x, g_pre, w_in_t, dep=None):
    L = x.shape[0]
    tm = min(512, L)
    groups = [(T_QA, F32), (T_KA, F32), (T_VA, MXU_DTYPE), (T_GA, F32),
              (T_QS, MXU_DTYPE), (T_KS, MXU_DTYPE), (T_VS, MXU_DTYPE), (T_ZA, F32)]
    widths = [rows * (2 if heads else 1) for (_, rows, heads), _ in groups]

    def body(x_ref, g_ref, w_ref, *outs):
        xv = x_ref[...]
        u = _mx(xv * _rms_r(xv) * g_ref[...])
        for ref, (grp, _) in zip(outs, groups):
            first, rows, heads = grp
            val = _dot_nt(u, w_ref[first:first + rows, :])
            if heads:
                val = _spread_heads(val)
            if grp is T_ZA:
                val = jnp.where(lax.broadcasted_iota(jnp.int32, val.shape, 1) < ZA_COLS, val, 0.0)
            if grp is T_QS:
                val = val * 0.125
            ref[...] = val.astype(ref.dtype)

    body, extra, extra_specs = _after(body, 3, dep)
    return pl.pallas_call(
        body, name="proj_fwd", grid=(L // tm,),
        in_specs=[_row_spec(tm, D_MODEL), _full_spec((1, D_MODEL)), _vmem_spec()] + extra_specs,
        out_specs=[_row_spec(tm, w) for w in widths],
        out_shape=[jax.ShapeDtypeStruct((L, w), dt) for w, (_, dt) in zip(widths, groups)],
        compiler_params=_params(("arbitrary",), VMEM_BIG),
    )(x, g_pre, w_in_t, *extra)


def _tri_masks():
    row = lax.broadcasted_iota(jnp.int32, (GLA_CHUNK, GLA_CHUNK), 0)
    col = lax.broadcasted_iota(jnp.int32, (GLA_CHUNK, GLA_CHUNK), 1)
    return row >= col, row <= col


def _chunk_sums(tri_m, x):
    hi = _mx(x)
    rest = x - hi.astype(F32)
    mid = _mx(rest)
    lo = _mx(rest - mid.astype(F32))
    return _dot(tri_m, hi) + _dot(tri_m, mid) + _dot(tri_m, lo)


def _gla_block_pre(q_r, k_r, z_r, w_r, b_r, rev, nc, qd_s, ki_s, ks_s, dec_s, keep=None):
    tri_f, tri_b = _tri_masks()
    tri_m = _mx((tri_b if rev else tri_f).astype(F32))
    g = _dot(_mx(z_r[...]), w_r[...]) + b_r[...]
    la = (jnp.minimum(g, 0.0) - jnp.log(1.0 + jnp.exp(-jnp.abs(g)))) * (1.0 / GLA_GATE_NORM)
    sums, lasts = [], []
    for c in range(nc):
        b_c = _chunk_sums(tri_m, la[GLA_CHUNK * c:GLA_CHUNK * (c + 1)])
        blast = b_c[0:1] if rev else b_c[GLA_CHUNK - 1:GLA_CHUNK]
        dec_s[c] = _spread_heads(jnp.exp(blast))
        sums.append(b_c)
        lasts.append(jnp.broadcast_to(blast, b_c.shape))
    b = jnp.concatenate(sums, axis=0)
    eb = jnp.exp(b)
    enb = jnp.exp(-b)
    elb = jnp.exp(jnp.concatenate(lasts, axis=0) - b)
    q, k = _squeeze_heads(q_r[...]), _squeeze_heads(k_r[...])
    qd_s[...] = _spread_heads(q * 0.125 * eb).astype(qd_s.dtype)
    ki_s[...] = _spread_heads(k * enb).astype(ki_s.dtype)
    ks_s[...] = _spread_heads(k * elb).astype(ks_s.dtype)
    if keep is not None:
        keep[0][...] = g
        for ref, val in zip(keep[1:], (eb, enb, elb)):
            ref[...] = _spread_heads(val)


def _gla_fwd_call(qa, ka, va, za, wgf, bgf, wgb, bgb):
    L = qa.shape[0]
    br = min(512, L)
    nb, nc, n_chunks = L // br, br // GLA_CHUNK, L // GLA_CHUNK
    hw = GLA_HEADS * HEAD_PAD

    def body(qaf, kaf, vaf, zaf, qab, kab, vab, zab, wgf_r, bgf_r, wgb_r, bgb_r,
             of_r, ob_r, sf_r, sb_r, st_f, st_b, pre_f, pre_b):
        @pl.when(pl.program_id(0) == 0)
        def _():
            st_f[...] = jnp.zeros_like(st_f)
            st_b[...] = jnp.zeros_like(st_b)

        _gla_block_pre(qaf, kaf, zaf, wgf_r, bgf_r, False, nc, *pre_f)
        _gla_block_pre(qab, kab, zab, wgb_r, bgb_r, True, nc, *pre_b)
        tri_f, tri_b = _tri_masks()

        def one(tri, pre, v_r, o_r, s_r, st, ci):
            qd_s, ki_s, ks_s, dec_s = pre
            rows = pl.ds(pl.multiple_of(ci * GLA_CHUNK, GLA_CHUNK), GLA_CHUNK)
            dec = dec_s[ci]
            heads = range(GLA_HEADS)
            lanes = [slice(HEAD_PAD * h, HEAD_PAD * (h + 1)) for h in heads]
            qd = [qd_s[rows, sl] for sl in lanes]
            v = [v_r[rows, sl] for sl in lanes]
            s_t = [st[h] for h in heads]
            a = [_dot_nt(qd[h], ki_s[rows, lanes[h]]) for h in heads]
            carried = [_dot_nt(qd[h], _mx(s_t[h])) for h in heads]
            grown = [_dot_tn(v[h], ks_s[rows, lanes[h]]) for h in heads]
            a = [_mx(jnp.where(tri, a[h], 0.0)) for h in heads]
            inner = [_dot(a[h], v[h]) for h in heads]
            for h in heads:
                s_r[ci, h] = s_t[h].astype(s_r.dtype)
                o_r[rows, lanes[h]] = inner[h] + carried[h]
                st[h] = s_t[h] * dec[:, lanes[h]] + grown[h]

        def loop(t, carry):
            one(tri_f, pre_f, vaf, of_r, sf_r, st_f, t)
            one(tri_b, pre_b, vab, ob_r, sb_r, st_b, nc - 1 - t)
            return carry

        lax.fori_loop(0, nc, loop, 0, unroll=True)

    fwd = lambda i: (i, 0)
    bwd = lambda i: (nb - 1 - i, 0)
    ins = lambda m: [pl.BlockSpec((br, hw), m), pl.BlockSpec((br, hw), m),
                     pl.BlockSpec((br, hw), m), pl.BlockSpec((br, 128), m)]
    wspecs = [_full_spec((128, hw // 2)), _full_spec((1, hw // 2))] * 2
    s_shape = (nc, GLA_HEADS, HEAD_PAD, HEAD_PAD)
    pre_scratch = [pltpu.VMEM((br, hw), MXU_DTYPE)] * 3 + [pltpu.VMEM((nc, 1, hw), F32)]
    return pl.pallas_call(
        body, name="gla_fwd", grid=(nb,),
        in_specs=ins(fwd) + ins(bwd) + wspecs,
        out_specs=[pl.BlockSpec((br, hw), fwd), pl.BlockSpec((br, hw), bwd),
                   pl.BlockSpec(s_shape, lambda i: (i, 0, 0, 0)),
                   pl.BlockSpec(s_shape, lambda i: (nb - 1 - i, 0, 0, 0))],
        out_shape=[jax.ShapeDtypeStruct((L, hw), F32), jax.ShapeDtypeStruct((L, hw), F32),
                   jax.ShapeDtypeStruct((n_chunks,) + s_shape[1:], MXU_DTYPE),
                   jax.ShapeDtypeStruct((n_chunks,) + s_shape[1:], MXU_DTYPE)],
        scratch_shapes=[pltpu.VMEM(s_shape[1:], F32), pltpu.VMEM(s_shape[1:], F32), pre_scratch, pre_scratch],
        compiler_params=_params(("arbitrary",), VMEM_BIG),
    )(qa, ka, va, za, qa, ka, va, za, wgf, bgf, wgb, bgb)


def _gla_bwd_call(qa, ka, va, za, do, sf, sb, wgf, bgf, wgb, bgb, dep=None):
    L = qa.shape[0]
    br = min(512, L)
    nb, nc = L // br, br // GLA_CHUNK
    hw = GLA_HEADS * HEAD_PAD

    def body(qaf, kaf, vaf, zaf, dof, sf_r, qab, kab, vab, zab, dob, sb_r, wgf_r, bgf_r, wgb_r, bgb_r,
             dqf, dkf, dvf, dzf, dwf, dbf, dqb, dkb, dvb, dzb, dwb, dbb, gt_f, gt_b, pre_f, pre_b):
        @pl.when(pl.program_id(0) == 0)
        def _():
            for ref in (gt_f, gt_b, dwf, dbf, dwb, dbb):
                ref[...] = jnp.zeros_like(ref)

        _gla_block_pre(qaf, kaf, zaf, wgf_r, bgf_r, False, nc, *pre_f[:4], keep=pre_f[4:8])
        _gla_block_pre(qab, kab, zab, wgb_r, bgb_r, True, nc, *pre_b[:4], keep=pre_b[4:8])
        tri_f, tri_b = _tri_masks()
        row_w = lax.broadcasted_iota(jnp.int32, (GLA_CHUNK, HEAD_PAD), 0)

        def one(rev, pre, q_r, k_r, v_r, do_r, s_r, dq_r, dk_r, dv_r, gt, ci):
            qd_s, ki_s, ks_s, dec_s, _, eb_s, enb_s, elb_s, db_s = pre
            tri = tri_b if rev else tri_f
            last_row = 0 if rev else GLA_CHUNK - 1
            rows = pl.ds(pl.multiple_of(ci * GLA_CHUNK, GLA_CHUNK), GLA_CHUNK)
            dec = dec_s[ci]
            heads = range(GLA_HEADS)
            lanes = [slice(HEAD_PAD * h, HEAD_PAD * (h + 1)) for h in heads]
            qd = [qd_s[rows, sl] for sl in lanes]
            ki = [ki_s[rows, sl] for sl in lanes]
            ks = [ks_s[rows, sl] for sl in lanes]
            v = [v_r[rows, sl] for sl in lanes]
            do_h = [_mx(do_r[rows, sl]) for sl in lanes]
            s_t = [s_r[ci, h] for h in heads]
            g_t = [gt[h] for h in heads]
            g_m = [_mx(g_t[h]) for h in heads]
            a = [_dot_nt(qd[h], ki[h]) for h in heads]
            da = [_dot_nt(do_h[h], v[h]) for h in heads]
            dv_carried = [_dot_nt(ks[h], g_m[h]) for h in heads]
            dqd_carried = [_dot(do_h[h], _mx(s_t[h])) for h in heads]
            dks = [_dot(v[h], g_m[h]) for h in heads]
            g_grown = [_dot_tn(do_h[h], qd[h]) for h in heads]
            a = [_mx(jnp.where(tri, a[h], 0.0)) for h in heads]
            da = [_mx(jnp.where(tri, da[h], 0.0)) for h in heads]
            dv_inner = [_dot_tn(a[h], do_h[h]) for h in heads]
            dqd_inner = [_dot(da[h], ki[h]) for h in heads]
            dki = [_dot_tn(da[h], qd[h]) for h in heads]
            dq, dk = [], []
            for h in heads:
                sl = lanes[h]
                dv_r[rows, sl] = (dv_inner[h] + dv_carried[h]).astype(dv_r.dtype)
                ddec = jnp.sum(g_t[h] * s_t[h].astype(F32), axis=0, keepdims=True)
                gt[h] = g_t[h] * dec[:, sl] + g_grown[h]
                dq.append((dqd_inner[h] + dqd_carried[h]) * eb_s[rows, sl] * 0.125)
                dk_state = dks[h] * elb_s[rows, sl]
                dk.append(dki[h] * enb_s[rows, sl] + dk_state)
                k = k_r[rows, sl]
                dblast = jnp.sum(dk_state * k, axis=0, keepdims=True) + dec[:, sl] * ddec
                db_s[rows, sl] = q_r[rows, sl] * dq[h] - k * dk[h] + jnp.where(row_w == last_row, dblast, 0.0)
            low = _low_half(GLA_CHUNK)
            for pair in range(GLA_HEADS // 2):
                psl = slice(HEAD_PAD * pair, HEAD_PAD * (pair + 1))
                for ref, val in ((dq_r, dq), (dk_r, dk)):
                    both = jnp.where(low, val[2 * pair], pltpu.roll(val[2 * pair + 1], 64, 1))
                    ref[rows, psl] = both.astype(ref.dtype)

        def loop(t, carry):
            one(False, pre_f, qaf, kaf, vaf, dof, sf_r, dqf, dkf, dvf, gt_f, nc - 1 - t)
            one(True, pre_b, qab, kab, vab, dob, sb_r, dqb, dkb, dvb, gt_b, t)
            return carry

        lax.fori_loop(0, nc, loop, 0, unroll=True)

        def gate_grads(rev, pre, z_r, w_r, dz_r, dw_r, dbias_r):
            g_s, db_s = pre[4], pre[8]
            back_m = _mx((tri_f if rev else tri_b).astype(F32))
            db = _squeeze_heads(db_s[...])
            dla = jnp.concatenate([_chunk_sums(back_m, db[GLA_CHUNK * c:GLA_CHUNK * (c + 1)]) for c in range(nc)],
                                  axis=0)
            dg = dla * (1.0 / GLA_GATE_NORM) * (1.0 / (1.0 + jnp.exp(g_s[...])))
            dg_m = _mx(dg)
            dz_r[...] = _dot_nt(dg_m, w_r[...])
            dw_r[...] += _dot_tn(_mx(z_r[...]), dg_m)
            dbias_r[...] += jnp.sum(dg, axis=0, keepdims=True)

        gate_grads(False, pre_f, zaf, wgf_r, dzf, dwf, dbf)
        gate_grads(True, pre_b, zab, wgb_r, dzb, dwb, dbb)

    last_first = lambda i: (nb - 1 - i, 0)
    first_last = lambda i: (i, 0)
    s_shape = (nc, GLA_HEADS, HEAD_PAD, HEAD_PAD)

    def ins(m):
        return [pl.BlockSpec((br, hw), m), pl.BlockSpec((br, hw), m), pl.BlockSpec((br, hw), m),
                pl.BlockSpec((br, 128), m), pl.BlockSpec((br, hw), m),
                pl.BlockSpec(s_shape, lambda i: m(i) + (0, 0))]

    def outs(m):
        return [pl.BlockSpec((br, hw // 2), m), pl.BlockSpec((br, hw // 2), m), pl.BlockSpec((br, hw), m),
                pl.BlockSpec((br, 128), m), _full_spec((128, hw // 2)), _full_spec((1, hw // 2))]

    out_shape = [jax.ShapeDtypeStruct((L, hw // 2), MXU_DTYPE)] * 2 + [
        jax.ShapeDtypeStruct((L, hw), MXU_DTYPE),
        jax.ShapeDtypeStruct((L, 128), F32), jax.ShapeDtypeStruct((128, hw // 2), F32),
        jax.ShapeDtypeStruct((1, hw // 2), F32)]
    wspecs = [_full_spec((128, hw // 2)), _full_spec((1, hw // 2))] * 2
    body, extra, extra_specs = _after(body, 16, dep)
    pre_scratch = ([pltpu.VMEM((br, hw), MXU_DTYPE)] * 3 + [pltpu.VMEM((nc, 1, hw), F32)]
                   + [pltpu.VMEM((br, hw // 2), F32)] + [pltpu.VMEM((br, hw), F32)] * 4)
    return pl.pallas_call(
        body, name="gla_bwd", grid=(nb,),
        in_specs=ins(last_first) + ins(first_last) + wspecs + extra_specs,
        out_specs=outs(last_first) + outs(first_last),
        out_shape=out_shape + out_shape,
        scratch_shapes=[pltpu.VMEM(s_shape[1:], F32), pltpu.VMEM(s_shape[1:], F32), pre_scratch, pre_scratch],
        compiler_params=_params(("arbitrary",), VMEM_BIG),
    )(qa, ka, va, za, do, sf, qa, ka, va, za, do, sb, wgf, bgf, wgb, bgb, *extra)


def _t5_buckets(rel):
    nb = REL_BUCKETS // 2
    ret = (rel > 0).astype(np.int32) * nb
    n = np.abs(rel)
    max_exact = nb // 2
    large = max_exact + (np.log(np.maximum(n, 1).astype(np.float32) / max_exact)
                         / math.log(REL_MAX_DIST / max_exact) * (nb - max_exact)).astype(np.int32)
    large = np.minimum(large, nb - 1)
    return ret + np.where(n < max_exact, n, large)


SWA_GROUP = SWA_Q_HEADS // SWA_KV_HEADS
SWA_SPAN = 3 * SWA_BLOCK
SWA_GROUP_LANES = SWA_GROUP * SWA_BLOCK


def _band_buckets():
    s = np.arange(SWA_SPAN)[:, None]
    c = np.arange(SWA_BLOCK)[None, :]
    return _t5_buckets(s - SWA_BLOCK - c).astype(np.int32)


def _swa_valid(n, seq_len):
    key_pos = (n - 1) * SWA_BLOCK + lax.broadcasted_iota(jnp.int32, (SWA_SPAN, 1), 0)
    return (key_pos >= 0) & (key_pos < seq_len)


def _swa_sink_row(sink_r, kv):
    lane = lax.broadcasted_iota(jnp.int32, (1, SWA_GROUP_LANES), 1)
    row = jnp.full((1, SWA_GROUP_LANES), sink_r[kv * SWA_GROUP], F32)
    for g in range(1, SWA_GROUP):
        row = jnp.where(lane >= g * SWA_BLOCK, sink_r[kv * SWA_GROUP + g], row)
    return row


SWA_STEP_BLOCKS = 8


def _swa_group(ref, kv, rows):
    first = kv * SWA_GROUP
    return jnp.concatenate([ref[rows, HEAD_PAD * h:HEAD_PAD * (h + 1)] for h in range(first, first + SWA_GROUP)],
                           axis=0)


def _swa_softmax(scores, bias_t, sink_row, valid):
    st = jnp.where(valid, scores + bias_t, -1e30)
    m = jnp.maximum(jnp.max(st, axis=0, keepdims=True), sink_row)
    p = jnp.exp(st - m)
    e_sink = jnp.exp(sink_row - m)
    inv = 1.0 / (jnp.sum(p, axis=0, keepdims=True) + e_sink)
    return p * inv, e_sink * inv


def _swa_fwd_call(qs, ks, vs, bias, sink, dep=None):
    L = qs.shape[0]

    def block(n, rows, q_r, k_r, v_r, bias_r, sink_r, o_r):
        span = pl.ds(pl.multiple_of(n * SWA_BLOCK, SWA_BLOCK), SWA_SPAN)
        valid = _swa_valid(n, L)
        groups = range(SWA_KV_HEADS)
        lanes = [slice(HEAD_PAD * kv, HEAD_PAD * (kv + 1)) for kv in groups]
        scores = [_dot_nt(k_r[span, lanes[kv]], _swa_group(q_r, kv, rows)) for kv in groups]
        probs = [_swa_softmax(scores[kv], bias_r[kv], _swa_sink_row(sink_r, kv), valid)[0] for kv in groups]
        low = _low_half(SWA_BLOCK)
        for kv in groups:
            og = _dot_tn(_mx(probs[kv]), v_r[span, lanes[kv]])
            for pair in range(SWA_GROUP // 2):
                even = og[2 * SWA_BLOCK * pair:2 * SWA_BLOCK * pair + SWA_BLOCK]
                odd = og[2 * SWA_BLOCK * pair + SWA_BLOCK:2 * SWA_BLOCK * (pair + 1)]
                first = HEAD_PAD * (kv * SWA_GROUP // 2 + pair)
                o_r[rows, first:first + HEAD_PAD] = jnp.where(low, even, pltpu.roll(odd, 64, 1)).astype(o_r.dtype)

    def body(*refs):
        for j in range(SWA_STEP_BLOCKS):
            block(SWA_STEP_BLOCKS * pl.program_id(0) + j, slice(SWA_BLOCK * j, SWA_BLOCK * (j + 1)), *refs)

    qw = SWA_Q_HEADS * HEAD_PAD
    tm = SWA_STEP_BLOCKS * SWA_BLOCK
    body, extra, extra_specs = _after(body, 5, dep)
    return pl.pallas_call(
        body, name="swa_fwd", grid=(L // tm,),
        in_specs=[_row_spec(tm, qw), _vmem_spec(), _vmem_spec(), _vmem_spec(),
                  pl.BlockSpec(memory_space=pltpu.SMEM)] + extra_specs,
        out_specs=_row_spec(tm, qw // 2),
        out_shape=jax.ShapeDtypeStruct((L, qw // 2), MXU_DTYPE),
        compiler_params=_params(("arbitrary",), VMEM_BIG),
    )(qs, ks, vs, bias, sink, *extra)


def _swa_bwd_call(qs, ks, vs, bias, sink, do, dep=None):
    L = qs.shape[0]
    qw = SWA_Q_HEADS * HEAD_PAD
    kw = SWA_KV_HEADS * HEAD_PAD

    def body(*refs):
        dk_r, dv_r, dbias_r, dsink_r = refs[7:]

        @pl.when(pl.program_id(0) == 0)
        def _():
            for ref in (dk_r, dv_r, dbias_r, dsink_r):
                ref[...] = jnp.zeros_like(ref)

        for j in range(SWA_STEP_BLOCKS):
            block(SWA_STEP_BLOCKS * pl.program_id(0) + j, slice(SWA_BLOCK * j, SWA_BLOCK * (j + 1)), *refs)

    def block(n, rows, q_r, k_r, v_r, bias_r, sink_r, do_r, dq_r, dk_r, dv_r, dbias_r, dsink_r):
        span = pl.ds(pl.multiple_of(n * SWA_BLOCK, SWA_BLOCK), SWA_SPAN)
        valid = _swa_valid(n, L)
        groups = range(SWA_KV_HEADS)
        lanes = [slice(HEAD_PAD * kv, HEAD_PAD * (kv + 1)) for kv in groups]
        kk = [k_r[span, sl] for sl in lanes]
        vv = [v_r[span, sl] for sl in lanes]
        qg = [_swa_group(q_r, kv, rows) for kv in groups]
        dog = [_swa_group(do_r, kv, rows) for kv in groups]
        scores = [_dot_nt(kk[kv], qg[kv]) for kv in groups]
        dp = [_dot_nt(vv[kv], dog[kv]) for kv in groups]
        probs = [_swa_softmax(scores[kv], bias_r[kv], _swa_sink_row(sink_r, kv), valid) for kv in groups]
        ds_m, pn_m = [], []
        for kv in groups:
            pn, p_sink = probs[kv]
            delta = jnp.sum(pn * dp[kv], axis=0, keepdims=True)
            ds = pn * (dp[kv] - delta)
            dsink_r[kv] -= p_sink * delta
            dbias_r[kv] += ds
            ds_m.append(_mx(ds))
            pn_m.append(_mx(pn))
        dqg = [_dot_tn(ds_m[kv], kk[kv]) * 0.125 for kv in groups]
        dkk = [_dot(ds_m[kv], qg[kv]) for kv in groups]
        dvv = [_dot(pn_m[kv], dog[kv]) for kv in groups]
        low = _low_half(SWA_BLOCK)
        for kv in groups:
            for pair in range(SWA_GROUP // 2):
                even = dqg[kv][2 * SWA_BLOCK * pair:2 * SWA_BLOCK * pair + SWA_BLOCK]
                odd = dqg[kv][2 * SWA_BLOCK * pair + SWA_BLOCK:2 * SWA_BLOCK * (pair + 1)]
                first = HEAD_PAD * (kv * SWA_GROUP // 2 + pair)
                dq_r[rows, first:first + HEAD_PAD] = jnp.where(low, even, pltpu.roll(odd, 64, 1)).astype(dq_r.dtype)
            dk_r[span, lanes[kv]] += dkk[kv]
            dv_r[span, lanes[kv]] += dvv[kv]

    tm = SWA_STEP_BLOCKS * SWA_BLOCK
    body, extra, extra_specs = _after(body, 6, dep)
    return pl.pallas_call(
        body, name="swa_bwd", grid=(L // tm,),
        in_specs=[_row_spec(tm, qw), _vmem_spec(), _vmem_spec(), _vmem_spec(),
                  pl.BlockSpec(memory_space=pltpu.SMEM), _row_spec(tm, qw)] + extra_specs,
        out_specs=[_row_spec(tm, qw // 2), _vmem_spec(), _vmem_spec(), _vmem_spec(), _vmem_spec()],
        out_shape=[jax.ShapeDtypeStruct((L, qw // 2), MXU_DTYPE),
                   jax.ShapeDtypeStruct((L + 2 * SWA_BLOCK, kw), F32),
                   jax.ShapeDtypeStruct((L + 2 * SWA_BLOCK, kw), F32),
                   jax.ShapeDtypeStruct((SWA_KV_HEADS, SWA_SPAN, SWA_GROUP_LANES), F32),
                   jax.ShapeDtypeStruct((SWA_KV_HEADS, 1, SWA_GROUP_LANES), F32)],
        compiler_params=_params(("arbitrary",), VMEM_BIG),
    )(qs, ks, vs, bias, sink, do, *extra)


def _bias_call(rel_bias, buckets, dep=None):
    def body(t_r, bk_r, o_r):
        bk = bk_r[...]
        s = lax.broadcasted_iota(jnp.int32, bk.shape, 0)
        c = lax.broadcasted_iota(jnp.int32, bk.shape, 1)
        in_band = jnp.abs(s - SWA_BLOCK - c) <= SWA_BLOCK
        for h in range(SWA_Q_HEADS):
            acc = jnp.zeros(bk.shape, F32)
            for b in range(REL_BUCKETS):
                acc = jnp.where(bk == b, t_r[b, h], acc)
            g = h % SWA_GROUP
            o_r[h // SWA_GROUP, :, SWA_BLOCK * g:SWA_BLOCK * (g + 1)] = jnp.where(in_band, acc, -1e30)

    body, extra, extra_specs = _after(body, 2, dep)
    return pl.pallas_call(
        body, name="band_bias",
        in_specs=[pl.BlockSpec(memory_space=pltpu.SMEM), _vmem_spec()] + extra_specs, out_specs=_vmem_spec(),
        out_shape=jax.ShapeDtypeStruct((SWA_KV_HEADS, SWA_SPAN, SWA_GROUP_LANES), F32),
    )(rel_bias, buckets, *extra)


def _relbias_call(dbias, dsink, buckets, dep=None):
    def body(db_r, ds_r, bk_r, o_r, os_r):
        bk = bk_r[...]
        rowi = lax.broadcasted_iota(jnp.int32, (REL_BUCKETS, 128), 0)
        lanei = lax.broadcasted_iota(jnp.int32, (REL_BUCKETS, 128), 1)
        lane1 = lax.broadcasted_iota(jnp.int32, (1, 128), 1)
        acc = jnp.zeros((REL_BUCKETS, 128), F32)
        acc_sink = jnp.zeros((1, 128), F32)
        heads = [(h // SWA_GROUP, slice(SWA_BLOCK * (h % SWA_GROUP), SWA_BLOCK * (h % SWA_GROUP + 1)))
                 for h in range(SWA_Q_HEADS)]
        for b in range(REL_BUCKETS):
            in_bucket = bk == b
            for h, (kv, lanes) in enumerate(heads):
                s = jnp.sum(jnp.where(in_bucket, db_r[kv, :, lanes], 0.0))
                acc = acc + jnp.where((rowi == b) & (lanei == h), s, 0.0)
        for h, (kv, lanes) in enumerate(heads):
            acc_sink = acc_sink + jnp.where(lane1 == h, jnp.sum(ds_r[kv, :, lanes]), 0.0)
        o_r[...] = acc
        os_r[...] = acc_sink

    body, extra, extra_specs = _after(body, 3, dep)
    return pl.pallas_call(
        body, name="relbias_grad",
        in_specs=[_vmem_spec()] * 3 + extra_specs, out_specs=[_vmem_spec()] * 2,
        out_shape=[jax.ShapeDtypeStruct((REL_BUCKETS, 128), F32), jax.ShapeDtypeStruct((1, 128), F32)],
    )(dbias, dsink, buckets, *extra)


def _mix_call(o_f, o_b, ga, o_s, x, gn, w_out_p, g_post, g_pre2, dep=None):
    L = x.shape[0]
    tm = min(512, L)
    hw = GLA_HEADS * HEAD_PAD

    def body(of_r, ob_r, ga_r, os_r, x_r, gn_r, w_r, gp_r, g2_r, cat_r, mix_r, h1_r, n2_r):
        gn_v = gn_r[...]
        for h in range(GLA_HEADS):
            sl = slice(HEAD_PAD * h, HEAD_PAD * (h + 1))
            oh = of_r[:, sl] + ob_r[:, sl]
            on = oh * _rms_r(oh) * gn_v
            gate = ga_r[:, sl]
            cat_r[:, sl] = (on * (gate * jax.nn.sigmoid(gate))).astype(cat_r.dtype)
        os_v = os_r[...]
        cat_r[:, hw:] = os_v
        mix = _dot(cat_r[:, :hw], w_r[:hw, :]) + _dot(os_v, w_r[hw:, :])
        mix_r[...] = mix
        h1 = x_r[...] + mix * _rms_r(mix) * gp_r[...]
        h1_r[...] = h1
        n2_r[...] = (h1 * _rms_r(h1) * g2_r[...]).astype(n2_r.dtype)

    body, extra, extra_specs = _after(body, 9, dep)
    return pl.pallas_call(
        body, name="mix_fwd", grid=(L // tm,),
        in_specs=[_row_spec(tm, hw), _row_spec(tm, hw), _row_spec(tm, hw), _row_spec(tm, OUT_PAD - hw),
                  _row_spec(tm, D_MODEL), _full_spec((1, HEAD_PAD)), _vmem_spec(),
                  _full_spec((1, D_MODEL)), _full_spec((1, D_MODEL))] + extra_specs,
        out_specs=[_row_spec(tm, OUT_PAD), _row_spec(tm, D_MODEL), _row_spec(tm, D_MODEL), _row_spec(tm, D_MODEL)],
        out_shape=[jax.ShapeDtypeStruct((L, OUT_PAD), MXU_DTYPE), jax.ShapeDtypeStruct((L, D_MODEL), F32),
                   jax.ShapeDtypeStruct((L, D_MODEL), F32), jax.ShapeDtypeStruct((L, D_MODEL), MXU_DTYPE)],
        compiler_params=_params(("arbitrary",), VMEM_BIG),
    )(o_f, o_b, ga, o_s, x, gn, w_out_p, g_post, g_pre2, *extra)


def _mlp_fwd_call(n2, h1, tgt, w_ud, g_post):
    L = n2.shape[0]
    tm = min(512, L)
    blk = D_FF // N_CHIPS

    def body(n2_r, h1_r, t_r, w_r, g_r, a_r, rz_r, dh2_r, dff_r, loss_r, dg_r):
        @pl.when(pl.program_id(0) == 0)
        def _():
            loss_r[...] = jnp.zeros_like(loss_r)
            dg_r[...] = jnp.zeros_like(dg_r)

        n2v = n2_r[...]
        ff = jnp.zeros((tm, D_MODEL), F32)
        for j in range(N_CHIPS):
            sl = slice(blk * j, blk * (j + 1))
            rz = jnp.maximum(_dot(n2v, w_r[j, 0]), 0.0)
            a = _mx(rz * rz)
            rz_r[:, sl] = rz.astype(rz_r.dtype)
            a_r[:, sl] = a
            ff = ff + _dot(a, w_r[j, 1])
        g = g_r[...]
        r = _rms_r(ff)
        err = h1_r[...] + ff * r * g - t_r[...]
        loss_r[...] += 0.5 * jnp.sum(err * err) / D_MODEL
        dh2 = err * (1.0 / D_MODEL)
        dh2_r[...] = dh2
        dff, dg = _rms_bwd(ff, r, g, dh2)
        dff_r[...] = dff.astype(dff_r.dtype)
        dg_r[...] += dg

    return pl.pallas_call(
        body, name="mlp_fwd", grid=(L // tm,),
        in_specs=[_row_spec(tm, D_MODEL), _row_spec(tm, D_MODEL), _row_spec(tm, D_MODEL),
                  _vmem_spec(), _full_spec((1, D_MODEL))],
        out_specs=[_row_spec(tm, D_FF), _row_spec(tm, D_FF), _row_spec(tm, D_MODEL), _row_spec(tm, D_MODEL),
                   _full_spec((1, 128)), _full_spec((1, D_MODEL))],
        out_shape=[jax.ShapeDtypeStruct((L, D_FF), MXU_DTYPE), jax.ShapeDtypeStruct((L, D_FF), MXU_DTYPE),
                   jax.ShapeDtypeStruct((L, D_MODEL), F32), jax.ShapeDtypeStruct((L, D_MODEL), MXU_DTYPE),
                   jax.ShapeDtypeStruct((1, 128), F32), jax.ShapeDtypeStruct((1, D_MODEL), F32)],
        compiler_params=_params(("arbitrary",), VMEM_BIG),
    )(n2, h1, tgt, w_ud, g_post)


def _mlp_bwd_call(dff, rz, w_ud):
    L = dff.shape[0]
    tm = min(512, L)
    blk = D_FF // N_CHIPS

    def body(dff_r, rz_r, w_r, dz_r, dn2_r):
        dffv = dff_r[...]
        dn2 = jnp.zeros((tm, D_MODEL), F32)
        for j in range(N_CHIPS):
            sl = slice(blk * j, blk * (j + 1))
            dz = _mx(_dot_nt(dffv, w_r[j, 1]) * 2.0 * rz_r[:, sl].astype(F32))
            dz_r[:, sl] = dz
            dn2 = dn2 + _dot_nt(dz, w_r[j, 0])
        dn2_r[...] = dn2

    return pl.pallas_call(
        body, name="mlp_bwd", grid=(L // tm,),
        in_specs=[_row_spec(tm, D_MODEL), _row_spec(tm, D_FF), _vmem_spec()],
        out_specs=[_row_spec(tm, D_FF), _row_spec(tm, D_MODEL)],
        out_shape=[jax.ShapeDtypeStruct((L, D_FF), MXU_DTYPE), jax.ShapeDtypeStruct((L, D_MODEL), F32)],
        compiler_params=_params(("arbitrary",), VMEM_BIG),
    )(dff, rz, w_ud)


def _mlp_wgrad_call(a, dff, n2, dz):
    L = a.shape[0]
    tf = 512
    per = (D_FF // N_CHIPS) // tf

    def body(a_r, dff_r, n2_r, dz_r, dwd_r, dwu_r):
        dwd_r[...] = _dot_tn(a_r[...], dff_r[...])
        dwu_r[...] = _dot_tn(n2_r[...], dz_r[...])

    return pl.pallas_call(
        body, name="mlp_wgrad", grid=(D_FF // tf,),
        in_specs=[pl.BlockSpec((L, tf), lambda j: (0, j)), _vmem_spec(), _vmem_spec(),
                  pl.BlockSpec((L, tf), lambda j: (0, j))],
        out_specs=[pl.BlockSpec((tf, D_MODEL), lambda j: (j, 0)),
                   pl.BlockSpec((None, D_MODEL, tf), lambda j: (j // per, 0, j % per))],
        out_shape=[jax.ShapeDtypeStruct((D_FF, D_MODEL), F32),
                   jax.ShapeDtypeStruct((N_CHIPS, D_MODEL, D_FF // N_CHIPS), F32)],
        compiler_params=_params(("arbitrary",), VMEM_BIG),
    )(a, dff, n2, dz)


def _mix_bwd_call(dn2, dh2, h1, mix, cat, o_f, o_b, ga, gn, g_post, g_pre2, w_out_p):
    L = dn2.shape[0]
    tm = min(512, L)
    hw = GLA_HEADS * HEAD_PAD

    def body(dn2_r, dh2_r, h1_r, mix_r, cat_r, of_r, ob_r, ga_r, gn_r, gp_r, g2_r, w_r,
             dh1_r, do_r, dga_r, dos_r, dw_r, dg2_r, dgp_r, dgn_r):
        @pl.when(pl.program_id(0) == 0)
        def _():
            for ref in (dw_r, dg2_r, dgp_r, dgn_r):
                ref[...] = jnp.zeros_like(ref)

        parts = [slice(start, start + min(256, tm)) for start in range(0, tm, 256)]
        dmix_m = []
        for rs in parts:
            h1 = h1_r[rs, :]
            dx2, dg2 = _rms_bwd(h1, _rms_r(h1), g2_r[...], dn2_r[rs, :])
            dh1 = dh2_r[rs, :] + dx2
            dh1_r[rs, :] = dh1
            dg2_r[...] += dg2
            mix = mix_r[rs, :]
            dmix, dgp = _rms_bwd(mix, _rms_r(mix), gp_r[...], dh1)
            dgp_r[...] += dgp
            dmix_m.append(_mx(dmix))
        dcat = [_dot_nt(d, w_r[...]) for d in dmix_m]
        for rs, d in zip(parts, dmix_m):
            dw_r[...] += _dot_tn(cat_r[rs, :], d)
        gn_v = gn_r[...]
        dgn = jnp.zeros((1, HEAD_PAD), F32)
        for rs, dc in zip(parts, dcat):
            dos_r[rs, :] = _spread_heads(dc[:, hw:]).astype(dos_r.dtype)
            for h in range(GLA_HEADS):
                sl = slice(HEAD_PAD * h, HEAD_PAD * (h + 1))
                oh = of_r[rs, sl] + ob_r[rs, sl]
                rr = _rms_r(oh)
                xh = oh * rr
                gate = ga_r[rs, sl]
                sg = jax.nn.sigmoid(gate)
                silu = gate * sg
                doa = dc[:, sl]
                dga_r[rs, sl] = (doa * (xh * gn_v) * (sg + silu * (1.0 - sg))).astype(dga_r.dtype)
                don = doa * silu
                gd = don * gn_v
                do_r[rs, sl] = rr * (gd - xh * jnp.mean(gd * xh, axis=-1, keepdims=True))
                dgn = dgn + jnp.sum(don * xh, axis=0, keepdims=True)
        dgn_r[...] += dgn

    return pl.pallas_call(
        body, name="mix_bwd", grid=(L // tm,),
        in_specs=[_row_spec(tm, D_MODEL)] * 4 + [_row_spec(tm, OUT_PAD)] + [_row_spec(tm, hw)] * 3
        + [_full_spec((1, HEAD_PAD)), _full_spec((1, D_MODEL)), _full_spec((1, D_MODEL)), _vmem_spec()],
        out_specs=[_row_spec(tm, D_MODEL), _row_spec(tm, hw), _row_spec(tm, hw),
                   _row_spec(tm, SWA_Q_HEADS * HEAD_PAD),
                   _full_spec((OUT_PAD, D_MODEL)), _full_spec((1, D_MODEL)), _full_spec((1, D_MODEL)),
                   _full_spec((1, HEAD_PAD))],
        out_shape=[jax.ShapeDtypeStruct((L, D_MODEL), F32), jax.ShapeDtypeStruct((L, hw), F32),
                   jax.ShapeDtypeStruct((L, hw), MXU_DTYPE),
                   jax.ShapeDtypeStruct((L, SWA_Q_HEADS * HEAD_PAD), MXU_DTYPE),
                   jax.ShapeDtypeStruct((OUT_PAD, D_MODEL), F32), jax.ShapeDtypeStruct((1, D_MODEL), F32),
                   jax.ShapeDtypeStruct((1, D_MODEL), F32), jax.ShapeDtypeStruct((1, HEAD_PAD), F32)],
        compiler_params=_params(("arbitrary",), VMEM_BIG),
    )(dn2, dh2, h1, mix, cat, o_f, o_b, ga, gn, g_post, g_pre2, w_out_p)


def _in_bwd_call(x, dh1, g_pre, w_in_t, pairs, singles, halos, dep=None):
    L = x.shape[0]
    tm = min(512, L)
    per = tm // SWA_BLOCK
    n_pair, n_single, n_halo = len(pairs), len(singles), len(halos)
    groups = [c for c, _ in pairs] + [c for c, _ in singles] + [c for c, _ in halos]

    def body(*refs):
        x_r, dh1_r, g_r, w_r = refs[:4]
        pair_refs = refs[4:4 + 2 * n_pair]
        single_refs = refs[4 + 2 * n_pair:4 + 2 * n_pair + n_single]
        halo_refs = refs[4 + 2 * n_pair + n_single:4 + 2 * n_pair + n_single + per * n_halo]
        dx_r, dw_r, dg_r = refs[4 + 2 * n_pair + n_single + per * n_halo:]

        @pl.when(pl.program_id(0) == 0)
        def _():
            dw_r[...] = jnp.zeros_like(dw_r)
            dg_r[...] = jnp.zeros_like(dg_r)

        xv = x_r[...]
        r = _rms_r(xv)
        g = g_r[...]
        u = _mx(xv * r * g)
        vals = [pair_refs[2 * i][...].astype(F32) + pair_refs[2 * i + 1][...].astype(F32) for i in range(n_pair)]
        vals += [ref[...].astype(F32) for ref in single_refs]
        vals += [jnp.concatenate([ref[...] for ref in halo_refs[per * i:per * (i + 1)]], axis=0)
                 for i in range(n_halo)]
        ds = [_mx(_squeeze_heads(val) if heads else val) for (_, _, heads), val in zip(groups, vals)]
        du = jnp.zeros((tm, D_MODEL), F32)
        for (first, rows, _), d in zip(groups, ds):
            du = du + _dot(d, w_r[first:first + rows, :])
        for (first, rows, _), d in zip(groups, ds):
            dw_r[first:first + rows, :] += _dot_tn(d, u)
        dx, dg = _rms_bwd(xv, r, g, du)
        dx_r[...] = dh1_r[...] + dx
        dg_r[...] += dg

    arrays = [a for _, pr in pairs for a in pr] + [a for _, a in singles]
    specs = [_row_spec(tm, a.shape[1]) for a in arrays]
    for _, a in halos:
        specs += [pl.BlockSpec((SWA_BLOCK, a.shape[1]), lambda i, j=j: (per * i + 1 + j, 0)) for j in range(per)]
        arrays += [a] * per
    body, extra, extra_specs = _after(body, 4 + len(arrays), dep)
    return pl.pallas_call(
        body, name="in_bwd", grid=(L // tm,),
        in_specs=[_row_spec(tm, D_MODEL), _row_spec(tm, D_MODEL), _full_spec((1, D_MODEL)), _vmem_spec()] + specs
        + extra_specs,
        out_specs=[_row_spec(tm, D_MODEL), _full_spec((IN_COLS, D_MODEL)), _full_spec((1, D_MODEL))],
        out_shape=[jax.ShapeDtypeStruct((L, D_MODEL), F32), jax.ShapeDtypeStruct((IN_COLS, D_MODEL), F32),
                   jax.ShapeDtypeStruct((1, D_MODEL), F32)],
        compiler_params=_params(("arbitrary",), VMEM_BIG),
    )(x, dh1, g_pre, w_in_t, *arrays, *extra)


def _adamw_math(w, g, m, v):
    m = ADAM_B1 * m + (1.0 - ADAM_B1) * g
    v = ADAM_B2 * v + (1.0 - ADAM_B2) * (g * g)
    m_hat = m / (1.0 - ADAM_B1 ** ADAM_STEP)
    v_hat = v / (1.0 - ADAM_B2 ** ADAM_STEP)
    delta = -ADAM_LR * (m_hat / (jnp.sqrt(v_hat) + ADAM_EPS) + ADAM_WD * w)
    return delta, m, v


def _adamw_call(w, g, m, v, name, dep=None):
    rows, cols = w.shape
    tr = min(256, rows)

    def body(w_r, g_r, m_r, v_r, g_out_r, d_r, nm_r, nv_r):
        g = g_r[...]
        g_out_r[...] = g
        d_r[...], nm_r[...], nv_r[...] = _adamw_math(w_r[...], g, m_r[...], v_r[...])

    if rows % tr == 0:
        spec, steps = _row_spec(tr, cols), rows // tr
    else:
        spec, steps = pl.BlockSpec((rows, 256), lambda i: (0, i)), cols // 256
    body, extra, extra_specs = _after(body, 4, dep)
    return pl.pallas_call(
        body, name=name, grid=(steps,),
        in_specs=[spec] * 4 + extra_specs, out_specs=[spec] * 4,
        out_shape=[jax.ShapeDtypeStruct(w.shape, F32)] * 4,
        compiler_params=_params(("arbitrary",)),
    )(w, g, m, v, *extra)


def _position():
    return lax.axis_index("x"), lax.axis_index("y"), lax.axis_index("c")


def _other_chips(x, y):
    return [(1 - x, y), (x, 1 - y), (1 - x, 1 - y)]


ROWS, COLS = -2, -1


def _half(ref, which, axis):
    size = ref.shape[axis] // 2
    span = pl.ds(pl.multiple_of(which * size, 16 if axis == ROWS else 128), size)
    index = [slice(None)] * len(ref.shape)
    index[axis] = span
    return ref.at[tuple(index)]


def _quarter(ref, half, which, axis):
    size = ref.shape[axis] // 4
    span = pl.ds(pl.multiple_of((2 * half + which) * size, 16 if axis == ROWS else 128), size)
    index = [slice(None)] * len(ref.shape)
    index[axis] = span
    return ref.at[tuple(index)]


def _first_gather_call(shards, axes, routed):
    n = len(shards)
    per = 7

    def body(*refs):
        srcs, outs = refs[:n], refs[n:2 * n]
        send_sems, recv_sems, local_sems = refs[2 * n:]
        x, y, c = _position()
        me, sibling = (x, y, c), (x, y, 1 - c)
        x_side, y_side, across = _other_chips(x, y)
        local = [pltpu.make_async_copy(srcs[a], outs[a].at[2 * x + y], local_sems.at[a]) for a in range(n)]
        for cp in local:
            cp.start()

        def copy(a, k, dst, to, src=None):
            return pltpu.make_async_remote_copy(
                src_ref=dst if src is None else src, dst_ref=dst, send_sem=send_sems.at[per * a + k],
                recv_sem=recv_sems.at[per * a + k], device_id=to, device_id_type=MESH_ID)

        def half(a, chip, pc):
            return _half(outs[a].at[2 * chip[0] + chip[1]], pc, axes[a])

        def quarter(a, chip, q):
            return _quarter(outs[a].at[2 * chip[0] + chip[1]], c, q, axes[a])

        sends = []
        for a in range(n):
            mine = _half(srcs[a], c, axes[a])
            targets = (x_side, y_side) if routed[a] else (x_side, y_side, across)
            sends += [copy(a, j, half(a, (x, y), c), (*chip, c), src=mine) for j, chip in enumerate(targets)]
        for cp in sends:
            cp.start()
        for a in range(n):
            for j, chip in enumerate((x_side, y_side)):
                copy(a, j, half(a, chip, c), me).wait_recv()
                if routed[a]:
                    other = (y_side, x_side)[j]
                    sends.append(copy(a, 2 + j, quarter(a, chip, j), (*other, c)))
                    sends[-1].start()
                sends.append(copy(a, 4 + j, half(a, chip, c), sibling))
                sends[-1].start()
        for a in range(n):
            if routed[a]:
                for j in range(2):
                    copy(a, 2 + j, quarter(a, across, j), me).wait_recv()
            else:
                copy(a, 2, half(a, across, c), me).wait_recv()
            sends.append(copy(a, 6, half(a, across, c), sibling))
            sends[-1].start()
        for a in range(n):
            for k, chip in ((4, x_side), (5, y_side), (6, across)):
                copy(a, k, half(a, chip, 1 - c), me).wait_recv()
        for cp in sends:
            cp.wait_send()
        for cp in local:
            cp.wait()

    return pl.pallas_call(
        body, name="first_gather",
        in_specs=[_any_spec()] * n, out_specs=[_any_spec()] * n,
        out_shape=[jax.ShapeDtypeStruct((N_CHIPS,) + s.shape, s.dtype) for s in shards],
        scratch_shapes=[pltpu.SemaphoreType.DMA((per * n,)), pltpu.SemaphoreType.DMA((per * n,)),
                        pltpu.SemaphoreType.DMA((n,))],
    )(*shards)


PAIR_PEERS, CHIP_PEERS = 1, 2


def _peers(which):
    x, y, c = _position()
    if which == PAIR_PEERS:
        return [(x, y, 1 - c)]
    return [(px, py, c) for px, py in _other_chips(x, y)]


def _split_start(name, arrays, n_copies, plan, peers=None):
    n = len(arrays)

    def body(*refs):
        ins, send_sems, recv_sems, token = refs[:n], refs[n], refs[n + 1], refs[-1]
        if peers is not None:
            barrier = pltpu.get_barrier_semaphore()
            targets = _peers(peers)
            for target in targets:
                pl.semaphore_signal(barrier, inc=1, device_id=target, device_id_type=MESH_ID)
            pl.semaphore_wait(barrier, len(targets))
        for k, (src, dst, to, _) in enumerate(plan(ins)):
            pltpu.make_async_remote_copy(src_ref=src, dst_ref=dst, send_sem=send_sems.at[k],
                                         recv_sem=recv_sems.at[k], device_id=to, device_id_type=MESH_ID).start()
        token[...] = jnp.zeros_like(token)

    hbm = pl.BlockSpec(memory_space=pltpu.HBM)
    sem = pl.BlockSpec(memory_space=pltpu.SEMAPHORE)
    out = pl.pallas_call(
        body, name=name,
        out_shape=(pltpu.SemaphoreType.DMA((n_copies,)), pltpu.SemaphoreType.DMA((n_copies,)))
        + tuple(pltpu.HBM(a.shape, a.dtype) for a in arrays) + (jax.ShapeDtypeStruct((8, 128), F32),),
        in_specs=[hbm] * n, out_specs=(sem, sem) + (hbm,) * n + (_vmem_spec(),),
        input_output_aliases={i: 2 + i for i in range(n)},
        compiler_params=pltpu.CompilerParams(has_side_effects=pltpu.SideEffectType.DATAFLOW_SIDE_EFFECTING,
                                             collective_id=peers),
    )(*[pltpu.with_memory_space_constraint(a, pltpu.HBM) for a in arrays])
    return (out[0], out[1], tuple(out[2:2 + n])), out[-1]


def _split_wait(name, handle, n_copies, plan, after):
    send_sems, recv_sems, arrays = handle
    n = len(arrays)

    def body(*refs):
        ins, s_sems, r_sems = refs[:n], refs[n], refs[n + 1]
        for k, (src, dst, to, landed) in enumerate(plan(ins)):
            cp = pltpu.make_async_remote_copy(src_ref=src, dst_ref=landed, send_sem=s_sems.at[k],
                                              recv_sem=r_sems.at[k], device_id=to, device_id_type=MESH_ID)
            cp.wait_send()
            cp.wait_recv()

    hbm = pl.BlockSpec(memory_space=pltpu.HBM)
    sem = pl.BlockSpec(memory_space=pltpu.SEMAPHORE)
    out = pl.pallas_call(
        body, name=name,
        out_shape=tuple(pltpu.HBM(a.shape, a.dtype) for a in arrays),
        in_specs=[hbm] * n + [sem, sem, _any_spec()], out_specs=(hbm,) * n,
        input_output_aliases={i: i for i in range(n)},
        compiler_params=pltpu.CompilerParams(has_side_effects=pltpu.SideEffectType.DATAFLOW_SIDE_EFFECTING),
    )(*arrays, send_sems, recv_sems, after)
    return tuple(out)


def _gather_plans(axes):
    n = len(axes)

    def stage_one(refs):
        x, y, c = _position()
        copies = []
        for a, axis in enumerate(axes):
            for px, py in _other_chips(x, y):
                if axis is None:
                    copies.append((refs[a], refs[n + a].at[2 * x + y], (px, py, c), refs[n + a].at[2 * px + py]))
                else:
                    copies.append((_half(refs[a], c, axis), _half(refs[n + a].at[2 * x + y], c, axis),
                                   (px, py, c), _half(refs[n + a].at[2 * px + py], c, axis)))
        return copies

    def stage_two(refs):
        x, y, c = _position()
        copies = []
        for a, axis in enumerate(axes):
            if axis is None:
                continue
            for px, py in _other_chips(x, y):
                piece = _half(refs[n + a].at[2 * px + py], c, axis)
                copies.append((piece, piece, (x, y, 1 - c), _half(refs[n + a].at[2 * px + py], 1 - c, axis)))
        return copies

    return stage_one, stage_two


def _pair_swap_plan(axes):
    n = len(axes)

    def plan(refs):
        x, y, c = _position()
        return [(_half(refs[a], 1 - c, axes[a]), refs[n + a], (x, y, 1 - c), refs[n + a]) for a in range(n)]

    return plan


def _chip_swap_plan(n):
    def plan(refs):
        x, y, c = _position()
        copies = []
        for a in range(n):
            for j, (px, py) in enumerate(_other_chips(x, y)):
                copies.append((refs[a].at[2 * px + py], refs[n + a].at[j], (px, py, c), refs[n + a].at[j]))
        return copies

    return plan


def _pair_join_plan(axes):
    def plan(refs):
        x, y, c = _position()
        copies = []
        for a, axis in enumerate(axes):
            mine = _half(refs[a], c, axis)
            copies.append((mine, mine, (x, y, 1 - c), _half(refs[a], 1 - c, axis)))
        return copies

    return plan


def _pair_add_call(gs, gots, pos, name, axes):
    n = len(gs)

    def body(pos_r, *refs):
        for g_r, got_r, o_r in zip(refs[:n], refs[n:2 * n], refs[2 * n:]):
            o_r[...] = (g_r[...] + got_r[...]).astype(o_r.dtype)

    def mine(axis):
        return (lambda j, p: (j, p[1], 0)) if axis == ROWS else (lambda j, p: (j, 0, p[1]))

    blocks = [(None,) + got.shape[1:] for got in gots]
    return pl.pallas_call(
        body, name=name,
        grid_spec=pltpu.PrefetchScalarGridSpec(
            num_scalar_prefetch=1, grid=(N_CHIPS,),
            in_specs=[pl.BlockSpec(blk, mine(axis)) for blk, axis in zip(blocks, axes)]
            + [pl.BlockSpec(blk, lambda j, p: (j, 0, 0)) for blk in blocks],
            out_specs=[pl.BlockSpec(blk, lambda j, p: (j, 0, 0)) for blk in blocks]),
        out_shape=[jax.ShapeDtypeStruct(got.shape, COMM_DTYPE) for got in gots],
        compiler_params=_params(("arbitrary",), VMEM_BIG),
    )(pos, *gs, *gots)


def _chip_add_call(hsums, gots, pos, name, axes):
    n = len(hsums)
    steps = 2

    def body(pos_r, *refs):
        for own_r, got_r, o_r in zip(refs[:n], refs[n:2 * n], refs[2 * n:]):
            acc = own_r[...].astype(F32)
            for j in range(3):
                acc = acc + got_r[j].astype(F32)
            o_r[...] = acc

    in_specs, got_specs, out_specs, out_shape = [], [], [], []
    for h, axis in zip(hsums, axes):
        if axis == ROWS:
            rows, cols = h.shape[1] // steps, h.shape[2]
            in_specs.append(pl.BlockSpec((None, rows, cols), lambda i, p: (p[0], i, 0)))
            got_specs.append(pl.BlockSpec((3, rows, cols), lambda i, p: (0, i, 0)))
            out_specs.append(pl.BlockSpec((rows, cols), lambda i, p: (p[1] * steps + i, 0)))
            out_shape.append(jax.ShapeDtypeStruct((2 * h.shape[1], cols), F32))
        else:
            rows, cols = h.shape[1], h.shape[2] // steps
            in_specs.append(pl.BlockSpec((None, rows, cols), lambda i, p: (p[0], 0, i)))
            got_specs.append(pl.BlockSpec((3, rows, cols), lambda i, p: (0, 0, i)))
            out_specs.append(pl.BlockSpec((rows, cols), lambda i, p: (0, p[1] * steps + i)))
            out_shape.append(jax.ShapeDtypeStruct((rows, 2 * h.shape[2]), F32))
    return pl.pallas_call(
        body, name=name,
        grid_spec=pltpu.PrefetchScalarGridSpec(
            num_scalar_prefetch=1, grid=(steps,), in_specs=in_specs + got_specs, out_specs=out_specs),
        out_shape=out_shape,
        compiler_params=_params(("arbitrary",), VMEM_BIG),
    )(pos, *hsums, *gots)


SMALL_NAMES = ("norm_mix_pre", "norm_mix_post", "norm_mlp_pre", "norm_mlp_post", "b_gate_fwd", "b_gate_bwd",
               "gla_norm", "swa_sink", "rel_bias")


N_DEVICES = 8


def _small_pack_call(grads, extras):
    operands = list(grads) + list(extras)

    def body(*refs):
        g_refs, (all_a, all_b) = refs[:len(operands)], refs[len(operands):]
        x, y, c = _position()
        me = 4 * x + 2 * y + c
        all_a[me] = jnp.zeros(all_a.shape[1:], F32)
        all_b[me] = jnp.zeros(all_b.shape[1:], F32)
        for i in range(4):
            all_a[me, i:i + 1, :] = g_refs[i][...]
        all_a[me, 4:5, 0:256] = g_refs[4][...]
        all_a[me, 5:6, 0:256] = g_refs[5][...]
        all_a[me, 6:7, 0:128] = g_refs[6][...]
        all_a[me, 7:8, 0:128] = g_refs[7][...]
        all_a[me, 7:8, 128:256] = g_refs[11][...]
        all_b[me, 0:32, 0:128] = g_refs[8][...]
        all_b[me, 32:48, :] = g_refs[9][...]
        all_b[me, 48:64, :] = g_refs[10][...]

    out_shape = [jax.ShapeDtypeStruct((N_DEVICES, 8, D_MODEL), F32), jax.ShapeDtypeStruct((N_DEVICES, 64, 256), F32)]
    return pl.pallas_call(
        body, name="small_pack",
        in_specs=[_whole_spec(a.shape) for a in operands], out_specs=[_whole_spec(s.shape) for s in out_shape],
        out_shape=out_shape,
    )(*operands)


def _everyone_plan(n):
    def plan(refs):
        x, y, c = _position()
        copies = []
        for k in range(1, N_DEVICES):
            px = 1 - x if (k >> 2) & 1 else x
            py = 1 - y if (k >> 1) & 1 else y
            pc = 1 - c if k & 1 else c
            for a in range(n):
                mine = refs[a].at[4 * x + 2 * y + c]
                copies.append((mine, mine, (px, py, pc), refs[a].at[4 * px + 2 * py + pc]))
        return copies

    return plan


def _small_adamw_call(all_a, all_b, params):
    n_small = len(SMALL_NAMES)
    wmv = [t for p in params for t in p]
    shapes = [p[0].shape for p in params]

    def body(*refs):
        all_a, all_b = refs[:2]
        wmv_refs = refs[2:2 + 3 * n_small]
        out_refs = refs[2 + 3 * n_small:]
        sum_a, sum_b = all_a[0], all_b[0]
        for d in range(1, N_DEVICES):
            sum_a = sum_a + all_a[d]
            sum_b = sum_b + all_b[d]
        gsum = [sum_a[0:1], sum_a[1:2], sum_a[2:3], sum_a[3:4], sum_a[4:5, 0:256], sum_a[5:6, 0:256],
                sum_a[6:7, 0:128], sum_a[7:8, 0:SWA_Q_HEADS], sum_b[0:32, 0:SWA_Q_HEADS]]
        for i in range(n_small):
            w_r, m_r, v_r = wmv_refs[3 * i:3 * i + 3]
            delta, new_m, new_v = _adamw_math(w_r[...], gsum[i], m_r[...], v_r[...])
            out_refs[4 * i][...] = gsum[i]
            out_refs[4 * i + 1][...] = delta
            out_refs[4 * i + 2][...] = new_m
            out_refs[4 * i + 3][...] = new_v
        out_refs[4 * n_small][...] = sum_b[32:48]
        out_refs[4 * n_small + 1][...] = sum_b[48:64]
        out_refs[4 * n_small + 2][...] = sum_a[7:8, 128:256]

    out_shape = [jax.ShapeDtypeStruct(s, F32) for s in shapes for _ in range(4)]
    out_shape += [jax.ShapeDtypeStruct((GLA_GATE_RANK, 256), F32)] * 2 + [jax.ShapeDtypeStruct((1, 128), F32)]
    out = pl.pallas_call(
        body, name="small_adamw",
        in_specs=[_whole_spec(a.shape) for a in [all_a, all_b] + wmv],
        out_specs=[_whole_spec(s.shape) for s in out_shape],
        out_shape=out_shape,
    )(all_a, all_b, *wmv)
    per_name = [tuple(out[4 * i:4 * i + 4]) for i in range(n_small)]
    return per_name, out[4 * n_small], out[4 * n_small + 1], out[4 * n_small + 2]


def _pad_gate(w, first_row):
    return jnp.pad(w, ((first_row, 128 - GLA_GATE_RANK - first_row), (0, 0)))


def _own_slot(shard, chip):
    zone = lax.empty((N_CHIPS,) + shard.shape, shard.dtype)
    return lax.dynamic_update_slice(zone, shard[None], (chip,) + (0,) * shard.ndim)


def _reduce_to_owners(grads, axes, pos, tag, overlap):
    n = len(grads)

    def half_shape(g, axis):
        return (N_CHIPS, g.shape[1] // 2, g.shape[2]) if axis == ROWS else (N_CHIPS, g.shape[1], g.shape[2] // 2)

    lands = [lax.empty(half_shape(g, axis), F32) for g, axis in zip(grads, axes)]
    handle, token = _split_start(tag + "_pair_start", list(grads) + lands, n, _pair_swap_plan(axes), PAIR_PEERS)
    got = _split_wait(tag + "_pair_wait", handle, n, _pair_swap_plan(axes), overlap[0](token))
    sums = list(_pair_add_call(got[:n], got[n:], pos, tag + "_pair_add", axes))
    lands = [lax.empty((3,) + s.shape[1:], s.dtype) for s in sums]
    handle, token = _split_start(tag + "_chip_start", sums + lands, 3 * n, _chip_swap_plan(n), CHIP_PEERS)
    got = _split_wait(tag + "_chip_wait", handle, 3 * n, _chip_swap_plan(n), overlap[1](token))
    halves = list(_chip_add_call(got[:n], got[n:], pos, tag + "_chip_add", axes))
    handle, token = _split_start(tag + "_join_start", halves, n, _pair_join_plan(axes), PAIR_PEERS)
    return _split_wait(tag + "_join_wait", handle, n, _pair_join_plan(axes), overlap[2](token))


def kernel(x, norm_mix_pre, w_in, w_gate_up_fwd, b_gate_fwd, w_gate_up_bwd, b_gate_bwd, gla_norm, swa_sink, rel_bias, w_out, norm_mix_post, norm_mlp_pre, w_up, w_down, norm_mlp_post, loss_target, m_norm_mix_pre, m_w_in, m_w_gate_up_fwd, m_b_gate_fwd, m_w_gate_up_bwd, m_b_gate_bwd, m_gla_norm, m_swa_sink, m_rel_bias, m_w_out, m_norm_mix_post, m_norm_mlp_pre, m_w_up, m_w_down, m_norm_mlp_post, v_norm_mix_pre, v_w_in, v_w_gate_up_fwd, v_b_gate_fwd, v_w_gate_up_bwd, v_b_gate_bwd, v_gla_norm, v_swa_sink, v_rel_bias, v_w_out, v_norm_mix_post, v_norm_mlp_pre, v_w_up, v_w_down, v_norm_mlp_post):
    given = dict(locals())
    cx, cy, cc = _position()
    chip = (2 * cx + cy).astype(jnp.int32)
    pos = jnp.stack([chip, cc.astype(jnp.int32)])
    seq, tgt = x[0], loss_target[0]

    gates = jnp.concatenate([w_gate_up_fwd[0], w_gate_up_bwd[0]], axis=0).astype(COMM_DTYPE)
    all_in, all_gates = _first_gather_call([w_in[0].T.astype(COMM_DTYPE), gates], [COLS, ROWS], [True, False])
    rest = [w_out[0].astype(COMM_DTYPE), jnp.stack([w_up[0], w_down[0]]).astype(COMM_DTYPE)]
    stage_one, stage_two = _gather_plans([None, ROWS])
    handle, token = _split_start("gather_chip_start", rest + [_own_slot(s, chip) for s in rest] + [all_gates], 6,
                                 stage_one, CHIP_PEERS)

    w_in_t = _mx(all_in.reshape(IN_COLS, D_MODEL))
    gates_full = jnp.concatenate([all_gates[j] for j in range(N_CHIPS)], axis=1)
    wgf_p = _mx(_pad_gate(gates_full[:GLA_GATE_RANK], 0))
    wgb_p = _mx(_pad_gate(gates_full[GLA_GATE_RANK:], GLA_GATE_RANK))
    bf_p, bb_p = b_gate_fwd, b_gate_bwd
    buckets = jnp.asarray(_band_buckets())
    sink1 = swa_sink.reshape(SWA_Q_HEADS)

    qa, ka, va, ga, qs, ks, vs, za = _proj_call(seq, norm_mix_pre, w_in_t, dep=token)
    halo = ((SWA_BLOCK, SWA_BLOCK), (0, 0))
    ks_p, vs_p = jnp.pad(ks, halo), jnp.pad(vs, halo)
    o_f, o_b, s_f, s_b = _gla_fwd_call(qa, ka, va, za, wgf_p, bf_p, wgb_p, bb_p)
    bias = _bias_call(rel_bias, buckets)
    o_s = _swa_fwd_call(qs, ks_p, vs_p, bias, sink1)
    arrays = _split_wait("gather_chip_wait", handle, 6, stage_one, o_s)
    w_out_full = _mx(arrays[2].reshape(N_CHIPS * R_OUT, D_MODEL))
    handle, token = _split_start("gather_pair_start", list(arrays), 3, stage_two, PAIR_PEERS)
    cat, mix, h1, n2 = _mix_call(o_f, o_b, ga, o_s, seq, gla_norm, w_out_full, norm_mix_post, norm_mlp_pre,
                                 dep=token)
    arrays = _split_wait("gather_pair_wait", handle, 3, stage_two, n2)
    w_ud = _mx(arrays[3])
    a, rz, dh2, dff, loss, d_post2 = _mlp_fwd_call(n2, h1, tgt, w_ud, norm_mlp_post)

    dz, dn2 = _mlp_bwd_call(dff, rz, w_ud)
    dw_down, dw_up4 = _mlp_wgrad_call(a, dff, n2, dz)
    dh1, do, dga, dos, dw_out, d_pre2, d_post, d_gn = _mix_bwd_call(
        dn2, dh2, h1, mix, cat, o_f, o_b, ga, gla_norm, norm_mix_post, norm_mlp_pre, w_out_full)
    done = {}

    def swa_backward(tok):
        done["swa"] = _swa_bwd_call(qs, ks_p, vs_p, bias, sink1, dos, dep=tok)
        return done["swa"][0]

    def gla_in_backward(tok):
        done["gla"] = _gla_bwd_call(qa, ka, va, za, do, s_f, s_b, wgf_p, bf_p, wgb_p, bb_p, dep=tok)
        dqf, dkf, dvf, dzf, _, _, dqb, dkb, dvb, dzb, _, _ = done["gla"]
        dqs, dks_p, dvs_p, _, _ = done["swa"]
        done["in"] = _in_bwd_call(
            seq, dh1, norm_mix_pre, w_in_t,
            pairs=[(_side_by_side(T_QA), (dqf, dqb)), (_side_by_side(T_KA), (dkf, dkb)), (T_VA, (dvf, dvb)),
                   (T_ZA, (dzf, dzb))],
            singles=[(T_GA, dga), (_side_by_side(T_QS), dqs)], halos=[(T_KS, dks_p), (T_VS, dvs_p)])
        return done["in"][0]

    def bias_backward(tok):
        done["rel"] = _relbias_call(done["swa"][3], done["swa"][4], buckets, dep=tok)
        return done["rel"][0]

    g_up, g_down, g_out = _reduce_to_owners(
        [dw_up4, dw_down.reshape(N_CHIPS, R_DOWN, D_MODEL), dw_out.reshape(N_CHIPS, R_OUT, D_MODEL)],
        [ROWS, ROWS, ROWS], pos, "mlp", [swa_backward, gla_in_backward, bias_backward])
    dx, dw_in_t, d_pre = done["in"]
    dwf, dbf, dwb, dbb = done["gla"][4], done["gla"][5], done["gla"][10], done["gla"][11]
    drel, dsink = done["rel"]

    small_grads = [d_pre, d_post, d_pre2, d_post2, dbf, dbb, d_gn, dsink, drel]
    gate_grads = [dwf[:GLA_GATE_RANK], dwb[GLA_GATE_RANK:2 * GLA_GATE_RANK]]
    small_params = [(given[n], given["m_" + n], given["v_" + n]) for n in SMALL_NAMES]
    upd = {}

    everyone = _everyone_plan(2)
    small_handle, small_token = _split_start(
        "small_start", list(_small_pack_call(small_grads, gate_grads + [loss])), 2 * (N_DEVICES - 1), everyone)

    def update_out(tok):
        upd["w_out"] = tuple(_adamw_call(w_out[0], g_out, m_w_out[0], v_w_out[0], "adamw_w_out",
                                         dep=tok + small_token))
        return upd["w_out"][1]

    def update_mlp(tok):
        upd["w_up"] = tuple(_adamw_call(w_up[0], g_up, m_w_up[0], v_w_up[0], "adamw_w_up", dep=tok))
        upd["w_down"] = tuple(
            _adamw_call(w_down[0], g_down, m_w_down[0], v_w_down[0], "adamw_w_down", dep=upd["w_up"][1]))
        all_a, all_b = _split_wait("small_wait", small_handle, 2 * (N_DEVICES - 1), everyone, upd["w_down"][1])
        per_name, done["gf_sum"], done["gb_sum"], upd["loss"] = _small_adamw_call(all_a, all_b, small_params)
        upd.update(dict(zip(SMALL_NAMES, per_name)))
        return per_name[0][1]

    def update_gates(tok):
        for name, total in (("w_gate_up_fwd", done["gf_sum"]), ("w_gate_up_bwd", done["gb_sum"])):
            g = lax.dynamic_slice(total, (0, chip * 64), (GLA_GATE_RANK, 64))
            upd[name] = tuple(_adamw_call(given[name][0], g, given["m_" + name][0], given["v_" + name][0],
                                          "adamw_" + name, dep=tok))
        return upd["w_gate_up_bwd"][1]

    (g_in_t,) = _reduce_to_owners([dw_in_t.reshape(N_CHIPS, R_IN, D_MODEL)], [COLS], pos, "in",
                                  [update_out, update_mlp, update_gates])
    upd["w_in"] = tuple(t.T for t in _adamw_call(w_in[0].T, g_in_t, m_w_in[0].T, v_w_in[0].T, "adamw_w_in"))

    big = ("w_in", "w_gate_up_fwd", "w_gate_up_bwd", "w_out", "w_up", "w_down")
    names = ["norm_mix_pre", "w_in", "w_gate_up_fwd", "b_gate_fwd", "w_gate_up_bwd", "b_gate_bwd", "gla_norm",
             "swa_sink", "rel_bias", "w_out", "norm_mix_post", "norm_mlp_pre", "w_up", "w_down", "norm_mlp_post"]
    outs = [upd["loss"][0, 0], dx[None]]
    for kind in range(4):
        outs += [upd[n][kind][None] if n in big else upd[n][kind] for n in names]
    return tuple(outs)
```

```python
import math

import numpy as np
import jax
import jax.numpy as jnp
from jax import lax
from jax.experimental import pallas as pl
from jax.experimental.pallas import tpu as pltpu

F32 = jnp.float32
MXU_DTYPE = jnp.bfloat16
COMM_DTYPE = jnp.bfloat16

D_MODEL = 1024
D_FF = 4096
N_CHIPS = 4
GLA_HEADS = 4
GLA_CHUNK = 64
GLA_GATE_RANK = 16
GLA_GATE_NORM = 16.0
SWA_Q_HEADS = 8
SWA_KV_HEADS = 2
SWA_BLOCK = 128
REL_BUCKETS = 32
REL_MAX_DIST = 128
NORM_EPS = 1e-6
HEAD_PAD = 128

ADAM_LR = 0.001
ADAM_B1 = 0.9
ADAM_B2 = 0.999
ADAM_EPS = 1e-08
ADAM_WD = 0.01
ADAM_STEP = 10

OUT_PAD = 1024

R_IN, R_OUT, R_DOWN = 584, 256, 1024

VMEM_BIG = 56 * 1024 * 1024
MESH_ID = pl.DeviceIdType.MESH


def _mx(a):
    return a.astype(MXU_DTYPE)


def _dot(a, b):
    return jnp.dot(a, b, preferred_element_type=F32)


def _dot_nt(a, b):
    return lax.dot_general(a, b, (((1,), (1,)), ((), ())), preferred_element_type=F32)


def _dot_tn(a, b):
    return lax.dot_general(a, b, (((0,), (0,)), ((), ())), preferred_element_type=F32)


def _rms_r(x):
    return lax.rsqrt(jnp.mean(x * x, axis=-1, keepdims=True) + NORM_EPS)


def _rms_bwd(x, r, g, dy):
    xh = x * r
    gdy = dy * g
    dx = r * (gdy - xh * jnp.mean(gdy * xh, axis=-1, keepdims=True))
    return dx, jnp.sum(dy * xh, axis=0, keepdims=True)


def _low_half(rows):
    return lax.broadcasted_iota(jnp.int32, (rows, HEAD_PAD), 1) < 64


def _spread_heads(x):
    low = _low_half(x.shape[0])
    parts = []
    for p in range(x.shape[1] // HEAD_PAD):
        pair = x[:, HEAD_PAD * p:HEAD_PAD * (p + 1)]
        parts += [jnp.where(low, pair, 0.0), jnp.where(low, pltpu.roll(pair, 64, 1), 0.0)]
    return jnp.concatenate(parts, axis=1)


def _squeeze_heads(x):
    low = _low_half(x.shape[0])
    parts = []
    for p in range(x.shape[1] // (2 * HEAD_PAD)):
        even = x[:, 2 * HEAD_PAD * p:2 * HEAD_PAD * p + HEAD_PAD]
        odd = x[:, 2 * HEAD_PAD * p + HEAD_PAD:2 * HEAD_PAD * (p + 1)]
        parts.append(jnp.where(low, even, pltpu.roll(odd, 64, 1)))
    return parts[0] if len(parts) == 1 else jnp.concatenate(parts, axis=1)


def _params(sem=None, vmem=None):
    kw = {}
    if sem is not None:
        kw["dimension_semantics"] = sem
    if vmem is not None:
        kw["vmem_limit_bytes"] = vmem
    return pltpu.CompilerParams(**kw)


def _vmem_spec():
    return pl.BlockSpec(memory_space=pltpu.VMEM)


def _whole_spec(shape):
    return pl.BlockSpec(shape, lambda: (0,) * len(shape))


def _row_spec(tm, width):
    return pl.BlockSpec((tm, width), lambda i: (i, 0))


def _full_spec(shape):
    return pl.BlockSpec(shape, lambda i: (0,) * len(shape))


def _any_spec():
    return pl.BlockSpec(memory_space=pl.ANY)


def _after(body, n_in, dep):
    if dep is None:
        return body, [], []
    return (lambda *refs: body(*refs[:n_in], *refs[n_in + 1:])), [dep], [_any_spec()]


T_QA, T_KA, T_VA, T_GA = (0, 256, 4), (256, 256, 4), (512, 512, 0), (1024, 512, 0)
T_QS, T_KS, T_VS = (1568, 512, 8), (2080, 128, 2), (2208, 128, 2)
T_ZA = (1536, 128, 0)
ZA_COLS = 2 * GLA_GATE_RANK
IN_COLS = 2336


def _side_by_side(group):
    return group[0], group[1], 0


def _proj_call(x, g_pre, w_in_t, dep=None):
    L = x.shape[0]
    tm = min(1024, L)
    groups = [(T_QA, F32), (T_KA, F32), (T_VA, MXU_DTYPE), (T_GA, F32),
              (T_QS, MXU_DTYPE), (T_KS, MXU_DTYPE), (T_VS, MXU_DTYPE), (T_ZA, F32)]
    widths = [rows * (2 if heads else 1) for (_, rows, heads), _ in groups]

    def body(x_ref, g_ref, w_ref, *outs):
        xv = x_ref[...]
        u = _mx(xv * _rms_r(xv) * g_ref[...])
        for ref, (grp, _) in zip(outs, groups):
            first, rows, heads = grp
            val = _dot_nt(u, w_ref[first:first + rows, :])
            if heads:
                val = _spread_heads(val)
            if grp is T_ZA:
                val = jnp.where(lax.broadcasted_iota(jnp.int32, val.shape, 1) < ZA_COLS, val, 0.0)
            if grp is T_QS:
                val = val * 0.125
            ref[...] = val.astype(ref.dtype)

    body, extra, extra_specs = _after(body, 3, dep)
    return pl.pallas_call(
        body, name="proj_fwd", grid=(L // tm,),
        in_specs=[_row_spec(tm, D_MODEL), _full_spec((1, D_MODEL)), _vmem_spec()] + extra_specs,
        out_specs=[_row_spec(tm, w) for w in widths],
        out_shape=[jax.ShapeDtypeStruct((L, w), dt) for w, (_, dt) in zip(widths, groups)],
        compiler_params=_params(("arbitrary",), VMEM_BIG),
    )(x, g_pre, w_in_t, *extra)


def _tri_masks():
    row = lax.broadcasted_iota(jnp.int32, (GLA_CHUNK, GLA_CHUNK), 0)
    col = lax.broadcasted_iota(jnp.int32, (GLA_CHUNK, GLA_CHUNK), 1)
    return row >= col, row <= col


def _chunk_sums(tri_m, x):
    hi = _mx(x)
    rest = x - hi.astype(F32)
    mid = _mx(rest)
    lo = _mx(rest - mid.astype(F32))
    return _dot(tri_m, hi) + _dot(tri_m, mid) + _dot(tri_m, lo)


def _gla_block_pre(q_r, k_r, z_r, w_r, b_r, rev, nc, qd_s, ki_s, ks_s, dec_s, keep=None):
    tri_f, tri_b = _tri_masks()
    tri_m = _mx((tri_b if rev else tri_f).astype(F32))
    g = _dot(_mx(z_r[...]), w_r[...]) + b_r[...]
    la = (jnp.minimum(g, 0.0) - jnp.log(1.0 + jnp.exp(-jnp.abs(g)))) * (1.0 / GLA_GATE_NORM)
    sums, lasts = [], []
    for c in range(nc):
        b_c = _chunk_sums(tri_m, la[GLA_CHUNK * c:GLA_CHUNK * (c + 1)])
        blast = b_c[0:1] if rev else b_c[GLA_CHUNK - 1:GLA_CHUNK]
        dec_s[c] = _spread_heads(jnp.exp(blast))
        sums.append(b_c)
        lasts.append(jnp.broadcast_to(blast, b_c.shape))
    b = jnp.concatenate(sums, axis=0)
    eb = jnp.exp(b)
    enb = jnp.exp(-b)
    elb = jnp.exp(jnp.concatenate(lasts, axis=0) - b)
    q, k = _squeeze_heads(q_r[...]), _squeeze_heads(k_r[...])
    qd_s[...] = _spread_heads(q * 0.125 * eb).astype(qd_s.dtype)
    ki_s[...] = _spread_heads(k * enb).astype(ki_s.dtype)
    ks_s[...] = _spread_heads(k * elb).astype(ks_s.dtype)
    if keep is not None:
        keep[0][...] = g
        for ref, val in zip(keep[1:], (eb, enb, elb)):
            ref[...] = _spread_heads(val)


def _gla_fwd_call(qa, ka, va, za, wgf, bgf, wgb, bgb):
    L = qa.shape[0]
    br = min(512, L)
    nb, nc, n_chunks = L // br, br // GLA_CHUNK, L // GLA_CHUNK
    hw = GLA_HEADS * HEAD_PAD

    def body(qaf, kaf, vaf, zaf, qab, kab, vab, zab, wgf_r, bgf_r, wgb_r, bgb_r,
             of_r, ob_r, sf_r, sb_r, st_f, st_b, pre_f, pre_b):
        @pl.when(pl.program_id(0) == 0)
        def _():
            st_f[...] = jnp.zeros_like(st_f)
            st_b[...] = jnp.zeros_like(st_b)

        _gla_block_pre(qaf, kaf, zaf, wgf_r, bgf_r, False, nc, *pre_f)
        _gla_block_pre(qab, kab, zab, wgb_r, bgb_r, True, nc, *pre_b)
        tri_f, tri_b = _tri_masks()

        def one(tri, pre, v_r, o_r, s_r, st, ci):
            qd_s, ki_s, ks_s, dec_s = pre
            rows = pl.ds(pl.multiple_of(ci * GLA_CHUNK, GLA_CHUNK), GLA_CHUNK)
            dec = dec_s[ci]
            heads = range(GLA_HEADS)
            lanes = [slice(HEAD_PAD * h, HEAD_PAD * (h + 1)) for h in heads]
            qd = [qd_s[rows, sl] for sl in lanes]
            v = [v_r[rows, sl] for sl in lanes]
            s_t = [st[h] for h in heads]
            a = [_dot_nt(qd[h], ki_s[rows, lanes[h]]) for h in heads]
            carried = [_dot_nt(qd[h], _mx(s_t[h])) for h in heads]
            grown = [_dot_tn(v[h], ks_s[rows, lanes[h]]) for h in heads]
            a = [_mx(jnp.where(tri, a[h], 0.0)) for h in heads]
            inner = [_dot(a[h], v[h]) for h in heads]
            for h in heads:
                s_r[ci, h] = s_t[h].astype(s_r.dtype)
                o_r[rows, lanes[h]] = inner[h] + carried[h]
                st[h] = s_t[h] * dec[:, lanes[h]] + grown[h]

        def loop(t, carry):
            one(tri_f, pre_f, vaf, of_r, sf_r, st_f, t)
            one(tri_b, pre_b, vab, ob_r, sb_r, st_b, nc - 1 - t)
            return carry

        lax.fori_loop(0, nc, loop, 0, unroll=True)

    fwd = lambda i: (i, 0)
    bwd = lambda i: (nb - 1 - i, 0)
    ins = lambda m: [pl.BlockSpec((br, hw), m), pl.BlockSpec((br, hw), m),
                     pl.BlockSpec((br, hw), m), pl.BlockSpec((br, 128), m)]
    wspecs = [_full_spec((128, hw // 2)), _full_spec((1, hw // 2))] * 2
    s_shape = (nc, GLA_HEADS, HEAD_PAD, HEAD_PAD)
    pre_scratch = [pltpu.VMEM((br, hw), MXU_DTYPE)] * 3 + [pltpu.VMEM((nc, 1, hw), F32)]
    return pl.pallas_call(
        body, name="gla_fwd", grid=(nb,),
        in_specs=ins(fwd) + ins(bwd) + wspecs,
        out_specs=[pl.BlockSpec((br, hw), fwd), pl.BlockSpec((br, hw), bwd),
                   pl.BlockSpec(s_shape, lambda i: (i, 0, 0, 0)),
                   pl.BlockSpec(s_shape, lambda i: (nb - 1 - i, 0, 0, 0))],
        out_shape=[jax.ShapeDtypeStruct((L, hw), F32), jax.ShapeDtypeStruct((L, hw), F32),
                   jax.ShapeDtypeStruct((n_chunks,) + s_shape[1:], MXU_DTYPE),
                   jax.ShapeDtypeStruct((n_chunks,) + s_shape[1:], MXU_DTYPE)],
        scratch_shapes=[pltpu.VMEM(s_shape[1:], F32), pltpu.VMEM(s_shape[1:], F32), pre_scratch, pre_scratch],
        compiler_params=_params(("arbitrary",), VMEM_BIG),
    )(qa, ka, va, za, qa, ka, va, za, wgf, bgf, wgb, bgb)


def _gla_bwd_call(qa, ka, va, za, do, sf, sb, wgf, bgf, wgb, bgb, dep=None):
    L = qa.shape[0]
    br = min(512, L)
    nb, nc = L // br, br // GLA_CHUNK
    hw = GLA_HEADS * HEAD_PAD

    def body(qaf, kaf, vaf, zaf, dof, sf_r, qab, kab, vab, zab, dob, sb_r, wgf_r, bgf_r, wgb_r, bgb_r,
             dqf, dkf, dvf, dzf, dwf, dbf, dqb, dkb, dvb, dzb, dwb, dbb, gt_f, gt_b, pre_f, pre_b):
        @pl.when(pl.program_id(0) == 0)
        def _():
            for ref in (gt_f, gt_b, dwf, dbf, dwb, dbb):
                ref[...] = jnp.zeros_like(ref)

        _gla_block_pre(qaf, kaf, zaf, wgf_r, bgf_r, False, nc, *pre_f[:4], keep=pre_f[4:8])
        _gla_block_pre(qab, kab, zab, wgb_r, bgb_r, True, nc, *pre_b[:4], keep=pre_b[4:8])
        tri_f, tri_b = _tri_masks()
        row_w = lax.broadcasted_iota(jnp.int32, (GLA_CHUNK, HEAD_PAD), 0)

        def one(rev, pre, q_r, k_r, v_r, do_r, s_r, dq_r, dk_r, dv_r, gt, ci):
            qd_s, ki_s, ks_s, dec_s, _, eb_s, enb_s, elb_s, db_s = pre
            tri = tri_b if rev else tri_f
            last_row = 0 if rev else GLA_CHUNK - 1
            rows = pl.ds(pl.multiple_of(ci * GLA_CHUNK, GLA_CHUNK), GLA_CHUNK)
            dec = dec_s[ci]
            heads = range(GLA_HEADS)
            lanes = [slice(HEAD_PAD * h, HEAD_PAD * (h + 1)) for h in heads]
            qd = [qd_s[rows, sl] for sl in lanes]
            ki = [ki_s[rows, sl] for sl in lanes]
            ks = [ks_s[rows, sl] for sl in lanes]
            v = [v_r[rows, sl] for sl in lanes]
            do_h = [_mx(do_r[rows, sl]) for sl in lanes]
            s_t = [s_r[ci, h] for h in heads]
            g_t = [gt[h] for h in heads]
            g_m = [_mx(g_t[h]) for h in heads]
            a = [_dot_nt(qd[h], ki[h]) for h in heads]
            da = [_dot_nt(do_h[h], v[h]) for h in heads]
            dv_carried = [_dot_nt(ks[h], g_m[h]) for h in heads]
            dqd_carried = [_dot(do_h[h], _mx(s_t[h])) for h in heads]
            dks = [_dot(v[h], g_m[h]) for h in heads]
            g_grown = [_dot_tn(do_h[h], qd[h]) for h in heads]
            a = [_mx(jnp.where(tri, a[h], 0.0)) for h in heads]
            da = [_mx(jnp.where(tri, da[h], 0.0)) for h in heads]
            dv_inner = [_dot_tn(a[h], do_h[h]) for h in heads]
            dqd_inner = [_dot(da[h], ki[h]) for h in heads]
            dki = [_dot_tn(da[h], qd[h]) for h in heads]
            dq, dk = [], []
            for h in heads:
                sl = lanes[h]
                dv_r[rows, sl] = (dv_inner[h] + dv_carried[h]).astype(dv_r.dtype)
                ddec = jnp.sum(g_t[h] * s_t[h].astype(F32), axis=0, keepdims=True)
                gt[h] = g_t[h] * dec[:, sl] + g_grown[h]
                dq.append((dqd_inner[h] + dqd_carried[h]) * eb_s[rows, sl] * 0.125)
                dk_state = dks[h] * elb_s[rows, sl]
                dk.append(dki[h] * enb_s[rows, sl] + dk_state)
                k = k_r[rows, sl]
                dblast = jnp.sum(dk_state * k, axis=0, keepdims=True) + dec[:, sl] * ddec
                db_s[rows, sl] = q_r[rows, sl] * dq[h] - k * dk[h] + jnp.where(row_w == last_row, dblast, 0.0)
            low = _low_half(GLA_CHUNK)
            for pair in range(GLA_HEADS // 2):
                psl = slice(HEAD_PAD * pair, HEAD_PAD * (pair + 1))
                for ref, val in ((dq_r, dq), (dk_r, dk)):
                    both = jnp.where(low, val[2 * pair], pltpu.roll(val[2 * pair + 1], 64, 1))
                    ref[rows, psl] = both.astype(ref.dtype)

        def loop(t, carry):
            one(False, pre_f, qaf, kaf, vaf, dof, sf_r, dqf, dkf, dvf, gt_f, nc - 1 - t)
            one(True, pre_b, qab, kab, vab, dob, sb_r, dqb, dkb, dvb, gt_b, t)
            return carry

        lax.fori_loop(0, nc, loop, 0, unroll=True)

        def gate_grads(rev, pre, z_r, w_r, dz_r, dw_r, dbias_r):
            g_s, db_s = pre[4], pre[8]
            back_m = _mx((tri_f if rev else tri_b).astype(F32))
            db = _squeeze_heads(db_s[...])
            dla = jnp.concatenate([_chunk_sums(back_m, db[GLA_CHUNK * c:GLA_CHUNK * (c + 1)]) for c in range(nc)],
                                  axis=0)
            dg = dla * (1.0 / GLA_GATE_NORM) * (1.0 / (1.0 + jnp.exp(g_s[...])))
            dg_m = _mx(dg)
            dz_r[...] = _dot_nt(dg_m, w_r[...])
            dw_r[...] += _dot_tn(_mx(z_r[...]), dg_m)
            dbias_r[...] += jnp.sum(dg, axis=0, keepdims=True)

        gate_grads(False, pre_f, zaf, wgf_r, dzf, dwf, dbf)
        gate_grads(True, pre_b, zab, wgb_r, dzb, dwb, dbb)

    last_first = lambda i: (nb - 1 - i, 0)
    first_last = lambda i: (i, 0)
    s_shape = (nc, GLA_HEADS, HEAD_PAD, HEAD_PAD)

    def ins(m):
        return [pl.BlockSpec((br, hw), m), pl.BlockSpec((br, hw), m), pl.BlockSpec((br, hw), m),
                pl.BlockSpec((br, 128), m), pl.BlockSpec((br, hw), m),
                pl.BlockSpec(s_shape, lambda i: m(i) + (0, 0))]

    def outs(m):
        return [pl.BlockSpec((br, hw // 2), m), pl.BlockSpec((br, hw // 2), m), pl.BlockSpec((br, hw), m),
                pl.BlockSpec((br, 128), m), _full_spec((128, hw // 2)), _full_spec((1, hw // 2))]

    out_shape = [jax.ShapeDtypeStruct((L, hw // 2), MXU_DTYPE)] * 2 + [
        jax.ShapeDtypeStruct((L, hw), MXU_DTYPE),
        jax.ShapeDtypeStruct((L, 128), F32), jax.ShapeDtypeStruct((128, hw // 2), F32),
        jax.ShapeDtypeStruct((1, hw // 2), F32)]
    wspecs = [_full_spec((128, hw // 2)), _full_spec((1, hw // 2))] * 2
    body, extra, extra_specs = _after(body, 16, dep)
    pre_scratch = ([pltpu.VMEM((br, hw), MXU_DTYPE)] * 3 + [pltpu.VMEM((nc, 1, hw), F32)]
                   + [pltpu.VMEM((br, hw // 2), F32)] + [pltpu.VMEM((br, hw), F32)] * 4)
    return pl.pallas_call(
        body, name="gla_bwd", grid=(nb,),
        in_specs=ins(last_first) + ins(first_last) + wspecs + extra_specs,
        out_specs=outs(last_first) + outs(first_last),
        out_shape=out_shape + out_shape,
        scratch_shapes=[pltpu.VMEM(s_shape[1:], F32), pltpu.VMEM(s_shape[1:], F32), pre_scratch, pre_scratch],
        compiler_params=_params(("arbitrary",), VMEM_BIG),
    )(qa, ka, va, za, do, sf, qa, ka, va, za, do, sb, wgf, bgf, wgb, bgb, *extra)


def _t5_buckets(rel):
    nb = REL_BUCKETS // 2
    ret = (rel > 0).astype(np.int32) * nb
    n = np.abs(rel)
    max_exact = nb // 2
    large = max_exact + (np.log(np.maximum(n, 1).astype(np.float32) / max_exact)
                         / math.log(REL_MAX_DIST / max_exact) * (nb - max_exact)).astype(np.int32)
    large = np.minimum(large, nb - 1)
    return ret + np.where(n < max_exact, n, large)


SWA_GROUP = SWA_Q_HEADS // SWA_KV_HEADS
SWA_SPAN = 3 * SWA_BLOCK
SWA_GROUP_LANES = SWA_GROUP * SWA_BLOCK


def _band_buckets():
    s = np.arange(SWA_SPAN)[:, None]
    c = np.arange(SWA_BLOCK)[None, :]
    return _t5_buckets(s - SWA_BLOCK - c).astype(np.int32)


def _swa_valid(n, seq_len):
    key_pos = (n - 1) * SWA_BLOCK + lax.broadcasted_iota(jnp.int32, (SWA_SPAN, 1), 0)
    return (key_pos >= 0) & (key_pos < seq_len)


def _swa_sink_row(sink_r, kv):
    lane = lax.broadcasted_iota(jnp.int32, (1, SWA_GROUP_LANES), 1)
    row = jnp.full((1, SWA_GROUP_LANES), sink_r[kv * SWA_GROUP], F32)
    for g in range(1, SWA_GROUP):
        row = jnp.where(lane >= g * SWA_BLOCK, sink_r[kv * SWA_GROUP + g], row)
    return row


SWA_STEP_BLOCKS = 8


def _swa_group(ref, kv, rows):
    first = kv * SWA_GROUP
    return jnp.concatenate([ref[rows, HEAD_PAD * h:HEAD_PAD * (h + 1)] for h in range(first, first + SWA_GROUP)],
                           axis=0)


def _swa_softmax(scores, bias_t, sink_row, valid):
    st = jnp.where(valid, scores + bias_t, -1e30)
    m = jnp.maximum(jnp.max(st, axis=0, keepdims=True), sink_row)
    p = jnp.exp(st - m)
    e_sink = jnp.exp(sink_row - m)
    inv = 1.0 / (jnp.sum(p, axis=0, keepdims=True) + e_sink)
    return p * inv, e_sink * inv


def _swa_fwd_call(qs, ks, vs, bias, sink, dep=None):
    L = qs.shape[0]

    def block(n, rows, q_r, k_r, v_r, bias_r, sink_r, o_r):
        span = pl.ds(pl.multiple_of(n * SWA_BLOCK, SWA_BLOCK), SWA_SPAN)
        valid = _swa_valid(n, L)
        groups = range(SWA_KV_HEADS)
        lanes = [slice(HEAD_PAD * kv, HEAD_PAD * (kv + 1)) for kv in groups]
        scores = [_dot_nt(k_r[span, lanes[kv]], _swa_group(q_r, kv, rows)) for kv in groups]
        probs = [_swa_softmax(scores[kv], bias_r[kv], _swa_sink_row(sink_r, kv), valid)[0] for kv in groups]
        low = _low_half(SWA_BLOCK)
        for kv in groups:
            og = _dot_tn(_mx(probs[kv]), v_r[span, lanes[kv]])
            for pair in range(SWA_GROUP // 2):
                even = og[2 * SWA_BLOCK * pair:2 * SWA_BLOCK * pair + SWA_BLOCK]
                odd = og[2 * SWA_BLOCK * pair + SWA_BLOCK:2 * SWA_BLOCK * (pair + 1)]
                first = HEAD_PAD * (kv * SWA_GROUP // 2 + pair)
                o_r[rows, first:first + HEAD_PAD] = jnp.where(low, even, pltpu.roll(odd, 64, 1)).astype(o_r.dtype)

    def body(*refs):
        for j in range(SWA_STEP_BLOCKS):
            block(SWA_STEP_BLOCKS * pl.program_id(0) + j, slice(SWA_BLOCK * j, SWA_BLOCK * (j + 1)), *refs)

    qw = SWA_Q_HEADS * HEAD_PAD
    tm = SWA_STEP_BLOCKS * SWA_BLOCK
    body, extra, extra_specs = _after(body, 5, dep)
    return pl.pallas_call(
        body, name="swa_fwd", grid=(L // tm,),
        in_specs=[_row_spec(tm, qw), _vmem_spec(), _vmem_spec(), _vmem_spec(),
                  pl.BlockSpec(memory_space=pltpu.SMEM)] + extra_specs,
        out_specs=_row_spec(tm, qw // 2),
        out_shape=jax.ShapeDtypeStruct((L, qw // 2), MXU_DTYPE),
        compiler_params=_params(("arbitrary",), VMEM_BIG),
    )(qs, ks, vs, bias, sink, *extra)


def _swa_bwd_call(qs, ks, vs, bias, sink, do, dep=None):
    L = qs.shape[0]
    qw = SWA_Q_HEADS * HEAD_PAD
    kw = SWA_KV_HEADS * HEAD_PAD

    def body(*refs):
        dk_r, dv_r, dbias_r, dsink_r = refs[7:]

        @pl.when(pl.program_id(0) == 0)
        def _():
            for ref in (dk_r, dv_r, dbias_r, dsink_r):
                ref[...] = jnp.zeros_like(ref)

        for j in range(SWA_STEP_BLOCKS):
            block(SWA_STEP_BLOCKS * pl.program_id(0) + j, slice(SWA_BLOCK * j, SWA_BLOCK * (j + 1)), *refs)

    def block(n, rows, q_r, k_r, v_r, bias_r, sink_r, do_r, dq_r, dk_r, dv_r, dbias_r, dsink_r):
        span = pl.ds(pl.multiple_of(n * SWA_BLOCK, SWA_BLOCK), SWA_SPAN)
        valid = _swa_valid(n, L)
        groups = range(SWA_KV_HEADS)
        lanes = [slice(HEAD_PAD * kv, HEAD_PAD * (kv + 1)) for kv in groups]
        kk = [k_r[span, sl] for sl in lanes]
        vv = [v_r[span, sl] for sl in lanes]
        qg = [_swa_group(q_r, kv, rows) for kv in groups]
        dog = [_swa_group(do_r, kv, rows) for kv in groups]
        scores = [_dot_nt(kk[kv], qg[kv]) for kv in groups]
        dp = [_dot_nt(vv[kv], dog[kv]) for kv in groups]
        probs = [_swa_softmax(scores[kv], bias_r[kv], _swa_sink_row(sink_r, kv), valid) for kv in groups]
        ds_m, pn_m = [], []
        for kv in groups:
            pn, p_sink = probs[kv]
            delta = jnp.sum(pn * dp[kv], axis=0, keepdims=True)
            ds = pn * (dp[kv] - delta)
            dsink_r[kv] -= p_sink * delta
            dbias_r[kv] += ds
            ds_m.append(_mx(ds))
            pn_m.append(_mx(pn))
        dqg = [_dot_tn(ds_m[kv], kk[kv]) * 0.125 for kv in groups]
        dkk = [_dot(ds_m[kv], qg[kv]) for kv in groups]
        dvv = [_dot(pn_m[kv], dog[kv]) for kv in groups]
        low = _low_half(SWA_BLOCK)
        for kv in groups:
            for pair in range(SWA_GROUP // 2):
                even = dqg[kv][2 * SWA_BLOCK * pair:2 * SWA_BLOCK * pair + SWA_BLOCK]
                odd = dqg[kv][2 * SWA_BLOCK * pair + SWA_BLOCK:2 * SWA_BLOCK * (pair + 1)]
                first = HEAD_PAD * (kv * SWA_GROUP // 2 + pair)
                dq_r[rows, first:first + HEAD_PAD] = jnp.where(low, even, pltpu.roll(odd, 64, 1)).astype(dq_r.dtype)
            dk_r[span, lanes[kv]] += dkk[kv]
            dv_r[span, lanes[kv]] += dvv[kv]

    tm = SWA_STEP_BLOCKS * SWA_BLOCK
    body, extra, extra_specs = _after(body, 6, dep)
    return pl.pallas_call(
        body, name="swa_bwd", grid=(L // tm,),
        in_specs=[_row_spec(tm, qw), _vmem_spec(), _vmem_spec(), _vmem_spec(),
                  pl.BlockSpec(memory_space=pltpu.SMEM), _row_spec(tm, qw)] + extra_specs,
        out_specs=[_row_spec(tm, qw // 2), _vmem_spec(), _vmem_spec(), _vmem_spec(), _vmem_spec()],
        out_shape=[jax.ShapeDtypeStruct((L, qw // 2), MXU_DTYPE),
                   jax.ShapeDtypeStruct((L + 2 * SWA_BLOCK, kw), F32),
                   jax.ShapeDtypeStruct((L + 2 * SWA_BLOCK, kw), F32),
                   jax.ShapeDtypeStruct((SWA_KV_HEADS, SWA_SPAN, SWA_GROUP_LANES), F32),
                   jax.ShapeDtypeStruct((SWA_KV_HEADS, 1, SWA_GROUP_LANES), F32)],
        compiler_params=_params(("arbitrary",), VMEM_BIG),
    )(qs, ks, vs, bias, sink, do, *extra)


def _bias_call(rel_bias, buckets, dep=None):
    def body(t_r, bk_r, o_r):
        bk = bk_r[...]
        s = lax.broadcasted_iota(jnp.int32, bk.shape, 0)
        c = lax.broadcasted_iota(jnp.int32, bk.shape, 1)
        in_band = jnp.abs(s - SWA_BLOCK - c) <= SWA_BLOCK
        for h in range(SWA_Q_HEADS):
            acc = jnp.zeros(bk.shape, F32)
            for b in range(REL_BUCKETS):
                acc = jnp.where(bk == b, t_r[b, h], acc)
            g = h % SWA_GROUP
            o_r[h // SWA_GROUP, :, SWA_BLOCK * g:SWA_BLOCK * (g + 1)] = jnp.where(in_band, acc, -1e30)

    body, extra, extra_specs = _after(body, 2, dep)
    return pl.pallas_call(
        body, name="band_bias",
        in_specs=[pl.BlockSpec(memory_space=pltpu.SMEM), _vmem_spec()] + extra_specs, out_specs=_vmem_spec(),
        out_shape=jax.ShapeDtypeStruct((SWA_KV_HEADS, SWA_SPAN, SWA_GROUP_LANES), F32),
    )(rel_bias, buckets, *extra)


def _relbias_call(dbias, dsink, buckets, dep=None):
    def body(db_r, ds_r, bk_r, o_r, os_r):
        bk = bk_r[...]
        rowi = lax.broadcasted_iota(jnp.int32, (REL_BUCKETS, 128), 0)
        lanei = lax.broadcasted_iota(jnp.int32, (REL_BUCKETS, 128), 1)
        lane1 = lax.broadcasted_iota(jnp.int32, (1, 128), 1)
        acc = jnp.zeros((REL_BUCKETS, 128), F32)
        acc_sink = jnp.zeros((1, 128), F32)
        heads = [(h // SWA_GROUP, slice(SWA_BLOCK * (h % SWA_GROUP), SWA_BLOCK * (h % SWA_GROUP + 1)))
                 for h in range(SWA_Q_HEADS)]
        for b in range(REL_BUCKETS):
            in_bucket = bk == b
            for h, (kv, lanes) in enumerate(heads):
                s = jnp.sum(jnp.where(in_bucket, db_r[kv, :, lanes], 0.0))
                acc = acc + jnp.where((rowi == b) & (lanei == h), s, 0.0)
        for h, (kv, lanes) in enumerate(heads):
            acc_sink = acc_sink + jnp.where(lane1 == h, jnp.sum(ds_r[kv, :, lanes]), 0.0)
        o_r[...] = acc
        os_r[...] = acc_sink

    body, extra, extra_specs = _after(body, 3, dep)
    return pl.pallas_call(
        body, name="relbias_grad",
        in_specs=[_vmem_spec()] * 3 + extra_specs, out_specs=[_vmem_spec()] * 2,
        out_shape=[jax.ShapeDtypeStruct((REL_BUCKETS, 128), F32), jax.ShapeDtypeStruct((1, 128), F32)],
    )(dbias, dsink, buckets, *extra)


def _mix_call(o_f, o_b, ga, o_s, x, gn, w_out_p, g_post, g_pre2, dep=None):
    L = x.shape[0]
    tm = min(512, L)
    hw = GLA_HEADS * HEAD_PAD

    def body(of_r, ob_r, ga_r, os_r, x_r, gn_r, w_r, gp_r, g2_r, cat_r, mix_r, h1_r, n2_r):
        gn_v = gn_r[...]
        for h in range(GLA_HEADS):
            sl = slice(HEAD_PAD * h, HEAD_PAD * (h + 1))
            oh = of_r[:, sl] + ob_r[:, sl]
            on = oh * _rms_r(oh) * gn_v
            gate = ga_r[:, sl]
            cat_r[:, sl] = (on * (gate * jax.nn.sigmoid(gate))).astype(cat_r.dtype)
        os_v = os_r[...]
        cat_r[:, hw:] = os_v
        mix = _dot(cat_r[:, :hw], w_r[:hw, :]) + _dot(os_v, w_r[hw:, :])
        mix_r[...] = mix
        h1 = x_r[...] + mix * _rms_r(mix) * gp_r[...]
        h1_r[...] = h1
        n2_r[...] = (h1 * _rms_r(h1) * g2_r[...]).astype(n2_r.dtype)

    body, extra, extra_specs = _after(body, 9, dep)
    return pl.pallas_call(
        body, name="mix_fwd", grid=(L // tm,),
        in_specs=[_row_spec(tm, hw), _row_spec(tm, hw), _row_spec(tm, hw), _row_spec(tm, OUT_PAD - hw),
                  _row_spec(tm, D_MODEL), _full_spec((1, HEAD_PAD)), _vmem_spec(),
                  _full_spec((1, D_MODEL)), _full_spec((1, D_MODEL))] + extra_specs,
        out_specs=[_row_spec(tm, OUT_PAD), _row_spec(tm, D_MODEL), _row_spec(tm, D_MODEL), _row_spec(tm, D_MODEL)],
        out_shape=[jax.ShapeDtypeStruct((L, OUT_PAD), MXU_DTYPE), jax.ShapeDtypeStruct((L, D_MODEL), F32),
                   jax.ShapeDtypeStruct((L, D_MODEL), F32), jax.ShapeDtypeStruct((L, D_MODEL), MXU_DTYPE)],
        compiler_params=_params(("arbitrary",), VMEM_BIG),
    )(o_f, o_b, ga, o_s, x, gn, w_out_p, g_post, g_pre2, *extra)


def _mlp_fwd_call(n2, h1, tgt, w_ud, g_post):
    L = n2.shape[0]
    tm = min(512, L)
    blk = D_FF // N_CHIPS

    def body(n2_r, h1_r, t_r, w_r, g_r, a_r, rz_r, dh2_r, dff_r, loss_r, dg_r):
        @pl.when(pl.program_id(0) == 0)
        def _():
            loss_r[...] = jnp.zeros_like(loss_r)
            dg_r[...] = jnp.zeros_like(dg_r)

        n2v = n2_r[...]
        ff = jnp.zeros((tm, D_MODEL), F32)
        for j in range(N_CHIPS):
            sl = slice(blk * j, blk * (j + 1))
            rz = jnp.maximum(_dot(n2v, w_r[j, 0]), 0.0)
            a = _mx(rz * rz)
            rz_r[:, sl] = rz.astype(rz_r.dtype)
            a_r[:, sl] = a
            ff = ff + _dot(a, w_r[j, 1])
        g = g_r[...]
        r = _rms_r(ff)
        err = h1_r[...] + ff * r * g - t_r[...]
        loss_r[...] += 0.5 * jnp.sum(err * err) / D_MODEL
        dh2 = err * (1.0 / D_MODEL)
        dh2_r[...] = dh2
        dff, dg = _rms_bwd(ff, r, g, dh2)
        dff_r[...] = dff.astype(dff_r.dtype)
        dg_r[...] += dg

    return pl.pallas_call(
        body, name="mlp_fwd", grid=(L // tm,),
        in_specs=[_row_spec(tm, D_MODEL), _row_spec(tm, D_MODEL), _row_spec(tm, D_MODEL),
                  _vmem_spec(), _full_spec((1, D_MODEL))],
        out_specs=[_row_spec(tm, D_FF), _row_spec(tm, D_FF), _row_spec(tm, D_MODEL), _row_spec(tm, D_MODEL),
                   _full_spec((1, 128)), _full_spec((1, D_MODEL))],
        out_shape=[jax.ShapeDtypeStruct((L, D_FF), MXU_DTYPE), jax.ShapeDtypeStruct((L, D_FF), MXU_DTYPE),
                   jax.ShapeDtypeStruct((L, D_MODEL), F32), jax.ShapeDtypeStruct((L, D_MODEL), MXU_DTYPE),
                   jax.ShapeDtypeStruct((1, 128), F32), jax.ShapeDtypeStruct((1, D_MODEL), F32)],
        compiler_params=_params(("arbitrary",), VMEM_BIG),
    )(n2, h1, tgt, w_ud, g_post)


def _mlp_bwd_call(dff, rz, w_ud):
    L = dff.shape[0]
    tm = min(512, L)
    blk = D_FF // N_CHIPS

    def body(dff_r, rz_r, w_r, dz_r, dn2_r):
        dffv = dff_r[...]
        dn2 = jnp.zeros((tm, D_MODEL), F32)
        for j in range(N_CHIPS):
            sl = slice(blk * j, blk * (j + 1))
            dz = _mx(_dot_nt(dffv, w_r[j, 1]) * 2.0 * rz_r[:, sl].astype(F32))
            dz_r[:, sl] = dz
            dn2 = dn2 + _dot_nt(dz, w_r[j, 0])
        dn2_r[...] = dn2

    return pl.pallas_call(
        body, name="mlp_bwd", grid=(L // tm,),
        in_specs=[_row_spec(tm, D_MODEL), _row_spec(tm, D_FF), _vmem_spec()],
        out_specs=[_row_spec(tm, D_FF), _row_spec(tm, D_MODEL)],
        out_shape=[jax.ShapeDtypeStruct((L, D_FF), MXU_DTYPE), jax.ShapeDtypeStruct((L, D_MODEL), F32)],
        compiler_params=_params(("arbitrary",), VMEM_BIG),
    )(dff, rz, w_ud)


def _mlp_wgrad_call(a, dff, n2, dz):
    L = a.shape[0]
    tf = 512
    per = (D_FF // N_CHIPS) // tf

    def body(a_r, dff_r, n2_r, dz_r, dwd_r, dwu_r):
        dwd_r[...] = _dot_tn(a_r[...], dff_r[...])
        dwu_r[...] = _dot_tn(n2_r[...], dz_r[...])

    return pl.pallas_call(
        body, name="mlp_wgrad", grid=(D_FF // tf,),
        in_specs=[pl.BlockSpec((L, tf), lambda j: (0, j)), _vmem_spec(), _vmem_spec(),
                  pl.BlockSpec((L, tf), lambda j: (0, j))],
        out_specs=[pl.BlockSpec((tf, D_MODEL), lambda j: (j, 0)),
                   pl.BlockSpec((None, D_MODEL, tf), lambda j: (j // per, 0, j % per))],
        out_shape=[jax.ShapeDtypeStruct((D_FF, D_MODEL), F32),
                   jax.ShapeDtypeStruct((N_CHIPS, D_MODEL, D_FF // N_CHIPS), F32)],
        compiler_params=_params(("arbitrary",), VMEM_BIG),
    )(a, dff, n2, dz)


def _mix_bwd_call(dn2, dh2, h1, mix, cat, o_f, o_b, ga, gn, g_post, g_pre2, w_out_p):
    L = dn2.shape[0]
    tm = min(512, L)
    hw = GLA_HEADS * HEAD_PAD

    def body(dn2_r, dh2_r, h1_r, mix_r, cat_r, of_r, ob_r, ga_r, gn_r, gp_r, g2_r, w_r,
             dh1_r, do_r, dga_r, dos_r, dw_r, dg2_r, dgp_r, dgn_r):
        @pl.when(pl.program_id(0) == 0)
        def _():
            for ref in (dw_r, dg2_r, dgp_r, dgn_r):
                ref[...] = jnp.zeros_like(ref)

        parts = [slice(start, start + min(256, tm)) for start in range(0, tm, 256)]
        dmix_m = []
        for rs in parts:
            h1 = h1_r[rs, :]
            dx2, dg2 = _rms_bwd(h1, _rms_r(h1), g2_r[...], dn2_r[rs, :])
            dh1 = dh2_r[rs, :] + dx2
            dh1_r[rs, :] = dh1
            dg2_r[...] += dg2
            mix = mix_r[rs, :]
            dmix, dgp = _rms_bwd(mix, _rms_r(mix), gp_r[...], dh1)
            dgp_r[...] += dgp
            dmix_m.append(_mx(dmix))
        dcat = [_dot_nt(d, w_r[...]) for d in dmix_m]
        for rs, d in zip(parts, dmix_m):
            dw_r[...] += _dot_tn(cat_r[rs, :], d)
        gn_v = gn_r[...]
        dgn = jnp.zeros((1, HEAD_PAD), F32)
        for rs, dc in zip(parts, dcat):
            dos_r[rs, :] = _spread_heads(dc[:, hw:]).astype(dos_r.dtype)
            for h in range(GLA_HEADS):
                sl = slice(HEAD_PAD * h, HEAD_PAD * (h + 1))
                oh = of_r[rs, sl] + ob_r[rs, sl]
                rr = _rms_r(oh)
                xh = oh * rr
                gate = ga_r[rs, sl]
                sg = jax.nn.sigmoid(gate)
                silu = gate * sg
                doa = dc[:, sl]
                dga_r[rs, sl] = (doa * (xh * gn_v) * (sg + silu * (1.0 - sg))).astype(dga_r.dtype)
                don = doa * silu
                gd = don * gn_v
                do_r[rs, sl] = rr * (gd - xh * jnp.mean(gd * xh, axis=-1, keepdims=True))
                dgn = dgn + jnp.sum(don * xh, axis=0, keepdims=True)
        dgn_r[...] += dgn

    return pl.pallas_call(
        body, name="mix_bwd", grid=(L // tm,),
        in_specs=[_row_spec(tm, D_MODEL)] * 4 + [_row_spec(tm, OUT_PAD)] + [_row_spec(tm, hw)] * 3
        + [_full_spec((1, HEAD_PAD)), _full_spec((1, D_MODEL)), _full_spec((1, D_MODEL)), _vmem_spec()],
        out_specs=[_row_spec(tm, D_MODEL), _row_spec(tm, hw), _row_spec(tm, hw),
                   _row_spec(tm, SWA_Q_HEADS * HEAD_PAD),
                   _full_spec((OUT_PAD, D_MODEL)), _full_spec((1, D_MODEL)), _full_spec((1, D_MODEL)),
                   _full_spec((1, HEAD_PAD))],
        out_shape=[jax.ShapeDtypeStruct((L, D_MODEL), F32), jax.ShapeDtypeStruct((L, hw), F32),
                   jax.ShapeDtypeStruct((L, hw), MXU_DTYPE),
                   jax.ShapeDtypeStruct((L, SWA_Q_HEADS * HEAD_PAD), MXU_DTYPE),
                   jax.ShapeDtypeStruct((OUT_PAD, D_MODEL), F32), jax.ShapeDtypeStruct((1, D_MODEL), F32),
                   jax.ShapeDtypeStruct((1, D_MODEL), F32), jax.ShapeDtypeStruct((1, HEAD_PAD), F32)],
        compiler_params=_params(("arbitrary",), VMEM_BIG),
    )(dn2, dh2, h1, mix, cat, o_f, o_b, ga, gn, g_post, g_pre2, w_out_p)


def _in_bwd_call(x, dh1, g_pre, w_in_t, pairs, singles, halos, dep=None):
    L = x.shape[0]
    tm = min(512, L)
    per = tm // SWA_BLOCK
    n_pair, n_single, n_halo = len(pairs), len(singles), len(halos)
    groups = [c for c, _ in pairs] + [c for c, _ in singles] + [c for c, _ in halos]

    def body(*refs):
        x_r, dh1_r, g_r, w_r = refs[:4]
        pair_refs = refs[4:4 + 2 * n_pair]
        single_refs = refs[4 + 2 * n_pair:4 + 2 * n_pair + n_single]
        halo_refs = refs[4 + 2 * n_pair + n_single:4 + 2 * n_pair + n_single + per * n_halo]
        dx_r, dw_r, dg_r = refs[4 + 2 * n_pair + n_single + per * n_halo:]

        @pl.when(pl.program_id(0) == 0)
        def _():
            dw_r[...] = jnp.zeros_like(dw_r)
            dg_r[...] = jnp.zeros_like(dg_r)

        xv = x_r[...]
        r = _rms_r(xv)
        g = g_r[...]
        u = _mx(xv * r * g)
        vals = [pair_refs[2 * i][...].astype(F32) + pair_refs[2 * i + 1][...].astype(F32) for i in range(n_pair)]
        vals += [ref[...].astype(F32) for ref in single_refs]
        vals += [jnp.concatenate([ref[...] for ref in halo_refs[per * i:per * (i + 1)]], axis=0)
                 for i in range(n_halo)]
        ds = [_mx(_squeeze_heads(val) if heads else val) for (_, _, heads), val in zip(groups, vals)]
        du = jnp.zeros((tm, D_MODEL), F32)
        for (first, rows, _), d in zip(groups, ds):
            du = du + _dot(d, w_r[first:first + rows, :])
        for (first, rows, _), d in zip(groups, ds):
            dw_r[first:first + rows, :] += _dot_tn(d, u)
        dx, dg = _rms_bwd(xv, r, g, du)
        dx_r[...] = dh1_r[...] + dx
        dg_r[...] += dg

    arrays = [a for _, pr in pairs for a in pr] + [a for _, a in singles]
    specs = [_row_spec(tm, a.shape[1]) for a in arrays]
    for _, a in halos:
        specs += [pl.BlockSpec((SWA_BLOCK, a.shape[1]), lambda i, j=j: (per * i + 1 + j, 0)) for j in range(per)]
        arrays += [a] * per
    body, extra, extra_specs = _after(body, 4 + len(arrays), dep)
    return pl.pallas_call(
        body, name="in_bwd", grid=(L // tm,),
        in_specs=[_row_spec(tm, D_MODEL), _row_spec(tm, D_MODEL), _full_spec((1, D_MODEL)), _vmem_spec()] + specs
        + extra_specs,
        out_specs=[_row_spec(tm, D_MODEL), _full_spec((IN_COLS, D_MODEL)), _full_spec((1, D_MODEL))],
        out_shape=[jax.ShapeDtypeStruct((L, D_MODEL), F32), jax.ShapeDtypeStruct((IN_COLS, D_MODEL), F32),
                   jax.ShapeDtypeStruct((1, D_MODEL), F32)],
        compiler_params=_params(("arbitrary",), VMEM_BIG),
    )(x, dh1, g_pre, w_in_t, *arrays, *extra)


def _adamw_math(w, g, m, v):
    m = ADAM_B1 * m + (1.0 - ADAM_B1) * g
    v = ADAM_B2 * v + (1.0 - ADAM_B2) * (g * g)
    m_hat = m / (1.0 - ADAM_B1 ** ADAM_STEP)
    v_hat = v / (1.0 - ADAM_B2 ** ADAM_STEP)
    delta = -ADAM_LR * (m_hat / (jnp.sqrt(v_hat) + ADAM_EPS) + ADAM_WD * w)
    return delta, m, v


def _adamw_call(w, g, m, v, name, dep=None):
    rows, cols = w.shape
    tr = min(256, rows)

    def body(w_r, g_r, m_r, v_r, g_out_r, d_r, nm_r, nv_r):
        g = g_r[...]
        g_out_r[...] = g
        d_r[...], nm_r[...], nv_r[...] = _adamw_math(w_r[...], g, m_r[...], v_r[...])

    if rows % tr == 0:
        spec, steps = _row_spec(tr, cols), rows // tr
    else:
        spec, steps = pl.BlockSpec((rows, 256), lambda i: (0, i)), cols // 256
    body, extra, extra_specs = _after(body, 4, dep)
    return pl.pallas_call(
        body, name=name, grid=(steps,),
        in_specs=[spec] * 4 + extra_specs, out_specs=[spec] * 4,
        out_shape=[jax.ShapeDtypeStruct(w.shape, F32)] * 4,
        compiler_params=_params(("arbitrary",)),
    )(w, g, m, v, *extra)


def _position():
    return lax.axis_index("x"), lax.axis_index("y"), lax.axis_index("c")


def _other_chips(x, y):
    return [(1 - x, y), (x, 1 - y), (1 - x, 1 - y)]


ROWS, COLS = -2, -1


def _half(ref, which, axis):
    size = ref.shape[axis] // 2
    span = pl.ds(pl.multiple_of(which * size, 16 if axis == ROWS else 128), size)
    index = [slice(None)] * len(ref.shape)
    index[axis] = span
    return ref.at[tuple(index)]


def _quarter(ref, half, which, axis):
    size = ref.shape[axis] // 4
    span = pl.ds(pl.multiple_of((2 * half + which) * size, 16 if axis == ROWS else 128), size)
    index = [slice(None)] * len(ref.shape)
    index[axis] = span
    return ref.at[tuple(index)]


def _first_gather_call(shards, axes, routed):
    n = len(shards)
    per = 7

    def body(*refs):
        srcs, outs = refs[:n], refs[n:2 * n]
        send_sems, recv_sems, local_sems = refs[2 * n:]
        x, y, c = _position()
        me, sibling = (x, y, c), (x, y, 1 - c)
        x_side, y_side, across = _other_chips(x, y)
        local = [pltpu.make_async_copy(srcs[a], outs[a].at[2 * x + y], local_sems.at[a]) for a in range(n)]
        for cp in local:
            cp.start()

        def copy(a, k, dst, to, src=None):
            return pltpu.make_async_remote_copy(
                src_ref=dst if src is None else src, dst_ref=dst, send_sem=send_sems.at[per * a + k],
                recv_sem=recv_sems.at[per * a + k], device_id=to, device_id_type=MESH_ID)

        def half(a, chip, pc):
            return _half(outs[a].at[2 * chip[0] + chip[1]], pc, axes[a])

        def quarter(a, chip, q):
            return _quarter(outs[a].at[2 * chip[0] + chip[1]], c, q, axes[a])

        sends = []
        for a in range(n):
            mine = _half(srcs[a], c, axes[a])
            targets = (x_side, y_side) if routed[a] else (x_side, y_side, across)
            sends += [copy(a, j, half(a, (x, y), c), (*chip, c), src=mine) for j, chip in enumerate(targets)]
        for cp in sends:
            cp.start()
        for a in range(n):
            for j, chip in enumerate((x_side, y_side)):
                copy(a, j, half(a, chip, c), me).wait_recv()
                if routed[a]:
                    other = (y_side, x_side)[j]
                    sends.append(copy(a, 2 + j, quarter(a, chip, j), (*other, c)))
                    sends[-1].start()
                sends.append(copy(a, 4 + j, half(a, chip, c), sibling))
                sends[-1].start()
        for a in range(n):
            if routed[a]:
                for j in range(2):
                    copy(a, 2 + j, quarter(a, across, j), me).wait_recv()
            else:
                copy(a, 2, half(a, across, c), me).wait_recv()
            sends.append(copy(a, 6, half(a, across, c), sibling))
            sends[-1].start()
        for a in range(n):
            for k, chip in ((4, x_side), (5, y_side), (6, across)):
                copy(a, k, half(a, chip, 1 - c), me).wait_recv()
        for cp in sends:
            cp.wait_send()
        for cp in local:
            cp.wait()

    return pl.pallas_call(
        body, name="first_gather",
        in_specs=[_any_spec()] * n, out_specs=[_any_spec()] * n,
        out_shape=[jax.ShapeDtypeStruct((N_CHIPS,) + s.shape, s.dtype) for s in shards],
        scratch_shapes=[pltpu.SemaphoreType.DMA((per * n,)), pltpu.SemaphoreType.DMA((per * n,)),
                        pltpu.SemaphoreType.DMA((n,))],
    )(*shards)


PAIR_PEERS, CHIP_PEERS = 1, 2


def _peers(which):
    x, y, c = _position()
    if which == PAIR_PEERS:
        return [(x, y, 1 - c)]
    return [(px, py, c) for px, py in _other_chips(x, y)]


def _split_start(name, arrays, n_copies, plan, peers=None):
    n = len(arrays)

    def body(*refs):
        ins, send_sems, recv_sems, token = refs[:n], refs[n], refs[n + 1], refs[-1]
        if peers is not None:
            barrier = pltpu.get_barrier_semaphore()
            targets = _peers(peers)
            for target in targets:
                pl.semaphore_signal(barrier, inc=1, device_id=target, device_id_type=MESH_ID)
            pl.semaphore_wait(barrier, len(targets))
        for k, (src, dst, to, _) in enumerate(plan(ins)):
            pltpu.make_async_remote_copy(src_ref=src, dst_ref=dst, send_sem=send_sems.at[k],
                                         recv_sem=recv_sems.at[k], device_id=to, device_id_type=MESH_ID).start()
        token[...] = jnp.zeros_like(token)

    hbm = pl.BlockSpec(memory_space=pltpu.HBM)
    sem = pl.BlockSpec(memory_space=pltpu.SEMAPHORE)
    out = pl.pallas_call(
        body, name=name,
        out_shape=(pltpu.SemaphoreType.DMA((n_copies,)), pltpu.SemaphoreType.DMA((n_copies,)))
        + tuple(pltpu.HBM(a.shape, a.dtype) for a in arrays) + (jax.ShapeDtypeStruct((8, 128), F32),),
        in_specs=[hbm] * n, out_specs=(sem, sem) + (hbm,) * n + (_vmem_spec(),),
        input_output_aliases={i: 2 + i for i in range(n)},
        compiler_params=pltpu.CompilerParams(has_side_effects=pltpu.SideEffectType.DATAFLOW_SIDE_EFFECTING,
                                             collective_id=peers),
    )(*[pltpu.with_memory_space_constraint(a, pltpu.HBM) for a in arrays])
    return (out[0], out[1], tuple(out[2:2 + n])), out[-1]


def _split_wait(name, handle, n_copies, plan, after):
    send_sems, recv_sems, arrays = handle
    n = len(arrays)

    def body(*refs):
        ins, s_sems, r_sems = refs[:n], refs[n], refs[n + 1]
        for k, (src, dst, to, landed) in enumerate(plan(ins)):
            cp = pltpu.make_async_remote_copy(src_ref=src, dst_ref=landed, send_sem=s_sems.at[k],
                                              recv_sem=r_sems.at[k], device_id=to, device_id_type=MESH_ID)
            cp.wait_send()
            cp.wait_recv()

    hbm = pl.BlockSpec(memory_space=pltpu.HBM)
    sem = pl.BlockSpec(memory_space=pltpu.SEMAPHORE)
    out = pl.pallas_call(
        body, name=name,
        out_shape=tuple(pltpu.HBM(a.shape, a.dtype) for a in arrays),
        in_specs=[hbm] * n + [sem, sem, _any_spec()], out_specs=(hbm,) * n,
        input_output_aliases={i: i for i in range(n)},
        compiler_params=pltpu.CompilerParams(has_side_effects=pltpu.SideEffectType.DATAFLOW_SIDE_EFFECTING),
    )(*arrays, send_sems, recv_sems, after)
    return tuple(out)


def _gather_plans(axes):
    n = len(axes)

    def stage_one(refs):
        x, y, c = _position()
        copies = []
        for a, axis in enumerate(axes):
            for px, py in _other_chips(x, y):
                copies.append((_half(refs[a], c, axis), _half(refs[n + a].at[2 * x + y], c, axis),
                               (px, py, c), _half(refs[n + a].at[2 * px + py], c, axis)))
        return copies

    def stage_two(refs):
        x, y, c = _position()
        copies = []
        for a, axis in enumerate(axes):
            for px, py in _other_chips(x, y):
                piece = _half(refs[n + a].at[2 * px + py], c, axis)
                copies.append((piece, piece, (x, y, 1 - c), _half(refs[n + a].at[2 * px + py], 1 - c, axis)))
        return copies

    return stage_one, stage_two


def _pair_swap_plan(axes):
    n = len(axes)

    def plan(refs):
        x, y, c = _position()
        return [(_half(refs[a], 1 - c, axes[a]), refs[n + a], (x, y, 1 - c), refs[n + a]) for a in range(n)]

    return plan


def _chip_swap_plan(n):
    def plan(refs):
        x, y, c = _position()
        copies = []
        for a in range(n):
            for j, (px, py) in enumerate(_other_chips(x, y)):
                copies.append((refs[a].at[2 * px + py], refs[n + a].at[j], (px, py, c), refs[n + a].at[j]))
        return copies

    return plan


def _pair_join_plan(axes):
    def plan(refs):
        x, y, c = _position()
        copies = []
        for a, axis in enumerate(axes):
            mine = _half(refs[a], c, axis)
            copies.append((mine, mine, (x, y, 1 - c), _half(refs[a], 1 - c, axis)))
        return copies

    return plan


def _pair_add_call(gs, gots, pos, name, axes):
    n = len(gs)

    def body(pos_r, *refs):
        for g_r, got_r, o_r in zip(refs[:n], refs[n:2 * n], refs[2 * n:]):
            o_r[...] = (g_r[...] + got_r[...]).astype(o_r.dtype)

    def mine(axis):
        return (lambda j, p: (j, p[1], 0)) if axis == ROWS else (lambda j, p: (j, 0, p[1]))

    blocks = [(None,) + got.shape[1:] for got in gots]
    return pl.pallas_call(
        body, name=name,
        grid_spec=pltpu.PrefetchScalarGridSpec(
            num_scalar_prefetch=1, grid=(N_CHIPS,),
            in_specs=[pl.BlockSpec(blk, mine(axis)) for blk, axis in zip(blocks, axes)]
            + [pl.BlockSpec(blk, lambda j, p: (j, 0, 0)) for blk in blocks],
            out_specs=[pl.BlockSpec(blk, lambda j, p: (j, 0, 0)) for blk in blocks]),
        out_shape=[jax.ShapeDtypeStruct(got.shape, COMM_DTYPE) for got in gots],
        compiler_params=_params(("arbitrary",), VMEM_BIG),
    )(pos, *gs, *gots)


def _chip_add_call(hsums, gots, pos, name, axes):
    n = len(hsums)
    steps = 2

    def body(pos_r, *refs):
        for own_r, got_r, o_r in zip(refs[:n], refs[n:2 * n], refs[2 * n:]):
            acc = own_r[...].astype(F32)
            for j in range(3):
                acc = acc + got_r[j].astype(F32)
            o_r[...] = acc

    in_specs, got_specs, out_specs, out_shape = [], [], [], []
    for h, axis in zip(hsums, axes):
        if axis == ROWS:
            rows, cols = h.shape[1] // steps, h.shape[2]
            in_specs.append(pl.BlockSpec((None, rows, cols), lambda i, p: (p[0], i, 0)))
            got_specs.append(pl.BlockSpec((3, rows, cols), lambda i, p: (0, i, 0)))
            out_specs.append(pl.BlockSpec((rows, cols), lambda i, p: (p[1] * steps + i, 0)))
            out_shape.append(jax.ShapeDtypeStruct((2 * h.shape[1], cols), F32))
        else:
            rows, cols = h.shape[1], h.shape[2] // steps
            in_specs.append(pl.BlockSpec((None, rows, cols), lambda i, p: (p[0], 0, i)))
            got_specs.append(pl.BlockSpec((3, rows, cols), lambda i, p: (0, 0, i)))
            out_specs.append(pl.BlockSpec((rows, cols), lambda i, p: (0, p[1] * steps + i)))
            out_shape.append(jax.ShapeDtypeStruct((rows, 2 * h.shape[2]), F32))
    return pl.pallas_call(
        body, name=name,
        grid_spec=pltpu.PrefetchScalarGridSpec(
            num_scalar_prefetch=1, grid=(steps,), in_specs=in_specs + got_specs, out_specs=out_specs),
        out_shape=out_shape,
        compiler_params=_params(("arbitrary",), VMEM_BIG),
    )(pos, *hsums, *gots)


SMALL_NAMES = ("norm_mix_pre", "norm_mix_post", "norm_mlp_pre", "norm_mlp_post", "b_gate_fwd", "b_gate_bwd",
               "gla_norm", "swa_sink", "rel_bias")


N_DEVICES = 8


def _small_pack_call(grads, extras):
    operands = list(grads) + list(extras)

    def body(*refs):
        g_refs, (all_a, all_b) = refs[:len(operands)], refs[len(operands):]
        x, y, c = _position()
        me = 4 * x + 2 * y + c
        all_a[me] = jnp.zeros(all_a.shape[1:], F32)
        all_b[me] = jnp.zeros(all_b.shape[1:], F32)
        for i in range(4):
            all_a[me, i:i + 1, :] = g_refs[i][...]
        all_a[me, 4:5, 0:256] = g_refs[4][...]
        all_a[me, 5:6, 0:256] = g_refs[5][...]
        all_a[me, 6:7, 0:128] = g_refs[6][...]
        all_a[me, 7:8, 0:128] = g_refs[7][...]
        all_a[me, 7:8, 128:256] = g_refs[11][...]
        all_b[me, 0:32, 0:128] = g_refs[8][...]
        all_b[me, 32:48, :] = g_refs[9][...]
        all_b[me, 48:64, :] = g_refs[10][...]

    out_shape = [jax.ShapeDtypeStruct((N_DEVICES, 8, D_MODEL), F32), jax.ShapeDtypeStruct((N_DEVICES, 64, 256), F32)]
    return pl.pallas_call(
        body, name="small_pack",
        in_specs=[_whole_spec(a.shape) for a in operands], out_specs=[_whole_spec(s.shape) for s in out_shape],
        out_shape=out_shape,
    )(*operands)


def _everyone_plan(n):
    def plan(refs):
        x, y, c = _position()
        copies = []
        for k in range(1, N_DEVICES):
            px = 1 - x if (k >> 2) & 1 else x
            py = 1 - y if (k >> 1) & 1 else y
            pc = 1 - c if k & 1 else c
            for a in range(n):
                mine = refs[a].at[4 * x + 2 * y + c]
                copies.append((mine, mine, (px, py, pc), refs[a].at[4 * px + 2 * py + pc]))
        return copies

    return plan


def _small_adamw_call(all_a, all_b, params):
    n_small = len(SMALL_NAMES)
    wmv = [t for p in params for t in p]
    shapes = [p[0].shape for p in params]

    def body(*refs):
        all_a, all_b = refs[:2]
        wmv_refs = refs[2:2 + 3 * n_small]
        out_refs = refs[2 + 3 * n_small:]
        sum_a, sum_b = all_a[0], all_b[0]
        for d in range(1, N_DEVICES):
            sum_a = sum_a + all_a[d]
            sum_b = sum_b + all_b[d]
        gsum = [sum_a[0:1], sum_a[1:2], sum_a[2:3], sum_a[3:4], sum_a[4:5, 0:256], sum_a[5:6, 0:256],
                sum_a[6:7, 0:128], sum_a[7:8, 0:SWA_Q_HEADS], sum_b[0:32, 0:SWA_Q_HEADS]]
        for i in range(n_small):
            w_r, m_r, v_r = wmv_refs[3 * i:3 * i + 3]
            delta, new_m, new_v = _adamw_math(w_r[...], gsum[i], m_r[...], v_r[...])
            out_refs[4 * i][...] = gsum[i]
            out_refs[4 * i + 1][...] = delta
            out_refs[4 * i + 2][...] = new_m
            out_refs[4 * i + 3][...] = new_v
        out_refs[4 * n_small][...] = sum_b[32:48]
        out_refs[4 * n_small + 1][...] = sum_b[48:64]
        out_refs[4 * n_small + 2][...] = sum_a[7:8, 128:256]

    out_shape = [jax.ShapeDtypeStruct(s, F32) for s in shapes for _ in range(4)]
    out_shape += [jax.ShapeDtypeStruct((GLA_GATE_RANK, 256), F32)] * 2 + [jax.ShapeDtypeStruct((1, 128), F32)]
    out = pl.pallas_call(
        body, name="small_adamw",
        in_specs=[_whole_spec(a.shape) for a in [all_a, all_b] + wmv],
        out_specs=[_whole_spec(s.shape) for s in out_shape],
        out_shape=out_shape,
    )(all_a, all_b, *wmv)
    per_name = [tuple(out[4 * i:4 * i + 4]) for i in range(n_small)]
    return per_name, out[4 * n_small], out[4 * n_small + 1], out[4 * n_small + 2]


def _pad_gate(w, first_row):
    return jnp.pad(w, ((first_row, 128 - GLA_GATE_RANK - first_row), (0, 0)))


def _own_slot(shard, chip):
    zone = lax.empty((N_CHIPS,) + shard.shape, shard.dtype)
    return lax.dynamic_update_slice(zone, shard[None], (chip,) + (0,) * shard.ndim)


def _reduce_to_owners(grads, axes, pos, tag, overlap):
    n = len(grads)

    def half_shape(g, axis):
        return (N_CHIPS, g.shape[1] // 2, g.shape[2]) if axis == ROWS else (N_CHIPS, g.shape[1], g.shape[2] // 2)

    lands = [lax.empty(half_shape(g, axis), F32) for g, axis in zip(grads, axes)]
    handle, token = _split_start(tag + "_pair_start", list(grads) + lands, n, _pair_swap_plan(axes), PAIR_PEERS)
    got = _split_wait(tag + "_pair_wait", handle, n, _pair_swap_plan(axes), overlap[0](token))
    sums = list(_pair_add_call(got[:n], got[n:], pos, tag + "_pair_add", axes))
    lands = [lax.empty((3,) + s.shape[1:], s.dtype) for s in sums]
    handle, token = _split_start(tag + "_chip_start", sums + lands, 3 * n, _chip_swap_plan(n), CHIP_PEERS)
    got = _split_wait(tag + "_chip_wait", handle, 3 * n, _chip_swap_plan(n), overlap[1](token))
    halves = list(_chip_add_call(got[:n], got[n:], pos, tag + "_chip_add", axes))
    handle, token = _split_start(tag + "_join_start", halves, n, _pair_join_plan(axes), PAIR_PEERS)
    return _split_wait(tag + "_join_wait", handle, n, _pair_join_plan(axes), overlap[2](token))


def kernel(x, norm_mix_pre, w_in, w_gate_up_fwd, b_gate_fwd, w_gate_up_bwd, b_gate_bwd, gla_norm, swa_sink, rel_bias, w_out, norm_mix_post, norm_mlp_pre, w_up, w_down, norm_mlp_post, loss_target, m_norm_mix_pre, m_w_in, m_w_gate_up_fwd, m_b_gate_fwd, m_w_gate_up_bwd, m_b_gate_bwd, m_gla_norm, m_swa_sink, m_rel_bias, m_w_out, m_norm_mix_post, m_norm_mlp_pre, m_w_up, m_w_down, m_norm_mlp_post, v_norm_mix_pre, v_w_in, v_w_gate_up_fwd, v_b_gate_fwd, v_w_gate_up_bwd, v_b_gate_bwd, v_gla_norm, v_swa_sink, v_rel_bias, v_w_out, v_norm_mix_post, v_norm_mlp_pre, v_w_up, v_w_down, v_norm_mlp_post):
    given = dict(locals())
    cx, cy, cc = _position()
    chip = (2 * cx + cy).astype(jnp.int32)
    pos = jnp.stack([chip, cc.astype(jnp.int32)])
    seq, tgt = x[0], loss_target[0]

    gates = jnp.concatenate([w_gate_up_fwd[0], w_gate_up_bwd[0]], axis=0).astype(COMM_DTYPE)
    all_in, all_gates = _first_gather_call([w_in[0].T.astype(COMM_DTYPE), gates], [COLS, ROWS], [True, False])
    rest = [w_out[0].astype(COMM_DTYPE), jnp.stack([w_up[0], w_down[0]]).astype(COMM_DTYPE)]
    stage_one, stage_two = _gather_plans([ROWS, ROWS])
    handle, token = _split_start("gather_chip_start", rest + [_own_slot(s, chip) for s in rest] + [all_gates], 6,
                                 stage_one, CHIP_PEERS)

    w_in_t = _mx(all_in.reshape(IN_COLS, D_MODEL))
    gates_full = jnp.concatenate([all_gates[j] for j in range(N_CHIPS)], axis=1)
    wgf_p = _mx(_pad_gate(gates_full[:GLA_GATE_RANK], 0))
    wgb_p = _mx(_pad_gate(gates_full[GLA_GATE_RANK:], GLA_GATE_RANK))
    bf_p, bb_p = b_gate_fwd, b_gate_bwd
    buckets = jnp.asarray(_band_buckets())
    sink1 = swa_sink.reshape(SWA_Q_HEADS)

    qa, ka, va, ga, qs, ks, vs, za = _proj_call(seq, norm_mix_pre, w_in_t, dep=token)
    halo = ((SWA_BLOCK, SWA_BLOCK), (0, 0))
    ks_p, vs_p = jnp.pad(ks, halo), jnp.pad(vs, halo)
    o_f, o_b, s_f, s_b = _gla_fwd_call(qa, ka, va, za, wgf_p, bf_p, wgb_p, bb_p)
    bias = _bias_call(rel_bias, buckets, dep=o_f)
    arrays = _split_wait("gather_chip_wait", handle, 6, stage_one, bias)
    handle, token = _split_start("gather_pair_start", list(arrays), 6, stage_two, PAIR_PEERS)
    o_s = _swa_fwd_call(qs, ks_p, vs_p, bias, sink1, dep=token)
    arrays = _split_wait("gather_pair_wait", handle, 6, stage_two, o_s)
    w_out_full = _mx(arrays[2].reshape(N_CHIPS * R_OUT, D_MODEL))
    w_ud = _mx(arrays[3])
    cat, mix, h1, n2 = _mix_call(o_f, o_b, ga, o_s, seq, gla_norm, w_out_full, norm_mix_post, norm_mlp_pre)
    a, rz, dh2, dff, loss, d_post2 = _mlp_fwd_call(n2, h1, tgt, w_ud, norm_mlp_post)

    dz, dn2 = _mlp_bwd_call(dff, rz, w_ud)
    dw_down, dw_up4 = _mlp_wgrad_call(a, dff, n2, dz)
    dh1, do, dga, dos, dw_out, d_pre2, d_post, d_gn = _mix_bwd_call(
        dn2, dh2, h1, mix, cat, o_f, o_b, ga, gla_norm, norm_mix_post, norm_mlp_pre, w_out_full)
    done = {}

    def swa_backward(tok):
        done["swa"] = _swa_bwd_call(qs, ks_p, vs_p, bias, sink1, dos, dep=tok)
        return done["swa"][0]

    def gla_in_backward(tok):
        done["gla"] = _gla_bwd_call(qa, ka, va, za, do, s_f, s_b, wgf_p, bf_p, wgb_p, bb_p, dep=tok)
        dqf, dkf, dvf, dzf, _, _, dqb, dkb, dvb, dzb, _, _ = done["gla"]
        dqs, dks_p, dvs_p, _, _ = done["swa"]
        done["in"] = _in_bwd_call(
            seq, dh1, norm_mix_pre, w_in_t,
            pairs=[(_side_by_side(T_QA), (dqf, dqb)), (_side_by_side(T_KA), (dkf, dkb)), (T_VA, (dvf, dvb)),
                   (T_ZA, (dzf, dzb))],
            singles=[(T_GA, dga), (_side_by_side(T_QS), dqs)], halos=[(T_KS, dks_p), (T_VS, dvs_p)])
        return done["in"][0]

    def bias_backward(tok):
        done["rel"] = _relbias_call(done["swa"][3], done["swa"][4], buckets, dep=tok)
        return done["rel"][0]

    g_up, g_down, g_out = _reduce_to_owners(
        [dw_up4, dw_down.reshape(N_CHIPS, R_DOWN, D_MODEL), dw_out.reshape(N_CHIPS, R_OUT, D_MODEL)],
        [ROWS, ROWS, ROWS], pos, "mlp", [swa_backward, gla_in_backward, bias_backward])
    dx, dw_in_t, d_pre = done["in"]
    dwf, dbf, dwb, dbb = done["gla"][4], done["gla"][5], done["gla"][10], done["gla"][11]
    drel, dsink = done["rel"]

    small_grads = [d_pre, d_post, d_pre2, d_post2, dbf, dbb, d_gn, dsink, drel]
    gate_grads = [dwf[:GLA_GATE_RANK], dwb[GLA_GATE_RANK:2 * GLA_GATE_RANK]]
    small_params = [(given[n], given["m_" + n], given["v_" + n]) for n in SMALL_NAMES]
    upd = {}

    everyone = _everyone_plan(2)
    small_handle, small_token = _split_start(
        "small_start", list(_small_pack_call(small_grads, gate_grads + [loss])), 2 * (N_DEVICES - 1), everyone)

    def update_out(tok):
        upd["w_out"] = tuple(_adamw_call(w_out[0], g_out, m_w_out[0], v_w_out[0], "adamw_w_out",
                                         dep=tok + small_token))
        return upd["w_out"][1]

    def update_mlp(tok):
        upd["w_up"] = tuple(_adamw_call(w_up[0], g_up, m_w_up[0], v_w_up[0], "adamw_w_up", dep=tok))
        upd["w_down"] = tuple(
            _adamw_call(w_down[0], g_down, m_w_down[0], v_w_down[0], "adamw_w_down", dep=upd["w_up"][1]))
        all_a, all_b = _split_wait("small_wait", small_handle, 2 * (N_DEVICES - 1), everyone, upd["w_down"][1])
        per_name, done["gf_sum"], done["gb_sum"], upd["loss"] = _small_adamw_call(all_a, all_b, small_params)
        upd.update(dict(zip(SMALL_NAMES, per_name)))
        return per_name[0][1]

    def update_gates(tok):
        for name, total in (("w_gate_up_fwd", done["gf_sum"]), ("w_gate_up_bwd", done["gb_sum"])):
            g = lax.dynamic_slice(total, (0, chip * 64), (GLA_GATE_RANK, 64))
            upd[name] = tuple(_adamw_call(given[name][0], g, given["m_" + name][0], given["v_" + name][0],
                                          "adamw_" + name, dep=tok))
        return upd["w_gate_up_bwd"][1]

    (g_in_t,) = _reduce_to_owners([dw_in_t.reshape(N_CHIPS, R_IN, D_MODEL)], [COLS], pos, "in",
                                  [update_out, update_mlp, update_gates])
    upd["w_in"] = tuple(t.T for t in _adamw_call(w_in[0].T, g_in_t, m_w_in[0].T, v_w_in[0].T, "adamw_w_in"))

    big = ("w_in", "w_gate_up_fwd", "w_gate_up_bwd", "w_out", "w_up", "w_down")
    names = ["norm_mix_pre", "w_in", "w_gate_up_fwd", "b_gate_fwd", "w_gate_up_bwd", "b_gate_bwd", "gla_norm",
             "swa_sink", "rel_bias", "w_out", "norm_mix_post", "norm_mlp_pre", "w_up", "w_down", "norm_mlp_post"]
    outs = [upd["loss"][0, 0], dx[None]]
    for kind in range(4):
        outs += [upd[n][kind][None] if n in big else upd[n][kind] for n in names]
    return tuple(outs)
```

```python
import math

import numpy as np
import jax
import jax.numpy as jnp
from jax import lax
from jax.experimental import pallas as pl
from jax.experimental.pallas import tpu as pltpu

F32 = jnp.float32
MXU_DTYPE = jnp.bfloat16
COMM_DTYPE = jnp.bfloat16

D_MODEL = 1024
D_FF = 4096
N_CHIPS = 4
GLA_HEADS = 4
GLA_CHUNK = 64
GLA_GATE_RANK = 16
GLA_GATE_NORM = 16.0
SWA_Q_HEADS = 8
SWA_KV_HEADS = 2
SWA_BLOCK = 128
REL_BUCKETS = 32
REL_MAX_DIST = 128
NORM_EPS = 1e-6
HEAD_PAD = 128

ADAM_LR = 0.001
ADAM_B1 = 0.9
ADAM_B2 = 0.999
ADAM_EPS = 1e-08
ADAM_WD = 0.01
ADAM_STEP = 10

OUT_PAD = 1024

R_IN, R_OUT, R_DOWN = 584, 256, 1024

VMEM_BIG = 56 * 1024 * 1024
MESH_ID = pl.DeviceIdType.MESH


def _mx(a):
    return a.astype(MXU_DTYPE)


def _dot(a, b):
    return jnp.dot(a, b, preferred_element_type=F32)


def _dot_nt(a, b):
    return lax.dot_general(a, b, (((1,), (1,)), ((), ())), preferred_element_type=F32)


def _dot_tn(a, b):
    return lax.dot_general(a, b, (((0,), (0,)), ((), ())), preferred_element_type=F32)


def _rms_r(x):
    return lax.rsqrt(jnp.mean(x * x, axis=-1, keepdims=True) + NORM_EPS)


def _rms_bwd(x, r, g, dy):
    xh = x * r
    gdy = dy * g
    dx = r * (gdy - xh * jnp.mean(gdy * xh, axis=-1, keepdims=True))
    return dx, jnp.sum(dy * xh, axis=0, keepdims=True)


def _low_half(rows):
    return lax.broadcasted_iota(jnp.int32, (rows, HEAD_PAD), 1) < 64


def _spread_heads(x):
    low = _low_half(x.shape[0])
    parts = []
    for p in range(x.shape[1] // HEAD_PAD):
        pair = x[:, HEAD_PAD * p:HEAD_PAD * (p + 1)]
        parts += [jnp.where(low, pair, 0.0), jnp.where(low, pltpu.roll(pair, 64, 1), 0.0)]
    return jnp.concatenate(parts, axis=1)


def _squeeze_heads(x):
    low = _low_half(x.shape[0])
    parts = []
    for p in range(x.shape[1] // (2 * HEAD_PAD)):
        even = x[:, 2 * HEAD_PAD * p:2 * HEAD_PAD * p + HEAD_PAD]
        odd = x[:, 2 * HEAD_PAD * p + HEAD_PAD:2 * HEAD_PAD * (p + 1)]
        parts.append(jnp.where(low, even, pltpu.roll(odd, 64, 1)))
    return parts[0] if len(parts) == 1 else jnp.concatenate(parts, axis=1)


def _params(sem=None, vmem=None):
    kw = {}
    if sem is not None:
        kw["dimension_semantics"] = sem
    if vmem is not None:
        kw["vmem_limit_bytes"] = vmem
    return pltpu.CompilerParams(**kw)


def _vmem_spec():
    return pl.BlockSpec(memory_space=pltpu.VMEM)


def _whole_spec(shape):
    return pl.BlockSpec(shape, lambda: (0,) * len(shape))


def _row_spec(tm, width):
    return pl.BlockSpec((tm, width), lambda i: (i, 0))


def _full_spec(shape):
    return pl.BlockSpec(shape, lambda i: (0,) * len(shape))


def _any_spec():
    return pl.BlockSpec(memory_space=pl.ANY)


def _after(body, n_in, dep):
    if dep is None:
        return body, [], []
    return (lambda *refs: body(*refs[:n_in], *refs[n_in + 1:])), [dep], [_any_spec()]


T_QA, T_KA, T_VA, T_GA = (0, 256, 4), (256, 256, 4), (512, 512, 0), (1024, 512, 0)
T_QS, T_KS, T_VS = (1568, 512, 8), (2080, 128, 2), (2208, 128, 2)
T_ZA = (1536, 128, 0)
ZA_COLS = 2 * GLA_GATE_RANK
IN_COLS = 2336


def _side_by_side(group):
    return group[0], group[1], 0


def _proj_call(x, g_pre, w_in_t, dep=None):
    L = x.shape[0]
    tm = min(512, L)
    groups = [(T_QA, F32), (T_KA, F32), (T_VA, MXU_DTYPE), (T_GA, F32),
              (T_QS, MXU_DTYPE), (T_KS, MXU_DTYPE), (T_VS, MXU_DTYPE), (T_ZA, F32)]
    widths = [rows * (2 if heads else 1) for (_, rows, heads), _ in groups]

    def body(x_ref, g_ref, w_ref, *outs):
        xv = x_ref[...]
        u = _mx(xv * _rms_r(xv) * g_ref[...])
        for ref, (grp, _) in zip(outs, groups):
            first, rows, heads = grp
            val = _dot_nt(u, w_ref[first:first + rows, :])
            if heads:
                val = _spread_heads(val)
            if grp is T_ZA:
                val = jnp.where(lax.broadcasted_iota(jnp.int32, val.shape, 1) < ZA_COLS, val, 0.0)
            if grp is T_QS:
                val = val * 0.125
            ref[...] = val.astype(ref.dtype)

    body, extra, extra_specs = _after(body, 3, dep)
    return pl.pallas_call(
        body, name="proj_fwd", grid=(L // tm,),
        in_specs=[_row_spec(tm, D_MODEL), _full_spec((1, D_MODEL)), _vmem_spec()] + extra_specs,
        out_specs=[_row_spec(tm, w) for w in widths],
        out_shape=[jax.ShapeDtypeStruct((L, w), dt) for w, (_, dt) in zip(widths, groups)],
        compiler_params=_params(("arbitrary",), VMEM_BIG),
    )(x, g_pre, w_in_t, *extra)


def _tri_masks():
    row = lax.broadcasted_iota(jnp.int32, (GLA_CHUNK, GLA_CHUNK), 0)
    col = lax.broadcasted_iota(jnp.int32, (GLA_CHUNK, GLA_CHUNK), 1)
    return row >= col, row <= col


def _chunk_sums(tri_m, x):
    hi = _mx(x)
    rest = x - hi.astype(F32)
    mid = _mx(rest)
    lo = _mx(rest - mid.astype(F32))
    return _dot(tri_m, hi) + _dot(tri_m, mid) + _dot(tri_m, lo)


def _gla_block_pre(q_r, k_r, z_r, w_r, b_r, rev, nc, qd_s, ki_s, ks_s, dec_s, keep=None):
    tri_f, tri_b = _tri_masks()
    tri_m = _mx((tri_b if rev else tri_f).astype(F32))
    g = _dot(_mx(z_r[...]), w_r[...]) + b_r[...]
    la = (jnp.minimum(g, 0.0) - jnp.log(1.0 + jnp.exp(-jnp.abs(g)))) * (1.0 / GLA_GATE_NORM)
    sums, lasts = [], []
    for c in range(nc):
        b_c = _chunk_sums(tri_m, la[GLA_CHUNK * c:GLA_CHUNK * (c + 1)])
        blast = b_c[0:1] if rev else b_c[GLA_CHUNK - 1:GLA_CHUNK]
        dec_s[c] = _spread_heads(jnp.exp(blast))
        sums.append(b_c)
        lasts.append(jnp.broadcast_to(blast, b_c.shape))
    b = jnp.concatenate(sums, axis=0)
    eb = jnp.exp(b)
    enb = jnp.exp(-b)
    elb = jnp.exp(jnp.concatenate(lasts, axis=0) - b)
    q, k = _squeeze_heads(q_r[...]), _squeeze_heads(k_r[...])
    qd_s[...] = _spread_heads(q * 0.125 * eb).astype(qd_s.dtype)
    ki_s[...] = _spread_heads(k * enb).astype(ki_s.dtype)
    ks_s[...] = _spread_heads(k * elb).astype(ks_s.dtype)
    if keep is not None:
        keep[0][...] = g
        for ref, val in zip(keep[1:], (eb, enb, elb)):
            ref[...] = _spread_heads(val)


def _gla_fwd_call(qa, ka, va, za, wgf, bgf, wgb, bgb):
    L = qa.shape[0]
    br = min(512, L)
    nb, nc, n_chunks = L // br, br // GLA_CHUNK, L // GLA_CHUNK
    hw = GLA_HEADS * HEAD_PAD

    def body(qaf, kaf, vaf, zaf, qab, kab, vab, zab, wgf_r, bgf_r, wgb_r, bgb_r,
             of_r, ob_r, sf_r, sb_r, st_f, st_b, pre_f, pre_b):
        @pl.when(pl.program_id(0) == 0)
        def _():
            st_f[...] = jnp.zeros_like(st_f)
            st_b[...] = jnp.zeros_like(st_b)

        _gla_block_pre(qaf, kaf, zaf, wgf_r, bgf_r, False, nc, *pre_f)
        _gla_block_pre(qab, kab, zab, wgb_r, bgb_r, True, nc, *pre_b)
        tri_f, tri_b = _tri_masks()

        def one(tri, pre, v_r, o_r, s_r, st, ci):
            qd_s, ki_s, ks_s, dec_s = pre
            rows = pl.ds(pl.multiple_of(ci * GLA_CHUNK, GLA_CHUNK), GLA_CHUNK)
            dec = dec_s[ci]
            heads = range(GLA_HEADS)
            lanes = [slice(HEAD_PAD * h, HEAD_PAD * (h + 1)) for h in heads]
            qd = [qd_s[rows, sl] for sl in lanes]
            v = [v_r[rows, sl] for sl in lanes]
            s_t = [st[h] for h in heads]
            a = [_dot_nt(qd[h], ki_s[rows, lanes[h]]) for h in heads]
            carried = [_dot_nt(qd[h], _mx(s_t[h])) for h in heads]
            grown = [_dot_tn(v[h], ks_s[rows, lanes[h]]) for h in heads]
            a = [_mx(jnp.where(tri, a[h], 0.0)) for h in heads]
            inner = [_dot(a[h], v[h]) for h in heads]
            for h in heads:
                s_r[ci, h] = s_t[h].astype(s_r.dtype)
                o_r[rows, lanes[h]] = inner[h] + carried[h]
                st[h] = s_t[h] * dec[:, lanes[h]] + grown[h]

        def loop(t, carry):
            one(tri_f, pre_f, vaf, of_r, sf_r, st_f, t)
            one(tri_b, pre_b, vab, ob_r, sb_r, st_b, nc - 1 - t)
            return carry

        lax.fori_loop(0, nc, loop, 0, unroll=True)

    fwd = lambda i: (i, 0)
    bwd = lambda i: (nb - 1 - i, 0)
    ins = lambda m: [pl.BlockSpec((br, hw), m), pl.BlockSpec((br, hw), m),
                     pl.BlockSpec((br, hw), m), pl.BlockSpec((br, 128), m)]
    wspecs = [_full_spec((128, hw // 2)), _full_spec((1, hw // 2))] * 2
    s_shape = (nc, GLA_HEADS, HEAD_PAD, HEAD_PAD)
    pre_scratch = [pltpu.VMEM((br, hw), MXU_DTYPE)] * 3 + [pltpu.VMEM((nc, 1, hw), F32)]
    return pl.pallas_call(
        body, name="gla_fwd", grid=(nb,),
        in_specs=ins(fwd) + ins(bwd) + wspecs,
        out_specs=[pl.BlockSpec((br, hw), fwd), pl.BlockSpec((br, hw), bwd),
                   pl.BlockSpec(s_shape, lambda i: (i, 0, 0, 0)),
                   pl.BlockSpec(s_shape, lambda i: (nb - 1 - i, 0, 0, 0))],
        out_shape=[jax.ShapeDtypeStruct((L, hw), F32), jax.ShapeDtypeStruct((L, hw), F32),
                   jax.ShapeDtypeStruct((n_chunks,) + s_shape[1:], MXU_DTYPE),
                   jax.ShapeDtypeStruct((n_chunks,) + s_shape[1:], MXU_DTYPE)],
        scratch_shapes=[pltpu.VMEM(s_shape[1:], F32), pltpu.VMEM(s_shape[1:], F32), pre_scratch, pre_scratch],
        compiler_params=_params(("arbitrary",), VMEM_BIG),
    )(qa, ka, va, za, qa, ka, va, za, wgf, bgf, wgb, bgb)


def _gla_bwd_call(qa, ka, va, za, do, sf, sb, wgf, bgf, wgb, bgb, dep=None):
    L = qa.shape[0]
    br = min(512, L)
    nb, nc = L // br, br // GLA_CHUNK
    hw = GLA_HEADS * HEAD_PAD

    def body(qaf, kaf, vaf, zaf, dof, sf_r, qab, kab, vab, zab, dob, sb_r, wgf_r, bgf_r, wgb_r, bgb_r,
             dqf, dkf, dvf, dzf, dwf, dbf, dqb, dkb, dvb, dzb, dwb, dbb, gt_f, gt_b, pre_f, pre_b):
        @pl.when(pl.program_id(0) == 0)
        def _():
            for ref in (gt_f, gt_b, dwf, dbf, dwb, dbb):
                ref[...] = jnp.zeros_like(ref)

        _gla_block_pre(qaf, kaf, zaf, wgf_r, bgf_r, False, nc, *pre_f[:4], keep=pre_f[4:8])
        _gla_block_pre(qab, kab, zab, wgb_r, bgb_r, True, nc, *pre_b[:4], keep=pre_b[4:8])
        tri_f, tri_b = _tri_masks()
        row_w = lax.broadcasted_iota(jnp.int32, (GLA_CHUNK, HEAD_PAD), 0)

        def one(rev, pre, q_r, k_r, v_r, do_r, s_r, dq_r, dk_r, dv_r, gt, ci):
            qd_s, ki_s, ks_s, dec_s, _, eb_s, enb_s, elb_s, db_s = pre
            tri = tri_b if rev else tri_f
            last_row = 0 if rev else GLA_CHUNK - 1
            rows = pl.ds(pl.multiple_of(ci * GLA_CHUNK, GLA_CHUNK), GLA_CHUNK)
            dec = dec_s[ci]
            heads = range(GLA_HEADS)
            lanes = [slice(HEAD_PAD * h, HEAD_PAD * (h + 1)) for h in heads]
            qd = [qd_s[rows, sl] for sl in lanes]
            ki = [ki_s[rows, sl] for sl in lanes]
            ks = [ks_s[rows, sl] for sl in lanes]
            v = [v_r[rows, sl] for sl in lanes]
            do_h = [_mx(do_r[rows, sl]) for sl in lanes]
            s_t = [s_r[ci, h] for h in heads]
            g_t = [gt[h] for h in heads]
            g_m = [_mx(g_t[h]) for h in heads]
            a = [_dot_nt(qd[h], ki[h]) for h in heads]
            da = [_dot_nt(do_h[h], v[h]) for h in heads]
            dv_carried = [_dot_nt(ks[h], g_m[h]) for h in heads]
            dqd_carried = [_dot(do_h[h], _mx(s_t[h])) for h in heads]
            dks = [_dot(v[h], g_m[h]) for h in heads]
            g_grown = [_dot_tn(do_h[h], qd[h]) for h in heads]
            a = [_mx(jnp.where(tri, a[h], 0.0)) for h in heads]
            da = [_mx(jnp.where(tri, da[h], 0.0)) for h in heads]
            dv_inner = [_dot_tn(a[h], do_h[h]) for h in heads]
            dqd_inner = [_dot(da[h], ki[h]) for h in heads]
            dki = [_dot_tn(da[h], qd[h]) for h in heads]
            dq, dk = [], []
            for h in heads:
                sl = lanes[h]
                dv_r[rows, sl] = (dv_inner[h] + dv_carried[h]).astype(dv_r.dtype)
                ddec = jnp.sum(g_t[h] * s_t[h].astype(F32), axis=0, keepdims=True)
                gt[h] = g_t[h] * dec[:, sl] + g_grown[h]
                dq.append((dqd_inner[h] + dqd_carried[h]) * eb_s[rows, sl] * 0.125)
                dk_state = dks[h] * elb_s[rows, sl]
                dk.append(dki[h] * enb_s[rows, sl] + dk_state)
                k = k_r[rows, sl]
                dblast = jnp.sum(dk_state * k, axis=0, keepdims=True) + dec[:, sl] * ddec
                db_s[rows, sl] = q_r[rows, sl] * dq[h] - k * dk[h] + jnp.where(row_w == last_row, dblast, 0.0)
            low = _low_half(GLA_CHUNK)
            for pair in range(GLA_HEADS // 2):
                psl = slice(HEAD_PAD * pair, HEAD_PAD * (pair + 1))
                for ref, val in ((dq_r, dq), (dk_r, dk)):
                    both = jnp.where(low, val[2 * pair], pltpu.roll(val[2 * pair + 1], 64, 1))
                    ref[rows, psl] = both.astype(ref.dtype)

        def loop(t, carry):
            one(False, pre_f, qaf, kaf, vaf, dof, sf_r, dqf, dkf, dvf, gt_f, nc - 1 - t)
            one(True, pre_b, qab, kab, vab, dob, sb_r, dqb, dkb, dvb, gt_b, t)
            return carry

        lax.fori_loop(0, nc, loop, 0, unroll=True)

        def gate_grads(rev, pre, z_r, w_r, dz_r, dw_r, dbias_r):
            g_s, db_s = pre[4], pre[8]
            back_m = _mx((tri_f if rev else tri_b).astype(F32))
            db = _squeeze_heads(db_s[...])
            dla = jnp.concatenate([_chunk_sums(back_m, db[GLA_CHUNK * c:GLA_CHUNK * (c + 1)]) for c in range(nc)],
                                  axis=0)
            dg = dla * (1.0 / GLA_GATE_NORM) * (1.0 / (1.0 + jnp.exp(g_s[...])))
            dg_m = _mx(dg)
            dz_r[...] = _dot_nt(dg_m, w_r[...])
            dw_r[...] += _dot_tn(_mx(z_r[...]), dg_m)
            dbias_r[...] += jnp.sum(dg, axis=0, keepdims=True)

        gate_grads(False, pre_f, zaf, wgf_r, dzf, dwf, dbf)
        gate_grads(True, pre_b, zab, wgb_r, dzb, dwb, dbb)

    last_first = lambda i: (nb - 1 - i, 0)
    first_last = lambda i: (i, 0)
    s_shape = (nc, GLA_HEADS, HEAD_PAD, HEAD_PAD)

    def ins(m):
        return [pl.BlockSpec((br, hw), m), pl.BlockSpec((br, hw), m), pl.BlockSpec((br, hw), m),
                pl.BlockSpec((br, 128), m), pl.BlockSpec((br, hw), m),
                pl.BlockSpec(s_shape, lambda i: m(i) + (0, 0))]

    def outs(m):
        return [pl.BlockSpec((br, hw // 2), m), pl.BlockSpec((br, hw // 2), m), pl.BlockSpec((br, hw), m),
                pl.BlockSpec((br, 128), m), _full_spec((128, hw // 2)), _full_spec((1, hw // 2))]

    out_shape = [jax.ShapeDtypeStruct((L, hw // 2), MXU_DTYPE)] * 2 + [
        jax.ShapeDtypeStruct((L, hw), MXU_DTYPE),
        jax.ShapeDtypeStruct((L, 128), F32), jax.ShapeDtypeStruct((128, hw // 2), F32),
        jax.ShapeDtypeStruct((1, hw // 2), F32)]
    wspecs = [_full_spec((128, hw // 2)), _full_spec((1, hw // 2))] * 2
    body, extra, extra_specs = _after(body, 16, dep)
    pre_scratch = ([pltpu.VMEM((br, hw), MXU_DTYPE)] * 3 + [pltpu.VMEM((nc, 1, hw), F32)]
                   + [pltpu.VMEM((br, hw // 2), F32)] + [pltpu.VMEM((br, hw), F32)] * 4)
    return pl.pallas_call(
        body, name="gla_bwd", grid=(nb,),
        in_specs=ins(last_first) + ins(first_last) + wspecs + extra_specs,
        out_specs=outs(last_first) + outs(first_last),
        out_shape=out_shape + out_shape,
        scratch_shapes=[pltpu.VMEM(s_shape[1:], F32), pltpu.VMEM(s_shape[1:], F32), pre_scratch, pre_scratch],
        compiler_params=_params(("arbitrary",), VMEM_BIG),
    )(qa, ka, va, za, do, sf, qa, ka, va, za, do, sb, wgf, bgf, wgb, bgb, *extra)


def _t5_buckets(rel):
    nb = REL_BUCKETS // 2
    ret = (rel > 0).astype(np.int32) * nb
    n = np.abs(rel)
    max_exact = nb // 2
    large = max_exact + (np.log(np.maximum(n, 1).astype(np.float32) / max_exact)
                         / math.log(REL_MAX_DIST / max_exact) * (nb - max_exact)).astype(np.int32)
    large = np.minimum(large, nb - 1)
    return ret + np.where(n < max_exact, n, large)


SWA_GROUP = SWA_Q_HEADS // SWA_KV_HEADS
SWA_SPAN = 3 * SWA_BLOCK
SWA_GROUP_LANES = SWA_GROUP * SWA_BLOCK


def _band_buckets():
    s = np.arange(SWA_SPAN)[:, None]
    c = np.arange(SWA_BLOCK)[None, :]
    return _t5_buckets(s - SWA_BLOCK - c).astype(np.int32)


def _swa_valid(n, seq_len):
    key_pos = (n - 1) * SWA_BLOCK + lax.broadcasted_iota(jnp.int32, (SWA_SPAN, 1), 0)
    return (key_pos >= 0) & (key_pos < seq_len)


def _swa_sink_row(sink_r, kv):
    lane = lax.broadcasted_iota(jnp.int32, (1, SWA_GROUP_LANES), 1)
    row = jnp.full((1, SWA_GROUP_LANES), sink_r[kv * SWA_GROUP], F32)
    for g in range(1, SWA_GROUP):
        row = jnp.where(lane >= g * SWA_BLOCK, sink_r[kv * SWA_GROUP + g], row)
    return row


SWA_STEP_BLOCKS = 8


def _swa_group(ref, kv, rows):
    first = kv * SWA_GROUP
    return jnp.concatenate([ref[rows, HEAD_PAD * h:HEAD_PAD * (h + 1)] for h in range(first, first + SWA_GROUP)],
                           axis=0)


def _swa_softmax(scores, bias_t, sink_row, valid):
    st = jnp.where(valid, scores + bias_t, -1e30)
    m = jnp.maximum(jnp.max(st, axis=0, keepdims=True), sink_row)
    p = jnp.exp(st - m)
    e_sink = jnp.exp(sink_row - m)
    inv = 1.0 / (jnp.sum(p, axis=0, keepdims=True) + e_sink)
    return p * inv, e_sink * inv


def _swa_fwd_call(qs, ks, vs, bias, sink, dep=None):
    L = qs.shape[0]

    def block(n, rows, q_r, k_r, v_r, bias_r, sink_r, o_r):
        span = pl.ds(pl.multiple_of(n * SWA_BLOCK, SWA_BLOCK), SWA_SPAN)
        valid = _swa_valid(n, L)
        groups = range(SWA_KV_HEADS)
        lanes = [slice(HEAD_PAD * kv, HEAD_PAD * (kv + 1)) for kv in groups]
        scores = [_dot_nt(k_r[span, lanes[kv]], _swa_group(q_r, kv, rows)) for kv in groups]
        probs = [_swa_softmax(scores[kv], bias_r[kv], _swa_sink_row(sink_r, kv), valid)[0] for kv in groups]
        low = _low_half(SWA_BLOCK)
        for kv in groups:
            og = _dot_tn(_mx(probs[kv]), v_r[span, lanes[kv]])
            for pair in range(SWA_GROUP // 2):
                even = og[2 * SWA_BLOCK * pair:2 * SWA_BLOCK * pair + SWA_BLOCK]
                odd = og[2 * SWA_BLOCK * pair + SWA_BLOCK:2 * SWA_BLOCK * (pair + 1)]
                first = HEAD_PAD * (kv * SWA_GROUP // 2 + pair)
                o_r[rows, first:first + HEAD_PAD] = jnp.where(low, even, pltpu.roll(odd, 64, 1)).astype(o_r.dtype)

    def body(*refs):
        for j in range(SWA_STEP_BLOCKS):
            block(SWA_STEP_BLOCKS * pl.program_id(0) + j, slice(SWA_BLOCK * j, SWA_BLOCK * (j + 1)), *refs)

    qw = SWA_Q_HEADS * HEAD_PAD
    tm = SWA_STEP_BLOCKS * SWA_BLOCK
    body, extra, extra_specs = _after(body, 5, dep)
    return pl.pallas_call(
        body, name="swa_fwd", grid=(L // tm,),
        in_specs=[_row_spec(tm, qw), _vmem_spec(), _vmem_spec(), _vmem_spec(),
                  pl.BlockSpec(memory_space=pltpu.SMEM)] + extra_specs,
        out_specs=_row_spec(tm, qw // 2),
        out_shape=jax.ShapeDtypeStruct((L, qw // 2), MXU_DTYPE),
        compiler_params=_params(("arbitrary",), VMEM_BIG),
    )(qs, ks, vs, bias, sink, *extra)


def _swa_bwd_call(qs, ks, vs, bias, sink, do, dep=None):
    L = qs.shape[0]
    qw = SWA_Q_HEADS * HEAD_PAD
    kw = SWA_KV_HEADS * HEAD_PAD

    def body(*refs):
        dk_r, dv_r, dbias_r, dsink_r = refs[7:]

        @pl.when(pl.program_id(0) == 0)
        def _():
            for ref in (dk_r, dv_r, dbias_r, dsink_r):
                ref[...] = jnp.zeros_like(ref)

        for j in range(SWA_STEP_BLOCKS):
            block(SWA_STEP_BLOCKS * pl.program_id(0) + j, slice(SWA_BLOCK * j, SWA_BLOCK * (j + 1)), *refs)

    def block(n, rows, q_r, k_r, v_r, bias_r, sink_r, do_r, dq_r, dk_r, dv_r, dbias_r, dsink_r):
        span = pl.ds(pl.multiple_of(n * SWA_BLOCK, SWA_BLOCK), SWA_SPAN)
        valid = _swa_valid(n, L)
        groups = range(SWA_KV_HEADS)
        lanes = [slice(HEAD_PAD * kv, HEAD_PAD * (kv + 1)) for kv in groups]
        kk = [k_r[span, sl] for sl in lanes]
        vv = [v_r[span, sl] for sl in lanes]
        qg = [_swa_group(q_r, kv, rows) for kv in groups]
        dog = [_swa_group(do_r, kv, rows) for kv in groups]
        scores = [_dot_nt(kk[kv], qg[kv]) for kv in groups]
        dp = [_dot_nt(vv[kv], dog[kv]) for kv in groups]
        probs = [_swa_softmax(scores[kv], bias_r[kv], _swa_sink_row(sink_r, kv), valid) for kv in groups]
        ds_m, pn_m = [], []
        for kv in groups:
            pn, p_sink = probs[kv]
            delta = jnp.sum(pn * dp[kv], axis=0, keepdims=True)
            ds = pn * (dp[kv] - delta)
            dsink_r[kv] -= p_sink * delta
            dbias_r[kv] += ds
            ds_m.append(_mx(ds))
            pn_m.append(_mx(pn))
        dqg = [_dot_tn(ds_m[kv], kk[kv]) * 0.125 for kv in groups]
        dkk = [_dot(ds_m[kv], qg[kv]) for kv in groups]
        dvv = [_dot(pn_m[kv], dog[kv]) for kv in groups]
        low = _low_half(SWA_BLOCK)
        for kv in groups:
            for pair in range(SWA_GROUP // 2):
                even = dqg[kv][2 * SWA_BLOCK * pair:2 * SWA_BLOCK * pair + SWA_BLOCK]
                odd = dqg[kv][2 * SWA_BLOCK * pair + SWA_BLOCK:2 * SWA_BLOCK * (pair + 1)]
                first = HEAD_PAD * (kv * SWA_GROUP // 2 + pair)
                dq_r[rows, first:first + HEAD_PAD] = jnp.where(low, even, pltpu.roll(odd, 64, 1)).astype(dq_r.dtype)
            dk_r[span, lanes[kv]] += dkk[kv]
            dv_r[span, lanes[kv]] += dvv[kv]

    tm = SWA_STEP_BLOCKS * SWA_BLOCK
    body, extra, extra_specs = _after(body, 6, dep)
    return pl.pallas_call(
        body, name="swa_bwd", grid=(L // tm,),
        in_specs=[_row_spec(tm, qw), _vmem_spec(), _vmem_spec(), _vmem_spec(),
                  pl.BlockSpec(memory_space=pltpu.SMEM), _row_spec(tm, qw)] + extra_specs,
        out_specs=[_row_spec(tm, qw // 2), _vmem_spec(), _vmem_spec(), _vmem_spec(), _vmem_spec()],
        out_shape=[jax.ShapeDtypeStruct((L, qw // 2), MXU_DTYPE),
                   jax.ShapeDtypeStruct((L + 2 * SWA_BLOCK, kw), F32),
                   jax.ShapeDtypeStruct((L + 2 * SWA_BLOCK, kw), F32),
                   jax.ShapeDtypeStruct((SWA_KV_HEADS, SWA_SPAN, SWA_GROUP_LANES), F32),
                   jax.ShapeDtypeStruct((SWA_KV_HEADS, 1, SWA_GROUP_LANES), F32)],
        compiler_params=_params(("arbitrary",), VMEM_BIG),
    )(qs, ks, vs, bias, sink, do, *extra)


def _bias_call(rel_bias, buckets, dep=None):
    def body(t_r, bk_r, o_r):
        bk = bk_r[...]
        s = lax.broadcasted_iota(jnp.int32, bk.shape, 0)
        c = lax.broadcasted_iota(jnp.int32, bk.shape, 1)
        in_band = jnp.abs(s - SWA_BLOCK - c) <= SWA_BLOCK
        for h in range(SWA_Q_HEADS):
            acc = jnp.zeros(bk.shape, F32)
            for b in range(REL_BUCKETS):
                acc = jnp.where(bk == b, t_r[b, h], acc)
            g = h % SWA_GROUP
            o_r[h // SWA_GROUP, :, SWA_BLOCK * g:SWA_BLOCK * (g + 1)] = jnp.where(in_band, acc, -1e30)

    body, extra, extra_specs = _after(body, 2, dep)
    return pl.pallas_call(
        body, name="band_bias",
        in_specs=[pl.BlockSpec(memory_space=pltpu.SMEM), _vmem_spec()] + extra_specs, out_specs=_vmem_spec(),
        out_shape=jax.ShapeDtypeStruct((SWA_KV_HEADS, SWA_SPAN, SWA_GROUP_LANES), F32),
    )(rel_bias, buckets, *extra)


def _relbias_call(dbias, dsink, buckets, dep=None):
    def body(db_r, ds_r, bk_r, o_r, os_r):
        bk = bk_r[...]
        rowi = lax.broadcasted_iota(jnp.int32, (REL_BUCKETS, 128), 0)
        lanei = lax.broadcasted_iota(jnp.int32, (REL_BUCKETS, 128), 1)
        lane1 = lax.broadcasted_iota(jnp.int32, (1, 128), 1)
        acc = jnp.zeros((REL_BUCKETS, 128), F32)
        acc_sink = jnp.zeros((1, 128), F32)
        heads = [(h // SWA_GROUP, slice(SWA_BLOCK * (h % SWA_GROUP), SWA_BLOCK * (h % SWA_GROUP + 1)))
                 for h in range(SWA_Q_HEADS)]
        for b in range(REL_BUCKETS):
            in_bucket = bk == b
            for h, (kv, lanes) in enumerate(heads):
                s = jnp.sum(jnp.where(in_bucket, db_r[kv, :, lanes], 0.0))
                acc = acc + jnp.where((rowi == b) & (lanei == h), s, 0.0)
        for h, (kv, lanes) in enumerate(heads):
            acc_sink = acc_sink + jnp.where(lane1 == h, jnp.sum(ds_r[kv, :, lanes]), 0.0)
        o_r[...] = acc
        os_r[...] = acc_sink

    body, extra, extra_specs = _after(body, 3, dep)
    return pl.pallas_call(
        body, name="relbias_grad",
        in_specs=[_vmem_spec()] * 3 + extra_specs, out_specs=[_vmem_spec()] * 2,
        out_shape=[jax.ShapeDtypeStruct((REL_BUCKETS, 128), F32), jax.ShapeDtypeStruct((1, 128), F32)],
    )(dbias, dsink, buckets, *extra)


def _mix_call(o_f, o_b, ga, o_s, x, gn, w_out_p, g_post, g_pre2, dep=None):
    L = x.shape[0]
    tm = min(512, L)
    hw = GLA_HEADS * HEAD_PAD

    def body(of_r, ob_r, ga_r, os_r, x_r, gn_r, w_r, gp_r, g2_r, cat_r, mix_r, h1_r, n2_r):
        gn_v = gn_r[...]
        for h in range(GLA_HEADS):
            sl = slice(HEAD_PAD * h, HEAD_PAD * (h + 1))
            oh = of_r[:, sl] + ob_r[:, sl]
            on = oh * _rms_r(oh) * gn_v
            gate = ga_r[:, sl]
            cat_r[:, sl] = (on * (gate * jax.nn.sigmoid(gate))).astype(cat_r.dtype)
        os_v = os_r[...]
        cat_r[:, hw:] = os_v
        mix = _dot(cat_r[:, :hw], w_r[:hw, :]) + _dot(os_v, w_r[hw:, :])
        mix_r[...] = mix
        h1 = x_r[...] + mix * _rms_r(mix) * gp_r[...]
        h1_r[...] = h1
        n2_r[...] = (h1 * _rms_r(h1) * g2_r[...]).astype(n2_r.dtype)

    body, extra, extra_specs = _after(body, 9, dep)
    return pl.pallas_call(
        body, name="mix_fwd", grid=(L // tm,),
        in_specs=[_row_spec(tm, hw), _row_spec(tm, hw), _row_spec(tm, hw), _row_spec(tm, OUT_PAD - hw),
                  _row_spec(tm, D_MODEL), _full_spec((1, HEAD_PAD)), _vmem_spec(),
                  _full_spec((1, D_MODEL)), _full_spec((1, D_MODEL))] + extra_specs,
        out_specs=[_row_spec(tm, OUT_PAD), _row_spec(tm, D_MODEL), _row_spec(tm, D_MODEL), _row_spec(tm, D_MODEL)],
        out_shape=[jax.ShapeDtypeStruct((L, OUT_PAD), MXU_DTYPE), jax.ShapeDtypeStruct((L, D_MODEL), F32),
                   jax.ShapeDtypeStruct((L, D_MODEL), F32), jax.ShapeDtypeStruct((L, D_MODEL), MXU_DTYPE)],
        compiler_params=_params(("arbitrary",), VMEM_BIG),
    )(o_f, o_b, ga, o_s, x, gn, w_out_p, g_post, g_pre2, *extra)


def _mlp_fwd_call(n2, h1, tgt, w_ud, g_post):
    L = n2.shape[0]
    tm = min(512, L)
    blk = D_FF // N_CHIPS

    def body(n2_r, h1_r, t_r, w_r, g_r, a_r, rz_r, dh2_r, dff_r, loss_r, dg_r):
        @pl.when(pl.program_id(0) == 0)
        def _():
            loss_r[...] = jnp.zeros_like(loss_r)
            dg_r[...] = jnp.zeros_like(dg_r)

        n2v = n2_r[...]
        ff = jnp.zeros((tm, D_MODEL), F32)
        for j in range(N_CHIPS):
            sl = slice(blk * j, blk * (j + 1))
            rz = jnp.maximum(_dot(n2v, w_r[j, 0]), 0.0)
            a = _mx(rz * rz)
            rz_r[:, sl] = rz.astype(rz_r.dtype)
            a_r[:, sl] = a
            ff = ff + _dot(a, w_r[j, 1])
        g = g_r[...]
        r = _rms_r(ff)
        err = h1_r[...] + ff * r * g - t_r[...]
        loss_r[...] += 0.5 * jnp.sum(err * err) / D_MODEL
        dh2 = err * (1.0 / D_MODEL)
        dh2_r[...] = dh2
        dff, dg = _rms_bwd(ff, r, g, dh2)
        dff_r[...] = dff.astype(dff_r.dtype)
        dg_r[...] += dg

    return pl.pallas_call(
        body, name="mlp_fwd", grid=(L // tm,),
        in_specs=[_row_spec(tm, D_MODEL), _row_spec(tm, D_MODEL), _row_spec(tm, D_MODEL),
                  _vmem_spec(), _full_spec((1, D_MODEL))],
        out_specs=[_row_spec(tm, D_FF), _row_spec(tm, D_FF), _row_spec(tm, D_MODEL), _row_spec(tm, D_MODEL),
                   _full_spec((1, 128)), _full_spec((1, D_MODEL))],
        out_shape=[jax.ShapeDtypeStruct((L, D_FF), MXU_DTYPE), jax.ShapeDtypeStruct((L, D_FF), MXU_DTYPE),
                   jax.ShapeDtypeStruct((L, D_MODEL), F32), jax.ShapeDtypeStruct((L, D_MODEL), MXU_DTYPE),
                   jax.ShapeDtypeStruct((1, 128), F32), jax.ShapeDtypeStruct((1, D_MODEL), F32)],
        compiler_params=_params(("arbitrary",), VMEM_BIG),
    )(n2, h1, tgt, w_ud, g_post)


def _mix_mlp_fwd_call(o_f, o_b, ga, o_s, x, tgt, gn, w_out_p, g_post, g_pre2, w_ud, g_post2):
    L = x.shape[0]
    tm = min(256, L)
    hw = GLA_HEADS * HEAD_PAD
    blk = D_FF // N_CHIPS

    def body(of_r, ob_r, ga_r, os_r, x_r, t_r, gn_r, w_r, gp_r, g2_r, wud_r, g3_r,
             cat_r, mix_r, h1_r, n2_r, a_r, rz_r, dh2_r, dff_r, loss_r, dg_r):
        @pl.when(pl.program_id(0) == 0)
        def _():
            loss_r[...] = jnp.zeros_like(loss_r)
            dg_r[...] = jnp.zeros_like(dg_r)

        gn_v = gn_r[...]
        for h in range(GLA_HEADS):
            sl = slice(HEAD_PAD * h, HEAD_PAD * (h + 1))
            oh = of_r[:, sl] + ob_r[:, sl]
            on = oh * _rms_r(oh) * gn_v
            gate = ga_r[:, sl]
            cat_r[:, sl] = (on * (gate * jax.nn.sigmoid(gate))).astype(cat_r.dtype)
        os_v = os_r[...]
        cat_r[:, hw:] = os_v
        mix = _dot(cat_r[:, :hw], w_r[:hw, :]) + _dot(os_v, w_r[hw:, :])
        mix_r[...] = mix
        h1 = x_r[...] + mix * _rms_r(mix) * gp_r[...]
        h1_r[...] = h1
        n2v = (h1 * _rms_r(h1) * g2_r[...]).astype(n2_r.dtype)
        n2_r[...] = n2v

        ff = jnp.zeros((tm, D_MODEL), F32)
        for j in range(N_CHIPS):
            sl = slice(blk * j, blk * (j + 1))
            rz = jnp.maximum(_dot(n2v, wud_r[j, 0]), 0.0)
            a = _mx(rz * rz)
            rz_r[:, sl] = rz.astype(rz_r.dtype)
            a_r[:, sl] = a
            ff = ff + _dot(a, wud_r[j, 1])
        g = g3_r[...]
        r = _rms_r(ff)
        err = h1 + ff * r * g - t_r[...]
        loss_r[...] += 0.5 * jnp.sum(err * err) / D_MODEL
        dh2 = err * (1.0 / D_MODEL)
        dh2_r[...] = dh2
        dff, dg = _rms_bwd(ff, r, g, dh2)
        dff_r[...] = dff.astype(dff_r.dtype)
        dg_r[...] += dg

    return pl.pallas_call(
        body, name="mix_mlp_fwd", grid=(L // tm,),
        in_specs=[_row_spec(tm, hw), _row_spec(tm, hw), _row_spec(tm, hw), _row_spec(tm, OUT_PAD - hw),
                  _row_spec(tm, D_MODEL), _row_spec(tm, D_MODEL), _full_spec((1, HEAD_PAD)), _vmem_spec(),
                  _full_spec((1, D_MODEL)), _full_spec((1, D_MODEL)), _vmem_spec(), _full_spec((1, D_MODEL))],
        out_specs=[_row_spec(tm, OUT_PAD), _row_spec(tm, D_MODEL), _row_spec(tm, D_MODEL), _row_spec(tm, D_MODEL),
                   _row_spec(tm, D_FF), _row_spec(tm, D_FF), _row_spec(tm, D_MODEL), _row_spec(tm, D_MODEL),
                   _full_spec((1, 128)), _full_spec((1, D_MODEL))],
        out_shape=[jax.ShapeDtypeStruct((L, OUT_PAD), MXU_DTYPE), jax.ShapeDtypeStruct((L, D_MODEL), F32),
                   jax.ShapeDtypeStruct((L, D_MODEL), F32), jax.ShapeDtypeStruct((L, D_MODEL), MXU_DTYPE),
                   jax.ShapeDtypeStruct((L, D_FF), MXU_DTYPE), jax.ShapeDtypeStruct((L, D_FF), MXU_DTYPE),
                   jax.ShapeDtypeStruct((L, D_MODEL), F32), jax.ShapeDtypeStruct((L, D_MODEL), MXU_DTYPE),
                   jax.ShapeDtypeStruct((1, 128), F32), jax.ShapeDtypeStruct((1, D_MODEL), F32)],
        compiler_params=_params(("arbitrary",), VMEM_BIG),
    )(o_f, o_b, ga, o_s, x, tgt, gn, w_out_p, g_post, g_pre2, w_ud, g_post2)


def _mlp_bwd_call(dff, rz, w_ud):
    L = dff.shape[0]
    tm = min(512, L)
    blk = D_FF // N_CHIPS

    def body(dff_r, rz_r, w_r, dz_r, dn2_r):
        dffv = dff_r[...]
        dn2 = jnp.zeros((tm, D_MODEL), F32)
        for j in range(N_CHIPS):
            sl = slice(blk * j, blk * (j + 1))
            dz = _mx(_dot_nt(dffv, w_r[j, 1]) * 2.0 * rz_r[:, sl].astype(F32))
            dz_r[:, sl] = dz
            dn2 = dn2 + _dot_nt(dz, w_r[j, 0])
        dn2_r[...] = dn2

    return pl.pallas_call(
        body, name="mlp_bwd", grid=(L // tm,),
        in_specs=[_row_spec(tm, D_MODEL), _row_spec(tm, D_FF), _vmem_spec()],
        out_specs=[_row_spec(tm, D_FF), _row_spec(tm, D_MODEL)],
        out_shape=[jax.ShapeDtypeStruct((L, D_FF), MXU_DTYPE), jax.ShapeDtypeStruct((L, D_MODEL), F32)],
        compiler_params=_params(("arbitrary",), VMEM_BIG),
    )(dff, rz, w_ud)


def _mlp_wgrad_call(a, dff, n2, dz):
    L = a.shape[0]
    tf = 512
    per = (D_FF // N_CHIPS) // tf

    def body(a_r, dff_r, n2_r, dz_r, dwd_r, dwu_r):
        dwd_r[...] = _dot_tn(a_r[...], dff_r[...])
        dwu_r[...] = _dot_tn(n2_r[...], dz_r[...])

    return pl.pallas_call(
        body, name="mlp_wgrad", grid=(D_FF // tf,),
        in_specs=[pl.BlockSpec((L, tf), lambda j: (0, j)), _vmem_spec(), _vmem_spec(),
                  pl.BlockSpec((L, tf), lambda j: (0, j))],
        out_specs=[pl.BlockSpec((tf, D_MODEL), lambda j: (j, 0)),
                   pl.BlockSpec((None, D_MODEL, tf), lambda j: (j // per, 0, j % per))],
        out_shape=[jax.ShapeDtypeStruct((D_FF, D_MODEL), F32),
                   jax.ShapeDtypeStruct((N_CHIPS, D_MODEL, D_FF // N_CHIPS), F32)],
        compiler_params=_params(("arbitrary",), VMEM_BIG),
    )(a, dff, n2, dz)


def _mix_bwd_call(dn2, dh2, h1, mix, cat, o_f, o_b, ga, gn, g_post, g_pre2, w_out_p):
    L = dn2.shape[0]
    tm = min(512, L)
    hw = GLA_HEADS * HEAD_PAD

    def body(dn2_r, dh2_r, h1_r, mix_r, cat_r, of_r, ob_r, ga_r, gn_r, gp_r, g2_r, w_r,
             dh1_r, do_r, dga_r, dos_r, dw_r, dg2_r, dgp_r, dgn_r):
        @pl.when(pl.program_id(0) == 0)
        def _():
            for ref in (dw_r, dg2_r, dgp_r, dgn_r):
                ref[...] = jnp.zeros_like(ref)

        parts = [slice(start, start + min(256, tm)) for start in range(0, tm, 256)]
        dmix_m = []
        for rs in parts:
            h1 = h1_r[rs, :]
            dx2, dg2 = _rms_bwd(h1, _rms_r(h1), g2_r[...], dn2_r[rs, :])
            dh1 = dh2_r[rs, :] + dx2
            dh1_r[rs, :] = dh1
            dg2_r[...] += dg2
            mix = mix_r[rs, :]
            dmix, dgp = _rms_bwd(mix, _rms_r(mix), gp_r[...], dh1)
            dgp_r[...] += dgp
            dmix_m.append(_mx(dmix))
        dcat = [_dot_nt(d, w_r[...]) for d in dmix_m]
        for rs, d in zip(parts, dmix_m):
            dw_r[...] += _dot_tn(cat_r[rs, :], d)
        gn_v = gn_r[...]
        dgn = jnp.zeros((1, HEAD_PAD), F32)
        for rs, dc in zip(parts, dcat):
            dos_r[rs, :] = _spread_heads(dc[:, hw:]).astype(dos_r.dtype)
            for h in range(GLA_HEADS):
                sl = slice(HEAD_PAD * h, HEAD_PAD * (h + 1))
                oh = of_r[rs, sl] + ob_r[rs, sl]
                rr = _rms_r(oh)
                xh = oh * rr
                gate = ga_r[rs, sl]
                sg = jax.nn.sigmoid(gate)
                silu = gate * sg
                doa = dc[:, sl]
                dga_r[rs, sl] = (doa * (xh * gn_v) * (sg + silu * (1.0 - sg))).astype(dga_r.dtype)
                don = doa * silu
                gd = don * gn_v
                do_r[rs, sl] = rr * (gd - xh * jnp.mean(gd * xh, axis=-1, keepdims=True))
                dgn = dgn + jnp.sum(don * xh, axis=0, keepdims=True)
        dgn_r[...] += dgn

    return pl.pallas_call(
        body, name="mix_bwd", grid=(L // tm,),
        in_specs=[_row_spec(tm, D_MODEL)] * 4 + [_row_spec(tm, OUT_PAD)] + [_row_spec(tm, hw)] * 3
        + [_full_spec((1, HEAD_PAD)), _full_spec((1, D_MODEL)), _full_spec((1, D_MODEL)), _vmem_spec()],
        out_specs=[_row_spec(tm, D_MODEL), _row_spec(tm, hw), _row_spec(tm, hw),
                   _row_spec(tm, SWA_Q_HEADS * HEAD_PAD),
                   _full_spec((OUT_PAD, D_MODEL)), _full_spec((1, D_MODEL)), _full_spec((1, D_MODEL)),
                   _full_spec((1, HEAD_PAD))],
        out_shape=[jax.ShapeDtypeStruct((L, D_MODEL), F32), jax.ShapeDtypeStruct((L, hw), F32),
                   jax.ShapeDtypeStruct((L, hw), MXU_DTYPE),
                   jax.ShapeDtypeStruct((L, SWA_Q_HEADS * HEAD_PAD), MXU_DTYPE),
                   jax.ShapeDtypeStruct((OUT_PAD, D_MODEL), F32), jax.ShapeDtypeStruct((1, D_MODEL), F32),
                   jax.ShapeDtypeStruct((1, D_MODEL), F32), jax.ShapeDtypeStruct((1, HEAD_PAD), F32)],
        compiler_params=_params(("arbitrary",), VMEM_BIG),
    )(dn2, dh2, h1, mix, cat, o_f, o_b, ga, gn, g_post, g_pre2, w_out_p)


def _in_bwd_call(x, dh1, g_pre, w_in_t, pairs, singles, halos, dep=None):
    L = x.shape[0]
    tm = min(512, L)
    per = tm // SWA_BLOCK
    n_pair, n_single, n_halo = len(pairs), len(singles), len(halos)
    groups = [c for c, _ in pairs] + [c for c, _ in singles] + [c for c, _ in halos]

    def body(*refs):
        x_r, dh1_r, g_r, w_r = refs[:4]
        pair_refs = refs[4:4 + 2 * n_pair]
        single_refs = refs[4 + 2 * n_pair:4 + 2 * n_pair + n_single]
        halo_refs = refs[4 + 2 * n_pair + n_single:4 + 2 * n_pair + n_single + per * n_halo]
        dx_r, dw_r, dg_r = refs[4 + 2 * n_pair + n_single + per * n_halo:]

        @pl.when(pl.program_id(0) == 0)
        def _():
            dw_r[...] = jnp.zeros_like(dw_r)
            dg_r[...] = jnp.zeros_like(dg_r)

        xv = x_r[...]
        r = _rms_r(xv)
        g = g_r[...]
        u = _mx(xv * r * g)
        vals = [pair_refs[2 * i][...].astype(F32) + pair_refs[2 * i + 1][...].astype(F32) for i in range(n_pair)]
        vals += [ref[...].astype(F32) for ref in single_refs]
        vals += [jnp.concatenate([ref[...] for ref in halo_refs[per * i:per * (i + 1)]], axis=0)
                 for i in range(n_halo)]
        ds = [_mx(_squeeze_heads(val) if heads else val) for (_, _, heads), val in zip(groups, vals)]
        du = jnp.zeros((tm, D_MODEL), F32)
        for (first, rows, _), d in zip(groups, ds):
            du = du + _dot(d, w_r[first:first + rows, :])
        for (first, rows, _), d in zip(groups, ds):
            dw_r[first:first + rows, :] += _dot_tn(d, u)
        dx, dg = _rms_bwd(xv, r, g, du)
        dx_r[...] = dh1_r[...] + dx
        dg_r[...] += dg

    arrays = [a for _, pr in pairs for a in pr] + [a for _, a in singles]
    specs = [_row_spec(tm, a.shape[1]) for a in arrays]
    for _, a in halos:
        specs += [pl.BlockSpec((SWA_BLOCK, a.shape[1]), lambda i, j=j: (per * i + 1 + j, 0)) for j in range(per)]
        arrays += [a] * per
    body, extra, extra_specs = _after(body, 4 + len(arrays), dep)
    return pl.pallas_call(
        body, name="in_bwd", grid=(L // tm,),
        in_specs=[_row_spec(tm, D_MODEL), _row_spec(tm, D_MODEL), _full_spec((1, D_MODEL)), _vmem_spec()] + specs
        + extra_specs,
        out_specs=[_row_spec(tm, D_MODEL), _full_spec((IN_COLS, D_MODEL)), _full_spec((1, D_MODEL))],
        out_shape=[jax.ShapeDtypeStruct((L, D_MODEL), F32), jax.ShapeDtypeStruct((IN_COLS, D_MODEL), F32),
                   jax.ShapeDtypeStruct((1, D_MODEL), F32)],
        compiler_params=_params(("arbitrary",), VMEM_BIG),
    )(x, dh1, g_pre, w_in_t, *arrays, *extra)


def _adamw_math(w, g, m, v):
    m = ADAM_B1 * m + (1.0 - ADAM_B1) * g
    v = ADAM_B2 * v + (1.0 - ADAM_B2) * (g * g)
    m_hat = m / (1.0 - ADAM_B1 ** ADAM_STEP)
    v_hat = v / (1.0 - ADAM_B2 ** ADAM_STEP)
    delta = -ADAM_LR * (m_hat / (jnp.sqrt(v_hat) + ADAM_EPS) + ADAM_WD * w)
    return delta, m, v


def _adamw_call(w, g, m, v, name, dep=None):
    rows, cols = w.shape
    tr = min(256, rows)

    def body(w_r, g_r, m_r, v_r, g_out_r, d_r, nm_r, nv_r):
        g = g_r[...]
        g_out_r[...] = g
        d_r[...], nm_r[...], nv_r[...] = _adamw_math(w_r[...], g, m_r[...], v_r[...])

    if rows % tr == 0:
        spec, steps = _row_spec(tr, cols), rows // tr
    else:
        spec, steps = pl.BlockSpec((rows, 256), lambda i: (0, i)), cols // 256
    body, extra, extra_specs = _after(body, 4, dep)
    return pl.pallas_call(
        body, name=name, grid=(steps,),
        in_specs=[spec] * 4 + extra_specs, out_specs=[spec] * 4,
        out_shape=[jax.ShapeDtypeStruct(w.shape, F32)] * 4,
        compiler_params=_params(("arbitrary",)),
    )(w, g, m, v, *extra)


def _position():
    return lax.axis_index("x"), lax.axis_index("y"), lax.axis_index("c")


def _other_chips(x, y):
    return [(1 - x, y), (x, 1 - y), (1 - x, 1 - y)]


ROWS, COLS = -2, -1


def _half(ref, which, axis):
    size = ref.shape[axis] // 2
    span = pl.ds(pl.multiple_of(which * size, 16 if axis == ROWS else 128), size)
    index = [slice(None)] * len(ref.shape)
    index[axis] = span
    return ref.at[tuple(index)]


def _quarter(ref, half, which, axis):
    size = ref.shape[axis] // 4
    span = pl.ds(pl.multiple_of((2 * half + which) * size, 16 if axis == ROWS else 128), size)
    index = [slice(None)] * len(ref.shape)
    index[axis] = span
    return ref.at[tuple(index)]


def _first_gather_call(shards, axes, routed):
    n = len(shards)
    per = 7

    def body(*refs):
        srcs, outs = refs[:n], refs[n:2 * n]
        send_sems, recv_sems, local_sems = refs[2 * n:]
        x, y, c = _position()
        me, sibling = (x, y, c), (x, y, 1 - c)
        x_side, y_side, across = _other_chips(x, y)
        local = [pltpu.make_async_copy(srcs[a], outs[a].at[2 * x + y], local_sems.at[a]) for a in range(n)]
        for cp in local:
            cp.start()

        def copy(a, k, dst, to, src=None):
            return pltpu.make_async_remote_copy(
                src_ref=dst if src is None else src, dst_ref=dst, send_sem=send_sems.at[per * a + k],
                recv_sem=recv_sems.at[per * a + k], device_id=to, device_id_type=MESH_ID)

        def half(a, chip, pc):
            return _half(outs[a].at[2 * chip[0] + chip[1]], pc, axes[a])

        def quarter(a, chip, q):
            return _quarter(outs[a].at[2 * chip[0] + chip[1]], c, q, axes[a])

        sends = []
        for a in range(n):
            mine = _half(srcs[a], c, axes[a])
            targets = (x_side, y_side) if routed[a] else (x_side, y_side, across)
            sends += [copy(a, j, half(a, (x, y), c), (*chip, c), src=mine) for j, chip in enumerate(targets)]
        for cp in sends:
            cp.start()
        for a in range(n):
            for j, chip in enumerate((x_side, y_side)):
                copy(a, j, half(a, chip, c), me).wait_recv()
                if routed[a]:
                    other = (y_side, x_side)[j]
                    sends.append(copy(a, 2 + j, quarter(a, chip, j), (*other, c)))
                    sends[-1].start()
                sends.append(copy(a, 4 + j, half(a, chip, c), sibling))
                sends[-1].start()
        for a in range(n):
            if routed[a]:
                for j in range(2):
                    copy(a, 2 + j, quarter(a, across, j), me).wait_recv()
            else:
                copy(a, 2, half(a, across, c), me).wait_recv()
            sends.append(copy(a, 6, half(a, across, c), sibling))
            sends[-1].start()
        for a in range(n):
            for k, chip in ((4, x_side), (5, y_side), (6, across)):
                copy(a, k, half(a, chip, 1 - c), me).wait_recv()
        for cp in sends:
            cp.wait_send()
        for cp in local:
            cp.wait()

    return pl.pallas_call(
        body, name="first_gather",
        in_specs=[_any_spec()] * n, out_specs=[_any_spec()] * n,
        out_shape=[jax.ShapeDtypeStruct((N_CHIPS,) + s.shape, s.dtype) for s in shards],
        scratch_shapes=[pltpu.SemaphoreType.DMA((per * n,)), pltpu.SemaphoreType.DMA((per * n,)),
                        pltpu.SemaphoreType.DMA((n,))],
    )(*shards)


PAIR_PEERS, CHIP_PEERS = 1, 2


def _peers(which):
    x, y, c = _position()
    if which == PAIR_PEERS:
        return [(x, y, 1 - c)]
    return [(px, py, c) for px, py in _other_chips(x, y)]


def _split_start(name, arrays, n_copies, plan, peers=None):
    n = len(arrays)

    def body(*refs):
        ins, send_sems, recv_sems, token = refs[:n], refs[n], refs[n + 1], refs[-1]
        if peers is not None:
            barrier = pltpu.get_barrier_semaphore()
            targets = _peers(peers)
            for target in targets:
                pl.semaphore_signal(barrier, inc=1, device_id=target, device_id_type=MESH_ID)
            pl.semaphore_wait(barrier, len(targets))
        for k, (src, dst, to, _) in enumerate(plan(ins)):
            pltpu.make_async_remote_copy(src_ref=src, dst_ref=dst, send_sem=send_sems.at[k],
                                         recv_sem=recv_sems.at[k], device_id=to, device_id_type=MESH_ID).start()
        token[...] = jnp.zeros_like(token)

    hbm = pl.BlockSpec(memory_space=pltpu.HBM)
    sem = pl.BlockSpec(memory_space=pltpu.SEMAPHORE)
    out = pl.pallas_call(
        body, name=name,
        out_shape=(pltpu.SemaphoreType.DMA((n_copies,)), pltpu.SemaphoreType.DMA((n_copies,)))
        + tuple(pltpu.HBM(a.shape, a.dtype) for a in arrays) + (jax.ShapeDtypeStruct((8, 128), F32),),
        in_specs=[hbm] * n, out_specs=(sem, sem) + (hbm,) * n + (_vmem_spec(),),
        input_output_aliases={i: 2 + i for i in range(n)},
        compiler_params=pltpu.CompilerParams(has_side_effects=pltpu.SideEffectType.DATAFLOW_SIDE_EFFECTING,
                                             collective_id=peers),
    )(*[pltpu.with_memory_space_constraint(a, pltpu.HBM) for a in arrays])
    return (out[0], out[1], tuple(out[2:2 + n])), out[-1]


def _split_wait(name, handle, n_copies, plan, after):
    send_sems, recv_sems, arrays = handle
    n = len(arrays)

    def body(*refs):
        ins, s_sems, r_sems = refs[:n], refs[n], refs[n + 1]
        for k, (src, dst, to, landed) in enumerate(plan(ins)):
            cp = pltpu.make_async_remote_copy(src_ref=src, dst_ref=landed, send_sem=s_sems.at[k],
                                              recv_sem=r_sems.at[k], device_id=to, device_id_type=MESH_ID)
            cp.wait_send()
            cp.wait_recv()

    hbm = pl.BlockSpec(memory_space=pltpu.HBM)
    sem = pl.BlockSpec(memory_space=pltpu.SEMAPHORE)
    out = pl.pallas_call(
        body, name=name,
        out_shape=tuple(pltpu.HBM(a.shape, a.dtype) for a in arrays),
        in_specs=[hbm] * n + [sem, sem, _any_spec()], out_specs=(hbm,) * n,
        input_output_aliases={i: i for i in range(n)},
        compiler_params=pltpu.CompilerParams(has_side_effects=pltpu.SideEffectType.DATAFLOW_SIDE_EFFECTING),
    )(*arrays, send_sems, recv_sems, after)
    return tuple(out)


def _gather_plans(axes):
    n = len(axes)

    def stage_one(refs):
        x, y, c = _position()
        copies = []
        for a, axis in enumerate(axes):
            for px, py in _other_chips(x, y):
                copies.append((_half(refs[a], c, axis), _half(refs[n + a].at[2 * x + y], c, axis),
                               (px, py, c), _half(refs[n + a].at[2 * px + py], c, axis)))
        return copies

    def stage_two(refs):
        x, y, c = _position()
        copies = []
        for a, axis in enumerate(axes):
            for px, py in _other_chips(x, y):
                piece = _half(refs[n + a].at[2 * px + py], c, axis)
                copies.append((piece, piece, (x, y, 1 - c), _half(refs[n + a].at[2 * px + py], 1 - c, axis)))
        return copies

    return stage_one, stage_two


def _pair_swap_plan(axes):
    n = len(axes)

    def plan(refs):
        x, y, c = _position()
        return [(_half(refs[a], 1 - c, axes[a]), refs[n + a], (x, y, 1 - c), refs[n + a]) for a in range(n)]

    return plan


def _chip_swap_plan(n):
    def plan(refs):
        x, y, c = _position()
        copies = []
        for a in range(n):
            for j, (px, py) in enumerate(_other_chips(x, y)):
                copies.append((refs[a].at[2 * px + py], refs[n + a].at[j], (px, py, c), refs[n + a].at[j]))
        return copies

    return plan


def _pair_join_plan(axes):
    def plan(refs):
        x, y, c = _position()
        copies = []
        for a, axis in enumerate(axes):
            mine = _half(refs[a], c, axis)
            copies.append((mine, mine, (x, y, 1 - c), _half(refs[a], 1 - c, axis)))
        return copies

    return plan


def _pair_add_call(gs, gots, pos, name, axes):
    n = len(gs)

    def body(pos_r, *refs):
        for g_r, got_r, o_r in zip(refs[:n], refs[n:2 * n], refs[2 * n:]):
            o_r[...] = (g_r[...] + got_r[...]).astype(o_r.dtype)

    def mine(axis):
        return (lambda j, p: (j, p[1], 0)) if axis == ROWS else (lambda j, p: (j, 0, p[1]))

    blocks = [(None,) + got.shape[1:] for got in gots]
    return pl.pallas_call(
        body, name=name,
        grid_spec=pltpu.PrefetchScalarGridSpec(
            num_scalar_prefetch=1, grid=(N_CHIPS,),
            in_specs=[pl.BlockSpec(blk, mine(axis)) for blk, axis in zip(blocks, axes)]
            + [pl.BlockSpec(blk, lambda j, p: (j, 0, 0)) for blk in blocks],
            out_specs=[pl.BlockSpec(blk, lambda j, p: (j, 0, 0)) for blk in blocks]),
        out_shape=[jax.ShapeDtypeStruct(got.shape, COMM_DTYPE) for got in gots],
        compiler_params=_params(("arbitrary",), VMEM_BIG),
    )(pos, *gs, *gots)


def _chip_add_call(hsums, gots, pos, name, axes):
    n = len(hsums)
    steps = 2

    def body(pos_r, *refs):
        for own_r, got_r, o_r in zip(refs[:n], refs[n:2 * n], refs[2 * n:]):
            acc = own_r[...].astype(F32)
            for j in range(3):
                acc = acc + got_r[j].astype(F32)
            o_r[...] = acc

    in_specs, got_specs, out_specs, out_shape = [], [], [], []
    for h, axis in zip(hsums, axes):
        if axis == ROWS:
            rows, cols = h.shape[1] // steps, h.shape[2]
            in_specs.append(pl.BlockSpec((None, rows, cols), lambda i, p: (p[0], i, 0)))
            got_specs.append(pl.BlockSpec((3, rows, cols), lambda i, p: (0, i, 0)))
            out_specs.append(pl.BlockSpec((rows, cols), lambda i, p: (p[1] * steps + i, 0)))
            out_shape.append(jax.ShapeDtypeStruct((2 * h.shape[1], cols), F32))
        else:
            rows, cols = h.shape[1], h.shape[2] // steps
            in_specs.append(pl.BlockSpec((None, rows, cols), lambda i, p: (p[0], 0, i)))
            got_specs.append(pl.BlockSpec((3, rows, cols), lambda i, p: (0, 0, i)))
            out_specs.append(pl.BlockSpec((rows, cols), lambda i, p: (0, p[1] * steps + i)))
            out_shape.append(jax.ShapeDtypeStruct((rows, 2 * h.shape[2]), F32))
    return pl.pallas_call(
        body, name=name,
        grid_spec=pltpu.PrefetchScalarGridSpec(
            num_scalar_prefetch=1, grid=(steps,), in_specs=in_specs + got_specs, out_specs=out_specs),
        out_shape=out_shape,
        compiler_params=_params(("arbitrary",), VMEM_BIG),
    )(pos, *hsums, *gots)


SMALL_NAMES = ("norm_mix_pre", "norm_mix_post", "norm_mlp_pre", "norm_mlp_post", "b_gate_fwd", "b_gate_bwd",
               "gla_norm", "swa_sink", "rel_bias")


N_DEVICES = 8


def _small_pack_call(grads, extras):
    operands = list(grads) + list(extras)

    def body(*refs):
        g_refs, (all_a, all_b) = refs[:len(operands)], refs[len(operands):]
        x, y, c = _position()
        me = 4 * x + 2 * y + c
        all_a[me] = jnp.zeros(all_a.shape[1:], F32)
        all_b[me] = jnp.zeros(all_b.shape[1:], F32)
        for i in range(4):
            all_a[me, i:i + 1, :] = g_refs[i][...]
        all_a[me, 4:5, 0:256] = g_refs[4][...]
        all_a[me, 5:6, 0:256] = g_refs[5][...]
        all_a[me, 6:7, 0:128] = g_refs[6][...]
        all_a[me, 7:8, 0:128] = g_refs[7][...]
        all_a[me, 7:8, 128:256] = g_refs[11][...]
        all_b[me, 0:32, 0:128] = g_refs[8][...]
        all_b[me, 32:48, :] = g_refs[9][...]
        all_b[me, 48:64, :] = g_refs[10][...]

    out_shape = [jax.ShapeDtypeStruct((N_DEVICES, 8, D_MODEL), F32), jax.ShapeDtypeStruct((N_DEVICES, 64, 256), F32)]
    return pl.pallas_call(
        body, name="small_pack",
        in_specs=[_whole_spec(a.shape) for a in operands], out_specs=[_whole_spec(s.shape) for s in out_shape],
        out_shape=out_shape,
    )(*operands)


def _everyone_plan(n):
    def plan(refs):
        x, y, c = _position()
        copies = []
        for k in range(1, N_DEVICES):
            px = 1 - x if (k >> 2) & 1 else x
            py = 1 - y if (k >> 1) & 1 else y
            pc = 1 - c if k & 1 else c
            for a in range(n):
                mine = refs[a].at[4 * x + 2 * y + c]
                copies.append((mine, mine, (px, py, pc), refs[a].at[4 * px + 2 * py + pc]))
        return copies

    return plan


def _small_adamw_call(all_a, all_b, params):
    n_small = len(SMALL_NAMES)
    wmv = [t for p in params for t in p]
    shapes = [p[0].shape for p in params]

    def body(*refs):
        all_a, all_b = refs[:2]
        wmv_refs = refs[2:2 + 3 * n_small]
        out_refs = refs[2 + 3 * n_small:]
        sum_a, sum_b = all_a[0], all_b[0]
        for d in range(1, N_DEVICES):
            sum_a = sum_a + all_a[d]
            sum_b = sum_b + all_b[d]
        gsum = [sum_a[0:1], sum_a[1:2], sum_a[2:3], sum_a[3:4], sum_a[4:5, 0:256], sum_a[5:6, 0:256],
                sum_a[6:7, 0:128], sum_a[7:8, 0:SWA_Q_HEADS], sum_b[0:32, 0:SWA_Q_HEADS]]
        for i in range(n_small):
            w_r, m_r, v_r = wmv_refs[3 * i:3 * i + 3]
            delta, new_m, new_v = _adamw_math(w_r[...], gsum[i], m_r[...], v_r[...])
            out_refs[4 * i][...] = gsum[i]
            out_refs[4 * i + 1][...] = delta
            out_refs[4 * i + 2][...] = new_m
            out_refs[4 * i + 3][...] = new_v
        out_refs[4 * n_small][...] = sum_b[32:48]
        out_refs[4 * n_small + 1][...] = sum_b[48:64]
        out_refs[4 * n_small + 2][...] = sum_a[7:8, 128:256]

    out_shape = [jax.ShapeDtypeStruct(s, F32) for s in shapes for _ in range(4)]
    out_shape += [jax.ShapeDtypeStruct((GLA_GATE_RANK, 256), F32)] * 2 + [jax.ShapeDtypeStruct((1, 128), F32)]
    out = pl.pallas_call(
        body, name="small_adamw",
        in_specs=[_whole_spec(a.shape) for a in [all_a, all_b] + wmv],
        out_specs=[_whole_spec(s.shape) for s in out_shape],
        out_shape=out_shape,
    )(all_a, all_b, *wmv)
    per_name = [tuple(out[4 * i:4 * i + 4]) for i in range(n_small)]
    return per_name, out[4 * n_small], out[4 * n_small + 1], out[4 * n_small + 2]


def _pad_gate(w, first_row):
    return jnp.pad(w, ((first_row, 128 - GLA_GATE_RANK - first_row), (0, 0)))


def _own_slot(shard, chip):
    zone = lax.empty((N_CHIPS,) + shard.shape, shard.dtype)
    return lax.dynamic_update_slice(zone, shard[None], (chip,) + (0,) * shard.ndim)


def _reduce_to_owners(grads, axes, pos, tag, overlap):
    n = len(grads)

    def half_shape(g, axis):
        return (N_CHIPS, g.shape[1] // 2, g.shape[2]) if axis == ROWS else (N_CHIPS, g.shape[1], g.shape[2] // 2)

    lands = [lax.empty(half_shape(g, axis), F32) for g, axis in zip(grads, axes)]
    handle, token = _split_start(tag + "_pair_start", list(grads) + lands, n, _pair_swap_plan(axes), PAIR_PEERS)
    got = _split_wait(tag + "_pair_wait", handle, n, _pair_swap_plan(axes), overlap[0](token))
    sums = list(_pair_add_call(got[:n], got[n:], pos, tag + "_pair_add", axes))
    lands = [lax.empty((3,) + s.shape[1:], s.dtype) for s in sums]
    handle, token = _split_start(tag + "_chip_start", sums + lands, 3 * n, _chip_swap_plan(n), CHIP_PEERS)
    got = _split_wait(tag + "_chip_wait", handle, 3 * n, _chip_swap_plan(n), overlap[1](token))
    halves = list(_chip_add_call(got[:n], got[n:], pos, tag + "_chip_add", axes))
    handle, token = _split_start(tag + "_join_start", halves, n, _pair_join_plan(axes), PAIR_PEERS)
    return _split_wait(tag + "_join_wait", handle, n, _pair_join_plan(axes), overlap[2](token))


def kernel(x, norm_mix_pre, w_in, w_gate_up_fwd, b_gate_fwd, w_gate_up_bwd, b_gate_bwd, gla_norm, swa_sink, rel_bias, w_out, norm_mix_post, norm_mlp_pre, w_up, w_down, norm_mlp_post, loss_target, m_norm_mix_pre, m_w_in, m_w_gate_up_fwd, m_b_gate_fwd, m_w_gate_up_bwd, m_b_gate_bwd, m_gla_norm, m_swa_sink, m_rel_bias, m_w_out, m_norm_mix_post, m_norm_mlp_pre, m_w_up, m_w_down, m_norm_mlp_post, v_norm_mix_pre, v_w_in, v_w_gate_up_fwd, v_b_gate_fwd, v_w_gate_up_bwd, v_b_gate_bwd, v_gla_norm, v_swa_sink, v_rel_bias, v_w_out, v_norm_mix_post, v_norm_mlp_pre, v_w_up, v_w_down, v_norm_mlp_post):
    given = dict(locals())
    cx, cy, cc = _position()
    chip = (2 * cx + cy).astype(jnp.int32)
    pos = jnp.stack([chip, cc.astype(jnp.int32)])
    seq, tgt = x[0], loss_target[0]

    gates = jnp.concatenate([w_gate_up_fwd[0], w_gate_up_bwd[0]], axis=0).astype(COMM_DTYPE)
    all_in, all_gates = _first_gather_call([w_in[0].T.astype(COMM_DTYPE), gates], [COLS, ROWS], [True, False])
    rest = [w_out[0].astype(COMM_DTYPE), jnp.stack([w_up[0], w_down[0]]).astype(COMM_DTYPE)]
    stage_one, stage_two = _gather_plans([ROWS, ROWS])
    handle, token = _split_start("gather_chip_start", rest + [_own_slot(s, chip) for s in rest] + [all_gates], 6,
                                 stage_one, CHIP_PEERS)

    w_in_t = _mx(all_in.reshape(IN_COLS, D_MODEL))
    gates_full = jnp.concatenate([all_gates[j] for j in range(N_CHIPS)], axis=1)
    wgf_p = _mx(_pad_gate(gates_full[:GLA_GATE_RANK], 0))
    wgb_p = _mx(_pad_gate(gates_full[GLA_GATE_RANK:], GLA_GATE_RANK))
    bf_p, bb_p = b_gate_fwd, b_gate_bwd
    buckets = jnp.asarray(_band_buckets())
    sink1 = swa_sink.reshape(SWA_Q_HEADS)

    qa, ka, va, ga, qs, ks, vs, za = _proj_call(seq, norm_mix_pre, w_in_t, dep=token)
    halo = ((SWA_BLOCK, SWA_BLOCK), (0, 0))
    ks_p, vs_p = jnp.pad(ks, halo), jnp.pad(vs, halo)
    o_f, o_b, s_f, s_b = _gla_fwd_call(qa, ka, va, za, wgf_p, bf_p, wgb_p, bb_p)
    bias = _bias_call(rel_bias, buckets, dep=o_f)
    arrays = _split_wait("gather_chip_wait", handle, 6, stage_one, bias)
    handle, token = _split_start("gather_pair_start", list(arrays), 6, stage_two, PAIR_PEERS)
    o_s = _swa_fwd_call(qs, ks_p, vs_p, bias, sink1, dep=token)
    arrays = _split_wait("gather_pair_wait", handle, 6, stage_two, o_s)
    w_out_full = _mx(arrays[2].reshape(N_CHIPS * R_OUT, D_MODEL))
    w_ud = _mx(arrays[3])
    cat, mix, h1, n2, a, rz, dh2, dff, loss, d_post2 = _mix_mlp_fwd_call(
        o_f, o_b, ga, o_s, seq, tgt, gla_norm, w_out_full, norm_mix_post, norm_mlp_pre, w_ud, norm_mlp_post)

    dz, dn2 = _mlp_bwd_call(dff, rz, w_ud)
    dw_down, dw_up4 = _mlp_wgrad_call(a, dff, n2, dz)
    dh1, do, dga, dos, dw_out, d_pre2, d_post, d_gn = _mix_bwd_call(
        dn2, dh2, h1, mix, cat, o_f, o_b, ga, gla_norm, norm_mix_post, norm_mlp_pre, w_out_full)
    done = {}

    def swa_backward(tok):
        done["swa"] = _swa_bwd_call(qs, ks_p, vs_p, bias, sink1, dos, dep=tok)
        return done["swa"][0]

    def gla_in_backward(tok):
        done["gla"] = _gla_bwd_call(qa, ka, va, za, do, s_f, s_b, wgf_p, bf_p, wgb_p, bb_p, dep=tok)
        dqf, dkf, dvf, dzf, _, _, dqb, dkb, dvb, dzb, _, _ = done["gla"]
        dqs, dks_p, dvs_p, _, _ = done["swa"]
        done["in"] = _in_bwd_call(
            seq, dh1, norm_mix_pre, w_in_t,
            pairs=[(_side_by_side(T_QA), (dqf, dqb)), (_side_by_side(T_KA), (dkf, dkb)), (T_VA, (dvf, dvb)),
                   (T_ZA, (dzf, dzb))],
            singles=[(T_GA, dga), (_side_by_side(T_QS), dqs)], halos=[(T_KS, dks_p), (T_VS, dvs_p)])
        return done["in"][0]

    def bias_backward(tok):
        done["rel"] = _relbias_call(done["swa"][3], done["swa"][4], buckets, dep=tok)
        return done["rel"][0]

    g_up, g_down, g_out = _reduce_to_owners(
        [dw_up4, dw_down.reshape(N_CHIPS, R_DOWN, D_MODEL), dw_out.reshape(N_CHIPS, R_OUT, D_MODEL)],
        [ROWS, ROWS, ROWS], pos, "mlp", [swa_backward, gla_in_backward, bias_backward])
    dx, dw_in_t, d_pre = done["in"]
    dwf, dbf, dwb, dbb = done["gla"][4], done["gla"][5], done["gla"][10], done["gla"][11]
    drel, dsink = done["rel"]

    small_grads = [d_pre, d_post, d_pre2, d_post2, dbf, dbb, d_gn, dsink, drel]
    gate_grads = [dwf[:GLA_GATE_RANK], dwb[GLA_GATE_RANK:2 * GLA_GATE_RANK]]
    small_params = [(given[n], given["m_" + n], given["v_" + n]) for n in SMALL_NAMES]
    upd = {}

    everyone = _everyone_plan(2)
    small_handle, small_token = _split_start(
        "small_start", list(_small_pack_call(small_grads, gate_grads + [loss])), 2 * (N_DEVICES - 1), everyone)

    def update_out(tok):
        upd["w_out"] = tuple(_adamw_call(w_out[0], g_out, m_w_out[0], v_w_out[0], "adamw_w_out",
                                         dep=tok + small_token))
        return upd["w_out"][1]

    def update_mlp(tok):
        upd["w_up"] = tuple(_adamw_call(w_up[0], g_up, m_w_up[0], v_w_up[0], "adamw_w_up", dep=tok))
        upd["w_down"] = tuple(
            _adamw_call(w_down[0], g_down, m_w_down[0], v_w_down[0], "adamw_w_down", dep=upd["w_up"][1]))
        all_a, all_b = _split_wait("small_wait", small_handle, 2 * (N_DEVICES - 1), everyone, upd["w_down"][1])
        per_name, done["gf_sum"], done["gb_sum"], upd["loss"] = _small_adamw_call(all_a, all_b, small_params)
        upd.update(dict(zip(SMALL_NAMES, per_name)))
        return per_name[0][1]

    def update_gates(tok):
        for name, total in (("w_gate_up_fwd", done["gf_sum"]), ("w_gate_up_bwd", done["gb_sum"])):
            g = lax.dynamic_slice(total, (0, chip * 64), (GLA_GATE_RANK, 64))
            upd[name] = tuple(_adamw_call(given[name][0], g, given["m_" + name][0], given["v_" + name][0],
                                          "adamw_" + name, dep=tok))
        return upd["w_gate_up_bwd"][1]

    (g_in_t,) = _reduce_to_owners([dw_in_t.reshape(N_CHIPS, R_IN, D_MODEL)], [COLS], pos, "in",
                                  [update_out, update_mlp, update_gates])
    upd["w_in"] = tuple(t.T for t in _adamw_call(w_in[0].T, g_in_t, m_w_in[0].T, v_w_in[0].T, "adamw_w_in"))

    big = ("w_in", "w_gate_up_fwd", "w_gate_up_bwd", "w_out", "w_up", "w_down")
    names = ["norm_mix_pre", "w_in", "w_gate_up_fwd", "b_gate_fwd", "w_gate_up_bwd", "b_gate_bwd", "gla_norm",
             "swa_sink", "rel_bias", "w_out", "norm_mix_post", "norm_mlp_pre", "w_up", "w_down", "norm_mlp_post"]
    outs = [upd["loss"][0, 0], dx[None]]
    for kind in range(4):
        outs += [upd[n][kind][None] if n in big else upd[n][kind] for n in names]
    return tuple(outs)
```

```python
import math

import numpy as np
import jax
import jax.numpy as jnp
from jax import lax
from jax.experimental import pallas as pl
from jax.experimental.pallas import tpu as pltpu

F32 = jnp.float32
MXU_DTYPE = jnp.bfloat16
COMM_DTYPE = jnp.bfloat16

D_MODEL = 1024
D_FF = 4096
N_CHIPS = 4
GLA_HEADS = 4
GLA_CHUNK = 64
GLA_GATE_RANK = 16
GLA_GATE_NORM = 16.0
SWA_Q_HEADS = 8
SWA_KV_HEADS = 2
SWA_BLOCK = 128
REL_BUCKETS = 32
REL_MAX_DIST = 128
NORM_EPS = 1e-6
HEAD_PAD = 128

ADAM_LR = 0.001
ADAM_B1 = 0.9
ADAM_B2 = 0.999
ADAM_EPS = 1e-08
ADAM_WD = 0.01
ADAM_STEP = 10

OUT_PAD = 1024

R_IN, R_OUT, R_DOWN = 584, 256, 1024

VMEM_BIG = 56 * 1024 * 1024
MESH_ID = pl.DeviceIdType.MESH


def _mx(a):
    return a.astype(MXU_DTYPE)


def _dot(a, b):
    return jnp.dot(a, b, preferred_element_type=F32)


def _dot_nt(a, b):
    return lax.dot_general(a, b, (((1,), (1,)), ((), ())), preferred_element_type=F32)


def _dot_tn(a, b):
    return lax.dot_general(a, b, (((0,), (0,)), ((), ())), preferred_element_type=F32)


def _rms_r(x):
    return lax.rsqrt(jnp.mean(x * x, axis=-1, keepdims=True) + NORM_EPS)


def _rms_bwd(x, r, g, dy):
    xh = x * r
    gdy = dy * g
    dx = r * (gdy - xh * jnp.mean(gdy * xh, axis=-1, keepdims=True))
    return dx, jnp.sum(dy * xh, axis=0, keepdims=True)


def _low_half(rows):
    return lax.broadcasted_iota(jnp.int32, (rows, HEAD_PAD), 1) < 64


def _spread_heads(x):
    low = _low_half(x.shape[0])
    parts = []
    for p in range(x.shape[1] // HEAD_PAD):
        pair = x[:, HEAD_PAD * p:HEAD_PAD * (p + 1)]
        parts += [jnp.where(low, pair, 0.0), jnp.where(low, pltpu.roll(pair, 64, 1), 0.0)]
    return jnp.concatenate(parts, axis=1)


def _squeeze_heads(x):
    low = _low_half(x.shape[0])
    parts = []
    for p in range(x.shape[1] // (2 * HEAD_PAD)):
        even = x[:, 2 * HEAD_PAD * p:2 * HEAD_PAD * p + HEAD_PAD]
        odd = x[:, 2 * HEAD_PAD * p + HEAD_PAD:2 * HEAD_PAD * (p + 1)]
        parts.append(jnp.where(low, even, pltpu.roll(odd, 64, 1)))
    return parts[0] if len(parts) == 1 else jnp.concatenate(parts, axis=1)


def _params(sem=None, vmem=None):
    kw = {}
    if sem is not None:
        kw["dimension_semantics"] = sem
    if vmem is not None:
        kw["vmem_limit_bytes"] = vmem
    return pltpu.CompilerParams(**kw)


def _vmem_spec():
    return pl.BlockSpec(memory_space=pltpu.VMEM)


def _whole_spec(shape):
    return pl.BlockSpec(shape, lambda: (0,) * len(shape))


def _row_spec(tm, width):
    return pl.BlockSpec((tm, width), lambda i: (i, 0))


def _full_spec(shape):
    return pl.BlockSpec(shape, lambda i: (0,) * len(shape))


def _any_spec():
    return pl.BlockSpec(memory_space=pl.ANY)


def _after(body, n_in, dep):
    if dep is None:
        return body, [], []
    return (lambda *refs: body(*refs[:n_in], *refs[n_in + 1:])), [dep], [_any_spec()]


T_QA, T_KA, T_VA, T_GA = (0, 256, 4), (256, 256, 4), (512, 512, 0), (1024, 512, 0)
T_QS, T_KS, T_VS = (1568, 512, 8), (2080, 128, 2), (2208, 128, 2)
T_ZA = (1536, 128, 0)
ZA_COLS = 2 * GLA_GATE_RANK
IN_COLS = 2336


def _side_by_side(group):
    return group[0], group[1], 0


def _proj_call(x, g_pre, w_in_t, dep=None):
    L = x.shape[0]
    tm = min(512, L)
    groups = [(T_QA, F32), (T_KA, F32), (T_VA, MXU_DTYPE), (T_GA, F32),
              (T_QS, MXU_DTYPE), (T_KS, MXU_DTYPE), (T_VS, MXU_DTYPE), (T_ZA, F32)]
    widths = [rows * (2 if heads else 1) for (_, rows, heads), _ in groups]

    def body(x_ref, g_ref, w_ref, *outs):
        xv = x_ref[...]
        u = _mx(xv * _rms_r(xv) * g_ref[...])
        for ref, (grp, _) in zip(outs, groups):
            first, rows, heads = grp
            val = _dot_nt(u, w_ref[first:first + rows, :])
            if heads:
                val = _spread_heads(val)
            if grp is T_ZA:
                val = jnp.where(lax.broadcasted_iota(jnp.int32, val.shape, 1) < ZA_COLS, val, 0.0)
            if grp is T_QS:
                val = val * 0.125
            ref[...] = val.astype(ref.dtype)

    body, extra, extra_specs = _after(body, 3, dep)
    return pl.pallas_call(
        body, name="proj_fwd", grid=(L // tm,),
        in_specs=[_row_spec(tm, D_MODEL), _full_spec((1, D_MODEL)), _vmem_spec()] + extra_specs,
        out_specs=[_row_spec(tm, w) for w in widths],
        out_shape=[jax.ShapeDtypeStruct((L, w), dt) for w, (_, dt) in zip(widths, groups)],
        compiler_params=_params(("arbitrary",), VMEM_BIG),
    )(x, g_pre, w_in_t, *extra)


def _tri_masks():
    row = lax.broadcasted_iota(jnp.int32, (GLA_CHUNK, GLA_CHUNK), 0)
    col = lax.broadcasted_iota(jnp.int32, (GLA_CHUNK, GLA_CHUNK), 1)
    return row >= col, row <= col


def _chunk_sums(tri_m, x):
    hi = _mx(x)
    rest = x - hi.astype(F32)
    mid = _mx(rest)
    lo = _mx(rest - mid.astype(F32))
    return _dot(tri_m, hi) + _dot(tri_m, mid) + _dot(tri_m, lo)


def _gla_block_pre(q_r, k_r, z_r, w_r, b_r, rev, nc, qd_s, ki_s, ks_s, dec_s, keep=None):
    tri_f, tri_b = _tri_masks()
    tri_m = _mx((tri_b if rev else tri_f).astype(F32))
    g = _dot(_mx(z_r[...]), w_r[...]) + b_r[...]
    la = (jnp.minimum(g, 0.0) - jnp.log(1.0 + jnp.exp(-jnp.abs(g)))) * (1.0 / GLA_GATE_NORM)
    sums, lasts = [], []
    for c in range(nc):
        b_c = _chunk_sums(tri_m, la[GLA_CHUNK * c:GLA_CHUNK * (c + 1)])
        blast = b_c[0:1] if rev else b_c[GLA_CHUNK - 1:GLA_CHUNK]
        dec_s[c] = _spread_heads(jnp.exp(blast))
        sums.append(b_c)
        lasts.append(jnp.broadcast_to(blast, b_c.shape))
    b = jnp.concatenate(sums, axis=0)
    eb = jnp.exp(b)
    enb = jnp.exp(-b)
    elb = jnp.exp(jnp.concatenate(lasts, axis=0) - b)
    q, k = _squeeze_heads(q_r[...]), _squeeze_heads(k_r[...])
    qd_s[...] = _spread_heads(q * 0.125 * eb).astype(qd_s.dtype)
    ki_s[...] = _spread_heads(k * enb).astype(ki_s.dtype)
    ks_s[...] = _spread_heads(k * elb).astype(ks_s.dtype)
    if keep is not None:
        keep[0][...] = g
        for ref, val in zip(keep[1:], (eb, enb, elb)):
            ref[...] = _spread_heads(val)


def _gla_fwd_call(qa, ka, va, za, wgf, bgf, wgb, bgb):
    L = qa.shape[0]
    br = min(512, L)
    nb, nc, n_chunks = L // br, br // GLA_CHUNK, L // GLA_CHUNK
    hw = GLA_HEADS * HEAD_PAD

    def body(qaf, kaf, vaf, zaf, qab, kab, vab, zab, wgf_r, bgf_r, wgb_r, bgb_r,
             of_r, ob_r, sf_r, sb_r, st_f, st_b, pre_f, pre_b):
        @pl.when(pl.program_id(0) == 0)
        def _():
            st_f[...] = jnp.zeros_like(st_f)
            st_b[...] = jnp.zeros_like(st_b)

        _gla_block_pre(qaf, kaf, zaf, wgf_r, bgf_r, False, nc, *pre_f)
        _gla_block_pre(qab, kab, zab, wgb_r, bgb_r, True, nc, *pre_b)
        tri_f, tri_b = _tri_masks()

        def one(tri, pre, v_r, o_r, s_r, st, ci):
            qd_s, ki_s, ks_s, dec_s = pre
            rows = pl.ds(pl.multiple_of(ci * GLA_CHUNK, GLA_CHUNK), GLA_CHUNK)
            dec = dec_s[ci]
            heads = range(GLA_HEADS)
            lanes = [slice(HEAD_PAD * h, HEAD_PAD * (h + 1)) for h in heads]
            qd = [qd_s[rows, sl] for sl in lanes]
            v = [v_r[rows, sl] for sl in lanes]
            s_t = [st[h] for h in heads]
            a = [_dot_nt(qd[h], ki_s[rows, lanes[h]]) for h in heads]
            carried = [_dot_nt(qd[h], _mx(s_t[h])) for h in heads]
            grown = [_dot_tn(v[h], ks_s[rows, lanes[h]]) for h in heads]
            a = [_mx(jnp.where(tri, a[h], 0.0)) for h in heads]
            inner = [_dot(a[h], v[h]) for h in heads]
            for h in heads:
                s_r[ci, h] = s_t[h].astype(s_r.dtype)
                o_r[rows, lanes[h]] = inner[h] + carried[h]
                st[h] = s_t[h] * dec[:, lanes[h]] + grown[h]

        def loop(t, carry):
            one(tri_f, pre_f, vaf, of_r, sf_r, st_f, t)
            one(tri_b, pre_b, vab, ob_r, sb_r, st_b, nc - 1 - t)
            return carry

        lax.fori_loop(0, nc, loop, 0, unroll=True)

    fwd = lambda i: (i, 0)
    bwd = lambda i: (nb - 1 - i, 0)
    ins = lambda m: [pl.BlockSpec((br, hw), m), pl.BlockSpec((br, hw), m),
                     pl.BlockSpec((br, hw), m), pl.BlockSpec((br, 128), m)]
    wspecs = [_full_spec((128, hw // 2)), _full_spec((1, hw // 2))] * 2
    s_shape = (nc, GLA_HEADS, HEAD_PAD, HEAD_PAD)
    pre_scratch = [pltpu.VMEM((br, hw), MXU_DTYPE)] * 3 + [pltpu.VMEM((nc, 1, hw), F32)]
    return pl.pallas_call(
        body, name="gla_fwd", grid=(nb,),
        in_specs=ins(fwd) + ins(bwd) + wspecs,
        out_specs=[pl.BlockSpec((br, hw), fwd), pl.BlockSpec((br, hw), bwd),
                   pl.BlockSpec(s_shape, lambda i: (i, 0, 0, 0)),
                   pl.BlockSpec(s_shape, lambda i: (nb - 1 - i, 0, 0, 0))],
        out_shape=[jax.ShapeDtypeStruct((L, hw), F32), jax.ShapeDtypeStruct((L, hw), F32),
                   jax.ShapeDtypeStruct((n_chunks,) + s_shape[1:], MXU_DTYPE),
                   jax.ShapeDtypeStruct((n_chunks,) + s_shape[1:], MXU_DTYPE)],
        scratch_shapes=[pltpu.VMEM(s_shape[1:], F32), pltpu.VMEM(s_shape[1:], F32), pre_scratch, pre_scratch],
        compiler_params=_params(("arbitrary",), VMEM_BIG),
    )(qa, ka, va, za, qa, ka, va, za, wgf, bgf, wgb, bgb)


def _gla_bwd_call(qa, ka, va, za, do, sf, sb, wgf, bgf, wgb, bgb, dep=None):
    L = qa.shape[0]
    br = min(512, L)
    nb, nc = L // br, br // GLA_CHUNK
    hw = GLA_HEADS * HEAD_PAD

    def body(qaf, kaf, vaf, zaf, dof, sf_r, qab, kab, vab, zab, dob, sb_r, wgf_r, bgf_r, wgb_r, bgb_r,
             dqf, dkf, dvf, dzf, dwf, dbf, dqb, dkb, dvb, dzb, dwb, dbb, gt_f, gt_b, pre_f, pre_b):
        @pl.when(pl.program_id(0) == 0)
        def _():
            for ref in (gt_f, gt_b, dwf, dbf, dwb, dbb):
                ref[...] = jnp.zeros_like(ref)

        _gla_block_pre(qaf, kaf, zaf, wgf_r, bgf_r, False, nc, *pre_f[:4], keep=pre_f[4:8])
        _gla_block_pre(qab, kab, zab, wgb_r, bgb_r, True, nc, *pre_b[:4], keep=pre_b[4:8])
        tri_f, tri_b = _tri_masks()
        row_w = lax.broadcasted_iota(jnp.int32, (GLA_CHUNK, HEAD_PAD), 0)

        def one(rev, pre, q_r, k_r, v_r, do_r, s_r, dq_r, dk_r, dv_r, gt, ci):
            qd_s, ki_s, ks_s, dec_s, _, eb_s, enb_s, elb_s, db_s = pre
            tri = tri_b if rev else tri_f
            last_row = 0 if rev else GLA_CHUNK - 1
            rows = pl.ds(pl.multiple_of(ci * GLA_CHUNK, GLA_CHUNK), GLA_CHUNK)
            dec = dec_s[ci]
            heads = range(GLA_HEADS)
            lanes = [slice(HEAD_PAD * h, HEAD_PAD * (h + 1)) for h in heads]
            qd = [qd_s[rows, sl] for sl in lanes]
            ki = [ki_s[rows, sl] for sl in lanes]
            ks = [ks_s[rows, sl] for sl in lanes]
            v = [v_r[rows, sl] for sl in lanes]
            do_h = [_mx(do_r[rows, sl]) for sl in lanes]
            s_t = [s_r[ci, h] for h in heads]
            g_t = [gt[h] for h in heads]
            g_m = [_mx(g_t[h]) for h in heads]
            a = [_dot_nt(qd[h], ki[h]) for h in heads]
            da = [_dot_nt(do_h[h], v[h]) for h in heads]
            dv_carried = [_dot_nt(ks[h], g_m[h]) for h in heads]
            dqd_carried = [_dot(do_h[h], _mx(s_t[h])) for h in heads]
            dks = [_dot(v[h], g_m[h]) for h in heads]
            g_grown = [_dot_tn(do_h[h], qd[h]) for h in heads]
            a = [_mx(jnp.where(tri, a[h], 0.0)) for h in heads]
            da = [_mx(jnp.where(tri, da[h], 0.0)) for h in heads]
            dv_inner = [_dot_tn(a[h], do_h[h]) for h in heads]
            dqd_inner = [_dot(da[h], ki[h]) for h in heads]
            dki = [_dot_tn(da[h], qd[h]) for h in heads]
            dq, dk = [], []
            for h in heads:
                sl = lanes[h]
                dv_r[rows, sl] = (dv_inner[h] + dv_carried[h]).astype(dv_r.dtype)
                ddec = jnp.sum(g_t[h] * s_t[h].astype(F32), axis=0, keepdims=True)
                gt[h] = g_t[h] * dec[:, sl] + g_grown[h]
                dq.append((dqd_inner[h] + dqd_carried[h]) * eb_s[rows, sl] * 0.125)
                dk_state = dks[h] * elb_s[rows, sl]
                dk.append(dki[h] * enb_s[rows, sl] + dk_state)
                k = k_r[rows, sl]
                dblast = jnp.sum(dk_state * k, axis=0, keepdims=True) + dec[:, sl] * ddec
                db_s[rows, sl] = q_r[rows, sl] * dq[h] - k * dk[h] + jnp.where(row_w == last_row, dblast, 0.0)
            low = _low_half(GLA_CHUNK)
            for pair in range(GLA_HEADS // 2):
                psl = slice(HEAD_PAD * pair, HEAD_PAD * (pair + 1))
                for ref, val in ((dq_r, dq), (dk_r, dk)):
                    both = jnp.where(low, val[2 * pair], pltpu.roll(val[2 * pair + 1], 64, 1))
                    ref[rows, psl] = both.astype(ref.dtype)

        def loop(t, carry):
            one(False, pre_f, qaf, kaf, vaf, dof, sf_r, dqf, dkf, dvf, gt_f, nc - 1 - t)
            one(True, pre_b, qab, kab, vab, dob, sb_r, dqb, dkb, dvb, gt_b, t)
            return carry

        lax.fori_loop(0, nc, loop, 0, unroll=True)

        def gate_grads(rev, pre, z_r, w_r, dz_r, dw_r, dbias_r):
            g_s, db_s = pre[4], pre[8]
            back_m = _mx((tri_f if rev else tri_b).astype(F32))
            db = _squeeze_heads(db_s[...])
            dla = jnp.concatenate([_chunk_sums(back_m, db[GLA_CHUNK * c:GLA_CHUNK * (c + 1)]) for c in range(nc)],
                                  axis=0)
            dg = dla * (1.0 / GLA_GATE_NORM) * (1.0 / (1.0 + jnp.exp(g_s[...])))
            dg_m = _mx(dg)
            dz_r[...] = _dot_nt(dg_m, w_r[...])
            dw_r[...] += _dot_tn(_mx(z_r[...]), dg_m)
            dbias_r[...] += jnp.sum(dg, axis=0, keepdims=True)

        gate_grads(False, pre_f, zaf, wgf_r, dzf, dwf, dbf)
        gate_grads(True, pre_b, zab, wgb_r, dzb, dwb, dbb)

    last_first = lambda i: (nb - 1 - i, 0)
    first_last = lambda i: (i, 0)
    s_shape = (nc, GLA_HEADS, HEAD_PAD, HEAD_PAD)

    def ins(m):
        return [pl.BlockSpec((br, hw), m), pl.BlockSpec((br, hw), m), pl.BlockSpec((br, hw), m),
                pl.BlockSpec((br, 128), m), pl.BlockSpec((br, hw), m),
                pl.BlockSpec(s_shape, lambda i: m(i) + (0, 0))]

    def outs(m):
        return [pl.BlockSpec((br, hw // 2), m), pl.BlockSpec((br, hw // 2), m), pl.BlockSpec((br, hw), m),
                pl.BlockSpec((br, 128), m), _full_spec((128, hw // 2)), _full_spec((1, hw // 2))]

    out_shape = [jax.ShapeDtypeStruct((L, hw // 2), MXU_DTYPE)] * 2 + [
        jax.ShapeDtypeStruct((L, hw), MXU_DTYPE),
        jax.ShapeDtypeStruct((L, 128), F32), jax.ShapeDtypeStruct((128, hw // 2), F32),
        jax.ShapeDtypeStruct((1, hw // 2), F32)]
    wspecs = [_full_spec((128, hw // 2)), _full_spec((1, hw // 2))] * 2
    body, extra, extra_specs = _after(body, 16, dep)
    pre_scratch = ([pltpu.VMEM((br, hw), MXU_DTYPE)] * 3 + [pltpu.VMEM((nc, 1, hw), F32)]
                   + [pltpu.VMEM((br, hw // 2), F32)] + [pltpu.VMEM((br, hw), F32)] * 4)
    return pl.pallas_call(
        body, name="gla_bwd", grid=(nb,),
        in_specs=ins(last_first) + ins(first_last) + wspecs + extra_specs,
        out_specs=outs(last_first) + outs(first_last),
        out_shape=out_shape + out_shape,
        scratch_shapes=[pltpu.VMEM(s_shape[1:], F32), pltpu.VMEM(s_shape[1:], F32), pre_scratch, pre_scratch],
        compiler_params=_params(("arbitrary",), VMEM_BIG),
    )(qa, ka, va, za, do, sf, qa, ka, va, za, do, sb, wgf, bgf, wgb, bgb, *extra)


def _t5_buckets(rel):
    nb = REL_BUCKETS // 2
    ret = (rel > 0).astype(np.int32) * nb
    n = np.abs(rel)
    max_exact = nb // 2
    large = max_exact + (np.log(np.maximum(n, 1).astype(np.float32) / max_exact)
                         / math.log(REL_MAX_DIST / max_exact) * (nb - max_exact)).astype(np.int32)
    large = np.minimum(large, nb - 1)
    return ret + np.where(n < max_exact, n, large)


SWA_GROUP = SWA_Q_HEADS // SWA_KV_HEADS
SWA_SPAN = 3 * SWA_BLOCK
SWA_GROUP_LANES = SWA_GROUP * SWA_BLOCK


def _band_buckets():
    s = np.arange(SWA_SPAN)[:, None]
    c = np.arange(SWA_BLOCK)[None, :]
    return _t5_buckets(s - SWA_BLOCK - c).astype(np.int32)


def _swa_valid(n, seq_len):
    key_pos = (n - 1) * SWA_BLOCK + lax.broadcasted_iota(jnp.int32, (SWA_SPAN, 1), 0)
    return (key_pos >= 0) & (key_pos < seq_len)


def _swa_sink_row(sink_r, kv):
    lane = lax.broadcasted_iota(jnp.int32, (1, SWA_GROUP_LANES), 1)
    row = jnp.full((1, SWA_GROUP_LANES), sink_r[kv * SWA_GROUP], F32)
    for g in range(1, SWA_GROUP):
        row = jnp.where(lane >= g * SWA_BLOCK, sink_r[kv * SWA_GROUP + g], row)
    return row


SWA_STEP_BLOCKS = 8


def _swa_group(ref, kv, rows):
    first = kv * SWA_GROUP
    return jnp.concatenate([ref[rows, HEAD_PAD * h:HEAD_PAD * (h + 1)] for h in range(first, first + SWA_GROUP)],
                           axis=0)


def _swa_softmax(scores, bias_t, sink_row, valid):
    st = jnp.where(valid, scores + bias_t, -1e30)
    m = jnp.maximum(jnp.max(st, axis=0, keepdims=True), sink_row)
    p = jnp.exp(st - m)
    e_sink = jnp.exp(sink_row - m)
    inv = 1.0 / (jnp.sum(p, axis=0, keepdims=True) + e_sink)
    return p * inv, e_sink * inv


def _swa_fwd_call(qs, ks, vs, bias, sink, dep=None):
    L = qs.shape[0]

    def block(n, rows, q_r, k_r, v_r, bias_r, sink_r, o_r):
        span = pl.ds(pl.multiple_of(n * SWA_BLOCK, SWA_BLOCK), SWA_SPAN)
        valid = _swa_valid(n, L)
        groups = range(SWA_KV_HEADS)
        lanes = [slice(HEAD_PAD * kv, HEAD_PAD * (kv + 1)) for kv in groups]
        scores = [_dot_nt(k_r[span, lanes[kv]], _swa_group(q_r, kv, rows)) for kv in groups]
        probs = [_swa_softmax(scores[kv], bias_r[kv], _swa_sink_row(sink_r, kv), valid)[0] for kv in groups]
        low = _low_half(SWA_BLOCK)
        for kv in groups:
            og = _dot_tn(_mx(probs[kv]), v_r[span, lanes[kv]])
            for pair in range(SWA_GROUP // 2):
                even = og[2 * SWA_BLOCK * pair:2 * SWA_BLOCK * pair + SWA_BLOCK]
                odd = og[2 * SWA_BLOCK * pair + SWA_BLOCK:2 * SWA_BLOCK * (pair + 1)]
                first = HEAD_PAD * (kv * SWA_GROUP // 2 + pair)
                o_r[rows, first:first + HEAD_PAD] = jnp.where(low, even, pltpu.roll(odd, 64, 1)).astype(o_r.dtype)

    def body(*refs):
        for j in range(SWA_STEP_BLOCKS):
            block(SWA_STEP_BLOCKS * pl.program_id(0) + j, slice(SWA_BLOCK * j, SWA_BLOCK * (j + 1)), *refs)

    qw = SWA_Q_HEADS * HEAD_PAD
    tm = SWA_STEP_BLOCKS * SWA_BLOCK
    body, extra, extra_specs = _after(body, 5, dep)
    return pl.pallas_call(
        body, name="swa_fwd", grid=(L // tm,),
        in_specs=[_row_spec(tm, qw), _vmem_spec(), _vmem_spec(), _vmem_spec(),
                  pl.BlockSpec(memory_space=pltpu.SMEM)] + extra_specs,
        out_specs=_row_spec(tm, qw // 2),
        out_shape=jax.ShapeDtypeStruct((L, qw // 2), MXU_DTYPE),
        compiler_params=_params(("arbitrary",), VMEM_BIG),
    )(qs, ks, vs, bias, sink, *extra)


def _swa_bwd_call(qs, ks, vs, bias, sink, do, dep=None):
    L = qs.shape[0]
    qw = SWA_Q_HEADS * HEAD_PAD
    kw = SWA_KV_HEADS * HEAD_PAD

    def body(*refs):
        dk_r, dv_r, dbias_r, dsink_r = refs[7:]

        @pl.when(pl.program_id(0) == 0)
        def _():
            for ref in (dk_r, dv_r, dbias_r, dsink_r):
                ref[...] = jnp.zeros_like(ref)

        for j in range(SWA_STEP_BLOCKS):
            block(SWA_STEP_BLOCKS * pl.program_id(0) + j, slice(SWA_BLOCK * j, SWA_BLOCK * (j + 1)), *refs)

    def block(n, rows, q_r, k_r, v_r, bias_r, sink_r, do_r, dq_r, dk_r, dv_r, dbias_r, dsink_r):
        span = pl.ds(pl.multiple_of(n * SWA_BLOCK, SWA_BLOCK), SWA_SPAN)
        valid = _swa_valid(n, L)
        groups = range(SWA_KV_HEADS)
        lanes = [slice(HEAD_PAD * kv, HEAD_PAD * (kv + 1)) for kv in groups]
        kk = [k_r[span, sl] for sl in lanes]
        vv = [v_r[span, sl] for sl in lanes]
        qg = [_swa_group(q_r, kv, rows) for kv in groups]
        dog = [_swa_group(do_r, kv, rows) for kv in groups]
        scores = [_dot_nt(kk[kv], qg[kv]) for kv in groups]
        dp = [_dot_nt(vv[kv], dog[kv]) for kv in groups]
        probs = [_swa_softmax(scores[kv], bias_r[kv], _swa_sink_row(sink_r, kv), valid) for kv in groups]
        ds_m, pn_m = [], []
        for kv in groups:
            pn, p_sink = probs[kv]
            delta = jnp.sum(pn * dp[kv], axis=0, keepdims=True)
            ds = pn * (dp[kv] - delta)
            dsink_r[kv] -= p_sink * delta
            dbias_r[kv] += ds
            ds_m.append(_mx(ds))
            pn_m.append(_mx(pn))
        dqg = [_dot_tn(ds_m[kv], kk[kv]) * 0.125 for kv in groups]
        dkk = [_dot(ds_m[kv], qg[kv]) for kv in groups]
        dvv = [_dot(pn_m[kv], dog[kv]) for kv in groups]
        low = _low_half(SWA_BLOCK)
        for kv in groups:
            for pair in range(SWA_GROUP // 2):
                even = dqg[kv][2 * SWA_BLOCK * pair:2 * SWA_BLOCK * pair + SWA_BLOCK]
                odd = dqg[kv][2 * SWA_BLOCK * pair + SWA_BLOCK:2 * SWA_BLOCK * (pair + 1)]
                first = HEAD_PAD * (kv * SWA_GROUP // 2 + pair)
                dq_r[rows, first:first + HEAD_PAD] = jnp.where(low, even, pltpu.roll(odd, 64, 1)).astype(dq_r.dtype)
            dk_r[span, lanes[kv]] += dkk[kv]
            dv_r[span, lanes[kv]] += dvv[kv]

    tm = SWA_STEP_BLOCKS * SWA_BLOCK
    body, extra, extra_specs = _after(body, 6, dep)
    return pl.pallas_call(
        body, name="swa_bwd", grid=(L // tm,),
        in_specs=[_row_spec(tm, qw), _vmem_spec(), _vmem_spec(), _vmem_spec(),
                  pl.BlockSpec(memory_space=pltpu.SMEM), _row_spec(tm, qw)] + extra_specs,
        out_specs=[_row_spec(tm, qw // 2), _vmem_spec(), _vmem_spec(), _vmem_spec(), _vmem_spec()],
        out_shape=[jax.ShapeDtypeStruct((L, qw // 2), MXU_DTYPE),
                   jax.ShapeDtypeStruct((L + 2 * SWA_BLOCK, kw), F32),
                   jax.ShapeDtypeStruct((L + 2 * SWA_BLOCK, kw), F32),
                   jax.ShapeDtypeStruct((SWA_KV_HEADS, SWA_SPAN, SWA_GROUP_LANES), F32),
                   jax.ShapeDtypeStruct((SWA_KV_HEADS, 1, SWA_GROUP_LANES), F32)],
        compiler_params=_params(("arbitrary",), VMEM_BIG),
    )(qs, ks, vs, bias, sink, do, *extra)


def _bias_call(rel_bias, buckets, dep=None):
    def body(t_r, bk_r, o_r):
        bk = bk_r[...]
        s = lax.broadcasted_iota(jnp.int32, bk.shape, 0)
        c = lax.broadcasted_iota(jnp.int32, bk.shape, 1)
        in_band = jnp.abs(s - SWA_BLOCK - c) <= SWA_BLOCK
        for h in range(SWA_Q_HEADS):
            acc = jnp.zeros(bk.shape, F32)
            for b in range(REL_BUCKETS):
                acc = jnp.where(bk == b, t_r[b, h], acc)
            g = h % SWA_GROUP
            o_r[h // SWA_GROUP, :, SWA_BLOCK * g:SWA_BLOCK * (g + 1)] = jnp.where(in_band, acc, -1e30)

    body, extra, extra_specs = _after(body, 2, dep)
    return pl.pallas_call(
        body, name="band_bias",
        in_specs=[pl.BlockSpec(memory_space=pltpu.SMEM), _vmem_spec()] + extra_specs, out_specs=_vmem_spec(),
        out_shape=jax.ShapeDtypeStruct((SWA_KV_HEADS, SWA_SPAN, SWA_GROUP_LANES), F32),
    )(rel_bias, buckets, *extra)


def _relbias_call(dbias, dsink, buckets, dep=None):
    def body(db_r, ds_r, bk_r, o_r, os_r):
        bk = bk_r[...]
        rowi = lax.broadcasted_iota(jnp.int32, (REL_BUCKETS, 128), 0)
        lanei = lax.broadcasted_iota(jnp.int32, (REL_BUCKETS, 128), 1)
        lane1 = lax.broadcasted_iota(jnp.int32, (1, 128), 1)
        acc = jnp.zeros((REL_BUCKETS, 128), F32)
        acc_sink = jnp.zeros((1, 128), F32)
        heads = [(h // SWA_GROUP, slice(SWA_BLOCK * (h % SWA_GROUP), SWA_BLOCK * (h % SWA_GROUP + 1)))
                 for h in range(SWA_Q_HEADS)]
        for b in range(REL_BUCKETS):
            in_bucket = bk == b
            for h, (kv, lanes) in enumerate(heads):
                s = jnp.sum(jnp.where(in_bucket, db_r[kv, :, lanes], 0.0))
                acc = acc + jnp.where((rowi == b) & (lanei == h), s, 0.0)
        for h, (kv, lanes) in enumerate(heads):
            acc_sink = acc_sink + jnp.where(lane1 == h, jnp.sum(ds_r[kv, :, lanes]), 0.0)
        o_r[...] = acc
        os_r[...] = acc_sink

    body, extra, extra_specs = _after(body, 3, dep)
    return pl.pallas_call(
        body, name="relbias_grad",
        in_specs=[_vmem_spec()] * 3 + extra_specs, out_specs=[_vmem_spec()] * 2,
        out_shape=[jax.ShapeDtypeStruct((REL_BUCKETS, 128), F32), jax.ShapeDtypeStruct((1, 128), F32)],
    )(dbias, dsink, buckets, *extra)


def _mix_call(o_f, o_b, ga, o_s, x, gn, w_out_p, g_post, g_pre2, dep=None):
    L = x.shape[0]
    tm = min(512, L)
    hw = GLA_HEADS * HEAD_PAD

    def body(of_r, ob_r, ga_r, os_r, x_r, gn_r, w_r, gp_r, g2_r, cat_r, mix_r, h1_r, n2_r):
        gn_v = gn_r[...]
        for h in range(GLA_HEADS):
            sl = slice(HEAD_PAD * h, HEAD_PAD * (h + 1))
            oh = of_r[:, sl] + ob_r[:, sl]
            on = oh * _rms_r(oh) * gn_v
            gate = ga_r[:, sl]
            cat_r[:, sl] = (on * (gate * jax.nn.sigmoid(gate))).astype(cat_r.dtype)
        os_v = os_r[...]
        cat_r[:, hw:] = os_v
        mix = _dot(cat_r[:, :hw], w_r[:hw, :]) + _dot(os_v, w_r[hw:, :])
        mix_r[...] = mix
        h1 = x_r[...] + mix * _rms_r(mix) * gp_r[...]
        h1_r[...] = h1
        n2_r[...] = (h1 * _rms_r(h1) * g2_r[...]).astype(n2_r.dtype)

    body, extra, extra_specs = _after(body, 9, dep)
    return pl.pallas_call(
        body, name="mix_fwd", grid=(L // tm,),
        in_specs=[_row_spec(tm, hw), _row_spec(tm, hw), _row_spec(tm, hw), _row_spec(tm, OUT_PAD - hw),
                  _row_spec(tm, D_MODEL), _full_spec((1, HEAD_PAD)), _vmem_spec(),
                  _full_spec((1, D_MODEL)), _full_spec((1, D_MODEL))] + extra_specs,
        out_specs=[_row_spec(tm, OUT_PAD), _row_spec(tm, D_MODEL), _row_spec(tm, D_MODEL), _row_spec(tm, D_MODEL)],
        out_shape=[jax.ShapeDtypeStruct((L, OUT_PAD), MXU_DTYPE), jax.ShapeDtypeStruct((L, D_MODEL), F32),
                   jax.ShapeDtypeStruct((L, D_MODEL), F32), jax.ShapeDtypeStruct((L, D_MODEL), MXU_DTYPE)],
        compiler_params=_params(("arbitrary",), VMEM_BIG),
    )(o_f, o_b, ga, o_s, x, gn, w_out_p, g_post, g_pre2, *extra)


def _mlp_fwd_call(n2, h1, tgt, w_ud, g_post):
    L = n2.shape[0]
    tm = min(512, L)
    blk = D_FF // N_CHIPS

    def body(n2_r, h1_r, t_r, w_r, g_r, a_r, rz_r, dh2_r, dff_r, loss_r, dg_r):
        @pl.when(pl.program_id(0) == 0)
        def _():
            loss_r[...] = jnp.zeros_like(loss_r)
            dg_r[...] = jnp.zeros_like(dg_r)

        n2v = n2_r[...]
        ff = jnp.zeros((tm, D_MODEL), F32)
        for j in range(N_CHIPS):
            sl = slice(blk * j, blk * (j + 1))
            rz = jnp.maximum(_dot(n2v, w_r[j, 0]), 0.0)
            a = _mx(rz * rz)
            rz_r[:, sl] = rz.astype(rz_r.dtype)
            a_r[:, sl] = a
            ff = ff + _dot(a, w_r[j, 1])
        g = g_r[...]
        r = _rms_r(ff)
        err = h1_r[...] + ff * r * g - t_r[...]
        loss_r[...] += 0.5 * jnp.sum(err * err) / D_MODEL
        dh2 = err * (1.0 / D_MODEL)
        dh2_r[...] = dh2
        dff, dg = _rms_bwd(ff, r, g, dh2)
        dff_r[...] = dff.astype(dff_r.dtype)
        dg_r[...] += dg

    return pl.pallas_call(
        body, name="mlp_fwd", grid=(L // tm,),
        in_specs=[_row_spec(tm, D_MODEL), _row_spec(tm, D_MODEL), _row_spec(tm, D_MODEL),
                  _vmem_spec(), _full_spec((1, D_MODEL))],
        out_specs=[_row_spec(tm, D_FF), _row_spec(tm, D_FF), _row_spec(tm, D_MODEL), _row_spec(tm, D_MODEL),
                   _full_spec((1, 128)), _full_spec((1, D_MODEL))],
        out_shape=[jax.ShapeDtypeStruct((L, D_FF), MXU_DTYPE), jax.ShapeDtypeStruct((L, D_FF), MXU_DTYPE),
                   jax.ShapeDtypeStruct((L, D_MODEL), F32), jax.ShapeDtypeStruct((L, D_MODEL), MXU_DTYPE),
                   jax.ShapeDtypeStruct((1, 128), F32), jax.ShapeDtypeStruct((1, D_MODEL), F32)],
        compiler_params=_params(("arbitrary",), VMEM_BIG),
    )(n2, h1, tgt, w_ud, g_post)


def _mix_mlp_fwd_call(o_f, o_b, ga, o_s, x, tgt, gn, w_out_p, g_post, g_pre2, w_ud, g_post2):
    L = x.shape[0]
    tm = min(256, L)
    hw = GLA_HEADS * HEAD_PAD
    blk = D_FF // N_CHIPS

    def body(of_r, ob_r, ga_r, os_r, x_r, t_r, gn_r, w_r, gp_r, g2_r, wud_r, g3_r,
             cat_r, mix_r, h1_r, n2_r, a_r, rz_r, dh2_r, dff_r, loss_r, dg_r):
        @pl.when(pl.program_id(0) == 0)
        def _():
            loss_r[...] = jnp.zeros_like(loss_r)
            dg_r[...] = jnp.zeros_like(dg_r)

        gn_v = gn_r[...]
        for h in range(GLA_HEADS):
            sl = slice(HEAD_PAD * h, HEAD_PAD * (h + 1))
            oh = of_r[:, sl] + ob_r[:, sl]
            on = oh * _rms_r(oh) * gn_v
            gate = ga_r[:, sl]
            cat_r[:, sl] = (on * (gate * jax.nn.sigmoid(gate))).astype(cat_r.dtype)
        os_v = os_r[...]
        cat_r[:, hw:] = os_v
        mix = _dot(cat_r[:, :hw], w_r[:hw, :]) + _dot(os_v, w_r[hw:, :])
        mix_r[...] = mix
        h1 = x_r[...] + mix * _rms_r(mix) * gp_r[...]
        h1_r[...] = h1
        n2v = (h1 * _rms_r(h1) * g2_r[...]).astype(n2_r.dtype)
        n2_r[...] = n2v

        ff = jnp.zeros((tm, D_MODEL), F32)
        for j in range(N_CHIPS):
            sl = slice(blk * j, blk * (j + 1))
            rz = jnp.maximum(_dot(n2v, wud_r[j, 0]), 0.0)
            a = _mx(rz * rz)
            rz_r[:, sl] = rz.astype(rz_r.dtype)
            a_r[:, sl] = a
            ff = ff + _dot(a, wud_r[j, 1])
        g = g3_r[...]
        r = _rms_r(ff)
        err = h1 + ff * r * g - t_r[...]
        loss_r[...] += 0.5 * jnp.sum(err * err) / D_MODEL
        dh2 = err * (1.0 / D_MODEL)
        dh2_r[...] = dh2
        dff, dg = _rms_bwd(ff, r, g, dh2)
        dff_r[...] = dff.astype(dff_r.dtype)
        dg_r[...] += dg

    return pl.pallas_call(
        body, name="mix_mlp_fwd", grid=(L // tm,),
        in_specs=[_row_spec(tm, hw), _row_spec(tm, hw), _row_spec(tm, hw), _row_spec(tm, OUT_PAD - hw),
                  _row_spec(tm, D_MODEL), _row_spec(tm, D_MODEL), _full_spec((1, HEAD_PAD)), _vmem_spec(),
                  _full_spec((1, D_MODEL)), _full_spec((1, D_MODEL)), _vmem_spec(), _full_spec((1, D_MODEL))],
        out_specs=[_row_spec(tm, OUT_PAD), _row_spec(tm, D_MODEL), _row_spec(tm, D_MODEL), _row_spec(tm, D_MODEL),
                   _row_spec(tm, D_FF), _row_spec(tm, D_FF), _row_spec(tm, D_MODEL), _row_spec(tm, D_MODEL),
                   _full_spec((1, 128)), _full_spec((1, D_MODEL))],
        out_shape=[jax.ShapeDtypeStruct((L, OUT_PAD), MXU_DTYPE), jax.ShapeDtypeStruct((L, D_MODEL), F32),
                   jax.ShapeDtypeStruct((L, D_MODEL), F32), jax.ShapeDtypeStruct((L, D_MODEL), MXU_DTYPE),
                   jax.ShapeDtypeStruct((L, D_FF), MXU_DTYPE), jax.ShapeDtypeStruct((L, D_FF), MXU_DTYPE),
                   jax.ShapeDtypeStruct((L, D_MODEL), F32), jax.ShapeDtypeStruct((L, D_MODEL), MXU_DTYPE),
                   jax.ShapeDtypeStruct((1, 128), F32), jax.ShapeDtypeStruct((1, D_MODEL), F32)],
        compiler_params=_params(("arbitrary",), VMEM_BIG),
    )(o_f, o_b, ga, o_s, x, tgt, gn, w_out_p, g_post, g_pre2, w_ud, g_post2)


def _mlp_bwd_call(dff, rz, w_ud):
    L = dff.shape[0]
    tm = min(512, L)
    blk = D_FF // N_CHIPS

    def body(dff_r, rz_r, w_r, dz_r, dn2_r):
        dffv = dff_r[...]
        dn2 = jnp.zeros((tm, D_MODEL), F32)
        for j in range(N_CHIPS):
            sl = slice(blk * j, blk * (j + 1))
            dz = _mx(_dot_nt(dffv, w_r[j, 1]) * 2.0 * rz_r[:, sl].astype(F32))
            dz_r[:, sl] = dz
            dn2 = dn2 + _dot_nt(dz, w_r[j, 0])
        dn2_r[...] = dn2

    return pl.pallas_call(
        body, name="mlp_bwd", grid=(L // tm,),
        in_specs=[_row_spec(tm, D_MODEL), _row_spec(tm, D_FF), _vmem_spec()],
        out_specs=[_row_spec(tm, D_FF), _row_spec(tm, D_MODEL)],
        out_shape=[jax.ShapeDtypeStruct((L, D_FF), MXU_DTYPE), jax.ShapeDtypeStruct((L, D_MODEL), F32)],
        compiler_params=_params(("arbitrary",), VMEM_BIG),
    )(dff, rz, w_ud)


def _mlp_wgrad_call(a, dff, n2, dz):
    L = a.shape[0]
    tf = 512
    per = (D_FF // N_CHIPS) // tf

    def body(a_r, dff_r, n2_r, dz_r, dwd_r, dwu_r):
        dwd_r[...] = _dot_tn(a_r[...], dff_r[...])
        dwu_r[...] = _dot_tn(n2_r[...], dz_r[...])

    return pl.pallas_call(
        body, name="mlp_wgrad", grid=(D_FF // tf,),
        in_specs=[pl.BlockSpec((L, tf), lambda j: (0, j)), _vmem_spec(), _vmem_spec(),
                  pl.BlockSpec((L, tf), lambda j: (0, j))],
        out_specs=[pl.BlockSpec((tf, D_MODEL), lambda j: (j, 0)),
                   pl.BlockSpec((None, D_MODEL, tf), lambda j: (j // per, 0, j % per))],
        out_shape=[jax.ShapeDtypeStruct((D_FF, D_MODEL), F32),
                   jax.ShapeDtypeStruct((N_CHIPS, D_MODEL, D_FF // N_CHIPS), F32)],
        compiler_params=_params(("arbitrary",), VMEM_BIG),
    )(a, dff, n2, dz)


def _mix_bwd_call(dn2, dh2, h1, mix, cat, o_f, o_b, ga, gn, g_post, g_pre2, w_out_p):
    L = dn2.shape[0]
    tm = min(512, L)
    hw = GLA_HEADS * HEAD_PAD

    def body(dn2_r, dh2_r, h1_r, mix_r, cat_r, of_r, ob_r, ga_r, gn_r, gp_r, g2_r, w_r,
             dh1_r, do_r, dga_r, dos_r, dw_r, dg2_r, dgp_r, dgn_r):
        @pl.when(pl.program_id(0) == 0)
        def _():
            for ref in (dw_r, dg2_r, dgp_r, dgn_r):
                ref[...] = jnp.zeros_like(ref)

        parts = [slice(start, start + min(256, tm)) for start in range(0, tm, 256)]
        dmix_m = []
        for rs in parts:
            h1 = h1_r[rs, :]
            dx2, dg2 = _rms_bwd(h1, _rms_r(h1), g2_r[...], dn2_r[rs, :])
            dh1 = dh2_r[rs, :] + dx2
            dh1_r[rs, :] = dh1
            dg2_r[...] += dg2
            mix = mix_r[rs, :]
            dmix, dgp = _rms_bwd(mix, _rms_r(mix), gp_r[...], dh1)
            dgp_r[...] += dgp
            dmix_m.append(_mx(dmix))
        dcat = [_dot_nt(d, w_r[...]) for d in dmix_m]
        for rs, d in zip(parts, dmix_m):
            dw_r[...] += _dot_tn(cat_r[rs, :], d)
        gn_v = gn_r[...]
        dgn = jnp.zeros((1, HEAD_PAD), F32)
        for rs, dc in zip(parts, dcat):
            dos_r[rs, :] = _spread_heads(dc[:, hw:]).astype(dos_r.dtype)
            for h in range(GLA_HEADS):
                sl = slice(HEAD_PAD * h, HEAD_PAD * (h + 1))
                oh = of_r[rs, sl] + ob_r[rs, sl]
                rr = _rms_r(oh)
                xh = oh * rr
                gate = ga_r[rs, sl]
                sg = jax.nn.sigmoid(gate)
                silu = gate * sg
                doa = dc[:, sl]
                dga_r[rs, sl] = (doa * (xh * gn_v) * (sg + silu * (1.0 - sg))).astype(dga_r.dtype)
                don = doa * silu
                gd = don * gn_v
                do_r[rs, sl] = rr * (gd - xh * jnp.mean(gd * xh, axis=-1, keepdims=True))
                dgn = dgn + jnp.sum(don * xh, axis=0, keepdims=True)
        dgn_r[...] += dgn

    return pl.pallas_call(
        body, name="mix_bwd", grid=(L // tm,),
        in_specs=[_row_spec(tm, D_MODEL)] * 4 + [_row_spec(tm, OUT_PAD)] + [_row_spec(tm, hw)] * 3
        + [_full_spec((1, HEAD_PAD)), _full_spec((1, D_MODEL)), _full_spec((1, D_MODEL)), _vmem_spec()],
        out_specs=[_row_spec(tm, D_MODEL), _row_spec(tm, hw), _row_spec(tm, hw),
                   _row_spec(tm, SWA_Q_HEADS * HEAD_PAD),
                   _full_spec((OUT_PAD, D_MODEL)), _full_spec((1, D_MODEL)), _full_spec((1, D_MODEL)),
                   _full_spec((1, HEAD_PAD))],
        out_shape=[jax.ShapeDtypeStruct((L, D_MODEL), F32), jax.ShapeDtypeStruct((L, hw), F32),
                   jax.ShapeDtypeStruct((L, hw), MXU_DTYPE),
                   jax.ShapeDtypeStruct((L, SWA_Q_HEADS * HEAD_PAD), MXU_DTYPE),
                   jax.ShapeDtypeStruct((OUT_PAD, D_MODEL), F32), jax.ShapeDtypeStruct((1, D_MODEL), F32),
                   jax.ShapeDtypeStruct((1, D_MODEL), F32), jax.ShapeDtypeStruct((1, HEAD_PAD), F32)],
        compiler_params=_params(("arbitrary",), VMEM_BIG),
    )(dn2, dh2, h1, mix, cat, o_f, o_b, ga, gn, g_post, g_pre2, w_out_p)


def _mlp_mix_bwd_call(dff, rz, w_ud, dh2, h1, mix, cat, o_f, o_b, ga, gn, g_post, g_pre2, w_out_p):
    L = dff.shape[0]
    tm = min(256, L)
    hw = GLA_HEADS * HEAD_PAD
    blk = D_FF // N_CHIPS

    def body(dff_r, rz_r, wud_r, dh2_r, h1_r, mix_r, cat_r, of_r, ob_r, ga_r, gn_r, gp_r, g2_r, w_r,
             dz_r, dh1_r, do_r, dga_r, dos_r, dw_r, dg2_r, dgp_r, dgn_r):
        @pl.when(pl.program_id(0) == 0)
        def _():
            for ref in (dw_r, dg2_r, dgp_r, dgn_r):
                ref[...] = jnp.zeros_like(ref)

        dffv = dff_r[...]
        dn2 = jnp.zeros((tm, D_MODEL), F32)
        for j in range(N_CHIPS):
            sl = slice(blk * j, blk * (j + 1))
            dz = _mx(_dot_nt(dffv, wud_r[j, 1]) * 2.0 * rz_r[:, sl].astype(F32))
            dz_r[:, sl] = dz
            dn2 = dn2 + _dot_nt(dz, wud_r[j, 0])

        h1 = h1_r[...]
        dx2, dg2 = _rms_bwd(h1, _rms_r(h1), g2_r[...], dn2)
        dh1 = dh2_r[...] + dx2
        dh1_r[...] = dh1
        dg2_r[...] += dg2
        mix = mix_r[...]
        dmix, dgp = _rms_bwd(mix, _rms_r(mix), gp_r[...], dh1)
        dgp_r[...] += dgp
        dmix_m = _mx(dmix)
        dc = _dot_nt(dmix_m, w_r[...])
        dw_r[...] += _dot_tn(cat_r[...], dmix_m)
        gn_v = gn_r[...]
        dgn = jnp.zeros((1, HEAD_PAD), F32)
        dos_r[...] = _spread_heads(dc[:, hw:]).astype(dos_r.dtype)
        for h in range(GLA_HEADS):
            sl = slice(HEAD_PAD * h, HEAD_PAD * (h + 1))
            oh = of_r[:, sl] + ob_r[:, sl]
            rr = _rms_r(oh)
            xh = oh * rr
            gate = ga_r[:, sl]
            sg = jax.nn.sigmoid(gate)
            silu = gate * sg
            doa = dc[:, sl]
            dga_r[:, sl] = (doa * (xh * gn_v) * (sg + silu * (1.0 - sg))).astype(dga_r.dtype)
            don = doa * silu
            gd = don * gn_v
            do_r[:, sl] = rr * (gd - xh * jnp.mean(gd * xh, axis=-1, keepdims=True))
            dgn = dgn + jnp.sum(don * xh, axis=0, keepdims=True)
        dgn_r[...] += dgn

    return pl.pallas_call(
        body, name="mlp_mix_bwd", grid=(L // tm,),
        in_specs=[_row_spec(tm, D_MODEL), _row_spec(tm, D_FF), _vmem_spec()] + [_row_spec(tm, D_MODEL)] * 3
        + [_row_spec(tm, OUT_PAD)] + [_row_spec(tm, hw)] * 3
        + [_full_spec((1, HEAD_PAD)), _full_spec((1, D_MODEL)), _full_spec((1, D_MODEL)), _vmem_spec()],
        out_specs=[_row_spec(tm, D_FF), _row_spec(tm, D_MODEL), _row_spec(tm, hw), _row_spec(tm, hw),
                   _row_spec(tm, SWA_Q_HEADS * HEAD_PAD),
                   _full_spec((OUT_PAD, D_MODEL)), _full_spec((1, D_MODEL)), _full_spec((1, D_MODEL)),
                   _full_spec((1, HEAD_PAD))],
        out_shape=[jax.ShapeDtypeStruct((L, D_FF), MXU_DTYPE),
                   jax.ShapeDtypeStruct((L, D_MODEL), F32), jax.ShapeDtypeStruct((L, hw), F32),
                   jax.ShapeDtypeStruct((L, hw), MXU_DTYPE),
                   jax.ShapeDtypeStruct((L, SWA_Q_HEADS * HEAD_PAD), MXU_DTYPE),
                   jax.ShapeDtypeStruct((OUT_PAD, D_MODEL), F32), jax.ShapeDtypeStruct((1, D_MODEL), F32),
                   jax.ShapeDtypeStruct((1, D_MODEL), F32), jax.ShapeDtypeStruct((1, HEAD_PAD), F32)],
        compiler_params=_params(("arbitrary",), VMEM_BIG),
    )(dff, rz, w_ud, dh2, h1, mix, cat, o_f, o_b, ga, gn, g_post, g_pre2, w_out_p)


def _in_bwd_call(x, dh1, g_pre, w_in_t, pairs, singles, halos, dep=None):
    L = x.shape[0]
    tm = min(512, L)
    per = tm // SWA_BLOCK
    n_pair, n_single, n_halo = len(pairs), len(singles), len(halos)
    groups = [c for c, _ in pairs] + [c for c, _ in singles] + [c for c, _ in halos]

    def body(*refs):
        x_r, dh1_r, g_r, w_r = refs[:4]
        pair_refs = refs[4:4 + 2 * n_pair]
        single_refs = refs[4 + 2 * n_pair:4 + 2 * n_pair + n_single]
        halo_refs = refs[4 + 2 * n_pair + n_single:4 + 2 * n_pair + n_single + per * n_halo]
        dx_r, dw_r, dg_r = refs[4 + 2 * n_pair + n_single + per * n_halo:]

        @pl.when(pl.program_id(0) == 0)
        def _():
            dw_r[...] = jnp.zeros_like(dw_r)
            dg_r[...] = jnp.zeros_like(dg_r)

        xv = x_r[...]
        r = _rms_r(xv)
        g = g_r[...]
        u = _mx(xv * r * g)
        vals = [pair_refs[2 * i][...].astype(F32) + pair_refs[2 * i + 1][...].astype(F32) for i in range(n_pair)]
        vals += [ref[...].astype(F32) for ref in single_refs]
        vals += [jnp.concatenate([ref[...] for ref in halo_refs[per * i:per * (i + 1)]], axis=0)
                 for i in range(n_halo)]
        ds = [_mx(_squeeze_heads(val) if heads else val) for (_, _, heads), val in zip(groups, vals)]
        du = jnp.zeros((tm, D_MODEL), F32)
        for (first, rows, _), d in zip(groups, ds):
            du = du + _dot(d, w_r[first:first + rows, :])
        for (first, rows, _), d in zip(groups, ds):
            dw_r[first:first + rows, :] += _dot_tn(d, u)
        dx, dg = _rms_bwd(xv, r, g, du)
        dx_r[...] = dh1_r[...] + dx
        dg_r[...] += dg

    arrays = [a for _, pr in pairs for a in pr] + [a for _, a in singles]
    specs = [_row_spec(tm, a.shape[1]) for a in arrays]
    for _, a in halos:
        specs += [pl.BlockSpec((SWA_BLOCK, a.shape[1]), lambda i, j=j: (per * i + 1 + j, 0)) for j in range(per)]
        arrays += [a] * per
    body, extra, extra_specs = _after(body, 4 + len(arrays), dep)
    return pl.pallas_call(
        body, name="in_bwd", grid=(L // tm,),
        in_specs=[_row_spec(tm, D_MODEL), _row_spec(tm, D_MODEL), _full_spec((1, D_MODEL)), _vmem_spec()] + specs
        + extra_specs,
        out_specs=[_row_spec(tm, D_MODEL), _full_spec((IN_COLS, D_MODEL)), _full_spec((1, D_MODEL))],
        out_shape=[jax.ShapeDtypeStruct((L, D_MODEL), F32), jax.ShapeDtypeStruct((IN_COLS, D_MODEL), F32),
                   jax.ShapeDtypeStruct((1, D_MODEL), F32)],
        compiler_params=_params(("arbitrary",), VMEM_BIG),
    )(x, dh1, g_pre, w_in_t, *arrays, *extra)


def _adamw_math(w, g, m, v):
    m = ADAM_B1 * m + (1.0 - ADAM_B1) * g
    v = ADAM_B2 * v + (1.0 - ADAM_B2) * (g * g)
    m_hat = m / (1.0 - ADAM_B1 ** ADAM_STEP)
    v_hat = v / (1.0 - ADAM_B2 ** ADAM_STEP)
    delta = -ADAM_LR * (m_hat / (jnp.sqrt(v_hat) + ADAM_EPS) + ADAM_WD * w)
    return delta, m, v


def _adamw_call(w, g, m, v, name, dep=None):
    rows, cols = w.shape
    tr = min(256, rows)

    def body(w_r, g_r, m_r, v_r, g_out_r, d_r, nm_r, nv_r):
        g = g_r[...]
        g_out_r[...] = g
        d_r[...], nm_r[...], nv_r[...] = _adamw_math(w_r[...], g, m_r[...], v_r[...])

    if rows % tr == 0:
        spec, steps = _row_spec(tr, cols), rows // tr
    else:
        spec, steps = pl.BlockSpec((rows, 256), lambda i: (0, i)), cols // 256
    body, extra, extra_specs = _after(body, 4, dep)
    return pl.pallas_call(
        body, name=name, grid=(steps,),
        in_specs=[spec] * 4 + extra_specs, out_specs=[spec] * 4,
        out_shape=[jax.ShapeDtypeStruct(w.shape, F32)] * 4,
        compiler_params=_params(("arbitrary",)),
    )(w, g, m, v, *extra)


def _position():
    return lax.axis_index("x"), lax.axis_index("y"), lax.axis_index("c")


def _other_chips(x, y):
    return [(1 - x, y), (x, 1 - y), (1 - x, 1 - y)]


ROWS, COLS = -2, -1


def _half(ref, which, axis):
    size = ref.shape[axis] // 2
    span = pl.ds(pl.multiple_of(which * size, 16 if axis == ROWS else 128), size)
    index = [slice(None)] * len(ref.shape)
    index[axis] = span
    return ref.at[tuple(index)]


def _quarter(ref, half, which, axis):
    size = ref.shape[axis] // 4
    span = pl.ds(pl.multiple_of((2 * half + which) * size, 16 if axis == ROWS else 128), size)
    index = [slice(None)] * len(ref.shape)
    index[axis] = span
    return ref.at[tuple(index)]


def _first_gather_call(shards, axes, routed):
    n = len(shards)
    per = 7

    def body(*refs):
        srcs, outs = refs[:n], refs[n:2 * n]
        send_sems, recv_sems, local_sems = refs[2 * n:]
        x, y, c = _position()
        me, sibling = (x, y, c), (x, y, 1 - c)
        x_side, y_side, across = _other_chips(x, y)
        local = [pltpu.make_async_copy(srcs[a], outs[a].at[2 * x + y], local_sems.at[a]) for a in range(n)]
        for cp in local:
            cp.start()

        def copy(a, k, dst, to, src=None):
            return pltpu.make_async_remote_copy(
                src_ref=dst if src is None else src, dst_ref=dst, send_sem=send_sems.at[per * a + k],
                recv_sem=recv_sems.at[per * a + k], device_id=to, device_id_type=MESH_ID)

        def half(a, chip, pc):
            return _half(outs[a].at[2 * chip[0] + chip[1]], pc, axes[a])

        def quarter(a, chip, q):
            return _quarter(outs[a].at[2 * chip[0] + chip[1]], c, q, axes[a])

        sends = []
        for a in range(n):
            mine = _half(srcs[a], c, axes[a])
            targets = (x_side, y_side) if routed[a] else (x_side, y_side, across)
            sends += [copy(a, j, half(a, (x, y), c), (*chip, c), src=mine) for j, chip in enumerate(targets)]
        for cp in sends:
            cp.start()
        for a in range(n):
            for j, chip in enumerate((x_side, y_side)):
                copy(a, j, half(a, chip, c), me).wait_recv()
                if routed[a]:
                    other = (y_side, x_side)[j]
                    sends.append(copy(a, 2 + j, quarter(a, chip, j), (*other, c)))
                    sends[-1].start()
                sends.append(copy(a, 4 + j, half(a, chip, c), sibling))
                sends[-1].start()
        for a in range(n):
            if routed[a]:
                for j in range(2):
                    copy(a, 2 + j, quarter(a, across, j), me).wait_recv()
            else:
                copy(a, 2, half(a, across, c), me).wait_recv()
            sends.append(copy(a, 6, half(a, across, c), sibling))
            sends[-1].start()
        for a in range(n):
            for k, chip in ((4, x_side), (5, y_side), (6, across)):
                copy(a, k, half(a, chip, 1 - c), me).wait_recv()
        for cp in sends:
            cp.wait_send()
        for cp in local:
            cp.wait()

    return pl.pallas_call(
        body, name="first_gather",
        in_specs=[_any_spec()] * n, out_specs=[_any_spec()] * n,
        out_shape=[jax.ShapeDtypeStruct((N_CHIPS,) + s.shape, s.dtype) for s in shards],
        scratch_shapes=[pltpu.SemaphoreType.DMA((per * n,)), pltpu.SemaphoreType.DMA((per * n,)),
                        pltpu.SemaphoreType.DMA((n,))],
    )(*shards)


PAIR_PEERS, CHIP_PEERS = 1, 2


def _peers(which):
    x, y, c = _position()
    if which == PAIR_PEERS:
        return [(x, y, 1 - c)]
    return [(px, py, c) for px, py in _other_chips(x, y)]


def _split_start(name, arrays, n_copies, plan, peers=None):
    n = len(arrays)

    def body(*refs):
        ins, send_sems, recv_sems, token = refs[:n], refs[n], refs[n + 1], refs[-1]
        if peers is not None:
            barrier = pltpu.get_barrier_semaphore()
            targets = _peers(peers)
            for target in targets:
                pl.semaphore_signal(barrier, inc=1, device_id=target, device_id_type=MESH_ID)
            pl.semaphore_wait(barrier, len(targets))
        for k, (src, dst, to, _) in enumerate(plan(ins)):
            pltpu.make_async_remote_copy(src_ref=src, dst_ref=dst, send_sem=send_sems.at[k],
                                         recv_sem=recv_sems.at[k], device_id=to, device_id_type=MESH_ID).start()
        token[...] = jnp.zeros_like(token)

    hbm = pl.BlockSpec(memory_space=pltpu.HBM)
    sem = pl.BlockSpec(memory_space=pltpu.SEMAPHORE)
    out = pl.pallas_call(
        body, name=name,
        out_shape=(pltpu.SemaphoreType.DMA((n_copies,)), pltpu.SemaphoreType.DMA((n_copies,)))
        + tuple(pltpu.HBM(a.shape, a.dtype) for a in arrays) + (jax.ShapeDtypeStruct((8, 128), F32),),
        in_specs=[hbm] * n, out_specs=(sem, sem) + (hbm,) * n + (_vmem_spec(),),
        input_output_aliases={i: 2 + i for i in range(n)},
        compiler_params=pltpu.CompilerParams(has_side_effects=pltpu.SideEffectType.DATAFLOW_SIDE_EFFECTING,
                                             collective_id=peers),
    )(*[pltpu.with_memory_space_constraint(a, pltpu.HBM) for a in arrays])
    return (out[0], out[1], tuple(out[2:2 + n])), out[-1]


def _split_wait(name, handle, n_copies, plan, after):
    send_sems, recv_sems, arrays = handle
    n = len(arrays)

    def body(*refs):
        ins, s_sems, r_sems = refs[:n], refs[n], refs[n + 1]
        for k, (src, dst, to, landed) in enumerate(plan(ins)):
            cp = pltpu.make_async_remote_copy(src_ref=src, dst_ref=landed, send_sem=s_sems.at[k],
                                              recv_sem=r_sems.at[k], device_id=to, device_id_type=MESH_ID)
            cp.wait_send()
            cp.wait_recv()

    hbm = pl.BlockSpec(memory_space=pltpu.HBM)
    sem = pl.BlockSpec(memory_space=pltpu.SEMAPHORE)
    out = pl.pallas_call(
        body, name=name,
        out_shape=tuple(pltpu.HBM(a.shape, a.dtype) for a in arrays),
        in_specs=[hbm] * n + [sem, sem, _any_spec()], out_specs=(hbm,) * n,
        input_output_aliases={i: i for i in range(n)},
        compiler_params=pltpu.CompilerParams(has_side_effects=pltpu.SideEffectType.DATAFLOW_SIDE_EFFECTING),
    )(*arrays, send_sems, recv_sems, after)
    return tuple(out)


def _gather_plans(axes):
    n = len(axes)

    def stage_one(refs):
        x, y, c = _position()
        copies = []
        for a, axis in enumerate(axes):
            for px, py in _other_chips(x, y):
                copies.append((_half(refs[a], c, axis), _half(refs[n + a].at[2 * x + y], c, axis),
                               (px, py, c), _half(refs[n + a].at[2 * px + py], c, axis)))
        return copies

    def stage_two(refs):
        x, y, c = _position()
        copies = []
        for a, axis in enumerate(axes):
            for px, py in _other_chips(x, y):
                piece = _half(refs[n + a].at[2 * px + py], c, axis)
                copies.append((piece, piece, (x, y, 1 - c), _half(refs[n + a].at[2 * px + py], 1 - c, axis)))
        return copies

    return stage_one, stage_two


def _pair_swap_plan(axes):
    n = len(axes)

    def plan(refs):
        x, y, c = _position()
        return [(_half(refs[a], 1 - c, axes[a]), refs[n + a], (x, y, 1 - c), refs[n + a]) for a in range(n)]

    return plan


def _chip_swap_plan(n):
    def plan(refs):
        x, y, c = _position()
        copies = []
        for a in range(n):
            for j, (px, py) in enumerate(_other_chips(x, y)):
                copies.append((refs[a].at[2 * px + py], refs[n + a].at[j], (px, py, c), refs[n + a].at[j]))
        return copies

    return plan


def _pair_join_plan(axes):
    def plan(refs):
        x, y, c = _position()
        copies = []
        for a, axis in enumerate(axes):
            mine = _half(refs[a], c, axis)
            copies.append((mine, mine, (x, y, 1 - c), _half(refs[a], 1 - c, axis)))
        return copies

    return plan


def _pair_add_call(gs, gots, pos, name, axes):
    n = len(gs)

    def body(pos_r, *refs):
        for g_r, got_r, o_r in zip(refs[:n], refs[n:2 * n], refs[2 * n:]):
            o_r[...] = (g_r[...] + got_r[...]).astype(o_r.dtype)

    def mine(axis):
        return (lambda j, p: (j, p[1], 0)) if axis == ROWS else (lambda j, p: (j, 0, p[1]))

    blocks = [(None,) + got.shape[1:] for got in gots]
    return pl.pallas_call(
        body, name=name,
        grid_spec=pltpu.PrefetchScalarGridSpec(
            num_scalar_prefetch=1, grid=(N_CHIPS,),
            in_specs=[pl.BlockSpec(blk, mine(axis)) for blk, axis in zip(blocks, axes)]
            + [pl.BlockSpec(blk, lambda j, p: (j, 0, 0)) for blk in blocks],
            out_specs=[pl.BlockSpec(blk, lambda j, p: (j, 0, 0)) for blk in blocks]),
        out_shape=[jax.ShapeDtypeStruct(got.shape, COMM_DTYPE) for got in gots],
        compiler_params=_params(("arbitrary",), VMEM_BIG),
    )(pos, *gs, *gots)


def _chip_add_call(hsums, gots, pos, name, axes):
    n = len(hsums)
    steps = 2

    def body(pos_r, *refs):
        for own_r, got_r, o_r in zip(refs[:n], refs[n:2 * n], refs[2 * n:]):
            acc = own_r[...].astype(F32)
            for j in range(3):
                acc = acc + got_r[j].astype(F32)
            o_r[...] = acc

    in_specs, got_specs, out_specs, out_shape = [], [], [], []
    for h, axis in zip(hsums, axes):
        if axis == ROWS:
            rows, cols = h.shape[1] // steps, h.shape[2]
            in_specs.append(pl.BlockSpec((None, rows, cols), lambda i, p: (p[0], i, 0)))
            got_specs.append(pl.BlockSpec((3, rows, cols), lambda i, p: (0, i, 0)))
            out_specs.append(pl.BlockSpec((rows, cols), lambda i, p: (p[1] * steps + i, 0)))
            out_shape.append(jax.ShapeDtypeStruct((2 * h.shape[1], cols), F32))
        else:
            rows, cols = h.shape[1], h.shape[2] // steps
            in_specs.append(pl.BlockSpec((None, rows, cols), lambda i, p: (p[0], 0, i)))
            got_specs.append(pl.BlockSpec((3, rows, cols), lambda i, p: (0, 0, i)))
            out_specs.append(pl.BlockSpec((rows, cols), lambda i, p: (0, p[1] * steps + i)))
            out_shape.append(jax.ShapeDtypeStruct((rows, 2 * h.shape[2]), F32))
    return pl.pallas_call(
        body, name=name,
        grid_spec=pltpu.PrefetchScalarGridSpec(
            num_scalar_prefetch=1, grid=(steps,), in_specs=in_specs + got_specs, out_specs=out_specs),
        out_shape=out_shape,
        compiler_params=_params(("arbitrary",), VMEM_BIG),
    )(pos, *hsums, *gots)


SMALL_NAMES = ("norm_mix_pre", "norm_mix_post", "norm_mlp_pre", "norm_mlp_post", "b_gate_fwd", "b_gate_bwd",
               "gla_norm", "swa_sink", "rel_bias")


N_DEVICES = 8


def _small_pack_call(grads, extras):
    operands = list(grads) + list(extras)

    def body(*refs):
        g_refs, (all_a, all_b) = refs[:len(operands)], refs[len(operands):]
        x, y, c = _position()
        me = 4 * x + 2 * y + c
        all_a[me] = jnp.zeros(all_a.shape[1:], F32)
        all_b[me] = jnp.zeros(all_b.shape[1:], F32)
        for i in range(4):
            all_a[me, i:i + 1, :] = g_refs[i][...]
        all_a[me, 4:5, 0:256] = g_refs[4][...]
        all_a[me, 5:6, 0:256] = g_refs[5][...]
        all_a[me, 6:7, 0:128] = g_refs[6][...]
        all_a[me, 7:8, 0:128] = g_refs[7][...]
        all_a[me, 7:8, 128:256] = g_refs[11][...]
        all_b[me, 0:32, 0:128] = g_refs[8][...]
        all_b[me, 32:48, :] = g_refs[9][...]
        all_b[me, 48:64, :] = g_refs[10][...]

    out_shape = [jax.ShapeDtypeStruct((N_DEVICES, 8, D_MODEL), F32), jax.ShapeDtypeStruct((N_DEVICES, 64, 256), F32)]
    return pl.pallas_call(
        body, name="small_pack",
        in_specs=[_whole_spec(a.shape) for a in operands], out_specs=[_whole_spec(s.shape) for s in out_shape],
        out_shape=out_shape,
    )(*operands)


def _everyone_plan(n):
    def plan(refs):
        x, y, c = _position()
        copies = []
        for k in range(1, N_DEVICES):
            px = 1 - x if (k >> 2) & 1 else x
            py = 1 - y if (k >> 1) & 1 else y
            pc = 1 - c if k & 1 else c
            for a in range(n):
                mine = refs[a].at[4 * x + 2 * y + c]
                copies.append((mine, mine, (px, py, pc), refs[a].at[4 * px + 2 * py + pc]))
        return copies

    return plan


def _small_adamw_call(all_a, all_b, params):
    n_small = len(SMALL_NAMES)
    wmv = [t for p in params for t in p]
    shapes = [p[0].shape for p in params]

    def body(*refs):
        all_a, all_b = refs[:2]
        wmv_refs = refs[2:2 + 3 * n_small]
        out_refs = refs[2 + 3 * n_small:]
        sum_a, sum_b = all_a[0], all_b[0]
        for d in range(1, N_DEVICES):
            sum_a = sum_a + all_a[d]
            sum_b = sum_b + all_b[d]
        gsum = [sum_a[0:1], sum_a[1:2], sum_a[2:3], sum_a[3:4], sum_a[4:5, 0:256], sum_a[5:6, 0:256],
                sum_a[6:7, 0:128], sum_a[7:8, 0:SWA_Q_HEADS], sum_b[0:32, 0:SWA_Q_HEADS]]
        for i in range(n_small):
            w_r, m_r, v_r = wmv_refs[3 * i:3 * i + 3]
            delta, new_m, new_v = _adamw_math(w_r[...], gsum[i], m_r[...], v_r[...])
            out_refs[4 * i][...] = gsum[i]
            out_refs[4 * i + 1][...] = delta
            out_refs[4 * i + 2][...] = new_m
            out_refs[4 * i + 3][...] = new_v
        out_refs[4 * n_small][...] = sum_b[32:48]
        out_refs[4 * n_small + 1][...] = sum_b[48:64]
        out_refs[4 * n_small + 2][...] = sum_a[7:8, 128:256]

    out_shape = [jax.ShapeDtypeStruct(s, F32) for s in shapes for _ in range(4)]
    out_shape += [jax.ShapeDtypeStruct((GLA_GATE_RANK, 256), F32)] * 2 + [jax.ShapeDtypeStruct((1, 128), F32)]
    out = pl.pallas_call(
        body, name="small_adamw",
        in_specs=[_whole_spec(a.shape) for a in [all_a, all_b] + wmv],
        out_specs=[_whole_spec(s.shape) for s in out_shape],
        out_shape=out_shape,
    )(all_a, all_b, *wmv)
    per_name = [tuple(out[4 * i:4 * i + 4]) for i in range(n_small)]
    return per_name, out[4 * n_small], out[4 * n_small + 1], out[4 * n_small + 2]


def _pad_gate(w, first_row):
    return jnp.pad(w, ((first_row, 128 - GLA_GATE_RANK - first_row), (0, 0)))


def _own_slot(shard, chip):
    zone = lax.empty((N_CHIPS,) + shard.shape, shard.dtype)
    return lax.dynamic_update_slice(zone, shard[None], (chip,) + (0,) * shard.ndim)


def _reduce_to_owners(grads, axes, pos, tag, overlap):
    n = len(grads)

    def half_shape(g, axis):
        return (N_CHIPS, g.shape[1] // 2, g.shape[2]) if axis == ROWS else (N_CHIPS, g.shape[1], g.shape[2] // 2)

    lands = [lax.empty(half_shape(g, axis), F32) for g, axis in zip(grads, axes)]
    handle, token = _split_start(tag + "_pair_start", list(grads) + lands, n, _pair_swap_plan(axes), PAIR_PEERS)
    got = _split_wait(tag + "_pair_wait", handle, n, _pair_swap_plan(axes), overlap[0](token))
    sums = list(_pair_add_call(got[:n], got[n:], pos, tag + "_pair_add", axes))
    lands = [lax.empty((3,) + s.shape[1:], s.dtype) for s in sums]
    handle, token = _split_start(tag + "_chip_start", sums + lands, 3 * n, _chip_swap_plan(n), CHIP_PEERS)
    got = _split_wait(tag + "_chip_wait", handle, 3 * n, _chip_swap_plan(n), overlap[1](token))
    halves = list(_chip_add_call(got[:n], got[n:], pos, tag + "_chip_add", axes))
    handle, token = _split_start(tag + "_join_start", halves, n, _pair_join_plan(axes), PAIR_PEERS)
    return _split_wait(tag + "_join_wait", handle, n, _pair_join_plan(axes), overlap[2](token))


def kernel(x, norm_mix_pre, w_in, w_gate_up_fwd, b_gate_fwd, w_gate_up_bwd, b_gate_bwd, gla_norm, swa_sink, rel_bias, w_out, norm_mix_post, norm_mlp_pre, w_up, w_down, norm_mlp_post, loss_target, m_norm_mix_pre, m_w_in, m_w_gate_up_fwd, m_b_gate_fwd, m_w_gate_up_bwd, m_b_gate_bwd, m_gla_norm, m_swa_sink, m_rel_bias, m_w_out, m_norm_mix_post, m_norm_mlp_pre, m_w_up, m_w_down, m_norm_mlp_post, v_norm_mix_pre, v_w_in, v_w_gate_up_fwd, v_b_gate_fwd, v_w_gate_up_bwd, v_b_gate_bwd, v_gla_norm, v_swa_sink, v_rel_bias, v_w_out, v_norm_mix_post, v_norm_mlp_pre, v_w_up, v_w_down, v_norm_mlp_post):
    given = dict(locals())
    cx, cy, cc = _position()
    chip = (2 * cx + cy).astype(jnp.int32)
    pos = jnp.stack([chip, cc.astype(jnp.int32)])
    seq, tgt = x[0], loss_target[0]

    gates = jnp.concatenate([w_gate_up_fwd[0], w_gate_up_bwd[0]], axis=0).astype(COMM_DTYPE)
    all_in, all_gates = _first_gather_call([w_in[0].T.astype(COMM_DTYPE), gates], [COLS, ROWS], [True, False])
    rest = [w_out[0].astype(COMM_DTYPE), jnp.stack([w_up[0], w_down[0]]).astype(COMM_DTYPE)]
    stage_one, stage_two = _gather_plans([ROWS, ROWS])
    handle, token = _split_start("gather_chip_start", rest + [_own_slot(s, chip) for s in rest] + [all_gates], 6,
                                 stage_one, CHIP_PEERS)

    w_in_t = _mx(all_in.reshape(IN_COLS, D_MODEL))
    gates_full = jnp.concatenate([all_gates[j] for j in range(N_CHIPS)], axis=1)
    wgf_p = _mx(_pad_gate(gates_full[:GLA_GATE_RANK], 0))
    wgb_p = _mx(_pad_gate(gates_full[GLA_GATE_RANK:], GLA_GATE_RANK))
    bf_p, bb_p = b_gate_fwd, b_gate_bwd
    buckets = jnp.asarray(_band_buckets())
    sink1 = swa_sink.reshape(SWA_Q_HEADS)

    qa, ka, va, ga, qs, ks, vs, za = _proj_call(seq, norm_mix_pre, w_in_t, dep=token)
    halo = ((SWA_BLOCK, SWA_BLOCK), (0, 0))
    ks_p, vs_p = jnp.pad(ks, halo), jnp.pad(vs, halo)
    o_f, o_b, s_f, s_b = _gla_fwd_call(qa, ka, va, za, wgf_p, bf_p, wgb_p, bb_p)
    bias = _bias_call(rel_bias, buckets, dep=o_f)
    arrays = _split_wait("gather_chip_wait", handle, 6, stage_one, bias)
    handle, token = _split_start("gather_pair_start", list(arrays), 6, stage_two, PAIR_PEERS)
    o_s = _swa_fwd_call(qs, ks_p, vs_p, bias, sink1, dep=token)
    arrays = _split_wait("gather_pair_wait", handle, 6, stage_two, o_s)
    w_out_full = _mx(arrays[2].reshape(N_CHIPS * R_OUT, D_MODEL))
    w_ud = _mx(arrays[3])
    cat, mix, h1, n2, a, rz, dh2, dff, loss, d_post2 = _mix_mlp_fwd_call(
        o_f, o_b, ga, o_s, seq, tgt, gla_norm, w_out_full, norm_mix_post, norm_mlp_pre, w_ud, norm_mlp_post)

    dz, dh1, do, dga, dos, dw_out, d_pre2, d_post, d_gn = _mlp_mix_bwd_call(
        dff, rz, w_ud, dh2, h1, mix, cat, o_f, o_b, ga, gla_norm, norm_mix_post, norm_mlp_pre, w_out_full)
    dw_down, dw_up4 = _mlp_wgrad_call(a, dff, n2, dz)
    done = {}

    def swa_backward(tok):
        done["swa"] = _swa_bwd_call(qs, ks_p, vs_p, bias, sink1, dos, dep=tok)
        return done["swa"][0]

    def gla_in_backward(tok):
        done["gla"] = _gla_bwd_call(qa, ka, va, za, do, s_f, s_b, wgf_p, bf_p, wgb_p, bb_p, dep=tok)
        dqf, dkf, dvf, dzf, _, _, dqb, dkb, dvb, dzb, _, _ = done["gla"]
        dqs, dks_p, dvs_p, _, _ = done["swa"]
        done["in"] = _in_bwd_call(
            seq, dh1, norm_mix_pre, w_in_t,
            pairs=[(_side_by_side(T_QA), (dqf, dqb)), (_side_by_side(T_KA), (dkf, dkb)), (T_VA, (dvf, dvb)),
                   (T_ZA, (dzf, dzb))],
            singles=[(T_GA, dga), (_side_by_side(T_QS), dqs)], halos=[(T_KS, dks_p), (T_VS, dvs_p)])
        return done["in"][0]

    def bias_backward(tok):
        done["rel"] = _relbias_call(done["swa"][3], done["swa"][4], buckets, dep=tok)
        return done["rel"][0]

    g_up, g_down, g_out = _reduce_to_owners(
        [dw_up4, dw_down.reshape(N_CHIPS, R_DOWN, D_MODEL), dw_out.reshape(N_CHIPS, R_OUT, D_MODEL)],
        [ROWS, ROWS, ROWS], pos, "mlp", [swa_backward, gla_in_backward, bias_backward])
    dx, dw_in_t, d_pre = done["in"]
    dwf, dbf, dwb, dbb = done["gla"][4], done["gla"][5], done["gla"][10], done["gla"][11]
    drel, dsink = done["rel"]

    small_grads = [d_pre, d_post, d_pre2, d_post2, dbf, dbb, d_gn, dsink, drel]
    gate_grads = [dwf[:GLA_GATE_RANK], dwb[GLA_GATE_RANK:2 * GLA_GATE_RANK]]
    small_params = [(given[n], given["m_" + n], given["v_" + n]) for n in SMALL_NAMES]
    upd = {}

    everyone = _everyone_plan(2)
    small_handle, small_token = _split_start(
        "small_start", list(_small_pack_call(small_grads, gate_grads + [loss])), 2 * (N_DEVICES - 1), everyone)

    def update_out(tok):
        upd["w_out"] = tuple(_adamw_call(w_out[0], g_out, m_w_out[0], v_w_out[0], "adamw_w_out",
                                         dep=tok + small_token))
        return upd["w_out"][1]

    def update_mlp(tok):
        upd["w_up"] = tuple(_adamw_call(w_up[0], g_up, m_w_up[0], v_w_up[0], "adamw_w_up", dep=tok))
        upd["w_down"] = tuple(
            _adamw_call(w_down[0], g_down, m_w_down[0], v_w_down[0], "adamw_w_down", dep=upd["w_up"][1]))
        all_a, all_b = _split_wait("small_wait", small_handle, 2 * (N_DEVICES - 1), everyone, upd["w_down"][1])
        per_name, done["gf_sum"], done["gb_sum"], upd["loss"] = _small_adamw_call(all_a, all_b, small_params)
        upd.update(dict(zip(SMALL_NAMES, per_name)))
        return per_name[0][1]

    def update_gates(tok):
        for name, total in (("w_gate_up_fwd", done["gf_sum"]), ("w_gate_up_bwd", done["gb_sum"])):
            g = lax.dynamic_slice(total, (0, chip * 64), (GLA_GATE_RANK, 64))
            upd[name] = tuple(_adamw_call(given[name][0], g, given["m_" + name][0], given["v_" + name][0],
                                          "adamw_" + name, dep=tok))
        return upd["w_gate_up_bwd"][1]

    (g_in_t,) = _reduce_to_owners([dw_in_t.reshape(N_CHIPS, R_IN, D_MODEL)], [COLS], pos, "in",
                                  [update_out, update_mlp, update_gates])
    upd["w_in"] = tuple(t.T for t in _adamw_call(w_in[0].T, g_in_t, m_w_in[0].T, v_w_in[0].T, "adamw_w_in"))

    big = ("w_in", "w_gate_up_fwd", "w_gate_up_bwd", "w_out", "w_up", "w_down")
    names = ["norm_mix_pre", "w_in", "w_gate_up_fwd", "b_gate_fwd", "w_gate_up_bwd", "b_gate_bwd", "gla_norm",
             "swa_sink", "rel_bias", "w_out", "norm_mix_post", "norm_mlp_pre", "w_up", "w_down", "norm_mlp_post"]
    outs = [upd["loss"][0, 0], dx[None]]
    for kind in range(4):
        outs += [upd[n][kind][None] if n in big else upd[n][kind] for n in names]
    return tuple(outs)
```

```python
import math

import numpy as np
import jax
import jax.numpy as jnp
from jax import lax
from jax.experimental import pallas as pl
from jax.experimental.pallas import tpu as pltpu

F32 = jnp.float32
MXU_DTYPE = jnp.bfloat16
COMM_DTYPE = jnp.bfloat16

D_MODEL = 1024
D_FF = 4096
N_CHIPS = 4
GLA_HEADS = 4
GLA_CHUNK = 64
GLA_GATE_RANK = 16
GLA_GATE_NORM = 16.0
SWA_Q_HEADS = 8
SWA_KV_HEADS = 2
SWA_BLOCK = 128
REL_BUCKETS = 32
REL_MAX_DIST = 128
NORM_EPS = 1e-6
HEAD_PAD = 128

ADAM_LR = 0.001
ADAM_B1 = 0.9
ADAM_B2 = 0.999
ADAM_EPS = 1e-08
ADAM_WD = 0.01
ADAM_STEP = 10

OUT_PAD = 1024

R_IN, R_OUT, R_DOWN = 584, 256, 1024

VMEM_BIG = 56 * 1024 * 1024
MESH_ID = pl.DeviceIdType.MESH


def _mx(a):
    return a.astype(MXU_DTYPE)


def _dot(a, b):
    return jnp.dot(a, b, preferred_element_type=F32)


def _dot_nt(a, b):
    return lax.dot_general(a, b, (((1,), (1,)), ((), ())), preferred_element_type=F32)


def _dot_tn(a, b):
    return lax.dot_general(a, b, (((0,), (0,)), ((), ())), preferred_element_type=F32)


def _rms_r(x):
    return lax.rsqrt(jnp.mean(x * x, axis=-1, keepdims=True) + NORM_EPS)


def _rms_bwd(x, r, g, dy):
    xh = x * r
    gdy = dy * g
    dx = r * (gdy - xh * jnp.mean(gdy * xh, axis=-1, keepdims=True))
    return dx, jnp.sum(dy * xh, axis=0, keepdims=True)


def _low_half(rows):
    return lax.broadcasted_iota(jnp.int32, (rows, HEAD_PAD), 1) < 64


def _spread_heads(x):
    low = _low_half(x.shape[0])
    parts = []
    for p in range(x.shape[1] // HEAD_PAD):
        pair = x[:, HEAD_PAD * p:HEAD_PAD * (p + 1)]
        parts += [jnp.where(low, pair, 0.0), jnp.where(low, pltpu.roll(pair, 64, 1), 0.0)]
    return jnp.concatenate(parts, axis=1)


def _squeeze_heads(x):
    low = _low_half(x.shape[0])
    parts = []
    for p in range(x.shape[1] // (2 * HEAD_PAD)):
        even = x[:, 2 * HEAD_PAD * p:2 * HEAD_PAD * p + HEAD_PAD]
        odd = x[:, 2 * HEAD_PAD * p + HEAD_PAD:2 * HEAD_PAD * (p + 1)]
        parts.append(jnp.where(low, even, pltpu.roll(odd, 64, 1)))
    return parts[0] if len(parts) == 1 else jnp.concatenate(parts, axis=1)


def _params(sem=None, vmem=None):
    kw = {}
    if sem is not None:
        kw["dimension_semantics"] = sem
    if vmem is not None:
        kw["vmem_limit_bytes"] = vmem
    return pltpu.CompilerParams(**kw)


def _vmem_spec():
    return pl.BlockSpec(memory_space=pltpu.VMEM)


def _whole_spec(shape):
    return pl.BlockSpec(shape, lambda: (0,) * len(shape))


def _row_spec(tm, width):
    return pl.BlockSpec((tm, width), lambda i: (i, 0))


def _full_spec(shape):
    return pl.BlockSpec(shape, lambda i: (0,) * len(shape))


def _any_spec():
    return pl.BlockSpec(memory_space=pl.ANY)


def _after(body, n_in, dep):
    if dep is None:
        return body, [], []
    return (lambda *refs: body(*refs[:n_in], *refs[n_in + 1:])), [dep], [_any_spec()]


T_QA, T_KA, T_VA, T_GA = (0, 256, 4), (256, 256, 4), (512, 512, 0), (1024, 512, 0)
T_QS, T_KS, T_VS = (1568, 512, 8), (2080, 128, 2), (2208, 128, 2)
T_ZA = (1536, 128, 0)
ZA_COLS = 2 * GLA_GATE_RANK
IN_COLS = 2336


def _side_by_side(group):
    return group[0], group[1], 0


def _proj_call(x, g_pre, w_in_t, dep=None):
    L = x.shape[0]
    tm = min(512, L)
    groups = [(T_QA, F32), (T_KA, F32), (T_VA, MXU_DTYPE), (T_GA, F32),
              (T_QS, MXU_DTYPE), (T_KS, MXU_DTYPE), (T_VS, MXU_DTYPE), (T_ZA, F32)]
    widths = [rows * (2 if heads else 1) for (_, rows, heads), _ in groups]

    def body(x_ref, g_ref, w_ref, *outs):
        xv = x_ref[...]
        u = _mx(xv * _rms_r(xv) * g_ref[...])
        for ref, (grp, _) in zip(outs, groups):
            first, rows, heads = grp
            val = _dot_nt(u, w_ref[first:first + rows, :])
            if heads:
                val = _spread_heads(val)
            if grp is T_ZA:
                val = jnp.where(lax.broadcasted_iota(jnp.int32, val.shape, 1) < ZA_COLS, val, 0.0)
            if grp is T_QS:
                val = val * 0.125
            ref[...] = val.astype(ref.dtype)

    body, extra, extra_specs = _after(body, 3, dep)
    return pl.pallas_call(
        body, name="proj_fwd", grid=(L // tm,),
        in_specs=[_row_spec(tm, D_MODEL), _full_spec((1, D_MODEL)), _vmem_spec()] + extra_specs,
        out_specs=[_row_spec(tm, w) for w in widths],
        out_shape=[jax.ShapeDtypeStruct((L, w), dt) for w, (_, dt) in zip(widths, groups)],
        compiler_params=_params(("arbitrary",), VMEM_BIG),
    )(x, g_pre, w_in_t, *extra)


def _tri_masks():
    row = lax.broadcasted_iota(jnp.int32, (GLA_CHUNK, GLA_CHUNK), 0)
    col = lax.broadcasted_iota(jnp.int32, (GLA_CHUNK, GLA_CHUNK), 1)
    return row >= col, row <= col


def _chunk_sums(tri_m, x):
    hi = _mx(x)
    rest = x - hi.astype(F32)
    mid = _mx(rest)
    lo = _mx(rest - mid.astype(F32))
    return _dot(tri_m, hi) + _dot(tri_m, mid) + _dot(tri_m, lo)


def _gla_block_pre(q_r, k_r, z_r, w_r, b_r, rev, nc, qd_s, ki_s, ks_s, dec_s, keep=None):
    tri_f, tri_b = _tri_masks()
    tri_m = _mx((tri_b if rev else tri_f).astype(F32))
    g = _dot(_mx(z_r[...]), w_r[...]) + b_r[...]
    la = (jnp.minimum(g, 0.0) - jnp.log(1.0 + jnp.exp(-jnp.abs(g)))) * (1.0 / GLA_GATE_NORM)
    sums, lasts = [], []
    for c in range(nc):
        b_c = _chunk_sums(tri_m, la[GLA_CHUNK * c:GLA_CHUNK * (c + 1)])
        blast = b_c[0:1] if rev else b_c[GLA_CHUNK - 1:GLA_CHUNK]
        dec_s[c] = _spread_heads(jnp.exp(blast))
        sums.append(b_c)
        lasts.append(jnp.broadcast_to(blast, b_c.shape))
    b = jnp.concatenate(sums, axis=0)
    eb = jnp.exp(b)
    enb = jnp.exp(-b)
    elb = jnp.exp(jnp.concatenate(lasts, axis=0) - b)
    q, k = _squeeze_heads(q_r[...]), _squeeze_heads(k_r[...])
    qd_s[...] = _spread_heads(q * 0.125 * eb).astype(qd_s.dtype)
    ki_s[...] = _spread_heads(k * enb).astype(ki_s.dtype)
    ks_s[...] = _spread_heads(k * elb).astype(ks_s.dtype)
    if keep is not None:
        keep[0][...] = g
        for ref, val in zip(keep[1:], (eb, enb, elb)):
            ref[...] = _spread_heads(val)


def _gla_fwd_call(qa, ka, va, za, wgf, bgf, wgb, bgb):
    L = qa.shape[0]
    br = min(512, L)
    nb, nc, n_chunks = L // br, br // GLA_CHUNK, L // GLA_CHUNK
    hw = GLA_HEADS * HEAD_PAD

    def body(qaf, kaf, vaf, zaf, qab, kab, vab, zab, wgf_r, bgf_r, wgb_r, bgb_r,
             of_r, ob_r, sf_r, sb_r, st_f, st_b, pre_f, pre_b):
        @pl.when(pl.program_id(0) == 0)
        def _():
            st_f[...] = jnp.zeros_like(st_f)
            st_b[...] = jnp.zeros_like(st_b)

        _gla_block_pre(qaf, kaf, zaf, wgf_r, bgf_r, False, nc, *pre_f)
        _gla_block_pre(qab, kab, zab, wgb_r, bgb_r, True, nc, *pre_b)
        tri_f, tri_b = _tri_masks()

        def one(tri, pre, v_r, o_r, s_r, st, ci):
            qd_s, ki_s, ks_s, dec_s = pre
            rows = pl.ds(pl.multiple_of(ci * GLA_CHUNK, GLA_CHUNK), GLA_CHUNK)
            dec = dec_s[ci]
            heads = range(GLA_HEADS)
            lanes = [slice(HEAD_PAD * h, HEAD_PAD * (h + 1)) for h in heads]
            qd = [qd_s[rows, sl] for sl in lanes]
            v = [v_r[rows, sl] for sl in lanes]
            s_t = [st[h] for h in heads]
            a = [_dot_nt(qd[h], ki_s[rows, lanes[h]]) for h in heads]
            carried = [_dot_nt(qd[h], _mx(s_t[h])) for h in heads]
            grown = [_dot_tn(v[h], ks_s[rows, lanes[h]]) for h in heads]
            a = [_mx(jnp.where(tri, a[h], 0.0)) for h in heads]
            inner = [_dot(a[h], v[h]) for h in heads]
            for h in heads:
                s_r[ci, h] = s_t[h].astype(s_r.dtype)
                o_r[rows, lanes[h]] = inner[h] + carried[h]
                st[h] = s_t[h] * dec[:, lanes[h]] + grown[h]

        def loop(t, carry):
            one(tri_f, pre_f, vaf, of_r, sf_r, st_f, t)
            one(tri_b, pre_b, vab, ob_r, sb_r, st_b, nc - 1 - t)
            return carry

        lax.fori_loop(0, nc, loop, 0, unroll=True)

    fwd = lambda i: (i, 0)
    bwd = lambda i: (nb - 1 - i, 0)
    ins = lambda m: [pl.BlockSpec((br, hw), m), pl.BlockSpec((br, hw), m),
                     pl.BlockSpec((br, hw), m), pl.BlockSpec((br, 128), m)]
    wspecs = [_full_spec((128, hw // 2)), _full_spec((1, hw // 2))] * 2
    s_shape = (nc, GLA_HEADS, HEAD_PAD, HEAD_PAD)
    pre_scratch = [pltpu.VMEM((br, hw), MXU_DTYPE)] * 3 + [pltpu.VMEM((nc, 1, hw), F32)]
    return pl.pallas_call(
        body, name="gla_fwd", grid=(nb,),
        in_specs=ins(fwd) + ins(bwd) + wspecs,
        out_specs=[pl.BlockSpec((br, hw), fwd), pl.BlockSpec((br, hw), bwd),
                   pl.BlockSpec(s_shape, lambda i: (i, 0, 0, 0)),
                   pl.BlockSpec(s_shape, lambda i: (nb - 1 - i, 0, 0, 0))],
        out_shape=[jax.ShapeDtypeStruct((L, hw), F32), jax.ShapeDtypeStruct((L, hw), F32),
                   jax.ShapeDtypeStruct((n_chunks,) + s_shape[1:], MXU_DTYPE),
                   jax.ShapeDtypeStruct((n_chunks,) + s_shape[1:], MXU_DTYPE)],
        scratch_shapes=[pltpu.VMEM(s_shape[1:], F32), pltpu.VMEM(s_shape[1:], F32), pre_scratch, pre_scratch],
        compiler_params=_params(("arbitrary",), VMEM_BIG),
    )(qa, ka, va, za, qa, ka, va, za, wgf, bgf, wgb, bgb)


def _gla_bwd_call(qa, ka, va, za, do, sf, sb, wgf, bgf, wgb, bgb, dep=None):
    L = qa.shape[0]
    br = min(512, L)
    nb, nc = L // br, br // GLA_CHUNK
    hw = GLA_HEADS * HEAD_PAD

    def body(qaf, kaf, vaf, zaf, dof, sf_r, qab, kab, vab, zab, dob, sb_r, wgf_r, bgf_r, wgb_r, bgb_r,
             dqf, dkf, dvf, dzf, dwf, dbf, dqb, dkb, dvb, dzb, dwb, dbb, gt_f, gt_b, pre_f, pre_b):
        @pl.when(pl.program_id(0) == 0)
        def _():
            for ref in (gt_f, gt_b, dwf, dbf, dwb, dbb):
                ref[...] = jnp.zeros_like(ref)

        _gla_block_pre(qaf, kaf, zaf, wgf_r, bgf_r, False, nc, *pre_f[:4], keep=pre_f[4:8])
        _gla_block_pre(qab, kab, zab, wgb_r, bgb_r, True, nc, *pre_b[:4], keep=pre_b[4:8])
        tri_f, tri_b = _tri_masks()
        row_w = lax.broadcasted_iota(jnp.int32, (GLA_CHUNK, HEAD_PAD), 0)

        def one(rev, pre, q_r, k_r, v_r, do_r, s_r, dq_r, dk_r, dv_r, gt, ci):
            qd_s, ki_s, ks_s, dec_s, _, eb_s, enb_s, elb_s, db_s = pre
            tri = tri_b if rev else tri_f
            last_row = 0 if rev else GLA_CHUNK - 1
            rows = pl.ds(pl.multiple_of(ci * GLA_CHUNK, GLA_CHUNK), GLA_CHUNK)
            dec = dec_s[ci]
            heads = range(GLA_HEADS)
            lanes = [slice(HEAD_PAD * h, HEAD_PAD * (h + 1)) for h in heads]
            qd = [qd_s[rows, sl] for sl in lanes]
            ki = [ki_s[rows, sl] for sl in lanes]
            ks = [ks_s[rows, sl] for sl in lanes]
            v = [v_r[rows, sl] for sl in lanes]
            do_h = [_mx(do_r[rows, sl]) for sl in lanes]
            s_t = [s_r[ci, h] for h in heads]
            g_t = [gt[h] for h in heads]
            g_m = [_mx(g_t[h]) for h in heads]
            a = [_dot_nt(qd[h], ki[h]) for h in heads]
            da = [_dot_nt(do_h[h], v[h]) for h in heads]
            dv_carried = [_dot_nt(ks[h], g_m[h]) for h in heads]
            dqd_carried = [_dot(do_h[h], _mx(s_t[h])) for h in heads]
            dks = [_dot(v[h], g_m[h]) for h in heads]
            g_grown = [_dot_tn(do_h[h], qd[h]) for h in heads]
            a = [_mx(jnp.where(tri, a[h], 0.0)) for h in heads]
            da = [_mx(jnp.where(tri, da[h], 0.0)) for h in heads]
            dv_inner = [_dot_tn(a[h], do_h[h]) for h in heads]
            dqd_inner = [_dot(da[h], ki[h]) for h in heads]
            dki = [_dot_tn(da[h], qd[h]) for h in heads]
            dq, dk = [], []
            for h in heads:
                sl = lanes[h]
                dv_r[rows, sl] = (dv_inner[h] + dv_carried[h]).astype(dv_r.dtype)
                ddec = jnp.sum(g_t[h] * s_t[h].astype(F32), axis=0, keepdims=True)
                gt[h] = g_t[h] * dec[:, sl] + g_grown[h]
                dq.append((dqd_inner[h] + dqd_carried[h]) * eb_s[rows, sl] * 0.125)
                dk_state = dks[h] * elb_s[rows, sl]
                dk.append(dki[h] * enb_s[rows, sl] + dk_state)
                k = k_r[rows, sl]
                dblast = jnp.sum(dk_state * k, axis=0, keepdims=True) + dec[:, sl] * ddec
                db_s[rows, sl] = q_r[rows, sl] * dq[h] - k * dk[h] + jnp.where(row_w == last_row, dblast, 0.0)
            low = _low_half(GLA_CHUNK)
            for pair in range(GLA_HEADS // 2):
                psl = slice(HEAD_PAD * pair, HEAD_PAD * (pair + 1))
                for ref, val in ((dq_r, dq), (dk_r, dk)):
                    both = jnp.where(low, val[2 * pair], pltpu.roll(val[2 * pair + 1], 64, 1))
                    ref[rows, psl] = both.astype(ref.dtype)

        def loop(t, carry):
            one(False, pre_f, qaf, kaf, vaf, dof, sf_r, dqf, dkf, dvf, gt_f, nc - 1 - t)
            one(True, pre_b, qab, kab, vab, dob, sb_r, dqb, dkb, dvb, gt_b, t)
            return carry

        lax.fori_loop(0, nc, loop, 0, unroll=True)

        def gate_grads(rev, pre, z_r, w_r, dz_r, dw_r, dbias_r):
            g_s, db_s = pre[4], pre[8]
            back_m = _mx((tri_f if rev else tri_b).astype(F32))
            db = _squeeze_heads(db_s[...])
            dla = jnp.concatenate([_chunk_sums(back_m, db[GLA_CHUNK * c:GLA_CHUNK * (c + 1)]) for c in range(nc)],
                                  axis=0)
            dg = dla * (1.0 / GLA_GATE_NORM) * (1.0 / (1.0 + jnp.exp(g_s[...])))
            dg_m = _mx(dg)
            dz_r[...] = _dot_nt(dg_m, w_r[...])
            dw_r[...] += _dot_tn(_mx(z_r[...]), dg_m)
            dbias_r[...] += jnp.sum(dg, axis=0, keepdims=True)

        gate_grads(False, pre_f, zaf, wgf_r, dzf, dwf, dbf)
        gate_grads(True, pre_b, zab, wgb_r, dzb, dwb, dbb)

    last_first = lambda i: (nb - 1 - i, 0)
    first_last = lambda i: (i, 0)
    s_shape = (nc, GLA_HEADS, HEAD_PAD, HEAD_PAD)

    def ins(m):
        return [pl.BlockSpec((br, hw), m), pl.BlockSpec((br, hw), m), pl.BlockSpec((br, hw), m),
                pl.BlockSpec((br, 128), m), pl.BlockSpec((br, hw), m),
                pl.BlockSpec(s_shape, lambda i: m(i) + (0, 0))]

    def outs(m):
        return [pl.BlockSpec((br, hw // 2), m), pl.BlockSpec((br, hw // 2), m), pl.BlockSpec((br, hw), m),
                pl.BlockSpec((br, 128), m), _full_spec((128, hw // 2)), _full_spec((1, hw // 2))]

    out_shape = [jax.ShapeDtypeStruct((L, hw // 2), MXU_DTYPE)] * 2 + [
        jax.ShapeDtypeStruct((L, hw), MXU_DTYPE),
        jax.ShapeDtypeStruct((L, 128), F32), jax.ShapeDtypeStruct((128, hw // 2), F32),
        jax.ShapeDtypeStruct((1, hw // 2), F32)]
    wspecs = [_full_spec((128, hw // 2)), _full_spec((1, hw // 2))] * 2
    body, extra, extra_specs = _after(body, 16, dep)
    pre_scratch = ([pltpu.VMEM((br, hw), MXU_DTYPE)] * 3 + [pltpu.VMEM((nc, 1, hw), F32)]
                   + [pltpu.VMEM((br, hw // 2), F32)] + [pltpu.VMEM((br, hw), F32)] * 4)
    return pl.pallas_call(
        body, name="gla_bwd", grid=(nb,),
        in_specs=ins(last_first) + ins(first_last) + wspecs + extra_specs,
        out_specs=outs(last_first) + outs(first_last),
        out_shape=out_shape + out_shape,
        scratch_shapes=[pltpu.VMEM(s_shape[1:], F32), pltpu.VMEM(s_shape[1:], F32), pre_scratch, pre_scratch],
        compiler_params=_params(("arbitrary",), VMEM_BIG),
    )(qa, ka, va, za, do, sf, qa, ka, va, za, do, sb, wgf, bgf, wgb, bgb, *extra)


def _t5_buckets(rel):
    nb = REL_BUCKETS // 2
    ret = (rel > 0).astype(np.int32) * nb
    n = np.abs(rel)
    max_exact = nb // 2
    large = max_exact + (np.log(np.maximum(n, 1).astype(np.float32) / max_exact)
                         / math.log(REL_MAX_DIST / max_exact) * (nb - max_exact)).astype(np.int32)
    large = np.minimum(large, nb - 1)
    return ret + np.where(n < max_exact, n, large)


SWA_GROUP = SWA_Q_HEADS // SWA_KV_HEADS
SWA_SPAN = 3 * SWA_BLOCK
SWA_GROUP_LANES = SWA_GROUP * SWA_BLOCK


def _band_buckets():
    s = np.arange(SWA_SPAN)[:, None]
    c = np.arange(SWA_BLOCK)[None, :]
    return _t5_buckets(s - SWA_BLOCK - c).astype(np.int32)


def _swa_valid(n, seq_len):
    key_pos = (n - 1) * SWA_BLOCK + lax.broadcasted_iota(jnp.int32, (SWA_SPAN, 1), 0)
    return (key_pos >= 0) & (key_pos < seq_len)


def _swa_sink_row(sink_r, kv):
    lane = lax.broadcasted_iota(jnp.int32, (1, SWA_GROUP_LANES), 1)
    row = jnp.full((1, SWA_GROUP_LANES), sink_r[kv * SWA_GROUP], F32)
    for g in range(1, SWA_GROUP):
        row = jnp.where(lane >= g * SWA_BLOCK, sink_r[kv * SWA_GROUP + g], row)
    return row


SWA_STEP_BLOCKS = 8


def _swa_group(ref, kv, rows):
    first = kv * SWA_GROUP
    return jnp.concatenate([ref[rows, HEAD_PAD * h:HEAD_PAD * (h + 1)] for h in range(first, first + SWA_GROUP)],
                           axis=0)


def _swa_softmax(scores, bias_t, sink_row, valid):
    st = jnp.where(valid, scores + bias_t, -1e30)
    m = jnp.maximum(jnp.max(st, axis=0, keepdims=True), sink_row)
    p = jnp.exp(st - m)
    e_sink = jnp.exp(sink_row - m)
    inv = 1.0 / (jnp.sum(p, axis=0, keepdims=True) + e_sink)
    return p * inv, e_sink * inv


def _swa_fwd_call(qs, ks, vs, bias, sink, dep=None):
    L = qs.shape[0]

    def block(n, rows, q_r, k_r, v_r, bias_r, sink_r, o_r):
        span = pl.ds(pl.multiple_of(n * SWA_BLOCK, SWA_BLOCK), SWA_SPAN)
        valid = _swa_valid(n, L)
        groups = range(SWA_KV_HEADS)
        lanes = [slice(HEAD_PAD * kv, HEAD_PAD * (kv + 1)) for kv in groups]
        scores = [_dot_nt(k_r[span, lanes[kv]], _swa_group(q_r, kv, rows)) for kv in groups]
        probs = [_swa_softmax(scores[kv], bias_r[kv], _swa_sink_row(sink_r, kv), valid)[0] for kv in groups]
        low = _low_half(SWA_BLOCK)
        for kv in groups:
            og = _dot_tn(_mx(probs[kv]), v_r[span, lanes[kv]])
            for pair in range(SWA_GROUP // 2):
                even = og[2 * SWA_BLOCK * pair:2 * SWA_BLOCK * pair + SWA_BLOCK]
                odd = og[2 * SWA_BLOCK * pair + SWA_BLOCK:2 * SWA_BLOCK * (pair + 1)]
                first = HEAD_PAD * (kv * SWA_GROUP // 2 + pair)
                o_r[rows, first:first + HEAD_PAD] = jnp.where(low, even, pltpu.roll(odd, 64, 1)).astype(o_r.dtype)

    def body(*refs):
        for j in range(SWA_STEP_BLOCKS):
            block(SWA_STEP_BLOCKS * pl.program_id(0) + j, slice(SWA_BLOCK * j, SWA_BLOCK * (j + 1)), *refs)

    qw = SWA_Q_HEADS * HEAD_PAD
    tm = SWA_STEP_BLOCKS * SWA_BLOCK
    body, extra, extra_specs = _after(body, 5, dep)
    return pl.pallas_call(
        body, name="swa_fwd", grid=(L // tm,),
        in_specs=[_row_spec(tm, qw), _vmem_spec(), _vmem_spec(), _vmem_spec(),
                  pl.BlockSpec(memory_space=pltpu.SMEM)] + extra_specs,
        out_specs=_row_spec(tm, qw // 2),
        out_shape=jax.ShapeDtypeStruct((L, qw // 2), MXU_DTYPE),
        compiler_params=_params(("arbitrary",), VMEM_BIG),
    )(qs, ks, vs, bias, sink, *extra)


def _swa_bwd_call(qs, ks, vs, bias, sink, do, dep=None):
    L = qs.shape[0]
    qw = SWA_Q_HEADS * HEAD_PAD
    kw = SWA_KV_HEADS * HEAD_PAD

    def body(*refs):
        dk_r, dv_r, dbias_r, dsink_r = refs[7:]

        @pl.when(pl.program_id(0) == 0)
        def _():
            for ref in (dk_r, dv_r, dbias_r, dsink_r):
                ref[...] = jnp.zeros_like(ref)

        for j in range(SWA_STEP_BLOCKS):
            block(SWA_STEP_BLOCKS * pl.program_id(0) + j, slice(SWA_BLOCK * j, SWA_BLOCK * (j + 1)), *refs)

    def block(n, rows, q_r, k_r, v_r, bias_r, sink_r, do_r, dq_r, dk_r, dv_r, dbias_r, dsink_r):
        span = pl.ds(pl.multiple_of(n * SWA_BLOCK, SWA_BLOCK), SWA_SPAN)
        valid = _swa_valid(n, L)
        groups = range(SWA_KV_HEADS)
        lanes = [slice(HEAD_PAD * kv, HEAD_PAD * (kv + 1)) for kv in groups]
        kk = [k_r[span, sl] for sl in lanes]
        vv = [v_r[span, sl] for sl in lanes]
        qg = [_swa_group(q_r, kv, rows) for kv in groups]
        dog = [_swa_group(do_r, kv, rows) for kv in groups]
        scores = [_dot_nt(kk[kv], qg[kv]) for kv in groups]
        dp = [_dot_nt(vv[kv], dog[kv]) for kv in groups]
        probs = [_swa_softmax(scores[kv], bias_r[kv], _swa_sink_row(sink_r, kv), valid) for kv in groups]
        ds_m, pn_m = [], []
        for kv in groups:
            pn, p_sink = probs[kv]
            delta = jnp.sum(pn * dp[kv], axis=0, keepdims=True)
            ds = pn * (dp[kv] - delta)
            dsink_r[kv] -= p_sink * delta
            dbias_r[kv] += ds
            ds_m.append(_mx(ds))
            pn_m.append(_mx(pn))
        dqg = [_dot_tn(ds_m[kv], kk[kv]) * 0.125 for kv in groups]
        dkk = [_dot(ds_m[kv], qg[kv]) for kv in groups]
        dvv = [_dot(pn_m[kv], dog[kv]) for kv in groups]
        low = _low_half(SWA_BLOCK)
        for kv in groups:
            for pair in range(SWA_GROUP // 2):
                even = dqg[kv][2 * SWA_BLOCK * pair:2 * SWA_BLOCK * pair + SWA_BLOCK]
                odd = dqg[kv][2 * SWA_BLOCK * pair + SWA_BLOCK:2 * SWA_BLOCK * (pair + 1)]
                first = HEAD_PAD * (kv * SWA_GROUP // 2 + pair)
                dq_r[rows, first:first + HEAD_PAD] = jnp.where(low, even, pltpu.roll(odd, 64, 1)).astype(dq_r.dtype)
            dk_r[span, lanes[kv]] += dkk[kv]
            dv_r[span, lanes[kv]] += dvv[kv]

    tm = SWA_STEP_BLOCKS * SWA_BLOCK
    body, extra, extra_specs = _after(body, 6, dep)
    return pl.pallas_call(
        body, name="swa_bwd", grid=(L // tm,),
        in_specs=[_row_spec(tm, qw), _vmem_spec(), _vmem_spec(), _vmem_spec(),
                  pl.BlockSpec(memory_space=pltpu.SMEM), _row_spec(tm, qw)] + extra_specs,
        out_specs=[_row_spec(tm, qw // 2), _vmem_spec(), _vmem_spec(), _vmem_spec(), _vmem_spec()],
        out_shape=[jax.ShapeDtypeStruct((L, qw // 2), MXU_DTYPE),
                   jax.ShapeDtypeStruct((L + 2 * SWA_BLOCK, kw), F32),
                   jax.ShapeDtypeStruct((L + 2 * SWA_BLOCK, kw), F32),
                   jax.ShapeDtypeStruct((SWA_KV_HEADS, SWA_SPAN, SWA_GROUP_LANES), F32),
                   jax.ShapeDtypeStruct((SWA_KV_HEADS, 1, SWA_GROUP_LANES), F32)],
        compiler_params=_params(("arbitrary",), VMEM_BIG),
    )(qs, ks, vs, bias, sink, do, *extra)


def _bias_call(rel_bias, buckets, dep=None):
    def body(t_r, bk_r, o_r):
        bk = bk_r[...]
        s = lax.broadcasted_iota(jnp.int32, bk.shape, 0)
        c = lax.broadcasted_iota(jnp.int32, bk.shape, 1)
        in_band = jnp.abs(s - SWA_BLOCK - c) <= SWA_BLOCK
        for h in range(SWA_Q_HEADS):
            acc = jnp.zeros(bk.shape, F32)
            for b in range(REL_BUCKETS):
                acc = jnp.where(bk == b, t_r[b, h], acc)
            g = h % SWA_GROUP
            o_r[h // SWA_GROUP, :, SWA_BLOCK * g:SWA_BLOCK * (g + 1)] = jnp.where(in_band, acc, -1e30)

    body, extra, extra_specs = _after(body, 2, dep)
    return pl.pallas_call(
        body, name="band_bias",
        in_specs=[pl.BlockSpec(memory_space=pltpu.SMEM), _vmem_spec()] + extra_specs, out_specs=_vmem_spec(),
        out_shape=jax.ShapeDtypeStruct((SWA_KV_HEADS, SWA_SPAN, SWA_GROUP_LANES), F32),
    )(rel_bias, buckets, *extra)


def _relbias_call(dbias, dsink, buckets, dep=None):
    def body(db_r, ds_r, bk_r, o_r, os_r):
        bk = bk_r[...]
        rowi = lax.broadcasted_iota(jnp.int32, (REL_BUCKETS, 128), 0)
        lanei = lax.broadcasted_iota(jnp.int32, (REL_BUCKETS, 128), 1)
        lane1 = lax.broadcasted_iota(jnp.int32, (1, 128), 1)
        acc = jnp.zeros((REL_BUCKETS, 128), F32)
        acc_sink = jnp.zeros((1, 128), F32)
        heads = [(h // SWA_GROUP, slice(SWA_BLOCK * (h % SWA_GROUP), SWA_BLOCK * (h % SWA_GROUP + 1)))
                 for h in range(SWA_Q_HEADS)]
        for b in range(REL_BUCKETS):
            in_bucket = bk == b
            for h, (kv, lanes) in enumerate(heads):
                s = jnp.sum(jnp.where(in_bucket, db_r[kv, :, lanes], 0.0))
                acc = acc + jnp.where((rowi == b) & (lanei == h), s, 0.0)
        for h, (kv, lanes) in enumerate(heads):
            acc_sink = acc_sink + jnp.where(lane1 == h, jnp.sum(ds_r[kv, :, lanes]), 0.0)
        o_r[...] = acc
        os_r[...] = acc_sink

    body, extra, extra_specs = _after(body, 3, dep)
    return pl.pallas_call(
        body, name="relbias_grad",
        in_specs=[_vmem_spec()] * 3 + extra_specs, out_specs=[_vmem_spec()] * 2,
        out_shape=[jax.ShapeDtypeStruct((REL_BUCKETS, 128), F32), jax.ShapeDtypeStruct((1, 128), F32)],
    )(dbias, dsink, buckets, *extra)


def _mix_call(o_f, o_b, ga, o_s, x, gn, w_out_p, g_post, g_pre2, dep=None):
    L = x.shape[0]
    tm = min(512, L)
    hw = GLA_HEADS * HEAD_PAD

    def body(of_r, ob_r, ga_r, os_r, x_r, gn_r, w_r, gp_r, g2_r, cat_r, mix_r, h1_r, n2_r):
        gn_v = gn_r[...]
        for h in range(GLA_HEADS):
            sl = slice(HEAD_PAD * h, HEAD_PAD * (h + 1))
            oh = of_r[:, sl] + ob_r[:, sl]
            on = oh * _rms_r(oh) * gn_v
            gate = ga_r[:, sl]
            cat_r[:, sl] = (on * (gate * jax.nn.sigmoid(gate))).astype(cat_r.dtype)
        os_v = os_r[...]
        cat_r[:, hw:] = os_v
        mix = _dot(cat_r[:, :hw], w_r[:hw, :]) + _dot(os_v, w_r[hw:, :])
        mix_r[...] = mix
        h1 = x_r[...] + mix * _rms_r(mix) * gp_r[...]
        h1_r[...] = h1
        n2_r[...] = (h1 * _rms_r(h1) * g2_r[...]).astype(n2_r.dtype)

    body, extra, extra_specs = _after(body, 9, dep)
    return pl.pallas_call(
        body, name="mix_fwd", grid=(L // tm,),
        in_specs=[_row_spec(tm, hw), _row_spec(tm, hw), _row_spec(tm, hw), _row_spec(tm, OUT_PAD - hw),
                  _row_spec(tm, D_MODEL), _full_spec((1, HEAD_PAD)), _vmem_spec(),
                  _full_spec((1, D_MODEL)), _full_spec((1, D_MODEL))] + extra_specs,
        out_specs=[_row_spec(tm, OUT_PAD), _row_spec(tm, D_MODEL), _row_spec(tm, D_MODEL), _row_spec(tm, D_MODEL)],
        out_shape=[jax.ShapeDtypeStruct((L, OUT_PAD), MXU_DTYPE), jax.ShapeDtypeStruct((L, D_MODEL), F32),
                   jax.ShapeDtypeStruct((L, D_MODEL), F32), jax.ShapeDtypeStruct((L, D_MODEL), MXU_DTYPE)],
        compiler_params=_params(("arbitrary",), VMEM_BIG),
    )(o_f, o_b, ga, o_s, x, gn, w_out_p, g_post, g_pre2, *extra)


def _mlp_fwd_call(n2, h1, tgt, w_ud, g_post):
    L = n2.shape[0]
    tm = min(512, L)
    blk = D_FF // N_CHIPS

    def body(n2_r, h1_r, t_r, w_r, g_r, a_r, rz_r, dh2_r, dff_r, loss_r, dg_r):
        @pl.when(pl.program_id(0) == 0)
        def _():
            loss_r[...] = jnp.zeros_like(loss_r)
            dg_r[...] = jnp.zeros_like(dg_r)

        n2v = n2_r[...]
        ff = jnp.zeros((tm, D_MODEL), F32)
        for j in range(N_CHIPS):
            sl = slice(blk * j, blk * (j + 1))
            rz = jnp.maximum(_dot(n2v, w_r[j, 0]), 0.0)
            a = _mx(rz * rz)
            rz_r[:, sl] = rz.astype(rz_r.dtype)
            a_r[:, sl] = a
            ff = ff + _dot(a, w_r[j, 1])
        g = g_r[...]
        r = _rms_r(ff)
        err = h1_r[...] + ff * r * g - t_r[...]
        loss_r[...] += 0.5 * jnp.sum(err * err) / D_MODEL
        dh2 = err * (1.0 / D_MODEL)
        dh2_r[...] = dh2
        dff, dg = _rms_bwd(ff, r, g, dh2)
        dff_r[...] = dff.astype(dff_r.dtype)
        dg_r[...] += dg

    return pl.pallas_call(
        body, name="mlp_fwd", grid=(L // tm,),
        in_specs=[_row_spec(tm, D_MODEL), _row_spec(tm, D_MODEL), _row_spec(tm, D_MODEL),
                  _vmem_spec(), _full_spec((1, D_MODEL))],
        out_specs=[_row_spec(tm, D_FF), _row_spec(tm, D_FF), _row_spec(tm, D_MODEL), _row_spec(tm, D_MODEL),
                   _full_spec((1, 128)), _full_spec((1, D_MODEL))],
        out_shape=[jax.ShapeDtypeStruct((L, D_FF), MXU_DTYPE), jax.ShapeDtypeStruct((L, D_FF), MXU_DTYPE),
                   jax.ShapeDtypeStruct((L, D_MODEL), F32), jax.ShapeDtypeStruct((L, D_MODEL), MXU_DTYPE),
                   jax.ShapeDtypeStruct((1, 128), F32), jax.ShapeDtypeStruct((1, D_MODEL), F32)],
        compiler_params=_params(("arbitrary",), VMEM_BIG),
    )(n2, h1, tgt, w_ud, g_post)


def _mix_mlp_fwd_call(o_f, o_b, ga, o_s, x, tgt, gn, w_out_p, g_post, g_pre2, w_ud, g_post2):
    L = x.shape[0]
    tm = min(256, L)
    hw = GLA_HEADS * HEAD_PAD
    blk = D_FF // N_CHIPS

    def body(of_r, ob_r, ga_r, os_r, x_r, t_r, gn_r, w_r, gp_r, g2_r, wud_r, g3_r,
             cat_r, mix_r, h1_r, n2_r, a_r, rz_r, dh2_r, dff_r, loss_r, dg_r):
        @pl.when(pl.program_id(0) == 0)
        def _():
            loss_r[...] = jnp.zeros_like(loss_r)
            dg_r[...] = jnp.zeros_like(dg_r)

        gn_v = gn_r[...]
        for h in range(GLA_HEADS):
            sl = slice(HEAD_PAD * h, HEAD_PAD * (h + 1))
            oh = of_r[:, sl] + ob_r[:, sl]
            on = oh * _rms_r(oh) * gn_v
            gate = ga_r[:, sl]
            cat_r[:, sl] = (on * (gate * jax.nn.sigmoid(gate))).astype(cat_r.dtype)
        os_v = os_r[...]
        cat_r[:, hw:] = os_v
        mix = _dot(cat_r[:, :hw], w_r[:hw, :]) + _dot(os_v, w_r[hw:, :])
        mix_r[...] = mix
        h1 = x_r[...] + mix * _rms_r(mix) * gp_r[...]
        h1_r[...] = h1
        n2v = (h1 * _rms_r(h1) * g2_r[...]).astype(n2_r.dtype)
        n2_r[...] = n2v

        ff = jnp.zeros((tm, D_MODEL), F32)
        for j in range(N_CHIPS):
            sl = slice(blk * j, blk * (j + 1))
            rz = jnp.maximum(_dot(n2v, wud_r[j, 0]), 0.0)
            a = _mx(rz * rz)
            rz_r[:, sl] = rz.astype(rz_r.dtype)
            a_r[:, sl] = a
            ff = ff + _dot(a, wud_r[j, 1])
        g = g3_r[...]
        r = _rms_r(ff)
        err = h1 + ff * r * g - t_r[...]
        loss_r[...] += 0.5 * jnp.sum(err * err) / D_MODEL
        dh2 = err * (1.0 / D_MODEL)
        dh2_r[...] = dh2
        dff, dg = _rms_bwd(ff, r, g, dh2)
        dff_r[...] = dff.astype(dff_r.dtype)
        dg_r[...] += dg

    return pl.pallas_call(
        body, name="mix_mlp_fwd", grid=(L // tm,),
        in_specs=[_row_spec(tm, hw), _row_spec(tm, hw), _row_spec(tm, hw), _row_spec(tm, OUT_PAD - hw),
                  _row_spec(tm, D_MODEL), _row_spec(tm, D_MODEL), _full_spec((1, HEAD_PAD)), _vmem_spec(),
                  _full_spec((1, D_MODEL)), _full_spec((1, D_MODEL)), _vmem_spec(), _full_spec((1, D_MODEL))],
        out_specs=[_row_spec(tm, OUT_PAD), _row_spec(tm, D_MODEL), _row_spec(tm, D_MODEL), _row_spec(tm, D_MODEL),
                   _row_spec(tm, D_FF), _row_spec(tm, D_FF), _row_spec(tm, D_MODEL), _row_spec(tm, D_MODEL),
                   _full_spec((1, 128)), _full_spec((1, D_MODEL))],
        out_shape=[jax.ShapeDtypeStruct((L, OUT_PAD), MXU_DTYPE), jax.ShapeDtypeStruct((L, D_MODEL), F32),
                   jax.ShapeDtypeStruct((L, D_MODEL), F32), jax.ShapeDtypeStruct((L, D_MODEL), MXU_DTYPE),
                   jax.ShapeDtypeStruct((L, D_FF), MXU_DTYPE), jax.ShapeDtypeStruct((L, D_FF), MXU_DTYPE),
                   jax.ShapeDtypeStruct((L, D_MODEL), F32), jax.ShapeDtypeStruct((L, D_MODEL), MXU_DTYPE),
                   jax.ShapeDtypeStruct((1, 128), F32), jax.ShapeDtypeStruct((1, D_MODEL), F32)],
        compiler_params=_params(("arbitrary",), VMEM_BIG),
    )(o_f, o_b, ga, o_s, x, tgt, gn, w_out_p, g_post, g_pre2, w_ud, g_post2)


def _mlp_bwd_call(dff, rz, w_ud):
    L = dff.shape[0]
    tm = min(512, L)
    blk = D_FF // N_CHIPS

    def body(dff_r, rz_r, w_r, dz_r, dn2_r):
        dffv = dff_r[...]
        dn2 = jnp.zeros((tm, D_MODEL), F32)
        for j in range(N_CHIPS):
            sl = slice(blk * j, blk * (j + 1))
            dz = _mx(_dot_nt(dffv, w_r[j, 1]) * 2.0 * rz_r[:, sl].astype(F32))
            dz_r[:, sl] = dz
            dn2 = dn2 + _dot_nt(dz, w_r[j, 0])
        dn2_r[...] = dn2

    return pl.pallas_call(
        body, name="mlp_bwd", grid=(L // tm,),
        in_specs=[_row_spec(tm, D_MODEL), _row_spec(tm, D_FF), _vmem_spec()],
        out_specs=[_row_spec(tm, D_FF), _row_spec(tm, D_MODEL)],
        out_shape=[jax.ShapeDtypeStruct((L, D_FF), MXU_DTYPE), jax.ShapeDtypeStruct((L, D_MODEL), F32)],
        compiler_params=_params(("arbitrary",), VMEM_BIG),
    )(dff, rz, w_ud)


def _mlp_wgrad_call(a, dff, n2, dz):
    L = a.shape[0]
    tf = 512
    per = (D_FF // N_CHIPS) // tf
    n_steps = D_FF // tf
    depth = 3

    def body(a_hbm, dff_r, n2_r, dz_hbm, dwd_r, dwu_r, a_buf, dz_buf, sems):
        j = pl.program_id(0)

        def copies(block, slot):
            col = pl.multiple_of(block * tf, tf)
            return (pltpu.make_async_copy(a_hbm.at[:, pl.ds(col, tf)], a_buf.at[slot], sems.at[0, slot]),
                    pltpu.make_async_copy(dz_hbm.at[:, pl.ds(col, tf)], dz_buf.at[slot], sems.at[1, slot]))

        @pl.when(j == 0)
        def _():
            for b in range(depth - 1):
                for cp in copies(b, b):
                    cp.start()

        @pl.when(j + (depth - 1) < n_steps)
        def _():
            for cp in copies(j + (depth - 1), (j + (depth - 1)) % depth):
                cp.start()

        slot = j % depth
        for cp in copies(j, slot):
            cp.wait()
        dwd_r[...] = _dot_tn(a_buf[slot], dff_r[...])
        dwu_r[...] = _dot_tn(n2_r[...], dz_buf[slot])

    return pl.pallas_call(
        body, name="mlp_wgrad", grid=(n_steps,),
        in_specs=[_any_spec(), _vmem_spec(), _vmem_spec(), _any_spec()],
        out_specs=[pl.BlockSpec((tf, D_MODEL), lambda j: (j, 0)),
                   pl.BlockSpec((None, D_MODEL, tf), lambda j: (j // per, 0, j % per))],
        out_shape=[jax.ShapeDtypeStruct((D_FF, D_MODEL), F32),
                   jax.ShapeDtypeStruct((N_CHIPS, D_MODEL, D_FF // N_CHIPS), F32)],
        scratch_shapes=[pltpu.VMEM((depth, L, tf), a.dtype), pltpu.VMEM((depth, L, tf), dz.dtype),
                        pltpu.SemaphoreType.DMA((2, depth))],
        compiler_params=_params(("arbitrary",), VMEM_BIG),
    )(a, dff, n2, dz)


def _mix_bwd_call(dn2, dh2, h1, mix, cat, o_f, o_b, ga, gn, g_post, g_pre2, w_out_p):
    L = dn2.shape[0]
    tm = min(512, L)
    hw = GLA_HEADS * HEAD_PAD

    def body(dn2_r, dh2_r, h1_r, mix_r, cat_r, of_r, ob_r, ga_r, gn_r, gp_r, g2_r, w_r,
             dh1_r, do_r, dga_r, dos_r, dw_r, dg2_r, dgp_r, dgn_r):
        @pl.when(pl.program_id(0) == 0)
        def _():
            for ref in (dw_r, dg2_r, dgp_r, dgn_r):
                ref[...] = jnp.zeros_like(ref)

        parts = [slice(start, start + min(256, tm)) for start in range(0, tm, 256)]
        dmix_m = []
        for rs in parts:
            h1 = h1_r[rs, :]
            dx2, dg2 = _rms_bwd(h1, _rms_r(h1), g2_r[...], dn2_r[rs, :])
            dh1 = dh2_r[rs, :] + dx2
            dh1_r[rs, :] = dh1
            dg2_r[...] += dg2
            mix = mix_r[rs, :]
            dmix, dgp = _rms_bwd(mix, _rms_r(mix), gp_r[...], dh1)
            dgp_r[...] += dgp
            dmix_m.append(_mx(dmix))
        dcat = [_dot_nt(d, w_r[...]) for d in dmix_m]
        for rs, d in zip(parts, dmix_m):
            dw_r[...] += _dot_tn(cat_r[rs, :], d)
        gn_v = gn_r[...]
        dgn = jnp.zeros((1, HEAD_PAD), F32)
        for rs, dc in zip(parts, dcat):
            dos_r[rs, :] = _spread_heads(dc[:, hw:]).astype(dos_r.dtype)
            for h in range(GLA_HEADS):
                sl = slice(HEAD_PAD * h, HEAD_PAD * (h + 1))
                oh = of_r[rs, sl] + ob_r[rs, sl]
                rr = _rms_r(oh)
                xh = oh * rr
                gate = ga_r[rs, sl]
                sg = jax.nn.sigmoid(gate)
                silu = gate * sg
                doa = dc[:, sl]
                dga_r[rs, sl] = (doa * (xh * gn_v) * (sg + silu * (1.0 - sg))).astype(dga_r.dtype)
                don = doa * silu
                gd = don * gn_v
                do_r[rs, sl] = rr * (gd - xh * jnp.mean(gd * xh, axis=-1, keepdims=True))
                dgn = dgn + jnp.sum(don * xh, axis=0, keepdims=True)
        dgn_r[...] += dgn

    return pl.pallas_call(
        body, name="mix_bwd", grid=(L // tm,),
        in_specs=[_row_spec(tm, D_MODEL)] * 4 + [_row_spec(tm, OUT_PAD)] + [_row_spec(tm, hw)] * 3
        + [_full_spec((1, HEAD_PAD)), _full_spec((1, D_MODEL)), _full_spec((1, D_MODEL)), _vmem_spec()],
        out_specs=[_row_spec(tm, D_MODEL), _row_spec(tm, hw), _row_spec(tm, hw),
                   _row_spec(tm, SWA_Q_HEADS * HEAD_PAD),
                   _full_spec((OUT_PAD, D_MODEL)), _full_spec((1, D_MODEL)), _full_spec((1, D_MODEL)),
                   _full_spec((1, HEAD_PAD))],
        out_shape=[jax.ShapeDtypeStruct((L, D_MODEL), F32), jax.ShapeDtypeStruct((L, hw), F32),
                   jax.ShapeDtypeStruct((L, hw), MXU_DTYPE),
                   jax.ShapeDtypeStruct((L, SWA_Q_HEADS * HEAD_PAD), MXU_DTYPE),
                   jax.ShapeDtypeStruct((OUT_PAD, D_MODEL), F32), jax.ShapeDtypeStruct((1, D_MODEL), F32),
                   jax.ShapeDtypeStruct((1, D_MODEL), F32), jax.ShapeDtypeStruct((1, HEAD_PAD), F32)],
        compiler_params=_params(("arbitrary",), VMEM_BIG),
    )(dn2, dh2, h1, mix, cat, o_f, o_b, ga, gn, g_post, g_pre2, w_out_p)


def _mlp_mix_bwd_call(dff, rz, w_ud, dh2, h1, mix, cat, o_f, o_b, ga, gn, g_post, g_pre2, w_out_p):
    L = dff.shape[0]
    tm = min(256, L)
    hw = GLA_HEADS * HEAD_PAD
    blk = D_FF // N_CHIPS

    def body(dff_r, rz_r, wud_r, dh2_r, h1_r, mix_r, cat_r, of_r, ob_r, ga_r, gn_r, gp_r, g2_r, w_r,
             dz_r, dh1_r, do_r, dga_r, dos_r, dw_r, dg2_r, dgp_r, dgn_r):
        @pl.when(pl.program_id(0) == 0)
        def _():
            for ref in (dw_r, dg2_r, dgp_r, dgn_r):
                ref[...] = jnp.zeros_like(ref)

        dffv = dff_r[...]
        dn2 = jnp.zeros((tm, D_MODEL), F32)
        for j in range(N_CHIPS):
            sl = slice(blk * j, blk * (j + 1))
            dz = _mx(_dot_nt(dffv, wud_r[j, 1]) * 2.0 * rz_r[:, sl].astype(F32))
            dz_r[:, sl] = dz
            dn2 = dn2 + _dot_nt(dz, wud_r[j, 0])

        h1 = h1_r[...]
        dx2, dg2 = _rms_bwd(h1, _rms_r(h1), g2_r[...], dn2)
        dh1 = dh2_r[...] + dx2
        dh1_r[...] = dh1
        dg2_r[...] += dg2
        mix = mix_r[...]
        dmix, dgp = _rms_bwd(mix, _rms_r(mix), gp_r[...], dh1)
        dgp_r[...] += dgp
        dmix_m = _mx(dmix)
        dc = _dot_nt(dmix_m, w_r[...])
        dw_r[...] += _dot_tn(cat_r[...], dmix_m)
        gn_v = gn_r[...]
        dgn = jnp.zeros((1, HEAD_PAD), F32)
        dos_r[...] = _spread_heads(dc[:, hw:]).astype(dos_r.dtype)
        for h in range(GLA_HEADS):
            sl = slice(HEAD_PAD * h, HEAD_PAD * (h + 1))
            oh = of_r[:, sl] + ob_r[:, sl]
            rr = _rms_r(oh)
            xh = oh * rr
            gate = ga_r[:, sl]
            sg = jax.nn.sigmoid(gate)
            silu = gate * sg
            doa = dc[:, sl]
            dga_r[:, sl] = (doa * (xh * gn_v) * (sg + silu * (1.0 - sg))).astype(dga_r.dtype)
            don = doa * silu
            gd = don * gn_v
            do_r[:, sl] = rr * (gd - xh * jnp.mean(gd * xh, axis=-1, keepdims=True))
            dgn = dgn + jnp.sum(don * xh, axis=0, keepdims=True)
        dgn_r[...] += dgn

    return pl.pallas_call(
        body, name="mlp_mix_bwd", grid=(L // tm,),
        in_specs=[_row_spec(tm, D_MODEL), _row_spec(tm, D_FF), _vmem_spec()] + [_row_spec(tm, D_MODEL)] * 3
        + [_row_spec(tm, OUT_PAD)] + [_row_spec(tm, hw)] * 3
        + [_full_spec((1, HEAD_PAD)), _full_spec((1, D_MODEL)), _full_spec((1, D_MODEL)), _vmem_spec()],
        out_specs=[_row_spec(tm, D_FF), _row_spec(tm, D_MODEL), _row_spec(tm, hw), _row_spec(tm, hw),
                   _row_spec(tm, SWA_Q_HEADS * HEAD_PAD),
                   _full_spec((OUT_PAD, D_MODEL)), _full_spec((1, D_MODEL)), _full_spec((1, D_MODEL)),
                   _full_spec((1, HEAD_PAD))],
        out_shape=[jax.ShapeDtypeStruct((L, D_FF), MXU_DTYPE),
                   jax.ShapeDtypeStruct((L, D_MODEL), F32), jax.ShapeDtypeStruct((L, hw), F32),
                   jax.ShapeDtypeStruct((L, hw), MXU_DTYPE),
                   jax.ShapeDtypeStruct((L, SWA_Q_HEADS * HEAD_PAD), MXU_DTYPE),
                   jax.ShapeDtypeStruct((OUT_PAD, D_MODEL), F32), jax.ShapeDtypeStruct((1, D_MODEL), F32),
                   jax.ShapeDtypeStruct((1, D_MODEL), F32), jax.ShapeDtypeStruct((1, HEAD_PAD), F32)],
        compiler_params=_params(("arbitrary",), VMEM_BIG),
    )(dff, rz, w_ud, dh2, h1, mix, cat, o_f, o_b, ga, gn, g_post, g_pre2, w_out_p)


def _in_bwd_call(x, dh1, g_pre, w_in_t, pairs, singles, halos, dep=None):
    L = x.shape[0]
    tm = min(512, L)
    per = tm // SWA_BLOCK
    n_pair, n_single, n_halo = len(pairs), len(singles), len(halos)
    groups = [c for c, _ in pairs] + [c for c, _ in singles] + [c for c, _ in halos]

    def body(*refs):
        x_r, dh1_r, g_r, w_r = refs[:4]
        pair_refs = refs[4:4 + 2 * n_pair]
        single_refs = refs[4 + 2 * n_pair:4 + 2 * n_pair + n_single]
        halo_refs = refs[4 + 2 * n_pair + n_single:4 + 2 * n_pair + n_single + per * n_halo]
        dx_r, dw_r, dg_r = refs[4 + 2 * n_pair + n_single + per * n_halo:]

        @pl.when(pl.program_id(0) == 0)
        def _():
            dw_r[...] = jnp.zeros_like(dw_r)
            dg_r[...] = jnp.zeros_like(dg_r)

        xv = x_r[...]
        r = _rms_r(xv)
        g = g_r[...]
        u = _mx(xv * r * g)
        vals = [pair_refs[2 * i][...].astype(F32) + pair_refs[2 * i + 1][...].astype(F32) for i in range(n_pair)]
        vals += [ref[...].astype(F32) for ref in single_refs]
        vals += [jnp.concatenate([ref[...] for ref in halo_refs[per * i:per * (i + 1)]], axis=0)
                 for i in range(n_halo)]
        ds = [_mx(_squeeze_heads(val) if heads else val) for (_, _, heads), val in zip(groups, vals)]
        du = jnp.zeros((tm, D_MODEL), F32)
        for (first, rows, _), d in zip(groups, ds):
            du = du + _dot(d, w_r[first:first + rows, :])
        for (first, rows, _), d in zip(groups, ds):
            dw_r[first:first + rows, :] += _dot_tn(d, u)
        dx, dg = _rms_bwd(xv, r, g, du)
        dx_r[...] = dh1_r[...] + dx
        dg_r[...] += dg

    arrays = [a for _, pr in pairs for a in pr] + [a for _, a in singles]
    specs = [_row_spec(tm, a.shape[1]) for a in arrays]
    for _, a in halos:
        specs += [pl.BlockSpec((SWA_BLOCK, a.shape[1]), lambda i, j=j: (per * i + 1 + j, 0)) for j in range(per)]
        arrays += [a] * per
    body, extra, extra_specs = _after(body, 4 + len(arrays), dep)
    return pl.pallas_call(
        body, name="in_bwd", grid=(L // tm,),
        in_specs=[_row_spec(tm, D_MODEL), _row_spec(tm, D_MODEL), _full_spec((1, D_MODEL)), _vmem_spec()] + specs
        + extra_specs,
        out_specs=[_row_spec(tm, D_MODEL), _full_spec((IN_COLS, D_MODEL)), _full_spec((1, D_MODEL))],
        out_shape=[jax.ShapeDtypeStruct((L, D_MODEL), F32), jax.ShapeDtypeStruct((IN_COLS, D_MODEL), F32),
                   jax.ShapeDtypeStruct((1, D_MODEL), F32)],
        compiler_params=_params(("arbitrary",), VMEM_BIG),
    )(x, dh1, g_pre, w_in_t, *arrays, *extra)


def _adamw_math(w, g, m, v):
    m = ADAM_B1 * m + (1.0 - ADAM_B1) * g
    v = ADAM_B2 * v + (1.0 - ADAM_B2) * (g * g)
    m_hat = m / (1.0 - ADAM_B1 ** ADAM_STEP)
    v_hat = v / (1.0 - ADAM_B2 ** ADAM_STEP)
    delta = -ADAM_LR * (m_hat / (jnp.sqrt(v_hat) + ADAM_EPS) + ADAM_WD * w)
    return delta, m, v


def _adamw_call(w, g, m, v, name, dep=None):
    rows, cols = w.shape
    tr = min(256, rows)

    def body(w_r, g_r, m_r, v_r, g_out_r, d_r, nm_r, nv_r):
        g = g_r[...]
        g_out_r[...] = g
        d_r[...], nm_r[...], nv_r[...] = _adamw_math(w_r[...], g, m_r[...], v_r[...])

    if rows % tr == 0:
        spec, steps = _row_spec(tr, cols), rows // tr
    else:
        spec, steps = pl.BlockSpec((rows, 256), lambda i: (0, i)), cols // 256
    body, extra, extra_specs = _after(body, 4, dep)
    return pl.pallas_call(
        body, name=name, grid=(steps,),
        in_specs=[spec] * 4 + extra_specs, out_specs=[spec] * 4,
        out_shape=[jax.ShapeDtypeStruct(w.shape, F32)] * 4,
        compiler_params=_params(("arbitrary",)),
    )(w, g, m, v, *extra)


def _position():
    return lax.axis_index("x"), lax.axis_index("y"), lax.axis_index("c")


def _other_chips(x, y):
    return [(1 - x, y), (x, 1 - y), (1 - x, 1 - y)]


ROWS, COLS = -2, -1


def _half(ref, which, axis):
    size = ref.shape[axis] // 2
    span = pl.ds(pl.multiple_of(which * size, 16 if axis == ROWS else 128), size)
    index = [slice(None)] * len(ref.shape)
    index[axis] = span
    return ref.at[tuple(index)]


def _quarter(ref, half, which, axis):
    size = ref.shape[axis] // 4
    span = pl.ds(pl.multiple_of((2 * half + which) * size, 16 if axis == ROWS else 128), size)
    index = [slice(None)] * len(ref.shape)
    index[axis] = span
    return ref.at[tuple(index)]


def _first_gather_call(shards, axes, routed):
    n = len(shards)
    per = 7

    def body(*refs):
        srcs, outs = refs[:n], refs[n:2 * n]
        send_sems, recv_sems, local_sems = refs[2 * n:]
        x, y, c = _position()
        me, sibling = (x, y, c), (x, y, 1 - c)
        x_side, y_side, across = _other_chips(x, y)
        local = [pltpu.make_async_copy(srcs[a], outs[a].at[2 * x + y], local_sems.at[a]) for a in range(n)]
        for cp in local:
            cp.start()

        def copy(a, k, dst, to, src=None):
            return pltpu.make_async_remote_copy(
                src_ref=dst if src is None else src, dst_ref=dst, send_sem=send_sems.at[per * a + k],
                recv_sem=recv_sems.at[per * a + k], device_id=to, device_id_type=MESH_ID)

        def half(a, chip, pc):
            return _half(outs[a].at[2 * chip[0] + chip[1]], pc, axes[a])

        def quarter(a, chip, q):
            return _quarter(outs[a].at[2 * chip[0] + chip[1]], c, q, axes[a])

        sends = []
        for a in range(n):
            mine = _half(srcs[a], c, axes[a])
            targets = (x_side, y_side) if routed[a] else (x_side, y_side, across)
            sends += [copy(a, j, half(a, (x, y), c), (*chip, c), src=mine) for j, chip in enumerate(targets)]
        for cp in sends:
            cp.start()
        for a in range(n):
            for j, chip in enumerate((x_side, y_side)):
                copy(a, j, half(a, chip, c), me).wait_recv()
                if routed[a]:
                    other = (y_side, x_side)[j]
                    sends.append(copy(a, 2 + j, quarter(a, chip, j), (*other, c)))
                    sends[-1].start()
                sends.append(copy(a, 4 + j, half(a, chip, c), sibling))
                sends[-1].start()
        for a in range(n):
            if routed[a]:
                for j in range(2):
                    copy(a, 2 + j, quarter(a, across, j), me).wait_recv()
            else:
                copy(a, 2, half(a, across, c), me).wait_recv()
            sends.append(copy(a, 6, half(a, across, c), sibling))
            sends[-1].start()
        for a in range(n):
            for k, chip in ((4, x_side), (5, y_side), (6, across)):
                copy(a, k, half(a, chip, 1 - c), me).wait_recv()
        for cp in sends:
            cp.wait_send()
        for cp in local:
            cp.wait()

    return pl.pallas_call(
        body, name="first_gather",
        in_specs=[_any_spec()] * n, out_specs=[_any_spec()] * n,
        out_shape=[jax.ShapeDtypeStruct((N_CHIPS,) + s.shape, s.dtype) for s in shards],
        scratch_shapes=[pltpu.SemaphoreType.DMA((per * n,)), pltpu.SemaphoreType.DMA((per * n,)),
                        pltpu.SemaphoreType.DMA((n,))],
    )(*shards)


PAIR_PEERS, CHIP_PEERS = 1, 2


def _peers(which):
    x, y, c = _position()
    if which == PAIR_PEERS:
        return [(x, y, 1 - c)]
    return [(px, py, c) for px, py in _other_chips(x, y)]


def _split_start(name, arrays, n_copies, plan, peers=None):
    n = len(arrays)

    def body(*refs):
        ins, send_sems, recv_sems, token = refs[:n], refs[n], refs[n + 1], refs[-1]
        if peers is not None:
            barrier = pltpu.get_barrier_semaphore()
            targets = _peers(peers)
            for target in targets:
                pl.semaphore_signal(barrier, inc=1, device_id=target, device_id_type=MESH_ID)
            pl.semaphore_wait(barrier, len(targets))
        for k, (src, dst, to, _) in enumerate(plan(ins)):
            pltpu.make_async_remote_copy(src_ref=src, dst_ref=dst, send_sem=send_sems.at[k],
                                         recv_sem=recv_sems.at[k], device_id=to, device_id_type=MESH_ID).start()
        token[...] = jnp.zeros_like(token)

    hbm = pl.BlockSpec(memory_space=pltpu.HBM)
    sem = pl.BlockSpec(memory_space=pltpu.SEMAPHORE)
    out = pl.pallas_call(
        body, name=name,
        out_shape=(pltpu.SemaphoreType.DMA((n_copies,)), pltpu.SemaphoreType.DMA((n_copies,)))
        + tuple(pltpu.HBM(a.shape, a.dtype) for a in arrays) + (jax.ShapeDtypeStruct((8, 128), F32),),
        in_specs=[hbm] * n, out_specs=(sem, sem) + (hbm,) * n + (_vmem_spec(),),
        input_output_aliases={i: 2 + i for i in range(n)},
        compiler_params=pltpu.CompilerParams(has_side_effects=pltpu.SideEffectType.DATAFLOW_SIDE_EFFECTING,
                                             collective_id=peers),
    )(*[pltpu.with_memory_space_constraint(a, pltpu.HBM) for a in arrays])
    return (out[0], out[1], tuple(out[2:2 + n])), out[-1]


def _split_wait(name, handle, n_copies, plan, after):
    send_sems, recv_sems, arrays = handle
    n = len(arrays)

    def body(*refs):
        ins, s_sems, r_sems = refs[:n], refs[n], refs[n + 1]
        for k, (src, dst, to, landed) in enumerate(plan(ins)):
            cp = pltpu.make_async_remote_copy(src_ref=src, dst_ref=landed, send_sem=s_sems.at[k],
                                              recv_sem=r_sems.at[k], device_id=to, device_id_type=MESH_ID)
            cp.wait_send()
            cp.wait_recv()

    hbm = pl.BlockSpec(memory_space=pltpu.HBM)
    sem = pl.BlockSpec(memory_space=pltpu.SEMAPHORE)
    out = pl.pallas_call(
        body, name=name,
        out_shape=tuple(pltpu.HBM(a.shape, a.dtype) for a in arrays),
        in_specs=[hbm] * n + [sem, sem, _any_spec()], out_specs=(hbm,) * n,
        input_output_aliases={i: i for i in range(n)},
        compiler_params=pltpu.CompilerParams(has_side_effects=pltpu.SideEffectType.DATAFLOW_SIDE_EFFECTING),
    )(*arrays, send_sems, recv_sems, after)
    return tuple(out)


def _gather_plans(axes):
    n = len(axes)

    def stage_one(refs):
        x, y, c = _position()
        copies = []
        for a, axis in enumerate(axes):
            for px, py in _other_chips(x, y):
                copies.append((_half(refs[a], c, axis), _half(refs[n + a].at[2 * x + y], c, axis),
                               (px, py, c), _half(refs[n + a].at[2 * px + py], c, axis)))
        return copies

    def stage_two(refs):
        x, y, c = _position()
        copies = []
        for a, axis in enumerate(axes):
            for px, py in _other_chips(x, y):
                piece = _half(refs[n + a].at[2 * px + py], c, axis)
                copies.append((piece, piece, (x, y, 1 - c), _half(refs[n + a].at[2 * px + py], 1 - c, axis)))
        return copies

    return stage_one, stage_two


def _pair_swap_plan(axes):
    n = len(axes)

    def plan(refs):
        x, y, c = _position()
        return [(_half(refs[a], 1 - c, axes[a]), refs[n + a], (x, y, 1 - c), refs[n + a]) for a in range(n)]

    return plan


def _chip_swap_plan(n):
    def plan(refs):
        x, y, c = _position()
        copies = []
        for a in range(n):
            for j, (px, py) in enumerate(_other_chips(x, y)):
                copies.append((refs[a].at[2 * px + py], refs[n + a].at[j], (px, py, c), refs[n + a].at[j]))
        return copies

    return plan


def _pair_join_plan(axes):
    def plan(refs):
        x, y, c = _position()
        copies = []
        for a, axis in enumerate(axes):
            mine = _half(refs[a], c, axis)
            copies.append((mine, mine, (x, y, 1 - c), _half(refs[a], 1 - c, axis)))
        return copies

    return plan


def _pair_add_call(gs, gots, pos, name, axes):
    n = len(gs)

    def body(pos_r, *refs):
        for g_r, got_r, o_r in zip(refs[:n], refs[n:2 * n], refs[2 * n:]):
            o_r[...] = (g_r[...] + got_r[...]).astype(o_r.dtype)

    def mine(axis):
        return (lambda j, p: (j, p[1], 0)) if axis == ROWS else (lambda j, p: (j, 0, p[1]))

    blocks = [(None,) + got.shape[1:] for got in gots]
    return pl.pallas_call(
        body, name=name,
        grid_spec=pltpu.PrefetchScalarGridSpec(
            num_scalar_prefetch=1, grid=(N_CHIPS,),
            in_specs=[pl.BlockSpec(blk, mine(axis)) for blk, axis in zip(blocks, axes)]
            + [pl.BlockSpec(blk, lambda j, p: (j, 0, 0)) for blk in blocks],
            out_specs=[pl.BlockSpec(blk, lambda j, p: (j, 0, 0)) for blk in blocks]),
        out_shape=[jax.ShapeDtypeStruct(got.shape, COMM_DTYPE) for got in gots],
        compiler_params=_params(("arbitrary",), VMEM_BIG),
    )(pos, *gs, *gots)


def _chip_add_call(hsums, gots, pos, name, axes):
    n = len(hsums)
    steps = 2

    def body(pos_r, *refs):
        for own_r, got_r, o_r in zip(refs[:n], refs[n:2 * n], refs[2 * n:]):
            acc = own_r[...].astype(F32)
            for j in range(3):
                acc = acc + got_r[j].astype(F32)
            o_r[...] = acc

    in_specs, got_specs, out_specs, out_shape = [], [], [], []
    for h, axis in zip(hsums, axes):
        if axis == ROWS:
            rows, cols = h.shape[1] // steps, h.shape[2]
            in_specs.append(pl.BlockSpec((None, rows, cols), lambda i, p: (p[0], i, 0)))
            got_specs.append(pl.BlockSpec((3, rows, cols), lambda i, p: (0, i, 0)))
            out_specs.append(pl.BlockSpec((rows, cols), lambda i, p: (p[1] * steps + i, 0)))
            out_shape.append(jax.ShapeDtypeStruct((2 * h.shape[1], cols), F32))
        else:
            rows, cols = h.shape[1], h.shape[2] // steps
            in_specs.append(pl.BlockSpec((None, rows, cols), lambda i, p: (p[0], 0, i)))
            got_specs.append(pl.BlockSpec((3, rows, cols), lambda i, p: (0, 0, i)))
            out_specs.append(pl.BlockSpec((rows, cols), lambda i, p: (0, p[1] * steps + i)))
            out_shape.append(jax.ShapeDtypeStruct((rows, 2 * h.shape[2]), F32))
    return pl.pallas_call(
        body, name=name,
        grid_spec=pltpu.PrefetchScalarGridSpec(
            num_scalar_prefetch=1, grid=(steps,), in_specs=in_specs + got_specs, out_specs=out_specs),
        out_shape=out_shape,
        compiler_params=_params(("arbitrary",), VMEM_BIG),
    )(pos, *hsums, *gots)


SMALL_NAMES = ("norm_mix_pre", "norm_mix_post", "norm_mlp_pre", "norm_mlp_post", "b_gate_fwd", "b_gate_bwd",
               "gla_norm", "swa_sink", "rel_bias")


N_DEVICES = 8


def _small_pack_call(grads, extras):
    operands = list(grads) + list(extras)

    def body(*refs):
        g_refs, (all_a, all_b) = refs[:len(operands)], refs[len(operands):]
        x, y, c = _position()
        me = 4 * x + 2 * y + c
        all_a[me] = jnp.zeros(all_a.shape[1:], F32)
        all_b[me] = jnp.zeros(all_b.shape[1:], F32)
        for i in range(4):
            all_a[me, i:i + 1, :] = g_refs[i][...]
        all_a[me, 4:5, 0:256] = g_refs[4][...]
        all_a[me, 5:6, 0:256] = g_refs[5][...]
        all_a[me, 6:7, 0:128] = g_refs[6][...]
        all_a[me, 7:8, 0:128] = g_refs[7][...]
        all_a[me, 7:8, 128:256] = g_refs[11][...]
        all_b[me, 0:32, 0:128] = g_refs[8][...]
        all_b[me, 32:48, :] = g_refs[9][...]
        all_b[me, 48:64, :] = g_refs[10][...]

    out_shape = [jax.ShapeDtypeStruct((N_DEVICES, 8, D_MODEL), F32), jax.ShapeDtypeStruct((N_DEVICES, 64, 256), F32)]
    return pl.pallas_call(
        body, name="small_pack",
        in_specs=[_whole_spec(a.shape) for a in operands], out_specs=[_whole_spec(s.shape) for s in out_shape],
        out_shape=out_shape,
    )(*operands)


def _everyone_plan(n):
    def plan(refs):
        x, y, c = _position()
        copies = []
        for k in range(1, N_DEVICES):
            px = 1 - x if (k >> 2) & 1 else x
            py = 1 - y if (k >> 1) & 1 else y
            pc = 1 - c if k & 1 else c
            for a in range(n):
                mine = refs[a].at[4 * x + 2 * y + c]
                copies.append((mine, mine, (px, py, pc), refs[a].at[4 * px + 2 * py + pc]))
        return copies

    return plan


def _small_adamw_call(all_a, all_b, params):
    n_small = len(SMALL_NAMES)
    wmv = [t for p in params for t in p]
    shapes = [p[0].shape for p in params]

    def body(*refs):
        all_a, all_b = refs[:2]
        wmv_refs = refs[2:2 + 3 * n_small]
        out_refs = refs[2 + 3 * n_small:]
        sum_a, sum_b = all_a[0], all_b[0]
        for d in range(1, N_DEVICES):
            sum_a = sum_a + all_a[d]
            sum_b = sum_b + all_b[d]
        gsum = [sum_a[0:1], sum_a[1:2], sum_a[2:3], sum_a[3:4], sum_a[4:5, 0:256], sum_a[5:6, 0:256],
                sum_a[6:7, 0:128], sum_a[7:8, 0:SWA_Q_HEADS], sum_b[0:32, 0:SWA_Q_HEADS]]
        for i in range(n_small):
            w_r, m_r, v_r = wmv_refs[3 * i:3 * i + 3]
            delta, new_m, new_v = _adamw_math(w_r[...], gsum[i], m_r[...], v_r[...])
            out_refs[4 * i][...] = gsum[i]
            out_refs[4 * i + 1][...] = delta
            out_refs[4 * i + 2][...] = new_m
            out_refs[4 * i + 3][...] = new_v
        out_refs[4 * n_small][...] = sum_b[32:48]
        out_refs[4 * n_small + 1][...] = sum_b[48:64]
        out_refs[4 * n_small + 2][...] = sum_a[7:8, 128:256]

    out_shape = [jax.ShapeDtypeStruct(s, F32) for s in shapes for _ in range(4)]
    out_shape += [jax.ShapeDtypeStruct((GLA_GATE_RANK, 256), F32)] * 2 + [jax.ShapeDtypeStruct((1, 128), F32)]
    out = pl.pallas_call(
        body, name="small_adamw",
        in_specs=[_whole_spec(a.shape) for a in [all_a, all_b] + wmv],
        out_specs=[_whole_spec(s.shape) for s in out_shape],
        out_shape=out_shape,
    )(all_a, all_b, *wmv)
    per_name = [tuple(out[4 * i:4 * i + 4]) for i in range(n_small)]
    return per_name, out[4 * n_small], out[4 * n_small + 1], out[4 * n_small + 2]


def _pad_gate(w, first_row):
    return jnp.pad(w, ((first_row, 128 - GLA_GATE_RANK - first_row), (0, 0)))


def _own_slot(shard, chip):
    zone = lax.empty((N_CHIPS,) + shard.shape, shard.dtype)
    return lax.dynamic_update_slice(zone, shard[None], (chip,) + (0,) * shard.ndim)


def _reduce_to_owners(grads, axes, pos, tag, overlap):
    n = len(grads)

    def half_shape(g, axis):
        return (N_CHIPS, g.shape[1] // 2, g.shape[2]) if axis == ROWS else (N_CHIPS, g.shape[1], g.shape[2] // 2)

    lands = [lax.empty(half_shape(g, axis), F32) for g, axis in zip(grads, axes)]
    handle, token = _split_start(tag + "_pair_start", list(grads) + lands, n, _pair_swap_plan(axes), PAIR_PEERS)
    got = _split_wait(tag + "_pair_wait", handle, n, _pair_swap_plan(axes), overlap[0](token))
    sums = list(_pair_add_call(got[:n], got[n:], pos, tag + "_pair_add", axes))
    lands = [lax.empty((3,) + s.shape[1:], s.dtype) for s in sums]
    handle, token = _split_start(tag + "_chip_start", sums + lands, 3 * n, _chip_swap_plan(n), CHIP_PEERS)
    got = _split_wait(tag + "_chip_wait", handle, 3 * n, _chip_swap_plan(n), overlap[1](token))
    halves = list(_chip_add_call(got[:n], got[n:], pos, tag + "_chip_add", axes))
    handle, token = _split_start(tag + "_join_start", halves, n, _pair_join_plan(axes), PAIR_PEERS)
    return _split_wait(tag + "_join_wait", handle, n, _pair_join_plan(axes), overlap[2](token))


def kernel(x, norm_mix_pre, w_in, w_gate_up_fwd, b_gate_fwd, w_gate_up_bwd, b_gate_bwd, gla_norm, swa_sink, rel_bias, w_out, norm_mix_post, norm_mlp_pre, w_up, w_down, norm_mlp_post, loss_target, m_norm_mix_pre, m_w_in, m_w_gate_up_fwd, m_b_gate_fwd, m_w_gate_up_bwd, m_b_gate_bwd, m_gla_norm, m_swa_sink, m_rel_bias, m_w_out, m_norm_mix_post, m_norm_mlp_pre, m_w_up, m_w_down, m_norm_mlp_post, v_norm_mix_pre, v_w_in, v_w_gate_up_fwd, v_b_gate_fwd, v_w_gate_up_bwd, v_b_gate_bwd, v_gla_norm, v_swa_sink, v_rel_bias, v_w_out, v_norm_mix_post, v_norm_mlp_pre, v_w_up, v_w_down, v_norm_mlp_post):
    given = dict(locals())
    cx, cy, cc = _position()
    chip = (2 * cx + cy).astype(jnp.int32)
    pos = jnp.stack([chip, cc.astype(jnp.int32)])
    seq, tgt = x[0], loss_target[0]

    gates = jnp.concatenate([w_gate_up_fwd[0], w_gate_up_bwd[0]], axis=0).astype(COMM_DTYPE)
    all_in, all_gates = _first_gather_call([w_in[0].T.astype(COMM_DTYPE), gates], [COLS, ROWS], [True, False])
    rest = [w_out[0].astype(COMM_DTYPE), jnp.stack([w_up[0], w_down[0]]).astype(COMM_DTYPE)]
    stage_one, stage_two = _gather_plans([ROWS, ROWS])
    handle, token = _split_start("gather_chip_start", rest + [_own_slot(s, chip) for s in rest] + [all_gates], 6,
                                 stage_one, CHIP_PEERS)

    w_in_t = _mx(all_in.reshape(IN_COLS, D_MODEL))
    gates_full = jnp.concatenate([all_gates[j] for j in range(N_CHIPS)], axis=1)
    wgf_p = _mx(_pad_gate(gates_full[:GLA_GATE_RANK], 0))
    wgb_p = _mx(_pad_gate(gates_full[GLA_GATE_RANK:], GLA_GATE_RANK))
    bf_p, bb_p = b_gate_fwd, b_gate_bwd
    buckets = jnp.asarray(_band_buckets())
    sink1 = swa_sink.reshape(SWA_Q_HEADS)

    qa, ka, va, ga, qs, ks, vs, za = _proj_call(seq, norm_mix_pre, w_in_t, dep=token)
    halo = ((SWA_BLOCK, SWA_BLOCK), (0, 0))
    ks_p, vs_p = jnp.pad(ks, halo), jnp.pad(vs, halo)
    o_f, o_b, s_f, s_b = _gla_fwd_call(qa, ka, va, za, wgf_p, bf_p, wgb_p, bb_p)
    bias = _bias_call(rel_bias, buckets, dep=o_f)
    arrays = _split_wait("gather_chip_wait", handle, 6, stage_one, bias)
    handle, token = _split_start("gather_pair_start", list(arrays), 6, stage_two, PAIR_PEERS)
    o_s = _swa_fwd_call(qs, ks_p, vs_p, bias, sink1, dep=token)
    arrays = _split_wait("gather_pair_wait", handle, 6, stage_two, o_s)
    w_out_full = _mx(arrays[2].reshape(N_CHIPS * R_OUT, D_MODEL))
    w_ud = _mx(arrays[3])
    cat, mix, h1, n2, a, rz, dh2, dff, loss, d_post2 = _mix_mlp_fwd_call(
        o_f, o_b, ga, o_s, seq, tgt, gla_norm, w_out_full, norm_mix_post, norm_mlp_pre, w_ud, norm_mlp_post)

    dz, dh1, do, dga, dos, dw_out, d_pre2, d_post, d_gn = _mlp_mix_bwd_call(
        dff, rz, w_ud, dh2, h1, mix, cat, o_f, o_b, ga, gla_norm, norm_mix_post, norm_mlp_pre, w_out_full)
    dw_down, dw_up4 = _mlp_wgrad_call(a, dff, n2, dz)
    done = {}

    def swa_backward(tok):
        done["swa"] = _swa_bwd_call(qs, ks_p, vs_p, bias, sink1, dos, dep=tok)
        return done["swa"][0]

    def gla_in_backward(tok):
        done["gla"] = _gla_bwd_call(qa, ka, va, za, do, s_f, s_b, wgf_p, bf_p, wgb_p, bb_p, dep=tok)
        dqf, dkf, dvf, dzf, _, _, dqb, dkb, dvb, dzb, _, _ = done["gla"]
        dqs, dks_p, dvs_p, _, _ = done["swa"]
        done["in"] = _in_bwd_call(
            seq, dh1, norm_mix_pre, w_in_t,
            pairs=[(_side_by_side(T_QA), (dqf, dqb)), (_side_by_side(T_KA), (dkf, dkb)), (T_VA, (dvf, dvb)),
                   (T_ZA, (dzf, dzb))],
            singles=[(T_GA, dga), (_side_by_side(T_QS), dqs)], halos=[(T_KS, dks_p), (T_VS, dvs_p)])
        return done["in"][0]

    def bias_backward(tok):
        done["rel"] = _relbias_call(done["swa"][3], done["swa"][4], buckets, dep=tok)
        return done["rel"][0]

    g_up, g_down, g_out = _reduce_to_owners(
        [dw_up4, dw_down.reshape(N_CHIPS, R_DOWN, D_MODEL), dw_out.reshape(N_CHIPS, R_OUT, D_MODEL)],
        [ROWS, ROWS, ROWS], pos, "mlp", [swa_backward, gla_in_backward, bias_backward])
    dx, dw_in_t, d_pre = done["in"]
    dwf, dbf, dwb, dbb = done["gla"][4], done["gla"][5], done["gla"][10], done["gla"][11]
    drel, dsink = done["rel"]

    small_grads = [d_pre, d_post, d_pre2, d_post2, dbf, dbb, d_gn, dsink, drel]
    gate_grads = [dwf[:GLA_GATE_RANK], dwb[GLA_GATE_RANK:2 * GLA_GATE_RANK]]
    small_params = [(given[n], given["m_" + n], given["v_" + n]) for n in SMALL_NAMES]
    upd = {}

    everyone = _everyone_plan(2)
    small_handle, small_token = _split_start(
        "small_start", list(_small_pack_call(small_grads, gate_grads + [loss])), 2 * (N_DEVICES - 1), everyone)

    def update_out(tok):
        upd["w_out"] = tuple(_adamw_call(w_out[0], g_out, m_w_out[0], v_w_out[0], "adamw_w_out",
                                         dep=tok + small_token))
        return upd["w_out"][1]

    def update_mlp(tok):
        upd["w_up"] = tuple(_adamw_call(w_up[0], g_up, m_w_up[0], v_w_up[0], "adamw_w_up", dep=tok))
        upd["w_down"] = tuple(
            _adamw_call(w_down[0], g_down, m_w_down[0], v_w_down[0], "adamw_w_down", dep=upd["w_up"][1]))
        all_a, all_b = _split_wait("small_wait", small_handle, 2 * (N_DEVICES - 1), everyone, upd["w_down"][1])
        per_name, done["gf_sum"], done["gb_sum"], upd["loss"] = _small_adamw_call(all_a, all_b, small_params)
        upd.update(dict(zip(SMALL_NAMES, per_name)))
        return per_name[0][1]

    def update_gates(tok):
        for name, total in (("w_gate_up_fwd", done["gf_sum"]), ("w_gate_up_bwd", done["gb_sum"])):
            g = lax.dynamic_slice(total, (0, chip * 64), (GLA_GATE_RANK, 64))
            upd[name] = tuple(_adamw_call(given[name][0], g, given["m_" + name][0], given["v_" + name][0],
                                          "adamw_" + name, dep=tok))
        return upd["w_gate_up_bwd"][1]

    (g_in_t,) = _reduce_to_owners([dw_in_t.reshape(N_CHIPS, R_IN, D_MODEL)], [COLS], pos, "in",
                                  [update_out, update_mlp, update_gates])
    upd["w_in"] = tuple(t.T for t in _adamw_call(w_in[0].T, g_in_t, m_w_in[0].T, v_w_in[0].T, "adamw_w_in"))

    big = ("w_in", "w_gate_up_fwd", "w_gate_up_bwd", "w_out", "w_up", "w_down")
    names = ["norm_mix_pre", "w_in", "w_gate_up_fwd", "b_gate_fwd", "w_gate_up_bwd", "b_gate_bwd", "gla_norm",
             "swa_sink", "rel_bias", "w_out", "norm_mix_post", "norm_mlp_pre", "w_up", "w_down", "norm_mlp_post"]
    outs = [upd["loss"][0, 0], dx[None]]
    for kind in range(4):
        outs += [upd[n][kind][None] if n in big else upd[n][kind] for n in names]
    return tuple(outs)
```

```python
import math

import numpy as np
import jax
import jax.numpy as jnp
from jax import lax
from jax.experimental import pallas as pl
from jax.experimental.pallas import tpu as pltpu

F32 = jnp.float32
MXU_DTYPE = jnp.bfloat16
COMM_DTYPE = jnp.bfloat16

D_MODEL = 1024
D_FF = 4096
N_CHIPS = 4
GLA_HEADS = 4
GLA_CHUNK = 64
GLA_GATE_RANK = 16
GLA_GATE_NORM = 16.0
SWA_Q_HEADS = 8
SWA_KV_HEADS = 2
SWA_BLOCK = 128
REL_BUCKETS = 32
REL_MAX_DIST = 128
NORM_EPS = 1e-6
HEAD_PAD = 128

ADAM_LR = 0.001
ADAM_B1 = 0.9
ADAM_B2 = 0.999
ADAM_EPS = 1e-08
ADAM_WD = 0.01
ADAM_STEP = 10

OUT_PAD = 1024

R_IN, R_OUT, R_DOWN = 584, 256, 1024

VMEM_BIG = 56 * 1024 * 1024
MESH_ID = pl.DeviceIdType.MESH


def _mx(a):
    return a.astype(MXU_DTYPE)


def _dot(a, b):
    return jnp.dot(a, b, preferred_element_type=F32)


def _dot_nt(a, b):
    return lax.dot_general(a, b, (((1,), (1,)), ((), ())), preferred_element_type=F32)


def _dot_tn(a, b):
    return lax.dot_general(a, b, (((0,), (0,)), ((), ())), preferred_element_type=F32)


def _rms_r(x):
    return lax.rsqrt(jnp.mean(x * x, axis=-1, keepdims=True) + NORM_EPS)


def _rms_bwd(x, r, g, dy):
    xh = x * r
    gdy = dy * g
    dx = r * (gdy - xh * jnp.mean(gdy * xh, axis=-1, keepdims=True))
    return dx, jnp.sum(dy * xh, axis=0, keepdims=True)


def _low_half(rows):
    return lax.broadcasted_iota(jnp.int32, (rows, HEAD_PAD), 1) < 64


def _spread_heads(x):
    low = _low_half(x.shape[0])
    parts = []
    for p in range(x.shape[1] // HEAD_PAD):
        pair = x[:, HEAD_PAD * p:HEAD_PAD * (p + 1)]
        parts += [jnp.where(low, pair, 0.0), jnp.where(low, pltpu.roll(pair, 64, 1), 0.0)]
    return jnp.concatenate(parts, axis=1)


def _squeeze_heads(x):
    low = _low_half(x.shape[0])
    parts = []
    for p in range(x.shape[1] // (2 * HEAD_PAD)):
        even = x[:, 2 * HEAD_PAD * p:2 * HEAD_PAD * p + HEAD_PAD]
        odd = x[:, 2 * HEAD_PAD * p + HEAD_PAD:2 * HEAD_PAD * (p + 1)]
        parts.append(jnp.where(low, even, pltpu.roll(odd, 64, 1)))
    return parts[0] if len(parts) == 1 else jnp.concatenate(parts, axis=1)


def _params(sem=None, vmem=None):
    kw = {}
    if sem is not None:
        kw["dimension_semantics"] = sem
    if vmem is not None:
        kw["vmem_limit_bytes"] = vmem
    return pltpu.CompilerParams(**kw)


def _vmem_spec():
    return pl.BlockSpec(memory_space=pltpu.VMEM)


def _whole_spec(shape):
    return pl.BlockSpec(shape, lambda: (0,) * len(shape))


def _row_spec(tm, width):
    return pl.BlockSpec((tm, width), lambda i: (i, 0))


def _full_spec(shape):
    return pl.BlockSpec(shape, lambda i: (0,) * len(shape))


def _any_spec():
    return pl.BlockSpec(memory_space=pl.ANY)


def _after(body, n_in, dep):
    if dep is None:
        return body, [], []
    return (lambda *refs: body(*refs[:n_in], *refs[n_in + 1:])), [dep], [_any_spec()]


T_QA, T_KA, T_VA, T_GA = (0, 256, 4), (256, 256, 4), (512, 512, 0), (1024, 512, 0)
T_QS, T_KS, T_VS = (1568, 512, 8), (2080, 128, 2), (2208, 128, 2)
T_ZA = (1536, 128, 0)
ZA_COLS = 2 * GLA_GATE_RANK
IN_COLS = 2336


def _side_by_side(group):
    return group[0], group[1], 0


def _proj_call(x, g_pre, w_in_t, dep=None):
    L = x.shape[0]
    tm = min(512, L)
    groups = [(T_QA, F32), (T_KA, F32), (T_VA, MXU_DTYPE), (T_GA, F32),
              (T_QS, MXU_DTYPE), (T_KS, MXU_DTYPE), (T_VS, MXU_DTYPE), (T_ZA, F32)]
    widths = [rows * (2 if heads else 1) for (_, rows, heads), _ in groups]

    def body(x_ref, g_ref, w_ref, *outs):
        xv = x_ref[...]
        u = _mx(xv * _rms_r(xv) * g_ref[...])
        for ref, (grp, _) in zip(outs, groups):
            first, rows, heads = grp
            val = _dot_nt(u, w_ref[first:first + rows, :])
            if heads:
                val = _spread_heads(val)
            if grp is T_ZA:
                val = jnp.where(lax.broadcasted_iota(jnp.int32, val.shape, 1) < ZA_COLS, val, 0.0)
            if grp is T_QS:
                val = val * 0.125
            ref[...] = val.astype(ref.dtype)

    body, extra, extra_specs = _after(body, 3, dep)
    return pl.pallas_call(
        body, name="proj_fwd", grid=(L // tm,),
        in_specs=[_row_spec(tm, D_MODEL), _full_spec((1, D_MODEL)), _vmem_spec()] + extra_specs,
        out_specs=[_row_spec(tm, w) for w in widths],
        out_shape=[jax.ShapeDtypeStruct((L, w), dt) for w, (_, dt) in zip(widths, groups)],
        compiler_params=_params(("arbitrary",), VMEM_BIG),
    )(x, g_pre, w_in_t, *extra)


def _tri_masks():
    row = lax.broadcasted_iota(jnp.int32, (GLA_CHUNK, GLA_CHUNK), 0)
    col = lax.broadcasted_iota(jnp.int32, (GLA_CHUNK, GLA_CHUNK), 1)
    return row >= col, row <= col


def _chunk_sums(tri_m, x):
    hi = _mx(x)
    rest = x - hi.astype(F32)
    mid = _mx(rest)
    lo = _mx(rest - mid.astype(F32))
    return _dot(tri_m, hi) + _dot(tri_m, mid) + _dot(tri_m, lo)


def _gla_block_pre(q_r, k_r, z_r, w_r, b_r, rev, nc, qd_s, ki_s, ks_s, dec_s, keep=None):
    tri_f, tri_b = _tri_masks()
    tri_m = _mx((tri_b if rev else tri_f).astype(F32))
    g = _dot(_mx(z_r[...]), w_r[...]) + b_r[...]
    la = (jnp.minimum(g, 0.0) - jnp.log(1.0 + jnp.exp(-jnp.abs(g)))) * (1.0 / GLA_GATE_NORM)
    sums, lasts = [], []
    for c in range(nc):
        b_c = _chunk_sums(tri_m, la[GLA_CHUNK * c:GLA_CHUNK * (c + 1)])
        blast = b_c[0:1] if rev else b_c[GLA_CHUNK - 1:GLA_CHUNK]
        dec_s[c] = _spread_heads(jnp.exp(blast))
        sums.append(b_c)
        lasts.append(jnp.broadcast_to(blast, b_c.shape))
    b = jnp.concatenate(sums, axis=0)
    eb = jnp.exp(b)
    enb = jnp.exp(-b)
    elb = jnp.exp(jnp.concatenate(lasts, axis=0) - b)
    q, k = _squeeze_heads(q_r[...]), _squeeze_heads(k_r[...])
    qd_s[...] = _spread_heads(q * 0.125 * eb).astype(qd_s.dtype)
    ki_s[...] = _spread_heads(k * enb).astype(ki_s.dtype)
    ks_s[...] = _spread_heads(k * elb).astype(ks_s.dtype)
    if keep is not None:
        keep[0][...] = g
        for ref, val in zip(keep[1:], (eb, enb, elb)):
            ref[...] = _spread_heads(val)


def _gla_fwd_call(qa, ka, va, za, wgf, bgf, wgb, bgb):
    L = qa.shape[0]
    br = min(512, L)
    nb, nc, n_chunks = L // br, br // GLA_CHUNK, L // GLA_CHUNK
    hw = GLA_HEADS * HEAD_PAD

    def body(qaf, kaf, vaf, zaf, qab, kab, vab, zab, wgf_r, bgf_r, wgb_r, bgb_r,
             of_r, ob_r, sf_r, sb_r, st_f, st_b, pre_f, pre_b):
        @pl.when(pl.program_id(0) == 0)
        def _():
            st_f[...] = jnp.zeros_like(st_f)
            st_b[...] = jnp.zeros_like(st_b)

        _gla_block_pre(qaf, kaf, zaf, wgf_r, bgf_r, False, nc, *pre_f)
        _gla_block_pre(qab, kab, zab, wgb_r, bgb_r, True, nc, *pre_b)
        tri_f, tri_b = _tri_masks()

        def one(tri, pre, v_r, o_r, s_r, st, ci):
            qd_s, ki_s, ks_s, dec_s = pre
            rows = pl.ds(pl.multiple_of(ci * GLA_CHUNK, GLA_CHUNK), GLA_CHUNK)
            dec = dec_s[ci]
            heads = range(GLA_HEADS)
            lanes = [slice(HEAD_PAD * h, HEAD_PAD * (h + 1)) for h in heads]
            qd = [qd_s[rows, sl] for sl in lanes]
            v = [v_r[rows, sl] for sl in lanes]
            s_t = [st[h] for h in heads]
            a = [_dot_nt(qd[h], ki_s[rows, lanes[h]]) for h in heads]
            carried = [_dot_nt(qd[h], _mx(s_t[h])) for h in heads]
            grown = [_dot_tn(v[h], ks_s[rows, lanes[h]]) for h in heads]
            a = [_mx(jnp.where(tri, a[h], 0.0)) for h in heads]
            inner = [_dot(a[h], v[h]) for h in heads]
            for h in heads:
                s_r[ci, h] = s_t[h].astype(s_r.dtype)
                o_r[rows, lanes[h]] = inner[h] + carried[h]
                st[h] = s_t[h] * dec[:, lanes[h]] + grown[h]

        def loop(t, carry):
            one(tri_f, pre_f, vaf, of_r, sf_r, st_f, t)
            one(tri_b, pre_b, vab, ob_r, sb_r, st_b, nc - 1 - t)
            return carry

        lax.fori_loop(0, nc, loop, 0, unroll=True)

    fwd = lambda i: (i, 0)
    bwd = lambda i: (nb - 1 - i, 0)
    ins = lambda m: [pl.BlockSpec((br, hw), m), pl.BlockSpec((br, hw), m),
                     pl.BlockSpec((br, hw), m), pl.BlockSpec((br, 128), m)]
    wspecs = [_full_spec((128, hw // 2)), _full_spec((1, hw // 2))] * 2
    s_shape = (nc, GLA_HEADS, HEAD_PAD, HEAD_PAD)
    pre_scratch = [pltpu.VMEM((br, hw), MXU_DTYPE)] * 3 + [pltpu.VMEM((nc, 1, hw), F32)]
    return pl.pallas_call(
        body, name="gla_fwd", grid=(nb,),
        in_specs=ins(fwd) + ins(bwd) + wspecs,
        out_specs=[pl.BlockSpec((br, hw), fwd), pl.BlockSpec((br, hw), bwd),
                   pl.BlockSpec(s_shape, lambda i: (i, 0, 0, 0)),
                   pl.BlockSpec(s_shape, lambda i: (nb - 1 - i, 0, 0, 0))],
        out_shape=[jax.ShapeDtypeStruct((L, hw), F32), jax.ShapeDtypeStruct((L, hw), F32),
                   jax.ShapeDtypeStruct((n_chunks,) + s_shape[1:], MXU_DTYPE),
                   jax.ShapeDtypeStruct((n_chunks,) + s_shape[1:], MXU_DTYPE)],
        scratch_shapes=[pltpu.VMEM(s_shape[1:], F32), pltpu.VMEM(s_shape[1:], F32), pre_scratch, pre_scratch],
        compiler_params=_params(("arbitrary",), VMEM_BIG),
    )(qa, ka, va, za, qa, ka, va, za, wgf, bgf, wgb, bgb)


def _gla_bwd_call(qa, ka, va, za, do, sf, sb, wgf, bgf, wgb, bgb, dep=None):
    L = qa.shape[0]
    br = min(512, L)
    nb, nc = L // br, br // GLA_CHUNK
    hw = GLA_HEADS * HEAD_PAD

    def body(qaf, kaf, vaf, zaf, dof, sf_r, qab, kab, vab, zab, dob, sb_r, wgf_r, bgf_r, wgb_r, bgb_r,
             dqf, dkf, dvf, dzf, dwf, dbf, dqb, dkb, dvb, dzb, dwb, dbb, gt_f, gt_b, pre_f, pre_b):
        @pl.when(pl.program_id(0) == 0)
        def _():
            for ref in (gt_f, gt_b, dwf, dbf, dwb, dbb):
                ref[...] = jnp.zeros_like(ref)

        _gla_block_pre(qaf, kaf, zaf, wgf_r, bgf_r, False, nc, *pre_f[:4], keep=pre_f[4:8])
        _gla_block_pre(qab, kab, zab, wgb_r, bgb_r, True, nc, *pre_b[:4], keep=pre_b[4:8])
        tri_f, tri_b = _tri_masks()
        row_w = lax.broadcasted_iota(jnp.int32, (GLA_CHUNK, HEAD_PAD), 0)

        def one(rev, pre, q_r, k_r, v_r, do_r, s_r, dq_r, dk_r, dv_r, gt, ci):
            qd_s, ki_s, ks_s, dec_s, _, eb_s, enb_s, elb_s, db_s = pre
            tri = tri_b if rev else tri_f
            last_row = 0 if rev else GLA_CHUNK - 1
            rows = pl.ds(pl.multiple_of(ci * GLA_CHUNK, GLA_CHUNK), GLA_CHUNK)
            dec = dec_s[ci]
            heads = range(GLA_HEADS)
            lanes = [slice(HEAD_PAD * h, HEAD_PAD * (h + 1)) for h in heads]
            qd = [qd_s[rows, sl] for sl in lanes]
            ki = [ki_s[rows, sl] for sl in lanes]
            ks = [ks_s[rows, sl] for sl in lanes]
            v = [v_r[rows, sl] for sl in lanes]
            do_h = [_mx(do_r[rows, sl]) for sl in lanes]
            s_t = [s_r[ci, h] for h in heads]
            g_t = [gt[h] for h in heads]
            g_m = [_mx(g_t[h]) for h in heads]
            a = [_dot_nt(qd[h], ki[h]) for h in heads]
            da = [_dot_nt(do_h[h], v[h]) for h in heads]
            dv_carried = [_dot_nt(ks[h], g_m[h]) for h in heads]
            dqd_carried = [_dot(do_h[h], _mx(s_t[h])) for h in heads]
            dks = [_dot(v[h], g_m[h]) for h in heads]
            g_grown = [_dot_tn(do_h[h], qd[h]) for h in heads]
            a = [_mx(jnp.where(tri, a[h], 0.0)) for h in heads]
            da = [_mx(jnp.where(tri, da[h], 0.0)) for h in heads]
            dv_inner = [_dot_tn(a[h], do_h[h]) for h in heads]
            dqd_inner = [_dot(da[h], ki[h]) for h in heads]
            dki = [_dot_tn(da[h], qd[h]) for h in heads]
            dq, dk = [], []
            for h in heads:
                sl = lanes[h]
                dv_r[rows, sl] = (dv_inner[h] + dv_carried[h]).astype(dv_r.dtype)
                ddec = jnp.sum(g_t[h] * s_t[h].astype(F32), axis=0, keepdims=True)
                gt[h] = g_t[h] * dec[:, sl] + g_grown[h]
                dq.append((dqd_inner[h] + dqd_carried[h]) * eb_s[rows, sl] * 0.125)
                dk_state = dks[h] * elb_s[rows, sl]
                dk.append(dki[h] * enb_s[rows, sl] + dk_state)
                k = k_r[rows, sl]
                dblast = jnp.sum(dk_state * k, axis=0, keepdims=True) + dec[:, sl] * ddec
                db_s[rows, sl] = q_r[rows, sl] * dq[h] - k * dk[h] + jnp.where(row_w == last_row, dblast, 0.0)
            low = _low_half(GLA_CHUNK)
            for pair in range(GLA_HEADS // 2):
                psl = slice(HEAD_PAD * pair, HEAD_PAD * (pair + 1))
                for ref, val in ((dq_r, dq), (dk_r, dk)):
                    both = jnp.where(low, val[2 * pair], pltpu.roll(val[2 * pair + 1], 64, 1))
                    ref[rows, psl] = both.astype(ref.dtype)

        def loop(t, carry):
            one(False, pre_f, qaf, kaf, vaf, dof, sf_r, dqf, dkf, dvf, gt_f, nc - 1 - t)
            one(True, pre_b, qab, kab, vab, dob, sb_r, dqb, dkb, dvb, gt_b, t)
            return carry

        lax.fori_loop(0, nc, loop, 0, unroll=True)

        def gate_grads(rev, pre, z_r, w_r, dz_r, dw_r, dbias_r):
            g_s, db_s = pre[4], pre[8]
            back_m = _mx((tri_f if rev else tri_b).astype(F32))
            db = _squeeze_heads(db_s[...])
            dla = jnp.concatenate([_chunk_sums(back_m, db[GLA_CHUNK * c:GLA_CHUNK * (c + 1)]) for c in range(nc)],
                                  axis=0)
            dg = dla * (1.0 / GLA_GATE_NORM) * (1.0 / (1.0 + jnp.exp(g_s[...])))
            dg_m = _mx(dg)
            dz_r[...] = _dot_nt(dg_m, w_r[...])
            dw_r[...] += _dot_tn(_mx(z_r[...]), dg_m)
            dbias_r[...] += jnp.sum(dg, axis=0, keepdims=True)

        gate_grads(False, pre_f, zaf, wgf_r, dzf, dwf, dbf)
        gate_grads(True, pre_b, zab, wgb_r, dzb, dwb, dbb)

    last_first = lambda i: (nb - 1 - i, 0)
    first_last = lambda i: (i, 0)
    s_shape = (nc, GLA_HEADS, HEAD_PAD, HEAD_PAD)

    def ins(m):
        return [pl.BlockSpec((br, hw), m), pl.BlockSpec((br, hw), m), pl.BlockSpec((br, hw), m),
                pl.BlockSpec((br, 128), m), pl.BlockSpec((br, hw), m),
                pl.BlockSpec(s_shape, lambda i: m(i) + (0, 0))]

    def outs(m):
        return [pl.BlockSpec((br, hw // 2), m), pl.BlockSpec((br, hw // 2), m), pl.BlockSpec((br, hw), m),
                pl.BlockSpec((br, 128), m), _full_spec((128, hw // 2)), _full_spec((1, hw // 2))]

    out_shape = [jax.ShapeDtypeStruct((L, hw // 2), MXU_DTYPE)] * 2 + [
        jax.ShapeDtypeStruct((L, hw), MXU_DTYPE),
        jax.ShapeDtypeStruct((L, 128), F32), jax.ShapeDtypeStruct((128, hw // 2), F32),
        jax.ShapeDtypeStruct((1, hw // 2), F32)]
    wspecs = [_full_spec((128, hw // 2)), _full_spec((1, hw // 2))] * 2
    body, extra, extra_specs = _after(body, 16, dep)
    pre_scratch = ([pltpu.VMEM((br, hw), MXU_DTYPE)] * 3 + [pltpu.VMEM((nc, 1, hw), F32)]
                   + [pltpu.VMEM((br, hw // 2), F32)] + [pltpu.VMEM((br, hw), F32)] * 4)
    return pl.pallas_call(
        body, name="gla_bwd", grid=(nb,),
        in_specs=ins(last_first) + ins(first_last) + wspecs + extra_specs,
        out_specs=outs(last_first) + outs(first_last),
        out_shape=out_shape + out_shape,
        scratch_shapes=[pltpu.VMEM(s_shape[1:], F32), pltpu.VMEM(s_shape[1:], F32), pre_scratch, pre_scratch],
        compiler_params=_params(("arbitrary",), VMEM_BIG),
    )(qa, ka, va, za, do, sf, qa, ka, va, za, do, sb, wgf, bgf, wgb, bgb, *extra)


def _t5_buckets(rel):
    nb = REL_BUCKETS // 2
    ret = (rel > 0).astype(np.int32) * nb
    n = np.abs(rel)
    max_exact = nb // 2
    large = max_exact + (np.log(np.maximum(n, 1).astype(np.float32) / max_exact)
                         / math.log(REL_MAX_DIST / max_exact) * (nb - max_exact)).astype(np.int32)
    large = np.minimum(large, nb - 1)
    return ret + np.where(n < max_exact, n, large)


SWA_GROUP = SWA_Q_HEADS // SWA_KV_HEADS
SWA_SPAN = 3 * SWA_BLOCK
SWA_GROUP_LANES = SWA_GROUP * SWA_BLOCK


def _band_buckets():
    s = np.arange(SWA_SPAN)[:, None]
    c = np.arange(SWA_BLOCK)[None, :]
    return _t5_buckets(s - SWA_BLOCK - c).astype(np.int32)


def _swa_valid(n, seq_len):
    key_pos = (n - 1) * SWA_BLOCK + lax.broadcasted_iota(jnp.int32, (SWA_SPAN, 1), 0)
    return (key_pos >= 0) & (key_pos < seq_len)


def _swa_sink_row(sink_r, kv):
    lane = lax.broadcasted_iota(jnp.int32, (1, SWA_GROUP_LANES), 1)
    row = jnp.full((1, SWA_GROUP_LANES), sink_r[kv * SWA_GROUP], F32)
    for g in range(1, SWA_GROUP):
        row = jnp.where(lane >= g * SWA_BLOCK, sink_r[kv * SWA_GROUP + g], row)
    return row


SWA_STEP_BLOCKS = 8


def _swa_group(ref, kv, rows):
    first = kv * SWA_GROUP
    return jnp.concatenate([ref[rows, HEAD_PAD * h:HEAD_PAD * (h + 1)] for h in range(first, first + SWA_GROUP)],
                           axis=0)


def _swa_softmax(scores, bias_t, sink_row, valid):
    st = jnp.where(valid, scores + bias_t, -1e30)
    m = jnp.maximum(jnp.max(st, axis=0, keepdims=True), sink_row)
    p = jnp.exp(st - m)
    e_sink = jnp.exp(sink_row - m)
    inv = 1.0 / (jnp.sum(p, axis=0, keepdims=True) + e_sink)
    return p * inv, e_sink * inv


def _swa_fwd_call(qs, ks, vs, bias, sink, dep=None):
    L = qs.shape[0]

    def block(n, rows, q_r, k_r, v_r, bias_r, sink_r, o_r):
        span = pl.ds(pl.multiple_of(n * SWA_BLOCK, SWA_BLOCK), SWA_SPAN)
        valid = _swa_valid(n, L)
        groups = range(SWA_KV_HEADS)
        lanes = [slice(HEAD_PAD * kv, HEAD_PAD * (kv + 1)) for kv in groups]
        scores = [_dot_nt(k_r[span, lanes[kv]], _swa_group(q_r, kv, rows)) for kv in groups]
        probs = [_swa_softmax(scores[kv], bias_r[kv], _swa_sink_row(sink_r, kv), valid)[0] for kv in groups]
        low = _low_half(SWA_BLOCK)
        for kv in groups:
            og = _dot_tn(_mx(probs[kv]), v_r[span, lanes[kv]])
            for pair in range(SWA_GROUP // 2):
                even = og[2 * SWA_BLOCK * pair:2 * SWA_BLOCK * pair + SWA_BLOCK]
                odd = og[2 * SWA_BLOCK * pair + SWA_BLOCK:2 * SWA_BLOCK * (pair + 1)]
                first = HEAD_PAD * (kv * SWA_GROUP // 2 + pair)
                o_r[rows, first:first + HEAD_PAD] = jnp.where(low, even, pltpu.roll(odd, 64, 1)).astype(o_r.dtype)

    def body(*refs):
        for j in range(SWA_STEP_BLOCKS):
            block(SWA_STEP_BLOCKS * pl.program_id(0) + j, slice(SWA_BLOCK * j, SWA_BLOCK * (j + 1)), *refs)

    qw = SWA_Q_HEADS * HEAD_PAD
    tm = SWA_STEP_BLOCKS * SWA_BLOCK
    body, extra, extra_specs = _after(body, 5, dep)
    return pl.pallas_call(
        body, name="swa_fwd", grid=(L // tm,),
        in_specs=[_row_spec(tm, qw), _vmem_spec(), _vmem_spec(), _vmem_spec(),
                  pl.BlockSpec(memory_space=pltpu.SMEM)] + extra_specs,
        out_specs=_row_spec(tm, qw // 2),
        out_shape=jax.ShapeDtypeStruct((L, qw // 2), MXU_DTYPE),
        compiler_params=_params(("arbitrary",), VMEM_BIG),
    )(qs, ks, vs, bias, sink, *extra)


def _swa_bwd_call(qs, ks, vs, bias, sink, do, dep=None):
    L = qs.shape[0]
    qw = SWA_Q_HEADS * HEAD_PAD
    kw = SWA_KV_HEADS * HEAD_PAD

    def body(*refs):
        dk_r, dv_r, dbias_r, dsink_r = refs[7:]

        @pl.when(pl.program_id(0) == 0)
        def _():
            for ref in (dk_r, dv_r, dbias_r, dsink_r):
                ref[...] = jnp.zeros_like(ref)

        for j in range(SWA_STEP_BLOCKS):
            block(SWA_STEP_BLOCKS * pl.program_id(0) + j, slice(SWA_BLOCK * j, SWA_BLOCK * (j + 1)), *refs)

    def block(n, rows, q_r, k_r, v_r, bias_r, sink_r, do_r, dq_r, dk_r, dv_r, dbias_r, dsink_r):
        span = pl.ds(pl.multiple_of(n * SWA_BLOCK, SWA_BLOCK), SWA_SPAN)
        valid = _swa_valid(n, L)
        groups = range(SWA_KV_HEADS)
        lanes = [slice(HEAD_PAD * kv, HEAD_PAD * (kv + 1)) for kv in groups]
        kk = [k_r[span, sl] for sl in lanes]
        vv = [v_r[span, sl] for sl in lanes]
        qg = [_swa_group(q_r, kv, rows) for kv in groups]
        dog = [_swa_group(do_r, kv, rows) for kv in groups]
        scores = [_dot_nt(kk[kv], qg[kv]) for kv in groups]
        dp = [_dot_nt(vv[kv], dog[kv]) for kv in groups]
        probs = [_swa_softmax(scores[kv], bias_r[kv], _swa_sink_row(sink_r, kv), valid) for kv in groups]
        ds_m, pn_m = [], []
        for kv in groups:
            pn, p_sink = probs[kv]
            delta = jnp.sum(pn * dp[kv], axis=0, keepdims=True)
            ds = pn * (dp[kv] - delta)
            dsink_r[kv] -= p_sink * delta
            dbias_r[kv] += ds
            ds_m.append(_mx(ds))
            pn_m.append(_mx(pn))
        dqg = [_dot_tn(ds_m[kv], kk[kv]) * 0.125 for kv in groups]
        dkk = [_dot(ds_m[kv], qg[kv]) for kv in groups]
        dvv = [_dot(pn_m[kv], dog[kv]) for kv in groups]
        low = _low_half(SWA_BLOCK)
        for kv in groups:
            for pair in range(SWA_GROUP // 2):
                even = dqg[kv][2 * SWA_BLOCK * pair:2 * SWA_BLOCK * pair + SWA_BLOCK]
                odd = dqg[kv][2 * SWA_BLOCK * pair + SWA_BLOCK:2 * SWA_BLOCK * (pair + 1)]
                first = HEAD_PAD * (kv * SWA_GROUP // 2 + pair)
                dq_r[rows, first:first + HEAD_PAD] = jnp.where(low, even, pltpu.roll(odd, 64, 1)).astype(dq_r.dtype)
            dk_r[span, lanes[kv]] += dkk[kv]
            dv_r[span, lanes[kv]] += dvv[kv]

    tm = SWA_STEP_BLOCKS * SWA_BLOCK
    body, extra, extra_specs = _after(body, 6, dep)
    return pl.pallas_call(
        body, name="swa_bwd", grid=(L // tm,),
        in_specs=[_row_spec(tm, qw), _vmem_spec(), _vmem_spec(), _vmem_spec(),
                  pl.BlockSpec(memory_space=pltpu.SMEM), _row_spec(tm, qw)] + extra_specs,
        out_specs=[_row_spec(tm, qw // 2), _vmem_spec(), _vmem_spec(), _vmem_spec(), _vmem_spec()],
        out_shape=[jax.ShapeDtypeStruct((L, qw // 2), MXU_DTYPE),
                   jax.ShapeDtypeStruct((L + 2 * SWA_BLOCK, kw), F32),
                   jax.ShapeDtypeStruct((L + 2 * SWA_BLOCK, kw), F32),
                   jax.ShapeDtypeStruct((SWA_KV_HEADS, SWA_SPAN, SWA_GROUP_LANES), F32),
                   jax.ShapeDtypeStruct((SWA_KV_HEADS, 1, SWA_GROUP_LANES), F32)],
        compiler_params=_params(("arbitrary",), VMEM_BIG),
    )(qs, ks, vs, bias, sink, do, *extra)


def _bias_call(rel_bias, buckets, dep=None):
    def body(t_r, bk_r, o_r):
        bk = bk_r[...]
        s = lax.broadcasted_iota(jnp.int32, bk.shape, 0)
        c = lax.broadcasted_iota(jnp.int32, bk.shape, 1)
        in_band = jnp.abs(s - SWA_BLOCK - c) <= SWA_BLOCK
        for h in range(SWA_Q_HEADS):
            acc = jnp.zeros(bk.shape, F32)
            for b in range(REL_BUCKETS):
                acc = jnp.where(bk == b, t_r[b, h], acc)
            g = h % SWA_GROUP
            o_r[h // SWA_GROUP, :, SWA_BLOCK * g:SWA_BLOCK * (g + 1)] = jnp.where(in_band, acc, -1e30)

    body, extra, extra_specs = _after(body, 2, dep)
    return pl.pallas_call(
        body, name="band_bias",
        in_specs=[pl.BlockSpec(memory_space=pltpu.SMEM), _vmem_spec()] + extra_specs, out_specs=_vmem_spec(),
        out_shape=jax.ShapeDtypeStruct((SWA_KV_HEADS, SWA_SPAN, SWA_GROUP_LANES), F32),
    )(rel_bias, buckets, *extra)


def _relbias_call(dbias, dsink, buckets, dep=None):
    def body(db_r, ds_r, bk_r, o_r, os_r):
        bk = bk_r[...]
        rowi = lax.broadcasted_iota(jnp.int32, (REL_BUCKETS, 128), 0)
        lanei = lax.broadcasted_iota(jnp.int32, (REL_BUCKETS, 128), 1)
        lane1 = lax.broadcasted_iota(jnp.int32, (1, 128), 1)
        acc = jnp.zeros((REL_BUCKETS, 128), F32)
        acc_sink = jnp.zeros((1, 128), F32)
        heads = [(h // SWA_GROUP, slice(SWA_BLOCK * (h % SWA_GROUP), SWA_BLOCK * (h % SWA_GROUP + 1)))
                 for h in range(SWA_Q_HEADS)]
        for b in range(REL_BUCKETS):
            in_bucket = bk == b
            for h, (kv, lanes) in enumerate(heads):
                s = jnp.sum(jnp.where(in_bucket, db_r[kv, :, lanes], 0.0))
                acc = acc + jnp.where((rowi == b) & (lanei == h), s, 0.0)
        for h, (kv, lanes) in enumerate(heads):
            acc_sink = acc_sink + jnp.where(lane1 == h, jnp.sum(ds_r[kv, :, lanes]), 0.0)
        o_r[...] = acc
        os_r[...] = acc_sink

    body, extra, extra_specs = _after(body, 3, dep)
    return pl.pallas_call(
        body, name="relbias_grad",
        in_specs=[_vmem_spec()] * 3 + extra_specs, out_specs=[_vmem_spec()] * 2,
        out_shape=[jax.ShapeDtypeStruct((REL_BUCKETS, 128), F32), jax.ShapeDtypeStruct((1, 128), F32)],
    )(dbias, dsink, buckets, *extra)


def _mix_call(o_f, o_b, ga, o_s, x, gn, w_out_p, g_post, g_pre2, dep=None):
    L = x.shape[0]
    tm = min(512, L)
    hw = GLA_HEADS * HEAD_PAD

    def body(of_r, ob_r, ga_r, os_r, x_r, gn_r, w_r, gp_r, g2_r, cat_r, mix_r, h1_r, n2_r):
        gn_v = gn_r[...]
        for h in range(GLA_HEADS):
            sl = slice(HEAD_PAD * h, HEAD_PAD * (h + 1))
            oh = of_r[:, sl] + ob_r[:, sl]
            on = oh * _rms_r(oh) * gn_v
            gate = ga_r[:, sl]
            cat_r[:, sl] = (on * (gate * jax.nn.sigmoid(gate))).astype(cat_r.dtype)
        os_v = os_r[...]
        cat_r[:, hw:] = os_v
        mix = _dot(cat_r[:, :hw], w_r[:hw, :]) + _dot(os_v, w_r[hw:, :])
        mix_r[...] = mix
        h1 = x_r[...] + mix * _rms_r(mix) * gp_r[...]
        h1_r[...] = h1
        n2_r[...] = (h1 * _rms_r(h1) * g2_r[...]).astype(n2_r.dtype)

    body, extra, extra_specs = _after(body, 9, dep)
    return pl.pallas_call(
        body, name="mix_fwd", grid=(L // tm,),
        in_specs=[_row_spec(tm, hw), _row_spec(tm, hw), _row_spec(tm, hw), _row_spec(tm, OUT_PAD - hw),
                  _row_spec(tm, D_MODEL), _full_spec((1, HEAD_PAD)), _vmem_spec(),
                  _full_spec((1, D_MODEL)), _full_spec((1, D_MODEL))] + extra_specs,
        out_specs=[_row_spec(tm, OUT_PAD), _row_spec(tm, D_MODEL), _row_spec(tm, D_MODEL), _row_spec(tm, D_MODEL)],
        out_shape=[jax.ShapeDtypeStruct((L, OUT_PAD), MXU_DTYPE), jax.ShapeDtypeStruct((L, D_MODEL), F32),
                   jax.ShapeDtypeStruct((L, D_MODEL), F32), jax.ShapeDtypeStruct((L, D_MODEL), MXU_DTYPE)],
        compiler_params=_params(("arbitrary",), VMEM_BIG),
    )(o_f, o_b, ga, o_s, x, gn, w_out_p, g_post, g_pre2, *extra)


def _mlp_fwd_call(n2, h1, tgt, w_ud, g_post):
    L = n2.shape[0]
    tm = min(512, L)
    blk = D_FF // N_CHIPS

    def body(n2_r, h1_r, t_r, w_r, g_r, a_r, rz_r, dh2_r, dff_r, loss_r, dg_r):
        @pl.when(pl.program_id(0) == 0)
        def _():
            loss_r[...] = jnp.zeros_like(loss_r)
            dg_r[...] = jnp.zeros_like(dg_r)

        n2v = n2_r[...]
        ff = jnp.zeros((tm, D_MODEL), F32)
        for j in range(N_CHIPS):
            sl = slice(blk * j, blk * (j + 1))
            rz = jnp.maximum(_dot(n2v, w_r[j, 0]), 0.0)
            a = _mx(rz * rz)
            rz_r[:, sl] = rz.astype(rz_r.dtype)
            a_r[:, sl] = a
            ff = ff + _dot(a, w_r[j, 1])
        g = g_r[...]
        r = _rms_r(ff)
        err = h1_r[...] + ff * r * g - t_r[...]
        loss_r[...] += 0.5 * jnp.sum(err * err) / D_MODEL
        dh2 = err * (1.0 / D_MODEL)
        dh2_r[...] = dh2
        dff, dg = _rms_bwd(ff, r, g, dh2)
        dff_r[...] = dff.astype(dff_r.dtype)
        dg_r[...] += dg

    return pl.pallas_call(
        body, name="mlp_fwd", grid=(L // tm,),
        in_specs=[_row_spec(tm, D_MODEL), _row_spec(tm, D_MODEL), _row_spec(tm, D_MODEL),
                  _vmem_spec(), _full_spec((1, D_MODEL))],
        out_specs=[_row_spec(tm, D_FF), _row_spec(tm, D_FF), _row_spec(tm, D_MODEL), _row_spec(tm, D_MODEL),
                   _full_spec((1, 128)), _full_spec((1, D_MODEL))],
        out_shape=[jax.ShapeDtypeStruct((L, D_FF), MXU_DTYPE), jax.ShapeDtypeStruct((L, D_FF), MXU_DTYPE),
                   jax.ShapeDtypeStruct((L, D_MODEL), F32), jax.ShapeDtypeStruct((L, D_MODEL), MXU_DTYPE),
                   jax.ShapeDtypeStruct((1, 128), F32), jax.ShapeDtypeStruct((1, D_MODEL), F32)],
        compiler_params=_params(("arbitrary",), VMEM_BIG),
    )(n2, h1, tgt, w_ud, g_post)


def _mix_mlp_fwd_call(o_f, o_b, ga, o_s, x, tgt, gn, w_out_p, g_post, g_pre2, w_ud, g_post2):
    L = x.shape[0]
    tm = min(256, L)
    hw = GLA_HEADS * HEAD_PAD
    blk = D_FF // N_CHIPS

    def body(of_r, ob_r, ga_r, os_r, x_r, t_r, gn_r, w_r, gp_r, g2_r, wud_r, g3_r,
             cat_r, h1_r, n2_r, a_r, rz_r, dh2_r, dff_r, loss_r, dg_r):
        @pl.when(pl.program_id(0) == 0)
        def _():
            loss_r[...] = jnp.zeros_like(loss_r)
            dg_r[...] = jnp.zeros_like(dg_r)

        gn_v = gn_r[...]
        for h in range(GLA_HEADS):
            sl = slice(HEAD_PAD * h, HEAD_PAD * (h + 1))
            oh = of_r[:, sl] + ob_r[:, sl]
            on = oh * _rms_r(oh) * gn_v
            gate = ga_r[:, sl]
            cat_r[:, sl] = (on * (gate * jax.nn.sigmoid(gate))).astype(cat_r.dtype)
        os_v = os_r[...]
        cat_r[:, hw:] = os_v
        mix = _dot(cat_r[:, :hw], w_r[:hw, :]) + _dot(os_v, w_r[hw:, :])
        h1 = x_r[...] + mix * _rms_r(mix) * gp_r[...]
        h1_r[...] = h1
        n2v =(h1 * _rms_r(h1) * g2_r[...]).astype(n2_r.dtype)
        n2_r[...] = n2v

        ff = jnp.zeros((tm, D_MODEL), F32)
        for j in range(N_CHIPS):
            sl = slice(blk * j, blk * (j + 1))
            rz = jnp.maximum(_dot(n2v, wud_r[j, 0]), 0.0)
            a = _mx(rz * rz)
            rz_r[:, sl] = rz.astype(rz_r.dtype)
            a_r[:, sl] = a
            ff = ff + _dot(a, wud_r[j, 1])
        g = g3_r[...]
        r = _rms_r(ff)
        err = h1 + ff * r * g - t_r[...]
        loss_r[...] += 0.5 * jnp.sum(err * err) / D_MODEL
        dh2 = err * (1.0 / D_MODEL)
        dh2_r[...] = dh2
        dff, dg = _rms_bwd(ff, r, g, dh2)
        dff_r[...] = dff.astype(dff_r.dtype)
        dg_r[...] += dg

    return pl.pallas_call(
        body, name="mix_mlp_fwd", grid=(L // tm,),
        in_specs=[_row_spec(tm, hw), _row_spec(tm, hw), _row_spec(tm, hw), _row_spec(tm, OUT_PAD - hw),
                  _row_spec(tm, D_MODEL), _row_spec(tm, D_MODEL), _full_spec((1, HEAD_PAD)), _vmem_spec(),
                  _full_spec((1, D_MODEL)), _full_spec((1, D_MODEL)), _vmem_spec(), _full_spec((1, D_MODEL))],
        out_specs=[_row_spec(tm, OUT_PAD), _row_spec(tm, D_MODEL), _row_spec(tm, D_MODEL),
                   _row_spec(tm, D_FF), _row_spec(tm, D_FF), _row_spec(tm, D_MODEL), _row_spec(tm, D_MODEL),
                   _full_spec((1, 128)), _full_spec((1, D_MODEL))],
        out_shape=[jax.ShapeDtypeStruct((L, OUT_PAD), MXU_DTYPE),
                   jax.ShapeDtypeStruct((L, D_MODEL), F32), jax.ShapeDtypeStruct((L, D_MODEL), MXU_DTYPE),
                   jax.ShapeDtypeStruct((L, D_FF), MXU_DTYPE), jax.ShapeDtypeStruct((L, D_FF), MXU_DTYPE),
                   jax.ShapeDtypeStruct((L, D_MODEL), F32), jax.ShapeDtypeStruct((L, D_MODEL), MXU_DTYPE),
                   jax.ShapeDtypeStruct((1, 128), F32), jax.ShapeDtypeStruct((1, D_MODEL), F32)],
        compiler_params=_params(("arbitrary",), VMEM_BIG),
    )(o_f, o_b, ga, o_s, x, tgt, gn, w_out_p, g_post, g_pre2, w_ud, g_post2)


def _mlp_bwd_call(dff, rz, w_ud):
    L = dff.shape[0]
    tm = min(512, L)
    blk = D_FF // N_CHIPS

    def body(dff_r, rz_r, w_r, dz_r, dn2_r):
        dffv = dff_r[...]
        dn2 = jnp.zeros((tm, D_MODEL), F32)
        for j in range(N_CHIPS):
            sl = slice(blk * j, blk * (j + 1))
            dz = _mx(_dot_nt(dffv, w_r[j, 1]) * 2.0 * rz_r[:, sl].astype(F32))
            dz_r[:, sl] = dz
            dn2 = dn2 + _dot_nt(dz, w_r[j, 0])
        dn2_r[...] = dn2

    return pl.pallas_call(
        body, name="mlp_bwd", grid=(L // tm,),
        in_specs=[_row_spec(tm, D_MODEL), _row_spec(tm, D_FF), _vmem_spec()],
        out_specs=[_row_spec(tm, D_FF), _row_spec(tm, D_MODEL)],
        out_shape=[jax.ShapeDtypeStruct((L, D_FF), MXU_DTYPE), jax.ShapeDtypeStruct((L, D_MODEL), F32)],
        compiler_params=_params(("arbitrary",), VMEM_BIG),
    )(dff, rz, w_ud)


def _mlp_wgrad_call(a, dff, n2, dz):
    L = a.shape[0]
    tf = 512
    per = (D_FF // N_CHIPS) // tf

    def body(a_r, dff_r, n2_r, dz_r, dwd_r, dwu_r):
        dwd_r[...] = _dot_tn(a_r[...], dff_r[...])
        dwu_r[...] = _dot_tn(n2_r[...], dz_r[...])

    return pl.pallas_call(
        body, name="mlp_wgrad", grid=(D_FF // tf,),
        in_specs=[pl.BlockSpec((L, tf), lambda j: (0, j)), _vmem_spec(), _vmem_spec(),
                  pl.BlockSpec((L, tf), lambda j: (0, j))],
        out_specs=[pl.BlockSpec((tf, D_MODEL), lambda j: (j, 0)),
                   pl.BlockSpec((None, D_MODEL, tf), lambda j: (j // per, 0, j % per))],
        out_shape=[jax.ShapeDtypeStruct((D_FF, D_MODEL), F32),
                   jax.ShapeDtypeStruct((N_CHIPS, D_MODEL, D_FF // N_CHIPS), F32)],
        compiler_params=_params(("arbitrary",), VMEM_BIG),
    )(a, dff, n2, dz)


def _mix_bwd_call(dn2, dh2, h1, mix, cat, o_f, o_b, ga, gn, g_post, g_pre2, w_out_p):
    L = dn2.shape[0]
    tm = min(512, L)
    hw = GLA_HEADS * HEAD_PAD

    def body(dn2_r, dh2_r, h1_r, mix_r, cat_r, of_r, ob_r, ga_r, gn_r, gp_r, g2_r, w_r,
             dh1_r, do_r, dga_r, dos_r, dw_r, dg2_r, dgp_r, dgn_r):
        @pl.when(pl.program_id(0) == 0)
        def _():
            for ref in (dw_r, dg2_r, dgp_r, dgn_r):
                ref[...] = jnp.zeros_like(ref)

        parts = [slice(start, start + min(256, tm)) for start in range(0, tm, 256)]
        dmix_m = []
        for rs in parts:
            h1 = h1_r[rs, :]
            dx2, dg2 = _rms_bwd(h1, _rms_r(h1), g2_r[...], dn2_r[rs, :])
            dh1 = dh2_r[rs, :] + dx2
            dh1_r[rs, :] = dh1
            dg2_r[...] += dg2
            mix = mix_r[rs, :]
            dmix, dgp = _rms_bwd(mix, _rms_r(mix), gp_r[...], dh1)
            dgp_r[...] += dgp
            dmix_m.append(_mx(dmix))
        dcat = [_dot_nt(d, w_r[...]) for d in dmix_m]
        for rs, d in zip(parts, dmix_m):
            dw_r[...] += _dot_tn(cat_r[rs, :], d)
        gn_v = gn_r[...]
        dgn = jnp.zeros((1, HEAD_PAD), F32)
        for rs, dc in zip(parts, dcat):
            dos_r[rs, :] = _spread_heads(dc[:, hw:]).astype(dos_r.dtype)
            for h in range(GLA_HEADS):
                sl = slice(HEAD_PAD * h, HEAD_PAD * (h + 1))
                oh = of_r[rs, sl] + ob_r[rs, sl]
                rr = _rms_r(oh)
                xh = oh * rr
                gate = ga_r[rs, sl]
                sg = jax.nn.sigmoid(gate)
                silu = gate * sg
                doa = dc[:, sl]
                dga_r[rs, sl] = (doa * (xh * gn_v) * (sg + silu * (1.0 - sg))).astype(dga_r.dtype)
                don = doa * silu
                gd = don * gn_v
                do_r[rs, sl] = rr * (gd - xh * jnp.mean(gd * xh, axis=-1, keepdims=True))
                dgn = dgn + jnp.sum(don * xh, axis=0, keepdims=True)
        dgn_r[...] += dgn

    return pl.pallas_call(
        body, name="mix_bwd", grid=(L // tm,),
        in_specs=[_row_spec(tm, D_MODEL)] * 4 + [_row_spec(tm, OUT_PAD)] + [_row_spec(tm, hw)] * 3
        + [_full_spec((1, HEAD_PAD)), _full_spec((1, D_MODEL)), _full_spec((1, D_MODEL)), _vmem_spec()],
        out_specs=[_row_spec(tm, D_MODEL), _row_spec(tm, hw), _row_spec(tm, hw),
                   _row_spec(tm, SWA_Q_HEADS * HEAD_PAD),
                   _full_spec((OUT_PAD, D_MODEL)), _full_spec((1, D_MODEL)), _full_spec((1, D_MODEL)),
                   _full_spec((1, HEAD_PAD))],
        out_shape=[jax.ShapeDtypeStruct((L, D_MODEL), F32), jax.ShapeDtypeStruct((L, hw), F32),
                   jax.ShapeDtypeStruct((L, hw), MXU_DTYPE),
                   jax.ShapeDtypeStruct((L, SWA_Q_HEADS * HEAD_PAD), MXU_DTYPE),
                   jax.ShapeDtypeStruct((OUT_PAD, D_MODEL), F32), jax.ShapeDtypeStruct((1, D_MODEL), F32),
                   jax.ShapeDtypeStruct((1, D_MODEL), F32), jax.ShapeDtypeStruct((1, HEAD_PAD), F32)],
        compiler_params=_params(("arbitrary",), VMEM_BIG),
    )(dn2, dh2, h1, mix, cat, o_f, o_b, ga, gn, g_post, g_pre2, w_out_p)


def _mlp_mix_bwd_call(dff, rz, w_ud, dh2, h1, cat, o_f, o_b, ga, gn, g_post, g_pre2, w_out_p):
    L = dff.shape[0]
    tm = min(256, L)
    hw = GLA_HEADS * HEAD_PAD
    blk = D_FF // N_CHIPS

    def body(dff_r, rz_r, wud_r, dh2_r, h1_r, cat_r, of_r, ob_r, ga_r, gn_r, gp_r, g2_r, w_r,
             dz_r, dh1_r, do_r, dga_r, dos_r, dw_r, dg2_r, dgp_r, dgn_r):
        @pl.when(pl.program_id(0) == 0)
        def _():
            for ref in (dw_r, dg2_r, dgp_r, dgn_r):
                ref[...] = jnp.zeros_like(ref)

        dffv = dff_r[...]
        dn2 = jnp.zeros((tm, D_MODEL), F32)
        for j in range(N_CHIPS):
            sl = slice(blk * j, blk * (j + 1))
            dz = _mx(_dot_nt(dffv, wud_r[j, 1]) * 2.0 * rz_r[:, sl].astype(F32))
            dz_r[:, sl] = dz
            dn2 = dn2 + _dot_nt(dz, wud_r[j, 0])

        h1 = h1_r[...]
        dx2, dg2 = _rms_bwd(h1, _rms_r(h1), g2_r[...], dn2)
        dh1 = dh2_r[...] + dx2
        dh1_r[...] = dh1
        dg2_r[...] += dg2
        mix = _dot(cat_r[:, :hw], w_r[:hw, :]) + _dot(cat_r[:, hw:], w_r[hw:, :])
        dmix, dgp = _rms_bwd(mix, _rms_r(mix), gp_r[...], dh1)
        dgp_r[...] += dgp
        dmix_m = _mx(dmix)
        dc = _dot_nt(dmix_m, w_r[...])
        dw_r[...] += _dot_tn(cat_r[...], dmix_m)
        gn_v = gn_r[...]
        dgn = jnp.zeros((1, HEAD_PAD), F32)
        dos_r[...] = _spread_heads(dc[:, hw:]).astype(dos_r.dtype)
        for h in range(GLA_HEADS):
            sl = slice(HEAD_PAD * h, HEAD_PAD * (h + 1))
            oh = of_r[:, sl] + ob_r[:, sl]
            rr = _rms_r(oh)
            xh = oh * rr
            gate = ga_r[:, sl]
            sg = jax.nn.sigmoid(gate)
            silu = gate * sg
            doa = dc[:, sl]
            dga_r[:, sl] = (doa * (xh * gn_v) * (sg + silu * (1.0 - sg))).astype(dga_r.dtype)
            don = doa * silu
            gd = don * gn_v
            do_r[:, sl] = rr * (gd - xh * jnp.mean(gd * xh, axis=-1, keepdims=True))
            dgn = dgn + jnp.sum(don * xh, axis=0, keepdims=True)
        dgn_r[...] += dgn

    return pl.pallas_call(
        body, name="mlp_mix_bwd", grid=(L // tm,),
        in_specs=[_row_spec(tm, D_MODEL), _row_spec(tm, D_FF), _vmem_spec()] + [_row_spec(tm, D_MODEL)] * 2
        + [_row_spec(tm, OUT_PAD)] + [_row_spec(tm, hw)] * 3
        + [_full_spec((1, HEAD_PAD)), _full_spec((1, D_MODEL)), _full_spec((1, D_MODEL)), _vmem_spec()],
        out_specs=[_row_spec(tm, D_FF), _row_spec(tm, D_MODEL), _row_spec(tm, hw), _row_spec(tm, hw),
                   _row_spec(tm, SWA_Q_HEADS * HEAD_PAD),
                   _full_spec((OUT_PAD, D_MODEL)), _full_spec((1, D_MODEL)), _full_spec((1, D_MODEL)),
                   _full_spec((1, HEAD_PAD))],
        out_shape=[jax.ShapeDtypeStruct((L, D_FF), MXU_DTYPE),
                   jax.ShapeDtypeStruct((L, D_MODEL), F32), jax.ShapeDtypeStruct((L, hw), F32),
                   jax.ShapeDtypeStruct((L, hw), MXU_DTYPE),
                   jax.ShapeDtypeStruct((L, SWA_Q_HEADS * HEAD_PAD), MXU_DTYPE),
                   jax.ShapeDtypeStruct((OUT_PAD, D_MODEL), F32), jax.ShapeDtypeStruct((1, D_MODEL), F32),
                   jax.ShapeDtypeStruct((1, D_MODEL), F32), jax.ShapeDtypeStruct((1, HEAD_PAD), F32)],
        compiler_params=_params(("arbitrary",), VMEM_BIG),
    )(dff, rz, w_ud, dh2, h1, cat, o_f, o_b, ga, gn, g_post, g_pre2, w_out_p)


def _in_bwd_call(x, dh1, g_pre, w_in_t, pairs, singles, halos, dep=None):
    L = x.shape[0]
    tm = min(512, L)
    per = tm // SWA_BLOCK
    n_pair, n_single, n_halo = len(pairs), len(singles), len(halos)
    groups = [c for c, _ in pairs] + [c for c, _ in singles] + [c for c, _ in halos]

    def body(*refs):
        x_r, dh1_r, g_r, w_r = refs[:4]
        pair_refs = refs[4:4 + 2 * n_pair]
        single_refs = refs[4 + 2 * n_pair:4 + 2 * n_pair + n_single]
        halo_refs = refs[4 + 2 * n_pair + n_single:4 + 2 * n_pair + n_single + per * n_halo]
        dx_r, dw_r, dg_r = refs[4 + 2 * n_pair + n_single + per * n_halo:]

        @pl.when(pl.program_id(0) == 0)
        def _():
            dw_r[...] = jnp.zeros_like(dw_r)
            dg_r[...] = jnp.zeros_like(dg_r)

        xv = x_r[...]
        r = _rms_r(xv)
        g = g_r[...]
        u = _mx(xv * r * g)
        vals = [pair_refs[2 * i][...].astype(F32) + pair_refs[2 * i + 1][...].astype(F32) for i in range(n_pair)]
        vals += [ref[...].astype(F32) for ref in single_refs]
        vals += [jnp.concatenate([ref[...] for ref in halo_refs[per * i:per * (i + 1)]], axis=0)
                 for i in range(n_halo)]
        ds = [_mx(_squeeze_heads(val) if heads else val) for (_, _, heads), val in zip(groups, vals)]
        du = jnp.zeros((tm, D_MODEL), F32)
        for (first, rows, _), d in zip(groups, ds):
            du = du + _dot(d, w_r[first:first + rows, :])
        for (first, rows, _), d in zip(groups, ds):
            dw_r[first:first + rows, :] += _dot_tn(d, u)
        dx, dg = _rms_bwd(xv, r, g, du)
        dx_r[...] = dh1_r[...] + dx
        dg_r[...] += dg

    arrays = [a for _, pr in pairs for a in pr] + [a for _, a in singles]
    specs = [_row_spec(tm, a.shape[1]) for a in arrays]
    for _, a in halos:
        specs += [pl.BlockSpec((SWA_BLOCK, a.shape[1]), lambda i, j=j: (per * i + 1 + j, 0)) for j in range(per)]
        arrays += [a] * per
    body, extra, extra_specs = _after(body, 4 + len(arrays), dep)
    return pl.pallas_call(
        body, name="in_bwd", grid=(L // tm,),
        in_specs=[_row_spec(tm, D_MODEL), _row_spec(tm, D_MODEL), _full_spec((1, D_MODEL)), _vmem_spec()] + specs
        + extra_specs,
        out_specs=[_row_spec(tm, D_MODEL), _full_spec((IN_COLS, D_MODEL)), _full_spec((1, D_MODEL))],
        out_shape=[jax.ShapeDtypeStruct((L, D_MODEL), F32), jax.ShapeDtypeStruct((IN_COLS, D_MODEL), F32),
                   jax.ShapeDtypeStruct((1, D_MODEL), F32)],
        compiler_params=_params(("arbitrary",), VMEM_BIG),
    )(x, dh1, g_pre, w_in_t, *arrays, *extra)


def _adamw_math(w, g, m, v):
    m = ADAM_B1 * m + (1.0 - ADAM_B1) * g
    v = ADAM_B2 * v + (1.0 - ADAM_B2) * (g * g)
    m_hat = m / (1.0 - ADAM_B1 ** ADAM_STEP)
    v_hat = v / (1.0 - ADAM_B2 ** ADAM_STEP)
    delta = -ADAM_LR * (m_hat / (jnp.sqrt(v_hat) + ADAM_EPS) + ADAM_WD * w)
    return delta, m, v


def _adamw_call(w, g, m, v, name, dep=None):
    rows, cols = w.shape
    tr = min(256, rows)

    def body(w_r, g_r, m_r, v_r, g_out_r, d_r, nm_r, nv_r):
        g = g_r[...]
        g_out_r[...] = g
        d_r[...], nm_r[...], nv_r[...] = _adamw_math(w_r[...], g, m_r[...], v_r[...])

    if rows % tr == 0:
        spec, steps = _row_spec(tr, cols), rows // tr
    else:
        spec, steps = pl.BlockSpec((rows, 256), lambda i: (0, i)), cols // 256
    body, extra, extra_specs = _after(body, 4, dep)
    return pl.pallas_call(
        body, name=name, grid=(steps,),
        in_specs=[spec] * 4 + extra_specs, out_specs=[spec] * 4,
        out_shape=[jax.ShapeDtypeStruct(w.shape, F32)] * 4,
        compiler_params=_params(("arbitrary",)),
    )(w, g, m, v, *extra)


def _position():
    return lax.axis_index("x"), lax.axis_index("y"), lax.axis_index("c")


def _other_chips(x, y):
    return [(1 - x, y), (x, 1 - y), (1 - x, 1 - y)]


ROWS, COLS = -2, -1


def _half(ref, which, axis):
    size = ref.shape[axis] // 2
    span = pl.ds(pl.multiple_of(which * size, 16 if axis == ROWS else 128), size)
    index = [slice(None)] * len(ref.shape)
    index[axis] = span
    return ref.at[tuple(index)]


def _quarter(ref, half, which, axis):
    size = ref.shape[axis] // 4
    span = pl.ds(pl.multiple_of((2 * half + which) * size, 16 if axis == ROWS else 128), size)
    index = [slice(None)] * len(ref.shape)
    index[axis] = span
    return ref.at[tuple(index)]


def _first_gather_call(shards, axes, routed):
    n = len(shards)
    per = 7

    def body(*refs):
        srcs, outs = refs[:n], refs[n:2 * n]
        send_sems, recv_sems, local_sems = refs[2 * n:]
        x, y, c = _position()
        me, sibling = (x, y, c), (x, y, 1 - c)
        x_side, y_side, across = _other_chips(x, y)
        local = [pltpu.make_async_copy(srcs[a], outs[a].at[2 * x + y], local_sems.at[a]) for a in range(n)]
        for cp in local:
            cp.start()

        def copy(a, k, dst, to, src=None):
            return pltpu.make_async_remote_copy(
                src_ref=dst if src is None else src, dst_ref=dst, send_sem=send_sems.at[per * a + k],
                recv_sem=recv_sems.at[per * a + k], device_id=to, device_id_type=MESH_ID)

        def half(a, chip, pc):
            return _half(outs[a].at[2 * chip[0] + chip[1]], pc, axes[a])

        def quarter(a, chip, q):
            return _quarter(outs[a].at[2 * chip[0] + chip[1]], c, q, axes[a])

        sends = []
        for a in range(n):
            mine = _half(srcs[a], c, axes[a])
            targets = (x_side, y_side) if routed[a] else (x_side, y_side, across)
            sends += [copy(a, j, half(a, (x, y), c), (*chip, c), src=mine) for j, chip in enumerate(targets)]
        for cp in sends:
            cp.start()
        for a in range(n):
            for j, chip in enumerate((x_side, y_side)):
                copy(a, j, half(a, chip, c), me).wait_recv()
                if routed[a]:
                    other = (y_side, x_side)[j]
                    sends.append(copy(a, 2 + j, quarter(a, chip, j), (*other, c)))
                    sends[-1].start()
                sends.append(copy(a, 4 + j, half(a, chip, c), sibling))
                sends[-1].start()
        for a in range(n):
            if routed[a]:
                for j in range(2):
                    copy(a, 2 + j, quarter(a, across, j), me).wait_recv()
            else:
                copy(a, 2, half(a, across, c), me).wait_recv()
            sends.append(copy(a, 6, half(a, across, c), sibling))
            sends[-1].start()
        for a in range(n):
            for k, chip in ((4, x_side), (5, y_side), (6, across)):
                copy(a, k, half(a, chip, 1 - c), me).wait_recv()
        for cp in sends:
            cp.wait_send()
        for cp in local:
            cp.wait()

    return pl.pallas_call(
        body, name="first_gather",
        in_specs=[_any_spec()] * n, out_specs=[_any_spec()] * n,
        out_shape=[jax.ShapeDtypeStruct((N_CHIPS,) + s.shape, s.dtype) for s in shards],
        scratch_shapes=[pltpu.SemaphoreType.DMA((per * n,)), pltpu.SemaphoreType.DMA((per * n,)),
                        pltpu.SemaphoreType.DMA((n,))],
    )(*shards)


PAIR_PEERS, CHIP_PEERS = 1, 2


def _peers(which):
    x, y, c = _position()
    if which == PAIR_PEERS:
        return [(x, y, 1 - c)]
    return [(px, py, c) for px, py in _other_chips(x, y)]


def _split_start(name, arrays, n_copies, plan, peers=None):
    n = len(arrays)

    def body(*refs):
        ins, send_sems, recv_sems, token = refs[:n], refs[n], refs[n + 1], refs[-1]
        if peers is not None:
            barrier = pltpu.get_barrier_semaphore()
            targets = _peers(peers)
            for target in targets:
                pl.semaphore_signal(barrier, inc=1, device_id=target, device_id_type=MESH_ID)
            pl.semaphore_wait(barrier, len(targets))
        for k, (src, dst, to, _) in enumerate(plan(ins)):
            pltpu.make_async_remote_copy(src_ref=src, dst_ref=dst, send_sem=send_sems.at[k],
                                         recv_sem=recv_sems.at[k], device_id=to, device_id_type=MESH_ID).start()
        token[...] = jnp.zeros_like(token)

    hbm = pl.BlockSpec(memory_space=pltpu.HBM)
    sem = pl.BlockSpec(memory_space=pltpu.SEMAPHORE)
    out = pl.pallas_call(
        body, name=name,
        out_shape=(pltpu.SemaphoreType.DMA((n_copies,)), pltpu.SemaphoreType.DMA((n_copies,)))
        + tuple(pltpu.HBM(a.shape, a.dtype) for a in arrays) + (jax.ShapeDtypeStruct((8, 128), F32),),
        in_specs=[hbm] * n, out_specs=(sem, sem) + (hbm,) * n + (_vmem_spec(),),
        input_output_aliases={i: 2 + i for i in range(n)},
        compiler_params=pltpu.CompilerParams(has_side_effects=pltpu.SideEffectType.DATAFLOW_SIDE_EFFECTING,
                                             collective_id=peers),
    )(*[pltpu.with_memory_space_constraint(a, pltpu.HBM) for a in arrays])
    return (out[0], out[1], tuple(out[2:2 + n])), out[-1]


def _split_wait(name, handle, n_copies, plan, after):
    send_sems, recv_sems, arrays = handle
    n = len(arrays)

    def body(*refs):
        ins, s_sems, r_sems = refs[:n], refs[n], refs[n + 1]
        for k, (src, dst, to, landed) in enumerate(plan(ins)):
            cp = pltpu.make_async_remote_copy(src_ref=src, dst_ref=landed, send_sem=s_sems.at[k],
                                              recv_sem=r_sems.at[k], device_id=to, device_id_type=MESH_ID)
            cp.wait_send()
            cp.wait_recv()

    hbm = pl.BlockSpec(memory_space=pltpu.HBM)
    sem = pl.BlockSpec(memory_space=pltpu.SEMAPHORE)
    out = pl.pallas_call(
        body, name=name,
        out_shape=tuple(pltpu.HBM(a.shape, a.dtype) for a in arrays),
        in_specs=[hbm] * n + [sem, sem, _any_spec()], out_specs=(hbm,) * n,
        input_output_aliases={i: i for i in range(n)},
        compiler_params=pltpu.CompilerParams(has_side_effects=pltpu.SideEffectType.DATAFLOW_SIDE_EFFECTING),
    )(*arrays, send_sems, recv_sems, after)
    return tuple(out)


def _gather_plans(axes):
    n = len(axes)

    def stage_one(refs):
        x, y, c = _position()
        copies = []
        for a, axis in enumerate(axes):
            for px, py in _other_chips(x, y):
                copies.append((_half(refs[a], c, axis), _half(refs[n + a].at[2 * x + y], c, axis),
                               (px, py, c), _half(refs[n + a].at[2 * px + py], c, axis)))
        return copies

    def stage_two(refs):
        x, y, c = _position()
        copies = []
        for a, axis in enumerate(axes):
            for px, py in _other_chips(x, y):
                piece = _half(refs[n + a].at[2 * px + py], c, axis)
                copies.append((piece, piece, (x, y, 1 - c), _half(refs[n + a].at[2 * px + py], 1 - c, axis)))
        return copies

    return stage_one, stage_two


def _pair_swap_plan(axes):
    n = len(axes)

    def plan(refs):
        x, y, c = _position()
        return [(_half(refs[a], 1 - c, axes[a]), refs[n + a], (x, y, 1 - c), refs[n + a]) for a in range(n)]

    return plan


def _chip_swap_plan(n):
    def plan(refs):
        x, y, c = _position()
        copies = []
        for a in range(n):
            for j, (px, py) in enumerate(_other_chips(x, y)):
                copies.append((refs[a].at[2 * px + py], refs[n + a].at[j], (px, py, c), refs[n + a].at[j]))
        return copies

    return plan


def _pair_join_plan(axes):
    def plan(refs):
        x, y, c = _position()
        copies = []
        for a, axis in enumerate(axes):
            mine = _half(refs[a], c, axis)
            copies.append((mine, mine, (x, y, 1 - c), _half(refs[a], 1 - c, axis)))
        return copies

    return plan


def _pair_add_call(gs, gots, pos, name, axes):
    n = len(gs)

    def body(pos_r, *refs):
        for g_r, got_r, o_r in zip(refs[:n], refs[n:2 * n], refs[2 * n:]):
            o_r[...] = (g_r[...] + got_r[...]).astype(o_r.dtype)

    def mine(axis):
        return (lambda j, p: (j, p[1], 0)) if axis == ROWS else (lambda j, p: (j, 0, p[1]))

    blocks = [(None,) + got.shape[1:] for got in gots]
    return pl.pallas_call(
        body, name=name,
        grid_spec=pltpu.PrefetchScalarGridSpec(
            num_scalar_prefetch=1, grid=(N_CHIPS,),
            in_specs=[pl.BlockSpec(blk, mine(axis)) for blk, axis in zip(blocks, axes)]
            + [pl.BlockSpec(blk, lambda j, p: (j, 0, 0)) for blk in blocks],
            out_specs=[pl.BlockSpec(blk, lambda j, p: (j, 0, 0)) for blk in blocks]),
        out_shape=[jax.ShapeDtypeStruct(got.shape, COMM_DTYPE) for got in gots],
        compiler_params=_params(("arbitrary",), VMEM_BIG),
    )(pos, *gs, *gots)


def _chip_add_call(hsums, gots, pos, name, axes):
    n = len(hsums)
    steps = 2

    def body(pos_r, *refs):
        for own_r, got_r, o_r in zip(refs[:n], refs[n:2 * n], refs[2 * n:]):
            acc = own_r[...].astype(F32)
            for j in range(3):
                acc = acc + got_r[j].astype(F32)
            o_r[...] = acc

    in_specs, got_specs, out_specs, out_shape = [], [], [], []
    for h, axis in zip(hsums, axes):
        if axis == ROWS:
            rows, cols = h.shape[1] // steps, h.shape[2]
            in_specs.append(pl.BlockSpec((None, rows, cols), lambda i, p: (p[0], i, 0)))
            got_specs.append(pl.BlockSpec((3, rows, cols), lambda i, p: (0, i, 0)))
            out_specs.append(pl.BlockSpec((rows, cols), lambda i, p: (p[1] * steps + i, 0)))
            out_shape.append(jax.ShapeDtypeStruct((2 * h.shape[1], cols), F32))
        else:
            rows, cols = h.shape[1], h.shape[2] // steps
            in_specs.append(pl.BlockSpec((None, rows, cols), lambda i, p: (p[0], 0, i)))
            got_specs.append(pl.BlockSpec((3, rows, cols), lambda i, p: (0, 0, i)))
            out_specs.append(pl.BlockSpec((rows, cols), lambda i, p: (0, p[1] * steps + i)))
            out_shape.append(jax.ShapeDtypeStruct((rows, 2 * h.shape[2]), F32))
    return pl.pallas_call(
        body, name=name,
        grid_spec=pltpu.PrefetchScalarGridSpec(
            num_scalar_prefetch=1, grid=(steps,), in_specs=in_specs + got_specs, out_specs=out_specs),
        out_shape=out_shape,
        compiler_params=_params(("arbitrary",), VMEM_BIG),
    )(pos, *hsums, *gots)


SMALL_NAMES = ("norm_mix_pre", "norm_mix_post", "norm_mlp_pre", "norm_mlp_post", "b_gate_fwd", "b_gate_bwd",
               "gla_norm", "swa_sink", "rel_bias")


N_DEVICES = 8


def _small_pack_call(grads, extras):
    operands = list(grads) + list(extras)

    def body(*refs):
        g_refs, (all_a, all_b) = refs[:len(operands)], refs[len(operands):]
        x, y, c = _position()
        me = 4 * x + 2 * y + c
        all_a[me] = jnp.zeros(all_a.shape[1:], F32)
        all_b[me] = jnp.zeros(all_b.shape[1:], F32)
        for i in range(4):
            all_a[me, i:i + 1, :] = g_refs[i][...]
        all_a[me, 4:5, 0:256] = g_refs[4][...]
        all_a[me, 5:6, 0:256] = g_refs[5][...]
        all_a[me, 6:7, 0:128] = g_refs[6][...]
        all_a[me, 7:8, 0:128] = g_refs[7][...]
        all_a[me, 7:8, 128:256] = g_refs[11][...]
        all_b[me, 0:32, 0:128] = g_refs[8][...]
        all_b[me, 32:48, :] = g_refs[9][...]
        all_b[me, 48:64, :] = g_refs[10][...]

    out_shape = [jax.ShapeDtypeStruct((N_DEVICES, 8, D_MODEL), F32), jax.ShapeDtypeStruct((N_DEVICES, 64, 256), F32)]
    return pl.pallas_call(
        body, name="small_pack",
        in_specs=[_whole_spec(a.shape) for a in operands], out_specs=[_whole_spec(s.shape) for s in out_shape],
        out_shape=out_shape,
    )(*operands)


def _everyone_plan(n):
    def plan(refs):
        x, y, c = _position()
        copies = []
        for k in range(1, N_DEVICES):
            px = 1 - x if (k >> 2) & 1 else x
            py = 1 - y if (k >> 1) & 1 else y
            pc = 1 - c if k & 1 else c
            for a in range(n):
                mine = refs[a].at[4 * x + 2 * y + c]
                copies.append((mine, mine, (px, py, pc), refs[a].at[4 * px + 2 * py + pc]))
        return copies

    return plan


def _small_adamw_call(all_a, all_b, params):
    n_small = len(SMALL_NAMES)
    wmv = [t for p in params for t in p]
    shapes = [p[0].shape for p in params]

    def body(*refs):
        all_a, all_b = refs[:2]
        wmv_refs = refs[2:2 + 3 * n_small]
        out_refs = refs[2 + 3 * n_small:]
        sum_a, sum_b = all_a[0], all_b[0]
        for d in range(1, N_DEVICES):
            sum_a = sum_a + all_a[d]
            sum_b = sum_b + all_b[d]
        gsum = [sum_a[0:1], sum_a[1:2], sum_a[2:3], sum_a[3:4], sum_a[4:5, 0:256], sum_a[5:6, 0:256],
                sum_a[6:7, 0:128], sum_a[7:8, 0:SWA_Q_HEADS], sum_b[0:32, 0:SWA_Q_HEADS]]
        for i in range(n_small):
            w_r, m_r, v_r = wmv_refs[3 * i:3 * i + 3]
            delta, new_m, new_v = _adamw_math(w_r[...], gsum[i], m_r[...], v_r[...])
            out_refs[4 * i][...] = gsum[i]
            out_refs[4 * i + 1][...] = delta
            out_refs[4 * i + 2][...] = new_m
            out_refs[4 * i + 3][...] = new_v
        out_refs[4 * n_small][...] = sum_b[32:48]
        out_refs[4 * n_small + 1][...] = sum_b[48:64]
        out_refs[4 * n_small + 2][...] = sum_a[7:8, 128:256]

    out_shape = [jax.ShapeDtypeStruct(s, F32) for s in shapes for _ in range(4)]
    out_shape += [jax.ShapeDtypeStruct((GLA_GATE_RANK, 256), F32)] * 2 + [jax.ShapeDtypeStruct((1, 128), F32)]
    out = pl.pallas_call(
        body, name="small_adamw",
        in_specs=[_whole_spec(a.shape) for a in [all_a, all_b] + wmv],
        out_specs=[_whole_spec(s.shape) for s in out_shape],
        out_shape=out_shape,
    )(all_a, all_b, *wmv)
    per_name = [tuple(out[4 * i:4 * i + 4]) for i in range(n_small)]
    return per_name, out[4 * n_small], out[4 * n_small + 1], out[4 * n_small + 2]


def _pad_gate(w, first_row):
    return jnp.pad(w, ((first_row, 128 - GLA_GATE_RANK - first_row), (0, 0)))


def _own_slot(shard, chip):
    zone = lax.empty((N_CHIPS,) + shard.shape, shard.dtype)
    return lax.dynamic_update_slice(zone, shard[None], (chip,) + (0,) * shard.ndim)


def _reduce_to_owners(grads, axes, pos, tag, overlap):
    n = len(grads)

    def half_shape(g, axis):
        return (N_CHIPS, g.shape[1] // 2, g.shape[2]) if axis == ROWS else (N_CHIPS, g.shape[1], g.shape[2] // 2)

    lands = [lax.empty(half_shape(g, axis), F32) for g, axis in zip(grads, axes)]
    handle, token = _split_start(tag + "_pair_start", list(grads) + lands, n, _pair_swap_plan(axes), PAIR_PEERS)
    got = _split_wait(tag + "_pair_wait", handle, n, _pair_swap_plan(axes), overlap[0](token))
    sums = list(_pair_add_call(got[:n], got[n:], pos, tag + "_pair_add", axes))
    lands = [lax.empty((3,) + s.shape[1:], s.dtype) for s in sums]
    handle, token = _split_start(tag + "_chip_start", sums + lands, 3 * n, _chip_swap_plan(n), CHIP_PEERS)
    got = _split_wait(tag + "_chip_wait", handle, 3 * n, _chip_swap_plan(n), overlap[1](token))
    halves = list(_chip_add_call(got[:n], got[n:], pos, tag + "_chip_add", axes))
    handle, token = _split_start(tag + "_join_start", halves, n, _pair_join_plan(axes), PAIR_PEERS)
    return _split_wait(tag + "_join_wait", handle, n, _pair_join_plan(axes), overlap[2](token))


def kernel(x, norm_mix_pre, w_in, w_gate_up_fwd, b_gate_fwd, w_gate_up_bwd, b_gate_bwd, gla_norm, swa_sink, rel_bias, w_out, norm_mix_post, norm_mlp_pre, w_up, w_down, norm_mlp_post, loss_target, m_norm_mix_pre, m_w_in, m_w_gate_up_fwd, m_b_gate_fwd, m_w_gate_up_bwd, m_b_gate_bwd, m_gla_norm, m_swa_sink, m_rel_bias, m_w_out, m_norm_mix_post, m_norm_mlp_pre, m_w_up, m_w_down, m_norm_mlp_post, v_norm_mix_pre, v_w_in, v_w_gate_up_fwd, v_b_gate_fwd, v_w_gate_up_bwd, v_b_gate_bwd, v_gla_norm, v_swa_sink, v_rel_bias, v_w_out, v_norm_mix_post, v_norm_mlp_pre, v_w_up, v_w_down, v_norm_mlp_post):
    given = dict(locals())
    cx, cy, cc = _position()
    chip = (2 * cx + cy).astype(jnp.int32)
    pos = jnp.stack([chip, cc.astype(jnp.int32)])
    seq, tgt = x[0], loss_target[0]

    gates = jnp.concatenate([w_gate_up_fwd[0], w_gate_up_bwd[0]], axis=0).astype(COMM_DTYPE)
    all_in, all_gates = _first_gather_call([w_in[0].T.astype(COMM_DTYPE), gates], [COLS, ROWS], [True, False])
    rest = [w_out[0].astype(COMM_DTYPE), jnp.stack([w_up[0], w_down[0]]).astype(COMM_DTYPE)]
    stage_one, stage_two = _gather_plans([ROWS, ROWS])
    handle, token = _split_start("gather_chip_start", rest + [_own_slot(s, chip) for s in rest] + [all_gates], 6,
                                 stage_one, CHIP_PEERS)

    w_in_t = _mx(all_in.reshape(IN_COLS, D_MODEL))
    gates_full = jnp.concatenate([all_gates[j] for j in range(N_CHIPS)], axis=1)
    wgf_p = _mx(_pad_gate(gates_full[:GLA_GATE_RANK], 0))
    wgb_p = _mx(_pad_gate(gates_full[GLA_GATE_RANK:], GLA_GATE_RANK))
    bf_p, bb_p = b_gate_fwd, b_gate_bwd
    buckets = jnp.asarray(_band_buckets())
    sink1 = swa_sink.reshape(SWA_Q_HEADS)

    qa, ka, va, ga, qs, ks, vs, za = _proj_call(seq, norm_mix_pre, w_in_t, dep=token)
    halo = ((SWA_BLOCK, SWA_BLOCK), (0, 0))
    ks_p, vs_p = jnp.pad(ks, halo), jnp.pad(vs, halo)
    o_f, o_b, s_f, s_b = _gla_fwd_call(qa, ka, va, za, wgf_p, bf_p, wgb_p, bb_p)
    bias = _bias_call(rel_bias, buckets, dep=o_f)
    arrays = _split_wait("gather_chip_wait", handle, 6, stage_one, bias)
    handle, token = _split_start("gather_pair_start", list(arrays), 6, stage_two, PAIR_PEERS)
    o_s = _swa_fwd_call(qs, ks_p, vs_p, bias, sink1, dep=token)
    arrays = _split_wait("gather_pair_wait", handle, 6, stage_two, o_s)
    w_out_full = _mx(arrays[2].reshape(N_CHIPS * R_OUT, D_MODEL))
    w_ud = _mx(arrays[3])
    cat, h1, n2, a, rz, dh2, dff, loss, d_post2 = _mix_mlp_fwd_call(
        o_f, o_b, ga, o_s, seq, tgt, gla_norm, w_out_full, norm_mix_post, norm_mlp_pre, w_ud, norm_mlp_post)

    dz, dh1, do, dga, dos, dw_out, d_pre2, d_post, d_gn = _mlp_mix_bwd_call(
        dff, rz, w_ud, dh2, h1, cat, o_f, o_b, ga, gla_norm, norm_mix_post, norm_mlp_pre, w_out_full)
    dw_down, dw_up4 = _mlp_wgrad_call(a, dff, n2, dz)
    done = {}

    def swa_backward(tok):
        done["swa"] = _swa_bwd_call(qs, ks_p, vs_p, bias, sink1, dos, dep=tok)
        return done["swa"][0]

    def gla_in_backward(tok):
        done["gla"] = _gla_bwd_call(qa, ka, va, za, do, s_f, s_b, wgf_p, bf_p, wgb_p, bb_p, dep=tok)
        dqf, dkf, dvf, dzf, _, _, dqb, dkb, dvb, dzb, _, _ = done["gla"]
        dqs, dks_p, dvs_p, _, _ = done["swa"]
        done["in"] = _in_bwd_call(
            seq, dh1, norm_mix_pre, w_in_t,
            pairs=[(_side_by_side(T_QA), (dqf, dqb)), (_side_by_side(T_KA), (dkf, dkb)), (T_VA, (dvf, dvb)),
                   (T_ZA, (dzf, dzb))],
            singles=[(T_GA, dga), (_side_by_side(T_QS), dqs)], halos=[(T_KS, dks_p), (T_VS, dvs_p)])
        return done["in"][0]

    def bias_backward(tok):
        done["rel"] = _relbias_call(done["swa"][3], done["swa"][4], buckets, dep=tok)
        return done["rel"][0]

    g_up, g_down, g_out = _reduce_to_owners(
        [dw_up4, dw_down.reshape(N_CHIPS, R_DOWN, D_MODEL), dw_out.reshape(N_CHIPS, R_OUT, D_MODEL)],
        [ROWS, ROWS, ROWS], pos, "mlp", [swa_backward, gla_in_backward, bias_backward])
    dx, dw_in_t, d_pre = done["in"]
    dwf, dbf, dwb, dbb = done["gla"][4], done["gla"][5], done["gla"][10], done["gla"][11]
    drel, dsink = done["rel"]

    small_grads = [d_pre, d_post, d_pre2, d_post2, dbf, dbb, d_gn, dsink, drel]
    gate_grads = [dwf[:GLA_GATE_RANK], dwb[GLA_GATE_RANK:2 * GLA_GATE_RANK]]
    small_params = [(given[n], given["m_" + n], given["v_" + n]) for n in SMALL_NAMES]
    upd = {}

    everyone = _everyone_plan(2)
    small_handle, small_token = _split_start(
        "small_start", list(_small_pack_call(small_grads, gate_grads + [loss])), 2 * (N_DEVICES - 1), everyone)

    def update_out(tok):
        upd["w_out"] = tuple(_adamw_call(w_out[0], g_out, m_w_out[0], v_w_out[0], "adamw_w_out",
                                         dep=tok + small_token))
        return upd["w_out"][1]

    def update_mlp(tok):
        upd["w_up"] = tuple(_adamw_call(w_up[0], g_up, m_w_up[0], v_w_up[0], "adamw_w_up", dep=tok))
        upd["w_down"] = tuple(
            _adamw_call(w_down[0], g_down, m_w_down[0], v_w_down[0], "adamw_w_down", dep=upd["w_up"][1]))
        all_a, all_b = _split_wait("small_wait", small_handle, 2 * (N_DEVICES - 1), everyone, upd["w_down"][1])
        per_name, done["gf_sum"], done["gb_sum"], upd["loss"] = _small_adamw_call(all_a, all_b, small_params)
        upd.update(dict(zip(SMALL_NAMES, per_name)))
        return per_name[0][1]

    def update_gates(tok):
        for name, total in (("w_gate_up_fwd", done["gf_sum"]), ("w_gate_up_bwd", done["gb_sum"])):
            g = lax.dynamic_slice(total, (0, chip * 64), (GLA_GATE_RANK, 64))
            upd[name] = tuple(_adamw_call(given[name][0], g, given["m_" + name][0], given["v_" + name][0],
                                          "adamw_" + name, dep=tok))
        return upd["w_gate_up_bwd"][1]

    (g_in_t,) = _reduce_to_owners([dw_in_t.reshape(N_CHIPS, R_IN, D_MODEL)], [COLS], pos, "in",
                                  [update_out, update_mlp, update_gates])
    upd["w_in"] = tuple(t.T for t in _adamw_call(w_in[0].T, g_in_t, m_w_in[0].T, v_w_in[0].T, "adamw_w_in"))

    big = ("w_in", "w_gate_up_fwd", "w_gate_up_bwd", "w_out", "w_up", "w_down")
    names = ["norm_mix_pre", "w_in", "w_gate_up_fwd", "b_gate_fwd", "w_gate_up_bwd", "b_gate_bwd", "gla_norm",
             "swa_sink", "rel_bias", "w_out", "norm_mix_post", "norm_mlp_pre", "w_up", "w_down", "norm_mlp_post"]
    outs = [upd["loss"][0, 0], dx[None]]
    for kind in range(4):
        outs += [upd[n][kind][None] if n in big else upd[n][kind] for n in names]
    return tuple(outs)
```
